```python
import jax, jax.numpy as jnp
from jax import lax
import numpy as np

D_MODEL = 2048
BATCH = 8
SEQ = 4096
DEPTH = 1

ATTN_WIDTH = D_MODEL // 2
HEAD_DIM = 64
N_Q_HEADS = ATTN_WIDTH // HEAD_DIM
N_KV_HEADS = 2
GROUP = N_Q_HEADS // N_KV_HEADS
KV_WIDTH = N_KV_HEADS * HEAD_DIM
WINDOW = 128
BLOCK = 128
RNN_WIDTH = D_MODEL - ATTN_WIDTH
RNN_HEAD_DIM = 128
N_RNN_HEADS = RNN_WIDTH // RNN_HEAD_DIM
CHUNK = 64
MIX_WIDTH = ATTN_WIDTH + RNN_WIDTH
D_FF = 4 * D_MODEL
IN_WIDTH = ATTN_WIDTH + 2 * KV_WIDTH + 4 * RNN_WIDTH
SPLITS = tuple(int(s) for s in np.cumsum([ATTN_WIDTH, KV_WIDTH, KV_WIDTH,
                                           RNN_WIDTH, RNN_WIDTH, RNN_WIDTH]))
EPS = 1e-6

kernel_name = "hymba_swa_sink_hgrn2_sqrelu_sandwich"


def rmsnorm(x, gain):
    xf = x.astype(jnp.float32)
    y = xf * lax.rsqrt(jnp.mean(xf * xf, axis=-1, keepdims=True) + EPS)
    return (y * gain.astype(jnp.float32)).astype(x.dtype)


def alibi_slopes(n_heads):
    return jnp.exp2(-8.0 * jnp.arange(1, n_heads + 1, dtype=jnp.float32) / n_heads)


def sliding_window_attention(q, k, v, sinks):
    B, S, _ = q.shape
    nb = S // BLOCK
    qb = q.reshape(B, nb, BLOCK, N_KV_HEADS, GROUP, HEAD_DIM)
    kb = k.reshape(B, nb, BLOCK, N_KV_HEADS, HEAD_DIM)
    vb = v.reshape(B, nb, BLOCK, N_KV_HEADS, HEAD_DIM)
    pad = ((0, 0), (1, 0), (0, 0), (0, 0), (0, 0))
    kcat = jnp.concatenate([jnp.pad(kb, pad)[:, :-1], kb], axis=2)
    vcat = jnp.concatenate([jnp.pad(vb, pad)[:, :-1], vb], axis=2)
    scores = jnp.einsum('bnqhgd,bnkhd->bnhgqk', qb, kcat,
                        preferred_element_type=jnp.float32) * (HEAD_DIM ** -0.5)
    q_pos = jnp.arange(BLOCK) + BLOCK
    k_pos = jnp.arange(2 * BLOCK)
    dist = (q_pos[:, None] - k_pos[None, :]).astype(jnp.float32)
    band = (dist >= 0) & (dist < WINDOW)
    abs_k = jnp.arange(nb)[:, None] * BLOCK - BLOCK + k_pos[None, :]
    valid = band[None] & (abs_k >= 0)[:, None, :]
    slopes = alibi_slopes(N_Q_HEADS).reshape(N_KV_HEADS, GROUP, 1, 1)
    scores = scores - slopes * dist
    scores = jnp.where(valid[None, :, None, None], scores, -jnp.inf)
    sink = sinks.astype(jnp.float32).reshape(N_KV_HEADS, GROUP, 1, 1)
    m = jnp.maximum(jnp.max(scores, axis=-1, keepdims=True), sink)
    p = jnp.exp(scores - m)
    probs = p / (jnp.sum(p, axis=-1, keepdims=True) + jnp.exp(sink - m))
    out = jnp.einsum('bnhgqk,bnkhd->bnqhgd', probs.astype(v.dtype), vcat)
    return out.reshape(B, S, ATTN_WIDTH)


def hgrn2_chunkwise(q, f_logit, i, g, lb, norm_gain):
    B, S, _ = q.shape
    nc = S // CHUNK
    f32 = jnp.float32
    f = lb + (1.0 - lb) * jax.nn.sigmoid(f_logit.astype(f32))
    log_f = jnp.log(f)
    key = 1.0 - f
    qf = jax.nn.silu(q.astype(f32))
    vf = i.astype(f32)

    def to_chunks(t):
        return t.reshape(B, nc, CHUNK, N_RNN_HEADS, RNN_HEAD_DIM).transpose(1, 0, 3, 2, 4)

    causal = jnp.tril(jnp.ones((CHUNK, CHUNK), dtype=bool))

    def step(state, inp):
        qc, kc, vc, lfc = inp
        b = jnp.cumsum(lfc, axis=-2)
        o_inter = jnp.einsum('bhtk,bhkv->bhtv', qc * jnp.exp(b), state)
        diff = b[:, :, :, None, :] - b[:, :, None, :, :]
        decay = jnp.exp(jnp.where(causal[:, :, None], diff, -jnp.inf))
        att = jnp.einsum('bhtk,bhsk,bhtsk->bhts', qc, kc, decay)
        o_intra = jnp.einsum('bhts,bhsv->bhtv', att, vc)
        b_last = b[:, :, -1:, :]
        new_state = (jnp.exp(b_last[:, :, 0, :])[..., None] * state
                     + jnp.einsum('bhsk,bhsv->bhkv', kc * jnp.exp(b_last - b), vc))
        return new_state, o_inter + o_intra

    s0 = jnp.zeros((B, N_RNN_HEADS, RNN_HEAD_DIM, RNN_HEAD_DIM), f32)
    _, o = lax.scan(step, s0, (to_chunks(qf), to_chunks(key), to_chunks(vf), to_chunks(log_f)))
    o = o.transpose(1, 0, 3, 2, 4).reshape(B, S, N_RNN_HEADS, RNN_HEAD_DIM)
    o = o * lax.rsqrt(jnp.mean(o * o, axis=-1, keepdims=True) + EPS) * norm_gain.astype(f32)
    gate = jax.nn.silu(g.astype(f32)).reshape(B, S, N_RNN_HEADS, RNN_HEAD_DIM)
    return (o * gate).reshape(B, S, RNN_WIDTH).astype(q.dtype)


def _fwd_setup_inputs(seed: int = 0) -> dict:
    key = jax.random.key(seed)
    ks = jax.random.split(key, 14)
    f32 = jnp.float32

    def gain(k, shape):
        return 1.0 + 0.05 * jax.random.normal(k, shape, f32)

    return {
        "x": jax.random.normal(ks[0], (BATCH, SEQ, D_MODEL), f32),
        "w_in": jax.random.normal(ks[1], (DEPTH, D_MODEL, IN_WIDTH), f32) * D_MODEL ** -0.5,
        "attn_sinks": 0.5 * jax.random.normal(ks[2], (DEPTH, N_Q_HEADS), f32),
        "attn_out_gain": gain(ks[3], (DEPTH, ATTN_WIDTH)),
        "rnn_lb_logits": 0.1 * jax.random.normal(ks[4], (DEPTH + 1, RNN_WIDTH), f32),
        "rnn_norm_gain": gain(ks[5], (DEPTH, RNN_HEAD_DIM)),
        "w_out": jax.random.normal(ks[6], (DEPTH, MIX_WIDTH, D_MODEL), f32) * MIX_WIDTH ** -0.5,
        "mix_pre_gain": gain(ks[7], (DEPTH, D_MODEL)),
        "mix_post_gain": gain(ks[8], (DEPTH, D_MODEL)),
        "mlp_pre_gain": gain(ks[9], (DEPTH, D_MODEL)),
        "mlp_post_gain": gain(ks[10], (DEPTH, D_MODEL)),
        "w_up": jax.random.normal(ks[11], (DEPTH, D_MODEL, D_FF), f32) * D_MODEL ** -0.5,
        "w_down": jax.random.normal(ks[12], (DEPTH, D_FF, D_MODEL), f32) * D_FF ** -0.5,
    }


def _fwd_reference(x, w_in, attn_sinks, attn_out_gain, rnn_lb_logits, rnn_norm_gain, w_out,
              mix_pre_gain, mix_post_gain, mlp_pre_gain, mlp_post_gain, w_up, w_down):
    lb_all = jnp.cumsum(jax.nn.softmax(rnn_lb_logits.astype(jnp.float32), axis=0), axis=0)
    for layer in range(DEPTH):
        h = rmsnorm(x, mix_pre_gain[layer])
        proj = jnp.einsum('bsd,de->bse', h, w_in[layer])
        q_a, k_a, v_a, q_r, f_r, i_r, g_r = jnp.split(proj, SPLITS, axis=-1)
        attn = sliding_window_attention(q_a, k_a, v_a, attn_sinks[layer])
        attn = rmsnorm(attn, attn_out_gain[layer])
        rnn = hgrn2_chunkwise(q_r, f_r, i_r, g_r, lb_all[layer], rnn_norm_gain[layer])
        mixed = jnp.einsum('bse,ed->bsd', jnp.concatenate([attn, rnn], axis=-1), w_out[layer])
        x = x + rmsnorm(mixed, mix_post_gain[layer])
        h = rmsnorm(x, mlp_pre_gain[layer])
        u = jax.nn.relu(jnp.einsum('bsd,df->bsf', h, w_up[layer]))
        y = jnp.einsum('bsf,fd->bsd', u * u, w_down[layer])
        x = x + rmsnorm(y, mlp_post_gain[layer])
    return x


import jax as _jax
import jax.numpy as _jnp

TWIN_FORMAT = 'train_step'
FWD_PARAMS = ['x', 'w_in', 'attn_sinks', 'attn_out_gain', 'rnn_lb_logits', 'rnn_norm_gain', 'w_out', 'mix_pre_gain', 'mix_post_gain', 'mlp_pre_gain', 'mlp_post_gain', 'w_up', 'w_down']
TWIN_WEIGHTS = ['w_in', 'attn_sinks', 'attn_out_gain', 'rnn_lb_logits', 'rnn_norm_gain', 'w_out', 'mix_pre_gain', 'mix_post_gain', 'mlp_pre_gain', 'mlp_post_gain', 'w_up', 'w_down']
TWIN_DIFF_INPUT = 'x'
TWIN_INPUTS = ['x', 'w_in', 'attn_sinks', 'attn_out_gain', 'rnn_lb_logits', 'rnn_norm_gain', 'w_out', 'mix_pre_gain', 'mix_post_gain', 'mlp_pre_gain', 'mlp_post_gain', 'w_up', 'w_down', 'loss_target', 'm_w_in', 'm_attn_sinks', 'm_attn_out_gain', 'm_rnn_lb_logits', 'm_rnn_norm_gain', 'm_w_out', 'm_mix_pre_gain', 'm_mix_post_gain', 'm_mlp_pre_gain', 'm_mlp_post_gain', 'm_w_up', 'm_w_down', 'v_w_in', 'v_attn_sinks', 'v_attn_out_gain', 'v_rnn_lb_logits', 'v_rnn_norm_gain', 'v_w_out', 'v_mix_pre_gain', 'v_mix_post_gain', 'v_mlp_pre_gain', 'v_mlp_post_gain', 'v_w_up', 'v_w_down']
TWIN_OUTPUTS = ['loss', 'grad_x', 'grad_w_in', 'grad_attn_sinks', 'grad_attn_out_gain', 'grad_rnn_lb_logits', 'grad_rnn_norm_gain', 'grad_w_out', 'grad_mix_pre_gain', 'grad_mix_post_gain', 'grad_mlp_pre_gain', 'grad_mlp_post_gain', 'grad_w_up', 'grad_w_down', 'delta_w_in', 'delta_attn_sinks', 'delta_attn_out_gain', 'delta_rnn_lb_logits', 'delta_rnn_norm_gain', 'delta_w_out', 'delta_mix_pre_gain', 'delta_mix_post_gain', 'delta_mlp_pre_gain', 'delta_mlp_post_gain', 'delta_w_up', 'delta_w_down', 'new_m_w_in', 'new_m_attn_sinks', 'new_m_attn_out_gain', 'new_m_rnn_lb_logits', 'new_m_rnn_norm_gain', 'new_m_w_out', 'new_m_mix_pre_gain', 'new_m_mix_post_gain', 'new_m_mlp_pre_gain', 'new_m_mlp_post_gain', 'new_m_w_up', 'new_m_w_down', 'new_v_w_in', 'new_v_attn_sinks', 'new_v_attn_out_gain', 'new_v_rnn_lb_logits', 'new_v_rnn_norm_gain', 'new_v_w_out', 'new_v_mix_pre_gain', 'new_v_mix_post_gain', 'new_v_mlp_pre_gain', 'new_v_mlp_post_gain', 'new_v_w_up', 'new_v_w_down']
TWIN_LEAF_KINDS = {'loss': 'loss', 'grad_x': 'grad_x', 'grad_w_in': 'grad_w', 'grad_attn_sinks': 'grad_w', 'grad_attn_out_gain': 'grad_w', 'grad_rnn_lb_logits': 'grad_w', 'grad_rnn_norm_gain': 'grad_w', 'grad_w_out': 'grad_w', 'grad_mix_pre_gain': 'grad_w', 'grad_mix_post_gain': 'grad_w', 'grad_mlp_pre_gain': 'grad_w', 'grad_mlp_post_gain': 'grad_w', 'grad_w_up': 'grad_w', 'grad_w_down': 'grad_w', 'delta_w_in': 'delta_w', 'delta_attn_sinks': 'delta_w', 'delta_attn_out_gain': 'delta_w', 'delta_rnn_lb_logits': 'delta_w', 'delta_rnn_norm_gain': 'delta_w', 'delta_w_out': 'delta_w', 'delta_mix_pre_gain': 'delta_w', 'delta_mix_post_gain': 'delta_w', 'delta_mlp_pre_gain': 'delta_w', 'delta_mlp_post_gain': 'delta_w', 'delta_w_up': 'delta_w', 'delta_w_down': 'delta_w', 'new_m_w_in': 'new_m', 'new_m_attn_sinks': 'new_m', 'new_m_attn_out_gain': 'new_m', 'new_m_rnn_lb_logits': 'new_m', 'new_m_rnn_norm_gain': 'new_m', 'new_m_w_out': 'new_m', 'new_m_mix_pre_gain': 'new_m', 'new_m_mix_post_gain': 'new_m', 'new_m_mlp_pre_gain': 'new_m', 'new_m_mlp_post_gain': 'new_m', 'new_m_w_up': 'new_m', 'new_m_w_down': 'new_m', 'new_v_w_in': 'new_v', 'new_v_attn_sinks': 'new_v', 'new_v_attn_out_gain': 'new_v', 'new_v_rnn_lb_logits': 'new_v', 'new_v_rnn_norm_gain': 'new_v', 'new_v_w_out': 'new_v', 'new_v_mix_pre_gain': 'new_v', 'new_v_mix_post_gain': 'new_v', 'new_v_mlp_pre_gain': 'new_v', 'new_v_mlp_post_gain': 'new_v', 'new_v_w_up': 'new_v', 'new_v_w_down': 'new_v'}


def _forward(args):
    return _fwd_reference(*[args[k] for k in FWD_PARAMS])


def _output_shape():
    def fwd():
        inp = _fwd_setup_inputs(0)
        return _fwd_reference(*[inp[k] for k in FWD_PARAMS])
    out = _jax.eval_shape(fwd)
    return out.shape, out.dtype

N_MICROBATCH = 1
ADAM_LR = 0.001
ADAM_B1 = 0.9
ADAM_B2 = 0.999
ADAM_EPS = 1e-08
ADAM_WD = 0.01
ADAM_STEP = 10
PER_EXAMPLE_BATCH_AXIS = {'x': 0, 'loss_target': 0}
SHARED_INPUTS = []
_WEIGHT_DTYPES = {'w_in': _jnp.float32, 'attn_sinks': _jnp.float32, 'attn_out_gain': _jnp.float32, 'rnn_lb_logits': _jnp.float32, 'rnn_norm_gain': _jnp.float32, 'w_out': _jnp.float32, 'mix_pre_gain': _jnp.float32, 'mix_post_gain': _jnp.float32, 'mlp_pre_gain': _jnp.float32, 'mlp_post_gain': _jnp.float32, 'w_up': _jnp.float32, 'w_down': _jnp.float32}
MOMENT_SCALE = {'w_in': 2.123717e-01, 'attn_sinks': 1.020234e+00, 'attn_out_gain': 7.403545e-01, 'rnn_lb_logits': 1.371286e-02, 'rnn_norm_gain': 5.344090e-01, 'w_out': 4.584359e-01, 'mix_pre_gain': 3.661285e-01, 'mix_post_gain': 1.605616e+01, 'mlp_pre_gain': 3.993710e-01, 'mlp_post_gain': 1.650171e+01, 'w_up': 2.101431e-01, 'w_down': 4.801319e-01}


def _to_microbatches(a, axis):
    t = _jnp.moveaxis(a, axis, 0)
    t = t.reshape((N_MICROBATCH, t.shape[0] // N_MICROBATCH) + t.shape[1:])
    return _jnp.moveaxis(t, 1, axis + 1)


def setup_inputs(seed: int = 0) -> dict:
    inp = _fwd_setup_inputs(seed)
    key = _jax.random.fold_in(_jax.random.key(seed), 7919)
    shape, _ = _output_shape()
    out = dict(inp)
    out["loss_target"] = _jax.random.normal(_jax.random.fold_in(key, 0), shape, _jnp.float32)
    for i, name in enumerate(TWIN_WEIGHTS):
        w = inp[name].astype(_jnp.float32)
        if MOMENT_SCALE is None:
            s = _jnp.sqrt(_jnp.mean(_jnp.square(w)) + 1e-30)
        else:
            s = MOMENT_SCALE[name]
        km, kv = _jax.random.split(_jax.random.fold_in(key, i + 1))
        out[name] = w
        out["m_" + name] = s * _jax.random.normal(km, w.shape, _jnp.float32)
        out["v_" + name] = (s * s) * _jax.random.uniform(kv, w.shape, _jnp.float32, 0.5, 1.5)
    if N_MICROBATCH > 1:
        for name, axis in PER_EXAMPLE_BATCH_AXIS.items():
            out[name] = _to_microbatches(out[name], axis)
    return {'x': out['x'], 'w_in': out['w_in'], 'attn_sinks': out['attn_sinks'], 'attn_out_gain': out['attn_out_gain'], 'rnn_lb_logits': out['rnn_lb_logits'], 'rnn_norm_gain': out['rnn_norm_gain'], 'w_out': out['w_out'], 'mix_pre_gain': out['mix_pre_gain'], 'mix_post_gain': out['mix_post_gain'], 'mlp_pre_gain': out['mlp_pre_gain'], 'mlp_post_gain': out['mlp_post_gain'], 'w_up': out['w_up'], 'w_down': out['w_down'], 'loss_target': out['loss_target'], 'm_w_in': out['m_w_in'], 'm_attn_sinks': out['m_attn_sinks'], 'm_attn_out_gain': out['m_attn_out_gain'], 'm_rnn_lb_logits': out['m_rnn_lb_logits'], 'm_rnn_norm_gain': out['m_rnn_norm_gain'], 'm_w_out': out['m_w_out'], 'm_mix_pre_gain': out['m_mix_pre_gain'], 'm_mix_post_gain': out['m_mix_post_gain'], 'm_mlp_pre_gain': out['m_mlp_pre_gain'], 'm_mlp_post_gain': out['m_mlp_post_gain'], 'm_w_up': out['m_w_up'], 'm_w_down': out['m_w_down'], 'v_w_in': out['v_w_in'], 'v_attn_sinks': out['v_attn_sinks'], 'v_attn_out_gain': out['v_attn_out_gain'], 'v_rnn_lb_logits': out['v_rnn_lb_logits'], 'v_rnn_norm_gain': out['v_rnn_norm_gain'], 'v_w_out': out['v_w_out'], 'v_mix_pre_gain': out['v_mix_pre_gain'], 'v_mix_post_gain': out['v_mix_post_gain'], 'v_mlp_pre_gain': out['v_mlp_pre_gain'], 'v_mlp_post_gain': out['v_mlp_post_gain'], 'v_w_up': out['v_w_up'], 'v_w_down': out['v_w_down']}


def _loss(weights, diff, rest, loss_target):
    with _jax.named_scope("forward"):
        args = {**rest, TWIN_DIFF_INPUT: diff, **{k: w.astype(_WEIGHT_DTYPES[k]) for k, w in weights.items()}}
        y = _forward(args)
    with _jax.named_scope("loss_head"):
        err = _jnp.square(y.astype(_jnp.float32) - loss_target)
        return 0.5 * _jnp.sum(_jnp.mean(err, axis=-1)) if err.ndim else 0.5 * err


def _adamw(w, g, m, v):
    m = ADAM_B1 * m + (1.0 - ADAM_B1) * g
    v = ADAM_B2 * v + (1.0 - ADAM_B2) * _jnp.square(g)
    m_hat = m / (1.0 - ADAM_B1 ** ADAM_STEP)
    v_hat = v / (1.0 - ADAM_B2 ** ADAM_STEP)
    delta = -ADAM_LR * (m_hat / (_jnp.sqrt(v_hat) + ADAM_EPS) + ADAM_WD * w)
    return delta, m, v


def reference(x, w_in, attn_sinks, attn_out_gain, rnn_lb_logits, rnn_norm_gain, w_out, mix_pre_gain, mix_post_gain, mlp_pre_gain, mlp_post_gain, w_up, w_down, loss_target, m_w_in, m_attn_sinks, m_attn_out_gain, m_rnn_lb_logits, m_rnn_norm_gain, m_w_out, m_mix_pre_gain, m_mix_post_gain, m_mlp_pre_gain, m_mlp_post_gain, m_w_up, m_w_down, v_w_in, v_attn_sinks, v_attn_out_gain, v_rnn_lb_logits, v_rnn_norm_gain, v_w_out, v_mix_pre_gain, v_mix_post_gain, v_mlp_pre_gain, v_mlp_post_gain, v_w_up, v_w_down):
    given = dict(x=x, w_in=w_in, attn_sinks=attn_sinks, attn_out_gain=attn_out_gain, rnn_lb_logits=rnn_lb_logits, rnn_norm_gain=rnn_norm_gain, w_out=w_out, mix_pre_gain=mix_pre_gain, mix_post_gain=mix_post_gain, mlp_pre_gain=mlp_pre_gain, mlp_post_gain=mlp_post_gain, w_up=w_up, w_down=w_down, loss_target=loss_target, m_w_in=m_w_in, m_attn_sinks=m_attn_sinks, m_attn_out_gain=m_attn_out_gain, m_rnn_lb_logits=m_rnn_lb_logits, m_rnn_norm_gain=m_rnn_norm_gain, m_w_out=m_w_out, m_mix_pre_gain=m_mix_pre_gain, m_mix_post_gain=m_mix_post_gain, m_mlp_pre_gain=m_mlp_pre_gain, m_mlp_post_gain=m_mlp_post_gain, m_w_up=m_w_up, m_w_down=m_w_down, v_w_in=v_w_in, v_attn_sinks=v_attn_sinks, v_attn_out_gain=v_attn_out_gain, v_rnn_lb_logits=v_rnn_lb_logits, v_rnn_norm_gain=v_rnn_norm_gain, v_w_out=v_w_out, v_mix_pre_gain=v_mix_pre_gain, v_mix_post_gain=v_mix_post_gain, v_mlp_pre_gain=v_mlp_pre_gain, v_mlp_post_gain=v_mlp_post_gain, v_w_up=v_w_up, v_w_down=v_w_down)
    weights = {n: given[n] for n in TWIN_WEIGHTS}
    shared = {n: given[n] for n in SHARED_INPUTS}
    per_example = {n: given[n] for n in ['x']}
    grad_fn = _jax.value_and_grad(_loss, argnums=(0, 1))

    def one_microbatch(ex, loss_target):
        ex = dict(ex)
        diff = ex.pop(TWIN_DIFF_INPUT)
        return grad_fn(weights, diff, {**shared, **ex}, loss_target)

    if N_MICROBATCH == 1:
        loss, (grad_w, grad_x) = one_microbatch(per_example, given["loss_target"])
    else:
        def body(carry, xs):
            loss_sum, grad_sum = carry
            l_k, (gw_k, gx_k) = one_microbatch(xs[0], xs[1])
            with _jax.named_scope("update"):
                return (loss_sum + l_k, _jax.tree.map(_jnp.add, grad_sum, gw_k)), gx_k

        init = (_jnp.zeros((), _jnp.float32), _jax.tree.map(_jnp.zeros_like, weights))
        (loss, grad_w), grad_x = _jax.lax.scan(body, init, (per_example, given["loss_target"]))
    with _jax.named_scope("update"):
        delta_w, new_m, new_v = {}, {}, {}
        for n in TWIN_WEIGHTS:
            delta_w[n], new_m[n], new_v[n] = _adamw(weights[n], grad_w[n], given["m_" + n], given["v_" + n])
    return (loss, grad_x, *[grad_w[n] for n in TWIN_WEIGHTS], *[delta_w[n] for n in TWIN_WEIGHTS],
            *[new_m[n] for n in TWIN_WEIGHTS], *[new_v[n] for n in TWIN_WEIGHTS])
```

```python
import functools

import jax
import jax.numpy as jnp
from jax import lax
from jax.experimental import pallas as pl
from jax.experimental.pallas import tpu as pltpu

F32 = jnp.float32
BF16 = jnp.bfloat16
MESH = pl.DeviceIdType.MESH

EPS = 1e-6
D_MODEL = 2048
ATTN_W = 1024
HEAD_DIM = 64
N_Q = 16
N_KV = 2
GROUP = 8
BLK = 128
RNN_W = 1024
RNN_HD = 128
N_RNN = 8
CHUNK = 64
SUB = 16
D_FF = 8192
IN_W = 5376
N_CHIPS = 4
KV_COL = ATTN_W
QR_COL = ATTN_W + 2 * 128
FR_COL = QR_COL + RNN_W
IR_COL = FR_COL + RNN_W
GR_COL = IR_COL + RNN_W

ADAM_LR = 0.001
ADAM_B1 = 0.9
ADAM_B2 = 0.999
ADAM_EPS = 1e-08
ADAM_WD = 0.01
ADAM_STEP = 10

VMEM_LIMIT = 48 * 1024 * 1024

NT = (((1,), (1,)), ((), ()))
TN = (((0,), (0,)), ((), ()))


def _params(sem=None):
    return pltpu.CompilerParams(dimension_semantics=sem, vmem_limit_bytes=VMEM_LIMIT)


def _sigmoid(x):
    return 1.0 / (1.0 + jnp.exp(-x))


def _mm(a, w, *, tm, tn, tk, out_dtype, name, a_square=False, relu=False, mul2=None):
    m, k = a.shape
    _, n = w.shape
    nk = k // tk
    assert m % tm == 0 and n % tn == 0 and k % tk == 0

    def body(*refs):
        if mul2 is not None:
            a_ref, w_ref, e_ref, o_ref, acc_ref = refs
        else:
            a_ref, w_ref, o_ref, acc_ref = refs
            e_ref = None
        kk = pl.program_id(2)
        av = a_ref[...]
        if a_square:
            af = av.astype(F32)
            av = (af * af).astype(BF16)
        part = jnp.dot(av, w_ref[...], preferred_element_type=F32)

        def finish(r):
            if relu:
                r = jnp.maximum(r, 0.0)
            if e_ref is not None:
                r = 2.0 * e_ref[...].astype(F32) * r
            o_ref[...] = r.astype(out_dtype)

        if nk == 1:
            finish(part)
        else:
            @pl.when(kk == 0)
            def _():
                acc_ref[...] = part

            @pl.when(kk > 0)
            def _():
                acc_ref[...] += part

            @pl.when(kk == nk - 1)
            def _():
                finish(acc_ref[...])

    in_specs = [pl.BlockSpec((tm, tk), lambda i, j, kk: (i, kk)),
                pl.BlockSpec((tk, tn), lambda i, j, kk: (kk, j))]
    args = [a, w]
    if mul2 is not None:
        in_specs.append(pl.BlockSpec((tm, tn), lambda i, j, kk: (i, j)))
        args.append(mul2)
    acc_shape = (tm, tn) if nk > 1 else (8, 128)
    return pl.pallas_call(
        body, name=name, grid=(m // tm, n // tn, nk),
        in_specs=in_specs, out_specs=pl.BlockSpec((tm, tn), lambda i, j, kk: (i, j)),
        out_shape=jax.ShapeDtypeStruct((m, n), out_dtype),
        scratch_shapes=[pltpu.VMEM(acc_shape, F32)],
        compiler_params=_params(("parallel", "parallel", "arbitrary")),
    )(*args)


def _mm_tn(a, b, *, tm, tn, tt, name, a_square=False, n_split=1):
    t, m = a.shape
    _, n = b.shape
    assert t % tt == 0 and m % tm == 0 and n % tn == 0 and (n // n_split) % tn == 0
    per = n // n_split // tn

    def body(a_ref, b_ref, o_ref):
        ti = pl.program_id(2)
        av = a_ref[...]
        if a_square:
            af = av.astype(F32)
            av = (af * af).astype(BF16)
        part = lax.dot_general(av, b_ref[...], TN, preferred_element_type=F32)

        @pl.when(ti == 0)
        def _():
            o_ref[...] = part

        @pl.when(ti > 0)
        def _():
            o_ref[...] += part

    return pl.pallas_call(
        body, name=name, grid=(m // tm, n // tn, t // tt),
        in_specs=[pl.BlockSpec((tt, tm), lambda i, j, ti: (ti, i)),
                  pl.BlockSpec((tt, tn), lambda i, j, ti: (ti, j))],
        out_specs=pl.BlockSpec((None, tm, tn), lambda i, j, ti: (j // per, i, j % per)),
        out_shape=jax.ShapeDtypeStruct((n_split, m, n // n_split), F32),
        compiler_params=_params(("parallel", "parallel", "arbitrary")),
    )(a, b)


def _rstd(x):
    return lax.rsqrt(jnp.mean(x * x, axis=-1, keepdims=True) + EPS)


def _rms_cast(x, g, *, tm, name):
    t, d = x.shape

    def body(x_ref, g_ref, o_ref):
        xv = x_ref[...]
        o_ref[...] = (xv * _rstd(xv) * g_ref[...]).astype(BF16)

    return pl.pallas_call(
        body, name=name, grid=(t // tm,),
        in_specs=[pl.BlockSpec((tm, d), lambda i: (i, 0)), pl.BlockSpec((1, d), lambda i: (0, 0))],
        out_specs=pl.BlockSpec((tm, d), lambda i: (i, 0)),
        out_shape=jax.ShapeDtypeStruct((t, d), BF16),
        compiler_params=_params(("parallel",)),
    )(x, g)


def _mix_cat(attn, rnn, gain, *, tm, name):
    t = attn.shape[0]

    def body(a_ref, r_ref, g_ref, o_ref):
        av = a_ref[...]
        o_ref[:, :ATTN_W] = (av * _rstd(av) * g_ref[...]).astype(BF16)
        o_ref[:, ATTN_W:] = r_ref[...].astype(BF16)

    return pl.pallas_call(
        body, name=name, grid=(t // tm,),
        in_specs=[pl.BlockSpec((tm, ATTN_W), lambda i: (i, 0)), pl.BlockSpec((tm, RNN_W), lambda i: (i, 0)),
                  pl.BlockSpec((1, ATTN_W), lambda i: (0, 0))],
        out_specs=pl.BlockSpec((tm, D_MODEL), lambda i: (i, 0)),
        out_shape=jax.ShapeDtypeStruct((t, D_MODEL), BF16),
        compiler_params=_params(("parallel",)),
    )(attn, rnn, gain)


def _post_norm_res(mixed, g_post, res, g_next, *, tm, name):
    t, d = mixed.shape

    def body(m_ref, gp_ref, r_ref, gn_ref, x1_ref, h2_ref):
        mv = m_ref[...]
        x1 = r_ref[...] + mv * _rstd(mv) * gp_ref[...]
        x1_ref[...] = x1
        h2_ref[...] = (x1 * _rstd(x1) * gn_ref[...]).astype(BF16)

    row = pl.BlockSpec((tm, d), lambda i: (i, 0))
    vec = pl.BlockSpec((1, d), lambda i: (0, 0))
    return pl.pallas_call(
        body, name=name, grid=(t // tm,),
        in_specs=[row, vec, row, vec], out_specs=[row, row],
        out_shape=[jax.ShapeDtypeStruct((t, d), F32), jax.ShapeDtypeStruct((t, d), BF16)],
        compiler_params=_params(("parallel",)),
    )(mixed, g_post, res, g_next)


def _rms_bwd(dyn, xin, g, res, *, tm, out_dtype, name, col_block=0):
    t, d = xin.shape

    def body(*refs):
        if res is not None:
            dy_ref, x_ref, g_ref, r_ref, dx_ref, dg_ref = refs
        else:
            dy_ref, x_ref, g_ref, dx_ref, dg_ref = refs
        i = pl.program_id(0)
        xv = x_ref[...]
        dy = dy_ref[...].astype(F32)
        r = _rstd(xv)
        xh = xv * r
        part = jnp.sum(dy * xh, axis=0, keepdims=True)

        @pl.when(i == 0)
        def _():
            dg_ref[...] = part

        @pl.when(i > 0)
        def _():
            dg_ref[...] += part

        tt = dy * g_ref[...]
        dx = r * (tt - xh * jnp.mean(tt * xh, axis=-1, keepdims=True))
        if res is not None:
            dx = dx + r_ref[...]
        dx_ref[...] = dx.astype(out_dtype)

    row = pl.BlockSpec((tm, d), lambda i: (i, 0))
    vec = pl.BlockSpec((1, d), lambda i: (0, 0))
    in_specs = [pl.BlockSpec((tm, d), lambda i: (i, col_block)), row, vec]
    args = [dyn, xin, g]
    if res is not None:
        in_specs.append(row)
        args.append(res)
    return pl.pallas_call(
        body, name=name, grid=(t // tm,),
        in_specs=in_specs, out_specs=[row, vec],
        out_shape=[jax.ShapeDtypeStruct((t, d), out_dtype), jax.ShapeDtypeStruct((1, d), F32)],
        compiler_params=_params(("arbitrary",)),
    )(*args)


def _loss_head(y, g_post, x1, target, *, tm, name):
    t, d = y.shape

    def body(y_ref, g_ref, x1_ref, t_ref, dy_ref, dx2_ref, loss_ref, dg_ref):
        i = pl.program_id(0)
        yv = y_ref[...]
        r = _rstd(yv)
        yh = yv * r
        gv = g_ref[...]
        err = x1_ref[...] + yh * gv - t_ref[...]
        lpart = 0.5 * jnp.sum(jnp.mean(err * err, axis=-1, keepdims=True), axis=0, keepdims=True)
        dx2 = err * (1.0 / d)
        dgp = jnp.sum(dx2 * yh, axis=0, keepdims=True)
        lane = lax.broadcasted_iota(jnp.int32, (1, 128), 1)
        lrow = jnp.where(lane == 0, lpart, 0.0)

        @pl.when(i == 0)
        def _():
            dg_ref[...] = dgp
            loss_ref[...] = lrow

        @pl.when(i > 0)
        def _():
            dg_ref[...] += dgp
            loss_ref[...] += lrow

        tt = dx2 * gv
        dy_ref[...] = (r * (tt - yh * jnp.mean(tt * yh, axis=-1, keepdims=True))).astype(BF16)
        dx2_ref[...] = dx2

    row = pl.BlockSpec((tm, d), lambda i: (i, 0))
    vec = pl.BlockSpec((1, d), lambda i: (0, 0))
    return pl.pallas_call(
        body, name=name, grid=(t // tm,),
        in_specs=[row, vec, row, row],
        out_specs=[row, row, pl.BlockSpec((1, 128), lambda i: (0, 0)), vec],
        out_shape=[jax.ShapeDtypeStruct((t, d), BF16), jax.ShapeDtypeStruct((t, d), F32),
                   jax.ShapeDtypeStruct((1, 128), F32), jax.ShapeDtypeStruct((1, d), F32)],
        compiler_params=_params(("arbitrary",)),
    )(y, g_post, x1, target)


def _alibi_slope(h):
    return 2.0 ** (-8.0 * (h + 1) / N_Q)


def _swa_mask(n):
    row = lax.broadcasted_iota(jnp.int32, (BLK, 2 * BLK), 0)
    col = lax.broadcasted_iota(jnp.int32, (BLK, 2 * BLK), 1)
    dist = row + BLK - col
    valid = (dist >= 0) & (dist < BLK) & ((col >= BLK) | (n > 0))
    return valid, dist.astype(F32)


def _swa_scores(q_ref, kcat, hh, valid, distf):
    qh = q_ref[:, hh * HEAD_DIM:(hh + 1) * HEAD_DIM].astype(BF16)
    s = lax.dot_general(qh, kcat, NT, preferred_element_type=F32) * (HEAD_DIM ** -0.5)
    s = s - _alibi_slope(hh) * distf
    return qh, jnp.where(valid, s, -1e30)


def _kv_cat(kvp_ref, kvc_ref, off):
    return jnp.concatenate([kvp_ref[:, off:off + HEAD_DIM], kvc_ref[:, off:off + HEAD_DIM]], axis=0).astype(BF16)


def _swa_fwd(proj, sinks, *, name):
    t = proj.shape[0]
    nb = t // BLK
    kvb = KV_COL // (2 * 128)

    def body(q_ref, kvc_ref, kvp_ref, sink_ref, o_ref, lse_ref):
        n = pl.program_id(0)
        valid, distf = _swa_mask(n)
        for kvh in range(N_KV):
            kcat = _kv_cat(kvp_ref, kvc_ref, kvh * HEAD_DIM)
            vcat = _kv_cat(kvp_ref, kvc_ref, 128 + kvh * HEAD_DIM)
            for gi in range(GROUP):
                hh = kvh * GROUP + gi
                _, s = _swa_scores(q_ref, kcat, hh, valid, distf)
                sink = sink_ref[0:1, hh:hh + 1]
                mx = jnp.maximum(jnp.max(s, axis=1, keepdims=True), sink)
                p = jnp.exp(s - mx)
                l = jnp.sum(p, axis=1, keepdims=True) + jnp.exp(sink - mx)
                probs = p / l
                o_ref[:, hh * HEAD_DIM:(hh + 1) * HEAD_DIM] = jnp.dot(
                    probs.astype(BF16), vcat, preferred_element_type=F32)
                lse_ref[:, hh:hh + 1] = mx + jnp.log(l)

    return pl.pallas_call(
        body, name=name, grid=(nb,),
        in_specs=[pl.BlockSpec((BLK, ATTN_W), lambda n: (n, 0)),
                  pl.BlockSpec((BLK, 256), lambda n: (n, kvb)),
                  pl.BlockSpec((BLK, 256), lambda n: (jnp.maximum(n - 1, 0), kvb)),
                  pl.BlockSpec((1, N_Q), lambda n: (0, 0))],
        out_specs=[pl.BlockSpec((BLK, ATTN_W), lambda n: (n, 0)), pl.BlockSpec((BLK, N_Q), lambda n: (n, 0))],
        out_shape=[jax.ShapeDtypeStruct((t, ATTN_W), F32), jax.ShapeDtypeStruct((t, N_Q), F32)],
        compiler_params=_params(("parallel",)),
    )(proj, proj, proj, sinks)


def _swa_bwd(proj, sinks, dattn, lse, *, name):
    t = proj.shape[0]
    nb = t // BLK
    kvb = KV_COL // (2 * 128)

    def body(q_ref, kvc_ref, kvp_ref, sink_ref, do_ref, lse_ref, dq_ref, dkv_ref, dsink_ref, carry_ref):
        n = pl.program_id(0)

        @pl.when(n == 0)
        def _():
            dsink_ref[...] = jnp.zeros_like(dsink_ref)
            carry_ref[...] = jnp.zeros_like(carry_ref)

        @pl.when(n < nb)
        def _():
            valid, distf = _swa_mask(n)
            for kvh in range(N_KV):
                kcat = _kv_cat(kvp_ref, kvc_ref, kvh * HEAD_DIM)
                vcat = _kv_cat(kvp_ref, kvc_ref, 128 + kvh * HEAD_DIM)
                dk_cat = jnp.zeros((2 * BLK, HEAD_DIM), F32)
                dv_cat = jnp.zeros((2 * BLK, HEAD_DIM), F32)
                for gi in range(GROUP):
                    hh = kvh * GROUP + gi
                    qh, s = _swa_scores(q_ref, kcat, hh, valid, distf)
                    lse_h = lse_ref[:, hh:hh + 1]
                    probs = jnp.exp(s - lse_h)
                    doh = do_ref[:, hh * HEAD_DIM:(hh + 1) * HEAD_DIM].astype(BF16)
                    dprobs = lax.dot_general(doh, vcat, NT, preferred_element_type=F32)
                    delta = jnp.sum(probs * dprobs, axis=1, keepdims=True)
                    ds = (probs * (dprobs - delta)).astype(BF16)
                    psink = jnp.exp(sink_ref[0:1, hh:hh + 1] - lse_h)
                    dsink_ref[0:1, hh:hh + 1] += -jnp.sum(psink * delta, axis=0, keepdims=True)
                    dv_cat = dv_cat + lax.dot_general(probs.astype(BF16), doh, TN, preferred_element_type=F32)
                    dq_ref[:, hh * HEAD_DIM:(hh + 1) * HEAD_DIM] = jnp.dot(
                        ds, kcat, preferred_element_type=F32) * (HEAD_DIM ** -0.5)
                    dk_cat = dk_cat + lax.dot_general(ds, qh, TN, preferred_element_type=F32)
                dk_cat = dk_cat * (HEAD_DIM ** -0.5)
                ko = kvh * HEAD_DIM
                vo = 128 + kvh * HEAD_DIM
                dkv_ref[:, ko:ko + HEAD_DIM] = carry_ref[:, ko:ko + HEAD_DIM] + dk_cat[:BLK]
                dkv_ref[:, vo:vo + HEAD_DIM] = carry_ref[:, vo:vo + HEAD_DIM] + dv_cat[:BLK]
                carry_ref[:, ko:ko + HEAD_DIM] = dk_cat[BLK:]
                carry_ref[:, vo:vo + HEAD_DIM] = dv_cat[BLK:]

        @pl.when(n == nb)
        def _():
            dkv_ref[...] = carry_ref[...]

    last = nb - 1
    return pl.pallas_call(
        body, name=name, grid=(nb + 1,),
        in_specs=[pl.BlockSpec((BLK, ATTN_W), lambda n: (jnp.minimum(n, last), 0)),
                  pl.BlockSpec((BLK, 256), lambda n: (jnp.minimum(n, last), kvb)),
                  pl.BlockSpec((BLK, 256), lambda n: (jnp.maximum(jnp.minimum(n, last) - 1, 0), kvb)),
                  pl.BlockSpec((1, N_Q), lambda n: (0, 0)),
                  pl.BlockSpec((BLK, ATTN_W), lambda n: (jnp.minimum(n, last), 0)),
                  pl.BlockSpec((BLK, N_Q), lambda n: (jnp.minimum(n, last), 0))],
        out_specs=[pl.BlockSpec((BLK, ATTN_W), lambda n: (jnp.minimum(n, last), 0)),
                   pl.BlockSpec((BLK, 256), lambda n: (jnp.maximum(n - 1, 0), 0)),
                   pl.BlockSpec((1, N_Q), lambda n: (0, 0))],
        out_shape=[jax.ShapeDtypeStruct((t, ATTN_W), F32), jax.ShapeDtypeStruct((t, 256), F32),
                   jax.ShapeDtypeStruct((1, N_Q), F32)],
        scratch_shapes=[pltpu.VMEM((BLK, 256), F32)],
        compiler_params=_params(("arbitrary",)),
    )(proj, proj, proj, sinks, dattn, lse)


def _cumsum_rows(x):
    n = x.shape[0]
    row = lax.broadcasted_iota(jnp.int32, x.shape, 0)
    s = 1
    while s < n:
        x = x + jnp.where(row >= s, pltpu.roll(x, s, axis=0), 0.0)
        s *= 2
    return x


def _rev_cumsum_rows(x):
    n = x.shape[0]
    row = lax.broadcasted_iota(jnp.int32, x.shape, 0)
    s = 1
    while s < n:
        x = x + jnp.where(row < n - s, pltpu.roll(x, n - s, axis=0), 0.0)
        s *= 2
    return x


def _lower_bound(lbl_ref):
    l0 = lbl_ref[0:1, :]
    l1 = lbl_ref[1:2, :]
    mx = jnp.maximum(l0, l1)
    e0 = jnp.exp(l0 - mx)
    e1 = jnp.exp(l1 - mx)
    return e0 / (e0 + e1)


def _hgrn_gates(z, lb):
    sg = _sigmoid(z)
    f = lb + (1.0 - lb) * sg
    return sg, f, jnp.log(f), 1.0 - f


def _sub_factors(b, i):
    rows = lax.broadcasted_iota(jnp.int32, (CHUNK, RNN_HD), 0)
    ref = b[SUB * i - 1:SUB * i, :]
    qfac = jnp.exp(b[SUB * i:SUB * (i + 1), :] - ref)
    kfac = jnp.where(rows < SUB * i, jnp.exp(jnp.minimum(ref - b, 0.0)), 0.0)
    return qfac, kfac


def _diag_decay(bi, s):
    trow = lax.broadcasted_iota(jnp.int32, (SUB, RNN_HD), 0)
    return jnp.where(trow >= s, jnp.exp(jnp.minimum(bi - bi[s:s + 1, :], 0.0)), 0.0)


def _hgrn_fwd(proj, lb_logits, norm_gain, *, tb, name):
    t = proj.shape[0]
    ntb = t // tb
    nch = tb // CHUNK
    qb, fb, ib, gb = QR_COL // 128, FR_COL // 128, IR_COL // 128, GR_COL // 128

    def body(q_ref, f_ref, i_ref, g_ref, lbl_ref, gain_ref, o_ref, out_ref, s0_ref, st_ref, ob_ref):
        c = pl.program_id(1)

        @pl.when(c == 0)
        def _():
            st_ref[...] = jnp.zeros_like(st_ref)

        lb = _lower_bound(lbl_ref)
        gain = gain_ref[...]

        def chunk(ci, carry):
            r0 = pl.multiple_of(ci * CHUNK, CHUNK)
            rows = pl.ds(r0, CHUNK)
            _, _, lf, k = _hgrn_gates(f_ref[rows, :], lb)
            qr = q_ref[rows, :]
            q = qr * _sigmoid(qr)
            v = i_ref[rows, :]
            b = _cumsum_rows(lf)
            st = st_ref[...]
            s0_ref[ci] = st
            ob_ref[...] = lax.dot_general((q * jnp.exp(b)).astype(BF16), st.astype(BF16), NT,
                                          preferred_element_type=F32)
            vb = v.astype(BF16)
            for i in range(CHUNK // SUB):
                blk = slice(SUB * i, SUB * (i + 1))
                qi, ki, vi, bi = q[blk], k[blk], v[blk], b[blk]
                oi = ob_ref[blk, :]
                if i > 0:
                    qfac, kfac = _sub_factors(b, i)
                    att = lax.dot_general((qi * qfac).astype(BF16), (k * kfac).astype(BF16), NT,
                                          preferred_element_type=F32)
                    oi = oi + jnp.dot(att.astype(BF16), vb, preferred_element_type=F32)
                for s in range(SUB):
                    e = _diag_decay(bi, s)
                    a = jnp.sum(qi * ki[s:s + 1, :] * e, axis=1, keepdims=True)
                    oi = oi + a * vi[s:s + 1, :]
                ob_ref[blk, :] = oi
            blast = b[CHUNK - 1:CHUNK, :]
            khat = (k * jnp.exp(blast - b)).astype(BF16)
            st_ref[...] = st * jnp.exp(blast) + lax.dot_general(vb, khat, TN, preferred_element_type=F32)
            o = ob_ref[...]
            o_ref[rows, :] = o
            gr = g_ref[rows, :]
            out_ref[rows, :] = o * _rstd(o) * gain * (gr * _sigmoid(gr))
            return carry

        lax.fori_loop(0, nch, chunk, 0)

    def col(base):
        return pl.BlockSpec((tb, RNN_HD), lambda h, c: (c, base + h))

    return pl.pallas_call(
        body, name=name, grid=(N_RNN, ntb),
        in_specs=[col(qb), col(fb), col(ib), col(gb),
                  pl.BlockSpec((2, RNN_HD), lambda h, c: (0, h)), pl.BlockSpec((1, RNN_HD), lambda h, c: (0, 0))],
        out_specs=[pl.BlockSpec((tb, RNN_HD), lambda h, c: (c, h)), pl.BlockSpec((tb, RNN_HD), lambda h, c: (c, h)),
                   pl.BlockSpec((None, nch, RNN_HD, RNN_HD), lambda h, c: (h, c, 0, 0))],
        out_shape=[jax.ShapeDtypeStruct((t, RNN_W), F32), jax.ShapeDtypeStruct((t, RNN_W), F32),
                   jax.ShapeDtypeStruct((N_RNN, t // CHUNK, RNN_HD, RNN_HD), F32)],
        scratch_shapes=[pltpu.VMEM((RNN_HD, RNN_HD), F32), pltpu.VMEM((CHUNK, RNN_HD), F32)],
        compiler_params=_params(("parallel", "arbitrary")),
    )(proj, proj, proj, proj, lb_logits, norm_gain)


def _hgrn_bwd(proj, lb_logits, norm_gain, o_pre, s0, dcat, *, tb, name):
    t = proj.shape[0]
    ntb = t // tb
    nch = tb // CHUNK
    qb, fb, ib, gb = QR_COL // 128, FR_COL // 128, IR_COL // 128, GR_COL // 128
    nsub = CHUNK // SUB

    def body(q_ref, f_ref, i_ref, g_ref, lbl_ref, gain_ref, o_ref, s0_ref, dout_ref,
             dq_ref, df_ref, di_ref, dg_ref, dlb_ref, dgain_ref,
             dst_ref, dqa_ref, dka_ref, dva_ref):
        c = pl.program_id(1)

        @pl.when(c == 0)
        def _():
            dst_ref[...] = jnp.zeros_like(dst_ref)
            dlb_ref[...] = jnp.zeros_like(dlb_ref)
            dgain_ref[...] = jnp.zeros_like(dgain_ref)

        lb = _lower_bound(lbl_ref)
        gain = gain_ref[...]

        def chunk(cj, carry):
            ci = nch - 1 - cj
            r0 = pl.multiple_of(ci * CHUNK, CHUNK)
            rows = pl.ds(r0, CHUNK)
            sg, f, lf, k = _hgrn_gates(f_ref[rows, :], lb)
            qr = q_ref[rows, :]
            sq = _sigmoid(qr)
            q = qr * sq
            v = i_ref[rows, :]
            b = _cumsum_rows(lf)

            dout = dout_ref[rows, :]
            o = o_ref[rows, :]
            gr = g_ref[rows, :]
            sgg = _sigmoid(gr)
            gate = gr * sgg
            rs = _rstd(o)
            nrm = o * rs
            dg_ref[rows, :] = dout * nrm * gain * (sgg * (1.0 + gr * (1.0 - sgg)))
            dn = dout * gate
            dgain_ref[...] += jnp.sum(dn * nrm, axis=0, keepdims=True)
            tt = dn * gain
            do = rs * (tt - nrm * jnp.mean(tt * nrm, axis=-1, keepdims=True))

            dob = do.astype(BF16)
            vb = v.astype(BF16)
            eb = jnp.exp(b)
            blast = b[CHUNK - 1:CHUNK, :]
            ebl = jnp.exp(blast - b)
            dst = dst_ref[...]
            dstb = dst.astype(BF16)
            khat = (k * ebl).astype(BF16)
            s0 = s0_ref[ci]
            dqa_ref[...] = eb * jnp.dot(dob, s0.astype(BF16), preferred_element_type=F32)
            dk_state = ebl * jnp.dot(vb, dstb, preferred_element_type=F32)
            dka_ref[...] = dk_state
            d_blast = (jnp.sum(k * dk_state, axis=0, keepdims=True)
                       + jnp.exp(blast) * jnp.sum(dst * s0, axis=0, keepdims=True))
            dva_ref[...] = lax.dot_general(khat, dstb, NT, preferred_element_type=F32)
            dst_ref[...] = dst * jnp.exp(blast) + lax.dot_general(dob, (q * eb).astype(BF16), TN,
                                                                  preferred_element_type=F32)
            pm = lax.dot_general(dob, vb, NT, preferred_element_type=F32)
            for i in range(nsub):
                blk = slice(SUB * i, SUB * (i + 1))
                qi, ki, vi, bi, doi = q[blk], k[blk], v[blk], b[blk], do[blk]
                dqi = dqa_ref[blk, :]
                if i > 0:
                    qfac, kfac = _sub_factors(b, i)
                    qt = (qi * qfac).astype(BF16)
                    kt = (k * kfac).astype(BF16)
                    att = lax.dot_general(qt, kt, NT, preferred_element_type=F32).astype(BF16)
                    pmi = pm[blk, :].astype(BF16)
                    dva_ref[...] += lax.dot_general(att, doi.astype(BF16), TN, preferred_element_type=F32)
                    dqi = dqi + qfac * jnp.dot(pmi, kt, preferred_element_type=F32)
                    dka_ref[...] += kfac * lax.dot_general(pmi, qt, TN, preferred_element_type=F32)
                dki = dka_ref[blk, :]
                dvi = dva_ref[blk, :]
                trow = lax.broadcasted_iota(jnp.int32, (SUB, RNN_HD), 0)
                for s in range(SUB):
                    e = _diag_decay(bi, s)
                    ks = ki[s:s + 1, :]
                    vs = vi[s:s + 1, :]
                    a = jnp.sum(qi * ks * e, axis=1, keepdims=True)
                    p = jnp.sum(doi * vs, axis=1, keepdims=True)
                    dvs = jnp.sum(a * doi, axis=0, keepdims=True)
                    dks = jnp.sum(p * qi * e, axis=0, keepdims=True)
                    dqi = dqi + p * ks * e
                    dki = dki + jnp.where(trow == s, dks, 0.0)
                    dvi = dvi + jnp.where(trow == s, dvs, 0.0)
                dqa_ref[blk, :] = dqi
                dka_ref[blk, :] = dki
                dva_ref[blk, :] = dvi

            dq = dqa_ref[...]
            dk = dka_ref[...]
            lastrow = lax.broadcasted_iota(jnp.int32, (CHUNK, RNN_HD), 0) == CHUNK - 1
            dlf = _rev_cumsum_rows(q * dq - k * dk + jnp.where(lastrow, d_blast, 0.0))
            dff = dlf / f - dk
            df_ref[rows, :] = dff * (1.0 - lb) * sg * (1.0 - sg)
            dlb_ref[...] += jnp.sum(dff * (1.0 - sg), axis=0, keepdims=True)
            dq_ref[rows, :] = dq * (sq * (1.0 + qr * (1.0 - sq)))
            di_ref[rows, :] = dva_ref[...]
            return carry

        lax.fori_loop(0, nch, chunk, 0)

    def col(base):
        return pl.BlockSpec((tb, RNN_HD), lambda h, c: (ntb - 1 - c, base + h))

    outc = pl.BlockSpec((tb, RNN_HD), lambda h, c: (ntb - 1 - c, h))
    hb = ATTN_W // RNN_HD
    return pl.pallas_call(
        body, name=name, grid=(N_RNN, ntb),
        in_specs=[col(qb), col(fb), col(ib), col(gb),
                  pl.BlockSpec((2, RNN_HD), lambda h, c: (0, h)), pl.BlockSpec((1, RNN_HD), lambda h, c: (0, 0)),
                  outc,
                  pl.BlockSpec((None, nch, RNN_HD, RNN_HD), lambda h, c: (h, ntb - 1 - c, 0, 0)),
                  pl.BlockSpec((tb, RNN_HD), lambda h, c: (ntb - 1 - c, hb + h))],
        out_specs=[outc, outc, outc, outc,
                   pl.BlockSpec((1, RNN_HD), lambda h, c: (0, h)),
                   pl.BlockSpec((None, 1, RNN_HD), lambda h, c: (h, 0, 0))],
        out_shape=[jax.ShapeDtypeStruct((t, RNN_W), F32)] * 4
        + [jax.ShapeDtypeStruct((1, RNN_W), F32), jax.ShapeDtypeStruct((N_RNN, 1, RNN_HD), F32)],
        scratch_shapes=[pltpu.VMEM((RNN_HD, RNN_HD), F32),
                        pltpu.VMEM((CHUNK, RNN_HD), F32), pltpu.VMEM((CHUNK, RNN_HD), F32),
                        pltpu.VMEM((CHUNK, RNN_HD), F32)],
        compiler_params=_params(("parallel", "arbitrary")),
    )(proj, proj, proj, proj, lb_logits, norm_gain, o_pre, s0, dcat)


def _place():
    return lax.axis_index("x"), lax.axis_index("y"), lax.axis_index("c")


ANY = pl.BlockSpec(memory_space=pl.ANY)


def _all_gather_halves(shards, *, name):
    n = len(shards)

    def body(*refs):
        ins, outs = refs[:n], refs[n:2 * n]
        send_sems, recv_sems, local_sems = refs[2 * n:]
        x, y, c = _place()
        sibling = (x, y, 1 - c)
        chips = [(1 - x, y), (x, 1 - y), (1 - x, 1 - y)]

        def copy(a, k, block, to, src=None):
            slot = outs[a].at[4 * block[0] + 2 * block[1] + block[2]]
            return pltpu.make_async_remote_copy(
                src_ref=slot if src is None else src, dst_ref=slot,
                send_sem=send_sems.at[a, k], recv_sem=recv_sems.at[a, k],
                device_id=to, device_id_type=MESH)

        mine, first, passed = [], [], []
        for a in range(n):
            cp = pltpu.make_async_copy(ins[a], outs[a].at[pl.ds(4 * x + 2 * y, 2)], local_sems.at[a])
            cp.start()
            mine.append(cp)
            for j, chip in enumerate(chips):
                cp = copy(a, j, (x, y, c), (*chip, c), src=ins[a].at[c])
                cp.start()
                first.append(cp)
        for a in range(n):
            for j, chip in enumerate(chips):
                copy(a, j, (*chip, c), (x, y, c)).wait_recv()
                cp = copy(a, 3 + j, (*chip, c), sibling)
                cp.start()
                passed.append(cp)
        for a in range(n):
            for j, chip in enumerate(chips):
                copy(a, 3 + j, (*chip, 1 - c), (x, y, c)).wait_recv()
        for cp in first + passed:
            cp.wait_send()
        for cp in mine:
            cp.wait()

    return pl.pallas_call(
        body, name=name,
        in_specs=[ANY] * n, out_specs=[ANY] * n,
        out_shape=[jax.ShapeDtypeStruct((8,) + s.shape[1:], s.dtype) for s in shards],
        scratch_shapes=[pltpu.SemaphoreType.DMA((n, 6)), pltpu.SemaphoreType.DMA((n, 6)),
                        pltpu.SemaphoreType.DMA((n,))],
    )(*shards)


def _pair_exchange(grads, *, name):
    n = len(grads)

    def body(*refs):
        ins, outs = refs[:n], refs[n:2 * n]
        send_sems, recv_sems = refs[2 * n:]
        x, y, c = _place()
        copies = []
        for a in range(n):
            cp = pltpu.make_async_remote_copy(
                src_ref=ins[a].at[:, 1 - c], dst_ref=outs[a],
                send_sem=send_sems.at[a], recv_sem=recv_sems.at[a],
                device_id=(x, y, 1 - c), device_id_type=MESH)
            cp.start()
            copies.append(cp)
        for cp in copies:
            cp.wait()

    return pl.pallas_call(
        body, name=name,
        in_specs=[ANY] * n, out_specs=[ANY] * n,
        out_shape=[jax.ShapeDtypeStruct((4,) + g.shape[2:], g.dtype) for g in grads],
        scratch_shapes=[pltpu.SemaphoreType.DMA((n,)), pltpu.SemaphoreType.DMA((n,))],
    )(*grads)


def _chip_exchange(wires, *, name):
    n = len(wires)

    def body(*refs):
        ins, outs = refs[:n], refs[n:2 * n]
        send_sems, recv_sems = refs[2 * n:]
        x, y, c = _place()
        chips = [(1 - x, y), (x, 1 - y), (1 - x, 1 - y)]
        copies = []
        for a in range(n):
            for j, (px, py) in enumerate(chips):
                cp = pltpu.make_async_remote_copy(
                    src_ref=ins[a].at[2 * px + py], dst_ref=outs[a].at[j],
                    send_sem=send_sems.at[a, j], recv_sem=recv_sems.at[a, j],
                    device_id=(px, py, c), device_id_type=MESH)
                cp.start()
                copies.append(cp)
        for cp in copies:
            cp.wait()

    return pl.pallas_call(
        body, name=name,
        in_specs=[ANY] * n, out_specs=[ANY] * n,
        out_shape=[jax.ShapeDtypeStruct((3,) + w.shape[1:], w.dtype) for w in wires],
        scratch_shapes=[pltpu.SemaphoreType.DMA((n, 3)), pltpu.SemaphoreType.DMA((n, 3))],
    )(*wires)


def _sibling_share(halves, *, name):
    n = len(halves)

    def body(*refs):
        ins, outs = refs[:n], refs[n:2 * n]
        send_sems, recv_sems, local_sems = refs[2 * n:]
        x, y, c = _place()
        copies, local = [], []
        for a in range(n):
            lc = pltpu.make_async_copy(ins[a], outs[a].at[c], local_sems.at[a])
            lc.start()
            local.append(lc)
            cp = pltpu.make_async_remote_copy(
                src_ref=ins[a], dst_ref=outs[a].at[c],
                send_sem=send_sems.at[a], recv_sem=recv_sems.at[a],
                device_id=(x, y, 1 - c), device_id_type=MESH)
            cp.start()
            copies.append(cp)
        for cp in copies:
            cp.wait()
        for lc in local:
            lc.wait()

    return pl.pallas_call(
        body, name=name,
        in_specs=[ANY] * n, out_specs=[ANY] * n,
        out_shape=[jax.ShapeDtypeStruct((2,) + h.shape, h.dtype) for h in halves],
        scratch_shapes=[pltpu.SemaphoreType.DMA((n,)), pltpu.SemaphoreType.DMA((n,)),
                        pltpu.SemaphoreType.DMA((n,))],
    )(*halves)


def _row_tile(rows, cols, budget=1 << 20):
    tr = rows
    while tr * cols > budget and tr % 16 == 0:
        tr //= 2
    return tr


def _pair_sum(g, sib, where, *, name):
    _, _, rh, cols = g.shape
    tr = _row_tile(rh, cols)

    def body(w_ref, g_ref, s_ref, o_ref):
        o_ref[...] = (g_ref[...] + s_ref[...]).astype(BF16)

    return pl.pallas_call(
        body, name=name,
        grid_spec=pltpu.PrefetchScalarGridSpec(
            num_scalar_prefetch=1, grid=(4, rh // tr),
            in_specs=[pl.BlockSpec((None, None, tr, cols), lambda s, i, w: (s, w[1], i, 0)),
                      pl.BlockSpec((None, tr, cols), lambda s, i, w: (s, i, 0))],
            out_specs=pl.BlockSpec((None, tr, cols), lambda s, i, w: (s, i, 0))),
        out_shape=jax.ShapeDtypeStruct((4, rh, cols), BF16),
        compiler_params=_params(("parallel", "parallel")),
    )(where, g, sib)


def _final_half(g, sib, recv, where, *, name):
    _, _, rh, cols = g.shape
    tr = _row_tile(rh, cols)

    def body(w_ref, g_ref, s_ref, r_ref, o_ref):
        acc = g_ref[...] + s_ref[...]
        for j in range(3):
            acc = acc + r_ref[j].astype(F32)
        o_ref[...] = acc

    return pl.pallas_call(
        body, name=name,
        grid_spec=pltpu.PrefetchScalarGridSpec(
            num_scalar_prefetch=1, grid=(rh // tr,),
            in_specs=[pl.BlockSpec((None, None, tr, cols), lambda i, w: (w[0], w[1], i, 0)),
                      pl.BlockSpec((None, tr, cols), lambda i, w: (w[0], i, 0)),
                      pl.BlockSpec((3, tr, cols), lambda i, w: (0, i, 0))],
            out_specs=pl.BlockSpec((tr, cols), lambda i, w: (i, 0))),
        out_shape=jax.ShapeDtypeStruct((rh, cols), F32),
        compiler_params=_params(("parallel",)),
    )(where, g, sib, recv)


def _adamw_math(w, g, m, v):
    m = ADAM_B1 * m + (1.0 - ADAM_B1) * g
    v = ADAM_B2 * v + (1.0 - ADAM_B2) * (g * g)
    m_hat = m / (1.0 - ADAM_B1 ** ADAM_STEP)
    v_hat = v / (1.0 - ADAM_B2 ** ADAM_STEP)
    delta = -ADAM_LR * (m_hat / (jnp.sqrt(v_hat) + ADAM_EPS) + ADAM_WD * w)
    return delta, m, v


def _adamw(w, g, m, v, *, name):
    rows, cols = w.shape
    tr = _row_tile(rows, cols, budget=1 << 19)

    def body(w_ref, g_ref, m_ref, v_ref, d_ref, nm_ref, nv_ref):
        d, nm, nv = _adamw_math(w_ref[...], g_ref[...], m_ref[...], v_ref[...])
        d_ref[...] = d
        nm_ref[...] = nm
        nv_ref[...] = nv

    blk = pl.BlockSpec((tr, cols), lambda i: (i, 0))
    return pl.pallas_call(
        body, name=name, grid=(rows // tr,),
        in_specs=[blk] * 4, out_specs=[blk] * 3,
        out_shape=[jax.ShapeDtypeStruct((rows, cols), F32)] * 3,
        compiler_params=_params(("parallel",)),
    )(w, g, m, v)


SEG_LOSS = 0
SEG_SINK = 128
SEG_AGAIN = 256
SEG_L0 = SEG_AGAIN + ATTN_W
SEG_L1 = SEG_L0 + RNN_W
SEG_RGAIN = SEG_L1 + RNN_W
SEG_G = SEG_RGAIN + 128
N_PACK = SEG_G + 4 * D_MODEL


def _pack(sinks, again, l0, l1, rgain, gains, loss=None):
    z = lambda k: jnp.zeros((1, k), F32)
    first = z(128) if loss is None else loss
    return jnp.concatenate([first, sinks, z(128 - N_Q), again, l0, l1, rgain] + list(gains), axis=1)


def _small_reduce_adamw(part, w, m, v, *, name):
    def body(p_ref, w_ref, m_ref, v_ref, g_ref, d_ref, nm_ref, nv_ref, buf_ref, send_sems, recv_sems):
        x, y, c = _place()
        me = 4 * x + 2 * y + c
        copies = []
        for k in range(1, 8):
            dx, dy, dc = (k >> 2) & 1, (k >> 1) & 1, k & 1
            to = (x ^ dx, y ^ dy, c ^ dc)
            cp = pltpu.make_async_remote_copy(
                src_ref=p_ref, dst_ref=buf_ref.at[me],
                send_sem=send_sems.at[k - 1], recv_sem=recv_sems.at[k - 1],
                device_id=to, device_id_type=MESH)
            cp.start()
            copies.append(cp)
        buf_ref[me] = p_ref[...]
        for cp in copies:
            cp.wait()
        tot = buf_ref[0]
        for j in range(1, 8):
            tot = tot + buf_ref[j]
        g_ref[...] = tot
        l0 = w_ref[:, SEG_L0:SEG_L0 + RNN_W]
        l1 = w_ref[:, SEG_L1:SEG_L1 + RNN_W]
        mx = jnp.maximum(l0, l1)
        e0 = jnp.exp(l0 - mx)
        e1 = jnp.exp(l1 - mx)
        lb = e0 / (e0 + e1)
        gl0 = tot[:, SEG_L0:SEG_L0 + RNN_W] * lb * (1.0 - lb)
        g_ref[:, SEG_L0:SEG_L0 + RNN_W] = gl0
        g_ref[:, SEG_L1:SEG_L1 + RNN_W] = -gl0
        d, nm, nv = _adamw_math(w_ref[...], g_ref[...], m_ref[...], v_ref[...])
        d_ref[...] = d
        nm_ref[...] = nm
        nv_ref[...] = nv

    vm = pl.BlockSpec(memory_space=pltpu.VMEM)
    return pl.pallas_call(
        body, name=name,
        in_specs=[vm] * 4, out_specs=[vm] * 4,
        out_shape=[jax.ShapeDtypeStruct((1, N_PACK), F32)] * 4,
        scratch_shapes=[pltpu.VMEM((8, 1, N_PACK), F32), pltpu.SemaphoreType.DMA((7,)),
                        pltpu.SemaphoreType.DMA((7,))],
    )(part, w, m, v)


def _layer_grads(xs, tgt, w_in, w_in_t, w_out, w_out_t, w_up, w_up_t, w_dn, w_dn_t,
                 sinks, again, lb_logits, rgain, g_mix_pre, g_mix_post, g_mlp_pre, g_mlp_post):
    tm = 512
    h1 = _rms_cast(xs, g_mix_pre, tm=tm, name="h1_norm")
    proj = _mm(h1, w_in, tm=tm, tn=768, tk=D_MODEL, out_dtype=F32, name="in_proj")
    attn, lse = _swa_fwd(proj, sinks, name="swa_fwd")
    o_pre, rnn, s0 = _hgrn_fwd(proj, lb_logits, rgain, tb=512, name="hgrn_fwd")
    cat = _mix_cat(attn, rnn, again, tm=tm, name="mix_cat")
    mixed = _mm(cat, w_out, tm=tm, tn=1024, tk=D_MODEL, out_dtype=F32, name="out_proj")
    x1, h2 = _post_norm_res(mixed, g_mix_post, xs, g_mlp_pre, tm=256, name="mix_post")
    u = _mm(h2, w_up, tm=tm, tn=1024, tk=D_MODEL, out_dtype=BF16, relu=True, name="mlp_up")
    yv = _mm(u, w_dn, tm=tm, tn=1024, tk=2048, out_dtype=F32, a_square=True, name="mlp_down")
    dy, dx2, loss_row, dg_mlp_post = _loss_head(yv, g_mlp_post, x1, tgt, tm=256, name="loss_head")
    du = _mm(dy, w_dn_t, tm=tm, tn=1024, tk=D_MODEL, out_dtype=BF16, mul2=u, name="mlp_down_bwd")
    d_w_dn = _mm_tn(u, dy, tm=1024, tn=1024, tt=512, a_square=True, name="w_down_grad")
    d_w_up = _mm_tn(h2, du, tm=1024, tn=1024, tt=512, n_split=N_CHIPS, name="w_up_grad")
    dh2 = _mm(du, w_up_t, tm=tm, tn=1024, tk=2048, out_dtype=F32, name="mlp_up_bwd")
    dx1, dg_mlp_pre = _rms_bwd(dh2, x1, g_mlp_pre, dx2, tm=256, out_dtype=F32, name="mlp_pre_bwd")
    dmixed, dg_mix_post = _rms_bwd(dx1, mixed, g_mix_post, None, tm=256, out_dtype=BF16, name="mix_post_bwd")
    d_w_out = _mm_tn(cat, dmixed, tm=1024, tn=1024, tt=512, name="w_out_grad")
    dcat = _mm(dmixed, w_out_t, tm=tm, tn=1024, tk=D_MODEL, out_dtype=F32, name="out_proj_bwd")
    dattn, dg_again = _rms_bwd(dcat, attn, again, None, tm=tm, out_dtype=F32, name="attn_norm_bwd")
    dq_a, dkv, dsinks = _swa_bwd(proj, sinks, dattn, lse, name="swa_bwd")
    dq_r, df_r, di_r, dg_r, dlb, dgain_h = _hgrn_bwd(proj, lb_logits, rgain, o_pre, s0, dcat, tb=512,
                                                     name="hgrn_bwd")
    dproj = jnp.concatenate([dq_a, dkv, dq_r, df_r, di_r, dg_r], axis=1).astype(BF16)
    d_w_in = _mm_tn(h1, dproj, tm=1024, tn=768, tt=512, name="w_in_grad")
    dh1 = _mm(dproj, w_in_t, tm=tm, tn=1024, tk=768, out_dtype=F32, name="in_proj_bwd")
    gx, dg_mix_pre = _rms_bwd(dh1, xs, g_mix_pre, dx1, tm=256, out_dtype=F32, name="mix_pre_bwd")

    shard = IN_W // N_CHIPS
    big = [d_w_in[0].reshape(D_MODEL, N_CHIPS, shard).transpose(1, 0, 2),
           d_w_out.reshape(N_CHIPS, D_MODEL // N_CHIPS, D_MODEL),
           d_w_up,
           d_w_dn.reshape(N_CHIPS, D_FF // N_CHIPS, D_MODEL)]
    drgain = jnp.sum(dgain_h, axis=0)
    small = _pack(dsinks, dg_again, dlb, jnp.zeros_like(dlb), drgain,
                  [dg_mix_pre, dg_mix_post, dg_mlp_pre, dg_mlp_post], loss=loss_row)
    return gx, big, small


def kernel(x, w_in, attn_sinks, attn_out_gain, rnn_lb_logits, rnn_norm_gain, w_out, mix_pre_gain, mix_post_gain, mlp_pre_gain, mlp_post_gain, w_up, w_down, loss_target, m_w_in, m_attn_sinks, m_attn_out_gain, m_rnn_lb_logits, m_rnn_norm_gain, m_w_out, m_mix_pre_gain, m_mix_post_gain, m_mlp_pre_gain, m_mlp_post_gain, m_w_up, m_w_down, v_w_in, v_attn_sinks, v_attn_out_gain, v_rnn_lb_logits, v_rnn_norm_gain, v_w_out, v_mix_pre_gain, v_mix_post_gain, v_mlp_pre_gain, v_mlp_post_gain, v_w_up, v_w_down):
    ax, ay, ac = _place()
    where = jnp.stack([2 * ax + ay, ac]).astype(jnp.int32)
    big_w = [w_in[0], w_out[0], w_up[0], w_down[0]]
    big_m = [m_w_in[0], m_w_out[0], m_w_up[0], m_w_down[0]]
    big_v = [v_w_in[0], v_w_out[0], v_w_up[0], v_w_down[0]]

    halves = [w.astype(BF16).reshape(2, w.shape[0] // 2, w.shape[1]) for w in big_w]
    g_in, g_out, g_up, g_dn = _all_gather_halves(halves, name="gather_weights")
    shard = IN_W // N_CHIPS
    w_in4 = g_in.reshape(N_CHIPS, D_MODEL, shard)
    w_in_f = w_in4.transpose(1, 0, 2).reshape(D_MODEL, IN_W)
    w_in_t = w_in4.transpose(0, 2, 1).reshape(IN_W, D_MODEL)
    w_out_f = g_out.reshape(D_MODEL, D_MODEL)
    w_up4 = g_up.reshape(N_CHIPS, D_MODEL, D_FF // N_CHIPS)
    w_up_f = w_up4.transpose(1, 0, 2).reshape(D_MODEL, D_FF)
    w_up_t = w_up4.transpose(0, 2, 1).reshape(D_FF, D_MODEL)
    w_dn_f = g_dn.reshape(D_FF, D_MODEL)

    gx, big_g, small_part = _layer_grads(
        x[0], loss_target[0], w_in_f, w_in_t, w_out_f, w_out_f.T, w_up_f, w_up_t, w_dn_f, w_dn_f.T,
        attn_sinks, attn_out_gain, rnn_lb_logits, rnn_norm_gain,
        mix_pre_gain, mix_post_gain, mlp_pre_gain, mlp_post_gain)

    g5 = [g.reshape(4, 2, g.shape[1] // 2, g.shape[2]) for g in big_g]
    names = ["w_in", "w_out", "w_up", "w_down"]
    sib = _pair_exchange(g5, name="pair_exchange")
    wires = [_pair_sum(g, s, where, name="pair_sum_" + nm) for g, s, nm in zip(g5, sib, names)]
    recv = _chip_exchange(wires, name="chip_exchange")
    fins = [_final_half(g, s, r, where, name="final_half_" + nm) for g, s, r, nm in zip(g5, sib, recv, names)]
    full = _sibling_share(fins, name="sibling_share")
    grads, deltas, new_m, new_v = [], [], [], []
    for f, w, m, v, nm in zip(full, big_w, big_m, big_v, names):
        g = f.reshape(w.shape)
        d, nm_, nv_ = _adamw(w, g, m, v, name="adamw_" + nm)
        grads.append(g[None])
        deltas.append(d[None])
        new_m.append(nm_[None])
        new_v.append(nv_[None])

    def pack_params(sinks, again, logits, rgain, gains):
        return _pack(sinks, again, logits[0:1], logits[1:2], rgain, gains)

    pw = pack_params(attn_sinks, attn_out_gain, rnn_lb_logits, rnn_norm_gain,
                     [mix_pre_gain, mix_post_gain, mlp_pre_gain, mlp_post_gain])
    pm = pack_params(m_attn_sinks, m_attn_out_gain, m_rnn_lb_logits, m_rnn_norm_gain,
                     [m_mix_pre_gain, m_mix_post_gain, m_mlp_pre_gain, m_mlp_post_gain])
    pv = pack_params(v_attn_sinks, v_attn_out_gain, v_rnn_lb_logits, v_rnn_norm_gain,
                     [v_mix_pre_gain, v_mix_post_gain, v_mlp_pre_gain, v_mlp_post_gain])
    packs = _small_reduce_adamw(small_part, pw, pm, pv, name="small_reduce_adamw")

    def unpack(p):
        seg = lambda o, k: p[:, o:o + k]
        logits = jnp.concatenate([seg(SEG_L0, RNN_W), seg(SEG_L1, RNN_W)], axis=0)
        gains = [seg(SEG_G + i * D_MODEL, D_MODEL) for i in range(4)]
        return dict(sinks=seg(SEG_SINK, N_Q), again=seg(SEG_AGAIN, ATTN_W), logits=logits,
                    rgain=seg(SEG_RGAIN, RNN_HD), gains=gains)

    def order(small, big):
        return [big[0], small["sinks"], small["again"], small["logits"], small["rgain"], big[1],
                *small["gains"], big[2], big[3]]

    loss = packs[0][0, 0]
    outs = [loss, gx[None]]
    for p, b in zip(packs, [grads, deltas, new_m, new_v]):
        outs += order(unpack(p), b)
    return tuple(outs)
```

```python
import functools

import jax
import jax.numpy as jnp
from jax import lax
from jax.experimental import pallas as pl
from jax.experimental.pallas import tpu as pltpu

F32 = jnp.float32
BF16 = jnp.bfloat16
MESH = pl.DeviceIdType.MESH

EPS = 1e-6
D_MODEL = 2048
ATTN_W = 1024
HEAD_DIM = 64
N_Q = 16
N_KV = 2
GROUP = 8
BLK = 128
RNN_W = 1024
RNN_HD = 128
N_RNN = 8
CHUNK = 64
SUB = 16
D_FF = 8192
IN_W = 5376
N_CHIPS = 4
KV_COL = ATTN_W
QR_COL = ATTN_W + 2 * 128
FR_COL = QR_COL + RNN_W
IR_COL = FR_COL + RNN_W
GR_COL = IR_COL + RNN_W

ADAM_LR = 0.001
ADAM_B1 = 0.9
ADAM_B2 = 0.999
ADAM_EPS = 1e-08
ADAM_WD = 0.01
ADAM_STEP = 10

VMEM_LIMIT = 48 * 1024 * 1024

NT = (((1,), (1,)), ((), ()))
TN = (((0,), (0,)), ((), ()))


def _params(sem=None):
    return pltpu.CompilerParams(dimension_semantics=sem, vmem_limit_bytes=VMEM_LIMIT)


def _sigmoid(x):
    return 1.0 / (1.0 + jnp.exp(-x))


def _mm(a, w, *, tm, tn, tk, out_dtype, name, a_square=False, relu=False, mul2=None):
    m, k = a.shape
    _, n = w.shape
    nk = k // tk
    assert m % tm == 0 and n % tn == 0 and k % tk == 0

    def body(*refs):
        if mul2 is not None:
            a_ref, w_ref, e_ref, o_ref, acc_ref = refs
        else:
            a_ref, w_ref, o_ref, acc_ref = refs
            e_ref = None
        kk = pl.program_id(2)
        av = a_ref[...]
        if a_square:
            af = av.astype(F32)
            av = (af * af).astype(BF16)
        part = jnp.dot(av, w_ref[...], preferred_element_type=F32)

        def finish(r):
            if relu:
                r = jnp.maximum(r, 0.0)
            if e_ref is not None:
                r = 2.0 * e_ref[...].astype(F32) * r
            o_ref[...] = r.astype(out_dtype)

        if nk == 1:
            finish(part)
        else:
            @pl.when(kk == 0)
            def _():
                acc_ref[...] = part

            @pl.when(kk > 0)
            def _():
                acc_ref[...] += part

            @pl.when(kk == nk - 1)
            def _():
                finish(acc_ref[...])

    in_specs = [pl.BlockSpec((tm, tk), lambda i, j, kk: (i, kk)),
                pl.BlockSpec((tk, tn), lambda i, j, kk: (kk, j))]
    args = [a, w]
    if mul2 is not None:
        in_specs.append(pl.BlockSpec((tm, tn), lambda i, j, kk: (i, j)))
        args.append(mul2)
    acc_shape = (tm, tn) if nk > 1 else (8, 128)
    return pl.pallas_call(
        body, name=name, grid=(m // tm, n // tn, nk),
        in_specs=in_specs, out_specs=pl.BlockSpec((tm, tn), lambda i, j, kk: (i, j)),
        out_shape=jax.ShapeDtypeStruct((m, n), out_dtype),
        scratch_shapes=[pltpu.VMEM(acc_shape, F32)],
        compiler_params=_params(("parallel", "parallel", "arbitrary")),
    )(*args)


def _mm_tn(a, b, *, tm, tn, tt, name, a_square=False, n_split=1):
    t, m = a.shape
    _, n = b.shape
    assert t % tt == 0 and m % tm == 0 and n % tn == 0 and (n // n_split) % tn == 0
    per = n // n_split // tn

    def body(a_ref, b_ref, o_ref):
        ti = pl.program_id(2)
        av = a_ref[...]
        if a_square:
            af = av.astype(F32)
            av = (af * af).astype(BF16)
        part = lax.dot_general(av, b_ref[...], TN, preferred_element_type=F32)

        @pl.when(ti == 0)
        def _():
            o_ref[...] = part

        @pl.when(ti > 0)
        def _():
            o_ref[...] += part

    return pl.pallas_call(
        body, name=name, grid=(m // tm, n // tn, t // tt),
        in_specs=[pl.BlockSpec((tt, tm), lambda i, j, ti: (ti, i)),
                  pl.BlockSpec((tt, tn), lambda i, j, ti: (ti, j))],
        out_specs=pl.BlockSpec((None, tm, tn), lambda i, j, ti: (j // per, i, j % per)),
        out_shape=jax.ShapeDtypeStruct((n_split, m, n // n_split), F32),
        compiler_params=_params(("parallel", "parallel", "arbitrary")),
    )(a, b)


def _rstd(x):
    return lax.rsqrt(jnp.mean(x * x, axis=-1, keepdims=True) + EPS)


def _rms_cast(x, g, *, tm, name):
    t, d = x.shape

    def body(x_ref, g_ref, o_ref):
        xv = x_ref[...]
        o_ref[...] = (xv * _rstd(xv) * g_ref[...]).astype(BF16)

    return pl.pallas_call(
        body, name=name, grid=(t // tm,),
        in_specs=[pl.BlockSpec((tm, d), lambda i: (i, 0)), pl.BlockSpec((1, d), lambda i: (0, 0))],
        out_specs=pl.BlockSpec((tm, d), lambda i: (i, 0)),
        out_shape=jax.ShapeDtypeStruct((t, d), BF16),
        compiler_params=_params(("parallel",)),
    )(x, g)


def _mix_cat(attn, rnn, gain, *, tm, name):
    t = attn.shape[0]

    def body(a_ref, r_ref, g_ref, o_ref):
        av = a_ref[...]
        o_ref[:, :ATTN_W] = (av * _rstd(av) * g_ref[...]).astype(BF16)
        o_ref[:, ATTN_W:] = r_ref[...].astype(BF16)

    return pl.pallas_call(
        body, name=name, grid=(t // tm,),
        in_specs=[pl.BlockSpec((tm, ATTN_W), lambda i: (i, 0)), pl.BlockSpec((tm, RNN_W), lambda i: (i, 0)),
                  pl.BlockSpec((1, ATTN_W), lambda i: (0, 0))],
        out_specs=pl.BlockSpec((tm, D_MODEL), lambda i: (i, 0)),
        out_shape=jax.ShapeDtypeStruct((t, D_MODEL), BF16),
        compiler_params=_params(("parallel",)),
    )(attn, rnn, gain)


def _post_norm_res(mixed, g_post, res, g_next, *, tm, name):
    t, d = mixed.shape

    def body(m_ref, gp_ref, r_ref, gn_ref, x1_ref, h2_ref):
        mv = m_ref[...]
        x1 = r_ref[...] + mv * _rstd(mv) * gp_ref[...]
        x1_ref[...] = x1
        h2_ref[...] = (x1 * _rstd(x1) * gn_ref[...]).astype(BF16)

    row = pl.BlockSpec((tm, d), lambda i: (i, 0))
    vec = pl.BlockSpec((1, d), lambda i: (0, 0))
    return pl.pallas_call(
        body, name=name, grid=(t // tm,),
        in_specs=[row, vec, row, vec], out_specs=[row, row],
        out_shape=[jax.ShapeDtypeStruct((t, d), F32), jax.ShapeDtypeStruct((t, d), BF16)],
        compiler_params=_params(("parallel",)),
    )(mixed, g_post, res, g_next)


def _rms_bwd(dyn, xin, g, res, *, tm, out_dtype, name, col_block=0):
    t, d = xin.shape

    def body(*refs):
        if res is not None:
            dy_ref, x_ref, g_ref, r_ref, dx_ref, dg_ref = refs
        else:
            dy_ref, x_ref, g_ref, dx_ref, dg_ref = refs
        i = pl.program_id(0)
        xv = x_ref[...]
        dy = dy_ref[...].astype(F32)
        r = _rstd(xv)
        xh = xv * r
        part = jnp.sum(dy * xh, axis=0, keepdims=True)

        @pl.when(i == 0)
        def _():
            dg_ref[...] = part

        @pl.when(i > 0)
        def _():
            dg_ref[...] += part

        tt = dy * g_ref[...]
        dx = r * (tt - xh * jnp.mean(tt * xh, axis=-1, keepdims=True))
        if res is not None:
            dx = dx + r_ref[...]
        dx_ref[...] = dx.astype(out_dtype)

    row = pl.BlockSpec((tm, d), lambda i: (i, 0))
    vec = pl.BlockSpec((1, d), lambda i: (0, 0))
    in_specs = [pl.BlockSpec((tm, d), lambda i: (i, col_block)), row, vec]
    args = [dyn, xin, g]
    if res is not None:
        in_specs.append(row)
        args.append(res)
    return pl.pallas_call(
        body, name=name, grid=(t // tm,),
        in_specs=in_specs, out_specs=[row, vec],
        out_shape=[jax.ShapeDtypeStruct((t, d), out_dtype), jax.ShapeDtypeStruct((1, d), F32)],
        compiler_params=_params(("arbitrary",)),
    )(*args)


def _loss_head(y, g_post, x1, target, *, tm, name):
    t, d = y.shape

    def body(y_ref, g_ref, x1_ref, t_ref, dy_ref, dx2_ref, loss_ref, dg_ref):
        i = pl.program_id(0)
        yv = y_ref[...]
        r = _rstd(yv)
        yh = yv * r
        gv = g_ref[...]
        err = x1_ref[...] + yh * gv - t_ref[...]
        lpart = 0.5 * jnp.sum(jnp.mean(err * err, axis=-1, keepdims=True), axis=0, keepdims=True)
        dx2 = err * (1.0 / d)
        dgp = jnp.sum(dx2 * yh, axis=0, keepdims=True)
        lane = lax.broadcasted_iota(jnp.int32, (1, 128), 1)
        lrow = jnp.where(lane == 0, lpart, 0.0)

        @pl.when(i == 0)
        def _():
            dg_ref[...] = dgp
            loss_ref[...] = lrow

        @pl.when(i > 0)
        def _():
            dg_ref[...] += dgp
            loss_ref[...] += lrow

        tt = dx2 * gv
        dy_ref[...] = (r * (tt - yh * jnp.mean(tt * yh, axis=-1, keepdims=True))).astype(BF16)
        dx2_ref[...] = dx2

    row = pl.BlockSpec((tm, d), lambda i: (i, 0))
    vec = pl.BlockSpec((1, d), lambda i: (0, 0))
    return pl.pallas_call(
        body, name=name, grid=(t // tm,),
        in_specs=[row, vec, row, row],
        out_specs=[row, row, pl.BlockSpec((1, 128), lambda i: (0, 0)), vec],
        out_shape=[jax.ShapeDtypeStruct((t, d), BF16), jax.ShapeDtypeStruct((t, d), F32),
                   jax.ShapeDtypeStruct((1, 128), F32), jax.ShapeDtypeStruct((1, d), F32)],
        compiler_params=_params(("arbitrary",)),
    )(y, g_post, x1, target)


def _alibi_slope(h):
    return 2.0 ** (-8.0 * (h + 1) / N_Q)


def _swa_mask(n):
    row = lax.broadcasted_iota(jnp.int32, (BLK, 2 * BLK), 0)
    col = lax.broadcasted_iota(jnp.int32, (BLK, 2 * BLK), 1)
    dist = row + BLK - col
    valid = (dist >= 0) & (dist < BLK) & ((col >= BLK) | (n > 0))
    return valid, dist.astype(F32)


def _swa_scores(q_ref, kcat, hh, valid, distf):
    qh = q_ref[:, hh * HEAD_DIM:(hh + 1) * HEAD_DIM].astype(BF16)
    s = lax.dot_general(qh, kcat, NT, preferred_element_type=F32) * (HEAD_DIM ** -0.5)
    s = s - _alibi_slope(hh) * distf
    return qh, jnp.where(valid, s, -1e30)


def _kv_cat(kvp_ref, kvc_ref, off):
    return jnp.concatenate([kvp_ref[:, off:off + HEAD_DIM], kvc_ref[:, off:off + HEAD_DIM]], axis=0).astype(BF16)


def _swa_fwd(proj, sinks, *, name):
    t = proj.shape[0]
    nb = t // BLK
    kvb = KV_COL // (2 * 128)

    def body(q_ref, kvc_ref, kvp_ref, sink_ref, o_ref, lse_ref):
        n = pl.program_id(0)
        valid, distf = _swa_mask(n)
        for kvh in range(N_KV):
            kcat = _kv_cat(kvp_ref, kvc_ref, kvh * HEAD_DIM)
            vcat = _kv_cat(kvp_ref, kvc_ref, 128 + kvh * HEAD_DIM)
            for gi in range(GROUP):
                hh = kvh * GROUP + gi
                _, s = _swa_scores(q_ref, kcat, hh, valid, distf)
                sink = sink_ref[0:1, hh:hh + 1]
                mx = jnp.maximum(jnp.max(s, axis=1, keepdims=True), sink)
                p = jnp.exp(s - mx)
                l = jnp.sum(p, axis=1, keepdims=True) + jnp.exp(sink - mx)
                probs = p / l
                o_ref[:, hh * HEAD_DIM:(hh + 1) * HEAD_DIM] = jnp.dot(
                    probs.astype(BF16), vcat, preferred_element_type=F32)
                lse_ref[:, hh:hh + 1] = mx + jnp.log(l)

    return pl.pallas_call(
        body, name=name, grid=(nb,),
        in_specs=[pl.BlockSpec((BLK, ATTN_W), lambda n: (n, 0)),
                  pl.BlockSpec((BLK, 256), lambda n: (n, kvb)),
                  pl.BlockSpec((BLK, 256), lambda n: (jnp.maximum(n - 1, 0), kvb)),
                  pl.BlockSpec((1, N_Q), lambda n: (0, 0))],
        out_specs=[pl.BlockSpec((BLK, ATTN_W), lambda n: (n, 0)), pl.BlockSpec((BLK, N_Q), lambda n: (n, 0))],
        out_shape=[jax.ShapeDtypeStruct((t, ATTN_W), F32), jax.ShapeDtypeStruct((t, N_Q), F32)],
        compiler_params=_params(("parallel",)),
    )(proj, proj, proj, sinks)


def _swa_bwd(proj, sinks, dattn, lse, *, name):
    t = proj.shape[0]
    nb = t // BLK
    kvb = KV_COL // (2 * 128)

    def body(q_ref, kvc_ref, kvp_ref, sink_ref, do_ref, lse_ref, dq_ref, dkv_ref, dsink_ref, carry_ref):
        n = pl.program_id(0)

        @pl.when(n == 0)
        def _():
            dsink_ref[...] = jnp.zeros_like(dsink_ref)
            carry_ref[...] = jnp.zeros_like(carry_ref)

        @pl.when(n < nb)
        def _():
            valid, distf = _swa_mask(n)
            for kvh in range(N_KV):
                kcat = _kv_cat(kvp_ref, kvc_ref, kvh * HEAD_DIM)
                vcat = _kv_cat(kvp_ref, kvc_ref, 128 + kvh * HEAD_DIM)
                dk_cat = jnp.zeros((2 * BLK, HEAD_DIM), F32)
                dv_cat = jnp.zeros((2 * BLK, HEAD_DIM), F32)
                for gi in range(GROUP):
                    hh = kvh * GROUP + gi
                    qh, s = _swa_scores(q_ref, kcat, hh, valid, distf)
                    lse_h = lse_ref[:, hh:hh + 1]
                    probs = jnp.exp(s - lse_h)
                    doh = do_ref[:, hh * HEAD_DIM:(hh + 1) * HEAD_DIM].astype(BF16)
                    dprobs = lax.dot_general(doh, vcat, NT, preferred_element_type=F32)
                    delta = jnp.sum(probs * dprobs, axis=1, keepdims=True)
                    ds = (probs * (dprobs - delta)).astype(BF16)
                    psink = jnp.exp(sink_ref[0:1, hh:hh + 1] - lse_h)
                    dsink_ref[0:1, hh:hh + 1] += -jnp.sum(psink * delta, axis=0, keepdims=True)
                    dv_cat = dv_cat + lax.dot_general(probs.astype(BF16), doh, TN, preferred_element_type=F32)
                    dq_ref[:, hh * HEAD_DIM:(hh + 1) * HEAD_DIM] = jnp.dot(
                        ds, kcat, preferred_element_type=F32) * (HEAD_DIM ** -0.5)
                    dk_cat = dk_cat + lax.dot_general(ds, qh, TN, preferred_element_type=F32)
                dk_cat = dk_cat * (HEAD_DIM ** -0.5)
                ko = kvh * HEAD_DIM
                vo = 128 + kvh * HEAD_DIM
                dkv_ref[:, ko:ko + HEAD_DIM] = carry_ref[:, ko:ko + HEAD_DIM] + dk_cat[:BLK]
                dkv_ref[:, vo:vo + HEAD_DIM] = carry_ref[:, vo:vo + HEAD_DIM] + dv_cat[:BLK]
                carry_ref[:, ko:ko + HEAD_DIM] = dk_cat[BLK:]
                carry_ref[:, vo:vo + HEAD_DIM] = dv_cat[BLK:]

        @pl.when(n == nb)
        def _():
            dkv_ref[...] = carry_ref[...]

    last = nb - 1
    return pl.pallas_call(
        body, name=name, grid=(nb + 1,),
        in_specs=[pl.BlockSpec((BLK, ATTN_W), lambda n: (jnp.minimum(n, last), 0)),
                  pl.BlockSpec((BLK, 256), lambda n: (jnp.minimum(n, last), kvb)),
                  pl.BlockSpec((BLK, 256), lambda n: (jnp.maximum(jnp.minimum(n, last) - 1, 0), kvb)),
                  pl.BlockSpec((1, N_Q), lambda n: (0, 0)),
                  pl.BlockSpec((BLK, ATTN_W), lambda n: (jnp.minimum(n, last), 0)),
                  pl.BlockSpec((BLK, N_Q), lambda n: (jnp.minimum(n, last), 0))],
        out_specs=[pl.BlockSpec((BLK, ATTN_W), lambda n: (jnp.minimum(n, last), 0)),
                   pl.BlockSpec((BLK, 256), lambda n: (jnp.maximum(n - 1, 0), 0)),
                   pl.BlockSpec((1, N_Q), lambda n: (0, 0))],
        out_shape=[jax.ShapeDtypeStruct((t, ATTN_W), F32), jax.ShapeDtypeStruct((t, 256), F32),
                   jax.ShapeDtypeStruct((1, N_Q), F32)],
        scratch_shapes=[pltpu.VMEM((BLK, 256), F32)],
        compiler_params=_params(("arbitrary",)),
    )(proj, proj, proj, sinks, dattn, lse)


def _cumsum_rows(x):
    n = x.shape[0]
    row = lax.broadcasted_iota(jnp.int32, x.shape, 0)
    s = 1
    while s < n:
        x = x + jnp.where(row >= s, pltpu.roll(x, s, axis=0), 0.0)
        s *= 2
    return x


def _rev_cumsum_rows(x):
    n = x.shape[0]
    row = lax.broadcasted_iota(jnp.int32, x.shape, 0)
    s = 1
    while s < n:
        x = x + jnp.where(row < n - s, pltpu.roll(x, n - s, axis=0), 0.0)
        s *= 2
    return x


def _lower_bound(lbl_ref):
    l0 = lbl_ref[0:1, :]
    l1 = lbl_ref[1:2, :]
    mx = jnp.maximum(l0, l1)
    e0 = jnp.exp(l0 - mx)
    e1 = jnp.exp(l1 - mx)
    return e0 / (e0 + e1)


def _hgrn_gates(z, lb):
    sg = _sigmoid(z)
    f = lb + (1.0 - lb) * sg
    return sg, f, jnp.log(f), 1.0 - f


def _sub_factors(b, i):
    rows = lax.broadcasted_iota(jnp.int32, (CHUNK, RNN_HD), 0)
    ref = b[SUB * i - 1:SUB * i, :]
    qfac = jnp.exp(b[SUB * i:SUB * (i + 1), :] - ref)
    kfac = jnp.where(rows < SUB * i, jnp.exp(jnp.minimum(ref - b, 0.0)), 0.0)
    return qfac, kfac


def _diag_decay(bi, s):
    trow = lax.broadcasted_iota(jnp.int32, (SUB, RNN_HD), 0)
    return jnp.where(trow >= s, jnp.exp(jnp.minimum(bi - bi[s:s + 1, :], 0.0)), 0.0)


def _hgrn_fwd(proj, lb_logits, norm_gain, *, tb, name):
    t = proj.shape[0]
    ntb = t // tb
    nch = tb // CHUNK
    qb, fb, ib, gb = QR_COL // 128, FR_COL // 128, IR_COL // 128, GR_COL // 128

    def body(q_ref, f_ref, i_ref, g_ref, lbl_ref, gain_ref, o_ref, out_ref, s0_ref, st_ref, ob_ref):
        c = pl.program_id(1)

        @pl.when(c == 0)
        def _():
            st_ref[...] = jnp.zeros_like(st_ref)

        lb = _lower_bound(lbl_ref)
        gain = gain_ref[...]

        def chunk(ci, carry):
            r0 = pl.multiple_of(ci * CHUNK, CHUNK)
            rows = pl.ds(r0, CHUNK)
            _, _, lf, k = _hgrn_gates(f_ref[rows, :], lb)
            qr = q_ref[rows, :]
            q = qr * _sigmoid(qr)
            v = i_ref[rows, :]
            b = _cumsum_rows(lf)
            st = st_ref[...]
            s0_ref[ci] = st
            ob_ref[...] = lax.dot_general((q * jnp.exp(b)).astype(BF16), st.astype(BF16), NT,
                                          preferred_element_type=F32)
            vb = v.astype(BF16)
            for i in range(CHUNK // SUB):
                blk = slice(SUB * i, SUB * (i + 1))
                qi, ki, vi, bi = q[blk], k[blk], v[blk], b[blk]
                oi = ob_ref[blk, :]
                if i > 0:
                    qfac, kfac = _sub_factors(b, i)
                    att = lax.dot_general((qi * qfac).astype(BF16), (k * kfac).astype(BF16), NT,
                                          preferred_element_type=F32)
                    oi = oi + jnp.dot(att.astype(BF16), vb, preferred_element_type=F32)
                for s in range(SUB):
                    e = _diag_decay(bi, s)
                    a = jnp.sum(qi * ki[s:s + 1, :] * e, axis=1, keepdims=True)
                    oi = oi + a * vi[s:s + 1, :]
                ob_ref[blk, :] = oi
            blast = b[CHUNK - 1:CHUNK, :]
            khat = (k * jnp.exp(blast - b)).astype(BF16)
            st_ref[...] = st * jnp.exp(blast) + lax.dot_general(vb, khat, TN, preferred_element_type=F32)
            o = ob_ref[...]
            o_ref[rows, :] = o
            gr = g_ref[rows, :]
            out_ref[rows, :] = o * _rstd(o) * gain * (gr * _sigmoid(gr))
            return carry

        lax.fori_loop(0, nch, chunk, 0)

    def col(base):
        return pl.BlockSpec((tb, RNN_HD), lambda h, c: (c, base + h))

    return pl.pallas_call(
        body, name=name, grid=(N_RNN, ntb),
        in_specs=[col(qb), col(fb), col(ib), col(gb),
                  pl.BlockSpec((2, RNN_HD), lambda h, c: (0, h)), pl.BlockSpec((1, RNN_HD), lambda h, c: (0, 0))],
        out_specs=[pl.BlockSpec((tb, RNN_HD), lambda h, c: (c, h)), pl.BlockSpec((tb, RNN_HD), lambda h, c: (c, h)),
                   pl.BlockSpec((None, nch, RNN_HD, RNN_HD), lambda h, c: (h, c, 0, 0))],
        out_shape=[jax.ShapeDtypeStruct((t, RNN_W), F32), jax.ShapeDtypeStruct((t, RNN_W), F32),
                   jax.ShapeDtypeStruct((N_RNN, t // CHUNK, RNN_HD, RNN_HD), F32)],
        scratch_shapes=[pltpu.VMEM((RNN_HD, RNN_HD), F32), pltpu.VMEM((CHUNK, RNN_HD), F32)],
        compiler_params=_params(("parallel", "arbitrary")),
    )(proj, proj, proj, proj, lb_logits, norm_gain)


def _hgrn_bwd(proj, lb_logits, norm_gain, o_pre, s0, dcat, *, tb, name):
    t = proj.shape[0]
    ntb = t // tb
    nch = tb // CHUNK
    qb, fb, ib, gb = QR_COL // 128, FR_COL // 128, IR_COL // 128, GR_COL // 128
    nsub = CHUNK // SUB

    def body(q_ref, f_ref, i_ref, g_ref, lbl_ref, gain_ref, o_ref, s0_ref, dout_ref,
             dq_ref, df_ref, di_ref, dg_ref, dlb_ref, dgain_ref,
             dst_ref, dqa_ref, dka_ref, dva_ref):
        c = pl.program_id(1)

        @pl.when(c == 0)
        def _():
            dst_ref[...] = jnp.zeros_like(dst_ref)
            dlb_ref[...] = jnp.zeros_like(dlb_ref)
            dgain_ref[...] = jnp.zeros_like(dgain_ref)

        lb = _lower_bound(lbl_ref)
        gain = gain_ref[...]

        def chunk(cj, carry):
            ci = nch - 1 - cj
            r0 = pl.multiple_of(ci * CHUNK, CHUNK)
            rows = pl.ds(r0, CHUNK)
            sg, f, lf, k = _hgrn_gates(f_ref[rows, :], lb)
            qr = q_ref[rows, :]
            sq = _sigmoid(qr)
            q = qr * sq
            v = i_ref[rows, :]
            b = _cumsum_rows(lf)

            dout = dout_ref[rows, :]
            o = o_ref[rows, :]
            gr = g_ref[rows, :]
            sgg = _sigmoid(gr)
            gate = gr * sgg
            rs = _rstd(o)
            nrm = o * rs
            dg_ref[rows, :] = dout * nrm * gain * (sgg * (1.0 + gr * (1.0 - sgg)))
            dn = dout * gate
            dgain_ref[...] += jnp.sum(dn * nrm, axis=0, keepdims=True)
            tt = dn * gain
            do = rs * (tt - nrm * jnp.mean(tt * nrm, axis=-1, keepdims=True))

            dob = do.astype(BF16)
            vb = v.astype(BF16)
            eb = jnp.exp(b)
            blast = b[CHUNK - 1:CHUNK, :]
            ebl = jnp.exp(blast - b)
            dst = dst_ref[...]
            dstb = dst.astype(BF16)
            khat = (k * ebl).astype(BF16)
            s0 = s0_ref[ci]
            dqa_ref[...] = eb * jnp.dot(dob, s0.astype(BF16), preferred_element_type=F32)
            dk_state = ebl * jnp.dot(vb, dstb, preferred_element_type=F32)
            dka_ref[...] = dk_state
            d_blast = (jnp.sum(k * dk_state, axis=0, keepdims=True)
                       + jnp.exp(blast) * jnp.sum(dst * s0, axis=0, keepdims=True))
            dva_ref[...] = lax.dot_general(khat, dstb, NT, preferred_element_type=F32)
            dst_ref[...] = dst * jnp.exp(blast) + lax.dot_general(dob, (q * eb).astype(BF16), TN,
                                                                  preferred_element_type=F32)
            pm = lax.dot_general(dob, vb, NT, preferred_element_type=F32)
            for i in range(nsub):
                blk = slice(SUB * i, SUB * (i + 1))
                qi, ki, vi, bi, doi = q[blk], k[blk], v[blk], b[blk], do[blk]
                dqi = dqa_ref[blk, :]
                if i > 0:
                    qfac, kfac = _sub_factors(b, i)
                    qt = (qi * qfac).astype(BF16)
                    kt = (k * kfac).astype(BF16)
                    att = lax.dot_general(qt, kt, NT, preferred_element_type=F32).astype(BF16)
                    pmi = pm[blk, :].astype(BF16)
                    dva_ref[...] += lax.dot_general(att, doi.astype(BF16), TN, preferred_element_type=F32)
                    dqi = dqi + qfac * jnp.dot(pmi, kt, preferred_element_type=F32)
                    dka_ref[...] += kfac * lax.dot_general(pmi, qt, TN, preferred_element_type=F32)
                dki = dka_ref[blk, :]
                dvi = dva_ref[blk, :]
                trow = lax.broadcasted_iota(jnp.int32, (SUB, RNN_HD), 0)
                for s in range(SUB):
                    e = _diag_decay(bi, s)
                    ks = ki[s:s + 1, :]
                    vs = vi[s:s + 1, :]
                    a = jnp.sum(qi * ks * e, axis=1, keepdims=True)
                    p = jnp.sum(doi * vs, axis=1, keepdims=True)
                    dvs = jnp.sum(a * doi, axis=0, keepdims=True)
                    dks = jnp.sum(p * qi * e, axis=0, keepdims=True)
                    dqi = dqi + p * ks * e
                    dki = dki + jnp.where(trow == s, dks, 0.0)
                    dvi = dvi + jnp.where(trow == s, dvs, 0.0)
                dqa_ref[blk, :] = dqi
                dka_ref[blk, :] = dki
                dva_ref[blk, :] = dvi

            dq = dqa_ref[...]
            dk = dka_ref[...]
            lastrow = lax.broadcasted_iota(jnp.int32, (CHUNK, RNN_HD), 0) == CHUNK - 1
            dlf = _rev_cumsum_rows(q * dq - k * dk + jnp.where(lastrow, d_blast, 0.0))
            dff = dlf / f - dk
            df_ref[rows, :] = dff * (1.0 - lb) * sg * (1.0 - sg)
            dlb_ref[...] += jnp.sum(dff * (1.0 - sg), axis=0, keepdims=True)
            dq_ref[rows, :] = dq * (sq * (1.0 + qr * (1.0 - sq)))
            di_ref[rows, :] = dva_ref[...]
            return carry

        lax.fori_loop(0, nch, chunk, 0)

    def col(base):
        return pl.BlockSpec((tb, RNN_HD), lambda h, c: (ntb - 1 - c, base + h))

    outc = pl.BlockSpec((tb, RNN_HD), lambda h, c: (ntb - 1 - c, h))
    hb = ATTN_W // RNN_HD
    return pl.pallas_call(
        body, name=name, grid=(N_RNN, ntb),
        in_specs=[col(qb), col(fb), col(ib), col(gb),
                  pl.BlockSpec((2, RNN_HD), lambda h, c: (0, h)), pl.BlockSpec((1, RNN_HD), lambda h, c: (0, 0)),
                  outc,
                  pl.BlockSpec((None, nch, RNN_HD, RNN_HD), lambda h, c: (h, ntb - 1 - c, 0, 0)),
                  pl.BlockSpec((tb, RNN_HD), lambda h, c: (ntb - 1 - c, hb + h))],
        out_specs=[outc, outc, outc, outc,
                   pl.BlockSpec((1, RNN_HD), lambda h, c: (0, h)),
                   pl.BlockSpec((None, 1, RNN_HD), lambda h, c: (h, 0, 0))],
        out_shape=[jax.ShapeDtypeStruct((t, RNN_W), F32)] * 4
        + [jax.ShapeDtypeStruct((1, RNN_W), F32), jax.ShapeDtypeStruct((N_RNN, 1, RNN_HD), F32)],
        scratch_shapes=[pltpu.VMEM((RNN_HD, RNN_HD), F32),
                        pltpu.VMEM((CHUNK, RNN_HD), F32), pltpu.VMEM((CHUNK, RNN_HD), F32),
                        pltpu.VMEM((CHUNK, RNN_HD), F32)],
        compiler_params=_params(("parallel", "arbitrary")),
    )(proj, proj, proj, proj, lb_logits, norm_gain, o_pre, s0, dcat)


def _place():
    return lax.axis_index("x"), lax.axis_index("y"), lax.axis_index("c")


ANY = pl.BlockSpec(memory_space=pl.ANY)


def _all_gather_halves(shards, *, name):
    n = len(shards)

    def body(*refs):
        ins, outs = refs[:n], refs[n:2 * n]
        send_sems, recv_sems = refs[2 * n:]
        x, y, c = _place()
        sibling = (x, y, 1 - c)
        chips = [(1 - x, y), (x, 1 - y), (1 - x, 1 - y)]

        def copy(a, k, block, to, src=None):
            slot = outs[a].at[4 * block[0] + 2 * block[1] + block[2]]
            return pltpu.make_async_remote_copy(
                src_ref=slot if src is None else src, dst_ref=slot,
                send_sem=send_sems.at[a, k], recv_sem=recv_sems.at[a, k],
                device_id=to, device_id_type=MESH)

        first, passed = [], []
        for a in range(n):
            for j, chip in enumerate(chips):
                cp = copy(a, j, (x, y, c), (*chip, c), src=ins[a].at[c])
                cp.start()
                first.append(cp)
        for a in range(n):
            for j, chip in enumerate(chips):
                copy(a, j, (*chip, c), (x, y, c)).wait_recv()
                cp = copy(a, 3 + j, (*chip, c), sibling)
                cp.start()
                passed.append(cp)
        for a in range(n):
            for j, chip in enumerate(chips):
                copy(a, 3 + j, (*chip, 1 - c), (x, y, c)).wait_recv()
        for cp in first + passed:
            cp.wait_send()

    return pl.pallas_call(
        body, name=name,
        in_specs=[ANY] * n, out_specs=[ANY] * n,
        out_shape=[jax.ShapeDtypeStruct((8,) + s.shape[1:], s.dtype) for s in shards],
        scratch_shapes=[pltpu.SemaphoreType.DMA((n, 6)), pltpu.SemaphoreType.DMA((n, 6))],
    )(*shards)


def _pair_exchange(grads, *, name):
    n = len(grads)

    def body(*refs):
        ins, outs = refs[:n], refs[n:2 * n]
        send_sems, recv_sems = refs[2 * n:]
        x, y, c = _place()
        copies = []
        for a in range(n):
            cp = pltpu.make_async_remote_copy(
                src_ref=ins[a].at[:, 1 - c], dst_ref=outs[a],
                send_sem=send_sems.at[a], recv_sem=recv_sems.at[a],
                device_id=(x, y, 1 - c), device_id_type=MESH)
            cp.start()
            copies.append(cp)
        for cp in copies:
            cp.wait()

    return pl.pallas_call(
        body, name=name,
        in_specs=[ANY] * n, out_specs=[ANY] * n,
        out_shape=[jax.ShapeDtypeStruct((4,) + g.shape[2:], g.dtype) for g in grads],
        scratch_shapes=[pltpu.SemaphoreType.DMA((n,)), pltpu.SemaphoreType.DMA((n,))],
    )(*grads)


def _chip_exchange(wires, *, name):
    n = len(wires)

    def body(*refs):
        ins, outs = refs[:n], refs[n:2 * n]
        send_sems, recv_sems = refs[2 * n:]
        x, y, c = _place()
        chips = [(1 - x, y), (x, 1 - y), (1 - x, 1 - y)]
        copies = []
        for a in range(n):
            for j, (px, py) in enumerate(chips):
                cp = pltpu.make_async_remote_copy(
                    src_ref=ins[a].at[2 * px + py], dst_ref=outs[a].at[j],
                    send_sem=send_sems.at[a, j], recv_sem=recv_sems.at[a, j],
                    device_id=(px, py, c), device_id_type=MESH)
                cp.start()
                copies.append(cp)
        for cp in copies:
            cp.wait()

    return pl.pallas_call(
        body, name=name,
        in_specs=[ANY] * n, out_specs=[ANY] * n,
        out_shape=[jax.ShapeDtypeStruct((3,) + w.shape[1:], w.dtype) for w in wires],
        scratch_shapes=[pltpu.SemaphoreType.DMA((n, 3)), pltpu.SemaphoreType.DMA((n, 3))],
    )(*wires)


def _sibling_share(halves, *, name):
    n = len(halves)

    def body(*refs):
        ins, outs = refs[:n], refs[n:2 * n]
        send_sems, recv_sems = refs[2 * n:]
        x, y, c = _place()
        copies = []
        for a in range(n):
            cp = pltpu.make_async_remote_copy(
                src_ref=ins[a], dst_ref=outs[a],
                send_sem=send_sems.at[a], recv_sem=recv_sems.at[a],
                device_id=(x, y, 1 - c), device_id_type=MESH)
            cp.start()
            copies.append(cp)
        for cp in copies:
            cp.wait()

    return pl.pallas_call(
        body, name=name,
        in_specs=[ANY] * n, out_specs=[ANY] * n,
        out_shape=[jax.ShapeDtypeStruct(h.shape, h.dtype) for h in halves],
        scratch_shapes=[pltpu.SemaphoreType.DMA((n,)), pltpu.SemaphoreType.DMA((n,))],
    )(*halves)


def _row_tile(rows, cols, budget=1 << 20):
    tr = rows
    while tr * cols > budget and tr % 16 == 0:
        tr //= 2
    return tr


def _pair_sum(g, sib, where, *, name):
    _, _, rh, cols = g.shape
    tr = _row_tile(rh, cols)

    def body(w_ref, g_ref, s_ref, o_ref):
        o_ref[...] = (g_ref[...] + s_ref[...]).astype(BF16)

    return pl.pallas_call(
        body, name=name,
        grid_spec=pltpu.PrefetchScalarGridSpec(
            num_scalar_prefetch=1, grid=(4, rh // tr),
            in_specs=[pl.BlockSpec((None, None, tr, cols), lambda s, i, w: (s, w[1], i, 0)),
                      pl.BlockSpec((None, tr, cols), lambda s, i, w: (s, i, 0))],
            out_specs=pl.BlockSpec((None, tr, cols), lambda s, i, w: (s, i, 0))),
        out_shape=jax.ShapeDtypeStruct((4, rh, cols), BF16),
        compiler_params=_params(("parallel", "parallel")),
    )(where, g, sib)


def _final_half(g, sib, recv, where, *, name):
    _, _, rh, cols = g.shape
    tr = _row_tile(rh, cols)

    def body(w_ref, g_ref, s_ref, r_ref, o_ref):
        acc = g_ref[...] + s_ref[...]
        for j in range(3):
            acc = acc + r_ref[j].astype(F32)
        o_ref[...] = acc

    return pl.pallas_call(
        body, name=name,
        grid_spec=pltpu.PrefetchScalarGridSpec(
            num_scalar_prefetch=1, grid=(rh // tr,),
            in_specs=[pl.BlockSpec((None, None, tr, cols), lambda i, w: (w[0], w[1], i, 0)),
                      pl.BlockSpec((None, tr, cols), lambda i, w: (w[0], i, 0)),
                      pl.BlockSpec((3, tr, cols), lambda i, w: (0, i, 0))],
            out_specs=pl.BlockSpec((tr, cols), lambda i, w: (i, 0))),
        out_shape=jax.ShapeDtypeStruct((rh, cols), F32),
        compiler_params=_params(("parallel",)),
    )(where, g, sib, recv)


def _adamw_math(w, g, m, v):
    m = ADAM_B1 * m + (1.0 - ADAM_B1) * g
    v = ADAM_B2 * v + (1.0 - ADAM_B2) * (g * g)
    m_hat = m / (1.0 - ADAM_B1 ** ADAM_STEP)
    v_hat = v / (1.0 - ADAM_B2 ** ADAM_STEP)
    delta = -ADAM_LR * (m_hat / (jnp.sqrt(v_hat) + ADAM_EPS) + ADAM_WD * w)
    return delta, m, v


def _adamw(w, mine, theirs, m, v, where, *, name):
    rows, cols = w.shape
    tr = _row_tile(rows // 2, cols, budget=1 << 19)
    nh = rows // 2 // tr

    def body(wh_ref, w_ref, a_ref, b_ref, m_ref, v_ref, g_ref, d_ref, nm_ref, nv_ref):
        g = jnp.where(pl.program_id(0) // nh == wh_ref[1], a_ref[...], b_ref[...])
        d, nm, nv = _adamw_math(w_ref[...], g, m_ref[...], v_ref[...])
        g_ref[...] = g
        d_ref[...] = d
        nm_ref[...] = nm
        nv_ref[...] = nv

    blk = pl.BlockSpec((tr, cols), lambda i, wh: (i, 0))
    half = pl.BlockSpec((tr, cols), lambda i, wh: (i % nh, 0))
    return pl.pallas_call(
        body, name=name,
        grid_spec=pltpu.PrefetchScalarGridSpec(
            num_scalar_prefetch=1, grid=(rows // tr,),
            in_specs=[blk, half, half, blk, blk], out_specs=[blk] * 4),
        out_shape=[jax.ShapeDtypeStruct((rows, cols), F32)] * 4,
        compiler_params=_params(("parallel",)),
    )(where, w, mine, theirs, m, v)


SEG_LOSS = 0
SEG_SINK = 128
SEG_AGAIN = 256
SEG_L0 = SEG_AGAIN + ATTN_W
SEG_L1 = SEG_L0 + RNN_W
SEG_RGAIN = SEG_L1 + RNN_W
SEG_G = SEG_RGAIN + 128
N_PACK = SEG_G + 4 * D_MODEL


def _pack(sinks, again, l0, l1, rgain, gains, loss=None):
    z = lambda k: jnp.zeros((1, k), F32)
    first = z(128) if loss is None else loss
    return jnp.concatenate([first, sinks, z(128 - N_Q), again, l0, l1, rgain] + list(gains), axis=1)


def _small_reduce_adamw(part, w, m, v, *, name):
    def body(p_ref, w_ref, m_ref, v_ref, g_ref, d_ref, nm_ref, nv_ref, buf_ref, send_sems, recv_sems):
        x, y, c = _place()
        me = 4 * x + 2 * y + c
        copies = []
        for k in range(1, 8):
            dx, dy, dc = (k >> 2) & 1, (k >> 1) & 1, k & 1
            to = (x ^ dx, y ^ dy, c ^ dc)
            cp = pltpu.make_async_remote_copy(
                src_ref=p_ref, dst_ref=buf_ref.at[me],
                send_sem=send_sems.at[k - 1], recv_sem=recv_sems.at[k - 1],
                device_id=to, device_id_type=MESH)
            cp.start()
            copies.append(cp)
        buf_ref[me] = p_ref[...]
        for cp in copies:
            cp.wait()
        tot = buf_ref[0]
        for j in range(1, 8):
            tot = tot + buf_ref[j]
        g_ref[...] = tot
        l0 = w_ref[:, SEG_L0:SEG_L0 + RNN_W]
        l1 = w_ref[:, SEG_L1:SEG_L1 + RNN_W]
        mx = jnp.maximum(l0, l1)
        e0 = jnp.exp(l0 - mx)
        e1 = jnp.exp(l1 - mx)
        lb = e0 / (e0 + e1)
        gl0 = tot[:, SEG_L0:SEG_L0 + RNN_W] * lb * (1.0 - lb)
        g_ref[:, SEG_L0:SEG_L0 + RNN_W] = gl0
        g_ref[:, SEG_L1:SEG_L1 + RNN_W] = -gl0
        d, nm, nv = _adamw_math(w_ref[...], g_ref[...], m_ref[...], v_ref[...])
        d_ref[...] = d
        nm_ref[...] = nm
        nv_ref[...] = nv

    vm = pl.BlockSpec(memory_space=pltpu.VMEM)
    return pl.pallas_call(
        body, name=name,
        in_specs=[vm] * 4, out_specs=[vm] * 4,
        out_shape=[jax.ShapeDtypeStruct((1, N_PACK), F32)] * 4,
        scratch_shapes=[pltpu.VMEM((8, 1, N_PACK), F32), pltpu.SemaphoreType.DMA((7,)),
                        pltpu.SemaphoreType.DMA((7,))],
    )(part, w, m, v)


def _layer_grads(xs, tgt, w_in, w_in_t, w_out, w_out_t, w_up, w_up_t, w_dn, w_dn_t,
                 sinks, again, lb_logits, rgain, g_mix_pre, g_mix_post, g_mlp_pre, g_mlp_post):
    tm = 512
    h1 = _rms_cast(xs, g_mix_pre, tm=tm, name="h1_norm")
    proj = _mm(h1, w_in, tm=tm, tn=768, tk=D_MODEL, out_dtype=F32, name="in_proj")
    attn, lse = _swa_fwd(proj, sinks, name="swa_fwd")
    o_pre, rnn, s0 = _hgrn_fwd(proj, lb_logits, rgain, tb=512, name="hgrn_fwd")
    cat = _mix_cat(attn, rnn, again, tm=tm, name="mix_cat")
    mixed = _mm(cat, w_out, tm=tm, tn=1024, tk=D_MODEL, out_dtype=F32, name="out_proj")
    x1, h2 = _post_norm_res(mixed, g_mix_post, xs, g_mlp_pre, tm=256, name="mix_post")
    u = _mm(h2, w_up, tm=tm, tn=1024, tk=D_MODEL, out_dtype=BF16, relu=True, name="mlp_up")
    yv = _mm(u, w_dn, tm=tm, tn=1024, tk=2048, out_dtype=F32, a_square=True, name="mlp_down")
    dy, dx2, loss_row, dg_mlp_post = _loss_head(yv, g_mlp_post, x1, tgt, tm=256, name="loss_head")
    du = _mm(dy, w_dn_t, tm=tm, tn=1024, tk=D_MODEL, out_dtype=BF16, mul2=u, name="mlp_down_bwd")
    d_w_dn = _mm_tn(u, dy, tm=1024, tn=1024, tt=512, a_square=True, name="w_down_grad")
    d_w_up = _mm_tn(h2, du, tm=1024, tn=1024, tt=512, n_split=N_CHIPS, name="w_up_grad")
    dh2 = _mm(du, w_up_t, tm=tm, tn=1024, tk=2048, out_dtype=F32, name="mlp_up_bwd")
    dx1, dg_mlp_pre = _rms_bwd(dh2, x1, g_mlp_pre, dx2, tm=256, out_dtype=F32, name="mlp_pre_bwd")
    dmixed, dg_mix_post = _rms_bwd(dx1, mixed, g_mix_post, None, tm=256, out_dtype=BF16, name="mix_post_bwd")
    d_w_out = _mm_tn(cat, dmixed, tm=1024, tn=1024, tt=512, name="w_out_grad")
    dcat = _mm(dmixed, w_out_t, tm=tm, tn=1024, tk=D_MODEL, out_dtype=F32, name="out_proj_bwd")
    dattn, dg_again = _rms_bwd(dcat, attn, again, None, tm=tm, out_dtype=F32, name="attn_norm_bwd")
    dq_a, dkv, dsinks = _swa_bwd(proj, sinks, dattn, lse, name="swa_bwd")
    dq_r, df_r, di_r, dg_r, dlb, dgain_h = _hgrn_bwd(proj, lb_logits, rgain, o_pre, s0, dcat, tb=512,
                                                     name="hgrn_bwd")
    dproj = jnp.concatenate([dq_a, dkv, dq_r, df_r, di_r, dg_r], axis=1).astype(BF16)
    d_w_in = _mm_tn(h1, dproj, tm=1024, tn=768, tt=512, name="w_in_grad")
    dh1 = _mm(dproj, w_in_t, tm=tm, tn=1024, tk=768, out_dtype=F32, name="in_proj_bwd")
    gx, dg_mix_pre = _rms_bwd(dh1, xs, g_mix_pre, dx1, tm=256, out_dtype=F32, name="mix_pre_bwd")

    shard = IN_W // N_CHIPS
    big = [d_w_in[0].reshape(D_MODEL, N_CHIPS, shard).transpose(1, 0, 2),
           d_w_out.reshape(N_CHIPS, D_MODEL // N_CHIPS, D_MODEL),
           d_w_up,
           d_w_dn.reshape(N_CHIPS, D_FF // N_CHIPS, D_MODEL)]
    drgain = jnp.sum(dgain_h, axis=0)
    small = _pack(dsinks, dg_again, dlb, jnp.zeros_like(dlb), drgain,
                  [dg_mix_pre, dg_mix_post, dg_mlp_pre, dg_mlp_post], loss=loss_row)
    return gx, big, small


def kernel(x, w_in, attn_sinks, attn_out_gain, rnn_lb_logits, rnn_norm_gain, w_out, mix_pre_gain, mix_post_gain, mlp_pre_gain, mlp_post_gain, w_up, w_down, loss_target, m_w_in, m_attn_sinks, m_attn_out_gain, m_rnn_lb_logits, m_rnn_norm_gain, m_w_out, m_mix_pre_gain, m_mix_post_gain, m_mlp_pre_gain, m_mlp_post_gain, m_w_up, m_w_down, v_w_in, v_attn_sinks, v_attn_out_gain, v_rnn_lb_logits, v_rnn_norm_gain, v_w_out, v_mix_pre_gain, v_mix_post_gain, v_mlp_pre_gain, v_mlp_post_gain, v_w_up, v_w_down):
    ax, ay, ac = _place()
    where = jnp.stack([2 * ax + ay, ac]).astype(jnp.int32)
    big_w = [w_in[0], w_out[0], w_up[0], w_down[0]]
    big_m = [m_w_in[0], m_w_out[0], m_w_up[0], m_w_down[0]]
    big_v = [v_w_in[0], v_w_out[0], v_w_up[0], v_w_down[0]]

    halves = [w.astype(BF16).reshape(2, w.shape[0] // 2, w.shape[1]) for w in big_w]
    gathered = _all_gather_halves(halves, name="gather_weights")
    g_in, g_out, g_up, g_dn = [
        lax.dynamic_update_slice(g, h, (2 * where[0], 0, 0)) for g, h in zip(gathered, halves)]
    shard = IN_W // N_CHIPS
    w_in4 = g_in.reshape(N_CHIPS, D_MODEL, shard)
    w_in_f = w_in4.transpose(1, 0, 2).reshape(D_MODEL, IN_W)
    w_in_t = w_in4.transpose(0, 2, 1).reshape(IN_W, D_MODEL)
    w_out_f = g_out.reshape(D_MODEL, D_MODEL)
    w_up4 = g_up.reshape(N_CHIPS, D_MODEL, D_FF // N_CHIPS)
    w_up_f = w_up4.transpose(1, 0, 2).reshape(D_MODEL, D_FF)
    w_up_t = w_up4.transpose(0, 2, 1).reshape(D_FF, D_MODEL)
    w_dn_f = g_dn.reshape(D_FF, D_MODEL)

    gx, big_g, small_part = _layer_grads(
        x[0], loss_target[0], w_in_f, w_in_t, w_out_f, w_out_f.T, w_up_f, w_up_t, w_dn_f, w_dn_f.T,
        attn_sinks, attn_out_gain, rnn_lb_logits, rnn_norm_gain,
        mix_pre_gain, mix_post_gain, mlp_pre_gain, mlp_post_gain)

    g5 = [g.reshape(4, 2, g.shape[1] // 2, g.shape[2]) for g in big_g]
    names = ["w_in", "w_out", "w_up", "w_down"]
    sib = _pair_exchange(g5, name="pair_exchange")
    wires = [_pair_sum(g, s, where, name="pair_sum_" + nm) for g, s, nm in zip(g5, sib, names)]
    recv = _chip_exchange(wires, name="chip_exchange")
    fins = [_final_half(g, s, r, where, name="final_half_" + nm) for g, s, r, nm in zip(g5, sib, recv, names)]
    theirs = _sibling_share(fins, name="sibling_share")
    grads, deltas, new_m, new_v = [], [], [], []
    for f, o, w, m, v, nm in zip(fins, theirs, big_w, big_m, big_v, names):
        g, d, nm_, nv_ = _adamw(w, f, o, m, v, where, name="adamw_" + nm)
        grads.append(g[None])
        deltas.append(d[None])
        new_m.append(nm_[None])
        new_v.append(nv_[None])

    def pack_params(sinks, again, logits, rgain, gains):
        return _pack(sinks, again, logits[0:1], logits[1:2], rgain, gains)

    pw = pack_params(attn_sinks, attn_out_gain, rnn_lb_logits, rnn_norm_gain,
                     [mix_pre_gain, mix_post_gain, mlp_pre_gain, mlp_post_gain])
    pm = pack_params(m_attn_sinks, m_attn_out_gain, m_rnn_lb_logits, m_rnn_norm_gain,
                     [m_mix_pre_gain, m_mix_post_gain, m_mlp_pre_gain, m_mlp_post_gain])
    pv = pack_params(v_attn_sinks, v_attn_out_gain, v_rnn_lb_logits, v_rnn_norm_gain,
                     [v_mix_pre_gain, v_mix_post_gain, v_mlp_pre_gain, v_mlp_post_gain])
    packs = _small_reduce_adamw(small_part, pw, pm, pv, name="small_reduce_adamw")

    def unpack(p):
        seg = lambda o, k: p[:, o:o + k]
        logits = jnp.concatenate([seg(SEG_L0, RNN_W), seg(SEG_L1, RNN_W)], axis=0)
        gains = [seg(SEG_G + i * D_MODEL, D_MODEL) for i in range(4)]
        return dict(sinks=seg(SEG_SINK, N_Q), again=seg(SEG_AGAIN, ATTN_W), logits=logits,
                    rgain=seg(SEG_RGAIN, RNN_HD), gains=gains)

    def order(small, big):
        return [big[0], small["sinks"], small["again"], small["logits"], small["rgain"], big[1],
                *small["gains"], big[2], big[3]]

    loss = packs[0][0, 0]
    outs = [loss, gx[None]]
    for p, b in zip(packs, [grads, deltas, new_m, new_v]):
        outs += order(unpack(p), b)
    return tuple(outs)
```

```python
import functools

import jax
import jax.numpy as jnp
from jax import lax
from jax.experimental import pallas as pl
from jax.experimental.pallas import tpu as pltpu

F32 = jnp.float32
BF16 = jnp.bfloat16
MESH = pl.DeviceIdType.MESH

EPS = 1e-6
D_MODEL = 2048
ATTN_W = 1024
HEAD_DIM = 64
N_Q = 16
N_KV = 2
GROUP = 8
BLK = 128
RNN_W = 1024
RNN_HD = 128
N_RNN = 8
CHUNK = 64
SUB = 16
D_FF = 8192
IN_W = 5376
N_CHIPS = 4
KV_COL = ATTN_W
QR_COL = ATTN_W + 2 * 128
FR_COL = QR_COL + RNN_W
IR_COL = FR_COL + RNN_W
GR_COL = IR_COL + RNN_W

ADAM_LR = 0.001
ADAM_B1 = 0.9
ADAM_B2 = 0.999
ADAM_EPS = 1e-08
ADAM_WD = 0.01
ADAM_STEP = 10

VMEM_LIMIT = 48 * 1024 * 1024

NT = (((1,), (1,)), ((), ()))
TN = (((0,), (0,)), ((), ()))


def _params(sem=None):
    return pltpu.CompilerParams(dimension_semantics=sem, vmem_limit_bytes=VMEM_LIMIT)


def _sigmoid(x):
    return 1.0 / (1.0 + jnp.exp(-x))


ANY = pl.BlockSpec(memory_space=pl.ANY)


def _place():
    return lax.axis_index("x"), lax.axis_index("y"), lax.axis_index("c")


def _other_chips(x, y):
    return [(1 - x, y), (x, 1 - y), (1 - x, 1 - y)]


class _Exchange:
    def __init__(self, srcs, outs, ncopy, build, aliases=None):
        self.srcs, self.outs, self.ncopy, self.build = list(srcs), list(outs), ncopy, build
        self.aliases = aliases or {}


def _remote(src, dst, send_sems, recv_sems, k, to):
    return pltpu.make_async_remote_copy(src_ref=src, dst_ref=dst, send_sem=send_sems.at[k],
                                        recv_sem=recv_sems.at[k], device_id=to, device_id_type=MESH)


def _call(body, *, name, grid, in_specs, out_specs, out_shape, args, scratch_shapes=(), semantics=None,
          exchanges=()):
    in_specs, out_specs, out_shape = list(in_specs), list(out_specs), list(out_shape)
    scratch_shapes = list(scratch_shapes)
    ni, no, ns = len(in_specs), len(out_specs), len(scratch_shapes)
    xsrc = [s for x in exchanges for s in x.srcs]
    xout = [o for x in exchanges for o in x.outs]
    nxi, nxo = len(xsrc), len(xout)
    aliases = {}
    a0 = b0 = 0
    for x in exchanges:
        for si, oi in x.aliases.items():
            aliases[ni + a0 + si] = no + b0 + oi
        a0 += len(x.srcs)
        b0 += len(x.outs)
    sems = []
    for x in exchanges:
        sems += [pltpu.SemaphoreType.DMA((x.ncopy,)), pltpu.SemaphoreType.DMA((x.ncopy,))]

    def wrapped(*refs):
        ins, xi = refs[:ni], refs[ni:ni + nxi]
        outs, xo = refs[ni + nxi:ni + nxi + no], refs[ni + nxi + no:ni + nxi + no + nxo]
        rest = refs[ni + nxi + no + nxo:]
        scr, sm = rest[:ns], rest[ns:]

        def copies():
            cps = []
            a = b = 0
            for k, x in enumerate(exchanges):
                cps += x.build(xi[a:a + len(x.srcs)], xo[b:b + len(x.outs)], sm[2 * k], sm[2 * k + 1])
                a += len(x.srcs)
                b += len(x.outs)
            return cps

        def start():
            for cp in copies():
                cp.start()

        def wait():
            for cp in copies():
                cp.wait()

        if not exchanges:
            body(*ins, *outs, *scr)
        elif not grid:
            start()
            body(*ins, *outs, *scr)
            wait()
        else:
            first = last = None
            for ax, g in enumerate(grid):
                f = pl.program_id(ax) == 0
                l = pl.program_id(ax) == g - 1
                first = f if first is None else first & f
                last = l if last is None else last & l
            pl.when(first)(start)
            body(*ins, *outs, *scr)
            pl.when(last)(wait)

    if exchanges and semantics is not None:
        semantics = ("arbitrary",) * len(grid)
    kwargs = dict(grid=grid) if grid else {}
    res = pl.pallas_call(
        wrapped, name=name,
        in_specs=in_specs + [ANY] * nxi, out_specs=out_specs + [ANY] * nxo,
        out_shape=out_shape + xout, scratch_shapes=scratch_shapes + sems,
        input_output_aliases=aliases,
        compiler_params=_params(semantics), **kwargs,
    )(*args, *xsrc)
    res = list(res)
    mine, theirs = res[:no], res[no:]
    per = []
    b = 0
    for x in exchanges:
        per.append(theirs[b:b + len(x.outs)])
        b += len(x.outs)
    return mine, per


def _run_exchange(x, *, name):
    return _call(lambda: None, name=name, grid=(), in_specs=[], out_specs=[], out_shape=[], args=[],
                 exchanges=[x])[1][0]


def _x_gather_ici(halves):
    n = len(halves)

    def build(srcs, outs, ss, rs):
        x, y, c = _place()
        return [_remote(srcs[a].at[c], outs[a].at[4 * x + 2 * y + c], ss, rs, 3 * a + j, (px, py, c))
                for a in range(n) for j, (px, py) in enumerate(_other_chips(x, y))]

    outs = [jax.ShapeDtypeStruct((8,) + h.shape[1:], h.dtype) for h in halves]
    return _Exchange(halves, outs, 3 * n, build)


def _x_gather_d2d(bufs):
    n = len(bufs)

    def build(srcs, outs, ss, rs):
        x, y, c = _place()
        cps = []
        for a in range(n):
            for j, (px, py) in enumerate(_other_chips(x, y)):
                slot = 4 * px + 2 * py + c
                cps.append(_remote(srcs[a].at[slot], outs[a].at[slot], ss, rs, 3 * a + j, (x, y, 1 - c)))
        return cps

    outs = [jax.ShapeDtypeStruct(b.shape, b.dtype) for b in bufs]
    return _Exchange(bufs, outs, 3 * n, build, aliases={a: a for a in range(n)})


def _x_pair(grads):
    n = len(grads)

    def build(srcs, outs, ss, rs):
        x, y, c = _place()
        return [_remote(srcs[a].at[:, 1 - c], outs[a], ss, rs, a, (x, y, 1 - c)) for a in range(n)]

    outs = [jax.ShapeDtypeStruct((4,) + g.shape[2:], g.dtype) for g in grads]
    return _Exchange(grads, outs, n, build)


def _x_chip(wires):
    n = len(wires)

    def build(srcs, outs, ss, rs):
        x, y, c = _place()
        return [_remote(srcs[a].at[2 * px + py], outs[a].at[j], ss, rs, 3 * a + j, (px, py, c))
                for a in range(n) for j, (px, py) in enumerate(_other_chips(x, y))]

    outs = [jax.ShapeDtypeStruct((3,) + w.shape[1:], w.dtype) for w in wires]
    return _Exchange(wires, outs, 3 * n, build)


def _x_share(halves):
    n = len(halves)

    def build(srcs, outs, ss, rs):
        x, y, c = _place()
        return [_remote(srcs[a], outs[a], ss, rs, a, (x, y, 1 - c)) for a in range(n)]

    outs = [jax.ShapeDtypeStruct(h.shape, h.dtype) for h in halves]
    return _Exchange(halves, outs, n, build)


def _mm(a, w, *, tm, tn, tk, out_dtype, name, a_square=False, relu=False, mul2=None, exchanges=()):
    m, k = a.shape
    _, n = w.shape
    nk = k // tk
    assert m % tm == 0 and n % tn == 0 and k % tk == 0

    def body(*refs):
        if mul2 is not None:
            a_ref, w_ref, e_ref, o_ref, acc_ref = refs
        else:
            a_ref, w_ref, o_ref, acc_ref = refs
            e_ref = None
        kk = pl.program_id(2)
        av = a_ref[...]
        if a_square:
            af = av.astype(F32)
            av = (af * af).astype(BF16)
        part = jnp.dot(av, w_ref[...], preferred_element_type=F32)

        def finish(r):
            if relu:
                r = jnp.maximum(r, 0.0)
            if e_ref is not None:
                r = 2.0 * e_ref[...].astype(F32) * r
            o_ref[...] = r.astype(out_dtype)

        if nk == 1:
            finish(part)
        else:
            @pl.when(kk == 0)
            def _():
                acc_ref[...] = part

            @pl.when(kk > 0)
            def _():
                acc_ref[...] += part

            @pl.when(kk == nk - 1)
            def _():
                finish(acc_ref[...])

    in_specs = [pl.BlockSpec((tm, tk), lambda i, j, kk: (i, kk)),
                pl.BlockSpec((tk, tn), lambda i, j, kk: (kk, j))]
    args = [a, w]
    if mul2 is not None:
        in_specs.append(pl.BlockSpec((tm, tn), lambda i, j, kk: (i, j)))
        args.append(mul2)
    acc_shape = (tm, tn) if nk > 1 else (8, 128)
    (out,), per = _call(
        body, name=name, grid=(m // tm, n // tn, nk),
        in_specs=in_specs, out_specs=[pl.BlockSpec((tm, tn), lambda i, j, kk: (i, j))],
        out_shape=[jax.ShapeDtypeStruct((m, n), out_dtype)], args=args,
        scratch_shapes=[pltpu.VMEM(acc_shape, F32)],
        semantics=("parallel", "parallel", "arbitrary"), exchanges=exchanges)
    return (out, per) if exchanges else out


def _mm_tn(a, b, *, tm, tn, tt, name, a_square=False, n_split=1, exchanges=()):
    t, m = a.shape
    _, n = b.shape
    assert t % tt == 0 and m % tm == 0 and n % tn == 0 and (n // n_split) % tn == 0
    per = n // n_split // tn

    def body(a_ref, b_ref, o_ref):
        ti = pl.program_id(2)
        av = a_ref[...]
        if a_square:
            af = av.astype(F32)
            av = (af * af).astype(BF16)
        part = lax.dot_general(av, b_ref[...], TN, preferred_element_type=F32)

        @pl.when(ti == 0)
        def _():
            o_ref[...] = part

        @pl.when(ti > 0)
        def _():
            o_ref[...] += part

    (out,), xres = _call(
        body, name=name, grid=(m // tm, n // tn, t // tt),
        in_specs=[pl.BlockSpec((tt, tm), lambda i, j, ti: (ti, i)),
                  pl.BlockSpec((tt, tn), lambda i, j, ti: (ti, j))],
        out_specs=[pl.BlockSpec((None, tm, tn), lambda i, j, ti: (j // per, i, j % per))],
        out_shape=[jax.ShapeDtypeStruct((n_split, m, n // n_split), F32)], args=[a, b],
        semantics=("parallel", "parallel", "arbitrary"), exchanges=exchanges)
    return (out, xres) if exchanges else out


def _rstd(x):
    return lax.rsqrt(jnp.mean(x * x, axis=-1, keepdims=True) + EPS)


def _rms_cast(x, g, *, tm, name):
    t, d = x.shape

    def body(x_ref, g_ref, o_ref):
        xv = x_ref[...]
        o_ref[...] = (xv * _rstd(xv) * g_ref[...]).astype(BF16)

    return pl.pallas_call(
        body, name=name, grid=(t // tm,),
        in_specs=[pl.BlockSpec((tm, d), lambda i: (i, 0)), pl.BlockSpec((1, d), lambda i: (0, 0))],
        out_specs=pl.BlockSpec((tm, d), lambda i: (i, 0)),
        out_shape=jax.ShapeDtypeStruct((t, d), BF16),
        compiler_params=_params(("parallel",)),
    )(x, g)


def _mix_cat(attn, rnn, gain, *, tm, name):
    t = attn.shape[0]

    def body(a_ref, r_ref, g_ref, o_ref):
        av = a_ref[...]
        o_ref[:, :ATTN_W] = (av * _rstd(av) * g_ref[...]).astype(BF16)
        o_ref[:, ATTN_W:] = r_ref[...].astype(BF16)

    return pl.pallas_call(
        body, name=name, grid=(t // tm,),
        in_specs=[pl.BlockSpec((tm, ATTN_W), lambda i: (i, 0)), pl.BlockSpec((tm, RNN_W), lambda i: (i, 0)),
                  pl.BlockSpec((1, ATTN_W), lambda i: (0, 0))],
        out_specs=pl.BlockSpec((tm, D_MODEL), lambda i: (i, 0)),
        out_shape=jax.ShapeDtypeStruct((t, D_MODEL), BF16),
        compiler_params=_params(("parallel",)),
    )(attn, rnn, gain)


def _post_norm_res(mixed, g_post, res, g_next, *, tm, name):
    t, d = mixed.shape

    def body(m_ref, gp_ref, r_ref, gn_ref, x1_ref, h2_ref):
        mv = m_ref[...]
        x1 = r_ref[...] + mv * _rstd(mv) * gp_ref[...]
        x1_ref[...] = x1
        h2_ref[...] = (x1 * _rstd(x1) * gn_ref[...]).astype(BF16)

    row = pl.BlockSpec((tm, d), lambda i: (i, 0))
    vec = pl.BlockSpec((1, d), lambda i: (0, 0))
    return pl.pallas_call(
        body, name=name, grid=(t // tm,),
        in_specs=[row, vec, row, vec], out_specs=[row, row],
        out_shape=[jax.ShapeDtypeStruct((t, d), F32), jax.ShapeDtypeStruct((t, d), BF16)],
        compiler_params=_params(("parallel",)),
    )(mixed, g_post, res, g_next)


def _rms_bwd(dyn, xin, g, res, *, tm, out_dtype, name, col_block=0, exchanges=()):
    t, d = xin.shape

    def body(*refs):
        if res is not None:
            dy_ref, x_ref, g_ref, r_ref, dx_ref, dg_ref = refs
        else:
            dy_ref, x_ref, g_ref, dx_ref, dg_ref = refs
        i = pl.program_id(0)
        xv = x_ref[...]
        dy = dy_ref[...].astype(F32)
        r = _rstd(xv)
        xh = xv * r
        part = jnp.sum(dy * xh, axis=0, keepdims=True)

        @pl.when(i == 0)
        def _():
            dg_ref[...] = part

        @pl.when(i > 0)
        def _():
            dg_ref[...] += part

        tt = dy * g_ref[...]
        dx = r * (tt - xh * jnp.mean(tt * xh, axis=-1, keepdims=True))
        if res is not None:
            dx = dx + r_ref[...]
        dx_ref[...] = dx.astype(out_dtype)

    row = pl.BlockSpec((tm, d), lambda i: (i, 0))
    vec = pl.BlockSpec((1, d), lambda i: (0, 0))
    in_specs = [pl.BlockSpec((tm, d), lambda i: (i, col_block)), row, vec]
    args = [dyn, xin, g]
    if res is not None:
        in_specs.append(row)
        args.append(res)
    res, xres = _call(
        body, name=name, grid=(t // tm,),
        in_specs=in_specs, out_specs=[row, vec],
        out_shape=[jax.ShapeDtypeStruct((t, d), out_dtype), jax.ShapeDtypeStruct((1, d), F32)], args=args,
        semantics=("arbitrary",), exchanges=exchanges)
    return (*res, xres) if exchanges else res


def _loss_head(y, g_post, x1, target, *, tm, name):
    t, d = y.shape

    def body(y_ref, g_ref, x1_ref, t_ref, dy_ref, dx2_ref, loss_ref, dg_ref):
        i = pl.program_id(0)
        yv = y_ref[...]
        r = _rstd(yv)
        yh = yv * r
        gv = g_ref[...]
        err = x1_ref[...] + yh * gv - t_ref[...]
        lpart = 0.5 * jnp.sum(jnp.mean(err * err, axis=-1, keepdims=True), axis=0, keepdims=True)
        dx2 = err * (1.0 / d)
        dgp = jnp.sum(dx2 * yh, axis=0, keepdims=True)
        lane = lax.broadcasted_iota(jnp.int32, (1, 128), 1)
        lrow = jnp.where(lane == 0, lpart, 0.0)

        @pl.when(i == 0)
        def _():
            dg_ref[...] = dgp
            loss_ref[...] = lrow

        @pl.when(i > 0)
        def _():
            dg_ref[...] += dgp
            loss_ref[...] += lrow

        tt = dx2 * gv
        dy_ref[...] = (r * (tt - yh * jnp.mean(tt * yh, axis=-1, keepdims=True))).astype(BF16)
        dx2_ref[...] = dx2

    row = pl.BlockSpec((tm, d), lambda i: (i, 0))
    vec = pl.BlockSpec((1, d), lambda i: (0, 0))
    return pl.pallas_call(
        body, name=name, grid=(t // tm,),
        in_specs=[row, vec, row, row],
        out_specs=[row, row, pl.BlockSpec((1, 128), lambda i: (0, 0)), vec],
        out_shape=[jax.ShapeDtypeStruct((t, d), BF16), jax.ShapeDtypeStruct((t, d), F32),
                   jax.ShapeDtypeStruct((1, 128), F32), jax.ShapeDtypeStruct((1, d), F32)],
        compiler_params=_params(("arbitrary",)),
    )(y, g_post, x1, target)


def _alibi_slope(h):
    return 2.0 ** (-8.0 * (h + 1) / N_Q)


def _swa_mask(n):
    row = lax.broadcasted_iota(jnp.int32, (BLK, 2 * BLK), 0)
    col = lax.broadcasted_iota(jnp.int32, (BLK, 2 * BLK), 1)
    dist = row + BLK - col
    valid = (dist >= 0) & (dist < BLK) & ((col >= BLK) | (n > 0))
    return valid, dist.astype(F32)


def _swa_scores(q_ref, kcat, hh, valid, distf):
    qh = q_ref[:, hh * HEAD_DIM:(hh + 1) * HEAD_DIM].astype(BF16)
    s = lax.dot_general(qh, kcat, NT, preferred_element_type=F32) * (HEAD_DIM ** -0.5)
    s = s - _alibi_slope(hh) * distf
    return qh, jnp.where(valid, s, -1e30)


def _kv_cat(kvp_ref, kvc_ref, off):
    return jnp.concatenate([kvp_ref[:, off:off + HEAD_DIM], kvc_ref[:, off:off + HEAD_DIM]], axis=0).astype(BF16)


def _swa_fwd(proj, sinks, *, name, exchanges=()):
    t = proj.shape[0]
    nb = t // BLK
    kvb = KV_COL // (2 * 128)

    def body(q_ref, kvc_ref, kvp_ref, sink_ref, o_ref, lse_ref):
        n = pl.program_id(0)
        valid, distf = _swa_mask(n)
        for kvh in range(N_KV):
            kcat = _kv_cat(kvp_ref, kvc_ref, kvh * HEAD_DIM)
            vcat = _kv_cat(kvp_ref, kvc_ref, 128 + kvh * HEAD_DIM)
            for gi in range(GROUP):
                hh = kvh * GROUP + gi
                _, s = _swa_scores(q_ref, kcat, hh, valid, distf)
                sink = sink_ref[0:1, hh:hh + 1]
                mx = jnp.maximum(jnp.max(s, axis=1, keepdims=True), sink)
                p = jnp.exp(s - mx)
                l = jnp.sum(p, axis=1, keepdims=True) + jnp.exp(sink - mx)
                probs = p / l
                o_ref[:, hh * HEAD_DIM:(hh + 1) * HEAD_DIM] = jnp.dot(
                    probs.astype(BF16), vcat, preferred_element_type=F32)
                lse_ref[:, hh:hh + 1] = mx + jnp.log(l)

    res, xres = _call(
        body, name=name, grid=(nb,),
        in_specs=[pl.BlockSpec((BLK, ATTN_W), lambda n: (n, 0)),
                  pl.BlockSpec((BLK, 256), lambda n: (n, kvb)),
                  pl.BlockSpec((BLK, 256), lambda n: (jnp.maximum(n - 1, 0), kvb)),
                  pl.BlockSpec((1, N_Q), lambda n: (0, 0))],
        out_specs=[pl.BlockSpec((BLK, ATTN_W), lambda n: (n, 0)), pl.BlockSpec((BLK, N_Q), lambda n: (n, 0))],
        out_shape=[jax.ShapeDtypeStruct((t, ATTN_W), F32), jax.ShapeDtypeStruct((t, N_Q), F32)],
        args=[proj, proj, proj, sinks], semantics=("parallel",), exchanges=exchanges)
    return (*res, xres) if exchanges else res


def _swa_bwd(proj, sinks, dattn, lse, *, name, exchanges=()):
    t = proj.shape[0]
    nb = t // BLK
    kvb = KV_COL // (2 * 128)

    def body(q_ref, kvc_ref, kvp_ref, sink_ref, do_ref, lse_ref, dq_ref, dkv_ref, dsink_ref, carry_ref):
        n = pl.program_id(0)

        @pl.when(n == 0)
        def _():
            dsink_ref[...] = jnp.zeros_like(dsink_ref)
            carry_ref[...] = jnp.zeros_like(carry_ref)

        @pl.when(n < nb)
        def _():
            valid, distf = _swa_mask(n)
            for kvh in range(N_KV):
                kcat = _kv_cat(kvp_ref, kvc_ref, kvh * HEAD_DIM)
                vcat = _kv_cat(kvp_ref, kvc_ref, 128 + kvh * HEAD_DIM)
                dk_cat = jnp.zeros((2 * BLK, HEAD_DIM), F32)
                dv_cat = jnp.zeros((2 * BLK, HEAD_DIM), F32)
                for gi in range(GROUP):
                    hh = kvh * GROUP + gi
                    qh, s = _swa_scores(q_ref, kcat, hh, valid, distf)
                    lse_h = lse_ref[:, hh:hh + 1]
                    probs = jnp.exp(s - lse_h)
                    doh = do_ref[:, hh * HEAD_DIM:(hh + 1) * HEAD_DIM].astype(BF16)
                    dprobs = lax.dot_general(doh, vcat, NT, preferred_element_type=F32)
                    delta = jnp.sum(probs * dprobs, axis=1, keepdims=True)
                    ds = (probs * (dprobs - delta)).astype(BF16)
                    psink = jnp.exp(sink_ref[0:1, hh:hh + 1] - lse_h)
                    dsink_ref[0:1, hh:hh + 1] += -jnp.sum(psink * delta, axis=0, keepdims=True)
                    dv_cat = dv_cat + lax.dot_general(probs.astype(BF16), doh, TN, preferred_element_type=F32)
                    dq_ref[:, hh * HEAD_DIM:(hh + 1) * HEAD_DIM] = jnp.dot(
                        ds, kcat, preferred_element_type=F32) * (HEAD_DIM ** -0.5)
                    dk_cat = dk_cat + lax.dot_general(ds, qh, TN, preferred_element_type=F32)
                dk_cat = dk_cat * (HEAD_DIM ** -0.5)
                ko = kvh * HEAD_DIM
                vo = 128 + kvh * HEAD_DIM
                dkv_ref[:, ko:ko + HEAD_DIM] = carry_ref[:, ko:ko + HEAD_DIM] + dk_cat[:BLK]
                dkv_ref[:, vo:vo + HEAD_DIM] = carry_ref[:, vo:vo + HEAD_DIM] + dv_cat[:BLK]
                carry_ref[:, ko:ko + HEAD_DIM] = dk_cat[BLK:]
                carry_ref[:, vo:vo + HEAD_DIM] = dv_cat[BLK:]

        @pl.when(n == nb)
        def _():
            dkv_ref[...] = carry_ref[...]

    last = nb - 1
    res, xres = _call(
        body, name=name, grid=(nb + 1,),
        in_specs=[pl.BlockSpec((BLK, ATTN_W), lambda n: (jnp.minimum(n, last), 0)),
                  pl.BlockSpec((BLK, 256), lambda n: (jnp.minimum(n, last), kvb)),
                  pl.BlockSpec((BLK, 256), lambda n: (jnp.maximum(jnp.minimum(n, last) - 1, 0), kvb)),
                  pl.BlockSpec((1, N_Q), lambda n: (0, 0)),
                  pl.BlockSpec((BLK, ATTN_W), lambda n: (jnp.minimum(n, last), 0)),
                  pl.BlockSpec((BLK, N_Q), lambda n: (jnp.minimum(n, last), 0))],
        out_specs=[pl.BlockSpec((BLK, ATTN_W), lambda n: (jnp.minimum(n, last), 0)),
                   pl.BlockSpec((BLK, 256), lambda n: (jnp.maximum(n - 1, 0), 0)),
                   pl.BlockSpec((1, N_Q), lambda n: (0, 0))],
        out_shape=[jax.ShapeDtypeStruct((t, ATTN_W), F32), jax.ShapeDtypeStruct((t, 256), F32),
                   jax.ShapeDtypeStruct((1, N_Q), F32)],
        scratch_shapes=[pltpu.VMEM((BLK, 256), F32)],
        args=[proj, proj, proj, sinks, dattn, lse], semantics=("arbitrary",), exchanges=exchanges)
    return (*res, xres) if exchanges else res


def _cumsum_rows(x):
    n = x.shape[0]
    row = lax.broadcasted_iota(jnp.int32, x.shape, 0)
    s = 1
    while s < n:
        x = x + jnp.where(row >= s, pltpu.roll(x, s, axis=0), 0.0)
        s *= 2
    return x


def _rev_cumsum_rows(x):
    n = x.shape[0]
    row = lax.broadcasted_iota(jnp.int32, x.shape, 0)
    s = 1
    while s < n:
        x = x + jnp.where(row < n - s, pltpu.roll(x, n - s, axis=0), 0.0)
        s *= 2
    return x


def _lower_bound(lbl_ref):
    l0 = lbl_ref[0:1, :]
    l1 = lbl_ref[1:2, :]
    mx = jnp.maximum(l0, l1)
    e0 = jnp.exp(l0 - mx)
    e1 = jnp.exp(l1 - mx)
    return e0 / (e0 + e1)


def _hgrn_gates(z, lb):
    sg = _sigmoid(z)
    f = lb + (1.0 - lb) * sg
    return sg, f, jnp.log(f), 1.0 - f


def _sub_factors(b, i):
    rows = lax.broadcasted_iota(jnp.int32, (CHUNK, RNN_HD), 0)
    ref = b[SUB * i - 1:SUB * i, :]
    qfac = jnp.exp(b[SUB * i:SUB * (i + 1), :] - ref)
    kfac = jnp.where(rows < SUB * i, jnp.exp(jnp.minimum(ref - b, 0.0)), 0.0)
    return qfac, kfac


def _diag_decay(bi, s):
    trow = lax.broadcasted_iota(jnp.int32, (SUB, RNN_HD), 0)
    return jnp.where(trow >= s, jnp.exp(jnp.minimum(bi - bi[s:s + 1, :], 0.0)), 0.0)


def _hgrn_fwd(proj, lb_logits, norm_gain, *, tb, name, exchanges=()):
    t = proj.shape[0]
    ntb = t // tb
    nch = tb // CHUNK
    qb, fb, ib, gb = QR_COL // 128, FR_COL // 128, IR_COL // 128, GR_COL // 128

    def body(q_ref, f_ref, i_ref, g_ref, lbl_ref, gain_ref, o_ref, out_ref, s0_ref, st_ref, ob_ref):
        c = pl.program_id(1)

        @pl.when(c == 0)
        def _():
            st_ref[...] = jnp.zeros_like(st_ref)

        lb = _lower_bound(lbl_ref)
        gain = gain_ref[...]

        def chunk(ci, carry):
            r0 = pl.multiple_of(ci * CHUNK, CHUNK)
            rows = pl.ds(r0, CHUNK)
            _, _, lf, k = _hgrn_gates(f_ref[rows, :], lb)
            qr = q_ref[rows, :]
            q = qr * _sigmoid(qr)
            v = i_ref[rows, :]
            b = _cumsum_rows(lf)
            st = st_ref[...]
            s0_ref[ci] = st
            ob_ref[...] = lax.dot_general((q * jnp.exp(b)).astype(BF16), st.astype(BF16), NT,
                                          preferred_element_type=F32)
            vb = v.astype(BF16)
            for i in range(CHUNK // SUB):
                blk = slice(SUB * i, SUB * (i + 1))
                qi, ki, vi, bi = q[blk], k[blk], v[blk], b[blk]
                oi = ob_ref[blk, :]
                if i > 0:
                    qfac, kfac = _sub_factors(b, i)
                    att = lax.dot_general((qi * qfac).astype(BF16), (k * kfac).astype(BF16), NT,
                                          preferred_element_type=F32)
                    oi = oi + jnp.dot(att.astype(BF16), vb, preferred_element_type=F32)
                for s in range(SUB):
                    e = _diag_decay(bi, s)
                    a = jnp.sum(qi * ki[s:s + 1, :] * e, axis=1, keepdims=True)
                    oi = oi + a * vi[s:s + 1, :]
                ob_ref[blk, :] = oi
            blast = b[CHUNK - 1:CHUNK, :]
            khat = (k * jnp.exp(blast - b)).astype(BF16)
            st_ref[...] = st * jnp.exp(blast) + lax.dot_general(vb, khat, TN, preferred_element_type=F32)
            o = ob_ref[...]
            o_ref[rows, :] = o
            gr = g_ref[rows, :]
            out_ref[rows, :] = o * _rstd(o) * gain * (gr * _sigmoid(gr))
            return carry

        lax.fori_loop(0, nch, chunk, 0)

    def col(base):
        return pl.BlockSpec((tb, RNN_HD), lambda h, c: (c, base + h))

    res, xres = _call(
        body, name=name, grid=(N_RNN, ntb),
        in_specs=[col(qb), col(fb), col(ib), col(gb),
                  pl.BlockSpec((2, RNN_HD), lambda h, c: (0, h)), pl.BlockSpec((1, RNN_HD), lambda h, c: (0, 0))],
        out_specs=[pl.BlockSpec((tb, RNN_HD), lambda h, c: (c, h)), pl.BlockSpec((tb, RNN_HD), lambda h, c: (c, h)),
                   pl.BlockSpec((None, nch, RNN_HD, RNN_HD), lambda h, c: (h, c, 0, 0))],
        out_shape=[jax.ShapeDtypeStruct((t, RNN_W), F32), jax.ShapeDtypeStruct((t, RNN_W), F32),
                   jax.ShapeDtypeStruct((N_RNN, t // CHUNK, RNN_HD, RNN_HD), F32)],
        scratch_shapes=[pltpu.VMEM((RNN_HD, RNN_HD), F32), pltpu.VMEM((CHUNK, RNN_HD), F32)],
        args=[proj, proj, proj, proj, lb_logits, norm_gain],
        semantics=("parallel", "arbitrary"), exchanges=exchanges)
    return (*res, xres) if exchanges else res


def _hgrn_bwd(proj, lb_logits, norm_gain, o_pre, s0, dcat, *, tb, name, exchanges=()):
    t = proj.shape[0]
    ntb = t // tb
    nch = tb // CHUNK
    qb, fb, ib, gb = QR_COL // 128, FR_COL // 128, IR_COL // 128, GR_COL // 128
    nsub = CHUNK // SUB

    def body(q_ref, f_ref, i_ref, g_ref, lbl_ref, gain_ref, o_ref, s0_ref, dout_ref,
             dq_ref, df_ref, di_ref, dg_ref, dlb_ref, dgain_ref,
             dst_ref, dqa_ref, dka_ref, dva_ref):
        c = pl.program_id(1)

        @pl.when(c == 0)
        def _():
            dst_ref[...] = jnp.zeros_like(dst_ref)
            dlb_ref[...] = jnp.zeros_like(dlb_ref)
            dgain_ref[...] = jnp.zeros_like(dgain_ref)

        lb = _lower_bound(lbl_ref)
        gain = gain_ref[...]

        def chunk(cj, carry):
            ci = nch - 1 - cj
            r0 = pl.multiple_of(ci * CHUNK, CHUNK)
            rows = pl.ds(r0, CHUNK)
            sg, f, lf, k = _hgrn_gates(f_ref[rows, :], lb)
            qr = q_ref[rows, :]
            sq = _sigmoid(qr)
            q = qr * sq
            v = i_ref[rows, :]
            b = _cumsum_rows(lf)

            dout = dout_ref[rows, :]
            o = o_ref[rows, :]
            gr = g_ref[rows, :]
            sgg = _sigmoid(gr)
            gate = gr * sgg
            rs = _rstd(o)
            nrm = o * rs
            dg_ref[rows, :] = dout * nrm * gain * (sgg * (1.0 + gr * (1.0 - sgg)))
            dn = dout * gate
            dgain_ref[...] += jnp.sum(dn * nrm, axis=0, keepdims=True)
            tt = dn * gain
            do = rs * (tt - nrm * jnp.mean(tt * nrm, axis=-1, keepdims=True))

            dob = do.astype(BF16)
            vb = v.astype(BF16)
            eb = jnp.exp(b)
            blast = b[CHUNK - 1:CHUNK, :]
            ebl = jnp.exp(blast - b)
            dst = dst_ref[...]
            dstb = dst.astype(BF16)
            khat = (k * ebl).astype(BF16)
            s0 = s0_ref[ci]
            dqa_ref[...] = eb * jnp.dot(dob, s0.astype(BF16), preferred_element_type=F32)
            dk_state = ebl * jnp.dot(vb, dstb, preferred_element_type=F32)
            dka_ref[...] = dk_state
            d_blast = (jnp.sum(k * dk_state, axis=0, keepdims=True)
                       + jnp.exp(blast) * jnp.sum(dst * s0, axis=0, keepdims=True))
            dva_ref[...] = lax.dot_general(khat, dstb, NT, preferred_element_type=F32)
            dst_ref[...] = dst * jnp.exp(blast) + lax.dot_general(dob, (q * eb).astype(BF16), TN,
                                                                  preferred_element_type=F32)
            pm = lax.dot_general(dob, vb, NT, preferred_element_type=F32)
            for i in range(nsub):
                blk = slice(SUB * i, SUB * (i + 1))
                qi, ki, vi, bi, doi = q[blk], k[blk], v[blk], b[blk], do[blk]
                dqi = dqa_ref[blk, :]
                if i > 0:
                    qfac, kfac = _sub_factors(b, i)
                    qt = (qi * qfac).astype(BF16)
                    kt = (k * kfac).astype(BF16)
                    att = lax.dot_general(qt, kt, NT, preferred_element_type=F32).astype(BF16)
                    pmi = pm[blk, :].astype(BF16)
                    dva_ref[...] += lax.dot_general(att, doi.astype(BF16), TN, preferred_element_type=F32)
                    dqi = dqi + qfac * jnp.dot(pmi, kt, preferred_element_type=F32)
                    dka_ref[...] += kfac * lax.dot_general(pmi, qt, TN, preferred_element_type=F32)
                dki = dka_ref[blk, :]
                dvi = dva_ref[blk, :]
                trow = lax.broadcasted_iota(jnp.int32, (SUB, RNN_HD), 0)
                for s in range(SUB):
                    e = _diag_decay(bi, s)
                    ks = ki[s:s + 1, :]
                    vs = vi[s:s + 1, :]
                    a = jnp.sum(qi * ks * e, axis=1, keepdims=True)
                    p = jnp.sum(doi * vs, axis=1, keepdims=True)
                    dvs = jnp.sum(a * doi, axis=0, keepdims=True)
                    dks = jnp.sum(p * qi * e, axis=0, keepdims=True)
                    dqi = dqi + p * ks * e
                    dki = dki + jnp.where(trow == s, dks, 0.0)
                    dvi = dvi + jnp.where(trow == s, dvs, 0.0)
                dqa_ref[blk, :] = dqi
                dka_ref[blk, :] = dki
                dva_ref[blk, :] = dvi

            dq = dqa_ref[...]
            dk = dka_ref[...]
            lastrow = lax.broadcasted_iota(jnp.int32, (CHUNK, RNN_HD), 0) == CHUNK - 1
            dlf = _rev_cumsum_rows(q * dq - k * dk + jnp.where(lastrow, d_blast, 0.0))
            dff = dlf / f - dk
            df_ref[rows, :] = dff * (1.0 - lb) * sg * (1.0 - sg)
            dlb_ref[...] += jnp.sum(dff * (1.0 - sg), axis=0, keepdims=True)
            dq_ref[rows, :] = dq * (sq * (1.0 + qr * (1.0 - sq)))
            di_ref[rows, :] = dva_ref[...]
            return carry

        lax.fori_loop(0, nch, chunk, 0)

    def col(base):
        return pl.BlockSpec((tb, RNN_HD), lambda h, c: (ntb - 1 - c, base + h))

    outc = pl.BlockSpec((tb, RNN_HD), lambda h, c: (ntb - 1 - c, h))
    hb = ATTN_W // RNN_HD
    res, xres = _call(
        body, name=name, grid=(N_RNN, ntb),
        in_specs=[col(qb), col(fb), col(ib), col(gb),
                  pl.BlockSpec((2, RNN_HD), lambda h, c: (0, h)), pl.BlockSpec((1, RNN_HD), lambda h, c: (0, 0)),
                  outc,
                  pl.BlockSpec((None, nch, RNN_HD, RNN_HD), lambda h, c: (h, ntb - 1 - c, 0, 0)),
                  pl.BlockSpec((tb, RNN_HD), lambda h, c: (ntb - 1 - c, hb + h))],
        out_specs=[outc, outc, outc, outc,
                   pl.BlockSpec((1, RNN_HD), lambda h, c: (0, h)),
                   pl.BlockSpec((None, 1, RNN_HD), lambda h, c: (h, 0, 0))],
        out_shape=[jax.ShapeDtypeStruct((t, RNN_W), F32)] * 4
        + [jax.ShapeDtypeStruct((1, RNN_W), F32), jax.ShapeDtypeStruct((N_RNN, 1, RNN_HD), F32)],
        scratch_shapes=[pltpu.VMEM((RNN_HD, RNN_HD), F32),
                        pltpu.VMEM((CHUNK, RNN_HD), F32), pltpu.VMEM((CHUNK, RNN_HD), F32),
                        pltpu.VMEM((CHUNK, RNN_HD), F32)],
        args=[proj, proj, proj, proj, lb_logits, norm_gain, o_pre, s0, dcat],
        semantics=("parallel", "arbitrary"), exchanges=exchanges)
    return (*res, xres) if exchanges else res


def _all_gather_halves(shards, *, name):
    n = len(shards)

    def body(*refs):
        ins, outs = refs[:n], refs[n:2 * n]
        send_sems, recv_sems = refs[2 * n:]
        x, y, c = _place()
        sibling = (x, y, 1 - c)
        chips = [(1 - x, y), (x, 1 - y), (1 - x, 1 - y)]

        def copy(a, k, block, to, src=None):
            slot = outs[a].at[4 * block[0] + 2 * block[1] + block[2]]
            return pltpu.make_async_remote_copy(
                src_ref=slot if src is None else src, dst_ref=slot,
                send_sem=send_sems.at[a, k], recv_sem=recv_sems.at[a, k],
                device_id=to, device_id_type=MESH)

        first, passed = [], []
        for a in range(n):
            for j, chip in enumerate(chips):
                cp = copy(a, j, (x, y, c), (*chip, c), src=ins[a].at[c])
                cp.start()
                first.append(cp)
        for a in range(n):
            for j, chip in enumerate(chips):
                copy(a, j, (*chip, c), (x, y, c)).wait_recv()
                cp = copy(a, 3 + j, (*chip, c), sibling)
                cp.start()
                passed.append(cp)
        for a in range(n):
            for j, chip in enumerate(chips):
                copy(a, 3 + j, (*chip, 1 - c), (x, y, c)).wait_recv()
        for cp in first + passed:
            cp.wait_send()

    return pl.pallas_call(
        body, name=name,
        in_specs=[ANY] * n, out_specs=[ANY] * n,
        out_shape=[jax.ShapeDtypeStruct((8,) + s.shape[1:], s.dtype) for s in shards],
        scratch_shapes=[pltpu.SemaphoreType.DMA((n, 6)), pltpu.SemaphoreType.DMA((n, 6))],
    )(*shards)


def _row_tile(rows, cols, budget=1 << 20):
    tr = rows
    while tr * cols > budget and tr % 16 == 0:
        tr //= 2
    return tr


def _pair_sum(g, sib, where, *, name):
    _, _, rh, cols = g.shape
    tr = _row_tile(rh, cols)

    def body(w_ref, g_ref, s_ref, o_ref):
        o_ref[...] = (g_ref[...] + s_ref[...]).astype(BF16)

    return pl.pallas_call(
        body, name=name,
        grid_spec=pltpu.PrefetchScalarGridSpec(
            num_scalar_prefetch=1, grid=(4, rh // tr),
            in_specs=[pl.BlockSpec((None, None, tr, cols), lambda s, i, w: (s, w[1], i, 0)),
                      pl.BlockSpec((None, tr, cols), lambda s, i, w: (s, i, 0))],
            out_specs=pl.BlockSpec((None, tr, cols), lambda s, i, w: (s, i, 0))),
        out_shape=jax.ShapeDtypeStruct((4, rh, cols), BF16),
        compiler_params=_params(("parallel", "parallel")),
    )(where, g, sib)


def _final_half(g, sib, recv, where, *, name):
    _, _, rh, cols = g.shape
    tr = _row_tile(rh, cols)

    def body(w_ref, g_ref, s_ref, r_ref, o_ref):
        acc = g_ref[...] + s_ref[...]
        for j in range(3):
            acc = acc + r_ref[j].astype(F32)
        o_ref[...] = acc

    return pl.pallas_call(
        body, name=name,
        grid_spec=pltpu.PrefetchScalarGridSpec(
            num_scalar_prefetch=1, grid=(rh // tr,),
            in_specs=[pl.BlockSpec((None, None, tr, cols), lambda i, w: (w[0], w[1], i, 0)),
                      pl.BlockSpec((None, tr, cols), lambda i, w: (w[0], i, 0)),
                      pl.BlockSpec((3, tr, cols), lambda i, w: (0, i, 0))],
            out_specs=pl.BlockSpec((tr, cols), lambda i, w: (i, 0))),
        out_shape=jax.ShapeDtypeStruct((rh, cols), F32),
        compiler_params=_params(("parallel",)),
    )(where, g, sib, recv)


def _adamw_math(w, g, m, v):
    m = ADAM_B1 * m + (1.0 - ADAM_B1) * g
    v = ADAM_B2 * v + (1.0 - ADAM_B2) * (g * g)
    m_hat = m / (1.0 - ADAM_B1 ** ADAM_STEP)
    v_hat = v / (1.0 - ADAM_B2 ** ADAM_STEP)
    delta = -ADAM_LR * (m_hat / (jnp.sqrt(v_hat) + ADAM_EPS) + ADAM_WD * w)
    return delta, m, v


def _adamw(w, mine, theirs, m, v, where, *, name):
    rows, cols = w.shape
    tr = _row_tile(rows // 2, cols, budget=1 << 19)
    nh = rows // 2 // tr

    def body(wh_ref, w_ref, a_ref, b_ref, m_ref, v_ref, g_ref, d_ref, nm_ref, nv_ref):
        g = jnp.where(pl.program_id(0) // nh == wh_ref[1], a_ref[...], b_ref[...])
        d, nm, nv = _adamw_math(w_ref[...], g, m_ref[...], v_ref[...])
        g_ref[...] = g
        d_ref[...] = d
        nm_ref[...] = nm
        nv_ref[...] = nv

    blk = pl.BlockSpec((tr, cols), lambda i, wh: (i, 0))
    half = pl.BlockSpec((tr, cols), lambda i, wh: (i % nh, 0))
    return pl.pallas_call(
        body, name=name,
        grid_spec=pltpu.PrefetchScalarGridSpec(
            num_scalar_prefetch=1, grid=(rows // tr,),
            in_specs=[blk, half, half, blk, blk], out_specs=[blk] * 4),
        out_shape=[jax.ShapeDtypeStruct((rows, cols), F32)] * 4,
        compiler_params=_params(("parallel",)),
    )(where, w, mine, theirs, m, v)


SEG_LOSS = 0
SEG_SINK = 128
SEG_AGAIN = 256
SEG_L0 = SEG_AGAIN + ATTN_W
SEG_L1 = SEG_L0 + RNN_W
SEG_RGAIN = SEG_L1 + RNN_W
SEG_G = SEG_RGAIN + 128
N_PACK = SEG_G + 4 * D_MODEL


def _pack(sinks, again, l0, l1, rgain, gains, loss=None):
    z = lambda k: jnp.zeros((1, k), F32)
    first = z(128) if loss is None else loss
    return jnp.concatenate([first, sinks, z(128 - N_Q), again, l0, l1, rgain] + list(gains), axis=1)


def _small_reduce_adamw(part, w, m, v, *, name):
    def body(p_ref, w_ref, m_ref, v_ref, g_ref, d_ref, nm_ref, nv_ref, buf_ref, send_sems, recv_sems):
        x, y, c = _place()
        me = 4 * x + 2 * y + c
        copies = []
        for k in range(1, 8):
            dx, dy, dc = (k >> 2) & 1, (k >> 1) & 1, k & 1
            to = (x ^ dx, y ^ dy, c ^ dc)
            cp = pltpu.make_async_remote_copy(
                src_ref=p_ref, dst_ref=buf_ref.at[me],
                send_sem=send_sems.at[k - 1], recv_sem=recv_sems.at[k - 1],
                device_id=to, device_id_type=MESH)
            cp.start()
            copies.append(cp)
        buf_ref[me] = p_ref[...]
        for cp in copies:
            cp.wait()
        tot = buf_ref[0]
        for j in range(1, 8):
            tot = tot + buf_ref[j]
        g_ref[...] = tot
        l0 = w_ref[:, SEG_L0:SEG_L0 + RNN_W]
        l1 = w_ref[:, SEG_L1:SEG_L1 + RNN_W]
        mx = jnp.maximum(l0, l1)
        e0 = jnp.exp(l0 - mx)
        e1 = jnp.exp(l1 - mx)
        lb = e0 / (e0 + e1)
        gl0 = tot[:, SEG_L0:SEG_L0 + RNN_W] * lb * (1.0 - lb)
        g_ref[:, SEG_L0:SEG_L0 + RNN_W] = gl0
        g_ref[:, SEG_L1:SEG_L1 + RNN_W] = -gl0
        d, nm, nv = _adamw_math(w_ref[...], g_ref[...], m_ref[...], v_ref[...])
        d_ref[...] = d
        nm_ref[...] = nm
        nv_ref[...] = nv

    vm = pl.BlockSpec(memory_space=pltpu.VMEM)
    return pl.pallas_call(
        body, name=name,
        in_specs=[vm] * 4, out_specs=[vm] * 4,
        out_shape=[jax.ShapeDtypeStruct((1, N_PACK), F32)] * 4,
        scratch_shapes=[pltpu.VMEM((8, 1, N_PACK), F32), pltpu.SemaphoreType.DMA((7,)),
                        pltpu.SemaphoreType.DMA((7,))],
    )(part, w, m, v)


def _layer_grads(xs, tgt, halves, where, sinks, again, lb_logits, rgain,
                 g_mix_pre, g_mix_post, g_mlp_pre, g_mlp_post):
    tm = 512
    h_in, h_out, h_up, h_dn = halves

    def whole(buf, own):
        return lax.dynamic_update_slice(buf, own, (2 * where[0], 0, 0))

    shard = IN_W // N_CHIPS
    w_in4 = whole(_all_gather_halves([h_in], name="gather_w_in")[0], h_in).reshape(N_CHIPS, D_MODEL, shard)
    w_in = w_in4.transpose(1, 0, 2).reshape(D_MODEL, IN_W)
    w_in_t = w_in4.transpose(0, 2, 1).reshape(IN_W, D_MODEL)
    h1 = _rms_cast(xs, g_mix_pre, tm=tm, name="h1_norm")
    proj, ((b_out,),) = _mm(h1, w_in, tm=tm, tn=768, tk=D_MODEL, out_dtype=F32, name="in_proj",
                            exchanges=[_x_gather_ici([h_out])])
    attn, lse, ((b_up,), (b_out,)) = _swa_fwd(proj, sinks, name="swa_fwd",
                                              exchanges=[_x_gather_ici([h_up]), _x_gather_d2d([b_out])])
    w_out = whole(b_out, h_out).reshape(D_MODEL, D_MODEL)
    o_pre, rnn, s0, ((b_dn,), (b_up,)) = _hgrn_fwd(proj, lb_logits, rgain, tb=512, name="hgrn_fwd",
                                                   exchanges=[_x_gather_ici([h_dn]), _x_gather_d2d([b_up])])
    w_up4 = whole(b_up, h_up).reshape(N_CHIPS, D_MODEL, D_FF // N_CHIPS)
    w_up = w_up4.transpose(1, 0, 2).reshape(D_MODEL, D_FF)
    w_up_t = w_up4.transpose(0, 2, 1).reshape(D_FF, D_MODEL)
    cat = _mix_cat(attn, rnn, again, tm=tm, name="mix_cat")
    mixed, ((b_dn,),) = _mm(cat, w_out, tm=tm, tn=1024, tk=D_MODEL, out_dtype=F32, name="out_proj",
                            exchanges=[_x_gather_d2d([b_dn])])
    w_dn = whole(b_dn, h_dn).reshape(D_FF, D_MODEL)
    x1, h2 = _post_norm_res(mixed, g_mix_post, xs, g_mlp_pre, tm=256, name="mix_post")
    u = _mm(h2, w_up, tm=tm, tn=1024, tk=D_MODEL, out_dtype=BF16, relu=True, name="mlp_up")
    yv = _mm(u, w_dn, tm=tm, tn=1024, tk=2048, out_dtype=F32, a_square=True, name="mlp_down")
    dy, dx2, loss_row, dg_mlp_post = _loss_head(yv, g_mlp_post, x1, tgt, tm=256, name="loss_head")

    def halved(g):
        return g.reshape(N_CHIPS, 2, g.shape[1] // 2, g.shape[2])
    du = _mm(dy, w_dn.T, tm=tm, tn=1024, tk=D_MODEL, out_dtype=BF16, mul2=u, name="mlp_down_bwd")
    g_dn = halved(_mm_tn(u, dy, tm=1024, tn=1024, tt=512, a_square=True, name="w_down_grad")
                  .reshape(N_CHIPS, D_FF // N_CHIPS, D_MODEL))
    d_w_up, ((sib_dn,),) = _mm_tn(h2, du, tm=1024, tn=1024, tt=512, n_split=N_CHIPS, name="w_up_grad",
                                  exchanges=[_x_pair([g_dn])])
    g_up = halved(d_w_up)
    wire_dn = _pair_sum(g_dn, sib_dn, where, name="pair_sum_w_down")
    dh2, ((recv_dn,), (sib_up,)) = _mm(du, w_up_t, tm=tm, tn=1024, tk=2048, out_dtype=F32, name="mlp_up_bwd",
                                       exchanges=[_x_chip([wire_dn]), _x_pair([g_up])])
    wire_up = _pair_sum(g_up, sib_up, where, name="pair_sum_w_up")
    fin_dn = _final_half(g_dn, sib_dn, recv_dn, where, name="final_half_w_down")
    dx1, dg_mlp_pre = _rms_bwd(dh2, x1, g_mlp_pre, dx2, tm=256, out_dtype=F32, name="mlp_pre_bwd")
    dmixed, dg_mix_post = _rms_bwd(dx1, mixed, g_mix_post, None, tm=256, out_dtype=BF16, name="mix_post_bwd")
    d_w_out, ((oth_dn,),) = _mm_tn(cat, dmixed, tm=1024, tn=1024, tt=512, name="w_out_grad",
                                   exchanges=[_x_share([fin_dn])])
    g_out = halved(d_w_out.reshape(N_CHIPS, D_MODEL // N_CHIPS, D_MODEL))
    dcat, ((sib_out,),) = _mm(dmixed, w_out.T, tm=tm, tn=1024, tk=D_MODEL, out_dtype=F32, name="out_proj_bwd",
                              exchanges=[_x_pair([g_out])])
    wire_out = _pair_sum(g_out, sib_out, where, name="pair_sum_w_out")
    dattn, dg_again = _rms_bwd(dcat, attn, again, None, tm=tm, out_dtype=F32, name="attn_norm_bwd")
    dq_a, dkv, dsinks, ((recv_out,),) = _swa_bwd(proj, sinks, dattn, lse, name="swa_bwd",
                                                 exchanges=[_x_chip([wire_out])])
    dq_r, df_r, di_r, dg_r, dlb, dgain_h, ((recv_up,),) = _hgrn_bwd(
        proj, lb_logits, rgain, o_pre, s0, dcat, tb=512, name="hgrn_bwd", exchanges=[_x_chip([wire_up])])
    fin_up = _final_half(g_up, sib_up, recv_up, where, name="final_half_w_up")
    fin_out = _final_half(g_out, sib_out, recv_out, where, name="final_half_w_out")
    dproj = jnp.concatenate([dq_a, dkv, dq_r, df_r, di_r, dg_r], axis=1).astype(BF16)
    d_w_in, ((oth_up, oth_out),) = _mm_tn(h1, dproj, tm=1024, tn=768, tt=512, name="w_in_grad",
                                          exchanges=[_x_share([fin_up, fin_out])])
    g_in = halved(d_w_in[0].reshape(D_MODEL, N_CHIPS, shard).transpose(1, 0, 2))
    dh1, ((sib_in,),) = _mm(dproj, w_in_t, tm=tm, tn=1024, tk=768, out_dtype=F32, name="in_proj_bwd",
                            exchanges=[_x_pair([g_in])])
    wire_in = _pair_sum(g_in, sib_in, where, name="pair_sum_w_in")
    gx, dg_mix_pre, ((recv_in,),) = _rms_bwd(dh1, xs, g_mix_pre, dx1, tm=256, out_dtype=F32, name="mix_pre_bwd",
                                             exchanges=[_x_chip([wire_in])])
    fin_in = _final_half(g_in, sib_in, recv_in, where, name="final_half_w_in")
    (oth_in,) = _run_exchange(_x_share([fin_in]), name="share_w_in")

    big = [(fin_in, oth_in), (fin_out, oth_out), (fin_up, oth_up), (fin_dn, oth_dn)]
    drgain = jnp.sum(dgain_h, axis=0)
    small = _pack(dsinks, dg_again, dlb, jnp.zeros_like(dlb), drgain,
                  [dg_mix_pre, dg_mix_post, dg_mlp_pre, dg_mlp_post], loss=loss_row)
    return gx, big, small


def kernel(x, w_in, attn_sinks, attn_out_gain, rnn_lb_logits, rnn_norm_gain, w_out, mix_pre_gain, mix_post_gain, mlp_pre_gain, mlp_post_gain, w_up, w_down, loss_target, m_w_in, m_attn_sinks, m_attn_out_gain, m_rnn_lb_logits, m_rnn_norm_gain, m_w_out, m_mix_pre_gain, m_mix_post_gain, m_mlp_pre_gain, m_mlp_post_gain, m_w_up, m_w_down, v_w_in, v_attn_sinks, v_attn_out_gain, v_rnn_lb_logits, v_rnn_norm_gain, v_w_out, v_mix_pre_gain, v_mix_post_gain, v_mlp_pre_gain, v_mlp_post_gain, v_w_up, v_w_down):
    ax, ay, ac = _place()
    where = jnp.stack([2 * ax + ay, ac]).astype(jnp.int32)
    big_w = [w_in[0], w_out[0], w_up[0], w_down[0]]
    big_m = [m_w_in[0], m_w_out[0], m_w_up[0], m_w_down[0]]
    big_v = [v_w_in[0], v_w_out[0], v_w_up[0], v_w_down[0]]

    halves = [w.astype(BF16).reshape(2, w.shape[0] // 2, w.shape[1]) for w in big_w]
    gx, big_g, small_part = _layer_grads(
        x[0], loss_target[0], halves, where, attn_sinks, attn_out_gain, rnn_lb_logits, rnn_norm_gain,
        mix_pre_gain, mix_post_gain, mlp_pre_gain, mlp_post_gain)

    names = ["w_in", "w_out", "w_up", "w_down"]
    grads, deltas, new_m, new_v = [], [], [], []
    for (f, o), w, m, v, nm in zip(big_g, big_w, big_m, big_v, names):
        g, d, nm_, nv_ = _adamw(w, f, o, m, v, where, name="adamw_" + nm)
        grads.append(g[None])
        deltas.append(d[None])
        new_m.append(nm_[None])
        new_v.append(nv_[None])

    def pack_params(sinks, again, logits, rgain, gains):
        return _pack(sinks, again, logits[0:1], logits[1:2], rgain, gains)

    pw = pack_params(attn_sinks, attn_out_gain, rnn_lb_logits, rnn_norm_gain,
                     [mix_pre_gain, mix_post_gain, mlp_pre_gain, mlp_post_gain])
    pm = pack_params(m_attn_sinks, m_attn_out_gain, m_rnn_lb_logits, m_rnn_norm_gain,
                     [m_mix_pre_gain, m_mix_post_gain, m_mlp_pre_gain, m_mlp_post_gain])
    pv = pack_params(v_attn_sinks, v_attn_out_gain, v_rnn_lb_logits, v_rnn_norm_gain,
                     [v_mix_pre_gain, v_mix_post_gain, v_mlp_pre_gain, v_mlp_post_gain])
    packs = _small_reduce_adamw(small_part, pw, pm, pv, name="small_reduce_adamw")

    def unpack(p):
        seg = lambda o, k: p[:, o:o + k]
        logits = jnp.concatenate([seg(SEG_L0, RNN_W), seg(SEG_L1, RNN_W)], axis=0)
        gains = [seg(SEG_G + i * D_MODEL, D_MODEL) for i in range(4)]
        return dict(sinks=seg(SEG_SINK, N_Q), again=seg(SEG_AGAIN, ATTN_W), logits=logits,
                    rgain=seg(SEG_RGAIN, RNN_HD), gains=gains)

    def order(small, big):
        return [big[0], small["sinks"], small["again"], small["logits"], small["rgain"], big[1],
                *small["gains"], big[2], big[3]]

    loss = packs[0][0, 0]
    outs = [loss, gx[None]]
    for p, b in zip(packs, [grads, deltas, new_m, new_v]):
        outs += order(unpack(p), b)
    return tuple(outs)
```

```python
import functools

import jax
import jax.numpy as jnp
from jax import lax
from jax.experimental import pallas as pl
from jax.experimental.pallas import tpu as pltpu

F32 = jnp.float32
BF16 = jnp.bfloat16
MESH = pl.DeviceIdType.MESH

EPS = 1e-6
D_MODEL = 2048
ATTN_W = 1024
HEAD_DIM = 64
N_Q = 16
N_KV = 2
GROUP = 8
BLK = 128
RNN_W = 1024
RNN_HD = 128
N_RNN = 8
CHUNK = 64
SUB = 16
D_FF = 8192
IN_W = 5376
N_CHIPS = 4
KV_COL = ATTN_W
QR_COL = ATTN_W + 2 * 128
FR_COL = QR_COL + RNN_W
IR_COL = FR_COL + RNN_W
GR_COL = IR_COL + RNN_W

ADAM_LR = 0.001
ADAM_B1 = 0.9
ADAM_B2 = 0.999
ADAM_EPS = 1e-08
ADAM_WD = 0.01
ADAM_STEP = 10

VMEM_LIMIT = 48 * 1024 * 1024

NT = (((1,), (1,)), ((), ()))
TN = (((0,), (0,)), ((), ()))


def _params(sem=None):
    return pltpu.CompilerParams(dimension_semantics=sem, vmem_limit_bytes=VMEM_LIMIT)


def _sigmoid(x):
    return 1.0 / (1.0 + jnp.exp(-x))


ANY = pl.BlockSpec(memory_space=pl.ANY)


def _place():
    return lax.axis_index("x"), lax.axis_index("y"), lax.axis_index("c")


def _other_chips(x, y):
    return [(1 - x, y), (x, 1 - y), (1 - x, 1 - y)]


class _Exchange:
    def __init__(self, srcs, outs, ncopy, build, aliases=None):
        self.srcs, self.outs, self.ncopy, self.build = list(srcs), list(outs), ncopy, build
        self.aliases = aliases or {}


def _remote(src, dst, send_sems, recv_sems, k, to):
    return pltpu.make_async_remote_copy(src_ref=src, dst_ref=dst, send_sem=send_sems.at[k],
                                        recv_sem=recv_sems.at[k], device_id=to, device_id_type=MESH)


def _call(body, *, name, grid, in_specs, out_specs, out_shape, args, scratch_shapes=(), semantics=None,
          exchanges=()):
    in_specs, out_specs, out_shape = list(in_specs), list(out_specs), list(out_shape)
    scratch_shapes = list(scratch_shapes)
    ni, no, ns = len(in_specs), len(out_specs), len(scratch_shapes)
    xsrc = [s for x in exchanges for s in x.srcs]
    xout = [o for x in exchanges for o in x.outs]
    nxi, nxo = len(xsrc), len(xout)
    aliases = {}
    a0 = b0 = 0
    for x in exchanges:
        for si, oi in x.aliases.items():
            aliases[ni + a0 + si] = no + b0 + oi
        a0 += len(x.srcs)
        b0 += len(x.outs)
    sems = []
    for x in exchanges:
        sems += [pltpu.SemaphoreType.DMA((x.ncopy,)), pltpu.SemaphoreType.DMA((x.ncopy,))]

    def wrapped(*refs):
        ins, xi = refs[:ni], refs[ni:ni + nxi]
        outs, xo = refs[ni + nxi:ni + nxi + no], refs[ni + nxi + no:ni + nxi + no + nxo]
        rest = refs[ni + nxi + no + nxo:]
        scr, sm = rest[:ns], rest[ns:]

        def copies():
            cps = []
            a = b = 0
            for k, x in enumerate(exchanges):
                cps += x.build(xi[a:a + len(x.srcs)], xo[b:b + len(x.outs)], sm[2 * k], sm[2 * k + 1])
                a += len(x.srcs)
                b += len(x.outs)
            return cps

        def start():
            for cp in copies():
                cp.start()

        def wait():
            for cp in copies():
                cp.wait()

        if not exchanges:
            body(*ins, *outs, *scr)
        elif not grid:
            start()
            body(*ins, *outs, *scr)
            wait()
        else:
            first = last = None
            for ax, g in enumerate(grid):
                f = pl.program_id(ax) == 0
                l = pl.program_id(ax) == g - 1
                first = f if first is None else first & f
                last = l if last is None else last & l
            pl.when(first)(start)
            body(*ins, *outs, *scr)
            pl.when(last)(wait)

    if exchanges and semantics is not None:
        semantics = ("arbitrary",) * len(grid)
    kwargs = dict(grid=grid) if grid else {}
    res = pl.pallas_call(
        wrapped, name=name,
        in_specs=in_specs + [ANY] * nxi, out_specs=out_specs + [ANY] * nxo,
        out_shape=out_shape + xout, scratch_shapes=scratch_shapes + sems,
        input_output_aliases=aliases,
        compiler_params=_params(semantics), **kwargs,
    )(*args, *xsrc)
    res = list(res)
    mine, theirs = res[:no], res[no:]
    per = []
    b = 0
    for x in exchanges:
        per.append(theirs[b:b + len(x.outs)])
        b += len(x.outs)
    return mine, per


def _run_exchange(x, *, name):
    return _call(lambda: None, name=name, grid=(), in_specs=[], out_specs=[], out_shape=[], args=[],
                 exchanges=[x])[1][0]


def _x_gather_ici(halves):
    n = len(halves)

    def build(srcs, outs, ss, rs):
        x, y, c = _place()
        return [_remote(srcs[a].at[c], outs[a].at[4 * x + 2 * y + c], ss, rs, 3 * a + j, (px, py, c))
                for a in range(n) for j, (px, py) in enumerate(_other_chips(x, y))]

    outs = [jax.ShapeDtypeStruct((8,) + h.shape[1:], h.dtype) for h in halves]
    return _Exchange(halves, outs, 3 * n, build)


def _x_gather_d2d(bufs):
    n = len(bufs)

    def build(srcs, outs, ss, rs):
        x, y, c = _place()
        cps = []
        for a in range(n):
            for j, (px, py) in enumerate(_other_chips(x, y)):
                slot = 4 * px + 2 * py + c
                cps.append(_remote(srcs[a].at[slot], outs[a].at[slot], ss, rs, 3 * a + j, (x, y, 1 - c)))
        return cps

    outs = [jax.ShapeDtypeStruct(b.shape, b.dtype) for b in bufs]
    return _Exchange(bufs, outs, 3 * n, build, aliases={a: a for a in range(n)})


def _x_pair(grads):
    n = len(grads)

    def build(srcs, outs, ss, rs):
        x, y, c = _place()
        return [_remote(srcs[a].at[:, 1 - c], outs[a], ss, rs, a, (x, y, 1 - c)) for a in range(n)]

    outs = [jax.ShapeDtypeStruct((4,) + g.shape[2:], g.dtype) for g in grads]
    return _Exchange(grads, outs, n, build)


def _x_chip(wires):
    n = len(wires)

    def build(srcs, outs, ss, rs):
        x, y, c = _place()
        return [_remote(srcs[a].at[2 * px + py], outs[a].at[j], ss, rs, 3 * a + j, (px, py, c))
                for a in range(n) for j, (px, py) in enumerate(_other_chips(x, y))]

    outs = [jax.ShapeDtypeStruct((3,) + w.shape[1:], w.dtype) for w in wires]
    return _Exchange(wires, outs, 3 * n, build)


def _x_share(halves):
    n = len(halves)

    def build(srcs, outs, ss, rs):
        x, y, c = _place()
        return [_remote(srcs[a], outs[a], ss, rs, a, (x, y, 1 - c)) for a in range(n)]

    outs = [jax.ShapeDtypeStruct(h.shape, h.dtype) for h in halves]
    return _Exchange(halves, outs, n, build)


def _mm(a, w, *, tm, tn, tk, out_dtype, name, a_square=False, relu=False, mul2=None, exchanges=()):
    m, k = a.shape
    _, n = w.shape
    nk = k // tk
    assert m % tm == 0 and n % tn == 0 and k % tk == 0

    def body(*refs):
        if mul2 is not None:
            a_ref, w_ref, e_ref, o_ref, acc_ref = refs
        else:
            a_ref, w_ref, o_ref, acc_ref = refs
            e_ref = None
        kk = pl.program_id(2)
        av = a_ref[...]
        if a_square:
            af = av.astype(F32)
            av = (af * af).astype(BF16)
        part = jnp.dot(av, w_ref[...], preferred_element_type=F32)

        def finish(r):
            if relu:
                r = jnp.maximum(r, 0.0)
            if e_ref is not None:
                r = 2.0 * e_ref[...].astype(F32) * r
            o_ref[...] = r.astype(out_dtype)

        if nk == 1:
            finish(part)
        else:
            @pl.when(kk == 0)
            def _():
                acc_ref[...] = part

            @pl.when(kk > 0)
            def _():
                acc_ref[...] += part

            @pl.when(kk == nk - 1)
            def _():
                finish(acc_ref[...])

    in_specs = [pl.BlockSpec((tm, tk), lambda i, j, kk: (i, kk)),
                pl.BlockSpec((tk, tn), lambda i, j, kk: (kk, j))]
    args = [a, w]
    if mul2 is not None:
        in_specs.append(pl.BlockSpec((tm, tn), lambda i, j, kk: (i, j)))
        args.append(mul2)
    acc_shape = (tm, tn) if nk > 1 else (8, 128)
    (out,), per = _call(
        body, name=name, grid=(m // tm, n // tn, nk),
        in_specs=in_specs, out_specs=[pl.BlockSpec((tm, tn), lambda i, j, kk: (i, j))],
        out_shape=[jax.ShapeDtypeStruct((m, n), out_dtype)], args=args,
        scratch_shapes=[pltpu.VMEM(acc_shape, F32)],
        semantics=("parallel", "parallel", "arbitrary"), exchanges=exchanges)
    return (out, per) if exchanges else out


def _mm_tn(a, b, *, tm, tn, tt, name, a_square=False, n_split=1, exchanges=()):
    t, m = a.shape
    _, n = b.shape
    assert t % tt == 0 and m % tm == 0 and n % tn == 0 and (n // n_split) % tn == 0
    per = n // n_split // tn

    def body(a_ref, b_ref, o_ref):
        ti = pl.program_id(2)
        av = a_ref[...]
        if a_square:
            af = av.astype(F32)
            av = (af * af).astype(BF16)
        part = lax.dot_general(av, b_ref[...], TN, preferred_element_type=F32)

        @pl.when(ti == 0)
        def _():
            o_ref[...] = part

        @pl.when(ti > 0)
        def _():
            o_ref[...] += part

    (out,), xres = _call(
        body, name=name, grid=(m // tm, n // tn, t // tt),
        in_specs=[pl.BlockSpec((tt, tm), lambda i, j, ti: (ti, i)),
                  pl.BlockSpec((tt, tn), lambda i, j, ti: (ti, j))],
        out_specs=[pl.BlockSpec((None, tm, tn), lambda i, j, ti: (j // per, i, j % per))],
        out_shape=[jax.ShapeDtypeStruct((n_split, m, n // n_split), F32)], args=[a, b],
        semantics=("parallel", "parallel", "arbitrary"), exchanges=exchanges)
    return (out, xres) if exchanges else out


def _rstd(x):
    return lax.rsqrt(jnp.mean(x * x, axis=-1, keepdims=True) + EPS)


def _rms_cast(x, g, *, tm, name):
    t, d = x.shape

    def body(x_ref, g_ref, o_ref):
        xv = x_ref[...]
        o_ref[...] = (xv * _rstd(xv) * g_ref[...]).astype(BF16)

    return pl.pallas_call(
        body, name=name, grid=(t // tm,),
        in_specs=[pl.BlockSpec((tm, d), lambda i: (i, 0)), pl.BlockSpec((1, d), lambda i: (0, 0))],
        out_specs=pl.BlockSpec((tm, d), lambda i: (i, 0)),
        out_shape=jax.ShapeDtypeStruct((t, d), BF16),
        compiler_params=_params(("parallel",)),
    )(x, g)


def _mix_cat(attn, rnn, gain, *, tm, name):
    t = attn.shape[0]

    def body(a_ref, r_ref, g_ref, o_ref):
        av = a_ref[...]
        o_ref[:, :ATTN_W] = (av * _rstd(av) * g_ref[...]).astype(BF16)
        o_ref[:, ATTN_W:] = r_ref[...].astype(BF16)

    return pl.pallas_call(
        body, name=name, grid=(t // tm,),
        in_specs=[pl.BlockSpec((tm, ATTN_W), lambda i: (i, 0)), pl.BlockSpec((tm, RNN_W), lambda i: (i, 0)),
                  pl.BlockSpec((1, ATTN_W), lambda i: (0, 0))],
        out_specs=pl.BlockSpec((tm, D_MODEL), lambda i: (i, 0)),
        out_shape=jax.ShapeDtypeStruct((t, D_MODEL), BF16),
        compiler_params=_params(("parallel",)),
    )(attn, rnn, gain)


def _post_norm_res(mixed, g_post, res, g_next, *, tm, name):
    t, d = mixed.shape

    def body(m_ref, gp_ref, r_ref, gn_ref, x1_ref, h2_ref):
        mv = m_ref[...]
        x1 = r_ref[...] + mv * _rstd(mv) * gp_ref[...]
        x1_ref[...] = x1
        h2_ref[...] = (x1 * _rstd(x1) * gn_ref[...]).astype(BF16)

    row = pl.BlockSpec((tm, d), lambda i: (i, 0))
    vec = pl.BlockSpec((1, d), lambda i: (0, 0))
    return pl.pallas_call(
        body, name=name, grid=(t // tm,),
        in_specs=[row, vec, row, vec], out_specs=[row, row],
        out_shape=[jax.ShapeDtypeStruct((t, d), F32), jax.ShapeDtypeStruct((t, d), BF16)],
        compiler_params=_params(("parallel",)),
    )(mixed, g_post, res, g_next)


def _rms_bwd(dyn, xin, g, res, *, tm, out_dtype, name, col_block=0, exchanges=()):
    t, d = xin.shape

    def body(*refs):
        if res is not None:
            dy_ref, x_ref, g_ref, r_ref, dx_ref, dg_ref = refs
        else:
            dy_ref, x_ref, g_ref, dx_ref, dg_ref = refs
        i = pl.program_id(0)
        xv = x_ref[...]
        dy = dy_ref[...].astype(F32)
        r = _rstd(xv)
        xh = xv * r
        part = jnp.sum(dy * xh, axis=0, keepdims=True)

        @pl.when(i == 0)
        def _():
            dg_ref[...] = part

        @pl.when(i > 0)
        def _():
            dg_ref[...] += part

        tt = dy * g_ref[...]
        dx = r * (tt - xh * jnp.mean(tt * xh, axis=-1, keepdims=True))
        if res is not None:
            dx = dx + r_ref[...]
        dx_ref[...] = dx.astype(out_dtype)

    row = pl.BlockSpec((tm, d), lambda i: (i, 0))
    vec = pl.BlockSpec((1, d), lambda i: (0, 0))
    in_specs = [pl.BlockSpec((tm, d), lambda i: (i, col_block)), row, vec]
    args = [dyn, xin, g]
    if res is not None:
        in_specs.append(row)
        args.append(res)
    res, xres = _call(
        body, name=name, grid=(t // tm,),
        in_specs=in_specs, out_specs=[row, vec],
        out_shape=[jax.ShapeDtypeStruct((t, d), out_dtype), jax.ShapeDtypeStruct((1, d), F32)], args=args,
        semantics=("arbitrary",), exchanges=exchanges)
    return (*res, xres) if exchanges else res


def _loss_head(y, g_post, x1, target, *, tm, name):
    t, d = y.shape

    def body(y_ref, g_ref, x1_ref, t_ref, dy_ref, dx2_ref, loss_ref, dg_ref):
        i = pl.program_id(0)
        yv = y_ref[...]
        r = _rstd(yv)
        yh = yv * r
        gv = g_ref[...]
        err = x1_ref[...] + yh * gv - t_ref[...]
        lpart = 0.5 * jnp.sum(jnp.mean(err * err, axis=-1, keepdims=True), axis=0, keepdims=True)
        dx2 = err * (1.0 / d)
        dgp = jnp.sum(dx2 * yh, axis=0, keepdims=True)
        lane = lax.broadcasted_iota(jnp.int32, (1, 128), 1)
        lrow = jnp.where(lane == 0, lpart, 0.0)

        @pl.when(i == 0)
        def _():
            dg_ref[...] = dgp
            loss_ref[...] = lrow

        @pl.when(i > 0)
        def _():
            dg_ref[...] += dgp
            loss_ref[...] += lrow

        tt = dx2 * gv
        dy_ref[...] = (r * (tt - yh * jnp.mean(tt * yh, axis=-1, keepdims=True))).astype(BF16)
        dx2_ref[...] = dx2

    row = pl.BlockSpec((tm, d), lambda i: (i, 0))
    vec = pl.BlockSpec((1, d), lambda i: (0, 0))
    return pl.pallas_call(
        body, name=name, grid=(t // tm,),
        in_specs=[row, vec, row, row],
        out_specs=[row, row, pl.BlockSpec((1, 128), lambda i: (0, 0)), vec],
        out_shape=[jax.ShapeDtypeStruct((t, d), BF16), jax.ShapeDtypeStruct((t, d), F32),
                   jax.ShapeDtypeStruct((1, 128), F32), jax.ShapeDtypeStruct((1, d), F32)],
        compiler_params=_params(("arbitrary",)),
    )(y, g_post, x1, target)


def _alibi_slope(h):
    return 2.0 ** (-8.0 * (h + 1) / N_Q)


def _swa_mask(n):
    row = lax.broadcasted_iota(jnp.int32, (BLK, 2 * BLK), 0)
    col = lax.broadcasted_iota(jnp.int32, (BLK, 2 * BLK), 1)
    dist = row + BLK - col
    valid = (dist >= 0) & (dist < BLK) & ((col >= BLK) | (n > 0))
    return valid, dist.astype(F32)


def _swa_scores(q_ref, kcat, hh, valid, distf):
    qh = q_ref[:, hh * HEAD_DIM:(hh + 1) * HEAD_DIM].astype(BF16)
    s = lax.dot_general(qh, kcat, NT, preferred_element_type=F32) * (HEAD_DIM ** -0.5)
    s = s - _alibi_slope(hh) * distf
    return qh, jnp.where(valid, s, -1e30)


def _kv_cat(kvp_ref, kvc_ref, off):
    return jnp.concatenate([kvp_ref[:, off:off + HEAD_DIM], kvc_ref[:, off:off + HEAD_DIM]], axis=0).astype(BF16)


def _swa_fwd(proj, sinks, *, name, exchanges=()):
    t = proj.shape[0]
    nb = t // BLK
    kvb = KV_COL // (2 * 128)

    def body(q_ref, kvc_ref, kvp_ref, sink_ref, o_ref, lse_ref):
        n = pl.program_id(0)
        valid, distf = _swa_mask(n)
        for kvh in range(N_KV):
            kcat = _kv_cat(kvp_ref, kvc_ref, kvh * HEAD_DIM)
            vcat = _kv_cat(kvp_ref, kvc_ref, 128 + kvh * HEAD_DIM)
            for gi in range(GROUP):
                hh = kvh * GROUP + gi
                _, s = _swa_scores(q_ref, kcat, hh, valid, distf)
                sink = sink_ref[0:1, hh:hh + 1]
                mx = jnp.maximum(jnp.max(s, axis=1, keepdims=True), sink)
                p = jnp.exp(s - mx)
                l = jnp.sum(p, axis=1, keepdims=True) + jnp.exp(sink - mx)
                probs = p / l
                o_ref[:, hh * HEAD_DIM:(hh + 1) * HEAD_DIM] = jnp.dot(
                    probs.astype(BF16), vcat, preferred_element_type=F32)
                lse_ref[:, hh:hh + 1] = mx + jnp.log(l)

    res, xres = _call(
        body, name=name, grid=(nb,),
        in_specs=[pl.BlockSpec((BLK, ATTN_W), lambda n: (n, 0)),
                  pl.BlockSpec((BLK, 256), lambda n: (n, kvb)),
                  pl.BlockSpec((BLK, 256), lambda n: (jnp.maximum(n - 1, 0), kvb)),
                  pl.BlockSpec((1, N_Q), lambda n: (0, 0))],
        out_specs=[pl.BlockSpec((BLK, ATTN_W), lambda n: (n, 0)), pl.BlockSpec((BLK, N_Q), lambda n: (n, 0))],
        out_shape=[jax.ShapeDtypeStruct((t, ATTN_W), F32), jax.ShapeDtypeStruct((t, N_Q), F32)],
        args=[proj, proj, proj, sinks], semantics=("parallel",), exchanges=exchanges)
    return (*res, xres) if exchanges else res


def _swa_bwd(proj, sinks, dattn, lse, *, name, exchanges=()):
    t = proj.shape[0]
    nb = t // BLK
    kvb = KV_COL // (2 * 128)

    def body(q_ref, kvc_ref, kvp_ref, sink_ref, do_ref, lse_ref, dq_ref, dkv_ref, dsink_ref, carry_ref):
        n = pl.program_id(0)

        @pl.when(n == 0)
        def _():
            dsink_ref[...] = jnp.zeros_like(dsink_ref)
            carry_ref[...] = jnp.zeros_like(carry_ref)

        @pl.when(n < nb)
        def _():
            valid, distf = _swa_mask(n)
            for kvh in range(N_KV):
                kcat = _kv_cat(kvp_ref, kvc_ref, kvh * HEAD_DIM)
                vcat = _kv_cat(kvp_ref, kvc_ref, 128 + kvh * HEAD_DIM)
                dk_cat = jnp.zeros((2 * BLK, HEAD_DIM), F32)
                dv_cat = jnp.zeros((2 * BLK, HEAD_DIM), F32)
                for gi in range(GROUP):
                    hh = kvh * GROUP + gi
                    qh, s = _swa_scores(q_ref, kcat, hh, valid, distf)
                    lse_h = lse_ref[:, hh:hh + 1]
                    probs = jnp.exp(s - lse_h)
                    doh = do_ref[:, hh * HEAD_DIM:(hh + 1) * HEAD_DIM].astype(BF16)
                    dprobs = lax.dot_general(doh, vcat, NT, preferred_element_type=F32)
                    delta = jnp.sum(probs * dprobs, axis=1, keepdims=True)
                    ds = (probs * (dprobs - delta)).astype(BF16)
                    psink = jnp.exp(sink_ref[0:1, hh:hh + 1] - lse_h)
                    dsink_ref[0:1, hh:hh + 1] += -jnp.sum(psink * delta, axis=0, keepdims=True)
                    dv_cat = dv_cat + lax.dot_general(probs.astype(BF16), doh, TN, preferred_element_type=F32)
                    dq_ref[:, hh * HEAD_DIM:(hh + 1) * HEAD_DIM] = jnp.dot(
                        ds, kcat, preferred_element_type=F32) * (HEAD_DIM ** -0.5)
                    dk_cat = dk_cat + lax.dot_general(ds, qh, TN, preferred_element_type=F32)
                dk_cat = dk_cat * (HEAD_DIM ** -0.5)
                ko = kvh * HEAD_DIM
                vo = 128 + kvh * HEAD_DIM
                dkv_ref[:, ko:ko + HEAD_DIM] = carry_ref[:, ko:ko + HEAD_DIM] + dk_cat[:BLK]
                dkv_ref[:, vo:vo + HEAD_DIM] = carry_ref[:, vo:vo + HEAD_DIM] + dv_cat[:BLK]
                carry_ref[:, ko:ko + HEAD_DIM] = dk_cat[BLK:]
                carry_ref[:, vo:vo + HEAD_DIM] = dv_cat[BLK:]

        @pl.when(n == nb)
        def _():
            dkv_ref[...] = carry_ref[...]

    last = nb - 1
    res, xres = _call(
        body, name=name, grid=(nb + 1,),
        in_specs=[pl.BlockSpec((BLK, ATTN_W), lambda n: (jnp.minimum(n, last), 0)),
                  pl.BlockSpec((BLK, 256), lambda n: (jnp.minimum(n, last), kvb)),
                  pl.BlockSpec((BLK, 256), lambda n: (jnp.maximum(jnp.minimum(n, last) - 1, 0), kvb)),
                  pl.BlockSpec((1, N_Q), lambda n: (0, 0)),
                  pl.BlockSpec((BLK, ATTN_W), lambda n: (jnp.minimum(n, last), 0)),
                  pl.BlockSpec((BLK, N_Q), lambda n: (jnp.minimum(n, last), 0))],
        out_specs=[pl.BlockSpec((BLK, ATTN_W), lambda n: (jnp.minimum(n, last), 0)),
                   pl.BlockSpec((BLK, 256), lambda n: (jnp.maximum(n - 1, 0), 0)),
                   pl.BlockSpec((1, N_Q), lambda n: (0, 0))],
        out_shape=[jax.ShapeDtypeStruct((t, ATTN_W), F32), jax.ShapeDtypeStruct((t, 256), F32),
                   jax.ShapeDtypeStruct((1, N_Q), F32)],
        scratch_shapes=[pltpu.VMEM((BLK, 256), F32)],
        args=[proj, proj, proj, sinks, dattn, lse], semantics=("arbitrary",), exchanges=exchanges)
    return (*res, xres) if exchanges else res


def _cumsum_rows(x):
    n = x.shape[0]
    row = lax.broadcasted_iota(jnp.int32, x.shape, 0)
    s = 1
    while s < n:
        x = x + jnp.where(row >= s, pltpu.roll(x, s, axis=0), 0.0)
        s *= 2
    return x


def _rev_cumsum_rows(x):
    n = x.shape[0]
    row = lax.broadcasted_iota(jnp.int32, x.shape, 0)
    s = 1
    while s < n:
        x = x + jnp.where(row < n - s, pltpu.roll(x, n - s, axis=0), 0.0)
        s *= 2
    return x


def _lower_bound(lbl_ref):
    l0 = lbl_ref[0:1, :]
    l1 = lbl_ref[1:2, :]
    mx = jnp.maximum(l0, l1)
    e0 = jnp.exp(l0 - mx)
    e1 = jnp.exp(l1 - mx)
    return e0 / (e0 + e1)


def _hgrn_gates(z, lb):
    sg = _sigmoid(z)
    f = lb + (1.0 - lb) * sg
    return sg, f, jnp.log(f), 1.0 - f


def _sub_factors(b, i):
    rows = lax.broadcasted_iota(jnp.int32, (CHUNK, RNN_HD), 0)
    ref = b[SUB * i - 1:SUB * i, :]
    qfac = jnp.exp(b[SUB * i:SUB * (i + 1), :] - ref)
    kfac = jnp.where(rows < SUB * i, jnp.exp(jnp.minimum(ref - b, 0.0)), 0.0)
    return qfac, kfac


def _diag_decay(bi, s):
    trow = lax.broadcasted_iota(jnp.int32, (SUB, RNN_HD), 0)
    return jnp.where(trow >= s, jnp.exp(jnp.minimum(bi - bi[s:s + 1, :], 0.0)), 0.0)


def _hgrn_fwd(proj, lb_logits, norm_gain, *, tb, name, exchanges=()):
    t = proj.shape[0]
    ntb = t // tb
    nch = tb // CHUNK
    qb, fb, ib, gb = QR_COL // 128, FR_COL // 128, IR_COL // 128, GR_COL // 128

    def body(q_ref, f_ref, i_ref, g_ref, lbl_ref, gain_ref, o_ref, out_ref, s0_ref, st_ref, ob_ref):
        c = pl.program_id(1)

        @pl.when(c == 0)
        def _():
            st_ref[...] = jnp.zeros_like(st_ref)

        lb = _lower_bound(lbl_ref)
        gain = gain_ref[...]

        def chunk(ci, carry):
            r0 = pl.multiple_of(ci * CHUNK, CHUNK)
            rows = pl.ds(r0, CHUNK)
            _, _, lf, k = _hgrn_gates(f_ref[rows, :], lb)
            qr = q_ref[rows, :]
            q = qr * _sigmoid(qr)
            v = i_ref[rows, :]
            b = _cumsum_rows(lf)
            st = st_ref[...]
            s0_ref[ci] = st
            ob_ref[...] = lax.dot_general((q * jnp.exp(b)).astype(BF16), st.astype(BF16), NT,
                                          preferred_element_type=F32)
            vb = v.astype(BF16)
            for i in range(CHUNK // SUB):
                blk = slice(SUB * i, SUB * (i + 1))
                qi, ki, vi, bi = q[blk], k[blk], v[blk], b[blk]
                oi = ob_ref[blk, :]
                if i > 0:
                    qfac, kfac = _sub_factors(b, i)
                    att = lax.dot_general((qi * qfac).astype(BF16), (k * kfac).astype(BF16), NT,
                                          preferred_element_type=F32)
                    oi = oi + jnp.dot(att.astype(BF16), vb, preferred_element_type=F32)
                for s in range(SUB):
                    e = _diag_decay(bi, s)
                    a = jnp.sum(qi * ki[s:s + 1, :] * e, axis=1, keepdims=True)
                    oi = oi + a * vi[s:s + 1, :]
                ob_ref[blk, :] = oi
            blast = b[CHUNK - 1:CHUNK, :]
            khat = (k * jnp.exp(blast - b)).astype(BF16)
            st_ref[...] = st * jnp.exp(blast) + lax.dot_general(vb, khat, TN, preferred_element_type=F32)
            o = ob_ref[...]
            o_ref[rows, :] = o
            gr = g_ref[rows, :]
            out_ref[rows, :] = o * _rstd(o) * gain * (gr * _sigmoid(gr))
            return carry

        lax.fori_loop(0, nch, chunk, 0, unroll=True)

    def col(base):
        return pl.BlockSpec((tb, RNN_HD), lambda h, c: (c, base + h))

    res, xres = _call(
        body, name=name, grid=(N_RNN, ntb),
        in_specs=[col(qb), col(fb), col(ib), col(gb),
                  pl.BlockSpec((2, RNN_HD), lambda h, c: (0, h)), pl.BlockSpec((1, RNN_HD), lambda h, c: (0, 0))],
        out_specs=[pl.BlockSpec((tb, RNN_HD), lambda h, c: (c, h)), pl.BlockSpec((tb, RNN_HD), lambda h, c: (c, h)),
                   pl.BlockSpec((None, nch, RNN_HD, RNN_HD), lambda h, c: (h, c, 0, 0))],
        out_shape=[jax.ShapeDtypeStruct((t, RNN_W), F32), jax.ShapeDtypeStruct((t, RNN_W), F32),
                   jax.ShapeDtypeStruct((N_RNN, t // CHUNK, RNN_HD, RNN_HD), F32)],
        scratch_shapes=[pltpu.VMEM((RNN_HD, RNN_HD), F32), pltpu.VMEM((CHUNK, RNN_HD), F32)],
        args=[proj, proj, proj, proj, lb_logits, norm_gain],
        semantics=("parallel", "arbitrary"), exchanges=exchanges)
    return (*res, xres) if exchanges else res


def _hgrn_bwd(proj, lb_logits, norm_gain, o_pre, s0, dcat, *, tb, name, exchanges=()):
    t = proj.shape[0]
    ntb = t // tb
    nch = tb // CHUNK
    qb, fb, ib, gb = QR_COL // 128, FR_COL // 128, IR_COL // 128, GR_COL // 128
    nsub = CHUNK // SUB

    def body(q_ref, f_ref, i_ref, g_ref, lbl_ref, gain_ref, o_ref, s0_ref, dout_ref,
             dq_ref, df_ref, di_ref, dg_ref, dlb_ref, dgain_ref,
             dst_ref, dqa_ref, dka_ref, dva_ref):
        c = pl.program_id(1)

        @pl.when(c == 0)
        def _():
            dst_ref[...] = jnp.zeros_like(dst_ref)
            dlb_ref[...] = jnp.zeros_like(dlb_ref)
            dgain_ref[...] = jnp.zeros_like(dgain_ref)

        lb = _lower_bound(lbl_ref)
        gain = gain_ref[...]

        def chunk(cj, carry):
            ci = nch - 1 - cj
            r0 = pl.multiple_of(ci * CHUNK, CHUNK)
            rows = pl.ds(r0, CHUNK)
            sg, f, lf, k = _hgrn_gates(f_ref[rows, :], lb)
            qr = q_ref[rows, :]
            sq = _sigmoid(qr)
            q = qr * sq
            v = i_ref[rows, :]
            b = _cumsum_rows(lf)

            dout = dout_ref[rows, :]
            o = o_ref[rows, :]
            gr = g_ref[rows, :]
            sgg = _sigmoid(gr)
            gate = gr * sgg
            rs = _rstd(o)
            nrm = o * rs
            dg_ref[rows, :] = dout * nrm * gain * (sgg * (1.0 + gr * (1.0 - sgg)))
            dn = dout * gate
            dgain_ref[...] += jnp.sum(dn * nrm, axis=0, keepdims=True)
            tt = dn * gain
            do = rs * (tt - nrm * jnp.mean(tt * nrm, axis=-1, keepdims=True))

            dob = do.astype(BF16)
            vb = v.astype(BF16)
            eb = jnp.exp(b)
            blast = b[CHUNK - 1:CHUNK, :]
            ebl = jnp.exp(blast - b)
            dst = dst_ref[...]
            dstb = dst.astype(BF16)
            khat = (k * ebl).astype(BF16)
            s0 = s0_ref[ci]
            dqa_ref[...] = eb * jnp.dot(dob, s0.astype(BF16), preferred_element_type=F32)
            dk_state = ebl * jnp.dot(vb, dstb, preferred_element_type=F32)
            dka_ref[...] = dk_state
            d_blast = (jnp.sum(k * dk_state, axis=0, keepdims=True)
                       + jnp.exp(blast) * jnp.sum(dst * s0, axis=0, keepdims=True))
            dva_ref[...] = lax.dot_general(khat, dstb, NT, preferred_element_type=F32)
            dst_ref[...] = dst * jnp.exp(blast) + lax.dot_general(dob, (q * eb).astype(BF16), TN,
                                                                  preferred_element_type=F32)
            pm = lax.dot_general(dob, vb, NT, preferred_element_type=F32)
            for i in range(nsub):
                blk = slice(SUB * i, SUB * (i + 1))
                qi, ki, vi, bi, doi = q[blk], k[blk], v[blk], b[blk], do[blk]
                dqi = dqa_ref[blk, :]
                if i > 0:
                    qfac, kfac = _sub_factors(b, i)
                    qt = (qi * qfac).astype(BF16)
                    kt = (k * kfac).astype(BF16)
                    att = lax.dot_general(qt, kt, NT, preferred_element_type=F32).astype(BF16)
                    pmi = pm[blk, :].astype(BF16)
                    dva_ref[...] += lax.dot_general(att, doi.astype(BF16), TN, preferred_element_type=F32)
                    dqi = dqi + qfac * jnp.dot(pmi, kt, preferred_element_type=F32)
                    dka_ref[...] += kfac * lax.dot_general(pmi, qt, TN, preferred_element_type=F32)
                dki = dka_ref[blk, :]
                dvi = dva_ref[blk, :]
                trow = lax.broadcasted_iota(jnp.int32, (SUB, RNN_HD), 0)
                for s in range(SUB):
                    e = _diag_decay(bi, s)
                    ks = ki[s:s + 1, :]
                    vs = vi[s:s + 1, :]
                    a = jnp.sum(qi * ks * e, axis=1, keepdims=True)
                    p = jnp.sum(doi * vs, axis=1, keepdims=True)
                    dvs = jnp.sum(a * doi, axis=0, keepdims=True)
                    dks = jnp.sum(p * qi * e, axis=0, keepdims=True)
                    dqi = dqi + p * ks * e
                    dki = dki + jnp.where(trow == s, dks, 0.0)
                    dvi = dvi + jnp.where(trow == s, dvs, 0.0)
                dqa_ref[blk, :] = dqi
                dka_ref[blk, :] = dki
                dva_ref[blk, :] = dvi

            dq = dqa_ref[...]
            dk = dka_ref[...]
            lastrow = lax.broadcasted_iota(jnp.int32, (CHUNK, RNN_HD), 0) == CHUNK - 1
            dlf = _rev_cumsum_rows(q * dq - k * dk + jnp.where(lastrow, d_blast, 0.0))
            dff = dlf / f - dk
            df_ref[rows, :] = dff * (1.0 - lb) * sg * (1.0 - sg)
            dlb_ref[...] += jnp.sum(dff * (1.0 - sg), axis=0, keepdims=True)
            dq_ref[rows, :] = dq * (sq * (1.0 + qr * (1.0 - sq)))
            di_ref[rows, :] = dva_ref[...]
            return carry

        lax.fori_loop(0, nch, chunk, 0, unroll=True)

    def col(base):
        return pl.BlockSpec((tb, RNN_HD), lambda h, c: (ntb - 1 - c, base + h))

    outc = pl.BlockSpec((tb, RNN_HD), lambda h, c: (ntb - 1 - c, h))
    hb = ATTN_W // RNN_HD
    res, xres = _call(
        body, name=name, grid=(N_RNN, ntb),
        in_specs=[col(qb), col(fb), col(ib), col(gb),
                  pl.BlockSpec((2, RNN_HD), lambda h, c: (0, h)), pl.BlockSpec((1, RNN_HD), lambda h, c: (0, 0)),
                  outc,
                  pl.BlockSpec((None, nch, RNN_HD, RNN_HD), lambda h, c: (h, ntb - 1 - c, 0, 0)),
                  pl.BlockSpec((tb, RNN_HD), lambda h, c: (ntb - 1 - c, hb + h))],
        out_specs=[outc, outc, outc, outc,
                   pl.BlockSpec((1, RNN_HD), lambda h, c: (0, h)),
                   pl.BlockSpec((None, 1, RNN_HD), lambda h, c: (h, 0, 0))],
        out_shape=[jax.ShapeDtypeStruct((t, RNN_W), F32)] * 4
        + [jax.ShapeDtypeStruct((1, RNN_W), F32), jax.ShapeDtypeStruct((N_RNN, 1, RNN_HD), F32)],
        scratch_shapes=[pltpu.VMEM((RNN_HD, RNN_HD), F32),
                        pltpu.VMEM((CHUNK, RNN_HD), F32), pltpu.VMEM((CHUNK, RNN_HD), F32),
                        pltpu.VMEM((CHUNK, RNN_HD), F32)],
        args=[proj, proj, proj, proj, lb_logits, norm_gain, o_pre, s0, dcat],
        semantics=("parallel", "arbitrary"), exchanges=exchanges)
    return (*res, xres) if exchanges else res


def _all_gather_halves(shards, *, name):
    n = len(shards)

    def body(*refs):
        ins, outs = refs[:n], refs[n:2 * n]
        send_sems, recv_sems = refs[2 * n:]
        x, y, c = _place()
        sibling = (x, y, 1 - c)
        chips = [(1 - x, y), (x, 1 - y), (1 - x, 1 - y)]

        def copy(a, k, block, to, src=None):
            slot = outs[a].at[4 * block[0] + 2 * block[1] + block[2]]
            return pltpu.make_async_remote_copy(
                src_ref=slot if src is None else src, dst_ref=slot,
                send_sem=send_sems.at[a, k], recv_sem=recv_sems.at[a, k],
                device_id=to, device_id_type=MESH)

        first, passed = [], []
        for a in range(n):
            for j, chip in enumerate(chips):
                cp = copy(a, j, (x, y, c), (*chip, c), src=ins[a].at[c])
                cp.start()
                first.append(cp)
        for a in range(n):
            for j, chip in enumerate(chips):
                copy(a, j, (*chip, c), (x, y, c)).wait_recv()
                cp = copy(a, 3 + j, (*chip, c), sibling)
                cp.start()
                passed.append(cp)
        for a in range(n):
            for j, chip in enumerate(chips):
                copy(a, 3 + j, (*chip, 1 - c), (x, y, c)).wait_recv()
        for cp in first + passed:
            cp.wait_send()

    return pl.pallas_call(
        body, name=name,
        in_specs=[ANY] * n, out_specs=[ANY] * n,
        out_shape=[jax.ShapeDtypeStruct((8,) + s.shape[1:], s.dtype) for s in shards],
        scratch_shapes=[pltpu.SemaphoreType.DMA((n, 6)), pltpu.SemaphoreType.DMA((n, 6))],
    )(*shards)


def _row_tile(rows, cols, budget=1 << 20):
    tr = rows
    while tr * cols > budget and tr % 16 == 0:
        tr //= 2
    return tr


def _pair_sum(g, sib, where, *, name):
    _, _, rh, cols = g.shape
    tr = _row_tile(rh, cols)

    def body(w_ref, g_ref, s_ref, o_ref):
        o_ref[...] = (g_ref[...] + s_ref[...]).astype(BF16)

    return pl.pallas_call(
        body, name=name,
        grid_spec=pltpu.PrefetchScalarGridSpec(
            num_scalar_prefetch=1, grid=(4, rh // tr),
            in_specs=[pl.BlockSpec((None, None, tr, cols), lambda s, i, w: (s, w[1], i, 0)),
                      pl.BlockSpec((None, tr, cols), lambda s, i, w: (s, i, 0))],
            out_specs=pl.BlockSpec((None, tr, cols), lambda s, i, w: (s, i, 0))),
        out_shape=jax.ShapeDtypeStruct((4, rh, cols), BF16),
        compiler_params=_params(("parallel", "parallel")),
    )(where, g, sib)


def _final_half(g, sib, recv, where, *, name):
    _, _, rh, cols = g.shape
    tr = _row_tile(rh, cols)

    def body(w_ref, g_ref, s_ref, r_ref, o_ref):
        acc = g_ref[...] + s_ref[...]
        for j in range(3):
            acc = acc + r_ref[j].astype(F32)
        o_ref[...] = acc

    return pl.pallas_call(
        body, name=name,
        grid_spec=pltpu.PrefetchScalarGridSpec(
            num_scalar_prefetch=1, grid=(rh // tr,),
            in_specs=[pl.BlockSpec((None, None, tr, cols), lambda i, w: (w[0], w[1], i, 0)),
                      pl.BlockSpec((None, tr, cols), lambda i, w: (w[0], i, 0)),
                      pl.BlockSpec((3, tr, cols), lambda i, w: (0, i, 0))],
            out_specs=pl.BlockSpec((tr, cols), lambda i, w: (i, 0))),
        out_shape=jax.ShapeDtypeStruct((rh, cols), F32),
        compiler_params=_params(("parallel",)),
    )(where, g, sib, recv)


def _adamw_math(w, g, m, v):
    m = ADAM_B1 * m + (1.0 - ADAM_B1) * g
    v = ADAM_B2 * v + (1.0 - ADAM_B2) * (g * g)
    m_hat = m / (1.0 - ADAM_B1 ** ADAM_STEP)
    v_hat = v / (1.0 - ADAM_B2 ** ADAM_STEP)
    delta = -ADAM_LR * (m_hat / (jnp.sqrt(v_hat) + ADAM_EPS) + ADAM_WD * w)
    return delta, m, v


def _adamw(w, mine, theirs, m, v, where, *, name):
    rows, cols = w.shape
    tr = _row_tile(rows // 2, cols, budget=1 << 19)
    nh = rows // 2 // tr

    def body(wh_ref, w_ref, a_ref, b_ref, m_ref, v_ref, g_ref, d_ref, nm_ref, nv_ref):
        g = jnp.where(pl.program_id(0) // nh == wh_ref[1], a_ref[...], b_ref[...])
        d, nm, nv = _adamw_math(w_ref[...], g, m_ref[...], v_ref[...])
        g_ref[...] = g
        d_ref[...] = d
        nm_ref[...] = nm
        nv_ref[...] = nv

    blk = pl.BlockSpec((tr, cols), lambda i, wh: (i, 0))
    half = pl.BlockSpec((tr, cols), lambda i, wh: (i % nh, 0))
    return pl.pallas_call(
        body, name=name,
        grid_spec=pltpu.PrefetchScalarGridSpec(
            num_scalar_prefetch=1, grid=(rows // tr,),
            in_specs=[blk, half, half, blk, blk], out_specs=[blk] * 4),
        out_shape=[jax.ShapeDtypeStruct((rows, cols), F32)] * 4,
        compiler_params=_params(("parallel",)),
    )(where, w, mine, theirs, m, v)


SEG_LOSS = 0
SEG_SINK = 128
SEG_AGAIN = 256
SEG_L0 = SEG_AGAIN + ATTN_W
SEG_L1 = SEG_L0 + RNN_W
SEG_RGAIN = SEG_L1 + RNN_W
SEG_G = SEG_RGAIN + 128
N_PACK = SEG_G + 4 * D_MODEL


def _pack(sinks, again, l0, l1, rgain, gains, loss=None):
    z = lambda k: jnp.zeros((1, k), F32)
    first = z(128) if loss is None else loss
    return jnp.concatenate([first, sinks, z(128 - N_Q), again, l0, l1, rgain] + list(gains), axis=1)


def _small_reduce_adamw(part, w, m, v, *, name):
    def body(p_ref, w_ref, m_ref, v_ref, g_ref, d_ref, nm_ref, nv_ref, buf_ref, send_sems, recv_sems):
        x, y, c = _place()
        me = 4 * x + 2 * y + c
        copies = []
        for k in range(1, 8):
            dx, dy, dc = (k >> 2) & 1, (k >> 1) & 1, k & 1
            to = (x ^ dx, y ^ dy, c ^ dc)
            cp = pltpu.make_async_remote_copy(
                src_ref=p_ref, dst_ref=buf_ref.at[me],
                send_sem=send_sems.at[k - 1], recv_sem=recv_sems.at[k - 1],
                device_id=to, device_id_type=MESH)
            cp.start()
            copies.append(cp)
        buf_ref[me] = p_ref[...]
        for cp in copies:
            cp.wait()
        tot = buf_ref[0]
        for j in range(1, 8):
            tot = tot + buf_ref[j]
        g_ref[...] = tot
        l0 = w_ref[:, SEG_L0:SEG_L0 + RNN_W]
        l1 = w_ref[:, SEG_L1:SEG_L1 + RNN_W]
        mx = jnp.maximum(l0, l1)
        e0 = jnp.exp(l0 - mx)
        e1 = jnp.exp(l1 - mx)
        lb = e0 / (e0 + e1)
        gl0 = tot[:, SEG_L0:SEG_L0 + RNN_W] * lb * (1.0 - lb)
        g_ref[:, SEG_L0:SEG_L0 + RNN_W] = gl0
        g_ref[:, SEG_L1:SEG_L1 + RNN_W] = -gl0
        d, nm, nv = _adamw_math(w_ref[...], g_ref[...], m_ref[...], v_ref[...])
        d_ref[...] = d
        nm_ref[...] = nm
        nv_ref[...] = nv

    vm = pl.BlockSpec(memory_space=pltpu.VMEM)
    return pl.pallas_call(
        body, name=name,
        in_specs=[vm] * 4, out_specs=[vm] * 4,
        out_shape=[jax.ShapeDtypeStruct((1, N_PACK), F32)] * 4,
        scratch_shapes=[pltpu.VMEM((8, 1, N_PACK), F32), pltpu.SemaphoreType.DMA((7,)),
                        pltpu.SemaphoreType.DMA((7,))],
    )(part, w, m, v)


def _layer_grads(xs, tgt, halves, where, sinks, again, lb_logits, rgain,
                 g_mix_pre, g_mix_post, g_mlp_pre, g_mlp_post):
    tm = 512
    h_in, h_out, h_up, h_dn = halves

    def whole(buf, own):
        return lax.dynamic_update_slice(buf, own, (2 * where[0], 0, 0))

    shard = IN_W // N_CHIPS
    w_in4 = whole(_all_gather_halves([h_in], name="gather_w_in")[0], h_in).reshape(N_CHIPS, D_MODEL, shard)
    w_in = w_in4.transpose(1, 0, 2).reshape(D_MODEL, IN_W)
    w_in_t = w_in4.transpose(0, 2, 1).reshape(IN_W, D_MODEL)
    h1 = _rms_cast(xs, g_mix_pre, tm=tm, name="h1_norm")
    proj, ((b_out,),) = _mm(h1, w_in, tm=1024, tn=768, tk=D_MODEL, out_dtype=F32, name="in_proj",
                            exchanges=[_x_gather_ici([h_out])])
    attn, lse, ((b_up,), (b_out,)) = _swa_fwd(proj, sinks, name="swa_fwd",
                                              exchanges=[_x_gather_ici([h_up]), _x_gather_d2d([b_out])])
    w_out = whole(b_out, h_out).reshape(D_MODEL, D_MODEL)
    o_pre, rnn, s0, ((b_dn,), (b_up,)) = _hgrn_fwd(proj, lb_logits, rgain, tb=512, name="hgrn_fwd",
                                                   exchanges=[_x_gather_ici([h_dn]), _x_gather_d2d([b_up])])
    w_up4 = whole(b_up, h_up).reshape(N_CHIPS, D_MODEL, D_FF // N_CHIPS)
    w_up = w_up4.transpose(1, 0, 2).reshape(D_MODEL, D_FF)
    w_up_t = w_up4.transpose(0, 2, 1).reshape(D_FF, D_MODEL)
    cat = _mix_cat(attn, rnn, again, tm=tm, name="mix_cat")
    mixed, ((b_dn,),) = _mm(cat, w_out, tm=1024, tn=1024, tk=D_MODEL, out_dtype=F32, name="out_proj",
                            exchanges=[_x_gather_d2d([b_dn])])
    w_dn = whole(b_dn, h_dn).reshape(D_FF, D_MODEL)
    x1, h2 = _post_norm_res(mixed, g_mix_post, xs, g_mlp_pre, tm=256, name="mix_post")
    u = _mm(h2, w_up, tm=1024, tn=1024, tk=D_MODEL, out_dtype=BF16, relu=True, name="mlp_up")
    yv = _mm(u, w_dn, tm=1024, tn=1024, tk=2048, out_dtype=F32, a_square=True, name="mlp_down")
    dy, dx2, loss_row, dg_mlp_post = _loss_head(yv, g_mlp_post, x1, tgt, tm=256, name="loss_head")

    def halved(g):
        return g.reshape(N_CHIPS, 2, g.shape[1] // 2, g.shape[2])
    du = _mm(dy, w_dn.T, tm=1024, tn=1024, tk=D_MODEL, out_dtype=BF16, mul2=u, name="mlp_down_bwd")
    g_dn = halved(_mm_tn(u, dy, tm=1024, tn=1024, tt=2048, a_square=True, name="w_down_grad")
                  .reshape(N_CHIPS, D_FF // N_CHIPS, D_MODEL))
    d_w_up, ((sib_dn,),) = _mm_tn(h2, du, tm=1024, tn=1024, tt=2048, n_split=N_CHIPS, name="w_up_grad",
                                  exchanges=[_x_pair([g_dn])])
    g_up = halved(d_w_up)
    wire_dn = _pair_sum(g_dn, sib_dn, where, name="pair_sum_w_down")
    dh2, ((recv_dn,), (sib_up,)) = _mm(du, w_up_t, tm=1024, tn=1024, tk=2048, out_dtype=F32, name="mlp_up_bwd",
                                       exchanges=[_x_chip([wire_dn]), _x_pair([g_up])])
    wire_up = _pair_sum(g_up, sib_up, where, name="pair_sum_w_up")
    fin_dn = _final_half(g_dn, sib_dn, recv_dn, where, name="final_half_w_down")
    dx1, dg_mlp_pre = _rms_bwd(dh2, x1, g_mlp_pre, dx2, tm=256, out_dtype=F32, name="mlp_pre_bwd")
    dmixed, dg_mix_post = _rms_bwd(dx1, mixed, g_mix_post, None, tm=256, out_dtype=BF16, name="mix_post_bwd")
    d_w_out, ((oth_dn,),) = _mm_tn(cat, dmixed, tm=1024, tn=1024, tt=2048, name="w_out_grad",
                                   exchanges=[_x_share([fin_dn])])
    g_out = halved(d_w_out.reshape(N_CHIPS, D_MODEL // N_CHIPS, D_MODEL))
    dcat, ((sib_out,),) = _mm(dmixed, w_out.T, tm=1024, tn=1024, tk=D_MODEL, out_dtype=F32, name="out_proj_bwd",
                              exchanges=[_x_pair([g_out])])
    wire_out = _pair_sum(g_out, sib_out, where, name="pair_sum_w_out")
    dattn, dg_again = _rms_bwd(dcat, attn, again, None, tm=tm, out_dtype=F32, name="attn_norm_bwd")
    dq_a, dkv, dsinks, ((recv_out,),) = _swa_bwd(proj, sinks, dattn, lse, name="swa_bwd",
                                                 exchanges=[_x_chip([wire_out])])
    dq_r, df_r, di_r, dg_r, dlb, dgain_h, ((recv_up,),) = _hgrn_bwd(
        proj, lb_logits, rgain, o_pre, s0, dcat, tb=512, name="hgrn_bwd", exchanges=[_x_chip([wire_up])])
    fin_up = _final_half(g_up, sib_up, recv_up, where, name="final_half_w_up")
    fin_out = _final_half(g_out, sib_out, recv_out, where, name="final_half_w_out")
    dproj = jnp.concatenate([dq_a, dkv, dq_r, df_r, di_r, dg_r], axis=1).astype(BF16)
    d_w_in, ((oth_up, oth_out),) = _mm_tn(h1, dproj, tm=1024, tn=768, tt=2048, name="w_in_grad",
                                          exchanges=[_x_share([fin_up, fin_out])])
    g_in = halved(d_w_in[0].reshape(D_MODEL, N_CHIPS, shard).transpose(1, 0, 2))
    dh1, ((sib_in,),) = _mm(dproj, w_in_t, tm=1024, tn=1024, tk=2688, out_dtype=F32, name="in_proj_bwd",
                            exchanges=[_x_pair([g_in])])
    wire_in = _pair_sum(g_in, sib_in, where, name="pair_sum_w_in")
    gx, dg_mix_pre, ((recv_in,),) = _rms_bwd(dh1, xs, g_mix_pre, dx1, tm=256, out_dtype=F32, name="mix_pre_bwd",
                                             exchanges=[_x_chip([wire_in])])
    fin_in = _final_half(g_in, sib_in, recv_in, where, name="final_half_w_in")
    (oth_in,) = _run_exchange(_x_share([fin_in]), name="share_w_in")

    big = [(fin_in, oth_in), (fin_out, oth_out), (fin_up, oth_up), (fin_dn, oth_dn)]
    drgain = jnp.sum(dgain_h, axis=0)
    small = _pack(dsinks, dg_again, dlb, jnp.zeros_like(dlb), drgain,
                  [dg_mix_pre, dg_mix_post, dg_mlp_pre, dg_mlp_post], loss=loss_row)
    return gx, big, small


def kernel(x, w_in, attn_sinks, attn_out_gain, rnn_lb_logits, rnn_norm_gain, w_out, mix_pre_gain, mix_post_gain, mlp_pre_gain, mlp_post_gain, w_up, w_down, loss_target, m_w_in, m_attn_sinks, m_attn_out_gain, m_rnn_lb_logits, m_rnn_norm_gain, m_w_out, m_mix_pre_gain, m_mix_post_gain, m_mlp_pre_gain, m_mlp_post_gain, m_w_up, m_w_down, v_w_in, v_attn_sinks, v_attn_out_gain, v_rnn_lb_logits, v_rnn_norm_gain, v_w_out, v_mix_pre_gain, v_mix_post_gain, v_mlp_pre_gain, v_mlp_post_gain, v_w_up, v_w_down):
    ax, ay, ac = _place()
    where = jnp.stack([2 * ax + ay, ac]).astype(jnp.int32)
    big_w = [w_in[0], w_out[0], w_up[0], w_down[0]]
    big_m = [m_w_in[0], m_w_out[0], m_w_up[0], m_w_down[0]]
    big_v = [v_w_in[0], v_w_out[0], v_w_up[0], v_w_down[0]]

    halves = [w.astype(BF16).reshape(2, w.shape[0] // 2, w.shape[1]) for w in big_w]
    gx, big_g, small_part = _layer_grads(
        x[0], loss_target[0], halves, where, attn_sinks, attn_out_gain, rnn_lb_logits, rnn_norm_gain,
        mix_pre_gain, mix_post_gain, mlp_pre_gain, mlp_post_gain)

    names = ["w_in", "w_out", "w_up", "w_down"]
    grads, deltas, new_m, new_v = [], [], [], []
    for (f, o), w, m, v, nm in zip(big_g, big_w, big_m, big_v, names):
        g, d, nm_, nv_ = _adamw(w, f, o, m, v, where, name="adamw_" + nm)
        grads.append(g[None])
        deltas.append(d[None])
        new_m.append(nm_[None])
        new_v.append(nv_[None])

    def pack_params(sinks, again, logits, rgain, gains):
        return _pack(sinks, again, logits[0:1], logits[1:2], rgain, gains)

    pw = pack_params(attn_sinks, attn_out_gain, rnn_lb_logits, rnn_norm_gain,
                     [mix_pre_gain, mix_post_gain, mlp_pre_gain, mlp_post_gain])
    pm = pack_params(m_attn_sinks, m_attn_out_gain, m_rnn_lb_logits, m_rnn_norm_gain,
                     [m_mix_pre_gain, m_mix_post_gain, m_mlp_pre_gain, m_mlp_post_gain])
    pv = pack_params(v_attn_sinks, v_attn_out_gain, v_rnn_lb_logits, v_rnn_norm_gain,
                     [v_mix_pre_gain, v_mix_post_gain, v_mlp_pre_gain, v_mlp_post_gain])
    packs = _small_reduce_adamw(small_part, pw, pm, pv, name="small_reduce_adamw")

    def unpack(p):
        seg = lambda o, k: p[:, o:o + k]
        logits = jnp.concatenate([seg(SEG_L0, RNN_W), seg(SEG_L1, RNN_W)], axis=0)
        gains = [seg(SEG_G + i * D_MODEL, D_MODEL) for i in range(4)]
        return dict(sinks=seg(SEG_SINK, N_Q), again=seg(SEG_AGAIN, ATTN_W), logits=logits,
                    rgain=seg(SEG_RGAIN, RNN_HD), gains=gains)

    def order(small, big):
        return [big[0], small["sinks"], small["again"], small["logits"], small["rgain"], big[1],
                *small["gains"], big[2], big[3]]

    loss = packs[0][0, 0]
    outs = [loss, gx[None]]
    for p, b in zip(packs, [grads, deltas, new_m, new_v]):
        outs += order(unpack(p), b)
    return tuple(outs)
```

```python
import functools

import jax
import jax.numpy as jnp
from jax import lax
from jax.experimental import pallas as pl
from jax.experimental.pallas import tpu as pltpu

F32 = jnp.float32
BF16 = jnp.bfloat16
MESH = pl.DeviceIdType.MESH

EPS = 1e-6
D_MODEL = 2048
ATTN_W = 1024
HEAD_DIM = 64
N_Q = 16
N_KV = 2
GROUP = 8
BLK = 128
RNN_W = 1024
RNN_HD = 128
N_RNN = 8
CHUNK = 64
SUB_FWD = 16
SUB_BWD = 8
D_FF = 8192
IN_W = 5376
N_CHIPS = 4
KV_COL = ATTN_W
QR_COL = ATTN_W + 2 * 128
FR_COL = QR_COL + RNN_W
IR_COL = FR_COL + RNN_W
GR_COL = IR_COL + RNN_W

ADAM_LR = 0.001
ADAM_B1 = 0.9
ADAM_B2 = 0.999
ADAM_EPS = 1e-08
ADAM_WD = 0.01
ADAM_STEP = 10

VMEM_LIMIT = 48 * 1024 * 1024

NT = (((1,), (1,)), ((), ()))
TN = (((0,), (0,)), ((), ()))


def _params(sem=None):
    return pltpu.CompilerParams(dimension_semantics=sem, vmem_limit_bytes=VMEM_LIMIT)


def _sigmoid(x):
    return 1.0 / (1.0 + jnp.exp(-x))


ANY = pl.BlockSpec(memory_space=pl.ANY)


def _place():
    return lax.axis_index("x"), lax.axis_index("y"), lax.axis_index("c")


def _other_chips(x, y):
    return [(1 - x, y), (x, 1 - y), (1 - x, 1 - y)]


class _Exchange:
    def __init__(self, srcs, outs, ncopy, build, aliases=None):
        self.srcs, self.outs, self.ncopy, self.build = list(srcs), list(outs), ncopy, build
        self.aliases = aliases or {}


def _remote(src, dst, send_sems, recv_sems, k, to):
    return pltpu.make_async_remote_copy(src_ref=src, dst_ref=dst, send_sem=send_sems.at[k],
                                        recv_sem=recv_sems.at[k], device_id=to, device_id_type=MESH)


def _call(body, *, name, grid, in_specs, out_specs, out_shape, args, scratch_shapes=(), semantics=None,
          exchanges=()):
    in_specs, out_specs, out_shape = list(in_specs), list(out_specs), list(out_shape)
    scratch_shapes = list(scratch_shapes)
    ni, no, ns = len(in_specs), len(out_specs), len(scratch_shapes)
    xsrc = [s for x in exchanges for s in x.srcs]
    xout = [o for x in exchanges for o in x.outs]
    nxi, nxo = len(xsrc), len(xout)
    aliases = {}
    a0 = b0 = 0
    for x in exchanges:
        for si, oi in x.aliases.items():
            aliases[ni + a0 + si] = no + b0 + oi
        a0 += len(x.srcs)
        b0 += len(x.outs)
    sems = []
    for x in exchanges:
        sems += [pltpu.SemaphoreType.DMA((x.ncopy,)), pltpu.SemaphoreType.DMA((x.ncopy,))]

    def wrapped(*refs):
        ins, xi = refs[:ni], refs[ni:ni + nxi]
        outs, xo = refs[ni + nxi:ni + nxi + no], refs[ni + nxi + no:ni + nxi + no + nxo]
        rest = refs[ni + nxi + no + nxo:]
        scr, sm = rest[:ns], rest[ns:]

        def copies():
            cps = []
            a = b = 0
            for k, x in enumerate(exchanges):
                cps += x.build(xi[a:a + len(x.srcs)], xo[b:b + len(x.outs)], sm[2 * k], sm[2 * k + 1])
                a += len(x.srcs)
                b += len(x.outs)
            return cps

        def start():
            for cp in copies():
                cp.start()

        def wait():
            for cp in copies():
                cp.wait()

        if not exchanges:
            body(*ins, *outs, *scr)
        elif not grid:
            start()
            body(*ins, *outs, *scr)
            wait()
        else:
            first = last = None
            for ax, g in enumerate(grid):
                f = pl.program_id(ax) == 0
                l = pl.program_id(ax) == g - 1
                first = f if first is None else first & f
                last = l if last is None else last & l
            pl.when(first)(start)
            body(*ins, *outs, *scr)
            pl.when(last)(wait)

    if exchanges and semantics is not None:
        semantics = ("arbitrary",) * len(grid)
    kwargs = dict(grid=grid) if grid else {}
    res = pl.pallas_call(
        wrapped, name=name,
        in_specs=in_specs + [ANY] * nxi, out_specs=out_specs + [ANY] * nxo,
        out_shape=out_shape + xout, scratch_shapes=scratch_shapes + sems,
        input_output_aliases=aliases,
        compiler_params=_params(semantics), **kwargs,
    )(*args, *xsrc)
    res = list(res)
    mine, theirs = res[:no], res[no:]
    per = []
    b = 0
    for x in exchanges:
        per.append(theirs[b:b + len(x.outs)])
        b += len(x.outs)
    return mine, per


def _run_exchange(x, *, name):
    return _call(lambda: None, name=name, grid=(), in_specs=[], out_specs=[], out_shape=[], args=[],
                 exchanges=[x])[1][0]


def _x_gather_ici(halves):
    n = len(halves)

    def build(srcs, outs, ss, rs):
        x, y, c = _place()
        return [_remote(srcs[a].at[c], outs[a].at[4 * x + 2 * y + c], ss, rs, 3 * a + j, (px, py, c))
                for a in range(n) for j, (px, py) in enumerate(_other_chips(x, y))]

    outs = [jax.ShapeDtypeStruct((8,) + h.shape[1:], h.dtype) for h in halves]
    return _Exchange(halves, outs, 3 * n, build)


def _x_gather_d2d(bufs):
    n = len(bufs)

    def build(srcs, outs, ss, rs):
        x, y, c = _place()
        cps = []
        for a in range(n):
            for j, (px, py) in enumerate(_other_chips(x, y)):
                slot = 4 * px + 2 * py + c
                cps.append(_remote(srcs[a].at[slot], outs[a].at[slot], ss, rs, 3 * a + j, (x, y, 1 - c)))
        return cps

    outs = [jax.ShapeDtypeStruct(b.shape, b.dtype) for b in bufs]
    return _Exchange(bufs, outs, 3 * n, build, aliases={a: a for a in range(n)})


def _x_pair(grads):
    n = len(grads)

    def build(srcs, outs, ss, rs):
        x, y, c = _place()
        return [_remote(srcs[a].at[:, 1 - c], outs[a], ss, rs, a, (x, y, 1 - c)) for a in range(n)]

    outs = [jax.ShapeDtypeStruct((4,) + g.shape[2:], g.dtype) for g in grads]
    return _Exchange(grads, outs, n, build)


def _x_chip(wires):
    n = len(wires)

    def build(srcs, outs, ss, rs):
        x, y, c = _place()
        return [_remote(srcs[a].at[2 * px + py], outs[a].at[j], ss, rs, 3 * a + j, (px, py, c))
                for a in range(n) for j, (px, py) in enumerate(_other_chips(x, y))]

    outs = [jax.ShapeDtypeStruct((3,) + w.shape[1:], w.dtype) for w in wires]
    return _Exchange(wires, outs, 3 * n, build)


def _x_share(halves):
    n = len(halves)

    def build(srcs, outs, ss, rs):
        x, y, c = _place()
        return [_remote(srcs[a], outs[a], ss, rs, a, (x, y, 1 - c)) for a in range(n)]

    outs = [jax.ShapeDtypeStruct(h.shape, h.dtype) for h in halves]
    return _Exchange(halves, outs, n, build)


def _mm(a, w, *, tm, tn, tk, out_dtype, name, a_square=False, relu=False, mul2=None, w_layout="kn",
        exchanges=()):
    m, k = a.shape
    if w_layout == "kn":
        n = w.shape[1]
        w_spec = pl.BlockSpec((tk, tn), lambda i, j, kk: (kk, j))
    elif w_layout == "nk":
        n = w.shape[0]
        w_spec = pl.BlockSpec((tn, tk), lambda i, j, kk: (j, kk))
    elif w_layout == "skn":
        n = w.shape[0] * w.shape[2]
        per_n = w.shape[2] // tn
        w_spec = pl.BlockSpec((None, tk, tn), lambda i, j, kk: (j // per_n, kk, j % per_n))
    else:
        assert w_layout == "snk"
        n = w.shape[1]
        per_k = w.shape[2] // tk
        w_spec = pl.BlockSpec((None, tn, tk), lambda i, j, kk: (kk // per_k, j, kk % per_k))
    w_dims = NT if w_layout in ("nk", "snk") else (((1,), (0,)), ((), ()))
    nk = k // tk
    assert m % tm == 0 and n % tn == 0 and k % tk == 0

    def body(*refs):
        if mul2 is not None:
            a_ref, w_ref, e_ref, o_ref, acc_ref = refs
        else:
            a_ref, w_ref, o_ref, acc_ref = refs
            e_ref = None
        kk = pl.program_id(2)
        av = a_ref[...]
        if a_square:
            af = av.astype(F32)
            av = (af * af).astype(BF16)
        part = lax.dot_general(av, w_ref[...], w_dims, preferred_element_type=F32)

        def finish(r):
            if relu:
                r = jnp.maximum(r, 0.0)
            if e_ref is not None:
                r = 2.0 * e_ref[...].astype(F32) * r
            o_ref[...] = r.astype(out_dtype)

        if nk == 1:
            finish(part)
        else:
            @pl.when(kk == 0)
            def _():
                acc_ref[...] = part

            @pl.when(kk > 0)
            def _():
                acc_ref[...] += part

            @pl.when(kk == nk - 1)
            def _():
                finish(acc_ref[...])

    in_specs = [pl.BlockSpec((tm, tk), lambda i, j, kk: (i, kk)), w_spec]
    args = [a, w]
    if mul2 is not None:
        in_specs.append(pl.BlockSpec((tm, tn), lambda i, j, kk: (i, j)))
        args.append(mul2)
    acc_shape = (tm, tn) if nk > 1 else (8, 128)
    (out,), per = _call(
        body, name=name, grid=(m // tm, n // tn, nk),
        in_specs=in_specs, out_specs=[pl.BlockSpec((tm, tn), lambda i, j, kk: (i, j))],
        out_shape=[jax.ShapeDtypeStruct((m, n), out_dtype)], args=args,
        scratch_shapes=[pltpu.VMEM(acc_shape, F32)],
        semantics=("parallel", "parallel", "arbitrary"), exchanges=exchanges)
    return (out, per) if exchanges else out


def _mm_tn(a, b, *, tm, tn, tt, name, a_square=False, n_split=1, exchanges=()):
    t, m = a.shape
    _, n = b.shape
    assert t % tt == 0 and m % tm == 0 and n % tn == 0 and (n // n_split) % tn == 0
    per = n // n_split // tn

    def body(a_ref, b_ref, o_ref):
        ti = pl.program_id(2)
        av = a_ref[...]
        if a_square:
            af = av.astype(F32)
            av = (af * af).astype(BF16)
        part = lax.dot_general(av, b_ref[...], TN, preferred_element_type=F32)

        @pl.when(ti == 0)
        def _():
            o_ref[...] = part

        @pl.when(ti > 0)
        def _():
            o_ref[...] += part

    (out,), xres = _call(
        body, name=name, grid=(m // tm, n // tn, t // tt),
        in_specs=[pl.BlockSpec((tt, tm), lambda i, j, ti: (ti, i)),
                  pl.BlockSpec((tt, tn), lambda i, j, ti: (ti, j))],
        out_specs=[pl.BlockSpec((None, tm, tn), lambda i, j, ti: (j // per, i, j % per))],
        out_shape=[jax.ShapeDtypeStruct((n_split, m, n // n_split), F32)], args=[a, b],
        semantics=("parallel", "parallel", "arbitrary"), exchanges=exchanges)
    return (out, xres) if exchanges else out


def _rstd(x):
    return lax.rsqrt(jnp.mean(x * x, axis=-1, keepdims=True) + EPS)


def _rms_cast(x, g, *, tm, name):
    t, d = x.shape

    def body(x_ref, g_ref, o_ref):
        xv = x_ref[...]
        o_ref[...] = (xv * _rstd(xv) * g_ref[...]).astype(BF16)

    return pl.pallas_call(
        body, name=name, grid=(t // tm,),
        in_specs=[pl.BlockSpec((tm, d), lambda i: (i, 0)), pl.BlockSpec((1, d), lambda i: (0, 0))],
        out_specs=pl.BlockSpec((tm, d), lambda i: (i, 0)),
        out_shape=jax.ShapeDtypeStruct((t, d), BF16),
        compiler_params=_params(("parallel",)),
    )(x, g)


def _mix_cat(attn, rnn, gain, *, tm, name):
    t = attn.shape[0]

    def body(a_ref, r_ref, g_ref, o_ref):
        av = a_ref[...]
        o_ref[:, :ATTN_W] = (av * _rstd(av) * g_ref[...]).astype(BF16)
        o_ref[:, ATTN_W:] = r_ref[...].astype(BF16)

    return pl.pallas_call(
        body, name=name, grid=(t // tm,),
        in_specs=[pl.BlockSpec((tm, ATTN_W), lambda i: (i, 0)), pl.BlockSpec((tm, RNN_W), lambda i: (i, 0)),
                  pl.BlockSpec((1, ATTN_W), lambda i: (0, 0))],
        out_specs=pl.BlockSpec((tm, D_MODEL), lambda i: (i, 0)),
        out_shape=jax.ShapeDtypeStruct((t, D_MODEL), BF16),
        compiler_params=_params(("parallel",)),
    )(attn, rnn, gain)


def _post_norm_res(mixed, g_post, res, g_next, *, tm, name):
    t, d = mixed.shape

    def body(m_ref, gp_ref, r_ref, gn_ref, x1_ref, h2_ref):
        mv = m_ref[...]
        x1 = r_ref[...] + mv * _rstd(mv) * gp_ref[...]
        x1_ref[...] = x1
        h2_ref[...] = (x1 * _rstd(x1) * gn_ref[...]).astype(BF16)

    row = pl.BlockSpec((tm, d), lambda i: (i, 0))
    vec = pl.BlockSpec((1, d), lambda i: (0, 0))
    return pl.pallas_call(
        body, name=name, grid=(t // tm,),
        in_specs=[row, vec, row, vec], out_specs=[row, row],
        out_shape=[jax.ShapeDtypeStruct((t, d), F32), jax.ShapeDtypeStruct((t, d), BF16)],
        compiler_params=_params(("parallel",)),
    )(mixed, g_post, res, g_next)


def _rms_bwd(dyn, xin, g, res, *, tm, out_dtype, name, col_block=0, exchanges=()):
    t, d = xin.shape

    def body(*refs):
        if res is not None:
            dy_ref, x_ref, g_ref, r_ref, dx_ref, dg_ref = refs
        else:
            dy_ref, x_ref, g_ref, dx_ref, dg_ref = refs
        i = pl.program_id(0)
        xv = x_ref[...]
        dy = dy_ref[...].astype(F32)
        r = _rstd(xv)
        xh = xv * r
        part = jnp.sum(dy * xh, axis=0, keepdims=True)

        @pl.when(i == 0)
        def _():
            dg_ref[...] = part

        @pl.when(i > 0)
        def _():
            dg_ref[...] += part

        tt = dy * g_ref[...]
        dx = r * (tt - xh * jnp.mean(tt * xh, axis=-1, keepdims=True))
        if res is not None:
            dx = dx + r_ref[...]
        dx_ref[...] = dx.astype(out_dtype)

    row = pl.BlockSpec((tm, d), lambda i: (i, 0))
    vec = pl.BlockSpec((1, d), lambda i: (0, 0))
    in_specs = [pl.BlockSpec((tm, d), lambda i: (i, col_block)), row, vec]
    args = [dyn, xin, g]
    if res is not None:
        in_specs.append(row)
        args.append(res)
    res, xres = _call(
        body, name=name, grid=(t // tm,),
        in_specs=in_specs, out_specs=[row, vec],
        out_shape=[jax.ShapeDtypeStruct((t, d), out_dtype), jax.ShapeDtypeStruct((1, d), F32)], args=args,
        semantics=("arbitrary",), exchanges=exchanges)
    return (*res, xres) if exchanges else res


def _loss_head(y, g_post, x1, target, *, tm, name):
    t, d = y.shape

    def body(y_ref, g_ref, x1_ref, t_ref, dy_ref, dx2_ref, loss_ref, dg_ref):
        i = pl.program_id(0)
        yv = y_ref[...]
        r = _rstd(yv)
        yh = yv * r
        gv = g_ref[...]
        err = x1_ref[...] + yh * gv - t_ref[...]
        lpart = 0.5 * jnp.sum(jnp.mean(err * err, axis=-1, keepdims=True), axis=0, keepdims=True)
        dx2 = err * (1.0 / d)
        dgp = jnp.sum(dx2 * yh, axis=0, keepdims=True)
        lane = lax.broadcasted_iota(jnp.int32, (1, 128), 1)
        lrow = jnp.where(lane == 0, lpart, 0.0)

        @pl.when(i == 0)
        def _():
            dg_ref[...] = dgp
            loss_ref[...] = lrow

        @pl.when(i > 0)
        def _():
            dg_ref[...] += dgp
            loss_ref[...] += lrow

        tt = dx2 * gv
        dy_ref[...] = (r * (tt - yh * jnp.mean(tt * yh, axis=-1, keepdims=True))).astype(BF16)
        dx2_ref[...] = dx2

    row = pl.BlockSpec((tm, d), lambda i: (i, 0))
    vec = pl.BlockSpec((1, d), lambda i: (0, 0))
    return pl.pallas_call(
        body, name=name, grid=(t // tm,),
        in_specs=[row, vec, row, row],
        out_specs=[row, row, pl.BlockSpec((1, 128), lambda i: (0, 0)), vec],
        out_shape=[jax.ShapeDtypeStruct((t, d), BF16), jax.ShapeDtypeStruct((t, d), F32),
                   jax.ShapeDtypeStruct((1, 128), F32), jax.ShapeDtypeStruct((1, d), F32)],
        compiler_params=_params(("arbitrary",)),
    )(y, g_post, x1, target)


def _alibi_slope(h):
    return 2.0 ** (-8.0 * (h + 1) / N_Q)


PAIR = 2 * HEAD_DIM
N_PAIRS = N_Q // 2
PAIRS_PER_KV = GROUP // 2
SMEM = pl.BlockSpec(memory_space=pltpu.SMEM)


def _swa_mask(n):
    key = lax.broadcasted_iota(jnp.int32, (2 * BLK, BLK), 0)
    qry = lax.broadcasted_iota(jnp.int32, (2 * BLK, BLK), 1)
    dist = qry + BLK - key
    valid = (dist >= 0) & (dist < BLK) & ((key >= BLK) | (n > 0))
    return valid, dist.astype(F32)


def _block_diag(kvp_ref, kvc_ref, off):
    a = jnp.concatenate([kvp_ref[:, off:off + HEAD_DIM], kvc_ref[:, off:off + HEAD_DIM]], axis=0).astype(BF16)
    z = jnp.zeros_like(a)
    return jnp.concatenate([jnp.concatenate([a, z], axis=1), jnp.concatenate([z, a], axis=1)], axis=0)


def _swa_scores(s2, e, hh, valid, distf):
    s = s2[2 * BLK * e:2 * BLK * (e + 1)] * (HEAD_DIM ** -0.5) - _alibi_slope(hh) * distf
    return jnp.where(valid, s, -1e30)


def _swa_fwd(proj, sinks, *, name, exchanges=()):
    t = proj.shape[0]
    nb = t // BLK
    kvb = KV_COL // (2 * 128)

    def body(sink_ref, q_ref, kvc_ref, kvp_ref, o_ref, lse_ref):
        n = pl.program_id(0)
        valid, distf = _swa_mask(n)
        for kvh in range(N_KV):
            k2 = _block_diag(kvp_ref, kvc_ref, kvh * HEAD_DIM)
            v2 = _block_diag(kvp_ref, kvc_ref, 128 + kvh * HEAD_DIM)
            for jp in range(PAIRS_PER_KV):
                pair = kvh * PAIRS_PER_KV + jp
                lanes = slice(pair * PAIR, (pair + 1) * PAIR)
                s2 = lax.dot_general(k2, q_ref[:, lanes].astype(BF16), NT, preferred_element_type=F32)
                probs = []
                for e in range(2):
                    hh = 2 * pair + e
                    s = _swa_scores(s2, e, hh, valid, distf)
                    sink = sink_ref[0, hh]
                    mx = jnp.maximum(jnp.max(s, axis=0, keepdims=True), sink)
                    p = jnp.exp(s - mx)
                    l = jnp.sum(p, axis=0, keepdims=True) + jnp.exp(sink - mx)
                    probs.append((p * (1.0 / l)).astype(BF16))
                    lse_ref[hh:hh + 1, :] = mx + jnp.log(l)
                o_ref[:, lanes] = lax.dot_general(jnp.concatenate(probs, axis=0), v2, TN,
                                                  preferred_element_type=F32)

    res, xres = _call(
        body, name=name, grid=(nb,),
        in_specs=[SMEM,
                  pl.BlockSpec((BLK, ATTN_W), lambda n: (n, 0)),
                  pl.BlockSpec((BLK, 256), lambda n: (n, kvb)),
                  pl.BlockSpec((BLK, 256), lambda n: (jnp.maximum(n - 1, 0), kvb))],
        out_specs=[pl.BlockSpec((BLK, ATTN_W), lambda n: (n, 0)),
                   pl.BlockSpec((None, N_Q, BLK), lambda n: (n, 0, 0))],
        out_shape=[jax.ShapeDtypeStruct((t, ATTN_W), F32), jax.ShapeDtypeStruct((nb, N_Q, BLK), F32)],
        args=[sinks, proj, proj, proj], semantics=("parallel",), exchanges=exchanges)
    return (*res, xres) if exchanges else res


def _swa_bwd(proj, sinks, dattn, lse, *, name, exchanges=()):
    t = proj.shape[0]
    nb = t // BLK
    kvb = KV_COL // (2 * 128)

    def body(sink_ref, q_ref, kvc_ref, kvp_ref, do_ref, lse_ref, dq_ref, dkv_ref, dsink_ref, carry_ref):
        n = pl.program_id(0)

        @pl.when(n == 0)
        def _():
            dsink_ref[...] = jnp.zeros_like(dsink_ref)
            carry_ref[...] = jnp.zeros_like(carry_ref)

        @pl.when(n < nb)
        def _():
            valid, distf = _swa_mask(n)
            for kvh in range(N_KV):
                k2 = _block_diag(kvp_ref, kvc_ref, kvh * HEAD_DIM)
                v2 = _block_diag(kvp_ref, kvc_ref, 128 + kvh * HEAD_DIM)
                dk2 = jnp.zeros((4 * BLK, PAIR), F32)
                dv2 = jnp.zeros((4 * BLK, PAIR), F32)
                for jp in range(PAIRS_PER_KV):
                    pair = kvh * PAIRS_PER_KV + jp
                    lanes = slice(pair * PAIR, (pair + 1) * PAIR)
                    q2 = q_ref[:, lanes].astype(BF16)
                    do2 = do_ref[:, lanes].astype(BF16)
                    s2 = lax.dot_general(k2, q2, NT, preferred_element_type=F32)
                    dp2 = lax.dot_general(v2, do2, NT, preferred_element_type=F32)
                    probs, dss = [], []
                    for e in range(2):
                        hh = 2 * pair + e
                        lse_h = lse_ref[hh:hh + 1, :]
                        p = jnp.exp(_swa_scores(s2, e, hh, valid, distf) - lse_h)
                        dp = dp2[2 * BLK * e:2 * BLK * (e + 1)]
                        delta = jnp.sum(p * dp, axis=0, keepdims=True)
                        dsink_ref[hh:hh + 1, :] += -jnp.exp(sink_ref[0, hh] - lse_h) * delta
                        probs.append(p.astype(BF16))
                        dss.append((p * (dp - delta)).astype(BF16))
                    ds2 = jnp.concatenate(dss, axis=0)
                    dq_ref[:, lanes] = lax.dot_general(ds2, k2, TN, preferred_element_type=F32) * (HEAD_DIM ** -0.5)
                    dk2 = dk2 + jnp.dot(ds2, q2, preferred_element_type=F32)
                    dv2 = dv2 + jnp.dot(jnp.concatenate(probs, axis=0), do2, preferred_element_type=F32)
                dk_cat = (dk2[:2 * BLK, :HEAD_DIM] + dk2[2 * BLK:, HEAD_DIM:]) * (HEAD_DIM ** -0.5)
                dv_cat = dv2[:2 * BLK, :HEAD_DIM] + dv2[2 * BLK:, HEAD_DIM:]
                ko = kvh * HEAD_DIM
                vo = 128 + kvh * HEAD_DIM
                dkv_ref[:, ko:ko + HEAD_DIM] = carry_ref[:, ko:ko + HEAD_DIM] + dk_cat[:BLK]
                dkv_ref[:, vo:vo + HEAD_DIM] = carry_ref[:, vo:vo + HEAD_DIM] + dv_cat[:BLK]
                carry_ref[:, ko:ko + HEAD_DIM] = dk_cat[BLK:]
                carry_ref[:, vo:vo + HEAD_DIM] = dv_cat[BLK:]

        @pl.when(n == nb)
        def _():
            dkv_ref[...] = carry_ref[...]

    last = nb - 1
    res, xres = _call(
        body, name=name, grid=(nb + 1,),
        in_specs=[SMEM,
                  pl.BlockSpec((BLK, ATTN_W), lambda n: (jnp.minimum(n, last), 0)),
                  pl.BlockSpec((BLK, 256), lambda n: (jnp.minimum(n, last), kvb)),
                  pl.BlockSpec((BLK, 256), lambda n: (jnp.maximum(jnp.minimum(n, last) - 1, 0), kvb)),
                  pl.BlockSpec((BLK, ATTN_W), lambda n: (jnp.minimum(n, last), 0)),
                  pl.BlockSpec((None, N_Q, BLK), lambda n: (jnp.minimum(n, last), 0, 0))],
        out_specs=[pl.BlockSpec((BLK, ATTN_W), lambda n: (jnp.minimum(n, last), 0)),
                   pl.BlockSpec((BLK, 256), lambda n: (jnp.maximum(n - 1, 0), 0)),
                   pl.BlockSpec((N_Q, BLK), lambda n: (0, 0))],
        out_shape=[jax.ShapeDtypeStruct((t, ATTN_W), F32), jax.ShapeDtypeStruct((t, 256), F32),
                   jax.ShapeDtypeStruct((N_Q, BLK), F32)],
        scratch_shapes=[pltpu.VMEM((BLK, 256), F32)],
        args=[sinks, proj, proj, proj, dattn, lse], semantics=("arbitrary",), exchanges=exchanges)
    return (*res, xres) if exchanges else res


def _cumsum_rows(x):
    n = x.shape[0]
    row = lax.broadcasted_iota(jnp.int32, x.shape, 0)
    s = 1
    while s < n:
        x = x + jnp.where(row >= s, pltpu.roll(x, s, axis=0), 0.0)
        s *= 2
    return x


def _rev_cumsum_rows(x):
    n = x.shape[0]
    row = lax.broadcasted_iota(jnp.int32, x.shape, 0)
    s = 1
    while s < n:
        x = x + jnp.where(row < n - s, pltpu.roll(x, n - s, axis=0), 0.0)
        s *= 2
    return x


def _lower_bound(lbl_ref):
    l0 = lbl_ref[0:1, :]
    l1 = lbl_ref[1:2, :]
    mx = jnp.maximum(l0, l1)
    e0 = jnp.exp(l0 - mx)
    e1 = jnp.exp(l1 - mx)
    return e0 / (e0 + e1)


def _hgrn_gates(z, lb):
    sg = _sigmoid(z)
    f = lb + (1.0 - lb) * sg
    return sg, f, jnp.log(f), 1.0 - f


def _sub_factors(b, i, sub):
    rows = lax.broadcasted_iota(jnp.int32, (CHUNK, RNN_HD), 0)
    ref = b[sub * i - 1:sub * i, :]
    qfac = jnp.exp(b[sub * i:sub * (i + 1), :] - ref)
    kfac = jnp.where(rows < sub * i, jnp.exp(ref - b), 0.0)
    return qfac, kfac


def _diag_decay(bi, s):
    trow = lax.broadcasted_iota(jnp.int32, bi.shape, 0)
    return jnp.where(trow >= s, jnp.exp(bi - bi[s:s + 1, :]), 0.0)


def _hgrn_fwd(proj, lb_logits, norm_gain, *, tb, name, exchanges=()):
    t = proj.shape[0]
    ntb = t // tb
    nch = tb // CHUNK
    qb, fb, ib, gb = QR_COL // 128, FR_COL // 128, IR_COL // 128, GR_COL // 128

    def body(q_ref, f_ref, i_ref, g_ref, lbl_ref, gain_ref, o_ref, out_ref, s0_ref, st_ref, ob_ref):
        c = pl.program_id(1)

        @pl.when(c == 0)
        def _():
            st_ref[...] = jnp.zeros_like(st_ref)

        lb = _lower_bound(lbl_ref)
        gain = gain_ref[...]

        def chunk(ci, carry):
            r0 = pl.multiple_of(ci * CHUNK, CHUNK)
            rows = pl.ds(r0, CHUNK)
            _, _, lf, k = _hgrn_gates(f_ref[rows, :], lb)
            qr = q_ref[rows, :]
            q = qr * _sigmoid(qr)
            v = i_ref[rows, :]
            b = _cumsum_rows(lf)
            st = st_ref[...]
            s0_ref[ci] = st
            ob_ref[...] = lax.dot_general((q * jnp.exp(b)).astype(BF16), st.astype(BF16), NT,
                                          preferred_element_type=F32)
            vb = v.astype(BF16)
            for i in range(CHUNK // SUB_FWD):
                blk = slice(SUB_FWD * i, SUB_FWD * (i + 1))
                qi, ki, vi, bi = q[blk], k[blk], v[blk], b[blk]
                oi = ob_ref[blk, :]
                if i > 0:
                    qfac, kfac = _sub_factors(b, i, SUB_FWD)
                    att = lax.dot_general((qi * qfac).astype(BF16), (k * kfac).astype(BF16), NT,
                                          preferred_element_type=F32)
                    oi = oi + jnp.dot(att.astype(BF16), vb, preferred_element_type=F32)
                for s in range(SUB_FWD):
                    qe = qi * _diag_decay(bi, s)
                    a = jnp.sum(qe * ki[s:s + 1, :], axis=1, keepdims=True)
                    oi = oi + a * vi[s:s + 1, :]
                ob_ref[blk, :] = oi
            blast = b[CHUNK - 1:CHUNK, :]
            khat = (k * jnp.exp(blast - b)).astype(BF16)
            st_ref[...] = st * jnp.exp(blast) + lax.dot_general(vb, khat, TN, preferred_element_type=F32)
            o = ob_ref[...]
            o_ref[rows, :] = o
            gr = g_ref[rows, :]
            out_ref[rows, :] = o * _rstd(o) * gain * (gr * _sigmoid(gr))
            return carry

        lax.fori_loop(0, nch, chunk, 0, unroll=True)

    def col(base):
        return pl.BlockSpec((tb, RNN_HD), lambda h, c: (c, base + h))

    res, xres = _call(
        body, name=name, grid=(N_RNN, ntb),
        in_specs=[col(qb), col(fb), col(ib), col(gb),
                  pl.BlockSpec((2, RNN_HD), lambda h, c: (0, h)), pl.BlockSpec((1, RNN_HD), lambda h, c: (0, 0))],
        out_specs=[pl.BlockSpec((tb, RNN_HD), lambda h, c: (c, h)), pl.BlockSpec((tb, RNN_HD), lambda h, c: (c, h)),
                   pl.BlockSpec((None, nch, RNN_HD, RNN_HD), lambda h, c: (h, c, 0, 0))],
        out_shape=[jax.ShapeDtypeStruct((t, RNN_W), F32), jax.ShapeDtypeStruct((t, RNN_W), F32),
                   jax.ShapeDtypeStruct((N_RNN, t // CHUNK, RNN_HD, RNN_HD), F32)],
        scratch_shapes=[pltpu.VMEM((RNN_HD, RNN_HD), F32), pltpu.VMEM((CHUNK, RNN_HD), F32)],
        args=[proj, proj, proj, proj, lb_logits, norm_gain],
        semantics=("parallel", "arbitrary"), exchanges=exchanges)
    return (*res, xres) if exchanges else res


def _hgrn_bwd(proj, lb_logits, norm_gain, o_pre, s0, dcat, *, tb, name, exchanges=()):
    t = proj.shape[0]
    ntb = t // tb
    nch = tb // CHUNK
    qb, fb, ib, gb = QR_COL // 128, FR_COL // 128, IR_COL // 128, GR_COL // 128
    sub = SUB_BWD
    nsub = CHUNK // sub

    def body(q_ref, f_ref, i_ref, g_ref, lbl_ref, gain_ref, o_ref, s0_ref, dout_ref,
             dq_ref, df_ref, di_ref, dg_ref, dlb_ref, dgain_ref,
             dst_ref, dqa_ref, dka_ref, dva_ref):
        c = pl.program_id(1)

        @pl.when(c == 0)
        def _():
            dst_ref[...] = jnp.zeros_like(dst_ref)
            dlb_ref[...] = jnp.zeros_like(dlb_ref)
            dgain_ref[...] = jnp.zeros_like(dgain_ref)

        lb = _lower_bound(lbl_ref)
        gain = gain_ref[...]

        def chunk(cj, carry):
            ci = nch - 1 - cj
            r0 = pl.multiple_of(ci * CHUNK, CHUNK)
            rows = pl.ds(r0, CHUNK)
            sg, f, lf, k = _hgrn_gates(f_ref[rows, :], lb)
            qr = q_ref[rows, :]
            sq = _sigmoid(qr)
            q = qr * sq
            v = i_ref[rows, :]
            b = _cumsum_rows(lf)

            dout = dout_ref[rows, :]
            o = o_ref[rows, :]
            gr = g_ref[rows, :]
            sgg = _sigmoid(gr)
            gate = gr * sgg
            rs = _rstd(o)
            nrm = o * rs
            dg_ref[rows, :] = dout * nrm * gain * (sgg * (1.0 + gr * (1.0 - sgg)))
            dn = dout * gate
            dgain_ref[...] += jnp.sum(dn * nrm, axis=0, keepdims=True)
            tt = dn * gain
            do = rs * (tt - nrm * jnp.mean(tt * nrm, axis=-1, keepdims=True))

            dob = do.astype(BF16)
            vb = v.astype(BF16)
            eb = jnp.exp(b)
            blast = b[CHUNK - 1:CHUNK, :]
            ebl = jnp.exp(blast - b)
            dst = dst_ref[...]
            dstb = dst.astype(BF16)
            khat = (k * ebl).astype(BF16)
            s0 = s0_ref[ci]
            dqa_ref[...] = eb * jnp.dot(dob, s0.astype(BF16), preferred_element_type=F32)
            dk_state = ebl * jnp.dot(vb, dstb, preferred_element_type=F32)
            dka_ref[...] = dk_state
            d_blast = (jnp.sum(k * dk_state, axis=0, keepdims=True)
                       + jnp.exp(blast) * jnp.sum(dst * s0, axis=0, keepdims=True))
            dva_ref[...] = lax.dot_general(khat, dstb, NT, preferred_element_type=F32)
            dst_ref[...] = dst * jnp.exp(blast) + lax.dot_general(dob, (q * eb).astype(BF16), TN,
                                                                  preferred_element_type=F32)
            pm = lax.dot_general(dob, vb, NT, preferred_element_type=F32)
            for i in range(nsub):
                blk = slice(sub * i, sub * (i + 1))
                qi, ki, vi, bi, doi = q[blk], k[blk], v[blk], b[blk], do[blk]
                dqi = dqa_ref[blk, :]
                if i > 0:
                    qfac, kfac = _sub_factors(b, i, sub)
                    qt = (qi * qfac).astype(BF16)
                    kt = (k * kfac).astype(BF16)
                    att = lax.dot_general(qt, kt, NT, preferred_element_type=F32).astype(BF16)
                    pmi = pm[blk, :].astype(BF16)
                    dva_ref[...] += lax.dot_general(att, doi.astype(BF16), TN, preferred_element_type=F32)
                    dqi = dqi + qfac * jnp.dot(pmi, kt, preferred_element_type=F32)
                    dka_ref[...] += kfac * lax.dot_general(pmi, qt, TN, preferred_element_type=F32)
                for s in range(sub):
                    e = _diag_decay(bi, s)
                    ks = ki[s:s + 1, :]
                    row = slice(sub * i + s, sub * i + s + 1)
                    a = jnp.sum(qi * e * ks, axis=1, keepdims=True)
                    pe = jnp.sum(doi * vi[s:s + 1, :], axis=1, keepdims=True) * e
                    dqi = dqi + pe * ks
                    dka_ref[row, :] += jnp.sum(pe * qi, axis=0, keepdims=True)
                    dva_ref[row, :] += jnp.sum(a * doi, axis=0, keepdims=True)
                dqa_ref[blk, :] = dqi

            dq = dqa_ref[...]
            dk = dka_ref[...]
            lastrow = lax.broadcasted_iota(jnp.int32, (CHUNK, RNN_HD), 0) == CHUNK - 1
            dlf = _rev_cumsum_rows(q * dq - k * dk + jnp.where(lastrow, d_blast, 0.0))
            dff = dlf / f - dk
            df_ref[rows, :] = dff * (1.0 - lb) * sg * (1.0 - sg)
            dlb_ref[...] += jnp.sum(dff * (1.0 - sg), axis=0, keepdims=True)
            dq_ref[rows, :] = dq * (sq * (1.0 + qr * (1.0 - sq)))
            di_ref[rows, :] = dva_ref[...]
            return carry

        lax.fori_loop(0, nch, chunk, 0, unroll=True)

    def col(base):
        return pl.BlockSpec((tb, RNN_HD), lambda h, c: (ntb - 1 - c, base + h))

    outc = pl.BlockSpec((tb, RNN_HD), lambda h, c: (ntb - 1 - c, h))
    hb = ATTN_W // RNN_HD
    res, xres = _call(
        body, name=name, grid=(N_RNN, ntb),
        in_specs=[col(qb), col(fb), col(ib), col(gb),
                  pl.BlockSpec((2, RNN_HD), lambda h, c: (0, h)), pl.BlockSpec((1, RNN_HD), lambda h, c: (0, 0)),
                  outc,
                  pl.BlockSpec((None, nch, RNN_HD, RNN_HD), lambda h, c: (h, ntb - 1 - c, 0, 0)),
                  pl.BlockSpec((tb, RNN_HD), lambda h, c: (ntb - 1 - c, hb + h))],
        out_specs=[outc, outc, outc, outc,
                   pl.BlockSpec((1, RNN_HD), lambda h, c: (0, h)),
                   pl.BlockSpec((None, 1, RNN_HD), lambda h, c: (h, 0, 0))],
        out_shape=[jax.ShapeDtypeStruct((t, RNN_W), F32)] * 4
        + [jax.ShapeDtypeStruct((1, RNN_W), F32), jax.ShapeDtypeStruct((N_RNN, 1, RNN_HD), F32)],
        scratch_shapes=[pltpu.VMEM((RNN_HD, RNN_HD), F32),
                        pltpu.VMEM((CHUNK, RNN_HD), F32), pltpu.VMEM((CHUNK, RNN_HD), F32),
                        pltpu.VMEM((CHUNK, RNN_HD), F32)],
        args=[proj, proj, proj, proj, lb_logits, norm_gain, o_pre, s0, dcat],
        semantics=("parallel", "arbitrary"), exchanges=exchanges)
    return (*res, xres) if exchanges else res


def _all_gather_halves(shards, *, name):
    n = len(shards)

    def body(*refs):
        ins, outs = refs[:n], refs[n:2 * n]
        send_sems, recv_sems = refs[2 * n:]
        x, y, c = _place()
        sibling = (x, y, 1 - c)
        chips = [(1 - x, y), (x, 1 - y), (1 - x, 1 - y)]

        def copy(a, k, block, to, src=None):
            slot = outs[a].at[4 * block[0] + 2 * block[1] + block[2]]
            return pltpu.make_async_remote_copy(
                src_ref=slot if src is None else src, dst_ref=slot,
                send_sem=send_sems.at[a, k], recv_sem=recv_sems.at[a, k],
                device_id=to, device_id_type=MESH)

        first, passed = [], []
        for a in range(n):
            for j, chip in enumerate(chips):
                cp = copy(a, j, (x, y, c), (*chip, c), src=ins[a].at[c])
                cp.start()
                first.append(cp)
        for a in range(n):
            for j, chip in enumerate(chips):
                copy(a, j, (*chip, c), (x, y, c)).wait_recv()
                cp = copy(a, 3 + j, (*chip, c), sibling)
                cp.start()
                passed.append(cp)
        for a in range(n):
            for j, chip in enumerate(chips):
                copy(a, 3 + j, (*chip, 1 - c), (x, y, c)).wait_recv()
        for cp in first + passed:
            cp.wait_send()

    return pl.pallas_call(
        body, name=name,
        in_specs=[ANY] * n, out_specs=[ANY] * n,
        out_shape=[jax.ShapeDtypeStruct((8,) + s.shape[1:], s.dtype) for s in shards],
        scratch_shapes=[pltpu.SemaphoreType.DMA((n, 6)), pltpu.SemaphoreType.DMA((n, 6))],
    )(*shards)


def _row_tile(rows, cols, budget=1 << 20):
    tr = rows
    while tr * cols > budget and tr % 16 == 0:
        tr //= 2
    return tr


def _pair_sum(g, sib, where, *, name):
    _, _, rh, cols = g.shape
    tr = _row_tile(rh, cols)

    def body(w_ref, g_ref, s_ref, o_ref):
        o_ref[...] = (g_ref[...] + s_ref[...]).astype(BF16)

    return pl.pallas_call(
        body, name=name,
        grid_spec=pltpu.PrefetchScalarGridSpec(
            num_scalar_prefetch=1, grid=(4, rh // tr),
            in_specs=[pl.BlockSpec((None, None, tr, cols), lambda s, i, w: (s, w[1], i, 0)),
                      pl.BlockSpec((None, tr, cols), lambda s, i, w: (s, i, 0))],
            out_specs=pl.BlockSpec((None, tr, cols), lambda s, i, w: (s, i, 0))),
        out_shape=jax.ShapeDtypeStruct((4, rh, cols), BF16),
        compiler_params=_params(("parallel", "parallel")),
    )(where, g, sib)


def _final_half(g, sib, recv, where, *, name):
    _, _, rh, cols = g.shape
    tr = _row_tile(rh, cols)

    def body(w_ref, g_ref, s_ref, r_ref, o_ref):
        acc = g_ref[...] + s_ref[...]
        for j in range(3):
            acc = acc + r_ref[j].astype(F32)
        o_ref[...] = acc

    return pl.pallas_call(
        body, name=name,
        grid_spec=pltpu.PrefetchScalarGridSpec(
            num_scalar_prefetch=1, grid=(rh // tr,),
            in_specs=[pl.BlockSpec((None, None, tr, cols), lambda i, w: (w[0], w[1], i, 0)),
                      pl.BlockSpec((None, tr, cols), lambda i, w: (w[0], i, 0)),
                      pl.BlockSpec((3, tr, cols), lambda i, w: (0, i, 0))],
            out_specs=pl.BlockSpec((tr, cols), lambda i, w: (i, 0))),
        out_shape=jax.ShapeDtypeStruct((rh, cols), F32),
        compiler_params=_params(("parallel",)),
    )(where, g, sib, recv)


def _adamw_math(w, g, m, v):
    m = ADAM_B1 * m + (1.0 - ADAM_B1) * g
    v = ADAM_B2 * v + (1.0 - ADAM_B2) * (g * g)
    m_hat = m / (1.0 - ADAM_B1 ** ADAM_STEP)
    v_hat = v / (1.0 - ADAM_B2 ** ADAM_STEP)
    delta = -ADAM_LR * (m_hat / (jnp.sqrt(v_hat) + ADAM_EPS) + ADAM_WD * w)
    return delta, m, v


def _adamw(w, mine, theirs, m, v, where, *, name):
    rows, cols = w.shape
    tr = _row_tile(rows // 2, cols, budget=1 << 19)
    nh = rows // 2 // tr

    def body(wh_ref, w_ref, a_ref, b_ref, m_ref, v_ref, g_ref, d_ref, nm_ref, nv_ref):
        g = jnp.where(pl.program_id(0) // nh == wh_ref[1], a_ref[...], b_ref[...])
        d, nm, nv = _adamw_math(w_ref[...], g, m_ref[...], v_ref[...])
        g_ref[...] = g
        d_ref[...] = d
        nm_ref[...] = nm
        nv_ref[...] = nv

    blk = pl.BlockSpec((tr, cols), lambda i, wh: (i, 0))
    half = pl.BlockSpec((tr, cols), lambda i, wh: (i % nh, 0))
    return pl.pallas_call(
        body, name=name,
        grid_spec=pltpu.PrefetchScalarGridSpec(
            num_scalar_prefetch=1, grid=(rows // tr,),
            in_specs=[blk, half, half, blk, blk], out_specs=[blk] * 4),
        out_shape=[jax.ShapeDtypeStruct((rows, cols), F32)] * 4,
        compiler_params=_params(("parallel",)),
    )(where, w, mine, theirs, m, v)


SEG_LOSS = 0
SEG_SINK = 128
SEG_AGAIN = 256
SEG_L0 = SEG_AGAIN + ATTN_W
SEG_L1 = SEG_L0 + RNN_W
SEG_RGAIN = SEG_L1 + RNN_W
SEG_G = SEG_RGAIN + 128
N_PACK = SEG_G + 4 * D_MODEL


def _pack(sinks, again, l0, l1, rgain, gains, loss=None):
    z = lambda k: jnp.zeros((1, k), F32)
    first = z(128) if loss is None else loss
    return jnp.concatenate([first, sinks, z(128 - N_Q), again, l0, l1, rgain] + list(gains), axis=1)


def _small_reduce_adamw(part, w, m, v, *, name):
    def body(p_ref, w_ref, m_ref, v_ref, g_ref, d_ref, nm_ref, nv_ref, buf_ref, send_sems, recv_sems):
        x, y, c = _place()
        me = 4 * x + 2 * y + c
        copies = []
        for k in range(1, 8):
            dx, dy, dc = (k >> 2) & 1, (k >> 1) & 1, k & 1
            to = (x ^ dx, y ^ dy, c ^ dc)
            cp = pltpu.make_async_remote_copy(
                src_ref=p_ref, dst_ref=buf_ref.at[me],
                send_sem=send_sems.at[k - 1], recv_sem=recv_sems.at[k - 1],
                device_id=to, device_id_type=MESH)
            cp.start()
            copies.append(cp)
        buf_ref[me] = p_ref[...]
        for cp in copies:
            cp.wait()
        tot = buf_ref[0]
        for j in range(1, 8):
            tot = tot + buf_ref[j]
        g_ref[...] = tot
        l0 = w_ref[:, SEG_L0:SEG_L0 + RNN_W]
        l1 = w_ref[:, SEG_L1:SEG_L1 + RNN_W]
        mx = jnp.maximum(l0, l1)
        e0 = jnp.exp(l0 - mx)
        e1 = jnp.exp(l1 - mx)
        lb = e0 / (e0 + e1)
        gl0 = tot[:, SEG_L0:SEG_L0 + RNN_W] * lb * (1.0 - lb)
        g_ref[:, SEG_L0:SEG_L0 + RNN_W] = gl0
        g_ref[:, SEG_L1:SEG_L1 + RNN_W] = -gl0
        d, nm, nv = _adamw_math(w_ref[...], g_ref[...], m_ref[...], v_ref[...])
        d_ref[...] = d
        nm_ref[...] = nm
        nv_ref[...] = nv

    vm = pl.BlockSpec(memory_space=pltpu.VMEM)
    return pl.pallas_call(
        body, name=name,
        in_specs=[vm] * 4, out_specs=[vm] * 4,
        out_shape=[jax.ShapeDtypeStruct((1, N_PACK), F32)] * 4,
        scratch_shapes=[pltpu.VMEM((8, 1, N_PACK), F32), pltpu.SemaphoreType.DMA((7,)),
                        pltpu.SemaphoreType.DMA((7,))],
    )(part, w, m, v)


def _layer_grads(xs, tgt, halves, where, sinks, again, lb_logits, rgain,
                 g_mix_pre, g_mix_post, g_mlp_pre, g_mlp_post):
    tm = 512
    h_in, h_out, h_up, h_dn = halves

    def whole(buf, own):
        return lax.dynamic_update_slice(buf, own, (2 * where[0], 0, 0))

    shard = IN_W // N_CHIPS
    w_in4 = whole(_all_gather_halves([h_in], name="gather_w_in")[0], h_in).reshape(N_CHIPS, D_MODEL, shard)
    w_in = w_in4.transpose(1, 0, 2).reshape(D_MODEL, IN_W)
    h1 = _rms_cast(xs, g_mix_pre, tm=tm, name="h1_norm")
    proj, ((b_out,),) = _mm(h1, w_in, tm=1024, tn=768, tk=D_MODEL, out_dtype=F32, name="in_proj",
                            exchanges=[_x_gather_ici([h_out])])
    attn, lse, ((b_up,), (b_out,)) = _swa_fwd(proj, sinks, name="swa_fwd",
                                              exchanges=[_x_gather_ici([h_up]), _x_gather_d2d([b_out])])
    w_out = whole(b_out, h_out).reshape(D_MODEL, D_MODEL)
    o_pre, rnn, s0, ((b_dn,), (b_up,)) = _hgrn_fwd(proj, lb_logits, rgain, tb=512, name="hgrn_fwd",
                                                   exchanges=[_x_gather_ici([h_dn]), _x_gather_d2d([b_up])])
    w_up4 = whole(b_up, h_up).reshape(N_CHIPS, D_MODEL, D_FF // N_CHIPS)
    cat = _mix_cat(attn, rnn, again, tm=tm, name="mix_cat")
    mixed, ((b_dn,),) = _mm(cat, w_out, tm=1024, tn=1024, tk=D_MODEL, out_dtype=F32, name="out_proj",
                            exchanges=[_x_gather_d2d([b_dn])])
    w_dn = whole(b_dn, h_dn).reshape(D_FF, D_MODEL)
    x1, h2 = _post_norm_res(mixed, g_mix_post, xs, g_mlp_pre, tm=256, name="mix_post")
    u = _mm(h2, w_up4, tm=1024, tn=1024, tk=D_MODEL, out_dtype=BF16, relu=True, w_layout="skn", name="mlp_up")
    yv = _mm(u, w_dn, tm=1024, tn=1024, tk=2048, out_dtype=F32, a_square=True, name="mlp_down")
    dy, dx2, loss_row, dg_mlp_post = _loss_head(yv, g_mlp_post, x1, tgt, tm=256, name="loss_head")

    def halved(g):
        return g.reshape(N_CHIPS, 2, g.shape[1] // 2, g.shape[2])
    du = _mm(dy, w_dn, tm=1024, tn=1024, tk=D_MODEL, out_dtype=BF16, mul2=u, w_layout="nk", name="mlp_down_bwd")
    g_dn = halved(_mm_tn(u, dy, tm=1024, tn=1024, tt=2048, a_square=True, name="w_down_grad")
                  .reshape(N_CHIPS, D_FF // N_CHIPS, D_MODEL))
    d_w_up, ((sib_dn,),) = _mm_tn(h2, du, tm=1024, tn=1024, tt=2048, n_split=N_CHIPS, name="w_up_grad",
                                  exchanges=[_x_pair([g_dn])])
    g_up = halved(d_w_up)
    wire_dn = _pair_sum(g_dn, sib_dn, where, name="pair_sum_w_down")
    dh2, ((recv_dn,), (sib_up,)) = _mm(du, w_up4, tm=1024, tn=1024, tk=2048, out_dtype=F32, w_layout="snk", name="mlp_up_bwd",
                                       exchanges=[_x_chip([wire_dn]), _x_pair([g_up])])
    wire_up = _pair_sum(g_up, sib_up, where, name="pair_sum_w_up")
    fin_dn = _final_half(g_dn, sib_dn, recv_dn, where, name="final_half_w_down")
    dx1, dg_mlp_pre = _rms_bwd(dh2, x1, g_mlp_pre, dx2, tm=256, out_dtype=F32, name="mlp_pre_bwd")
    dmixed, dg_mix_post = _rms_bwd(dx1, mixed, g_mix_post, None, tm=256, out_dtype=BF16, name="mix_post_bwd")
    d_w_out, ((oth_dn,),) = _mm_tn(cat, dmixed, tm=1024, tn=1024, tt=2048, name="w_out_grad",
                                   exchanges=[_x_share([fin_dn])])
    g_out = halved(d_w_out.reshape(N_CHIPS, D_MODEL // N_CHIPS, D_MODEL))
    dcat, ((sib_out,),) = _mm(dmixed, w_out, tm=1024, tn=1024, tk=D_MODEL, out_dtype=F32, w_layout="nk", name="out_proj_bwd",
                              exchanges=[_x_pair([g_out])])
    wire_out = _pair_sum(g_out, sib_out, where, name="pair_sum_w_out")
    dattn, dg_again = _rms_bwd(dcat, attn, again, None, tm=tm, out_dtype=F32, name="attn_norm_bwd")
    dq_a, dkv, dsinks, ((recv_out,),) = _swa_bwd(proj, sinks, dattn, lse, name="swa_bwd",
                                                 exchanges=[_x_chip([wire_out])])
    dq_r, df_r, di_r, dg_r, dlb, dgain_h, ((recv_up,),) = _hgrn_bwd(
        proj, lb_logits, rgain, o_pre, s0, dcat, tb=512, name="hgrn_bwd", exchanges=[_x_chip([wire_up])])
    fin_up = _final_half(g_up, sib_up, recv_up, where, name="final_half_w_up")
    fin_out = _final_half(g_out, sib_out, recv_out, where, name="final_half_w_out")
    dproj = jnp.concatenate([dq_a, dkv, dq_r, df_r, di_r, dg_r], axis=1).astype(BF16)
    d_w_in, ((oth_up, oth_out),) = _mm_tn(h1, dproj, tm=1024, tn=768, tt=2048, name="w_in_grad",
                                          exchanges=[_x_share([fin_up, fin_out])])
    g_in = halved(d_w_in[0].reshape(D_MODEL, N_CHIPS, shard).transpose(1, 0, 2))
    dh1, ((sib_in,),) = _mm(dproj, w_in, tm=1024, tn=1024, tk=2688, out_dtype=F32, w_layout="nk", name="in_proj_bwd",
                            exchanges=[_x_pair([g_in])])
    wire_in = _pair_sum(g_in, sib_in, where, name="pair_sum_w_in")
    gx, dg_mix_pre, ((recv_in,),) = _rms_bwd(dh1, xs, g_mix_pre, dx1, tm=256, out_dtype=F32, name="mix_pre_bwd",
                                             exchanges=[_x_chip([wire_in])])
    fin_in = _final_half(g_in, sib_in, recv_in, where, name="final_half_w_in")
    (oth_in,) = _run_exchange(_x_share([fin_in]), name="share_w_in")

    big = [(fin_in, oth_in), (fin_out, oth_out), (fin_up, oth_up), (fin_dn, oth_dn)]
    drgain = jnp.sum(dgain_h, axis=0)
    small = _pack(jnp.sum(dsinks, axis=1)[None, :], dg_again, dlb, jnp.zeros_like(dlb), drgain,
                  [dg_mix_pre, dg_mix_post, dg_mlp_pre, dg_mlp_post], loss=loss_row)
    return gx, big, small


def kernel(x, w_in, attn_sinks, attn_out_gain, rnn_lb_logits, rnn_norm_gain, w_out, mix_pre_gain, mix_post_gain, mlp_pre_gain, mlp_post_gain, w_up, w_down, loss_target, m_w_in, m_attn_sinks, m_attn_out_gain, m_rnn_lb_logits, m_rnn_norm_gain, m_w_out, m_mix_pre_gain, m_mix_post_gain, m_mlp_pre_gain, m_mlp_post_gain, m_w_up, m_w_down, v_w_in, v_attn_sinks, v_attn_out_gain, v_rnn_lb_logits, v_rnn_norm_gain, v_w_out, v_mix_pre_gain, v_mix_post_gain, v_mlp_pre_gain, v_mlp_post_gain, v_w_up, v_w_down):
    ax, ay, ac = _place()
    where = jnp.stack([2 * ax + ay, ac]).astype(jnp.int32)
    big_w = [w_in[0], w_out[0], w_up[0], w_down[0]]
    big_m = [m_w_in[0], m_w_out[0], m_w_up[0], m_w_down[0]]
    big_v = [v_w_in[0], v_w_out[0], v_w_up[0], v_w_down[0]]

    halves = [w.astype(BF16).reshape(2, w.shape[0] // 2, w.shape[1]) for w in big_w]
    gx, big_g, small_part = _layer_grads(
        x[0], loss_target[0], halves, where, attn_sinks, attn_out_gain, rnn_lb_logits, rnn_norm_gain,
        mix_pre_gain, mix_post_gain, mlp_pre_gain, mlp_post_gain)

    names = ["w_in", "w_out", "w_up", "w_down"]
    grads, deltas, new_m, new_v = [], [], [], []
    for (f, o), w, m, v, nm in zip(big_g, big_w, big_m, big_v, names):
        g, d, nm_, nv_ = _adamw(w, f, o, m, v, where, name="adamw_" + nm)
        grads.append(g[None])
        deltas.append(d[None])
        new_m.append(nm_[None])
        new_v.append(nv_[None])

    def pack_params(sinks, again, logits, rgain, gains):
        return _pack(sinks, again, logits[0:1], logits[1:2], rgain, gains)

    pw = pack_params(attn_sinks, attn_out_gain, rnn_lb_logits, rnn_norm_gain,
                     [mix_pre_gain, mix_post_gain, mlp_pre_gain, mlp_post_gain])
    pm = pack_params(m_attn_sinks, m_attn_out_gain, m_rnn_lb_logits, m_rnn_norm_gain,
                     [m_mix_pre_gain, m_mix_post_gain, m_mlp_pre_gain, m_mlp_post_gain])
    pv = pack_params(v_attn_sinks, v_attn_out_gain, v_rnn_lb_logits, v_rnn_norm_gain,
                     [v_mix_pre_gain, v_mix_post_gain, v_mlp_pre_gain, v_mlp_post_gain])
    packs = _small_reduce_adamw(small_part, pw, pm, pv, name="small_reduce_adamw")

    def unpack(p):
        seg = lambda o, k: p[:, o:o + k]
        logits = jnp.concatenate([seg(SEG_L0, RNN_W), seg(SEG_L1, RNN_W)], axis=0)
        gains = [seg(SEG_G + i * D_MODEL, D_MODEL) for i in range(4)]
        return dict(sinks=seg(SEG_SINK, N_Q), again=seg(SEG_AGAIN, ATTN_W), logits=logits,
                    rgain=seg(SEG_RGAIN, RNN_HD), gains=gains)

    def order(small, big):
        return [big[0], small["sinks"], small["again"], small["logits"], small["rgain"], big[1],
                *small["gains"], big[2], big[3]]

    loss = packs[0][0, 0]
    outs = [loss, gx[None]]
    for p, b in zip(packs, [grads, deltas, new_m, new_v]):
        outs += order(unpack(p), b)
    return tuple(outs)
```

```python
import functools

import jax
import jax.numpy as jnp
from jax import lax
from jax.experimental import pallas as pl
from jax.experimental.pallas import tpu as pltpu

F32 = jnp.float32
BF16 = jnp.bfloat16
MESH = pl.DeviceIdType.MESH

EPS = 1e-6
D_MODEL = 2048
ATTN_W = 1024
HEAD_DIM = 64
N_Q = 16
N_KV = 2
GROUP = 8
BLK = 128
RNN_W = 1024
RNN_HD = 128
N_RNN = 8
CHUNK = 64
SUB_FWD = 16
SUB_BWD = 8
D_FF = 8192
IN_W = 5376
N_CHIPS = 4
KV_COL = ATTN_W
QR_COL = ATTN_W + 2 * 128
FR_COL = QR_COL + RNN_W
IR_COL = FR_COL + RNN_W
GR_COL = IR_COL + RNN_W

ADAM_LR = 0.001
ADAM_B1 = 0.9
ADAM_B2 = 0.999
ADAM_EPS = 1e-08
ADAM_WD = 0.01
ADAM_STEP = 10

VMEM_LIMIT = 48 * 1024 * 1024

NT = (((1,), (1,)), ((), ()))
TN = (((0,), (0,)), ((), ()))


def _params(sem=None):
    return pltpu.CompilerParams(dimension_semantics=sem, vmem_limit_bytes=VMEM_LIMIT)


def _sigmoid(x):
    return 1.0 / (1.0 + jnp.exp(-x))


ANY = pl.BlockSpec(memory_space=pl.ANY)


def _place():
    return lax.axis_index("x"), lax.axis_index("y"), lax.axis_index("c")


def _other_chips(x, y):
    return [(1 - x, y), (x, 1 - y), (1 - x, 1 - y)]


class _Exchange:
    def __init__(self, srcs, outs, ncopy, build, aliases=None):
        self.srcs, self.outs, self.ncopy, self.build = list(srcs), list(outs), ncopy, build
        self.aliases = aliases or {}


def _remote(src, dst, send_sems, recv_sems, k, to):
    return pltpu.make_async_remote_copy(src_ref=src, dst_ref=dst, send_sem=send_sems.at[k],
                                        recv_sem=recv_sems.at[k], device_id=to, device_id_type=MESH)


def _call(body, *, name, grid, in_specs, out_specs, out_shape, args, scratch_shapes=(), semantics=None,
          exchanges=()):
    in_specs, out_specs, out_shape = list(in_specs), list(out_specs), list(out_shape)
    scratch_shapes = list(scratch_shapes)
    ni, no, ns = len(in_specs), len(out_specs), len(scratch_shapes)
    xsrc = [s for x in exchanges for s in x.srcs]
    xout = [o for x in exchanges for o in x.outs]
    nxi, nxo = len(xsrc), len(xout)
    aliases = {}
    a0 = b0 = 0
    for x in exchanges:
        for si, oi in x.aliases.items():
            aliases[ni + a0 + si] = no + b0 + oi
        a0 += len(x.srcs)
        b0 += len(x.outs)
    sems = []
    for x in exchanges:
        sems += [pltpu.SemaphoreType.DMA((x.ncopy,)), pltpu.SemaphoreType.DMA((x.ncopy,))]

    def wrapped(*refs):
        ins, xi = refs[:ni], refs[ni:ni + nxi]
        outs, xo = refs[ni + nxi:ni + nxi + no], refs[ni + nxi + no:ni + nxi + no + nxo]
        rest = refs[ni + nxi + no + nxo:]
        scr, sm = rest[:ns], rest[ns:]

        def copies():
            cps = []
            a = b = 0
            for k, x in enumerate(exchanges):
                cps += x.build(xi[a:a + len(x.srcs)], xo[b:b + len(x.outs)], sm[2 * k], sm[2 * k + 1])
                a += len(x.srcs)
                b += len(x.outs)
            return cps

        def start():
            for cp in copies():
                cp.start()

        def wait():
            for cp in copies():
                cp.wait()

        if not exchanges:
            body(*ins, *outs, *scr)
        elif not grid:
            start()
            body(*ins, *outs, *scr)
            wait()
        else:
            first = last = None
            for ax, g in enumerate(grid):
                f = pl.program_id(ax) == 0
                l = pl.program_id(ax) == g - 1
                first = f if first is None else first & f
                last = l if last is None else last & l
            pl.when(first)(start)
            body(*ins, *outs, *scr)
            pl.when(last)(wait)

    if exchanges and semantics is not None:
        semantics = ("arbitrary",) * len(grid)
    kwargs = dict(grid=grid) if grid else {}
    res = pl.pallas_call(
        wrapped, name=name,
        in_specs=in_specs + [ANY] * nxi, out_specs=out_specs + [ANY] * nxo,
        out_shape=out_shape + xout, scratch_shapes=scratch_shapes + sems,
        input_output_aliases=aliases,
        compiler_params=_params(semantics), **kwargs,
    )(*args, *xsrc)
    res = list(res)
    mine, theirs = res[:no], res[no:]
    per = []
    b = 0
    for x in exchanges:
        per.append(theirs[b:b + len(x.outs)])
        b += len(x.outs)
    return mine, per


def _run_exchange(x, *, name):
    return _call(lambda: None, name=name, grid=(), in_specs=[], out_specs=[], out_shape=[], args=[],
                 exchanges=[x])[1][0]


def _x_gather_ici(halves, rows=None, into=None):
    n = len(halves)
    rows = rows or [(0, h.shape[1]) for h in halves]

    def build(srcs, outs, ss, rs):
        x, y, c = _place()
        cps = []
        for a in range(n):
            piece = pl.ds(*rows[a])
            for j, (px, py) in enumerate(_other_chips(x, y)):
                cps.append(_remote(srcs[a].at[c, piece], outs[a].at[4 * x + 2 * y + c, piece], ss, rs,
                                   3 * a + j, (px, py, c)))
        return cps

    outs = [jax.ShapeDtypeStruct((8,) + h.shape[1:], h.dtype) for h in halves]
    if into is None:
        return _Exchange(halves, outs, 3 * n, build)
    return _Exchange(list(halves) + list(into), outs, 3 * n, build, aliases={n + a: a for a in range(n)})


def _x_gather_d2d(bufs):
    n = len(bufs)

    def build(srcs, outs, ss, rs):
        x, y, c = _place()
        cps = []
        for a in range(n):
            for j, (px, py) in enumerate(_other_chips(x, y)):
                slot = 4 * px + 2 * py + c
                cps.append(_remote(srcs[a].at[slot], outs[a].at[slot], ss, rs, 3 * a + j, (x, y, 1 - c)))
        return cps

    outs = [jax.ShapeDtypeStruct(b.shape, b.dtype) for b in bufs]
    return _Exchange(bufs, outs, 3 * n, build, aliases={a: a for a in range(n)})


def _x_pair(grads):
    n = len(grads)

    def build(srcs, outs, ss, rs):
        x, y, c = _place()
        return [_remote(srcs[a].at[:, 1 - c], outs[a], ss, rs, a, (x, y, 1 - c)) for a in range(n)]

    outs = [jax.ShapeDtypeStruct((4,) + g.shape[2:], g.dtype) for g in grads]
    return _Exchange(grads, outs, n, build)


def _x_chip(wires, rows=None, into=None):
    n = len(wires)
    rows = rows or [(0, w.shape[1]) for w in wires]

    def build(srcs, outs, ss, rs):
        x, y, c = _place()
        cps = []
        for a in range(n):
            piece = pl.ds(*rows[a])
            for j, (px, py) in enumerate(_other_chips(x, y)):
                cps.append(_remote(srcs[a].at[2 * px + py, piece], outs[a].at[j, piece], ss, rs,
                                   3 * a + j, (px, py, c)))
        return cps

    outs = [jax.ShapeDtypeStruct((3,) + w.shape[1:], w.dtype) for w in wires]
    if into is None:
        return _Exchange(wires, outs, 3 * n, build)
    return _Exchange(list(wires) + list(into), outs, 3 * n, build, aliases={n + a: a for a in range(n)})


def _x_share(halves):
    n = len(halves)

    def build(srcs, outs, ss, rs):
        x, y, c = _place()
        return [_remote(srcs[a], outs[a], ss, rs, a, (x, y, 1 - c)) for a in range(n)]

    outs = [jax.ShapeDtypeStruct(h.shape, h.dtype) for h in halves]
    return _Exchange(halves, outs, n, build)


def _mm(a, w, *, tm, tn, tk, out_dtype, name, a_square=False, relu=False, mul2=None, w_layout="kn",
        exchanges=()):
    m, k = a.shape
    if w_layout == "kn":
        n = w.shape[1]
        w_spec = pl.BlockSpec((tk, tn), lambda i, j, kk: (kk, j))
    elif w_layout == "nk":
        n = w.shape[0]
        w_spec = pl.BlockSpec((tn, tk), lambda i, j, kk: (j, kk))
    elif w_layout == "skn":
        n = w.shape[0] * w.shape[2]
        per_n = w.shape[2] // tn
        w_spec = pl.BlockSpec((None, tk, tn), lambda i, j, kk: (j // per_n, kk, j % per_n))
    else:
        assert w_layout == "snk"
        n = w.shape[1]
        per_k = w.shape[2] // tk
        w_spec = pl.BlockSpec((None, tn, tk), lambda i, j, kk: (kk // per_k, j, kk % per_k))
    w_dims = NT if w_layout in ("nk", "snk") else (((1,), (0,)), ((), ()))
    nk = k // tk
    assert m % tm == 0 and n % tn == 0 and k % tk == 0

    def body(*refs):
        if mul2 is not None:
            a_ref, w_ref, e_ref, o_ref, acc_ref = refs
        else:
            a_ref, w_ref, o_ref, acc_ref = refs
            e_ref = None
        kk = pl.program_id(2)
        av = a_ref[...]
        if a_square:
            af = av.astype(F32)
            av = (af * af).astype(BF16)
        part = lax.dot_general(av, w_ref[...], w_dims, preferred_element_type=F32)

        def finish(r):
            if relu:
                r = jnp.maximum(r, 0.0)
            if e_ref is not None:
                r = 2.0 * e_ref[...].astype(F32) * r
            o_ref[...] = r.astype(out_dtype)

        if nk == 1:
            finish(part)
        else:
            @pl.when(kk == 0)
            def _():
                acc_ref[...] = part

            @pl.when(kk > 0)
            def _():
                acc_ref[...] += part

            @pl.when(kk == nk - 1)
            def _():
                finish(acc_ref[...])

    in_specs = [pl.BlockSpec((tm, tk), lambda i, j, kk: (i, kk)), w_spec]
    args = [a, w]
    if mul2 is not None:
        in_specs.append(pl.BlockSpec((tm, tn), lambda i, j, kk: (i, j)))
        args.append(mul2)
    acc_shape = (tm, tn) if nk > 1 else (8, 128)
    (out,), per = _call(
        body, name=name, grid=(m // tm, n // tn, nk),
        in_specs=in_specs, out_specs=[pl.BlockSpec((tm, tn), lambda i, j, kk: (i, j))],
        out_shape=[jax.ShapeDtypeStruct((m, n), out_dtype)], args=args,
        scratch_shapes=[pltpu.VMEM(acc_shape, F32)],
        semantics=("parallel", "parallel", "arbitrary"), exchanges=exchanges)
    return (out, per) if exchanges else out


def _mm_tn(a, b, *, tm, tn, tt, name, a_square=False, n_split=1, m_blocks=None, exchanges=()):
    t, m = a.shape
    _, n = b.shape
    assert t % tt == 0 and m % tm == 0 and n % tn == 0 and (n // n_split) % tn == 0
    per = n // n_split // tn
    count, stride, first = m_blocks or (m // tm, 1, 0)
    m = count * tm

    def body(a_ref, b_ref, o_ref):
        ti = pl.program_id(2)
        av = a_ref[...]
        if a_square:
            af = av.astype(F32)
            av = (af * af).astype(BF16)
        part = lax.dot_general(av, b_ref[...], TN, preferred_element_type=F32)

        @pl.when(ti == 0)
        def _():
            o_ref[...] = part

        @pl.when(ti > 0)
        def _():
            o_ref[...] += part

    (out,), xres = _call(
        body, name=name, grid=(m // tm, n // tn, t // tt),
        in_specs=[pl.BlockSpec((tt, tm), lambda i, j, ti: (ti, first + stride * i)),
                  pl.BlockSpec((tt, tn), lambda i, j, ti: (ti, j))],
        out_specs=[pl.BlockSpec((None, tm, tn), lambda i, j, ti: (j // per, i, j % per))],
        out_shape=[jax.ShapeDtypeStruct((n_split, m, n // n_split), F32)], args=[a, b],
        semantics=("parallel", "parallel", "arbitrary"), exchanges=exchanges)
    return (out, xres) if exchanges else out


def _rstd(x):
    return lax.rsqrt(jnp.mean(x * x, axis=-1, keepdims=True) + EPS)


def _rms_cast(x, g, *, tm, name):
    t, d = x.shape

    def body(x_ref, g_ref, o_ref):
        xv = x_ref[...]
        o_ref[...] = (xv * _rstd(xv) * g_ref[...]).astype(BF16)

    return pl.pallas_call(
        body, name=name, grid=(t // tm,),
        in_specs=[pl.BlockSpec((tm, d), lambda i: (i, 0)), pl.BlockSpec((1, d), lambda i: (0, 0))],
        out_specs=pl.BlockSpec((tm, d), lambda i: (i, 0)),
        out_shape=jax.ShapeDtypeStruct((t, d), BF16),
        compiler_params=_params(("parallel",)),
    )(x, g)


def _mix_cat(attn, rnn, gain, *, tm, name):
    t = attn.shape[0]

    def body(a_ref, r_ref, g_ref, o_ref):
        av = a_ref[...]
        o_ref[:, :ATTN_W] = (av * _rstd(av) * g_ref[...]).astype(BF16)
        o_ref[:, ATTN_W:] = r_ref[...].astype(BF16)

    return pl.pallas_call(
        body, name=name, grid=(t // tm,),
        in_specs=[pl.BlockSpec((tm, ATTN_W), lambda i: (i, 0)), pl.BlockSpec((tm, RNN_W), lambda i: (i, 0)),
                  pl.BlockSpec((1, ATTN_W), lambda i: (0, 0))],
        out_specs=pl.BlockSpec((tm, D_MODEL), lambda i: (i, 0)),
        out_shape=jax.ShapeDtypeStruct((t, D_MODEL), BF16),
        compiler_params=_params(("parallel",)),
    )(attn, rnn, gain)


def _post_norm_res(mixed, g_post, res, g_next, *, tm, name):
    t, d = mixed.shape

    def body(m_ref, gp_ref, r_ref, gn_ref, x1_ref, h2_ref):
        mv = m_ref[...]
        x1 = r_ref[...] + mv * _rstd(mv) * gp_ref[...]
        x1_ref[...] = x1
        h2_ref[...] = (x1 * _rstd(x1) * gn_ref[...]).astype(BF16)

    row = pl.BlockSpec((tm, d), lambda i: (i, 0))
    vec = pl.BlockSpec((1, d), lambda i: (0, 0))
    return pl.pallas_call(
        body, name=name, grid=(t // tm,),
        in_specs=[row, vec, row, vec], out_specs=[row, row],
        out_shape=[jax.ShapeDtypeStruct((t, d), F32), jax.ShapeDtypeStruct((t, d), BF16)],
        compiler_params=_params(("parallel",)),
    )(mixed, g_post, res, g_next)


def _rms_bwd(dyn, xin, g, res, *, tm, out_dtype, name, col_block=0, exchanges=()):
    t, d = xin.shape

    def body(*refs):
        if res is not None:
            dy_ref, x_ref, g_ref, r_ref, dx_ref, dg_ref = refs
        else:
            dy_ref, x_ref, g_ref, dx_ref, dg_ref = refs
        i = pl.program_id(0)
        xv = x_ref[...]
        dy = dy_ref[...].astype(F32)
        r = _rstd(xv)
        xh = xv * r
        part = jnp.sum(dy * xh, axis=0, keepdims=True)

        @pl.when(i == 0)
        def _():
            dg_ref[...] = part

        @pl.when(i > 0)
        def _():
            dg_ref[...] += part

        tt = dy * g_ref[...]
        dx = r * (tt - xh * jnp.mean(tt * xh, axis=-1, keepdims=True))
        if res is not None:
            dx = dx + r_ref[...]
        dx_ref[...] = dx.astype(out_dtype)

    row = pl.BlockSpec((tm, d), lambda i: (i, 0))
    vec = pl.BlockSpec((1, d), lambda i: (0, 0))
    in_specs = [pl.BlockSpec((tm, d), lambda i: (i, col_block)), row, vec]
    args = [dyn, xin, g]
    if res is not None:
        in_specs.append(row)
        args.append(res)
    res, xres = _call(
        body, name=name, grid=(t // tm,),
        in_specs=in_specs, out_specs=[row, vec],
        out_shape=[jax.ShapeDtypeStruct((t, d), out_dtype), jax.ShapeDtypeStruct((1, d), F32)], args=args,
        semantics=("arbitrary",), exchanges=exchanges)
    return (*res, xres) if exchanges else res


def _loss_head(y, g_post, x1, target, *, tm, name):
    t, d = y.shape

    def body(y_ref, g_ref, x1_ref, t_ref, dy_ref, dx2_ref, loss_ref, dg_ref):
        i = pl.program_id(0)
        yv = y_ref[...]
        r = _rstd(yv)
        yh = yv * r
        gv = g_ref[...]
        err = x1_ref[...] + yh * gv - t_ref[...]
        lpart = 0.5 * jnp.sum(jnp.mean(err * err, axis=-1, keepdims=True), axis=0, keepdims=True)
        dx2 = err * (1.0 / d)
        dgp = jnp.sum(dx2 * yh, axis=0, keepdims=True)
        lane = lax.broadcasted_iota(jnp.int32, (1, 128), 1)
        lrow = jnp.where(lane == 0, lpart, 0.0)

        @pl.when(i == 0)
        def _():
            dg_ref[...] = dgp
            loss_ref[...] = lrow

        @pl.when(i > 0)
        def _():
            dg_ref[...] += dgp
            loss_ref[...] += lrow

        tt = dx2 * gv
        dy_ref[...] = (r * (tt - yh * jnp.mean(tt * yh, axis=-1, keepdims=True))).astype(BF16)
        dx2_ref[...] = dx2

    row = pl.BlockSpec((tm, d), lambda i: (i, 0))
    vec = pl.BlockSpec((1, d), lambda i: (0, 0))
    return pl.pallas_call(
        body, name=name, grid=(t // tm,),
        in_specs=[row, vec, row, row],
        out_specs=[row, row, pl.BlockSpec((1, 128), lambda i: (0, 0)), vec],
        out_shape=[jax.ShapeDtypeStruct((t, d), BF16), jax.ShapeDtypeStruct((t, d), F32),
                   jax.ShapeDtypeStruct((1, 128), F32), jax.ShapeDtypeStruct((1, d), F32)],
        compiler_params=_params(("arbitrary",)),
    )(y, g_post, x1, target)


def _alibi_slope(h):
    return 2.0 ** (-8.0 * (h + 1) / N_Q)


PAIR = 2 * HEAD_DIM
N_PAIRS = N_Q // 2
PAIRS_PER_KV = GROUP // 2
SMEM = pl.BlockSpec(memory_space=pltpu.SMEM)


def _swa_mask(n):
    key = lax.broadcasted_iota(jnp.int32, (2 * BLK, BLK), 0)
    qry = lax.broadcasted_iota(jnp.int32, (2 * BLK, BLK), 1)
    dist = qry + BLK - key
    valid = (dist >= 0) & (dist < BLK) & ((key >= BLK) | (n > 0))
    return valid, dist.astype(F32)


def _block_diag(kvp_ref, kvc_ref, off):
    a = jnp.concatenate([kvp_ref[:, off:off + HEAD_DIM], kvc_ref[:, off:off + HEAD_DIM]], axis=0).astype(BF16)
    z = jnp.zeros_like(a)
    return jnp.concatenate([jnp.concatenate([a, z], axis=1), jnp.concatenate([z, a], axis=1)], axis=0)


def _swa_scores(s2, e, hh, valid, distf):
    s = s2[2 * BLK * e:2 * BLK * (e + 1)] * (HEAD_DIM ** -0.5) - _alibi_slope(hh) * distf
    return jnp.where(valid, s, -1e30)


def _swa_fwd(proj, sinks, *, name, exchanges=()):
    t = proj.shape[0]
    nb = t // BLK
    kvb = KV_COL // (2 * 128)

    def body(sink_ref, q_ref, kvc_ref, kvp_ref, o_ref, lse_ref):
        n = pl.program_id(0)
        valid, distf = _swa_mask(n)
        for kvh in range(N_KV):
            k2 = _block_diag(kvp_ref, kvc_ref, kvh * HEAD_DIM)
            v2 = _block_diag(kvp_ref, kvc_ref, 128 + kvh * HEAD_DIM)
            for jp in range(PAIRS_PER_KV):
                pair = kvh * PAIRS_PER_KV + jp
                lanes = slice(pair * PAIR, (pair + 1) * PAIR)
                s2 = lax.dot_general(k2, q_ref[:, lanes].astype(BF16), NT, preferred_element_type=F32)
                probs = []
                for e in range(2):
                    hh = 2 * pair + e
                    s = _swa_scores(s2, e, hh, valid, distf)
                    sink = sink_ref[0, hh]
                    mx = jnp.maximum(jnp.max(s, axis=0, keepdims=True), sink)
                    p = jnp.exp(s - mx)
                    l = jnp.sum(p, axis=0, keepdims=True) + jnp.exp(sink - mx)
                    probs.append((p * (1.0 / l)).astype(BF16))
                    lse_ref[hh:hh + 1, :] = mx + jnp.log(l)
                o_ref[:, lanes] = lax.dot_general(jnp.concatenate(probs, axis=0), v2, TN,
                                                  preferred_element_type=F32)

    res, xres = _call(
        body, name=name, grid=(nb,),
        in_specs=[SMEM,
                  pl.BlockSpec((BLK, ATTN_W), lambda n: (n, 0)),
                  pl.BlockSpec((BLK, 256), lambda n: (n, kvb)),
                  pl.BlockSpec((BLK, 256), lambda n: (jnp.maximum(n - 1, 0), kvb))],
        out_specs=[pl.BlockSpec((BLK, ATTN_W), lambda n: (n, 0)),
                   pl.BlockSpec((None, N_Q, BLK), lambda n: (n, 0, 0))],
        out_shape=[jax.ShapeDtypeStruct((t, ATTN_W), F32), jax.ShapeDtypeStruct((nb, N_Q, BLK), F32)],
        args=[sinks, proj, proj, proj], semantics=("parallel",), exchanges=exchanges)
    return (*res, xres) if exchanges else res


def _swa_bwd(proj, sinks, dattn, lse, *, name, exchanges=()):
    t = proj.shape[0]
    nb = t // BLK
    kvb = KV_COL // (2 * 128)

    def body(sink_ref, q_ref, kvc_ref, kvp_ref, do_ref, lse_ref, dq_ref, dkv_ref, dsink_ref, carry_ref):
        n = pl.program_id(0)

        @pl.when(n == 0)
        def _():
            dsink_ref[...] = jnp.zeros_like(dsink_ref)
            carry_ref[...] = jnp.zeros_like(carry_ref)

        @pl.when(n < nb)
        def _():
            valid, distf = _swa_mask(n)
            for kvh in range(N_KV):
                k2 = _block_diag(kvp_ref, kvc_ref, kvh * HEAD_DIM)
                v2 = _block_diag(kvp_ref, kvc_ref, 128 + kvh * HEAD_DIM)
                dk2 = jnp.zeros((4 * BLK, PAIR), F32)
                dv2 = jnp.zeros((4 * BLK, PAIR), F32)
                for jp in range(PAIRS_PER_KV):
                    pair = kvh * PAIRS_PER_KV + jp
                    lanes = slice(pair * PAIR, (pair + 1) * PAIR)
                    q2 = q_ref[:, lanes].astype(BF16)
                    do2 = do_ref[:, lanes].astype(BF16)
                    s2 = lax.dot_general(k2, q2, NT, preferred_element_type=F32)
                    dp2 = lax.dot_general(v2, do2, NT, preferred_element_type=F32)
                    probs, dss = [], []
                    for e in range(2):
                        hh = 2 * pair + e
                        lse_h = lse_ref[hh:hh + 1, :]
                        p = jnp.exp(_swa_scores(s2, e, hh, valid, distf) - lse_h)
                        dp = dp2[2 * BLK * e:2 * BLK * (e + 1)]
                        delta = jnp.sum(p * dp, axis=0, keepdims=True)
                        dsink_ref[hh:hh + 1, :] += -jnp.exp(sink_ref[0, hh] - lse_h) * delta
                        probs.append(p.astype(BF16))
                        dss.append((p * (dp - delta)).astype(BF16))
                    ds2 = jnp.concatenate(dss, axis=0)
                    dq_ref[:, lanes] = lax.dot_general(ds2, k2, TN, preferred_element_type=F32) * (HEAD_DIM ** -0.5)
                    dk2 = dk2 + jnp.dot(ds2, q2, preferred_element_type=F32)
                    dv2 = dv2 + jnp.dot(jnp.concatenate(probs, axis=0), do2, preferred_element_type=F32)
                dk_cat = (dk2[:2 * BLK, :HEAD_DIM] + dk2[2 * BLK:, HEAD_DIM:]) * (HEAD_DIM ** -0.5)
                dv_cat = dv2[:2 * BLK, :HEAD_DIM] + dv2[2 * BLK:, HEAD_DIM:]
                ko = kvh * HEAD_DIM
                vo = 128 + kvh * HEAD_DIM
                dkv_ref[:, ko:ko + HEAD_DIM] = carry_ref[:, ko:ko + HEAD_DIM] + dk_cat[:BLK]
                dkv_ref[:, vo:vo + HEAD_DIM] = carry_ref[:, vo:vo + HEAD_DIM] + dv_cat[:BLK]
                carry_ref[:, ko:ko + HEAD_DIM] = dk_cat[BLK:]
                carry_ref[:, vo:vo + HEAD_DIM] = dv_cat[BLK:]

        @pl.when(n == nb)
        def _():
            dkv_ref[...] = carry_ref[...]

    last = nb - 1
    res, xres = _call(
        body, name=name, grid=(nb + 1,),
        in_specs=[SMEM,
                  pl.BlockSpec((BLK, ATTN_W), lambda n: (jnp.minimum(n, last), 0)),
                  pl.BlockSpec((BLK, 256), lambda n: (jnp.minimum(n, last), kvb)),
                  pl.BlockSpec((BLK, 256), lambda n: (jnp.maximum(jnp.minimum(n, last) - 1, 0), kvb)),
                  pl.BlockSpec((BLK, ATTN_W), lambda n: (jnp.minimum(n, last), 0)),
                  pl.BlockSpec((None, N_Q, BLK), lambda n: (jnp.minimum(n, last), 0, 0))],
        out_specs=[pl.BlockSpec((BLK, ATTN_W), lambda n: (jnp.minimum(n, last), 0)),
                   pl.BlockSpec((BLK, 256), lambda n: (jnp.maximum(n - 1, 0), 0)),
                   pl.BlockSpec((N_Q, BLK), lambda n: (0, 0))],
        out_shape=[jax.ShapeDtypeStruct((t, ATTN_W), F32), jax.ShapeDtypeStruct((t, 256), F32),
                   jax.ShapeDtypeStruct((N_Q, BLK), F32)],
        scratch_shapes=[pltpu.VMEM((BLK, 256), F32)],
        args=[sinks, proj, proj, proj, dattn, lse], semantics=("arbitrary",), exchanges=exchanges)
    return (*res, xres) if exchanges else res


def _cumsum_rows(x):
    n = x.shape[0]
    row = lax.broadcasted_iota(jnp.int32, x.shape, 0)
    s = 1
    while s < n:
        x = x + jnp.where(row >= s, pltpu.roll(x, s, axis=0), 0.0)
        s *= 2
    return x


def _rev_cumsum_rows(x):
    n = x.shape[0]
    row = lax.broadcasted_iota(jnp.int32, x.shape, 0)
    s = 1
    while s < n:
        x = x + jnp.where(row < n - s, pltpu.roll(x, n - s, axis=0), 0.0)
        s *= 2
    return x


def _lower_bound(lbl_ref):
    l0 = lbl_ref[0:1, :]
    l1 = lbl_ref[1:2, :]
    mx = jnp.maximum(l0, l1)
    e0 = jnp.exp(l0 - mx)
    e1 = jnp.exp(l1 - mx)
    return e0 / (e0 + e1)


def _hgrn_gates(z, lb):
    sg = _sigmoid(z)
    f = lb + (1.0 - lb) * sg
    return sg, f, jnp.log(f), 1.0 - f


def _sub_factors(b, i, sub):
    rows = lax.broadcasted_iota(jnp.int32, (CHUNK, RNN_HD), 0)
    ref = b[sub * i - 1:sub * i, :]
    qfac = jnp.exp(b[sub * i:sub * (i + 1), :] - ref)
    kfac = jnp.where(rows < sub * i, jnp.exp(ref - b), 0.0)
    return qfac, kfac


def _diag_decay(bi, s):
    trow = lax.broadcasted_iota(jnp.int32, bi.shape, 0)
    return jnp.where(trow >= s, jnp.exp(bi - bi[s:s + 1, :]), 0.0)


def _hgrn_fwd(proj, lb_logits, norm_gain, *, tb, name, exchanges=()):
    t = proj.shape[0]
    ntb = t // tb
    nch = tb // CHUNK
    qb, fb, ib, gb = QR_COL // 128, FR_COL // 128, IR_COL // 128, GR_COL // 128

    def body(q_ref, f_ref, i_ref, g_ref, lbl_ref, gain_ref, o_ref, out_ref, s0_ref, st_ref, ob_ref):
        c = pl.program_id(1)

        @pl.when(c == 0)
        def _():
            st_ref[...] = jnp.zeros_like(st_ref)

        lb = _lower_bound(lbl_ref)
        gain = gain_ref[...]

        def chunk(ci, carry):
            r0 = pl.multiple_of(ci * CHUNK, CHUNK)
            rows = pl.ds(r0, CHUNK)
            _, _, lf, k = _hgrn_gates(f_ref[rows, :], lb)
            qr = q_ref[rows, :]
            q = qr * _sigmoid(qr)
            v = i_ref[rows, :]
            b = _cumsum_rows(lf)
            st = st_ref[...]
            s0_ref[ci] = st
            ob_ref[...] = lax.dot_general((q * jnp.exp(b)).astype(BF16), st.astype(BF16), NT,
                                          preferred_element_type=F32)
            vb = v.astype(BF16)
            for i in range(CHUNK // SUB_FWD):
                blk = slice(SUB_FWD * i, SUB_FWD * (i + 1))
                qi, ki, vi, bi = q[blk], k[blk], v[blk], b[blk]
                oi = ob_ref[blk, :]
                if i > 0:
                    qfac, kfac = _sub_factors(b, i, SUB_FWD)
                    att = lax.dot_general((qi * qfac).astype(BF16), (k * kfac).astype(BF16), NT,
                                          preferred_element_type=F32)
                    oi = oi + jnp.dot(att.astype(BF16), vb, preferred_element_type=F32)
                for s in range(SUB_FWD):
                    qe = qi * _diag_decay(bi, s)
                    a = jnp.sum(qe * ki[s:s + 1, :], axis=1, keepdims=True)
                    oi = oi + a * vi[s:s + 1, :]
                ob_ref[blk, :] = oi
            blast = b[CHUNK - 1:CHUNK, :]
            khat = (k * jnp.exp(blast - b)).astype(BF16)
            st_ref[...] = st * jnp.exp(blast) + lax.dot_general(vb, khat, TN, preferred_element_type=F32)
            o = ob_ref[...]
            o_ref[rows, :] = o
            gr = g_ref[rows, :]
            out_ref[rows, :] = o * _rstd(o) * gain * (gr * _sigmoid(gr))
            return carry

        lax.fori_loop(0, nch, chunk, 0, unroll=True)

    def col(base):
        return pl.BlockSpec((tb, RNN_HD), lambda h, c: (c, base + h))

    res, xres = _call(
        body, name=name, grid=(N_RNN, ntb),
        in_specs=[col(qb), col(fb), col(ib), col(gb),
                  pl.BlockSpec((2, RNN_HD), lambda h, c: (0, h)), pl.BlockSpec((1, RNN_HD), lambda h, c: (0, 0))],
        out_specs=[pl.BlockSpec((tb, RNN_HD), lambda h, c: (c, h)), pl.BlockSpec((tb, RNN_HD), lambda h, c: (c, h)),
                   pl.BlockSpec((None, nch, RNN_HD, RNN_HD), lambda h, c: (h, c, 0, 0))],
        out_shape=[jax.ShapeDtypeStruct((t, RNN_W), F32), jax.ShapeDtypeStruct((t, RNN_W), F32),
                   jax.ShapeDtypeStruct((N_RNN, t // CHUNK, RNN_HD, RNN_HD), F32)],
        scratch_shapes=[pltpu.VMEM((RNN_HD, RNN_HD), F32), pltpu.VMEM((CHUNK, RNN_HD), F32)],
        args=[proj, proj, proj, proj, lb_logits, norm_gain],
        semantics=("parallel", "arbitrary"), exchanges=exchanges)
    return (*res, xres) if exchanges else res


def _hgrn_bwd(proj, lb_logits, norm_gain, o_pre, s0, dcat, *, tb, name, exchanges=()):
    t = proj.shape[0]
    ntb = t // tb
    nch = tb // CHUNK
    qb, fb, ib, gb = QR_COL // 128, FR_COL // 128, IR_COL // 128, GR_COL // 128
    sub = SUB_BWD
    nsub = CHUNK // sub

    def body(q_ref, f_ref, i_ref, g_ref, lbl_ref, gain_ref, o_ref, s0_ref, dout_ref,
             dq_ref, df_ref, di_ref, dg_ref, dlb_ref, dgain_ref,
             dst_ref, dqa_ref, dka_ref, dva_ref):
        c = pl.program_id(1)

        @pl.when(c == 0)
        def _():
            dst_ref[...] = jnp.zeros_like(dst_ref)
            dlb_ref[...] = jnp.zeros_like(dlb_ref)
            dgain_ref[...] = jnp.zeros_like(dgain_ref)

        lb = _lower_bound(lbl_ref)
        gain = gain_ref[...]

        def chunk(cj, carry):
            ci = nch - 1 - cj
            r0 = pl.multiple_of(ci * CHUNK, CHUNK)
            rows = pl.ds(r0, CHUNK)
            sg, f, lf, k = _hgrn_gates(f_ref[rows, :], lb)
            qr = q_ref[rows, :]
            sq = _sigmoid(qr)
            q = qr * sq
            v = i_ref[rows, :]
            b = _cumsum_rows(lf)

            dout = dout_ref[rows, :]
            o = o_ref[rows, :]
            gr = g_ref[rows, :]
            sgg = _sigmoid(gr)
            gate = gr * sgg
            rs = _rstd(o)
            nrm = o * rs
            dg_ref[rows, :] = dout * nrm * gain * (sgg * (1.0 + gr * (1.0 - sgg)))
            dn = dout * gate
            dgain_ref[...] += jnp.sum(dn * nrm, axis=0, keepdims=True)
            tt = dn * gain
            do = rs * (tt - nrm * jnp.mean(tt * nrm, axis=-1, keepdims=True))

            dob = do.astype(BF16)
            vb = v.astype(BF16)
            eb = jnp.exp(b)
            blast = b[CHUNK - 1:CHUNK, :]
            ebl = jnp.exp(blast - b)
            dst = dst_ref[...]
            dstb = dst.astype(BF16)
            khat = (k * ebl).astype(BF16)
            s0 = s0_ref[ci]
            dqa_ref[...] = eb * jnp.dot(dob, s0.astype(BF16), preferred_element_type=F32)
            dk_state = ebl * jnp.dot(vb, dstb, preferred_element_type=F32)
            dka_ref[...] = dk_state
            d_blast = (jnp.sum(k * dk_state, axis=0, keepdims=True)
                       + jnp.exp(blast) * jnp.sum(dst * s0, axis=0, keepdims=True))
            dva_ref[...] = lax.dot_general(khat, dstb, NT, preferred_element_type=F32)
            dst_ref[...] = dst * jnp.exp(blast) + lax.dot_general(dob, (q * eb).astype(BF16), TN,
                                                                  preferred_element_type=F32)
            pm = lax.dot_general(dob, vb, NT, preferred_element_type=F32)
            for i in range(nsub):
                blk = slice(sub * i, sub * (i + 1))
                qi, ki, vi, bi, doi = q[blk], k[blk], v[blk], b[blk], do[blk]
                dqi = dqa_ref[blk, :]
                if i > 0:
                    qfac, kfac = _sub_factors(b, i, sub)
                    qt = (qi * qfac).astype(BF16)
                    kt = (k * kfac).astype(BF16)
                    att = lax.dot_general(qt, kt, NT, preferred_element_type=F32).astype(BF16)
                    pmi = pm[blk, :].astype(BF16)
                    dva_ref[...] += lax.dot_general(att, doi.astype(BF16), TN, preferred_element_type=F32)
                    dqi = dqi + qfac * jnp.dot(pmi, kt, preferred_element_type=F32)
                    dka_ref[...] += kfac * lax.dot_general(pmi, qt, TN, preferred_element_type=F32)
                for s in range(sub):
                    e = _diag_decay(bi, s)
                    ks = ki[s:s + 1, :]
                    row = slice(sub * i + s, sub * i + s + 1)
                    a = jnp.sum(qi * e * ks, axis=1, keepdims=True)
                    pe = jnp.sum(doi * vi[s:s + 1, :], axis=1, keepdims=True) * e
                    dqi = dqi + pe * ks
                    dka_ref[row, :] += jnp.sum(pe * qi, axis=0, keepdims=True)
                    dva_ref[row, :] += jnp.sum(a * doi, axis=0, keepdims=True)
                dqa_ref[blk, :] = dqi

            dq = dqa_ref[...]
            dk = dka_ref[...]
            lastrow = lax.broadcasted_iota(jnp.int32, (CHUNK, RNN_HD), 0) == CHUNK - 1
            dlf = _rev_cumsum_rows(q * dq - k * dk + jnp.where(lastrow, d_blast, 0.0))
            dff = dlf / f - dk
            df_ref[rows, :] = dff * (1.0 - lb) * sg * (1.0 - sg)
            dlb_ref[...] += jnp.sum(dff * (1.0 - sg), axis=0, keepdims=True)
            dq_ref[rows, :] = dq * (sq * (1.0 + qr * (1.0 - sq)))
            di_ref[rows, :] = dva_ref[...]
            return carry

        lax.fori_loop(0, nch, chunk, 0, unroll=True)

    def col(base):
        return pl.BlockSpec((tb, RNN_HD), lambda h, c: (ntb - 1 - c, base + h))

    outc = pl.BlockSpec((tb, RNN_HD), lambda h, c: (ntb - 1 - c, h))
    hb = ATTN_W // RNN_HD
    res, xres = _call(
        body, name=name, grid=(N_RNN, ntb),
        in_specs=[col(qb), col(fb), col(ib), col(gb),
                  pl.BlockSpec((2, RNN_HD), lambda h, c: (0, h)), pl.BlockSpec((1, RNN_HD), lambda h, c: (0, 0)),
                  outc,
                  pl.BlockSpec((None, nch, RNN_HD, RNN_HD), lambda h, c: (h, ntb - 1 - c, 0, 0)),
                  pl.BlockSpec((tb, RNN_HD), lambda h, c: (ntb - 1 - c, hb + h))],
        out_specs=[outc, outc, outc, outc,
                   pl.BlockSpec((1, RNN_HD), lambda h, c: (0, h)),
                   pl.BlockSpec((None, 1, RNN_HD), lambda h, c: (h, 0, 0))],
        out_shape=[jax.ShapeDtypeStruct((t, RNN_W), F32)] * 4
        + [jax.ShapeDtypeStruct((1, RNN_W), F32), jax.ShapeDtypeStruct((N_RNN, 1, RNN_HD), F32)],
        scratch_shapes=[pltpu.VMEM((RNN_HD, RNN_HD), F32),
                        pltpu.VMEM((CHUNK, RNN_HD), F32), pltpu.VMEM((CHUNK, RNN_HD), F32),
                        pltpu.VMEM((CHUNK, RNN_HD), F32)],
        args=[proj, proj, proj, proj, lb_logits, norm_gain, o_pre, s0, dcat],
        semantics=("parallel", "arbitrary"), exchanges=exchanges)
    return (*res, xres) if exchanges else res


def _all_gather_halves(shards, *, name):
    n = len(shards)

    def body(*refs):
        ins, outs = refs[:n], refs[n:2 * n]
        send_sems, recv_sems = refs[2 * n:]
        x, y, c = _place()
        sibling = (x, y, 1 - c)
        chips = [(1 - x, y), (x, 1 - y), (1 - x, 1 - y)]

        def copy(a, k, block, to, src=None):
            slot = outs[a].at[4 * block[0] + 2 * block[1] + block[2]]
            return pltpu.make_async_remote_copy(
                src_ref=slot if src is None else src, dst_ref=slot,
                send_sem=send_sems.at[a, k], recv_sem=recv_sems.at[a, k],
                device_id=to, device_id_type=MESH)

        first, passed = [], []
        for a in range(n):
            for j, chip in enumerate(chips):
                cp = copy(a, j, (x, y, c), (*chip, c), src=ins[a].at[c])
                cp.start()
                first.append(cp)
        for a in range(n):
            for j, chip in enumerate(chips):
                copy(a, j, (*chip, c), (x, y, c)).wait_recv()
                cp = copy(a, 3 + j, (*chip, c), sibling)
                cp.start()
                passed.append(cp)
        for a in range(n):
            for j, chip in enumerate(chips):
                copy(a, 3 + j, (*chip, 1 - c), (x, y, c)).wait_recv()
        for cp in first + passed:
            cp.wait_send()

    return pl.pallas_call(
        body, name=name,
        in_specs=[ANY] * n, out_specs=[ANY] * n,
        out_shape=[jax.ShapeDtypeStruct((8,) + s.shape[1:], s.dtype) for s in shards],
        scratch_shapes=[pltpu.SemaphoreType.DMA((n, 6)), pltpu.SemaphoreType.DMA((n, 6))],
    )(*shards)


def _row_tile(rows, cols, budget=1 << 20):
    tr = rows
    while tr * cols > budget and tr % 16 == 0:
        tr //= 2
    return tr


def _pair_sum(g, sib, where, *, name):
    _, _, rh, cols = g.shape
    tr = _row_tile(rh, cols)

    def body(w_ref, g_ref, s_ref, o_ref):
        o_ref[...] = (g_ref[...] + s_ref[...]).astype(BF16)

    return pl.pallas_call(
        body, name=name,
        grid_spec=pltpu.PrefetchScalarGridSpec(
            num_scalar_prefetch=1, grid=(4, rh // tr),
            in_specs=[pl.BlockSpec((None, None, tr, cols), lambda s, i, w: (s, w[1], i, 0)),
                      pl.BlockSpec((None, tr, cols), lambda s, i, w: (s, i, 0))],
            out_specs=pl.BlockSpec((None, tr, cols), lambda s, i, w: (s, i, 0))),
        out_shape=jax.ShapeDtypeStruct((4, rh, cols), BF16),
        compiler_params=_params(("parallel", "parallel")),
    )(where, g, sib)


def _final_half(g, sib, recv, where, *, name):
    _, _, rh, cols = g.shape
    tr = _row_tile(rh, cols)

    def body(w_ref, g_ref, s_ref, r_ref, o_ref):
        acc = g_ref[...] + s_ref[...]
        for j in range(3):
            acc = acc + r_ref[j].astype(F32)
        o_ref[...] = acc

    return pl.pallas_call(
        body, name=name,
        grid_spec=pltpu.PrefetchScalarGridSpec(
            num_scalar_prefetch=1, grid=(rh // tr,),
            in_specs=[pl.BlockSpec((None, None, tr, cols), lambda i, w: (w[0], w[1], i, 0)),
                      pl.BlockSpec((None, tr, cols), lambda i, w: (w[0], i, 0)),
                      pl.BlockSpec((3, tr, cols), lambda i, w: (0, i, 0))],
            out_specs=pl.BlockSpec((tr, cols), lambda i, w: (i, 0))),
        out_shape=jax.ShapeDtypeStruct((rh, cols), F32),
        compiler_params=_params(("parallel",)),
    )(where, g, sib, recv)


def _adamw_math(w, g, m, v):
    m = ADAM_B1 * m + (1.0 - ADAM_B1) * g
    v = ADAM_B2 * v + (1.0 - ADAM_B2) * (g * g)
    m_hat = m / (1.0 - ADAM_B1 ** ADAM_STEP)
    v_hat = v / (1.0 - ADAM_B2 ** ADAM_STEP)
    delta = -ADAM_LR * (m_hat / (jnp.sqrt(v_hat) + ADAM_EPS) + ADAM_WD * w)
    return delta, m, v


def _adamw(w, mine, theirs, m, v, where, *, name):
    rows, cols = w.shape
    tr = _row_tile(rows // 2, cols, budget=1 << 19)
    nh = rows // 2 // tr

    def body(wh_ref, w_ref, a_ref, b_ref, m_ref, v_ref, g_ref, d_ref, nm_ref, nv_ref):
        g = jnp.where(pl.program_id(0) // nh == wh_ref[1], a_ref[...], b_ref[...])
        d, nm, nv = _adamw_math(w_ref[...], g, m_ref[...], v_ref[...])
        g_ref[...] = g
        d_ref[...] = d
        nm_ref[...] = nm
        nv_ref[...] = nv

    blk = pl.BlockSpec((tr, cols), lambda i, wh: (i, 0))
    half = pl.BlockSpec((tr, cols), lambda i, wh: (i % nh, 0))
    return pl.pallas_call(
        body, name=name,
        grid_spec=pltpu.PrefetchScalarGridSpec(
            num_scalar_prefetch=1, grid=(rows // tr,),
            in_specs=[blk, half, half, blk, blk], out_specs=[blk] * 4),
        out_shape=[jax.ShapeDtypeStruct((rows, cols), F32)] * 4,
        compiler_params=_params(("parallel",)),
    )(where, w, mine, theirs, m, v)


SEG_LOSS = 0
SEG_SINK = 128
SEG_AGAIN = 256
SEG_L0 = SEG_AGAIN + ATTN_W
SEG_L1 = SEG_L0 + RNN_W
SEG_RGAIN = SEG_L1 + RNN_W
SEG_G = SEG_RGAIN + 128
N_PACK = SEG_G + 4 * D_MODEL


def _pack(sinks, again, l0, l1, rgain, gains, loss=None):
    z = lambda k: jnp.zeros((1, k), F32)
    first = z(128) if loss is None else loss
    return jnp.concatenate([first, sinks, z(128 - N_Q), again, l0, l1, rgain] + list(gains), axis=1)


def _small_reduce_adamw(part, w, m, v, *, name):
    def body(p_ref, w_ref, m_ref, v_ref, g_ref, d_ref, nm_ref, nv_ref, buf_ref, send_sems, recv_sems):
        x, y, c = _place()
        me = 4 * x + 2 * y + c
        copies = []
        for k in range(1, 8):
            dx, dy, dc = (k >> 2) & 1, (k >> 1) & 1, k & 1
            to = (x ^ dx, y ^ dy, c ^ dc)
            cp = pltpu.make_async_remote_copy(
                src_ref=p_ref, dst_ref=buf_ref.at[me],
                send_sem=send_sems.at[k - 1], recv_sem=recv_sems.at[k - 1],
                device_id=to, device_id_type=MESH)
            cp.start()
            copies.append(cp)
        buf_ref[me] = p_ref[...]
        for cp in copies:
            cp.wait()
        tot = buf_ref[0]
        for j in range(1, 8):
            tot = tot + buf_ref[j]
        g_ref[...] = tot
        l0 = w_ref[:, SEG_L0:SEG_L0 + RNN_W]
        l1 = w_ref[:, SEG_L1:SEG_L1 + RNN_W]
        mx = jnp.maximum(l0, l1)
        e0 = jnp.exp(l0 - mx)
        e1 = jnp.exp(l1 - mx)
        lb = e0 / (e0 + e1)
        gl0 = tot[:, SEG_L0:SEG_L0 + RNN_W] * lb * (1.0 - lb)
        g_ref[:, SEG_L0:SEG_L0 + RNN_W] = gl0
        g_ref[:, SEG_L1:SEG_L1 + RNN_W] = -gl0
        d, nm, nv = _adamw_math(w_ref[...], g_ref[...], m_ref[...], v_ref[...])
        d_ref[...] = d
        nm_ref[...] = nm
        nv_ref[...] = nv

    vm = pl.BlockSpec(memory_space=pltpu.VMEM)
    return pl.pallas_call(
        body, name=name,
        in_specs=[vm] * 4, out_specs=[vm] * 4,
        out_shape=[jax.ShapeDtypeStruct((1, N_PACK), F32)] * 4,
        scratch_shapes=[pltpu.VMEM((8, 1, N_PACK), F32), pltpu.SemaphoreType.DMA((7,)),
                        pltpu.SemaphoreType.DMA((7,))],
    )(part, w, m, v)


def _layer_grads(xs, tgt, halves, where, sinks, again, lb_logits, rgain,
                 g_mix_pre, g_mix_post, g_mlp_pre, g_mlp_post):
    tm = 512
    h_in, h_out, h_up, h_dn = halves

    def whole(buf, own):
        return lax.dynamic_update_slice(buf, own, (2 * where[0], 0, 0))

    shard = IN_W // N_CHIPS
    w_in4 = whole(_all_gather_halves([h_in], name="gather_w_in")[0], h_in).reshape(N_CHIPS, D_MODEL, shard)
    w_in = w_in4.transpose(1, 0, 2).reshape(D_MODEL, IN_W)
    h1 = _rms_cast(xs, g_mix_pre, tm=tm, name="h1_norm")
    proj, ((b_out,), (b_up,)) = _mm(
        h1, w_in, tm=1024, tn=768, tk=D_MODEL, out_dtype=F32, name="in_proj",
        exchanges=[_x_gather_ici([h_out]), _x_gather_ici([h_up], rows=[(0, 384)])])
    attn, lse, ((b_up,), (b_out,)) = _swa_fwd(
        proj, sinks, name="swa_fwd",
        exchanges=[_x_gather_ici([h_up], rows=[(384, 320)], into=[b_up]), _x_gather_d2d([b_out])])
    w_out = whole(b_out, h_out).reshape(D_MODEL, D_MODEL)
    o_pre, rnn, s0, ((b_up,), (b_dn,)) = _hgrn_fwd(
        proj, lb_logits, rgain, tb=512, name="hgrn_fwd",
        exchanges=[_x_gather_ici([h_up], rows=[(704, 320)], into=[b_up]), _x_gather_ici([h_dn], rows=[(0, 960)])])
    cat = _mix_cat(attn, rnn, again, tm=tm, name="mix_cat")
    mixed, ((b_up,), (b_dn,)) = _mm(
        cat, w_out, tm=1024, tn=1024, tk=D_MODEL, out_dtype=F32, name="out_proj",
        exchanges=[_x_gather_d2d([b_up]), _x_gather_ici([h_dn], rows=[(960, 64)], into=[b_dn])])
    w_up4 = whole(b_up, h_up).reshape(N_CHIPS, D_MODEL, D_FF // N_CHIPS)
    x1, h2 = _post_norm_res(mixed, g_mix_post, xs, g_mlp_pre, tm=256, name="mix_post")
    u, ((b_dn,),) = _mm(h2, w_up4, tm=1024, tn=1024, tk=D_MODEL, out_dtype=BF16, relu=True, w_layout="skn",
                        name="mlp_up", exchanges=[_x_gather_d2d([b_dn])])
    w_dn = whole(b_dn, h_dn).reshape(D_FF, D_MODEL)
    yv = _mm(u, w_dn, tm=1024, tn=1024, tk=2048, out_dtype=F32, a_square=True, name="mlp_down")
    dy, dx2, loss_row, dg_mlp_post = _loss_head(yv, g_mlp_post, x1, tgt, tm=256, name="loss_head")

    def halved(g):
        return g.reshape(N_CHIPS, 2, g.shape[1] // 2, g.shape[2])
    du = _mm(dy, w_dn, tm=1024, tn=1024, tk=D_MODEL, out_dtype=BF16, mul2=u, w_layout="nk", name="mlp_down_bwd")
    g_dn = halved(_mm_tn(u, dy, tm=1024, tn=1024, tt=2048, a_square=True, name="w_down_grad")
                  .reshape(N_CHIPS, D_FF // N_CHIPS, D_MODEL))
    d_w_up, ((sib_dn,),) = _mm_tn(h2, du, tm=1024, tn=1024, tt=2048, n_split=N_CHIPS, name="w_up_grad",
                                  exchanges=[_x_pair([g_dn])])
    g_up = halved(d_w_up)
    wire_dn = _pair_sum(g_dn, sib_dn, where, name="pair_sum_w_down")
    dh2, ((recv_dn,), (sib_up,)) = _mm(du, w_up4, tm=1024, tn=1024, tk=2048, out_dtype=F32, w_layout="snk", name="mlp_up_bwd",
                                       exchanges=[_x_chip([wire_dn], rows=[(0, 800)]), _x_pair([g_up])])
    wire_up = _pair_sum(g_up, sib_up, where, name="pair_sum_w_up")
    dx1, dg_mlp_pre, ((recv_dn,),) = _rms_bwd(dh2, x1, g_mlp_pre, dx2, tm=256, out_dtype=F32, name="mlp_pre_bwd",
                                              exchanges=[_x_chip([wire_dn], rows=[(800, 224)], into=[recv_dn])])
    fin_dn = _final_half(g_dn, sib_dn, recv_dn, where, name="final_half_w_down")
    dmixed, dg_mix_post = _rms_bwd(dx1, mixed, g_mix_post, None, tm=256, out_dtype=BF16, name="mix_post_bwd")
    d_w_out, ((oth_dn,),) = _mm_tn(cat, dmixed, tm=1024, tn=1024, tt=2048, name="w_out_grad",
                                   exchanges=[_x_share([fin_dn])])
    g_out = halved(d_w_out.reshape(N_CHIPS, D_MODEL // N_CHIPS, D_MODEL))
    dcat, ((sib_out,),) = _mm(dmixed, w_out, tm=1024, tn=1024, tk=D_MODEL, out_dtype=F32, w_layout="nk", name="out_proj_bwd",
                              exchanges=[_x_pair([g_out])])
    wire_out = _pair_sum(g_out, sib_out, where, name="pair_sum_w_out")
    dattn, dg_again = _rms_bwd(dcat, attn, again, None, tm=tm, out_dtype=F32, name="attn_norm_bwd")
    dq_a, dkv, dsinks, ((recv_out,), (recv_up,)) = _swa_bwd(
        proj, sinks, dattn, lse, name="swa_bwd",
        exchanges=[_x_chip([wire_out]), _x_chip([wire_up], rows=[(0, 448)])])
    dq_r, df_r, di_r, dg_r, dlb, dgain_h, ((recv_up,),) = _hgrn_bwd(
        proj, lb_logits, rgain, o_pre, s0, dcat, tb=512, name="hgrn_bwd",
        exchanges=[_x_chip([wire_up], rows=[(448, 576)], into=[recv_up])])
    fin_up = _final_half(g_up, sib_up, recv_up, where, name="final_half_w_up")
    fin_out = _final_half(g_out, sib_out, recv_out, where, name="final_half_w_out")
    dproj = jnp.concatenate([dq_a, dkv, dq_r, df_r, di_r, dg_r], axis=1).astype(BF16)
    piece_rows = D_MODEL // 4

    def w_in_piece(pc, exchanges):
        d, xres = _mm_tn(h1, dproj, tm=piece_rows, tn=768, tt=2048, m_blocks=(2, 2, pc),
                         name="w_in_grad_%d" % pc, exchanges=exchanges)
        return d[0].reshape(2, piece_rows, N_CHIPS, shard).transpose(2, 0, 1, 3), xres

    g_in0, ((oth_up, oth_out),) = w_in_piece(0, [_x_share([fin_up, fin_out])])
    g_in1, ((sib_in0,),) = w_in_piece(1, [_x_pair([g_in0])])
    wire_in0 = _pair_sum(g_in0, sib_in0, where, name="pair_sum_w_in_0")
    dh1, ((recv_in0,), (sib_in1,)) = _mm(
        dproj, w_in, tm=1024, tn=1024, tk=2688, out_dtype=F32, w_layout="nk", name="in_proj_bwd",
        exchanges=[_x_chip([wire_in0]), _x_pair([g_in1])])
    wire_in1 = _pair_sum(g_in1, sib_in1, where, name="pair_sum_w_in_1")
    gx, dg_mix_pre, ((recv_in1,),) = _rms_bwd(dh1, xs, g_mix_pre, dx1, tm=256, out_dtype=F32, name="mix_pre_bwd",
                                              exchanges=[_x_chip([wire_in1])])
    fin_in0 = _final_half(g_in0, sib_in0, recv_in0, where, name="final_half_w_in_0")
    fin_in1 = _final_half(g_in1, sib_in1, recv_in1, where, name="final_half_w_in_1")
    oth_in0, oth_in1 = _run_exchange(_x_share([fin_in0, fin_in1]), name="share_w_in")
    fin_in = jnp.concatenate([fin_in0, fin_in1], axis=0)
    oth_in = jnp.concatenate([oth_in0, oth_in1], axis=0)

    big = [(fin_in, oth_in), (fin_out, oth_out), (fin_up, oth_up), (fin_dn, oth_dn)]
    drgain = jnp.sum(dgain_h, axis=0)
    small = _pack(jnp.sum(dsinks, axis=1)[None, :], dg_again, dlb, jnp.zeros_like(dlb), drgain,
                  [dg_mix_pre, dg_mix_post, dg_mlp_pre, dg_mlp_post], loss=loss_row)
    return gx, big, small


def kernel(x, w_in, attn_sinks, attn_out_gain, rnn_lb_logits, rnn_norm_gain, w_out, mix_pre_gain, mix_post_gain, mlp_pre_gain, mlp_post_gain, w_up, w_down, loss_target, m_w_in, m_attn_sinks, m_attn_out_gain, m_rnn_lb_logits, m_rnn_norm_gain, m_w_out, m_mix_pre_gain, m_mix_post_gain, m_mlp_pre_gain, m_mlp_post_gain, m_w_up, m_w_down, v_w_in, v_attn_sinks, v_attn_out_gain, v_rnn_lb_logits, v_rnn_norm_gain, v_w_out, v_mix_pre_gain, v_mix_post_gain, v_mlp_pre_gain, v_mlp_post_gain, v_w_up, v_w_down):
    ax, ay, ac = _place()
    where = jnp.stack([2 * ax + ay, ac]).astype(jnp.int32)
    big_w = [w_in[0], w_out[0], w_up[0], w_down[0]]
    big_m = [m_w_in[0], m_w_out[0], m_w_up[0], m_w_down[0]]
    big_v = [v_w_in[0], v_w_out[0], v_w_up[0], v_w_down[0]]

    halves = [w.astype(BF16).reshape(2, w.shape[0] // 2, w.shape[1]) for w in big_w]
    gx, big_g, small_part = _layer_grads(
        x[0], loss_target[0], halves, where, attn_sinks, attn_out_gain, rnn_lb_logits, rnn_norm_gain,
        mix_pre_gain, mix_post_gain, mlp_pre_gain, mlp_post_gain)

    names = ["w_in", "w_out", "w_up", "w_down"]
    grads, deltas, new_m, new_v = [], [], [], []
    for (f, o), w, m, v, nm in zip(big_g, big_w, big_m, big_v, names):
        g, d, nm_, nv_ = _adamw(w, f, o, m, v, where, name="adamw_" + nm)
        grads.append(g[None])
        deltas.append(d[None])
        new_m.append(nm_[None])
        new_v.append(nv_[None])

    def pack_params(sinks, again, logits, rgain, gains):
        return _pack(sinks, again, logits[0:1], logits[1:2], rgain, gains)

    pw = pack_params(attn_sinks, attn_out_gain, rnn_lb_logits, rnn_norm_gain,
                     [mix_pre_gain, mix_post_gain, mlp_pre_gain, mlp_post_gain])
    pm = pack_params(m_attn_sinks, m_attn_out_gain, m_rnn_lb_logits, m_rnn_norm_gain,
                     [m_mix_pre_gain, m_mix_post_gain, m_mlp_pre_gain, m_mlp_post_gain])
    pv = pack_params(v_attn_sinks, v_attn_out_gain, v_rnn_lb_logits, v_rnn_norm_gain,
                     [v_mix_pre_gain, v_mix_post_gain, v_mlp_pre_gain, v_mlp_post_gain])
    packs = _small_reduce_adamw(small_part, pw, pm, pv, name="small_reduce_adamw")

    def unpack(p):
        seg = lambda o, k: p[:, o:o + k]
        logits = jnp.concatenate([seg(SEG_L0, RNN_W), seg(SEG_L1, RNN_W)], axis=0)
        gains = [seg(SEG_G + i * D_MODEL, D_MODEL) for i in range(4)]
        return dict(sinks=seg(SEG_SINK, N_Q), again=seg(SEG_AGAIN, ATTN_W), logits=logits,
                    rgain=seg(SEG_RGAIN, RNN_HD), gains=gains)

    def order(small, big):
        return [big[0], small["sinks"], small["again"], small["logits"], small["rgain"], big[1],
                *small["gains"], big[2], big[3]]

    loss = packs[0][0, 0]
    outs = [loss, gx[None]]
    for p, b in zip(packs, [grads, deltas, new_m, new_v]):
        outs += order(unpack(p), b)
    return tuple(outs)
```

```python
import functools

import jax
import jax.numpy as jnp
from jax import lax
from jax.experimental import pallas as pl
from jax.experimental.pallas import tpu as pltpu

F32 = jnp.float32
BF16 = jnp.bfloat16
MESH = pl.DeviceIdType.MESH

EPS = 1e-6
D_MODEL = 2048
ATTN_W = 1024
HEAD_DIM = 64
N_Q = 16
N_KV = 2
GROUP = 8
BLK = 128
RNN_W = 1024
RNN_HD = 128
N_RNN = 8
CHUNK = 64
SUB_FWD = 16
SUB_BWD = 8
D_FF = 8192
IN_W = 5376
N_CHIPS = 4
KV_COL = ATTN_W
QR_COL = ATTN_W + 2 * 128
FR_COL = QR_COL + RNN_W
IR_COL = FR_COL + RNN_W
GR_COL = IR_COL + RNN_W

ADAM_LR = 0.001
ADAM_B1 = 0.9
ADAM_B2 = 0.999
ADAM_EPS = 1e-08
ADAM_WD = 0.01
ADAM_STEP = 10

VMEM_LIMIT = 48 * 1024 * 1024

NT = (((1,), (1,)), ((), ()))
TN = (((0,), (0,)), ((), ()))


def _params(sem=None):
    return pltpu.CompilerParams(dimension_semantics=sem, vmem_limit_bytes=VMEM_LIMIT)


def _sigmoid(x):
    return 1.0 / (1.0 + jnp.exp(-x))


ANY = pl.BlockSpec(memory_space=pl.ANY)


def _place():
    return lax.axis_index("x"), lax.axis_index("y"), lax.axis_index("c")


def _other_chips(x, y):
    return [(1 - x, y), (x, 1 - y), (1 - x, 1 - y)]


class _Exchange:
    def __init__(self, srcs, outs, ncopy, build, aliases=None):
        self.srcs, self.outs, self.ncopy, self.build = list(srcs), list(outs), ncopy, build
        self.aliases = aliases or {}


def _remote(src, dst, send_sems, recv_sems, k, to):
    return pltpu.make_async_remote_copy(src_ref=src, dst_ref=dst, send_sem=send_sems.at[k],
                                        recv_sem=recv_sems.at[k], device_id=to, device_id_type=MESH)


def _call(body, *, name, grid, in_specs, out_specs, out_shape, args, scratch_shapes=(), semantics=None,
          exchanges=()):
    in_specs, out_specs, out_shape = list(in_specs), list(out_specs), list(out_shape)
    scratch_shapes = list(scratch_shapes)
    ni, no, ns = len(in_specs), len(out_specs), len(scratch_shapes)
    xsrc = [s for x in exchanges for s in x.srcs]
    xout = [o for x in exchanges for o in x.outs]
    nxi, nxo = len(xsrc), len(xout)
    aliases = {}
    a0 = b0 = 0
    for x in exchanges:
        for si, oi in x.aliases.items():
            aliases[ni + a0 + si] = no + b0 + oi
        a0 += len(x.srcs)
        b0 += len(x.outs)
    sems = []
    for x in exchanges:
        sems += [pltpu.SemaphoreType.DMA((x.ncopy,)), pltpu.SemaphoreType.DMA((x.ncopy,))]

    def wrapped(*refs):
        ins, xi = refs[:ni], refs[ni:ni + nxi]
        outs, xo = refs[ni + nxi:ni + nxi + no], refs[ni + nxi + no:ni + nxi + no + nxo]
        rest = refs[ni + nxi + no + nxo:]
        scr, sm = rest[:ns], rest[ns:]

        def copies():
            cps = []
            a = b = 0
            for k, x in enumerate(exchanges):
                cps += x.build(xi[a:a + len(x.srcs)], xo[b:b + len(x.outs)], sm[2 * k], sm[2 * k + 1])
                a += len(x.srcs)
                b += len(x.outs)
            return cps

        def start():
            for cp in copies():
                cp.start()

        def wait():
            for cp in copies():
                cp.wait()

        if not exchanges:
            body(*ins, *outs, *scr)
        elif not grid:
            start()
            body(*ins, *outs, *scr)
            wait()
        else:
            first = last = None
            for ax, g in enumerate(grid):
                f = pl.program_id(ax) == 0
                l = pl.program_id(ax) == g - 1
                first = f if first is None else first & f
                last = l if last is None else last & l
            pl.when(first)(start)
            body(*ins, *outs, *scr)
            pl.when(last)(wait)

    if exchanges and semantics is not None:
        semantics = ("arbitrary",) * len(grid)
    kwargs = dict(grid=grid) if grid else {}
    res = pl.pallas_call(
        wrapped, name=name,
        in_specs=in_specs + [ANY] * nxi, out_specs=out_specs + [ANY] * nxo,
        out_shape=out_shape + xout, scratch_shapes=scratch_shapes + sems,
        input_output_aliases=aliases,
        compiler_params=_params(semantics), **kwargs,
    )(*args, *xsrc)
    res = list(res)
    mine, theirs = res[:no], res[no:]
    per = []
    b = 0
    for x in exchanges:
        per.append(theirs[b:b + len(x.outs)])
        b += len(x.outs)
    return mine, per


def _run_exchange(x, *, name):
    return _call(lambda: None, name=name, grid=(), in_specs=[], out_specs=[], out_shape=[], args=[],
                 exchanges=[x])[1][0]


def _x_gather_ici(bufs, rows=None):
    n = len(bufs)
    rows = rows or [(0, b.shape[1]) for b in bufs]

    def build(srcs, outs, ss, rs):
        x, y, c = _place()
        cps = []
        for a in range(n):
            piece = pl.ds(*rows[a])
            for j, (px, py) in enumerate(_other_chips(x, y)):
                cps.append(_remote(srcs[a].at[4 * x + 2 * y + c, piece], outs[a].at[4 * x + 2 * y + c, piece],
                                   ss, rs, 3 * a + j, (px, py, c)))
        return cps

    outs = [jax.ShapeDtypeStruct(b.shape, b.dtype) for b in bufs]
    return _Exchange(bufs, outs, 3 * n, build, aliases={a: a for a in range(n)})


def _x_gather_d2d(bufs):
    n = len(bufs)

    def build(srcs, outs, ss, rs):
        x, y, c = _place()
        cps = []
        for a in range(n):
            for j, (px, py) in enumerate(_other_chips(x, y)):
                slot = 4 * px + 2 * py + c
                cps.append(_remote(srcs[a].at[slot], outs[a].at[slot], ss, rs, 3 * a + j, (x, y, 1 - c)))
        return cps

    outs = [jax.ShapeDtypeStruct(b.shape, b.dtype) for b in bufs]
    return _Exchange(bufs, outs, 3 * n, build, aliases={a: a for a in range(n)})


def _x_pair(grads):
    n = len(grads)

    def build(srcs, outs, ss, rs):
        x, y, c = _place()
        return [_remote(srcs[a].at[:, 1 - c], outs[a], ss, rs, a, (x, y, 1 - c)) for a in range(n)]

    outs = [jax.ShapeDtypeStruct((4,) + g.shape[2:], g.dtype) for g in grads]
    return _Exchange(grads, outs, n, build)


def _x_chip(wires, rows=None, into=None):
    n = len(wires)
    rows = rows or [(0, w.shape[1]) for w in wires]

    def build(srcs, outs, ss, rs):
        x, y, c = _place()
        cps = []
        for a in range(n):
            piece = pl.ds(*rows[a])
            for j, (px, py) in enumerate(_other_chips(x, y)):
                cps.append(_remote(srcs[a].at[2 * px + py, piece], outs[a].at[j, piece], ss, rs,
                                   3 * a + j, (px, py, c)))
        return cps

    outs = [jax.ShapeDtypeStruct((3,) + w.shape[1:], w.dtype) for w in wires]
    if into is None:
        return _Exchange(wires, outs, 3 * n, build)
    return _Exchange(list(wires) + list(into), outs, 3 * n, build, aliases={n + a: a for a in range(n)})


def _x_share(halves):
    n = len(halves)

    def build(srcs, outs, ss, rs):
        x, y, c = _place()
        return [_remote(srcs[a], outs[a], ss, rs, a, (x, y, 1 - c)) for a in range(n)]

    outs = [jax.ShapeDtypeStruct(h.shape, h.dtype) for h in halves]
    return _Exchange(halves, outs, n, build)


def _mm(a, w, *, tm, tn, tk, out_dtype, name, a_square=False, relu=False, mul2=None, w_layout="kn",
        a_slabs=False, exchanges=()):
    if a_slabs:
        m, k = a.shape[1], a.shape[0] * a.shape[2]
        per_ka = a.shape[2] // tk
        a_spec = pl.BlockSpec((None, tm, tk), lambda i, j, kk: (kk // per_ka, i, kk % per_ka))
    else:
        m, k = a.shape
        a_spec = pl.BlockSpec((tm, tk), lambda i, j, kk: (i, kk))
    if w_layout == "kn":
        n = w.shape[1]
        w_spec = pl.BlockSpec((tk, tn), lambda i, j, kk: (kk, j))
    elif w_layout == "nk":
        n = w.shape[0]
        w_spec = pl.BlockSpec((tn, tk), lambda i, j, kk: (j, kk))
    elif w_layout == "skn":
        n = w.shape[0] * w.shape[2]
        per_n = w.shape[2] // tn
        w_spec = pl.BlockSpec((None, tk, tn), lambda i, j, kk: (j // per_n, kk, j % per_n))
    else:
        assert w_layout == "snk"
        n = w.shape[1]
        per_k = w.shape[2] // tk
        w_spec = pl.BlockSpec((None, tn, tk), lambda i, j, kk: (kk // per_k, j, kk % per_k))
    w_dims = NT if w_layout in ("nk", "snk") else (((1,), (0,)), ((), ()))
    nk = k // tk
    assert m % tm == 0 and n % tn == 0 and k % tk == 0

    def body(*refs):
        if mul2 is not None:
            a_ref, w_ref, e_ref, o_ref, acc_ref = refs
        else:
            a_ref, w_ref, o_ref, acc_ref = refs
            e_ref = None
        kk = pl.program_id(2)
        av = a_ref[...]
        if a_square:
            af = av.astype(F32)
            av = (af * af).astype(BF16)
        part = lax.dot_general(av, w_ref[...], w_dims, preferred_element_type=F32)

        def finish(r):
            if relu:
                r = jnp.maximum(r, 0.0)
            if e_ref is not None:
                r = 2.0 * e_ref[...].astype(F32) * r
            o_ref[...] = r.astype(out_dtype)

        if nk == 1:
            finish(part)
        else:
            @pl.when(kk == 0)
            def _():
                acc_ref[...] = part

            @pl.when(kk > 0)
            def _():
                acc_ref[...] += part

            @pl.when(kk == nk - 1)
            def _():
                finish(acc_ref[...])

    in_specs = [a_spec, w_spec]
    args = [a, w]
    if mul2 is not None:
        in_specs.append(pl.BlockSpec((tm, tn), lambda i, j, kk: (i, j)))
        args.append(mul2)
    acc_shape = (tm, tn) if nk > 1 else (8, 128)
    (out,), per = _call(
        body, name=name, grid=(m // tm, n // tn, nk),
        in_specs=in_specs, out_specs=[pl.BlockSpec((tm, tn), lambda i, j, kk: (i, j))],
        out_shape=[jax.ShapeDtypeStruct((m, n), out_dtype)], args=args,
        scratch_shapes=[pltpu.VMEM(acc_shape, F32)],
        semantics=("parallel", "parallel", "arbitrary"), exchanges=exchanges)
    return (out, per) if exchanges else out


def _mm_tn(a, b, *, tm, tn, tt, name, a_square=False, n_split=1, m_blocks=None, b_slabs=False, exchanges=()):
    t, m = a.shape
    if b_slabs:
        n = b.shape[0] * b.shape[2]
        assert b.shape[0] == n_split and b.shape[2] == tn
        b_spec = pl.BlockSpec((None, tt, tn), lambda i, j, ti: (j, ti, 0))
    else:
        n = b.shape[1]
        b_spec = pl.BlockSpec((tt, tn), lambda i, j, ti: (ti, j))
    assert t % tt == 0 and m % tm == 0 and n % tn == 0 and (n // n_split) % tn == 0
    per = n // n_split // tn
    count, stride, first = m_blocks or (m // tm, 1, 0)
    m = count * tm

    def body(a_ref, b_ref, o_ref):
        ti = pl.program_id(2)
        av = a_ref[...]
        if a_square:
            af = av.astype(F32)
            av = (af * af).astype(BF16)
        part = lax.dot_general(av, b_ref[...], TN, preferred_element_type=F32)

        @pl.when(ti == 0)
        def _():
            o_ref[...] = part

        @pl.when(ti > 0)
        def _():
            o_ref[...] += part

    (out,), xres = _call(
        body, name=name, grid=(m // tm, n // tn, t // tt),
        in_specs=[pl.BlockSpec((tt, tm), lambda i, j, ti: (ti, first + stride * i)), b_spec],
        out_specs=[pl.BlockSpec((None, tm, tn), lambda i, j, ti: (j // per, i, j % per))],
        out_shape=[jax.ShapeDtypeStruct((n_split, m, n // n_split), F32)], args=[a, b],
        semantics=("parallel", "parallel", "arbitrary"), exchanges=exchanges)
    return (out, xres) if exchanges else out


def _rstd(x):
    return lax.rsqrt(jnp.mean(x * x, axis=-1, keepdims=True) + EPS)


def _rms_cast(x, g, *, tm, name):
    t, d = x.shape

    def body(x_ref, g_ref, o_ref):
        xv = x_ref[...]
        o_ref[...] = (xv * _rstd(xv) * g_ref[...]).astype(BF16)

    return pl.pallas_call(
        body, name=name, grid=(t // tm,),
        in_specs=[pl.BlockSpec((tm, d), lambda i: (i, 0)), pl.BlockSpec((1, d), lambda i: (0, 0))],
        out_specs=pl.BlockSpec((tm, d), lambda i: (i, 0)),
        out_shape=jax.ShapeDtypeStruct((t, d), BF16),
        compiler_params=_params(("parallel",)),
    )(x, g)


def _mix_cat(attn, rnn, gain, *, tm, name):
    t = attn.shape[0]

    def body(a_ref, r_ref, g_ref, o_ref):
        av = a_ref[...]
        o_ref[:, :ATTN_W] = (av * _rstd(av) * g_ref[...]).astype(BF16)
        o_ref[:, ATTN_W:] = r_ref[...].astype(BF16)

    return pl.pallas_call(
        body, name=name, grid=(t // tm,),
        in_specs=[pl.BlockSpec((tm, ATTN_W), lambda i: (i, 0)), pl.BlockSpec((tm, RNN_W), lambda i: (i, 0)),
                  pl.BlockSpec((1, ATTN_W), lambda i: (0, 0))],
        out_specs=pl.BlockSpec((tm, D_MODEL), lambda i: (i, 0)),
        out_shape=jax.ShapeDtypeStruct((t, D_MODEL), BF16),
        compiler_params=_params(("parallel",)),
    )(attn, rnn, gain)


def _post_norm_res(mixed, g_post, res, g_next, *, tm, name):
    t, d = mixed.shape

    def body(m_ref, gp_ref, r_ref, gn_ref, x1_ref, h2_ref):
        mv = m_ref[...]
        x1 = r_ref[...] + mv * _rstd(mv) * gp_ref[...]
        x1_ref[...] = x1
        h2_ref[...] = (x1 * _rstd(x1) * gn_ref[...]).astype(BF16)

    row = pl.BlockSpec((tm, d), lambda i: (i, 0))
    vec = pl.BlockSpec((1, d), lambda i: (0, 0))
    return pl.pallas_call(
        body, name=name, grid=(t // tm,),
        in_specs=[row, vec, row, vec], out_specs=[row, row],
        out_shape=[jax.ShapeDtypeStruct((t, d), F32), jax.ShapeDtypeStruct((t, d), BF16)],
        compiler_params=_params(("parallel",)),
    )(mixed, g_post, res, g_next)


def _rms_bwd(dyn, xin, g, res, *, tm, out_dtype, name, col_block=0, exchanges=()):
    t, d = xin.shape

    def body(*refs):
        if res is not None:
            dy_ref, x_ref, g_ref, r_ref, dx_ref, dg_ref = refs
        else:
            dy_ref, x_ref, g_ref, dx_ref, dg_ref = refs
        i = pl.program_id(0)
        xv = x_ref[...]
        dy = dy_ref[...].astype(F32)
        r = _rstd(xv)
        xh = xv * r
        part = jnp.sum(dy * xh, axis=0, keepdims=True)

        @pl.when(i == 0)
        def _():
            dg_ref[...] = part

        @pl.when(i > 0)
        def _():
            dg_ref[...] += part

        tt = dy * g_ref[...]
        dx = r * (tt - xh * jnp.mean(tt * xh, axis=-1, keepdims=True))
        if res is not None:
            dx = dx + r_ref[...]
        dx_ref[...] = dx.astype(out_dtype)

    row = pl.BlockSpec((tm, d), lambda i: (i, 0))
    vec = pl.BlockSpec((1, d), lambda i: (0, 0))
    in_specs = [pl.BlockSpec((tm, d), lambda i: (i, col_block)), row, vec]
    args = [dyn, xin, g]
    if res is not None:
        in_specs.append(row)
        args.append(res)
    res, xres = _call(
        body, name=name, grid=(t // tm,),
        in_specs=in_specs, out_specs=[row, vec],
        out_shape=[jax.ShapeDtypeStruct((t, d), out_dtype), jax.ShapeDtypeStruct((1, d), F32)], args=args,
        semantics=("arbitrary",), exchanges=exchanges)
    return (*res, xres) if exchanges else res


def _loss_head(y, g_post, x1, target, *, tm, name):
    t, d = y.shape

    def body(y_ref, g_ref, x1_ref, t_ref, dy_ref, dx2_ref, loss_ref, dg_ref):
        i = pl.program_id(0)
        yv = y_ref[...]
        r = _rstd(yv)
        yh = yv * r
        gv = g_ref[...]
        err = x1_ref[...] + yh * gv - t_ref[...]
        lpart = 0.5 * jnp.sum(jnp.mean(err * err, axis=-1, keepdims=True), axis=0, keepdims=True)
        dx2 = err * (1.0 / d)
        dgp = jnp.sum(dx2 * yh, axis=0, keepdims=True)
        lane = lax.broadcasted_iota(jnp.int32, (1, 128), 1)
        lrow = jnp.where(lane == 0, lpart, 0.0)

        @pl.when(i == 0)
        def _():
            dg_ref[...] = dgp
            loss_ref[...] = lrow

        @pl.when(i > 0)
        def _():
            dg_ref[...] += dgp
            loss_ref[...] += lrow

        tt = dx2 * gv
        dy_ref[...] = (r * (tt - yh * jnp.mean(tt * yh, axis=-1, keepdims=True))).astype(BF16)
        dx2_ref[...] = dx2

    row = pl.BlockSpec((tm, d), lambda i: (i, 0))
    vec = pl.BlockSpec((1, d), lambda i: (0, 0))
    return pl.pallas_call(
        body, name=name, grid=(t // tm,),
        in_specs=[row, vec, row, row],
        out_specs=[row, row, pl.BlockSpec((1, 128), lambda i: (0, 0)), vec],
        out_shape=[jax.ShapeDtypeStruct((t, d), BF16), jax.ShapeDtypeStruct((t, d), F32),
                   jax.ShapeDtypeStruct((1, 128), F32), jax.ShapeDtypeStruct((1, d), F32)],
        compiler_params=_params(("arbitrary",)),
    )(y, g_post, x1, target)


def _alibi_slope(h):
    return 2.0 ** (-8.0 * (h + 1) / N_Q)


PAIR = 2 * HEAD_DIM
N_PAIRS = N_Q // 2
PAIRS_PER_KV = GROUP // 2
SMEM = pl.BlockSpec(memory_space=pltpu.SMEM)


def _swa_mask(n):
    key = lax.broadcasted_iota(jnp.int32, (2 * BLK, BLK), 0)
    qry = lax.broadcasted_iota(jnp.int32, (2 * BLK, BLK), 1)
    dist = qry + BLK - key
    valid = (dist >= 0) & (dist < BLK) & ((key >= BLK) | (n > 0))
    return valid, dist.astype(F32)


def _block_diag(kvp_ref, kvc_ref, off):
    a = jnp.concatenate([kvp_ref[:, off:off + HEAD_DIM], kvc_ref[:, off:off + HEAD_DIM]], axis=0).astype(BF16)
    z = jnp.zeros_like(a)
    return jnp.concatenate([jnp.concatenate([a, z], axis=1), jnp.concatenate([z, a], axis=1)], axis=0)


def _swa_scores(s2, e, hh, valid, distf):
    s = s2[2 * BLK * e:2 * BLK * (e + 1)] * (HEAD_DIM ** -0.5) - _alibi_slope(hh) * distf
    return jnp.where(valid, s, -1e30)


def _swa_fwd(proj, sinks, *, name, exchanges=()):
    t = proj.shape[0]
    nb = t // BLK
    kvb = KV_COL // (2 * 128)

    def body(sink_ref, q_ref, kvc_ref, kvp_ref, o_ref, lse_ref):
        n = pl.program_id(0)
        valid, distf = _swa_mask(n)
        for kvh in range(N_KV):
            k2 = _block_diag(kvp_ref, kvc_ref, kvh * HEAD_DIM)
            v2 = _block_diag(kvp_ref, kvc_ref, 128 + kvh * HEAD_DIM)
            for jp in range(PAIRS_PER_KV):
                pair = kvh * PAIRS_PER_KV + jp
                lanes = slice(pair * PAIR, (pair + 1) * PAIR)
                s2 = lax.dot_general(k2, q_ref[:, lanes].astype(BF16), NT, preferred_element_type=F32)
                probs = []
                for e in range(2):
                    hh = 2 * pair + e
                    s = _swa_scores(s2, e, hh, valid, distf)
                    sink = sink_ref[0, hh]
                    mx = jnp.maximum(jnp.max(s, axis=0, keepdims=True), sink)
                    p = jnp.exp(s - mx)
                    l = jnp.sum(p, axis=0, keepdims=True) + jnp.exp(sink - mx)
                    probs.append((p * (1.0 / l)).astype(BF16))
                    lse_ref[hh:hh + 1, :] = mx + jnp.log(l)
                o_ref[:, lanes] = lax.dot_general(jnp.concatenate(probs, axis=0), v2, TN,
                                                  preferred_element_type=F32)

    res, xres = _call(
        body, name=name, grid=(nb,),
        in_specs=[SMEM,
                  pl.BlockSpec((BLK, ATTN_W), lambda n: (n, 0)),
                  pl.BlockSpec((BLK, 256), lambda n: (n, kvb)),
                  pl.BlockSpec((BLK, 256), lambda n: (jnp.maximum(n - 1, 0), kvb))],
        out_specs=[pl.BlockSpec((BLK, ATTN_W), lambda n: (n, 0)),
                   pl.BlockSpec((None, N_Q, BLK), lambda n: (n, 0, 0))],
        out_shape=[jax.ShapeDtypeStruct((t, ATTN_W), F32), jax.ShapeDtypeStruct((nb, N_Q, BLK), F32)],
        args=[sinks, proj, proj, proj], semantics=("parallel",), exchanges=exchanges)
    return (*res, xres) if exchanges else res


def _swa_bwd(proj, sinks, dattn, lse, *, name, exchanges=()):
    t = proj.shape[0]
    nb = t // BLK
    kvb = KV_COL // (2 * 128)

    def body(sink_ref, q_ref, kvc_ref, kvp_ref, do_ref, lse_ref, dq_ref, dkv_ref, dsink_ref, carry_ref):
        n = pl.program_id(0)

        @pl.when(n == 0)
        def _():
            dsink_ref[...] = jnp.zeros_like(dsink_ref)
            carry_ref[...] = jnp.zeros_like(carry_ref)

        @pl.when(n < nb)
        def _():
            valid, distf = _swa_mask(n)
            for kvh in range(N_KV):
                k2 = _block_diag(kvp_ref, kvc_ref, kvh * HEAD_DIM)
                v2 = _block_diag(kvp_ref, kvc_ref, 128 + kvh * HEAD_DIM)
                dk2 = jnp.zeros((4 * BLK, PAIR), F32)
                dv2 = jnp.zeros((4 * BLK, PAIR), F32)
                for jp in range(PAIRS_PER_KV):
                    pair = kvh * PAIRS_PER_KV + jp
                    lanes = slice(pair * PAIR, (pair + 1) * PAIR)
                    q2 = q_ref[:, lanes].astype(BF16)
                    do2 = do_ref[:, lanes].astype(BF16)
                    s2 = lax.dot_general(k2, q2, NT, preferred_element_type=F32)
                    dp2 = lax.dot_general(v2, do2, NT, preferred_element_type=F32)
                    probs, dss = [], []
                    for e in range(2):
                        hh = 2 * pair + e
                        lse_h = lse_ref[hh:hh + 1, :]
                        p = jnp.exp(_swa_scores(s2, e, hh, valid, distf) - lse_h)
                        dp = dp2[2 * BLK * e:2 * BLK * (e + 1)]
                        delta = jnp.sum(p * dp, axis=0, keepdims=True)
                        dsink_ref[hh:hh + 1, :] += -jnp.exp(sink_ref[0, hh] - lse_h) * delta
                        probs.append(p.astype(BF16))
                        dss.append((p * (dp - delta)).astype(BF16))
                    ds2 = jnp.concatenate(dss, axis=0)
                    dq_ref[:, lanes] = lax.dot_general(ds2, k2, TN, preferred_element_type=F32) * (HEAD_DIM ** -0.5)
                    dk2 = dk2 + jnp.dot(ds2, q2, preferred_element_type=F32)
                    dv2 = dv2 + jnp.dot(jnp.concatenate(probs, axis=0), do2, preferred_element_type=F32)
                dk_cat = (dk2[:2 * BLK, :HEAD_DIM] + dk2[2 * BLK:, HEAD_DIM:]) * (HEAD_DIM ** -0.5)
                dv_cat = dv2[:2 * BLK, :HEAD_DIM] + dv2[2 * BLK:, HEAD_DIM:]
                ko = kvh * HEAD_DIM
                vo = 128 + kvh * HEAD_DIM
                dkv_ref[:, ko:ko + HEAD_DIM] = carry_ref[:, ko:ko + HEAD_DIM] + dk_cat[:BLK]
                dkv_ref[:, vo:vo + HEAD_DIM] = carry_ref[:, vo:vo + HEAD_DIM] + dv_cat[:BLK]
                carry_ref[:, ko:ko + HEAD_DIM] = dk_cat[BLK:]
                carry_ref[:, vo:vo + HEAD_DIM] = dv_cat[BLK:]

        @pl.when(n == nb)
        def _():
            dkv_ref[...] = carry_ref[...]

    last = nb - 1
    res, xres = _call(
        body, name=name, grid=(nb + 1,),
        in_specs=[SMEM,
                  pl.BlockSpec((BLK, ATTN_W), lambda n: (jnp.minimum(n, last), 0)),
                  pl.BlockSpec((BLK, 256), lambda n: (jnp.minimum(n, last), kvb)),
                  pl.BlockSpec((BLK, 256), lambda n: (jnp.maximum(jnp.minimum(n, last) - 1, 0), kvb)),
                  pl.BlockSpec((BLK, ATTN_W), lambda n: (jnp.minimum(n, last), 0)),
                  pl.BlockSpec((None, N_Q, BLK), lambda n: (jnp.minimum(n, last), 0, 0))],
        out_specs=[pl.BlockSpec((BLK, ATTN_W), lambda n: (jnp.minimum(n, last), 0)),
                   pl.BlockSpec((BLK, 256), lambda n: (jnp.maximum(n - 1, 0), 0)),
                   pl.BlockSpec((N_Q, BLK), lambda n: (0, 0))],
        out_shape=[jax.ShapeDtypeStruct((t, ATTN_W), F32), jax.ShapeDtypeStruct((t, 256), F32),
                   jax.ShapeDtypeStruct((N_Q, BLK), F32)],
        scratch_shapes=[pltpu.VMEM((BLK, 256), F32)],
        args=[sinks, proj, proj, proj, dattn, lse], semantics=("arbitrary",), exchanges=exchanges)
    return (*res, xres) if exchanges else res


def _cumsum_rows(x):
    n = x.shape[0]
    row = lax.broadcasted_iota(jnp.int32, x.shape, 0)
    s = 1
    while s < n:
        x = x + jnp.where(row >= s, pltpu.roll(x, s, axis=0), 0.0)
        s *= 2
    return x


def _rev_cumsum_rows(x):
    n = x.shape[0]
    row = lax.broadcasted_iota(jnp.int32, x.shape, 0)
    s = 1
    while s < n:
        x = x + jnp.where(row < n - s, pltpu.roll(x, n - s, axis=0), 0.0)
        s *= 2
    return x


def _lower_bound(lbl_ref):
    l0 = lbl_ref[0:1, :]
    l1 = lbl_ref[1:2, :]
    mx = jnp.maximum(l0, l1)
    e0 = jnp.exp(l0 - mx)
    e1 = jnp.exp(l1 - mx)
    return e0 / (e0 + e1)


def _hgrn_gates(z, lb):
    sg = _sigmoid(z)
    f = lb + (1.0 - lb) * sg
    return sg, f, jnp.log(f), 1.0 - f


def _sub_factors(b, i, sub):
    rows = lax.broadcasted_iota(jnp.int32, (CHUNK, RNN_HD), 0)
    ref = b[sub * i - 1:sub * i, :]
    qfac = jnp.exp(b[sub * i:sub * (i + 1), :] - ref)
    kfac = jnp.where(rows < sub * i, jnp.exp(ref - b), 0.0)
    return qfac, kfac


def _diag_decay(bi, s):
    trow = lax.broadcasted_iota(jnp.int32, bi.shape, 0)
    return jnp.where(trow >= s, jnp.exp(bi - bi[s:s + 1, :]), 0.0)


def _hgrn_fwd(proj, lb_logits, norm_gain, *, tb, name, exchanges=()):
    t = proj.shape[0]
    ntb = t // tb
    nch = tb // CHUNK
    qb, fb, ib, gb = QR_COL // 128, FR_COL // 128, IR_COL // 128, GR_COL // 128

    def body(q_ref, f_ref, i_ref, g_ref, lbl_ref, gain_ref, o_ref, out_ref, s0_ref, st_ref):
        c = pl.program_id(1)

        @pl.when(c == 0)
        def _():
            st_ref[...] = jnp.zeros_like(st_ref)

        lb = _lower_bound(lbl_ref)
        gain = gain_ref[...]

        def chunk(ci, st):
            rows = slice(ci * CHUNK, (ci + 1) * CHUNK)
            _, _, lf, k = _hgrn_gates(f_ref[rows, :], lb)
            qr = q_ref[rows, :]
            q = qr * _sigmoid(qr)
            v = i_ref[rows, :]
            b = _cumsum_rows(lf)
            s0_ref[ci] = st
            o_inter = lax.dot_general((q * jnp.exp(b)).astype(BF16), st.astype(BF16), NT,
                                      preferred_element_type=F32)
            vb = v.astype(BF16)
            blast = b[CHUNK - 1:CHUNK, :]
            khat = (k * jnp.exp(blast - b)).astype(BF16)
            st = st * jnp.exp(blast) + lax.dot_general(vb, khat, TN, preferred_element_type=F32)
            blocks = []
            for i in range(CHUNK // SUB_FWD):
                blk = slice(SUB_FWD * i, SUB_FWD * (i + 1))
                qi, ki, vi, bi = q[blk], k[blk], v[blk], b[blk]
                oi = o_inter[blk]
                if i > 0:
                    qfac, kfac = _sub_factors(b, i, SUB_FWD)
                    att = lax.dot_general((qi * qfac).astype(BF16), (k * kfac).astype(BF16), NT,
                                          preferred_element_type=F32)
                    oi = oi + jnp.dot(att.astype(BF16), vb, preferred_element_type=F32)
                for s in range(SUB_FWD):
                    qe = qi * _diag_decay(bi, s)
                    a = jnp.sum(qe * ki[s:s + 1, :], axis=1, keepdims=True)
                    oi = oi + a * vi[s:s + 1, :]
                blocks.append(oi)
            o = jnp.concatenate(blocks, axis=0)
            o_ref[rows, :] = o
            gr = g_ref[rows, :]
            out_ref[rows, :] = o * _rstd(o) * gain * (gr * _sigmoid(gr))
            return st

        st = st_ref[...]
        for ci in range(nch):
            st = chunk(ci, st)
        st_ref[...] = st

    def col(base):
        return pl.BlockSpec((tb, RNN_HD), lambda h, c: (c, base + h))

    res, xres = _call(
        body, name=name, grid=(N_RNN, ntb),
        in_specs=[col(qb), col(fb), col(ib), col(gb),
                  pl.BlockSpec((2, RNN_HD), lambda h, c: (0, h)), pl.BlockSpec((1, RNN_HD), lambda h, c: (0, 0))],
        out_specs=[pl.BlockSpec((tb, RNN_HD), lambda h, c: (c, h)), pl.BlockSpec((tb, RNN_HD), lambda h, c: (c, h)),
                   pl.BlockSpec((None, nch, RNN_HD, RNN_HD), lambda h, c: (h, c, 0, 0))],
        out_shape=[jax.ShapeDtypeStruct((t, RNN_W), F32), jax.ShapeDtypeStruct((t, RNN_W), F32),
                   jax.ShapeDtypeStruct((N_RNN, t // CHUNK, RNN_HD, RNN_HD), F32)],
        scratch_shapes=[pltpu.VMEM((RNN_HD, RNN_HD), F32)],
        args=[proj, proj, proj, proj, lb_logits, norm_gain],
        semantics=("parallel", "arbitrary"), exchanges=exchanges)
    return (*res, xres) if exchanges else res


def _hgrn_bwd(proj, lb_logits, norm_gain, o_pre, s0, dcat, *, tb, name, exchanges=()):
    t = proj.shape[0]
    ntb = t // tb
    nch = tb // CHUNK
    qb, fb, ib, gb = QR_COL // 128, FR_COL // 128, IR_COL // 128, GR_COL // 128
    sub = SUB_BWD
    nsub = CHUNK // sub

    def body(q_ref, f_ref, i_ref, g_ref, lbl_ref, gain_ref, o_ref, s0_ref, dout_ref,
             dq_ref, df_ref, di_ref, dg_ref, dlb_ref, dgain_ref,
             dst_ref, dqs_ref, dks_ref, dvs_ref):
        c = pl.program_id(1)

        @pl.when(c == 0)
        def _():
            dst_ref[...] = jnp.zeros_like(dst_ref)
            dlb_ref[...] = jnp.zeros_like(dlb_ref)
            dgain_ref[...] = jnp.zeros_like(dgain_ref)

        lb = _lower_bound(lbl_ref)
        gain = gain_ref[...]

        def chunk(ci, dst):
            rows = slice(ci * CHUNK, (ci + 1) * CHUNK)
            dqa_ref, dka_ref, dva_ref = dqs_ref.at[ci], dks_ref.at[ci], dvs_ref.at[ci]
            sg, f, lf, k = _hgrn_gates(f_ref[rows, :], lb)
            qr = q_ref[rows, :]
            sq = _sigmoid(qr)
            q = qr * sq
            v = i_ref[rows, :]
            b = _cumsum_rows(lf)

            dout = dout_ref[rows, :]
            o = o_ref[rows, :]
            gr = g_ref[rows, :]
            sgg = _sigmoid(gr)
            gate = gr * sgg
            rs = _rstd(o)
            nrm = o * rs
            dg_ref[rows, :] = dout * nrm * gain * (sgg * (1.0 + gr * (1.0 - sgg)))
            dn = dout * gate
            dgain_ref[...] += jnp.sum(dn * nrm, axis=0, keepdims=True)
            tt = dn * gain
            do = rs * (tt - nrm * jnp.mean(tt * nrm, axis=-1, keepdims=True))

            dob = do.astype(BF16)
            vb = v.astype(BF16)
            eb = jnp.exp(b)
            blast = b[CHUNK - 1:CHUNK, :]
            ebl = jnp.exp(blast - b)
            dstb = dst.astype(BF16)
            khat = (k * ebl).astype(BF16)
            s0 = s0_ref[ci]
            dqa_ref[...] = eb * jnp.dot(dob, s0.astype(BF16), preferred_element_type=F32)
            dk_state = ebl * jnp.dot(vb, dstb, preferred_element_type=F32)
            dka_ref[...] = dk_state
            d_blast = (jnp.sum(k * dk_state, axis=0, keepdims=True)
                       + jnp.exp(blast) * jnp.sum(dst * s0, axis=0, keepdims=True))
            dva_ref[...] = lax.dot_general(khat, dstb, NT, preferred_element_type=F32)
            dst_next = dst * jnp.exp(blast) + lax.dot_general(dob, (q * eb).astype(BF16), TN,
                                                              preferred_element_type=F32)
            pm = lax.dot_general(dob, vb, NT, preferred_element_type=F32)
            for i in range(nsub):
                blk = slice(sub * i, sub * (i + 1))
                qi, ki, vi, bi, doi = q[blk], k[blk], v[blk], b[blk], do[blk]
                dqi = dqa_ref[blk, :]
                if i > 0:
                    qfac, kfac = _sub_factors(b, i, sub)
                    qt = (qi * qfac).astype(BF16)
                    kt = (k * kfac).astype(BF16)
                    att = lax.dot_general(qt, kt, NT, preferred_element_type=F32).astype(BF16)
                    pmi = pm[blk, :].astype(BF16)
                    dva_ref[...] += lax.dot_general(att, doi.astype(BF16), TN, preferred_element_type=F32)
                    dqi = dqi + qfac * jnp.dot(pmi, kt, preferred_element_type=F32)
                    dka_ref[...] += kfac * lax.dot_general(pmi, qt, TN, preferred_element_type=F32)
                for s in range(sub):
                    e = _diag_decay(bi, s)
                    ks = ki[s:s + 1, :]
                    row = slice(sub * i + s, sub * i + s + 1)
                    a = jnp.sum(qi * e * ks, axis=1, keepdims=True)
                    pe = jnp.sum(doi * vi[s:s + 1, :], axis=1, keepdims=True) * e
                    dqi = dqi + pe * ks
                    dka_ref[row, :] += jnp.sum(pe * qi, axis=0, keepdims=True)
                    dva_ref[row, :] += jnp.sum(a * doi, axis=0, keepdims=True)
                dqa_ref[blk, :] = dqi

            dq = dqa_ref[...]
            dk = dka_ref[...]
            lastrow = lax.broadcasted_iota(jnp.int32, (CHUNK, RNN_HD), 0) == CHUNK - 1
            dlf = _rev_cumsum_rows(q * dq - k * dk + jnp.where(lastrow, d_blast, 0.0))
            dff = dlf / f - dk
            df_ref[rows, :] = dff * (1.0 - lb) * sg * (1.0 - sg)
            dlb_ref[...] += jnp.sum(dff * (1.0 - sg), axis=0, keepdims=True)
            dq_ref[rows, :] = dq * (sq * (1.0 + qr * (1.0 - sq)))
            di_ref[rows, :] = dva_ref[...]
            return dst_next

        dst = dst_ref[...]
        for ci in reversed(range(nch)):
            dst = chunk(ci, dst)
        dst_ref[...] = dst

    def col(base):
        return pl.BlockSpec((tb, RNN_HD), lambda h, c: (ntb - 1 - c, base + h))

    outc = pl.BlockSpec((tb, RNN_HD), lambda h, c: (ntb - 1 - c, h))
    hb = ATTN_W // RNN_HD
    res, xres = _call(
        body, name=name, grid=(N_RNN, ntb),
        in_specs=[col(qb), col(fb), col(ib), col(gb),
                  pl.BlockSpec((2, RNN_HD), lambda h, c: (0, h)), pl.BlockSpec((1, RNN_HD), lambda h, c: (0, 0)),
                  outc,
                  pl.BlockSpec((None, nch, RNN_HD, RNN_HD), lambda h, c: (h, ntb - 1 - c, 0, 0)),
                  pl.BlockSpec((tb, RNN_HD), lambda h, c: (ntb - 1 - c, hb + h))],
        out_specs=[outc, outc, outc, outc,
                   pl.BlockSpec((1, RNN_HD), lambda h, c: (0, h)),
                   pl.BlockSpec((None, 1, RNN_HD), lambda h, c: (h, 0, 0))],
        out_shape=[jax.ShapeDtypeStruct((t, RNN_W), F32)] * 4
        + [jax.ShapeDtypeStruct((1, RNN_W), F32), jax.ShapeDtypeStruct((N_RNN, 1, RNN_HD), F32)],
        scratch_shapes=[pltpu.VMEM((RNN_HD, RNN_HD), F32),
                        pltpu.VMEM((nch, CHUNK, RNN_HD), F32), pltpu.VMEM((nch, CHUNK, RNN_HD), F32),
                        pltpu.VMEM((nch, CHUNK, RNN_HD), F32)],
        args=[proj, proj, proj, proj, lb_logits, norm_gain, o_pre, s0, dcat],
        semantics=("parallel", "arbitrary"), exchanges=exchanges)
    return (*res, xres) if exchanges else res


def _cast_slots(w, where, *, name):
    rows, cols = w.shape
    rh = rows // 2
    tr = _row_tile(rh, cols)
    nh = rh // tr

    def body(wh_ref, w_ref, o_ref):
        o_ref[...] = w_ref[...].astype(BF16)

    return pl.pallas_call(
        body, name=name,
        grid_spec=pltpu.PrefetchScalarGridSpec(
            num_scalar_prefetch=1, grid=(2, nh),
            in_specs=[pl.BlockSpec((tr, cols), lambda h, i, wh: (h * nh + i, 0))],
            out_specs=pl.BlockSpec((None, tr, cols), lambda h, i, wh: (2 * wh[0] + h, i, 0))),
        out_shape=jax.ShapeDtypeStruct((8, rh, cols), BF16),
        compiler_params=_params(("parallel", "parallel")),
    )(where, w)


def _all_gather_halves(bufs, *, name):
    n = len(bufs)

    def body(*refs):
        ins, outs = refs[:n], refs[n:2 * n]
        send_sems, recv_sems = refs[2 * n:]
        x, y, c = _place()
        sibling = (x, y, 1 - c)
        chips = [(1 - x, y), (x, 1 - y), (1 - x, 1 - y)]

        def copy(a, k, block, to, src=None):
            slot = outs[a].at[4 * block[0] + 2 * block[1] + block[2]]
            return pltpu.make_async_remote_copy(
                src_ref=slot if src is None else src, dst_ref=slot,
                send_sem=send_sems.at[a, k], recv_sem=recv_sems.at[a, k],
                device_id=to, device_id_type=MESH)

        first, passed = [], []
        for a in range(n):
            for j, chip in enumerate(chips):
                cp = copy(a, j, (x, y, c), (*chip, c), src=ins[a].at[4 * x + 2 * y + c])
                cp.start()
                first.append(cp)
        for a in range(n):
            for j, chip in enumerate(chips):
                copy(a, j, (*chip, c), (x, y, c)).wait_recv()
                cp = copy(a, 3 + j, (*chip, c), sibling)
                cp.start()
                passed.append(cp)
        for a in range(n):
            for j, chip in enumerate(chips):
                copy(a, 3 + j, (*chip, 1 - c), (x, y, c)).wait_recv()
        for cp in first + passed:
            cp.wait_send()

    return pl.pallas_call(
        body, name=name,
        in_specs=[ANY] * n, out_specs=[ANY] * n,
        out_shape=[jax.ShapeDtypeStruct(b.shape, b.dtype) for b in bufs],
        scratch_shapes=[pltpu.SemaphoreType.DMA((n, 6)), pltpu.SemaphoreType.DMA((n, 6))],
        input_output_aliases={a: a for a in range(n)},
    )(*bufs)


def _row_tile(rows, cols, budget=1 << 20):
    tr = rows
    while tr * cols > budget and tr % 16 == 0:
        tr //= 2
    return tr


def _pair_sum(g, sib, where, *, name):
    _, _, rh, cols = g.shape
    tr = _row_tile(rh, cols)

    def body(w_ref, g_ref, s_ref, o_ref):
        o_ref[...] = (g_ref[...] + s_ref[...]).astype(BF16)

    return pl.pallas_call(
        body, name=name,
        grid_spec=pltpu.PrefetchScalarGridSpec(
            num_scalar_prefetch=1, grid=(4, rh // tr),
            in_specs=[pl.BlockSpec((None, None, tr, cols), lambda s, i, w: (s, w[1], i, 0)),
                      pl.BlockSpec((None, tr, cols), lambda s, i, w: (s, i, 0))],
            out_specs=pl.BlockSpec((None, tr, cols), lambda s, i, w: (s, i, 0))),
        out_shape=jax.ShapeDtypeStruct((4, rh, cols), BF16),
        compiler_params=_params(("parallel", "parallel")),
    )(where, g, sib)


def _final_half(g, sib, recv, where, *, name):
    _, _, rh, cols = g.shape
    tr = _row_tile(rh, cols)

    def body(w_ref, g_ref, s_ref, r_ref, o_ref):
        acc = g_ref[...] + s_ref[...]
        for j in range(3):
            acc = acc + r_ref[j].astype(F32)
        o_ref[...] = acc

    return pl.pallas_call(
        body, name=name,
        grid_spec=pltpu.PrefetchScalarGridSpec(
            num_scalar_prefetch=1, grid=(rh // tr,),
            in_specs=[pl.BlockSpec((None, None, tr, cols), lambda i, w: (w[0], w[1], i, 0)),
                      pl.BlockSpec((None, tr, cols), lambda i, w: (w[0], i, 0)),
                      pl.BlockSpec((3, tr, cols), lambda i, w: (0, i, 0))],
            out_specs=pl.BlockSpec((tr, cols), lambda i, w: (i, 0))),
        out_shape=jax.ShapeDtypeStruct((rh, cols), F32),
        compiler_params=_params(("parallel",)),
    )(where, g, sib, recv)


def _adamw_math(w, g, m, v):
    m = ADAM_B1 * m + (1.0 - ADAM_B1) * g
    v = ADAM_B2 * v + (1.0 - ADAM_B2) * (g * g)
    m_hat = m / (1.0 - ADAM_B1 ** ADAM_STEP)
    v_hat = v / (1.0 - ADAM_B2 ** ADAM_STEP)
    delta = -ADAM_LR * (m_hat / (jnp.sqrt(v_hat) + ADAM_EPS) + ADAM_WD * w)
    return delta, m, v


def _adamw(w, mine, theirs, m, v, where, *, name):
    rows, cols = w.shape
    tr = _row_tile(rows // 2, cols, budget=1 << 19)
    nh = rows // 2 // tr

    def body(wh_ref, w_ref, a_ref, b_ref, m_ref, v_ref, g_ref, d_ref, nm_ref, nv_ref):
        g = jnp.where(pl.program_id(0) // nh == wh_ref[1], a_ref[...], b_ref[...])
        d, nm, nv = _adamw_math(w_ref[...], g, m_ref[...], v_ref[...])
        g_ref[...] = g
        d_ref[...] = d
        nm_ref[...] = nm
        nv_ref[...] = nv

    blk = pl.BlockSpec((tr, cols), lambda i, wh: (i, 0))
    half = pl.BlockSpec((tr, cols), lambda i, wh: (i % nh, 0))
    return pl.pallas_call(
        body, name=name,
        grid_spec=pltpu.PrefetchScalarGridSpec(
            num_scalar_prefetch=1, grid=(rows // tr,),
            in_specs=[blk, half, half, blk, blk], out_specs=[blk] * 4),
        out_shape=[jax.ShapeDtypeStruct((rows, cols), F32)] * 4,
        compiler_params=_params(("parallel",)),
    )(where, w, mine, theirs, m, v)


SEG_LOSS = 0
SEG_SINK = 128
SEG_AGAIN = 256
SEG_L0 = SEG_AGAIN + ATTN_W
SEG_L1 = SEG_L0 + RNN_W
SEG_RGAIN = SEG_L1 + RNN_W
SEG_G = SEG_RGAIN + 128
N_PACK = SEG_G + 4 * D_MODEL


def _pack(sinks, again, l0, l1, rgain, gains, loss=None):
    z = lambda k: jnp.zeros((1, k), F32)
    first = z(128) if loss is None else loss
    return jnp.concatenate([first, sinks, z(128 - N_Q), again, l0, l1, rgain] + list(gains), axis=1)


def _small_reduce_adamw(part, w, m, v, *, name):
    def body(p_ref, w_ref, m_ref, v_ref, g_ref, d_ref, nm_ref, nv_ref, buf_ref, send_sems, recv_sems):
        x, y, c = _place()
        me = 4 * x + 2 * y + c
        copies = []
        for k in range(1, 8):
            dx, dy, dc = (k >> 2) & 1, (k >> 1) & 1, k & 1
            to = (x ^ dx, y ^ dy, c ^ dc)
            cp = pltpu.make_async_remote_copy(
                src_ref=p_ref, dst_ref=buf_ref.at[me],
                send_sem=send_sems.at[k - 1], recv_sem=recv_sems.at[k - 1],
                device_id=to, device_id_type=MESH)
            cp.start()
            copies.append(cp)
        buf_ref[me] = p_ref[...]
        for cp in copies:
            cp.wait()
        tot = buf_ref[0]
        for j in range(1, 8):
            tot = tot + buf_ref[j]
        g_ref[...] = tot
        l0 = w_ref[:, SEG_L0:SEG_L0 + RNN_W]
        l1 = w_ref[:, SEG_L1:SEG_L1 + RNN_W]
        mx = jnp.maximum(l0, l1)
        e0 = jnp.exp(l0 - mx)
        e1 = jnp.exp(l1 - mx)
        lb = e0 / (e0 + e1)
        gl0 = tot[:, SEG_L0:SEG_L0 + RNN_W] * lb * (1.0 - lb)
        g_ref[:, SEG_L0:SEG_L0 + RNN_W] = gl0
        g_ref[:, SEG_L1:SEG_L1 + RNN_W] = -gl0
        d, nm, nv = _adamw_math(w_ref[...], g_ref[...], m_ref[...], v_ref[...])
        d_ref[...] = d
        nm_ref[...] = nm
        nv_ref[...] = nv

    vm = pl.BlockSpec(memory_space=pltpu.VMEM)
    return pl.pallas_call(
        body, name=name,
        in_specs=[vm] * 4, out_specs=[vm] * 4,
        out_shape=[jax.ShapeDtypeStruct((1, N_PACK), F32)] * 4,
        scratch_shapes=[pltpu.VMEM((8, 1, N_PACK), F32), pltpu.SemaphoreType.DMA((7,)),
                        pltpu.SemaphoreType.DMA((7,))],
    )(part, w, m, v)


def _layer_grads(xs, tgt, bufs, where, sinks, again, lb_logits, rgain,
                 g_mix_pre, g_mix_post, g_mlp_pre, g_mlp_post):
    tm = 512
    b_in, b_out, b_up, b_dn = bufs

    shard = IN_W // N_CHIPS
    w_in4 = _all_gather_halves([b_in], name="gather_w_in")[0].reshape(N_CHIPS, D_MODEL, shard)
    w_in = w_in4.transpose(1, 0, 2).reshape(D_MODEL, IN_W)
    h1 = _rms_cast(xs, g_mix_pre, tm=tm, name="h1_norm")
    proj, ((b_out,), (b_up,)) = _mm(
        h1, w_in, tm=1024, tn=768, tk=D_MODEL, out_dtype=F32, name="in_proj",
        exchanges=[_x_gather_ici([b_out]), _x_gather_ici([b_up], rows=[(0, 384)])])
    attn, lse, ((b_up,), (b_out,)) = _swa_fwd(
        proj, sinks, name="swa_fwd",
        exchanges=[_x_gather_ici([b_up], rows=[(384, 320)]), _x_gather_d2d([b_out])])
    w_out = b_out.reshape(D_MODEL, D_MODEL)
    o_pre, rnn, s0, ((b_up,), (b_dn,)) = _hgrn_fwd(
        proj, lb_logits, rgain, tb=512, name="hgrn_fwd",
        exchanges=[_x_gather_ici([b_up], rows=[(704, 320)]), _x_gather_ici([b_dn], rows=[(0, 960)])])
    cat = _mix_cat(attn, rnn, again, tm=tm, name="mix_cat")
    mixed, ((b_up,), (b_dn,)) = _mm(
        cat, w_out, tm=1024, tn=1024, tk=D_MODEL, out_dtype=F32, name="out_proj",
        exchanges=[_x_gather_d2d([b_up]), _x_gather_ici([b_dn], rows=[(960, 64)])])
    w_up4 = b_up.reshape(N_CHIPS, D_MODEL, D_FF // N_CHIPS)
    x1, h2 = _post_norm_res(mixed, g_mix_post, xs, g_mlp_pre, tm=256, name="mix_post")
    u, ((b_dn,),) = _mm(h2, w_up4, tm=1024, tn=1024, tk=D_MODEL, out_dtype=BF16, relu=True, w_layout="skn",
                        name="mlp_up", exchanges=[_x_gather_d2d([b_dn])])
    w_dn = b_dn.reshape(D_FF, D_MODEL)
    yv = _mm(u, w_dn, tm=1024, tn=1024, tk=2048, out_dtype=F32, a_square=True, name="mlp_down")
    dy, dx2, loss_row, dg_mlp_post = _loss_head(yv, g_mlp_post, x1, tgt, tm=256, name="loss_head")

    def halved(g):
        return g.reshape(N_CHIPS, 2, g.shape[1] // 2, g.shape[2])
    du = _mm(dy, w_dn, tm=1024, tn=1024, tk=D_MODEL, out_dtype=BF16, mul2=u, w_layout="nk", name="mlp_down_bwd")
    g_dn = halved(_mm_tn(u, dy, tm=1024, tn=1024, tt=2048, a_square=True, name="w_down_grad")
                  .reshape(N_CHIPS, D_FF // N_CHIPS, D_MODEL))
    d_w_up, ((sib_dn,),) = _mm_tn(h2, du, tm=1024, tn=1024, tt=2048, n_split=N_CHIPS, name="w_up_grad",
                                  exchanges=[_x_pair([g_dn])])
    g_up = halved(d_w_up)
    wire_dn = _pair_sum(g_dn, sib_dn, where, name="pair_sum_w_down")
    dh2, ((recv_dn,), (sib_up,)) = _mm(du, w_up4, tm=1024, tn=1024, tk=2048, out_dtype=F32, w_layout="snk", name="mlp_up_bwd",
                                       exchanges=[_x_chip([wire_dn], rows=[(0, 800)]), _x_pair([g_up])])
    wire_up = _pair_sum(g_up, sib_up, where, name="pair_sum_w_up")
    dx1, dg_mlp_pre, ((recv_dn,),) = _rms_bwd(dh2, x1, g_mlp_pre, dx2, tm=256, out_dtype=F32, name="mlp_pre_bwd",
                                              exchanges=[_x_chip([wire_dn], rows=[(800, 224)], into=[recv_dn])])
    fin_dn = _final_half(g_dn, sib_dn, recv_dn, where, name="final_half_w_down")
    dmixed, dg_mix_post = _rms_bwd(dx1, mixed, g_mix_post, None, tm=256, out_dtype=BF16, name="mix_post_bwd")
    d_w_out, ((oth_dn,),) = _mm_tn(cat, dmixed, tm=1024, tn=1024, tt=2048, name="w_out_grad",
                                   exchanges=[_x_share([fin_dn])])
    g_out = halved(d_w_out.reshape(N_CHIPS, D_MODEL // N_CHIPS, D_MODEL))
    dcat, ((sib_out,),) = _mm(dmixed, w_out, tm=1024, tn=1024, tk=D_MODEL, out_dtype=F32, w_layout="nk", name="out_proj_bwd",
                              exchanges=[_x_pair([g_out])])
    wire_out = _pair_sum(g_out, sib_out, where, name="pair_sum_w_out")
    dattn, dg_again = _rms_bwd(dcat, attn, again, None, tm=tm, out_dtype=F32, name="attn_norm_bwd")
    dq_a, dkv, dsinks, ((recv_out,), (recv_up,)) = _swa_bwd(
        proj, sinks, dattn, lse, name="swa_bwd",
        exchanges=[_x_chip([wire_out]), _x_chip([wire_up], rows=[(0, 448)])])
    dq_r, df_r, di_r, dg_r, dlb, dgain_h, ((recv_up,),) = _hgrn_bwd(
        proj, lb_logits, rgain, o_pre, s0, dcat, tb=512, name="hgrn_bwd",
        exchanges=[_x_chip([wire_up], rows=[(448, 576)], into=[recv_up])])
    fin_up = _final_half(g_up, sib_up, recv_up, where, name="final_half_w_up")
    fin_out = _final_half(g_out, sib_out, recv_out, where, name="final_half_w_out")
    dproj = jnp.concatenate([dq_a, dkv, dq_r, df_r, di_r, dg_r], axis=1).astype(BF16)
    dproj4 = dproj.reshape(dproj.shape[0], N_CHIPS, shard).transpose(1, 0, 2)
    piece_rows = D_MODEL // 4

    def w_in_piece(pc, exchanges):
        d, xres = _mm_tn(h1, dproj4, tm=piece_rows, tn=shard, tt=2048, n_split=N_CHIPS, b_slabs=True,
                         m_blocks=(2, 2, pc), name="w_in_grad_%d" % pc, exchanges=exchanges)
        return d.reshape(N_CHIPS, 2, piece_rows, shard), xres

    g_in0, ((oth_up, oth_out),) = w_in_piece(0, [_x_share([fin_up, fin_out])])
    g_in1, ((sib_in0,),) = w_in_piece(1, [_x_pair([g_in0])])
    wire_in0 = _pair_sum(g_in0, sib_in0, where, name="pair_sum_w_in_0")
    dh1, ((recv_in0,), (sib_in1,)) = _mm(
        dproj4, w_in4, tm=1024, tn=1024, tk=shard, out_dtype=F32, w_layout="snk", a_slabs=True, name="in_proj_bwd",
        exchanges=[_x_chip([wire_in0]), _x_pair([g_in1])])
    wire_in1 = _pair_sum(g_in1, sib_in1, where, name="pair_sum_w_in_1")
    gx, dg_mix_pre, ((recv_in1,),) = _rms_bwd(dh1, xs, g_mix_pre, dx1, tm=256, out_dtype=F32, name="mix_pre_bwd",
                                              exchanges=[_x_chip([wire_in1])])
    fin_in0 = _final_half(g_in0, sib_in0, recv_in0, where, name="final_half_w_in_0")
    fin_in1 = _final_half(g_in1, sib_in1, recv_in1, where, name="final_half_w_in_1")
    oth_in0, oth_in1 = _run_exchange(_x_share([fin_in0, fin_in1]), name="share_w_in")
    fin_in = jnp.concatenate([fin_in0, fin_in1], axis=0)
    oth_in = jnp.concatenate([oth_in0, oth_in1], axis=0)

    big = [(fin_in, oth_in), (fin_out, oth_out), (fin_up, oth_up), (fin_dn, oth_dn)]
    drgain = jnp.sum(dgain_h, axis=0)
    small = _pack(jnp.sum(dsinks, axis=1)[None, :], dg_again, dlb, jnp.zeros_like(dlb), drgain,
                  [dg_mix_pre, dg_mix_post, dg_mlp_pre, dg_mlp_post], loss=loss_row)
    return gx, big, small


def kernel(x, w_in, attn_sinks, attn_out_gain, rnn_lb_logits, rnn_norm_gain, w_out, mix_pre_gain, mix_post_gain, mlp_pre_gain, mlp_post_gain, w_up, w_down, loss_target, m_w_in, m_attn_sinks, m_attn_out_gain, m_rnn_lb_logits, m_rnn_norm_gain, m_w_out, m_mix_pre_gain, m_mix_post_gain, m_mlp_pre_gain, m_mlp_post_gain, m_w_up, m_w_down, v_w_in, v_attn_sinks, v_attn_out_gain, v_rnn_lb_logits, v_rnn_norm_gain, v_w_out, v_mix_pre_gain, v_mix_post_gain, v_mlp_pre_gain, v_mlp_post_gain, v_w_up, v_w_down):
    ax, ay, ac = _place()
    where = jnp.stack([2 * ax + ay, ac]).astype(jnp.int32)
    big_w = [w_in[0], w_out[0], w_up[0], w_down[0]]
    big_m = [m_w_in[0], m_w_out[0], m_w_up[0], m_w_down[0]]
    big_v = [v_w_in[0], v_w_out[0], v_w_up[0], v_w_down[0]]

    names = ["w_in", "w_out", "w_up", "w_down"]
    bufs = [_cast_slots(w, where, name="cast_" + nm) for w, nm in zip(big_w, names)]
    gx, big_g, small_part = _layer_grads(
        x[0], loss_target[0], bufs, where, attn_sinks, attn_out_gain, rnn_lb_logits, rnn_norm_gain,
        mix_pre_gain, mix_post_gain, mlp_pre_gain, mlp_post_gain)

    grads, deltas, new_m, new_v = [], [], [], []
    for (f, o), w, m, v, nm in zip(big_g, big_w, big_m, big_v, names):
        g, d, nm_, nv_ = _adamw(w, f, o, m, v, where, name="adamw_" + nm)
        grads.append(g[None])
        deltas.append(d[None])
        new_m.append(nm_[None])
        new_v.append(nv_[None])

    def pack_params(sinks, again, logits, rgain, gains):
        return _pack(sinks, again, logits[0:1], logits[1:2], rgain, gains)

    pw = pack_params(attn_sinks, attn_out_gain, rnn_lb_logits, rnn_norm_gain,
                     [mix_pre_gain, mix_post_gain, mlp_pre_gain, mlp_post_gain])
    pm = pack_params(m_attn_sinks, m_attn_out_gain, m_rnn_lb_logits, m_rnn_norm_gain,
                     [m_mix_pre_gain, m_mix_post_gain, m_mlp_pre_gain, m_mlp_post_gain])
    pv = pack_params(v_attn_sinks, v_attn_out_gain, v_rnn_lb_logits, v_rnn_norm_gain,
                     [v_mix_pre_gain, v_mix_post_gain, v_mlp_pre_gain, v_mlp_post_gain])
    packs = _small_reduce_adamw(small_part, pw, pm, pv, name="small_reduce_adamw")

    def unpack(p):
        seg = lambda o, k: p[:, o:o + k]
        logits = jnp.concatenate([seg(SEG_L0, RNN_W), seg(SEG_L1, RNN_W)], axis=0)
        gains = [seg(SEG_G + i * D_MODEL, D_MODEL) for i in range(4)]
        return dict(sinks=seg(SEG_SINK, N_Q), again=seg(SEG_AGAIN, ATTN_W), logits=logits,
                    rgain=seg(SEG_RGAIN, RNN_HD), gains=gains)

    def order(small, big):
        return [big[0], small["sinks"], small["again"], small["logits"], small["rgain"], big[1],
                *small["gains"], big[2], big[3]]

    loss = packs[0][0, 0]
    outs = [loss, gx[None]]
    for p, b in zip(packs, [grads, deltas, new_m, new_v]):
        outs += order(unpack(p), b)
    return tuple(outs)
```

```python
import functools

import jax
import jax.numpy as jnp
from jax import lax
from jax.experimental import pallas as pl
from jax.experimental.pallas import tpu as pltpu

F32 = jnp.float32
BF16 = jnp.bfloat16
MESH = pl.DeviceIdType.MESH

EPS = 1e-6
D_MODEL = 2048
ATTN_W = 1024
HEAD_DIM = 64
N_Q = 16
N_KV = 2
GROUP = 8
BLK = 128
RNN_W = 1024
RNN_HD = 128
N_RNN = 8
CHUNK = 64
SUB_FWD = 16
SUB_BWD = 8
D_FF = 8192
IN_W = 5376
N_CHIPS = 4
KV_COL = ATTN_W
QR_COL = ATTN_W + 2 * 128
FR_COL = QR_COL + RNN_W
IR_COL = FR_COL + RNN_W
GR_COL = IR_COL + RNN_W

ADAM_LR = 0.001
ADAM_B1 = 0.9
ADAM_B2 = 0.999
ADAM_EPS = 1e-08
ADAM_WD = 0.01
ADAM_STEP = 10

VMEM_LIMIT = 48 * 1024 * 1024

NT = (((1,), (1,)), ((), ()))
TN = (((0,), (0,)), ((), ()))


def _params(sem=None):
    return pltpu.CompilerParams(dimension_semantics=sem, vmem_limit_bytes=VMEM_LIMIT)


def _sigmoid(x):
    return 1.0 / (1.0 + jnp.exp(-x))


ANY = pl.BlockSpec(memory_space=pl.ANY)


def _place():
    return lax.axis_index("x"), lax.axis_index("y"), lax.axis_index("c")


def _other_chips(x, y):
    return [(1 - x, y), (x, 1 - y), (1 - x, 1 - y)]


class _Exchange:
    def __init__(self, srcs, outs, ncopy, build, aliases=None):
        self.srcs, self.outs, self.ncopy, self.build = list(srcs), list(outs), ncopy, build
        self.aliases = aliases or {}


def _remote(src, dst, send_sems, recv_sems, k, to):
    return pltpu.make_async_remote_copy(src_ref=src, dst_ref=dst, send_sem=send_sems.at[k],
                                        recv_sem=recv_sems.at[k], device_id=to, device_id_type=MESH)


def _call(body, *, name, grid, in_specs, out_specs, out_shape, args, scratch_shapes=(), semantics=None,
          exchanges=()):
    in_specs, out_specs, out_shape = list(in_specs), list(out_specs), list(out_shape)
    scratch_shapes = list(scratch_shapes)
    ni, no, ns = len(in_specs), len(out_specs), len(scratch_shapes)
    xsrc = [s for x in exchanges for s in x.srcs]
    xout = [o for x in exchanges for o in x.outs]
    nxi, nxo = len(xsrc), len(xout)
    aliases = {}
    a0 = b0 = 0
    for x in exchanges:
        for si, oi in x.aliases.items():
            aliases[ni + a0 + si] = no + b0 + oi
        a0 += len(x.srcs)
        b0 += len(x.outs)
    sems = []
    for x in exchanges:
        sems += [pltpu.SemaphoreType.DMA((x.ncopy,)), pltpu.SemaphoreType.DMA((x.ncopy,))]

    def wrapped(*refs):
        ins, xi = refs[:ni], refs[ni:ni + nxi]
        outs, xo = refs[ni + nxi:ni + nxi + no], refs[ni + nxi + no:ni + nxi + no + nxo]
        rest = refs[ni + nxi + no + nxo:]
        scr, sm = rest[:ns], rest[ns:]

        def copies():
            cps = []
            a = b = 0
            for k, x in enumerate(exchanges):
                cps += x.build(xi[a:a + len(x.srcs)], xo[b:b + len(x.outs)], sm[2 * k], sm[2 * k + 1])
                a += len(x.srcs)
                b += len(x.outs)
            return cps

        def start():
            for cp in copies():
                cp.start()

        def wait():
            for cp in copies():
                cp.wait()

        if not exchanges:
            body(*ins, *outs, *scr)
        elif not grid:
            start()
            body(*ins, *outs, *scr)
            wait()
        else:
            first = last = None
            for ax, g in enumerate(grid):
                f = pl.program_id(ax) == 0
                l = pl.program_id(ax) == g - 1
                first = f if first is None else first & f
                last = l if last is None else last & l
            pl.when(first)(start)
            body(*ins, *outs, *scr)
            pl.when(last)(wait)

    if exchanges and semantics is not None:
        semantics = ("arbitrary",) * len(grid)
    kwargs = dict(grid=grid) if grid else {}
    res = pl.pallas_call(
        wrapped, name=name,
        in_specs=in_specs + [ANY] * nxi, out_specs=out_specs + [ANY] * nxo,
        out_shape=out_shape + xout, scratch_shapes=scratch_shapes + sems,
        input_output_aliases=aliases,
        compiler_params=_params(semantics), **kwargs,
    )(*args, *xsrc)
    res = list(res)
    mine, theirs = res[:no], res[no:]
    per = []
    b = 0
    for x in exchanges:
        per.append(theirs[b:b + len(x.outs)])
        b += len(x.outs)
    return mine, per


def _run_exchange(x, *, name):
    return _call(lambda: None, name=name, grid=(), in_specs=[], out_specs=[], out_shape=[], args=[],
                 exchanges=[x])[1][0]


def _x_gather(bufs, ici=None, d2d=None):
    n = len(bufs)
    plan = [(a, kind, rows[a]) for a in range(n) for kind, rows in (("ici", ici), ("d2d", d2d))
            if rows is not None and rows[a] is not None]

    def build(srcs, outs, ss, rs):
        x, y, c = _place()
        cps = []
        for q, (a, kind, rows) in enumerate(plan):
            piece = pl.ds(*rows)
            for j, (px, py) in enumerate(_other_chips(x, y)):
                slot, to = (4 * x + 2 * y + c, (px, py, c)) if kind == "ici" else (4 * px + 2 * py + c, (x, y, 1 - c))
                cps.append(_remote(srcs[a].at[slot, piece], outs[a].at[slot, piece], ss, rs, 3 * q + j, to))
        return cps

    outs = [jax.ShapeDtypeStruct(b.shape, b.dtype) for b in bufs]
    return _Exchange(bufs, outs, 3 * len(plan), build, aliases={a: a for a in range(n)})


def _x_pair(grads):
    n = len(grads)

    def build(srcs, outs, ss, rs):
        x, y, c = _place()
        return [_remote(srcs[a].at[:, 1 - c], outs[a], ss, rs, a, (x, y, 1 - c)) for a in range(n)]

    outs = [jax.ShapeDtypeStruct((4,) + g.shape[2:], g.dtype) for g in grads]
    return _Exchange(grads, outs, n, build)


def _x_chip(wires, rows=None, into=None):
    n = len(wires)
    rows = rows or [(0, w.shape[1]) for w in wires]

    def build(srcs, outs, ss, rs):
        x, y, c = _place()
        cps = []
        for a in range(n):
            piece = pl.ds(*rows[a])
            for j, (px, py) in enumerate(_other_chips(x, y)):
                cps.append(_remote(srcs[a].at[2 * px + py, piece], outs[a].at[j, piece], ss, rs,
                                   3 * a + j, (px, py, c)))
        return cps

    outs = [jax.ShapeDtypeStruct((3,) + w.shape[1:], w.dtype) for w in wires]
    if into is None:
        return _Exchange(wires, outs, 3 * n, build)
    return _Exchange(list(wires) + list(into), outs, 3 * n, build, aliases={n + a: a for a in range(n)})


def _x_share(halves):
    n = len(halves)

    def build(srcs, outs, ss, rs):
        x, y, c = _place()
        return [_remote(srcs[a], outs[a], ss, rs, a, (x, y, 1 - c)) for a in range(n)]

    outs = [jax.ShapeDtypeStruct(h.shape, h.dtype) for h in halves]
    return _Exchange(halves, outs, n, build)


def _mm(a, w, *, tm, tn, tk, out_dtype, name, a_square=False, relu=False, mul2=None, w_layout="kn",
        a_slabs=False, exchanges=()):
    if a_slabs:
        m, k = a.shape[1], a.shape[0] * a.shape[2]
        per_ka = a.shape[2] // tk
        a_spec = pl.BlockSpec((None, tm, tk), lambda i, j, kk: (kk // per_ka, i, kk % per_ka))
    else:
        m, k = a.shape
        a_spec = pl.BlockSpec((tm, tk), lambda i, j, kk: (i, kk))
    if w_layout == "kn":
        n = w.shape[1]
        w_spec = pl.BlockSpec((tk, tn), lambda i, j, kk: (kk, j))
    elif w_layout == "nk":
        n = w.shape[0]
        w_spec = pl.BlockSpec((tn, tk), lambda i, j, kk: (j, kk))
    elif w_layout == "skn":
        n = w.shape[0] * w.shape[2]
        per_n = w.shape[2] // tn
        w_spec = pl.BlockSpec((None, tk, tn), lambda i, j, kk: (j // per_n, kk, j % per_n))
    else:
        assert w_layout == "snk"
        n = w.shape[1]
        per_k = w.shape[2] // tk
        w_spec = pl.BlockSpec((None, tn, tk), lambda i, j, kk: (kk // per_k, j, kk % per_k))
    w_dims = NT if w_layout in ("nk", "snk") else (((1,), (0,)), ((), ()))
    nk = k // tk
    assert m % tm == 0 and n % tn == 0 and k % tk == 0

    def body(*refs):
        if mul2 is not None:
            a_ref, w_ref, e_ref, o_ref, acc_ref = refs
        else:
            a_ref, w_ref, o_ref, acc_ref = refs
            e_ref = None
        kk = pl.program_id(2)
        av = a_ref[...]
        if a_square:
            af = av.astype(F32)
            av = (af * af).astype(BF16)
        part = lax.dot_general(av, w_ref[...], w_dims, preferred_element_type=F32)

        def finish(r):
            if relu:
                r = jnp.maximum(r, 0.0)
            if e_ref is not None:
                r = 2.0 * e_ref[...].astype(F32) * r
            o_ref[...] = r.astype(out_dtype)

        if nk == 1:
            finish(part)
        else:
            @pl.when(kk == 0)
            def _():
                acc_ref[...] = part

            @pl.when(kk > 0)
            def _():
                acc_ref[...] += part

            @pl.when(kk == nk - 1)
            def _():
                finish(acc_ref[...])

    in_specs = [a_spec, w_spec]
    args = [a, w]
    if mul2 is not None:
        in_specs.append(pl.BlockSpec((tm, tn), lambda i, j, kk: (i, j)))
        args.append(mul2)
    acc_shape = (tm, tn) if nk > 1 else (8, 128)
    (out,), per = _call(
        body, name=name, grid=(m // tm, n // tn, nk),
        in_specs=in_specs, out_specs=[pl.BlockSpec((tm, tn), lambda i, j, kk: (i, j))],
        out_shape=[jax.ShapeDtypeStruct((m, n), out_dtype)], args=args,
        scratch_shapes=[pltpu.VMEM(acc_shape, F32)],
        semantics=("parallel", "parallel", "arbitrary"), exchanges=exchanges)
    return (out, per) if exchanges else out


def _mm_tn(a, b, *, tm, tn, tt, name, a_square=False, n_split=1, m_blocks=None, b_slabs=False, exchanges=()):
    t, m = a.shape
    if b_slabs:
        n = b.shape[0] * b.shape[2]
        assert b.shape[0] == n_split and b.shape[2] == tn
        b_spec = pl.BlockSpec((None, tt, tn), lambda i, j, ti: (j, ti, 0))
    else:
        n = b.shape[1]
        b_spec = pl.BlockSpec((tt, tn), lambda i, j, ti: (ti, j))
    assert t % tt == 0 and m % tm == 0 and n % tn == 0 and (n // n_split) % tn == 0
    per = n // n_split // tn
    count, stride, first = m_blocks or (m // tm, 1, 0)
    m = count * tm

    def body(a_ref, b_ref, o_ref):
        ti = pl.program_id(2)
        av = a_ref[...]
        if a_square:
            af = av.astype(F32)
            av = (af * af).astype(BF16)
        part = lax.dot_general(av, b_ref[...], TN, preferred_element_type=F32)

        @pl.when(ti == 0)
        def _():
            o_ref[...] = part

        @pl.when(ti > 0)
        def _():
            o_ref[...] += part

    (out,), xres = _call(
        body, name=name, grid=(m // tm, n // tn, t // tt),
        in_specs=[pl.BlockSpec((tt, tm), lambda i, j, ti: (ti, first + stride * i)), b_spec],
        out_specs=[pl.BlockSpec((None, tm, tn), lambda i, j, ti: (j // per, i, j % per))],
        out_shape=[jax.ShapeDtypeStruct((n_split, m, n // n_split), F32)], args=[a, b],
        semantics=("parallel", "parallel", "arbitrary"), exchanges=exchanges)
    return (out, xres) if exchanges else out


def _rstd(x):
    return lax.rsqrt(jnp.mean(x * x, axis=-1, keepdims=True) + EPS)


def _rms_cast(x, g, *, tm, name):
    t, d = x.shape

    def body(x_ref, g_ref, o_ref):
        xv = x_ref[...]
        o_ref[...] = (xv * _rstd(xv) * g_ref[...]).astype(BF16)

    return pl.pallas_call(
        body, name=name, grid=(t // tm,),
        in_specs=[pl.BlockSpec((tm, d), lambda i: (i, 0)), pl.BlockSpec((1, d), lambda i: (0, 0))],
        out_specs=pl.BlockSpec((tm, d), lambda i: (i, 0)),
        out_shape=jax.ShapeDtypeStruct((t, d), BF16),
        compiler_params=_params(("parallel",)),
    )(x, g)


def _mix_cat(attn, rnn, gain, *, tm, name):
    t = attn.shape[0]

    def body(a_ref, r_ref, g_ref, o_ref):
        av = a_ref[...]
        o_ref[:, :ATTN_W] = (av * _rstd(av) * g_ref[...]).astype(BF16)
        o_ref[:, ATTN_W:] = r_ref[...].astype(BF16)

    return pl.pallas_call(
        body, name=name, grid=(t // tm,),
        in_specs=[pl.BlockSpec((tm, ATTN_W), lambda i: (i, 0)), pl.BlockSpec((tm, RNN_W), lambda i: (i, 0)),
                  pl.BlockSpec((1, ATTN_W), lambda i: (0, 0))],
        out_specs=pl.BlockSpec((tm, D_MODEL), lambda i: (i, 0)),
        out_shape=jax.ShapeDtypeStruct((t, D_MODEL), BF16),
        compiler_params=_params(("parallel",)),
    )(attn, rnn, gain)


def _post_norm_res(mixed, g_post, res, g_next, *, tm, name):
    t, d = mixed.shape

    def body(m_ref, gp_ref, r_ref, gn_ref, x1_ref, h2_ref):
        mv = m_ref[...]
        x1 = r_ref[...] + mv * _rstd(mv) * gp_ref[...]
        x1_ref[...] = x1
        h2_ref[...] = (x1 * _rstd(x1) * gn_ref[...]).astype(BF16)

    row = pl.BlockSpec((tm, d), lambda i: (i, 0))
    vec = pl.BlockSpec((1, d), lambda i: (0, 0))
    return pl.pallas_call(
        body, name=name, grid=(t // tm,),
        in_specs=[row, vec, row, vec], out_specs=[row, row],
        out_shape=[jax.ShapeDtypeStruct((t, d), F32), jax.ShapeDtypeStruct((t, d), BF16)],
        compiler_params=_params(("parallel",)),
    )(mixed, g_post, res, g_next)


def _rms_bwd(dyn, xin, g, res, *, tm, out_dtype, name, col_block=0, exchanges=()):
    t, d = xin.shape

    def body(*refs):
        if res is not None:
            dy_ref, x_ref, g_ref, r_ref, dx_ref, dg_ref = refs
        else:
            dy_ref, x_ref, g_ref, dx_ref, dg_ref = refs
        i = pl.program_id(0)
        xv = x_ref[...]
        dy = dy_ref[...].astype(F32)
        r = _rstd(xv)
        xh = xv * r
        part = jnp.sum(dy * xh, axis=0, keepdims=True)

        @pl.when(i == 0)
        def _():
            dg_ref[...] = part

        @pl.when(i > 0)
        def _():
            dg_ref[...] += part

        tt = dy * g_ref[...]
        dx = r * (tt - xh * jnp.mean(tt * xh, axis=-1, keepdims=True))
        if res is not None:
            dx = dx + r_ref[...]
        dx_ref[...] = dx.astype(out_dtype)

    row = pl.BlockSpec((tm, d), lambda i: (i, 0))
    vec = pl.BlockSpec((1, d), lambda i: (0, 0))
    in_specs = [pl.BlockSpec((tm, d), lambda i: (i, col_block)), row, vec]
    args = [dyn, xin, g]
    if res is not None:
        in_specs.append(row)
        args.append(res)
    res, xres = _call(
        body, name=name, grid=(t // tm,),
        in_specs=in_specs, out_specs=[row, vec],
        out_shape=[jax.ShapeDtypeStruct((t, d), out_dtype), jax.ShapeDtypeStruct((1, d), F32)], args=args,
        semantics=("arbitrary",), exchanges=exchanges)
    return (*res, xres) if exchanges else res


def _loss_head(y, g_post, x1, target, *, tm, name):
    t, d = y.shape

    def body(y_ref, g_ref, x1_ref, t_ref, dy_ref, dx2_ref, loss_ref, dg_ref):
        i = pl.program_id(0)
        yv = y_ref[...]
        r = _rstd(yv)
        yh = yv * r
        gv = g_ref[...]
        err = x1_ref[...] + yh * gv - t_ref[...]
        lpart = 0.5 * jnp.sum(jnp.mean(err * err, axis=-1, keepdims=True), axis=0, keepdims=True)
        dx2 = err * (1.0 / d)
        dgp = jnp.sum(dx2 * yh, axis=0, keepdims=True)
        lane = lax.broadcasted_iota(jnp.int32, (1, 128), 1)
        lrow = jnp.where(lane == 0, lpart, 0.0)

        @pl.when(i == 0)
        def _():
            dg_ref[...] = dgp
            loss_ref[...] = lrow

        @pl.when(i > 0)
        def _():
            dg_ref[...] += dgp
            loss_ref[...] += lrow

        tt = dx2 * gv
        dy_ref[...] = (r * (tt - yh * jnp.mean(tt * yh, axis=-1, keepdims=True))).astype(BF16)
        dx2_ref[...] = dx2

    row = pl.BlockSpec((tm, d), lambda i: (i, 0))
    vec = pl.BlockSpec((1, d), lambda i: (0, 0))
    return pl.pallas_call(
        body, name=name, grid=(t // tm,),
        in_specs=[row, vec, row, row],
        out_specs=[row, row, pl.BlockSpec((1, 128), lambda i: (0, 0)), vec],
        out_shape=[jax.ShapeDtypeStruct((t, d), BF16), jax.ShapeDtypeStruct((t, d), F32),
                   jax.ShapeDtypeStruct((1, 128), F32), jax.ShapeDtypeStruct((1, d), F32)],
        compiler_params=_params(("arbitrary",)),
    )(y, g_post, x1, target)


def _alibi_slope(h):
    return 2.0 ** (-8.0 * (h + 1) / N_Q)


PAIR = 2 * HEAD_DIM
N_PAIRS = N_Q // 2
PAIRS_PER_KV = GROUP // 2
SMEM = pl.BlockSpec(memory_space=pltpu.SMEM)


def _swa_mask(n):
    key = lax.broadcasted_iota(jnp.int32, (2 * BLK, BLK), 0)
    qry = lax.broadcasted_iota(jnp.int32, (2 * BLK, BLK), 1)
    dist = qry + BLK - key
    valid = (dist >= 0) & (dist < BLK) & ((key >= BLK) | (n > 0))
    return valid, dist.astype(F32)


def _block_diag(kvp_ref, kvc_ref, off):
    a = jnp.concatenate([kvp_ref[:, off:off + HEAD_DIM], kvc_ref[:, off:off + HEAD_DIM]], axis=0).astype(BF16)
    z = jnp.zeros_like(a)
    return jnp.concatenate([jnp.concatenate([a, z], axis=1), jnp.concatenate([z, a], axis=1)], axis=0)


def _swa_scores(s2, e, hh, valid, distf):
    s = s2[2 * BLK * e:2 * BLK * (e + 1)] * (HEAD_DIM ** -0.5) - _alibi_slope(hh) * distf
    return jnp.where(valid, s, -1e30)


def _swa_fwd(proj, sinks, *, name, exchanges=()):
    t = proj.shape[0]
    nb = t // BLK
    kvb = KV_COL // (2 * 128)

    def body(sink_ref, q_ref, kvc_ref, kvp_ref, o_ref, lse_ref):
        n = pl.program_id(0)
        valid, distf = _swa_mask(n)
        for kvh in range(N_KV):
            k2 = _block_diag(kvp_ref, kvc_ref, kvh * HEAD_DIM)
            v2 = _block_diag(kvp_ref, kvc_ref, 128 + kvh * HEAD_DIM)
            for jp in range(PAIRS_PER_KV):
                pair = kvh * PAIRS_PER_KV + jp
                lanes = slice(pair * PAIR, (pair + 1) * PAIR)
                s2 = lax.dot_general(k2, q_ref[:, lanes].astype(BF16), NT, preferred_element_type=F32)
                probs = []
                for e in range(2):
                    hh = 2 * pair + e
                    s = _swa_scores(s2, e, hh, valid, distf)
                    sink = sink_ref[0, hh]
                    mx = jnp.maximum(jnp.max(s, axis=0, keepdims=True), sink)
                    p = jnp.exp(s - mx)
                    l = jnp.sum(p, axis=0, keepdims=True) + jnp.exp(sink - mx)
                    probs.append((p * (1.0 / l)).astype(BF16))
                    lse_ref[hh:hh + 1, :] = mx + jnp.log(l)
                o_ref[:, lanes] = lax.dot_general(jnp.concatenate(probs, axis=0), v2, TN,
                                                  preferred_element_type=F32)

    res, xres = _call(
        body, name=name, grid=(nb,),
        in_specs=[SMEM,
                  pl.BlockSpec((BLK, ATTN_W), lambda n: (n, 0)),
                  pl.BlockSpec((BLK, 256), lambda n: (n, kvb)),
                  pl.BlockSpec((BLK, 256), lambda n: (jnp.maximum(n - 1, 0), kvb))],
        out_specs=[pl.BlockSpec((BLK, ATTN_W), lambda n: (n, 0)),
                   pl.BlockSpec((None, N_Q, BLK), lambda n: (n, 0, 0))],
        out_shape=[jax.ShapeDtypeStruct((t, ATTN_W), F32), jax.ShapeDtypeStruct((nb, N_Q, BLK), F32)],
        args=[sinks, proj, proj, proj], semantics=("parallel",), exchanges=exchanges)
    return (*res, xres) if exchanges else res


def _swa_bwd(proj, sinks, dattn, lse, *, name, exchanges=()):
    t = proj.shape[0]
    nb = t // BLK
    kvb = KV_COL // (2 * 128)

    def body(sink_ref, q_ref, kvc_ref, kvp_ref, do_ref, lse_ref, dq_ref, dkv_ref, dsink_ref, carry_ref):
        n = pl.program_id(0)

        @pl.when(n == 0)
        def _():
            dsink_ref[...] = jnp.zeros_like(dsink_ref)
            carry_ref[...] = jnp.zeros_like(carry_ref)

        @pl.when(n < nb)
        def _():
            valid, distf = _swa_mask(n)
            for kvh in range(N_KV):
                k2 = _block_diag(kvp_ref, kvc_ref, kvh * HEAD_DIM)
                v2 = _block_diag(kvp_ref, kvc_ref, 128 + kvh * HEAD_DIM)
                dk2 = jnp.zeros((4 * BLK, PAIR), F32)
                dv2 = jnp.zeros((4 * BLK, PAIR), F32)
                for jp in range(PAIRS_PER_KV):
                    pair = kvh * PAIRS_PER_KV + jp
                    lanes = slice(pair * PAIR, (pair + 1) * PAIR)
                    q2 = q_ref[:, lanes].astype(BF16)
                    do2 = do_ref[:, lanes].astype(BF16)
                    s2 = lax.dot_general(k2, q2, NT, preferred_element_type=F32)
                    dp2 = lax.dot_general(v2, do2, NT, preferred_element_type=F32)
                    probs, dss = [], []
                    for e in range(2):
                        hh = 2 * pair + e
                        lse_h = lse_ref[hh:hh + 1, :]
                        p = jnp.exp(_swa_scores(s2, e, hh, valid, distf) - lse_h)
                        dp = dp2[2 * BLK * e:2 * BLK * (e + 1)]
                        delta = jnp.sum(p * dp, axis=0, keepdims=True)
                        dsink_ref[hh:hh + 1, :] += -jnp.exp(sink_ref[0, hh] - lse_h) * delta
                        probs.append(p.astype(BF16))
                        dss.append((p * (dp - delta)).astype(BF16))
                    ds2 = jnp.concatenate(dss, axis=0)
                    dq_ref[:, lanes] = (lax.dot_general(ds2, k2, TN, preferred_element_type=F32)
                                        * (HEAD_DIM ** -0.5)).astype(BF16)
                    dk2 = dk2 + jnp.dot(ds2, q2, preferred_element_type=F32)
                    dv2 = dv2 + jnp.dot(jnp.concatenate(probs, axis=0), do2, preferred_element_type=F32)
                dk_cat = (dk2[:2 * BLK, :HEAD_DIM] + dk2[2 * BLK:, HEAD_DIM:]) * (HEAD_DIM ** -0.5)
                dv_cat = dv2[:2 * BLK, :HEAD_DIM] + dv2[2 * BLK:, HEAD_DIM:]
                ko = kvh * HEAD_DIM
                vo = 128 + kvh * HEAD_DIM
                dkv_ref[:, ko:ko + HEAD_DIM] = (carry_ref[:, ko:ko + HEAD_DIM] + dk_cat[:BLK]).astype(BF16)
                dkv_ref[:, vo:vo + HEAD_DIM] = (carry_ref[:, vo:vo + HEAD_DIM] + dv_cat[:BLK]).astype(BF16)
                carry_ref[:, ko:ko + HEAD_DIM] = dk_cat[BLK:]
                carry_ref[:, vo:vo + HEAD_DIM] = dv_cat[BLK:]

        @pl.when(n == nb)
        def _():
            dkv_ref[...] = carry_ref[...].astype(BF16)

    last = nb - 1
    res, xres = _call(
        body, name=name, grid=(nb + 1,),
        in_specs=[SMEM,
                  pl.BlockSpec((BLK, ATTN_W), lambda n: (jnp.minimum(n, last), 0)),
                  pl.BlockSpec((BLK, 256), lambda n: (jnp.minimum(n, last), kvb)),
                  pl.BlockSpec((BLK, 256), lambda n: (jnp.maximum(jnp.minimum(n, last) - 1, 0), kvb)),
                  pl.BlockSpec((BLK, ATTN_W), lambda n: (jnp.minimum(n, last), 0)),
                  pl.BlockSpec((None, N_Q, BLK), lambda n: (jnp.minimum(n, last), 0, 0))],
        out_specs=[pl.BlockSpec((BLK, ATTN_W), lambda n: (jnp.minimum(n, last), 0)),
                   pl.BlockSpec((BLK, 256), lambda n: (jnp.maximum(n - 1, 0), 0)),
                   pl.BlockSpec((N_Q, BLK), lambda n: (0, 0))],
        out_shape=[jax.ShapeDtypeStruct((t, ATTN_W), BF16), jax.ShapeDtypeStruct((t, 256), BF16),
                   jax.ShapeDtypeStruct((N_Q, BLK), F32)],
        scratch_shapes=[pltpu.VMEM((BLK, 256), F32)],
        args=[sinks, proj, proj, proj, dattn, lse], semantics=("arbitrary",), exchanges=exchanges)
    return (*res, xres) if exchanges else res


def _cumsum_rows(x):
    n = x.shape[0]
    row = lax.broadcasted_iota(jnp.int32, x.shape, 0)
    s = 1
    while s < n:
        x = x + jnp.where(row >= s, pltpu.roll(x, s, axis=0), 0.0)
        s *= 2
    return x


def _rev_cumsum_rows(x):
    n = x.shape[0]
    row = lax.broadcasted_iota(jnp.int32, x.shape, 0)
    s = 1
    while s < n:
        x = x + jnp.where(row < n - s, pltpu.roll(x, n - s, axis=0), 0.0)
        s *= 2
    return x


def _lower_bound(lbl_ref):
    l0 = lbl_ref[0:1, :]
    l1 = lbl_ref[1:2, :]
    mx = jnp.maximum(l0, l1)
    e0 = jnp.exp(l0 - mx)
    e1 = jnp.exp(l1 - mx)
    return e0 / (e0 + e1)


def _hgrn_gates(z, lb):
    sg = _sigmoid(z)
    f = lb + (1.0 - lb) * sg
    return sg, f, jnp.log(f), 1.0 - f


def _sub_factors(b, i, sub):
    rows = lax.broadcasted_iota(jnp.int32, (CHUNK, RNN_HD), 0)
    ref = b[sub * i - 1:sub * i, :]
    qfac = jnp.exp(b[sub * i:sub * (i + 1), :] - ref)
    kfac = jnp.where(rows < sub * i, jnp.exp(ref - b), 0.0)
    return qfac, kfac


def _diag_decay(bi, s):
    trow = lax.broadcasted_iota(jnp.int32, bi.shape, 0)
    return jnp.where(trow >= s, jnp.exp(bi - bi[s:s + 1, :]), 0.0)


def _hgrn_fwd(proj, lb_logits, norm_gain, *, tb, name, exchanges=()):
    t = proj.shape[0]
    ntb = t // tb
    nch = tb // CHUNK
    qb, fb, ib, gb = QR_COL // 128, FR_COL // 128, IR_COL // 128, GR_COL // 128

    def body(q_ref, f_ref, i_ref, g_ref, lbl_ref, gain_ref, o_ref, out_ref, s0_ref, st_ref):
        c = pl.program_id(1)

        @pl.when(c == 0)
        def _():
            st_ref[...] = jnp.zeros_like(st_ref)

        lb = _lower_bound(lbl_ref)
        gain = gain_ref[...]

        def chunk(ci, st):
            rows = slice(ci * CHUNK, (ci + 1) * CHUNK)
            _, _, lf, k = _hgrn_gates(f_ref[rows, :], lb)
            qr = q_ref[rows, :]
            q = qr * _sigmoid(qr)
            v = i_ref[rows, :]
            b = _cumsum_rows(lf)
            s0_ref[ci] = st
            o_inter = lax.dot_general((q * jnp.exp(b)).astype(BF16), st.astype(BF16), NT,
                                      preferred_element_type=F32)
            vb = v.astype(BF16)
            blast = b[CHUNK - 1:CHUNK, :]
            khat = (k * jnp.exp(blast - b)).astype(BF16)
            st = st * jnp.exp(blast) + lax.dot_general(vb, khat, TN, preferred_element_type=F32)
            blocks = []
            for i in range(CHUNK // SUB_FWD):
                blk = slice(SUB_FWD * i, SUB_FWD * (i + 1))
                qi, ki, vi, bi = q[blk], k[blk], v[blk], b[blk]
                oi = o_inter[blk]
                if i > 0:
                    qfac, kfac = _sub_factors(b, i, SUB_FWD)
                    att = lax.dot_general((qi * qfac).astype(BF16), (k * kfac).astype(BF16), NT,
                                          preferred_element_type=F32)
                    oi = oi + jnp.dot(att.astype(BF16), vb, preferred_element_type=F32)
                for s in range(SUB_FWD):
                    qe = qi * _diag_decay(bi, s)
                    a = jnp.sum(qe * ki[s:s + 1, :], axis=1, keepdims=True)
                    oi = oi + a * vi[s:s + 1, :]
                blocks.append(oi)
            o = jnp.concatenate(blocks, axis=0)
            o_ref[rows, :] = o
            gr = g_ref[rows, :]
            out_ref[rows, :] = o * _rstd(o) * gain * (gr * _sigmoid(gr))
            return st

        st = st_ref[...]
        for ci in range(nch):
            st = chunk(ci, st)
        st_ref[...] = st

    def col(base):
        return pl.BlockSpec((tb, RNN_HD), lambda h, c: (c, base + h))

    res, xres = _call(
        body, name=name, grid=(N_RNN, ntb),
        in_specs=[col(qb), col(fb), col(ib), col(gb),
                  pl.BlockSpec((2, RNN_HD), lambda h, c: (0, h)), pl.BlockSpec((1, RNN_HD), lambda h, c: (0, 0))],
        out_specs=[pl.BlockSpec((tb, RNN_HD), lambda h, c: (c, h)), pl.BlockSpec((tb, RNN_HD), lambda h, c: (c, h)),
                   pl.BlockSpec((None, nch, RNN_HD, RNN_HD), lambda h, c: (h, c, 0, 0))],
        out_shape=[jax.ShapeDtypeStruct((t, RNN_W), F32), jax.ShapeDtypeStruct((t, RNN_W), F32),
                   jax.ShapeDtypeStruct((N_RNN, t // CHUNK, RNN_HD, RNN_HD), F32)],
        scratch_shapes=[pltpu.VMEM((RNN_HD, RNN_HD), F32)],
        args=[proj, proj, proj, proj, lb_logits, norm_gain],
        semantics=("parallel", "arbitrary"), exchanges=exchanges)
    return (*res, xres) if exchanges else res


def _hgrn_bwd(proj, lb_logits, norm_gain, o_pre, s0, dcat, *, tb, name, exchanges=()):
    t = proj.shape[0]
    ntb = t // tb
    nch = tb // CHUNK
    qb, fb, ib, gb = QR_COL // 128, FR_COL // 128, IR_COL // 128, GR_COL // 128
    sub = SUB_BWD
    nsub = CHUNK // sub

    def body(q_ref, f_ref, i_ref, g_ref, lbl_ref, gain_ref, o_ref, s0_ref, dout_ref,
             dq_ref, df_ref, di_ref, dg_ref, dlb_ref, dgain_ref,
             dst_ref, dqs_ref, dks_ref, dvs_ref):
        c = pl.program_id(1)

        @pl.when(c == 0)
        def _():
            dst_ref[...] = jnp.zeros_like(dst_ref)
            dlb_ref[...] = jnp.zeros_like(dlb_ref)
            dgain_ref[...] = jnp.zeros_like(dgain_ref)

        lb = _lower_bound(lbl_ref)
        gain = gain_ref[...]

        def chunk(ci, dst):
            rows = slice(ci * CHUNK, (ci + 1) * CHUNK)
            dqa_ref, dka_ref, dva_ref = dqs_ref.at[ci], dks_ref.at[ci], dvs_ref.at[ci]
            sg, f, lf, k = _hgrn_gates(f_ref[rows, :], lb)
            qr = q_ref[rows, :]
            sq = _sigmoid(qr)
            q = qr * sq
            v = i_ref[rows, :]
            b = _cumsum_rows(lf)

            dout = dout_ref[rows, :]
            o = o_ref[rows, :]
            gr = g_ref[rows, :]
            sgg = _sigmoid(gr)
            gate = gr * sgg
            rs = _rstd(o)
            nrm = o * rs
            dg_ref[rows, :] = (dout * nrm * gain * (sgg * (1.0 + gr * (1.0 - sgg)))).astype(BF16)
            dn = dout * gate
            dgain_ref[...] += jnp.sum(dn * nrm, axis=0, keepdims=True)
            tt = dn * gain
            do = rs * (tt - nrm * jnp.mean(tt * nrm, axis=-1, keepdims=True))

            dob = do.astype(BF16)
            vb = v.astype(BF16)
            eb = jnp.exp(b)
            blast = b[CHUNK - 1:CHUNK, :]
            ebl = jnp.exp(blast - b)
            dstb = dst.astype(BF16)
            khat = (k * ebl).astype(BF16)
            s0 = s0_ref[ci]
            dqa_ref[...] = eb * jnp.dot(dob, s0.astype(BF16), preferred_element_type=F32)
            dk_state = ebl * jnp.dot(vb, dstb, preferred_element_type=F32)
            dka_ref[...] = dk_state
            d_blast = (jnp.sum(k * dk_state, axis=0, keepdims=True)
                       + jnp.exp(blast) * jnp.sum(dst * s0, axis=0, keepdims=True))
            dva_ref[...] = lax.dot_general(khat, dstb, NT, preferred_element_type=F32)
            dst_next = dst * jnp.exp(blast) + lax.dot_general(dob, (q * eb).astype(BF16), TN,
                                                              preferred_element_type=F32)
            pm = lax.dot_general(dob, vb, NT, preferred_element_type=F32)
            for i in range(nsub):
                blk = slice(sub * i, sub * (i + 1))
                qi, ki, vi, bi, doi = q[blk], k[blk], v[blk], b[blk], do[blk]
                dqi = dqa_ref[blk, :]
                if i > 0:
                    qfac, kfac = _sub_factors(b, i, sub)
                    qt = (qi * qfac).astype(BF16)
                    kt = (k * kfac).astype(BF16)
                    att = lax.dot_general(qt, kt, NT, preferred_element_type=F32).astype(BF16)
                    pmi = pm[blk, :].astype(BF16)
                    dva_ref[...] += lax.dot_general(att, doi.astype(BF16), TN, preferred_element_type=F32)
                    dqi = dqi + qfac * jnp.dot(pmi, kt, preferred_element_type=F32)
                    dka_ref[...] += kfac * lax.dot_general(pmi, qt, TN, preferred_element_type=F32)
                for s in range(sub):
                    e = _diag_decay(bi, s)
                    ks = ki[s:s + 1, :]
                    row = slice(sub * i + s, sub * i + s + 1)
                    a = jnp.sum(qi * e * ks, axis=1, keepdims=True)
                    pe = jnp.sum(doi * vi[s:s + 1, :], axis=1, keepdims=True) * e
                    dqi = dqi + pe * ks
                    dka_ref[row, :] += jnp.sum(pe * qi, axis=0, keepdims=True)
                    dva_ref[row, :] += jnp.sum(a * doi, axis=0, keepdims=True)
                dqa_ref[blk, :] = dqi

            dq = dqa_ref[...]
            dk = dka_ref[...]
            lastrow = lax.broadcasted_iota(jnp.int32, (CHUNK, RNN_HD), 0) == CHUNK - 1
            dlf = _rev_cumsum_rows(q * dq - k * dk + jnp.where(lastrow, d_blast, 0.0))
            dff = dlf / f - dk
            df_ref[rows, :] = (dff * (1.0 - lb) * sg * (1.0 - sg)).astype(BF16)
            dlb_ref[...] += jnp.sum(dff * (1.0 - sg), axis=0, keepdims=True)
            dq_ref[rows, :] = (dq * (sq * (1.0 + qr * (1.0 - sq)))).astype(BF16)
            di_ref[rows, :] = dva_ref[...].astype(BF16)
            return dst_next

        dst = dst_ref[...]
        for ci in reversed(range(nch)):
            dst = chunk(ci, dst)
        dst_ref[...] = dst

    def col(base):
        return pl.BlockSpec((tb, RNN_HD), lambda h, c: (ntb - 1 - c, base + h))

    outc = pl.BlockSpec((tb, RNN_HD), lambda h, c: (ntb - 1 - c, h))
    hb = ATTN_W // RNN_HD
    res, xres = _call(
        body, name=name, grid=(N_RNN, ntb),
        in_specs=[col(qb), col(fb), col(ib), col(gb),
                  pl.BlockSpec((2, RNN_HD), lambda h, c: (0, h)), pl.BlockSpec((1, RNN_HD), lambda h, c: (0, 0)),
                  outc,
                  pl.BlockSpec((None, nch, RNN_HD, RNN_HD), lambda h, c: (h, ntb - 1 - c, 0, 0)),
                  pl.BlockSpec((tb, RNN_HD), lambda h, c: (ntb - 1 - c, hb + h))],
        out_specs=[outc, outc, outc, outc,
                   pl.BlockSpec((1, RNN_HD), lambda h, c: (0, h)),
                   pl.BlockSpec((None, 1, RNN_HD), lambda h, c: (h, 0, 0))],
        out_shape=[jax.ShapeDtypeStruct((t, RNN_W), BF16)] * 4
        + [jax.ShapeDtypeStruct((1, RNN_W), F32), jax.ShapeDtypeStruct((N_RNN, 1, RNN_HD), F32)],
        scratch_shapes=[pltpu.VMEM((RNN_HD, RNN_HD), F32),
                        pltpu.VMEM((nch, CHUNK, RNN_HD), F32), pltpu.VMEM((nch, CHUNK, RNN_HD), F32),
                        pltpu.VMEM((nch, CHUNK, RNN_HD), F32)],
        args=[proj, proj, proj, proj, lb_logits, norm_gain, o_pre, s0, dcat],
        semantics=("parallel", "arbitrary"), exchanges=exchanges)
    return (*res, xres) if exchanges else res


def _cast_slots(w, where, *, name):
    _, rows, cols = w.shape
    rh = rows // 2
    tr = _row_tile(rh, cols)
    nh = rh // tr

    def body(wh_ref, w_ref, o_ref):
        o_ref[...] = w_ref[...].astype(BF16)

    return pl.pallas_call(
        body, name=name,
        grid_spec=pltpu.PrefetchScalarGridSpec(
            num_scalar_prefetch=1, grid=(2, nh),
            in_specs=[pl.BlockSpec((None, tr, cols), lambda h, i, wh: (0, h * nh + i, 0))],
            out_specs=pl.BlockSpec((None, tr, cols), lambda h, i, wh: (2 * wh[0] + h, i, 0))),
        out_shape=jax.ShapeDtypeStruct((8, rh, cols), BF16),
        compiler_params=_params(("parallel", "parallel")),
    )(where, w)


def _all_gather_halves(bufs, *, name):
    n = len(bufs)

    def body(*refs):
        ins, outs = refs[:n], refs[n:2 * n]
        send_sems, recv_sems = refs[2 * n:]
        x, y, c = _place()
        sibling = (x, y, 1 - c)
        chips = [(1 - x, y), (x, 1 - y), (1 - x, 1 - y)]

        def copy(a, k, block, to, src=None):
            slot = outs[a].at[4 * block[0] + 2 * block[1] + block[2]]
            return pltpu.make_async_remote_copy(
                src_ref=slot if src is None else src, dst_ref=slot,
                send_sem=send_sems.at[a, k], recv_sem=recv_sems.at[a, k],
                device_id=to, device_id_type=MESH)

        first, passed = [], []
        for a in range(n):
            for j, chip in enumerate(chips):
                cp = copy(a, j, (x, y, c), (*chip, c), src=ins[a].at[4 * x + 2 * y + c])
                cp.start()
                first.append(cp)
        for a in range(n):
            for j, chip in enumerate(chips):
                copy(a, j, (*chip, c), (x, y, c)).wait_recv()
                cp = copy(a, 3 + j, (*chip, c), sibling)
                cp.start()
                passed.append(cp)
        for a in range(n):
            for j, chip in enumerate(chips):
                copy(a, 3 + j, (*chip, 1 - c), (x, y, c)).wait_recv()
        for cp in first + passed:
            cp.wait_send()

    return pl.pallas_call(
        body, name=name,
        in_specs=[ANY] * n, out_specs=[ANY] * n,
        out_shape=[jax.ShapeDtypeStruct(b.shape, b.dtype) for b in bufs],
        scratch_shapes=[pltpu.SemaphoreType.DMA((n, 6)), pltpu.SemaphoreType.DMA((n, 6))],
        input_output_aliases={a: a for a in range(n)},
    )(*bufs)


def _row_tile(rows, cols, budget=1 << 20):
    tr = rows
    while tr * cols > budget and tr % 16 == 0:
        tr //= 2
    return tr


def _pair_sum(g, sib, where, *, name):
    _, _, rh, cols = g.shape
    tr = _row_tile(rh, cols)

    def body(w_ref, g_ref, s_ref, o_ref):
        o_ref[...] = (g_ref[...] + s_ref[...]).astype(BF16)

    return pl.pallas_call(
        body, name=name,
        grid_spec=pltpu.PrefetchScalarGridSpec(
            num_scalar_prefetch=1, grid=(4, rh // tr),
            in_specs=[pl.BlockSpec((None, None, tr, cols), lambda s, i, w: (s, w[1], i, 0)),
                      pl.BlockSpec((None, tr, cols), lambda s, i, w: (s, i, 0))],
            out_specs=pl.BlockSpec((None, tr, cols), lambda s, i, w: (s, i, 0))),
        out_shape=jax.ShapeDtypeStruct((4, rh, cols), BF16),
        compiler_params=_params(("parallel", "parallel")),
    )(where, g, sib)


def _final_half(g, sib, recv, where, *, name):
    _, _, rh, cols = g.shape
    tr = _row_tile(rh, cols)

    def body(w_ref, g_ref, s_ref, r_ref, o_ref):
        acc = g_ref[...] + s_ref[...]
        for j in range(3):
            acc = acc + r_ref[j].astype(F32)
        o_ref[...] = acc

    return pl.pallas_call(
        body, name=name,
        grid_spec=pltpu.PrefetchScalarGridSpec(
            num_scalar_prefetch=1, grid=(rh // tr,),
            in_specs=[pl.BlockSpec((None, None, tr, cols), lambda i, w: (w[0], w[1], i, 0)),
                      pl.BlockSpec((None, tr, cols), lambda i, w: (w[0], i, 0)),
                      pl.BlockSpec((3, tr, cols), lambda i, w: (0, i, 0))],
            out_specs=pl.BlockSpec((tr, cols), lambda i, w: (i, 0))),
        out_shape=jax.ShapeDtypeStruct((rh, cols), F32),
        compiler_params=_params(("parallel",)),
    )(where, g, sib, recv)


def _adamw_math(w, g, m, v):
    m = ADAM_B1 * m + (1.0 - ADAM_B1) * g
    v = ADAM_B2 * v + (1.0 - ADAM_B2) * (g * g)
    m_hat = m / (1.0 - ADAM_B1 ** ADAM_STEP)
    v_hat = v / (1.0 - ADAM_B2 ** ADAM_STEP)
    delta = -ADAM_LR * (m_hat / (jnp.sqrt(v_hat) + ADAM_EPS) + ADAM_WD * w)
    return delta, m, v


def _adamw(w, mine, theirs, m, v, where, *, name):
    _, rows, cols = w.shape
    tr = _row_tile(rows // 2, cols, budget=1 << 19)
    nh = rows // 2 // tr

    def body(wh_ref, w_ref, a_ref, b_ref, m_ref, v_ref, g_ref, d_ref, nm_ref, nv_ref):
        g = jnp.where(pl.program_id(0) // nh == wh_ref[1], a_ref[...], b_ref[...])
        d, nm, nv = _adamw_math(w_ref[...], g, m_ref[...], v_ref[...])
        g_ref[...] = g
        d_ref[...] = d
        nm_ref[...] = nm
        nv_ref[...] = nv

    blk = pl.BlockSpec((None, tr, cols), lambda i, wh: (0, i, 0))
    half = pl.BlockSpec((tr, cols), lambda i, wh: (i % nh, 0))
    return pl.pallas_call(
        body, name=name,
        grid_spec=pltpu.PrefetchScalarGridSpec(
            num_scalar_prefetch=1, grid=(rows // tr,),
            in_specs=[blk, half, half, blk, blk], out_specs=[blk] * 4),
        out_shape=[jax.ShapeDtypeStruct((1, rows, cols), F32)] * 4,
        compiler_params=_params(("parallel",)),
    )(where, w, mine, theirs, m, v)


SEG_LOSS = 0
SEG_SINK = 128
SEG_AGAIN = 256
SEG_L0 = SEG_AGAIN + ATTN_W
SEG_L1 = SEG_L0 + RNN_W
SEG_RGAIN = SEG_L1 + RNN_W
SEG_G = SEG_RGAIN + 128
N_PACK = SEG_G + 4 * D_MODEL


def _pack(sinks, again, l0, l1, rgain, gains, loss=None):
    z = lambda k: jnp.zeros((1, k), F32)
    first = z(128) if loss is None else loss
    return jnp.concatenate([first, sinks, z(128 - N_Q), again, l0, l1, rgain] + list(gains), axis=1)


def _small_reduce_adamw(part, w, m, v, *, name):
    def body(p_ref, w_ref, m_ref, v_ref, g_ref, d_ref, nm_ref, nv_ref, buf_ref, send_sems, recv_sems):
        x, y, c = _place()
        me = 4 * x + 2 * y + c
        copies = []
        for k in range(1, 8):
            dx, dy, dc = (k >> 2) & 1, (k >> 1) & 1, k & 1
            to = (x ^ dx, y ^ dy, c ^ dc)
            cp = pltpu.make_async_remote_copy(
                src_ref=p_ref, dst_ref=buf_ref.at[me],
                send_sem=send_sems.at[k - 1], recv_sem=recv_sems.at[k - 1],
                device_id=to, device_id_type=MESH)
            cp.start()
            copies.append(cp)
        buf_ref[me] = p_ref[...]
        for cp in copies:
            cp.wait()
        tot = buf_ref[0]
        for j in range(1, 8):
            tot = tot + buf_ref[j]
        g_ref[...] = tot
        l0 = w_ref[:, SEG_L0:SEG_L0 + RNN_W]
        l1 = w_ref[:, SEG_L1:SEG_L1 + RNN_W]
        mx = jnp.maximum(l0, l1)
        e0 = jnp.exp(l0 - mx)
        e1 = jnp.exp(l1 - mx)
        lb = e0 / (e0 + e1)
        gl0 = tot[:, SEG_L0:SEG_L0 + RNN_W] * lb * (1.0 - lb)
        g_ref[:, SEG_L0:SEG_L0 + RNN_W] = gl0
        g_ref[:, SEG_L1:SEG_L1 + RNN_W] = -gl0
        d, nm, nv = _adamw_math(w_ref[...], g_ref[...], m_ref[...], v_ref[...])
        d_ref[...] = d
        nm_ref[...] = nm
        nv_ref[...] = nv

    vm = pl.BlockSpec(memory_space=pltpu.VMEM)
    return pl.pallas_call(
        body, name=name,
        in_specs=[vm] * 4, out_specs=[vm] * 4,
        out_shape=[jax.ShapeDtypeStruct((1, N_PACK), F32)] * 4,
        scratch_shapes=[pltpu.VMEM((8, 1, N_PACK), F32), pltpu.SemaphoreType.DMA((7,)),
                        pltpu.SemaphoreType.DMA((7,))],
    )(part, w, m, v)


def _layer_grads(xs, tgt, bufs, where, sinks, again, lb_logits, rgain,
                 g_mix_pre, g_mix_post, g_mlp_pre, g_mlp_post):
    tm = 512
    b_in, b_out, b_up, b_dn = bufs

    shard = IN_W // N_CHIPS
    w_in4 = _all_gather_halves([b_in], name="gather_w_in")[0].reshape(N_CHIPS, D_MODEL, shard)
    w_in = w_in4.transpose(1, 0, 2).reshape(D_MODEL, IN_W)
    h1 = _rms_cast(xs, g_mix_pre, tm=tm, name="h1_norm")
    proj, ((b_out, b_up),) = _mm(
        h1, w_in, tm=1024, tn=768, tk=D_MODEL, out_dtype=F32, name="in_proj",
        exchanges=[_x_gather([b_out, b_up], ici=[(0, 256), (0, 384)])])
    attn, lse, ((b_out, b_up),) = _swa_fwd(
        proj, sinks, name="swa_fwd",
        exchanges=[_x_gather([b_out, b_up], ici=[None, (384, 320)], d2d=[(0, 256), None])])
    w_out = b_out.reshape(D_MODEL, D_MODEL)
    o_pre, rnn, s0, ((b_up, b_dn),) = _hgrn_fwd(
        proj, lb_logits, rgain, tb=512, name="hgrn_fwd",
        exchanges=[_x_gather([b_up, b_dn], ici=[(704, 320), (0, 704)])])
    cat = _mix_cat(attn, rnn, again, tm=tm, name="mix_cat")
    mixed, ((b_up, b_dn),) = _mm(
        cat, w_out, tm=1024, tn=1024, tk=D_MODEL, out_dtype=F32, name="out_proj",
        exchanges=[_x_gather([b_up, b_dn], ici=[None, (704, 320)], d2d=[(0, 1024), (0, 704)])])
    w_up4 = b_up.reshape(N_CHIPS, D_MODEL, D_FF // N_CHIPS)
    x1, h2 = _post_norm_res(mixed, g_mix_post, xs, g_mlp_pre, tm=256, name="mix_post")
    u, ((b_dn,),) = _mm(h2, w_up4, tm=1024, tn=1024, tk=D_MODEL, out_dtype=BF16, relu=True, w_layout="skn",
                        name="mlp_up", exchanges=[_x_gather([b_dn], d2d=[(704, 320)])])
    w_dn = b_dn.reshape(D_FF, D_MODEL)
    yv = _mm(u, w_dn, tm=1024, tn=1024, tk=2048, out_dtype=F32, a_square=True, name="mlp_down")
    dy, dx2, loss_row, dg_mlp_post = _loss_head(yv, g_mlp_post, x1, tgt, tm=256, name="loss_head")

    def halved(g):
        return g.reshape(N_CHIPS, 2, g.shape[1] // 2, g.shape[2])
    du = _mm(dy, w_dn, tm=1024, tn=1024, tk=D_MODEL, out_dtype=BF16, mul2=u, w_layout="nk", name="mlp_down_bwd")
    g_dn = halved(_mm_tn(u, dy, tm=1024, tn=1024, tt=2048, a_square=True, name="w_down_grad")
                  .reshape(N_CHIPS, D_FF // N_CHIPS, D_MODEL))
    d_w_up, ((sib_dn,),) = _mm_tn(h2, du, tm=1024, tn=1024, tt=2048, n_split=N_CHIPS, name="w_up_grad",
                                  exchanges=[_x_pair([g_dn])])
    g_up = halved(d_w_up)
    wire_dn = _pair_sum(g_dn, sib_dn, where, name="pair_sum_w_down")
    dh2, ((recv_dn,), (sib_up,)) = _mm(du, w_up4, tm=1024, tn=1024, tk=2048, out_dtype=F32, w_layout="snk", name="mlp_up_bwd",
                                       exchanges=[_x_chip([wire_dn], rows=[(0, 800)]), _x_pair([g_up])])
    wire_up = _pair_sum(g_up, sib_up, where, name="pair_sum_w_up")
    dx1, dg_mlp_pre, ((recv_dn,),) = _rms_bwd(dh2, x1, g_mlp_pre, dx2, tm=256, out_dtype=F32, name="mlp_pre_bwd",
                                              exchanges=[_x_chip([wire_dn], rows=[(800, 224)], into=[recv_dn])])
    fin_dn = _final_half(g_dn, sib_dn, recv_dn, where, name="final_half_w_down")
    dmixed, dg_mix_post = _rms_bwd(dx1, mixed, g_mix_post, None, tm=256, out_dtype=BF16, name="mix_post_bwd")
    d_w_out, ((oth_dn,),) = _mm_tn(cat, dmixed, tm=1024, tn=1024, tt=2048, name="w_out_grad",
                                   exchanges=[_x_share([fin_dn])])
    g_out = halved(d_w_out.reshape(N_CHIPS, D_MODEL // N_CHIPS, D_MODEL))
    dcat, ((sib_out,),) = _mm(dmixed, w_out, tm=1024, tn=1024, tk=D_MODEL, out_dtype=F32, w_layout="nk", name="out_proj_bwd",
                              exchanges=[_x_pair([g_out])])
    wire_out = _pair_sum(g_out, sib_out, where, name="pair_sum_w_out")
    dattn, dg_again = _rms_bwd(dcat, attn, again, None, tm=tm, out_dtype=F32, name="attn_norm_bwd")
    dq_a, dkv, dsinks, ((recv_out,), (recv_up,)) = _swa_bwd(
        proj, sinks, dattn, lse, name="swa_bwd",
        exchanges=[_x_chip([wire_out]), _x_chip([wire_up], rows=[(0, 448)])])
    dq_r, df_r, di_r, dg_r, dlb, dgain_h, ((recv_up,),) = _hgrn_bwd(
        proj, lb_logits, rgain, o_pre, s0, dcat, tb=512, name="hgrn_bwd",
        exchanges=[_x_chip([wire_up], rows=[(448, 576)], into=[recv_up])])
    fin_up = _final_half(g_up, sib_up, recv_up, where, name="final_half_w_up")
    fin_out = _final_half(g_out, sib_out, recv_out, where, name="final_half_w_out")
    dproj = jnp.concatenate([dq_a, dkv, dq_r, df_r, di_r, dg_r], axis=1).astype(BF16)
    dproj4 = dproj.reshape(dproj.shape[0], N_CHIPS, shard).transpose(1, 0, 2)
    piece_rows = D_MODEL // 4

    def w_in_piece(pc, exchanges):
        d, xres = _mm_tn(h1, dproj4, tm=piece_rows, tn=shard, tt=2048, n_split=N_CHIPS, b_slabs=True,
                         m_blocks=(2, 2, pc), name="w_in_grad_%d" % pc, exchanges=exchanges)
        return d.reshape(N_CHIPS, 2, piece_rows, shard), xres

    g_in0, ((oth_up, oth_out),) = w_in_piece(0, [_x_share([fin_up, fin_out])])
    g_in1, ((sib_in0,),) = w_in_piece(1, [_x_pair([g_in0])])
    wire_in0 = _pair_sum(g_in0, sib_in0, where, name="pair_sum_w_in_0")
    dh1, ((recv_in0,), (sib_in1,)) = _mm(
        dproj4, w_in4, tm=1024, tn=1024, tk=shard, out_dtype=F32, w_layout="snk", a_slabs=True, name="in_proj_bwd",
        exchanges=[_x_chip([wire_in0]), _x_pair([g_in1])])
    wire_in1 = _pair_sum(g_in1, sib_in1, where, name="pair_sum_w_in_1")
    gx, dg_mix_pre, ((recv_in1,),) = _rms_bwd(dh1, xs, g_mix_pre, dx1, tm=256, out_dtype=F32, name="mix_pre_bwd",
                                              exchanges=[_x_chip([wire_in1])])
    fin_in0 = _final_half(g_in0, sib_in0, recv_in0, where, name="final_half_w_in_0")
    fin_in1 = _final_half(g_in1, sib_in1, recv_in1, where, name="final_half_w_in_1")
    oth_in0, oth_in1 = _run_exchange(_x_share([fin_in0, fin_in1]), name="share_w_in")
    fin_in = jnp.concatenate([fin_in0, fin_in1], axis=0)
    oth_in = jnp.concatenate([oth_in0, oth_in1], axis=0)

    big = [(fin_in, oth_in), (fin_out, oth_out), (fin_up, oth_up), (fin_dn, oth_dn)]
    drgain = jnp.sum(dgain_h, axis=0)
    small = _pack(jnp.sum(dsinks, axis=1)[None, :], dg_again, dlb, jnp.zeros_like(dlb), drgain,
                  [dg_mix_pre, dg_mix_post, dg_mlp_pre, dg_mlp_post], loss=loss_row)
    return gx, big, small


def kernel(x, w_in, attn_sinks, attn_out_gain, rnn_lb_logits, rnn_norm_gain, w_out, mix_pre_gain, mix_post_gain, mlp_pre_gain, mlp_post_gain, w_up, w_down, loss_target, m_w_in, m_attn_sinks, m_attn_out_gain, m_rnn_lb_logits, m_rnn_norm_gain, m_w_out, m_mix_pre_gain, m_mix_post_gain, m_mlp_pre_gain, m_mlp_post_gain, m_w_up, m_w_down, v_w_in, v_attn_sinks, v_attn_out_gain, v_rnn_lb_logits, v_rnn_norm_gain, v_w_out, v_mix_pre_gain, v_mix_post_gain, v_mlp_pre_gain, v_mlp_post_gain, v_w_up, v_w_down):
    ax, ay, ac = _place()
    where = jnp.stack([2 * ax + ay, ac]).astype(jnp.int32)
    big_w = [w_in, w_out, w_up, w_down]
    big_m = [m_w_in, m_w_out, m_w_up, m_w_down]
    big_v = [v_w_in, v_w_out, v_w_up, v_w_down]

    names = ["w_in", "w_out", "w_up", "w_down"]
    bufs = [_cast_slots(w, where, name="cast_" + nm) for w, nm in zip(big_w, names)]
    gx, big_g, small_part = _layer_grads(
        x[0], loss_target[0], bufs, where, attn_sinks, attn_out_gain, rnn_lb_logits, rnn_norm_gain,
        mix_pre_gain, mix_post_gain, mlp_pre_gain, mlp_post_gain)

    grads, deltas, new_m, new_v = [], [], [], []
    for (f, o), w, m, v, nm in zip(big_g, big_w, big_m, big_v, names):
        g, d, nm_, nv_ = _adamw(w, f, o, m, v, where, name="adamw_" + nm)
        grads.append(g)
        deltas.append(d)
        new_m.append(nm_)
        new_v.append(nv_)

    def pack_params(sinks, again, logits, rgain, gains):
        return _pack(sinks, again, logits[0:1], logits[1:2], rgain, gains)

    pw = pack_params(attn_sinks, attn_out_gain, rnn_lb_logits, rnn_norm_gain,
                     [mix_pre_gain, mix_post_gain, mlp_pre_gain, mlp_post_gain])
    pm = pack_params(m_attn_sinks, m_attn_out_gain, m_rnn_lb_logits, m_rnn_norm_gain,
                     [m_mix_pre_gain, m_mix_post_gain, m_mlp_pre_gain, m_mlp_post_gain])
    pv = pack_params(v_attn_sinks, v_attn_out_gain, v_rnn_lb_logits, v_rnn_norm_gain,
                     [v_mix_pre_gain, v_mix_post_gain, v_mlp_pre_gain, v_mlp_post_gain])
    packs = _small_reduce_adamw(small_part, pw, pm, pv, name="small_reduce_adamw")

    def unpack(p):
        seg = lambda o, k: p[:, o:o + k]
        logits = jnp.concatenate([seg(SEG_L0, RNN_W), seg(SEG_L1, RNN_W)], axis=0)
        gains = [seg(SEG_G + i * D_MODEL, D_MODEL) for i in range(4)]
        return dict(sinks=seg(SEG_SINK, N_Q), again=seg(SEG_AGAIN, ATTN_W), logits=logits,
                    rgain=seg(SEG_RGAIN, RNN_HD), gains=gains)

    def order(small, big):
        return [big[0], small["sinks"], small["again"], small["logits"], small["rgain"], big[1],
                *small["gains"], big[2], big[3]]

    loss = packs[0][0, 0]
    outs = [loss, gx[None]]
    for p, b in zip(packs, [grads, deltas, new_m, new_v]):
        outs += order(unpack(p), b)
    return tuple(outs)
```

```python
import functools

import jax
import jax.numpy as jnp
from jax import lax
from jax.experimental import pallas as pl
from jax.experimental.pallas import tpu as pltpu

F32 = jnp.float32
BF16 = jnp.bfloat16
MESH = pl.DeviceIdType.MESH

EPS = 1e-6
D_MODEL = 2048
ATTN_W = 1024
HEAD_DIM = 64
N_Q = 16
N_KV = 2
GROUP = 8
BLK = 128
RNN_W = 1024
RNN_HD = 128
N_RNN = 8
CHUNK = 64
SUB_FWD = 16
SUB_BWD = 8
D_FF = 8192
IN_W = 5376
N_CHIPS = 4
KV_COL = ATTN_W
QR_COL = ATTN_W + 2 * 128
FR_COL = QR_COL + RNN_W
IR_COL = FR_COL + RNN_W
GR_COL = IR_COL + RNN_W

ADAM_LR = 0.001
ADAM_B1 = 0.9
ADAM_B2 = 0.999
ADAM_EPS = 1e-08
ADAM_WD = 0.01
ADAM_STEP = 10

VMEM_LIMIT = 48 * 1024 * 1024

NT = (((1,), (1,)), ((), ()))
TN = (((0,), (0,)), ((), ()))


def _params(sem=None):
    return pltpu.CompilerParams(dimension_semantics=sem, vmem_limit_bytes=VMEM_LIMIT)


def _sigmoid(x):
    return 1.0 / (1.0 + jnp.exp(-x))


ANY = pl.BlockSpec(memory_space=pl.ANY)


def _place():
    return lax.axis_index("x"), lax.axis_index("y"), lax.axis_index("c")


def _other_chips(x, y):
    return [(1 - x, y), (x, 1 - y), (1 - x, 1 - y)]


class _Exchange:
    def __init__(self, srcs, outs, ncopy, build, aliases=None):
        self.srcs, self.outs, self.ncopy, self.build = list(srcs), list(outs), ncopy, build
        self.aliases = aliases or {}


def _remote(src, dst, send_sems, recv_sems, k, to):
    return pltpu.make_async_remote_copy(src_ref=src, dst_ref=dst, send_sem=send_sems.at[k],
                                        recv_sem=recv_sems.at[k], device_id=to, device_id_type=MESH)


def _call(body, *, name, grid, in_specs, out_specs, out_shape, args, scratch_shapes=(), semantics=None,
          exchanges=()):
    in_specs, out_specs, out_shape = list(in_specs), list(out_specs), list(out_shape)
    scratch_shapes = list(scratch_shapes)
    ni, no, ns = len(in_specs), len(out_specs), len(scratch_shapes)
    xsrc = [s for x in exchanges for s in x.srcs]
    xout = [o for x in exchanges for o in x.outs]
    nxi, nxo = len(xsrc), len(xout)
    aliases = {}
    a0 = b0 = 0
    for x in exchanges:
        for si, oi in x.aliases.items():
            aliases[ni + a0 + si] = no + b0 + oi
        a0 += len(x.srcs)
        b0 += len(x.outs)
    sems = []
    for x in exchanges:
        sems += [pltpu.SemaphoreType.DMA((x.ncopy,)), pltpu.SemaphoreType.DMA((x.ncopy,))]

    def wrapped(*refs):
        ins, xi = refs[:ni], refs[ni:ni + nxi]
        outs, xo = refs[ni + nxi:ni + nxi + no], refs[ni + nxi + no:ni + nxi + no + nxo]
        rest = refs[ni + nxi + no + nxo:]
        scr, sm = rest[:ns], rest[ns:]

        def copies():
            cps = []
            a = b = 0
            for k, x in enumerate(exchanges):
                cps += x.build(xi[a:a + len(x.srcs)], xo[b:b + len(x.outs)], sm[2 * k], sm[2 * k + 1])
                a += len(x.srcs)
                b += len(x.outs)
            return cps

        def start():
            for cp in copies():
                cp.start()

        def wait():
            for cp in copies():
                cp.wait()

        if not exchanges:
            body(*ins, *outs, *scr)
        elif not grid:
            start()
            body(*ins, *outs, *scr)
            wait()
        else:
            first = last = None
            for ax, g in enumerate(grid):
                f = pl.program_id(ax) == 0
                l = pl.program_id(ax) == g - 1
                first = f if first is None else first & f
                last = l if last is None else last & l
            pl.when(first)(start)
            body(*ins, *outs, *scr)
            pl.when(last)(wait)

    if exchanges and semantics is not None:
        semantics = ("arbitrary",) * len(grid)
    kwargs = dict(grid=grid) if grid else {}
    res = pl.pallas_call(
        wrapped, name=name,
        in_specs=in_specs + [ANY] * nxi, out_specs=out_specs + [ANY] * nxo,
        out_shape=out_shape + xout, scratch_shapes=scratch_shapes + sems,
        input_output_aliases=aliases,
        compiler_params=_params(semantics), **kwargs,
    )(*args, *xsrc)
    res = list(res)
    mine, theirs = res[:no], res[no:]
    per = []
    b = 0
    for x in exchanges:
        per.append(theirs[b:b + len(x.outs)])
        b += len(x.outs)
    return mine, per


def _run_exchange(x, *, name):
    return _call(lambda: None, name=name, grid=(), in_specs=[], out_specs=[], out_shape=[], args=[],
                 exchanges=[x])[1][0]


def _x_gather(bufs, ici=None, d2d=None):
    n = len(bufs)
    plan = [(a, kind, rows[a]) for a in range(n) for kind, rows in (("ici", ici), ("d2d", d2d))
            if rows is not None and rows[a] is not None]

    def build(srcs, outs, ss, rs):
        x, y, c = _place()
        cps = []
        for q, (a, kind, rows) in enumerate(plan):
            piece = pl.ds(*rows)
            for j, (px, py) in enumerate(_other_chips(x, y)):
                slot, to = (4 * x + 2 * y + c, (px, py, c)) if kind == "ici" else (4 * px + 2 * py + c, (x, y, 1 - c))
                cps.append(_remote(srcs[a].at[slot, piece], outs[a].at[slot, piece], ss, rs, 3 * q + j, to))
        return cps

    outs = [jax.ShapeDtypeStruct(b.shape, b.dtype) for b in bufs]
    return _Exchange(bufs, outs, 3 * len(plan), build, aliases={a: a for a in range(n)})


def _x_pair(grads, halves_last=False):
    n = len(grads)

    def build(srcs, outs, ss, rs):
        x, y, c = _place()

        def half(r):
            if not halves_last:
                return r.at[:, 1 - c]
            ch = r.shape[2] // 2
            return r.at[:, :, pl.ds(pl.multiple_of((1 - c) * ch, 128), ch)]

        return [_remote(half(srcs[a]), outs[a], ss, rs, a, (x, y, 1 - c)) for a in range(n)]

    if halves_last:
        outs = [jax.ShapeDtypeStruct(g.shape[:2] + (g.shape[2] // 2,), g.dtype) for g in grads]
    else:
        outs = [jax.ShapeDtypeStruct((4,) + g.shape[2:], g.dtype) for g in grads]
    return _Exchange(grads, outs, n, build)


def _x_chip(wires, rows=None, into=None):
    n = len(wires)
    rows = rows or [(0, w.shape[1]) for w in wires]

    def build(srcs, outs, ss, rs):
        x, y, c = _place()
        cps = []
        for a in range(n):
            piece = pl.ds(*rows[a])
            for j, (px, py) in enumerate(_other_chips(x, y)):
                cps.append(_remote(srcs[a].at[2 * px + py, piece], outs[a].at[j, piece], ss, rs,
                                   3 * a + j, (px, py, c)))
        return cps

    outs = [jax.ShapeDtypeStruct((3,) + w.shape[1:], w.dtype) for w in wires]
    if into is None:
        return _Exchange(wires, outs, 3 * n, build)
    return _Exchange(list(wires) + list(into), outs, 3 * n, build, aliases={n + a: a for a in range(n)})


def _x_share(halves):
    n = len(halves)

    def build(srcs, outs, ss, rs):
        x, y, c = _place()
        return [_remote(srcs[a], outs[a], ss, rs, a, (x, y, 1 - c)) for a in range(n)]

    outs = [jax.ShapeDtypeStruct(h.shape, h.dtype) for h in halves]
    return _Exchange(halves, outs, n, build)


def _mm(a, w, *, tm, tn, tk, out_dtype, name, a_square=False, relu=False, mul2=None, w_layout="kn",
        exchanges=()):
    m, k = a.shape
    a_spec = pl.BlockSpec((tm, tk), lambda i, j, kk: (i, kk))
    if w_layout == "kn":
        n = w.shape[1]
        w_spec = pl.BlockSpec((tk, tn), lambda i, j, kk: (kk, j))
    elif w_layout == "nk":
        n = w.shape[0]
        w_spec = pl.BlockSpec((tn, tk), lambda i, j, kk: (j, kk))
    elif w_layout == "skn":
        n = w.shape[0] * w.shape[2]
        per_n = w.shape[2] // tn
        w_spec = pl.BlockSpec((None, tk, tn), lambda i, j, kk: (j // per_n, kk, j % per_n))
    else:
        assert w_layout == "snk"
        n = w.shape[1]
        per_k = w.shape[2] // tk
        w_spec = pl.BlockSpec((None, tn, tk), lambda i, j, kk: (kk // per_k, j, kk % per_k))
    w_dims = NT if w_layout in ("nk", "snk") else (((1,), (0,)), ((), ()))
    nk = k // tk
    assert m % tm == 0 and n % tn == 0 and k % tk == 0

    def body(*refs):
        if mul2 is not None:
            a_ref, w_ref, e_ref, o_ref, acc_ref = refs
        else:
            a_ref, w_ref, o_ref, acc_ref = refs
            e_ref = None
        kk = pl.program_id(2)
        av = a_ref[...]
        if a_square:
            af = av.astype(F32)
            av = (af * af).astype(BF16)
        part = lax.dot_general(av, w_ref[...], w_dims, preferred_element_type=F32)

        def finish(r):
            if relu:
                r = jnp.maximum(r, 0.0)
            if e_ref is not None:
                r = 2.0 * e_ref[...].astype(F32) * r
            o_ref[...] = r.astype(out_dtype)

        if nk == 1:
            finish(part)
        else:
            @pl.when(kk == 0)
            def _():
                acc_ref[...] = part

            @pl.when(kk > 0)
            def _():
                acc_ref[...] += part

            @pl.when(kk == nk - 1)
            def _():
                finish(acc_ref[...])

    in_specs = [a_spec, w_spec]
    args = [a, w]
    if mul2 is not None:
        in_specs.append(pl.BlockSpec((tm, tn), lambda i, j, kk: (i, j)))
        args.append(mul2)
    acc_shape = (tm, tn) if nk > 1 else (8, 128)
    (out,), per = _call(
        body, name=name, grid=(m // tm, n // tn, nk),
        in_specs=in_specs, out_specs=[pl.BlockSpec((tm, tn), lambda i, j, kk: (i, j))],
        out_shape=[jax.ShapeDtypeStruct((m, n), out_dtype)], args=args,
        scratch_shapes=[pltpu.VMEM(acc_shape, F32)],
        semantics=("parallel", "parallel", "arbitrary"), exchanges=exchanges)
    return (out, per) if exchanges else out


def _mm_tn(a, b, *, tm, tn, tt, name, a_square=False, n_split=1, n_blocks=None, exchanges=()):
    t, m = a.shape
    n = b.shape[1]
    assert t % tt == 0 and m % tm == 0 and n % tn == 0
    count, stride, first = n_blocks or (n // tn, 1, 0)
    n = count * tn
    assert (n // n_split) % tn == 0
    per = n // n_split // tn

    def body(a_ref, b_ref, o_ref):
        ti = pl.program_id(2)
        av = a_ref[...]
        if a_square:
            af = av.astype(F32)
            av = (af * af).astype(BF16)
        part = lax.dot_general(av, b_ref[...], TN, preferred_element_type=F32)

        @pl.when(ti == 0)
        def _():
            o_ref[...] = part

        @pl.when(ti > 0)
        def _():
            o_ref[...] += part

    (out,), xres = _call(
        body, name=name, grid=(m // tm, n // tn, t // tt),
        in_specs=[pl.BlockSpec((tt, tm), lambda i, j, ti: (ti, i)),
                  pl.BlockSpec((tt, tn), lambda i, j, ti: (ti, first + stride * j))],
        out_specs=[pl.BlockSpec((None, tm, tn), lambda i, j, ti: (j // per, i, j % per))],
        out_shape=[jax.ShapeDtypeStruct((n_split, m, n // n_split), F32)], args=[a, b],
        semantics=("parallel", "parallel", "arbitrary"), exchanges=exchanges)
    return (out, xres) if exchanges else out


def _rstd(x):
    return lax.rsqrt(jnp.mean(x * x, axis=-1, keepdims=True) + EPS)


def _rms_cast(x, g, *, tm, name):
    t, d = x.shape

    def body(x_ref, g_ref, o_ref):
        xv = x_ref[...]
        o_ref[...] = (xv * _rstd(xv) * g_ref[...]).astype(BF16)

    return pl.pallas_call(
        body, name=name, grid=(t // tm,),
        in_specs=[pl.BlockSpec((tm, d), lambda i: (i, 0)), pl.BlockSpec((1, d), lambda i: (0, 0))],
        out_specs=pl.BlockSpec((tm, d), lambda i: (i, 0)),
        out_shape=jax.ShapeDtypeStruct((t, d), BF16),
        compiler_params=_params(("parallel",)),
    )(x, g)


def _mix_cat(attn, rnn, gain, *, tm, name):
    t = attn.shape[0]

    def body(a_ref, r_ref, g_ref, o_ref):
        av = a_ref[...]
        o_ref[:, :ATTN_W] = (av * _rstd(av) * g_ref[...]).astype(BF16)
        o_ref[:, ATTN_W:] = r_ref[...].astype(BF16)

    return pl.pallas_call(
        body, name=name, grid=(t // tm,),
        in_specs=[pl.BlockSpec((tm, ATTN_W), lambda i: (i, 0)), pl.BlockSpec((tm, RNN_W), lambda i: (i, 0)),
                  pl.BlockSpec((1, ATTN_W), lambda i: (0, 0))],
        out_specs=pl.BlockSpec((tm, D_MODEL), lambda i: (i, 0)),
        out_shape=jax.ShapeDtypeStruct((t, D_MODEL), BF16),
        compiler_params=_params(("parallel",)),
    )(attn, rnn, gain)


def _post_norm_res(mixed, g_post, res, g_next, *, tm, name):
    t, d = mixed.shape

    def body(m_ref, gp_ref, r_ref, gn_ref, x1_ref, h2_ref):
        mv = m_ref[...]
        x1 = r_ref[...] + mv * _rstd(mv) * gp_ref[...]
        x1_ref[...] = x1
        h2_ref[...] = (x1 * _rstd(x1) * gn_ref[...]).astype(BF16)

    row = pl.BlockSpec((tm, d), lambda i: (i, 0))
    vec = pl.BlockSpec((1, d), lambda i: (0, 0))
    return pl.pallas_call(
        body, name=name, grid=(t // tm,),
        in_specs=[row, vec, row, vec], out_specs=[row, row],
        out_shape=[jax.ShapeDtypeStruct((t, d), F32), jax.ShapeDtypeStruct((t, d), BF16)],
        compiler_params=_params(("parallel",)),
    )(mixed, g_post, res, g_next)


def _rms_bwd(dyn, xin, g, res, *, tm, out_dtype, name, col_block=0, exchanges=()):
    t, d = xin.shape

    def body(*refs):
        if res is not None:
            dy_ref, x_ref, g_ref, r_ref, dx_ref, dg_ref = refs
        else:
            dy_ref, x_ref, g_ref, dx_ref, dg_ref = refs
        i = pl.program_id(0)
        xv = x_ref[...]
        dy = dy_ref[...].astype(F32)
        r = _rstd(xv)
        xh = xv * r
        part = jnp.sum(dy * xh, axis=0, keepdims=True)

        @pl.when(i == 0)
        def _():
            dg_ref[...] = part

        @pl.when(i > 0)
        def _():
            dg_ref[...] += part

        tt = dy * g_ref[...]
        dx = r * (tt - xh * jnp.mean(tt * xh, axis=-1, keepdims=True))
        if res is not None:
            dx = dx + r_ref[...]
        dx_ref[...] = dx.astype(out_dtype)

    row = pl.BlockSpec((tm, d), lambda i: (i, 0))
    vec = pl.BlockSpec((1, d), lambda i: (0, 0))
    in_specs = [pl.BlockSpec((tm, d), lambda i: (i, col_block)), row, vec]
    args = [dyn, xin, g]
    if res is not None:
        in_specs.append(row)
        args.append(res)
    res, xres = _call(
        body, name=name, grid=(t // tm,),
        in_specs=in_specs, out_specs=[row, vec],
        out_shape=[jax.ShapeDtypeStruct((t, d), out_dtype), jax.ShapeDtypeStruct((1, d), F32)], args=args,
        semantics=("arbitrary",), exchanges=exchanges)
    return (*res, xres) if exchanges else res


def _loss_head(y, g_post, x1, target, *, tm, name):
    t, d = y.shape

    def body(y_ref, g_ref, x1_ref, t_ref, dy_ref, dx2_ref, loss_ref, dg_ref):
        i = pl.program_id(0)
        yv = y_ref[...]
        r = _rstd(yv)
        yh = yv * r
        gv = g_ref[...]
        err = x1_ref[...] + yh * gv - t_ref[...]
        lpart = 0.5 * jnp.sum(jnp.mean(err * err, axis=-1, keepdims=True), axis=0, keepdims=True)
        dx2 = err * (1.0 / d)
        dgp = jnp.sum(dx2 * yh, axis=0, keepdims=True)
        lane = lax.broadcasted_iota(jnp.int32, (1, 128), 1)
        lrow = jnp.where(lane == 0, lpart, 0.0)

        @pl.when(i == 0)
        def _():
            dg_ref[...] = dgp
            loss_ref[...] = lrow

        @pl.when(i > 0)
        def _():
            dg_ref[...] += dgp
            loss_ref[...] += lrow

        tt = dx2 * gv
        dy_ref[...] = (r * (tt - yh * jnp.mean(tt * yh, axis=-1, keepdims=True))).astype(BF16)
        dx2_ref[...] = dx2

    row = pl.BlockSpec((tm, d), lambda i: (i, 0))
    vec = pl.BlockSpec((1, d), lambda i: (0, 0))
    return pl.pallas_call(
        body, name=name, grid=(t // tm,),
        in_specs=[row, vec, row, row],
        out_specs=[row, row, pl.BlockSpec((1, 128), lambda i: (0, 0)), vec],
        out_shape=[jax.ShapeDtypeStruct((t, d), BF16), jax.ShapeDtypeStruct((t, d), F32),
                   jax.ShapeDtypeStruct((1, 128), F32), jax.ShapeDtypeStruct((1, d), F32)],
        compiler_params=_params(("arbitrary",)),
    )(y, g_post, x1, target)


def _alibi_slope(h):
    return 2.0 ** (-8.0 * (h + 1) / N_Q)


PAIR = 2 * HEAD_DIM
N_PAIRS = N_Q // 2
PAIRS_PER_KV = GROUP // 2
SMEM = pl.BlockSpec(memory_space=pltpu.SMEM)


def _swa_mask(n):
    key = lax.broadcasted_iota(jnp.int32, (2 * BLK, BLK), 0)
    qry = lax.broadcasted_iota(jnp.int32, (2 * BLK, BLK), 1)
    dist = qry + BLK - key
    valid = (dist >= 0) & (dist < BLK) & ((key >= BLK) | (n > 0))
    return valid, dist.astype(F32)


def _block_diag(kvp_ref, kvc_ref, off):
    a = jnp.concatenate([kvp_ref[:, off:off + HEAD_DIM], kvc_ref[:, off:off + HEAD_DIM]], axis=0).astype(BF16)
    z = jnp.zeros_like(a)
    return jnp.concatenate([jnp.concatenate([a, z], axis=1), jnp.concatenate([z, a], axis=1)], axis=0)


def _swa_scores(s2, e, hh, valid, distf):
    s = s2[2 * BLK * e:2 * BLK * (e + 1)] * (HEAD_DIM ** -0.5) - _alibi_slope(hh) * distf
    return jnp.where(valid, s, -1e30)


def _swa_fwd(proj, sinks, *, name, exchanges=()):
    t = proj.shape[0]
    nb = t // BLK
    kvb = KV_COL // (2 * 128)

    def body(sink_ref, q_ref, kvc_ref, kvp_ref, o_ref, lse_ref):
        n = pl.program_id(0)
        valid, distf = _swa_mask(n)
        for kvh in range(N_KV):
            k2 = _block_diag(kvp_ref, kvc_ref, kvh * HEAD_DIM)
            v2 = _block_diag(kvp_ref, kvc_ref, 128 + kvh * HEAD_DIM)
            for jp in range(PAIRS_PER_KV):
                pair = kvh * PAIRS_PER_KV + jp
                lanes = slice(pair * PAIR, (pair + 1) * PAIR)
                s2 = lax.dot_general(k2, q_ref[:, lanes].astype(BF16), NT, preferred_element_type=F32)
                probs = []
                for e in range(2):
                    hh = 2 * pair + e
                    s = _swa_scores(s2, e, hh, valid, distf)
                    sink = sink_ref[0, hh]
                    mx = jnp.maximum(jnp.max(s, axis=0, keepdims=True), sink)
                    p = jnp.exp(s - mx)
                    l = jnp.sum(p, axis=0, keepdims=True) + jnp.exp(sink - mx)
                    probs.append((p * (1.0 / l)).astype(BF16))
                    lse_ref[hh:hh + 1, :] = mx + jnp.log(l)
                o_ref[:, lanes] = lax.dot_general(jnp.concatenate(probs, axis=0), v2, TN,
                                                  preferred_element_type=F32)

    res, xres = _call(
        body, name=name, grid=(nb,),
        in_specs=[SMEM,
                  pl.BlockSpec((BLK, ATTN_W), lambda n: (n, 0)),
                  pl.BlockSpec((BLK, 256), lambda n: (n, kvb)),
                  pl.BlockSpec((BLK, 256), lambda n: (jnp.maximum(n - 1, 0), kvb))],
        out_specs=[pl.BlockSpec((BLK, ATTN_W), lambda n: (n, 0)),
                   pl.BlockSpec((None, N_Q, BLK), lambda n: (n, 0, 0))],
        out_shape=[jax.ShapeDtypeStruct((t, ATTN_W), F32), jax.ShapeDtypeStruct((nb, N_Q, BLK), F32)],
        args=[sinks, proj, proj, proj], semantics=("parallel",), exchanges=exchanges)
    return (*res, xres) if exchanges else res


def _swa_bwd(proj, sinks, dattn, lse, *, name, exchanges=()):
    t = proj.shape[0]
    nb = t // BLK
    kvb = KV_COL // (2 * 128)

    def body(sink_ref, q_ref, kvc_ref, kvp_ref, do_ref, lse_ref, dq_ref, dkv_ref, dsink_ref, carry_ref):
        n = pl.program_id(0)

        @pl.when(n == 0)
        def _():
            dsink_ref[...] = jnp.zeros_like(dsink_ref)
            carry_ref[...] = jnp.zeros_like(carry_ref)

        @pl.when(n < nb)
        def _():
            valid, distf = _swa_mask(n)
            for kvh in range(N_KV):
                k2 = _block_diag(kvp_ref, kvc_ref, kvh * HEAD_DIM)
                v2 = _block_diag(kvp_ref, kvc_ref, 128 + kvh * HEAD_DIM)
                dk2 = jnp.zeros((4 * BLK, PAIR), F32)
                dv2 = jnp.zeros((4 * BLK, PAIR), F32)
                for jp in range(PAIRS_PER_KV):
                    pair = kvh * PAIRS_PER_KV + jp
                    lanes = slice(pair * PAIR, (pair + 1) * PAIR)
                    q2 = q_ref[:, lanes].astype(BF16)
                    do2 = do_ref[:, lanes].astype(BF16)
                    s2 = lax.dot_general(k2, q2, NT, preferred_element_type=F32)
                    dp2 = lax.dot_general(v2, do2, NT, preferred_element_type=F32)
                    probs, dss = [], []
                    for e in range(2):
                        hh = 2 * pair + e
                        lse_h = lse_ref[hh:hh + 1, :]
                        p = jnp.exp(_swa_scores(s2, e, hh, valid, distf) - lse_h)
                        dp = dp2[2 * BLK * e:2 * BLK * (e + 1)]
                        delta = jnp.sum(p * dp, axis=0, keepdims=True)
                        dsink_ref[hh:hh + 1, :] += -jnp.exp(sink_ref[0, hh] - lse_h) * delta
                        probs.append(p.astype(BF16))
                        dss.append((p * (dp - delta)).astype(BF16))
                    ds2 = jnp.concatenate(dss, axis=0)
                    dq_ref[:, lanes] = (lax.dot_general(ds2, k2, TN, preferred_element_type=F32)
                                        * (HEAD_DIM ** -0.5)).astype(BF16)
                    dk2 = dk2 + jnp.dot(ds2, q2, preferred_element_type=F32)
                    dv2 = dv2 + jnp.dot(jnp.concatenate(probs, axis=0), do2, preferred_element_type=F32)
                dk_cat = (dk2[:2 * BLK, :HEAD_DIM] + dk2[2 * BLK:, HEAD_DIM:]) * (HEAD_DIM ** -0.5)
                dv_cat = dv2[:2 * BLK, :HEAD_DIM] + dv2[2 * BLK:, HEAD_DIM:]
                ko = kvh * HEAD_DIM
                vo = 128 + kvh * HEAD_DIM
                dkv_ref[:, ko:ko + HEAD_DIM] = (carry_ref[:, ko:ko + HEAD_DIM] + dk_cat[:BLK]).astype(BF16)
                dkv_ref[:, vo:vo + HEAD_DIM] = (carry_ref[:, vo:vo + HEAD_DIM] + dv_cat[:BLK]).astype(BF16)
                carry_ref[:, ko:ko + HEAD_DIM] = dk_cat[BLK:]
                carry_ref[:, vo:vo + HEAD_DIM] = dv_cat[BLK:]

        @pl.when(n == nb)
        def _():
            dkv_ref[...] = carry_ref[...].astype(BF16)

    last = nb - 1
    res, xres = _call(
        body, name=name, grid=(nb + 1,),
        in_specs=[SMEM,
                  pl.BlockSpec((BLK, ATTN_W), lambda n: (jnp.minimum(n, last), 0)),
                  pl.BlockSpec((BLK, 256), lambda n: (jnp.minimum(n, last), kvb)),
                  pl.BlockSpec((BLK, 256), lambda n: (jnp.maximum(jnp.minimum(n, last) - 1, 0), kvb)),
                  pl.BlockSpec((BLK, ATTN_W), lambda n: (jnp.minimum(n, last), 0)),
                  pl.BlockSpec((None, N_Q, BLK), lambda n: (jnp.minimum(n, last), 0, 0))],
        out_specs=[pl.BlockSpec((BLK, ATTN_W), lambda n: (jnp.minimum(n, last), 0)),
                   pl.BlockSpec((BLK, 256), lambda n: (jnp.maximum(n - 1, 0), 0)),
                   pl.BlockSpec((N_Q, BLK), lambda n: (0, 0))],
        out_shape=[jax.ShapeDtypeStruct((t, ATTN_W), BF16), jax.ShapeDtypeStruct((t, 256), BF16),
                   jax.ShapeDtypeStruct((N_Q, BLK), F32)],
        scratch_shapes=[pltpu.VMEM((BLK, 256), F32)],
        args=[sinks, proj, proj, proj, dattn, lse], semantics=("arbitrary",), exchanges=exchanges)
    return (*res, xres) if exchanges else res


def _cumsum_rows(x):
    n = x.shape[0]
    row = lax.broadcasted_iota(jnp.int32, x.shape, 0)
    s = 1
    while s < n:
        x = x + jnp.where(row >= s, pltpu.roll(x, s, axis=0), 0.0)
        s *= 2
    return x


def _rev_cumsum_rows(x):
    n = x.shape[0]
    row = lax.broadcasted_iota(jnp.int32, x.shape, 0)
    s = 1
    while s < n:
        x = x + jnp.where(row < n - s, pltpu.roll(x, n - s, axis=0), 0.0)
        s *= 2
    return x


def _lower_bound(lbl_ref):
    l0 = lbl_ref[0:1, :]
    l1 = lbl_ref[1:2, :]
    mx = jnp.maximum(l0, l1)
    e0 = jnp.exp(l0 - mx)
    e1 = jnp.exp(l1 - mx)
    return e0 / (e0 + e1)


def _hgrn_gates(z, lb):
    sg = _sigmoid(z)
    f = lb + (1.0 - lb) * sg
    return sg, f, jnp.log(f), 1.0 - f


def _sub_factors(b, i, sub):
    rows = lax.broadcasted_iota(jnp.int32, (CHUNK, RNN_HD), 0)
    ref = b[sub * i - 1:sub * i, :]
    qfac = jnp.exp(b[sub * i:sub * (i + 1), :] - ref)
    kfac = jnp.where(rows < sub * i, jnp.exp(ref - b), 0.0)
    return qfac, kfac


def _diag_decay(bi, s):
    trow = lax.broadcasted_iota(jnp.int32, bi.shape, 0)
    return jnp.where(trow >= s, jnp.exp(bi - bi[s:s + 1, :]), 0.0)


def _hgrn_fwd(proj, lb_logits, norm_gain, *, tb, name, exchanges=()):
    t = proj.shape[0]
    ntb = t // tb
    nch = tb // CHUNK
    qb, fb, ib, gb = QR_COL // 128, FR_COL // 128, IR_COL // 128, GR_COL // 128

    def body(q_ref, f_ref, i_ref, g_ref, lbl_ref, gain_ref, o_ref, out_ref, s0_ref, st_ref):
        c = pl.program_id(1)

        @pl.when(c == 0)
        def _():
            st_ref[...] = jnp.zeros_like(st_ref)

        lb = _lower_bound(lbl_ref)
        gain = gain_ref[...]

        def chunk(ci, st):
            rows = slice(ci * CHUNK, (ci + 1) * CHUNK)
            _, _, lf, k = _hgrn_gates(f_ref[rows, :], lb)
            qr = q_ref[rows, :]
            q = qr * _sigmoid(qr)
            v = i_ref[rows, :]
            b = _cumsum_rows(lf)
            s0_ref[ci] = st
            o_inter = lax.dot_general((q * jnp.exp(b)).astype(BF16), st.astype(BF16), NT,
                                      preferred_element_type=F32)
            vb = v.astype(BF16)
            blast = b[CHUNK - 1:CHUNK, :]
            khat = (k * jnp.exp(blast - b)).astype(BF16)
            st = st * jnp.exp(blast) + lax.dot_general(vb, khat, TN, preferred_element_type=F32)
            blocks = []
            for i in range(CHUNK // SUB_FWD):
                blk = slice(SUB_FWD * i, SUB_FWD * (i + 1))
                qi, ki, vi, bi = q[blk], k[blk], v[blk], b[blk]
                oi = o_inter[blk]
                if i > 0:
                    qfac, kfac = _sub_factors(b, i, SUB_FWD)
                    att = lax.dot_general((qi * qfac).astype(BF16), (k * kfac).astype(BF16), NT,
                                          preferred_element_type=F32)
                    oi = oi + jnp.dot(att.astype(BF16), vb, preferred_element_type=F32)
                for s in range(SUB_FWD):
                    qe = qi * _diag_decay(bi, s)
                    a = jnp.sum(qe * ki[s:s + 1, :], axis=1, keepdims=True)
                    oi = oi + a * vi[s:s + 1, :]
                blocks.append(oi)
            o = jnp.concatenate(blocks, axis=0)
            o_ref[rows, :] = o
            gr = g_ref[rows, :]
            out_ref[rows, :] = o * _rstd(o) * gain * (gr * _sigmoid(gr))
            return st

        st = st_ref[...]
        for ci in range(nch):
            st = chunk(ci, st)
        st_ref[...] = st

    def col(base):
        return pl.BlockSpec((tb, RNN_HD), lambda h, c: (c, base + h))

    res, xres = _call(
        body, name=name, grid=(N_RNN, ntb),
        in_specs=[col(qb), col(fb), col(ib), col(gb),
                  pl.BlockSpec((2, RNN_HD), lambda h, c: (0, h)), pl.BlockSpec((1, RNN_HD), lambda h, c: (0, 0))],
        out_specs=[pl.BlockSpec((tb, RNN_HD), lambda h, c: (c, h)), pl.BlockSpec((tb, RNN_HD), lambda h, c: (c, h)),
                   pl.BlockSpec((None, nch, RNN_HD, RNN_HD), lambda h, c: (h, c, 0, 0))],
        out_shape=[jax.ShapeDtypeStruct((t, RNN_W), F32), jax.ShapeDtypeStruct((t, RNN_W), F32),
                   jax.ShapeDtypeStruct((N_RNN, t // CHUNK, RNN_HD, RNN_HD), F32)],
        scratch_shapes=[pltpu.VMEM((RNN_HD, RNN_HD), F32)],
        args=[proj, proj, proj, proj, lb_logits, norm_gain],
        semantics=("parallel", "arbitrary"), exchanges=exchanges)
    return (*res, xres) if exchanges else res


def _hgrn_bwd(proj, lb_logits, norm_gain, o_pre, s0, dcat, *, tb, name, exchanges=()):
    t = proj.shape[0]
    ntb = t // tb
    nch = tb // CHUNK
    qb, fb, ib, gb = QR_COL // 128, FR_COL // 128, IR_COL // 128, GR_COL // 128
    sub = SUB_BWD
    nsub = CHUNK // sub

    def body(q_ref, f_ref, i_ref, g_ref, lbl_ref, gain_ref, o_ref, s0_ref, dout_ref,
             dq_ref, df_ref, di_ref, dg_ref, dlb_ref, dgain_ref,
             dst_ref, dqs_ref, dks_ref, dvs_ref):
        c = pl.program_id(1)

        @pl.when(c == 0)
        def _():
            dst_ref[...] = jnp.zeros_like(dst_ref)
            dlb_ref[...] = jnp.zeros_like(dlb_ref)
            dgain_ref[...] = jnp.zeros_like(dgain_ref)

        lb = _lower_bound(lbl_ref)
        gain = gain_ref[...]

        def chunk(ci, dst):
            rows = slice(ci * CHUNK, (ci + 1) * CHUNK)
            dqa_ref, dka_ref, dva_ref = dqs_ref.at[ci], dks_ref.at[ci], dvs_ref.at[ci]
            sg, f, lf, k = _hgrn_gates(f_ref[rows, :], lb)
            qr = q_ref[rows, :]
            sq = _sigmoid(qr)
            q = qr * sq
            v = i_ref[rows, :]
            b = _cumsum_rows(lf)

            dout = dout_ref[rows, :]
            o = o_ref[rows, :]
            gr = g_ref[rows, :]
            sgg = _sigmoid(gr)
            gate = gr * sgg
            rs = _rstd(o)
            nrm = o * rs
            dg_ref[rows, :] = (dout * nrm * gain * (sgg * (1.0 + gr * (1.0 - sgg)))).astype(BF16)
            dn = dout * gate
            dgain_ref[...] += jnp.sum(dn * nrm, axis=0, keepdims=True)
            tt = dn * gain
            do = rs * (tt - nrm * jnp.mean(tt * nrm, axis=-1, keepdims=True))

            dob = do.astype(BF16)
            vb = v.astype(BF16)
            eb = jnp.exp(b)
            blast = b[CHUNK - 1:CHUNK, :]
            ebl = jnp.exp(blast - b)
            dstb = dst.astype(BF16)
            khat = (k * ebl).astype(BF16)
            s0 = s0_ref[ci]
            dqa_ref[...] = eb * jnp.dot(dob, s0.astype(BF16), preferred_element_type=F32)
            dk_state = ebl * jnp.dot(vb, dstb, preferred_element_type=F32)
            dka_ref[...] = dk_state
            d_blast = (jnp.sum(k * dk_state, axis=0, keepdims=True)
                       + jnp.exp(blast) * jnp.sum(dst * s0, axis=0, keepdims=True))
            dva_ref[...] = lax.dot_general(khat, dstb, NT, preferred_element_type=F32)
            dst_next = dst * jnp.exp(blast) + lax.dot_general(dob, (q * eb).astype(BF16), TN,
                                                              preferred_element_type=F32)
            pm = lax.dot_general(dob, vb, NT, preferred_element_type=F32)
            for i in range(nsub):
                blk = slice(sub * i, sub * (i + 1))
                qi, ki, vi, bi, doi = q[blk], k[blk], v[blk], b[blk], do[blk]
                dqi = dqa_ref[blk, :]
                if i > 0:
                    qfac, kfac = _sub_factors(b, i, sub)
                    qt = (qi * qfac).astype(BF16)
                    kt = (k * kfac).astype(BF16)
                    att = lax.dot_general(qt, kt, NT, preferred_element_type=F32).astype(BF16)
                    pmi = pm[blk, :].astype(BF16)
                    dva_ref[...] += lax.dot_general(att, doi.astype(BF16), TN, preferred_element_type=F32)
                    dqi = dqi + qfac * jnp.dot(pmi, kt, preferred_element_type=F32)
                    dka_ref[...] += kfac * lax.dot_general(pmi, qt, TN, preferred_element_type=F32)
                for s in range(sub):
                    e = _diag_decay(bi, s)
                    ks = ki[s:s + 1, :]
                    row = slice(sub * i + s, sub * i + s + 1)
                    a = jnp.sum(qi * e * ks, axis=1, keepdims=True)
                    pe = jnp.sum(doi * vi[s:s + 1, :], axis=1, keepdims=True) * e
                    dqi = dqi + pe * ks
                    dka_ref[row, :] += jnp.sum(pe * qi, axis=0, keepdims=True)
                    dva_ref[row, :] += jnp.sum(a * doi, axis=0, keepdims=True)
                dqa_ref[blk, :] = dqi

            dq = dqa_ref[...]
            dk = dka_ref[...]
            lastrow = lax.broadcasted_iota(jnp.int32, (CHUNK, RNN_HD), 0) == CHUNK - 1
            dlf = _rev_cumsum_rows(q * dq - k * dk + jnp.where(lastrow, d_blast, 0.0))
            dff = dlf / f - dk
            df_ref[rows, :] = (dff * (1.0 - lb) * sg * (1.0 - sg)).astype(BF16)
            dlb_ref[...] += jnp.sum(dff * (1.0 - sg), axis=0, keepdims=True)
            dq_ref[rows, :] = (dq * (sq * (1.0 + qr * (1.0 - sq)))).astype(BF16)
            di_ref[rows, :] = dva_ref[...].astype(BF16)
            return dst_next

        dst = dst_ref[...]
        for ci in reversed(range(nch)):
            dst = chunk(ci, dst)
        dst_ref[...] = dst

    def col(base):
        return pl.BlockSpec((tb, RNN_HD), lambda h, c: (ntb - 1 - c, base + h))

    outc = pl.BlockSpec((tb, RNN_HD), lambda h, c: (ntb - 1 - c, h))
    hb = ATTN_W // RNN_HD
    res, xres = _call(
        body, name=name, grid=(N_RNN, ntb),
        in_specs=[col(qb), col(fb), col(ib), col(gb),
                  pl.BlockSpec((2, RNN_HD), lambda h, c: (0, h)), pl.BlockSpec((1, RNN_HD), lambda h, c: (0, 0)),
                  outc,
                  pl.BlockSpec((None, nch, RNN_HD, RNN_HD), lambda h, c: (h, ntb - 1 - c, 0, 0)),
                  pl.BlockSpec((tb, RNN_HD), lambda h, c: (ntb - 1 - c, hb + h))],
        out_specs=[outc, outc, outc, outc,
                   pl.BlockSpec((1, RNN_HD), lambda h, c: (0, h)),
                   pl.BlockSpec((None, 1, RNN_HD), lambda h, c: (h, 0, 0))],
        out_shape=[jax.ShapeDtypeStruct((t, RNN_W), BF16)] * 4
        + [jax.ShapeDtypeStruct((1, RNN_W), F32), jax.ShapeDtypeStruct((N_RNN, 1, RNN_HD), F32)],
        scratch_shapes=[pltpu.VMEM((RNN_HD, RNN_HD), F32),
                        pltpu.VMEM((nch, CHUNK, RNN_HD), F32), pltpu.VMEM((nch, CHUNK, RNN_HD), F32),
                        pltpu.VMEM((nch, CHUNK, RNN_HD), F32)],
        args=[proj, proj, proj, proj, lb_logits, norm_gain, o_pre, s0, dcat],
        semantics=("parallel", "arbitrary"), exchanges=exchanges)
    return (*res, xres) if exchanges else res


def _cast_slots(w, where, *, name):
    _, rows, cols = w.shape
    rh = rows // 2
    tr = _row_tile(rh, cols)
    nh = rh // tr

    def body(wh_ref, w_ref, o_ref):
        o_ref[...] = w_ref[...].astype(BF16)

    return pl.pallas_call(
        body, name=name,
        grid_spec=pltpu.PrefetchScalarGridSpec(
            num_scalar_prefetch=1, grid=(2, nh),
            in_specs=[pl.BlockSpec((None, tr, cols), lambda h, i, wh: (0, h * nh + i, 0))],
            out_specs=pl.BlockSpec((None, tr, cols), lambda h, i, wh: (2 * wh[0] + h, i, 0))),
        out_shape=jax.ShapeDtypeStruct((8, rh, cols), BF16),
        compiler_params=_params(("parallel", "parallel")),
    )(where, w)


def _all_gather_halves(bufs, *, name):
    n = len(bufs)

    def body(*refs):
        ins, outs = refs[:n], refs[n:2 * n]
        send_sems, recv_sems = refs[2 * n:]
        x, y, c = _place()
        sibling = (x, y, 1 - c)
        chips = [(1 - x, y), (x, 1 - y), (1 - x, 1 - y)]

        def copy(a, k, block, to, src=None):
            slot = outs[a].at[4 * block[0] + 2 * block[1] + block[2]]
            return pltpu.make_async_remote_copy(
                src_ref=slot if src is None else src, dst_ref=slot,
                send_sem=send_sems.at[a, k], recv_sem=recv_sems.at[a, k],
                device_id=to, device_id_type=MESH)

        first, passed = [], []
        for a in range(n):
            for j, chip in enumerate(chips):
                cp = copy(a, j, (x, y, c), (*chip, c), src=ins[a].at[4 * x + 2 * y + c])
                cp.start()
                first.append(cp)
        for a in range(n):
            for j, chip in enumerate(chips):
                copy(a, j, (*chip, c), (x, y, c)).wait_recv()
                cp = copy(a, 3 + j, (*chip, c), sibling)
                cp.start()
                passed.append(cp)
        for a in range(n):
            for j, chip in enumerate(chips):
                copy(a, 3 + j, (*chip, 1 - c), (x, y, c)).wait_recv()
        for cp in first + passed:
            cp.wait_send()

    return pl.pallas_call(
        body, name=name,
        in_specs=[ANY] * n, out_specs=[ANY] * n,
        out_shape=[jax.ShapeDtypeStruct(b.shape, b.dtype) for b in bufs],
        scratch_shapes=[pltpu.SemaphoreType.DMA((n, 6)), pltpu.SemaphoreType.DMA((n, 6))],
        input_output_aliases={a: a for a in range(n)},
    )(*bufs)


def _row_tile(rows, cols, budget=1 << 20):
    tr = rows
    while tr * cols > budget and tr % 16 == 0:
        tr //= 2
    return tr


def _half_spec(g, tr, halves_last, slab):
    if halves_last:
        return pl.BlockSpec((None, tr, g.shape[2] // 2), lambda *a: (slab(*a), a[-2], a[-1][1]))
    return pl.BlockSpec((None, None, tr, g.shape[3]), lambda *a: (slab(*a), a[-1][1], a[-2], 0))


def _pair_sum(g, sib, where, *, name, halves_last=False):
    rh, cols = sib.shape[1:]
    tr = _row_tile(rh, cols)

    def body(w_ref, g_ref, s_ref, o_ref):
        o_ref[...] = (g_ref[...] + s_ref[...]).astype(BF16)

    return pl.pallas_call(
        body, name=name,
        grid_spec=pltpu.PrefetchScalarGridSpec(
            num_scalar_prefetch=1, grid=(4, rh // tr),
            in_specs=[_half_spec(g, tr, halves_last, lambda s, i, w: s),
                      pl.BlockSpec((None, tr, cols), lambda s, i, w: (s, i, 0))],
            out_specs=pl.BlockSpec((None, tr, cols), lambda s, i, w: (s, i, 0))),
        out_shape=jax.ShapeDtypeStruct((4, rh, cols), BF16),
        compiler_params=_params(("parallel", "parallel")),
    )(where, g, sib)


def _final_half(g, sib, recv, where, *, name, halves_last=False):
    rh, cols = sib.shape[1:]
    tr = _row_tile(rh, cols)

    def body(w_ref, g_ref, s_ref, r_ref, o_ref):
        acc = g_ref[...] + s_ref[...]
        for j in range(3):
            acc = acc + r_ref[j].astype(F32)
        o_ref[...] = acc

    return pl.pallas_call(
        body, name=name,
        grid_spec=pltpu.PrefetchScalarGridSpec(
            num_scalar_prefetch=1, grid=(rh // tr,),
            in_specs=[_half_spec(g, tr, halves_last, lambda i, w: w[0]),
                      pl.BlockSpec((None, tr, cols), lambda i, w: (w[0], i, 0)),
                      pl.BlockSpec((3, tr, cols), lambda i, w: (0, i, 0))],
            out_specs=pl.BlockSpec((tr, cols), lambda i, w: (i, 0))),
        out_shape=jax.ShapeDtypeStruct((rh, cols), F32),
        compiler_params=_params(("parallel",)),
    )(where, g, sib, recv)


def _adamw_math(w, g, m, v):
    m = ADAM_B1 * m + (1.0 - ADAM_B1) * g
    v = ADAM_B2 * v + (1.0 - ADAM_B2) * (g * g)
    m_hat = m / (1.0 - ADAM_B1 ** ADAM_STEP)
    v_hat = v / (1.0 - ADAM_B2 ** ADAM_STEP)
    delta = -ADAM_LR * (m_hat / (jnp.sqrt(v_hat) + ADAM_EPS) + ADAM_WD * w)
    return delta, m, v


def _adamw(w, mine, theirs, m, v, where, *, name, halves_last=False):
    _, rows, cols = w.shape
    if halves_last:
        cols //= 2
        tr = _row_tile(rows, cols, budget=1 << 19)
        grid = (rows // tr, 2)
        blk = pl.BlockSpec((None, tr, cols), lambda i, h, wh: (0, i, h))
        half = pl.BlockSpec((tr, cols), lambda i, h, wh: (i, 0))
        which = lambda: pl.program_id(1)
    else:
        tr = _row_tile(rows // 2, cols, budget=1 << 19)
        nh = rows // 2 // tr
        grid = (rows // tr,)
        blk = pl.BlockSpec((None, tr, cols), lambda i, wh: (0, i, 0))
        half = pl.BlockSpec((tr, cols), lambda i, wh: (i % nh, 0))
        which = lambda: pl.program_id(0) // nh

    def body(wh_ref, w_ref, a_ref, b_ref, m_ref, v_ref, g_ref, d_ref, nm_ref, nv_ref):
        g = jnp.where(which() == wh_ref[1], a_ref[...], b_ref[...])
        d, nm, nv = _adamw_math(w_ref[...], g, m_ref[...], v_ref[...])
        g_ref[...] = g
        d_ref[...] = d
        nm_ref[...] = nm
        nv_ref[...] = nv

    rows, cols = w.shape[1:]
    return pl.pallas_call(
        body, name=name,
        grid_spec=pltpu.PrefetchScalarGridSpec(
            num_scalar_prefetch=1, grid=grid,
            in_specs=[blk, half, half, blk, blk], out_specs=[blk] * 4),
        out_shape=[jax.ShapeDtypeStruct((1, rows, cols), F32)] * 4,
        compiler_params=_params(("parallel",) * len(grid)),
    )(where, w, mine, theirs, m, v)


SEG_LOSS = 0
SEG_SINK = 128
SEG_AGAIN = 256
SEG_L0 = SEG_AGAIN + ATTN_W
SEG_L1 = SEG_L0 + RNN_W
SEG_RGAIN = SEG_L1 + RNN_W
SEG_G = SEG_RGAIN + 128
N_PACK = SEG_G + 4 * D_MODEL


def _pack(sinks, again, l0, l1, rgain, gains, loss=None):
    z = lambda k: jnp.zeros((1, k), F32)
    first = z(128) if loss is None else loss
    return jnp.concatenate([first, sinks, z(128 - N_Q), again, l0, l1, rgain] + list(gains), axis=1)


def _small_reduce_adamw(part, w, m, v, *, name):
    def body(p_ref, w_ref, m_ref, v_ref, g_ref, d_ref, nm_ref, nv_ref, buf_ref, send_sems, recv_sems):
        x, y, c = _place()
        me = 4 * x + 2 * y + c
        copies = []
        for k in range(1, 8):
            dx, dy, dc = (k >> 2) & 1, (k >> 1) & 1, k & 1
            to = (x ^ dx, y ^ dy, c ^ dc)
            cp = pltpu.make_async_remote_copy(
                src_ref=p_ref, dst_ref=buf_ref.at[me],
                send_sem=send_sems.at[k - 1], recv_sem=recv_sems.at[k - 1],
                device_id=to, device_id_type=MESH)
            cp.start()
            copies.append(cp)
        buf_ref[me] = p_ref[...]
        for cp in copies:
            cp.wait()
        tot = buf_ref[0]
        for j in range(1, 8):
            tot = tot + buf_ref[j]
        g_ref[...] = tot
        l0 = w_ref[:, SEG_L0:SEG_L0 + RNN_W]
        l1 = w_ref[:, SEG_L1:SEG_L1 + RNN_W]
        mx = jnp.maximum(l0, l1)
        e0 = jnp.exp(l0 - mx)
        e1 = jnp.exp(l1 - mx)
        lb = e0 / (e0 + e1)
        gl0 = tot[:, SEG_L0:SEG_L0 + RNN_W] * lb * (1.0 - lb)
        g_ref[:, SEG_L0:SEG_L0 + RNN_W] = gl0
        g_ref[:, SEG_L1:SEG_L1 + RNN_W] = -gl0
        d, nm, nv = _adamw_math(w_ref[...], g_ref[...], m_ref[...], v_ref[...])
        d_ref[...] = d
        nm_ref[...] = nm
        nv_ref[...] = nv

    vm = pl.BlockSpec(memory_space=pltpu.VMEM)
    return pl.pallas_call(
        body, name=name,
        in_specs=[vm] * 4, out_specs=[vm] * 4,
        out_shape=[jax.ShapeDtypeStruct((1, N_PACK), F32)] * 4,
        scratch_shapes=[pltpu.VMEM((8, 1, N_PACK), F32), pltpu.SemaphoreType.DMA((7,)),
                        pltpu.SemaphoreType.DMA((7,))],
    )(part, w, m, v)


def _layer_grads(xs, tgt, bufs, where, sinks, again, lb_logits, rgain,
                 g_mix_pre, g_mix_post, g_mlp_pre, g_mlp_post):
    tm = 512
    b_in, b_out, b_up, b_dn = bufs

    shard = IN_W // N_CHIPS
    w_in_t = _all_gather_halves([b_in], name="gather_w_in")[0].reshape(IN_W, D_MODEL)
    h1 = _rms_cast(xs, g_mix_pre, tm=tm, name="h1_norm")
    proj, ((b_out, b_up),) = _mm(
        h1, w_in_t, tm=1024, tn=768, tk=D_MODEL, out_dtype=F32, w_layout="nk", name="in_proj",
        exchanges=[_x_gather([b_out, b_up], ici=[(0, 256), (0, 384)])])
    attn, lse, ((b_out, b_up),) = _swa_fwd(
        proj, sinks, name="swa_fwd",
        exchanges=[_x_gather([b_out, b_up], ici=[None, (384, 320)], d2d=[(0, 256), None])])
    w_out = b_out.reshape(D_MODEL, D_MODEL)
    o_pre, rnn, s0, ((b_up, b_dn),) = _hgrn_fwd(
        proj, lb_logits, rgain, tb=512, name="hgrn_fwd",
        exchanges=[_x_gather([b_up, b_dn], ici=[(704, 320), (0, 704)])])
    cat = _mix_cat(attn, rnn, again, tm=tm, name="mix_cat")
    mixed, ((b_up, b_dn),) = _mm(
        cat, w_out, tm=1024, tn=1024, tk=D_MODEL, out_dtype=F32, name="out_proj",
        exchanges=[_x_gather([b_up, b_dn], ici=[None, (704, 320)], d2d=[(0, 1024), (0, 704)])])
    w_up4 = b_up.reshape(N_CHIPS, D_MODEL, D_FF // N_CHIPS)
    x1, h2 = _post_norm_res(mixed, g_mix_post, xs, g_mlp_pre, tm=256, name="mix_post")
    u, ((b_dn,),) = _mm(h2, w_up4, tm=1024, tn=1024, tk=D_MODEL, out_dtype=BF16, relu=True, w_layout="skn",
                        name="mlp_up", exchanges=[_x_gather([b_dn], d2d=[(704, 320)])])
    w_dn = b_dn.reshape(D_FF, D_MODEL)
    yv = _mm(u, w_dn, tm=1024, tn=1024, tk=2048, out_dtype=F32, a_square=True, name="mlp_down")
    dy, dx2, loss_row, dg_mlp_post = _loss_head(yv, g_mlp_post, x1, tgt, tm=256, name="loss_head")

    def halved(g):
        return g.reshape(N_CHIPS, 2, g.shape[1] // 2, g.shape[2])
    du = _mm(dy, w_dn, tm=1024, tn=1024, tk=D_MODEL, out_dtype=BF16, mul2=u, w_layout="nk", name="mlp_down_bwd")
    g_dn = halved(_mm_tn(u, dy, tm=1024, tn=1024, tt=2048, a_square=True, name="w_down_grad")
                  .reshape(N_CHIPS, D_FF // N_CHIPS, D_MODEL))
    d_w_up, ((sib_dn,),) = _mm_tn(h2, du, tm=1024, tn=1024, tt=2048, n_split=N_CHIPS, name="w_up_grad",
                                  exchanges=[_x_pair([g_dn])])
    g_up = halved(d_w_up)
    wire_dn = _pair_sum(g_dn, sib_dn, where, name="pair_sum_w_down")
    dh2, ((recv_dn,), (sib_up,)) = _mm(du, w_up4, tm=1024, tn=1024, tk=2048, out_dtype=F32, w_layout="snk", name="mlp_up_bwd",
                                       exchanges=[_x_chip([wire_dn], rows=[(0, 800)]), _x_pair([g_up])])
    wire_up = _pair_sum(g_up, sib_up, where, name="pair_sum_w_up")
    dx1, dg_mlp_pre, ((recv_dn,),) = _rms_bwd(dh2, x1, g_mlp_pre, dx2, tm=256, out_dtype=F32, name="mlp_pre_bwd",
                                              exchanges=[_x_chip([wire_dn], rows=[(800, 224)], into=[recv_dn])])
    fin_dn = _final_half(g_dn, sib_dn, recv_dn, where, name="final_half_w_down")
    dmixed, dg_mix_post = _rms_bwd(dx1, mixed, g_mix_post, None, tm=256, out_dtype=BF16, name="mix_post_bwd")
    d_w_out, ((oth_dn,),) = _mm_tn(cat, dmixed, tm=1024, tn=1024, tt=2048, name="w_out_grad",
                                   exchanges=[_x_share([fin_dn])])
    g_out = halved(d_w_out.reshape(N_CHIPS, D_MODEL // N_CHIPS, D_MODEL))
    dcat, ((sib_out,),) = _mm(dmixed, w_out, tm=1024, tn=1024, tk=D_MODEL, out_dtype=F32, w_layout="nk", name="out_proj_bwd",
                              exchanges=[_x_pair([g_out])])
    wire_out = _pair_sum(g_out, sib_out, where, name="pair_sum_w_out")
    dattn, dg_again = _rms_bwd(dcat, attn, again, None, tm=tm, out_dtype=F32, name="attn_norm_bwd")
    dq_a, dkv, dsinks, ((recv_out,), (recv_up,)) = _swa_bwd(
        proj, sinks, dattn, lse, name="swa_bwd",
        exchanges=[_x_chip([wire_out]), _x_chip([wire_up], rows=[(0, 448)])])
    dq_r, df_r, di_r, dg_r, dlb, dgain_h, ((recv_up,),) = _hgrn_bwd(
        proj, lb_logits, rgain, o_pre, s0, dcat, tb=512, name="hgrn_bwd",
        exchanges=[_x_chip([wire_up], rows=[(448, 576)], into=[recv_up])])
    fin_up = _final_half(g_up, sib_up, recv_up, where, name="final_half_w_up")
    fin_out = _final_half(g_out, sib_out, recv_out, where, name="final_half_w_out")
    dproj = jnp.concatenate([dq_a, dkv, dq_r, df_r, di_r, dg_r], axis=1)
    piece_cols = D_MODEL // 4

    def w_in_piece(pc, exchanges):
        d, xres = _mm_tn(dproj, h1, tm=768, tn=piece_cols, tt=2048, n_blocks=(2, 2, pc),
                         name="w_in_grad_%d" % pc, exchanges=exchanges)
        return d.reshape(N_CHIPS, shard, 2 * piece_cols), xres

    g_in0, ((oth_up, oth_out),) = w_in_piece(0, [_x_share([fin_up, fin_out])])
    g_in1, ((sib_in0,),) = w_in_piece(1, [_x_pair([g_in0], halves_last=True)])
    wire_in0 = _pair_sum(g_in0, sib_in0, where, name="pair_sum_w_in_0", halves_last=True)
    dh1, ((recv_in0,), (sib_in1,)) = _mm(
        dproj, w_in_t, tm=1024, tn=1024, tk=2688, out_dtype=F32, name="in_proj_bwd",
        exchanges=[_x_chip([wire_in0]), _x_pair([g_in1], halves_last=True)])
    wire_in1 = _pair_sum(g_in1, sib_in1, where, name="pair_sum_w_in_1", halves_last=True)
    gx, dg_mix_pre, ((recv_in1,),) = _rms_bwd(dh1, xs, g_mix_pre, dx1, tm=256, out_dtype=F32, name="mix_pre_bwd",
                                              exchanges=[_x_chip([wire_in1])])
    fin_in0 = _final_half(g_in0, sib_in0, recv_in0, where, name="final_half_w_in_0", halves_last=True)
    fin_in1 = _final_half(g_in1, sib_in1, recv_in1, where, name="final_half_w_in_1", halves_last=True)
    oth_in0, oth_in1 = _run_exchange(_x_share([fin_in0, fin_in1]), name="share_w_in")
    fin_in = jnp.concatenate([fin_in0, fin_in1], axis=1)
    oth_in = jnp.concatenate([oth_in0, oth_in1], axis=1)

    big = [(fin_in, oth_in), (fin_out, oth_out), (fin_up, oth_up), (fin_dn, oth_dn)]
    drgain = jnp.sum(dgain_h, axis=0)
    small = _pack(jnp.sum(dsinks, axis=1)[None, :], dg_again, dlb, jnp.zeros_like(dlb), drgain,
                  [dg_mix_pre, dg_mix_post, dg_mlp_pre, dg_mlp_post], loss=loss_row)
    return gx, big, small


def kernel(x, w_in, attn_sinks, attn_out_gain, rnn_lb_logits, rnn_norm_gain, w_out, mix_pre_gain, mix_post_gain, mlp_pre_gain, mlp_post_gain, w_up, w_down, loss_target, m_w_in, m_attn_sinks, m_attn_out_gain, m_rnn_lb_logits, m_rnn_norm_gain, m_w_out, m_mix_pre_gain, m_mix_post_gain, m_mlp_pre_gain, m_mlp_post_gain, m_w_up, m_w_down, v_w_in, v_attn_sinks, v_attn_out_gain, v_rnn_lb_logits, v_rnn_norm_gain, v_w_out, v_mix_pre_gain, v_mix_post_gain, v_mlp_pre_gain, v_mlp_post_gain, v_w_up, v_w_down):
    ax, ay, ac = _place()
    where = jnp.stack([2 * ax + ay, ac]).astype(jnp.int32)
    t = lambda a: jnp.swapaxes(a, 1, 2)
    big_w = [t(w_in), w_out, w_up, w_down]
    big_m = [t(m_w_in), m_w_out, m_w_up, m_w_down]
    big_v = [t(v_w_in), v_w_out, v_w_up, v_w_down]

    names = ["w_in", "w_out", "w_up", "w_down"]
    bufs = [_cast_slots(w, where, name="cast_" + nm) for w, nm in zip(big_w, names)]
    gx, big_g, small_part = _layer_grads(
        x[0], loss_target[0], bufs, where, attn_sinks, attn_out_gain, rnn_lb_logits, rnn_norm_gain,
        mix_pre_gain, mix_post_gain, mlp_pre_gain, mlp_post_gain)

    grads, deltas, new_m, new_v = [], [], [], []
    for (f, o), w, m, v, nm in zip(big_g, big_w, big_m, big_v, names):
        res = _adamw(w, f, o, m, v, where, name="adamw_" + nm, halves_last=(nm == "w_in"))
        if nm == "w_in":
            res = [t(r) for r in res]
        g, d, nm_, nv_ = res
        grads.append(g)
        deltas.append(d)
        new_m.append(nm_)
        new_v.append(nv_)

    def pack_params(sinks, again, logits, rgain, gains):
        return _pack(sinks, again, logits[0:1], logits[1:2], rgain, gains)

    pw = pack_params(attn_sinks, attn_out_gain, rnn_lb_logits, rnn_norm_gain,
                     [mix_pre_gain, mix_post_gain, mlp_pre_gain, mlp_post_gain])
    pm = pack_params(m_attn_sinks, m_attn_out_gain, m_rnn_lb_logits, m_rnn_norm_gain,
                     [m_mix_pre_gain, m_mix_post_gain, m_mlp_pre_gain, m_mlp_post_gain])
    pv = pack_params(v_attn_sinks, v_attn_out_gain, v_rnn_lb_logits, v_rnn_norm_gain,
                     [v_mix_pre_gain, v_mix_post_gain, v_mlp_pre_gain, v_mlp_post_gain])
    packs = _small_reduce_adamw(small_part, pw, pm, pv, name="small_reduce_adamw")

    def unpack(p):
        seg = lambda o, k: p[:, o:o + k]
        logits = jnp.concatenate([seg(SEG_L0, RNN_W), seg(SEG_L1, RNN_W)], axis=0)
        gains = [seg(SEG_G + i * D_MODEL, D_MODEL) for i in range(4)]
        return dict(sinks=seg(SEG_SINK, N_Q), again=seg(SEG_AGAIN, ATTN_W), logits=logits,
                    rgain=seg(SEG_RGAIN, RNN_HD), gains=gains)

    def order(small, big):
        return [big[0], small["sinks"], small["again"], small["logits"], small["rgain"], big[1],
                *small["gains"], big[2], big[3]]

    loss = packs[0][0, 0]
    outs = [loss, gx[None]]
    for p, b in zip(packs, [grads, deltas, new_m, new_v]):
        outs += order(unpack(p), b)
    return tuple(outs)
```

```python
import functools

import jax
import jax.numpy as jnp
from jax import lax
from jax.experimental import pallas as pl
from jax.experimental.pallas import tpu as pltpu

F32 = jnp.float32
BF16 = jnp.bfloat16
MESH = pl.DeviceIdType.MESH

EPS = 1e-6
D_MODEL = 2048
ATTN_W = 1024
HEAD_DIM = 64
N_Q = 16
N_KV = 2
GROUP = 8
BLK = 128
RNN_W = 1024
RNN_HD = 128
N_RNN = 8
CHUNK = 64
SUB_FWD = 16
SUB_BWD = 8
D_FF = 8192
IN_W = 5376
N_CHIPS = 4
KV_COL = ATTN_W
QR_COL = ATTN_W + 2 * 128
FR_COL = QR_COL + RNN_W
IR_COL = FR_COL + RNN_W
GR_COL = IR_COL + RNN_W

ADAM_LR = 0.001
ADAM_B1 = 0.9
ADAM_B2 = 0.999
ADAM_EPS = 1e-08
ADAM_WD = 0.01
ADAM_STEP = 10

VMEM_LIMIT = 48 * 1024 * 1024

NT = (((1,), (1,)), ((), ()))
TN = (((0,), (0,)), ((), ()))


def _params(sem=None):
    return pltpu.CompilerParams(dimension_semantics=sem, vmem_limit_bytes=VMEM_LIMIT)


def _sigmoid(x):
    return 1.0 / (1.0 + jnp.exp(-x))


ANY = pl.BlockSpec(memory_space=pl.ANY)


def _place():
    return lax.axis_index("x"), lax.axis_index("y"), lax.axis_index("c")


def _other_chips(x, y):
    return [(1 - x, y), (x, 1 - y), (1 - x, 1 - y)]


class _Exchange:
    def __init__(self, srcs, outs, ncopy, build, aliases=None):
        self.srcs, self.outs, self.ncopy, self.build = list(srcs), list(outs), ncopy, build
        self.aliases = aliases or {}


def _remote(src, dst, send_sems, recv_sems, k, to):
    return pltpu.make_async_remote_copy(src_ref=src, dst_ref=dst, send_sem=send_sems.at[k],
                                        recv_sem=recv_sems.at[k], device_id=to, device_id_type=MESH)


def _call(body, *, name, grid, in_specs, out_specs, out_shape, args, scratch_shapes=(), semantics=None,
          exchanges=()):
    in_specs, out_specs, out_shape = list(in_specs), list(out_specs), list(out_shape)
    scratch_shapes = list(scratch_shapes)
    ni, no, ns = len(in_specs), len(out_specs), len(scratch_shapes)
    xsrc = [s for x in exchanges for s in x.srcs]
    xout = [o for x in exchanges for o in x.outs]
    nxi, nxo = len(xsrc), len(xout)
    aliases = {}
    a0 = b0 = 0
    for x in exchanges:
        for si, oi in x.aliases.items():
            aliases[ni + a0 + si] = no + b0 + oi
        a0 += len(x.srcs)
        b0 += len(x.outs)
    sems = []
    for x in exchanges:
        sems += [pltpu.SemaphoreType.DMA((x.ncopy,)), pltpu.SemaphoreType.DMA((x.ncopy,))]

    def wrapped(*refs):
        ins, xi = refs[:ni], refs[ni:ni + nxi]
        outs, xo = refs[ni + nxi:ni + nxi + no], refs[ni + nxi + no:ni + nxi + no + nxo]
        rest = refs[ni + nxi + no + nxo:]
        scr, sm = rest[:ns], rest[ns:]

        def copies():
            cps = []
            a = b = 0
            for k, x in enumerate(exchanges):
                cps += x.build(xi[a:a + len(x.srcs)], xo[b:b + len(x.outs)], sm[2 * k], sm[2 * k + 1])
                a += len(x.srcs)
                b += len(x.outs)
            return cps

        def start():
            for cp in copies():
                cp.start()

        def wait():
            for cp in copies():
                cp.wait()

        if not exchanges:
            body(*ins, *outs, *scr)
        elif not grid:
            start()
            body(*ins, *outs, *scr)
            wait()
        else:
            first = last = None
            for ax, g in enumerate(grid):
                f = pl.program_id(ax) == 0
                l = pl.program_id(ax) == g - 1
                first = f if first is None else first & f
                last = l if last is None else last & l
            pl.when(first)(start)
            body(*ins, *outs, *scr)
            pl.when(last)(wait)

    if exchanges and semantics is not None:
        semantics = ("arbitrary",) * len(grid)
    kwargs = dict(grid=grid) if grid else {}
    res = pl.pallas_call(
        wrapped, name=name,
        in_specs=in_specs + [ANY] * nxi, out_specs=out_specs + [ANY] * nxo,
        out_shape=out_shape + xout, scratch_shapes=scratch_shapes + sems,
        input_output_aliases=aliases,
        compiler_params=_params(semantics), **kwargs,
    )(*args, *xsrc)
    res = list(res)
    mine, theirs = res[:no], res[no:]
    per = []
    b = 0
    for x in exchanges:
        per.append(theirs[b:b + len(x.outs)])
        b += len(x.outs)
    return mine, per


def _run_exchange(x, *, name):
    return _call(lambda: None, name=name, grid=(), in_specs=[], out_specs=[], out_shape=[], args=[],
                 exchanges=[x])[1][0]


def _x_gather(bufs, ici=None, d2d=None):
    n = len(bufs)
    plan = [(a, kind, rows[a]) for a in range(n) for kind, rows in (("ici", ici), ("d2d", d2d))
            if rows is not None and rows[a] is not None]

    def build(srcs, outs, ss, rs):
        x, y, c = _place()
        cps = []
        for q, (a, kind, rows) in enumerate(plan):
            piece = pl.ds(*rows)
            for j, (px, py) in enumerate(_other_chips(x, y)):
                slot, to = (4 * x + 2 * y + c, (px, py, c)) if kind == "ici" else (4 * px + 2 * py + c, (x, y, 1 - c))
                cps.append(_remote(srcs[a].at[slot, piece], outs[a].at[slot, piece], ss, rs, 3 * q + j, to))
        return cps

    outs = [jax.ShapeDtypeStruct(b.shape, b.dtype) for b in bufs]
    return _Exchange(bufs, outs, 3 * len(plan), build, aliases={a: a for a in range(n)})


def _x_pair(grads, halves_last=False):
    n = len(grads)

    def build(srcs, outs, ss, rs):
        x, y, c = _place()

        def half(r):
            if not halves_last:
                return r.at[:, 1 - c]
            ch = r.shape[2] // 2
            return r.at[:, :, pl.ds(pl.multiple_of((1 - c) * ch, 128), ch)]

        return [_remote(half(srcs[a]), outs[a], ss, rs, a, (x, y, 1 - c)) for a in range(n)]

    if halves_last:
        outs = [jax.ShapeDtypeStruct(g.shape[:2] + (g.shape[2] // 2,), g.dtype) for g in grads]
    else:
        outs = [jax.ShapeDtypeStruct((4,) + g.shape[2:], g.dtype) for g in grads]
    return _Exchange(grads, outs, n, build)


def _x_chip(wires, rows=None, into=None):
    n = len(wires)
    rows = rows or [(0, w.shape[1]) for w in wires]

    def build(srcs, outs, ss, rs):
        x, y, c = _place()
        cps = []
        for a in range(n):
            piece = pl.ds(*rows[a])
            for j, (px, py) in enumerate(_other_chips(x, y)):
                cps.append(_remote(srcs[a].at[2 * px + py, piece], outs[a].at[j, piece], ss, rs,
                                   3 * a + j, (px, py, c)))
        return cps

    outs = [jax.ShapeDtypeStruct((3,) + w.shape[1:], w.dtype) for w in wires]
    if into is None:
        return _Exchange(wires, outs, 3 * n, build)
    return _Exchange(list(wires) + list(into), outs, 3 * n, build, aliases={n + a: a for a in range(n)})


def _x_share(halves):
    n = len(halves)

    def build(srcs, outs, ss, rs):
        x, y, c = _place()
        return [_remote(srcs[a], outs[a], ss, rs, a, (x, y, 1 - c)) for a in range(n)]

    outs = [jax.ShapeDtypeStruct(h.shape, h.dtype) for h in halves]
    return _Exchange(halves, outs, n, build)


def _mm(a, w, *, tm, tn, tk, out_dtype, name, a_square=False, relu=False, mul2=None, w_layout="kn",
        exchanges=()):
    m, k = a.shape
    a_spec = pl.BlockSpec((tm, tk), lambda i, j, kk: (i, kk))
    if w_layout == "kn":
        n = w.shape[1]
        w_spec = pl.BlockSpec((tk, tn), lambda i, j, kk: (kk, j))
    elif w_layout == "nk":
        n = w.shape[0]
        w_spec = pl.BlockSpec((tn, tk), lambda i, j, kk: (j, kk))
    elif w_layout == "skn":
        n = w.shape[0] * w.shape[2]
        per_n = w.shape[2] // tn
        w_spec = pl.BlockSpec((None, tk, tn), lambda i, j, kk: (j // per_n, kk, j % per_n))
    else:
        assert w_layout == "snk"
        n = w.shape[1]
        per_k = w.shape[2] // tk
        w_spec = pl.BlockSpec((None, tn, tk), lambda i, j, kk: (kk // per_k, j, kk % per_k))
    w_dims = NT if w_layout in ("nk", "snk") else (((1,), (0,)), ((), ()))
    nk = k // tk
    assert m % tm == 0 and n % tn == 0 and k % tk == 0

    def body(*refs):
        if mul2 is not None:
            a_ref, w_ref, e_ref, o_ref, acc_ref = refs
        else:
            a_ref, w_ref, o_ref, acc_ref = refs
            e_ref = None
        kk = pl.program_id(2)
        av = a_ref[...]
        if a_square:
            af = av.astype(F32)
            av = (af * af).astype(BF16)
        part = lax.dot_general(av, w_ref[...], w_dims, preferred_element_type=F32)

        def finish(r):
            if relu:
                r = jnp.maximum(r, 0.0)
            if e_ref is not None:
                r = 2.0 * e_ref[...].astype(F32) * r
            o_ref[...] = r.astype(out_dtype)

        if nk == 1:
            finish(part)
        else:
            @pl.when(kk == 0)
            def _():
                acc_ref[...] = part

            @pl.when(kk > 0)
            def _():
                acc_ref[...] += part

            @pl.when(kk == nk - 1)
            def _():
                finish(acc_ref[...])

    in_specs = [a_spec, w_spec]
    args = [a, w]
    if mul2 is not None:
        in_specs.append(pl.BlockSpec((tm, tn), lambda i, j, kk: (i, j)))
        args.append(mul2)
    acc_shape = (tm, tn) if nk > 1 else (8, 128)
    (out,), per = _call(
        body, name=name, grid=(m // tm, n // tn, nk),
        in_specs=in_specs, out_specs=[pl.BlockSpec((tm, tn), lambda i, j, kk: (i, j))],
        out_shape=[jax.ShapeDtypeStruct((m, n), out_dtype)], args=args,
        scratch_shapes=[pltpu.VMEM(acc_shape, F32)],
        semantics=("parallel", "parallel", "arbitrary"), exchanges=exchanges)
    return (out, per) if exchanges else out


def _mm_tn(a, b, *, tm, tn, tt, name, a_square=False, n_split=1, n_blocks=None, exchanges=()):
    t, m = a.shape
    n = b.shape[1]
    assert t % tt == 0 and m % tm == 0 and n % tn == 0
    count, stride, first = n_blocks or (n // tn, 1, 0)
    n = count * tn
    assert (n // n_split) % tn == 0
    per = n // n_split // tn

    def body(a_ref, b_ref, o_ref):
        ti = pl.program_id(2)
        av = a_ref[...]
        if a_square:
            af = av.astype(F32)
            av = (af * af).astype(BF16)
        part = lax.dot_general(av, b_ref[...], TN, preferred_element_type=F32)

        @pl.when(ti == 0)
        def _():
            o_ref[...] = part

        @pl.when(ti > 0)
        def _():
            o_ref[...] += part

    (out,), xres = _call(
        body, name=name, grid=(m // tm, n // tn, t // tt),
        in_specs=[pl.BlockSpec((tt, tm), lambda i, j, ti: (ti, i)),
                  pl.BlockSpec((tt, tn), lambda i, j, ti: (ti, first + stride * j))],
        out_specs=[pl.BlockSpec((None, tm, tn), lambda i, j, ti: (j // per, i, j % per))],
        out_shape=[jax.ShapeDtypeStruct((n_split, m, n // n_split), F32)], args=[a, b],
        semantics=("parallel", "parallel", "arbitrary"), exchanges=exchanges)
    return (out, xres) if exchanges else out


def _rstd(x):
    return lax.rsqrt(jnp.mean(x * x, axis=-1, keepdims=True) + EPS)


def _rms_cast(x, g, *, tm, name):
    t, d = x.shape

    def body(x_ref, g_ref, o_ref):
        xv = x_ref[...]
        o_ref[...] = (xv * _rstd(xv) * g_ref[...]).astype(BF16)

    return pl.pallas_call(
        body, name=name, grid=(t // tm,),
        in_specs=[pl.BlockSpec((tm, d), lambda i: (i, 0)), pl.BlockSpec((1, d), lambda i: (0, 0))],
        out_specs=pl.BlockSpec((tm, d), lambda i: (i, 0)),
        out_shape=jax.ShapeDtypeStruct((t, d), BF16),
        compiler_params=_params(("parallel",)),
    )(x, g)


def _mix_cat(attn, rnn, gain, *, tm, name):
    t = attn.shape[0]

    def body(a_ref, r_ref, g_ref, o_ref):
        av = a_ref[...]
        o_ref[:, :ATTN_W] = (av * _rstd(av) * g_ref[...]).astype(BF16)
        o_ref[:, ATTN_W:] = r_ref[...].astype(BF16)

    return pl.pallas_call(
        body, name=name, grid=(t // tm,),
        in_specs=[pl.BlockSpec((tm, ATTN_W), lambda i: (i, 0)), pl.BlockSpec((tm, RNN_W), lambda i: (i, 0)),
                  pl.BlockSpec((1, ATTN_W), lambda i: (0, 0))],
        out_specs=pl.BlockSpec((tm, D_MODEL), lambda i: (i, 0)),
        out_shape=jax.ShapeDtypeStruct((t, D_MODEL), BF16),
        compiler_params=_params(("parallel",)),
    )(attn, rnn, gain)


def _post_norm_res(mixed, g_post, res, g_next, *, tm, name, exchanges=()):
    t, d = mixed.shape

    def body(m_ref, gp_ref, r_ref, gn_ref, x1_ref, h2_ref):
        mv = m_ref[...]
        x1 = r_ref[...] + mv * _rstd(mv) * gp_ref[...]
        x1_ref[...] = x1
        h2_ref[...] = (x1 * _rstd(x1) * gn_ref[...]).astype(BF16)

    row = pl.BlockSpec((tm, d), lambda i: (i, 0))
    vec = pl.BlockSpec((1, d), lambda i: (0, 0))
    res_, xres = _call(
        body, name=name, grid=(t // tm,),
        in_specs=[row, vec, row, vec], out_specs=[row, row],
        out_shape=[jax.ShapeDtypeStruct((t, d), F32), jax.ShapeDtypeStruct((t, d), BF16)],
        args=[mixed, g_post, res, g_next], semantics=("parallel",), exchanges=exchanges)
    return (*res_, xres) if exchanges else res_


def _rms_bwd(dyn, xin, g, res, *, tm, out_dtype, name, col_block=0, exchanges=()):
    t, d = xin.shape

    def body(*refs):
        if res is not None:
            dy_ref, x_ref, g_ref, r_ref, dx_ref, dg_ref = refs
        else:
            dy_ref, x_ref, g_ref, dx_ref, dg_ref = refs
        i = pl.program_id(0)
        xv = x_ref[...]
        dy = dy_ref[...].astype(F32)
        r = _rstd(xv)
        xh = xv * r
        part = jnp.sum(dy * xh, axis=0, keepdims=True)

        @pl.when(i == 0)
        def _():
            dg_ref[...] = part

        @pl.when(i > 0)
        def _():
            dg_ref[...] += part

        tt = dy * g_ref[...]
        dx = r * (tt - xh * jnp.mean(tt * xh, axis=-1, keepdims=True))
        if res is not None:
            dx = dx + r_ref[...]
        dx_ref[...] = dx.astype(out_dtype)

    row = pl.BlockSpec((tm, d), lambda i: (i, 0))
    vec = pl.BlockSpec((1, d), lambda i: (0, 0))
    in_specs = [pl.BlockSpec((tm, d), lambda i: (i, col_block)), row, vec]
    args = [dyn, xin, g]
    if res is not None:
        in_specs.append(row)
        args.append(res)
    res, xres = _call(
        body, name=name, grid=(t // tm,),
        in_specs=in_specs, out_specs=[row, vec],
        out_shape=[jax.ShapeDtypeStruct((t, d), out_dtype), jax.ShapeDtypeStruct((1, d), F32)], args=args,
        semantics=("arbitrary",), exchanges=exchanges)
    return (*res, xres) if exchanges else res


def _loss_head(y, g_post, x1, target, *, tm, name):
    t, d = y.shape

    def body(y_ref, g_ref, x1_ref, t_ref, dy_ref, dx2_ref, loss_ref, dg_ref):
        i = pl.program_id(0)
        yv = y_ref[...]
        r = _rstd(yv)
        yh = yv * r
        gv = g_ref[...]
        err = x1_ref[...] + yh * gv - t_ref[...]
        lpart = 0.5 * jnp.sum(jnp.mean(err * err, axis=-1, keepdims=True), axis=0, keepdims=True)
        dx2 = err * (1.0 / d)
        dgp = jnp.sum(dx2 * yh, axis=0, keepdims=True)
        lane = lax.broadcasted_iota(jnp.int32, (1, 128), 1)
        lrow = jnp.where(lane == 0, lpart, 0.0)

        @pl.when(i == 0)
        def _():
            dg_ref[...] = dgp
            loss_ref[...] = lrow

        @pl.when(i > 0)
        def _():
            dg_ref[...] += dgp
            loss_ref[...] += lrow

        tt = dx2 * gv
        dy_ref[...] = (r * (tt - yh * jnp.mean(tt * yh, axis=-1, keepdims=True))).astype(BF16)
        dx2_ref[...] = dx2

    row = pl.BlockSpec((tm, d), lambda i: (i, 0))
    vec = pl.BlockSpec((1, d), lambda i: (0, 0))
    return pl.pallas_call(
        body, name=name, grid=(t // tm,),
        in_specs=[row, vec, row, row],
        out_specs=[row, row, pl.BlockSpec((1, 128), lambda i: (0, 0)), vec],
        out_shape=[jax.ShapeDtypeStruct((t, d), BF16), jax.ShapeDtypeStruct((t, d), F32),
                   jax.ShapeDtypeStruct((1, 128), F32), jax.ShapeDtypeStruct((1, d), F32)],
        compiler_params=_params(("arbitrary",)),
    )(y, g_post, x1, target)


def _alibi_slope(h):
    return 2.0 ** (-8.0 * (h + 1) / N_Q)


PAIR = 2 * HEAD_DIM
N_PAIRS = N_Q // 2
PAIRS_PER_KV = GROUP // 2
SMEM = pl.BlockSpec(memory_space=pltpu.SMEM)


def _swa_mask(n):
    key = lax.broadcasted_iota(jnp.int32, (2 * BLK, BLK), 0)
    qry = lax.broadcasted_iota(jnp.int32, (2 * BLK, BLK), 1)
    dist = qry + BLK - key
    valid = (dist >= 0) & (dist < BLK) & ((key >= BLK) | (n > 0))
    return valid, dist.astype(F32)


def _block_diag(kvp_ref, kvc_ref, off):
    a = jnp.concatenate([kvp_ref[:, off:off + HEAD_DIM], kvc_ref[:, off:off + HEAD_DIM]], axis=0).astype(BF16)
    z = jnp.zeros_like(a)
    return jnp.concatenate([jnp.concatenate([a, z], axis=1), jnp.concatenate([z, a], axis=1)], axis=0)


def _swa_scores(s2, e, hh, valid, distf):
    s = s2[2 * BLK * e:2 * BLK * (e + 1)] * (HEAD_DIM ** -0.5) - _alibi_slope(hh) * distf
    return jnp.where(valid, s, -1e30)


def _swa_fwd(proj, sinks, *, name, exchanges=()):
    t = proj.shape[0]
    nb = t // BLK
    kvb = KV_COL // (2 * 128)

    def body(sink_ref, q_ref, kvc_ref, kvp_ref, o_ref, lse_ref):
        n = pl.program_id(0)
        valid, distf = _swa_mask(n)
        for kvh in range(N_KV):
            k2 = _block_diag(kvp_ref, kvc_ref, kvh * HEAD_DIM)
            v2 = _block_diag(kvp_ref, kvc_ref, 128 + kvh * HEAD_DIM)
            for jp in range(PAIRS_PER_KV):
                pair = kvh * PAIRS_PER_KV + jp
                lanes = slice(pair * PAIR, (pair + 1) * PAIR)
                s2 = lax.dot_general(k2, q_ref[:, lanes].astype(BF16), NT, preferred_element_type=F32)
                probs = []
                for e in range(2):
                    hh = 2 * pair + e
                    s = _swa_scores(s2, e, hh, valid, distf)
                    sink = sink_ref[0, hh]
                    mx = jnp.maximum(jnp.max(s, axis=0, keepdims=True), sink)
                    p = jnp.exp(s - mx)
                    l = jnp.sum(p, axis=0, keepdims=True) + jnp.exp(sink - mx)
                    probs.append((p * (1.0 / l)).astype(BF16))
                    lse_ref[hh:hh + 1, :] = mx + jnp.log(l)
                o_ref[:, lanes] = lax.dot_general(jnp.concatenate(probs, axis=0), v2, TN,
                                                  preferred_element_type=F32)

    res, xres = _call(
        body, name=name, grid=(nb,),
        in_specs=[SMEM,
                  pl.BlockSpec((BLK, ATTN_W), lambda n: (n, 0)),
                  pl.BlockSpec((BLK, 256), lambda n: (n, kvb)),
                  pl.BlockSpec((BLK, 256), lambda n: (jnp.maximum(n - 1, 0), kvb))],
        out_specs=[pl.BlockSpec((BLK, ATTN_W), lambda n: (n, 0)),
                   pl.BlockSpec((None, N_Q, BLK), lambda n: (n, 0, 0))],
        out_shape=[jax.ShapeDtypeStruct((t, ATTN_W), F32), jax.ShapeDtypeStruct((nb, N_Q, BLK), F32)],
        args=[sinks, proj, proj, proj], semantics=("parallel",), exchanges=exchanges)
    return (*res, xres) if exchanges else res


def _swa_bwd(proj, sinks, dattn, lse, *, name, exchanges=()):
    t = proj.shape[0]
    nb = t // BLK
    kvb = KV_COL // (2 * 128)

    def body(sink_ref, q_ref, kvc_ref, kvp_ref, do_ref, lse_ref, dq_ref, dkv_ref, dsink_ref, carry_ref):
        n = pl.program_id(0)

        @pl.when(n == 0)
        def _():
            dsink_ref[...] = jnp.zeros_like(dsink_ref)
            carry_ref[...] = jnp.zeros_like(carry_ref)

        @pl.when(n < nb)
        def _():
            valid, distf = _swa_mask(n)
            for kvh in range(N_KV):
                k2 = _block_diag(kvp_ref, kvc_ref, kvh * HEAD_DIM)
                v2 = _block_diag(kvp_ref, kvc_ref, 128 + kvh * HEAD_DIM)
                dk2 = jnp.zeros((4 * BLK, PAIR), F32)
                dv2 = jnp.zeros((4 * BLK, PAIR), F32)
                for jp in range(PAIRS_PER_KV):
                    pair = kvh * PAIRS_PER_KV + jp
                    lanes = slice(pair * PAIR, (pair + 1) * PAIR)
                    q2 = q_ref[:, lanes].astype(BF16)
                    do2 = do_ref[:, lanes].astype(BF16)
                    s2 = lax.dot_general(k2, q2, NT, preferred_element_type=F32)
                    dp2 = lax.dot_general(v2, do2, NT, preferred_element_type=F32)
                    probs, dss = [], []
                    for e in range(2):
                        hh = 2 * pair + e
                        lse_h = lse_ref[hh:hh + 1, :]
                        p = jnp.exp(_swa_scores(s2, e, hh, valid, distf) - lse_h)
                        dp = dp2[2 * BLK * e:2 * BLK * (e + 1)]
                        delta = jnp.sum(p * dp, axis=0, keepdims=True)
                        dsink_ref[hh:hh + 1, :] += -jnp.exp(sink_ref[0, hh] - lse_h) * delta
                        probs.append(p.astype(BF16))
                        dss.append((p * (dp - delta)).astype(BF16))
                    ds2 = jnp.concatenate(dss, axis=0)
                    dq_ref[:, lanes] = (lax.dot_general(ds2, k2, TN, preferred_element_type=F32)
                                        * (HEAD_DIM ** -0.5)).astype(BF16)
                    dk2 = dk2 + jnp.dot(ds2, q2, preferred_element_type=F32)
                    dv2 = dv2 + jnp.dot(jnp.concatenate(probs, axis=0), do2, preferred_element_type=F32)
                dk_cat = (dk2[:2 * BLK, :HEAD_DIM] + dk2[2 * BLK:, HEAD_DIM:]) * (HEAD_DIM ** -0.5)
                dv_cat = dv2[:2 * BLK, :HEAD_DIM] + dv2[2 * BLK:, HEAD_DIM:]
                ko = kvh * HEAD_DIM
                vo = 128 + kvh * HEAD_DIM
                dkv_ref[:, ko:ko + HEAD_DIM] = (carry_ref[:, ko:ko + HEAD_DIM] + dk_cat[:BLK]).astype(BF16)
                dkv_ref[:, vo:vo + HEAD_DIM] = (carry_ref[:, vo:vo + HEAD_DIM] + dv_cat[:BLK]).astype(BF16)
                carry_ref[:, ko:ko + HEAD_DIM] = dk_cat[BLK:]
                carry_ref[:, vo:vo + HEAD_DIM] = dv_cat[BLK:]

        @pl.when(n == nb)
        def _():
            dkv_ref[...] = carry_ref[...].astype(BF16)

    last = nb - 1
    res, xres = _call(
        body, name=name, grid=(nb + 1,),
        in_specs=[SMEM,
                  pl.BlockSpec((BLK, ATTN_W), lambda n: (jnp.minimum(n, last), 0)),
                  pl.BlockSpec((BLK, 256), lambda n: (jnp.minimum(n, last), kvb)),
                  pl.BlockSpec((BLK, 256), lambda n: (jnp.maximum(jnp.minimum(n, last) - 1, 0), kvb)),
                  pl.BlockSpec((BLK, ATTN_W), lambda n: (jnp.minimum(n, last), 0)),
                  pl.BlockSpec((None, N_Q, BLK), lambda n: (jnp.minimum(n, last), 0, 0))],
        out_specs=[pl.BlockSpec((BLK, ATTN_W), lambda n: (jnp.minimum(n, last), 0)),
                   pl.BlockSpec((BLK, 256), lambda n: (jnp.maximum(n - 1, 0), 0)),
                   pl.BlockSpec((N_Q, BLK), lambda n: (0, 0))],
        out_shape=[jax.ShapeDtypeStruct((t, ATTN_W), BF16), jax.ShapeDtypeStruct((t, 256), BF16),
                   jax.ShapeDtypeStruct((N_Q, BLK), F32)],
        scratch_shapes=[pltpu.VMEM((BLK, 256), F32)],
        args=[sinks, proj, proj, proj, dattn, lse], semantics=("arbitrary",), exchanges=exchanges)
    return (*res, xres) if exchanges else res


def _cumsum_rows(x):
    n = x.shape[0]
    row = lax.broadcasted_iota(jnp.int32, x.shape, 0)
    s = 1
    while s < n:
        x = x + jnp.where(row >= s, pltpu.roll(x, s, axis=0), 0.0)
        s *= 2
    return x


def _rev_cumsum_rows(x):
    n = x.shape[0]
    row = lax.broadcasted_iota(jnp.int32, x.shape, 0)
    s = 1
    while s < n:
        x = x + jnp.where(row < n - s, pltpu.roll(x, n - s, axis=0), 0.0)
        s *= 2
    return x


def _lower_bound(lbl_ref):
    l0 = lbl_ref[0:1, :]
    l1 = lbl_ref[1:2, :]
    mx = jnp.maximum(l0, l1)
    e0 = jnp.exp(l0 - mx)
    e1 = jnp.exp(l1 - mx)
    return e0 / (e0 + e1)


def _hgrn_gates(z, lb):
    sg = _sigmoid(z)
    f = lb + (1.0 - lb) * sg
    return sg, f, jnp.log(f), 1.0 - f


def _sub_factors(b, i, sub):
    rows = lax.broadcasted_iota(jnp.int32, (CHUNK, RNN_HD), 0)
    ref = b[sub * i - 1:sub * i, :]
    qfac = jnp.exp(b[sub * i:sub * (i + 1), :] - ref)
    kfac = jnp.where(rows < sub * i, jnp.exp(ref - b), 0.0)
    return qfac, kfac


def _diag_decay(bi, s):
    trow = lax.broadcasted_iota(jnp.int32, bi.shape, 0)
    return jnp.where(trow >= s, jnp.exp(bi - bi[s:s + 1, :]), 0.0)


def _hgrn_fwd(proj, lb_logits, norm_gain, *, tb, name, exchanges=()):
    t = proj.shape[0]
    ntb = t // tb
    nch = tb // CHUNK
    qb, fb, ib, gb = QR_COL // 128, FR_COL // 128, IR_COL // 128, GR_COL // 128

    def body(q_ref, f_ref, i_ref, g_ref, lbl_ref, gain_ref, o_ref, out_ref, s0_ref, st_ref):
        c = pl.program_id(1)

        @pl.when(c == 0)
        def _():
            st_ref[...] = jnp.zeros_like(st_ref)

        lb = _lower_bound(lbl_ref)
        gain = gain_ref[...]

        def chunk(ci, st):
            rows = slice(ci * CHUNK, (ci + 1) * CHUNK)
            _, _, lf, k = _hgrn_gates(f_ref[rows, :], lb)
            qr = q_ref[rows, :]
            q = qr * _sigmoid(qr)
            v = i_ref[rows, :]
            b = _cumsum_rows(lf)
            s0_ref[ci] = st
            o_inter = lax.dot_general((q * jnp.exp(b)).astype(BF16), st.astype(BF16), NT,
                                      preferred_element_type=F32)
            vb = v.astype(BF16)
            blast = b[CHUNK - 1:CHUNK, :]
            khat = (k * jnp.exp(blast - b)).astype(BF16)
            st = st * jnp.exp(blast) + lax.dot_general(vb, khat, TN, preferred_element_type=F32)
            blocks = []
            for i in range(CHUNK // SUB_FWD):
                blk = slice(SUB_FWD * i, SUB_FWD * (i + 1))
                qi, ki, vi, bi = q[blk], k[blk], v[blk], b[blk]
                oi = o_inter[blk]
                if i > 0:
                    qfac, kfac = _sub_factors(b, i, SUB_FWD)
                    att = lax.dot_general((qi * qfac).astype(BF16), (k * kfac).astype(BF16), NT,
                                          preferred_element_type=F32)
                    oi = oi + jnp.dot(att.astype(BF16), vb, preferred_element_type=F32)
                for s in range(SUB_FWD):
                    qe = qi * _diag_decay(bi, s)
                    a = jnp.sum(qe * ki[s:s + 1, :], axis=1, keepdims=True)
                    oi = oi + a * vi[s:s + 1, :]
                blocks.append(oi)
            o = jnp.concatenate(blocks, axis=0)
            o_ref[rows, :] = o
            gr = g_ref[rows, :]
            out_ref[rows, :] = o * _rstd(o) * gain * (gr * _sigmoid(gr))
            return st

        st = st_ref[...]
        for ci in range(nch):
            st = chunk(ci, st)
        st_ref[...] = st

    def col(base):
        return pl.BlockSpec((tb, RNN_HD), lambda h, c: (c, base + h))

    res, xres = _call(
        body, name=name, grid=(N_RNN, ntb),
        in_specs=[col(qb), col(fb), col(ib), col(gb),
                  pl.BlockSpec((2, RNN_HD), lambda h, c: (0, h)), pl.BlockSpec((1, RNN_HD), lambda h, c: (0, 0))],
        out_specs=[pl.BlockSpec((tb, RNN_HD), lambda h, c: (c, h)), pl.BlockSpec((tb, RNN_HD), lambda h, c: (c, h)),
                   pl.BlockSpec((None, nch, RNN_HD, RNN_HD), lambda h, c: (h, c, 0, 0))],
        out_shape=[jax.ShapeDtypeStruct((t, RNN_W), F32), jax.ShapeDtypeStruct((t, RNN_W), F32),
                   jax.ShapeDtypeStruct((N_RNN, t // CHUNK, RNN_HD, RNN_HD), F32)],
        scratch_shapes=[pltpu.VMEM((RNN_HD, RNN_HD), F32)],
        args=[proj, proj, proj, proj, lb_logits, norm_gain],
        semantics=("parallel", "arbitrary"), exchanges=exchanges)
    return (*res, xres) if exchanges else res


def _hgrn_bwd(proj, lb_logits, norm_gain, o_pre, s0, dcat, *, tb, name, exchanges=()):
    t = proj.shape[0]
    ntb = t // tb
    nch = tb // CHUNK
    qb, fb, ib, gb = QR_COL // 128, FR_COL // 128, IR_COL // 128, GR_COL // 128
    sub = SUB_BWD
    nsub = CHUNK // sub

    def body(q_ref, f_ref, i_ref, g_ref, lbl_ref, gain_ref, o_ref, s0_ref, dout_ref,
             dq_ref, df_ref, di_ref, dg_ref, dlb_ref, dgain_ref,
             dst_ref, dqs_ref, dks_ref, dvs_ref):
        c = pl.program_id(1)

        @pl.when(c == 0)
        def _():
            dst_ref[...] = jnp.zeros_like(dst_ref)
            dlb_ref[...] = jnp.zeros_like(dlb_ref)
            dgain_ref[...] = jnp.zeros_like(dgain_ref)

        lb = _lower_bound(lbl_ref)
        gain = gain_ref[...]

        def chunk(ci, dst):
            rows = slice(ci * CHUNK, (ci + 1) * CHUNK)
            dqa_ref, dka_ref, dva_ref = dqs_ref.at[ci], dks_ref.at[ci], dvs_ref.at[ci]
            sg, f, lf, k = _hgrn_gates(f_ref[rows, :], lb)
            qr = q_ref[rows, :]
            sq = _sigmoid(qr)
            q = qr * sq
            v = i_ref[rows, :]
            b = _cumsum_rows(lf)

            dout = dout_ref[rows, :]
            o = o_ref[rows, :]
            gr = g_ref[rows, :]
            sgg = _sigmoid(gr)
            gate = gr * sgg
            rs = _rstd(o)
            nrm = o * rs
            dg_ref[rows, :] = (dout * nrm * gain * (sgg * (1.0 + gr * (1.0 - sgg)))).astype(BF16)
            dn = dout * gate
            dgain_ref[...] += jnp.sum(dn * nrm, axis=0, keepdims=True)
            tt = dn * gain
            do = rs * (tt - nrm * jnp.mean(tt * nrm, axis=-1, keepdims=True))

            dob = do.astype(BF16)
            vb = v.astype(BF16)
            eb = jnp.exp(b)
            blast = b[CHUNK - 1:CHUNK, :]
            ebl = jnp.exp(blast - b)
            dstb = dst.astype(BF16)
            khat = (k * ebl).astype(BF16)
            s0 = s0_ref[ci]
            dqa_ref[...] = eb * jnp.dot(dob, s0.astype(BF16), preferred_element_type=F32)
            dk_state = ebl * jnp.dot(vb, dstb, preferred_element_type=F32)
            dka_ref[...] = dk_state
            d_blast = (jnp.sum(k * dk_state, axis=0, keepdims=True)
                       + jnp.exp(blast) * jnp.sum(dst * s0, axis=0, keepdims=True))
            dva_ref[...] = lax.dot_general(khat, dstb, NT, preferred_element_type=F32)
            dst_next = dst * jnp.exp(blast) + lax.dot_general(dob, (q * eb).astype(BF16), TN,
                                                              preferred_element_type=F32)
            pm = lax.dot_general(dob, vb, NT, preferred_element_type=F32)
            for i in range(nsub):
                blk = slice(sub * i, sub * (i + 1))
                qi, ki, vi, bi, doi = q[blk], k[blk], v[blk], b[blk], do[blk]
                dqi = dqa_ref[blk, :]
                if i > 0:
                    qfac, kfac = _sub_factors(b, i, sub)
                    qt = (qi * qfac).astype(BF16)
                    kt = (k * kfac).astype(BF16)
                    att = lax.dot_general(qt, kt, NT, preferred_element_type=F32).astype(BF16)
                    pmi = pm[blk, :].astype(BF16)
                    dva_ref[...] += lax.dot_general(att, doi.astype(BF16), TN, preferred_element_type=F32)
                    dqi = dqi + qfac * jnp.dot(pmi, kt, preferred_element_type=F32)
                    dka_ref[...] += kfac * lax.dot_general(pmi, qt, TN, preferred_element_type=F32)
                for s in range(sub):
                    e = _diag_decay(bi, s)
                    ks = ki[s:s + 1, :]
                    row = slice(sub * i + s, sub * i + s + 1)
                    a = jnp.sum(qi * e * ks, axis=1, keepdims=True)
                    pe = jnp.sum(doi * vi[s:s + 1, :], axis=1, keepdims=True) * e
                    dqi = dqi + pe * ks
                    dka_ref[row, :] += jnp.sum(pe * qi, axis=0, keepdims=True)
                    dva_ref[row, :] += jnp.sum(a * doi, axis=0, keepdims=True)
                dqa_ref[blk, :] = dqi

            dq = dqa_ref[...]
            dk = dka_ref[...]
            lastrow = lax.broadcasted_iota(jnp.int32, (CHUNK, RNN_HD), 0) == CHUNK - 1
            dlf = _rev_cumsum_rows(q * dq - k * dk + jnp.where(lastrow, d_blast, 0.0))
            dff = dlf / f - dk
            df_ref[rows, :] = (dff * (1.0 - lb) * sg * (1.0 - sg)).astype(BF16)
            dlb_ref[...] += jnp.sum(dff * (1.0 - sg), axis=0, keepdims=True)
            dq_ref[rows, :] = (dq * (sq * (1.0 + qr * (1.0 - sq)))).astype(BF16)
            di_ref[rows, :] = dva_ref[...].astype(BF16)
            return dst_next

        dst = dst_ref[...]
        for ci in reversed(range(nch)):
            dst = chunk(ci, dst)
        dst_ref[...] = dst

    def col(base):
        return pl.BlockSpec((tb, RNN_HD), lambda h, c: (ntb - 1 - c, base + h))

    outc = pl.BlockSpec((tb, RNN_HD), lambda h, c: (ntb - 1 - c, h))
    hb = ATTN_W // RNN_HD
    res, xres = _call(
        body, name=name, grid=(N_RNN, ntb),
        in_specs=[col(qb), col(fb), col(ib), col(gb),
                  pl.BlockSpec((2, RNN_HD), lambda h, c: (0, h)), pl.BlockSpec((1, RNN_HD), lambda h, c: (0, 0)),
                  outc,
                  pl.BlockSpec((None, nch, RNN_HD, RNN_HD), lambda h, c: (h, ntb - 1 - c, 0, 0)),
                  pl.BlockSpec((tb, RNN_HD), lambda h, c: (ntb - 1 - c, hb + h))],
        out_specs=[outc, outc, outc, outc,
                   pl.BlockSpec((1, RNN_HD), lambda h, c: (0, h)),
                   pl.BlockSpec((None, 1, RNN_HD), lambda h, c: (h, 0, 0))],
        out_shape=[jax.ShapeDtypeStruct((t, RNN_W), BF16)] * 4
        + [jax.ShapeDtypeStruct((1, RNN_W), F32), jax.ShapeDtypeStruct((N_RNN, 1, RNN_HD), F32)],
        scratch_shapes=[pltpu.VMEM((RNN_HD, RNN_HD), F32),
                        pltpu.VMEM((nch, CHUNK, RNN_HD), F32), pltpu.VMEM((nch, CHUNK, RNN_HD), F32),
                        pltpu.VMEM((nch, CHUNK, RNN_HD), F32)],
        args=[proj, proj, proj, proj, lb_logits, norm_gain, o_pre, s0, dcat],
        semantics=("parallel", "arbitrary"), exchanges=exchanges)
    return (*res, xres) if exchanges else res


def _cast_slots(w, where, *, name):
    _, rows, cols = w.shape
    rh = rows // 2
    tr = _row_tile(rh, cols)
    nh = rh // tr

    def body(wh_ref, w_ref, o_ref):
        o_ref[...] = w_ref[...].astype(BF16)

    return pl.pallas_call(
        body, name=name,
        grid_spec=pltpu.PrefetchScalarGridSpec(
            num_scalar_prefetch=1, grid=(2, nh),
            in_specs=[pl.BlockSpec((None, tr, cols), lambda h, i, wh: (0, h * nh + i, 0))],
            out_specs=pl.BlockSpec((None, tr, cols), lambda h, i, wh: (2 * wh[0] + h, i, 0))),
        out_shape=jax.ShapeDtypeStruct((8, rh, cols), BF16),
        compiler_params=_params(("parallel", "parallel")),
    )(where, w)


def _all_gather_halves(bufs, *, name):
    n = len(bufs)

    def body(*refs):
        ins, outs = refs[:n], refs[n:2 * n]
        send_sems, recv_sems = refs[2 * n:]
        x, y, c = _place()
        sibling = (x, y, 1 - c)
        chips = [(1 - x, y), (x, 1 - y), (1 - x, 1 - y)]

        def copy(a, k, block, to, src=None):
            slot = outs[a].at[4 * block[0] + 2 * block[1] + block[2]]
            return pltpu.make_async_remote_copy(
                src_ref=slot if src is None else src, dst_ref=slot,
                send_sem=send_sems.at[a, k], recv_sem=recv_sems.at[a, k],
                device_id=to, device_id_type=MESH)

        first, passed = [], []
        for a in range(n):
            for j, chip in enumerate(chips):
                cp = copy(a, j, (x, y, c), (*chip, c), src=ins[a].at[4 * x + 2 * y + c])
                cp.start()
                first.append(cp)
        for a in range(n):
            for j, chip in enumerate(chips):
                copy(a, j, (*chip, c), (x, y, c)).wait_recv()
                cp = copy(a, 3 + j, (*chip, c), sibling)
                cp.start()
                passed.append(cp)
        for a in range(n):
            for j, chip in enumerate(chips):
                copy(a, 3 + j, (*chip, 1 - c), (x, y, c)).wait_recv()
        for cp in first + passed:
            cp.wait_send()

    return pl.pallas_call(
        body, name=name,
        in_specs=[ANY] * n, out_specs=[ANY] * n,
        out_shape=[jax.ShapeDtypeStruct(b.shape, b.dtype) for b in bufs],
        scratch_shapes=[pltpu.SemaphoreType.DMA((n, 6)), pltpu.SemaphoreType.DMA((n, 6))],
        input_output_aliases={a: a for a in range(n)},
    )(*bufs)


def _row_tile(rows, cols, budget=1 << 20):
    tr = rows
    while tr * cols > budget and tr % 16 == 0:
        tr //= 2
    return tr


def _half_spec(g, tr, halves_last, slab):
    if halves_last:
        return pl.BlockSpec((None, tr, g.shape[2] // 2), lambda *a: (slab(*a), a[-2], a[-1][1]))
    return pl.BlockSpec((None, None, tr, g.shape[3]), lambda *a: (slab(*a), a[-1][1], a[-2], 0))


def _pair_sum(g, sib, where, *, name, halves_last=False):
    rh, cols = sib.shape[1:]
    tr = _row_tile(rh, cols)

    def body(w_ref, g_ref, s_ref, o_ref):
        o_ref[...] = (g_ref[...] + s_ref[...]).astype(BF16)

    return pl.pallas_call(
        body, name=name,
        grid_spec=pltpu.PrefetchScalarGridSpec(
            num_scalar_prefetch=1, grid=(4, rh // tr),
            in_specs=[_half_spec(g, tr, halves_last, lambda s, i, w: s),
                      pl.BlockSpec((None, tr, cols), lambda s, i, w: (s, i, 0))],
            out_specs=pl.BlockSpec((None, tr, cols), lambda s, i, w: (s, i, 0))),
        out_shape=jax.ShapeDtypeStruct((4, rh, cols), BF16),
        compiler_params=_params(("parallel", "parallel")),
    )(where, g, sib)


def _final_half(g, sib, recv, where, *, name, halves_last=False):
    rh, cols = sib.shape[1:]
    tr = _row_tile(rh, cols)

    def body(w_ref, g_ref, s_ref, r_ref, o_ref):
        acc = g_ref[...] + s_ref[...]
        for j in range(3):
            acc = acc + r_ref[j].astype(F32)
        o_ref[...] = acc

    return pl.pallas_call(
        body, name=name,
        grid_spec=pltpu.PrefetchScalarGridSpec(
            num_scalar_prefetch=1, grid=(rh // tr,),
            in_specs=[_half_spec(g, tr, halves_last, lambda i, w: w[0]),
                      pl.BlockSpec((None, tr, cols), lambda i, w: (w[0], i, 0)),
                      pl.BlockSpec((3, tr, cols), lambda i, w: (0, i, 0))],
            out_specs=pl.BlockSpec((tr, cols), lambda i, w: (i, 0))),
        out_shape=jax.ShapeDtypeStruct((rh, cols), F32),
        compiler_params=_params(("parallel",)),
    )(where, g, sib, recv)


def _adamw_math(w, g, m, v):
    m = ADAM_B1 * m + (1.0 - ADAM_B1) * g
    v = ADAM_B2 * v + (1.0 - ADAM_B2) * (g * g)
    m_hat = m / (1.0 - ADAM_B1 ** ADAM_STEP)
    v_hat = v / (1.0 - ADAM_B2 ** ADAM_STEP)
    delta = -ADAM_LR * (m_hat / (jnp.sqrt(v_hat) + ADAM_EPS) + ADAM_WD * w)
    return delta, m, v


def _adamw(w, mine, theirs, m, v, where, *, name, halves_last=False):
    _, rows, cols = w.shape
    if halves_last:
        cols //= 2
        tr = _row_tile(rows, cols, budget=1 << 19)
        grid = (rows // tr, 2)
        blk = pl.BlockSpec((None, tr, cols), lambda i, h, wh: (0, i, h))
        half = pl.BlockSpec((tr, cols), lambda i, h, wh: (i, 0))
        which = lambda: pl.program_id(1)
    else:
        tr = _row_tile(rows // 2, cols, budget=1 << 19)
        nh = rows // 2 // tr
        grid = (rows // tr,)
        blk = pl.BlockSpec((None, tr, cols), lambda i, wh: (0, i, 0))
        half = pl.BlockSpec((tr, cols), lambda i, wh: (i % nh, 0))
        which = lambda: pl.program_id(0) // nh

    def body(wh_ref, w_ref, a_ref, b_ref, m_ref, v_ref, g_ref, d_ref, nm_ref, nv_ref):
        g = jnp.where(which() == wh_ref[1], a_ref[...], b_ref[...])
        d, nm, nv = _adamw_math(w_ref[...], g, m_ref[...], v_ref[...])
        g_ref[...] = g
        d_ref[...] = d
        nm_ref[...] = nm
        nv_ref[...] = nv

    rows, cols = w.shape[1:]
    return pl.pallas_call(
        body, name=name,
        grid_spec=pltpu.PrefetchScalarGridSpec(
            num_scalar_prefetch=1, grid=grid,
            in_specs=[blk, half, half, blk, blk], out_specs=[blk] * 4),
        out_shape=[jax.ShapeDtypeStruct((1, rows, cols), F32)] * 4,
        compiler_params=_params(("parallel",) * len(grid)),
    )(where, w, mine, theirs, m, v)


SEG_LOSS = 0
SEG_SINK = 128
SEG_AGAIN = 256
SEG_L0 = SEG_AGAIN + ATTN_W
SEG_L1 = SEG_L0 + RNN_W
SEG_RGAIN = SEG_L1 + RNN_W
SEG_G = SEG_RGAIN + 128
N_PACK = SEG_G + 4 * D_MODEL


def _pack(sinks, again, l0, l1, rgain, gains, loss=None):
    z = lambda k: jnp.zeros((1, k), F32)
    first = z(128) if loss is None else loss
    return jnp.concatenate([first, sinks, z(128 - N_Q), again, l0, l1, rgain] + list(gains), axis=1)


def _small_reduce_adamw(part, w, m, v, *, name):
    def body(p_ref, w_ref, m_ref, v_ref, g_ref, d_ref, nm_ref, nv_ref, buf_ref, send_sems, recv_sems):
        x, y, c = _place()
        me = 4 * x + 2 * y + c
        copies = []
        for k in range(1, 8):
            dx, dy, dc = (k >> 2) & 1, (k >> 1) & 1, k & 1
            to = (x ^ dx, y ^ dy, c ^ dc)
            cp = pltpu.make_async_remote_copy(
                src_ref=p_ref, dst_ref=buf_ref.at[me],
                send_sem=send_sems.at[k - 1], recv_sem=recv_sems.at[k - 1],
                device_id=to, device_id_type=MESH)
            cp.start()
            copies.append(cp)
        buf_ref[me] = p_ref[...]
        for cp in copies:
            cp.wait()
        tot = buf_ref[0]
        for j in range(1, 8):
            tot = tot + buf_ref[j]
        g_ref[...] = tot
        l0 = w_ref[:, SEG_L0:SEG_L0 + RNN_W]
        l1 = w_ref[:, SEG_L1:SEG_L1 + RNN_W]
        mx = jnp.maximum(l0, l1)
        e0 = jnp.exp(l0 - mx)
        e1 = jnp.exp(l1 - mx)
        lb = e0 / (e0 + e1)
        gl0 = tot[:, SEG_L0:SEG_L0 + RNN_W] * lb * (1.0 - lb)
        g_ref[:, SEG_L0:SEG_L0 + RNN_W] = gl0
        g_ref[:, SEG_L1:SEG_L1 + RNN_W] = -gl0
        d, nm, nv = _adamw_math(w_ref[...], g_ref[...], m_ref[...], v_ref[...])
        d_ref[...] = d
        nm_ref[...] = nm
        nv_ref[...] = nv

    vm = pl.BlockSpec(memory_space=pltpu.VMEM)
    return pl.pallas_call(
        body, name=name,
        in_specs=[vm] * 4, out_specs=[vm] * 4,
        out_shape=[jax.ShapeDtypeStruct((1, N_PACK), F32)] * 4,
        scratch_shapes=[pltpu.VMEM((8, 1, N_PACK), F32), pltpu.SemaphoreType.DMA((7,)),
                        pltpu.SemaphoreType.DMA((7,))],
    )(part, w, m, v)


def _layer_grads(xs, tgt, bufs, where, sinks, again, lb_logits, rgain,
                 g_mix_pre, g_mix_post, g_mlp_pre, g_mlp_post):
    tm = 512
    b_in, b_out, b_up, b_dn = bufs

    shard = IN_W // N_CHIPS
    w_in_t = _all_gather_halves([b_in], name="gather_w_in")[0].reshape(IN_W, D_MODEL)
    h1 = _rms_cast(xs, g_mix_pre, tm=tm, name="h1_norm")
    proj, ((b_out, b_up),) = _mm(
        h1, w_in_t, tm=1024, tn=768, tk=D_MODEL, out_dtype=F32, w_layout="nk", name="in_proj",
        exchanges=[_x_gather([b_out, b_up], ici=[(0, 256), (0, 384)])])
    attn, lse, ((b_out, b_up),) = _swa_fwd(
        proj, sinks, name="swa_fwd",
        exchanges=[_x_gather([b_out, b_up], ici=[None, (384, 320)], d2d=[(0, 256), None])])
    w_out = b_out.reshape(D_MODEL, D_MODEL)
    o_pre, rnn, s0, ((b_up, b_dn),) = _hgrn_fwd(
        proj, lb_logits, rgain, tb=512, name="hgrn_fwd",
        exchanges=[_x_gather([b_up, b_dn], ici=[(704, 320), (0, 608)])])
    cat = _mix_cat(attn, rnn, again, tm=tm, name="mix_cat")
    mixed, ((b_up, b_dn),) = _mm(
        cat, w_out, tm=1024, tn=1024, tk=D_MODEL, out_dtype=F32, name="out_proj",
        exchanges=[_x_gather([b_up, b_dn], ici=[None, (608, 256)], d2d=[(0, 1024), (0, 608)])])
    w_up4 = b_up.reshape(N_CHIPS, D_MODEL, D_FF // N_CHIPS)
    x1, h2, ((b_dn,),) = _post_norm_res(
        mixed, g_mix_post, xs, g_mlp_pre, tm=256, name="mix_post",
        exchanges=[_x_gather([b_dn], ici=[(864, 160)], d2d=[(608, 256)])])
    u, ((b_dn,),) = _mm(h2, w_up4, tm=1024, tn=1024, tk=D_MODEL, out_dtype=BF16, relu=True, w_layout="skn",
                        name="mlp_up", exchanges=[_x_gather([b_dn], d2d=[(864, 160)])])
    w_dn = b_dn.reshape(D_FF, D_MODEL)
    yv = _mm(u, w_dn, tm=1024, tn=1024, tk=2048, out_dtype=F32, a_square=True, name="mlp_down")
    dy, dx2, loss_row, dg_mlp_post = _loss_head(yv, g_mlp_post, x1, tgt, tm=256, name="loss_head")

    def halved(g):
        return g.reshape(N_CHIPS, 2, g.shape[1] // 2, g.shape[2])
    du = _mm(dy, w_dn, tm=1024, tn=1024, tk=D_MODEL, out_dtype=BF16, mul2=u, w_layout="nk", name="mlp_down_bwd")
    g_dn = halved(_mm_tn(u, dy, tm=1024, tn=1024, tt=2048, a_square=True, name="w_down_grad")
                  .reshape(N_CHIPS, D_FF // N_CHIPS, D_MODEL))
    d_w_up, ((sib_dn,),) = _mm_tn(h2, du, tm=1024, tn=1024, tt=2048, n_split=N_CHIPS, name="w_up_grad",
                                  exchanges=[_x_pair([g_dn])])
    g_up = halved(d_w_up)
    wire_dn = _pair_sum(g_dn, sib_dn, where, name="pair_sum_w_down")
    dh2, ((recv_dn,), (sib_up,)) = _mm(du, w_up4, tm=1024, tn=1024, tk=2048, out_dtype=F32, w_layout="snk", name="mlp_up_bwd",
                                       exchanges=[_x_chip([wire_dn], rows=[(0, 800)]), _x_pair([g_up])])
    wire_up = _pair_sum(g_up, sib_up, where, name="pair_sum_w_up")
    dx1, dg_mlp_pre, ((recv_dn,),) = _rms_bwd(dh2, x1, g_mlp_pre, dx2, tm=256, out_dtype=F32, name="mlp_pre_bwd",
                                              exchanges=[_x_chip([wire_dn], rows=[(800, 224)], into=[recv_dn])])
    fin_dn = _final_half(g_dn, sib_dn, recv_dn, where, name="final_half_w_down")
    dmixed, dg_mix_post = _rms_bwd(dx1, mixed, g_mix_post, None, tm=256, out_dtype=BF16, name="mix_post_bwd")
    d_w_out, ((oth_dn,),) = _mm_tn(cat, dmixed, tm=1024, tn=1024, tt=2048, name="w_out_grad",
                                   exchanges=[_x_share([fin_dn])])
    g_out = halved(d_w_out.reshape(N_CHIPS, D_MODEL // N_CHIPS, D_MODEL))
    dcat, ((sib_out,),) = _mm(dmixed, w_out, tm=1024, tn=1024, tk=D_MODEL, out_dtype=F32, w_layout="nk", name="out_proj_bwd",
                              exchanges=[_x_pair([g_out])])
    wire_out = _pair_sum(g_out, sib_out, where, name="pair_sum_w_out")
    dattn, dg_again = _rms_bwd(dcat, attn, again, None, tm=tm, out_dtype=F32, name="attn_norm_bwd")
    dq_a, dkv, dsinks, ((recv_out,), (recv_up,)) = _swa_bwd(
        proj, sinks, dattn, lse, name="swa_bwd",
        exchanges=[_x_chip([wire_out]), _x_chip([wire_up], rows=[(0, 320)])])
    dq_r, df_r, di_r, dg_r, dlb, dgain_h, ((recv_up,),) = _hgrn_bwd(
        proj, lb_logits, rgain, o_pre, s0, dcat, tb=512, name="hgrn_bwd",
        exchanges=[_x_chip([wire_up], rows=[(320, 704)], into=[recv_up])])
    fin_up = _final_half(g_up, sib_up, recv_up, where, name="final_half_w_up")
    fin_out = _final_half(g_out, sib_out, recv_out, where, name="final_half_w_out")
    dproj = jnp.concatenate([dq_a, dkv, dq_r, df_r, di_r, dg_r], axis=1)
    piece_cols = D_MODEL // 4

    def w_in_piece(pc, exchanges):
        d, xres = _mm_tn(dproj, h1, tm=896, tn=piece_cols, tt=2048, n_blocks=(2, 2, pc),
                         name="w_in_grad_%d" % pc, exchanges=exchanges)
        return d.reshape(N_CHIPS, shard, 2 * piece_cols), xres

    g_in0, ((oth_up, oth_out),) = w_in_piece(0, [_x_share([fin_up, fin_out])])
    g_in1, ((sib_in0,),) = w_in_piece(1, [_x_pair([g_in0], halves_last=True)])
    wire_in0 = _pair_sum(g_in0, sib_in0, where, name="pair_sum_w_in_0", halves_last=True)
    dh1, ((recv_in0,), (sib_in1,)) = _mm(
        dproj, w_in_t, tm=1024, tn=1024, tk=2688, out_dtype=F32, name="in_proj_bwd",
        exchanges=[_x_chip([wire_in0]), _x_pair([g_in1], halves_last=True)])
    wire_in1 = _pair_sum(g_in1, sib_in1, where, name="pair_sum_w_in_1", halves_last=True)
    gx, dg_mix_pre, ((recv_in1,),) = _rms_bwd(dh1, xs, g_mix_pre, dx1, tm=256, out_dtype=F32, name="mix_pre_bwd",
                                              exchanges=[_x_chip([wire_in1])])
    fin_in0 = _final_half(g_in0, sib_in0, recv_in0, where, name="final_half_w_in_0", halves_last=True)
    fin_in1 = _final_half(g_in1, sib_in1, recv_in1, where, name="final_half_w_in_1", halves_last=True)
    oth_in0, oth_in1 = _run_exchange(_x_share([fin_in0, fin_in1]), name="share_w_in")
    fin_in = jnp.concatenate([fin_in0, fin_in1], axis=1)
    oth_in = jnp.concatenate([oth_in0, oth_in1], axis=1)

    big = [(fin_in, oth_in), (fin_out, oth_out), (fin_up, oth_up), (fin_dn, oth_dn)]
    drgain = jnp.sum(dgain_h, axis=0)
    small = _pack(jnp.sum(dsinks, axis=1)[None, :], dg_again, dlb, jnp.zeros_like(dlb), drgain,
                  [dg_mix_pre, dg_mix_post, dg_mlp_pre, dg_mlp_post], loss=loss_row)
    return gx, big, small


def kernel(x, w_in, attn_sinks, attn_out_gain, rnn_lb_logits, rnn_norm_gain, w_out, mix_pre_gain, mix_post_gain, mlp_pre_gain, mlp_post_gain, w_up, w_down, loss_target, m_w_in, m_attn_sinks, m_attn_out_gain, m_rnn_lb_logits, m_rnn_norm_gain, m_w_out, m_mix_pre_gain, m_mix_post_gain, m_mlp_pre_gain, m_mlp_post_gain, m_w_up, m_w_down, v_w_in, v_attn_sinks, v_attn_out_gain, v_rnn_lb_logits, v_rnn_norm_gain, v_w_out, v_mix_pre_gain, v_mix_post_gain, v_mlp_pre_gain, v_mlp_post_gain, v_w_up, v_w_down):
    ax, ay, ac = _place()
    where = jnp.stack([2 * ax + ay, ac]).astype(jnp.int32)
    t = lambda a: jnp.swapaxes(a, 1, 2)
    big_w = [t(w_in), w_out, w_up, w_down]
    big_m = [t(m_w_in), m_w_out, m_w_up, m_w_down]
    big_v = [t(v_w_in), v_w_out, v_w_up, v_w_down]

    names = ["w_in", "w_out", "w_up", "w_down"]
    bufs = [_cast_slots(w, where, name="cast_" + nm) for w, nm in zip(big_w, names)]
    gx, big_g, small_part = _layer_grads(
        x[0], loss_target[0], bufs, where, attn_sinks, attn_out_gain, rnn_lb_logits, rnn_norm_gain,
        mix_pre_gain, mix_post_gain, mlp_pre_gain, mlp_post_gain)

    grads, deltas, new_m, new_v = [], [], [], []
    for (f, o), w, m, v, nm in zip(big_g, big_w, big_m, big_v, names):
        res = _adamw(w, f, o, m, v, where, name="adamw_" + nm, halves_last=(nm == "w_in"))
        if nm == "w_in":
            res = [t(r) for r in res]
        g, d, nm_, nv_ = res
        grads.append(g)
        deltas.append(d)
        new_m.append(nm_)
        new_v.append(nv_)

    def pack_params(sinks, again, logits, rgain, gains):
        return _pack(sinks, again, logits[0:1], logits[1:2], rgain, gains)

    pw = pack_params(attn_sinks, attn_out_gain, rnn_lb_logits, rnn_norm_gain,
                     [mix_pre_gain, mix_post_gain, mlp_pre_gain, mlp_post_gain])
    pm = pack_params(m_attn_sinks, m_attn_out_gain, m_rnn_lb_logits, m_rnn_norm_gain,
                     [m_mix_pre_gain, m_mix_post_gain, m_mlp_pre_gain, m_mlp_post_gain])
    pv = pack_params(v_attn_sinks, v_attn_out_gain, v_rnn_lb_logits, v_rnn_norm_gain,
                     [v_mix_pre_gain, v_mix_post_gain, v_mlp_pre_gain, v_mlp_post_gain])
    packs = _small_reduce_adamw(small_part, pw, pm, pv, name="small_reduce_adamw")

    def unpack(p):
        seg = lambda o, k: p[:, o:o + k]
        logits = jnp.concatenate([seg(SEG_L0, RNN_W), seg(SEG_L1, RNN_W)], axis=0)
        gains = [seg(SEG_G + i * D_MODEL, D_MODEL) for i in range(4)]
        return dict(sinks=seg(SEG_SINK, N_Q), again=seg(SEG_AGAIN, ATTN_W), logits=logits,
                    rgain=seg(SEG_RGAIN, RNN_HD), gains=gains)

    def order(small, big):
        return [big[0], small["sinks"], small["again"], small["logits"], small["rgain"], big[1],
                *small["gains"], big[2], big[3]]

    loss = packs[0][0, 0]
    outs = [loss, gx[None]]
    for p, b in zip(packs, [grads, deltas, new_m, new_v]):
        outs += order(unpack(p), b)
    return tuple(outs)
```

```python
import functools

import jax
import jax.numpy as jnp
from jax import lax
from jax.experimental import pallas as pl
from jax.experimental.pallas import tpu as pltpu

F32 = jnp.float32
BF16 = jnp.bfloat16
MESH = pl.DeviceIdType.MESH

EPS = 1e-6
D_MODEL = 2048
ATTN_W = 1024
HEAD_DIM = 64
N_Q = 16
N_KV = 2
GROUP = 8
BLK = 128
RNN_W = 1024
RNN_HD = 128
N_RNN = 8
CHUNK = 64
SUB_FWD = 16
SUB_BWD = 8
D_FF = 8192
IN_W = 5376
N_CHIPS = 4
KV_COL = ATTN_W
QR_COL = ATTN_W + 2 * 128
FR_COL = QR_COL + RNN_W
IR_COL = FR_COL + RNN_W
GR_COL = IR_COL + RNN_W

ADAM_LR = 0.001
ADAM_B1 = 0.9
ADAM_B2 = 0.999
ADAM_EPS = 1e-08
ADAM_WD = 0.01
ADAM_STEP = 10

VMEM_LIMIT = 48 * 1024 * 1024

NT = (((1,), (1,)), ((), ()))
TN = (((0,), (0,)), ((), ()))


def _params(sem=None):
    return pltpu.CompilerParams(dimension_semantics=sem, vmem_limit_bytes=VMEM_LIMIT)


def _sigmoid(x):
    return 1.0 / (1.0 + jnp.exp(-x))


ANY = pl.BlockSpec(memory_space=pl.ANY)


def _place():
    return lax.axis_index("x"), lax.axis_index("y"), lax.axis_index("c")


def _other_chips(x, y):
    return [(1 - x, y), (x, 1 - y), (1 - x, 1 - y)]


class _Exchange:
    def __init__(self, srcs, outs, ncopy, build, aliases=None):
        self.srcs, self.outs, self.ncopy, self.build = list(srcs), list(outs), ncopy, build
        self.aliases = aliases or {}


def _remote(src, dst, send_sems, recv_sems, k, to):
    return pltpu.make_async_remote_copy(src_ref=src, dst_ref=dst, send_sem=send_sems.at[k],
                                        recv_sem=recv_sems.at[k], device_id=to, device_id_type=MESH)


def _call(body, *, name, grid, in_specs, out_specs, out_shape, args, scratch_shapes=(), semantics=None,
          exchanges=(), into=None):
    in_specs, out_specs, out_shape = list(in_specs), list(out_specs), list(out_shape)
    scratch_shapes = list(scratch_shapes)
    ni, no, ns = len(in_specs), len(out_specs), len(scratch_shapes)
    xsrc = [s for x in exchanges for s in x.srcs]
    xout = [o for x in exchanges for o in x.outs]
    into = into or {}
    xsrc += [into[k] for k in sorted(into)]
    nxi, nxo = len(xsrc), len(xout)
    aliases = {nxi - len(into) + ni + q: k for q, k in enumerate(sorted(into))}
    a0 = b0 = 0
    for x in exchanges:
        for si, oi in x.aliases.items():
            aliases[ni + a0 + si] = no + b0 + oi
        a0 += len(x.srcs)
        b0 += len(x.outs)
    sems = []
    for x in exchanges:
        sems += [pltpu.SemaphoreType.DMA((x.ncopy,)), pltpu.SemaphoreType.DMA((x.ncopy,))]

    def wrapped(*refs):
        ins, xi = refs[:ni], refs[ni:ni + nxi]
        outs, xo = refs[ni + nxi:ni + nxi + no], refs[ni + nxi + no:ni + nxi + no + nxo]
        rest = refs[ni + nxi + no + nxo:]
        scr, sm = rest[:ns], rest[ns:]

        def copies():
            cps = []
            a = b = 0
            for k, x in enumerate(exchanges):
                cps += x.build(xi[a:a + len(x.srcs)], xo[b:b + len(x.outs)], sm[2 * k], sm[2 * k + 1])
                a += len(x.srcs)
                b += len(x.outs)
            return cps

        def start():
            for cp in copies():
                cp.start()

        def wait():
            for cp in copies():
                cp.wait()

        if not exchanges:
            body(*ins, *outs, *scr)
        elif not grid:
            start()
            body(*ins, *outs, *scr)
            wait()
        else:
            first = last = None
            for ax, g in enumerate(grid):
                f = pl.program_id(ax) == 0
                l = pl.program_id(ax) == g - 1
                first = f if first is None else first & f
                last = l if last is None else last & l
            pl.when(first)(start)
            body(*ins, *outs, *scr)
            pl.when(last)(wait)

    if exchanges and semantics is not None:
        semantics = ("arbitrary",) * len(grid)
    kwargs = dict(grid=grid) if grid else {}
    res = pl.pallas_call(
        wrapped, name=name,
        in_specs=in_specs + [ANY] * nxi, out_specs=out_specs + [ANY] * nxo,
        out_shape=out_shape + xout, scratch_shapes=scratch_shapes + sems,
        input_output_aliases=aliases,
        compiler_params=_params(semantics), **kwargs,
    )(*args, *xsrc)
    res = list(res)
    mine, theirs = res[:no], res[no:]
    per = []
    b = 0
    for x in exchanges:
        per.append(theirs[b:b + len(x.outs)])
        b += len(x.outs)
    return mine, per


def _run_exchange(x, *, name):
    return _call(lambda: None, name=name, grid=(), in_specs=[], out_specs=[], out_shape=[], args=[],
                 exchanges=[x])[1][0]


def _x_gather(bufs, ici=None, d2d=None):
    n = len(bufs)
    plan = [(a, kind, rows[a]) for a in range(n) for kind, rows in (("ici", ici), ("d2d", d2d))
            if rows is not None and rows[a] is not None]

    def build(srcs, outs, ss, rs):
        x, y, c = _place()
        cps = []
        for q, (a, kind, rows) in enumerate(plan):
            piece = pl.ds(*rows)
            for j, (px, py) in enumerate(_other_chips(x, y)):
                slot, to = (4 * x + 2 * y + c, (px, py, c)) if kind == "ici" else (4 * px + 2 * py + c, (x, y, 1 - c))
                cps.append(_remote(srcs[a].at[slot, piece], outs[a].at[slot, piece], ss, rs, 3 * q + j, to))
        return cps

    outs = [jax.ShapeDtypeStruct(b.shape, b.dtype) for b in bufs]
    return _Exchange(bufs, outs, 3 * len(plan), build, aliases={a: a for a in range(n)})


def _x_pair(grads, halves_last=False):
    n = len(grads)

    def build(srcs, outs, ss, rs):
        x, y, c = _place()

        def half(r):
            if not halves_last:
                return r.at[:, 1 - c]
            ch = r.shape[2] // 2
            return r.at[:, :, pl.ds(pl.multiple_of((1 - c) * ch, 128), ch)]

        return [_remote(half(srcs[a]), outs[a], ss, rs, a, (x, y, 1 - c)) for a in range(n)]

    if halves_last:
        outs = [jax.ShapeDtypeStruct(g.shape[:2] + (g.shape[2] // 2,), g.dtype) for g in grads]
    else:
        outs = [jax.ShapeDtypeStruct((4,) + g.shape[2:], g.dtype) for g in grads]
    return _Exchange(grads, outs, n, build)


def _x_chip(wires, rows=None, into=None):
    n = len(wires)
    rows = rows or [(0, w.shape[1]) for w in wires]

    def build(srcs, outs, ss, rs):
        x, y, c = _place()
        cps = []
        for a in range(n):
            piece = pl.ds(*rows[a])
            for j, (px, py) in enumerate(_other_chips(x, y)):
                cps.append(_remote(srcs[a].at[2 * px + py, piece], outs[a].at[j, piece], ss, rs,
                                   3 * a + j, (px, py, c)))
        return cps

    outs = [jax.ShapeDtypeStruct((3,) + w.shape[1:], w.dtype) for w in wires]
    if into is None:
        return _Exchange(wires, outs, 3 * n, build)
    return _Exchange(list(wires) + list(into), outs, 3 * n, build, aliases={n + a: a for a in range(n)})


def _x_share(halves):
    n = len(halves)

    def build(srcs, outs, ss, rs):
        x, y, c = _place()
        return [_remote(srcs[a], outs[a], ss, rs, a, (x, y, 1 - c)) for a in range(n)]

    outs = [jax.ShapeDtypeStruct(h.shape, h.dtype) for h in halves]
    return _Exchange(halves, outs, n, build)


def _mm(a, w, *, tm, tn, tk, out_dtype, name, a_square=False, relu=False, mul2=None, w_layout="kn",
        m_blocks=None, out_into=None, exchanges=()):
    m, k = a.shape
    m_first, m_count = m_blocks or (0, m // tm)
    a_spec = pl.BlockSpec((tm, tk), lambda i, j, kk: (i + m_first, kk))
    if w_layout == "kn":
        n = w.shape[1]
        w_spec = pl.BlockSpec((tk, tn), lambda i, j, kk: (kk, j))
    elif w_layout == "nk":
        n = w.shape[0]
        w_spec = pl.BlockSpec((tn, tk), lambda i, j, kk: (j, kk))
    elif w_layout == "skn":
        n = w.shape[0] * w.shape[2]
        per_n = w.shape[2] // tn
        w_spec = pl.BlockSpec((None, tk, tn), lambda i, j, kk: (j // per_n, kk, j % per_n))
    else:
        assert w_layout == "snk"
        n = w.shape[1]
        per_k = w.shape[2] // tk
        w_spec = pl.BlockSpec((None, tn, tk), lambda i, j, kk: (kk // per_k, j, kk % per_k))
    w_dims = NT if w_layout in ("nk", "snk") else (((1,), (0,)), ((), ()))
    nk = k // tk
    assert m % tm == 0 and n % tn == 0 and k % tk == 0

    def body(*refs):
        if mul2 is not None:
            a_ref, w_ref, e_ref, o_ref, acc_ref = refs
        else:
            a_ref, w_ref, o_ref, acc_ref = refs
            e_ref = None
        kk = pl.program_id(2)
        av = a_ref[...]
        if a_square:
            af = av.astype(F32)
            av = (af * af).astype(BF16)
        part = lax.dot_general(av, w_ref[...], w_dims, preferred_element_type=F32)

        def finish(r):
            if relu:
                r = jnp.maximum(r, 0.0)
            if e_ref is not None:
                r = 2.0 * e_ref[...].astype(F32) * r
            o_ref[...] = r.astype(out_dtype)

        if nk == 1:
            finish(part)
        else:
            @pl.when(kk == 0)
            def _():
                acc_ref[...] = part

            @pl.when(kk > 0)
            def _():
                acc_ref[...] += part

            @pl.when(kk == nk - 1)
            def _():
                finish(acc_ref[...])

    in_specs = [a_spec, w_spec]
    args = [a, w]
    if mul2 is not None:
        in_specs.append(pl.BlockSpec((tm, tn), lambda i, j, kk: (i + m_first, j)))
        args.append(mul2)
    acc_shape = (tm, tn) if nk > 1 else (8, 128)
    (out,), per = _call(
        body, name=name, grid=(m_count, n // tn, nk),
        in_specs=in_specs, out_specs=[pl.BlockSpec((tm, tn), lambda i, j, kk: (i + m_first, j))],
        out_shape=[jax.ShapeDtypeStruct((m, n), out_dtype)], args=args,
        scratch_shapes=[pltpu.VMEM(acc_shape, F32)],
        semantics=("parallel", "parallel", "arbitrary"), exchanges=exchanges,
        into=None if out_into is None else {0: out_into})
    return (out, per) if exchanges else out


def _mm_tn(a, b, *, tm, tn, tt, name, a_square=False, n_split=1, n_blocks=None, exchanges=()):
    t, m = a.shape
    n = b.shape[1]
    assert t % tt == 0 and m % tm == 0 and n % tn == 0
    count, stride, first = n_blocks or (n // tn, 1, 0)
    n = count * tn
    assert (n // n_split) % tn == 0
    per = n // n_split // tn

    def body(a_ref, b_ref, o_ref):
        ti = pl.program_id(2)
        av = a_ref[...]
        if a_square:
            af = av.astype(F32)
            av = (af * af).astype(BF16)
        part = lax.dot_general(av, b_ref[...], TN, preferred_element_type=F32)

        @pl.when(ti == 0)
        def _():
            o_ref[...] = part

        @pl.when(ti > 0)
        def _():
            o_ref[...] += part

    (out,), xres = _call(
        body, name=name, grid=(m // tm, n // tn, t // tt),
        in_specs=[pl.BlockSpec((tt, tm), lambda i, j, ti: (ti, i)),
                  pl.BlockSpec((tt, tn), lambda i, j, ti: (ti, first + stride * j))],
        out_specs=[pl.BlockSpec((None, tm, tn), lambda i, j, ti: (j // per, i, j % per))],
        out_shape=[jax.ShapeDtypeStruct((n_split, m, n // n_split), F32)], args=[a, b],
        semantics=("parallel", "parallel", "arbitrary"), exchanges=exchanges)
    return (out, xres) if exchanges else out


def _rstd(x):
    return lax.rsqrt(jnp.mean(x * x, axis=-1, keepdims=True) + EPS)


def _rms_cast(x, g, *, tm, name):
    t, d = x.shape

    def body(x_ref, g_ref, o_ref):
        xv = x_ref[...]
        o_ref[...] = (xv * _rstd(xv) * g_ref[...]).astype(BF16)

    return pl.pallas_call(
        body, name=name, grid=(t // tm,),
        in_specs=[pl.BlockSpec((tm, d), lambda i: (i, 0)), pl.BlockSpec((1, d), lambda i: (0, 0))],
        out_specs=pl.BlockSpec((tm, d), lambda i: (i, 0)),
        out_shape=jax.ShapeDtypeStruct((t, d), BF16),
        compiler_params=_params(("parallel",)),
    )(x, g)


def _mix_cat(attn, rnn, gain, *, tm, name):
    t = attn.shape[0]

    def body(a_ref, r_ref, g_ref, o_ref):
        av = a_ref[...]
        o_ref[:, :ATTN_W] = (av * _rstd(av) * g_ref[...]).astype(BF16)
        o_ref[:, ATTN_W:] = r_ref[...].astype(BF16)

    return pl.pallas_call(
        body, name=name, grid=(t // tm,),
        in_specs=[pl.BlockSpec((tm, ATTN_W), lambda i: (i, 0)), pl.BlockSpec((tm, RNN_W), lambda i: (i, 0)),
                  pl.BlockSpec((1, ATTN_W), lambda i: (0, 0))],
        out_specs=pl.BlockSpec((tm, D_MODEL), lambda i: (i, 0)),
        out_shape=jax.ShapeDtypeStruct((t, D_MODEL), BF16),
        compiler_params=_params(("parallel",)),
    )(attn, rnn, gain)


def _post_norm_res(mixed, g_post, res, g_next, *, tm, name, exchanges=()):
    t, d = mixed.shape

    def body(m_ref, gp_ref, r_ref, gn_ref, x1_ref, h2_ref):
        mv = m_ref[...]
        x1 = r_ref[...] + mv * _rstd(mv) * gp_ref[...]
        x1_ref[...] = x1
        h2_ref[...] = (x1 * _rstd(x1) * gn_ref[...]).astype(BF16)

    row = pl.BlockSpec((tm, d), lambda i: (i, 0))
    vec = pl.BlockSpec((1, d), lambda i: (0, 0))
    res_, xres = _call(
        body, name=name, grid=(t // tm,),
        in_specs=[row, vec, row, vec], out_specs=[row, row],
        out_shape=[jax.ShapeDtypeStruct((t, d), F32), jax.ShapeDtypeStruct((t, d), BF16)],
        args=[mixed, g_post, res, g_next], semantics=("parallel",), exchanges=exchanges)
    return (*res_, xres) if exchanges else res_


def _rms_bwd(dyn, xin, g, res, *, tm, out_dtype, name, col_block=0, exchanges=()):
    t, d = xin.shape

    def body(*refs):
        if res is not None:
            dy_ref, x_ref, g_ref, r_ref, dx_ref, dg_ref = refs
        else:
            dy_ref, x_ref, g_ref, dx_ref, dg_ref = refs
        i = pl.program_id(0)
        xv = x_ref[...]
        dy = dy_ref[...].astype(F32)
        r = _rstd(xv)
        xh = xv * r
        part = jnp.sum(dy * xh, axis=0, keepdims=True)

        @pl.when(i == 0)
        def _():
            dg_ref[...] = part

        @pl.when(i > 0)
        def _():
            dg_ref[...] += part

        tt = dy * g_ref[...]
        dx = r * (tt - xh * jnp.mean(tt * xh, axis=-1, keepdims=True))
        if res is not None:
            dx = dx + r_ref[...]
        dx_ref[...] = dx.astype(out_dtype)

    row = pl.BlockSpec((tm, d), lambda i: (i, 0))
    vec = pl.BlockSpec((1, d), lambda i: (0, 0))
    in_specs = [pl.BlockSpec((tm, d), lambda i: (i, col_block)), row, vec]
    args = [dyn, xin, g]
    if res is not None:
        in_specs.append(row)
        args.append(res)
    res, xres = _call(
        body, name=name, grid=(t // tm,),
        in_specs=in_specs, out_specs=[row, vec],
        out_shape=[jax.ShapeDtypeStruct((t, d), out_dtype), jax.ShapeDtypeStruct((1, d), F32)], args=args,
        semantics=("arbitrary",), exchanges=exchanges)
    return (*res, xres) if exchanges else res


def _loss_head(y, g_post, x1, target, *, tm, name):
    t, d = y.shape

    def body(y_ref, g_ref, x1_ref, t_ref, dy_ref, dx2_ref, loss_ref, dg_ref):
        i = pl.program_id(0)
        yv = y_ref[...]
        r = _rstd(yv)
        yh = yv * r
        gv = g_ref[...]
        err = x1_ref[...] + yh * gv - t_ref[...]
        lpart = 0.5 * jnp.sum(jnp.mean(err * err, axis=-1, keepdims=True), axis=0, keepdims=True)
        dx2 = err * (1.0 / d)
        dgp = jnp.sum(dx2 * yh, axis=0, keepdims=True)
        lane = lax.broadcasted_iota(jnp.int32, (1, 128), 1)
        lrow = jnp.where(lane == 0, lpart, 0.0)

        @pl.when(i == 0)
        def _():
            dg_ref[...] = dgp
            loss_ref[...] = lrow

        @pl.when(i > 0)
        def _():
            dg_ref[...] += dgp
            loss_ref[...] += lrow

        tt = dx2 * gv
        dy_ref[...] = (r * (tt - yh * jnp.mean(tt * yh, axis=-1, keepdims=True))).astype(BF16)
        dx2_ref[...] = dx2

    row = pl.BlockSpec((tm, d), lambda i: (i, 0))
    vec = pl.BlockSpec((1, d), lambda i: (0, 0))
    return pl.pallas_call(
        body, name=name, grid=(t // tm,),
        in_specs=[row, vec, row, row],
        out_specs=[row, row, pl.BlockSpec((1, 128), lambda i: (0, 0)), vec],
        out_shape=[jax.ShapeDtypeStruct((t, d), BF16), jax.ShapeDtypeStruct((t, d), F32),
                   jax.ShapeDtypeStruct((1, 128), F32), jax.ShapeDtypeStruct((1, d), F32)],
        compiler_params=_params(("arbitrary",)),
    )(y, g_post, x1, target)


def _alibi_slope(h):
    return 2.0 ** (-8.0 * (h + 1) / N_Q)


PAIR = 2 * HEAD_DIM
N_PAIRS = N_Q // 2
PAIRS_PER_KV = GROUP // 2
SMEM = pl.BlockSpec(memory_space=pltpu.SMEM)


def _swa_mask(n):
    key = lax.broadcasted_iota(jnp.int32, (2 * BLK, BLK), 0)
    qry = lax.broadcasted_iota(jnp.int32, (2 * BLK, BLK), 1)
    dist = qry + BLK - key
    valid = (dist >= 0) & (dist < BLK) & ((key >= BLK) | (n > 0))
    return valid, dist.astype(F32)


def _block_diag(kvp_ref, kvc_ref, off):
    a = jnp.concatenate([kvp_ref[:, off:off + HEAD_DIM], kvc_ref[:, off:off + HEAD_DIM]], axis=0).astype(BF16)
    z = jnp.zeros_like(a)
    return jnp.concatenate([jnp.concatenate([a, z], axis=1), jnp.concatenate([z, a], axis=1)], axis=0)


def _swa_scores(s2, e, hh, valid, distf):
    s = s2[2 * BLK * e:2 * BLK * (e + 1)] * (HEAD_DIM ** -0.5) - _alibi_slope(hh) * distf
    return jnp.where(valid, s, -1e30)


def _swa_fwd(proj, sinks, *, name, exchanges=()):
    t = proj.shape[0]
    nb = t // BLK
    kvb = KV_COL // (2 * 128)

    def body(sink_ref, q_ref, kvc_ref, kvp_ref, o_ref, lse_ref):
        n = pl.program_id(0)
        valid, distf = _swa_mask(n)
        for kvh in range(N_KV):
            k2 = _block_diag(kvp_ref, kvc_ref, kvh * HEAD_DIM)
            v2 = _block_diag(kvp_ref, kvc_ref, 128 + kvh * HEAD_DIM)
            for jp in range(PAIRS_PER_KV):
                pair = kvh * PAIRS_PER_KV + jp
                lanes = slice(pair * PAIR, (pair + 1) * PAIR)
                s2 = lax.dot_general(k2, q_ref[:, lanes].astype(BF16), NT, preferred_element_type=F32)
                probs = []
                for e in range(2):
                    hh = 2 * pair + e
                    s = _swa_scores(s2, e, hh, valid, distf)
                    sink = sink_ref[0, hh]
                    mx = jnp.maximum(jnp.max(s, axis=0, keepdims=True), sink)
                    p = jnp.exp(s - mx)
                    l = jnp.sum(p, axis=0, keepdims=True) + jnp.exp(sink - mx)
                    probs.append((p * (1.0 / l)).astype(BF16))
                    lse_ref[hh:hh + 1, :] = mx + jnp.log(l)
                o_ref[:, lanes] = lax.dot_general(jnp.concatenate(probs, axis=0), v2, TN,
                                                  preferred_element_type=F32)

    res, xres = _call(
        body, name=name, grid=(nb,),
        in_specs=[SMEM,
                  pl.BlockSpec((BLK, ATTN_W), lambda n: (n, 0)),
                  pl.BlockSpec((BLK, 256), lambda n: (n, kvb)),
                  pl.BlockSpec((BLK, 256), lambda n: (jnp.maximum(n - 1, 0), kvb))],
        out_specs=[pl.BlockSpec((BLK, ATTN_W), lambda n: (n, 0)),
                   pl.BlockSpec((None, N_Q, BLK), lambda n: (n, 0, 0))],
        out_shape=[jax.ShapeDtypeStruct((t, ATTN_W), F32), jax.ShapeDtypeStruct((nb, N_Q, BLK), F32)],
        args=[sinks, proj, proj, proj], semantics=("parallel",), exchanges=exchanges)
    return (*res, xres) if exchanges else res


def _swa_bwd(proj, sinks, dattn, lse, *, name, exchanges=()):
    t = proj.shape[0]
    nb = t // BLK
    kvb = KV_COL // (2 * 128)

    def body(sink_ref, q_ref, kvc_ref, kvp_ref, do_ref, lse_ref, dq_ref, dkv_ref, dsink_ref, carry_ref):
        n = pl.program_id(0)

        @pl.when(n == 0)
        def _():
            dsink_ref[...] = jnp.zeros_like(dsink_ref)
            carry_ref[...] = jnp.zeros_like(carry_ref)

        @pl.when(n < nb)
        def _():
            valid, distf = _swa_mask(n)
            for kvh in range(N_KV):
                k2 = _block_diag(kvp_ref, kvc_ref, kvh * HEAD_DIM)
                v2 = _block_diag(kvp_ref, kvc_ref, 128 + kvh * HEAD_DIM)
                dk2 = jnp.zeros((4 * BLK, PAIR), F32)
                dv2 = jnp.zeros((4 * BLK, PAIR), F32)
                for jp in range(PAIRS_PER_KV):
                    pair = kvh * PAIRS_PER_KV + jp
                    lanes = slice(pair * PAIR, (pair + 1) * PAIR)
                    q2 = q_ref[:, lanes].astype(BF16)
                    do2 = do_ref[:, lanes].astype(BF16)
                    s2 = lax.dot_general(k2, q2, NT, preferred_element_type=F32)
                    dp2 = lax.dot_general(v2, do2, NT, preferred_element_type=F32)
                    probs, dss = [], []
                    for e in range(2):
                        hh = 2 * pair + e
                        lse_h = lse_ref[hh:hh + 1, :]
                        p = jnp.exp(_swa_scores(s2, e, hh, valid, distf) - lse_h)
                        dp = dp2[2 * BLK * e:2 * BLK * (e + 1)]
                        delta = jnp.sum(p * dp, axis=0, keepdims=True)
                        dsink_ref[hh:hh + 1, :] += -jnp.exp(sink_ref[0, hh] - lse_h) * delta
                        probs.append(p.astype(BF16))
                        dss.append((p * (dp - delta)).astype(BF16))
                    ds2 = jnp.concatenate(dss, axis=0)
                    dq_ref[:, lanes] = (lax.dot_general(ds2, k2, TN, preferred_element_type=F32)
                                        * (HEAD_DIM ** -0.5)).astype(BF16)
                    dk2 = dk2 + jnp.dot(ds2, q2, preferred_element_type=F32)
                    dv2 = dv2 + jnp.dot(jnp.concatenate(probs, axis=0), do2, preferred_element_type=F32)
                dk_cat = (dk2[:2 * BLK, :HEAD_DIM] + dk2[2 * BLK:, HEAD_DIM:]) * (HEAD_DIM ** -0.5)
                dv_cat = dv2[:2 * BLK, :HEAD_DIM] + dv2[2 * BLK:, HEAD_DIM:]
                ko = kvh * HEAD_DIM
                vo = 128 + kvh * HEAD_DIM
                dkv_ref[:, ko:ko + HEAD_DIM] = (carry_ref[:, ko:ko + HEAD_DIM] + dk_cat[:BLK]).astype(BF16)
                dkv_ref[:, vo:vo + HEAD_DIM] = (carry_ref[:, vo:vo + HEAD_DIM] + dv_cat[:BLK]).astype(BF16)
                carry_ref[:, ko:ko + HEAD_DIM] = dk_cat[BLK:]
                carry_ref[:, vo:vo + HEAD_DIM] = dv_cat[BLK:]

        @pl.when(n == nb)
        def _():
            dkv_ref[...] = carry_ref[...].astype(BF16)

    last = nb - 1
    res, xres = _call(
        body, name=name, grid=(nb + 1,),
        in_specs=[SMEM,
                  pl.BlockSpec((BLK, ATTN_W), lambda n: (jnp.minimum(n, last), 0)),
                  pl.BlockSpec((BLK, 256), lambda n: (jnp.minimum(n, last), kvb)),
                  pl.BlockSpec((BLK, 256), lambda n: (jnp.maximum(jnp.minimum(n, last) - 1, 0), kvb)),
                  pl.BlockSpec((BLK, ATTN_W), lambda n: (jnp.minimum(n, last), 0)),
                  pl.BlockSpec((None, N_Q, BLK), lambda n: (jnp.minimum(n, last), 0, 0))],
        out_specs=[pl.BlockSpec((BLK, ATTN_W), lambda n: (jnp.minimum(n, last), 0)),
                   pl.BlockSpec((BLK, 256), lambda n: (jnp.maximum(n - 1, 0), 0)),
                   pl.BlockSpec((N_Q, BLK), lambda n: (0, 0))],
        out_shape=[jax.ShapeDtypeStruct((t, ATTN_W), BF16), jax.ShapeDtypeStruct((t, 256), BF16),
                   jax.ShapeDtypeStruct((N_Q, BLK), F32)],
        scratch_shapes=[pltpu.VMEM((BLK, 256), F32)],
        args=[sinks, proj, proj, proj, dattn, lse], semantics=("arbitrary",), exchanges=exchanges)
    return (*res, xres) if exchanges else res


def _cumsum_rows(x):
    n = x.shape[0]
    row = lax.broadcasted_iota(jnp.int32, x.shape, 0)
    s = 1
    while s < n:
        x = x + jnp.where(row >= s, pltpu.roll(x, s, axis=0), 0.0)
        s *= 2
    return x


def _rev_cumsum_rows(x):
    n = x.shape[0]
    row = lax.broadcasted_iota(jnp.int32, x.shape, 0)
    s = 1
    while s < n:
        x = x + jnp.where(row < n - s, pltpu.roll(x, n - s, axis=0), 0.0)
        s *= 2
    return x


def _lower_bound(lbl_ref):
    l0 = lbl_ref[0:1, :]
    l1 = lbl_ref[1:2, :]
    mx = jnp.maximum(l0, l1)
    e0 = jnp.exp(l0 - mx)
    e1 = jnp.exp(l1 - mx)
    return e0 / (e0 + e1)


def _hgrn_gates(z, lb):
    sg = _sigmoid(z)
    f = lb + (1.0 - lb) * sg
    return sg, f, jnp.log(f), 1.0 - f


def _sub_factors(b, i, sub):
    rows = lax.broadcasted_iota(jnp.int32, (CHUNK, RNN_HD), 0)
    ref = b[sub * i - 1:sub * i, :]
    qfac = jnp.exp(b[sub * i:sub * (i + 1), :] - ref)
    kfac = jnp.where(rows < sub * i, jnp.exp(ref - b), 0.0)
    return qfac, kfac


def _diag_decay(bi, s):
    trow = lax.broadcasted_iota(jnp.int32, bi.shape, 0)
    return jnp.where(trow >= s, jnp.exp(bi - bi[s:s + 1, :]), 0.0)


def _hgrn_fwd(proj, lb_logits, norm_gain, *, tb, name, exchanges=()):
    t = proj.shape[0]
    ntb = t // tb
    nch = tb // CHUNK
    qb, fb, ib, gb = QR_COL // 128, FR_COL // 128, IR_COL // 128, GR_COL // 128

    def body(q_ref, f_ref, i_ref, g_ref, lbl_ref, gain_ref, o_ref, out_ref, s0_ref, st_ref):
        c = pl.program_id(1)

        @pl.when(c == 0)
        def _():
            st_ref[...] = jnp.zeros_like(st_ref)

        lb = _lower_bound(lbl_ref)
        gain = gain_ref[...]

        def chunk(ci, st):
            rows = slice(ci * CHUNK, (ci + 1) * CHUNK)
            _, _, lf, k = _hgrn_gates(f_ref[rows, :], lb)
            qr = q_ref[rows, :]
            q = qr * _sigmoid(qr)
            v = i_ref[rows, :]
            b = _cumsum_rows(lf)
            s0_ref[ci] = st
            o_inter = lax.dot_general((q * jnp.exp(b)).astype(BF16), st.astype(BF16), NT,
                                      preferred_element_type=F32)
            vb = v.astype(BF16)
            blast = b[CHUNK - 1:CHUNK, :]
            khat = (k * jnp.exp(blast - b)).astype(BF16)
            st = st * jnp.exp(blast) + lax.dot_general(vb, khat, TN, preferred_element_type=F32)
            blocks = []
            for i in range(CHUNK // SUB_FWD):
                blk = slice(SUB_FWD * i, SUB_FWD * (i + 1))
                qi, ki, vi, bi = q[blk], k[blk], v[blk], b[blk]
                oi = o_inter[blk]
                if i > 0:
                    qfac, kfac = _sub_factors(b, i, SUB_FWD)
                    att = lax.dot_general((qi * qfac).astype(BF16), (k * kfac).astype(BF16), NT,
                                          preferred_element_type=F32)
                    oi = oi + jnp.dot(att.astype(BF16), vb, preferred_element_type=F32)
                for s in range(SUB_FWD):
                    qe = qi * _diag_decay(bi, s)
                    a = jnp.sum(qe * ki[s:s + 1, :], axis=1, keepdims=True)
                    oi = oi + a * vi[s:s + 1, :]
                blocks.append(oi)
            o = jnp.concatenate(blocks, axis=0)
            o_ref[rows, :] = o
            gr = g_ref[rows, :]
            out_ref[rows, :] = o * _rstd(o) * gain * (gr * _sigmoid(gr))
            return st

        st = st_ref[...]
        for ci in range(nch):
            st = chunk(ci, st)
        st_ref[...] = st

    def col(base):
        return pl.BlockSpec((tb, RNN_HD), lambda h, c: (c, base + h))

    res, xres = _call(
        body, name=name, grid=(N_RNN, ntb),
        in_specs=[col(qb), col(fb), col(ib), col(gb),
                  pl.BlockSpec((2, RNN_HD), lambda h, c: (0, h)), pl.BlockSpec((1, RNN_HD), lambda h, c: (0, 0))],
        out_specs=[pl.BlockSpec((tb, RNN_HD), lambda h, c: (c, h)), pl.BlockSpec((tb, RNN_HD), lambda h, c: (c, h)),
                   pl.BlockSpec((None, nch, RNN_HD, RNN_HD), lambda h, c: (h, c, 0, 0))],
        out_shape=[jax.ShapeDtypeStruct((t, RNN_W), F32), jax.ShapeDtypeStruct((t, RNN_W), F32),
                   jax.ShapeDtypeStruct((N_RNN, t // CHUNK, RNN_HD, RNN_HD), F32)],
        scratch_shapes=[pltpu.VMEM((RNN_HD, RNN_HD), F32)],
        args=[proj, proj, proj, proj, lb_logits, norm_gain],
        semantics=("parallel", "arbitrary"), exchanges=exchanges)
    return (*res, xres) if exchanges else res


def _hgrn_bwd(proj, lb_logits, norm_gain, o_pre, s0, dcat, *, tb, name, exchanges=()):
    t = proj.shape[0]
    ntb = t // tb
    nch = tb // CHUNK
    qb, fb, ib, gb = QR_COL // 128, FR_COL // 128, IR_COL // 128, GR_COL // 128
    sub = SUB_BWD
    nsub = CHUNK // sub

    def body(q_ref, f_ref, i_ref, g_ref, lbl_ref, gain_ref, o_ref, s0_ref, dout_ref,
             dq_ref, df_ref, di_ref, dg_ref, dlb_ref, dgain_ref,
             dst_ref, dqs_ref, dks_ref, dvs_ref):
        c = pl.program_id(1)

        @pl.when(c == 0)
        def _():
            dst_ref[...] = jnp.zeros_like(dst_ref)
            dlb_ref[...] = jnp.zeros_like(dlb_ref)
            dgain_ref[...] = jnp.zeros_like(dgain_ref)

        lb = _lower_bound(lbl_ref)
        gain = gain_ref[...]

        def chunk(ci, dst):
            rows = slice(ci * CHUNK, (ci + 1) * CHUNK)
            dqa_ref, dka_ref, dva_ref = dqs_ref.at[ci], dks_ref.at[ci], dvs_ref.at[ci]
            sg, f, lf, k = _hgrn_gates(f_ref[rows, :], lb)
            qr = q_ref[rows, :]
            sq = _sigmoid(qr)
            q = qr * sq
            v = i_ref[rows, :]
            b = _cumsum_rows(lf)

            dout = dout_ref[rows, :]
            o = o_ref[rows, :]
            gr = g_ref[rows, :]
            sgg = _sigmoid(gr)
            gate = gr * sgg
            rs = _rstd(o)
            nrm = o * rs
            dg_ref[rows, :] = (dout * nrm * gain * (sgg * (1.0 + gr * (1.0 - sgg)))).astype(BF16)
            dn = dout * gate
            dgain_ref[...] += jnp.sum(dn * nrm, axis=0, keepdims=True)
            tt = dn * gain
            do = rs * (tt - nrm * jnp.mean(tt * nrm, axis=-1, keepdims=True))

            dob = do.astype(BF16)
            vb = v.astype(BF16)
            eb = jnp.exp(b)
            blast = b[CHUNK - 1:CHUNK, :]
            ebl = jnp.exp(blast - b)
            dstb = dst.astype(BF16)
            khat = (k * ebl).astype(BF16)
            s0 = s0_ref[ci]
            dqa_ref[...] = eb * jnp.dot(dob, s0.astype(BF16), preferred_element_type=F32)
            dk_state = ebl * jnp.dot(vb, dstb, preferred_element_type=F32)
            dka_ref[...] = dk_state
            d_blast = (jnp.sum(k * dk_state, axis=0, keepdims=True)
                       + jnp.exp(blast) * jnp.sum(dst * s0, axis=0, keepdims=True))
            dva_ref[...] = lax.dot_general(khat, dstb, NT, preferred_element_type=F32)
            dst_next = dst * jnp.exp(blast) + lax.dot_general(dob, (q * eb).astype(BF16), TN,
                                                              preferred_element_type=F32)
            pm = lax.dot_general(dob, vb, NT, preferred_element_type=F32)
            for i in range(nsub):
                blk = slice(sub * i, sub * (i + 1))
                qi, ki, vi, bi, doi = q[blk], k[blk], v[blk], b[blk], do[blk]
                dqi = dqa_ref[blk, :]
                if i > 0:
                    qfac, kfac = _sub_factors(b, i, sub)
                    qt = (qi * qfac).astype(BF16)
                    kt = (k * kfac).astype(BF16)
                    att = lax.dot_general(qt, kt, NT, preferred_element_type=F32).astype(BF16)
                    pmi = pm[blk, :].astype(BF16)
                    dva_ref[...] += lax.dot_general(att, doi.astype(BF16), TN, preferred_element_type=F32)
                    dqi = dqi + qfac * jnp.dot(pmi, kt, preferred_element_type=F32)
                    dka_ref[...] += kfac * lax.dot_general(pmi, qt, TN, preferred_element_type=F32)
                for s in range(sub):
                    e = _diag_decay(bi, s)
                    ks = ki[s:s + 1, :]
                    row = slice(sub * i + s, sub * i + s + 1)
                    a = jnp.sum(qi * e * ks, axis=1, keepdims=True)
                    pe = jnp.sum(doi * vi[s:s + 1, :], axis=1, keepdims=True) * e
                    dqi = dqi + pe * ks
                    dka_ref[row, :] += jnp.sum(pe * qi, axis=0, keepdims=True)
                    dva_ref[row, :] += jnp.sum(a * doi, axis=0, keepdims=True)
                dqa_ref[blk, :] = dqi

            dq = dqa_ref[...]
            dk = dka_ref[...]
            lastrow = lax.broadcasted_iota(jnp.int32, (CHUNK, RNN_HD), 0) == CHUNK - 1
            dlf = _rev_cumsum_rows(q * dq - k * dk + jnp.where(lastrow, d_blast, 0.0))
            dff = dlf / f - dk
            df_ref[rows, :] = (dff * (1.0 - lb) * sg * (1.0 - sg)).astype(BF16)
            dlb_ref[...] += jnp.sum(dff * (1.0 - sg), axis=0, keepdims=True)
            dq_ref[rows, :] = (dq * (sq * (1.0 + qr * (1.0 - sq)))).astype(BF16)
            di_ref[rows, :] = dva_ref[...].astype(BF16)
            return dst_next

        dst = dst_ref[...]
        for ci in reversed(range(nch)):
            dst = chunk(ci, dst)
        dst_ref[...] = dst

    def col(base):
        return pl.BlockSpec((tb, RNN_HD), lambda h, c: (ntb - 1 - c, base + h))

    outc = pl.BlockSpec((tb, RNN_HD), lambda h, c: (ntb - 1 - c, h))
    hb = ATTN_W // RNN_HD
    res, xres = _call(
        body, name=name, grid=(N_RNN, ntb),
        in_specs=[col(qb), col(fb), col(ib), col(gb),
                  pl.BlockSpec((2, RNN_HD), lambda h, c: (0, h)), pl.BlockSpec((1, RNN_HD), lambda h, c: (0, 0)),
                  outc,
                  pl.BlockSpec((None, nch, RNN_HD, RNN_HD), lambda h, c: (h, ntb - 1 - c, 0, 0)),
                  pl.BlockSpec((tb, RNN_HD), lambda h, c: (ntb - 1 - c, hb + h))],
        out_specs=[outc, outc, outc, outc,
                   pl.BlockSpec((1, RNN_HD), lambda h, c: (0, h)),
                   pl.BlockSpec((None, 1, RNN_HD), lambda h, c: (h, 0, 0))],
        out_shape=[jax.ShapeDtypeStruct((t, RNN_W), BF16)] * 4
        + [jax.ShapeDtypeStruct((1, RNN_W), F32), jax.ShapeDtypeStruct((N_RNN, 1, RNN_HD), F32)],
        scratch_shapes=[pltpu.VMEM((RNN_HD, RNN_HD), F32),
                        pltpu.VMEM((nch, CHUNK, RNN_HD), F32), pltpu.VMEM((nch, CHUNK, RNN_HD), F32),
                        pltpu.VMEM((nch, CHUNK, RNN_HD), F32)],
        args=[proj, proj, proj, proj, lb_logits, norm_gain, o_pre, s0, dcat],
        semantics=("parallel", "arbitrary"), exchanges=exchanges)
    return (*res, xres) if exchanges else res


def _cast_slots(w, where, *, name):
    _, rows, cols = w.shape
    rh = rows // 2
    tr = _row_tile(rh, cols)
    nh = rh // tr

    def body(wh_ref, w_ref, o_ref):
        o_ref[...] = w_ref[...].astype(BF16)

    return pl.pallas_call(
        body, name=name,
        grid_spec=pltpu.PrefetchScalarGridSpec(
            num_scalar_prefetch=1, grid=(2, nh),
            in_specs=[pl.BlockSpec((None, tr, cols), lambda h, i, wh: (0, h * nh + i, 0))],
            out_specs=pl.BlockSpec((None, tr, cols), lambda h, i, wh: (2 * wh[0] + h, i, 0))),
        out_shape=jax.ShapeDtypeStruct((8, rh, cols), BF16),
        compiler_params=_params(("parallel", "parallel")),
    )(where, w)


def _all_gather_halves(bufs, *, name):
    n = len(bufs)

    def body(*refs):
        ins, outs = refs[:n], refs[n:2 * n]
        send_sems, recv_sems = refs[2 * n:]
        x, y, c = _place()
        sibling = (x, y, 1 - c)
        chips = [(1 - x, y), (x, 1 - y), (1 - x, 1 - y)]

        def copy(a, k, block, to, src=None):
            slot = outs[a].at[4 * block[0] + 2 * block[1] + block[2]]
            return pltpu.make_async_remote_copy(
                src_ref=slot if src is None else src, dst_ref=slot,
                send_sem=send_sems.at[a, k], recv_sem=recv_sems.at[a, k],
                device_id=to, device_id_type=MESH)

        first, passed = [], []
        for a in range(n):
            for j, chip in enumerate(chips):
                cp = copy(a, j, (x, y, c), (*chip, c), src=ins[a].at[4 * x + 2 * y + c])
                cp.start()
                first.append(cp)
        for a in range(n):
            for j, chip in enumerate(chips):
                copy(a, j, (*chip, c), (x, y, c)).wait_recv()
                cp = copy(a, 3 + j, (*chip, c), sibling)
                cp.start()
                passed.append(cp)
        for a in range(n):
            for j, chip in enumerate(chips):
                copy(a, 3 + j, (*chip, 1 - c), (x, y, c)).wait_recv()
        for cp in first + passed:
            cp.wait_send()

    return pl.pallas_call(
        body, name=name,
        in_specs=[ANY] * n, out_specs=[ANY] * n,
        out_shape=[jax.ShapeDtypeStruct(b.shape, b.dtype) for b in bufs],
        scratch_shapes=[pltpu.SemaphoreType.DMA((n, 6)), pltpu.SemaphoreType.DMA((n, 6))],
        input_output_aliases={a: a for a in range(n)},
    )(*bufs)


def _row_tile(rows, cols, budget=1 << 20):
    tr = rows
    while tr * cols > budget and tr % 16 == 0:
        tr //= 2
    return tr


def _half_spec(g, tr, halves_last, slab):
    if halves_last:
        return pl.BlockSpec((None, tr, g.shape[2] // 2), lambda *a: (slab(*a), a[-2], a[-1][1]))
    return pl.BlockSpec((None, None, tr, g.shape[3]), lambda *a: (slab(*a), a[-1][1], a[-2], 0))


def _pair_sum(g, sib, where, *, name, halves_last=False):
    rh, cols = sib.shape[1:]
    tr = _row_tile(rh, cols)

    def body(w_ref, g_ref, s_ref, o_ref):
        o_ref[...] = (g_ref[...] + s_ref[...]).astype(BF16)

    return pl.pallas_call(
        body, name=name,
        grid_spec=pltpu.PrefetchScalarGridSpec(
            num_scalar_prefetch=1, grid=(4, rh // tr),
            in_specs=[_half_spec(g, tr, halves_last, lambda s, i, w: s),
                      pl.BlockSpec((None, tr, cols), lambda s, i, w: (s, i, 0))],
            out_specs=pl.BlockSpec((None, tr, cols), lambda s, i, w: (s, i, 0))),
        out_shape=jax.ShapeDtypeStruct((4, rh, cols), BF16),
        compiler_params=_params(("parallel", "parallel")),
    )(where, g, sib)


def _final_half(g, sib, recv, where, *, name, halves_last=False):
    rh, cols = sib.shape[1:]
    tr = _row_tile(rh, cols)

    def body(w_ref, g_ref, s_ref, r_ref, o_ref):
        acc = g_ref[...] + s_ref[...]
        for j in range(3):
            acc = acc + r_ref[j].astype(F32)
        o_ref[...] = acc

    return pl.pallas_call(
        body, name=name,
        grid_spec=pltpu.PrefetchScalarGridSpec(
            num_scalar_prefetch=1, grid=(rh // tr,),
            in_specs=[_half_spec(g, tr, halves_last, lambda i, w: w[0]),
                      pl.BlockSpec((None, tr, cols), lambda i, w: (w[0], i, 0)),
                      pl.BlockSpec((3, tr, cols), lambda i, w: (0, i, 0))],
            out_specs=pl.BlockSpec((tr, cols), lambda i, w: (i, 0))),
        out_shape=jax.ShapeDtypeStruct((rh, cols), F32),
        compiler_params=_params(("parallel",)),
    )(where, g, sib, recv)


def _adamw_math(w, g, m, v):
    m = ADAM_B1 * m + (1.0 - ADAM_B1) * g
    v = ADAM_B2 * v + (1.0 - ADAM_B2) * (g * g)
    m_hat = m / (1.0 - ADAM_B1 ** ADAM_STEP)
    v_hat = v / (1.0 - ADAM_B2 ** ADAM_STEP)
    delta = -ADAM_LR * (m_hat / (jnp.sqrt(v_hat) + ADAM_EPS) + ADAM_WD * w)
    return delta, m, v


def _adamw(w, mine, theirs, m, v, where, *, name, halves_last=False):
    _, rows, cols = w.shape
    if halves_last:
        cols //= 2
        tr = _row_tile(rows, cols, budget=1 << 19)
        grid = (rows // tr, 2)
        blk = pl.BlockSpec((None, tr, cols), lambda i, h, wh: (0, i, h))
        half = pl.BlockSpec((tr, cols), lambda i, h, wh: (i, 0))
        which = lambda: pl.program_id(1)
    else:
        tr = _row_tile(rows // 2, cols, budget=1 << 19)
        nh = rows // 2 // tr
        grid = (rows // tr,)
        blk = pl.BlockSpec((None, tr, cols), lambda i, wh: (0, i, 0))
        half = pl.BlockSpec((tr, cols), lambda i, wh: (i % nh, 0))
        which = lambda: pl.program_id(0) // nh

    def body(wh_ref, w_ref, a_ref, b_ref, m_ref, v_ref, g_ref, d_ref, nm_ref, nv_ref):
        g = jnp.where(which() == wh_ref[1], a_ref[...], b_ref[...])
        d, nm, nv = _adamw_math(w_ref[...], g, m_ref[...], v_ref[...])
        g_ref[...] = g
        d_ref[...] = d
        nm_ref[...] = nm
        nv_ref[...] = nv

    rows, cols = w.shape[1:]
    return pl.pallas_call(
        body, name=name,
        grid_spec=pltpu.PrefetchScalarGridSpec(
            num_scalar_prefetch=1, grid=grid,
            in_specs=[blk, half, half, blk, blk], out_specs=[blk] * 4),
        out_shape=[jax.ShapeDtypeStruct((1, rows, cols), F32)] * 4,
        compiler_params=_params(("parallel",) * len(grid)),
    )(where, w, mine, theirs, m, v)


SEG_LOSS = 0
SEG_SINK = 128
SEG_AGAIN = 256
SEG_L0 = SEG_AGAIN + ATTN_W
SEG_L1 = SEG_L0 + RNN_W
SEG_RGAIN = SEG_L1 + RNN_W
SEG_G = SEG_RGAIN + 128
N_PACK = SEG_G + 4 * D_MODEL


def _pack(sinks, again, l0, l1, rgain, gains, loss=None):
    z = lambda k: jnp.zeros((1, k), F32)
    first = z(128) if loss is None else loss
    return jnp.concatenate([first, sinks, z(128 - N_Q), again, l0, l1, rgain] + list(gains), axis=1)


def _small_reduce_adamw(part, w, m, v, *, name):
    def body(p_ref, w_ref, m_ref, v_ref, g_ref, d_ref, nm_ref, nv_ref, buf_ref, send_sems, recv_sems):
        x, y, c = _place()
        me = 4 * x + 2 * y + c
        copies = []
        for k in range(1, 8):
            dx, dy, dc = (k >> 2) & 1, (k >> 1) & 1, k & 1
            to = (x ^ dx, y ^ dy, c ^ dc)
            cp = pltpu.make_async_remote_copy(
                src_ref=p_ref, dst_ref=buf_ref.at[me],
                send_sem=send_sems.at[k - 1], recv_sem=recv_sems.at[k - 1],
                device_id=to, device_id_type=MESH)
            cp.start()
            copies.append(cp)
        buf_ref[me] = p_ref[...]
        for cp in copies:
            cp.wait()
        tot = buf_ref[0]
        for j in range(1, 8):
            tot = tot + buf_ref[j]
        g_ref[...] = tot
        l0 = w_ref[:, SEG_L0:SEG_L0 + RNN_W]
        l1 = w_ref[:, SEG_L1:SEG_L1 + RNN_W]
        mx = jnp.maximum(l0, l1)
        e0 = jnp.exp(l0 - mx)
        e1 = jnp.exp(l1 - mx)
        lb = e0 / (e0 + e1)
        gl0 = tot[:, SEG_L0:SEG_L0 + RNN_W] * lb * (1.0 - lb)
        g_ref[:, SEG_L0:SEG_L0 + RNN_W] = gl0
        g_ref[:, SEG_L1:SEG_L1 + RNN_W] = -gl0
        d, nm, nv = _adamw_math(w_ref[...], g_ref[...], m_ref[...], v_ref[...])
        d_ref[...] = d
        nm_ref[...] = nm
        nv_ref[...] = nv

    vm = pl.BlockSpec(memory_space=pltpu.VMEM)
    return pl.pallas_call(
        body, name=name,
        in_specs=[vm] * 4, out_specs=[vm] * 4,
        out_shape=[jax.ShapeDtypeStruct((1, N_PACK), F32)] * 4,
        scratch_shapes=[pltpu.VMEM((8, 1, N_PACK), F32), pltpu.SemaphoreType.DMA((7,)),
                        pltpu.SemaphoreType.DMA((7,))],
    )(part, w, m, v)


def _layer_grads(xs, tgt, bufs, where, sinks, again, lb_logits, rgain,
                 g_mix_pre, g_mix_post, g_mlp_pre, g_mlp_post):
    tm = 512
    b_in, b_out, b_up, b_dn = bufs

    shard = IN_W // N_CHIPS
    w_in_t = _all_gather_halves([b_in], name="gather_w_in")[0].reshape(IN_W, D_MODEL)
    h1 = _rms_cast(xs, g_mix_pre, tm=tm, name="h1_norm")
    proj, ((b_out, b_up),) = _mm(
        h1, w_in_t, tm=1024, tn=768, tk=D_MODEL, out_dtype=F32, w_layout="nk", name="in_proj",
        exchanges=[_x_gather([b_out, b_up], ici=[(0, 256), (0, 384)])])
    attn, lse, ((b_out, b_up),) = _swa_fwd(
        proj, sinks, name="swa_fwd",
        exchanges=[_x_gather([b_out, b_up], ici=[None, (384, 320)], d2d=[(0, 256), None])])
    w_out = b_out.reshape(D_MODEL, D_MODEL)
    o_pre, rnn, s0, ((b_up, b_dn),) = _hgrn_fwd(
        proj, lb_logits, rgain, tb=512, name="hgrn_fwd",
        exchanges=[_x_gather([b_up, b_dn], ici=[(704, 320), (0, 608)])])
    cat = _mix_cat(attn, rnn, again, tm=tm, name="mix_cat")
    mixed, ((b_up, b_dn),) = _mm(
        cat, w_out, tm=1024, tn=1024, tk=D_MODEL, out_dtype=F32, name="out_proj",
        exchanges=[_x_gather([b_up, b_dn], ici=[None, (608, 256)], d2d=[(0, 1024), (0, 608)])])
    w_up4 = b_up.reshape(N_CHIPS, D_MODEL, D_FF // N_CHIPS)
    x1, h2, ((b_dn,),) = _post_norm_res(
        mixed, g_mix_post, xs, g_mlp_pre, tm=256, name="mix_post",
        exchanges=[_x_gather([b_dn], ici=[(864, 160)], d2d=[(608, 256)])])
    u, ((b_dn,),) = _mm(h2, w_up4, tm=1024, tn=1024, tk=D_MODEL, out_dtype=BF16, relu=True, w_layout="skn",
                        name="mlp_up", exchanges=[_x_gather([b_dn], d2d=[(864, 160)])])
    w_dn = b_dn.reshape(D_FF, D_MODEL)
    yv = _mm(u, w_dn, tm=1024, tn=1024, tk=2048, out_dtype=F32, a_square=True, name="mlp_down")
    dy, dx2, loss_row, dg_mlp_post = _loss_head(yv, g_mlp_post, x1, tgt, tm=256, name="loss_head")

    def halved(g):
        return g.reshape(N_CHIPS, 2, g.shape[1] // 2, g.shape[2])
    du = _mm(dy, w_dn, tm=1024, tn=1024, tk=D_MODEL, out_dtype=BF16, mul2=u, w_layout="nk", name="mlp_down_bwd")
    g_dn = halved(_mm_tn(u, dy, tm=1024, tn=1024, tt=2048, a_square=True, name="w_down_grad")
                  .reshape(N_CHIPS, D_FF // N_CHIPS, D_MODEL))
    d_w_up, ((sib_dn,),) = _mm_tn(h2, du, tm=1024, tn=1024, tt=2048, n_split=N_CHIPS, name="w_up_grad",
                                  exchanges=[_x_pair([g_dn])])
    g_up = halved(d_w_up)
    wire_dn = _pair_sum(g_dn, sib_dn, where, name="pair_sum_w_down")
    dh2, ((recv_dn,), (sib_up,)) = _mm(du, w_up4, tm=1024, tn=1024, tk=2048, out_dtype=F32, w_layout="snk", name="mlp_up_bwd",
                                       exchanges=[_x_chip([wire_dn], rows=[(0, 800)]), _x_pair([g_up])])
    wire_up = _pair_sum(g_up, sib_up, where, name="pair_sum_w_up")
    dx1, dg_mlp_pre, ((recv_dn,),) = _rms_bwd(dh2, x1, g_mlp_pre, dx2, tm=256, out_dtype=F32, name="mlp_pre_bwd",
                                              exchanges=[_x_chip([wire_dn], rows=[(800, 224)], into=[recv_dn])])
    fin_dn = _final_half(g_dn, sib_dn, recv_dn, where, name="final_half_w_down")
    dmixed, dg_mix_post = _rms_bwd(dx1, mixed, g_mix_post, None, tm=256, out_dtype=BF16, name="mix_post_bwd")
    d_w_out, ((oth_dn,),) = _mm_tn(cat, dmixed, tm=1024, tn=1024, tt=2048, name="w_out_grad",
                                   exchanges=[_x_share([fin_dn])])
    g_out = halved(d_w_out.reshape(N_CHIPS, D_MODEL // N_CHIPS, D_MODEL))
    dcat, ((sib_out,),) = _mm(dmixed, w_out, tm=1024, tn=1024, tk=D_MODEL, out_dtype=F32, w_layout="nk", name="out_proj_bwd",
                              exchanges=[_x_pair([g_out])])
    wire_out = _pair_sum(g_out, sib_out, where, name="pair_sum_w_out")
    dattn, dg_again = _rms_bwd(dcat, attn, again, None, tm=tm, out_dtype=F32, name="attn_norm_bwd")
    dq_a, dkv, dsinks, ((recv_out,), (recv_up,)) = _swa_bwd(
        proj, sinks, dattn, lse, name="swa_bwd",
        exchanges=[_x_chip([wire_out]), _x_chip([wire_up], rows=[(0, 320)])])
    dq_r, df_r, di_r, dg_r, dlb, dgain_h, ((recv_up,),) = _hgrn_bwd(
        proj, lb_logits, rgain, o_pre, s0, dcat, tb=512, name="hgrn_bwd",
        exchanges=[_x_chip([wire_up], rows=[(320, 704)], into=[recv_up])])
    fin_up = _final_half(g_up, sib_up, recv_up, where, name="final_half_w_up")
    fin_out = _final_half(g_out, sib_out, recv_out, where, name="final_half_w_out")
    dproj = jnp.concatenate([dq_a, dkv, dq_r, df_r, di_r, dg_r], axis=1)
    piece_cols = D_MODEL // 4

    def w_in_piece(pc, exchanges):
        d, xres = _mm_tn(dproj, h1, tm=896, tn=piece_cols, tt=2048, n_blocks=(2, 2, pc),
                         name="w_in_grad_%d" % pc, exchanges=exchanges)
        return d.reshape(N_CHIPS, shard, 2 * piece_cols), xres

    g_in0, ((oth_up, oth_out),) = w_in_piece(0, [_x_share([fin_up, fin_out])])
    g_in1, ((sib_in0,),) = w_in_piece(1, [_x_pair([g_in0], halves_last=True)])
    wire_in0 = _pair_sum(g_in0, sib_in0, where, name="pair_sum_w_in_0", halves_last=True)
    dh1, ((recv_in0,), (sib_in1,)) = _mm(
        dproj, w_in_t, tm=1024, tn=1024, tk=2688, out_dtype=F32, m_blocks=(0, 2), name="in_proj_bwd_0",
        exchanges=[_x_chip([wire_in0]), _x_pair([g_in1], halves_last=True)])
    wire_in1 = _pair_sum(g_in1, sib_in1, where, name="pair_sum_w_in_1", halves_last=True)
    dh1, ((recv_in1,),) = _mm(
        dproj, w_in_t, tm=1024, tn=1024, tk=2688, out_dtype=F32, m_blocks=(2, 2), out_into=dh1,
        name="in_proj_bwd_1", exchanges=[_x_chip([wire_in1])])
    gx, dg_mix_pre = _rms_bwd(dh1, xs, g_mix_pre, dx1, tm=256, out_dtype=F32, name="mix_pre_bwd")
    fin_in0 = _final_half(g_in0, sib_in0, recv_in0, where, name="final_half_w_in_0", halves_last=True)
    fin_in1 = _final_half(g_in1, sib_in1, recv_in1, where, name="final_half_w_in_1", halves_last=True)
    oth_in0, oth_in1 = _run_exchange(_x_share([fin_in0, fin_in1]), name="share_w_in")
    fin_in = jnp.concatenate([fin_in0, fin_in1], axis=1)
    oth_in = jnp.concatenate([oth_in0, oth_in1], axis=1)

    big = [(fin_in, oth_in), (fin_out, oth_out), (fin_up, oth_up), (fin_dn, oth_dn)]
    drgain = jnp.sum(dgain_h, axis=0)
    small = _pack(jnp.sum(dsinks, axis=1)[None, :], dg_again, dlb, jnp.zeros_like(dlb), drgain,
                  [dg_mix_pre, dg_mix_post, dg_mlp_pre, dg_mlp_post], loss=loss_row)
    return gx, big, small


def kernel(x, w_in, attn_sinks, attn_out_gain, rnn_lb_logits, rnn_norm_gain, w_out, mix_pre_gain, mix_post_gain, mlp_pre_gain, mlp_post_gain, w_up, w_down, loss_target, m_w_in, m_attn_sinks, m_attn_out_gain, m_rnn_lb_logits, m_rnn_norm_gain, m_w_out, m_mix_pre_gain, m_mix_post_gain, m_mlp_pre_gain, m_mlp_post_gain, m_w_up, m_w_down, v_w_in, v_attn_sinks, v_attn_out_gain, v_rnn_lb_logits, v_rnn_norm_gain, v_w_out, v_mix_pre_gain, v_mix_post_gain, v_mlp_pre_gain, v_mlp_post_gain, v_w_up, v_w_down):
    ax, ay, ac = _place()
    where = jnp.stack([2 * ax + ay, ac]).astype(jnp.int32)
    t = lambda a: jnp.swapaxes(a, 1, 2)
    big_w = [t(w_in), w_out, w_up, w_down]
    big_m = [t(m_w_in), m_w_out, m_w_up, m_w_down]
    big_v = [t(v_w_in), v_w_out, v_w_up, v_w_down]

    names = ["w_in", "w_out", "w_up", "w_down"]
    bufs = [_cast_slots(w, where, name="cast_" + nm) for w, nm in zip(big_w, names)]
    gx, big_g, small_part = _layer_grads(
        x[0], loss_target[0], bufs, where, attn_sinks, attn_out_gain, rnn_lb_logits, rnn_norm_gain,
        mix_pre_gain, mix_post_gain, mlp_pre_gain, mlp_post_gain)

    grads, deltas, new_m, new_v = [], [], [], []
    for (f, o), w, m, v, nm in zip(big_g, big_w, big_m, big_v, names):
        res = _adamw(w, f, o, m, v, where, name="adamw_" + nm, halves_last=(nm == "w_in"))
        if nm == "w_in":
            res = [t(r) for r in res]
        g, d, nm_, nv_ = res
        grads.append(g)
        deltas.append(d)
        new_m.append(nm_)
        new_v.append(nv_)

    def pack_params(sinks, again, logits, rgain, gains):
        return _pack(sinks, again, logits[0:1], logits[1:2], rgain, gains)

    pw = pack_params(attn_sinks, attn_out_gain, rnn_lb_logits, rnn_norm_gain,
                     [mix_pre_gain, mix_post_gain, mlp_pre_gain, mlp_post_gain])
    pm = pack_params(m_attn_sinks, m_attn_out_gain, m_rnn_lb_logits, m_rnn_norm_gain,
                     [m_mix_pre_gain, m_mix_post_gain, m_mlp_pre_gain, m_mlp_post_gain])
    pv = pack_params(v_attn_sinks, v_attn_out_gain, v_rnn_lb_logits, v_rnn_norm_gain,
                     [v_mix_pre_gain, v_mix_post_gain, v_mlp_pre_gain, v_mlp_post_gain])
    packs = _small_reduce_adamw(small_part, pw, pm, pv, name="small_reduce_adamw")

    def unpack(p):
        seg = lambda o, k: p[:, o:o + k]
        logits = jnp.concatenate([seg(SEG_L0, RNN_W), seg(SEG_L1, RNN_W)], axis=0)
        gains = [seg(SEG_G + i * D_MODEL, D_MODEL) for i in range(4)]
        return dict(sinks=seg(SEG_SINK, N_Q), again=seg(SEG_AGAIN, ATTN_W), logits=logits,
                    rgain=seg(SEG_RGAIN, RNN_HD), gains=gains)

    def order(small, big):
        return [big[0], small["sinks"], small["again"], small["logits"], small["rgain"], big[1],
                *small["gains"], big[2], big[3]]

    loss = packs[0][0, 0]
    outs = [loss, gx[None]]
    for p, b in zip(packs, [grads, deltas, new_m, new_v]):
        outs += order(unpack(p), b)
    return tuple(outs)
```

```python
import functools

import jax
import jax.numpy as jnp
from jax import lax
from jax.experimental import pallas as pl
from jax.experimental.pallas import tpu as pltpu

F32 = jnp.float32
BF16 = jnp.bfloat16
MESH = pl.DeviceIdType.MESH

EPS = 1e-6
D_MODEL = 2048
ATTN_W = 1024
HEAD_DIM = 64
N_Q = 16
N_KV = 2
GROUP = 8
BLK = 128
RNN_W = 1024
RNN_HD = 128
N_RNN = 8
CHUNK = 64
SUB_FWD = 16
SUB_BWD = 8
D_FF = 8192
IN_W = 5376
N_CHIPS = 4
KV_COL = ATTN_W
QR_COL = ATTN_W + 2 * 128
FR_COL = QR_COL + RNN_W
IR_COL = FR_COL + RNN_W
GR_COL = IR_COL + RNN_W

ADAM_LR = 0.001
ADAM_B1 = 0.9
ADAM_B2 = 0.999
ADAM_EPS = 1e-08
ADAM_WD = 0.01
ADAM_STEP = 10

VMEM_LIMIT = 48 * 1024 * 1024

NT = (((1,), (1,)), ((), ()))
TN = (((0,), (0,)), ((), ()))


def _params(sem=None):
    return pltpu.CompilerParams(dimension_semantics=sem, vmem_limit_bytes=VMEM_LIMIT)


def _sigmoid(x):
    return 1.0 / (1.0 + jnp.exp(-x))


ANY = pl.BlockSpec(memory_space=pl.ANY)


def _place():
    return lax.axis_index("x"), lax.axis_index("y"), lax.axis_index("c")


def _other_chips(x, y):
    return [(1 - x, y), (x, 1 - y), (1 - x, 1 - y)]


class _Exchange:
    def __init__(self, srcs, outs, ncopy, build, aliases=None):
        self.srcs, self.outs, self.ncopy, self.build = list(srcs), list(outs), ncopy, build
        self.aliases = aliases or {}


def _remote(src, dst, send_sems, recv_sems, k, to):
    return pltpu.make_async_remote_copy(src_ref=src, dst_ref=dst, send_sem=send_sems.at[k],
                                        recv_sem=recv_sems.at[k], device_id=to, device_id_type=MESH)


def _call(body, *, name, grid, in_specs, out_specs, out_shape, args, scratch_shapes=(), semantics=None,
          exchanges=(), into=None):
    in_specs, out_specs, out_shape = list(in_specs), list(out_specs), list(out_shape)
    scratch_shapes = list(scratch_shapes)
    ni, no, ns = len(in_specs), len(out_specs), len(scratch_shapes)
    xsrc = [s for x in exchanges for s in x.srcs]
    xout = [o for x in exchanges for o in x.outs]
    into = into or {}
    xsrc += [into[k] for k in sorted(into)]
    nxi, nxo = len(xsrc), len(xout)
    aliases = {nxi - len(into) + ni + q: k for q, k in enumerate(sorted(into))}
    a0 = b0 = 0
    for x in exchanges:
        for si, oi in x.aliases.items():
            aliases[ni + a0 + si] = no + b0 + oi
        a0 += len(x.srcs)
        b0 += len(x.outs)
    sems = []
    for x in exchanges:
        sems += [pltpu.SemaphoreType.DMA((x.ncopy,)), pltpu.SemaphoreType.DMA((x.ncopy,))]

    def wrapped(*refs):
        ins, xi = refs[:ni], refs[ni:ni + nxi]
        outs, xo = refs[ni + nxi:ni + nxi + no], refs[ni + nxi + no:ni + nxi + no + nxo]
        rest = refs[ni + nxi + no + nxo:]
        scr, sm = rest[:ns], rest[ns:]

        def copies():
            cps = []
            a = b = 0
            for k, x in enumerate(exchanges):
                cps += x.build(xi[a:a + len(x.srcs)], xo[b:b + len(x.outs)], sm[2 * k], sm[2 * k + 1])
                a += len(x.srcs)
                b += len(x.outs)
            return cps

        def start():
            for cp in copies():
                cp.start()

        def wait():
            for cp in copies():
                cp.wait()

        if not exchanges:
            body(*ins, *outs, *scr)
        elif not grid:
            start()
            body(*ins, *outs, *scr)
            wait()
        else:
            first = last = None
            for ax, g in enumerate(grid):
                f = pl.program_id(ax) == 0
                l = pl.program_id(ax) == g - 1
                first = f if first is None else first & f
                last = l if last is None else last & l
            pl.when(first)(start)
            body(*ins, *outs, *scr)
            pl.when(last)(wait)

    if exchanges and semantics is not None:
        semantics = ("arbitrary",) * len(grid)
    kwargs = dict(grid=grid) if grid else {}
    res = pl.pallas_call(
        wrapped, name=name,
        in_specs=in_specs + [ANY] * nxi, out_specs=out_specs + [ANY] * nxo,
        out_shape=out_shape + xout, scratch_shapes=scratch_shapes + sems,
        input_output_aliases=aliases,
        compiler_params=_params(semantics), **kwargs,
    )(*args, *xsrc)
    res = list(res)
    mine, theirs = res[:no], res[no:]
    per = []
    b = 0
    for x in exchanges:
        per.append(theirs[b:b + len(x.outs)])
        b += len(x.outs)
    return mine, per


def _run_exchange(x, *, name):
    return _call(lambda: None, name=name, grid=(), in_specs=[], out_specs=[], out_shape=[], args=[],
                 exchanges=[x])[1][0]


def _x_gather(bufs, ici=None, d2d=None):
    n = len(bufs)
    plan = [(a, kind, rows[a]) for a in range(n) for kind, rows in (("ici", ici), ("d2d", d2d))
            if rows is not None and rows[a] is not None]

    def build(srcs, outs, ss, rs):
        x, y, c = _place()
        cps = []
        for q, (a, kind, rows) in enumerate(plan):
            piece = pl.ds(*rows)
            for j, (px, py) in enumerate(_other_chips(x, y)):
                slot, to = (4 * x + 2 * y + c, (px, py, c)) if kind == "ici" else (4 * px + 2 * py + c, (x, y, 1 - c))
                cps.append(_remote(srcs[a].at[slot, piece], outs[a].at[slot, piece], ss, rs, 3 * q + j, to))
        return cps

    outs = [jax.ShapeDtypeStruct(b.shape, b.dtype) for b in bufs]
    return _Exchange(bufs, outs, 3 * len(plan), build, aliases={a: a for a in range(n)})


def _x_pair(grads, halves_last=False):
    n = len(grads)

    def build(srcs, outs, ss, rs):
        x, y, c = _place()

        def half(r):
            if not halves_last:
                return r.at[:, 1 - c]
            ch = r.shape[2] // 2
            return r.at[:, :, pl.ds(pl.multiple_of((1 - c) * ch, 128), ch)]

        return [_remote(half(srcs[a]), outs[a], ss, rs, a, (x, y, 1 - c)) for a in range(n)]

    if halves_last:
        outs = [jax.ShapeDtypeStruct(g.shape[:2] + (g.shape[2] // 2,), g.dtype) for g in grads]
    else:
        outs = [jax.ShapeDtypeStruct((4,) + g.shape[2:], g.dtype) for g in grads]
    return _Exchange(grads, outs, n, build)


def _x_chip(wires, rows=None, into=None):
    n = len(wires)
    rows = rows or [(0, w.shape[1]) for w in wires]

    def build(srcs, outs, ss, rs):
        x, y, c = _place()
        cps = []
        for a in range(n):
            piece = pl.ds(*rows[a])
            for j, (px, py) in enumerate(_other_chips(x, y)):
                cps.append(_remote(srcs[a].at[2 * px + py, piece], outs[a].at[j, piece], ss, rs,
                                   3 * a + j, (px, py, c)))
        return cps

    outs = [jax.ShapeDtypeStruct((3,) + w.shape[1:], w.dtype) for w in wires]
    if into is None:
        return _Exchange(wires, outs, 3 * n, build)
    return _Exchange(list(wires) + list(into), outs, 3 * n, build, aliases={n + a: a for a in range(n)})


def _x_share(halves):
    n = len(halves)

    def build(srcs, outs, ss, rs):
        x, y, c = _place()
        return [_remote(srcs[a], outs[a], ss, rs, a, (x, y, 1 - c)) for a in range(n)]

    outs = [jax.ShapeDtypeStruct(h.shape, h.dtype) for h in halves]
    return _Exchange(halves, outs, n, build)


def _mm(a, w, *, tm, tn, tk, out_dtype, name, a_square=False, relu=False, mul2=None, w_layout="kn",
        m_blocks=None, out_into=None, exchanges=()):
    m, k = a.shape
    m_first, m_count = m_blocks or (0, m // tm)
    a_spec = pl.BlockSpec((tm, tk), lambda i, j, kk: (i + m_first, kk))
    if w_layout == "kn":
        n = w.shape[1]
        w_spec = pl.BlockSpec((tk, tn), lambda i, j, kk: (kk, j))
    elif w_layout == "nk":
        n = w.shape[0]
        w_spec = pl.BlockSpec((tn, tk), lambda i, j, kk: (j, kk))
    elif w_layout == "skn":
        n = w.shape[0] * w.shape[2]
        per_n = w.shape[2] // tn
        w_spec = pl.BlockSpec((None, tk, tn), lambda i, j, kk: (j // per_n, kk, j % per_n))
    else:
        assert w_layout == "snk"
        n = w.shape[1]
        per_k = w.shape[2] // tk
        w_spec = pl.BlockSpec((None, tn, tk), lambda i, j, kk: (kk // per_k, j, kk % per_k))
    w_dims = NT if w_layout in ("nk", "snk") else (((1,), (0,)), ((), ()))
    nk = k // tk
    assert m % tm == 0 and n % tn == 0 and k % tk == 0

    def body(*refs):
        if mul2 is not None:
            a_ref, w_ref, e_ref, o_ref, acc_ref = refs
        else:
            a_ref, w_ref, o_ref, acc_ref = refs
            e_ref = None
        kk = pl.program_id(2)
        av = a_ref[...]
        if a_square:
            af = av.astype(F32)
            av = (af * af).astype(BF16)
        part = lax.dot_general(av, w_ref[...], w_dims, preferred_element_type=F32)

        def finish(r):
            if relu:
                r = jnp.maximum(r, 0.0)
            if e_ref is not None:
                r = 2.0 * e_ref[...].astype(F32) * r
            o_ref[...] = r.astype(out_dtype)

        if nk == 1:
            finish(part)
        else:
            @pl.when(kk == 0)
            def _():
                acc_ref[...] = part

            @pl.when(kk > 0)
            def _():
                acc_ref[...] += part

            @pl.when(kk == nk - 1)
            def _():
                finish(acc_ref[...])

    in_specs = [a_spec, w_spec]
    args = [a, w]
    if mul2 is not None:
        in_specs.append(pl.BlockSpec((tm, tn), lambda i, j, kk: (i + m_first, j)))
        args.append(mul2)
    acc_shape = (tm, tn) if nk > 1 else (8, 128)
    (out,), per = _call(
        body, name=name, grid=(m_count, n // tn, nk),
        in_specs=in_specs, out_specs=[pl.BlockSpec((tm, tn), lambda i, j, kk: (i + m_first, j))],
        out_shape=[jax.ShapeDtypeStruct((m, n), out_dtype)], args=args,
        scratch_shapes=[pltpu.VMEM(acc_shape, F32)],
        semantics=("parallel", "parallel", "arbitrary"), exchanges=exchanges,
        into=None if out_into is None else {0: out_into})
    return (out, per) if exchanges else out


def _mm_tn(a, b, *, tm, tn, tt, name, a_square=False, n_split=1, n_blocks=None, exchanges=()):
    t, m = a.shape
    n = b.shape[1]
    assert t % tt == 0 and m % tm == 0 and n % tn == 0
    count, stride, first = n_blocks or (n // tn, 1, 0)
    n = count * tn
    assert (n // n_split) % tn == 0
    per = n // n_split // tn

    def body(a_ref, b_ref, o_ref):
        ti = pl.program_id(2)
        av = a_ref[...]
        if a_square:
            af = av.astype(F32)
            av = (af * af).astype(BF16)
        part = lax.dot_general(av, b_ref[...], TN, preferred_element_type=F32)

        @pl.when(ti == 0)
        def _():
            o_ref[...] = part

        @pl.when(ti > 0)
        def _():
            o_ref[...] += part

    (out,), xres = _call(
        body, name=name, grid=(m // tm, n // tn, t // tt),
        in_specs=[pl.BlockSpec((tt, tm), lambda i, j, ti: (ti, i)),
                  pl.BlockSpec((tt, tn), lambda i, j, ti: (ti, first + stride * j))],
        out_specs=[pl.BlockSpec((None, tm, tn), lambda i, j, ti: (j // per, i, j % per))],
        out_shape=[jax.ShapeDtypeStruct((n_split, m, n // n_split), F32)], args=[a, b],
        semantics=("parallel", "parallel", "arbitrary"), exchanges=exchanges)
    return (out, xres) if exchanges else out


def _rstd(x):
    return lax.rsqrt(jnp.mean(x * x, axis=-1, keepdims=True) + EPS)


def _rms_cast(x, g, *, tm, name):
    t, d = x.shape

    def body(x_ref, g_ref, o_ref):
        xv = x_ref[...]
        o_ref[...] = (xv * _rstd(xv) * g_ref[...]).astype(BF16)

    return pl.pallas_call(
        body, name=name, grid=(t // tm,),
        in_specs=[pl.BlockSpec((tm, d), lambda i: (i, 0)), pl.BlockSpec((1, d), lambda i: (0, 0))],
        out_specs=pl.BlockSpec((tm, d), lambda i: (i, 0)),
        out_shape=jax.ShapeDtypeStruct((t, d), BF16),
        compiler_params=_params(("parallel",)),
    )(x, g)


def _mix_cat(attn, rnn, gain, *, tm, name):
    t = attn.shape[0]

    def body(a_ref, r_ref, g_ref, o_ref):
        av = a_ref[...]
        o_ref[:, :ATTN_W] = (av * _rstd(av) * g_ref[...]).astype(BF16)
        o_ref[:, ATTN_W:] = r_ref[...].astype(BF16)

    return pl.pallas_call(
        body, name=name, grid=(t // tm,),
        in_specs=[pl.BlockSpec((tm, ATTN_W), lambda i: (i, 0)), pl.BlockSpec((tm, RNN_W), lambda i: (i, 0)),
                  pl.BlockSpec((1, ATTN_W), lambda i: (0, 0))],
        out_specs=pl.BlockSpec((tm, D_MODEL), lambda i: (i, 0)),
        out_shape=jax.ShapeDtypeStruct((t, D_MODEL), BF16),
        compiler_params=_params(("parallel",)),
    )(attn, rnn, gain)


def _post_norm_res(mixed, g_post, res, g_next, *, tm, name, exchanges=()):
    t, d = mixed.shape

    def body(m_ref, gp_ref, r_ref, gn_ref, x1_ref, h2_ref):
        mv = m_ref[...]
        x1 = r_ref[...] + mv * _rstd(mv) * gp_ref[...]
        x1_ref[...] = x1
        h2_ref[...] = (x1 * _rstd(x1) * gn_ref[...]).astype(BF16)

    row = pl.BlockSpec((tm, d), lambda i: (i, 0))
    vec = pl.BlockSpec((1, d), lambda i: (0, 0))
    res_, xres = _call(
        body, name=name, grid=(t // tm,),
        in_specs=[row, vec, row, vec], out_specs=[row, row],
        out_shape=[jax.ShapeDtypeStruct((t, d), F32), jax.ShapeDtypeStruct((t, d), BF16)],
        args=[mixed, g_post, res, g_next], semantics=("parallel",), exchanges=exchanges)
    return (*res_, xres) if exchanges else res_


def _rms_bwd(dyn, xin, g, res, *, tm, out_dtype, name, col_block=0, exchanges=()):
    t, d = xin.shape

    def body(*refs):
        if res is not None:
            dy_ref, x_ref, g_ref, r_ref, dx_ref, dg_ref = refs
        else:
            dy_ref, x_ref, g_ref, dx_ref, dg_ref = refs
        i = pl.program_id(0)
        xv = x_ref[...]
        dy = dy_ref[...].astype(F32)
        r = _rstd(xv)
        xh = xv * r
        part = jnp.sum(dy * xh, axis=0, keepdims=True)

        @pl.when(i == 0)
        def _():
            dg_ref[...] = part

        @pl.when(i > 0)
        def _():
            dg_ref[...] += part

        tt = dy * g_ref[...]
        dx = r * (tt - xh * jnp.mean(tt * xh, axis=-1, keepdims=True))
        if res is not None:
            dx = dx + r_ref[...]
        dx_ref[...] = dx.astype(out_dtype)

    row = pl.BlockSpec((tm, d), lambda i: (i, 0))
    vec = pl.BlockSpec((1, d), lambda i: (0, 0))
    in_specs = [pl.BlockSpec((tm, d), lambda i: (i, col_block)), row, vec]
    args = [dyn, xin, g]
    if res is not None:
        in_specs.append(row)
        args.append(res)
    res, xres = _call(
        body, name=name, grid=(t // tm,),
        in_specs=in_specs, out_specs=[row, vec],
        out_shape=[jax.ShapeDtypeStruct((t, d), out_dtype), jax.ShapeDtypeStruct((1, d), F32)], args=args,
        semantics=("arbitrary",), exchanges=exchanges)
    return (*res, xres) if exchanges else res


def _loss_head(y, g_post, x1, target, *, tm, name):
    t, d = y.shape

    def body(y_ref, g_ref, x1_ref, t_ref, dy_ref, dx2_ref, loss_ref, dg_ref):
        i = pl.program_id(0)
        yv = y_ref[...]
        r = _rstd(yv)
        yh = yv * r
        gv = g_ref[...]
        err = x1_ref[...] + yh * gv - t_ref[...]
        lpart = 0.5 * jnp.sum(jnp.mean(err * err, axis=-1, keepdims=True), axis=0, keepdims=True)
        dx2 = err * (1.0 / d)
        dgp = jnp.sum(dx2 * yh, axis=0, keepdims=True)
        lane = lax.broadcasted_iota(jnp.int32, (1, 128), 1)
        lrow = jnp.where(lane == 0, lpart, 0.0)

        @pl.when(i == 0)
        def _():
            dg_ref[...] = dgp
            loss_ref[...] = lrow

        @pl.when(i > 0)
        def _():
            dg_ref[...] += dgp
            loss_ref[...] += lrow

        tt = dx2 * gv
        dy_ref[...] = (r * (tt - yh * jnp.mean(tt * yh, axis=-1, keepdims=True))).astype(BF16)
        dx2_ref[...] = dx2

    row = pl.BlockSpec((tm, d), lambda i: (i, 0))
    vec = pl.BlockSpec((1, d), lambda i: (0, 0))
    return pl.pallas_call(
        body, name=name, grid=(t // tm,),
        in_specs=[row, vec, row, row],
        out_specs=[row, row, pl.BlockSpec((1, 128), lambda i: (0, 0)), vec],
        out_shape=[jax.ShapeDtypeStruct((t, d), BF16), jax.ShapeDtypeStruct((t, d), F32),
                   jax.ShapeDtypeStruct((1, 128), F32), jax.ShapeDtypeStruct((1, d), F32)],
        compiler_params=_params(("arbitrary",)),
    )(y, g_post, x1, target)


def _alibi_slope(h):
    return 2.0 ** (-8.0 * (h + 1) / N_Q)


PAIR = 2 * HEAD_DIM
N_PAIRS = N_Q // 2
PAIRS_PER_KV = GROUP // 2
SMEM = pl.BlockSpec(memory_space=pltpu.SMEM)


def _swa_mask(n):
    key = lax.broadcasted_iota(jnp.int32, (2 * BLK, BLK), 0)
    qry = lax.broadcasted_iota(jnp.int32, (2 * BLK, BLK), 1)
    dist = qry + BLK - key
    valid = (dist >= 0) & (dist < BLK) & ((key >= BLK) | (n > 0))
    return valid, dist.astype(F32)


def _block_diag(kvp_ref, kvc_ref, off):
    a = jnp.concatenate([kvp_ref[:, off:off + HEAD_DIM], kvc_ref[:, off:off + HEAD_DIM]], axis=0).astype(BF16)
    z = jnp.zeros_like(a)
    return jnp.concatenate([jnp.concatenate([a, z], axis=1), jnp.concatenate([z, a], axis=1)], axis=0)


def _swa_scores(s2, e, hh, valid, distf):
    s = s2[2 * BLK * e:2 * BLK * (e + 1)] * (HEAD_DIM ** -0.5) - _alibi_slope(hh) * distf
    return jnp.where(valid, s, -1e30)


def _swa_fwd(proj, sinks, *, name, exchanges=()):
    t = proj.shape[0]
    nb = t // BLK
    kvb = KV_COL // (2 * 128)

    def body(sink_ref, q_ref, kvc_ref, kvp_ref, o_ref, lse_ref):
        n = pl.program_id(0)
        valid, distf = _swa_mask(n)
        for kvh in range(N_KV):
            k2 = _block_diag(kvp_ref, kvc_ref, kvh * HEAD_DIM)
            v2 = _block_diag(kvp_ref, kvc_ref, 128 + kvh * HEAD_DIM)
            for jp in range(PAIRS_PER_KV):
                pair = kvh * PAIRS_PER_KV + jp
                lanes = slice(pair * PAIR, (pair + 1) * PAIR)
                s2 = lax.dot_general(k2, q_ref[:, lanes].astype(BF16), NT, preferred_element_type=F32)
                probs = []
                for e in range(2):
                    hh = 2 * pair + e
                    s = _swa_scores(s2, e, hh, valid, distf)
                    sink = sink_ref[0, hh]
                    mx = jnp.maximum(jnp.max(s, axis=0, keepdims=True), sink)
                    p = jnp.exp(s - mx)
                    l = jnp.sum(p, axis=0, keepdims=True) + jnp.exp(sink - mx)
                    probs.append((p * (1.0 / l)).astype(BF16))
                    lse_ref[hh:hh + 1, :] = mx + jnp.log(l)
                o_ref[:, lanes] = lax.dot_general(jnp.concatenate(probs, axis=0), v2, TN,
                                                  preferred_element_type=F32)

    res, xres = _call(
        body, name=name, grid=(nb,),
        in_specs=[SMEM,
                  pl.BlockSpec((BLK, ATTN_W), lambda n: (n, 0)),
                  pl.BlockSpec((BLK, 256), lambda n: (n, kvb)),
                  pl.BlockSpec((BLK, 256), lambda n: (jnp.maximum(n - 1, 0), kvb))],
        out_specs=[pl.BlockSpec((BLK, ATTN_W), lambda n: (n, 0)),
                   pl.BlockSpec((None, N_Q, BLK), lambda n: (n, 0, 0))],
        out_shape=[jax.ShapeDtypeStruct((t, ATTN_W), F32), jax.ShapeDtypeStruct((nb, N_Q, BLK), F32)],
        args=[sinks, proj, proj, proj], semantics=("parallel",), exchanges=exchanges)
    return (*res, xres) if exchanges else res


def _swa_bwd(proj, sinks, dattn, lse, *, name, exchanges=()):
    t = proj.shape[0]
    nb = t // BLK
    kvb = KV_COL // (2 * 128)

    def body(sink_ref, q_ref, kvc_ref, kvp_ref, do_ref, lse_ref, dq_ref, dkv_ref, dsink_ref, carry_ref):
        n = pl.program_id(0)

        @pl.when(n == 0)
        def _():
            dsink_ref[...] = jnp.zeros_like(dsink_ref)
            carry_ref[...] = jnp.zeros_like(carry_ref)

        @pl.when(n < nb)
        def _():
            valid, distf = _swa_mask(n)
            for kvh in range(N_KV):
                k2 = _block_diag(kvp_ref, kvc_ref, kvh * HEAD_DIM)
                v2 = _block_diag(kvp_ref, kvc_ref, 128 + kvh * HEAD_DIM)
                dk2 = jnp.zeros((4 * BLK, PAIR), F32)
                dv2 = jnp.zeros((4 * BLK, PAIR), F32)
                for jp in range(PAIRS_PER_KV):
                    pair = kvh * PAIRS_PER_KV + jp
                    lanes = slice(pair * PAIR, (pair + 1) * PAIR)
                    q2 = q_ref[:, lanes].astype(BF16)
                    do2 = do_ref[:, lanes].astype(BF16)
                    s2 = lax.dot_general(k2, q2, NT, preferred_element_type=F32)
                    dp2 = lax.dot_general(v2, do2, NT, preferred_element_type=F32)
                    probs, dss = [], []
                    for e in range(2):
                        hh = 2 * pair + e
                        lse_h = lse_ref[hh:hh + 1, :]
                        p = jnp.exp(_swa_scores(s2, e, hh, valid, distf) - lse_h)
                        dp = dp2[2 * BLK * e:2 * BLK * (e + 1)]
                        delta = jnp.sum(p * dp, axis=0, keepdims=True)
                        dsink_ref[hh:hh + 1, :] += -jnp.exp(sink_ref[0, hh] - lse_h) * delta
                        probs.append(p.astype(BF16))
                        dss.append((p * (dp - delta)).astype(BF16))
                    ds2 = jnp.concatenate(dss, axis=0)
                    dq_ref[:, lanes] = (lax.dot_general(ds2, k2, TN, preferred_element_type=F32)
                                        * (HEAD_DIM ** -0.5)).astype(BF16)
                    dk2 = dk2 + jnp.dot(ds2, q2, preferred_element_type=F32)
                    dv2 = dv2 + jnp.dot(jnp.concatenate(probs, axis=0), do2, preferred_element_type=F32)
                dk_cat = (dk2[:2 * BLK, :HEAD_DIM] + dk2[2 * BLK:, HEAD_DIM:]) * (HEAD_DIM ** -0.5)
                dv_cat = dv2[:2 * BLK, :HEAD_DIM] + dv2[2 * BLK:, HEAD_DIM:]
                ko = kvh * HEAD_DIM
                vo = 128 + kvh * HEAD_DIM
                dkv_ref[:, ko:ko + HEAD_DIM] = (carry_ref[:, ko:ko + HEAD_DIM] + dk_cat[:BLK]).astype(BF16)
                dkv_ref[:, vo:vo + HEAD_DIM] = (carry_ref[:, vo:vo + HEAD_DIM] + dv_cat[:BLK]).astype(BF16)
                carry_ref[:, ko:ko + HEAD_DIM] = dk_cat[BLK:]
                carry_ref[:, vo:vo + HEAD_DIM] = dv_cat[BLK:]

        @pl.when(n == nb)
        def _():
            dkv_ref[...] = carry_ref[...].astype(BF16)

    last = nb - 1
    res, xres = _call(
        body, name=name, grid=(nb + 1,),
        in_specs=[SMEM,
                  pl.BlockSpec((BLK, ATTN_W), lambda n: (jnp.minimum(n, last), 0)),
                  pl.BlockSpec((BLK, 256), lambda n: (jnp.minimum(n, last), kvb)),
                  pl.BlockSpec((BLK, 256), lambda n: (jnp.maximum(jnp.minimum(n, last) - 1, 0), kvb)),
                  pl.BlockSpec((BLK, ATTN_W), lambda n: (jnp.minimum(n, last), 0)),
                  pl.BlockSpec((None, N_Q, BLK), lambda n: (jnp.minimum(n, last), 0, 0))],
        out_specs=[pl.BlockSpec((BLK, ATTN_W), lambda n: (jnp.minimum(n, last), 0)),
                   pl.BlockSpec((BLK, 256), lambda n: (jnp.maximum(n - 1, 0), 0)),
                   pl.BlockSpec((N_Q, BLK), lambda n: (0, 0))],
        out_shape=[jax.ShapeDtypeStruct((t, ATTN_W), BF16), jax.ShapeDtypeStruct((t, 256), BF16),
                   jax.ShapeDtypeStruct((N_Q, BLK), F32)],
        scratch_shapes=[pltpu.VMEM((BLK, 256), F32)],
        args=[sinks, proj, proj, proj, dattn, lse], semantics=("arbitrary",), exchanges=exchanges)
    return (*res, xres) if exchanges else res


def _cumsum_rows(x):
    n = x.shape[0]
    row = lax.broadcasted_iota(jnp.int32, x.shape, 0)
    s = 1
    while s < n:
        x = x + jnp.where(row >= s, pltpu.roll(x, s, axis=0), 0.0)
        s *= 2
    return x


def _rev_cumsum_rows(x):
    n = x.shape[0]
    row = lax.broadcasted_iota(jnp.int32, x.shape, 0)
    s = 1
    while s < n:
        x = x + jnp.where(row < n - s, pltpu.roll(x, n - s, axis=0), 0.0)
        s *= 2
    return x


def _lower_bound(lbl_ref):
    l0 = lbl_ref[0:1, :]
    l1 = lbl_ref[1:2, :]
    mx = jnp.maximum(l0, l1)
    e0 = jnp.exp(l0 - mx)
    e1 = jnp.exp(l1 - mx)
    return e0 / (e0 + e1)


def _hgrn_gates(z, lb):
    sg = _sigmoid(z)
    f = lb + (1.0 - lb) * sg
    return sg, f, jnp.log(f), 1.0 - f


def _sub_factors(b, i, sub):
    rows = lax.broadcasted_iota(jnp.int32, (CHUNK, RNN_HD), 0)
    ref = b[sub * i - 1:sub * i, :]
    qfac = jnp.exp(b[sub * i:sub * (i + 1), :] - ref)
    kfac = jnp.where(rows < sub * i, jnp.exp(ref - b), 0.0)
    return qfac, kfac


def _diag_decay(bi, s):
    trow = lax.broadcasted_iota(jnp.int32, bi.shape, 0)
    return jnp.where(trow >= s, jnp.exp(bi - bi[s:s + 1, :]), 0.0)


def _hgrn_fwd(proj, lb_logits, norm_gain, *, tb, name, exchanges=()):
    t = proj.shape[0]
    ntb = t // tb
    nch = tb // CHUNK
    qb, fb, ib, gb = QR_COL // 128, FR_COL // 128, IR_COL // 128, GR_COL // 128

    def body(q_ref, f_ref, i_ref, g_ref, lbl_ref, gain_ref, o_ref, out_ref, s0_ref, st_ref):
        c = pl.program_id(1)

        @pl.when(c == 0)
        def _():
            st_ref[...] = jnp.zeros_like(st_ref)

        lb = _lower_bound(lbl_ref)
        gain = gain_ref[...]

        def chunk(ci, st):
            rows = slice(ci * CHUNK, (ci + 1) * CHUNK)
            _, _, lf, k = _hgrn_gates(f_ref[rows, :], lb)
            qr = q_ref[rows, :]
            q = qr * _sigmoid(qr)
            v = i_ref[rows, :]
            b = _cumsum_rows(lf)
            s0_ref[ci] = st
            o_inter = lax.dot_general((q * jnp.exp(b)).astype(BF16), st.astype(BF16), NT,
                                      preferred_element_type=F32)
            vb = v.astype(BF16)
            blast = b[CHUNK - 1:CHUNK, :]
            khat = (k * jnp.exp(blast - b)).astype(BF16)
            st = st * jnp.exp(blast) + lax.dot_general(vb, khat, TN, preferred_element_type=F32)
            blocks = []
            for i in range(CHUNK // SUB_FWD):
                blk = slice(SUB_FWD * i, SUB_FWD * (i + 1))
                qi, ki, vi, bi = q[blk], k[blk], v[blk], b[blk]
                oi = o_inter[blk]
                if i > 0:
                    qfac, kfac = _sub_factors(b, i, SUB_FWD)
                    att = lax.dot_general((qi * qfac).astype(BF16), (k * kfac).astype(BF16), NT,
                                          preferred_element_type=F32)
                    oi = oi + jnp.dot(att.astype(BF16), vb, preferred_element_type=F32)
                for s in range(SUB_FWD):
                    qe = qi * _diag_decay(bi, s)
                    a = jnp.sum(qe * ki[s:s + 1, :], axis=1, keepdims=True)
                    oi = oi + a * vi[s:s + 1, :]
                blocks.append(oi)
            o = jnp.concatenate(blocks, axis=0)
            o_ref[rows, :] = o
            gr = g_ref[rows, :]
            out_ref[rows, :] = o * _rstd(o) * gain * (gr * _sigmoid(gr))
            return st

        st = st_ref[...]
        for ci in range(nch):
            st = chunk(ci, st)
        st_ref[...] = st

    def col(base):
        return pl.BlockSpec((tb, RNN_HD), lambda h, c: (c, base + h))

    res, xres = _call(
        body, name=name, grid=(N_RNN, ntb),
        in_specs=[col(qb), col(fb), col(ib), col(gb),
                  pl.BlockSpec((2, RNN_HD), lambda h, c: (0, h)), pl.BlockSpec((1, RNN_HD), lambda h, c: (0, 0))],
        out_specs=[pl.BlockSpec((tb, RNN_HD), lambda h, c: (c, h)), pl.BlockSpec((tb, RNN_HD), lambda h, c: (c, h)),
                   pl.BlockSpec((None, nch, RNN_HD, RNN_HD), lambda h, c: (h, c, 0, 0))],
        out_shape=[jax.ShapeDtypeStruct((t, RNN_W), F32), jax.ShapeDtypeStruct((t, RNN_W), F32),
                   jax.ShapeDtypeStruct((N_RNN, t // CHUNK, RNN_HD, RNN_HD), F32)],
        scratch_shapes=[pltpu.VMEM((RNN_HD, RNN_HD), F32)],
        args=[proj, proj, proj, proj, lb_logits, norm_gain],
        semantics=("parallel", "arbitrary"), exchanges=exchanges)
    return (*res, xres) if exchanges else res


def _hgrn_bwd(proj, lb_logits, norm_gain, o_pre, s0, dcat, *, tb, name, exchanges=()):
    t = proj.shape[0]
    ntb = t // tb
    nch = tb // CHUNK
    qb, fb, ib, gb = QR_COL // 128, FR_COL // 128, IR_COL // 128, GR_COL // 128
    sub = SUB_BWD
    nsub = CHUNK // sub

    def body(q_ref, f_ref, i_ref, g_ref, lbl_ref, gain_ref, o_ref, s0_ref, dout_ref,
             dq_ref, df_ref, di_ref, dg_ref, dlb_ref, dgain_ref,
             dst_ref, dqs_ref, dks_ref, dvs_ref):
        c = pl.program_id(1)

        @pl.when(c == 0)
        def _():
            dst_ref[...] = jnp.zeros_like(dst_ref)
            dlb_ref[...] = jnp.zeros_like(dlb_ref)
            dgain_ref[...] = jnp.zeros_like(dgain_ref)

        lb = _lower_bound(lbl_ref)
        gain = gain_ref[...]

        def chunk(ci, dst):
            rows = slice(ci * CHUNK, (ci + 1) * CHUNK)
            dqa_ref, dka_ref, dva_ref = dqs_ref.at[ci], dks_ref.at[ci], dvs_ref.at[ci]
            sg, f, lf, k = _hgrn_gates(f_ref[rows, :], lb)
            qr = q_ref[rows, :]
            sq = _sigmoid(qr)
            q = qr * sq
            v = i_ref[rows, :]
            b = _cumsum_rows(lf)

            dout = dout_ref[rows, :]
            o = o_ref[rows, :]
            gr = g_ref[rows, :]
            sgg = _sigmoid(gr)
            gate = gr * sgg
            rs = _rstd(o)
            nrm = o * rs
            dg_ref[rows, :] = (dout * nrm * gain * (sgg * (1.0 + gr * (1.0 - sgg)))).astype(BF16)
            dn = dout * gate
            dgain_ref[...] += jnp.sum(dn * nrm, axis=0, keepdims=True)
            tt = dn * gain
            do = rs * (tt - nrm * jnp.mean(tt * nrm, axis=-1, keepdims=True))

            dob = do.astype(BF16)
            vb = v.astype(BF16)
            eb = jnp.exp(b)
            blast = b[CHUNK - 1:CHUNK, :]
            ebl = jnp.exp(blast - b)
            dstb = dst.astype(BF16)
            khat = (k * ebl).astype(BF16)
            s0 = s0_ref[ci]
            dqa_ref[...] = eb * jnp.dot(dob, s0.astype(BF16), preferred_element_type=F32)
            dk_state = ebl * jnp.dot(vb, dstb, preferred_element_type=F32)
            dka_ref[...] = dk_state
            d_blast = (jnp.sum(k * dk_state, axis=0, keepdims=True)
                       + jnp.exp(blast) * jnp.sum(dst * s0, axis=0, keepdims=True))
            dva_ref[...] = lax.dot_general(khat, dstb, NT, preferred_element_type=F32)
            dst_next = dst * jnp.exp(blast) + lax.dot_general(dob, (q * eb).astype(BF16), TN,
                                                              preferred_element_type=F32)
            pm = lax.dot_general(dob, vb, NT, preferred_element_type=F32)
            for i in range(nsub):
                blk = slice(sub * i, sub * (i + 1))
                qi, ki, vi, bi, doi = q[blk], k[blk], v[blk], b[blk], do[blk]
                dqi = dqa_ref[blk, :]
                if i > 0:
                    qfac, kfac = _sub_factors(b, i, sub)
                    qt = (qi * qfac).astype(BF16)
                    kt = (k * kfac).astype(BF16)
                    att = lax.dot_general(qt, kt, NT, preferred_element_type=F32).astype(BF16)
                    pmi = pm[blk, :].astype(BF16)
                    dva_ref[...] += lax.dot_general(att, doi.astype(BF16), TN, preferred_element_type=F32)
                    dqi = dqi + qfac * jnp.dot(pmi, kt, preferred_element_type=F32)
                    dka_ref[...] += kfac * lax.dot_general(pmi, qt, TN, preferred_element_type=F32)
                dqa_ref[blk, :] = dqi
                srow = lax.broadcasted_iota(jnp.int32, (sub, RNN_HD), 0)
                dki = jnp.zeros((sub, RNN_HD), F32)
                dvi = jnp.zeros((sub, RNN_HD), F32)
                for tq in range(sub):
                    qt, dot_ = qi[tq:tq + 1, :], doi[tq:tq + 1, :]
                    e = jnp.where(srow <= tq, jnp.exp(bi[tq:tq + 1, :] - bi), 0.0)
                    ke = ki * e
                    p = jnp.sum(vi * dot_, axis=1, keepdims=True)
                    a = jnp.sum(ke * qt, axis=1, keepdims=True)
                    dki = dki + p * (qt * e)
                    dvi = dvi + a * dot_
                    row = slice(sub * i + tq, sub * i + tq + 1)
                    dqa_ref[row, :] += jnp.sum(p * ke, axis=0, keepdims=True)
                dka_ref[blk, :] += dki
                dva_ref[blk, :] += dvi

            dq = dqa_ref[...]
            dk = dka_ref[...]
            lastrow = lax.broadcasted_iota(jnp.int32, (CHUNK, RNN_HD), 0) == CHUNK - 1
            dlf = _rev_cumsum_rows(q * dq - k * dk + jnp.where(lastrow, d_blast, 0.0))
            dff = dlf / f - dk
            df_ref[rows, :] = (dff * (1.0 - lb) * sg * (1.0 - sg)).astype(BF16)
            dlb_ref[...] += jnp.sum(dff * (1.0 - sg), axis=0, keepdims=True)
            dq_ref[rows, :] = (dq * (sq * (1.0 + qr * (1.0 - sq)))).astype(BF16)
            di_ref[rows, :] = dva_ref[...].astype(BF16)
            return dst_next

        dst = dst_ref[...]
        for ci in reversed(range(nch)):
            dst = chunk(ci, dst)
        dst_ref[...] = dst

    def col(base):
        return pl.BlockSpec((tb, RNN_HD), lambda h, c: (ntb - 1 - c, base + h))

    outc = pl.BlockSpec((tb, RNN_HD), lambda h, c: (ntb - 1 - c, h))
    hb = ATTN_W // RNN_HD
    res, xres = _call(
        body, name=name, grid=(N_RNN, ntb),
        in_specs=[col(qb), col(fb), col(ib), col(gb),
                  pl.BlockSpec((2, RNN_HD), lambda h, c: (0, h)), pl.BlockSpec((1, RNN_HD), lambda h, c: (0, 0)),
                  outc,
                  pl.BlockSpec((None, nch, RNN_HD, RNN_HD), lambda h, c: (h, ntb - 1 - c, 0, 0)),
                  pl.BlockSpec((tb, RNN_HD), lambda h, c: (ntb - 1 - c, hb + h))],
        out_specs=[outc, outc, outc, outc,
                   pl.BlockSpec((1, RNN_HD), lambda h, c: (0, h)),
                   pl.BlockSpec((None, 1, RNN_HD), lambda h, c: (h, 0, 0))],
        out_shape=[jax.ShapeDtypeStruct((t, RNN_W), BF16)] * 4
        + [jax.ShapeDtypeStruct((1, RNN_W), F32), jax.ShapeDtypeStruct((N_RNN, 1, RNN_HD), F32)],
        scratch_shapes=[pltpu.VMEM((RNN_HD, RNN_HD), F32),
                        pltpu.VMEM((nch, CHUNK, RNN_HD), F32), pltpu.VMEM((nch, CHUNK, RNN_HD), F32),
                        pltpu.VMEM((nch, CHUNK, RNN_HD), F32)],
        args=[proj, proj, proj, proj, lb_logits, norm_gain, o_pre, s0, dcat],
        semantics=("parallel", "arbitrary"), exchanges=exchanges)
    return (*res, xres) if exchanges else res


def _cast_slots(w, where, *, name):
    _, rows, cols = w.shape
    rh = rows // 2
    tr = _row_tile(rh, cols)
    nh = rh // tr

    def body(wh_ref, w_ref, o_ref):
        o_ref[...] = w_ref[...].astype(BF16)

    return pl.pallas_call(
        body, name=name,
        grid_spec=pltpu.PrefetchScalarGridSpec(
            num_scalar_prefetch=1, grid=(2, nh),
            in_specs=[pl.BlockSpec((None, tr, cols), lambda h, i, wh: (0, h * nh + i, 0))],
            out_specs=pl.BlockSpec((None, tr, cols), lambda h, i, wh: (2 * wh[0] + h, i, 0))),
        out_shape=jax.ShapeDtypeStruct((8, rh, cols), BF16),
        compiler_params=_params(("parallel", "parallel")),
    )(where, w)


def _all_gather_halves(bufs, *, name):
    n = len(bufs)

    def body(*refs):
        ins, outs = refs[:n], refs[n:2 * n]
        send_sems, recv_sems = refs[2 * n:]
        x, y, c = _place()
        sibling = (x, y, 1 - c)
        chips = [(1 - x, y), (x, 1 - y), (1 - x, 1 - y)]

        def copy(a, k, block, to, src=None):
            slot = outs[a].at[4 * block[0] + 2 * block[1] + block[2]]
            return pltpu.make_async_remote_copy(
                src_ref=slot if src is None else src, dst_ref=slot,
                send_sem=send_sems.at[a, k], recv_sem=recv_sems.at[a, k],
                device_id=to, device_id_type=MESH)

        first, passed = [], []
        for a in range(n):
            for j, chip in enumerate(chips):
                cp = copy(a, j, (x, y, c), (*chip, c), src=ins[a].at[4 * x + 2 * y + c])
                cp.start()
                first.append(cp)
        for a in range(n):
            for j, chip in enumerate(chips):
                copy(a, j, (*chip, c), (x, y, c)).wait_recv()
                cp = copy(a, 3 + j, (*chip, c), sibling)
                cp.start()
                passed.append(cp)
        for a in range(n):
            for j, chip in enumerate(chips):
                copy(a, 3 + j, (*chip, 1 - c), (x, y, c)).wait_recv()
        for cp in first + passed:
            cp.wait_send()

    return pl.pallas_call(
        body, name=name,
        in_specs=[ANY] * n, out_specs=[ANY] * n,
        out_shape=[jax.ShapeDtypeStruct(b.shape, b.dtype) for b in bufs],
        scratch_shapes=[pltpu.SemaphoreType.DMA((n, 6)), pltpu.SemaphoreType.DMA((n, 6))],
        input_output_aliases={a: a for a in range(n)},
    )(*bufs)


def _row_tile(rows, cols, budget=1 << 20):
    tr = rows
    while tr * cols > budget and tr % 16 == 0:
        tr //= 2
    return tr


def _half_spec(g, tr, halves_last, slab):
    if halves_last:
        return pl.BlockSpec((None, tr, g.shape[2] // 2), lambda *a: (slab(*a), a[-2], a[-1][1]))
    return pl.BlockSpec((None, None, tr, g.shape[3]), lambda *a: (slab(*a), a[-1][1], a[-2], 0))


def _pair_sum(g, sib, where, *, name, halves_last=False):
    rh, cols = sib.shape[1:]
    tr = _row_tile(rh, cols)

    def body(w_ref, g_ref, s_ref, o_ref):
        o_ref[...] = (g_ref[...] + s_ref[...]).astype(BF16)

    return pl.pallas_call(
        body, name=name,
        grid_spec=pltpu.PrefetchScalarGridSpec(
            num_scalar_prefetch=1, grid=(4, rh // tr),
            in_specs=[_half_spec(g, tr, halves_last, lambda s, i, w: s),
                      pl.BlockSpec((None, tr, cols), lambda s, i, w: (s, i, 0))],
            out_specs=pl.BlockSpec((None, tr, cols), lambda s, i, w: (s, i, 0))),
        out_shape=jax.ShapeDtypeStruct((4, rh, cols), BF16),
        compiler_params=_params(("parallel", "parallel")),
    )(where, g, sib)


def _final_half(g, sib, recv, where, *, name, halves_last=False):
    rh, cols = sib.shape[1:]
    tr = _row_tile(rh, cols)

    def body(w_ref, g_ref, s_ref, r_ref, o_ref):
        acc = g_ref[...] + s_ref[...]
        for j in range(3):
            acc = acc + r_ref[j].astype(F32)
        o_ref[...] = acc

    return pl.pallas_call(
        body, name=name,
        grid_spec=pltpu.PrefetchScalarGridSpec(
            num_scalar_prefetch=1, grid=(rh // tr,),
            in_specs=[_half_spec(g, tr, halves_last, lambda i, w: w[0]),
                      pl.BlockSpec((None, tr, cols), lambda i, w: (w[0], i, 0)),
                      pl.BlockSpec((3, tr, cols), lambda i, w: (0, i, 0))],
            out_specs=pl.BlockSpec((tr, cols), lambda i, w: (i, 0))),
        out_shape=jax.ShapeDtypeStruct((rh, cols), F32),
        compiler_params=_params(("parallel",)),
    )(where, g, sib, recv)


def _adamw_math(w, g, m, v):
    m = ADAM_B1 * m + (1.0 - ADAM_B1) * g
    v = ADAM_B2 * v + (1.0 - ADAM_B2) * (g * g)
    m_hat = m / (1.0 - ADAM_B1 ** ADAM_STEP)
    v_hat = v / (1.0 - ADAM_B2 ** ADAM_STEP)
    delta = -ADAM_LR * (m_hat / (jnp.sqrt(v_hat) + ADAM_EPS) + ADAM_WD * w)
    return delta, m, v


def _adamw(w, mine, theirs, m, v, where, *, name, halves_last=False):
    _, rows, cols = w.shape
    if halves_last:
        cols //= 2
        tr = _row_tile(rows, cols, budget=1 << 19)
        grid = (rows // tr, 2)
        blk = pl.BlockSpec((None, tr, cols), lambda i, h, wh: (0, i, h))
        half = pl.BlockSpec((tr, cols), lambda i, h, wh: (i, 0))
        which = lambda: pl.program_id(1)
    else:
        tr = _row_tile(rows // 2, cols, budget=1 << 19)
        nh = rows // 2 // tr
        grid = (rows // tr,)
        blk = pl.BlockSpec((None, tr, cols), lambda i, wh: (0, i, 0))
        half = pl.BlockSpec((tr, cols), lambda i, wh: (i % nh, 0))
        which = lambda: pl.program_id(0) // nh

    def body(wh_ref, w_ref, a_ref, b_ref, m_ref, v_ref, g_ref, d_ref, nm_ref, nv_ref):
        g = jnp.where(which() == wh_ref[1], a_ref[...], b_ref[...])
        d, nm, nv = _adamw_math(w_ref[...], g, m_ref[...], v_ref[...])
        g_ref[...] = g
        d_ref[...] = d
        nm_ref[...] = nm
        nv_ref[...] = nv

    rows, cols = w.shape[1:]
    return pl.pallas_call(
        body, name=name,
        grid_spec=pltpu.PrefetchScalarGridSpec(
            num_scalar_prefetch=1, grid=grid,
            in_specs=[blk, half, half, blk, blk], out_specs=[blk] * 4),
        out_shape=[jax.ShapeDtypeStruct((1, rows, cols), F32)] * 4,
        compiler_params=_params(("parallel",) * len(grid)),
    )(where, w, mine, theirs, m, v)


SEG_LOSS = 0
SEG_SINK = 128
SEG_AGAIN = 256
SEG_L0 = SEG_AGAIN + ATTN_W
SEG_L1 = SEG_L0 + RNN_W
SEG_RGAIN = SEG_L1 + RNN_W
SEG_G = SEG_RGAIN + 128
N_PACK = SEG_G + 4 * D_MODEL


def _pack(sinks, again, l0, l1, rgain, gains, loss=None):
    z = lambda k: jnp.zeros((1, k), F32)
    first = z(128) if loss is None else loss
    return jnp.concatenate([first, sinks, z(128 - N_Q), again, l0, l1, rgain] + list(gains), axis=1)


def _small_reduce_adamw(part, w, m, v, *, name):
    def body(p_ref, w_ref, m_ref, v_ref, g_ref, d_ref, nm_ref, nv_ref, buf_ref, send_sems, recv_sems):
        x, y, c = _place()
        me = 4 * x + 2 * y + c
        copies = []
        for k in range(1, 8):
            dx, dy, dc = (k >> 2) & 1, (k >> 1) & 1, k & 1
            to = (x ^ dx, y ^ dy, c ^ dc)
            cp = pltpu.make_async_remote_copy(
                src_ref=p_ref, dst_ref=buf_ref.at[me],
                send_sem=send_sems.at[k - 1], recv_sem=recv_sems.at[k - 1],
                device_id=to, device_id_type=MESH)
            cp.start()
            copies.append(cp)
        buf_ref[me] = p_ref[...]
        for cp in copies:
            cp.wait()
        tot = buf_ref[0]
        for j in range(1, 8):
            tot = tot + buf_ref[j]
        g_ref[...] = tot
        l0 = w_ref[:, SEG_L0:SEG_L0 + RNN_W]
        l1 = w_ref[:, SEG_L1:SEG_L1 + RNN_W]
        mx = jnp.maximum(l0, l1)
        e0 = jnp.exp(l0 - mx)
        e1 = jnp.exp(l1 - mx)
        lb = e0 / (e0 + e1)
        gl0 = tot[:, SEG_L0:SEG_L0 + RNN_W] * lb * (1.0 - lb)
        g_ref[:, SEG_L0:SEG_L0 + RNN_W] = gl0
        g_ref[:, SEG_L1:SEG_L1 + RNN_W] = -gl0
        d, nm, nv = _adamw_math(w_ref[...], g_ref[...], m_ref[...], v_ref[...])
        d_ref[...] = d
        nm_ref[...] = nm
        nv_ref[...] = nv

    vm = pl.BlockSpec(memory_space=pltpu.VMEM)
    return pl.pallas_call(
        body, name=name,
        in_specs=[vm] * 4, out_specs=[vm] * 4,
        out_shape=[jax.ShapeDtypeStruct((1, N_PACK), F32)] * 4,
        scratch_shapes=[pltpu.VMEM((8, 1, N_PACK), F32), pltpu.SemaphoreType.DMA((7,)),
                        pltpu.SemaphoreType.DMA((7,))],
    )(part, w, m, v)


def _layer_grads(xs, tgt, bufs, where, sinks, again, lb_logits, rgain,
                 g_mix_pre, g_mix_post, g_mlp_pre, g_mlp_post):
    tm = 512
    b_in, b_out, b_up, b_dn = bufs

    shard = IN_W // N_CHIPS
    w_in_t = _all_gather_halves([b_in], name="gather_w_in")[0].reshape(IN_W, D_MODEL)
    h1 = _rms_cast(xs, g_mix_pre, tm=tm, name="h1_norm")
    proj, ((b_out, b_up),) = _mm(
        h1, w_in_t, tm=1024, tn=768, tk=D_MODEL, out_dtype=F32, w_layout="nk", name="in_proj",
        exchanges=[_x_gather([b_out, b_up], ici=[(0, 256), (0, 384)])])
    attn, lse, ((b_out, b_up),) = _swa_fwd(
        proj, sinks, name="swa_fwd",
        exchanges=[_x_gather([b_out, b_up], ici=[None, (384, 320)], d2d=[(0, 256), None])])
    w_out = b_out.reshape(D_MODEL, D_MODEL)
    o_pre, rnn, s0, ((b_up, b_dn),) = _hgrn_fwd(
        proj, lb_logits, rgain, tb=512, name="hgrn_fwd",
        exchanges=[_x_gather([b_up, b_dn], ici=[(704, 320), (0, 608)])])
    cat = _mix_cat(attn, rnn, again, tm=tm, name="mix_cat")
    mixed, ((b_up, b_dn),) = _mm(
        cat, w_out, tm=1024, tn=1024, tk=D_MODEL, out_dtype=F32, name="out_proj",
        exchanges=[_x_gather([b_up, b_dn], ici=[None, (608, 256)], d2d=[(0, 1024), (0, 608)])])
    w_up4 = b_up.reshape(N_CHIPS, D_MODEL, D_FF // N_CHIPS)
    x1, h2, ((b_dn,),) = _post_norm_res(
        mixed, g_mix_post, xs, g_mlp_pre, tm=256, name="mix_post",
        exchanges=[_x_gather([b_dn], ici=[(864, 160)], d2d=[(608, 256)])])
    u, ((b_dn,),) = _mm(h2, w_up4, tm=1024, tn=1024, tk=D_MODEL, out_dtype=BF16, relu=True, w_layout="skn",
                        name="mlp_up", exchanges=[_x_gather([b_dn], d2d=[(864, 160)])])
    w_dn = b_dn.reshape(D_FF, D_MODEL)
    yv = _mm(u, w_dn, tm=1024, tn=1024, tk=2048, out_dtype=F32, a_square=True, name="mlp_down")
    dy, dx2, loss_row, dg_mlp_post = _loss_head(yv, g_mlp_post, x1, tgt, tm=256, name="loss_head")

    def halved(g):
        return g.reshape(N_CHIPS, 2, g.shape[1] // 2, g.shape[2])
    du = _mm(dy, w_dn, tm=1024, tn=1024, tk=D_MODEL, out_dtype=BF16, mul2=u, w_layout="nk", name="mlp_down_bwd")
    g_dn = halved(_mm_tn(u, dy, tm=1024, tn=1024, tt=2048, a_square=True, name="w_down_grad")
                  .reshape(N_CHIPS, D_FF // N_CHIPS, D_MODEL))
    d_w_up, ((sib_dn,),) = _mm_tn(h2, du, tm=1024, tn=1024, tt=2048, n_split=N_CHIPS, name="w_up_grad",
                                  exchanges=[_x_pair([g_dn])])
    g_up = halved(d_w_up)
    wire_dn = _pair_sum(g_dn, sib_dn, where, name="pair_sum_w_down")
    dh2, ((recv_dn,), (sib_up,)) = _mm(du, w_up4, tm=1024, tn=1024, tk=2048, out_dtype=F32, w_layout="snk", name="mlp_up_bwd",
                                       exchanges=[_x_chip([wire_dn], rows=[(0, 800)]), _x_pair([g_up])])
    wire_up = _pair_sum(g_up, sib_up, where, name="pair_sum_w_up")
    dx1, dg_mlp_pre, ((recv_dn,),) = _rms_bwd(dh2, x1, g_mlp_pre, dx2, tm=256, out_dtype=F32, name="mlp_pre_bwd",
                                              exchanges=[_x_chip([wire_dn], rows=[(800, 224)], into=[recv_dn])])
    fin_dn = _final_half(g_dn, sib_dn, recv_dn, where, name="final_half_w_down")
    dmixed, dg_mix_post = _rms_bwd(dx1, mixed, g_mix_post, None, tm=256, out_dtype=BF16, name="mix_post_bwd")
    d_w_out, ((oth_dn,),) = _mm_tn(cat, dmixed, tm=1024, tn=1024, tt=2048, name="w_out_grad",
                                   exchanges=[_x_share([fin_dn])])
    g_out = halved(d_w_out.reshape(N_CHIPS, D_MODEL // N_CHIPS, D_MODEL))
    dcat, ((sib_out,),) = _mm(dmixed, w_out, tm=1024, tn=1024, tk=D_MODEL, out_dtype=F32, w_layout="nk", name="out_proj_bwd",
                              exchanges=[_x_pair([g_out])])
    wire_out = _pair_sum(g_out, sib_out, where, name="pair_sum_w_out")
    dattn, dg_again = _rms_bwd(dcat, attn, again, None, tm=tm, out_dtype=F32, name="attn_norm_bwd")
    dq_a, dkv, dsinks, ((recv_out,), (recv_up,)) = _swa_bwd(
        proj, sinks, dattn, lse, name="swa_bwd",
        exchanges=[_x_chip([wire_out]), _x_chip([wire_up], rows=[(0, 320)])])
    dq_r, df_r, di_r, dg_r, dlb, dgain_h, ((recv_up,),) = _hgrn_bwd(
        proj, lb_logits, rgain, o_pre, s0, dcat, tb=512, name="hgrn_bwd",
        exchanges=[_x_chip([wire_up], rows=[(320, 704)], into=[recv_up])])
    fin_up = _final_half(g_up, sib_up, recv_up, where, name="final_half_w_up")
    fin_out = _final_half(g_out, sib_out, recv_out, where, name="final_half_w_out")
    dproj = jnp.concatenate([dq_a, dkv, dq_r, df_r, di_r, dg_r], axis=1)
    piece_cols = D_MODEL // 4

    def w_in_piece(pc, exchanges):
        d, xres = _mm_tn(dproj, h1, tm=896, tn=piece_cols, tt=2048, n_blocks=(2, 2, pc),
                         name="w_in_grad_%d" % pc, exchanges=exchanges)
        return d.reshape(N_CHIPS, shard, 2 * piece_cols), xres

    g_in0, ((oth_up, oth_out),) = w_in_piece(0, [_x_share([fin_up, fin_out])])
    g_in1, ((sib_in0,),) = w_in_piece(1, [_x_pair([g_in0], halves_last=True)])
    wire_in0 = _pair_sum(g_in0, sib_in0, where, name="pair_sum_w_in_0", halves_last=True)
    dh1, ((recv_in0,), (sib_in1,)) = _mm(
        dproj, w_in_t, tm=1024, tn=1024, tk=2688, out_dtype=F32, m_blocks=(0, 2), name="in_proj_bwd_0",
        exchanges=[_x_chip([wire_in0]), _x_pair([g_in1], halves_last=True)])
    wire_in1 = _pair_sum(g_in1, sib_in1, where, name="pair_sum_w_in_1", halves_last=True)
    dh1, ((recv_in1,),) = _mm(
        dproj, w_in_t, tm=1024, tn=1024, tk=2688, out_dtype=F32, m_blocks=(2, 2), out_into=dh1,
        name="in_proj_bwd_1", exchanges=[_x_chip([wire_in1])])
    gx, dg_mix_pre = _rms_bwd(dh1, xs, g_mix_pre, dx1, tm=256, out_dtype=F32, name="mix_pre_bwd")
    fin_in0 = _final_half(g_in0, sib_in0, recv_in0, where, name="final_half_w_in_0", halves_last=True)
    fin_in1 = _final_half(g_in1, sib_in1, recv_in1, where, name="final_half_w_in_1", halves_last=True)
    oth_in0, oth_in1 = _run_exchange(_x_share([fin_in0, fin_in1]), name="share_w_in")
    fin_in = jnp.concatenate([fin_in0, fin_in1], axis=1)
    oth_in = jnp.concatenate([oth_in0, oth_in1], axis=1)

    big = [(fin_in, oth_in), (fin_out, oth_out), (fin_up, oth_up), (fin_dn, oth_dn)]
    drgain = jnp.sum(dgain_h, axis=0)
    small = _pack(jnp.sum(dsinks, axis=1)[None, :], dg_again, dlb, jnp.zeros_like(dlb), drgain,
                  [dg_mix_pre, dg_mix_post, dg_mlp_pre, dg_mlp_post], loss=loss_row)
    return gx, big, small


def kernel(x, w_in, attn_sinks, attn_out_gain, rnn_lb_logits, rnn_norm_gain, w_out, mix_pre_gain, mix_post_gain, mlp_pre_gain, mlp_post_gain, w_up, w_down, loss_target, m_w_in, m_attn_sinks, m_attn_out_gain, m_rnn_lb_logits, m_rnn_norm_gain, m_w_out, m_mix_pre_gain, m_mix_post_gain, m_mlp_pre_gain, m_mlp_post_gain, m_w_up, m_w_down, v_w_in, v_attn_sinks, v_attn_out_gain, v_rnn_lb_logits, v_rnn_norm_gain, v_w_out, v_mix_pre_gain, v_mix_post_gain, v_mlp_pre_gain, v_mlp_post_gain, v_w_up, v_w_down):
    ax, ay, ac = _place()
    where = jnp.stack([2 * ax + ay, ac]).astype(jnp.int32)
    t = lambda a: jnp.swapaxes(a, 1, 2)
    big_w = [t(w_in), w_out, w_up, w_down]
    big_m = [t(m_w_in), m_w_out, m_w_up, m_w_down]
    big_v = [t(v_w_in), v_w_out, v_w_up, v_w_down]

    names = ["w_in", "w_out", "w_up", "w_down"]
    bufs = [_cast_slots(w, where, name="cast_" + nm) for w, nm in zip(big_w, names)]
    gx, big_g, small_part = _layer_grads(
        x[0], loss_target[0], bufs, where, attn_sinks, attn_out_gain, rnn_lb_logits, rnn_norm_gain,
        mix_pre_gain, mix_post_gain, mlp_pre_gain, mlp_post_gain)

    grads, deltas, new_m, new_v = [], [], [], []
    for (f, o), w, m, v, nm in zip(big_g, big_w, big_m, big_v, names):
        res = _adamw(w, f, o, m, v, where, name="adamw_" + nm, halves_last=(nm == "w_in"))
        if nm == "w_in":
            res = [t(r) for r in res]
        g, d, nm_, nv_ = res
        grads.append(g)
        deltas.append(d)
        new_m.append(nm_)
        new_v.append(nv_)

    def pack_params(sinks, again, logits, rgain, gains):
        return _pack(sinks, again, logits[0:1], logits[1:2], rgain, gains)

    pw = pack_params(attn_sinks, attn_out_gain, rnn_lb_logits, rnn_norm_gain,
                     [mix_pre_gain, mix_post_gain, mlp_pre_gain, mlp_post_gain])
    pm = pack_params(m_attn_sinks, m_attn_out_gain, m_rnn_lb_logits, m_rnn_norm_gain,
                     [m_mix_pre_gain, m_mix_post_gain, m_mlp_pre_gain, m_mlp_post_gain])
    pv = pack_params(v_attn_sinks, v_attn_out_gain, v_rnn_lb_logits, v_rnn_norm_gain,
                     [v_mix_pre_gain, v_mix_post_gain, v_mlp_pre_gain, v_mlp_post_gain])
    packs = _small_reduce_adamw(small_part, pw, pm, pv, name="small_reduce_adamw")

    def unpack(p):
        seg = lambda o, k: p[:, o:o + k]
        logits = jnp.concatenate([seg(SEG_L0, RNN_W), seg(SEG_L1, RNN_W)], axis=0)
        gains = [seg(SEG_G + i * D_MODEL, D_MODEL) for i in range(4)]
        return dict(sinks=seg(SEG_SINK, N_Q), again=seg(SEG_AGAIN, ATTN_W), logits=logits,
                    rgain=seg(SEG_RGAIN, RNN_HD), gains=gains)

    def order(small, big):
        return [big[0], small["sinks"], small["again"], small["logits"], small["rgain"], big[1],
                *small["gains"], big[2], big[3]]

    loss = packs[0][0, 0]
    outs = [loss, gx[None]]
    for p, b in zip(packs, [grads, deltas, new_m, new_v]):
        outs += order(unpack(p), b)
    return tuple(outs)
```

```python
import functools

import jax
import jax.numpy as jnp
from jax import lax
from jax.experimental import pallas as pl
from jax.experimental.pallas import tpu as pltpu

F32 = jnp.float32
BF16 = jnp.bfloat16
MESH = pl.DeviceIdType.MESH

EPS = 1e-6
D_MODEL = 2048
ATTN_W = 1024
HEAD_DIM = 64
N_Q = 16
N_KV = 2
GROUP = 8
BLK = 128
RNN_W = 1024
RNN_HD = 128
N_RNN = 8
CHUNK = 64
SUB_FWD = 16
SUB_BWD = 8
D_FF = 8192
IN_W = 5376
N_CHIPS = 4
KV_COL = ATTN_W
QR_COL = ATTN_W + 2 * 128
FR_COL = QR_COL + RNN_W
IR_COL = FR_COL + RNN_W
GR_COL = IR_COL + RNN_W

ADAM_LR = 0.001
ADAM_B1 = 0.9
ADAM_B2 = 0.999
ADAM_EPS = 1e-08
ADAM_WD = 0.01
ADAM_STEP = 10

VMEM_LIMIT = 48 * 1024 * 1024

NT = (((1,), (1,)), ((), ()))
TN = (((0,), (0,)), ((), ()))


def _params(sem=None):
    return pltpu.CompilerParams(dimension_semantics=sem, vmem_limit_bytes=VMEM_LIMIT)


def _sigmoid(x):
    return 1.0 / (1.0 + jnp.exp(-x))


ANY = pl.BlockSpec(memory_space=pl.ANY)


def _place():
    return lax.axis_index("x"), lax.axis_index("y"), lax.axis_index("c")


def _other_chips(x, y):
    return [(1 - x, y), (x, 1 - y), (1 - x, 1 - y)]


class _Exchange:
    def __init__(self, srcs, outs, ncopy, build, aliases=None):
        self.srcs, self.outs, self.ncopy, self.build = list(srcs), list(outs), ncopy, build
        self.aliases = aliases or {}


def _remote(src, dst, send_sems, recv_sems, k, to):
    return pltpu.make_async_remote_copy(src_ref=src, dst_ref=dst, send_sem=send_sems.at[k],
                                        recv_sem=recv_sems.at[k], device_id=to, device_id_type=MESH)


def _call(body, *, name, grid, in_specs, out_specs, out_shape, args, scratch_shapes=(), semantics=None,
          exchanges=(), into=None):
    in_specs, out_specs, out_shape = list(in_specs), list(out_specs), list(out_shape)
    scratch_shapes = list(scratch_shapes)
    ni, no, ns = len(in_specs), len(out_specs), len(scratch_shapes)
    xsrc = [s for x in exchanges for s in x.srcs]
    xout = [o for x in exchanges for o in x.outs]
    into = into or {}
    xsrc += [into[k] for k in sorted(into)]
    nxi, nxo = len(xsrc), len(xout)
    aliases = {nxi - len(into) + ni + q: k for q, k in enumerate(sorted(into))}
    a0 = b0 = 0
    for x in exchanges:
        for si, oi in x.aliases.items():
            aliases[ni + a0 + si] = no + b0 + oi
        a0 += len(x.srcs)
        b0 += len(x.outs)
    sems = []
    for x in exchanges:
        sems += [pltpu.SemaphoreType.DMA((x.ncopy,)), pltpu.SemaphoreType.DMA((x.ncopy,))]

    def wrapped(*refs):
        ins, xi = refs[:ni], refs[ni:ni + nxi]
        outs, xo = refs[ni + nxi:ni + nxi + no], refs[ni + nxi + no:ni + nxi + no + nxo]
        rest = refs[ni + nxi + no + nxo:]
        scr, sm = rest[:ns], rest[ns:]

        def copies():
            cps = []
            a = b = 0
            for k, x in enumerate(exchanges):
                cps += x.build(xi[a:a + len(x.srcs)], xo[b:b + len(x.outs)], sm[2 * k], sm[2 * k + 1])
                a += len(x.srcs)
                b += len(x.outs)
            return cps

        def start():
            for cp in copies():
                cp.start()

        def wait():
            for cp in copies():
                cp.wait()

        if not exchanges:
            body(*ins, *outs, *scr)
        elif not grid:
            start()
            body(*ins, *outs, *scr)
            wait()
        else:
            first = last = None
            for ax, g in enumerate(grid):
                f = pl.program_id(ax) == 0
                l = pl.program_id(ax) == g - 1
                first = f if first is None else first & f
                last = l if last is None else last & l
            pl.when(first)(start)
            body(*ins, *outs, *scr)
            pl.when(last)(wait)

    if exchanges and semantics is not None:
        semantics = ("arbitrary",) * len(grid)
    kwargs = dict(grid=grid) if grid else {}
    res = pl.pallas_call(
        wrapped, name=name,
        in_specs=in_specs + [ANY] * nxi, out_specs=out_specs + [ANY] * nxo,
        out_shape=out_shape + xout, scratch_shapes=scratch_shapes + sems,
        input_output_aliases=aliases,
        compiler_params=_params(semantics), **kwargs,
    )(*args, *xsrc)
    res = list(res)
    mine, theirs = res[:no], res[no:]
    per = []
    b = 0
    for x in exchanges:
        per.append(theirs[b:b + len(x.outs)])
        b += len(x.outs)
    return mine, per


def _run_exchange(x, *, name):
    return _call(lambda: None, name=name, grid=(), in_specs=[], out_specs=[], out_shape=[], args=[],
                 exchanges=[x])[1][0]


def _x_gather(bufs, ici=None, d2d=None):
    n = len(bufs)
    plan = [(a, kind, rows[a]) for a in range(n) for kind, rows in (("ici", ici), ("d2d", d2d))
            if rows is not None and rows[a] is not None]

    def build(srcs, outs, ss, rs):
        x, y, c = _place()
        cps = []
        for q, (a, kind, rows) in enumerate(plan):
            piece = pl.ds(*rows)
            for j, (px, py) in enumerate(_other_chips(x, y)):
                slot, to = (4 * x + 2 * y + c, (px, py, c)) if kind == "ici" else (4 * px + 2 * py + c, (x, y, 1 - c))
                cps.append(_remote(srcs[a].at[slot, piece], outs[a].at[slot, piece], ss, rs, 3 * q + j, to))
        return cps

    outs = [jax.ShapeDtypeStruct(b.shape, b.dtype) for b in bufs]
    return _Exchange(bufs, outs, 3 * len(plan), build, aliases={a: a for a in range(n)})


def _x_pair(grads, halves_last=False):
    n = len(grads)

    def build(srcs, outs, ss, rs):
        x, y, c = _place()

        def half(r):
            if not halves_last:
                return r.at[:, 1 - c]
            ch = r.shape[2] // 2
            return r.at[:, :, pl.ds(pl.multiple_of((1 - c) * ch, 128), ch)]

        return [_remote(half(srcs[a]), outs[a], ss, rs, a, (x, y, 1 - c)) for a in range(n)]

    if halves_last:
        outs = [jax.ShapeDtypeStruct(g.shape[:2] + (g.shape[2] // 2,), g.dtype) for g in grads]
    else:
        outs = [jax.ShapeDtypeStruct((4,) + g.shape[2:], g.dtype) for g in grads]
    return _Exchange(grads, outs, n, build)


def _x_chip(wires, rows=None, into=None):
    n = len(wires)
    rows = rows or [(0, w.shape[1]) for w in wires]

    def build(srcs, outs, ss, rs):
        x, y, c = _place()
        cps = []
        for a in range(n):
            piece = pl.ds(*rows[a])
            for j, (px, py) in enumerate(_other_chips(x, y)):
                cps.append(_remote(srcs[a].at[2 * px + py, piece], outs[a].at[j, piece], ss, rs,
                                   3 * a + j, (px, py, c)))
        return cps

    outs = [jax.ShapeDtypeStruct((3,) + w.shape[1:], w.dtype) for w in wires]
    if into is None:
        return _Exchange(wires, outs, 3 * n, build)
    return _Exchange(list(wires) + list(into), outs, 3 * n, build, aliases={n + a: a for a in range(n)})


def _x_share(halves):
    n = len(halves)

    def build(srcs, outs, ss, rs):
        x, y, c = _place()
        return [_remote(srcs[a], outs[a], ss, rs, a, (x, y, 1 - c)) for a in range(n)]

    outs = [jax.ShapeDtypeStruct(h.shape, h.dtype) for h in halves]
    return _Exchange(halves, outs, n, build)


def _mm(a, w, *, tm, tn, tk, out_dtype, name, a_square=False, relu=False, mul2=None, w_layout="kn",
        m_blocks=None, out_into=None, exchanges=()):
    m, k = a.shape
    m_first, m_count = m_blocks or (0, m // tm)
    a_spec = pl.BlockSpec((tm, tk), lambda i, j, kk: (i + m_first, kk))
    if w_layout == "kn":
        n = w.shape[1]
        w_spec = pl.BlockSpec((tk, tn), lambda i, j, kk: (kk, j))
    elif w_layout == "nk":
        n = w.shape[0]
        w_spec = pl.BlockSpec((tn, tk), lambda i, j, kk: (j, kk))
    elif w_layout == "skn":
        n = w.shape[0] * w.shape[2]
        per_n = w.shape[2] // tn
        w_spec = pl.BlockSpec((None, tk, tn), lambda i, j, kk: (j // per_n, kk, j % per_n))
    else:
        assert w_layout == "snk"
        n = w.shape[1]
        per_k = w.shape[2] // tk
        w_spec = pl.BlockSpec((None, tn, tk), lambda i, j, kk: (kk // per_k, j, kk % per_k))
    w_dims = NT if w_layout in ("nk", "snk") else (((1,), (0,)), ((), ()))
    nk = k // tk
    assert m % tm == 0 and n % tn == 0 and k % tk == 0

    def body(*refs):
        if mul2 is not None:
            a_ref, w_ref, e_ref, o_ref, acc_ref = refs
        else:
            a_ref, w_ref, o_ref, acc_ref = refs
            e_ref = None
        kk = pl.program_id(2)
        av = a_ref[...]
        if a_square:
            af = av.astype(F32)
            av = (af * af).astype(BF16)
        part = lax.dot_general(av, w_ref[...], w_dims, preferred_element_type=F32)

        def finish(r):
            if relu:
                r = jnp.maximum(r, 0.0)
            if e_ref is not None:
                r = 2.0 * e_ref[...].astype(F32) * r
            o_ref[...] = r.astype(out_dtype)

        if nk == 1:
            finish(part)
        else:
            @pl.when(kk == 0)
            def _():
                acc_ref[...] = part

            @pl.when(kk > 0)
            def _():
                acc_ref[...] += part

            @pl.when(kk == nk - 1)
            def _():
                finish(acc_ref[...])

    in_specs = [a_spec, w_spec]
    args = [a, w]
    if mul2 is not None:
        in_specs.append(pl.BlockSpec((tm, tn), lambda i, j, kk: (i + m_first, j)))
        args.append(mul2)
    acc_shape = (tm, tn) if nk > 1 else (8, 128)
    (out,), per = _call(
        body, name=name, grid=(m_count, n // tn, nk),
        in_specs=in_specs, out_specs=[pl.BlockSpec((tm, tn), lambda i, j, kk: (i + m_first, j))],
        out_shape=[jax.ShapeDtypeStruct((m, n), out_dtype)], args=args,
        scratch_shapes=[pltpu.VMEM(acc_shape, F32)],
        semantics=("parallel", "parallel", "arbitrary"), exchanges=exchanges,
        into=None if out_into is None else {0: out_into})
    return (out, per) if exchanges else out


def _mm_tn(a, b, *, tm, tn, tt, name, a_square=False, n_split=1, n_blocks=None, exchanges=()):
    t, m = a.shape
    n = b.shape[1]
    assert t % tt == 0 and m % tm == 0 and n % tn == 0
    count, stride, first = n_blocks or (n // tn, 1, 0)
    n = count * tn
    assert (n // n_split) % tn == 0
    per = n // n_split // tn

    def body(a_ref, b_ref, o_ref):
        ti = pl.program_id(2)
        av = a_ref[...]
        if a_square:
            af = av.astype(F32)
            av = (af * af).astype(BF16)
        part = lax.dot_general(av, b_ref[...], TN, preferred_element_type=F32)

        @pl.when(ti == 0)
        def _():
            o_ref[...] = part

        @pl.when(ti > 0)
        def _():
            o_ref[...] += part

    (out,), xres = _call(
        body, name=name, grid=(m // tm, n // tn, t // tt),
        in_specs=[pl.BlockSpec((tt, tm), lambda i, j, ti: (ti, i)),
                  pl.BlockSpec((tt, tn), lambda i, j, ti: (ti, first + stride * j))],
        out_specs=[pl.BlockSpec((None, tm, tn), lambda i, j, ti: (j // per, i, j % per))],
        out_shape=[jax.ShapeDtypeStruct((n_split, m, n // n_split), F32)], args=[a, b],
        semantics=("parallel", "parallel", "arbitrary"), exchanges=exchanges)
    return (out, xres) if exchanges else out


def _rstd(x):
    return lax.rsqrt(jnp.mean(x * x, axis=-1, keepdims=True) + EPS)


def _rms_cast(x, g, *, tm, name):
    t, d = x.shape

    def body(x_ref, g_ref, o_ref):
        xv = x_ref[...]
        o_ref[...] = (xv * _rstd(xv) * g_ref[...]).astype(BF16)

    return pl.pallas_call(
        body, name=name, grid=(t // tm,),
        in_specs=[pl.BlockSpec((tm, d), lambda i: (i, 0)), pl.BlockSpec((1, d), lambda i: (0, 0))],
        out_specs=pl.BlockSpec((tm, d), lambda i: (i, 0)),
        out_shape=jax.ShapeDtypeStruct((t, d), BF16),
        compiler_params=_params(("parallel",)),
    )(x, g)


def _mix_cat(attn, rnn, gain, *, tm, name):
    t = attn.shape[0]

    def body(a_ref, r_ref, g_ref, o_ref):
        av = a_ref[...]
        o_ref[:, :ATTN_W] = (av * _rstd(av) * g_ref[...]).astype(BF16)
        o_ref[:, ATTN_W:] = r_ref[...].astype(BF16)

    return pl.pallas_call(
        body, name=name, grid=(t // tm,),
        in_specs=[pl.BlockSpec((tm, ATTN_W), lambda i: (i, 0)), pl.BlockSpec((tm, RNN_W), lambda i: (i, 0)),
                  pl.BlockSpec((1, ATTN_W), lambda i: (0, 0))],
        out_specs=pl.BlockSpec((tm, D_MODEL), lambda i: (i, 0)),
        out_shape=jax.ShapeDtypeStruct((t, D_MODEL), BF16),
        compiler_params=_params(("parallel",)),
    )(attn, rnn, gain)


def _post_norm_res(mixed, g_post, res, g_next, *, tm, name, exchanges=()):
    t, d = mixed.shape

    def body(m_ref, gp_ref, r_ref, gn_ref, x1_ref, h2_ref):
        mv = m_ref[...]
        x1 = r_ref[...] + mv * _rstd(mv) * gp_ref[...]
        x1_ref[...] = x1
        h2_ref[...] = (x1 * _rstd(x1) * gn_ref[...]).astype(BF16)

    row = pl.BlockSpec((tm, d), lambda i: (i, 0))
    vec = pl.BlockSpec((1, d), lambda i: (0, 0))
    res_, xres = _call(
        body, name=name, grid=(t // tm,),
        in_specs=[row, vec, row, vec], out_specs=[row, row],
        out_shape=[jax.ShapeDtypeStruct((t, d), F32), jax.ShapeDtypeStruct((t, d), BF16)],
        args=[mixed, g_post, res, g_next], semantics=("parallel",), exchanges=exchanges)
    return (*res_, xres) if exchanges else res_


def _rms_bwd(dyn, xin, g, res, *, tm, out_dtype, name, col_block=0, exchanges=()):
    t, d = xin.shape

    def body(*refs):
        if res is not None:
            dy_ref, x_ref, g_ref, r_ref, dx_ref, dg_ref = refs
        else:
            dy_ref, x_ref, g_ref, dx_ref, dg_ref = refs
        i = pl.program_id(0)
        xv = x_ref[...]
        dy = dy_ref[...].astype(F32)
        r = _rstd(xv)
        xh = xv * r
        part = jnp.sum(dy * xh, axis=0, keepdims=True)

        @pl.when(i == 0)
        def _():
            dg_ref[...] = part

        @pl.when(i > 0)
        def _():
            dg_ref[...] += part

        tt = dy * g_ref[...]
        dx = r * (tt - xh * jnp.mean(tt * xh, axis=-1, keepdims=True))
        if res is not None:
            dx = dx + r_ref[...]
        dx_ref[...] = dx.astype(out_dtype)

    row = pl.BlockSpec((tm, d), lambda i: (i, 0))
    vec = pl.BlockSpec((1, d), lambda i: (0, 0))
    in_specs = [pl.BlockSpec((tm, d), lambda i: (i, col_block)), row, vec]
    args = [dyn, xin, g]
    if res is not None:
        in_specs.append(row)
        args.append(res)
    res, xres = _call(
        body, name=name, grid=(t // tm,),
        in_specs=in_specs, out_specs=[row, vec],
        out_shape=[jax.ShapeDtypeStruct((t, d), out_dtype), jax.ShapeDtypeStruct((1, d), F32)], args=args,
        semantics=("arbitrary",), exchanges=exchanges)
    return (*res, xres) if exchanges else res


def _loss_head(y, g_post, x1, target, *, tm, name):
    t, d = y.shape

    def body(y_ref, g_ref, x1_ref, t_ref, dy_ref, dx2_ref, loss_ref, dg_ref):
        i = pl.program_id(0)
        yv = y_ref[...]
        r = _rstd(yv)
        yh = yv * r
        gv = g_ref[...]
        err = x1_ref[...] + yh * gv - t_ref[...]
        lpart = 0.5 * jnp.sum(jnp.mean(err * err, axis=-1, keepdims=True), axis=0, keepdims=True)
        dx2 = err * (1.0 / d)
        dgp = jnp.sum(dx2 * yh, axis=0, keepdims=True)
        lane = lax.broadcasted_iota(jnp.int32, (1, 128), 1)
        lrow = jnp.where(lane == 0, lpart, 0.0)

        @pl.when(i == 0)
        def _():
            dg_ref[...] = dgp
            loss_ref[...] = lrow

        @pl.when(i > 0)
        def _():
            dg_ref[...] += dgp
            loss_ref[...] += lrow

        tt = dx2 * gv
        dy_ref[...] = (r * (tt - yh * jnp.mean(tt * yh, axis=-1, keepdims=True))).astype(BF16)
        dx2_ref[...] = dx2

    row = pl.BlockSpec((tm, d), lambda i: (i, 0))
    vec = pl.BlockSpec((1, d), lambda i: (0, 0))
    return pl.pallas_call(
        body, name=name, grid=(t // tm,),
        in_specs=[row, vec, row, row],
        out_specs=[row, row, pl.BlockSpec((1, 128), lambda i: (0, 0)), vec],
        out_shape=[jax.ShapeDtypeStruct((t, d), BF16), jax.ShapeDtypeStruct((t, d), F32),
                   jax.ShapeDtypeStruct((1, 128), F32), jax.ShapeDtypeStruct((1, d), F32)],
        compiler_params=_params(("arbitrary",)),
    )(y, g_post, x1, target)


def _alibi_slope(h):
    return 2.0 ** (-8.0 * (h + 1) / N_Q)


PAIR = 2 * HEAD_DIM
N_PAIRS = N_Q // 2
PAIRS_PER_KV = GROUP // 2
SMEM = pl.BlockSpec(memory_space=pltpu.SMEM)


def _swa_mask(n):
    key = lax.broadcasted_iota(jnp.int32, (2 * BLK, BLK), 0)
    qry = lax.broadcasted_iota(jnp.int32, (2 * BLK, BLK), 1)
    dist = qry + BLK - key
    valid = (dist >= 0) & (dist < BLK) & ((key >= BLK) | (n > 0))
    return valid, dist.astype(F32)


def _block_diag(kvp_ref, kvc_ref, off):
    a = jnp.concatenate([kvp_ref[:, off:off + HEAD_DIM], kvc_ref[:, off:off + HEAD_DIM]], axis=0).astype(BF16)
    z = jnp.zeros_like(a)
    return jnp.concatenate([jnp.concatenate([a, z], axis=1), jnp.concatenate([z, a], axis=1)], axis=0)


def _swa_scores(s2, e, hh, valid, distf):
    s = s2[2 * BLK * e:2 * BLK * (e + 1)] * (HEAD_DIM ** -0.5) - _alibi_slope(hh) * distf
    return jnp.where(valid, s, -1e30)


def _swa_fwd(proj, sinks, *, name, exchanges=()):
    t = proj.shape[0]
    nb = t // BLK
    kvb = KV_COL // (2 * 128)

    def body(sink_ref, q_ref, kvc_ref, kvp_ref, o_ref, lse_ref):
        n = pl.program_id(0)
        valid, distf = _swa_mask(n)
        for kvh in range(N_KV):
            k2 = _block_diag(kvp_ref, kvc_ref, kvh * HEAD_DIM)
            v2 = _block_diag(kvp_ref, kvc_ref, 128 + kvh * HEAD_DIM)
            for jp in range(PAIRS_PER_KV):
                pair = kvh * PAIRS_PER_KV + jp
                lanes = slice(pair * PAIR, (pair + 1) * PAIR)
                s2 = lax.dot_general(k2, q_ref[:, lanes].astype(BF16), NT, preferred_element_type=F32)
                probs = []
                for e in range(2):
                    hh = 2 * pair + e
                    s = _swa_scores(s2, e, hh, valid, distf)
                    sink = sink_ref[0, hh]
                    mx = jnp.maximum(jnp.max(s, axis=0, keepdims=True), sink)
                    p = jnp.exp(s - mx)
                    l = jnp.sum(p, axis=0, keepdims=True) + jnp.exp(sink - mx)
                    probs.append((p * (1.0 / l)).astype(BF16))
                    lse_ref[hh:hh + 1, :] = mx + jnp.log(l)
                o_ref[:, lanes] = lax.dot_general(jnp.concatenate(probs, axis=0), v2, TN,
                                                  preferred_element_type=F32)

    res, xres = _call(
        body, name=name, grid=(nb,),
        in_specs=[SMEM,
                  pl.BlockSpec((BLK, ATTN_W), lambda n: (n, 0)),
                  pl.BlockSpec((BLK, 256), lambda n: (n, kvb)),
                  pl.BlockSpec((BLK, 256), lambda n: (jnp.maximum(n - 1, 0), kvb))],
        out_specs=[pl.BlockSpec((BLK, ATTN_W), lambda n: (n, 0)),
                   pl.BlockSpec((None, N_Q, BLK), lambda n: (n, 0, 0))],
        out_shape=[jax.ShapeDtypeStruct((t, ATTN_W), F32), jax.ShapeDtypeStruct((nb, N_Q, BLK), F32)],
        args=[sinks, proj, proj, proj], semantics=("parallel",), exchanges=exchanges)
    return (*res, xres) if exchanges else res


def _swa_bwd(proj, sinks, dattn, lse, *, name, exchanges=()):
    t = proj.shape[0]
    nb = t // BLK
    kvb = KV_COL // (2 * 128)

    def body(sink_ref, q_ref, kvc_ref, kvp_ref, do_ref, lse_ref, dq_ref, dkv_ref, dsink_ref, carry_ref):
        n = pl.program_id(0)

        @pl.when(n == 0)
        def _():
            dsink_ref[...] = jnp.zeros_like(dsink_ref)
            carry_ref[...] = jnp.zeros_like(carry_ref)

        @pl.when(n < nb)
        def _():
            valid, distf = _swa_mask(n)
            for kvh in range(N_KV):
                k2 = _block_diag(kvp_ref, kvc_ref, kvh * HEAD_DIM)
                v2 = _block_diag(kvp_ref, kvc_ref, 128 + kvh * HEAD_DIM)
                dk2 = jnp.zeros((4 * BLK, PAIR), F32)
                dv2 = jnp.zeros((4 * BLK, PAIR), F32)
                for jp in range(PAIRS_PER_KV):
                    pair = kvh * PAIRS_PER_KV + jp
                    lanes = slice(pair * PAIR, (pair + 1) * PAIR)
                    q2 = q_ref[:, lanes].astype(BF16)
                    do2 = do_ref[:, lanes].astype(BF16)
                    s2 = lax.dot_general(k2, q2, NT, preferred_element_type=F32)
                    dp2 = lax.dot_general(v2, do2, NT, preferred_element_type=F32)
                    probs, dss = [], []
                    for e in range(2):
                        hh = 2 * pair + e
                        lse_h = lse_ref[hh:hh + 1, :]
                        p = jnp.exp(_swa_scores(s2, e, hh, valid, distf) - lse_h)
                        dp = dp2[2 * BLK * e:2 * BLK * (e + 1)]
                        delta = jnp.sum(p * dp, axis=0, keepdims=True)
                        dsink_ref[hh:hh + 1, :] += -jnp.exp(sink_ref[0, hh] - lse_h) * delta
                        probs.append(p.astype(BF16))
                        dss.append((p * (dp - delta)).astype(BF16))
                    ds2 = jnp.concatenate(dss, axis=0)
                    dq_ref[:, lanes] = (lax.dot_general(ds2, k2, TN, preferred_element_type=F32)
                                        * (HEAD_DIM ** -0.5)).astype(BF16)
                    dk2 = dk2 + jnp.dot(ds2, q2, preferred_element_type=F32)
                    dv2 = dv2 + jnp.dot(jnp.concatenate(probs, axis=0), do2, preferred_element_type=F32)
                dk_cat = (dk2[:2 * BLK, :HEAD_DIM] + dk2[2 * BLK:, HEAD_DIM:]) * (HEAD_DIM ** -0.5)
                dv_cat = dv2[:2 * BLK, :HEAD_DIM] + dv2[2 * BLK:, HEAD_DIM:]
                ko = kvh * HEAD_DIM
                vo = 128 + kvh * HEAD_DIM
                dkv_ref[:, ko:ko + HEAD_DIM] = (carry_ref[:, ko:ko + HEAD_DIM] + dk_cat[:BLK]).astype(BF16)
                dkv_ref[:, vo:vo + HEAD_DIM] = (carry_ref[:, vo:vo + HEAD_DIM] + dv_cat[:BLK]).astype(BF16)
                carry_ref[:, ko:ko + HEAD_DIM] = dk_cat[BLK:]
                carry_ref[:, vo:vo + HEAD_DIM] = dv_cat[BLK:]

        @pl.when(n == nb)
        def _():
            dkv_ref[...] = carry_ref[...].astype(BF16)

    last = nb - 1
    res, xres = _call(
        body, name=name, grid=(nb + 1,),
        in_specs=[SMEM,
                  pl.BlockSpec((BLK, ATTN_W), lambda n: (jnp.minimum(n, last), 0)),
                  pl.BlockSpec((BLK, 256), lambda n: (jnp.minimum(n, last), kvb)),
                  pl.BlockSpec((BLK, 256), lambda n: (jnp.maximum(jnp.minimum(n, last) - 1, 0), kvb)),
                  pl.BlockSpec((BLK, ATTN_W), lambda n: (jnp.minimum(n, last), 0)),
                  pl.BlockSpec((None, N_Q, BLK), lambda n: (jnp.minimum(n, last), 0, 0))],
        out_specs=[pl.BlockSpec((BLK, ATTN_W), lambda n: (jnp.minimum(n, last), 0)),
                   pl.BlockSpec((BLK, 256), lambda n: (jnp.maximum(n - 1, 0), 0)),
                   pl.BlockSpec((N_Q, BLK), lambda n: (0, 0))],
        out_shape=[jax.ShapeDtypeStruct((t, ATTN_W), BF16), jax.ShapeDtypeStruct((t, 256), BF16),
                   jax.ShapeDtypeStruct((N_Q, BLK), F32)],
        scratch_shapes=[pltpu.VMEM((BLK, 256), F32)],
        args=[sinks, proj, proj, proj, dattn, lse], semantics=("arbitrary",), exchanges=exchanges)
    return (*res, xres) if exchanges else res


def _cumsum_rows(x):
    n = x.shape[0]
    row = lax.broadcasted_iota(jnp.int32, x.shape, 0)
    s = 1
    while s < n:
        x = x + jnp.where(row >= s, pltpu.roll(x, s, axis=0), 0.0)
        s *= 2
    return x


def _rev_cumsum_rows(x):
    n = x.shape[0]
    row = lax.broadcasted_iota(jnp.int32, x.shape, 0)
    s = 1
    while s < n:
        x = x + jnp.where(row < n - s, pltpu.roll(x, n - s, axis=0), 0.0)
        s *= 2
    return x


def _lower_bound(lbl_ref):
    l0 = lbl_ref[0:1, :]
    l1 = lbl_ref[1:2, :]
    mx = jnp.maximum(l0, l1)
    e0 = jnp.exp(l0 - mx)
    e1 = jnp.exp(l1 - mx)
    return e0 / (e0 + e1)


def _hgrn_gates(z, lb):
    sg = _sigmoid(z)
    f = lb + (1.0 - lb) * sg
    return sg, f, jnp.log(f), 1.0 - f


def _sub_factors(b, i, sub):
    rows = lax.broadcasted_iota(jnp.int32, (CHUNK, RNN_HD), 0)
    ref = b[sub * i - 1:sub * i, :]
    qfac = jnp.exp(b[sub * i:sub * (i + 1), :] - ref)
    kfac = jnp.where(rows < sub * i, jnp.exp(ref - b), 0.0)
    return qfac, kfac


def _diag_decay(bi, s):
    trow = lax.broadcasted_iota(jnp.int32, bi.shape, 0)
    return jnp.where(trow >= s, jnp.exp(bi - bi[s:s + 1, :]), 0.0)


def _hgrn_fwd(proj, lb_logits, norm_gain, *, tb, name, exchanges=()):
    t = proj.shape[0]
    ntb = t // tb
    nch = tb // CHUNK
    qb, fb, ib, gb = QR_COL // 128, FR_COL // 128, IR_COL // 128, GR_COL // 128

    def body(q_ref, f_ref, i_ref, g_ref, lbl_ref, gain_ref, o_ref, out_ref, s0_ref, st_ref):
        c = pl.program_id(1)

        @pl.when(c == 0)
        def _():
            st_ref[...] = jnp.zeros_like(st_ref)

        lb = _lower_bound(lbl_ref)
        gain = gain_ref[...]

        def chunk(ci, st):
            rows = slice(ci * CHUNK, (ci + 1) * CHUNK)
            _, _, lf, k = _hgrn_gates(f_ref[rows, :], lb)
            qr = q_ref[rows, :]
            q = qr * _sigmoid(qr)
            v = i_ref[rows, :]
            b = _cumsum_rows(lf)
            s0_ref[ci] = st
            o_inter = lax.dot_general((q * jnp.exp(b)).astype(BF16), st.astype(BF16), NT,
                                      preferred_element_type=F32)
            vb = v.astype(BF16)
            blast = b[CHUNK - 1:CHUNK, :]
            khat = (k * jnp.exp(blast - b)).astype(BF16)
            st = st * jnp.exp(blast) + lax.dot_general(vb, khat, TN, preferred_element_type=F32)
            blocks = []
            for i in range(CHUNK // SUB_FWD):
                blk = slice(SUB_FWD * i, SUB_FWD * (i + 1))
                qi, ki, vi, bi = q[blk], k[blk], v[blk], b[blk]
                oi = o_inter[blk]
                if i > 0:
                    qfac, kfac = _sub_factors(b, i, SUB_FWD)
                    att = lax.dot_general((qi * qfac).astype(BF16), (k * kfac).astype(BF16), NT,
                                          preferred_element_type=F32)
                    oi = oi + jnp.dot(att.astype(BF16), vb, preferred_element_type=F32)
                for s in range(SUB_FWD):
                    qe = qi * _diag_decay(bi, s)
                    a = jnp.sum(qe * ki[s:s + 1, :], axis=1, keepdims=True)
                    oi = oi + a * vi[s:s + 1, :]
                blocks.append(oi)
            o = jnp.concatenate(blocks, axis=0)
            o_ref[rows, :] = o
            gr = g_ref[rows, :]
            out_ref[rows, :] = o * _rstd(o) * gain * (gr * _sigmoid(gr))
            return st

        st = st_ref[...]
        for ci in range(nch):
            st = chunk(ci, st)
        st_ref[...] = st

    def col(base):
        return pl.BlockSpec((tb, RNN_HD), lambda h, c: (c, base + h))

    res, xres = _call(
        body, name=name, grid=(N_RNN, ntb),
        in_specs=[col(qb), col(fb), col(ib), col(gb),
                  pl.BlockSpec((2, RNN_HD), lambda h, c: (0, h)), pl.BlockSpec((1, RNN_HD), lambda h, c: (0, 0))],
        out_specs=[pl.BlockSpec((tb, RNN_HD), lambda h, c: (c, h)), pl.BlockSpec((tb, RNN_HD), lambda h, c: (c, h)),
                   pl.BlockSpec((None, nch, RNN_HD, RNN_HD), lambda h, c: (h, c, 0, 0))],
        out_shape=[jax.ShapeDtypeStruct((t, RNN_W), F32), jax.ShapeDtypeStruct((t, RNN_W), F32),
                   jax.ShapeDtypeStruct((N_RNN, t // CHUNK, RNN_HD, RNN_HD), F32)],
        scratch_shapes=[pltpu.VMEM((RNN_HD, RNN_HD), F32)],
        args=[proj, proj, proj, proj, lb_logits, norm_gain],
        semantics=("parallel", "arbitrary"), exchanges=exchanges)
    return (*res, xres) if exchanges else res


def _hgrn_bwd(proj, lb_logits, norm_gain, o_pre, s0, dcat, *, tb, name, exchanges=()):
    t = proj.shape[0]
    ntb = t // tb
    nch = tb // CHUNK
    qb, fb, ib, gb = QR_COL // 128, FR_COL // 128, IR_COL // 128, GR_COL // 128
    sub = SUB_BWD
    nsub = CHUNK // sub

    def body(q_ref, f_ref, i_ref, g_ref, lbl_ref, gain_ref, o_ref, s0_ref, dout_ref,
             dq_ref, df_ref, di_ref, dg_ref, dlb_ref, dgain_ref,
             dst_ref, dqs_ref, dks_ref, dvs_ref):
        c = pl.program_id(1)

        @pl.when(c == 0)
        def _():
            dst_ref[...] = jnp.zeros_like(dst_ref)
            dlb_ref[...] = jnp.zeros_like(dlb_ref)
            dgain_ref[...] = jnp.zeros_like(dgain_ref)

        lb = _lower_bound(lbl_ref)
        gain = gain_ref[...]

        def chunk(ci, dst):
            rows = slice(ci * CHUNK, (ci + 1) * CHUNK)
            dqa_ref, dka_ref, dva_ref = dqs_ref.at[ci], dks_ref.at[ci], dvs_ref.at[ci]
            sg, f, lf, k = _hgrn_gates(f_ref[rows, :], lb)
            qr = q_ref[rows, :]
            sq = _sigmoid(qr)
            q = qr * sq
            v = i_ref[rows, :]
            b = _cumsum_rows(lf)

            dout = dout_ref[rows, :].astype(F32)
            o = o_ref[rows, :]
            gr = g_ref[rows, :]
            sgg = _sigmoid(gr)
            gate = gr * sgg
            rs = _rstd(o)
            nrm = o * rs
            dg_ref[rows, :] = (dout * nrm * gain * (sgg * (1.0 + gr * (1.0 - sgg)))).astype(BF16)
            dn = dout * gate
            dgain_ref[...] += jnp.sum(dn * nrm, axis=0, keepdims=True)
            tt = dn * gain
            do = rs * (tt - nrm * jnp.mean(tt * nrm, axis=-1, keepdims=True))

            dob = do.astype(BF16)
            vb = v.astype(BF16)
            eb = jnp.exp(b)
            blast = b[CHUNK - 1:CHUNK, :]
            ebl = jnp.exp(blast - b)
            dstb = dst.astype(BF16)
            khat = (k * ebl).astype(BF16)
            s0 = s0_ref[ci]
            dqa_ref[...] = eb * jnp.dot(dob, s0.astype(BF16), preferred_element_type=F32)
            dk_state = ebl * jnp.dot(vb, dstb, preferred_element_type=F32)
            dka_ref[...] = dk_state
            d_blast = (jnp.sum(k * dk_state, axis=0, keepdims=True)
                       + jnp.exp(blast) * jnp.sum(dst * s0, axis=0, keepdims=True))
            dva_ref[...] = lax.dot_general(khat, dstb, NT, preferred_element_type=F32)
            dst_next = dst * jnp.exp(blast) + lax.dot_general(dob, (q * eb).astype(BF16), TN,
                                                              preferred_element_type=F32)
            pm = lax.dot_general(dob, vb, NT, preferred_element_type=F32)
            for i in range(nsub):
                blk = slice(sub * i, sub * (i + 1))
                qi, ki, vi, bi, doi = q[blk], k[blk], v[blk], b[blk], do[blk]
                dqi = dqa_ref[blk, :]
                if i > 0:
                    qfac, kfac = _sub_factors(b, i, sub)
                    qt = (qi * qfac).astype(BF16)
                    kt = (k * kfac).astype(BF16)
                    att = lax.dot_general(qt, kt, NT, preferred_element_type=F32).astype(BF16)
                    pmi = pm[blk, :].astype(BF16)
                    dva_ref[...] += lax.dot_general(att, doi.astype(BF16), TN, preferred_element_type=F32)
                    dqi = dqi + qfac * jnp.dot(pmi, kt, preferred_element_type=F32)
                    dka_ref[...] += kfac * lax.dot_general(pmi, qt, TN, preferred_element_type=F32)
                dqa_ref[blk, :] = dqi
                srow = lax.broadcasted_iota(jnp.int32, (sub, RNN_HD), 0)
                dki = jnp.zeros((sub, RNN_HD), F32)
                dvi = jnp.zeros((sub, RNN_HD), F32)
                for tq in range(sub):
                    qt, dot_ = qi[tq:tq + 1, :], doi[tq:tq + 1, :]
                    e = jnp.where(srow <= tq, jnp.exp(bi[tq:tq + 1, :] - bi), 0.0)
                    ke = ki * e
                    p = jnp.sum(vi * dot_, axis=1, keepdims=True)
                    a = jnp.sum(ke * qt, axis=1, keepdims=True)
                    dki = dki + p * (qt * e)
                    dvi = dvi + a * dot_
                    row = slice(sub * i + tq, sub * i + tq + 1)
                    dqa_ref[row, :] += jnp.sum(p * ke, axis=0, keepdims=True)
                dka_ref[blk, :] += dki
                dva_ref[blk, :] += dvi

            dq = dqa_ref[...]
            dk = dka_ref[...]
            lastrow = lax.broadcasted_iota(jnp.int32, (CHUNK, RNN_HD), 0) == CHUNK - 1
            dlf = _rev_cumsum_rows(q * dq - k * dk + jnp.where(lastrow, d_blast, 0.0))
            dff = dlf / f - dk
            df_ref[rows, :] = (dff * (1.0 - lb) * sg * (1.0 - sg)).astype(BF16)
            dlb_ref[...] += jnp.sum(dff * (1.0 - sg), axis=0, keepdims=True)
            dq_ref[rows, :] = (dq * (sq * (1.0 + qr * (1.0 - sq)))).astype(BF16)
            di_ref[rows, :] = dva_ref[...].astype(BF16)
            return dst_next

        dst = dst_ref[...]
        for ci in reversed(range(nch)):
            dst = chunk(ci, dst)
        dst_ref[...] = dst

    def col(base):
        return pl.BlockSpec((tb, RNN_HD), lambda h, c: (ntb - 1 - c, base + h))

    outc = pl.BlockSpec((tb, RNN_HD), lambda h, c: (ntb - 1 - c, h))
    hb = ATTN_W // RNN_HD
    res, xres = _call(
        body, name=name, grid=(N_RNN, ntb),
        in_specs=[col(qb), col(fb), col(ib), col(gb),
                  pl.BlockSpec((2, RNN_HD), lambda h, c: (0, h)), pl.BlockSpec((1, RNN_HD), lambda h, c: (0, 0)),
                  outc,
                  pl.BlockSpec((None, nch, RNN_HD, RNN_HD), lambda h, c: (h, ntb - 1 - c, 0, 0)),
                  pl.BlockSpec((tb, RNN_HD), lambda h, c: (ntb - 1 - c, hb + h))],
        out_specs=[outc, outc, outc, outc,
                   pl.BlockSpec((1, RNN_HD), lambda h, c: (0, h)),
                   pl.BlockSpec((None, 1, RNN_HD), lambda h, c: (h, 0, 0))],
        out_shape=[jax.ShapeDtypeStruct((t, RNN_W), BF16)] * 4
        + [jax.ShapeDtypeStruct((1, RNN_W), F32), jax.ShapeDtypeStruct((N_RNN, 1, RNN_HD), F32)],
        scratch_shapes=[pltpu.VMEM((RNN_HD, RNN_HD), F32),
                        pltpu.VMEM((nch, CHUNK, RNN_HD), F32), pltpu.VMEM((nch, CHUNK, RNN_HD), F32),
                        pltpu.VMEM((nch, CHUNK, RNN_HD), F32)],
        args=[proj, proj, proj, proj, lb_logits, norm_gain, o_pre, s0, dcat],
        semantics=("parallel", "arbitrary"), exchanges=exchanges)
    return (*res, xres) if exchanges else res


def _cast_slots(w, where, *, name):
    _, rows, cols = w.shape
    rh = rows // 2
    tr = _row_tile(rh, cols)
    nh = rh // tr

    def body(wh_ref, w_ref, o_ref):
        o_ref[...] = w_ref[...].astype(BF16)

    return pl.pallas_call(
        body, name=name,
        grid_spec=pltpu.PrefetchScalarGridSpec(
            num_scalar_prefetch=1, grid=(2, nh),
            in_specs=[pl.BlockSpec((None, tr, cols), lambda h, i, wh: (0, h * nh + i, 0))],
            out_specs=pl.BlockSpec((None, tr, cols), lambda h, i, wh: (2 * wh[0] + h, i, 0))),
        out_shape=jax.ShapeDtypeStruct((8, rh, cols), BF16),
        compiler_params=_params(("parallel", "parallel")),
    )(where, w)


def _all_gather_halves(bufs, *, name):
    n = len(bufs)

    def body(*refs):
        ins, outs = refs[:n], refs[n:2 * n]
        send_sems, recv_sems = refs[2 * n:]
        x, y, c = _place()
        sibling = (x, y, 1 - c)
        chips = [(1 - x, y), (x, 1 - y), (1 - x, 1 - y)]

        def copy(a, k, block, to, src=None):
            slot = outs[a].at[4 * block[0] + 2 * block[1] + block[2]]
            return pltpu.make_async_remote_copy(
                src_ref=slot if src is None else src, dst_ref=slot,
                send_sem=send_sems.at[a, k], recv_sem=recv_sems.at[a, k],
                device_id=to, device_id_type=MESH)

        first, passed = [], []
        for a in range(n):
            for j, chip in enumerate(chips):
                cp = copy(a, j, (x, y, c), (*chip, c), src=ins[a].at[4 * x + 2 * y + c])
                cp.start()
                first.append(cp)
        for a in range(n):
            for j, chip in enumerate(chips):
                copy(a, j, (*chip, c), (x, y, c)).wait_recv()
                cp = copy(a, 3 + j, (*chip, c), sibling)
                cp.start()
                passed.append(cp)
        for a in range(n):
            for j, chip in enumerate(chips):
                copy(a, 3 + j, (*chip, 1 - c), (x, y, c)).wait_recv()
        for cp in first + passed:
            cp.wait_send()

    return pl.pallas_call(
        body, name=name,
        in_specs=[ANY] * n, out_specs=[ANY] * n,
        out_shape=[jax.ShapeDtypeStruct(b.shape, b.dtype) for b in bufs],
        scratch_shapes=[pltpu.SemaphoreType.DMA((n, 6)), pltpu.SemaphoreType.DMA((n, 6))],
        input_output_aliases={a: a for a in range(n)},
    )(*bufs)


def _row_tile(rows, cols, budget=1 << 20):
    tr = rows
    while tr * cols > budget and tr % 16 == 0:
        tr //= 2
    return tr


def _half_spec(g, tr, halves_last, slab):
    if halves_last:
        return pl.BlockSpec((None, tr, g.shape[2] // 2), lambda *a: (slab(*a), a[-2], a[-1][1]))
    return pl.BlockSpec((None, None, tr, g.shape[3]), lambda *a: (slab(*a), a[-1][1], a[-2], 0))


def _pair_sum(g, sib, where, *, name, halves_last=False):
    rh, cols = sib.shape[1:]
    tr = _row_tile(rh, cols)

    def body(w_ref, g_ref, s_ref, o_ref):
        o_ref[...] = (g_ref[...] + s_ref[...]).astype(BF16)

    return pl.pallas_call(
        body, name=name,
        grid_spec=pltpu.PrefetchScalarGridSpec(
            num_scalar_prefetch=1, grid=(4, rh // tr),
            in_specs=[_half_spec(g, tr, halves_last, lambda s, i, w: s),
                      pl.BlockSpec((None, tr, cols), lambda s, i, w: (s, i, 0))],
            out_specs=pl.BlockSpec((None, tr, cols), lambda s, i, w: (s, i, 0))),
        out_shape=jax.ShapeDtypeStruct((4, rh, cols), BF16),
        compiler_params=_params(("parallel", "parallel")),
    )(where, g, sib)


def _final_half(g, sib, recv, where, *, name, halves_last=False):
    rh, cols = sib.shape[1:]
    tr = _row_tile(rh, cols)

    def body(w_ref, g_ref, s_ref, r_ref, o_ref):
        acc = g_ref[...] + s_ref[...]
        for j in range(3):
            acc = acc + r_ref[j].astype(F32)
        o_ref[...] = acc

    return pl.pallas_call(
        body, name=name,
        grid_spec=pltpu.PrefetchScalarGridSpec(
            num_scalar_prefetch=1, grid=(rh // tr,),
            in_specs=[_half_spec(g, tr, halves_last, lambda i, w: w[0]),
                      pl.BlockSpec((None, tr, cols), lambda i, w: (w[0], i, 0)),
                      pl.BlockSpec((3, tr, cols), lambda i, w: (0, i, 0))],
            out_specs=pl.BlockSpec((tr, cols), lambda i, w: (i, 0))),
        out_shape=jax.ShapeDtypeStruct((rh, cols), F32),
        compiler_params=_params(("parallel",)),
    )(where, g, sib, recv)


def _adamw_math(w, g, m, v):
    m = ADAM_B1 * m + (1.0 - ADAM_B1) * g
    v = ADAM_B2 * v + (1.0 - ADAM_B2) * (g * g)
    m_hat = m / (1.0 - ADAM_B1 ** ADAM_STEP)
    v_hat = v / (1.0 - ADAM_B2 ** ADAM_STEP)
    delta = -ADAM_LR * (m_hat / (jnp.sqrt(v_hat) + ADAM_EPS) + ADAM_WD * w)
    return delta, m, v


def _adamw(w, mine, theirs, m, v, where, *, name, halves_last=False):
    _, rows, cols = w.shape
    if halves_last:
        cols //= 2
        tr = _row_tile(rows, cols, budget=1 << 19)
        grid = (rows // tr, 2)
        blk = pl.BlockSpec((None, tr, cols), lambda i, h, wh: (0, i, h))
        half = pl.BlockSpec((tr, cols), lambda i, h, wh: (i, 0))
        which = lambda: pl.program_id(1)
    else:
        tr = _row_tile(rows // 2, cols, budget=1 << 19)
        nh = rows // 2 // tr
        grid = (rows // tr,)
        blk = pl.BlockSpec((None, tr, cols), lambda i, wh: (0, i, 0))
        half = pl.BlockSpec((tr, cols), lambda i, wh: (i % nh, 0))
        which = lambda: pl.program_id(0) // nh

    def body(wh_ref, w_ref, a_ref, b_ref, m_ref, v_ref, g_ref, d_ref, nm_ref, nv_ref):
        g = jnp.where(which() == wh_ref[1], a_ref[...], b_ref[...])
        d, nm, nv = _adamw_math(w_ref[...], g, m_ref[...], v_ref[...])
        g_ref[...] = g
        d_ref[...] = d
        nm_ref[...] = nm
        nv_ref[...] = nv

    rows, cols = w.shape[1:]
    return pl.pallas_call(
        body, name=name,
        grid_spec=pltpu.PrefetchScalarGridSpec(
            num_scalar_prefetch=1, grid=grid,
            in_specs=[blk, half, half, blk, blk], out_specs=[blk] * 4),
        out_shape=[jax.ShapeDtypeStruct((1, rows, cols), F32)] * 4,
        compiler_params=_params(("parallel",) * len(grid)),
    )(where, w, mine, theirs, m, v)


SEG_LOSS = 0
SEG_SINK = 128
SEG_AGAIN = 256
SEG_L0 = SEG_AGAIN + ATTN_W
SEG_L1 = SEG_L0 + RNN_W
SEG_RGAIN = SEG_L1 + RNN_W
SEG_G = SEG_RGAIN + 128
N_PACK = SEG_G + 4 * D_MODEL


def _pack(sinks, again, l0, l1, rgain, gains, loss=None):
    z = lambda k: jnp.zeros((1, k), F32)
    first = z(128) if loss is None else loss
    return jnp.concatenate([first, sinks, z(128 - N_Q), again, l0, l1, rgain] + list(gains), axis=1)


def _small_reduce_adamw(part, w, m, v, *, name):
    def body(p_ref, w_ref, m_ref, v_ref, g_ref, d_ref, nm_ref, nv_ref, buf_ref, send_sems, recv_sems):
        x, y, c = _place()
        me = 4 * x + 2 * y + c
        copies = []
        for k in range(1, 8):
            dx, dy, dc = (k >> 2) & 1, (k >> 1) & 1, k & 1
            to = (x ^ dx, y ^ dy, c ^ dc)
            cp = pltpu.make_async_remote_copy(
                src_ref=p_ref, dst_ref=buf_ref.at[me],
                send_sem=send_sems.at[k - 1], recv_sem=recv_sems.at[k - 1],
                device_id=to, device_id_type=MESH)
            cp.start()
            copies.append(cp)
        buf_ref[me] = p_ref[...]
        for cp in copies:
            cp.wait()
        tot = buf_ref[0]
        for j in range(1, 8):
            tot = tot + buf_ref[j]
        g_ref[...] = tot
        l0 = w_ref[:, SEG_L0:SEG_L0 + RNN_W]
        l1 = w_ref[:, SEG_L1:SEG_L1 + RNN_W]
        mx = jnp.maximum(l0, l1)
        e0 = jnp.exp(l0 - mx)
        e1 = jnp.exp(l1 - mx)
        lb = e0 / (e0 + e1)
        gl0 = tot[:, SEG_L0:SEG_L0 + RNN_W] * lb * (1.0 - lb)
        g_ref[:, SEG_L0:SEG_L0 + RNN_W] = gl0
        g_ref[:, SEG_L1:SEG_L1 + RNN_W] = -gl0
        d, nm, nv = _adamw_math(w_ref[...], g_ref[...], m_ref[...], v_ref[...])
        d_ref[...] = d
        nm_ref[...] = nm
        nv_ref[...] = nv

    vm = pl.BlockSpec(memory_space=pltpu.VMEM)
    return pl.pallas_call(
        body, name=name,
        in_specs=[vm] * 4, out_specs=[vm] * 4,
        out_shape=[jax.ShapeDtypeStruct((1, N_PACK), F32)] * 4,
        scratch_shapes=[pltpu.VMEM((8, 1, N_PACK), F32), pltpu.SemaphoreType.DMA((7,)),
                        pltpu.SemaphoreType.DMA((7,))],
    )(part, w, m, v)


def _layer_grads(xs, tgt, bufs, where, sinks, again, lb_logits, rgain,
                 g_mix_pre, g_mix_post, g_mlp_pre, g_mlp_post):
    tm = 512
    b_in, b_out, b_up, b_dn = bufs

    shard = IN_W // N_CHIPS
    w_in_t = _all_gather_halves([b_in], name="gather_w_in")[0].reshape(IN_W, D_MODEL)
    h1 = _rms_cast(xs, g_mix_pre, tm=tm, name="h1_norm")
    proj, ((b_out, b_up),) = _mm(
        h1, w_in_t, tm=1024, tn=768, tk=D_MODEL, out_dtype=F32, w_layout="nk", name="in_proj",
        exchanges=[_x_gather([b_out, b_up], ici=[(0, 256), (0, 384)])])
    attn, lse, ((b_out, b_up),) = _swa_fwd(
        proj, sinks, name="swa_fwd",
        exchanges=[_x_gather([b_out, b_up], ici=[None, (384, 320)], d2d=[(0, 256), None])])
    w_out = b_out.reshape(D_MODEL, D_MODEL)
    o_pre, rnn, s0, ((b_up, b_dn),) = _hgrn_fwd(
        proj, lb_logits, rgain, tb=512, name="hgrn_fwd",
        exchanges=[_x_gather([b_up, b_dn], ici=[(704, 320), (0, 608)])])
    cat = _mix_cat(attn, rnn, again, tm=tm, name="mix_cat")
    mixed, ((b_up, b_dn),) = _mm(
        cat, w_out, tm=1024, tn=1024, tk=D_MODEL, out_dtype=F32, name="out_proj",
        exchanges=[_x_gather([b_up, b_dn], ici=[None, (608, 256)], d2d=[(0, 1024), (0, 608)])])
    w_up4 = b_up.reshape(N_CHIPS, D_MODEL, D_FF // N_CHIPS)
    x1, h2, ((b_dn,),) = _post_norm_res(
        mixed, g_mix_post, xs, g_mlp_pre, tm=256, name="mix_post",
        exchanges=[_x_gather([b_dn], ici=[(864, 160)], d2d=[(608, 256)])])
    u, ((b_dn,),) = _mm(h2, w_up4, tm=1024, tn=1024, tk=D_MODEL, out_dtype=BF16, relu=True, w_layout="skn",
                        name="mlp_up", exchanges=[_x_gather([b_dn], d2d=[(864, 160)])])
    w_dn = b_dn.reshape(D_FF, D_MODEL)
    yv = _mm(u, w_dn, tm=1024, tn=1024, tk=2048, out_dtype=F32, a_square=True, name="mlp_down")
    dy, dx2, loss_row, dg_mlp_post = _loss_head(yv, g_mlp_post, x1, tgt, tm=256, name="loss_head")

    def halved(g):
        return g.reshape(N_CHIPS, 2, g.shape[1] // 2, g.shape[2])
    du = _mm(dy, w_dn, tm=1024, tn=1024, tk=D_MODEL, out_dtype=BF16, mul2=u, w_layout="nk", name="mlp_down_bwd")
    g_dn = halved(_mm_tn(u, dy, tm=1024, tn=1024, tt=2048, a_square=True, name="w_down_grad")
                  .reshape(N_CHIPS, D_FF // N_CHIPS, D_MODEL))
    d_w_up, ((sib_dn,),) = _mm_tn(h2, du, tm=1024, tn=1024, tt=2048, n_split=N_CHIPS, name="w_up_grad",
                                  exchanges=[_x_pair([g_dn])])
    g_up = halved(d_w_up)
    wire_dn = _pair_sum(g_dn, sib_dn, where, name="pair_sum_w_down")
    dh2, ((recv_dn,), (sib_up,)) = _mm(du, w_up4, tm=1024, tn=1024, tk=2048, out_dtype=BF16, w_layout="snk", name="mlp_up_bwd",
                                       exchanges=[_x_chip([wire_dn], rows=[(0, 800)]), _x_pair([g_up])])
    wire_up = _pair_sum(g_up, sib_up, where, name="pair_sum_w_up")
    dx1, dg_mlp_pre, ((recv_dn,),) = _rms_bwd(dh2, x1, g_mlp_pre, dx2, tm=256, out_dtype=F32, name="mlp_pre_bwd",
                                              exchanges=[_x_chip([wire_dn], rows=[(800, 224)], into=[recv_dn])])
    fin_dn = _final_half(g_dn, sib_dn, recv_dn, where, name="final_half_w_down")
    dmixed, dg_mix_post = _rms_bwd(dx1, mixed, g_mix_post, None, tm=256, out_dtype=BF16, name="mix_post_bwd")
    d_w_out, ((oth_dn,),) = _mm_tn(cat, dmixed, tm=1024, tn=1024, tt=2048, name="w_out_grad",
                                   exchanges=[_x_share([fin_dn])])
    g_out = halved(d_w_out.reshape(N_CHIPS, D_MODEL // N_CHIPS, D_MODEL))
    dcat, ((sib_out,),) = _mm(dmixed, w_out, tm=1024, tn=1024, tk=D_MODEL, out_dtype=BF16, w_layout="nk", name="out_proj_bwd",
                              exchanges=[_x_pair([g_out])])
    wire_out = _pair_sum(g_out, sib_out, where, name="pair_sum_w_out")
    dattn, dg_again = _rms_bwd(dcat, attn, again, None, tm=tm, out_dtype=BF16, name="attn_norm_bwd")
    dq_a, dkv, dsinks, ((recv_out,), (recv_up,)) = _swa_bwd(
        proj, sinks, dattn, lse, name="swa_bwd",
        exchanges=[_x_chip([wire_out]), _x_chip([wire_up], rows=[(0, 320)])])
    dq_r, df_r, di_r, dg_r, dlb, dgain_h, ((recv_up,),) = _hgrn_bwd(
        proj, lb_logits, rgain, o_pre, s0, dcat, tb=512, name="hgrn_bwd",
        exchanges=[_x_chip([wire_up], rows=[(320, 704)], into=[recv_up])])
    fin_up = _final_half(g_up, sib_up, recv_up, where, name="final_half_w_up")
    fin_out = _final_half(g_out, sib_out, recv_out, where, name="final_half_w_out")
    dproj = jnp.concatenate([dq_a, dkv, dq_r, df_r, di_r, dg_r], axis=1)
    piece_cols = D_MODEL // 4

    def w_in_piece(pc, exchanges):
        d, xres = _mm_tn(dproj, h1, tm=896, tn=piece_cols, tt=2048, n_blocks=(2, 2, pc),
                         name="w_in_grad_%d" % pc, exchanges=exchanges)
        return d.reshape(N_CHIPS, shard, 2 * piece_cols), xres

    g_in0, ((oth_up, oth_out),) = w_in_piece(0, [_x_share([fin_up, fin_out])])
    g_in1, ((sib_in0,),) = w_in_piece(1, [_x_pair([g_in0], halves_last=True)])
    wire_in0 = _pair_sum(g_in0, sib_in0, where, name="pair_sum_w_in_0", halves_last=True)
    dh1, ((recv_in0,), (sib_in1,)) = _mm(
        dproj, w_in_t, tm=1024, tn=1024, tk=2688, out_dtype=BF16, m_blocks=(0, 2), name="in_proj_bwd_0",
        exchanges=[_x_chip([wire_in0]), _x_pair([g_in1], halves_last=True)])
    wire_in1 = _pair_sum(g_in1, sib_in1, where, name="pair_sum_w_in_1", halves_last=True)
    dh1, ((recv_in1,),) = _mm(
        dproj, w_in_t, tm=1024, tn=1024, tk=2688, out_dtype=BF16, m_blocks=(2, 2), out_into=dh1,
        name="in_proj_bwd_1", exchanges=[_x_chip([wire_in1])])
    gx, dg_mix_pre = _rms_bwd(dh1, xs, g_mix_pre, dx1, tm=256, out_dtype=F32, name="mix_pre_bwd")
    fin_in0 = _final_half(g_in0, sib_in0, recv_in0, where, name="final_half_w_in_0", halves_last=True)
    fin_in1 = _final_half(g_in1, sib_in1, recv_in1, where, name="final_half_w_in_1", halves_last=True)
    oth_in0, oth_in1 = _run_exchange(_x_share([fin_in0, fin_in1]), name="share_w_in")
    fin_in = jnp.concatenate([fin_in0, fin_in1], axis=1)
    oth_in = jnp.concatenate([oth_in0, oth_in1], axis=1)

    big = [(fin_in, oth_in), (fin_out, oth_out), (fin_up, oth_up), (fin_dn, oth_dn)]
    drgain = jnp.sum(dgain_h, axis=0)
    small = _pack(jnp.sum(dsinks, axis=1)[None, :], dg_again, dlb, jnp.zeros_like(dlb), drgain,
                  [dg_mix_pre, dg_mix_post, dg_mlp_pre, dg_mlp_post], loss=loss_row)
    return gx, big, small


def kernel(x, w_in, attn_sinks, attn_out_gain, rnn_lb_logits, rnn_norm_gain, w_out, mix_pre_gain, mix_post_gain, mlp_pre_gain, mlp_post_gain, w_up, w_down, loss_target, m_w_in, m_attn_sinks, m_attn_out_gain, m_rnn_lb_logits, m_rnn_norm_gain, m_w_out, m_mix_pre_gain, m_mix_post_gain, m_mlp_pre_gain, m_mlp_post_gain, m_w_up, m_w_down, v_w_in, v_attn_sinks, v_attn_out_gain, v_rnn_lb_logits, v_rnn_norm_gain, v_w_out, v_mix_pre_gain, v_mix_post_gain, v_mlp_pre_gain, v_mlp_post_gain, v_w_up, v_w_down):
    ax, ay, ac = _place()
    where = jnp.stack([2 * ax + ay, ac]).astype(jnp.int32)
    t = lambda a: jnp.swapaxes(a, 1, 2)
    big_w = [t(w_in), w_out, w_up, w_down]
    big_m = [t(m_w_in), m_w_out, m_w_up, m_w_down]
    big_v = [t(v_w_in), v_w_out, v_w_up, v_w_down]

    names = ["w_in", "w_out", "w_up", "w_down"]
    bufs = [_cast_slots(w, where, name="cast_" + nm) for w, nm in zip(big_w, names)]
    gx, big_g, small_part = _layer_grads(
        x[0], loss_target[0], bufs, where, attn_sinks, attn_out_gain, rnn_lb_logits, rnn_norm_gain,
        mix_pre_gain, mix_post_gain, mlp_pre_gain, mlp_post_gain)

    grads, deltas, new_m, new_v = [], [], [], []
    for (f, o), w, m, v, nm in zip(big_g, big_w, big_m, big_v, names):
        res = _adamw(w, f, o, m, v, where, name="adamw_" + nm, halves_last=(nm == "w_in"))
        if nm == "w_in":
            res = [t(r) for r in res]
        g, d, nm_, nv_ = res
        grads.append(g)
        deltas.append(d)
        new_m.append(nm_)
        new_v.append(nv_)

    def pack_params(sinks, again, logits, rgain, gains):
        return _pack(sinks, again, logits[0:1], logits[1:2], rgain, gains)

    pw = pack_params(attn_sinks, attn_out_gain, rnn_lb_logits, rnn_norm_gain,
                     [mix_pre_gain, mix_post_gain, mlp_pre_gain, mlp_post_gain])
    pm = pack_params(m_attn_sinks, m_attn_out_gain, m_rnn_lb_logits, m_rnn_norm_gain,
                     [m_mix_pre_gain, m_mix_post_gain, m_mlp_pre_gain, m_mlp_post_gain])
    pv = pack_params(v_attn_sinks, v_attn_out_gain, v_rnn_lb_logits, v_rnn_norm_gain,
                     [v_mix_pre_gain, v_mix_post_gain, v_mlp_pre_gain, v_mlp_post_gain])
    packs = _small_reduce_adamw(small_part, pw, pm, pv, name="small_reduce_adamw")

    def unpack(p):
        seg = lambda o, k: p[:, o:o + k]
        logits = jnp.concatenate([seg(SEG_L0, RNN_W), seg(SEG_L1, RNN_W)], axis=0)
        gains = [seg(SEG_G + i * D_MODEL, D_MODEL) for i in range(4)]
        return dict(sinks=seg(SEG_SINK, N_Q), again=seg(SEG_AGAIN, ATTN_W), logits=logits,
                    rgain=seg(SEG_RGAIN, RNN_HD), gains=gains)

    def order(small, big):
        return [big[0], small["sinks"], small["again"], small["logits"], small["rgain"], big[1],
                *small["gains"], big[2], big[3]]

    loss = packs[0][0, 0]
    outs = [loss, gx[None]]
    for p, b in zip(packs, [grads, deltas, new_m, new_v]):
        outs += order(unpack(p), b)
    return tuple(outs)
```

```python
import functools

import jax
import jax.numpy as jnp
from jax import lax
from jax.experimental import pallas as pl
from jax.experimental.pallas import tpu as pltpu

F32 = jnp.float32
BF16 = jnp.bfloat16
MESH = pl.DeviceIdType.MESH

EPS = 1e-6
D_MODEL = 2048
ATTN_W = 1024
HEAD_DIM = 64
N_Q = 16
N_KV = 2
GROUP = 8
BLK = 128
RNN_W = 1024
RNN_HD = 128
N_RNN = 8
CHUNK = 64
SUB_FWD = 16
SUB_BWD = 8
D_FF = 8192
IN_W = 5376
N_CHIPS = 4
KV_COL = ATTN_W
QR_COL = ATTN_W + 2 * 128
FR_COL = QR_COL + RNN_W
IR_COL = FR_COL + RNN_W
GR_COL = IR_COL + RNN_W

ADAM_LR = 0.001
ADAM_B1 = 0.9
ADAM_B2 = 0.999
ADAM_EPS = 1e-08
ADAM_WD = 0.01
ADAM_STEP = 10

VMEM_LIMIT = 48 * 1024 * 1024

NT = (((1,), (1,)), ((), ()))
TN = (((0,), (0,)), ((), ()))


def _params(sem=None):
    return pltpu.CompilerParams(dimension_semantics=sem, vmem_limit_bytes=VMEM_LIMIT)


def _sigmoid(x):
    return 1.0 / (1.0 + jnp.exp(-x))


ANY = pl.BlockSpec(memory_space=pl.ANY)


def _place():
    return lax.axis_index("x"), lax.axis_index("y"), lax.axis_index("c")


def _other_chips(x, y):
    return [(1 - x, y), (x, 1 - y), (1 - x, 1 - y)]


class _Exchange:
    def __init__(self, srcs, outs, ncopy, build, aliases=None):
        self.srcs, self.outs, self.ncopy, self.build = list(srcs), list(outs), ncopy, build
        self.aliases = aliases or {}


def _remote(src, dst, send_sems, recv_sems, k, to):
    return pltpu.make_async_remote_copy(src_ref=src, dst_ref=dst, send_sem=send_sems.at[k],
                                        recv_sem=recv_sems.at[k], device_id=to, device_id_type=MESH)


def _call(body, *, name, grid, in_specs, out_specs, out_shape, args, scratch_shapes=(), semantics=None,
          exchanges=(), into=None):
    in_specs, out_specs, out_shape = list(in_specs), list(out_specs), list(out_shape)
    scratch_shapes = list(scratch_shapes)
    ni, no, ns = len(in_specs), len(out_specs), len(scratch_shapes)
    xsrc = [s for x in exchanges for s in x.srcs]
    xout = [o for x in exchanges for o in x.outs]
    into = into or {}
    xsrc += [into[k] for k in sorted(into)]
    nxi, nxo = len(xsrc), len(xout)
    aliases = {nxi - len(into) + ni + q: k for q, k in enumerate(sorted(into))}
    a0 = b0 = 0
    for x in exchanges:
        for si, oi in x.aliases.items():
            aliases[ni + a0 + si] = no + b0 + oi
        a0 += len(x.srcs)
        b0 += len(x.outs)
    sems = []
    for x in exchanges:
        sems += [pltpu.SemaphoreType.DMA((x.ncopy,)), pltpu.SemaphoreType.DMA((x.ncopy,))]

    def wrapped(*refs):
        ins, xi = refs[:ni], refs[ni:ni + nxi]
        outs, xo = refs[ni + nxi:ni + nxi + no], refs[ni + nxi + no:ni + nxi + no + nxo]
        rest = refs[ni + nxi + no + nxo:]
        scr, sm = rest[:ns], rest[ns:]

        def copies():
            cps = []
            a = b = 0
            for k, x in enumerate(exchanges):
                cps += x.build(xi[a:a + len(x.srcs)], xo[b:b + len(x.outs)], sm[2 * k], sm[2 * k + 1])
                a += len(x.srcs)
                b += len(x.outs)
            return cps

        def start():
            for cp in copies():
                cp.start()

        def wait():
            for cp in copies():
                cp.wait()

        if not exchanges:
            body(*ins, *outs, *scr)
        elif not grid:
            start()
            body(*ins, *outs, *scr)
            wait()
        else:
            first = last = None
            for ax, g in enumerate(grid):
                f = pl.program_id(ax) == 0
                l = pl.program_id(ax) == g - 1
                first = f if first is None else first & f
                last = l if last is None else last & l
            pl.when(first)(start)
            body(*ins, *outs, *scr)
            pl.when(last)(wait)

    if exchanges and semantics is not None:
        semantics = ("arbitrary",) * len(grid)
    kwargs = dict(grid=grid) if grid else {}
    res = pl.pallas_call(
        wrapped, name=name,
        in_specs=in_specs + [ANY] * nxi, out_specs=out_specs + [ANY] * nxo,
        out_shape=out_shape + xout, scratch_shapes=scratch_shapes + sems,
        input_output_aliases=aliases,
        compiler_params=_params(semantics), **kwargs,
    )(*args, *xsrc)
    res = list(res)
    mine, theirs = res[:no], res[no:]
    per = []
    b = 0
    for x in exchanges:
        per.append(theirs[b:b + len(x.outs)])
        b += len(x.outs)
    return mine, per


def _run_exchange(x, *, name):
    return _call(lambda: None, name=name, grid=(), in_specs=[], out_specs=[], out_shape=[], args=[],
                 exchanges=[x])[1][0]


def _x_gather(bufs, ici=None, d2d=None):
    n = len(bufs)
    plan = [(a, kind, rows[a]) for a in range(n) for kind, rows in (("ici", ici), ("d2d", d2d))
            if rows is not None and rows[a] is not None]

    def build(srcs, outs, ss, rs):
        x, y, c = _place()
        cps = []
        for q, (a, kind, rows) in enumerate(plan):
            piece = pl.ds(*rows)
            for j, (px, py) in enumerate(_other_chips(x, y)):
                slot, to = (4 * x + 2 * y + c, (px, py, c)) if kind == "ici" else (4 * px + 2 * py + c, (x, y, 1 - c))
                cps.append(_remote(srcs[a].at[slot, piece], outs[a].at[slot, piece], ss, rs, 3 * q + j, to))
        return cps

    outs = [jax.ShapeDtypeStruct(b.shape, b.dtype) for b in bufs]
    return _Exchange(bufs, outs, 3 * len(plan), build, aliases={a: a for a in range(n)})


def _x_pair(grads, halves_last=False):
    n = len(grads)

    def build(srcs, outs, ss, rs):
        x, y, c = _place()

        def half(r):
            if not halves_last:
                return r.at[:, 1 - c]
            ch = r.shape[2] // 2
            return r.at[:, :, pl.ds(pl.multiple_of((1 - c) * ch, 128), ch)]

        return [_remote(half(srcs[a]), outs[a], ss, rs, a, (x, y, 1 - c)) for a in range(n)]

    if halves_last:
        outs = [jax.ShapeDtypeStruct(g.shape[:2] + (g.shape[2] // 2,), g.dtype) for g in grads]
    else:
        outs = [jax.ShapeDtypeStruct((4,) + g.shape[2:], g.dtype) for g in grads]
    return _Exchange(grads, outs, n, build)


def _x_chip(wires, rows=None, into=None):
    n = len(wires)
    rows = rows or [(0, w.shape[1]) for w in wires]

    def build(srcs, outs, ss, rs):
        x, y, c = _place()
        cps = []
        for a in range(n):
            piece = pl.ds(*rows[a])
            for j, (px, py) in enumerate(_other_chips(x, y)):
                cps.append(_remote(srcs[a].at[2 * px + py, piece], outs[a].at[j, piece], ss, rs,
                                   3 * a + j, (px, py, c)))
        return cps

    outs = [jax.ShapeDtypeStruct((3,) + w.shape[1:], w.dtype) for w in wires]
    if into is None:
        return _Exchange(wires, outs, 3 * n, build)
    return _Exchange(list(wires) + list(into), outs, 3 * n, build, aliases={n + a: a for a in range(n)})


def _x_share(halves):
    n = len(halves)

    def build(srcs, outs, ss, rs):
        x, y, c = _place()
        return [_remote(srcs[a], outs[a], ss, rs, a, (x, y, 1 - c)) for a in range(n)]

    outs = [jax.ShapeDtypeStruct(h.shape, h.dtype) for h in halves]
    return _Exchange(halves, outs, n, build)


def _mm(a, w, *, tm, tn, tk, out_dtype, name, a_square=False, relu=False, mul2=None, w_layout="kn",
        m_blocks=None, out_into=None, exchanges=()):
    m, k = a.shape
    m_first, m_count = m_blocks or (0, m // tm)
    a_spec = pl.BlockSpec((tm, tk), lambda i, j, kk: (i + m_first, kk))
    if w_layout == "kn":
        n = w.shape[1]
        w_spec = pl.BlockSpec((tk, tn), lambda i, j, kk: (kk, j))
    elif w_layout == "nk":
        n = w.shape[0]
        w_spec = pl.BlockSpec((tn, tk), lambda i, j, kk: (j, kk))
    elif w_layout == "skn":
        n = w.shape[0] * w.shape[2]
        per_n = w.shape[2] // tn
        w_spec = pl.BlockSpec((None, tk, tn), lambda i, j, kk: (j // per_n, kk, j % per_n))
    else:
        assert w_layout == "snk"
        n = w.shape[1]
        per_k = w.shape[2] // tk
        w_spec = pl.BlockSpec((None, tn, tk), lambda i, j, kk: (kk // per_k, j, kk % per_k))
    w_dims = NT if w_layout in ("nk", "snk") else (((1,), (0,)), ((), ()))
    nk = k // tk
    assert m % tm == 0 and n % tn == 0 and k % tk == 0

    def body(*refs):
        if mul2 is not None:
            a_ref, w_ref, e_ref, o_ref, acc_ref = refs
        else:
            a_ref, w_ref, o_ref, acc_ref = refs
            e_ref = None
        kk = pl.program_id(2)
        av = a_ref[...]
        if a_square:
            af = av.astype(F32)
            av = (af * af).astype(BF16)
        part = lax.dot_general(av, w_ref[...], w_dims, preferred_element_type=F32)

        def finish(r):
            if relu:
                r = jnp.maximum(r, 0.0)
            if e_ref is not None:
                r = 2.0 * e_ref[...].astype(F32) * r
            o_ref[...] = r.astype(out_dtype)

        if nk == 1:
            finish(part)
        else:
            @pl.when(kk == 0)
            def _():
                acc_ref[...] = part

            @pl.when(kk > 0)
            def _():
                acc_ref[...] += part

            @pl.when(kk == nk - 1)
            def _():
                finish(acc_ref[...])

    in_specs = [a_spec, w_spec]
    args = [a, w]
    if mul2 is not None:
        in_specs.append(pl.BlockSpec((tm, tn), lambda i, j, kk: (i + m_first, j)))
        args.append(mul2)
    acc_shape = (tm, tn) if nk > 1 else (8, 128)
    (out,), per = _call(
        body, name=name, grid=(m_count, n // tn, nk),
        in_specs=in_specs, out_specs=[pl.BlockSpec((tm, tn), lambda i, j, kk: (i + m_first, j))],
        out_shape=[jax.ShapeDtypeStruct((m, n), out_dtype)], args=args,
        scratch_shapes=[pltpu.VMEM(acc_shape, F32)],
        semantics=("parallel", "parallel", "arbitrary"), exchanges=exchanges,
        into=None if out_into is None else {0: out_into})
    return (out, per) if exchanges else out


def _mm_tn(a, b, *, tm, tn, tt, name, a_square=False, n_split=1, n_blocks=None, exchanges=()):
    t, m = a.shape
    n = b.shape[1]
    assert t % tt == 0 and m % tm == 0 and n % tn == 0
    count, stride, first = n_blocks or (n // tn, 1, 0)
    n = count * tn
    assert (n // n_split) % tn == 0
    per = n // n_split // tn

    def body(a_ref, b_ref, o_ref):
        ti = pl.program_id(2)
        av = a_ref[...]
        if a_square:
            af = av.astype(F32)
            av = (af * af).astype(BF16)
        part = lax.dot_general(av, b_ref[...], TN, preferred_element_type=F32)

        @pl.when(ti == 0)
        def _():
            o_ref[...] = part

        @pl.when(ti > 0)
        def _():
            o_ref[...] += part

    (out,), xres = _call(
        body, name=name, grid=(m // tm, n // tn, t // tt),
        in_specs=[pl.BlockSpec((tt, tm), lambda i, j, ti: (ti, i)),
                  pl.BlockSpec((tt, tn), lambda i, j, ti: (ti, first + stride * j))],
        out_specs=[pl.BlockSpec((None, tm, tn), lambda i, j, ti: (j // per, i, j % per))],
        out_shape=[jax.ShapeDtypeStruct((n_split, m, n // n_split), F32)], args=[a, b],
        semantics=("parallel", "parallel", "arbitrary"), exchanges=exchanges)
    return (out, xres) if exchanges else out


def _rstd(x):
    return lax.rsqrt(jnp.mean(x * x, axis=-1, keepdims=True) + EPS)


def _rms_cast(x, g, *, tm, name):
    t, d = x.shape

    def body(x_ref, g_ref, o_ref):
        xv = x_ref[...]
        o_ref[...] = (xv * _rstd(xv) * g_ref[...]).astype(BF16)

    return pl.pallas_call(
        body, name=name, grid=(t // tm,),
        in_specs=[pl.BlockSpec((tm, d), lambda i: (i, 0)), pl.BlockSpec((1, d), lambda i: (0, 0))],
        out_specs=pl.BlockSpec((tm, d), lambda i: (i, 0)),
        out_shape=jax.ShapeDtypeStruct((t, d), BF16),
        compiler_params=_params(("parallel",)),
    )(x, g)


def _mix_cat(attn, rnn, gain, *, tm, name):
    t = attn.shape[0]

    def body(a_ref, r_ref, g_ref, o_ref):
        av = a_ref[...]
        o_ref[:, :ATTN_W] = (av * _rstd(av) * g_ref[...]).astype(BF16)
        o_ref[:, ATTN_W:] = r_ref[...].astype(BF16)

    return pl.pallas_call(
        body, name=name, grid=(t // tm,),
        in_specs=[pl.BlockSpec((tm, ATTN_W), lambda i: (i, 0)), pl.BlockSpec((tm, RNN_W), lambda i: (i, 0)),
                  pl.BlockSpec((1, ATTN_W), lambda i: (0, 0))],
        out_specs=pl.BlockSpec((tm, D_MODEL), lambda i: (i, 0)),
        out_shape=jax.ShapeDtypeStruct((t, D_MODEL), BF16),
        compiler_params=_params(("parallel",)),
    )(attn, rnn, gain)


def _post_norm_res(mixed, g_post, res, g_next, *, tm, name, exchanges=()):
    t, d = mixed.shape

    def body(m_ref, gp_ref, r_ref, gn_ref, x1_ref, h2_ref):
        mv = m_ref[...].astype(F32)
        x1 = r_ref[...] + mv * _rstd(mv) * gp_ref[...]
        x1_ref[...] = x1
        h2_ref[...] = (x1 * _rstd(x1) * gn_ref[...]).astype(BF16)

    row = pl.BlockSpec((tm, d), lambda i: (i, 0))
    vec = pl.BlockSpec((1, d), lambda i: (0, 0))
    res_, xres = _call(
        body, name=name, grid=(t // tm,),
        in_specs=[row, vec, row, vec], out_specs=[row, row],
        out_shape=[jax.ShapeDtypeStruct((t, d), F32), jax.ShapeDtypeStruct((t, d), BF16)],
        args=[mixed, g_post, res, g_next], semantics=("parallel",), exchanges=exchanges)
    return (*res_, xres) if exchanges else res_


def _rms_bwd(dyn, xin, g, res, *, tm, out_dtype, name, col_block=0, exchanges=()):
    t, d = xin.shape

    def body(*refs):
        if res is not None:
            dy_ref, x_ref, g_ref, r_ref, dx_ref, dg_ref = refs
        else:
            dy_ref, x_ref, g_ref, dx_ref, dg_ref = refs
        i = pl.program_id(0)
        xv = x_ref[...].astype(F32)
        dy = dy_ref[...].astype(F32)
        r = _rstd(xv)
        xh = xv * r
        part = jnp.sum(dy * xh, axis=0, keepdims=True)

        @pl.when(i == 0)
        def _():
            dg_ref[...] = part

        @pl.when(i > 0)
        def _():
            dg_ref[...] += part

        tt = dy * g_ref[...]
        dx = r * (tt - xh * jnp.mean(tt * xh, axis=-1, keepdims=True))
        if res is not None:
            dx = dx + r_ref[...]
        dx_ref[...] = dx.astype(out_dtype)

    row = pl.BlockSpec((tm, d), lambda i: (i, 0))
    vec = pl.BlockSpec((1, d), lambda i: (0, 0))
    in_specs = [pl.BlockSpec((tm, d), lambda i: (i, col_block)), row, vec]
    args = [dyn, xin, g]
    if res is not None:
        in_specs.append(row)
        args.append(res)
    res, xres = _call(
        body, name=name, grid=(t // tm,),
        in_specs=in_specs, out_specs=[row, vec],
        out_shape=[jax.ShapeDtypeStruct((t, d), out_dtype), jax.ShapeDtypeStruct((1, d), F32)], args=args,
        semantics=("arbitrary",), exchanges=exchanges)
    return (*res, xres) if exchanges else res


def _loss_head(y, g_post, x1, target, *, tm, name):
    t, d = y.shape

    def body(y_ref, g_ref, x1_ref, t_ref, dy_ref, dx2_ref, loss_ref, dg_ref):
        i = pl.program_id(0)
        yv = y_ref[...].astype(F32)
        r = _rstd(yv)
        yh = yv * r
        gv = g_ref[...]
        err = x1_ref[...] + yh * gv - t_ref[...]
        lpart = 0.5 * jnp.sum(jnp.mean(err * err, axis=-1, keepdims=True), axis=0, keepdims=True)
        dx2 = err * (1.0 / d)
        dgp = jnp.sum(dx2 * yh, axis=0, keepdims=True)
        lane = lax.broadcasted_iota(jnp.int32, (1, 128), 1)
        lrow = jnp.where(lane == 0, lpart, 0.0)

        @pl.when(i == 0)
        def _():
            dg_ref[...] = dgp
            loss_ref[...] = lrow

        @pl.when(i > 0)
        def _():
            dg_ref[...] += dgp
            loss_ref[...] += lrow

        tt = dx2 * gv
        dy_ref[...] = (r * (tt - yh * jnp.mean(tt * yh, axis=-1, keepdims=True))).astype(BF16)
        dx2_ref[...] = dx2

    row = pl.BlockSpec((tm, d), lambda i: (i, 0))
    vec = pl.BlockSpec((1, d), lambda i: (0, 0))
    return pl.pallas_call(
        body, name=name, grid=(t // tm,),
        in_specs=[row, vec, row, row],
        out_specs=[row, row, pl.BlockSpec((1, 128), lambda i: (0, 0)), vec],
        out_shape=[jax.ShapeDtypeStruct((t, d), BF16), jax.ShapeDtypeStruct((t, d), F32),
                   jax.ShapeDtypeStruct((1, 128), F32), jax.ShapeDtypeStruct((1, d), F32)],
        compiler_params=_params(("arbitrary",)),
    )(y, g_post, x1, target)


def _alibi_slope(h):
    return 2.0 ** (-8.0 * (h + 1) / N_Q)


PAIR = 2 * HEAD_DIM
N_PAIRS = N_Q // 2
PAIRS_PER_KV = GROUP // 2
SMEM = pl.BlockSpec(memory_space=pltpu.SMEM)


def _swa_mask(n):
    key = lax.broadcasted_iota(jnp.int32, (2 * BLK, BLK), 0)
    qry = lax.broadcasted_iota(jnp.int32, (2 * BLK, BLK), 1)
    dist = qry + BLK - key
    valid = (dist >= 0) & (dist < BLK) & ((key >= BLK) | (n > 0))
    return valid, dist.astype(F32)


def _block_diag(kvp_ref, kvc_ref, off):
    a = jnp.concatenate([kvp_ref[:, off:off + HEAD_DIM], kvc_ref[:, off:off + HEAD_DIM]], axis=0).astype(BF16)
    z = jnp.zeros_like(a)
    return jnp.concatenate([jnp.concatenate([a, z], axis=1), jnp.concatenate([z, a], axis=1)], axis=0)


def _swa_scores(s2, e, hh, valid, distf):
    s = s2[2 * BLK * e:2 * BLK * (e + 1)] * (HEAD_DIM ** -0.5) - _alibi_slope(hh) * distf
    return jnp.where(valid, s, -1e30)


def _swa_fwd(proj, sinks, *, name, exchanges=()):
    t = proj.shape[0]
    nb = t // BLK
    kvb = KV_COL // (2 * 128)

    def body(sink_ref, q_ref, kvc_ref, kvp_ref, o_ref, lse_ref):
        n = pl.program_id(0)
        valid, distf = _swa_mask(n)
        for kvh in range(N_KV):
            k2 = _block_diag(kvp_ref, kvc_ref, kvh * HEAD_DIM)
            v2 = _block_diag(kvp_ref, kvc_ref, 128 + kvh * HEAD_DIM)
            for jp in range(PAIRS_PER_KV):
                pair = kvh * PAIRS_PER_KV + jp
                lanes = slice(pair * PAIR, (pair + 1) * PAIR)
                s2 = lax.dot_general(k2, q_ref[:, lanes].astype(BF16), NT, preferred_element_type=F32)
                probs = []
                for e in range(2):
                    hh = 2 * pair + e
                    s = _swa_scores(s2, e, hh, valid, distf)
                    sink = sink_ref[0, hh]
                    mx = jnp.maximum(jnp.max(s, axis=0, keepdims=True), sink)
                    p = jnp.exp(s - mx)
                    l = jnp.sum(p, axis=0, keepdims=True) + jnp.exp(sink - mx)
                    probs.append((p * (1.0 / l)).astype(BF16))
                    lse_ref[hh:hh + 1, :] = mx + jnp.log(l)
                o_ref[:, lanes] = lax.dot_general(jnp.concatenate(probs, axis=0), v2, TN,
                                                  preferred_element_type=F32)

    res, xres = _call(
        body, name=name, grid=(nb,),
        in_specs=[SMEM,
                  pl.BlockSpec((BLK, ATTN_W), lambda n: (n, 0)),
                  pl.BlockSpec((BLK, 256), lambda n: (n, kvb)),
                  pl.BlockSpec((BLK, 256), lambda n: (jnp.maximum(n - 1, 0), kvb))],
        out_specs=[pl.BlockSpec((BLK, ATTN_W), lambda n: (n, 0)),
                   pl.BlockSpec((None, N_Q, BLK), lambda n: (n, 0, 0))],
        out_shape=[jax.ShapeDtypeStruct((t, ATTN_W), F32), jax.ShapeDtypeStruct((nb, N_Q, BLK), F32)],
        args=[sinks, proj, proj, proj], semantics=("parallel",), exchanges=exchanges)
    return (*res, xres) if exchanges else res


def _swa_bwd(proj, sinks, dattn, lse, *, name, exchanges=()):
    t = proj.shape[0]
    nb = t // BLK
    kvb = KV_COL // (2 * 128)

    def body(sink_ref, q_ref, kvc_ref, kvp_ref, do_ref, lse_ref, dq_ref, dkv_ref, dsink_ref, carry_ref):
        n = pl.program_id(0)

        @pl.when(n == 0)
        def _():
            dsink_ref[...] = jnp.zeros_like(dsink_ref)
            carry_ref[...] = jnp.zeros_like(carry_ref)

        @pl.when(n < nb)
        def _():
            valid, distf = _swa_mask(n)
            for kvh in range(N_KV):
                k2 = _block_diag(kvp_ref, kvc_ref, kvh * HEAD_DIM)
                v2 = _block_diag(kvp_ref, kvc_ref, 128 + kvh * HEAD_DIM)
                dk2 = jnp.zeros((4 * BLK, PAIR), F32)
                dv2 = jnp.zeros((4 * BLK, PAIR), F32)
                for jp in range(PAIRS_PER_KV):
                    pair = kvh * PAIRS_PER_KV + jp
                    lanes = slice(pair * PAIR, (pair + 1) * PAIR)
                    q2 = q_ref[:, lanes].astype(BF16)
                    do2 = do_ref[:, lanes].astype(BF16)
                    s2 = lax.dot_general(k2, q2, NT, preferred_element_type=F32)
                    dp2 = lax.dot_general(v2, do2, NT, preferred_element_type=F32)
                    probs, dss = [], []
                    for e in range(2):
                        hh = 2 * pair + e
                        lse_h = lse_ref[hh:hh + 1, :]
                        p = jnp.exp(_swa_scores(s2, e, hh, valid, distf) - lse_h)
                        dp = dp2[2 * BLK * e:2 * BLK * (e + 1)]
                        delta = jnp.sum(p * dp, axis=0, keepdims=True)
                        dsink_ref[hh:hh + 1, :] += -jnp.exp(sink_ref[0, hh] - lse_h) * delta
                        probs.append(p.astype(BF16))
                        dss.append((p * (dp - delta)).astype(BF16))
                    ds2 = jnp.concatenate(dss, axis=0)
                    dq_ref[:, lanes] = (lax.dot_general(ds2, k2, TN, preferred_element_type=F32)
                                        * (HEAD_DIM ** -0.5)).astype(BF16)
                    dk2 = dk2 + jnp.dot(ds2, q2, preferred_element_type=F32)
                    dv2 = dv2 + jnp.dot(jnp.concatenate(probs, axis=0), do2, preferred_element_type=F32)
                dk_cat = (dk2[:2 * BLK, :HEAD_DIM] + dk2[2 * BLK:, HEAD_DIM:]) * (HEAD_DIM ** -0.5)
                dv_cat = dv2[:2 * BLK, :HEAD_DIM] + dv2[2 * BLK:, HEAD_DIM:]
                ko = kvh * HEAD_DIM
                vo = 128 + kvh * HEAD_DIM
                dkv_ref[:, ko:ko + HEAD_DIM] = (carry_ref[:, ko:ko + HEAD_DIM] + dk_cat[:BLK]).astype(BF16)
                dkv_ref[:, vo:vo + HEAD_DIM] = (carry_ref[:, vo:vo + HEAD_DIM] + dv_cat[:BLK]).astype(BF16)
                carry_ref[:, ko:ko + HEAD_DIM] = dk_cat[BLK:]
                carry_ref[:, vo:vo + HEAD_DIM] = dv_cat[BLK:]

        @pl.when(n == nb)
        def _():
            dkv_ref[...] = carry_ref[...].astype(BF16)

    last = nb - 1
    res, xres = _call(
        body, name=name, grid=(nb + 1,),
        in_specs=[SMEM,
                  pl.BlockSpec((BLK, ATTN_W), lambda n: (jnp.minimum(n, last), 0)),
                  pl.BlockSpec((BLK, 256), lambda n: (jnp.minimum(n, last), kvb)),
                  pl.BlockSpec((BLK, 256), lambda n: (jnp.maximum(jnp.minimum(n, last) - 1, 0), kvb)),
                  pl.BlockSpec((BLK, ATTN_W), lambda n: (jnp.minimum(n, last), 0)),
                  pl.BlockSpec((None, N_Q, BLK), lambda n: (jnp.minimum(n, last), 0, 0))],
        out_specs=[pl.BlockSpec((BLK, ATTN_W), lambda n: (jnp.minimum(n, last), 0)),
                   pl.BlockSpec((BLK, 256), lambda n: (jnp.maximum(n - 1, 0), 0)),
                   pl.BlockSpec((N_Q, BLK), lambda n: (0, 0))],
        out_shape=[jax.ShapeDtypeStruct((t, ATTN_W), BF16), jax.ShapeDtypeStruct((t, 256), BF16),
                   jax.ShapeDtypeStruct((N_Q, BLK), F32)],
        scratch_shapes=[pltpu.VMEM((BLK, 256), F32)],
        args=[sinks, proj, proj, proj, dattn, lse], semantics=("arbitrary",), exchanges=exchanges)
    return (*res, xres) if exchanges else res


def _cumsum_rows(x):
    n = x.shape[0]
    row = lax.broadcasted_iota(jnp.int32, x.shape, 0)
    s = 1
    while s < n:
        x = x + jnp.where(row >= s, pltpu.roll(x, s, axis=0), 0.0)
        s *= 2
    return x


def _rev_cumsum_rows(x):
    n = x.shape[0]
    row = lax.broadcasted_iota(jnp.int32, x.shape, 0)
    s = 1
    while s < n:
        x = x + jnp.where(row < n - s, pltpu.roll(x, n - s, axis=0), 0.0)
        s *= 2
    return x


def _lower_bound(lbl_ref):
    l0 = lbl_ref[0:1, :]
    l1 = lbl_ref[1:2, :]
    mx = jnp.maximum(l0, l1)
    e0 = jnp.exp(l0 - mx)
    e1 = jnp.exp(l1 - mx)
    return e0 / (e0 + e1)


def _hgrn_gates(z, lb):
    sg = _sigmoid(z)
    f = lb + (1.0 - lb) * sg
    return sg, f, jnp.log(f), 1.0 - f


def _sub_factors(b, i, sub):
    rows = lax.broadcasted_iota(jnp.int32, (CHUNK, RNN_HD), 0)
    ref = b[sub * i - 1:sub * i, :]
    qfac = jnp.exp(b[sub * i:sub * (i + 1), :] - ref)
    kfac = jnp.where(rows < sub * i, jnp.exp(ref - b), 0.0)
    return qfac, kfac


def _diag_decay(bi, s):
    trow = lax.broadcasted_iota(jnp.int32, bi.shape, 0)
    return jnp.where(trow >= s, jnp.exp(bi - bi[s:s + 1, :]), 0.0)


def _hgrn_fwd(proj, lb_logits, norm_gain, *, tb, name, exchanges=()):
    t = proj.shape[0]
    ntb = t // tb
    nch = tb // CHUNK
    qb, fb, ib, gb = QR_COL // 128, FR_COL // 128, IR_COL // 128, GR_COL // 128

    def body(q_ref, f_ref, i_ref, g_ref, lbl_ref, gain_ref, o_ref, out_ref, s0_ref, st_ref):
        c = pl.program_id(1)

        @pl.when(c == 0)
        def _():
            st_ref[...] = jnp.zeros_like(st_ref)

        lb = _lower_bound(lbl_ref)
        gain = gain_ref[...]

        def chunk(ci, st):
            rows = slice(ci * CHUNK, (ci + 1) * CHUNK)
            _, _, lf, k = _hgrn_gates(f_ref[rows, :], lb)
            qr = q_ref[rows, :]
            q = qr * _sigmoid(qr)
            v = i_ref[rows, :]
            b = _cumsum_rows(lf)
            s0_ref[ci] = st
            o_inter = lax.dot_general((q * jnp.exp(b)).astype(BF16), st.astype(BF16), NT,
                                      preferred_element_type=F32)
            vb = v.astype(BF16)
            blast = b[CHUNK - 1:CHUNK, :]
            khat = (k * jnp.exp(blast - b)).astype(BF16)
            st = st * jnp.exp(blast) + lax.dot_general(vb, khat, TN, preferred_element_type=F32)
            blocks = []
            for i in range(CHUNK // SUB_FWD):
                blk = slice(SUB_FWD * i, SUB_FWD * (i + 1))
                qi, ki, vi, bi = q[blk], k[blk], v[blk], b[blk]
                oi = o_inter[blk]
                if i > 0:
                    qfac, kfac = _sub_factors(b, i, SUB_FWD)
                    att = lax.dot_general((qi * qfac).astype(BF16), (k * kfac).astype(BF16), NT,
                                          preferred_element_type=F32)
                    oi = oi + jnp.dot(att.astype(BF16), vb, preferred_element_type=F32)
                for s in range(SUB_FWD):
                    qe = qi * _diag_decay(bi, s)
                    a = jnp.sum(qe * ki[s:s + 1, :], axis=1, keepdims=True)
                    oi = oi + a * vi[s:s + 1, :]
                blocks.append(oi)
            o = jnp.concatenate(blocks, axis=0)
            o_ref[rows, :] = o
            gr = g_ref[rows, :]
            out_ref[rows, :] = o * _rstd(o) * gain * (gr * _sigmoid(gr))
            return st

        st = st_ref[...]
        for ci in range(nch):
            st = chunk(ci, st)
        st_ref[...] = st

    def col(base):
        return pl.BlockSpec((tb, RNN_HD), lambda h, c: (c, base + h))

    res, xres = _call(
        body, name=name, grid=(N_RNN, ntb),
        in_specs=[col(qb), col(fb), col(ib), col(gb),
                  pl.BlockSpec((2, RNN_HD), lambda h, c: (0, h)), pl.BlockSpec((1, RNN_HD), lambda h, c: (0, 0))],
        out_specs=[pl.BlockSpec((tb, RNN_HD), lambda h, c: (c, h)), pl.BlockSpec((tb, RNN_HD), lambda h, c: (c, h)),
                   pl.BlockSpec((None, nch, RNN_HD, RNN_HD), lambda h, c: (h, c, 0, 0))],
        out_shape=[jax.ShapeDtypeStruct((t, RNN_W), F32), jax.ShapeDtypeStruct((t, RNN_W), F32),
                   jax.ShapeDtypeStruct((N_RNN, t // CHUNK, RNN_HD, RNN_HD), F32)],
        scratch_shapes=[pltpu.VMEM((RNN_HD, RNN_HD), F32)],
        args=[proj, proj, proj, proj, lb_logits, norm_gain],
        semantics=("parallel", "arbitrary"), exchanges=exchanges)
    return (*res, xres) if exchanges else res


def _hgrn_bwd(proj, lb_logits, norm_gain, o_pre, s0, dcat, *, tb, name, exchanges=()):
    t = proj.shape[0]
    ntb = t // tb
    nch = tb // CHUNK
    qb, fb, ib, gb = QR_COL // 128, FR_COL // 128, IR_COL // 128, GR_COL // 128
    sub = SUB_BWD
    nsub = CHUNK // sub

    def body(q_ref, f_ref, i_ref, g_ref, lbl_ref, gain_ref, o_ref, s0_ref, dout_ref,
             dq_ref, df_ref, di_ref, dg_ref, dlb_ref, dgain_ref,
             dst_ref, dqs_ref, dks_ref, dvs_ref):
        c = pl.program_id(1)

        @pl.when(c == 0)
        def _():
            dst_ref[...] = jnp.zeros_like(dst_ref)
            dlb_ref[...] = jnp.zeros_like(dlb_ref)
            dgain_ref[...] = jnp.zeros_like(dgain_ref)

        lb = _lower_bound(lbl_ref)
        gain = gain_ref[...]

        def chunk(ci, dst):
            rows = slice(ci * CHUNK, (ci + 1) * CHUNK)
            dqa_ref, dka_ref, dva_ref = dqs_ref.at[ci], dks_ref.at[ci], dvs_ref.at[ci]
            sg, f, lf, k = _hgrn_gates(f_ref[rows, :], lb)
            qr = q_ref[rows, :]
            sq = _sigmoid(qr)
            q = qr * sq
            v = i_ref[rows, :]
            b = _cumsum_rows(lf)

            dout = dout_ref[rows, :].astype(F32)
            o = o_ref[rows, :]
            gr = g_ref[rows, :]
            sgg = _sigmoid(gr)
            gate = gr * sgg
            rs = _rstd(o)
            nrm = o * rs
            dg_ref[rows, :] = (dout * nrm * gain * (sgg * (1.0 + gr * (1.0 - sgg)))).astype(BF16)
            dn = dout * gate
            dgain_ref[...] += jnp.sum(dn * nrm, axis=0, keepdims=True)
            tt = dn * gain
            do = rs * (tt - nrm * jnp.mean(tt * nrm, axis=-1, keepdims=True))

            dob = do.astype(BF16)
            vb = v.astype(BF16)
            eb = jnp.exp(b)
            blast = b[CHUNK - 1:CHUNK, :]
            ebl = jnp.exp(blast - b)
            dstb = dst.astype(BF16)
            khat = (k * ebl).astype(BF16)
            s0 = s0_ref[ci]
            dqa_ref[...] = eb * jnp.dot(dob, s0.astype(BF16), preferred_element_type=F32)
            dk_state = ebl * jnp.dot(vb, dstb, preferred_element_type=F32)
            dka_ref[...] = dk_state
            d_blast = (jnp.sum(k * dk_state, axis=0, keepdims=True)
                       + jnp.exp(blast) * jnp.sum(dst * s0, axis=0, keepdims=True))
            dva_ref[...] = lax.dot_general(khat, dstb, NT, preferred_element_type=F32)
            dst_next = dst * jnp.exp(blast) + lax.dot_general(dob, (q * eb).astype(BF16), TN,
                                                              preferred_element_type=F32)
            pm = lax.dot_general(dob, vb, NT, preferred_element_type=F32)
            for i in range(nsub):
                blk = slice(sub * i, sub * (i + 1))
                qi, ki, vi, bi, doi = q[blk], k[blk], v[blk], b[blk], do[blk]
                dqi = dqa_ref[blk, :]
                if i > 0:
                    qfac, kfac = _sub_factors(b, i, sub)
                    qt = (qi * qfac).astype(BF16)
                    kt = (k * kfac).astype(BF16)
                    att = lax.dot_general(qt, kt, NT, preferred_element_type=F32).astype(BF16)
                    pmi = pm[blk, :].astype(BF16)
                    dva_ref[...] += lax.dot_general(att, doi.astype(BF16), TN, preferred_element_type=F32)
                    dqi = dqi + qfac * jnp.dot(pmi, kt, preferred_element_type=F32)
                    dka_ref[...] += kfac * lax.dot_general(pmi, qt, TN, preferred_element_type=F32)
                dqa_ref[blk, :] = dqi
                srow = lax.broadcasted_iota(jnp.int32, (sub, RNN_HD), 0)
                dki = jnp.zeros((sub, RNN_HD), F32)
                dvi = jnp.zeros((sub, RNN_HD), F32)
                for tq in range(sub):
                    qt, dot_ = qi[tq:tq + 1, :], doi[tq:tq + 1, :]
                    e = jnp.where(srow <= tq, jnp.exp(bi[tq:tq + 1, :] - bi), 0.0)
                    ke = ki * e
                    p = jnp.sum(vi * dot_, axis=1, keepdims=True)
                    a = jnp.sum(ke * qt, axis=1, keepdims=True)
                    dki = dki + p * (qt * e)
                    dvi = dvi + a * dot_
                    row = slice(sub * i + tq, sub * i + tq + 1)
                    dqa_ref[row, :] += jnp.sum(p * ke, axis=0, keepdims=True)
                dka_ref[blk, :] += dki
                dva_ref[blk, :] += dvi

            dq = dqa_ref[...]
            dk = dka_ref[...]
            lastrow = lax.broadcasted_iota(jnp.int32, (CHUNK, RNN_HD), 0) == CHUNK - 1
            dlf = _rev_cumsum_rows(q * dq - k * dk + jnp.where(lastrow, d_blast, 0.0))
            dff = dlf / f - dk
            df_ref[rows, :] = (dff * (1.0 - lb) * sg * (1.0 - sg)).astype(BF16)
            dlb_ref[...] += jnp.sum(dff * (1.0 - sg), axis=0, keepdims=True)
            dq_ref[rows, :] = (dq * (sq * (1.0 + qr * (1.0 - sq)))).astype(BF16)
            di_ref[rows, :] = dva_ref[...].astype(BF16)
            return dst_next

        dst = dst_ref[...]
        for ci in reversed(range(nch)):
            dst = chunk(ci, dst)
        dst_ref[...] = dst

    def col(base):
        return pl.BlockSpec((tb, RNN_HD), lambda h, c: (ntb - 1 - c, base + h))

    outc = pl.BlockSpec((tb, RNN_HD), lambda h, c: (ntb - 1 - c, h))
    hb = ATTN_W // RNN_HD
    res, xres = _call(
        body, name=name, grid=(N_RNN, ntb),
        in_specs=[col(qb), col(fb), col(ib), col(gb),
                  pl.BlockSpec((2, RNN_HD), lambda h, c: (0, h)), pl.BlockSpec((1, RNN_HD), lambda h, c: (0, 0)),
                  outc,
                  pl.BlockSpec((None, nch, RNN_HD, RNN_HD), lambda h, c: (h, ntb - 1 - c, 0, 0)),
                  pl.BlockSpec((tb, RNN_HD), lambda h, c: (ntb - 1 - c, hb + h))],
        out_specs=[outc, outc, outc, outc,
                   pl.BlockSpec((1, RNN_HD), lambda h, c: (0, h)),
                   pl.BlockSpec((None, 1, RNN_HD), lambda h, c: (h, 0, 0))],
        out_shape=[jax.ShapeDtypeStruct((t, RNN_W), BF16)] * 4
        + [jax.ShapeDtypeStruct((1, RNN_W), F32), jax.ShapeDtypeStruct((N_RNN, 1, RNN_HD), F32)],
        scratch_shapes=[pltpu.VMEM((RNN_HD, RNN_HD), F32),
                        pltpu.VMEM((nch, CHUNK, RNN_HD), F32), pltpu.VMEM((nch, CHUNK, RNN_HD), F32),
                        pltpu.VMEM((nch, CHUNK, RNN_HD), F32)],
        args=[proj, proj, proj, proj, lb_logits, norm_gain, o_pre, s0, dcat],
        semantics=("parallel", "arbitrary"), exchanges=exchanges)
    return (*res, xres) if exchanges else res


def _cast_slots(w, where, *, name):
    _, rows, cols = w.shape
    rh = rows // 2
    tr = _row_tile(rh, cols)
    nh = rh // tr

    def body(wh_ref, w_ref, o_ref):
        o_ref[...] = w_ref[...].astype(BF16)

    return pl.pallas_call(
        body, name=name,
        grid_spec=pltpu.PrefetchScalarGridSpec(
            num_scalar_prefetch=1, grid=(2, nh),
            in_specs=[pl.BlockSpec((None, tr, cols), lambda h, i, wh: (0, h * nh + i, 0))],
            out_specs=pl.BlockSpec((None, tr, cols), lambda h, i, wh: (2 * wh[0] + h, i, 0))),
        out_shape=jax.ShapeDtypeStruct((8, rh, cols), BF16),
        compiler_params=_params(("parallel", "parallel")),
    )(where, w)


def _all_gather_halves(bufs, *, name):
    n = len(bufs)

    def body(*refs):
        ins, outs = refs[:n], refs[n:2 * n]
        send_sems, recv_sems = refs[2 * n:]
        x, y, c = _place()
        sibling = (x, y, 1 - c)
        chips = [(1 - x, y), (x, 1 - y), (1 - x, 1 - y)]

        def copy(a, k, block, to, src=None):
            slot = outs[a].at[4 * block[0] + 2 * block[1] + block[2]]
            return pltpu.make_async_remote_copy(
                src_ref=slot if src is None else src, dst_ref=slot,
                send_sem=send_sems.at[a, k], recv_sem=recv_sems.at[a, k],
                device_id=to, device_id_type=MESH)

        first, passed = [], []
        for a in range(n):
            for j, chip in enumerate(chips):
                cp = copy(a, j, (x, y, c), (*chip, c), src=ins[a].at[4 * x + 2 * y + c])
                cp.start()
                first.append(cp)
        for a in range(n):
            for j, chip in enumerate(chips):
                copy(a, j, (*chip, c), (x, y, c)).wait_recv()
                cp = copy(a, 3 + j, (*chip, c), sibling)
                cp.start()
                passed.append(cp)
        for a in range(n):
            for j, chip in enumerate(chips):
                copy(a, 3 + j, (*chip, 1 - c), (x, y, c)).wait_recv()
        for cp in first + passed:
            cp.wait_send()

    return pl.pallas_call(
        body, name=name,
        in_specs=[ANY] * n, out_specs=[ANY] * n,
        out_shape=[jax.ShapeDtypeStruct(b.shape, b.dtype) for b in bufs],
        scratch_shapes=[pltpu.SemaphoreType.DMA((n, 6)), pltpu.SemaphoreType.DMA((n, 6))],
        input_output_aliases={a: a for a in range(n)},
    )(*bufs)


def _row_tile(rows, cols, budget=1 << 20):
    tr = rows
    while tr * cols > budget and tr % 16 == 0:
        tr //= 2
    return tr


def _half_spec(g, tr, halves_last, slab):
    if halves_last:
        return pl.BlockSpec((None, tr, g.shape[2] // 2), lambda *a: (slab(*a), a[-2], a[-1][1]))
    return pl.BlockSpec((None, None, tr, g.shape[3]), lambda *a: (slab(*a), a[-1][1], a[-2], 0))


def _pair_sum(g, sib, where, *, name, halves_last=False):
    rh, cols = sib.shape[1:]
    tr = _row_tile(rh, cols)

    def body(w_ref, g_ref, s_ref, o_ref):
        o_ref[...] = (g_ref[...] + s_ref[...]).astype(BF16)

    return pl.pallas_call(
        body, name=name,
        grid_spec=pltpu.PrefetchScalarGridSpec(
            num_scalar_prefetch=1, grid=(4, rh // tr),
            in_specs=[_half_spec(g, tr, halves_last, lambda s, i, w: s),
                      pl.BlockSpec((None, tr, cols), lambda s, i, w: (s, i, 0))],
            out_specs=pl.BlockSpec((None, tr, cols), lambda s, i, w: (s, i, 0))),
        out_shape=jax.ShapeDtypeStruct((4, rh, cols), BF16),
        compiler_params=_params(("parallel", "parallel")),
    )(where, g, sib)


def _final_half(g, sib, recv, where, *, name, halves_last=False):
    rh, cols = sib.shape[1:]
    tr = _row_tile(rh, cols)

    def body(w_ref, g_ref, s_ref, r_ref, o_ref):
        acc = g_ref[...] + s_ref[...]
        for j in range(3):
            acc = acc + r_ref[j].astype(F32)
        o_ref[...] = acc

    return pl.pallas_call(
        body, name=name,
        grid_spec=pltpu.PrefetchScalarGridSpec(
            num_scalar_prefetch=1, grid=(rh // tr,),
            in_specs=[_half_spec(g, tr, halves_last, lambda i, w: w[0]),
                      pl.BlockSpec((None, tr, cols), lambda i, w: (w[0], i, 0)),
                      pl.BlockSpec((3, tr, cols), lambda i, w: (0, i, 0))],
            out_specs=pl.BlockSpec((tr, cols), lambda i, w: (i, 0))),
        out_shape=jax.ShapeDtypeStruct((rh, cols), F32),
        compiler_params=_params(("parallel",)),
    )(where, g, sib, recv)


def _adamw_math(w, g, m, v):
    m = ADAM_B1 * m + (1.0 - ADAM_B1) * g
    v = ADAM_B2 * v + (1.0 - ADAM_B2) * (g * g)
    m_hat = m / (1.0 - ADAM_B1 ** ADAM_STEP)
    v_hat = v / (1.0 - ADAM_B2 ** ADAM_STEP)
    delta = -ADAM_LR * (m_hat / (jnp.sqrt(v_hat) + ADAM_EPS) + ADAM_WD * w)
    return delta, m, v


def _adamw(w, mine, theirs, m, v, where, *, name, halves_last=False):
    _, rows, cols = w.shape
    if halves_last:
        cols //= 2
        tr = _row_tile(rows, cols, budget=1 << 19)
        grid = (rows // tr, 2)
        blk = pl.BlockSpec((None, tr, cols), lambda i, h, wh: (0, i, h))
        mine_spec = theirs_spec = pl.BlockSpec((tr, cols), lambda i, h, wh: (i, 0))
        which = lambda: pl.program_id(1)
    else:
        tr = _row_tile(rows // 2, cols, budget=1 << 19)
        nh = rows // 2 // tr
        grid = (rows // tr,)
        blk = pl.BlockSpec((None, tr, cols), lambda i, wh: (0, i, 0))
        mine_spec = pl.BlockSpec((tr, cols), lambda i, wh: (jnp.where(i // nh == wh[1], i % nh, 0), 0))
        theirs_spec = pl.BlockSpec((tr, cols), lambda i, wh: (jnp.where(i // nh == wh[1], 0, i % nh), 0))
        which = lambda: pl.program_id(0) // nh

    def body(wh_ref, w_ref, a_ref, b_ref, m_ref, v_ref, g_ref, d_ref, nm_ref, nv_ref):
        g = jnp.where(which() == wh_ref[1], a_ref[...], b_ref[...])
        d, nm, nv = _adamw_math(w_ref[...], g, m_ref[...], v_ref[...])
        g_ref[...] = g
        d_ref[...] = d
        nm_ref[...] = nm
        nv_ref[...] = nv

    rows, cols = w.shape[1:]
    return pl.pallas_call(
        body, name=name,
        grid_spec=pltpu.PrefetchScalarGridSpec(
            num_scalar_prefetch=1, grid=grid,
            in_specs=[blk, mine_spec, theirs_spec, blk, blk], out_specs=[blk] * 4),
        out_shape=[jax.ShapeDtypeStruct((1, rows, cols), F32)] * 4,
        compiler_params=_params(("parallel",) * len(grid)),
    )(where, w, mine, theirs, m, v)


SEG_LOSS = 0
SEG_SINK = 128
SEG_AGAIN = 256
SEG_L0 = SEG_AGAIN + ATTN_W
SEG_L1 = SEG_L0 + RNN_W
SEG_RGAIN = SEG_L1 + RNN_W
SEG_G = SEG_RGAIN + 128
N_PACK = SEG_G + 4 * D_MODEL


def _pack(sinks, again, l0, l1, rgain, gains, loss=None):
    z = lambda k: jnp.zeros((1, k), F32)
    first = z(128) if loss is None else loss
    return jnp.concatenate([first, sinks, z(128 - N_Q), again, l0, l1, rgain] + list(gains), axis=1)


def _small_reduce_adamw(part, w, m, v, *, name):
    def body(p_ref, w_ref, m_ref, v_ref, g_ref, d_ref, nm_ref, nv_ref, buf_ref, send_sems, recv_sems):
        x, y, c = _place()
        me = 4 * x + 2 * y + c
        copies = []
        for k in range(1, 8):
            dx, dy, dc = (k >> 2) & 1, (k >> 1) & 1, k & 1
            to = (x ^ dx, y ^ dy, c ^ dc)
            cp = pltpu.make_async_remote_copy(
                src_ref=p_ref, dst_ref=buf_ref.at[me],
                send_sem=send_sems.at[k - 1], recv_sem=recv_sems.at[k - 1],
                device_id=to, device_id_type=MESH)
            cp.start()
            copies.append(cp)
        buf_ref[me] = p_ref[...]
        for cp in copies:
            cp.wait()
        tot = buf_ref[0]
        for j in range(1, 8):
            tot = tot + buf_ref[j]
        g_ref[...] = tot
        l0 = w_ref[:, SEG_L0:SEG_L0 + RNN_W]
        l1 = w_ref[:, SEG_L1:SEG_L1 + RNN_W]
        mx = jnp.maximum(l0, l1)
        e0 = jnp.exp(l0 - mx)
        e1 = jnp.exp(l1 - mx)
        lb = e0 / (e0 + e1)
        gl0 = tot[:, SEG_L0:SEG_L0 + RNN_W] * lb * (1.0 - lb)
        g_ref[:, SEG_L0:SEG_L0 + RNN_W] = gl0
        g_ref[:, SEG_L1:SEG_L1 + RNN_W] = -gl0
        d, nm, nv = _adamw_math(w_ref[...], g_ref[...], m_ref[...], v_ref[...])
        d_ref[...] = d
        nm_ref[...] = nm
        nv_ref[...] = nv

    vm = pl.BlockSpec(memory_space=pltpu.VMEM)
    return pl.pallas_call(
        body, name=name,
        in_specs=[vm] * 4, out_specs=[vm] * 4,
        out_shape=[jax.ShapeDtypeStruct((1, N_PACK), F32)] * 4,
        scratch_shapes=[pltpu.VMEM((8, 1, N_PACK), F32), pltpu.SemaphoreType.DMA((7,)),
                        pltpu.SemaphoreType.DMA((7,))],
    )(part, w, m, v)


def _layer_grads(xs, tgt, bufs, where, sinks, again, lb_logits, rgain,
                 g_mix_pre, g_mix_post, g_mlp_pre, g_mlp_post):
    tm = 512
    b_in, b_out, b_up, b_dn = bufs

    shard = IN_W // N_CHIPS
    w_in_t = _all_gather_halves([b_in], name="gather_w_in")[0].reshape(IN_W, D_MODEL)
    h1 = _rms_cast(xs, g_mix_pre, tm=tm, name="h1_norm")
    proj, ((b_out, b_up),) = _mm(
        h1, w_in_t, tm=1024, tn=768, tk=D_MODEL, out_dtype=F32, w_layout="nk", name="in_proj",
        exchanges=[_x_gather([b_out, b_up], ici=[(0, 256), (0, 384)])])
    attn, lse, ((b_out, b_up),) = _swa_fwd(
        proj, sinks, name="swa_fwd",
        exchanges=[_x_gather([b_out, b_up], ici=[None, (384, 320)], d2d=[(0, 256), None])])
    w_out = b_out.reshape(D_MODEL, D_MODEL)
    o_pre, rnn, s0, ((b_up, b_dn),) = _hgrn_fwd(
        proj, lb_logits, rgain, tb=512, name="hgrn_fwd",
        exchanges=[_x_gather([b_up, b_dn], ici=[(704, 320), (0, 608)])])
    cat = _mix_cat(attn, rnn, again, tm=tm, name="mix_cat")
    mixed, ((b_up, b_dn),) = _mm(
        cat, w_out, tm=1024, tn=1024, tk=D_MODEL, out_dtype=BF16, name="out_proj",
        exchanges=[_x_gather([b_up, b_dn], ici=[None, (608, 256)], d2d=[(0, 1024), (0, 608)])])
    w_up4 = b_up.reshape(N_CHIPS, D_MODEL, D_FF // N_CHIPS)
    x1, h2, ((b_dn,),) = _post_norm_res(
        mixed, g_mix_post, xs, g_mlp_pre, tm=256, name="mix_post",
        exchanges=[_x_gather([b_dn], ici=[(864, 160)], d2d=[(608, 256)])])
    u, ((b_dn,),) = _mm(h2, w_up4, tm=1024, tn=1024, tk=D_MODEL, out_dtype=BF16, relu=True, w_layout="skn",
                        name="mlp_up", exchanges=[_x_gather([b_dn], d2d=[(864, 160)])])
    w_dn = b_dn.reshape(D_FF, D_MODEL)
    yv = _mm(u, w_dn, tm=1024, tn=1024, tk=2048, out_dtype=BF16, a_square=True, name="mlp_down")
    dy, dx2, loss_row, dg_mlp_post = _loss_head(yv, g_mlp_post, x1, tgt, tm=256, name="loss_head")

    def halved(g):
        return g.reshape(N_CHIPS, 2, g.shape[1] // 2, g.shape[2])
    du = _mm(dy, w_dn, tm=1024, tn=1024, tk=D_MODEL, out_dtype=BF16, mul2=u, w_layout="nk", name="mlp_down_bwd")
    g_dn = halved(_mm_tn(u, dy, tm=1024, tn=1024, tt=2048, a_square=True, name="w_down_grad")
                  .reshape(N_CHIPS, D_FF // N_CHIPS, D_MODEL))
    d_w_up, ((sib_dn,),) = _mm_tn(h2, du, tm=1024, tn=1024, tt=2048, n_split=N_CHIPS, name="w_up_grad",
                                  exchanges=[_x_pair([g_dn])])
    g_up = halved(d_w_up)
    wire_dn = _pair_sum(g_dn, sib_dn, where, name="pair_sum_w_down")
    dh2, ((recv_dn,), (sib_up,)) = _mm(du, w_up4, tm=1024, tn=1024, tk=2048, out_dtype=BF16, w_layout="snk", name="mlp_up_bwd",
                                       exchanges=[_x_chip([wire_dn], rows=[(0, 800)]), _x_pair([g_up])])
    wire_up = _pair_sum(g_up, sib_up, where, name="pair_sum_w_up")
    dx1, dg_mlp_pre, ((recv_dn,),) = _rms_bwd(dh2, x1, g_mlp_pre, dx2, tm=256, out_dtype=F32, name="mlp_pre_bwd",
                                              exchanges=[_x_chip([wire_dn], rows=[(800, 224)], into=[recv_dn])])
    fin_dn = _final_half(g_dn, sib_dn, recv_dn, where, name="final_half_w_down")
    dmixed, dg_mix_post = _rms_bwd(dx1, mixed, g_mix_post, None, tm=256, out_dtype=BF16, name="mix_post_bwd")
    d_w_out, ((oth_dn,),) = _mm_tn(cat, dmixed, tm=1024, tn=1024, tt=2048, name="w_out_grad",
                                   exchanges=[_x_share([fin_dn])])
    g_out = halved(d_w_out.reshape(N_CHIPS, D_MODEL // N_CHIPS, D_MODEL))
    dcat, ((sib_out,),) = _mm(dmixed, w_out, tm=1024, tn=1024, tk=D_MODEL, out_dtype=BF16, w_layout="nk", name="out_proj_bwd",
                              exchanges=[_x_pair([g_out])])
    wire_out = _pair_sum(g_out, sib_out, where, name="pair_sum_w_out")
    dattn, dg_again = _rms_bwd(dcat, attn, again, None, tm=tm, out_dtype=BF16, name="attn_norm_bwd")
    dq_a, dkv, dsinks, ((recv_out,), (recv_up,)) = _swa_bwd(
        proj, sinks, dattn, lse, name="swa_bwd",
        exchanges=[_x_chip([wire_out]), _x_chip([wire_up], rows=[(0, 320)])])
    dq_r, df_r, di_r, dg_r, dlb, dgain_h, ((recv_up,),) = _hgrn_bwd(
        proj, lb_logits, rgain, o_pre, s0, dcat, tb=512, name="hgrn_bwd",
        exchanges=[_x_chip([wire_up], rows=[(320, 704)], into=[recv_up])])
    fin_up = _final_half(g_up, sib_up, recv_up, where, name="final_half_w_up")
    fin_out = _final_half(g_out, sib_out, recv_out, where, name="final_half_w_out")
    dproj = jnp.concatenate([dq_a, dkv, dq_r, df_r, di_r, dg_r], axis=1)
    piece_cols = D_MODEL // 4

    def w_in_piece(pc, exchanges):
        d, xres = _mm_tn(dproj, h1, tm=896, tn=piece_cols, tt=2048, n_blocks=(2, 2, pc),
                         name="w_in_grad_%d" % pc, exchanges=exchanges)
        return d.reshape(N_CHIPS, shard, 2 * piece_cols), xres

    g_in0, ((oth_up, oth_out),) = w_in_piece(0, [_x_share([fin_up, fin_out])])
    g_in1, ((sib_in0,),) = w_in_piece(1, [_x_pair([g_in0], halves_last=True)])
    wire_in0 = _pair_sum(g_in0, sib_in0, where, name="pair_sum_w_in_0", halves_last=True)
    dh1, ((recv_in0,), (sib_in1,)) = _mm(
        dproj, w_in_t, tm=1024, tn=1024, tk=2688, out_dtype=BF16, m_blocks=(0, 2), name="in_proj_bwd_0",
        exchanges=[_x_chip([wire_in0]), _x_pair([g_in1], halves_last=True)])
    wire_in1 = _pair_sum(g_in1, sib_in1, where, name="pair_sum_w_in_1", halves_last=True)
    dh1, ((recv_in1,),) = _mm(
        dproj, w_in_t, tm=1024, tn=1024, tk=2688, out_dtype=BF16, m_blocks=(2, 2), out_into=dh1,
        name="in_proj_bwd_1", exchanges=[_x_chip([wire_in1])])
    gx, dg_mix_pre = _rms_bwd(dh1, xs, g_mix_pre, dx1, tm=256, out_dtype=F32, name="mix_pre_bwd")
    fin_in0 = _final_half(g_in0, sib_in0, recv_in0, where, name="final_half_w_in_0", halves_last=True)
    fin_in1 = _final_half(g_in1, sib_in1, recv_in1, where, name="final_half_w_in_1", halves_last=True)
    oth_in0, oth_in1 = _run_exchange(_x_share([fin_in0, fin_in1]), name="share_w_in")
    fin_in = jnp.concatenate([fin_in0, fin_in1], axis=1)
    oth_in = jnp.concatenate([oth_in0, oth_in1], axis=1)

    big = [(fin_in, oth_in), (fin_out, oth_out), (fin_up, oth_up), (fin_dn, oth_dn)]
    drgain = jnp.sum(dgain_h, axis=0)
    small = _pack(jnp.sum(dsinks, axis=1)[None, :], dg_again, dlb, jnp.zeros_like(dlb), drgain,
                  [dg_mix_pre, dg_mix_post, dg_mlp_pre, dg_mlp_post], loss=loss_row)
    return gx, big, small


def kernel(x, w_in, attn_sinks, attn_out_gain, rnn_lb_logits, rnn_norm_gain, w_out, mix_pre_gain, mix_post_gain, mlp_pre_gain, mlp_post_gain, w_up, w_down, loss_target, m_w_in, m_attn_sinks, m_attn_out_gain, m_rnn_lb_logits, m_rnn_norm_gain, m_w_out, m_mix_pre_gain, m_mix_post_gain, m_mlp_pre_gain, m_mlp_post_gain, m_w_up, m_w_down, v_w_in, v_attn_sinks, v_attn_out_gain, v_rnn_lb_logits, v_rnn_norm_gain, v_w_out, v_mix_pre_gain, v_mix_post_gain, v_mlp_pre_gain, v_mlp_post_gain, v_w_up, v_w_down):
    ax, ay, ac = _place()
    where = jnp.stack([2 * ax + ay, ac]).astype(jnp.int32)
    t = lambda a: jnp.swapaxes(a, 1, 2)
    big_w = [t(w_in), w_out, w_up, w_down]
    big_m = [t(m_w_in), m_w_out, m_w_up, m_w_down]
    big_v = [t(v_w_in), v_w_out, v_w_up, v_w_down]

    names = ["w_in", "w_out", "w_up", "w_down"]
    bufs = [_cast_slots(w, where, name="cast_" + nm) for w, nm in zip(big_w, names)]
    gx, big_g, small_part = _layer_grads(
        x[0], loss_target[0], bufs, where, attn_sinks, attn_out_gain, rnn_lb_logits, rnn_norm_gain,
        mix_pre_gain, mix_post_gain, mlp_pre_gain, mlp_post_gain)

    grads, deltas, new_m, new_v = [], [], [], []
    for (f, o), w, m, v, nm in zip(big_g, big_w, big_m, big_v, names):
        res = _adamw(w, f, o, m, v, where, name="adamw_" + nm, halves_last=(nm == "w_in"))
        if nm == "w_in":
            res = [t(r) for r in res]
        g, d, nm_, nv_ = res
        grads.append(g)
        deltas.append(d)
        new_m.append(nm_)
        new_v.append(nv_)

    def pack_params(sinks, again, logits, rgain, gains):
        return _pack(sinks, again, logits[0:1], logits[1:2], rgain, gains)

    pw = pack_params(attn_sinks, attn_out_gain, rnn_lb_logits, rnn_norm_gain,
                     [mix_pre_gain, mix_post_gain, mlp_pre_gain, mlp_post_gain])
    pm = pack_params(m_attn_sinks, m_attn_out_gain, m_rnn_lb_logits, m_rnn_norm_gain,
                     [m_mix_pre_gain, m_mix_post_gain, m_mlp_pre_gain, m_mlp_post_gain])
    pv = pack_params(v_attn_sinks, v_attn_out_gain, v_rnn_lb_logits, v_rnn_norm_gain,
                     [v_mix_pre_gain, v_mix_post_gain, v_mlp_pre_gain, v_mlp_post_gain])
    packs = _small_reduce_adamw(small_part, pw, pm, pv, name="small_reduce_adamw")

    def unpack(p):
        seg = lambda o, k: p[:, o:o + k]
        logits = jnp.concatenate([seg(SEG_L0, RNN_W), seg(SEG_L1, RNN_W)], axis=0)
        gains = [seg(SEG_G + i * D_MODEL, D_MODEL) for i in range(4)]
        return dict(sinks=seg(SEG_SINK, N_Q), again=seg(SEG_AGAIN, ATTN_W), logits=logits,
                    rgain=seg(SEG_RGAIN, RNN_HD), gains=gains)

    def order(small, big):
        return [big[0], small["sinks"], small["again"], small["logits"], small["rgain"], big[1],
                *small["gains"], big[2], big[3]]

    loss = packs[0][0, 0]
    outs = [loss, gx[None]]
    for p, b in zip(packs, [grads, deltas, new_m, new_v]):
        outs += order(unpack(p), b)
    return tuple(outs)
```

```python
import functools

import jax
import jax.numpy as jnp
from jax import lax
from jax.experimental import pallas as pl
from jax.experimental.pallas import tpu as pltpu

F32 = jnp.float32
BF16 = jnp.bfloat16
MESH = pl.DeviceIdType.MESH

EPS = 1e-6
D_MODEL = 2048
ATTN_W = 1024
HEAD_DIM = 64
N_Q = 16
N_KV = 2
GROUP = 8
BLK = 128
RNN_W = 1024
RNN_HD = 128
N_RNN = 8
CHUNK = 64
SUB_FWD = 16
SUB_BWD = 8
D_FF = 8192
IN_W = 5376
N_CHIPS = 4
KV_COL = ATTN_W
QR_COL = ATTN_W + 2 * 128
FR_COL = QR_COL + RNN_W
IR_COL = FR_COL + RNN_W
GR_COL = IR_COL + RNN_W

ADAM_LR = 0.001
ADAM_B1 = 0.9
ADAM_B2 = 0.999
ADAM_EPS = 1e-08
ADAM_WD = 0.01
ADAM_STEP = 10

VMEM_LIMIT = 48 * 1024 * 1024

NT = (((1,), (1,)), ((), ()))
TN = (((0,), (0,)), ((), ()))


def _params(sem=None):
    return pltpu.CompilerParams(dimension_semantics=sem, vmem_limit_bytes=VMEM_LIMIT)


def _sigmoid(x):
    return 1.0 / (1.0 + jnp.exp(-x))


ANY = pl.BlockSpec(memory_space=pl.ANY)


def _place():
    return lax.axis_index("x"), lax.axis_index("y"), lax.axis_index("c")


def _other_chips(x, y):
    return [(1 - x, y), (x, 1 - y), (1 - x, 1 - y)]


class _Exchange:
    def __init__(self, srcs, outs, ncopy, build, aliases=None):
        self.srcs, self.outs, self.ncopy, self.build = list(srcs), list(outs), ncopy, build
        self.aliases = aliases or {}


def _remote(src, dst, send_sems, recv_sems, k, to):
    return pltpu.make_async_remote_copy(src_ref=src, dst_ref=dst, send_sem=send_sems.at[k],
                                        recv_sem=recv_sems.at[k], device_id=to, device_id_type=MESH)


def _call(body, *, name, grid, in_specs, out_specs, out_shape, args, scratch_shapes=(), semantics=None,
          exchanges=(), into=None):
    in_specs, out_specs, out_shape = list(in_specs), list(out_specs), list(out_shape)
    scratch_shapes = list(scratch_shapes)
    ni, no, ns = len(in_specs), len(out_specs), len(scratch_shapes)
    xsrc = [s for x in exchanges for s in x.srcs]
    xout = [o for x in exchanges for o in x.outs]
    into = into or {}
    xsrc += [into[k] for k in sorted(into)]
    nxi, nxo = len(xsrc), len(xout)
    aliases = {nxi - len(into) + ni + q: k for q, k in enumerate(sorted(into))}
    a0 = b0 = 0
    for x in exchanges:
        for si, oi in x.aliases.items():
            aliases[ni + a0 + si] = no + b0 + oi
        a0 += len(x.srcs)
        b0 += len(x.outs)
    sems = []
    for x in exchanges:
        sems += [pltpu.SemaphoreType.DMA((x.ncopy,)), pltpu.SemaphoreType.DMA((x.ncopy,))]

    def wrapped(*refs):
        ins, xi = refs[:ni], refs[ni:ni + nxi]
        outs, xo = refs[ni + nxi:ni + nxi + no], refs[ni + nxi + no:ni + nxi + no + nxo]
        rest = refs[ni + nxi + no + nxo:]
        scr, sm = rest[:ns], rest[ns:]

        def copies():
            cps = []
            a = b = 0
            for k, x in enumerate(exchanges):
                cps += x.build(xi[a:a + len(x.srcs)], xo[b:b + len(x.outs)], sm[2 * k], sm[2 * k + 1])
                a += len(x.srcs)
                b += len(x.outs)
            return cps

        def start():
            for cp in copies():
                cp.start()

        def wait():
            for cp in copies():
                cp.wait()

        if not exchanges:
            body(*ins, *outs, *scr)
        elif not grid:
            start()
            body(*ins, *outs, *scr)
            wait()
        else:
            first = last = None
            for ax, g in enumerate(grid):
                f = pl.program_id(ax) == 0
                l = pl.program_id(ax) == g - 1
                first = f if first is None else first & f
                last = l if last is None else last & l
            pl.when(first)(start)
            body(*ins, *outs, *scr)
            pl.when(last)(wait)

    if exchanges and semantics is not None:
        semantics = ("arbitrary",) * len(grid)
    kwargs = dict(grid=grid) if grid else {}
    res = pl.pallas_call(
        wrapped, name=name,
        in_specs=in_specs + [ANY] * nxi, out_specs=out_specs + [ANY] * nxo,
        out_shape=out_shape + xout, scratch_shapes=scratch_shapes + sems,
        input_output_aliases=aliases,
        compiler_params=_params(semantics), **kwargs,
    )(*args, *xsrc)
    res = list(res)
    mine, theirs = res[:no], res[no:]
    per = []
    b = 0
    for x in exchanges:
        per.append(theirs[b:b + len(x.outs)])
        b += len(x.outs)
    return mine, per


def _run_exchange(x, *, name):
    return _call(lambda: None, name=name, grid=(), in_specs=[], out_specs=[], out_shape=[], args=[],
                 exchanges=[x])[1][0]


def _x_gather(bufs, ici=None, d2d=None):
    n = len(bufs)
    plan = [(a, kind, rows[a]) for a in range(n) for kind, rows in (("ici", ici), ("d2d", d2d))
            if rows is not None and rows[a] is not None]

    def build(srcs, outs, ss, rs):
        x, y, c = _place()
        cps = []
        for q, (a, kind, rows) in enumerate(plan):
            piece = pl.ds(*rows)
            for j, (px, py) in enumerate(_other_chips(x, y)):
                slot, to = (4 * x + 2 * y + c, (px, py, c)) if kind == "ici" else (4 * px + 2 * py + c, (x, y, 1 - c))
                cps.append(_remote(srcs[a].at[slot, piece], outs[a].at[slot, piece], ss, rs, 3 * q + j, to))
        return cps

    outs = [jax.ShapeDtypeStruct(b.shape, b.dtype) for b in bufs]
    return _Exchange(bufs, outs, 3 * len(plan), build, aliases={a: a for a in range(n)})


def _x_pair(grads, halves_last=False):
    n = len(grads)

    def build(srcs, outs, ss, rs):
        x, y, c = _place()

        def half(r):
            if not halves_last:
                return r.at[:, 1 - c]
            ch = r.shape[2] // 2
            return r.at[:, :, pl.ds(pl.multiple_of((1 - c) * ch, 128), ch)]

        return [_remote(half(srcs[a]), outs[a], ss, rs, a, (x, y, 1 - c)) for a in range(n)]

    if halves_last:
        outs = [jax.ShapeDtypeStruct(g.shape[:2] + (g.shape[2] // 2,), g.dtype) for g in grads]
    else:
        outs = [jax.ShapeDtypeStruct((4,) + g.shape[2:], g.dtype) for g in grads]
    return _Exchange(grads, outs, n, build)


def _x_chip(wires, rows=None, into=None):
    n = len(wires)
    rows = rows or [(0, w.shape[1]) for w in wires]

    def build(srcs, outs, ss, rs):
        x, y, c = _place()
        cps = []
        for a in range(n):
            piece = pl.ds(*rows[a])
            for j, (px, py) in enumerate(_other_chips(x, y)):
                cps.append(_remote(srcs[a].at[2 * px + py, piece], outs[a].at[j, piece], ss, rs,
                                   3 * a + j, (px, py, c)))
        return cps

    outs = [jax.ShapeDtypeStruct((3,) + w.shape[1:], w.dtype) for w in wires]
    if into is None:
        return _Exchange(wires, outs, 3 * n, build)
    return _Exchange(list(wires) + list(into), outs, 3 * n, build, aliases={n + a: a for a in range(n)})


def _x_share(halves):
    n = len(halves)

    def build(srcs, outs, ss, rs):
        x, y, c = _place()
        return [_remote(srcs[a], outs[a], ss, rs, a, (x, y, 1 - c)) for a in range(n)]

    outs = [jax.ShapeDtypeStruct(h.shape, h.dtype) for h in halves]
    return _Exchange(halves, outs, n, build)


def _mm(a, w, *, tm, tn, tk, out_dtype, name, a_square=False, relu=False, mul2=None, w_layout="kn",
        m_blocks=None, out_into=None, exchanges=()):
    m, k = a.shape
    m_first, m_count = m_blocks or (0, m // tm)
    a_spec = pl.BlockSpec((tm, tk), lambda i, j, kk: (i + m_first, kk))
    if w_layout == "kn":
        n = w.shape[1]
        w_spec = pl.BlockSpec((tk, tn), lambda i, j, kk: (kk, j))
    elif w_layout == "nk":
        n = w.shape[0]
        w_spec = pl.BlockSpec((tn, tk), lambda i, j, kk: (j, kk))
    elif w_layout == "skn":
        n = w.shape[0] * w.shape[2]
        per_n = w.shape[2] // tn
        w_spec = pl.BlockSpec((None, tk, tn), lambda i, j, kk: (j // per_n, kk, j % per_n))
    else:
        assert w_layout == "snk"
        n = w.shape[1]
        per_k = w.shape[2] // tk
        w_spec = pl.BlockSpec((None, tn, tk), lambda i, j, kk: (kk // per_k, j, kk % per_k))
    w_dims = NT if w_layout in ("nk", "snk") else (((1,), (0,)), ((), ()))
    nk = k // tk
    assert m % tm == 0 and n % tn == 0 and k % tk == 0

    def body(*refs):
        if mul2 is not None:
            a_ref, w_ref, e_ref, o_ref, acc_ref = refs
        else:
            a_ref, w_ref, o_ref, acc_ref = refs
            e_ref = None
        kk = pl.program_id(2)
        av = a_ref[...]
        if a_square:
            af = av.astype(F32)
            av = (af * af).astype(BF16)
        part = lax.dot_general(av, w_ref[...], w_dims, preferred_element_type=F32)

        def finish(r):
            if relu:
                r = jnp.maximum(r, 0.0)
            if e_ref is not None:
                r = 2.0 * e_ref[...].astype(F32) * r
            o_ref[...] = r.astype(out_dtype)

        if nk == 1:
            finish(part)
        else:
            @pl.when(kk == 0)
            def _():
                acc_ref[...] = part

            @pl.when(kk > 0)
            def _():
                acc_ref[...] += part

            @pl.when(kk == nk - 1)
            def _():
                finish(acc_ref[...])

    in_specs = [a_spec, w_spec]
    args = [a, w]
    if mul2 is not None:
        in_specs.append(pl.BlockSpec((tm, tn), lambda i, j, kk: (i + m_first, j)))
        args.append(mul2)
    acc_shape = (tm, tn) if nk > 1 else (8, 128)
    (out,), per = _call(
        body, name=name, grid=(m_count, n // tn, nk),
        in_specs=in_specs, out_specs=[pl.BlockSpec((tm, tn), lambda i, j, kk: (i + m_first, j))],
        out_shape=[jax.ShapeDtypeStruct((m, n), out_dtype)], args=args,
        scratch_shapes=[pltpu.VMEM(acc_shape, F32)],
        semantics=("parallel", "parallel", "arbitrary"), exchanges=exchanges,
        into=None if out_into is None else {0: out_into})
    return (out, per) if exchanges else out


def _mm_tn(a, b, *, tm, tn, tt, name, a_square=False, n_split=1, n_blocks=None, exchanges=()):
    t, m = a.shape
    n = b.shape[1]
    assert t % tt == 0 and m % tm == 0 and n % tn == 0
    count, stride, first = n_blocks or (n // tn, 1, 0)
    n = count * tn
    assert (n // n_split) % tn == 0
    per = n // n_split // tn

    def body(a_ref, b_ref, o_ref):
        ti = pl.program_id(2)
        av = a_ref[...]
        if a_square:
            af = av.astype(F32)
            av = (af * af).astype(BF16)
        part = lax.dot_general(av, b_ref[...], TN, preferred_element_type=F32)

        @pl.when(ti == 0)
        def _():
            o_ref[...] = part

        @pl.when(ti > 0)
        def _():
            o_ref[...] += part

    (out,), xres = _call(
        body, name=name, grid=(m // tm, n // tn, t // tt),
        in_specs=[pl.BlockSpec((tt, tm), lambda i, j, ti: (ti, i)),
                  pl.BlockSpec((tt, tn), lambda i, j, ti: (ti, first + stride * j))],
        out_specs=[pl.BlockSpec((None, tm, tn), lambda i, j, ti: (j // per, i, j % per))],
        out_shape=[jax.ShapeDtypeStruct((n_split, m, n // n_split), F32)], args=[a, b],
        semantics=("parallel", "parallel", "arbitrary"), exchanges=exchanges)
    return (out, xres) if exchanges else out


def _rstd(x):
    return lax.rsqrt(jnp.mean(x * x, axis=-1, keepdims=True) + EPS)


def _rms_cast(x, g, *, tm, name):
    t, d = x.shape

    def body(x_ref, g_ref, o_ref):
        xv = x_ref[...]
        o_ref[...] = (xv * _rstd(xv) * g_ref[...]).astype(BF16)

    return pl.pallas_call(
        body, name=name, grid=(t // tm,),
        in_specs=[pl.BlockSpec((tm, d), lambda i: (i, 0)), pl.BlockSpec((1, d), lambda i: (0, 0))],
        out_specs=pl.BlockSpec((tm, d), lambda i: (i, 0)),
        out_shape=jax.ShapeDtypeStruct((t, d), BF16),
        compiler_params=_params(("parallel",)),
    )(x, g)


def _mix_cat(attn, rnn, gain, *, tm, name):
    t = attn.shape[0]

    def body(a_ref, r_ref, g_ref, o_ref):
        av = a_ref[...]
        o_ref[:, :ATTN_W] = (av * _rstd(av) * g_ref[...]).astype(BF16)
        o_ref[:, ATTN_W:] = r_ref[...].astype(BF16)

    return pl.pallas_call(
        body, name=name, grid=(t // tm,),
        in_specs=[pl.BlockSpec((tm, ATTN_W), lambda i: (i, 0)), pl.BlockSpec((tm, RNN_W), lambda i: (i, 0)),
                  pl.BlockSpec((1, ATTN_W), lambda i: (0, 0))],
        out_specs=pl.BlockSpec((tm, D_MODEL), lambda i: (i, 0)),
        out_shape=jax.ShapeDtypeStruct((t, D_MODEL), BF16),
        compiler_params=_params(("parallel",)),
    )(attn, rnn, gain)


def _post_norm_res(mixed, g_post, res, g_next, *, tm, name, exchanges=()):
    t, d = mixed.shape

    def body(m_ref, gp_ref, r_ref, gn_ref, x1_ref, h2_ref):
        mv = m_ref[...].astype(F32)
        x1 = r_ref[...] + mv * _rstd(mv) * gp_ref[...]
        x1_ref[...] = x1
        h2_ref[...] = (x1 * _rstd(x1) * gn_ref[...]).astype(BF16)

    row = pl.BlockSpec((tm, d), lambda i: (i, 0))
    vec = pl.BlockSpec((1, d), lambda i: (0, 0))
    res_, xres = _call(
        body, name=name, grid=(t // tm,),
        in_specs=[row, vec, row, vec], out_specs=[row, row],
        out_shape=[jax.ShapeDtypeStruct((t, d), F32), jax.ShapeDtypeStruct((t, d), BF16)],
        args=[mixed, g_post, res, g_next], semantics=("parallel",), exchanges=exchanges)
    return (*res_, xres) if exchanges else res_


def _rms_bwd(dyn, xin, g, res, *, tm, out_dtype, name, col_block=0, exchanges=()):
    t, d = xin.shape

    def body(*refs):
        if res is not None:
            dy_ref, x_ref, g_ref, r_ref, dx_ref, dg_ref = refs
        else:
            dy_ref, x_ref, g_ref, dx_ref, dg_ref = refs
        i = pl.program_id(0)
        xv = x_ref[...].astype(F32)
        dy = dy_ref[...].astype(F32)
        r = _rstd(xv)
        xh = xv * r
        part = jnp.sum(dy * xh, axis=0, keepdims=True)

        @pl.when(i == 0)
        def _():
            dg_ref[...] = part

        @pl.when(i > 0)
        def _():
            dg_ref[...] += part

        tt = dy * g_ref[...]
        dx = r * (tt - xh * jnp.mean(tt * xh, axis=-1, keepdims=True))
        if res is not None:
            dx = dx + r_ref[...]
        dx_ref[...] = dx.astype(out_dtype)

    row = pl.BlockSpec((tm, d), lambda i: (i, 0))
    vec = pl.BlockSpec((1, d), lambda i: (0, 0))
    in_specs = [pl.BlockSpec((tm, d), lambda i: (i, col_block)), row, vec]
    args = [dyn, xin, g]
    if res is not None:
        in_specs.append(row)
        args.append(res)
    res, xres = _call(
        body, name=name, grid=(t // tm,),
        in_specs=in_specs, out_specs=[row, vec],
        out_shape=[jax.ShapeDtypeStruct((t, d), out_dtype), jax.ShapeDtypeStruct((1, d), F32)], args=args,
        semantics=("arbitrary",), exchanges=exchanges)
    return (*res, xres) if exchanges else res


def _loss_head(y, g_post, x1, target, *, tm, name):
    t, d = y.shape

    def body(y_ref, g_ref, x1_ref, t_ref, dy_ref, dx2_ref, loss_ref, dg_ref):
        i = pl.program_id(0)
        yv = y_ref[...].astype(F32)
        r = _rstd(yv)
        yh = yv * r
        gv = g_ref[...]
        err = x1_ref[...] + yh * gv - t_ref[...]
        lpart = 0.5 * jnp.sum(jnp.mean(err * err, axis=-1, keepdims=True), axis=0, keepdims=True)
        dx2 = err * (1.0 / d)
        dgp = jnp.sum(dx2 * yh, axis=0, keepdims=True)
        lane = lax.broadcasted_iota(jnp.int32, (1, 128), 1)
        lrow = jnp.where(lane == 0, lpart, 0.0)

        @pl.when(i == 0)
        def _():
            dg_ref[...] = dgp
            loss_ref[...] = lrow

        @pl.when(i > 0)
        def _():
            dg_ref[...] += dgp
            loss_ref[...] += lrow

        tt = dx2 * gv
        dy_ref[...] = (r * (tt - yh * jnp.mean(tt * yh, axis=-1, keepdims=True))).astype(BF16)
        dx2_ref[...] = dx2

    row = pl.BlockSpec((tm, d), lambda i: (i, 0))
    vec = pl.BlockSpec((1, d), lambda i: (0, 0))
    return pl.pallas_call(
        body, name=name, grid=(t // tm,),
        in_specs=[row, vec, row, row],
        out_specs=[row, row, pl.BlockSpec((1, 128), lambda i: (0, 0)), vec],
        out_shape=[jax.ShapeDtypeStruct((t, d), BF16), jax.ShapeDtypeStruct((t, d), F32),
                   jax.ShapeDtypeStruct((1, 128), F32), jax.ShapeDtypeStruct((1, d), F32)],
        compiler_params=_params(("arbitrary",)),
    )(y, g_post, x1, target)


def _alibi_slope(h):
    return 2.0 ** (-8.0 * (h + 1) / N_Q)


PAIR = 2 * HEAD_DIM
N_PAIRS = N_Q // 2
PAIRS_PER_KV = GROUP // 2
SMEM = pl.BlockSpec(memory_space=pltpu.SMEM)


def _swa_mask(n):
    key = lax.broadcasted_iota(jnp.int32, (2 * BLK, BLK), 0)
    qry = lax.broadcasted_iota(jnp.int32, (2 * BLK, BLK), 1)
    dist = qry + BLK - key
    valid = (dist >= 0) & (dist < BLK) & ((key >= BLK) | (n > 0))
    return valid, dist.astype(F32)


def _block_diag(kvp_ref, kvc_ref, off):
    a = jnp.concatenate([kvp_ref[:, off:off + HEAD_DIM], kvc_ref[:, off:off + HEAD_DIM]], axis=0).astype(BF16)
    z = jnp.zeros_like(a)
    return jnp.concatenate([jnp.concatenate([a, z], axis=1), jnp.concatenate([z, a], axis=1)], axis=0)


def _swa_scores(s2, e, hh, valid, distf):
    s = s2[2 * BLK * e:2 * BLK * (e + 1)] * (HEAD_DIM ** -0.5) - _alibi_slope(hh) * distf
    return jnp.where(valid, s, -1e30)


def _swa_fwd(proj, sinks, *, name, exchanges=()):
    t = proj.shape[0]
    nb = t // BLK
    kvb = KV_COL // (2 * 128)

    def body(sink_ref, q_ref, kvc_ref, kvp_ref, o_ref, lse_ref):
        n = pl.program_id(0)
        valid, distf = _swa_mask(n)
        for kvh in range(N_KV):
            k2 = _block_diag(kvp_ref, kvc_ref, kvh * HEAD_DIM)
            v2 = _block_diag(kvp_ref, kvc_ref, 128 + kvh * HEAD_DIM)
            for jp in range(PAIRS_PER_KV):
                pair = kvh * PAIRS_PER_KV + jp
                lanes = slice(pair * PAIR, (pair + 1) * PAIR)
                s2 = lax.dot_general(k2, q_ref[:, lanes].astype(BF16), NT, preferred_element_type=F32)
                probs = []
                for e in range(2):
                    hh = 2 * pair + e
                    s = _swa_scores(s2, e, hh, valid, distf)
                    sink = sink_ref[0, hh]
                    mx = jnp.maximum(jnp.max(s, axis=0, keepdims=True), sink)
                    p = jnp.exp(s - mx)
                    l = jnp.sum(p, axis=0, keepdims=True) + jnp.exp(sink - mx)
                    probs.append((p * (1.0 / l)).astype(BF16))
                    lse_ref[hh:hh + 1, :] = mx + jnp.log(l)
                o_ref[:, lanes] = lax.dot_general(jnp.concatenate(probs, axis=0), v2, TN,
                                                  preferred_element_type=F32)

    res, xres = _call(
        body, name=name, grid=(nb,),
        in_specs=[SMEM,
                  pl.BlockSpec((BLK, ATTN_W), lambda n: (n, 0)),
                  pl.BlockSpec((BLK, 256), lambda n: (n, kvb)),
                  pl.BlockSpec((BLK, 256), lambda n: (jnp.maximum(n - 1, 0), kvb))],
        out_specs=[pl.BlockSpec((BLK, ATTN_W), lambda n: (n, 0)),
                   pl.BlockSpec((None, N_Q, BLK), lambda n: (n, 0, 0))],
        out_shape=[jax.ShapeDtypeStruct((t, ATTN_W), F32), jax.ShapeDtypeStruct((nb, N_Q, BLK), F32)],
        args=[sinks, proj, proj, proj], semantics=("parallel",), exchanges=exchanges)
    return (*res, xres) if exchanges else res


def _swa_bwd(proj, sinks, dattn, lse, *, name, exchanges=()):
    t = proj.shape[0]
    nb = t // BLK
    kvb = KV_COL // (2 * 128)

    def body(sink_ref, q_ref, kvc_ref, kvp_ref, do_ref, lse_ref, dq_ref, dkv_ref, dsink_ref, carry_ref):
        n = pl.program_id(0)

        @pl.when(n == 0)
        def _():
            dsink_ref[...] = jnp.zeros_like(dsink_ref)
            carry_ref[...] = jnp.zeros_like(carry_ref)

        @pl.when(n < nb)
        def _():
            valid, distf = _swa_mask(n)
            for kvh in range(N_KV):
                k2 = _block_diag(kvp_ref, kvc_ref, kvh * HEAD_DIM)
                v2 = _block_diag(kvp_ref, kvc_ref, 128 + kvh * HEAD_DIM)
                dk2 = jnp.zeros((4 * BLK, PAIR), F32)
                dv2 = jnp.zeros((4 * BLK, PAIR), F32)
                for jp in range(PAIRS_PER_KV):
                    pair = kvh * PAIRS_PER_KV + jp
                    lanes = slice(pair * PAIR, (pair + 1) * PAIR)
                    q2 = q_ref[:, lanes].astype(BF16)
                    do2 = do_ref[:, lanes].astype(BF16)
                    s2 = lax.dot_general(k2, q2, NT, preferred_element_type=F32)
                    dp2 = lax.dot_general(v2, do2, NT, preferred_element_type=F32)
                    probs, dss = [], []
                    for e in range(2):
                        hh = 2 * pair + e
                        lse_h = lse_ref[hh:hh + 1, :]
                        p = jnp.exp(_swa_scores(s2, e, hh, valid, distf) - lse_h)
                        dp = dp2[2 * BLK * e:2 * BLK * (e + 1)]
                        delta = jnp.sum(p * dp, axis=0, keepdims=True)
                        dsink_ref[hh:hh + 1, :] += -jnp.exp(sink_ref[0, hh] - lse_h) * delta
                        probs.append(p.astype(BF16))
                        dss.append((p * (dp - delta)).astype(BF16))
                    ds2 = jnp.concatenate(dss, axis=0)
                    dq_ref[:, lanes] = (lax.dot_general(ds2, k2, TN, preferred_element_type=F32)
                                        * (HEAD_DIM ** -0.5)).astype(BF16)
                    dk2 = dk2 + jnp.dot(ds2, q2, preferred_element_type=F32)
                    dv2 = dv2 + jnp.dot(jnp.concatenate(probs, axis=0), do2, preferred_element_type=F32)
                dk_cat = (dk2[:2 * BLK, :HEAD_DIM] + dk2[2 * BLK:, HEAD_DIM:]) * (HEAD_DIM ** -0.5)
                dv_cat = dv2[:2 * BLK, :HEAD_DIM] + dv2[2 * BLK:, HEAD_DIM:]
                ko = kvh * HEAD_DIM
                vo = 128 + kvh * HEAD_DIM
                dkv_ref[:, ko:ko + HEAD_DIM] = (carry_ref[:, ko:ko + HEAD_DIM] + dk_cat[:BLK]).astype(BF16)
                dkv_ref[:, vo:vo + HEAD_DIM] = (carry_ref[:, vo:vo + HEAD_DIM] + dv_cat[:BLK]).astype(BF16)
                carry_ref[:, ko:ko + HEAD_DIM] = dk_cat[BLK:]
                carry_ref[:, vo:vo + HEAD_DIM] = dv_cat[BLK:]

        @pl.when(n == nb)
        def _():
            dkv_ref[...] = carry_ref[...].astype(BF16)

    last = nb - 1
    res, xres = _call(
        body, name=name, grid=(nb + 1,),
        in_specs=[SMEM,
                  pl.BlockSpec((BLK, ATTN_W), lambda n: (jnp.minimum(n, last), 0)),
                  pl.BlockSpec((BLK, 256), lambda n: (jnp.minimum(n, last), kvb)),
                  pl.BlockSpec((BLK, 256), lambda n: (jnp.maximum(jnp.minimum(n, last) - 1, 0), kvb)),
                  pl.BlockSpec((BLK, ATTN_W), lambda n: (jnp.minimum(n, last), 0)),
                  pl.BlockSpec((None, N_Q, BLK), lambda n: (jnp.minimum(n, last), 0, 0))],
        out_specs=[pl.BlockSpec((BLK, ATTN_W), lambda n: (jnp.minimum(n, last), 0)),
                   pl.BlockSpec((BLK, 256), lambda n: (jnp.maximum(n - 1, 0), 0)),
                   pl.BlockSpec((N_Q, BLK), lambda n: (0, 0))],
        out_shape=[jax.ShapeDtypeStruct((t, ATTN_W), BF16), jax.ShapeDtypeStruct((t, 256), BF16),
                   jax.ShapeDtypeStruct((N_Q, BLK), F32)],
        scratch_shapes=[pltpu.VMEM((BLK, 256), F32)],
        args=[sinks, proj, proj, proj, dattn, lse], semantics=("arbitrary",), exchanges=exchanges)
    return (*res, xres) if exchanges else res


def _cumsum_rows(x):
    n = x.shape[0]
    row = lax.broadcasted_iota(jnp.int32, x.shape, 0)
    s = 1
    while s < n:
        x = x + jnp.where(row >= s, pltpu.roll(x, s, axis=0), 0.0)
        s *= 2
    return x


def _rev_cumsum_rows(x):
    n = x.shape[0]
    row = lax.broadcasted_iota(jnp.int32, x.shape, 0)
    s = 1
    while s < n:
        x = x + jnp.where(row < n - s, pltpu.roll(x, n - s, axis=0), 0.0)
        s *= 2
    return x


def _lower_bound(lbl_ref):
    l0 = lbl_ref[0:1, :]
    l1 = lbl_ref[1:2, :]
    mx = jnp.maximum(l0, l1)
    e0 = jnp.exp(l0 - mx)
    e1 = jnp.exp(l1 - mx)
    return e0 / (e0 + e1)


def _hgrn_gates(z, lb):
    sg = _sigmoid(z)
    f = lb + (1.0 - lb) * sg
    return sg, f, jnp.log(f), 1.0 - f


def _sub_factors(b, k, i, sub, trim):
    need = -(-sub * i // 16) * 16 if trim else CHUNK
    rows = lax.broadcasted_iota(jnp.int32, (need, RNN_HD), 0)
    ref = b[sub * i - 1:sub * i, :]
    qfac = jnp.exp(b[sub * i:sub * (i + 1), :] - ref)
    kfac = jnp.where(rows < sub * i, jnp.exp(ref - b[:need]), 0.0)
    kt = (k[:need] * kfac).astype(BF16)
    if need < CHUNK:
        kt = jnp.concatenate([kt, jnp.zeros((CHUNK - need, RNN_HD), BF16)], axis=0)
    return qfac, kfac, kt


def _diag_decay(bi, s):
    trow = lax.broadcasted_iota(jnp.int32, bi.shape, 0)
    return jnp.where(trow >= s, jnp.exp(bi - bi[s:s + 1, :]), 0.0)


def _hgrn_fwd(proj, lb_logits, norm_gain, *, tb, name, exchanges=()):
    t = proj.shape[0]
    ntb = t // tb
    nch = tb // CHUNK
    qb, fb, ib, gb = QR_COL // 128, FR_COL // 128, IR_COL // 128, GR_COL // 128

    def body(q_ref, f_ref, i_ref, g_ref, lbl_ref, gain_ref, o_ref, out_ref, s0_ref, st_ref):
        c = pl.program_id(1)

        @pl.when(c == 0)
        def _():
            st_ref[...] = jnp.zeros_like(st_ref)

        lb = _lower_bound(lbl_ref)
        gain = gain_ref[...]

        def chunk(ci, st):
            rows = slice(ci * CHUNK, (ci + 1) * CHUNK)
            _, _, lf, k = _hgrn_gates(f_ref[rows, :], lb)
            qr = q_ref[rows, :]
            q = qr * _sigmoid(qr)
            v = i_ref[rows, :]
            b = _cumsum_rows(lf)
            s0_ref[ci] = st
            o_inter = lax.dot_general((q * jnp.exp(b)).astype(BF16), st.astype(BF16), NT,
                                      preferred_element_type=F32)
            vb = v.astype(BF16)
            blast = b[CHUNK - 1:CHUNK, :]
            khat = (k * jnp.exp(blast - b)).astype(BF16)
            st = st * jnp.exp(blast) + lax.dot_general(vb, khat, TN, preferred_element_type=F32)
            blocks = []
            for i in range(CHUNK // SUB_FWD):
                blk = slice(SUB_FWD * i, SUB_FWD * (i + 1))
                qi, ki, vi, bi = q[blk], k[blk], v[blk], b[blk]
                oi = o_inter[blk]
                if i > 0:
                    qfac, _, kt = _sub_factors(b, k, i, SUB_FWD, trim=True)
                    att = lax.dot_general((qi * qfac).astype(BF16), kt, NT,
                                          preferred_element_type=F32)
                    oi = oi + jnp.dot(att.astype(BF16), vb, preferred_element_type=F32)
                for s in range(SUB_FWD):
                    qe = qi * _diag_decay(bi, s)
                    a = jnp.sum(qe * ki[s:s + 1, :], axis=1, keepdims=True)
                    oi = oi + a * vi[s:s + 1, :]
                blocks.append(oi)
            o = jnp.concatenate(blocks, axis=0)
            o_ref[rows, :] = o
            gr = g_ref[rows, :]
            out_ref[rows, :] = o * _rstd(o) * gain * (gr * _sigmoid(gr))
            return st

        st = st_ref[...]
        for ci in range(nch):
            st = chunk(ci, st)
        st_ref[...] = st

    def col(base):
        return pl.BlockSpec((tb, RNN_HD), lambda h, c: (c, base + h))

    res, xres = _call(
        body, name=name, grid=(N_RNN, ntb),
        in_specs=[col(qb), col(fb), col(ib), col(gb),
                  pl.BlockSpec((2, RNN_HD), lambda h, c: (0, h)), pl.BlockSpec((1, RNN_HD), lambda h, c: (0, 0))],
        out_specs=[pl.BlockSpec((tb, RNN_HD), lambda h, c: (c, h)), pl.BlockSpec((tb, RNN_HD), lambda h, c: (c, h)),
                   pl.BlockSpec((None, nch, RNN_HD, RNN_HD), lambda h, c: (h, c, 0, 0))],
        out_shape=[jax.ShapeDtypeStruct((t, RNN_W), F32), jax.ShapeDtypeStruct((t, RNN_W), F32),
                   jax.ShapeDtypeStruct((N_RNN, t // CHUNK, RNN_HD, RNN_HD), F32)],
        scratch_shapes=[pltpu.VMEM((RNN_HD, RNN_HD), F32)],
        args=[proj, proj, proj, proj, lb_logits, norm_gain],
        semantics=("parallel", "arbitrary"), exchanges=exchanges)
    return (*res, xres) if exchanges else res


def _hgrn_bwd(proj, lb_logits, norm_gain, o_pre, s0, dcat, *, tb, name, exchanges=()):
    t = proj.shape[0]
    ntb = t // tb
    nch = tb // CHUNK
    qb, fb, ib, gb = QR_COL // 128, FR_COL // 128, IR_COL // 128, GR_COL // 128
    sub = SUB_BWD
    nsub = CHUNK // sub

    def body(q_ref, f_ref, i_ref, g_ref, lbl_ref, gain_ref, o_ref, s0_ref, dout_ref,
             dq_ref, df_ref, di_ref, dg_ref, dlb_ref, dgain_ref,
             dst_ref, dqs_ref, dks_ref, dvs_ref):
        c = pl.program_id(1)

        @pl.when(c == 0)
        def _():
            dst_ref[...] = jnp.zeros_like(dst_ref)
            dlb_ref[...] = jnp.zeros_like(dlb_ref)
            dgain_ref[...] = jnp.zeros_like(dgain_ref)

        lb = _lower_bound(lbl_ref)
        gain = gain_ref[...]

        def chunk(ci, dst):
            rows = slice(ci * CHUNK, (ci + 1) * CHUNK)
            dqa_ref, dka_ref, dva_ref = dqs_ref.at[ci], dks_ref.at[ci], dvs_ref.at[ci]
            sg, f, lf, k = _hgrn_gates(f_ref[rows, :], lb)
            qr = q_ref[rows, :]
            sq = _sigmoid(qr)
            q = qr * sq
            v = i_ref[rows, :]
            b = _cumsum_rows(lf)

            dout = dout_ref[rows, :].astype(F32)
            o = o_ref[rows, :]
            gr = g_ref[rows, :]
            sgg = _sigmoid(gr)
            gate = gr * sgg
            rs = _rstd(o)
            nrm = o * rs
            dg_ref[rows, :] = (dout * nrm * gain * (sgg * (1.0 + gr * (1.0 - sgg)))).astype(BF16)
            dn = dout * gate
            dgain_ref[...] += jnp.sum(dn * nrm, axis=0, keepdims=True)
            tt = dn * gain
            do = rs * (tt - nrm * jnp.mean(tt * nrm, axis=-1, keepdims=True))

            dob = do.astype(BF16)
            vb = v.astype(BF16)
            eb = jnp.exp(b)
            blast = b[CHUNK - 1:CHUNK, :]
            ebl = jnp.exp(blast - b)
            dstb = dst.astype(BF16)
            khat = (k * ebl).astype(BF16)
            s0 = s0_ref[ci]
            dqa_ref[...] = eb * jnp.dot(dob, s0.astype(BF16), preferred_element_type=F32)
            dk_state = ebl * jnp.dot(vb, dstb, preferred_element_type=F32)
            dka_ref[...] = dk_state
            d_blast = (jnp.sum(k * dk_state, axis=0, keepdims=True)
                       + jnp.exp(blast) * jnp.sum(dst * s0, axis=0, keepdims=True))
            dva_ref[...] = lax.dot_general(khat, dstb, NT, preferred_element_type=F32)
            dst_next = dst * jnp.exp(blast) + lax.dot_general(dob, (q * eb).astype(BF16), TN,
                                                              preferred_element_type=F32)
            pm = lax.dot_general(dob, vb, NT, preferred_element_type=F32)
            for i in range(nsub):
                blk = slice(sub * i, sub * (i + 1))
                qi, ki, vi, bi, doi = q[blk], k[blk], v[blk], b[blk], do[blk]
                dqi = dqa_ref[blk, :]
                if i > 0:
                    qfac, kfac, kt = _sub_factors(b, k, i, sub, trim=False)
                    qt = (qi * qfac).astype(BF16)
                    att = lax.dot_general(qt, kt, NT, preferred_element_type=F32).astype(BF16)
                    pmi = pm[blk, :].astype(BF16)
                    dva_ref[...] += lax.dot_general(att, doi.astype(BF16), TN, preferred_element_type=F32)
                    dqi = dqi + qfac * jnp.dot(pmi, kt, preferred_element_type=F32)
                    dka_ref[...] += kfac * lax.dot_general(pmi, qt, TN, preferred_element_type=F32)
                dqa_ref[blk, :] = dqi
                srow = lax.broadcasted_iota(jnp.int32, (sub, RNN_HD), 0)
                dki = jnp.zeros((sub, RNN_HD), F32)
                dvi = jnp.zeros((sub, RNN_HD), F32)
                for tq in range(sub):
                    qt, dot_ = qi[tq:tq + 1, :], doi[tq:tq + 1, :]
                    e = jnp.where(srow <= tq, jnp.exp(bi[tq:tq + 1, :] - bi), 0.0)
                    ke = ki * e
                    p = jnp.sum(vi * dot_, axis=1, keepdims=True)
                    a = jnp.sum(ke * qt, axis=1, keepdims=True)
                    dki = dki + p * (qt * e)
                    dvi = dvi + a * dot_
                    row = slice(sub * i + tq, sub * i + tq + 1)
                    dqa_ref[row, :] += jnp.sum(p * ke, axis=0, keepdims=True)
                dka_ref[blk, :] += dki
                dva_ref[blk, :] += dvi

            dq = dqa_ref[...]
            dk = dka_ref[...]
            lastrow = lax.broadcasted_iota(jnp.int32, (CHUNK, RNN_HD), 0) == CHUNK - 1
            dlf = _rev_cumsum_rows(q * dq - k * dk + jnp.where(lastrow, d_blast, 0.0))
            dff = dlf / f - dk
            df_ref[rows, :] = (dff * (1.0 - lb) * sg * (1.0 - sg)).astype(BF16)
            dlb_ref[...] += jnp.sum(dff * (1.0 - sg), axis=0, keepdims=True)
            dq_ref[rows, :] = (dq * (sq * (1.0 + qr * (1.0 - sq)))).astype(BF16)
            di_ref[rows, :] = dva_ref[...].astype(BF16)
            return dst_next

        dst = dst_ref[...]
        for ci in reversed(range(nch)):
            dst = chunk(ci, dst)
        dst_ref[...] = dst

    def col(base):
        return pl.BlockSpec((tb, RNN_HD), lambda h, c: (ntb - 1 - c, base + h))

    outc = pl.BlockSpec((tb, RNN_HD), lambda h, c: (ntb - 1 - c, h))
    hb = ATTN_W // RNN_HD
    res, xres = _call(
        body, name=name, grid=(N_RNN, ntb),
        in_specs=[col(qb), col(fb), col(ib), col(gb),
                  pl.BlockSpec((2, RNN_HD), lambda h, c: (0, h)), pl.BlockSpec((1, RNN_HD), lambda h, c: (0, 0)),
                  outc,
                  pl.BlockSpec((None, nch, RNN_HD, RNN_HD), lambda h, c: (h, ntb - 1 - c, 0, 0)),
                  pl.BlockSpec((tb, RNN_HD), lambda h, c: (ntb - 1 - c, hb + h))],
        out_specs=[outc, outc, outc, outc,
                   pl.BlockSpec((1, RNN_HD), lambda h, c: (0, h)),
                   pl.BlockSpec((None, 1, RNN_HD), lambda h, c: (h, 0, 0))],
        out_shape=[jax.ShapeDtypeStruct((t, RNN_W), BF16)] * 4
        + [jax.ShapeDtypeStruct((1, RNN_W), F32), jax.ShapeDtypeStruct((N_RNN, 1, RNN_HD), F32)],
        scratch_shapes=[pltpu.VMEM((RNN_HD, RNN_HD), F32),
                        pltpu.VMEM((nch, CHUNK, RNN_HD), F32), pltpu.VMEM((nch, CHUNK, RNN_HD), F32),
                        pltpu.VMEM((nch, CHUNK, RNN_HD), F32)],
        args=[proj, proj, proj, proj, lb_logits, norm_gain, o_pre, s0, dcat],
        semantics=("parallel", "arbitrary"), exchanges=exchanges)
    return (*res, xres) if exchanges else res


def _cast_slots(w, where, *, name):
    _, rows, cols = w.shape
    rh = rows // 2
    tr = _row_tile(rh, cols)
    nh = rh // tr

    def body(wh_ref, w_ref, o_ref):
        o_ref[...] = w_ref[...].astype(BF16)

    return pl.pallas_call(
        body, name=name,
        grid_spec=pltpu.PrefetchScalarGridSpec(
            num_scalar_prefetch=1, grid=(2, nh),
            in_specs=[pl.BlockSpec((None, tr, cols), lambda h, i, wh: (0, h * nh + i, 0))],
            out_specs=pl.BlockSpec((None, tr, cols), lambda h, i, wh: (2 * wh[0] + h, i, 0))),
        out_shape=jax.ShapeDtypeStruct((8, rh, cols), BF16),
        compiler_params=_params(("parallel", "parallel")),
    )(where, w)


def _all_gather_halves(bufs, *, name):
    n = len(bufs)

    def body(*refs):
        ins, outs = refs[:n], refs[n:2 * n]
        send_sems, recv_sems = refs[2 * n:]
        x, y, c = _place()
        sibling = (x, y, 1 - c)
        chips = [(1 - x, y), (x, 1 - y), (1 - x, 1 - y)]

        def copy(a, k, block, to, src=None):
            slot = outs[a].at[4 * block[0] + 2 * block[1] + block[2]]
            return pltpu.make_async_remote_copy(
                src_ref=slot if src is None else src, dst_ref=slot,
                send_sem=send_sems.at[a, k], recv_sem=recv_sems.at[a, k],
                device_id=to, device_id_type=MESH)

        first, passed = [], []
        for a in range(n):
            for j, chip in enumerate(chips):
                cp = copy(a, j, (x, y, c), (*chip, c), src=ins[a].at[4 * x + 2 * y + c])
                cp.start()
                first.append(cp)
        for a in range(n):
            for j, chip in enumerate(chips):
                copy(a, j, (*chip, c), (x, y, c)).wait_recv()
                cp = copy(a, 3 + j, (*chip, c), sibling)
                cp.start()
                passed.append(cp)
        for a in range(n):
            for j, chip in enumerate(chips):
                copy(a, 3 + j, (*chip, 1 - c), (x, y, c)).wait_recv()
        for cp in first + passed:
            cp.wait_send()

    return pl.pallas_call(
        body, name=name,
        in_specs=[ANY] * n, out_specs=[ANY] * n,
        out_shape=[jax.ShapeDtypeStruct(b.shape, b.dtype) for b in bufs],
        scratch_shapes=[pltpu.SemaphoreType.DMA((n, 6)), pltpu.SemaphoreType.DMA((n, 6))],
        input_output_aliases={a: a for a in range(n)},
    )(*bufs)


def _row_tile(rows, cols, budget=1 << 20):
    tr = rows
    while tr * cols > budget and tr % 16 == 0:
        tr //= 2
    return tr


def _half_spec(g, tr, halves_last, slab):
    if halves_last:
        return pl.BlockSpec((None, tr, g.shape[2] // 2), lambda *a: (slab(*a), a[-2], a[-1][1]))
    return pl.BlockSpec((None, None, tr, g.shape[3]), lambda *a: (slab(*a), a[-1][1], a[-2], 0))


def _pair_sum(g, sib, where, *, name, halves_last=False):
    rh, cols = sib.shape[1:]
    tr = _row_tile(rh, cols)

    def body(w_ref, g_ref, s_ref, o_ref):
        o_ref[...] = (g_ref[...] + s_ref[...]).astype(BF16)

    return pl.pallas_call(
        body, name=name,
        grid_spec=pltpu.PrefetchScalarGridSpec(
            num_scalar_prefetch=1, grid=(4, rh // tr),
            in_specs=[_half_spec(g, tr, halves_last, lambda s, i, w: s),
                      pl.BlockSpec((None, tr, cols), lambda s, i, w: (s, i, 0))],
            out_specs=pl.BlockSpec((None, tr, cols), lambda s, i, w: (s, i, 0))),
        out_shape=jax.ShapeDtypeStruct((4, rh, cols), BF16),
        compiler_params=_params(("parallel", "parallel")),
    )(where, g, sib)


def _final_half(g, sib, recv, where, *, name, halves_last=False):
    rh, cols = sib.shape[1:]
    tr = _row_tile(rh, cols)

    def body(w_ref, g_ref, s_ref, r_ref, o_ref):
        acc = g_ref[...] + s_ref[...]
        for j in range(3):
            acc = acc + r_ref[j].astype(F32)
        o_ref[...] = acc

    return pl.pallas_call(
        body, name=name,
        grid_spec=pltpu.PrefetchScalarGridSpec(
            num_scalar_prefetch=1, grid=(rh // tr,),
            in_specs=[_half_spec(g, tr, halves_last, lambda i, w: w[0]),
                      pl.BlockSpec((None, tr, cols), lambda i, w: (w[0], i, 0)),
                      pl.BlockSpec((3, tr, cols), lambda i, w: (0, i, 0))],
            out_specs=pl.BlockSpec((tr, cols), lambda i, w: (i, 0))),
        out_shape=jax.ShapeDtypeStruct((rh, cols), F32),
        compiler_params=_params(("parallel",)),
    )(where, g, sib, recv)


def _adamw_math(w, g, m, v):
    m = ADAM_B1 * m + (1.0 - ADAM_B1) * g
    v = ADAM_B2 * v + (1.0 - ADAM_B2) * (g * g)
    m_hat = m / (1.0 - ADAM_B1 ** ADAM_STEP)
    v_hat = v / (1.0 - ADAM_B2 ** ADAM_STEP)
    delta = -ADAM_LR * (m_hat / (jnp.sqrt(v_hat) + ADAM_EPS) + ADAM_WD * w)
    return delta, m, v


def _adamw(w, mine, theirs, m, v, where, *, name, halves_last=False):
    _, rows, cols = w.shape
    if halves_last:
        cols //= 2
        tr = _row_tile(rows, cols, budget=1 << 19)
        grid = (rows // tr, 2)
        blk = pl.BlockSpec((None, tr, cols), lambda i, h, wh: (0, i, h))
        mine_spec = theirs_spec = pl.BlockSpec((tr, cols), lambda i, h, wh: (i, 0))
        which = lambda: pl.program_id(1)
    else:
        tr = _row_tile(rows // 2, cols, budget=1 << 19)
        nh = rows // 2 // tr
        grid = (rows // tr,)
        blk = pl.BlockSpec((None, tr, cols), lambda i, wh: (0, i, 0))
        mine_spec = pl.BlockSpec((tr, cols), lambda i, wh: (jnp.where(i // nh == wh[1], i % nh, 0), 0))
        theirs_spec = pl.BlockSpec((tr, cols), lambda i, wh: (jnp.where(i // nh == wh[1], 0, i % nh), 0))
        which = lambda: pl.program_id(0) // nh

    def body(wh_ref, w_ref, a_ref, b_ref, m_ref, v_ref, g_ref, d_ref, nm_ref, nv_ref):
        g = jnp.where(which() == wh_ref[1], a_ref[...], b_ref[...])
        d, nm, nv = _adamw_math(w_ref[...], g, m_ref[...], v_ref[...])
        g_ref[...] = g
        d_ref[...] = d
        nm_ref[...] = nm
        nv_ref[...] = nv

    rows, cols = w.shape[1:]
    return pl.pallas_call(
        body, name=name,
        grid_spec=pltpu.PrefetchScalarGridSpec(
            num_scalar_prefetch=1, grid=grid,
            in_specs=[blk, mine_spec, theirs_spec, blk, blk], out_specs=[blk] * 4),
        out_shape=[jax.ShapeDtypeStruct((1, rows, cols), F32)] * 4,
        compiler_params=_params(("parallel",) * len(grid)),
    )(where, w, mine, theirs, m, v)


SEG_LOSS = 0
SEG_SINK = 128
SEG_AGAIN = 256
SEG_L0 = SEG_AGAIN + ATTN_W
SEG_L1 = SEG_L0 + RNN_W
SEG_RGAIN = SEG_L1 + RNN_W
SEG_G = SEG_RGAIN + 128
N_PACK = SEG_G + 4 * D_MODEL


def _pack(sinks, again, l0, l1, rgain, gains, loss=None):
    z = lambda k: jnp.zeros((1, k), F32)
    first = z(128) if loss is None else loss
    return jnp.concatenate([first, sinks, z(128 - N_Q), again, l0, l1, rgain] + list(gains), axis=1)


def _small_reduce_adamw(part, w, m, v, *, name):
    def body(p_ref, w_ref, m_ref, v_ref, g_ref, d_ref, nm_ref, nv_ref, buf_ref, send_sems, recv_sems):
        x, y, c = _place()
        me = 4 * x + 2 * y + c
        copies = []
        for k in range(1, 8):
            dx, dy, dc = (k >> 2) & 1, (k >> 1) & 1, k & 1
            to = (x ^ dx, y ^ dy, c ^ dc)
            cp = pltpu.make_async_remote_copy(
                src_ref=p_ref, dst_ref=buf_ref.at[me],
                send_sem=send_sems.at[k - 1], recv_sem=recv_sems.at[k - 1],
                device_id=to, device_id_type=MESH)
            cp.start()
            copies.append(cp)
        buf_ref[me] = p_ref[...]
        for cp in copies:
            cp.wait()
        tot = buf_ref[0]
        for j in range(1, 8):
            tot = tot + buf_ref[j]
        g_ref[...] = tot
        l0 = w_ref[:, SEG_L0:SEG_L0 + RNN_W]
        l1 = w_ref[:, SEG_L1:SEG_L1 + RNN_W]
        mx = jnp.maximum(l0, l1)
        e0 = jnp.exp(l0 - mx)
        e1 = jnp.exp(l1 - mx)
        lb = e0 / (e0 + e1)
        gl0 = tot[:, SEG_L0:SEG_L0 + RNN_W] * lb * (1.0 - lb)
        g_ref[:, SEG_L0:SEG_L0 + RNN_W] = gl0
        g_ref[:, SEG_L1:SEG_L1 + RNN_W] = -gl0
        d, nm, nv = _adamw_math(w_ref[...], g_ref[...], m_ref[...], v_ref[...])
        d_ref[...] = d
        nm_ref[...] = nm
        nv_ref[...] = nv

    vm = pl.BlockSpec(memory_space=pltpu.VMEM)
    return pl.pallas_call(
        body, name=name,
        in_specs=[vm] * 4, out_specs=[vm] * 4,
        out_shape=[jax.ShapeDtypeStruct((1, N_PACK), F32)] * 4,
        scratch_shapes=[pltpu.VMEM((8, 1, N_PACK), F32), pltpu.SemaphoreType.DMA((7,)),
                        pltpu.SemaphoreType.DMA((7,))],
    )(part, w, m, v)


def _layer_grads(xs, tgt, bufs, where, sinks, again, lb_logits, rgain,
                 g_mix_pre, g_mix_post, g_mlp_pre, g_mlp_post):
    tm = 512
    b_in, b_out, b_up, b_dn = bufs

    shard = IN_W // N_CHIPS
    w_in_t = _all_gather_halves([b_in], name="gather_w_in")[0].reshape(IN_W, D_MODEL)
    h1 = _rms_cast(xs, g_mix_pre, tm=tm, name="h1_norm")
    proj, ((b_out, b_up),) = _mm(
        h1, w_in_t, tm=1024, tn=768, tk=D_MODEL, out_dtype=F32, w_layout="nk", name="in_proj",
        exchanges=[_x_gather([b_out, b_up], ici=[(0, 256), (0, 336)])])
    attn, lse, ((b_out, b_up),) = _swa_fwd(
        proj, sinks, name="swa_fwd",
        exchanges=[_x_gather([b_out, b_up], ici=[None, (336, 320)], d2d=[(0, 256), None])])
    w_out = b_out.reshape(D_MODEL, D_MODEL)
    o_pre, rnn, s0, ((b_up, b_dn),) = _hgrn_fwd(
        proj, lb_logits, rgain, tb=512, name="hgrn_fwd",
        exchanges=[_x_gather([b_up, b_dn], ici=[(656, 368), (0, 400)])])
    cat = _mix_cat(attn, rnn, again, tm=tm, name="mix_cat")
    mixed, ((b_up, b_dn),) = _mm(
        cat, w_out, tm=1024, tn=1024, tk=D_MODEL, out_dtype=BF16, name="out_proj",
        exchanges=[_x_gather([b_up, b_dn], ici=[None, (400, 320)], d2d=[(0, 1024), (0, 400)])])
    w_up4 = b_up.reshape(N_CHIPS, D_MODEL, D_FF // N_CHIPS)
    x1, h2, ((b_dn,),) = _post_norm_res(
        mixed, g_mix_post, xs, g_mlp_pre, tm=256, name="mix_post",
        exchanges=[_x_gather([b_dn], ici=[(720, 304)], d2d=[(400, 320)])])
    u, ((b_dn,),) = _mm(h2, w_up4, tm=1024, tn=1024, tk=D_MODEL, out_dtype=BF16, relu=True, w_layout="skn",
                        name="mlp_up", exchanges=[_x_gather([b_dn], d2d=[(720, 304)])])
    w_dn = b_dn.reshape(D_FF, D_MODEL)
    yv = _mm(u, w_dn, tm=1024, tn=1024, tk=2048, out_dtype=BF16, a_square=True, name="mlp_down")
    dy, dx2, loss_row, dg_mlp_post = _loss_head(yv, g_mlp_post, x1, tgt, tm=256, name="loss_head")

    def halved(g):
        return g.reshape(N_CHIPS, 2, g.shape[1] // 2, g.shape[2])
    du = _mm(dy, w_dn, tm=1024, tn=1024, tk=D_MODEL, out_dtype=BF16, mul2=u, w_layout="nk", name="mlp_down_bwd")
    g_dn = halved(_mm_tn(u, dy, tm=1024, tn=1024, tt=2048, a_square=True, name="w_down_grad")
                  .reshape(N_CHIPS, D_FF // N_CHIPS, D_MODEL))
    d_w_up, ((sib_dn,),) = _mm_tn(h2, du, tm=1024, tn=1024, tt=2048, n_split=N_CHIPS, name="w_up_grad",
                                  exchanges=[_x_pair([g_dn])])
    g_up = halved(d_w_up)
    wire_dn = _pair_sum(g_dn, sib_dn, where, name="pair_sum_w_down")
    dh2, ((recv_dn,),) = _mm(du, w_up4, tm=1024, tn=1024, tk=2048, out_dtype=BF16, w_layout="snk", name="mlp_up_bwd",
                             exchanges=[_x_chip([wire_dn], rows=[(0, 832)])])
    dx1, dg_mlp_pre, ((recv_dn,),) = _rms_bwd(dh2, x1, g_mlp_pre, dx2, tm=256, out_dtype=F32, name="mlp_pre_bwd",
                                              exchanges=[_x_chip([wire_dn], rows=[(832, 192)], into=[recv_dn])])
    fin_dn = _final_half(g_dn, sib_dn, recv_dn, where, name="final_half_w_down")
    dmixed, dg_mix_post = _rms_bwd(dx1, mixed, g_mix_post, None, tm=256, out_dtype=BF16, name="mix_post_bwd")
    d_w_out, ((oth_dn,), (sib_up,)) = _mm_tn(cat, dmixed, tm=1024, tn=1024, tt=2048, name="w_out_grad",
                                             exchanges=[_x_share([fin_dn]), _x_pair([g_up])])
    wire_up = _pair_sum(g_up, sib_up, where, name="pair_sum_w_up")
    g_out = halved(d_w_out.reshape(N_CHIPS, D_MODEL // N_CHIPS, D_MODEL))
    dcat, ((sib_out,),) = _mm(dmixed, w_out, tm=1024, tn=1024, tk=D_MODEL, out_dtype=BF16, w_layout="nk", name="out_proj_bwd",
                              exchanges=[_x_pair([g_out])])
    wire_out = _pair_sum(g_out, sib_out, where, name="pair_sum_w_out")
    dattn, dg_again = _rms_bwd(dcat, attn, again, None, tm=tm, out_dtype=BF16, name="attn_norm_bwd")
    dq_a, dkv, dsinks, ((recv_out,), (recv_up,)) = _swa_bwd(
        proj, sinks, dattn, lse, name="swa_bwd",
        exchanges=[_x_chip([wire_out]), _x_chip([wire_up], rows=[(0, 320)])])
    dq_r, df_r, di_r, dg_r, dlb, dgain_h, ((recv_up,),) = _hgrn_bwd(
        proj, lb_logits, rgain, o_pre, s0, dcat, tb=512, name="hgrn_bwd",
        exchanges=[_x_chip([wire_up], rows=[(320, 704)], into=[recv_up])])
    fin_up = _final_half(g_up, sib_up, recv_up, where, name="final_half_w_up")
    fin_out = _final_half(g_out, sib_out, recv_out, where, name="final_half_w_out")
    dproj = jnp.concatenate([dq_a, dkv, dq_r, df_r, di_r, dg_r], axis=1)
    piece_cols = D_MODEL // 4

    def w_in_piece(pc, exchanges):
        d, xres = _mm_tn(dproj, h1, tm=896, tn=piece_cols, tt=2048, n_blocks=(2, 2, pc),
                         name="w_in_grad_%d" % pc, exchanges=exchanges)
        return d.reshape(N_CHIPS, shard, 2 * piece_cols), xres

    g_in0, ((oth_up, oth_out),) = w_in_piece(0, [_x_share([fin_up, fin_out])])
    g_in1, ((sib_in0,),) = w_in_piece(1, [_x_pair([g_in0], halves_last=True)])
    wire_in0 = _pair_sum(g_in0, sib_in0, where, name="pair_sum_w_in_0", halves_last=True)
    dh1, ((recv_in0,), (sib_in1,)) = _mm(
        dproj, w_in_t, tm=1024, tn=1024, tk=2688, out_dtype=BF16, m_blocks=(0, 2), name="in_proj_bwd_0",
        exchanges=[_x_chip([wire_in0]), _x_pair([g_in1], halves_last=True)])
    wire_in1 = _pair_sum(g_in1, sib_in1, where, name="pair_sum_w_in_1", halves_last=True)
    dh1, ((recv_in1,),) = _mm(
        dproj, w_in_t, tm=1024, tn=1024, tk=2688, out_dtype=BF16, m_blocks=(2, 2), out_into=dh1,
        name="in_proj_bwd_1", exchanges=[_x_chip([wire_in1])])
    gx, dg_mix_pre = _rms_bwd(dh1, xs, g_mix_pre, dx1, tm=256, out_dtype=F32, name="mix_pre_bwd")
    fin_in0 = _final_half(g_in0, sib_in0, recv_in0, where, name="final_half_w_in_0", halves_last=True)
    fin_in1 = _final_half(g_in1, sib_in1, recv_in1, where, name="final_half_w_in_1", halves_last=True)
    oth_in0, oth_in1 = _run_exchange(_x_share([fin_in0, fin_in1]), name="share_w_in")
    fin_in = jnp.concatenate([fin_in0, fin_in1], axis=1)
    oth_in = jnp.concatenate([oth_in0, oth_in1], axis=1)

    big = [(fin_in, oth_in), (fin_out, oth_out), (fin_up, oth_up), (fin_dn, oth_dn)]
    drgain = jnp.sum(dgain_h, axis=0)
    small = _pack(jnp.sum(dsinks, axis=1)[None, :], dg_again, dlb, jnp.zeros_like(dlb), drgain,
                  [dg_mix_pre, dg_mix_post, dg_mlp_pre, dg_mlp_post], loss=loss_row)
    return gx, big, small


def kernel(x, w_in, attn_sinks, attn_out_gain, rnn_lb_logits, rnn_norm_gain, w_out, mix_pre_gain, mix_post_gain, mlp_pre_gain, mlp_post_gain, w_up, w_down, loss_target, m_w_in, m_attn_sinks, m_attn_out_gain, m_rnn_lb_logits, m_rnn_norm_gain, m_w_out, m_mix_pre_gain, m_mix_post_gain, m_mlp_pre_gain, m_mlp_post_gain, m_w_up, m_w_down, v_w_in, v_attn_sinks, v_attn_out_gain, v_rnn_lb_logits, v_rnn_norm_gain, v_w_out, v_mix_pre_gain, v_mix_post_gain, v_mlp_pre_gain, v_mlp_post_gain, v_w_up, v_w_down):
    ax, ay, ac = _place()
    where = jnp.stack([2 * ax + ay, ac]).astype(jnp.int32)
    t = lambda a: jnp.swapaxes(a, 1, 2)
    big_w = [t(w_in), w_out, w_up, w_down]
    big_m = [t(m_w_in), m_w_out, m_w_up, m_w_down]
    big_v = [t(v_w_in), v_w_out, v_w_up, v_w_down]

    names = ["w_in", "w_out", "w_up", "w_down"]
    bufs = [_cast_slots(w, where, name="cast_" + nm) for w, nm in zip(big_w, names)]
    gx, big_g, small_part = _layer_grads(
        x[0], loss_target[0], bufs, where, attn_sinks, attn_out_gain, rnn_lb_logits, rnn_norm_gain,
        mix_pre_gain, mix_post_gain, mlp_pre_gain, mlp_post_gain)

    grads, deltas, new_m, new_v = [], [], [], []
    for (f, o), w, m, v, nm in zip(big_g, big_w, big_m, big_v, names):
        res = _adamw(w, f, o, m, v, where, name="adamw_" + nm, halves_last=(nm == "w_in"))
        if nm == "w_in":
            res = [t(r) for r in res]
        g, d, nm_, nv_ = res
        grads.append(g)
        deltas.append(d)
        new_m.append(nm_)
        new_v.append(nv_)

    def pack_params(sinks, again, logits, rgain, gains):
        return _pack(sinks, again, logits[0:1], logits[1:2], rgain, gains)

    pw = pack_params(attn_sinks, attn_out_gain, rnn_lb_logits, rnn_norm_gain,
                     [mix_pre_gain, mix_post_gain, mlp_pre_gain, mlp_post_gain])
    pm = pack_params(m_attn_sinks, m_attn_out_gain, m_rnn_lb_logits, m_rnn_norm_gain,
                     [m_mix_pre_gain, m_mix_post_gain, m_mlp_pre_gain, m_mlp_post_gain])
    pv = pack_params(v_attn_sinks, v_attn_out_gain, v_rnn_lb_logits, v_rnn_norm_gain,
                     [v_mix_pre_gain, v_mix_post_gain, v_mlp_pre_gain, v_mlp_post_gain])
    packs = _small_reduce_adamw(small_part, pw, pm, pv, name="small_reduce_adamw")

    def unpack(p):
        seg = lambda o, k: p[:, o:o + k]
        logits = jnp.concatenate([seg(SEG_L0, RNN_W), seg(SEG_L1, RNN_W)], axis=0)
        gains = [seg(SEG_G + i * D_MODEL, D_MODEL) for i in range(4)]
        return dict(sinks=seg(SEG_SINK, N_Q), again=seg(SEG_AGAIN, ATTN_W), logits=logits,
                    rgain=seg(SEG_RGAIN, RNN_HD), gains=gains)

    def order(small, big):
        return [big[0], small["sinks"], small["again"], small["logits"], small["rgain"], big[1],
                *small["gains"], big[2], big[3]]

    loss = packs[0][0, 0]
    outs = [loss, gx[None]]
    for p, b in zip(packs, [grads, deltas, new_m, new_v]):
        outs += order(unpack(p), b)
    return tuple(outs)
```

```python
import functools

import jax
import jax.numpy as jnp
from jax import lax
from jax.experimental import pallas as pl
from jax.experimental.pallas import tpu as pltpu

F32 = jnp.float32
BF16 = jnp.bfloat16
MESH = pl.DeviceIdType.MESH

EPS = 1e-6
D_MODEL = 2048
ATTN_W = 1024
HEAD_DIM = 64
N_Q = 16
N_KV = 2
GROUP = 8
BLK = 128
RNN_W = 1024
RNN_HD = 128
N_RNN = 8
CHUNK = 64
SUB_FWD = 16
SUB_BWD = 8
D_FF = 8192
IN_W = 5376
N_CHIPS = 4
KV_COL = ATTN_W
QR_COL = ATTN_W + 2 * 128
FR_COL = QR_COL + RNN_W
IR_COL = FR_COL + RNN_W
GR_COL = IR_COL + RNN_W

ADAM_LR = 0.001
ADAM_B1 = 0.9
ADAM_B2 = 0.999
ADAM_EPS = 1e-08
ADAM_WD = 0.01
ADAM_STEP = 10

VMEM_LIMIT = 48 * 1024 * 1024

NT = (((1,), (1,)), ((), ()))
TN = (((0,), (0,)), ((), ()))


def _params(sem=None):
    return pltpu.CompilerParams(dimension_semantics=sem, vmem_limit_bytes=VMEM_LIMIT)


def _sigmoid(x):
    return 1.0 / (1.0 + jnp.exp(-x))


ANY = pl.BlockSpec(memory_space=pl.ANY)


def _place():
    return lax.axis_index("x"), lax.axis_index("y"), lax.axis_index("c")


def _other_chips(x, y):
    return [(1 - x, y), (x, 1 - y), (1 - x, 1 - y)]


class _Exchange:
    def __init__(self, srcs, outs, ncopy, build, aliases=None):
        self.srcs, self.outs, self.ncopy, self.build = list(srcs), list(outs), ncopy, build
        self.aliases = aliases or {}


def _remote(src, dst, send_sems, recv_sems, k, to):
    return pltpu.make_async_remote_copy(src_ref=src, dst_ref=dst, send_sem=send_sems.at[k],
                                        recv_sem=recv_sems.at[k], device_id=to, device_id_type=MESH)


def _call(body, *, name, grid, in_specs, out_specs, out_shape, args, scratch_shapes=(), semantics=None,
          exchanges=(), into=None):
    in_specs, out_specs, out_shape = list(in_specs), list(out_specs), list(out_shape)
    scratch_shapes = list(scratch_shapes)
    ni, no, ns = len(in_specs), len(out_specs), len(scratch_shapes)
    xsrc = [s for x in exchanges for s in x.srcs]
    xout = [o for x in exchanges for o in x.outs]
    into = into or {}
    xsrc += [into[k] for k in sorted(into)]
    nxi, nxo = len(xsrc), len(xout)
    aliases = {nxi - len(into) + ni + q: k for q, k in enumerate(sorted(into))}
    a0 = b0 = 0
    for x in exchanges:
        for si, oi in x.aliases.items():
            aliases[ni + a0 + si] = no + b0 + oi
        a0 += len(x.srcs)
        b0 += len(x.outs)
    sems = []
    for x in exchanges:
        sems += [pltpu.SemaphoreType.DMA((x.ncopy,)), pltpu.SemaphoreType.DMA((x.ncopy,))]

    def wrapped(*refs):
        ins, xi = refs[:ni], refs[ni:ni + nxi]
        outs, xo = refs[ni + nxi:ni + nxi + no], refs[ni + nxi + no:ni + nxi + no + nxo]
        rest = refs[ni + nxi + no + nxo:]
        scr, sm = rest[:ns], rest[ns:]

        def copies():
            cps = []
            a = b = 0
            for k, x in enumerate(exchanges):
                cps += x.build(xi[a:a + len(x.srcs)], xo[b:b + len(x.outs)], sm[2 * k], sm[2 * k + 1])
                a += len(x.srcs)
                b += len(x.outs)
            return cps

        def start():
            for cp in copies():
                cp.start()

        def wait():
            for cp in copies():
                cp.wait()

        if not exchanges:
            body(*ins, *outs, *scr)
        elif not grid:
            start()
            body(*ins, *outs, *scr)
            wait()
        else:
            first = last = None
            for ax, g in enumerate(grid):
                f = pl.program_id(ax) == 0
                l = pl.program_id(ax) == g - 1
                first = f if first is None else first & f
                last = l if last is None else last & l
            pl.when(first)(start)
            body(*ins, *outs, *scr)
            pl.when(last)(wait)

    if exchanges and semantics is not None:
        semantics = ("arbitrary",) * len(grid)
    kwargs = dict(grid=grid) if grid else {}
    res = pl.pallas_call(
        wrapped, name=name,
        in_specs=in_specs + [ANY] * nxi, out_specs=out_specs + [ANY] * nxo,
        out_shape=out_shape + xout, scratch_shapes=scratch_shapes + sems,
        input_output_aliases=aliases,
        compiler_params=_params(semantics), **kwargs,
    )(*args, *xsrc)
    res = list(res)
    mine, theirs = res[:no], res[no:]
    per = []
    b = 0
    for x in exchanges:
        per.append(theirs[b:b + len(x.outs)])
        b += len(x.outs)
    return mine, per


def _run_exchange(x, *, name):
    return _call(lambda: None, name=name, grid=(), in_specs=[], out_specs=[], out_shape=[], args=[],
                 exchanges=[x])[1][0]


def _x_gather(bufs, ici=None, d2d=None, cross=None):
    n = len(bufs)
    plan = [(a, kind, rows[a]) for a in range(n) for kind, rows in (("ici", ici), ("d2d", d2d), ("cross", cross))
            if rows is not None and rows[a] is not None]

    def build(srcs, outs, ss, rs):
        x, y, c = _place()
        cps = []
        for q, (a, kind, rows) in enumerate(plan):
            piece = pl.ds(*rows)
            for j, (px, py) in enumerate(_other_chips(x, y)):
                if kind == "d2d":
                    slot, to = 4 * px + 2 * py + c, (x, y, 1 - c)
                else:
                    slot, to = 4 * x + 2 * y + c, (px, py, c if kind == "ici" else 1 - c)
                cps.append(_remote(srcs[a].at[slot, piece], outs[a].at[slot, piece], ss, rs, 3 * q + j, to))
        return cps

    outs = [jax.ShapeDtypeStruct(b.shape, b.dtype) for b in bufs]
    return _Exchange(bufs, outs, 3 * len(plan), build, aliases={a: a for a in range(n)})


def _x_pair(grads, halves_last=False):
    n = len(grads)

    def build(srcs, outs, ss, rs):
        x, y, c = _place()

        def half(r):
            if not halves_last:
                return r.at[:, 1 - c]
            ch = r.shape[2] // 2
            return r.at[:, :, pl.ds(pl.multiple_of((1 - c) * ch, 128), ch)]

        return [_remote(half(srcs[a]), outs[a], ss, rs, a, (x, y, 1 - c)) for a in range(n)]

    if halves_last:
        outs = [jax.ShapeDtypeStruct(g.shape[:2] + (g.shape[2] // 2,), g.dtype) for g in grads]
    else:
        outs = [jax.ShapeDtypeStruct((4,) + g.shape[2:], g.dtype) for g in grads]
    return _Exchange(grads, outs, n, build)


def _x_chip(wires, rows=None, into=None):
    n = len(wires)
    rows = rows or [(0, w.shape[1]) for w in wires]

    def build(srcs, outs, ss, rs):
        x, y, c = _place()
        cps = []
        for a in range(n):
            piece = pl.ds(*rows[a])
            for j, (px, py) in enumerate(_other_chips(x, y)):
                cps.append(_remote(srcs[a].at[2 * px + py, piece], outs[a].at[j, piece], ss, rs,
                                   3 * a + j, (px, py, c)))
        return cps

    outs = [jax.ShapeDtypeStruct((3,) + w.shape[1:], w.dtype) for w in wires]
    if into is None:
        return _Exchange(wires, outs, 3 * n, build)
    return _Exchange(list(wires) + list(into), outs, 3 * n, build, aliases={n + a: a for a in range(n)})


def _x_share(halves):
    n = len(halves)

    def build(srcs, outs, ss, rs):
        x, y, c = _place()
        return [_remote(srcs[a], outs[a], ss, rs, a, (x, y, 1 - c)) for a in range(n)]

    outs = [jax.ShapeDtypeStruct(h.shape, h.dtype) for h in halves]
    return _Exchange(halves, outs, n, build)


def _mm(a, w, *, tm, tn, tk, out_dtype, name, a_square=False, relu=False, mul2=None, w_layout="kn",
        m_blocks=None, out_into=None, exchanges=()):
    m, k = a.shape
    m_first, m_count = m_blocks or (0, m // tm)
    a_spec = pl.BlockSpec((tm, tk), lambda i, j, kk: (i + m_first, kk))
    if w_layout == "kn":
        n = w.shape[1]
        w_spec = pl.BlockSpec((tk, tn), lambda i, j, kk: (kk, j))
    elif w_layout == "nk":
        n = w.shape[0]
        w_spec = pl.BlockSpec((tn, tk), lambda i, j, kk: (j, kk))
    elif w_layout == "skn":
        n = w.shape[0] * w.shape[2]
        per_n = w.shape[2] // tn
        w_spec = pl.BlockSpec((None, tk, tn), lambda i, j, kk: (j // per_n, kk, j % per_n))
    else:
        assert w_layout == "snk"
        n = w.shape[1]
        per_k = w.shape[2] // tk
        w_spec = pl.BlockSpec((None, tn, tk), lambda i, j, kk: (kk // per_k, j, kk % per_k))
    w_dims = NT if w_layout in ("nk", "snk") else (((1,), (0,)), ((), ()))
    nk = k // tk
    assert m % tm == 0 and n % tn == 0 and k % tk == 0

    def body(*refs):
        if mul2 is not None:
            a_ref, w_ref, e_ref, o_ref, acc_ref = refs
        else:
            a_ref, w_ref, o_ref, acc_ref = refs
            e_ref = None
        kk = pl.program_id(2)
        av = a_ref[...]
        if a_square:
            af = av.astype(F32)
            av = (af * af).astype(BF16)
        part = lax.dot_general(av, w_ref[...], w_dims, preferred_element_type=F32)

        def finish(r):
            if relu:
                r = jnp.maximum(r, 0.0)
            if e_ref is not None:
                r = 2.0 * e_ref[...].astype(F32) * r
            o_ref[...] = r.astype(out_dtype)

        if nk == 1:
            finish(part)
        else:
            @pl.when(kk == 0)
            def _():
                acc_ref[...] = part

            @pl.when(kk > 0)
            def _():
                acc_ref[...] += part

            @pl.when(kk == nk - 1)
            def _():
                finish(acc_ref[...])

    in_specs = [a_spec, w_spec]
    args = [a, w]
    if mul2 is not None:
        in_specs.append(pl.BlockSpec((tm, tn), lambda i, j, kk: (i + m_first, j)))
        args.append(mul2)
    acc_shape = (tm, tn) if nk > 1 else (8, 128)
    (out,), per = _call(
        body, name=name, grid=(m_count, n // tn, nk),
        in_specs=in_specs, out_specs=[pl.BlockSpec((tm, tn), lambda i, j, kk: (i + m_first, j))],
        out_shape=[jax.ShapeDtypeStruct((m, n), out_dtype)], args=args,
        scratch_shapes=[pltpu.VMEM(acc_shape, F32)],
        semantics=("parallel", "parallel", "arbitrary"), exchanges=exchanges,
        into=None if out_into is None else {0: out_into})
    return (out, per) if exchanges else out


def _mm_tn(a, b, *, tm, tn, tt, name, a_square=False, n_split=1, n_blocks=None, exchanges=()):
    t, m = a.shape
    n = b.shape[1]
    assert t % tt == 0 and m % tm == 0 and n % tn == 0
    count, stride, first = n_blocks or (n // tn, 1, 0)
    n = count * tn
    assert (n // n_split) % tn == 0
    per = n // n_split // tn

    def body(a_ref, b_ref, o_ref):
        ti = pl.program_id(2)
        av = a_ref[...]
        if a_square:
            af = av.astype(F32)
            av = (af * af).astype(BF16)
        part = lax.dot_general(av, b_ref[...], TN, preferred_element_type=F32)

        @pl.when(ti == 0)
        def _():
            o_ref[...] = part

        @pl.when(ti > 0)
        def _():
            o_ref[...] += part

    (out,), xres = _call(
        body, name=name, grid=(m // tm, n // tn, t // tt),
        in_specs=[pl.BlockSpec((tt, tm), lambda i, j, ti: (ti, i)),
                  pl.BlockSpec((tt, tn), lambda i, j, ti: (ti, first + stride * j))],
        out_specs=[pl.BlockSpec((None, tm, tn), lambda i, j, ti: (j // per, i, j % per))],
        out_shape=[jax.ShapeDtypeStruct((n_split, m, n // n_split), F32)], args=[a, b],
        semantics=("parallel", "parallel", "arbitrary"), exchanges=exchanges)
    return (out, xres) if exchanges else out


def _rstd(x):
    return lax.rsqrt(jnp.mean(x * x, axis=-1, keepdims=True) + EPS)


def _rms_cast(x, g, *, tm, name):
    t, d = x.shape

    def body(x_ref, g_ref, o_ref):
        xv = x_ref[...]
        o_ref[...] = (xv * _rstd(xv) * g_ref[...]).astype(BF16)

    return pl.pallas_call(
        body, name=name, grid=(t // tm,),
        in_specs=[pl.BlockSpec((tm, d), lambda i: (i, 0)), pl.BlockSpec((1, d), lambda i: (0, 0))],
        out_specs=pl.BlockSpec((tm, d), lambda i: (i, 0)),
        out_shape=jax.ShapeDtypeStruct((t, d), BF16),
        compiler_params=_params(("parallel",)),
    )(x, g)


def _mix_cat(attn, rnn, gain, *, tm, name):
    t = attn.shape[0]

    def body(a_ref, r_ref, g_ref, o_ref):
        av = a_ref[...]
        o_ref[:, :ATTN_W] = (av * _rstd(av) * g_ref[...]).astype(BF16)
        o_ref[:, ATTN_W:] = r_ref[...].astype(BF16)

    return pl.pallas_call(
        body, name=name, grid=(t // tm,),
        in_specs=[pl.BlockSpec((tm, ATTN_W), lambda i: (i, 0)), pl.BlockSpec((tm, RNN_W), lambda i: (i, 0)),
                  pl.BlockSpec((1, ATTN_W), lambda i: (0, 0))],
        out_specs=pl.BlockSpec((tm, D_MODEL), lambda i: (i, 0)),
        out_shape=jax.ShapeDtypeStruct((t, D_MODEL), BF16),
        compiler_params=_params(("parallel",)),
    )(attn, rnn, gain)


def _post_norm_res(mixed, g_post, res, g_next, *, tm, name, exchanges=()):
    t, d = mixed.shape

    def body(m_ref, gp_ref, r_ref, gn_ref, x1_ref, h2_ref):
        mv = m_ref[...].astype(F32)
        x1 = r_ref[...] + mv * _rstd(mv) * gp_ref[...]
        x1_ref[...] = x1
        h2_ref[...] = (x1 * _rstd(x1) * gn_ref[...]).astype(BF16)

    row = pl.BlockSpec((tm, d), lambda i: (i, 0))
    vec = pl.BlockSpec((1, d), lambda i: (0, 0))
    res_, xres = _call(
        body, name=name, grid=(t // tm,),
        in_specs=[row, vec, row, vec], out_specs=[row, row],
        out_shape=[jax.ShapeDtypeStruct((t, d), F32), jax.ShapeDtypeStruct((t, d), BF16)],
        args=[mixed, g_post, res, g_next], semantics=("parallel",), exchanges=exchanges)
    return (*res_, xres) if exchanges else res_


def _rms_bwd(dyn, xin, g, res, *, tm, out_dtype, name, col_block=0, exchanges=()):
    t, d = xin.shape

    def body(*refs):
        if res is not None:
            dy_ref, x_ref, g_ref, r_ref, dx_ref, dg_ref = refs
        else:
            dy_ref, x_ref, g_ref, dx_ref, dg_ref = refs
        i = pl.program_id(0)
        xv = x_ref[...].astype(F32)
        dy = dy_ref[...].astype(F32)
        r = _rstd(xv)
        xh = xv * r
        part = jnp.sum(dy * xh, axis=0, keepdims=True)

        @pl.when(i == 0)
        def _():
            dg_ref[...] = part

        @pl.when(i > 0)
        def _():
            dg_ref[...] += part

        tt = dy * g_ref[...]
        dx = r * (tt - xh * jnp.mean(tt * xh, axis=-1, keepdims=True))
        if res is not None:
            dx = dx + r_ref[...]
        dx_ref[...] = dx.astype(out_dtype)

    row = pl.BlockSpec((tm, d), lambda i: (i, 0))
    vec = pl.BlockSpec((1, d), lambda i: (0, 0))
    in_specs = [pl.BlockSpec((tm, d), lambda i: (i, col_block)), row, vec]
    args = [dyn, xin, g]
    if res is not None:
        in_specs.append(row)
        args.append(res)
    res, xres = _call(
        body, name=name, grid=(t // tm,),
        in_specs=in_specs, out_specs=[row, vec],
        out_shape=[jax.ShapeDtypeStruct((t, d), out_dtype), jax.ShapeDtypeStruct((1, d), F32)], args=args,
        semantics=("arbitrary",), exchanges=exchanges)
    return (*res, xres) if exchanges else res


def _loss_head(y, g_post, x1, target, *, tm, name):
    t, d = y.shape

    def body(y_ref, g_ref, x1_ref, t_ref, dy_ref, dx2_ref, loss_ref, dg_ref):
        i = pl.program_id(0)
        yv = y_ref[...].astype(F32)
        r = _rstd(yv)
        yh = yv * r
        gv = g_ref[...]
        err = x1_ref[...] + yh * gv - t_ref[...]
        lpart = 0.5 * jnp.sum(jnp.mean(err * err, axis=-1, keepdims=True), axis=0, keepdims=True)
        dx2 = err * (1.0 / d)
        dgp = jnp.sum(dx2 * yh, axis=0, keepdims=True)
        lane = lax.broadcasted_iota(jnp.int32, (1, 128), 1)
        lrow = jnp.where(lane == 0, lpart, 0.0)

        @pl.when(i == 0)
        def _():
            dg_ref[...] = dgp
            loss_ref[...] = lrow

        @pl.when(i > 0)
        def _():
            dg_ref[...] += dgp
            loss_ref[...] += lrow

        tt = dx2 * gv
        dy_ref[...] = (r * (tt - yh * jnp.mean(tt * yh, axis=-1, keepdims=True))).astype(BF16)
        dx2_ref[...] = dx2

    row = pl.BlockSpec((tm, d), lambda i: (i, 0))
    vec = pl.BlockSpec((1, d), lambda i: (0, 0))
    return pl.pallas_call(
        body, name=name, grid=(t // tm,),
        in_specs=[row, vec, row, row],
        out_specs=[row, row, pl.BlockSpec((1, 128), lambda i: (0, 0)), vec],
        out_shape=[jax.ShapeDtypeStruct((t, d), BF16), jax.ShapeDtypeStruct((t, d), F32),
                   jax.ShapeDtypeStruct((1, 128), F32), jax.ShapeDtypeStruct((1, d), F32)],
        compiler_params=_params(("arbitrary",)),
    )(y, g_post, x1, target)


def _alibi_slope(h):
    return 2.0 ** (-8.0 * (h + 1) / N_Q)


PAIR = 2 * HEAD_DIM
N_PAIRS = N_Q // 2
PAIRS_PER_KV = GROUP // 2
SMEM = pl.BlockSpec(memory_space=pltpu.SMEM)


def _swa_mask(n):
    key = lax.broadcasted_iota(jnp.int32, (2 * BLK, BLK), 0)
    qry = lax.broadcasted_iota(jnp.int32, (2 * BLK, BLK), 1)
    dist = qry + BLK - key
    valid = (dist >= 0) & (dist < BLK) & ((key >= BLK) | (n > 0))
    return valid, dist.astype(F32)


def _block_diag(kvp_ref, kvc_ref, off):
    a = jnp.concatenate([kvp_ref[:, off:off + HEAD_DIM], kvc_ref[:, off:off + HEAD_DIM]], axis=0).astype(BF16)
    z = jnp.zeros_like(a)
    return jnp.concatenate([jnp.concatenate([a, z], axis=1), jnp.concatenate([z, a], axis=1)], axis=0)


def _swa_scores(s2, e, hh, valid, distf):
    s = s2[2 * BLK * e:2 * BLK * (e + 1)] * (HEAD_DIM ** -0.5) - _alibi_slope(hh) * distf
    return jnp.where(valid, s, -1e30)


def _swa_fwd(proj, sinks, *, name, exchanges=()):
    t = proj.shape[0]
    nb = t // BLK
    kvb = KV_COL // (2 * 128)

    def body(sink_ref, q_ref, kvc_ref, kvp_ref, o_ref, lse_ref):
        n = pl.program_id(0)
        valid, distf = _swa_mask(n)
        for kvh in range(N_KV):
            k2 = _block_diag(kvp_ref, kvc_ref, kvh * HEAD_DIM)
            v2 = _block_diag(kvp_ref, kvc_ref, 128 + kvh * HEAD_DIM)
            for jp in range(PAIRS_PER_KV):
                pair = kvh * PAIRS_PER_KV + jp
                lanes = slice(pair * PAIR, (pair + 1) * PAIR)
                s2 = lax.dot_general(k2, q_ref[:, lanes].astype(BF16), NT, preferred_element_type=F32)
                probs = []
                for e in range(2):
                    hh = 2 * pair + e
                    s = _swa_scores(s2, e, hh, valid, distf)
                    sink = sink_ref[0, hh]
                    mx = jnp.maximum(jnp.max(s, axis=0, keepdims=True), sink)
                    p = jnp.exp(s - mx)
                    l = jnp.sum(p, axis=0, keepdims=True) + jnp.exp(sink - mx)
                    probs.append((p * (1.0 / l)).astype(BF16))
                    lse_ref[hh:hh + 1, :] = mx + jnp.log(l)
                o_ref[:, lanes] = lax.dot_general(jnp.concatenate(probs, axis=0), v2, TN,
                                                  preferred_element_type=F32)

    res, xres = _call(
        body, name=name, grid=(nb,),
        in_specs=[SMEM,
                  pl.BlockSpec((BLK, ATTN_W), lambda n: (n, 0)),
                  pl.BlockSpec((BLK, 256), lambda n: (n, kvb)),
                  pl.BlockSpec((BLK, 256), lambda n: (jnp.maximum(n - 1, 0), kvb))],
        out_specs=[pl.BlockSpec((BLK, ATTN_W), lambda n: (n, 0)),
                   pl.BlockSpec((None, N_Q, BLK), lambda n: (n, 0, 0))],
        out_shape=[jax.ShapeDtypeStruct((t, ATTN_W), F32), jax.ShapeDtypeStruct((nb, N_Q, BLK), F32)],
        args=[sinks, proj, proj, proj], semantics=("parallel",), exchanges=exchanges)
    return (*res, xres) if exchanges else res


def _swa_bwd(proj, sinks, dattn, lse, *, name, exchanges=()):
    t = proj.shape[0]
    nb = t // BLK
    kvb = KV_COL // (2 * 128)

    def body(sink_ref, q_ref, kvc_ref, kvp_ref, do_ref, lse_ref, dq_ref, dkv_ref, dsink_ref, carry_ref):
        n = pl.program_id(0)

        @pl.when(n == 0)
        def _():
            dsink_ref[...] = jnp.zeros_like(dsink_ref)
            carry_ref[...] = jnp.zeros_like(carry_ref)

        @pl.when(n < nb)
        def _():
            valid, distf = _swa_mask(n)
            for kvh in range(N_KV):
                k2 = _block_diag(kvp_ref, kvc_ref, kvh * HEAD_DIM)
                v2 = _block_diag(kvp_ref, kvc_ref, 128 + kvh * HEAD_DIM)
                dk2 = jnp.zeros((4 * BLK, PAIR), F32)
                dv2 = jnp.zeros((4 * BLK, PAIR), F32)
                for jp in range(PAIRS_PER_KV):
                    pair = kvh * PAIRS_PER_KV + jp
                    lanes = slice(pair * PAIR, (pair + 1) * PAIR)
                    q2 = q_ref[:, lanes].astype(BF16)
                    do2 = do_ref[:, lanes].astype(BF16)
                    s2 = lax.dot_general(k2, q2, NT, preferred_element_type=F32)
                    dp2 = lax.dot_general(v2, do2, NT, preferred_element_type=F32)
                    probs, dss = [], []
                    for e in range(2):
                        hh = 2 * pair + e
                        lse_h = lse_ref[hh:hh + 1, :]
                        p = jnp.exp(_swa_scores(s2, e, hh, valid, distf) - lse_h)
                        dp = dp2[2 * BLK * e:2 * BLK * (e + 1)]
                        delta = jnp.sum(p * dp, axis=0, keepdims=True)
                        dsink_ref[hh:hh + 1, :] += -jnp.exp(sink_ref[0, hh] - lse_h) * delta
                        probs.append(p.astype(BF16))
                        dss.append((p * (dp - delta)).astype(BF16))
                    ds2 = jnp.concatenate(dss, axis=0)
                    dq_ref[:, lanes] = (lax.dot_general(ds2, k2, TN, preferred_element_type=F32)
                                        * (HEAD_DIM ** -0.5)).astype(BF16)
                    dk2 = dk2 + jnp.dot(ds2, q2, preferred_element_type=F32)
                    dv2 = dv2 + jnp.dot(jnp.concatenate(probs, axis=0), do2, preferred_element_type=F32)
                dk_cat = (dk2[:2 * BLK, :HEAD_DIM] + dk2[2 * BLK:, HEAD_DIM:]) * (HEAD_DIM ** -0.5)
                dv_cat = dv2[:2 * BLK, :HEAD_DIM] + dv2[2 * BLK:, HEAD_DIM:]
                ko = kvh * HEAD_DIM
                vo = 128 + kvh * HEAD_DIM
                dkv_ref[:, ko:ko + HEAD_DIM] = (carry_ref[:, ko:ko + HEAD_DIM] + dk_cat[:BLK]).astype(BF16)
                dkv_ref[:, vo:vo + HEAD_DIM] = (carry_ref[:, vo:vo + HEAD_DIM] + dv_cat[:BLK]).astype(BF16)
                carry_ref[:, ko:ko + HEAD_DIM] = dk_cat[BLK:]
                carry_ref[:, vo:vo + HEAD_DIM] = dv_cat[BLK:]

        @pl.when(n == nb)
        def _():
            dkv_ref[...] = carry_ref[...].astype(BF16)

    last = nb - 1
    res, xres = _call(
        body, name=name, grid=(nb + 1,),
        in_specs=[SMEM,
                  pl.BlockSpec((BLK, ATTN_W), lambda n: (jnp.minimum(n, last), 0)),
                  pl.BlockSpec((BLK, 256), lambda n: (jnp.minimum(n, last), kvb)),
                  pl.BlockSpec((BLK, 256), lambda n: (jnp.maximum(jnp.minimum(n, last) - 1, 0), kvb)),
                  pl.BlockSpec((BLK, ATTN_W), lambda n: (jnp.minimum(n, last), 0)),
                  pl.BlockSpec((None, N_Q, BLK), lambda n: (jnp.minimum(n, last), 0, 0))],
        out_specs=[pl.BlockSpec((BLK, ATTN_W), lambda n: (jnp.minimum(n, last), 0)),
                   pl.BlockSpec((BLK, 256), lambda n: (jnp.maximum(n - 1, 0), 0)),
                   pl.BlockSpec((N_Q, BLK), lambda n: (0, 0))],
        out_shape=[jax.ShapeDtypeStruct((t, ATTN_W), BF16), jax.ShapeDtypeStruct((t, 256), BF16),
                   jax.ShapeDtypeStruct((N_Q, BLK), F32)],
        scratch_shapes=[pltpu.VMEM((BLK, 256), F32)],
        args=[sinks, proj, proj, proj, dattn, lse], semantics=("arbitrary",), exchanges=exchanges)
    return (*res, xres) if exchanges else res


def _cumsum_rows(x):
    n = x.shape[0]
    row = lax.broadcasted_iota(jnp.int32, x.shape, 0)
    s = 1
    while s < n:
        x = x + jnp.where(row >= s, pltpu.roll(x, s, axis=0), 0.0)
        s *= 2
    return x


def _rev_cumsum_rows(x):
    n = x.shape[0]
    row = lax.broadcasted_iota(jnp.int32, x.shape, 0)
    s = 1
    while s < n:
        x = x + jnp.where(row < n - s, pltpu.roll(x, n - s, axis=0), 0.0)
        s *= 2
    return x


def _lower_bound(lbl_ref):
    l0 = lbl_ref[0:1, :]
    l1 = lbl_ref[1:2, :]
    mx = jnp.maximum(l0, l1)
    e0 = jnp.exp(l0 - mx)
    e1 = jnp.exp(l1 - mx)
    return e0 / (e0 + e1)


def _hgrn_gates(z, lb):
    sg = _sigmoid(z)
    f = lb + (1.0 - lb) * sg
    return sg, f, jnp.log(f), 1.0 - f


def _sub_factors(b, k, i, sub, trim):
    need = -(-sub * i // 16) * 16 if trim else CHUNK
    rows = lax.broadcasted_iota(jnp.int32, (need, RNN_HD), 0)
    ref = b[sub * i - 1:sub * i, :]
    qfac = jnp.exp(b[sub * i:sub * (i + 1), :] - ref)
    kfac = jnp.where(rows < sub * i, jnp.exp(ref - b[:need]), 0.0)
    kt = (k[:need] * kfac).astype(BF16)
    if need < CHUNK:
        kt = jnp.concatenate([kt, jnp.zeros((CHUNK - need, RNN_HD), BF16)], axis=0)
    return qfac, kfac, kt


def _diag_decay(bi, s):
    trow = lax.broadcasted_iota(jnp.int32, bi.shape, 0)
    return jnp.where(trow >= s, jnp.exp(bi - bi[s:s + 1, :]), 0.0)


def _hgrn_fwd(proj, lb_logits, norm_gain, *, tb, name, exchanges=()):
    t = proj.shape[0]
    ntb = t // tb
    nch = tb // CHUNK
    qb, fb, ib, gb = QR_COL // 128, FR_COL // 128, IR_COL // 128, GR_COL // 128

    def body(q_ref, f_ref, i_ref, g_ref, lbl_ref, gain_ref, o_ref, out_ref, s0_ref, st_ref):
        c = pl.program_id(1)

        @pl.when(c == 0)
        def _():
            st_ref[...] = jnp.zeros_like(st_ref)

        lb = _lower_bound(lbl_ref)
        gain = gain_ref[...]

        def chunk(ci, st):
            rows = slice(ci * CHUNK, (ci + 1) * CHUNK)
            _, _, lf, k = _hgrn_gates(f_ref[rows, :], lb)
            qr = q_ref[rows, :]
            q = qr * _sigmoid(qr)
            v = i_ref[rows, :]
            b = _cumsum_rows(lf)
            s0_ref[ci] = st
            o_inter = lax.dot_general((q * jnp.exp(b)).astype(BF16), st.astype(BF16), NT,
                                      preferred_element_type=F32)
            vb = v.astype(BF16)
            blast = b[CHUNK - 1:CHUNK, :]
            khat = (k * jnp.exp(blast - b)).astype(BF16)
            st = st * jnp.exp(blast) + lax.dot_general(vb, khat, TN, preferred_element_type=F32)
            blocks = []
            for i in range(CHUNK // SUB_FWD):
                blk = slice(SUB_FWD * i, SUB_FWD * (i + 1))
                qi, ki, vi, bi = q[blk], k[blk], v[blk], b[blk]
                oi = o_inter[blk]
                if i > 0:
                    qfac, _, kt = _sub_factors(b, k, i, SUB_FWD, trim=True)
                    att = lax.dot_general((qi * qfac).astype(BF16), kt, NT,
                                          preferred_element_type=F32)
                    oi = oi + jnp.dot(att.astype(BF16), vb, preferred_element_type=F32)
                for s in range(SUB_FWD):
                    qe = qi * _diag_decay(bi, s)
                    a = jnp.sum(qe * ki[s:s + 1, :], axis=1, keepdims=True)
                    oi = oi + a * vi[s:s + 1, :]
                blocks.append(oi)
            o = jnp.concatenate(blocks, axis=0)
            o_ref[rows, :] = o
            gr = g_ref[rows, :]
            out_ref[rows, :] = o * _rstd(o) * gain * (gr * _sigmoid(gr))
            return st

        st = st_ref[...]
        for ci in range(nch):
            st = chunk(ci, st)
        st_ref[...] = st

    def col(base):
        return pl.BlockSpec((tb, RNN_HD), lambda h, c: (c, base + h))

    res, xres = _call(
        body, name=name, grid=(N_RNN, ntb),
        in_specs=[col(qb), col(fb), col(ib), col(gb),
                  pl.BlockSpec((2, RNN_HD), lambda h, c: (0, h)), pl.BlockSpec((1, RNN_HD), lambda h, c: (0, 0))],
        out_specs=[pl.BlockSpec((tb, RNN_HD), lambda h, c: (c, h)), pl.BlockSpec((tb, RNN_HD), lambda h, c: (c, h)),
                   pl.BlockSpec((None, nch, RNN_HD, RNN_HD), lambda h, c: (h, c, 0, 0))],
        out_shape=[jax.ShapeDtypeStruct((t, RNN_W), F32), jax.ShapeDtypeStruct((t, RNN_W), F32),
                   jax.ShapeDtypeStruct((N_RNN, t // CHUNK, RNN_HD, RNN_HD), F32)],
        scratch_shapes=[pltpu.VMEM((RNN_HD, RNN_HD), F32)],
        args=[proj, proj, proj, proj, lb_logits, norm_gain],
        semantics=("parallel", "arbitrary"), exchanges=exchanges)
    return (*res, xres) if exchanges else res


def _hgrn_bwd(proj, lb_logits, norm_gain, o_pre, s0, dcat, *, tb, name, exchanges=()):
    t = proj.shape[0]
    ntb = t // tb
    nch = tb // CHUNK
    qb, fb, ib, gb = QR_COL // 128, FR_COL // 128, IR_COL // 128, GR_COL // 128
    sub = SUB_BWD
    nsub = CHUNK // sub

    def body(q_ref, f_ref, i_ref, g_ref, lbl_ref, gain_ref, o_ref, s0_ref, dout_ref,
             dq_ref, df_ref, di_ref, dg_ref, dlb_ref, dgain_ref,
             dst_ref, dqs_ref, dks_ref, dvs_ref):
        c = pl.program_id(1)

        @pl.when(c == 0)
        def _():
            dst_ref[...] = jnp.zeros_like(dst_ref)
            dlb_ref[...] = jnp.zeros_like(dlb_ref)
            dgain_ref[...] = jnp.zeros_like(dgain_ref)

        lb = _lower_bound(lbl_ref)
        gain = gain_ref[...]

        def chunk(ci, dst):
            rows = slice(ci * CHUNK, (ci + 1) * CHUNK)
            dqa_ref, dka_ref, dva_ref = dqs_ref.at[ci], dks_ref.at[ci], dvs_ref.at[ci]
            sg, f, lf, k = _hgrn_gates(f_ref[rows, :], lb)
            qr = q_ref[rows, :]
            sq = _sigmoid(qr)
            q = qr * sq
            v = i_ref[rows, :]
            b = _cumsum_rows(lf)

            dout = dout_ref[rows, :].astype(F32)
            o = o_ref[rows, :]
            gr = g_ref[rows, :]
            sgg = _sigmoid(gr)
            gate = gr * sgg
            rs = _rstd(o)
            nrm = o * rs
            dg_ref[rows, :] = (dout * nrm * gain * (sgg * (1.0 + gr * (1.0 - sgg)))).astype(BF16)
            dn = dout * gate
            dgain_ref[...] += jnp.sum(dn * nrm, axis=0, keepdims=True)
            tt = dn * gain
            do = rs * (tt - nrm * jnp.mean(tt * nrm, axis=-1, keepdims=True))

            dob = do.astype(BF16)
            vb = v.astype(BF16)
            eb = jnp.exp(b)
            blast = b[CHUNK - 1:CHUNK, :]
            ebl = jnp.exp(blast - b)
            dstb = dst.astype(BF16)
            khat = (k * ebl).astype(BF16)
            s0 = s0_ref[ci]
            dqa_ref[...] = eb * jnp.dot(dob, s0.astype(BF16), preferred_element_type=F32)
            dk_state = ebl * jnp.dot(vb, dstb, preferred_element_type=F32)
            dka_ref[...] = dk_state
            d_blast = (jnp.sum(k * dk_state, axis=0, keepdims=True)
                       + jnp.exp(blast) * jnp.sum(dst * s0, axis=0, keepdims=True))
            dva_ref[...] = lax.dot_general(khat, dstb, NT, preferred_element_type=F32)
            dst_next = dst * jnp.exp(blast) + lax.dot_general(dob, (q * eb).astype(BF16), TN,
                                                              preferred_element_type=F32)
            pm = lax.dot_general(dob, vb, NT, preferred_element_type=F32)
            for i in range(nsub):
                blk = slice(sub * i, sub * (i + 1))
                qi, ki, vi, bi, doi = q[blk], k[blk], v[blk], b[blk], do[blk]
                dqi = dqa_ref[blk, :]
                if i > 0:
                    qfac, kfac, kt = _sub_factors(b, k, i, sub, trim=False)
                    qt = (qi * qfac).astype(BF16)
                    att = lax.dot_general(qt, kt, NT, preferred_element_type=F32).astype(BF16)
                    pmi = pm[blk, :].astype(BF16)
                    dva_ref[...] += lax.dot_general(att, doi.astype(BF16), TN, preferred_element_type=F32)
                    dqi = dqi + qfac * jnp.dot(pmi, kt, preferred_element_type=F32)
                    dka_ref[...] += kfac * lax.dot_general(pmi, qt, TN, preferred_element_type=F32)
                dqa_ref[blk, :] = dqi
                srow = lax.broadcasted_iota(jnp.int32, (sub, RNN_HD), 0)
                dki = jnp.zeros((sub, RNN_HD), F32)
                dvi = jnp.zeros((sub, RNN_HD), F32)
                for tq in range(sub):
                    qt, dot_ = qi[tq:tq + 1, :], doi[tq:tq + 1, :]
                    e = jnp.where(srow <= tq, jnp.exp(bi[tq:tq + 1, :] - bi), 0.0)
                    ke = ki * e
                    p = jnp.sum(vi * dot_, axis=1, keepdims=True)
                    a = jnp.sum(ke * qt, axis=1, keepdims=True)
                    dki = dki + p * (qt * e)
                    dvi = dvi + a * dot_
                    row = slice(sub * i + tq, sub * i + tq + 1)
                    dqa_ref[row, :] += jnp.sum(p * ke, axis=0, keepdims=True)
                dka_ref[blk, :] += dki
                dva_ref[blk, :] += dvi

            dq = dqa_ref[...]
            dk = dka_ref[...]
            lastrow = lax.broadcasted_iota(jnp.int32, (CHUNK, RNN_HD), 0) == CHUNK - 1
            dlf = _rev_cumsum_rows(q * dq - k * dk + jnp.where(lastrow, d_blast, 0.0))
            dff = dlf / f - dk
            df_ref[rows, :] = (dff * (1.0 - lb) * sg * (1.0 - sg)).astype(BF16)
            dlb_ref[...] += jnp.sum(dff * (1.0 - sg), axis=0, keepdims=True)
            dq_ref[rows, :] = (dq * (sq * (1.0 + qr * (1.0 - sq)))).astype(BF16)
            di_ref[rows, :] = dva_ref[...].astype(BF16)
            return dst_next

        dst = dst_ref[...]
        for ci in reversed(range(nch)):
            dst = chunk(ci, dst)
        dst_ref[...] = dst

    def col(base):
        return pl.BlockSpec((tb, RNN_HD), lambda h, c: (ntb - 1 - c, base + h))

    outc = pl.BlockSpec((tb, RNN_HD), lambda h, c: (ntb - 1 - c, h))
    hb = ATTN_W // RNN_HD
    res, xres = _call(
        body, name=name, grid=(N_RNN, ntb),
        in_specs=[col(qb), col(fb), col(ib), col(gb),
                  pl.BlockSpec((2, RNN_HD), lambda h, c: (0, h)), pl.BlockSpec((1, RNN_HD), lambda h, c: (0, 0)),
                  outc,
                  pl.BlockSpec((None, nch, RNN_HD, RNN_HD), lambda h, c: (h, ntb - 1 - c, 0, 0)),
                  pl.BlockSpec((tb, RNN_HD), lambda h, c: (ntb - 1 - c, hb + h))],
        out_specs=[outc, outc, outc, outc,
                   pl.BlockSpec((1, RNN_HD), lambda h, c: (0, h)),
                   pl.BlockSpec((None, 1, RNN_HD), lambda h, c: (h, 0, 0))],
        out_shape=[jax.ShapeDtypeStruct((t, RNN_W), BF16)] * 4
        + [jax.ShapeDtypeStruct((1, RNN_W), F32), jax.ShapeDtypeStruct((N_RNN, 1, RNN_HD), F32)],
        scratch_shapes=[pltpu.VMEM((RNN_HD, RNN_HD), F32),
                        pltpu.VMEM((nch, CHUNK, RNN_HD), F32), pltpu.VMEM((nch, CHUNK, RNN_HD), F32),
                        pltpu.VMEM((nch, CHUNK, RNN_HD), F32)],
        args=[proj, proj, proj, proj, lb_logits, norm_gain, o_pre, s0, dcat],
        semantics=("parallel", "arbitrary"), exchanges=exchanges)
    return (*res, xres) if exchanges else res


def _cast_slots(w, where, *, name):
    _, rows, cols = w.shape
    rh = rows // 2
    tr = _row_tile(rh, cols)
    nh = rh // tr

    def body(wh_ref, w_ref, o_ref):
        o_ref[...] = w_ref[...].astype(BF16)

    return pl.pallas_call(
        body, name=name,
        grid_spec=pltpu.PrefetchScalarGridSpec(
            num_scalar_prefetch=1, grid=(2, nh),
            in_specs=[pl.BlockSpec((None, tr, cols), lambda h, i, wh: (0, h * nh + i, 0))],
            out_specs=pl.BlockSpec((None, tr, cols), lambda h, i, wh: (2 * wh[0] + h, i, 0))),
        out_shape=jax.ShapeDtypeStruct((8, rh, cols), BF16),
        compiler_params=_params(("parallel", "parallel")),
    )(where, w)


def _all_gather_halves(bufs, *, name):
    n = len(bufs)

    def body(*refs):
        ins, outs = refs[:n], refs[n:2 * n]
        send_sems, recv_sems = refs[2 * n:]
        x, y, c = _place()
        sibling = (x, y, 1 - c)
        chips = [(1 - x, y), (x, 1 - y), (1 - x, 1 - y)]

        def copy(a, k, block, to, src=None):
            slot = outs[a].at[4 * block[0] + 2 * block[1] + block[2]]
            return pltpu.make_async_remote_copy(
                src_ref=slot if src is None else src, dst_ref=slot,
                send_sem=send_sems.at[a, k], recv_sem=recv_sems.at[a, k],
                device_id=to, device_id_type=MESH)

        first, passed = [], []
        for a in range(n):
            for j, chip in enumerate(chips):
                cp = copy(a, j, (x, y, c), (*chip, c), src=ins[a].at[4 * x + 2 * y + c])
                cp.start()
                first.append(cp)
        for a in range(n):
            for j, chip in enumerate(chips):
                copy(a, j, (*chip, c), (x, y, c)).wait_recv()
                cp = copy(a, 3 + j, (*chip, c), sibling)
                cp.start()
                passed.append(cp)
        for a in range(n):
            for j, chip in enumerate(chips):
                copy(a, 3 + j, (*chip, 1 - c), (x, y, c)).wait_recv()
        for cp in first + passed:
            cp.wait_send()

    return pl.pallas_call(
        body, name=name,
        in_specs=[ANY] * n, out_specs=[ANY] * n,
        out_shape=[jax.ShapeDtypeStruct(b.shape, b.dtype) for b in bufs],
        scratch_shapes=[pltpu.SemaphoreType.DMA((n, 6)), pltpu.SemaphoreType.DMA((n, 6))],
        input_output_aliases={a: a for a in range(n)},
    )(*bufs)


def _row_tile(rows, cols, budget=1 << 20):
    tr = rows
    while tr * cols > budget and tr % 16 == 0:
        tr //= 2
    return tr


def _half_spec(g, tr, halves_last, slab):
    if halves_last:
        return pl.BlockSpec((None, tr, g.shape[2] // 2), lambda *a: (slab(*a), a[-2], a[-1][1]))
    return pl.BlockSpec((None, None, tr, g.shape[3]), lambda *a: (slab(*a), a[-1][1], a[-2], 0))


def _pair_sum(g, sib, where, *, name, halves_last=False):
    rh, cols = sib.shape[1:]
    tr = _row_tile(rh, cols)

    def body(w_ref, g_ref, s_ref, o_ref):
        o_ref[...] = (g_ref[...] + s_ref[...]).astype(BF16)

    return pl.pallas_call(
        body, name=name,
        grid_spec=pltpu.PrefetchScalarGridSpec(
            num_scalar_prefetch=1, grid=(4, rh // tr),
            in_specs=[_half_spec(g, tr, halves_last, lambda s, i, w: s),
                      pl.BlockSpec((None, tr, cols), lambda s, i, w: (s, i, 0))],
            out_specs=pl.BlockSpec((None, tr, cols), lambda s, i, w: (s, i, 0))),
        out_shape=jax.ShapeDtypeStruct((4, rh, cols), BF16),
        compiler_params=_params(("parallel", "parallel")),
    )(where, g, sib)


def _final_half(g, sib, recv, where, *, name, halves_last=False):
    rh, cols = sib.shape[1:]
    tr = _row_tile(rh, cols)

    def body(w_ref, g_ref, s_ref, r_ref, o_ref):
        acc = g_ref[...] + s_ref[...]
        for j in range(3):
            acc = acc + r_ref[j].astype(F32)
        o_ref[...] = acc

    return pl.pallas_call(
        body, name=name,
        grid_spec=pltpu.PrefetchScalarGridSpec(
            num_scalar_prefetch=1, grid=(rh // tr,),
            in_specs=[_half_spec(g, tr, halves_last, lambda i, w: w[0]),
                      pl.BlockSpec((None, tr, cols), lambda i, w: (w[0], i, 0)),
                      pl.BlockSpec((3, tr, cols), lambda i, w: (0, i, 0))],
            out_specs=pl.BlockSpec((tr, cols), lambda i, w: (i, 0))),
        out_shape=jax.ShapeDtypeStruct((rh, cols), F32),
        compiler_params=_params(("parallel",)),
    )(where, g, sib, recv)


def _adamw_math(w, g, m, v):
    m = ADAM_B1 * m + (1.0 - ADAM_B1) * g
    v = ADAM_B2 * v + (1.0 - ADAM_B2) * (g * g)
    m_hat = m / (1.0 - ADAM_B1 ** ADAM_STEP)
    v_hat = v / (1.0 - ADAM_B2 ** ADAM_STEP)
    delta = -ADAM_LR * (m_hat / (jnp.sqrt(v_hat) + ADAM_EPS) + ADAM_WD * w)
    return delta, m, v


def _adamw(w, mine, theirs, m, v, where, *, name, halves_last=False):
    _, rows, cols = w.shape
    if halves_last:
        cols //= 2
        tr = _row_tile(rows, cols, budget=1 << 19)
        grid = (rows // tr, 2)
        blk = pl.BlockSpec((None, tr, cols), lambda i, h, wh: (0, i, h))
        mine_spec = theirs_spec = pl.BlockSpec((tr, cols), lambda i, h, wh: (i, 0))
        which = lambda: pl.program_id(1)
    else:
        tr = _row_tile(rows // 2, cols, budget=1 << 19)
        nh = rows // 2 // tr
        grid = (rows // tr,)
        blk = pl.BlockSpec((None, tr, cols), lambda i, wh: (0, i, 0))
        mine_spec = pl.BlockSpec((tr, cols), lambda i, wh: (jnp.where(i // nh == wh[1], i % nh, 0), 0))
        theirs_spec = pl.BlockSpec((tr, cols), lambda i, wh: (jnp.where(i // nh == wh[1], 0, i % nh), 0))
        which = lambda: pl.program_id(0) // nh

    def body(wh_ref, w_ref, a_ref, b_ref, m_ref, v_ref, g_ref, d_ref, nm_ref, nv_ref):
        g = jnp.where(which() == wh_ref[1], a_ref[...], b_ref[...])
        d, nm, nv = _adamw_math(w_ref[...], g, m_ref[...], v_ref[...])
        g_ref[...] = g
        d_ref[...] = d
        nm_ref[...] = nm
        nv_ref[...] = nv

    rows, cols = w.shape[1:]
    return pl.pallas_call(
        body, name=name,
        grid_spec=pltpu.PrefetchScalarGridSpec(
            num_scalar_prefetch=1, grid=grid,
            in_specs=[blk, mine_spec, theirs_spec, blk, blk], out_specs=[blk] * 4),
        out_shape=[jax.ShapeDtypeStruct((1, rows, cols), F32)] * 4,
        compiler_params=_params(("parallel",) * len(grid)),
    )(where, w, mine, theirs, m, v)


SEG_LOSS = 0
SEG_SINK = 128
SEG_AGAIN = 256
SEG_L0 = SEG_AGAIN + ATTN_W
SEG_L1 = SEG_L0 + RNN_W
SEG_RGAIN = SEG_L1 + RNN_W
SEG_G = SEG_RGAIN + 128
N_PACK = SEG_G + 4 * D_MODEL


def _pack(sinks, again, l0, l1, rgain, gains, loss=None):
    z = lambda k: jnp.zeros((1, k), F32)
    first = z(128) if loss is None else loss
    return jnp.concatenate([first, sinks, z(128 - N_Q), again, l0, l1, rgain] + list(gains), axis=1)


def _small_reduce_adamw(part, w, m, v, *, name):
    def body(p_ref, w_ref, m_ref, v_ref, g_ref, d_ref, nm_ref, nv_ref, buf_ref, send_sems, recv_sems):
        x, y, c = _place()
        me = 4 * x + 2 * y + c
        copies = []
        for k in range(1, 8):
            dx, dy, dc = (k >> 2) & 1, (k >> 1) & 1, k & 1
            to = (x ^ dx, y ^ dy, c ^ dc)
            cp = pltpu.make_async_remote_copy(
                src_ref=p_ref, dst_ref=buf_ref.at[me],
                send_sem=send_sems.at[k - 1], recv_sem=recv_sems.at[k - 1],
                device_id=to, device_id_type=MESH)
            cp.start()
            copies.append(cp)
        buf_ref[me] = p_ref[...]
        for cp in copies:
            cp.wait()
        tot = buf_ref[0]
        for j in range(1, 8):
            tot = tot + buf_ref[j]
        g_ref[...] = tot
        l0 = w_ref[:, SEG_L0:SEG_L0 + RNN_W]
        l1 = w_ref[:, SEG_L1:SEG_L1 + RNN_W]
        mx = jnp.maximum(l0, l1)
        e0 = jnp.exp(l0 - mx)
        e1 = jnp.exp(l1 - mx)
        lb = e0 / (e0 + e1)
        gl0 = tot[:, SEG_L0:SEG_L0 + RNN_W] * lb * (1.0 - lb)
        g_ref[:, SEG_L0:SEG_L0 + RNN_W] = gl0
        g_ref[:, SEG_L1:SEG_L1 + RNN_W] = -gl0
        d, nm, nv = _adamw_math(w_ref[...], g_ref[...], m_ref[...], v_ref[...])
        d_ref[...] = d
        nm_ref[...] = nm
        nv_ref[...] = nv

    vm = pl.BlockSpec(memory_space=pltpu.VMEM)
    return pl.pallas_call(
        body, name=name,
        in_specs=[vm] * 4, out_specs=[vm] * 4,
        out_shape=[jax.ShapeDtypeStruct((1, N_PACK), F32)] * 4,
        scratch_shapes=[pltpu.VMEM((8, 1, N_PACK), F32), pltpu.SemaphoreType.DMA((7,)),
                        pltpu.SemaphoreType.DMA((7,))],
    )(part, w, m, v)


def _layer_grads(xs, tgt, bufs, where, sinks, again, lb_logits, rgain,
                 g_mix_pre, g_mix_post, g_mlp_pre, g_mlp_post):
    tm = 512
    b_in, b_out, b_up, b_dn = bufs

    shard = IN_W // N_CHIPS
    w_in_t = _all_gather_halves([b_in], name="gather_w_in")[0].reshape(IN_W, D_MODEL)
    h1 = _rms_cast(xs, g_mix_pre, tm=tm, name="h1_norm")
    proj, ((b_out, b_up),) = _mm(
        h1, w_in_t, tm=1024, tn=768, tk=D_MODEL, out_dtype=F32, w_layout="nk", name="in_proj",
        exchanges=[_x_gather([b_out, b_up], ici=[(0, 256), (0, 336)])])
    attn, lse, ((b_out, b_up),) = _swa_fwd(
        proj, sinks, name="swa_fwd",
        exchanges=[_x_gather([b_out, b_up], ici=[None, (336, 320)], d2d=[(0, 256), None])])
    w_out = b_out.reshape(D_MODEL, D_MODEL)
    o_pre, rnn, s0, ((b_up, b_dn),) = _hgrn_fwd(
        proj, lb_logits, rgain, tb=512, name="hgrn_fwd",
        exchanges=[_x_gather([b_up, b_dn], ici=[(656, 368), (0, 400)])])
    cat = _mix_cat(attn, rnn, again, tm=tm, name="mix_cat")
    mixed, ((b_up, b_dn),) = _mm(
        cat, w_out, tm=1024, tn=1024, tk=D_MODEL, out_dtype=BF16, name="out_proj",
        exchanges=[_x_gather([b_up, b_dn], ici=[None, (400, 320)], d2d=[(0, 1024), (0, 400)])])
    w_up4 = b_up.reshape(N_CHIPS, D_MODEL, D_FF // N_CHIPS)
    x1, h2, ((b_dn,),) = _post_norm_res(
        mixed, g_mix_post, xs, g_mlp_pre, tm=256, name="mix_post",
        exchanges=[_x_gather([b_dn], d2d=[(400, 320)])])
    u, ((b_dn,),) = _mm(h2, w_up4, tm=1024, tn=1024, tk=D_MODEL, out_dtype=BF16, relu=True, w_layout="skn",
                        name="mlp_up", exchanges=[_x_gather([b_dn], ici=[(720, 304)], cross=[(720, 304)])])
    w_dn = b_dn.reshape(D_FF, D_MODEL)
    yv = _mm(u, w_dn, tm=1024, tn=1024, tk=2048, out_dtype=BF16, a_square=True, name="mlp_down")
    dy, dx2, loss_row, dg_mlp_post = _loss_head(yv, g_mlp_post, x1, tgt, tm=256, name="loss_head")

    def halved(g):
        return g.reshape(N_CHIPS, 2, g.shape[1] // 2, g.shape[2])
    du = _mm(dy, w_dn, tm=1024, tn=1024, tk=D_MODEL, out_dtype=BF16, mul2=u, w_layout="nk", name="mlp_down_bwd")
    g_dn = halved(_mm_tn(u, dy, tm=1024, tn=1024, tt=2048, a_square=True, name="w_down_grad")
                  .reshape(N_CHIPS, D_FF // N_CHIPS, D_MODEL))
    d_w_up, ((sib_dn,),) = _mm_tn(h2, du, tm=1024, tn=1024, tt=2048, n_split=N_CHIPS, name="w_up_grad",
                                  exchanges=[_x_pair([g_dn])])
    g_up = halved(d_w_up)
    wire_dn = _pair_sum(g_dn, sib_dn, where, name="pair_sum_w_down")
    dh2, ((recv_dn,), (sib_up,)) = _mm(du, w_up4, tm=1024, tn=1024, tk=2048, out_dtype=BF16, w_layout="snk", name="mlp_up_bwd",
                                       exchanges=[_x_chip([wire_dn], rows=[(0, 800)]), _x_pair([g_up])])
    wire_up = _pair_sum(g_up, sib_up, where, name="pair_sum_w_up")
    dx1, dg_mlp_pre, ((recv_dn,),) = _rms_bwd(dh2, x1, g_mlp_pre, dx2, tm=256, out_dtype=F32, name="mlp_pre_bwd",
                                              exchanges=[_x_chip([wire_dn], rows=[(800, 224)], into=[recv_dn])])
    fin_dn = _final_half(g_dn, sib_dn, recv_dn, where, name="final_half_w_down")
    dmixed, dg_mix_post = _rms_bwd(dx1, mixed, g_mix_post, None, tm=256, out_dtype=BF16, name="mix_post_bwd")
    d_w_out, ((oth_dn,),) = _mm_tn(cat, dmixed, tm=1024, tn=1024, tt=2048, name="w_out_grad",
                                   exchanges=[_x_share([fin_dn])])
    g_out = halved(d_w_out.reshape(N_CHIPS, D_MODEL // N_CHIPS, D_MODEL))
    dcat, ((sib_out,),) = _mm(dmixed, w_out, tm=1024, tn=1024, tk=D_MODEL, out_dtype=BF16, w_layout="nk", name="out_proj_bwd",
                              exchanges=[_x_pair([g_out])])
    wire_out = _pair_sum(g_out, sib_out, where, name="pair_sum_w_out")
    dattn, dg_again = _rms_bwd(dcat, attn, again, None, tm=tm, out_dtype=BF16, name="attn_norm_bwd")
    dq_a, dkv, dsinks, ((recv_out,), (recv_up,)) = _swa_bwd(
        proj, sinks, dattn, lse, name="swa_bwd",
        exchanges=[_x_chip([wire_out]), _x_chip([wire_up], rows=[(0, 320)])])
    dq_r, df_r, di_r, dg_r, dlb, dgain_h, ((recv_up,),) = _hgrn_bwd(
        proj, lb_logits, rgain, o_pre, s0, dcat, tb=512, name="hgrn_bwd",
        exchanges=[_x_chip([wire_up], rows=[(320, 704)], into=[recv_up])])
    fin_up = _final_half(g_up, sib_up, recv_up, where, name="final_half_w_up")
    fin_out = _final_half(g_out, sib_out, recv_out, where, name="final_half_w_out")
    dproj = jnp.concatenate([dq_a, dkv, dq_r, df_r, di_r, dg_r], axis=1)
    piece_cols = D_MODEL // 4

    def w_in_piece(pc, exchanges):
        d, xres = _mm_tn(dproj, h1, tm=896, tn=piece_cols, tt=2048, n_blocks=(2, 2, pc),
                         name="w_in_grad_%d" % pc, exchanges=exchanges)
        return d.reshape(N_CHIPS, shard, 2 * piece_cols), xres

    g_in0, ((oth_up, oth_out),) = w_in_piece(0, [_x_share([fin_up, fin_out])])
    g_in1, ((sib_in0,),) = w_in_piece(1, [_x_pair([g_in0], halves_last=True)])
    wire_in0 = _pair_sum(g_in0, sib_in0, where, name="pair_sum_w_in_0", halves_last=True)
    dh1, ((recv_in0,), (sib_in1,)) = _mm(
        dproj, w_in_t, tm=1024, tn=1024, tk=2688, out_dtype=BF16, m_blocks=(0, 2), name="in_proj_bwd_0",
        exchanges=[_x_chip([wire_in0]), _x_pair([g_in1], halves_last=True)])
    wire_in1 = _pair_sum(g_in1, sib_in1, where, name="pair_sum_w_in_1", halves_last=True)
    dh1, ((recv_in1,),) = _mm(
        dproj, w_in_t, tm=1024, tn=1024, tk=2688, out_dtype=BF16, m_blocks=(2, 2), out_into=dh1,
        name="in_proj_bwd_1", exchanges=[_x_chip([wire_in1])])
    gx, dg_mix_pre = _rms_bwd(dh1, xs, g_mix_pre, dx1, tm=256, out_dtype=F32, name="mix_pre_bwd")
    fin_in0 = _final_half(g_in0, sib_in0, recv_in0, where, name="final_half_w_in_0", halves_last=True)
    fin_in1 = _final_half(g_in1, sib_in1, recv_in1, where, name="final_half_w_in_1", halves_last=True)
    oth_in0, oth_in1 = _run_exchange(_x_share([fin_in0, fin_in1]), name="share_w_in")
    fin_in = jnp.concatenate([fin_in0, fin_in1], axis=1)
    oth_in = jnp.concatenate([oth_in0, oth_in1], axis=1)

    big = [(fin_in, oth_in), (fin_out, oth_out), (fin_up, oth_up), (fin_dn, oth_dn)]
    drgain = jnp.sum(dgain_h, axis=0)
    small = _pack(jnp.sum(dsinks, axis=1)[None, :], dg_again, dlb, jnp.zeros_like(dlb), drgain,
                  [dg_mix_pre, dg_mix_post, dg_mlp_pre, dg_mlp_post], loss=loss_row)
    return gx, big, small


def kernel(x, w_in, attn_sinks, attn_out_gain, rnn_lb_logits, rnn_norm_gain, w_out, mix_pre_gain, mix_post_gain, mlp_pre_gain, mlp_post_gain, w_up, w_down, loss_target, m_w_in, m_attn_sinks, m_attn_out_gain, m_rnn_lb_logits, m_rnn_norm_gain, m_w_out, m_mix_pre_gain, m_mix_post_gain, m_mlp_pre_gain, m_mlp_post_gain, m_w_up, m_w_down, v_w_in, v_attn_sinks, v_attn_out_gain, v_rnn_lb_logits, v_rnn_norm_gain, v_w_out, v_mix_pre_gain, v_mix_post_gain, v_mlp_pre_gain, v_mlp_post_gain, v_w_up, v_w_down):
    ax, ay, ac = _place()
    where = jnp.stack([2 * ax + ay, ac]).astype(jnp.int32)
    t = lambda a: jnp.swapaxes(a, 1, 2)
    big_w = [t(w_in), w_out, w_up, w_down]
    big_m = [t(m_w_in), m_w_out, m_w_up, m_w_down]
    big_v = [t(v_w_in), v_w_out, v_w_up, v_w_down]

    names = ["w_in", "w_out", "w_up", "w_down"]
    bufs = [_cast_slots(w, where, name="cast_" + nm) for w, nm in zip(big_w, names)]
    gx, big_g, small_part = _layer_grads(
        x[0], loss_target[0], bufs, where, attn_sinks, attn_out_gain, rnn_lb_logits, rnn_norm_gain,
        mix_pre_gain, mix_post_gain, mlp_pre_gain, mlp_post_gain)

    grads, deltas, new_m, new_v = [], [], [], []
    for (f, o), w, m, v, nm in zip(big_g, big_w, big_m, big_v, names):
        res = _adamw(w, f, o, m, v, where, name="adamw_" + nm, halves_last=(nm == "w_in"))
        if nm == "w_in":
            res = [t(r) for r in res]
        g, d, nm_, nv_ = res
        grads.append(g)
        deltas.append(d)
        new_m.append(nm_)
        new_v.append(nv_)

    def pack_params(sinks, again, logits, rgain, gains):
        return _pack(sinks, again, logits[0:1], logits[1:2], rgain, gains)

    pw = pack_params(attn_sinks, attn_out_gain, rnn_lb_logits, rnn_norm_gain,
                     [mix_pre_gain, mix_post_gain, mlp_pre_gain, mlp_post_gain])
    pm = pack_params(m_attn_sinks, m_attn_out_gain, m_rnn_lb_logits, m_rnn_norm_gain,
                     [m_mix_pre_gain, m_mix_post_gain, m_mlp_pre_gain, m_mlp_post_gain])
    pv = pack_params(v_attn_sinks, v_attn_out_gain, v_rnn_lb_logits, v_rnn_norm_gain,
                     [v_mix_pre_gain, v_mix_post_gain, v_mlp_pre_gain, v_mlp_post_gain])
    packs = _small_reduce_adamw(small_part, pw, pm, pv, name="small_reduce_adamw")

    def unpack(p):
        seg = lambda o, k: p[:, o:o + k]
        logits = jnp.concatenate([seg(SEG_L0, RNN_W), seg(SEG_L1, RNN_W)], axis=0)
        gains = [seg(SEG_G + i * D_MODEL, D_MODEL) for i in range(4)]
        return dict(sinks=seg(SEG_SINK, N_Q), again=seg(SEG_AGAIN, ATTN_W), logits=logits,
                    rgain=seg(SEG_RGAIN, RNN_HD), gains=gains)

    def order(small, big):
        return [big[0], small["sinks"], small["again"], small["logits"], small["rgain"], big[1],
                *small["gains"], big[2], big[3]]

    loss = packs[0][0, 0]
    outs = [loss, gx[None]]
    for p, b in zip(packs, [grads, deltas, new_m, new_v]):
        outs += order(unpack(p), b)
    return tuple(outs)
```

```python
import functools

import jax
import jax.numpy as jnp
from jax import lax
from jax.experimental import pallas as pl
from jax.experimental.pallas import tpu as pltpu

F32 = jnp.float32
BF16 = jnp.bfloat16
MESH = pl.DeviceIdType.MESH

EPS = 1e-6
D_MODEL = 2048
ATTN_W = 1024
HEAD_DIM = 64
N_Q = 16
N_KV = 2
GROUP = 8
BLK = 128
RNN_W = 1024
RNN_HD = 128
N_RNN = 8
CHUNK = 64
SUB_FWD = 16
SUB_BWD = 8
D_FF = 8192
IN_W = 5376
N_CHIPS = 4
KV_COL = ATTN_W
QR_COL = ATTN_W + 2 * 128
FR_COL = QR_COL + RNN_W
IR_COL = FR_COL + RNN_W
GR_COL = IR_COL + RNN_W

ADAM_LR = 0.001
ADAM_B1 = 0.9
ADAM_B2 = 0.999
ADAM_EPS = 1e-08
ADAM_WD = 0.01
ADAM_STEP = 10

VMEM_LIMIT = 48 * 1024 * 1024

NT = (((1,), (1,)), ((), ()))
TN = (((0,), (0,)), ((), ()))


def _params(sem=None):
    return pltpu.CompilerParams(dimension_semantics=sem, vmem_limit_bytes=VMEM_LIMIT)


def _sigmoid(x):
    return 1.0 / (1.0 + jnp.exp(-x))


ANY = pl.BlockSpec(memory_space=pl.ANY)


def _place():
    return lax.axis_index("x"), lax.axis_index("y"), lax.axis_index("c")


def _other_chips(x, y):
    return [(1 - x, y), (x, 1 - y), (1 - x, 1 - y)]


class _Exchange:
    def __init__(self, srcs, outs, ncopy, build, aliases=None):
        self.srcs, self.outs, self.ncopy, self.build = list(srcs), list(outs), ncopy, build
        self.aliases = aliases or {}


def _remote(src, dst, send_sems, recv_sems, k, to):
    return pltpu.make_async_remote_copy(src_ref=src, dst_ref=dst, send_sem=send_sems.at[k],
                                        recv_sem=recv_sems.at[k], device_id=to, device_id_type=MESH)


def _call(body, *, name, grid, in_specs, out_specs, out_shape, args, scratch_shapes=(), semantics=None,
          exchanges=(), into=None):
    in_specs, out_specs, out_shape = list(in_specs), list(out_specs), list(out_shape)
    scratch_shapes = list(scratch_shapes)
    ni, no, ns = len(in_specs), len(out_specs), len(scratch_shapes)
    xsrc = [s for x in exchanges for s in x.srcs]
    xout = [o for x in exchanges for o in x.outs]
    into = into or {}
    xsrc += [into[k] for k in sorted(into)]
    nxi, nxo = len(xsrc), len(xout)
    aliases = {nxi - len(into) + ni + q: k for q, k in enumerate(sorted(into))}
    a0 = b0 = 0
    for x in exchanges:
        for si, oi in x.aliases.items():
            aliases[ni + a0 + si] = no + b0 + oi
        a0 += len(x.srcs)
        b0 += len(x.outs)
    sems = []
    for x in exchanges:
        sems += [pltpu.SemaphoreType.DMA((x.ncopy,)), pltpu.SemaphoreType.DMA((x.ncopy,))]

    def wrapped(*refs):
        ins, xi = refs[:ni], refs[ni:ni + nxi]
        outs, xo = refs[ni + nxi:ni + nxi + no], refs[ni + nxi + no:ni + nxi + no + nxo]
        rest = refs[ni + nxi + no + nxo:]
        scr, sm = rest[:ns], rest[ns:]

        def copies():
            cps = []
            a = b = 0
            for k, x in enumerate(exchanges):
                cps += x.build(xi[a:a + len(x.srcs)], xo[b:b + len(x.outs)], sm[2 * k], sm[2 * k + 1])
                a += len(x.srcs)
                b += len(x.outs)
            return cps

        def start():
            for cp in copies():
                cp.start()

        def wait():
            for cp in copies():
                cp.wait()

        if not exchanges:
            body(*ins, *outs, *scr)
        elif not grid:
            start()
            body(*ins, *outs, *scr)
            wait()
        else:
            first = last = None
            for ax, g in enumerate(grid):
                f = pl.program_id(ax) == 0
                l = pl.program_id(ax) == g - 1
                first = f if first is None else first & f
                last = l if last is None else last & l
            pl.when(first)(start)
            body(*ins, *outs, *scr)
            pl.when(last)(wait)

    if exchanges and semantics is not None:
        semantics = ("arbitrary",) * len(grid)
    kwargs = dict(grid=grid) if grid else {}
    res = pl.pallas_call(
        wrapped, name=name,
        in_specs=in_specs + [ANY] * nxi, out_specs=out_specs + [ANY] * nxo,
        out_shape=out_shape + xout, scratch_shapes=scratch_shapes + sems,
        input_output_aliases=aliases,
        compiler_params=_params(semantics), **kwargs,
    )(*args, *xsrc)
    res = list(res)
    mine, theirs = res[:no], res[no:]
    per = []
    b = 0
    for x in exchanges:
        per.append(theirs[b:b + len(x.outs)])
        b += len(x.outs)
    return mine, per


def _run_exchange(x, *, name):
    return _call(lambda: None, name=name, grid=(), in_specs=[], out_specs=[], out_shape=[], args=[],
                 exchanges=[x])[1][0]


def _x_gather(bufs, ici=None, d2d=None, cross=None):
    n = len(bufs)
    plan = [(a, kind, rows[a]) for a in range(n) for kind, rows in (("ici", ici), ("d2d", d2d), ("cross", cross))
            if rows is not None and rows[a] is not None]

    def build(srcs, outs, ss, rs):
        x, y, c = _place()
        cps = []
        for q, (a, kind, rows) in enumerate(plan):
            piece = pl.ds(*rows)
            for j, (px, py) in enumerate(_other_chips(x, y)):
                if kind == "d2d":
                    slot, to = 4 * px + 2 * py + c, (x, y, 1 - c)
                else:
                    slot, to = 4 * x + 2 * y + c, (px, py, c if kind == "ici" else 1 - c)
                cps.append(_remote(srcs[a].at[slot, piece], outs[a].at[slot, piece], ss, rs, 3 * q + j, to))
        return cps

    outs = [jax.ShapeDtypeStruct(b.shape, b.dtype) for b in bufs]
    return _Exchange(bufs, outs, 3 * len(plan), build, aliases={a: a for a in range(n)})


def _x_pair(grads, halves_last=False):
    n = len(grads)

    def build(srcs, outs, ss, rs):
        x, y, c = _place()

        def half(r):
            if not halves_last:
                return r.at[:, 1 - c]
            ch = r.shape[2] // 2
            return r.at[:, :, pl.ds(pl.multiple_of((1 - c) * ch, 128), ch)]

        return [_remote(half(srcs[a]), outs[a], ss, rs, a, (x, y, 1 - c)) for a in range(n)]

    if halves_last:
        outs = [jax.ShapeDtypeStruct(g.shape[:2] + (g.shape[2] // 2,), g.dtype) for g in grads]
    else:
        outs = [jax.ShapeDtypeStruct((4,) + g.shape[2:], g.dtype) for g in grads]
    return _Exchange(grads, outs, n, build)


def _x_chip(wires, rows=None, into=None):
    n = len(wires)
    rows = rows or [(0, w.shape[1]) for w in wires]

    def build(srcs, outs, ss, rs):
        x, y, c = _place()
        cps = []
        for a in range(n):
            piece = pl.ds(*rows[a])
            for j, (px, py) in enumerate(_other_chips(x, y)):
                cps.append(_remote(srcs[a].at[2 * px + py, piece], outs[a].at[j, piece], ss, rs,
                                   3 * a + j, (px, py, c)))
        return cps

    outs = [jax.ShapeDtypeStruct((3,) + w.shape[1:], w.dtype) for w in wires]
    if into is None:
        return _Exchange(wires, outs, 3 * n, build)
    return _Exchange(list(wires) + list(into), outs, 3 * n, build, aliases={n + a: a for a in range(n)})


def _x_share(halves):
    n = len(halves)

    def build(srcs, outs, ss, rs):
        x, y, c = _place()
        return [_remote(srcs[a], outs[a], ss, rs, a, (x, y, 1 - c)) for a in range(n)]

    outs = [jax.ShapeDtypeStruct(h.shape, h.dtype) for h in halves]
    return _Exchange(halves, outs, n, build)


def _mm(a, w, *, tm, tn, tk, out_dtype, name, a_square=False, relu=False, mul2=None, w_layout="kn",
        m_blocks=None, out_into=None, exchanges=()):
    m, k = a.shape
    m_first, m_count = m_blocks or (0, m // tm)
    a_spec = pl.BlockSpec((tm, tk), lambda i, j, kk: (i + m_first, kk))
    if w_layout == "kn":
        n = w.shape[1]
        w_spec = pl.BlockSpec((tk, tn), lambda i, j, kk: (kk, j))
    elif w_layout == "nk":
        n = w.shape[0]
        w_spec = pl.BlockSpec((tn, tk), lambda i, j, kk: (j, kk))
    elif w_layout == "skn":
        n = w.shape[0] * w.shape[2]
        per_n = w.shape[2] // tn
        w_spec = pl.BlockSpec((None, tk, tn), lambda i, j, kk: (j // per_n, kk, j % per_n))
    else:
        assert w_layout == "snk"
        n = w.shape[1]
        per_k = w.shape[2] // tk
        w_spec = pl.BlockSpec((None, tn, tk), lambda i, j, kk: (kk // per_k, j, kk % per_k))
    w_dims = NT if w_layout in ("nk", "snk") else (((1,), (0,)), ((), ()))
    nk = k // tk
    assert m % tm == 0 and n % tn == 0 and k % tk == 0

    def body(*refs):
        if mul2 is not None:
            a_ref, w_ref, e_ref, o_ref, acc_ref = refs
        else:
            a_ref, w_ref, o_ref, acc_ref = refs
            e_ref = None
        kk = pl.program_id(2)
        av = a_ref[...]
        if a_square:
            af = av.astype(F32)
            av = (af * af).astype(BF16)
        part = lax.dot_general(av, w_ref[...], w_dims, preferred_element_type=F32)

        def finish(r):
            if relu:
                r = jnp.maximum(r, 0.0)
            if e_ref is not None:
                r = 2.0 * e_ref[...].astype(F32) * r
            o_ref[...] = r.astype(out_dtype)

        if nk == 1:
            finish(part)
        else:
            @pl.when(kk == 0)
            def _():
                acc_ref[...] = part

            @pl.when(kk > 0)
            def _():
                acc_ref[...] += part

            @pl.when(kk == nk - 1)
            def _():
                finish(acc_ref[...])

    in_specs = [a_spec, w_spec]
    args = [a, w]
    if mul2 is not None:
        in_specs.append(pl.BlockSpec((tm, tn), lambda i, j, kk: (i + m_first, j)))
        args.append(mul2)
    acc_shape = (tm, tn) if nk > 1 else (8, 128)
    (out,), per = _call(
        body, name=name, grid=(m_count, n // tn, nk),
        in_specs=in_specs, out_specs=[pl.BlockSpec((tm, tn), lambda i, j, kk: (i + m_first, j))],
        out_shape=[jax.ShapeDtypeStruct((m, n), out_dtype)], args=args,
        scratch_shapes=[pltpu.VMEM(acc_shape, F32)],
        semantics=("parallel", "parallel", "arbitrary"), exchanges=exchanges,
        into=None if out_into is None else {0: out_into})
    return (out, per) if exchanges else out


def _mm_tn(a, b, *, tm, tn, tt, name, a_square=False, n_split=1, n_blocks=None, exchanges=()):
    t, m = a.shape
    n = b.shape[1]
    assert t % tt == 0 and m % tm == 0 and n % tn == 0
    count, stride, first = n_blocks or (n // tn, 1, 0)
    n = count * tn
    assert (n // n_split) % tn == 0
    per = n // n_split // tn

    def body(a_ref, b_ref, o_ref):
        ti = pl.program_id(2)
        av = a_ref[...]
        if a_square:
            af = av.astype(F32)
            av = (af * af).astype(BF16)
        part = lax.dot_general(av, b_ref[...], TN, preferred_element_type=F32)

        @pl.when(ti == 0)
        def _():
            o_ref[...] = part

        @pl.when(ti > 0)
        def _():
            o_ref[...] += part

    (out,), xres = _call(
        body, name=name, grid=(m // tm, n // tn, t // tt),
        in_specs=[pl.BlockSpec((tt, tm), lambda i, j, ti: (ti, i)),
                  pl.BlockSpec((tt, tn), lambda i, j, ti: (ti, first + stride * j))],
        out_specs=[pl.BlockSpec((None, tm, tn), lambda i, j, ti: (j // per, i, j % per))],
        out_shape=[jax.ShapeDtypeStruct((n_split, m, n // n_split), F32)], args=[a, b],
        semantics=("parallel", "parallel", "arbitrary"), exchanges=exchanges)
    return (out, xres) if exchanges else out


def _rstd(x):
    return lax.rsqrt(jnp.mean(x * x, axis=-1, keepdims=True) + EPS)


def _rms_cast(x, g, *, tm, name):
    t, d = x.shape

    def body(x_ref, g_ref, o_ref):
        xv = x_ref[...]
        o_ref[...] = (xv * _rstd(xv) * g_ref[...]).astype(BF16)

    return pl.pallas_call(
        body, name=name, grid=(t // tm,),
        in_specs=[pl.BlockSpec((tm, d), lambda i: (i, 0)), pl.BlockSpec((1, d), lambda i: (0, 0))],
        out_specs=pl.BlockSpec((tm, d), lambda i: (i, 0)),
        out_shape=jax.ShapeDtypeStruct((t, d), BF16),
        compiler_params=_params(("parallel",)),
    )(x, g)


def _mix_cat(attn, rnn, gain, *, tm, name):
    t = attn.shape[0]

    def body(a_ref, r_ref, g_ref, o_ref):
        av = a_ref[...]
        o_ref[:, :ATTN_W] = (av * _rstd(av) * g_ref[...]).astype(BF16)
        o_ref[:, ATTN_W:] = r_ref[...].astype(BF16)

    return pl.pallas_call(
        body, name=name, grid=(t // tm,),
        in_specs=[pl.BlockSpec((tm, ATTN_W), lambda i: (i, 0)), pl.BlockSpec((tm, RNN_W), lambda i: (i, 0)),
                  pl.BlockSpec((1, ATTN_W), lambda i: (0, 0))],
        out_specs=pl.BlockSpec((tm, D_MODEL), lambda i: (i, 0)),
        out_shape=jax.ShapeDtypeStruct((t, D_MODEL), BF16),
        compiler_params=_params(("parallel",)),
    )(attn, rnn, gain)


def _post_norm_res(mixed, g_post, res, g_next, *, tm, name, exchanges=()):
    t, d = mixed.shape

    def body(m_ref, gp_ref, r_ref, gn_ref, x1_ref, h2_ref):
        mv = m_ref[...].astype(F32)
        x1 = r_ref[...] + mv * _rstd(mv) * gp_ref[...]
        x1_ref[...] = x1
        h2_ref[...] = (x1 * _rstd(x1) * gn_ref[...]).astype(BF16)

    row = pl.BlockSpec((tm, d), lambda i: (i, 0))
    vec = pl.BlockSpec((1, d), lambda i: (0, 0))
    res_, xres = _call(
        body, name=name, grid=(t // tm,),
        in_specs=[row, vec, row, vec], out_specs=[row, row],
        out_shape=[jax.ShapeDtypeStruct((t, d), F32), jax.ShapeDtypeStruct((t, d), BF16)],
        args=[mixed, g_post, res, g_next], semantics=("parallel",), exchanges=exchanges)
    return (*res_, xres) if exchanges else res_


def _rms_bwd(dyn, xin, g, res, *, tm, out_dtype, name, col_block=0, exchanges=()):
    t, d = xin.shape

    def body(*refs):
        if res is not None:
            dy_ref, x_ref, g_ref, r_ref, dx_ref, dg_ref = refs
        else:
            dy_ref, x_ref, g_ref, dx_ref, dg_ref = refs
        i = pl.program_id(0)
        xv = x_ref[...].astype(F32)
        dy = dy_ref[...].astype(F32)
        r = _rstd(xv)
        xh = xv * r
        part = jnp.sum(dy * xh, axis=0, keepdims=True)

        @pl.when(i == 0)
        def _():
            dg_ref[...] = part

        @pl.when(i > 0)
        def _():
            dg_ref[...] += part

        tt = dy * g_ref[...]
        dx = r * (tt - xh * jnp.mean(tt * xh, axis=-1, keepdims=True))
        if res is not None:
            dx = dx + r_ref[...]
        dx_ref[...] = dx.astype(out_dtype)

    row = pl.BlockSpec((tm, d), lambda i: (i, 0))
    vec = pl.BlockSpec((1, d), lambda i: (0, 0))
    in_specs = [pl.BlockSpec((tm, d), lambda i: (i, col_block)), row, vec]
    args = [dyn, xin, g]
    if res is not None:
        in_specs.append(row)
        args.append(res)
    res, xres = _call(
        body, name=name, grid=(t // tm,),
        in_specs=in_specs, out_specs=[row, vec],
        out_shape=[jax.ShapeDtypeStruct((t, d), out_dtype), jax.ShapeDtypeStruct((1, d), F32)], args=args,
        semantics=("arbitrary",), exchanges=exchanges)
    return (*res, xres) if exchanges else res


def _loss_head(y, g_post, x1, target, *, tm, name):
    t, d = y.shape

    def body(y_ref, g_ref, x1_ref, t_ref, dy_ref, dx2_ref, loss_ref, dg_ref):
        i = pl.program_id(0)
        yv = y_ref[...].astype(F32)
        r = _rstd(yv)
        yh = yv * r
        gv = g_ref[...]
        err = x1_ref[...] + yh * gv - t_ref[...]
        lpart = 0.5 * jnp.sum(jnp.mean(err * err, axis=-1, keepdims=True), axis=0, keepdims=True)
        dx2 = err * (1.0 / d)
        dgp = jnp.sum(dx2 * yh, axis=0, keepdims=True)
        lane = lax.broadcasted_iota(jnp.int32, (1, 128), 1)
        lrow = jnp.where(lane == 0, lpart, 0.0)

        @pl.when(i == 0)
        def _():
            dg_ref[...] = dgp
            loss_ref[...] = lrow

        @pl.when(i > 0)
        def _():
            dg_ref[...] += dgp
            loss_ref[...] += lrow

        tt = dx2 * gv
        dy_ref[...] = (r * (tt - yh * jnp.mean(tt * yh, axis=-1, keepdims=True))).astype(BF16)
        dx2_ref[...] = dx2

    row = pl.BlockSpec((tm, d), lambda i: (i, 0))
    vec = pl.BlockSpec((1, d), lambda i: (0, 0))
    return pl.pallas_call(
        body, name=name, grid=(t // tm,),
        in_specs=[row, vec, row, row],
        out_specs=[row, row, pl.BlockSpec((1, 128), lambda i: (0, 0)), vec],
        out_shape=[jax.ShapeDtypeStruct((t, d), BF16), jax.ShapeDtypeStruct((t, d), F32),
                   jax.ShapeDtypeStruct((1, 128), F32), jax.ShapeDtypeStruct((1, d), F32)],
        compiler_params=_params(("arbitrary",)),
    )(y, g_post, x1, target)


def _alibi_slope(h):
    return 2.0 ** (-8.0 * (h + 1) / N_Q)


PAIR = 2 * HEAD_DIM
N_PAIRS = N_Q // 2
PAIRS_PER_KV = GROUP // 2
SMEM = pl.BlockSpec(memory_space=pltpu.SMEM)


def _swa_mask(n):
    key = lax.broadcasted_iota(jnp.int32, (2 * BLK, BLK), 0)
    qry = lax.broadcasted_iota(jnp.int32, (2 * BLK, BLK), 1)
    dist = qry + BLK - key
    valid = (dist >= 0) & (dist < BLK) & ((key >= BLK) | (n > 0))
    return valid, dist.astype(F32)


def _block_diag(kvp_ref, kvc_ref, off):
    a = jnp.concatenate([kvp_ref[:, off:off + HEAD_DIM], kvc_ref[:, off:off + HEAD_DIM]], axis=0).astype(BF16)
    z = jnp.zeros_like(a)
    return jnp.concatenate([jnp.concatenate([a, z], axis=1), jnp.concatenate([z, a], axis=1)], axis=0)


def _swa_scores(s2, e, hh, valid, distf):
    s = s2[2 * BLK * e:2 * BLK * (e + 1)] * (HEAD_DIM ** -0.5) - _alibi_slope(hh) * distf
    return jnp.where(valid, s, -1e30)


def _swa_fwd(proj, sinks, *, name, exchanges=()):
    t = proj.shape[0]
    nb = t // BLK
    kvb = KV_COL // (2 * 128)

    def body(sink_ref, q_ref, kvc_ref, kvp_ref, o_ref, lse_ref):
        n = pl.program_id(0)
        valid, distf = _swa_mask(n)
        for kvh in range(N_KV):
            k2 = _block_diag(kvp_ref, kvc_ref, kvh * HEAD_DIM)
            v2 = _block_diag(kvp_ref, kvc_ref, 128 + kvh * HEAD_DIM)
            for jp in range(PAIRS_PER_KV):
                pair = kvh * PAIRS_PER_KV + jp
                lanes = slice(pair * PAIR, (pair + 1) * PAIR)
                s2 = lax.dot_general(k2, q_ref[:, lanes].astype(BF16), NT, preferred_element_type=F32)
                probs = []
                for e in range(2):
                    hh = 2 * pair + e
                    s = _swa_scores(s2, e, hh, valid, distf)
                    sink = sink_ref[0, hh]
                    mx = jnp.maximum(jnp.max(s, axis=0, keepdims=True), sink)
                    p = jnp.exp(s - mx)
                    l = jnp.sum(p, axis=0, keepdims=True) + jnp.exp(sink - mx)
                    probs.append((p * (1.0 / l)).astype(BF16))
                    lse_ref[hh:hh + 1, :] = mx + jnp.log(l)
                o_ref[:, lanes] = lax.dot_general(jnp.concatenate(probs, axis=0), v2, TN,
                                                  preferred_element_type=F32)

    res, xres = _call(
        body, name=name, grid=(nb,),
        in_specs=[SMEM,
                  pl.BlockSpec((BLK, ATTN_W), lambda n: (n, 0)),
                  pl.BlockSpec((BLK, 256), lambda n: (n, kvb)),
                  pl.BlockSpec((BLK, 256), lambda n: (jnp.maximum(n - 1, 0), kvb))],
        out_specs=[pl.BlockSpec((BLK, ATTN_W), lambda n: (n, 0)),
                   pl.BlockSpec((None, N_Q, BLK), lambda n: (n, 0, 0))],
        out_shape=[jax.ShapeDtypeStruct((t, ATTN_W), F32), jax.ShapeDtypeStruct((nb, N_Q, BLK), F32)],
        args=[sinks, proj, proj, proj], semantics=("parallel",), exchanges=exchanges)
    return (*res, xres) if exchanges else res


def _swa_bwd(proj, sinks, dattn, lse, *, name, exchanges=()):
    t = proj.shape[0]
    nb = t // BLK
    kvb = KV_COL // (2 * 128)

    def body(sink_ref, q_ref, kvc_ref, kvp_ref, do_ref, lse_ref, dq_ref, dkv_ref, dsink_ref, carry_ref):
        n = pl.program_id(0)

        @pl.when(n == 0)
        def _():
            dsink_ref[...] = jnp.zeros_like(dsink_ref)
            carry_ref[...] = jnp.zeros_like(carry_ref)

        @pl.when(n < nb)
        def _():
            valid, distf = _swa_mask(n)
            for kvh in range(N_KV):
                k2 = _block_diag(kvp_ref, kvc_ref, kvh * HEAD_DIM)
                v2 = _block_diag(kvp_ref, kvc_ref, 128 + kvh * HEAD_DIM)
                dk2 = jnp.zeros((4 * BLK, PAIR), F32)
                dv2 = jnp.zeros((4 * BLK, PAIR), F32)
                for jp in range(PAIRS_PER_KV):
                    pair = kvh * PAIRS_PER_KV + jp
                    lanes = slice(pair * PAIR, (pair + 1) * PAIR)
                    q2 = q_ref[:, lanes].astype(BF16)
                    do2 = do_ref[:, lanes].astype(BF16)
                    s2 = lax.dot_general(k2, q2, NT, preferred_element_type=F32)
                    dp2 = lax.dot_general(v2, do2, NT, preferred_element_type=F32)
                    probs, dss = [], []
                    for e in range(2):
                        hh = 2 * pair + e
                        lse_h = lse_ref[hh:hh + 1, :]
                        p = jnp.exp(_swa_scores(s2, e, hh, valid, distf) - lse_h)
                        dp = dp2[2 * BLK * e:2 * BLK * (e + 1)]
                        delta = jnp.sum(p * dp, axis=0, keepdims=True)
                        dsink_ref[hh:hh + 1, :] += -jnp.exp(sink_ref[0, hh] - lse_h) * delta
                        probs.append(p.astype(BF16))
                        dss.append((p * (dp - delta)).astype(BF16))
                    ds2 = jnp.concatenate(dss, axis=0)
                    dq_ref[:, lanes] = (lax.dot_general(ds2, k2, TN, preferred_element_type=F32)
                                        * (HEAD_DIM ** -0.5)).astype(BF16)
                    dk2 = dk2 + jnp.dot(ds2, q2, preferred_element_type=F32)
                    dv2 = dv2 + jnp.dot(jnp.concatenate(probs, axis=0), do2, preferred_element_type=F32)
                dk_cat = (dk2[:2 * BLK, :HEAD_DIM] + dk2[2 * BLK:, HEAD_DIM:]) * (HEAD_DIM ** -0.5)
                dv_cat = dv2[:2 * BLK, :HEAD_DIM] + dv2[2 * BLK:, HEAD_DIM:]
                ko = kvh * HEAD_DIM
                vo = 128 + kvh * HEAD_DIM
                dkv_ref[:, ko:ko + HEAD_DIM] = (carry_ref[:, ko:ko + HEAD_DIM] + dk_cat[:BLK]).astype(BF16)
                dkv_ref[:, vo:vo + HEAD_DIM] = (carry_ref[:, vo:vo + HEAD_DIM] + dv_cat[:BLK]).astype(BF16)
                carry_ref[:, ko:ko + HEAD_DIM] = dk_cat[BLK:]
                carry_ref[:, vo:vo + HEAD_DIM] = dv_cat[BLK:]

        @pl.when(n == nb)
        def _():
            dkv_ref[...] = carry_ref[...].astype(BF16)

    last = nb - 1
    res, xres = _call(
        body, name=name, grid=(nb + 1,),
        in_specs=[SMEM,
                  pl.BlockSpec((BLK, ATTN_W), lambda n: (jnp.minimum(n, last), 0)),
                  pl.BlockSpec((BLK, 256), lambda n: (jnp.minimum(n, last), kvb)),
                  pl.BlockSpec((BLK, 256), lambda n: (jnp.maximum(jnp.minimum(n, last) - 1, 0), kvb)),
                  pl.BlockSpec((BLK, ATTN_W), lambda n: (jnp.minimum(n, last), 0)),
                  pl.BlockSpec((None, N_Q, BLK), lambda n: (jnp.minimum(n, last), 0, 0))],
        out_specs=[pl.BlockSpec((BLK, ATTN_W), lambda n: (jnp.minimum(n, last), 0)),
                   pl.BlockSpec((BLK, 256), lambda n: (jnp.maximum(n - 1, 0), 0)),
                   pl.BlockSpec((N_Q, BLK), lambda n: (0, 0))],
        out_shape=[jax.ShapeDtypeStruct((t, ATTN_W), BF16), jax.ShapeDtypeStruct((t, 256), BF16),
                   jax.ShapeDtypeStruct((N_Q, BLK), F32)],
        scratch_shapes=[pltpu.VMEM((BLK, 256), F32)],
        args=[sinks, proj, proj, proj, dattn, lse], semantics=("arbitrary",), exchanges=exchanges)
    return (*res, xres) if exchanges else res


def _cumsum_rows(x):
    n = x.shape[0]
    row = lax.broadcasted_iota(jnp.int32, x.shape, 0)
    s = 1
    while s < n:
        x = x + jnp.where(row >= s, pltpu.roll(x, s, axis=0), 0.0)
        s *= 2
    return x


def _rev_cumsum_rows(x):
    n = x.shape[0]
    row = lax.broadcasted_iota(jnp.int32, x.shape, 0)
    s = 1
    while s < n:
        x = x + jnp.where(row < n - s, pltpu.roll(x, n - s, axis=0), 0.0)
        s *= 2
    return x


def _lower_bound(lbl_ref):
    l0 = lbl_ref[0:1, :]
    l1 = lbl_ref[1:2, :]
    mx = jnp.maximum(l0, l1)
    e0 = jnp.exp(l0 - mx)
    e1 = jnp.exp(l1 - mx)
    return e0 / (e0 + e1)


def _hgrn_gates(z, lb):
    sg = _sigmoid(z)
    f = lb + (1.0 - lb) * sg
    return sg, f, jnp.log(f), 1.0 - f


def _sub_factors(b, k, i, sub, trim):
    need = -(-sub * i // 16) * 16 if trim else CHUNK
    rows = lax.broadcasted_iota(jnp.int32, (need, RNN_HD), 0)
    ref = b[sub * i - 1:sub * i, :]
    qfac = jnp.exp(b[sub * i:sub * (i + 1), :] - ref)
    kfac = jnp.where(rows < sub * i, jnp.exp(ref - b[:need]), 0.0)
    kt = (k[:need] * kfac).astype(BF16)
    if need < CHUNK:
        kt = jnp.concatenate([kt, jnp.zeros((CHUNK - need, RNN_HD), BF16)], axis=0)
    return qfac, kfac, kt


def _diag_decay(bi, s):
    trow = lax.broadcasted_iota(jnp.int32, bi.shape, 0)
    return jnp.where(trow >= s, jnp.exp(bi - bi[s:s + 1, :]), 0.0)


def _hgrn_fwd(proj, lb_logits, norm_gain, *, tb, name, exchanges=()):
    t = proj.shape[0]
    ntb = t // tb
    nch = tb // CHUNK
    qb, fb, ib, gb = QR_COL // 128, FR_COL // 128, IR_COL // 128, GR_COL // 128

    def body(q_ref, f_ref, i_ref, g_ref, lbl_ref, gain_ref, o_ref, out_ref, s0_ref, st_ref):
        c = pl.program_id(1)

        @pl.when(c == 0)
        def _():
            st_ref[...] = jnp.zeros_like(st_ref)

        lb = _lower_bound(lbl_ref)
        gain = gain_ref[...]

        def chunk(ci, st):
            rows = slice(ci * CHUNK, (ci + 1) * CHUNK)
            _, _, lf, k = _hgrn_gates(f_ref[rows, :], lb)
            qr = q_ref[rows, :]
            q = qr * _sigmoid(qr)
            v = i_ref[rows, :]
            b = _cumsum_rows(lf)
            s0_ref[ci] = st
            o_inter = lax.dot_general((q * jnp.exp(b)).astype(BF16), st.astype(BF16), NT,
                                      preferred_element_type=F32)
            vb = v.astype(BF16)
            blast = b[CHUNK - 1:CHUNK, :]
            khat = (k * jnp.exp(blast - b)).astype(BF16)
            st = st * jnp.exp(blast) + lax.dot_general(vb, khat, TN, preferred_element_type=F32)
            blocks = []
            for i in range(CHUNK // SUB_FWD):
                blk = slice(SUB_FWD * i, SUB_FWD * (i + 1))
                qi, ki, vi, bi = q[blk], k[blk], v[blk], b[blk]
                oi = o_inter[blk]
                if i > 0:
                    qfac, _, kt = _sub_factors(b, k, i, SUB_FWD, trim=True)
                    att = lax.dot_general((qi * qfac).astype(BF16), kt, NT,
                                          preferred_element_type=F32)
                    oi = oi + jnp.dot(att.astype(BF16), vb, preferred_element_type=F32)
                for s in range(SUB_FWD):
                    qe = qi * _diag_decay(bi, s)
                    a = jnp.sum(qe * ki[s:s + 1, :], axis=1, keepdims=True)
                    oi = oi + a * vi[s:s + 1, :]
                blocks.append(oi)
            o = jnp.concatenate(blocks, axis=0)
            o_ref[rows, :] = o
            gr = g_ref[rows, :]
            out_ref[rows, :] = o * _rstd(o) * gain * (gr * _sigmoid(gr))
            return st

        st = st_ref[...]
        for ci in range(nch):
            st = chunk(ci, st)
        st_ref[...] = st

    def col(base):
        return pl.BlockSpec((tb, RNN_HD), lambda h, c: (c, base + h))

    res, xres = _call(
        body, name=name, grid=(N_RNN, ntb),
        in_specs=[col(qb), col(fb), col(ib), col(gb),
                  pl.BlockSpec((2, RNN_HD), lambda h, c: (0, h)), pl.BlockSpec((1, RNN_HD), lambda h, c: (0, 0))],
        out_specs=[pl.BlockSpec((tb, RNN_HD), lambda h, c: (c, h)), pl.BlockSpec((tb, RNN_HD), lambda h, c: (c, h)),
                   pl.BlockSpec((None, nch, RNN_HD, RNN_HD), lambda h, c: (h, c, 0, 0))],
        out_shape=[jax.ShapeDtypeStruct((t, RNN_W), F32), jax.ShapeDtypeStruct((t, RNN_W), F32),
                   jax.ShapeDtypeStruct((N_RNN, t // CHUNK, RNN_HD, RNN_HD), F32)],
        scratch_shapes=[pltpu.VMEM((RNN_HD, RNN_HD), F32)],
        args=[proj, proj, proj, proj, lb_logits, norm_gain],
        semantics=("parallel", "arbitrary"), exchanges=exchanges)
    return (*res, xres) if exchanges else res


def _hgrn_bwd(proj, lb_logits, norm_gain, o_pre, s0, dcat, *, tb, name, exchanges=()):
    t = proj.shape[0]
    ntb = t // tb
    nch = tb // CHUNK
    qb, fb, ib, gb = QR_COL // 128, FR_COL // 128, IR_COL // 128, GR_COL // 128
    sub = SUB_BWD
    nsub = CHUNK // sub

    def body(q_ref, f_ref, i_ref, g_ref, lbl_ref, gain_ref, o_ref, s0_ref, dout_ref,
             dq_ref, df_ref, di_ref, dg_ref, dlb_ref, dgain_ref,
             dst_ref, dqs_ref, dks_ref, dvs_ref):
        c = pl.program_id(1)

        @pl.when(c == 0)
        def _():
            dst_ref[...] = jnp.zeros_like(dst_ref)
            dlb_ref[...] = jnp.zeros_like(dlb_ref)
            dgain_ref[...] = jnp.zeros_like(dgain_ref)

        lb = _lower_bound(lbl_ref)
        gain = gain_ref[...]

        def chunk(ci, dst):
            rows = slice(ci * CHUNK, (ci + 1) * CHUNK)
            dqa_ref, dka_ref, dva_ref = dqs_ref.at[ci], dks_ref.at[ci], dvs_ref.at[ci]
            sg, f, lf, k = _hgrn_gates(f_ref[rows, :], lb)
            qr = q_ref[rows, :]
            sq = _sigmoid(qr)
            q = qr * sq
            v = i_ref[rows, :]
            b = _cumsum_rows(lf)

            dout = dout_ref[rows, :].astype(F32)
            o = o_ref[rows, :]
            gr = g_ref[rows, :]
            sgg = _sigmoid(gr)
            gate = gr * sgg
            rs = _rstd(o)
            nrm = o * rs
            dg_ref[rows, :] = (dout * nrm * gain * (sgg * (1.0 + gr * (1.0 - sgg)))).astype(BF16)
            dn = dout * gate
            dgain_ref[...] += jnp.sum(dn * nrm, axis=0, keepdims=True)
            tt = dn * gain
            do = rs * (tt - nrm * jnp.mean(tt * nrm, axis=-1, keepdims=True))

            dob = do.astype(BF16)
            vb = v.astype(BF16)
            eb = jnp.exp(b)
            blast = b[CHUNK - 1:CHUNK, :]
            ebl = jnp.exp(blast - b)
            dstb = dst.astype(BF16)
            khat = (k * ebl).astype(BF16)
            s0 = s0_ref[ci]
            dqa_ref[...] = eb * jnp.dot(dob, s0.astype(BF16), preferred_element_type=F32)
            dk_state = ebl * jnp.dot(vb, dstb, preferred_element_type=F32)
            dka_ref[...] = dk_state
            d_blast = (jnp.sum(k * dk_state, axis=0, keepdims=True)
                       + jnp.exp(blast) * jnp.sum(dst * s0, axis=0, keepdims=True))
            dva_ref[...] = lax.dot_general(khat, dstb, NT, preferred_element_type=F32)
            dst_next = dst * jnp.exp(blast) + lax.dot_general(dob, (q * eb).astype(BF16), TN,
                                                              preferred_element_type=F32)
            pm = lax.dot_general(dob, vb, NT, preferred_element_type=F32)
            for i in range(nsub):
                blk = slice(sub * i, sub * (i + 1))
                qi, ki, vi, bi, doi = q[blk], k[blk], v[blk], b[blk], do[blk]
                dqi = dqa_ref[blk, :]
                if i > 0:
                    qfac, kfac, kt = _sub_factors(b, k, i, sub, trim=False)
                    qt = (qi * qfac).astype(BF16)
                    att = lax.dot_general(qt, kt, NT, preferred_element_type=F32).astype(BF16)
                    pmi = pm[blk, :].astype(BF16)
                    dva_ref[...] += lax.dot_general(att, doi.astype(BF16), TN, preferred_element_type=F32)
                    dqi = dqi + qfac * jnp.dot(pmi, kt, preferred_element_type=F32)
                    dka_ref[...] += kfac * lax.dot_general(pmi, qt, TN, preferred_element_type=F32)
                dqa_ref[blk, :] = dqi
                srow = lax.broadcasted_iota(jnp.int32, (sub, RNN_HD), 0)
                dki = jnp.zeros((sub, RNN_HD), F32)
                dvi = jnp.zeros((sub, RNN_HD), F32)
                for tq in range(sub):
                    qt, dot_ = qi[tq:tq + 1, :], doi[tq:tq + 1, :]
                    e = jnp.where(srow <= tq, jnp.exp(bi[tq:tq + 1, :] - bi), 0.0)
                    ke = ki * e
                    p = jnp.sum(vi * dot_, axis=1, keepdims=True)
                    a = jnp.sum(ke * qt, axis=1, keepdims=True)
                    dki = dki + p * (qt * e)
                    dvi = dvi + a * dot_
                    row = slice(sub * i + tq, sub * i + tq + 1)
                    dqa_ref[row, :] += jnp.sum(p * ke, axis=0, keepdims=True)
                dka_ref[blk, :] += dki
                dva_ref[blk, :] += dvi

            dq = dqa_ref[...]
            dk = dka_ref[...]
            lastrow = lax.broadcasted_iota(jnp.int32, (CHUNK, RNN_HD), 0) == CHUNK - 1
            dlf = _rev_cumsum_rows(q * dq - k * dk + jnp.where(lastrow, d_blast, 0.0))
            dff = dlf / f - dk
            df_ref[rows, :] = (dff * (1.0 - lb) * sg * (1.0 - sg)).astype(BF16)
            dlb_ref[...] += jnp.sum(dff * (1.0 - sg), axis=0, keepdims=True)
            dq_ref[rows, :] = (dq * (sq * (1.0 + qr * (1.0 - sq)))).astype(BF16)
            di_ref[rows, :] = dva_ref[...].astype(BF16)
            return dst_next

        dst = dst_ref[...]
        for ci in reversed(range(nch)):
            dst = chunk(ci, dst)
        dst_ref[...] = dst

    def col(base):
        return pl.BlockSpec((tb, RNN_HD), lambda h, c: (ntb - 1 - c, base + h))

    outc = pl.BlockSpec((tb, RNN_HD), lambda h, c: (ntb - 1 - c, h))
    hb = ATTN_W // RNN_HD
    res, xres = _call(
        body, name=name, grid=(N_RNN, ntb),
        in_specs=[col(qb), col(fb), col(ib), col(gb),
                  pl.BlockSpec((2, RNN_HD), lambda h, c: (0, h)), pl.BlockSpec((1, RNN_HD), lambda h, c: (0, 0)),
                  outc,
                  pl.BlockSpec((None, nch, RNN_HD, RNN_HD), lambda h, c: (h, ntb - 1 - c, 0, 0)),
                  pl.BlockSpec((tb, RNN_HD), lambda h, c: (ntb - 1 - c, hb + h))],
        out_specs=[outc, outc, outc, outc,
                   pl.BlockSpec((1, RNN_HD), lambda h, c: (0, h)),
                   pl.BlockSpec((None, 1, RNN_HD), lambda h, c: (h, 0, 0))],
        out_shape=[jax.ShapeDtypeStruct((t, RNN_W), BF16)] * 4
        + [jax.ShapeDtypeStruct((1, RNN_W), F32), jax.ShapeDtypeStruct((N_RNN, 1, RNN_HD), F32)],
        scratch_shapes=[pltpu.VMEM((RNN_HD, RNN_HD), F32),
                        pltpu.VMEM((nch, CHUNK, RNN_HD), F32), pltpu.VMEM((nch, CHUNK, RNN_HD), F32),
                        pltpu.VMEM((nch, CHUNK, RNN_HD), F32)],
        args=[proj, proj, proj, proj, lb_logits, norm_gain, o_pre, s0, dcat],
        semantics=("parallel", "arbitrary"), exchanges=exchanges)
    return (*res, xres) if exchanges else res


def _cast_slots(w, where, *, name):
    _, rows, cols = w.shape
    rh = rows // 2
    tr = _row_tile(rh, cols)
    nh = rh // tr

    def body(wh_ref, w_ref, o_ref):
        o_ref[...] = w_ref[...].astype(BF16)

    return pl.pallas_call(
        body, name=name,
        grid_spec=pltpu.PrefetchScalarGridSpec(
            num_scalar_prefetch=1, grid=(2, nh),
            in_specs=[pl.BlockSpec((None, tr, cols), lambda h, i, wh: (0, h * nh + i, 0))],
            out_specs=pl.BlockSpec((None, tr, cols), lambda h, i, wh: (2 * wh[0] + h, i, 0))),
        out_shape=jax.ShapeDtypeStruct((8, rh, cols), BF16),
        compiler_params=_params(("parallel", "parallel")),
    )(where, w)


def _all_gather_halves(bufs, *, name):
    n = len(bufs)

    def body(*refs):
        ins, outs = refs[:n], refs[n:2 * n]
        send_sems, recv_sems = refs[2 * n:]
        x, y, c = _place()
        sibling = (x, y, 1 - c)
        chips = [(1 - x, y), (x, 1 - y), (1 - x, 1 - y)]

        def copy(a, k, block, to, src=None):
            slot = outs[a].at[4 * block[0] + 2 * block[1] + block[2]]
            return pltpu.make_async_remote_copy(
                src_ref=slot if src is None else src, dst_ref=slot,
                send_sem=send_sems.at[a, k], recv_sem=recv_sems.at[a, k],
                device_id=to, device_id_type=MESH)

        first, passed = [], []
        for a in range(n):
            for j, chip in enumerate(chips):
                cp = copy(a, j, (x, y, c), (*chip, c), src=ins[a].at[4 * x + 2 * y + c])
                cp.start()
                first.append(cp)
        for a in range(n):
            for j, chip in enumerate(chips):
                copy(a, j, (*chip, c), (x, y, c)).wait_recv()
                cp = copy(a, 3 + j, (*chip, c), sibling)
                cp.start()
                passed.append(cp)
        for a in range(n):
            for j, chip in enumerate(chips):
                copy(a, 3 + j, (*chip, 1 - c), (x, y, c)).wait_recv()
        for cp in first + passed:
            cp.wait_send()

    return pl.pallas_call(
        body, name=name,
        in_specs=[ANY] * n, out_specs=[ANY] * n,
        out_shape=[jax.ShapeDtypeStruct(b.shape, b.dtype) for b in bufs],
        scratch_shapes=[pltpu.SemaphoreType.DMA((n, 6)), pltpu.SemaphoreType.DMA((n, 6))],
        input_output_aliases={a: a for a in range(n)},
    )(*bufs)


def _row_tile(rows, cols, budget=1 << 20):
    tr = rows
    while tr * cols > budget and tr % 16 == 0:
        tr //= 2
    return tr


def _half_spec(g, tr, halves_last, slab):
    if halves_last:
        return pl.BlockSpec((None, tr, g.shape[2] // 2), lambda *a: (slab(*a), a[-2], a[-1][1]))
    return pl.BlockSpec((None, None, tr, g.shape[3]), lambda *a: (slab(*a), a[-1][1], a[-2], 0))


def _pair_sum(g, sib, where, *, name, halves_last=False):
    rh, cols = sib.shape[1:]
    tr = _row_tile(rh, cols)

    def body(w_ref, g_ref, s_ref, o_ref):
        o_ref[...] = (g_ref[...] + s_ref[...]).astype(BF16)

    return pl.pallas_call(
        body, name=name,
        grid_spec=pltpu.PrefetchScalarGridSpec(
            num_scalar_prefetch=1, grid=(4, rh // tr),
            in_specs=[_half_spec(g, tr, halves_last, lambda s, i, w: s),
                      pl.BlockSpec((None, tr, cols), lambda s, i, w: (s, i, 0))],
            out_specs=pl.BlockSpec((None, tr, cols), lambda s, i, w: (s, i, 0))),
        out_shape=jax.ShapeDtypeStruct((4, rh, cols), BF16),
        compiler_params=_params(("parallel", "parallel")),
    )(where, g, sib)


def _final_half(g, sib, recv, where, *, name, halves_last=False):
    rh, cols = sib.shape[1:]
    tr = _row_tile(rh, cols)

    def body(w_ref, g_ref, s_ref, r_ref, o_ref):
        acc = g_ref[...] + s_ref[...]
        for j in range(3):
            acc = acc + r_ref[j].astype(F32)
        o_ref[...] = acc

    return pl.pallas_call(
        body, name=name,
        grid_spec=pltpu.PrefetchScalarGridSpec(
            num_scalar_prefetch=1, grid=(rh // tr,),
            in_specs=[_half_spec(g, tr, halves_last, lambda i, w: w[0]),
                      pl.BlockSpec((None, tr, cols), lambda i, w: (w[0], i, 0)),
                      pl.BlockSpec((3, tr, cols), lambda i, w: (0, i, 0))],
            out_specs=pl.BlockSpec((tr, cols), lambda i, w: (i, 0))),
        out_shape=jax.ShapeDtypeStruct((rh, cols), F32),
        compiler_params=_params(("parallel",)),
    )(where, g, sib, recv)


def _adamw_math(w, g, m, v):
    m = ADAM_B1 * m + (1.0 - ADAM_B1) * g
    v = ADAM_B2 * v + (1.0 - ADAM_B2) * (g * g)
    m_hat = m / (1.0 - ADAM_B1 ** ADAM_STEP)
    v_hat = v / (1.0 - ADAM_B2 ** ADAM_STEP)
    delta = -ADAM_LR * (m_hat / (jnp.sqrt(v_hat) + ADAM_EPS) + ADAM_WD * w)
    return delta, m, v


def _adamw(w, mine, theirs, m, v, where, *, name, halves_last=False):
    _, rows, cols = w.shape
    if halves_last:
        cols //= 2
        tr = _row_tile(rows, cols, budget=1 << 19)
        grid = (rows // tr, 2)
        blk = pl.BlockSpec((None, tr, cols), lambda i, h, wh: (0, i, h))
        mine_spec = theirs_spec = pl.BlockSpec((tr, cols), lambda i, h, wh: (i, 0))
        which = lambda: pl.program_id(1)
    else:
        tr = _row_tile(rows // 2, cols, budget=1 << 19)
        nh = rows // 2 // tr
        grid = (rows // tr,)
        blk = pl.BlockSpec((None, tr, cols), lambda i, wh: (0, i, 0))
        mine_spec = pl.BlockSpec((tr, cols), lambda i, wh: (jnp.where(i // nh == wh[1], i % nh, 0), 0))
        theirs_spec = pl.BlockSpec((tr, cols), lambda i, wh: (jnp.where(i // nh == wh[1], 0, i % nh), 0))
        which = lambda: pl.program_id(0) // nh

    def body(wh_ref, w_ref, a_ref, b_ref, m_ref, v_ref, g_ref, d_ref, nm_ref, nv_ref):
        g = jnp.where(which() == wh_ref[1], a_ref[...], b_ref[...])
        d, nm, nv = _adamw_math(w_ref[...], g, m_ref[...], v_ref[...])
        g_ref[...] = g
        d_ref[...] = d
        nm_ref[...] = nm
        nv_ref[...] = nv

    rows, cols = w.shape[1:]
    return pl.pallas_call(
        body, name=name,
        grid_spec=pltpu.PrefetchScalarGridSpec(
            num_scalar_prefetch=1, grid=grid,
            in_specs=[blk, mine_spec, theirs_spec, blk, blk], out_specs=[blk] * 4),
        out_shape=[jax.ShapeDtypeStruct((1, rows, cols), F32)] * 4,
        compiler_params=_params(("parallel",) * len(grid)),
    )(where, w, mine, theirs, m, v)


SEG_LOSS = 0
SEG_SINK = 128
SEG_AGAIN = 256
SEG_L0 = SEG_AGAIN + ATTN_W
SEG_L1 = SEG_L0 + RNN_W
SEG_RGAIN = SEG_L1 + RNN_W
SEG_G = SEG_RGAIN + 128
N_PACK = SEG_G + 4 * D_MODEL


def _pack(sinks, again, l0, l1, rgain, gains, loss=None):
    z = lambda k: jnp.zeros((1, k), F32)
    first = z(128) if loss is None else loss
    return jnp.concatenate([first, sinks, z(128 - N_Q), again, l0, l1, rgain] + list(gains), axis=1)


def _small_reduce_adamw(part, w, m, v, *, name):
    def body(p_ref, w_ref, m_ref, v_ref, g_ref, d_ref, nm_ref, nv_ref, buf_ref, send_sems, recv_sems):
        x, y, c = _place()
        me = 4 * x + 2 * y + c
        copies = []
        for k in range(1, 8):
            dx, dy, dc = (k >> 2) & 1, (k >> 1) & 1, k & 1
            to = (x ^ dx, y ^ dy, c ^ dc)
            cp = pltpu.make_async_remote_copy(
                src_ref=p_ref, dst_ref=buf_ref.at[me],
                send_sem=send_sems.at[k - 1], recv_sem=recv_sems.at[k - 1],
                device_id=to, device_id_type=MESH)
            cp.start()
            copies.append(cp)
        buf_ref[me] = p_ref[...]
        for cp in copies:
            cp.wait()
        tot = buf_ref[0]
        for j in range(1, 8):
            tot = tot + buf_ref[j]
        g_ref[...] = tot
        l0 = w_ref[:, SEG_L0:SEG_L0 + RNN_W]
        l1 = w_ref[:, SEG_L1:SEG_L1 + RNN_W]
        mx = jnp.maximum(l0, l1)
        e0 = jnp.exp(l0 - mx)
        e1 = jnp.exp(l1 - mx)
        lb = e0 / (e0 + e1)
        gl0 = tot[:, SEG_L0:SEG_L0 + RNN_W] * lb * (1.0 - lb)
        g_ref[:, SEG_L0:SEG_L0 + RNN_W] = gl0
        g_ref[:, SEG_L1:SEG_L1 + RNN_W] = -gl0
        d, nm, nv = _adamw_math(w_ref[...], g_ref[...], m_ref[...], v_ref[...])
        d_ref[...] = d
        nm_ref[...] = nm
        nv_ref[...] = nv

    vm = pl.BlockSpec(memory_space=pltpu.VMEM)
    return pl.pallas_call(
        body, name=name,
        in_specs=[vm] * 4, out_specs=[vm] * 4,
        out_shape=[jax.ShapeDtypeStruct((1, N_PACK), F32)] * 4,
        scratch_shapes=[pltpu.VMEM((8, 1, N_PACK), F32), pltpu.SemaphoreType.DMA((7,)),
                        pltpu.SemaphoreType.DMA((7,))],
    )(part, w, m, v)


def _layer_grads(xs, tgt, bufs, where, sinks, again, lb_logits, rgain,
                 g_mix_pre, g_mix_post, g_mlp_pre, g_mlp_post):
    tm = 512
    b_in, b_out, b_up, b_dn = bufs

    shard = IN_W // N_CHIPS
    w_in_t = _all_gather_halves([b_in], name="gather_w_in")[0].reshape(IN_W, D_MODEL)
    h1 = _rms_cast(xs, g_mix_pre, tm=tm, name="h1_norm")
    proj, ((b_out, b_up),) = _mm(
        h1, w_in_t, tm=1024, tn=768, tk=D_MODEL, out_dtype=F32, w_layout="nk", name="in_proj",
        exchanges=[_x_gather([b_out, b_up], ici=[(0, 256), (0, 336)])])
    attn, lse, ((b_out, b_up),) = _swa_fwd(
        proj, sinks, name="swa_fwd",
        exchanges=[_x_gather([b_out, b_up], ici=[None, (336, 320)], d2d=[(0, 256), None])])
    w_out = b_out.reshape(D_MODEL, D_MODEL)
    o_pre, rnn, s0, ((b_up, b_dn),) = _hgrn_fwd(
        proj, lb_logits, rgain, tb=512, name="hgrn_fwd",
        exchanges=[_x_gather([b_up, b_dn], ici=[(656, 368), (0, 400)])])
    cat = _mix_cat(attn, rnn, again, tm=tm, name="mix_cat")
    mixed, ((b_up, b_dn),) = _mm(
        cat, w_out, tm=1024, tn=1024, tk=D_MODEL, out_dtype=BF16, name="out_proj",
        exchanges=[_x_gather([b_up, b_dn], ici=[None, (400, 240)], d2d=[(0, 1024), (0, 400)])])
    w_up4 = b_up.reshape(N_CHIPS, D_MODEL, D_FF // N_CHIPS)
    x1, h2, ((b_dn,),) = _post_norm_res(
        mixed, g_mix_post, xs, g_mlp_pre, tm=256, name="mix_post",
        exchanges=[_x_gather([b_dn], d2d=[(400, 240)])])
    u, ((b_dn,),) = _mm(h2, w_up4, tm=1024, tn=1024, tk=D_MODEL, out_dtype=BF16, relu=True, w_layout="skn",
                        name="mlp_up", exchanges=[_x_gather([b_dn], ici=[(640, 384)], cross=[(640, 384)])])
    w_dn = b_dn.reshape(D_FF, D_MODEL)
    yv = _mm(u, w_dn, tm=1024, tn=1024, tk=2048, out_dtype=BF16, a_square=True, name="mlp_down")
    dy, dx2, loss_row, dg_mlp_post = _loss_head(yv, g_mlp_post, x1, tgt, tm=256, name="loss_head")

    def halved(g):
        return g.reshape(N_CHIPS, 2, g.shape[1] // 2, g.shape[2])
    du = _mm(dy, w_dn, tm=1024, tn=1024, tk=D_MODEL, out_dtype=BF16, mul2=u, w_layout="nk", name="mlp_down_bwd")
    g_dn = halved(_mm_tn(u, dy, tm=1024, tn=1024, tt=2048, a_square=True, name="w_down_grad")
                  .reshape(N_CHIPS, D_FF // N_CHIPS, D_MODEL))
    d_w_up, ((sib_dn,),) = _mm_tn(h2, du, tm=1024, tn=1024, tt=2048, n_split=N_CHIPS, name="w_up_grad",
                                  exchanges=[_x_pair([g_dn])])
    g_up = halved(d_w_up)
    wire_dn = _pair_sum(g_dn, sib_dn, where, name="pair_sum_w_down")
    dh2, ((recv_dn,), (sib_up,)) = _mm(du, w_up4, tm=1024, tn=1024, tk=2048, out_dtype=BF16, w_layout="snk", name="mlp_up_bwd",
                                       exchanges=[_x_chip([wire_dn], rows=[(0, 704)]), _x_pair([g_up])])
    wire_up = _pair_sum(g_up, sib_up, where, name="pair_sum_w_up")
    dx1, dg_mlp_pre, ((recv_dn,),) = _rms_bwd(dh2, x1, g_mlp_pre, dx2, tm=256, out_dtype=F32, name="mlp_pre_bwd",
                                              exchanges=[_x_chip([wire_dn], rows=[(704, 224)], into=[recv_dn])])
    dmixed, dg_mix_post = _rms_bwd(dx1, mixed, g_mix_post, None, tm=256, out_dtype=BF16, name="mix_post_bwd")
    d_w_out, ((recv_dn,),) = _mm_tn(cat, dmixed, tm=1024, tn=1024, tt=2048, name="w_out_grad",
                                    exchanges=[_x_chip([wire_dn], rows=[(928, 96)], into=[recv_dn])])
    fin_dn = _final_half(g_dn, sib_dn, recv_dn, where, name="final_half_w_down")
    g_out = halved(d_w_out.reshape(N_CHIPS, D_MODEL // N_CHIPS, D_MODEL))
    dcat, ((sib_out,), (oth_dn,)) = _mm(dmixed, w_out, tm=1024, tn=1024, tk=D_MODEL, out_dtype=BF16, w_layout="nk",
                                        name="out_proj_bwd", exchanges=[_x_pair([g_out]), _x_share([fin_dn])])
    wire_out = _pair_sum(g_out, sib_out, where, name="pair_sum_w_out")
    dattn, dg_again = _rms_bwd(dcat, attn, again, None, tm=tm, out_dtype=BF16, name="attn_norm_bwd")
    dq_a, dkv, dsinks, ((recv_out,), (recv_up,)) = _swa_bwd(
        proj, sinks, dattn, lse, name="swa_bwd",
        exchanges=[_x_chip([wire_out]), _x_chip([wire_up], rows=[(0, 320)])])
    dq_r, df_r, di_r, dg_r, dlb, dgain_h, ((recv_up,),) = _hgrn_bwd(
        proj, lb_logits, rgain, o_pre, s0, dcat, tb=512, name="hgrn_bwd",
        exchanges=[_x_chip([wire_up], rows=[(320, 704)], into=[recv_up])])
    fin_up = _final_half(g_up, sib_up, recv_up, where, name="final_half_w_up")
    fin_out = _final_half(g_out, sib_out, recv_out, where, name="final_half_w_out")
    dproj = jnp.concatenate([dq_a, dkv, dq_r, df_r, di_r, dg_r], axis=1)
    piece_cols = D_MODEL // 4

    def w_in_piece(pc, exchanges):
        d, xres = _mm_tn(dproj, h1, tm=896, tn=piece_cols, tt=2048, n_blocks=(2, 2, pc),
                         name="w_in_grad_%d" % pc, exchanges=exchanges)
        return d.reshape(N_CHIPS, shard, 2 * piece_cols), xres

    g_in0, ((oth_up, oth_out),) = w_in_piece(0, [_x_share([fin_up, fin_out])])
    g_in1, ((sib_in0,),) = w_in_piece(1, [_x_pair([g_in0], halves_last=True)])
    wire_in0 = _pair_sum(g_in0, sib_in0, where, name="pair_sum_w_in_0", halves_last=True)
    dh1, ((recv_in0,), (sib_in1,)) = _mm(
        dproj, w_in_t, tm=1024, tn=1024, tk=2688, out_dtype=BF16, m_blocks=(0, 2), name="in_proj_bwd_0",
        exchanges=[_x_chip([wire_in0]), _x_pair([g_in1], halves_last=True)])
    wire_in1 = _pair_sum(g_in1, sib_in1, where, name="pair_sum_w_in_1", halves_last=True)
    dh1, ((recv_in1,),) = _mm(
        dproj, w_in_t, tm=1024, tn=1024, tk=2688, out_dtype=BF16, m_blocks=(2, 2), out_into=dh1,
        name="in_proj_bwd_1", exchanges=[_x_chip([wire_in1])])
    gx, dg_mix_pre = _rms_bwd(dh1, xs, g_mix_pre, dx1, tm=256, out_dtype=F32, name="mix_pre_bwd")
    fin_in0 = _final_half(g_in0, sib_in0, recv_in0, where, name="final_half_w_in_0", halves_last=True)
    fin_in1 = _final_half(g_in1, sib_in1, recv_in1, where, name="final_half_w_in_1", halves_last=True)
    oth_in0, oth_in1 = _run_exchange(_x_share([fin_in0, fin_in1]), name="share_w_in")
    fin_in = jnp.concatenate([fin_in0, fin_in1], axis=1)
    oth_in = jnp.concatenate([oth_in0, oth_in1], axis=1)

    big = [(fin_in, oth_in), (fin_out, oth_out), (fin_up, oth_up), (fin_dn, oth_dn)]
    drgain = jnp.sum(dgain_h, axis=0)
    small = _pack(jnp.sum(dsinks, axis=1)[None, :], dg_again, dlb, jnp.zeros_like(dlb), drgain,
                  [dg_mix_pre, dg_mix_post, dg_mlp_pre, dg_mlp_post], loss=loss_row)
    return gx, big, small


def kernel(x, w_in, attn_sinks, attn_out_gain, rnn_lb_logits, rnn_norm_gain, w_out, mix_pre_gain, mix_post_gain, mlp_pre_gain, mlp_post_gain, w_up, w_down, loss_target, m_w_in, m_attn_sinks, m_attn_out_gain, m_rnn_lb_logits, m_rnn_norm_gain, m_w_out, m_mix_pre_gain, m_mix_post_gain, m_mlp_pre_gain, m_mlp_post_gain, m_w_up, m_w_down, v_w_in, v_attn_sinks, v_attn_out_gain, v_rnn_lb_logits, v_rnn_norm_gain, v_w_out, v_mix_pre_gain, v_mix_post_gain, v_mlp_pre_gain, v_mlp_post_gain, v_w_up, v_w_down):
    ax, ay, ac = _place()
    where = jnp.stack([2 * ax + ay, ac]).astype(jnp.int32)
    t = lambda a: jnp.swapaxes(a, 1, 2)
    big_w = [t(w_in), w_out, w_up, w_down]
    big_m = [t(m_w_in), m_w_out, m_w_up, m_w_down]
    big_v = [t(v_w_in), v_w_out, v_w_up, v_w_down]

    names = ["w_in", "w_out", "w_up", "w_down"]
    bufs = [_cast_slots(w, where, name="cast_" + nm) for w, nm in zip(big_w, names)]
    gx, big_g, small_part = _layer_grads(
        x[0], loss_target[0], bufs, where, attn_sinks, attn_out_gain, rnn_lb_logits, rnn_norm_gain,
        mix_pre_gain, mix_post_gain, mlp_pre_gain, mlp_post_gain)

    grads, deltas, new_m, new_v = [], [], [], []
    for (f, o), w, m, v, nm in zip(big_g, big_w, big_m, big_v, names):
        res = _adamw(w, f, o, m, v, where, name="adamw_" + nm, halves_last=(nm == "w_in"))
        if nm == "w_in":
            res = [t(r) for r in res]
        g, d, nm_, nv_ = res
        grads.append(g)
        deltas.append(d)
        new_m.append(nm_)
        new_v.append(nv_)

    def pack_params(sinks, again, logits, rgain, gains):
        return _pack(sinks, again, logits[0:1], logits[1:2], rgain, gains)

    pw = pack_params(attn_sinks, attn_out_gain, rnn_lb_logits, rnn_norm_gain,
                     [mix_pre_gain, mix_post_gain, mlp_pre_gain, mlp_post_gain])
    pm = pack_params(m_attn_sinks, m_attn_out_gain, m_rnn_lb_logits, m_rnn_norm_gain,
                     [m_mix_pre_gain, m_mix_post_gain, m_mlp_pre_gain, m_mlp_post_gain])
    pv = pack_params(v_attn_sinks, v_attn_out_gain, v_rnn_lb_logits, v_rnn_norm_gain,
                     [v_mix_pre_gain, v_mix_post_gain, v_mlp_pre_gain, v_mlp_post_gain])
    packs = _small_reduce_adamw(small_part, pw, pm, pv, name="small_reduce_adamw")

    def unpack(p):
        seg = lambda o, k: p[:, o:o + k]
        logits = jnp.concatenate([seg(SEG_L0, RNN_W), seg(SEG_L1, RNN_W)], axis=0)
        gains = [seg(SEG_G + i * D_MODEL, D_MODEL) for i in range(4)]
        return dict(sinks=seg(SEG_SINK, N_Q), again=seg(SEG_AGAIN, ATTN_W), logits=logits,
                    rgain=seg(SEG_RGAIN, RNN_HD), gains=gains)

    def order(small, big):
        return [big[0], small["sinks"], small["again"], small["logits"], small["rgain"], big[1],
                *small["gains"], big[2], big[3]]

    loss = packs[0][0, 0]
    outs = [loss, gx[None]]
    for p, b in zip(packs, [grads, deltas, new_m, new_v]):
        outs += order(unpack(p), b)
    return tuple(outs)
```

```python
import functools

import jax
import jax.numpy as jnp
from jax import lax
from jax.experimental import pallas as pl
from jax.experimental.pallas import tpu as pltpu

F32 = jnp.float32
BF16 = jnp.bfloat16
MESH = pl.DeviceIdType.MESH

EPS = 1e-6
D_MODEL = 2048
ATTN_W = 1024
HEAD_DIM = 64
N_Q = 16
N_KV = 2
GROUP = 8
BLK = 128
RNN_W = 1024
RNN_HD = 128
N_RNN = 8
CHUNK = 64
SUB_FWD = 16
SUB_BWD = 8
D_FF = 8192
IN_W = 5376
N_CHIPS = 4
KV_COL = ATTN_W
QR_COL = ATTN_W + 2 * 128
FR_COL = QR_COL + RNN_W
IR_COL = FR_COL + RNN_W
GR_COL = IR_COL + RNN_W

ADAM_LR = 0.001
ADAM_B1 = 0.9
ADAM_B2 = 0.999
ADAM_EPS = 1e-08
ADAM_WD = 0.01
ADAM_STEP = 10

VMEM_LIMIT = 48 * 1024 * 1024

NT = (((1,), (1,)), ((), ()))
TN = (((0,), (0,)), ((), ()))


def _params(sem=None):
    return pltpu.CompilerParams(dimension_semantics=sem, vmem_limit_bytes=VMEM_LIMIT)


def _sigmoid(x):
    return 1.0 / (1.0 + jnp.exp(-x))


ANY = pl.BlockSpec(memory_space=pl.ANY)


def _place():
    return lax.axis_index("x"), lax.axis_index("y"), lax.axis_index("c")


def _other_chips(x, y):
    return [(1 - x, y), (x, 1 - y), (1 - x, 1 - y)]


class _Exchange:
    def __init__(self, srcs, outs, ncopy, build, aliases=None):
        self.srcs, self.outs, self.ncopy, self.build = list(srcs), list(outs), ncopy, build
        self.aliases = aliases or {}


def _remote(src, dst, send_sems, recv_sems, k, to):
    return pltpu.make_async_remote_copy(src_ref=src, dst_ref=dst, send_sem=send_sems.at[k],
                                        recv_sem=recv_sems.at[k], device_id=to, device_id_type=MESH)


def _call(body, *, name, grid, in_specs, out_specs, out_shape, args, scratch_shapes=(), semantics=None,
          exchanges=(), into=None):
    in_specs, out_specs, out_shape = list(in_specs), list(out_specs), list(out_shape)
    scratch_shapes = list(scratch_shapes)
    ni, no, ns = len(in_specs), len(out_specs), len(scratch_shapes)
    xsrc = [s for x in exchanges for s in x.srcs]
    xout = [o for x in exchanges for o in x.outs]
    into = into or {}
    xsrc += [into[k] for k in sorted(into)]
    nxi, nxo = len(xsrc), len(xout)
    aliases = {nxi - len(into) + ni + q: k for q, k in enumerate(sorted(into))}
    a0 = b0 = 0
    for x in exchanges:
        for si, oi in x.aliases.items():
            aliases[ni + a0 + si] = no + b0 + oi
        a0 += len(x.srcs)
        b0 += len(x.outs)
    sems = []
    for x in exchanges:
        sems += [pltpu.SemaphoreType.DMA((x.ncopy,)), pltpu.SemaphoreType.DMA((x.ncopy,))]

    def wrapped(*refs):
        ins, xi = refs[:ni], refs[ni:ni + nxi]
        outs, xo = refs[ni + nxi:ni + nxi + no], refs[ni + nxi + no:ni + nxi + no + nxo]
        rest = refs[ni + nxi + no + nxo:]
        scr, sm = rest[:ns], rest[ns:]

        def copies():
            cps = []
            a = b = 0
            for k, x in enumerate(exchanges):
                cps += x.build(xi[a:a + len(x.srcs)], xo[b:b + len(x.outs)], sm[2 * k], sm[2 * k + 1])
                a += len(x.srcs)
                b += len(x.outs)
            return cps

        def start():
            for cp in copies():
                cp.start()

        def wait():
            for cp in copies():
                cp.wait()

        if not exchanges:
            body(*ins, *outs, *scr)
        elif not grid:
            start()
            body(*ins, *outs, *scr)
            wait()
        else:
            first = last = None
            for ax, g in enumerate(grid):
                f = pl.program_id(ax) == 0
                l = pl.program_id(ax) == g - 1
                first = f if first is None else first & f
                last = l if last is None else last & l
            pl.when(first)(start)
            body(*ins, *outs, *scr)
            pl.when(last)(wait)

    if exchanges and semantics is not None:
        semantics = ("arbitrary",) * len(grid)
    kwargs = dict(grid=grid) if grid else {}
    res = pl.pallas_call(
        wrapped, name=name,
        in_specs=in_specs + [ANY] * nxi, out_specs=out_specs + [ANY] * nxo,
        out_shape=out_shape + xout, scratch_shapes=scratch_shapes + sems,
        input_output_aliases=aliases,
        compiler_params=_params(semantics), **kwargs,
    )(*args, *xsrc)
    res = list(res)
    mine, theirs = res[:no], res[no:]
    per = []
    b = 0
    for x in exchanges:
        per.append(theirs[b:b + len(x.outs)])
        b += len(x.outs)
    return mine, per


def _run_exchange(x, *, name):
    return _call(lambda: None, name=name, grid=(), in_specs=[], out_specs=[], out_shape=[], args=[],
                 exchanges=[x])[1][0]


def _x_gather(bufs, ici=None, d2d=None, cross=None):
    n = len(bufs)
    plan = [(a, kind, rows[a]) for a in range(n) for kind, rows in (("ici", ici), ("d2d", d2d), ("cross", cross))
            if rows is not None and rows[a] is not None]

    def build(srcs, outs, ss, rs):
        x, y, c = _place()
        cps = []
        for q, (a, kind, rows) in enumerate(plan):
            piece = pl.ds(*rows)
            for j, (px, py) in enumerate(_other_chips(x, y)):
                if kind == "d2d":
                    slot, to = 4 * px + 2 * py + c, (x, y, 1 - c)
                else:
                    slot, to = 4 * x + 2 * y + c, (px, py, c if kind == "ici" else 1 - c)
                cps.append(_remote(srcs[a].at[slot, piece], outs[a].at[slot, piece], ss, rs, 3 * q + j, to))
        return cps

    outs = [jax.ShapeDtypeStruct(b.shape, b.dtype) for b in bufs]
    return _Exchange(bufs, outs, 3 * len(plan), build, aliases={a: a for a in range(n)})


def _x_pair(grads, halves_last=False):
    n = len(grads)

    def build(srcs, outs, ss, rs):
        x, y, c = _place()

        def half(r):
            if not halves_last:
                return r.at[:, 1 - c]
            ch = r.shape[2] // 2
            return r.at[:, :, pl.ds(pl.multiple_of((1 - c) * ch, 128), ch)]

        return [_remote(half(srcs[a]), outs[a], ss, rs, a, (x, y, 1 - c)) for a in range(n)]

    if halves_last:
        outs = [jax.ShapeDtypeStruct(g.shape[:2] + (g.shape[2] // 2,), g.dtype) for g in grads]
    else:
        outs = [jax.ShapeDtypeStruct((4,) + g.shape[2:], g.dtype) for g in grads]
    return _Exchange(grads, outs, n, build)


def _x_chip(wires, rows=None, into=None):
    n = len(wires)
    rows = rows or [(0, w.shape[1]) for w in wires]

    def build(srcs, outs, ss, rs):
        x, y, c = _place()
        cps = []
        for a in range(n):
            piece = pl.ds(*rows[a])
            for j, (px, py) in enumerate(_other_chips(x, y)):
                cps.append(_remote(srcs[a].at[2 * px + py, piece], outs[a].at[j, piece], ss, rs,
                                   3 * a + j, (px, py, c)))
        return cps

    outs = [jax.ShapeDtypeStruct((3,) + w.shape[1:], w.dtype) for w in wires]
    if into is None:
        return _Exchange(wires, outs, 3 * n, build)
    return _Exchange(list(wires) + list(into), outs, 3 * n, build, aliases={n + a: a for a in range(n)})


def _x_share(halves):
    n = len(halves)

    def build(srcs, outs, ss, rs):
        x, y, c = _place()
        return [_remote(srcs[a], outs[a], ss, rs, a, (x, y, 1 - c)) for a in range(n)]

    outs = [jax.ShapeDtypeStruct(h.shape, h.dtype) for h in halves]
    return _Exchange(halves, outs, n, build)


def _mm(a, w, *, tm, tn, tk, out_dtype, name, a_square=False, relu=False, mul2=None, w_layout="kn",
        m_blocks=None, out_into=None, exchanges=()):
    m, k = a.shape
    m_first, m_count = m_blocks or (0, m // tm)
    a_spec = pl.BlockSpec((tm, tk), lambda i, j, kk: (i + m_first, kk))
    if w_layout == "kn":
        n = w.shape[1]
        w_spec = pl.BlockSpec((tk, tn), lambda i, j, kk: (kk, j))
    elif w_layout == "nk":
        n = w.shape[0]
        w_spec = pl.BlockSpec((tn, tk), lambda i, j, kk: (j, kk))
    elif w_layout == "skn":
        n = w.shape[0] * w.shape[2]
        per_n = w.shape[2] // tn
        w_spec = pl.BlockSpec((None, tk, tn), lambda i, j, kk: (j // per_n, kk, j % per_n))
    else:
        assert w_layout == "snk"
        n = w.shape[1]
        per_k = w.shape[2] // tk
        w_spec = pl.BlockSpec((None, tn, tk), lambda i, j, kk: (kk // per_k, j, kk % per_k))
    w_dims = NT if w_layout in ("nk", "snk") else (((1,), (0,)), ((), ()))
    nk = k // tk
    assert m % tm == 0 and n % tn == 0 and k % tk == 0

    def body(*refs):
        if mul2 is not None:
            a_ref, w_ref, e_ref, o_ref, acc_ref = refs
        else:
            a_ref, w_ref, o_ref, acc_ref = refs
            e_ref = None
        kk = pl.program_id(2)
        av = a_ref[...]
        if a_square:
            af = av.astype(F32)
            av = (af * af).astype(BF16)
        part = lax.dot_general(av, w_ref[...], w_dims, preferred_element_type=F32)

        def finish(r):
            if relu:
                r = jnp.maximum(r, 0.0)
            if e_ref is not None:
                r = 2.0 * e_ref[...].astype(F32) * r
            o_ref[...] = r.astype(out_dtype)

        if nk == 1:
            finish(part)
        else:
            @pl.when(kk == 0)
            def _():
                acc_ref[...] = part

            @pl.when(kk > 0)
            def _():
                acc_ref[...] += part

            @pl.when(kk == nk - 1)
            def _():
                finish(acc_ref[...])

    in_specs = [a_spec, w_spec]
    args = [a, w]
    if mul2 is not None:
        in_specs.append(pl.BlockSpec((tm, tn), lambda i, j, kk: (i + m_first, j)))
        args.append(mul2)
    acc_shape = (tm, tn) if nk > 1 else (8, 128)
    (out,), per = _call(
        body, name=name, grid=(m_count, n // tn, nk),
        in_specs=in_specs, out_specs=[pl.BlockSpec((tm, tn), lambda i, j, kk: (i + m_first, j))],
        out_shape=[jax.ShapeDtypeStruct((m, n), out_dtype)], args=args,
        scratch_shapes=[pltpu.VMEM(acc_shape, F32)],
        semantics=("parallel", "parallel", "arbitrary"), exchanges=exchanges,
        into=None if out_into is None else {0: out_into})
    return (out, per) if exchanges else out


def _mm_tn(a, b, *, tm, tn, tt, name, a_square=False, n_split=1, b_blocks=None, exchanges=()):
    t, m = a.shape
    nb = len(b_blocks) if b_blocks else 1
    n = tn if b_blocks else b.shape[1]
    assert t % tt == 0 and m % tm == 0 and n % tn == 0 and (n // n_split) % tn == 0
    per = n // n_split // tn

    def body(a_ref, *refs):
        b_refs, o_ref = refs[:nb], refs[nb]
        ti = pl.program_id(2)
        av = a_ref[...]
        if a_square:
            af = av.astype(F32)
            av = (af * af).astype(BF16)
        bv = b_refs[0][...] if nb == 1 else jnp.concatenate([r[...] for r in b_refs], axis=1)
        part = lax.dot_general(av, bv, TN, preferred_element_type=F32)

        @pl.when(ti == 0)
        def _():
            o_ref[...] = part

        @pl.when(ti > 0)
        def _():
            o_ref[...] += part

    if b_blocks:
        b_specs = [pl.BlockSpec((tt, tn // nb), functools.partial(lambda blk, i, j, ti: (ti, blk), blk))
                   for blk in b_blocks]
    else:
        b_specs = [pl.BlockSpec((tt, tn), lambda i, j, ti: (ti, j))]
    (out,), xres = _call(
        body, name=name, grid=(m // tm, n // tn, t // tt),
        in_specs=[pl.BlockSpec((tt, tm), lambda i, j, ti: (ti, i))] + b_specs,
        out_specs=[pl.BlockSpec((None, tm, tn), lambda i, j, ti: (j // per, i, j % per))],
        out_shape=[jax.ShapeDtypeStruct((n_split, m, n // n_split), F32)], args=[a] + [b] * nb,
        semantics=("parallel", "parallel", "arbitrary"), exchanges=exchanges)
    return (out, xres) if exchanges else out


def _rstd(x):
    return lax.rsqrt(jnp.mean(x * x, axis=-1, keepdims=True) + EPS)


def _rms_cast_gather(x, g, buf, *, tm, name):
    t, d = x.shape
    steps = t // tm

    def body(x_ref, g_ref, b_in, o_ref, b_out, send_sems, recv_sems):
        i = pl.program_id(0)
        xc, yc, c = _place()
        chips = _other_chips(xc, yc)

        def slot(px, py, pc):
            return b_out.at[4 * px + 2 * py + pc]

        def sent(j):
            return _remote(b_in.at[4 * xc + 2 * yc + c], slot(xc, yc, c), send_sems, recv_sems, j, (*chips[j], c))

        def passed(j):
            return _remote(slot(*chips[j], c), slot(*chips[j], c), send_sems, recv_sems, 3 + j, (xc, yc, 1 - c))

        @pl.when(i == 0)
        def _():
            for j in range(3):
                sent(j).start()

        xv = x_ref[...]
        o_ref[...] = (xv * _rstd(xv) * g_ref[...]).astype(BF16)

        @pl.when(i == steps - 1)
        def _():
            for j in range(3):
                sent(j).wait_recv()
                passed(j).start()
            for j in range(3):
                passed(j).wait_recv()
                passed(j).wait_send()
                sent(j).wait_send()

    return pl.pallas_call(
        body, name=name, grid=(steps,),
        in_specs=[pl.BlockSpec((tm, d), lambda i: (i, 0)), pl.BlockSpec((1, d), lambda i: (0, 0)), ANY],
        out_specs=[pl.BlockSpec((tm, d), lambda i: (i, 0)), ANY],
        out_shape=[jax.ShapeDtypeStruct((t, d), BF16), jax.ShapeDtypeStruct(buf.shape, buf.dtype)],
        scratch_shapes=[pltpu.SemaphoreType.DMA((6,)), pltpu.SemaphoreType.DMA((6,))],
        input_output_aliases={2: 1},
        compiler_params=_params(("arbitrary",)),
    )(x, g, buf)


def _mix_cat(attn, rnn, gain, *, tm, name):
    t = attn.shape[0]

    def body(a_ref, r_ref, g_ref, o_ref):
        av = a_ref[...]
        o_ref[:, :ATTN_W] = (av * _rstd(av) * g_ref[...]).astype(BF16)
        o_ref[:, ATTN_W:] = r_ref[...].astype(BF16)

    return pl.pallas_call(
        body, name=name, grid=(t // tm,),
        in_specs=[pl.BlockSpec((tm, ATTN_W), lambda i: (i, 0)), pl.BlockSpec((tm, RNN_W), lambda i: (i, 0)),
                  pl.BlockSpec((1, ATTN_W), lambda i: (0, 0))],
        out_specs=pl.BlockSpec((tm, D_MODEL), lambda i: (i, 0)),
        out_shape=jax.ShapeDtypeStruct((t, D_MODEL), BF16),
        compiler_params=_params(("parallel",)),
    )(attn, rnn, gain)


def _post_norm_res(mixed, g_post, res, g_next, *, tm, name, exchanges=()):
    t, d = mixed.shape

    def body(m_ref, gp_ref, r_ref, gn_ref, x1_ref, h2_ref):
        mv = m_ref[...].astype(F32)
        x1 = r_ref[...] + mv * _rstd(mv) * gp_ref[...]
        x1_ref[...] = x1
        h2_ref[...] = (x1 * _rstd(x1) * gn_ref[...]).astype(BF16)

    row = pl.BlockSpec((tm, d), lambda i: (i, 0))
    vec = pl.BlockSpec((1, d), lambda i: (0, 0))
    res_, xres = _call(
        body, name=name, grid=(t // tm,),
        in_specs=[row, vec, row, vec], out_specs=[row, row],
        out_shape=[jax.ShapeDtypeStruct((t, d), F32), jax.ShapeDtypeStruct((t, d), BF16)],
        args=[mixed, g_post, res, g_next], semantics=("parallel",), exchanges=exchanges)
    return (*res_, xres) if exchanges else res_


def _rms_bwd(dyn, xin, g, res, *, tm, out_dtype, name, col_block=0, exchanges=()):
    t, d = xin.shape

    def body(*refs):
        if res is not None:
            dy_ref, x_ref, g_ref, r_ref, dx_ref, dg_ref = refs
        else:
            dy_ref, x_ref, g_ref, dx_ref, dg_ref = refs
        i = pl.program_id(0)
        xv = x_ref[...].astype(F32)
        dy = dy_ref[...].astype(F32)
        r = _rstd(xv)
        xh = xv * r
        part = jnp.sum(dy * xh, axis=0, keepdims=True)

        @pl.when(i == 0)
        def _():
            dg_ref[...] = part

        @pl.when(i > 0)
        def _():
            dg_ref[...] += part

        tt = dy * g_ref[...]
        dx = r * (tt - xh * jnp.mean(tt * xh, axis=-1, keepdims=True))
        if res is not None:
            dx = dx + r_ref[...]
        dx_ref[...] = dx.astype(out_dtype)

    row = pl.BlockSpec((tm, d), lambda i: (i, 0))
    vec = pl.BlockSpec((1, d), lambda i: (0, 0))
    in_specs = [pl.BlockSpec((tm, d), lambda i: (i, col_block)), row, vec]
    args = [dyn, xin, g]
    if res is not None:
        in_specs.append(row)
        args.append(res)
    res, xres = _call(
        body, name=name, grid=(t // tm,),
        in_specs=in_specs, out_specs=[row, vec],
        out_shape=[jax.ShapeDtypeStruct((t, d), out_dtype), jax.ShapeDtypeStruct((1, d), F32)], args=args,
        semantics=("arbitrary",), exchanges=exchanges)
    return (*res, xres) if exchanges else res


def _loss_head(y, g_post, x1, target, *, tm, name):
    t, d = y.shape

    def body(y_ref, g_ref, x1_ref, t_ref, dy_ref, dx2_ref, loss_ref, dg_ref):
        i = pl.program_id(0)
        yv = y_ref[...].astype(F32)
        r = _rstd(yv)
        yh = yv * r
        gv = g_ref[...]
        err = x1_ref[...] + yh * gv - t_ref[...]
        lpart = 0.5 * jnp.sum(jnp.mean(err * err, axis=-1, keepdims=True), axis=0, keepdims=True)
        dx2 = err * (1.0 / d)
        dgp = jnp.sum(dx2 * yh, axis=0, keepdims=True)
        lane = lax.broadcasted_iota(jnp.int32, (1, 128), 1)
        lrow = jnp.where(lane == 0, lpart, 0.0)

        @pl.when(i == 0)
        def _():
            dg_ref[...] = dgp
            loss_ref[...] = lrow

        @pl.when(i > 0)
        def _():
            dg_ref[...] += dgp
            loss_ref[...] += lrow

        tt = dx2 * gv
        dy_ref[...] = (r * (tt - yh * jnp.mean(tt * yh, axis=-1, keepdims=True))).astype(BF16)
        dx2_ref[...] = dx2

    row = pl.BlockSpec((tm, d), lambda i: (i, 0))
    vec = pl.BlockSpec((1, d), lambda i: (0, 0))
    return pl.pallas_call(
        body, name=name, grid=(t // tm,),
        in_specs=[row, vec, row, row],
        out_specs=[row, row, pl.BlockSpec((1, 128), lambda i: (0, 0)), vec],
        out_shape=[jax.ShapeDtypeStruct((t, d), BF16), jax.ShapeDtypeStruct((t, d), F32),
                   jax.ShapeDtypeStruct((1, 128), F32), jax.ShapeDtypeStruct((1, d), F32)],
        compiler_params=_params(("arbitrary",)),
    )(y, g_post, x1, target)


def _alibi_slope(h):
    return 2.0 ** (-8.0 * (h + 1) / N_Q)


PAIR = 2 * HEAD_DIM
N_PAIRS = N_Q // 2
PAIRS_PER_KV = GROUP // 2
SMEM = pl.BlockSpec(memory_space=pltpu.SMEM)


def _swa_mask(n):
    key = lax.broadcasted_iota(jnp.int32, (2 * BLK, BLK), 0)
    qry = lax.broadcasted_iota(jnp.int32, (2 * BLK, BLK), 1)
    dist = qry + BLK - key
    valid = (dist >= 0) & (dist < BLK) & ((key >= BLK) | (n > 0))
    return valid, dist.astype(F32)


def _block_diag(kvp_ref, kvc_ref, off):
    a = jnp.concatenate([kvp_ref[:, off:off + HEAD_DIM], kvc_ref[:, off:off + HEAD_DIM]], axis=0).astype(BF16)
    z = jnp.zeros_like(a)
    return jnp.concatenate([jnp.concatenate([a, z], axis=1), jnp.concatenate([z, a], axis=1)], axis=0)


def _swa_scores(s2, e, hh, valid, distf):
    s = s2[2 * BLK * e:2 * BLK * (e + 1)] * (HEAD_DIM ** -0.5) - _alibi_slope(hh) * distf
    return jnp.where(valid, s, -1e30)


def _swa_fwd(proj, sinks, *, name, exchanges=()):
    t = proj.shape[0]
    nb = t // BLK
    kvb = KV_COL // (2 * 128)

    def body(sink_ref, q_ref, kvc_ref, kvp_ref, o_ref, lse_ref):
        n = pl.program_id(0)
        valid, distf = _swa_mask(n)
        for kvh in range(N_KV):
            k2 = _block_diag(kvp_ref, kvc_ref, kvh * HEAD_DIM)
            v2 = _block_diag(kvp_ref, kvc_ref, 128 + kvh * HEAD_DIM)
            for jp in range(PAIRS_PER_KV):
                pair = kvh * PAIRS_PER_KV + jp
                lanes = slice(pair * PAIR, (pair + 1) * PAIR)
                s2 = lax.dot_general(k2, q_ref[:, lanes].astype(BF16), NT, preferred_element_type=F32)
                probs = []
                for e in range(2):
                    hh = 2 * pair + e
                    s = _swa_scores(s2, e, hh, valid, distf)
                    sink = sink_ref[0, hh]
                    mx = jnp.maximum(jnp.max(s, axis=0, keepdims=True), sink)
                    p = jnp.exp(s - mx)
                    l = jnp.sum(p, axis=0, keepdims=True) + jnp.exp(sink - mx)
                    probs.append((p * (1.0 / l)).astype(BF16))
                    lse_ref[hh:hh + 1, :] = mx + jnp.log(l)
                o_ref[:, lanes] = lax.dot_general(jnp.concatenate(probs, axis=0), v2, TN,
                                                  preferred_element_type=F32)

    res, xres = _call(
        body, name=name, grid=(nb,),
        in_specs=[SMEM,
                  pl.BlockSpec((BLK, ATTN_W), lambda n: (n, 0)),
                  pl.BlockSpec((BLK, 256), lambda n: (n, kvb)),
                  pl.BlockSpec((BLK, 256), lambda n: (jnp.maximum(n - 1, 0), kvb))],
        out_specs=[pl.BlockSpec((BLK, ATTN_W), lambda n: (n, 0)),
                   pl.BlockSpec((None, N_Q, BLK), lambda n: (n, 0, 0))],
        out_shape=[jax.ShapeDtypeStruct((t, ATTN_W), F32), jax.ShapeDtypeStruct((nb, N_Q, BLK), F32)],
        args=[sinks, proj, proj, proj], semantics=("parallel",), exchanges=exchanges)
    return (*res, xres) if exchanges else res


def _swa_bwd(proj, sinks, dattn, lse, *, name, exchanges=()):
    t = proj.shape[0]
    nb = t // BLK
    kvb = KV_COL // (2 * 128)

    def body(sink_ref, q_ref, kvc_ref, kvp_ref, do_ref, lse_ref, dq_ref, dkv_ref, dsink_ref, carry_ref):
        n = pl.program_id(0)

        @pl.when(n == 0)
        def _():
            dsink_ref[...] = jnp.zeros_like(dsink_ref)
            carry_ref[...] = jnp.zeros_like(carry_ref)

        @pl.when(n < nb)
        def _():
            valid, distf = _swa_mask(n)
            for kvh in range(N_KV):
                k2 = _block_diag(kvp_ref, kvc_ref, kvh * HEAD_DIM)
                v2 = _block_diag(kvp_ref, kvc_ref, 128 + kvh * HEAD_DIM)
                dk2 = jnp.zeros((4 * BLK, PAIR), F32)
                dv2 = jnp.zeros((4 * BLK, PAIR), F32)
                for jp in range(PAIRS_PER_KV):
                    pair = kvh * PAIRS_PER_KV + jp
                    lanes = slice(pair * PAIR, (pair + 1) * PAIR)
                    q2 = q_ref[:, lanes].astype(BF16)
                    do2 = do_ref[:, lanes].astype(BF16)
                    s2 = lax.dot_general(k2, q2, NT, preferred_element_type=F32)
                    dp2 = lax.dot_general(v2, do2, NT, preferred_element_type=F32)
                    probs, dss = [], []
                    for e in range(2):
                        hh = 2 * pair + e
                        lse_h = lse_ref[hh:hh + 1, :]
                        p = jnp.exp(_swa_scores(s2, e, hh, valid, distf) - lse_h)
                        dp = dp2[2 * BLK * e:2 * BLK * (e + 1)]
                        delta = jnp.sum(p * dp, axis=0, keepdims=True)
                        dsink_ref[hh:hh + 1, :] += -jnp.exp(sink_ref[0, hh] - lse_h) * delta
                        probs.append(p.astype(BF16))
                        dss.append((p * (dp - delta)).astype(BF16))
                    ds2 = jnp.concatenate(dss, axis=0)
                    dq_ref[:, lanes] = (lax.dot_general(ds2, k2, TN, preferred_element_type=F32)
                                        * (HEAD_DIM ** -0.5)).astype(BF16)
                    dk2 = dk2 + jnp.dot(ds2, q2, preferred_element_type=F32)
                    dv2 = dv2 + jnp.dot(jnp.concatenate(probs, axis=0), do2, preferred_element_type=F32)
                dk_cat = (dk2[:2 * BLK, :HEAD_DIM] + dk2[2 * BLK:, HEAD_DIM:]) * (HEAD_DIM ** -0.5)
                dv_cat = dv2[:2 * BLK, :HEAD_DIM] + dv2[2 * BLK:, HEAD_DIM:]
                ko = kvh * HEAD_DIM
                vo = 128 + kvh * HEAD_DIM
                dkv_ref[:, ko:ko + HEAD_DIM] = (carry_ref[:, ko:ko + HEAD_DIM] + dk_cat[:BLK]).astype(BF16)
                dkv_ref[:, vo:vo + HEAD_DIM] = (carry_ref[:, vo:vo + HEAD_DIM] + dv_cat[:BLK]).astype(BF16)
                carry_ref[:, ko:ko + HEAD_DIM] = dk_cat[BLK:]
                carry_ref[:, vo:vo + HEAD_DIM] = dv_cat[BLK:]

        @pl.when(n == nb)
        def _():
            dkv_ref[...] = carry_ref[...].astype(BF16)

    last = nb - 1
    res, xres = _call(
        body, name=name, grid=(nb + 1,),
        in_specs=[SMEM,
                  pl.BlockSpec((BLK, ATTN_W), lambda n: (jnp.minimum(n, last), 0)),
                  pl.BlockSpec((BLK, 256), lambda n: (jnp.minimum(n, last), kvb)),
                  pl.BlockSpec((BLK, 256), lambda n: (jnp.maximum(jnp.minimum(n, last) - 1, 0), kvb)),
                  pl.BlockSpec((BLK, ATTN_W), lambda n: (jnp.minimum(n, last), 0)),
                  pl.BlockSpec((None, N_Q, BLK), lambda n: (jnp.minimum(n, last), 0, 0))],
        out_specs=[pl.BlockSpec((BLK, ATTN_W), lambda n: (jnp.minimum(n, last), 0)),
                   pl.BlockSpec((BLK, 256), lambda n: (jnp.maximum(n - 1, 0), 0)),
                   pl.BlockSpec((N_Q, BLK), lambda n: (0, 0))],
        out_shape=[jax.ShapeDtypeStruct((t, ATTN_W), BF16), jax.ShapeDtypeStruct((t, 256), BF16),
                   jax.ShapeDtypeStruct((N_Q, BLK), F32)],
        scratch_shapes=[pltpu.VMEM((BLK, 256), F32)],
        args=[sinks, proj, proj, proj, dattn, lse], semantics=("arbitrary",), exchanges=exchanges)
    return (*res, xres) if exchanges else res


def _cumsum_rows(x):
    n = x.shape[0]
    row = lax.broadcasted_iota(jnp.int32, x.shape, 0)
    s = 1
    while s < n:
        x = x + jnp.where(row >= s, pltpu.roll(x, s, axis=0), 0.0)
        s *= 2
    return x


def _rev_cumsum_rows(x):
    n = x.shape[0]
    row = lax.broadcasted_iota(jnp.int32, x.shape, 0)
    s = 1
    while s < n:
        x = x + jnp.where(row < n - s, pltpu.roll(x, n - s, axis=0), 0.0)
        s *= 2
    return x


def _lower_bound(lbl_ref):
    l0 = lbl_ref[0:1, :]
    l1 = lbl_ref[1:2, :]
    mx = jnp.maximum(l0, l1)
    e0 = jnp.exp(l0 - mx)
    e1 = jnp.exp(l1 - mx)
    return e0 / (e0 + e1)


def _hgrn_gates(z, lb):
    sg = _sigmoid(z)
    f = lb + (1.0 - lb) * sg
    return sg, f, jnp.log(f), 1.0 - f


def _sub_factors(b, k, i, sub, trim):
    need = -(-sub * i // 16) * 16 if trim else CHUNK
    rows = lax.broadcasted_iota(jnp.int32, (need, RNN_HD), 0)
    ref = b[sub * i - 1:sub * i, :]
    qfac = jnp.exp(b[sub * i:sub * (i + 1), :] - ref)
    kfac = jnp.where(rows < sub * i, jnp.exp(ref - b[:need]), 0.0)
    kt = (k[:need] * kfac).astype(BF16)
    if need < CHUNK:
        kt = jnp.concatenate([kt, jnp.zeros((CHUNK - need, RNN_HD), BF16)], axis=0)
    return qfac, kfac, kt


def _diag_decay(bi, s):
    trow = lax.broadcasted_iota(jnp.int32, bi.shape, 0)
    return jnp.where(trow >= s, jnp.exp(bi - bi[s:s + 1, :]), 0.0)


def _hgrn_fwd(proj, lb_logits, norm_gain, *, tb, name, exchanges=()):
    t = proj.shape[0]
    ntb = t // tb
    nch = tb // CHUNK
    qb, fb, ib, gb = QR_COL // 128, FR_COL // 128, IR_COL // 128, GR_COL // 128

    def body(q_ref, f_ref, i_ref, g_ref, lbl_ref, gain_ref, o_ref, out_ref, s0_ref, st_ref):
        c = pl.program_id(1)

        @pl.when(c == 0)
        def _():
            st_ref[...] = jnp.zeros_like(st_ref)

        lb = _lower_bound(lbl_ref)
        gain = gain_ref[...]

        def chunk(ci, st):
            rows = slice(ci * CHUNK, (ci + 1) * CHUNK)
            _, _, lf, k = _hgrn_gates(f_ref[rows, :], lb)
            qr = q_ref[rows, :]
            q = qr * _sigmoid(qr)
            v = i_ref[rows, :]
            b = _cumsum_rows(lf)
            s0_ref[ci] = st
            o_inter = lax.dot_general((q * jnp.exp(b)).astype(BF16), st.astype(BF16), NT,
                                      preferred_element_type=F32)
            vb = v.astype(BF16)
            blast = b[CHUNK - 1:CHUNK, :]
            khat = (k * jnp.exp(blast - b)).astype(BF16)
            st = st * jnp.exp(blast) + lax.dot_general(vb, khat, TN, preferred_element_type=F32)
            blocks = []
            for i in range(CHUNK // SUB_FWD):
                blk = slice(SUB_FWD * i, SUB_FWD * (i + 1))
                qi, ki, vi, bi = q[blk], k[blk], v[blk], b[blk]
                oi = o_inter[blk]
                if i > 0:
                    qfac, _, kt = _sub_factors(b, k, i, SUB_FWD, trim=True)
                    att = lax.dot_general((qi * qfac).astype(BF16), kt, NT,
                                          preferred_element_type=F32)
                    oi = oi + jnp.dot(att.astype(BF16), vb, preferred_element_type=F32)
                for s in range(SUB_FWD):
                    qe = qi * _diag_decay(bi, s)
                    a = jnp.sum(qe * ki[s:s + 1, :], axis=1, keepdims=True)
                    oi = oi + a * vi[s:s + 1, :]
                blocks.append(oi)
            o = jnp.concatenate(blocks, axis=0)
            o_ref[rows, :] = o
            gr = g_ref[rows, :]
            out_ref[rows, :] = o * _rstd(o) * gain * (gr * _sigmoid(gr))
            return st

        st = st_ref[...]
        for ci in range(nch):
            st = chunk(ci, st)
        st_ref[...] = st

    def col(base):
        return pl.BlockSpec((tb, RNN_HD), lambda h, c: (c, base + h))

    res, xres = _call(
        body, name=name, grid=(N_RNN, ntb),
        in_specs=[col(qb), col(fb), col(ib), col(gb),
                  pl.BlockSpec((2, RNN_HD), lambda h, c: (0, h)), pl.BlockSpec((1, RNN_HD), lambda h, c: (0, 0))],
        out_specs=[pl.BlockSpec((tb, RNN_HD), lambda h, c: (c, h)), pl.BlockSpec((tb, RNN_HD), lambda h, c: (c, h)),
                   pl.BlockSpec((None, nch, RNN_HD, RNN_HD), lambda h, c: (h, c, 0, 0))],
        out_shape=[jax.ShapeDtypeStruct((t, RNN_W), F32), jax.ShapeDtypeStruct((t, RNN_W), F32),
                   jax.ShapeDtypeStruct((N_RNN, t // CHUNK, RNN_HD, RNN_HD), F32)],
        scratch_shapes=[pltpu.VMEM((RNN_HD, RNN_HD), F32)],
        args=[proj, proj, proj, proj, lb_logits, norm_gain],
        semantics=("parallel", "arbitrary"), exchanges=exchanges)
    return (*res, xres) if exchanges else res


def _hgrn_bwd(proj, lb_logits, norm_gain, o_pre, s0, dcat, *, tb, name, exchanges=()):
    t = proj.shape[0]
    ntb = t // tb
    nch = tb // CHUNK
    qb, fb, ib, gb = QR_COL // 128, FR_COL // 128, IR_COL // 128, GR_COL // 128
    sub = SUB_BWD
    nsub = CHUNK // sub

    def body(q_ref, f_ref, i_ref, g_ref, lbl_ref, gain_ref, o_ref, s0_ref, dout_ref,
             dq_ref, df_ref, di_ref, dg_ref, dlb_ref, dgain_ref,
             dst_ref, dqs_ref, dks_ref, dvs_ref):
        c = pl.program_id(1)

        @pl.when(c == 0)
        def _():
            dst_ref[...] = jnp.zeros_like(dst_ref)
            dlb_ref[...] = jnp.zeros_like(dlb_ref)
            dgain_ref[...] = jnp.zeros_like(dgain_ref)

        lb = _lower_bound(lbl_ref)
        gain = gain_ref[...]

        def chunk(ci, dst):
            rows = slice(ci * CHUNK, (ci + 1) * CHUNK)
            dqa_ref, dka_ref, dva_ref = dqs_ref.at[ci], dks_ref.at[ci], dvs_ref.at[ci]
            sg, f, lf, k = _hgrn_gates(f_ref[rows, :], lb)
            qr = q_ref[rows, :]
            sq = _sigmoid(qr)
            q = qr * sq
            v = i_ref[rows, :]
            b = _cumsum_rows(lf)

            dout = dout_ref[rows, :].astype(F32)
            o = o_ref[rows, :]
            gr = g_ref[rows, :]
            sgg = _sigmoid(gr)
            gate = gr * sgg
            rs = _rstd(o)
            nrm = o * rs
            dg_ref[rows, :] = (dout * nrm * gain * (sgg * (1.0 + gr * (1.0 - sgg)))).astype(BF16)
            dn = dout * gate
            dgain_ref[...] += jnp.sum(dn * nrm, axis=0, keepdims=True)
            tt = dn * gain
            do = rs * (tt - nrm * jnp.mean(tt * nrm, axis=-1, keepdims=True))

            dob = do.astype(BF16)
            vb = v.astype(BF16)
            eb = jnp.exp(b)
            blast = b[CHUNK - 1:CHUNK, :]
            ebl = jnp.exp(blast - b)
            dstb = dst.astype(BF16)
            khat = (k * ebl).astype(BF16)
            s0 = s0_ref[ci]
            dqa_ref[...] = eb * jnp.dot(dob, s0.astype(BF16), preferred_element_type=F32)
            dk_state = ebl * jnp.dot(vb, dstb, preferred_element_type=F32)
            dka_ref[...] = dk_state
            d_blast = (jnp.sum(k * dk_state, axis=0, keepdims=True)
                       + jnp.exp(blast) * jnp.sum(dst * s0, axis=0, keepdims=True))
            dva_ref[...] = lax.dot_general(khat, dstb, NT, preferred_element_type=F32)
            dst_next = dst * jnp.exp(blast) + lax.dot_general(dob, (q * eb).astype(BF16), TN,
                                                              preferred_element_type=F32)
            pm = lax.dot_general(dob, vb, NT, preferred_element_type=F32)
            for i in range(nsub):
                blk = slice(sub * i, sub * (i + 1))
                qi, ki, vi, bi, doi = q[blk], k[blk], v[blk], b[blk], do[blk]
                dqi = dqa_ref[blk, :]
                if i > 0:
                    qfac, kfac, kt = _sub_factors(b, k, i, sub, trim=False)
                    qt = (qi * qfac).astype(BF16)
                    att = lax.dot_general(qt, kt, NT, preferred_element_type=F32).astype(BF16)
                    pmi = pm[blk, :].astype(BF16)
                    dva_ref[...] += lax.dot_general(att, doi.astype(BF16), TN, preferred_element_type=F32)
                    dqi = dqi + qfac * jnp.dot(pmi, kt, preferred_element_type=F32)
                    dka_ref[...] += kfac * lax.dot_general(pmi, qt, TN, preferred_element_type=F32)
                dqa_ref[blk, :] = dqi
                srow = lax.broadcasted_iota(jnp.int32, (sub, RNN_HD), 0)
                dki = jnp.zeros((sub, RNN_HD), F32)
                dvi = jnp.zeros((sub, RNN_HD), F32)
                for tq in range(sub):
                    qt, dot_ = qi[tq:tq + 1, :], doi[tq:tq + 1, :]
                    e = jnp.where(srow <= tq, jnp.exp(bi[tq:tq + 1, :] - bi), 0.0)
                    ke = ki * e
                    p = jnp.sum(vi * dot_, axis=1, keepdims=True)
                    a = jnp.sum(ke * qt, axis=1, keepdims=True)
                    dki = dki + p * (qt * e)
                    dvi = dvi + a * dot_
                    row = slice(sub * i + tq, sub * i + tq + 1)
                    dqa_ref[row, :] += jnp.sum(p * ke, axis=0, keepdims=True)
                dka_ref[blk, :] += dki
                dva_ref[blk, :] += dvi

            dq = dqa_ref[...]
            dk = dka_ref[...]
            lastrow = lax.broadcasted_iota(jnp.int32, (CHUNK, RNN_HD), 0) == CHUNK - 1
            dlf = _rev_cumsum_rows(q * dq - k * dk + jnp.where(lastrow, d_blast, 0.0))
            dff = dlf / f - dk
            df_ref[rows, :] = (dff * (1.0 - lb) * sg * (1.0 - sg)).astype(BF16)
            dlb_ref[...] += jnp.sum(dff * (1.0 - sg), axis=0, keepdims=True)
            dq_ref[rows, :] = (dq * (sq * (1.0 + qr * (1.0 - sq)))).astype(BF16)
            di_ref[rows, :] = dva_ref[...].astype(BF16)
            return dst_next

        dst = dst_ref[...]
        for ci in reversed(range(nch)):
            dst = chunk(ci, dst)
        dst_ref[...] = dst

    def col(base):
        return pl.BlockSpec((tb, RNN_HD), lambda h, c: (ntb - 1 - c, base + h))

    outc = pl.BlockSpec((tb, RNN_HD), lambda h, c: (ntb - 1 - c, h))
    hb = ATTN_W // RNN_HD
    res, xres = _call(
        body, name=name, grid=(N_RNN, ntb),
        in_specs=[col(qb), col(fb), col(ib), col(gb),
                  pl.BlockSpec((2, RNN_HD), lambda h, c: (0, h)), pl.BlockSpec((1, RNN_HD), lambda h, c: (0, 0)),
                  outc,
                  pl.BlockSpec((None, nch, RNN_HD, RNN_HD), lambda h, c: (h, ntb - 1 - c, 0, 0)),
                  pl.BlockSpec((tb, RNN_HD), lambda h, c: (ntb - 1 - c, hb + h))],
        out_specs=[outc, outc, outc, outc,
                   pl.BlockSpec((1, RNN_HD), lambda h, c: (0, h)),
                   pl.BlockSpec((None, 1, RNN_HD), lambda h, c: (h, 0, 0))],
        out_shape=[jax.ShapeDtypeStruct((t, RNN_W), BF16)] * 4
        + [jax.ShapeDtypeStruct((1, RNN_W), F32), jax.ShapeDtypeStruct((N_RNN, 1, RNN_HD), F32)],
        scratch_shapes=[pltpu.VMEM((RNN_HD, RNN_HD), F32),
                        pltpu.VMEM((nch, CHUNK, RNN_HD), F32), pltpu.VMEM((nch, CHUNK, RNN_HD), F32),
                        pltpu.VMEM((nch, CHUNK, RNN_HD), F32)],
        args=[proj, proj, proj, proj, lb_logits, norm_gain, o_pre, s0, dcat],
        semantics=("parallel", "arbitrary"), exchanges=exchanges)
    return (*res, xres) if exchanges else res


def _cast_slots(w, where, *, name):
    _, rows, cols = w.shape
    rh = rows // 2
    tr = _row_tile(rh, cols)
    nh = rh // tr

    def body(wh_ref, w_ref, o_ref):
        o_ref[...] = w_ref[...].astype(BF16)

    return pl.pallas_call(
        body, name=name,
        grid_spec=pltpu.PrefetchScalarGridSpec(
            num_scalar_prefetch=1, grid=(2, nh),
            in_specs=[pl.BlockSpec((None, tr, cols), lambda h, i, wh: (0, h * nh + i, 0))],
            out_specs=pl.BlockSpec((None, tr, cols), lambda h, i, wh: (2 * wh[0] + h, i, 0))),
        out_shape=jax.ShapeDtypeStruct((8, rh, cols), BF16),
        compiler_params=_params(("parallel", "parallel")),
    )(where, w)


def _row_tile(rows, cols, budget=1 << 20):
    tr = rows
    while tr * cols > budget and tr % 16 == 0:
        tr //= 2
    return tr


def _half_spec(g, tr, halves_last, slab):
    if halves_last:
        return pl.BlockSpec((None, tr, g.shape[2] // 2), lambda *a: (slab(*a), a[-2], a[-1][1]))
    return pl.BlockSpec((None, None, tr, g.shape[3]), lambda *a: (slab(*a), a[-1][1], a[-2], 0))


def _pair_sum(g, sib, where, *, name, halves_last=False):
    rh, cols = sib.shape[1:]
    tr = _row_tile(rh, cols)

    def body(w_ref, g_ref, s_ref, o_ref):
        o_ref[...] = (g_ref[...] + s_ref[...]).astype(BF16)

    return pl.pallas_call(
        body, name=name,
        grid_spec=pltpu.PrefetchScalarGridSpec(
            num_scalar_prefetch=1, grid=(4, rh // tr),
            in_specs=[_half_spec(g, tr, halves_last, lambda s, i, w: s),
                      pl.BlockSpec((None, tr, cols), lambda s, i, w: (s, i, 0))],
            out_specs=pl.BlockSpec((None, tr, cols), lambda s, i, w: (s, i, 0))),
        out_shape=jax.ShapeDtypeStruct((4, rh, cols), BF16),
        compiler_params=_params(("parallel", "parallel")),
    )(where, g, sib)


def _final_half(g, sib, recv, where, *, name, halves_last=False):
    rh, cols = sib.shape[1:]
    tr = _row_tile(rh, cols)

    def body(w_ref, g_ref, s_ref, r_ref, o_ref):
        acc = g_ref[...] + s_ref[...]
        for j in range(3):
            acc = acc + r_ref[j].astype(F32)
        o_ref[...] = acc

    return pl.pallas_call(
        body, name=name,
        grid_spec=pltpu.PrefetchScalarGridSpec(
            num_scalar_prefetch=1, grid=(rh // tr,),
            in_specs=[_half_spec(g, tr, halves_last, lambda i, w: w[0]),
                      pl.BlockSpec((None, tr, cols), lambda i, w: (w[0], i, 0)),
                      pl.BlockSpec((3, tr, cols), lambda i, w: (0, i, 0))],
            out_specs=pl.BlockSpec((tr, cols), lambda i, w: (i, 0))),
        out_shape=jax.ShapeDtypeStruct((rh, cols), F32),
        compiler_params=_params(("parallel",)),
    )(where, g, sib, recv)


def _adamw_math(w, g, m, v):
    m = ADAM_B1 * m + (1.0 - ADAM_B1) * g
    v = ADAM_B2 * v + (1.0 - ADAM_B2) * (g * g)
    m_hat = m / (1.0 - ADAM_B1 ** ADAM_STEP)
    v_hat = v / (1.0 - ADAM_B2 ** ADAM_STEP)
    delta = -ADAM_LR * (m_hat / (jnp.sqrt(v_hat) + ADAM_EPS) + ADAM_WD * w)
    return delta, m, v


def _adamw(w, mine, theirs, m, v, where, *, name, halves_last=False):
    _, rows, cols = w.shape
    if halves_last:
        cols //= 2
        tr = _row_tile(rows, cols, budget=1 << 19)
        grid = (rows // tr, 2)
        blk = pl.BlockSpec((None, tr, cols), lambda i, h, wh: (0, i, h))
        mine_spec = theirs_spec = pl.BlockSpec((tr, cols), lambda i, h, wh: (i, 0))
        which = lambda: pl.program_id(1)
    else:
        tr = _row_tile(rows // 2, cols, budget=1 << 19)
        nh = rows // 2 // tr
        grid = (rows // tr,)
        blk = pl.BlockSpec((None, tr, cols), lambda i, wh: (0, i, 0))
        mine_spec = pl.BlockSpec((tr, cols), lambda i, wh: (jnp.where(i // nh == wh[1], i % nh, 0), 0))
        theirs_spec = pl.BlockSpec((tr, cols), lambda i, wh: (jnp.where(i // nh == wh[1], 0, i % nh), 0))
        which = lambda: pl.program_id(0) // nh

    def body(wh_ref, w_ref, a_ref, b_ref, m_ref, v_ref, g_ref, d_ref, nm_ref, nv_ref):
        g = jnp.where(which() == wh_ref[1], a_ref[...], b_ref[...])
        d, nm, nv = _adamw_math(w_ref[...], g, m_ref[...], v_ref[...])
        g_ref[...] = g
        d_ref[...] = d
        nm_ref[...] = nm
        nv_ref[...] = nv

    rows, cols = w.shape[1:]
    return pl.pallas_call(
        body, name=name,
        grid_spec=pltpu.PrefetchScalarGridSpec(
            num_scalar_prefetch=1, grid=grid,
            in_specs=[blk, mine_spec, theirs_spec, blk, blk], out_specs=[blk] * 4),
        out_shape=[jax.ShapeDtypeStruct((1, rows, cols), F32)] * 4,
        compiler_params=_params(("parallel",) * len(grid)),
    )(where, w, mine, theirs, m, v)


SEG_LOSS = 0
SEG_SINK = 128
SEG_AGAIN = 256
SEG_L0 = SEG_AGAIN + ATTN_W
SEG_L1 = SEG_L0 + RNN_W
SEG_RGAIN = SEG_L1 + RNN_W
SEG_G = SEG_RGAIN + 128
N_PACK = SEG_G + 4 * D_MODEL


def _pack(sinks, again, l0, l1, rgain, gains, loss=None):
    z = lambda k: jnp.zeros((1, k), F32)
    first = z(128) if loss is None else loss
    return jnp.concatenate([first, sinks, z(128 - N_Q), again, l0, l1, rgain] + list(gains), axis=1)


def _small_reduce_adamw(part, w, m, v, *, name):
    def body(p_ref, w_ref, m_ref, v_ref, g_ref, d_ref, nm_ref, nv_ref, buf_ref, send_sems, recv_sems):
        x, y, c = _place()
        me = 4 * x + 2 * y + c
        copies = []
        for k in range(1, 8):
            dx, dy, dc = (k >> 2) & 1, (k >> 1) & 1, k & 1
            to = (x ^ dx, y ^ dy, c ^ dc)
            cp = pltpu.make_async_remote_copy(
                src_ref=p_ref, dst_ref=buf_ref.at[me],
                send_sem=send_sems.at[k - 1], recv_sem=recv_sems.at[k - 1],
                device_id=to, device_id_type=MESH)
            cp.start()
            copies.append(cp)
        buf_ref[me] = p_ref[...]
        for cp in copies:
            cp.wait()
        tot = buf_ref[0]
        for j in range(1, 8):
            tot = tot + buf_ref[j]
        g_ref[...] = tot
        l0 = w_ref[:, SEG_L0:SEG_L0 + RNN_W]
        l1 = w_ref[:, SEG_L1:SEG_L1 + RNN_W]
        mx = jnp.maximum(l0, l1)
        e0 = jnp.exp(l0 - mx)
        e1 = jnp.exp(l1 - mx)
        lb = e0 / (e0 + e1)
        gl0 = tot[:, SEG_L0:SEG_L0 + RNN_W] * lb * (1.0 - lb)
        g_ref[:, SEG_L0:SEG_L0 + RNN_W] = gl0
        g_ref[:, SEG_L1:SEG_L1 + RNN_W] = -gl0
        d, nm, nv = _adamw_math(w_ref[...], g_ref[...], m_ref[...], v_ref[...])
        d_ref[...] = d
        nm_ref[...] = nm
        nv_ref[...] = nv

    vm = pl.BlockSpec(memory_space=pltpu.VMEM)
    return pl.pallas_call(
        body, name=name,
        in_specs=[vm] * 4, out_specs=[vm] * 4,
        out_shape=[jax.ShapeDtypeStruct((1, N_PACK), F32)] * 4,
        scratch_shapes=[pltpu.VMEM((8, 1, N_PACK), F32), pltpu.SemaphoreType.DMA((7,)),
                        pltpu.SemaphoreType.DMA((7,))],
    )(part, w, m, v)


def _layer_grads(xs, tgt, bufs, where, sinks, again, lb_logits, rgain,
                 g_mix_pre, g_mix_post, g_mlp_pre, g_mlp_post):
    tm = 512
    b_in, b_out, b_up, b_dn = bufs

    shard = IN_W // N_CHIPS
    h1, b_in = _rms_cast_gather(xs, g_mix_pre, b_in, tm=tm, name="h1_norm_gather_w_in")
    w_in_t = b_in.reshape(IN_W, D_MODEL)
    proj, ((b_out, b_up),) = _mm(
        h1, w_in_t, tm=1024, tn=768, tk=D_MODEL, out_dtype=F32, w_layout="nk", name="in_proj",
        exchanges=[_x_gather([b_out, b_up], ici=[(0, 256), (0, 336)])])
    attn, lse, ((b_out, b_up),) = _swa_fwd(
        proj, sinks, name="swa_fwd",
        exchanges=[_x_gather([b_out, b_up], ici=[None, (336, 320)], d2d=[(0, 256), None])])
    w_out = b_out.reshape(D_MODEL, D_MODEL)
    o_pre, rnn, s0, ((b_up, b_dn),) = _hgrn_fwd(
        proj, lb_logits, rgain, tb=512, name="hgrn_fwd",
        exchanges=[_x_gather([b_up, b_dn], ici=[(656, 368), (0, 400)])])
    cat = _mix_cat(attn, rnn, again, tm=tm, name="mix_cat")
    mixed, ((b_up, b_dn),) = _mm(
        cat, w_out, tm=1024, tn=1024, tk=D_MODEL, out_dtype=BF16, name="out_proj",
        exchanges=[_x_gather([b_up, b_dn], ici=[None, (400, 240)], d2d=[(0, 1024), (0, 400)])])
    w_up4 = b_up.reshape(N_CHIPS, D_MODEL, D_FF // N_CHIPS)
    x1, h2, ((b_dn,),) = _post_norm_res(
        mixed, g_mix_post, xs, g_mlp_pre, tm=256, name="mix_post",
        exchanges=[_x_gather([b_dn], d2d=[(400, 240)])])
    u, ((b_dn,),) = _mm(h2, w_up4, tm=1024, tn=1024, tk=D_MODEL, out_dtype=BF16, relu=True, w_layout="skn",
                        name="mlp_up", exchanges=[_x_gather([b_dn], ici=[(640, 384)], cross=[(640, 384)])])
    w_dn = b_dn.reshape(D_FF, D_MODEL)
    yv = _mm(u, w_dn, tm=1024, tn=1024, tk=2048, out_dtype=BF16, a_square=True, name="mlp_down")
    dy, dx2, loss_row, dg_mlp_post = _loss_head(yv, g_mlp_post, x1, tgt, tm=256, name="loss_head")

    def halved(g):
        return g.reshape(N_CHIPS, 2, g.shape[1] // 2, g.shape[2])
    du = _mm(dy, w_dn, tm=1024, tn=1024, tk=D_MODEL, out_dtype=BF16, mul2=u, w_layout="nk", name="mlp_down_bwd")
    g_dn = halved(_mm_tn(u, dy, tm=1024, tn=1024, tt=2048, a_square=True, name="w_down_grad")
                  .reshape(N_CHIPS, D_FF // N_CHIPS, D_MODEL))
    d_w_up, ((sib_dn,),) = _mm_tn(h2, du, tm=1024, tn=1024, tt=2048, n_split=N_CHIPS, name="w_up_grad",
                                  exchanges=[_x_pair([g_dn])])
    g_up = halved(d_w_up)
    wire_dn = _pair_sum(g_dn, sib_dn, where, name="pair_sum_w_down")
    dh2, ((recv_dn,), (sib_up,)) = _mm(du, w_up4, tm=1024, tn=1024, tk=2048, out_dtype=BF16, w_layout="snk", name="mlp_up_bwd",
                                       exchanges=[_x_chip([wire_dn], rows=[(0, 704)]), _x_pair([g_up])])
    wire_up = _pair_sum(g_up, sib_up, where, name="pair_sum_w_up")
    dx1, dg_mlp_pre, ((recv_dn,),) = _rms_bwd(dh2, x1, g_mlp_pre, dx2, tm=256, out_dtype=F32, name="mlp_pre_bwd",
                                              exchanges=[_x_chip([wire_dn], rows=[(704, 224)], into=[recv_dn])])
    dmixed, dg_mix_post = _rms_bwd(dx1, mixed, g_mix_post, None, tm=256, out_dtype=BF16, name="mix_post_bwd")
    d_w_out, ((recv_dn,),) = _mm_tn(cat, dmixed, tm=1024, tn=1024, tt=2048, name="w_out_grad",
                                    exchanges=[_x_chip([wire_dn], rows=[(928, 96)], into=[recv_dn])])
    fin_dn = _final_half(g_dn, sib_dn, recv_dn, where, name="final_half_w_down")
    g_out = halved(d_w_out.reshape(N_CHIPS, D_MODEL // N_CHIPS, D_MODEL))
    dcat, ((sib_out,), (oth_dn,)) = _mm(dmixed, w_out, tm=1024, tn=1024, tk=D_MODEL, out_dtype=BF16, w_layout="nk",
                                        name="out_proj_bwd", exchanges=[_x_pair([g_out]), _x_share([fin_dn])])
    wire_out = _pair_sum(g_out, sib_out, where, name="pair_sum_w_out")
    dattn, dg_again = _rms_bwd(dcat, attn, again, None, tm=tm, out_dtype=BF16, name="attn_norm_bwd")
    dq_a, dkv, dsinks, ((recv_out,), (recv_up,)) = _swa_bwd(
        proj, sinks, dattn, lse, name="swa_bwd",
        exchanges=[_x_chip([wire_out]), _x_chip([wire_up], rows=[(0, 320)])])
    dq_r, df_r, di_r, dg_r, dlb, dgain_h, ((recv_up,),) = _hgrn_bwd(
        proj, lb_logits, rgain, o_pre, s0, dcat, tb=512, name="hgrn_bwd",
        exchanges=[_x_chip([wire_up], rows=[(320, 704)], into=[recv_up])])
    fin_up = _final_half(g_up, sib_up, recv_up, where, name="final_half_w_up")
    fin_out = _final_half(g_out, sib_out, recv_out, where, name="final_half_w_out")
    dproj = jnp.concatenate([dq_a, dkv, dq_r, df_r, di_r, dg_r], axis=1)
    piece_cols = D_MODEL // 4

    def w_in_piece(pc, exchanges):
        d, xres = _mm_tn(dproj, h1, tm=896, tn=2 * piece_cols, tt=2048, b_blocks=(pc, pc + 2),
                         name="w_in_grad_%d" % pc, exchanges=exchanges)
        return d.reshape(N_CHIPS, shard, 2 * piece_cols), xres

    g_in0, ((oth_up, oth_out),) = w_in_piece(0, [_x_share([fin_up, fin_out])])
    g_in1, ((sib_in0,),) = w_in_piece(1, [_x_pair([g_in0], halves_last=True)])
    wire_in0 = _pair_sum(g_in0, sib_in0, where, name="pair_sum_w_in_0", halves_last=True)
    dh1, ((recv_in0,), (sib_in1,)) = _mm(
        dproj, w_in_t, tm=1024, tn=1024, tk=2688, out_dtype=BF16, m_blocks=(0, 2), name="in_proj_bwd_0",
        exchanges=[_x_chip([wire_in0]), _x_pair([g_in1], halves_last=True)])
    wire_in1 = _pair_sum(g_in1, sib_in1, where, name="pair_sum_w_in_1", halves_last=True)
    dh1, ((recv_in1,),) = _mm(
        dproj, w_in_t, tm=1024, tn=1024, tk=2688, out_dtype=BF16, m_blocks=(2, 2), out_into=dh1,
        name="in_proj_bwd_1", exchanges=[_x_chip([wire_in1])])
    gx, dg_mix_pre = _rms_bwd(dh1, xs, g_mix_pre, dx1, tm=256, out_dtype=F32, name="mix_pre_bwd")
    fin_in0 = _final_half(g_in0, sib_in0, recv_in0, where, name="final_half_w_in_0", halves_last=True)
    fin_in1 = _final_half(g_in1, sib_in1, recv_in1, where, name="final_half_w_in_1", halves_last=True)
    oth_in0, oth_in1 = _run_exchange(_x_share([fin_in0, fin_in1]), name="share_w_in")
    fin_in = jnp.concatenate([fin_in0, fin_in1], axis=1)
    oth_in = jnp.concatenate([oth_in0, oth_in1], axis=1)

    big = [(fin_in, oth_in), (fin_out, oth_out), (fin_up, oth_up), (fin_dn, oth_dn)]
    drgain = jnp.sum(dgain_h, axis=0)
    small = _pack(jnp.sum(dsinks, axis=1)[None, :], dg_again, dlb, jnp.zeros_like(dlb), drgain,
                  [dg_mix_pre, dg_mix_post, dg_mlp_pre, dg_mlp_post], loss=loss_row)
    return gx, big, small


def kernel(x, w_in, attn_sinks, attn_out_gain, rnn_lb_logits, rnn_norm_gain, w_out, mix_pre_gain, mix_post_gain, mlp_pre_gain, mlp_post_gain, w_up, w_down, loss_target, m_w_in, m_attn_sinks, m_attn_out_gain, m_rnn_lb_logits, m_rnn_norm_gain, m_w_out, m_mix_pre_gain, m_mix_post_gain, m_mlp_pre_gain, m_mlp_post_gain, m_w_up, m_w_down, v_w_in, v_attn_sinks, v_attn_out_gain, v_rnn_lb_logits, v_rnn_norm_gain, v_w_out, v_mix_pre_gain, v_mix_post_gain, v_mlp_pre_gain, v_mlp_post_gain, v_w_up, v_w_down):
    ax, ay, ac = _place()
    where = jnp.stack([2 * ax + ay, ac]).astype(jnp.int32)
    t = lambda a: jnp.swapaxes(a, 1, 2)
    big_w = [t(w_in), w_out, w_up, w_down]
    big_m = [t(m_w_in), m_w_out, m_w_up, m_w_down]
    big_v = [t(v_w_in), v_w_out, v_w_up, v_w_down]

    names = ["w_in", "w_out", "w_up", "w_down"]
    bufs = [_cast_slots(w, where, name="cast_" + nm) for w, nm in zip(big_w, names)]
    gx, big_g, small_part = _layer_grads(
        x[0], loss_target[0], bufs, where, attn_sinks, attn_out_gain, rnn_lb_logits, rnn_norm_gain,
        mix_pre_gain, mix_post_gain, mlp_pre_gain, mlp_post_gain)

    grads, deltas, new_m, new_v = [], [], [], []
    for (f, o), w, m, v, nm in zip(big_g, big_w, big_m, big_v, names):
        res = _adamw(w, f, o, m, v, where, name="adamw_" + nm, halves_last=(nm == "w_in"))
        if nm == "w_in":
            res = [t(r) for r in res]
        g, d, nm_, nv_ = res
        grads.append(g)
        deltas.append(d)
        new_m.append(nm_)
        new_v.append(nv_)

    def pack_params(sinks, again, logits, rgain, gains):
        return _pack(sinks, again, logits[0:1], logits[1:2], rgain, gains)

    pw = pack_params(attn_sinks, attn_out_gain, rnn_lb_logits, rnn_norm_gain,
                     [mix_pre_gain, mix_post_gain, mlp_pre_gain, mlp_post_gain])
    pm = pack_params(m_attn_sinks, m_attn_out_gain, m_rnn_lb_logits, m_rnn_norm_gain,
                     [m_mix_pre_gain, m_mix_post_gain, m_mlp_pre_gain, m_mlp_post_gain])
    pv = pack_params(v_attn_sinks, v_attn_out_gain, v_rnn_lb_logits, v_rnn_norm_gain,
                     [v_mix_pre_gain, v_mix_post_gain, v_mlp_pre_gain, v_mlp_post_gain])
    packs = _small_reduce_adamw(small_part, pw, pm, pv, name="small_reduce_adamw")

    def unpack(p):
        seg = lambda o, k: p[:, o:o + k]
        logits = jnp.concatenate([seg(SEG_L0, RNN_W), seg(SEG_L1, RNN_W)], axis=0)
        gains = [seg(SEG_G + i * D_MODEL, D_MODEL) for i in range(4)]
        return dict(sinks=seg(SEG_SINK, N_Q), again=seg(SEG_AGAIN, ATTN_W), logits=logits,
                    rgain=seg(SEG_RGAIN, RNN_HD), gains=gains)

    def order(small, big):
        return [big[0], small["sinks"], small["again"], small["logits"], small["rgain"], big[1],
                *small["gains"], big[2], big[3]]

    loss = packs[0][0, 0]
    outs = [loss, gx[None]]
    for p, b in zip(packs, [grads, deltas, new_m, new_v]):
        outs += order(unpack(p), b)
    return tuple(outs)
```

```python
import functools

import jax
import jax.numpy as jnp
from jax import lax
from jax.experimental import pallas as pl
from jax.experimental.pallas import tpu as pltpu

F32 = jnp.float32
BF16 = jnp.bfloat16
MESH = pl.DeviceIdType.MESH

EPS = 1e-6
D_MODEL = 2048
ATTN_W = 1024
HEAD_DIM = 64
N_Q = 16
N_KV = 2
GROUP = 8
BLK = 128
RNN_W = 1024
RNN_HD = 128
N_RNN = 8
CHUNK = 64
SUB_FWD = 16
SUB_BWD = 8
D_FF = 8192
IN_W = 5376
N_CHIPS = 4
KV_COL = ATTN_W
QR_COL = ATTN_W + 2 * 128
FR_COL = QR_COL + RNN_W
IR_COL = FR_COL + RNN_W
GR_COL = IR_COL + RNN_W

ADAM_LR = 0.001
ADAM_B1 = 0.9
ADAM_B2 = 0.999
ADAM_EPS = 1e-08
ADAM_WD = 0.01
ADAM_STEP = 10

VMEM_LIMIT = 48 * 1024 * 1024

NT = (((1,), (1,)), ((), ()))
TN = (((0,), (0,)), ((), ()))


def _params(sem=None):
    return pltpu.CompilerParams(dimension_semantics=sem, vmem_limit_bytes=VMEM_LIMIT)


def _sigmoid(x):
    return 1.0 / (1.0 + jnp.exp(-x))


ANY = pl.BlockSpec(memory_space=pl.ANY)


def _place():
    return lax.axis_index("x"), lax.axis_index("y"), lax.axis_index("c")


def _other_chips(x, y):
    return [(1 - x, y), (x, 1 - y), (1 - x, 1 - y)]


class _Exchange:
    def __init__(self, srcs, outs, ncopy, build, aliases=None):
        self.srcs, self.outs, self.ncopy, self.build = list(srcs), list(outs), ncopy, build
        self.aliases = aliases or {}


def _remote(src, dst, send_sems, recv_sems, k, to):
    return pltpu.make_async_remote_copy(src_ref=src, dst_ref=dst, send_sem=send_sems.at[k],
                                        recv_sem=recv_sems.at[k], device_id=to, device_id_type=MESH)


def _call(body, *, name, grid, in_specs, out_specs, out_shape, args, scratch_shapes=(), semantics=None,
          exchanges=(), into=None):
    in_specs, out_specs, out_shape = list(in_specs), list(out_specs), list(out_shape)
    scratch_shapes = list(scratch_shapes)
    ni, no, ns = len(in_specs), len(out_specs), len(scratch_shapes)
    xsrc = [s for x in exchanges for s in x.srcs]
    xout = [o for x in exchanges for o in x.outs]
    into = into or {}
    xsrc += [into[k] for k in sorted(into)]
    nxi, nxo = len(xsrc), len(xout)
    aliases = {nxi - len(into) + ni + q: k for q, k in enumerate(sorted(into))}
    a0 = b0 = 0
    for x in exchanges:
        for si, oi in x.aliases.items():
            aliases[ni + a0 + si] = no + b0 + oi
        a0 += len(x.srcs)
        b0 += len(x.outs)
    sems = []
    for x in exchanges:
        sems += [pltpu.SemaphoreType.DMA((x.ncopy,)), pltpu.SemaphoreType.DMA((x.ncopy,))]

    def wrapped(*refs):
        ins, xi = refs[:ni], refs[ni:ni + nxi]
        outs, xo = refs[ni + nxi:ni + nxi + no], refs[ni + nxi + no:ni + nxi + no + nxo]
        rest = refs[ni + nxi + no + nxo:]
        scr, sm = rest[:ns], rest[ns:]

        def copies():
            cps = []
            a = b = 0
            for k, x in enumerate(exchanges):
                cps += x.build(xi[a:a + len(x.srcs)], xo[b:b + len(x.outs)], sm[2 * k], sm[2 * k + 1])
                a += len(x.srcs)
                b += len(x.outs)
            return cps

        def start():
            for cp in copies():
                cp.start()

        def wait():
            for cp in copies():
                cp.wait()

        if not exchanges:
            body(*ins, *outs, *scr)
        elif not grid:
            start()
            body(*ins, *outs, *scr)
            wait()
        else:
            first = last = None
            for ax, g in enumerate(grid):
                f = pl.program_id(ax) == 0
                l = pl.program_id(ax) == g - 1
                first = f if first is None else first & f
                last = l if last is None else last & l
            pl.when(first)(start)
            body(*ins, *outs, *scr)
            pl.when(last)(wait)

    if exchanges and semantics is not None:
        semantics = ("arbitrary",) * len(grid)
    kwargs = dict(grid=grid) if grid else {}
    res = pl.pallas_call(
        wrapped, name=name,
        in_specs=in_specs + [ANY] * nxi, out_specs=out_specs + [ANY] * nxo,
        out_shape=out_shape + xout, scratch_shapes=scratch_shapes + sems,
        input_output_aliases=aliases,
        compiler_params=_params(semantics), **kwargs,
    )(*args, *xsrc)
    res = list(res)
    mine, theirs = res[:no], res[no:]
    per = []
    b = 0
    for x in exchanges:
        per.append(theirs[b:b + len(x.outs)])
        b += len(x.outs)
    return mine, per


def _run_exchange(x, *, name):
    return _call(lambda: None, name=name, grid=(), in_specs=[], out_specs=[], out_shape=[], args=[],
                 exchanges=[x])[1][0]


def _x_gather(bufs, ici=None, d2d=None, cross=None):
    n = len(bufs)
    plan = [(a, kind, rows[a]) for a in range(n) for kind, rows in (("ici", ici), ("d2d", d2d), ("cross", cross))
            if rows is not None and rows[a] is not None]

    def build(srcs, outs, ss, rs):
        x, y, c = _place()
        cps = []
        for q, (a, kind, rows) in enumerate(plan):
            piece = pl.ds(*rows)
            for j, (px, py) in enumerate(_other_chips(x, y)):
                if kind == "d2d":
                    slot, to = 4 * px + 2 * py + c, (x, y, 1 - c)
                else:
                    slot, to = 4 * x + 2 * y + c, (px, py, c if kind == "ici" else 1 - c)
                cps.append(_remote(srcs[a].at[slot, piece], outs[a].at[slot, piece], ss, rs, 3 * q + j, to))
        return cps

    outs = [jax.ShapeDtypeStruct(b.shape, b.dtype) for b in bufs]
    return _Exchange(bufs, outs, 3 * len(plan), build, aliases={a: a for a in range(n)})


def _x_pair(grads, halves_last=False):
    n = len(grads)

    def build(srcs, outs, ss, rs):
        x, y, c = _place()

        def half(r):
            if not halves_last:
                return r.at[:, 1 - c]
            ch = r.shape[2] // 2
            return r.at[:, :, pl.ds(pl.multiple_of((1 - c) * ch, 128), ch)]

        return [_remote(half(srcs[a]), outs[a], ss, rs, a, (x, y, 1 - c)) for a in range(n)]

    if halves_last:
        outs = [jax.ShapeDtypeStruct(g.shape[:2] + (g.shape[2] // 2,), g.dtype) for g in grads]
    else:
        outs = [jax.ShapeDtypeStruct((4,) + g.shape[2:], g.dtype) for g in grads]
    return _Exchange(grads, outs, n, build)


def _x_chip(wires, rows=None, into=None):
    n = len(wires)
    rows = rows or [(0, w.shape[1]) for w in wires]

    def build(srcs, outs, ss, rs):
        x, y, c = _place()
        cps = []
        for a in range(n):
            piece = pl.ds(*rows[a])
            for j, (px, py) in enumerate(_other_chips(x, y)):
                cps.append(_remote(srcs[a].at[2 * px + py, piece], outs[a].at[j, piece], ss, rs,
                                   3 * a + j, (px, py, c)))
        return cps

    outs = [jax.ShapeDtypeStruct((3,) + w.shape[1:], w.dtype) for w in wires]
    if into is None:
        return _Exchange(wires, outs, 3 * n, build)
    return _Exchange(list(wires) + list(into), outs, 3 * n, build, aliases={n + a: a for a in range(n)})


def _x_share(halves):
    n = len(halves)

    def build(srcs, outs, ss, rs):
        x, y, c = _place()
        return [_remote(srcs[a], outs[a], ss, rs, a, (x, y, 1 - c)) for a in range(n)]

    outs = [jax.ShapeDtypeStruct(h.shape, h.dtype) for h in halves]
    return _Exchange(halves, outs, n, build)


def _mm(a, w, *, tm, tn, tk, out_dtype, name, a_square=False, relu=False, mul2=None, w_layout="kn",
        m_blocks=None, out_into=None, exchanges=()):
    m, k = a.shape
    m_first, m_count = m_blocks or (0, m // tm)
    a_spec = pl.BlockSpec((tm, tk), lambda i, j, kk: (i + m_first, kk))
    if w_layout == "kn":
        n = w.shape[1]
        w_spec = pl.BlockSpec((tk, tn), lambda i, j, kk: (kk, j))
    elif w_layout == "nk":
        n = w.shape[0]
        w_spec = pl.BlockSpec((tn, tk), lambda i, j, kk: (j, kk))
    elif w_layout == "skn":
        n = w.shape[0] * w.shape[2]
        per_n = w.shape[2] // tn
        w_spec = pl.BlockSpec((None, tk, tn), lambda i, j, kk: (j // per_n, kk, j % per_n))
    else:
        assert w_layout == "snk"
        n = w.shape[1]
        per_k = w.shape[2] // tk
        w_spec = pl.BlockSpec((None, tn, tk), lambda i, j, kk: (kk // per_k, j, kk % per_k))
    w_dims = NT if w_layout in ("nk", "snk") else (((1,), (0,)), ((), ()))
    nk = k // tk
    assert m % tm == 0 and n % tn == 0 and k % tk == 0

    def body(*refs):
        if mul2 is not None:
            a_ref, w_ref, e_ref, o_ref, acc_ref = refs
        else:
            a_ref, w_ref, o_ref, acc_ref = refs
            e_ref = None
        kk = pl.program_id(2)
        av = a_ref[...]
        if a_square:
            af = av.astype(F32)
            av = (af * af).astype(BF16)
        part = lax.dot_general(av, w_ref[...], w_dims, preferred_element_type=F32)

        def finish(r):
            if relu:
                r = jnp.maximum(r, 0.0)
            if e_ref is not None:
                r = 2.0 * e_ref[...].astype(F32) * r
            o_ref[...] = r.astype(out_dtype)

        if nk == 1:
            finish(part)
        else:
            @pl.when(kk == 0)
            def _():
                acc_ref[...] = part

            @pl.when(kk > 0)
            def _():
                acc_ref[...] += part

            @pl.when(kk == nk - 1)
            def _():
                finish(acc_ref[...])

    in_specs = [a_spec, w_spec]
    args = [a, w]
    if mul2 is not None:
        in_specs.append(pl.BlockSpec((tm, tn), lambda i, j, kk: (i + m_first, j)))
        args.append(mul2)
    acc_shape = (tm, tn) if nk > 1 else (8, 128)
    (out,), per = _call(
        body, name=name, grid=(m_count, n // tn, nk),
        in_specs=in_specs, out_specs=[pl.BlockSpec((tm, tn), lambda i, j, kk: (i + m_first, j))],
        out_shape=[jax.ShapeDtypeStruct((m, n), out_dtype)], args=args,
        scratch_shapes=[pltpu.VMEM(acc_shape, F32)],
        semantics=("parallel", "parallel", "arbitrary"), exchanges=exchanges,
        into=None if out_into is None else {0: out_into})
    return (out, per) if exchanges else out


def _mm_tn(a, b, *, tm, tn, tt, name, a_square=False, n_split=1, b_blocks=None, exchanges=()):
    t, m = a.shape
    nb = len(b_blocks) if b_blocks else 1
    n = tn if b_blocks else b.shape[1]
    assert t % tt == 0 and m % tm == 0 and n % tn == 0 and (n // n_split) % tn == 0
    per = n // n_split // tn

    def body(a_ref, *refs):
        b_refs, o_ref = refs[:nb], refs[nb]
        ti = pl.program_id(2)
        av = a_ref[...]
        if a_square:
            af = av.astype(F32)
            av = (af * af).astype(BF16)
        bv = b_refs[0][...] if nb == 1 else jnp.concatenate([r[...] for r in b_refs], axis=1)
        part = lax.dot_general(av, bv, TN, preferred_element_type=F32)

        @pl.when(ti == 0)
        def _():
            o_ref[...] = part

        @pl.when(ti > 0)
        def _():
            o_ref[...] += part

    if b_blocks:
        b_specs = [pl.BlockSpec((tt, tn // nb), functools.partial(lambda blk, i, j, ti: (ti, blk), blk))
                   for blk in b_blocks]
    else:
        b_specs = [pl.BlockSpec((tt, tn), lambda i, j, ti: (ti, j))]
    (out,), xres = _call(
        body, name=name, grid=(m // tm, n // tn, t // tt),
        in_specs=[pl.BlockSpec((tt, tm), lambda i, j, ti: (ti, i))] + b_specs,
        out_specs=[pl.BlockSpec((None, tm, tn), lambda i, j, ti: (j // per, i, j % per))],
        out_shape=[jax.ShapeDtypeStruct((n_split, m, n // n_split), F32)], args=[a] + [b] * nb,
        semantics=("parallel", "parallel", "arbitrary"), exchanges=exchanges)
    return (out, xres) if exchanges else out


def _rstd(x):
    return lax.rsqrt(jnp.mean(x * x, axis=-1, keepdims=True) + EPS)


def _rms_cast_gather(x, g, buf, *, tm, name):
    t, d = x.shape
    steps = t // tm

    def body(x_ref, g_ref, b_in, o_ref, b_out, send_sems, recv_sems):
        i = pl.program_id(0)
        xc, yc, c = _place()
        chips = _other_chips(xc, yc)

        def slot(px, py, pc):
            return b_out.at[4 * px + 2 * py + pc]

        def sent(j):
            return _remote(b_in.at[4 * xc + 2 * yc + c], slot(xc, yc, c), send_sems, recv_sems, j, (*chips[j], c))

        def passed(j):
            return _remote(slot(*chips[j], c), slot(*chips[j], c), send_sems, recv_sems, 3 + j, (xc, yc, 1 - c))

        @pl.when(i == 0)
        def _():
            for j in range(3):
                sent(j).start()

        xv = x_ref[...]
        o_ref[...] = (xv * _rstd(xv) * g_ref[...]).astype(BF16)

        @pl.when(i == steps - 1)
        def _():
            for j in range(3):
                sent(j).wait_recv()
                passed(j).start()
            for j in range(3):
                passed(j).wait_recv()
                passed(j).wait_send()
                sent(j).wait_send()

    return pl.pallas_call(
        body, name=name, grid=(steps,),
        in_specs=[pl.BlockSpec((tm, d), lambda i: (i, 0)), pl.BlockSpec((1, d), lambda i: (0, 0)), ANY],
        out_specs=[pl.BlockSpec((tm, d), lambda i: (i, 0)), ANY],
        out_shape=[jax.ShapeDtypeStruct((t, d), BF16), jax.ShapeDtypeStruct(buf.shape, buf.dtype)],
        scratch_shapes=[pltpu.SemaphoreType.DMA((6,)), pltpu.SemaphoreType.DMA((6,))],
        input_output_aliases={2: 1},
        compiler_params=_params(("arbitrary",)),
    )(x, g, buf)


def _mix_cat(attn, rnn, gain, *, tm, name):
    t = attn.shape[0]

    def body(a_ref, r_ref, g_ref, o_ref):
        av = a_ref[...]
        o_ref[:, :ATTN_W] = (av * _rstd(av) * g_ref[...]).astype(BF16)
        o_ref[:, ATTN_W:] = r_ref[...].astype(BF16)

    return pl.pallas_call(
        body, name=name, grid=(t // tm,),
        in_specs=[pl.BlockSpec((tm, ATTN_W), lambda i: (i, 0)), pl.BlockSpec((tm, RNN_W), lambda i: (i, 0)),
                  pl.BlockSpec((1, ATTN_W), lambda i: (0, 0))],
        out_specs=pl.BlockSpec((tm, D_MODEL), lambda i: (i, 0)),
        out_shape=jax.ShapeDtypeStruct((t, D_MODEL), BF16),
        compiler_params=_params(("parallel",)),
    )(attn, rnn, gain)


def _post_norm_res(mixed, g_post, res, g_next, *, tm, name, exchanges=()):
    t, d = mixed.shape

    def body(m_ref, gp_ref, r_ref, gn_ref, x1_ref, h2_ref):
        mv = m_ref[...].astype(F32)
        x1 = r_ref[...] + mv * _rstd(mv) * gp_ref[...]
        x1_ref[...] = x1
        h2_ref[...] = (x1 * _rstd(x1) * gn_ref[...]).astype(BF16)

    row = pl.BlockSpec((tm, d), lambda i: (i, 0))
    vec = pl.BlockSpec((1, d), lambda i: (0, 0))
    res_, xres = _call(
        body, name=name, grid=(t // tm,),
        in_specs=[row, vec, row, vec], out_specs=[row, row],
        out_shape=[jax.ShapeDtypeStruct((t, d), F32), jax.ShapeDtypeStruct((t, d), BF16)],
        args=[mixed, g_post, res, g_next], semantics=("parallel",), exchanges=exchanges)
    return (*res_, xres) if exchanges else res_


def _rms_bwd(dyn, xin, g, res, *, tm, out_dtype, name, col_block=0, exchanges=()):
    t, d = xin.shape

    def body(*refs):
        if res is not None:
            dy_ref, x_ref, g_ref, r_ref, dx_ref, dg_ref = refs
        else:
            dy_ref, x_ref, g_ref, dx_ref, dg_ref = refs
        i = pl.program_id(0)
        xv = x_ref[...].astype(F32)
        dy = dy_ref[...].astype(F32)
        r = _rstd(xv)
        xh = xv * r
        part = jnp.sum(dy * xh, axis=0, keepdims=True)

        @pl.when(i == 0)
        def _():
            dg_ref[...] = part

        @pl.when(i > 0)
        def _():
            dg_ref[...] += part

        tt = dy * g_ref[...]
        dx = r * (tt - xh * jnp.mean(tt * xh, axis=-1, keepdims=True))
        if res is not None:
            dx = dx + r_ref[...]
        dx_ref[...] = dx.astype(out_dtype)

    row = pl.BlockSpec((tm, d), lambda i: (i, 0))
    vec = pl.BlockSpec((1, d), lambda i: (0, 0))
    in_specs = [pl.BlockSpec((tm, d), lambda i: (i, col_block)), row, vec]
    args = [dyn, xin, g]
    if res is not None:
        in_specs.append(row)
        args.append(res)
    res, xres = _call(
        body, name=name, grid=(t // tm,),
        in_specs=in_specs, out_specs=[row, vec],
        out_shape=[jax.ShapeDtypeStruct((t, d), out_dtype), jax.ShapeDtypeStruct((1, d), F32)], args=args,
        semantics=("arbitrary",), exchanges=exchanges)
    return (*res, xres) if exchanges else res


def _loss_head(y, g_post, x1, target, *, tm, name):
    t, d = y.shape

    def body(y_ref, g_ref, x1_ref, t_ref, dy_ref, dx2_ref, loss_ref, dg_ref):
        i = pl.program_id(0)
        yv = y_ref[...].astype(F32)
        r = _rstd(yv)
        yh = yv * r
        gv = g_ref[...]
        err = x1_ref[...] + yh * gv - t_ref[...]
        lpart = 0.5 * jnp.sum(jnp.mean(err * err, axis=-1, keepdims=True), axis=0, keepdims=True)
        dx2 = err * (1.0 / d)
        dgp = jnp.sum(dx2 * yh, axis=0, keepdims=True)
        lane = lax.broadcasted_iota(jnp.int32, (1, 128), 1)
        lrow = jnp.where(lane == 0, lpart, 0.0)

        @pl.when(i == 0)
        def _():
            dg_ref[...] = dgp
            loss_ref[...] = lrow

        @pl.when(i > 0)
        def _():
            dg_ref[...] += dgp
            loss_ref[...] += lrow

        tt = dx2 * gv
        dy_ref[...] = (r * (tt - yh * jnp.mean(tt * yh, axis=-1, keepdims=True))).astype(BF16)
        dx2_ref[...] = dx2

    row = pl.BlockSpec((tm, d), lambda i: (i, 0))
    vec = pl.BlockSpec((1, d), lambda i: (0, 0))
    return pl.pallas_call(
        body, name=name, grid=(t // tm,),
        in_specs=[row, vec, row, row],
        out_specs=[row, row, pl.BlockSpec((1, 128), lambda i: (0, 0)), vec],
        out_shape=[jax.ShapeDtypeStruct((t, d), BF16), jax.ShapeDtypeStruct((t, d), F32),
                   jax.ShapeDtypeStruct((1, 128), F32), jax.ShapeDtypeStruct((1, d), F32)],
        compiler_params=_params(("arbitrary",)),
    )(y, g_post, x1, target)


def _alibi_slope(h):
    return 2.0 ** (-8.0 * (h + 1) / N_Q)


PAIR = 2 * HEAD_DIM
N_PAIRS = N_Q // 2
PAIRS_PER_KV = GROUP // 2
SMEM = pl.BlockSpec(memory_space=pltpu.SMEM)


def _swa_mask(n):
    key = lax.broadcasted_iota(jnp.int32, (2 * BLK, BLK), 0)
    qry = lax.broadcasted_iota(jnp.int32, (2 * BLK, BLK), 1)
    dist = qry + BLK - key
    valid = (dist >= 0) & (dist < BLK) & ((key >= BLK) | (n > 0))
    return valid, dist.astype(F32)


def _block_diag(kvp_ref, kvc_ref, off):
    a = jnp.concatenate([kvp_ref[:, off:off + HEAD_DIM], kvc_ref[:, off:off + HEAD_DIM]], axis=0).astype(BF16)
    z = jnp.zeros_like(a)
    return jnp.concatenate([jnp.concatenate([a, z], axis=1), jnp.concatenate([z, a], axis=1)], axis=0)


def _swa_scores(s2, e, hh, valid, distf):
    s = s2[2 * BLK * e:2 * BLK * (e + 1)] * (HEAD_DIM ** -0.5) - _alibi_slope(hh) * distf
    return jnp.where(valid, s, -1e30)


def _swa_fwd(proj, sinks, *, name, exchanges=()):
    t = proj.shape[0]
    nb = t // BLK
    kvb = KV_COL // (2 * 128)

    def body(sink_ref, q_ref, kvc_ref, kvp_ref, o_ref, lse_ref):
        n = pl.program_id(0)
        valid, distf = _swa_mask(n)
        for kvh in range(N_KV):
            k2 = _block_diag(kvp_ref, kvc_ref, kvh * HEAD_DIM)
            v2 = _block_diag(kvp_ref, kvc_ref, 128 + kvh * HEAD_DIM)
            for jp in range(PAIRS_PER_KV):
                pair = kvh * PAIRS_PER_KV + jp
                lanes = slice(pair * PAIR, (pair + 1) * PAIR)
                s2 = lax.dot_general(k2, q_ref[:, lanes].astype(BF16), NT, preferred_element_type=F32)
                probs = []
                for e in range(2):
                    hh = 2 * pair + e
                    s = _swa_scores(s2, e, hh, valid, distf)
                    sink = sink_ref[0, hh]
                    mx = jnp.maximum(jnp.max(s, axis=0, keepdims=True), sink)
                    p = jnp.exp(s - mx)
                    l = jnp.sum(p, axis=0, keepdims=True) + jnp.exp(sink - mx)
                    probs.append((p * (1.0 / l)).astype(BF16))
                    lse_ref[hh:hh + 1, :] = mx + jnp.log(l)
                o_ref[:, lanes] = lax.dot_general(jnp.concatenate(probs, axis=0), v2, TN,
                                                  preferred_element_type=F32)

    res, xres = _call(
        body, name=name, grid=(nb,),
        in_specs=[SMEM,
                  pl.BlockSpec((BLK, ATTN_W), lambda n: (n, 0)),
                  pl.BlockSpec((BLK, 256), lambda n: (n, kvb)),
                  pl.BlockSpec((BLK, 256), lambda n: (jnp.maximum(n - 1, 0), kvb))],
        out_specs=[pl.BlockSpec((BLK, ATTN_W), lambda n: (n, 0)),
                   pl.BlockSpec((None, N_Q, BLK), lambda n: (n, 0, 0))],
        out_shape=[jax.ShapeDtypeStruct((t, ATTN_W), F32), jax.ShapeDtypeStruct((nb, N_Q, BLK), F32)],
        args=[sinks, proj, proj, proj], semantics=("parallel",), exchanges=exchanges)
    return (*res, xres) if exchanges else res


def _swa_bwd(proj, sinks, dattn, lse, *, name, exchanges=()):
    t = proj.shape[0]
    nb = t // BLK
    kvb = KV_COL // (2 * 128)

    def body(sink_ref, q_ref, kvc_ref, kvp_ref, do_ref, lse_ref, dq_ref, dkv_ref, dsink_ref, carry_ref):
        n = pl.program_id(0)

        @pl.when(n == 0)
        def _():
            dsink_ref[...] = jnp.zeros_like(dsink_ref)
            carry_ref[...] = jnp.zeros_like(carry_ref)

        @pl.when(n < nb)
        def _():
            valid, distf = _swa_mask(n)
            for kvh in range(N_KV):
                k2 = _block_diag(kvp_ref, kvc_ref, kvh * HEAD_DIM)
                v2 = _block_diag(kvp_ref, kvc_ref, 128 + kvh * HEAD_DIM)
                dk2 = jnp.zeros((4 * BLK, PAIR), F32)
                dv2 = jnp.zeros((4 * BLK, PAIR), F32)
                for jp in range(PAIRS_PER_KV):
                    pair = kvh * PAIRS_PER_KV + jp
                    lanes = slice(pair * PAIR, (pair + 1) * PAIR)
                    q2 = q_ref[:, lanes].astype(BF16)
                    do2 = do_ref[:, lanes].astype(BF16)
                    s2 = lax.dot_general(k2, q2, NT, preferred_element_type=F32)
                    dp2 = lax.dot_general(v2, do2, NT, preferred_element_type=F32)
                    probs, dss = [], []
                    for e in range(2):
                        hh = 2 * pair + e
                        lse_h = lse_ref[hh:hh + 1, :]
                        p = jnp.exp(_swa_scores(s2, e, hh, valid, distf) - lse_h)
                        dp = dp2[2 * BLK * e:2 * BLK * (e + 1)]
                        delta = jnp.sum(p * dp, axis=0, keepdims=True)
                        dsink_ref[hh:hh + 1, :] += -jnp.exp(sink_ref[0, hh] - lse_h) * delta
                        probs.append(p.astype(BF16))
                        dss.append((p * (dp - delta)).astype(BF16))
                    ds2 = jnp.concatenate(dss, axis=0)
                    dq_ref[:, lanes] = (lax.dot_general(ds2, k2, TN, preferred_element_type=F32)
                                        * (HEAD_DIM ** -0.5)).astype(BF16)
                    dk2 = dk2 + jnp.dot(ds2, q2, preferred_element_type=F32)
                    dv2 = dv2 + jnp.dot(jnp.concatenate(probs, axis=0), do2, preferred_element_type=F32)
                dk_cat = (dk2[:2 * BLK, :HEAD_DIM] + dk2[2 * BLK:, HEAD_DIM:]) * (HEAD_DIM ** -0.5)
                dv_cat = dv2[:2 * BLK, :HEAD_DIM] + dv2[2 * BLK:, HEAD_DIM:]
                ko = kvh * HEAD_DIM
                vo = 128 + kvh * HEAD_DIM
                dkv_ref[:, ko:ko + HEAD_DIM] = (carry_ref[:, ko:ko + HEAD_DIM] + dk_cat[:BLK]).astype(BF16)
                dkv_ref[:, vo:vo + HEAD_DIM] = (carry_ref[:, vo:vo + HEAD_DIM] + dv_cat[:BLK]).astype(BF16)
                carry_ref[:, ko:ko + HEAD_DIM] = dk_cat[BLK:]
                carry_ref[:, vo:vo + HEAD_DIM] = dv_cat[BLK:]

        @pl.when(n == nb)
        def _():
            dkv_ref[...] = carry_ref[...].astype(BF16)

    last = nb - 1
    res, xres = _call(
        body, name=name, grid=(nb + 1,),
        in_specs=[SMEM,
                  pl.BlockSpec((BLK, ATTN_W), lambda n: (jnp.minimum(n, last), 0)),
                  pl.BlockSpec((BLK, 256), lambda n: (jnp.minimum(n, last), kvb)),
                  pl.BlockSpec((BLK, 256), lambda n: (jnp.maximum(jnp.minimum(n, last) - 1, 0), kvb)),
                  pl.BlockSpec((BLK, ATTN_W), lambda n: (jnp.minimum(n, last), 0)),
                  pl.BlockSpec((None, N_Q, BLK), lambda n: (jnp.minimum(n, last), 0, 0))],
        out_specs=[pl.BlockSpec((BLK, ATTN_W), lambda n: (jnp.minimum(n, last), 0)),
                   pl.BlockSpec((BLK, 256), lambda n: (jnp.maximum(n - 1, 0), 0)),
                   pl.BlockSpec((N_Q, BLK), lambda n: (0, 0))],
        out_shape=[jax.ShapeDtypeStruct((t, ATTN_W), BF16), jax.ShapeDtypeStruct((t, 256), BF16),
                   jax.ShapeDtypeStruct((N_Q, BLK), F32)],
        scratch_shapes=[pltpu.VMEM((BLK, 256), F32)],
        args=[sinks, proj, proj, proj, dattn, lse], semantics=("arbitrary",), exchanges=exchanges)
    return (*res, xres) if exchanges else res


def _cumsum_rows(x):
    n = x.shape[0]
    row = lax.broadcasted_iota(jnp.int32, x.shape, 0)
    s = 1
    while s < n:
        x = x + jnp.where(row >= s, pltpu.roll(x, s, axis=0), 0.0)
        s *= 2
    return x


def _rev_cumsum_rows(x):
    n = x.shape[0]
    row = lax.broadcasted_iota(jnp.int32, x.shape, 0)
    s = 1
    while s < n:
        x = x + jnp.where(row < n - s, pltpu.roll(x, n - s, axis=0), 0.0)
        s *= 2
    return x


def _lower_bound(lbl_ref):
    l0 = lbl_ref[0:1, :]
    l1 = lbl_ref[1:2, :]
    mx = jnp.maximum(l0, l1)
    e0 = jnp.exp(l0 - mx)
    e1 = jnp.exp(l1 - mx)
    return e0 / (e0 + e1)


def _hgrn_gates(z, lb):
    sg = _sigmoid(z)
    f = lb + (1.0 - lb) * sg
    return sg, f, jnp.log(f), 1.0 - f


def _sub_factors(b, k, i, sub, trim):
    need = -(-sub * i // 16) * 16 if trim else CHUNK
    rows = lax.broadcasted_iota(jnp.int32, (need, RNN_HD), 0)
    ref = b[sub * i - 1:sub * i, :]
    qfac = jnp.exp(b[sub * i:sub * (i + 1), :] - ref)
    kfac = jnp.where(rows < sub * i, jnp.exp(ref - b[:need]), 0.0)
    kt = (k[:need] * kfac).astype(BF16)
    if need < CHUNK:
        kt = jnp.concatenate([kt, jnp.zeros((CHUNK - need, RNN_HD), BF16)], axis=0)
    return qfac, kfac, kt


def _diag_decay(bi, s):
    trow = lax.broadcasted_iota(jnp.int32, bi.shape, 0)
    return jnp.where(trow >= s, jnp.exp(bi - bi[s:s + 1, :]), 0.0)


def _hgrn_fwd(proj, lb_logits, norm_gain, *, tb, name, exchanges=()):
    t = proj.shape[0]
    ntb = t // tb
    nch = tb // CHUNK
    qb, fb, ib, gb = QR_COL // 128, FR_COL // 128, IR_COL // 128, GR_COL // 128

    def body(q_ref, f_ref, i_ref, g_ref, lbl_ref, gain_ref, o_ref, out_ref, s0_ref, st_ref):
        c = pl.program_id(1)

        @pl.when(c == 0)
        def _():
            st_ref[...] = jnp.zeros_like(st_ref)

        lb = _lower_bound(lbl_ref)
        gain = gain_ref[...]

        def chunk(ci, st):
            rows = slice(ci * CHUNK, (ci + 1) * CHUNK)
            _, _, lf, k = _hgrn_gates(f_ref[rows, :], lb)
            qr = q_ref[rows, :]
            q = qr * _sigmoid(qr)
            v = i_ref[rows, :]
            b = _cumsum_rows(lf)
            s0_ref[ci] = st
            o_inter = lax.dot_general((q * jnp.exp(b)).astype(BF16), st.astype(BF16), NT,
                                      preferred_element_type=F32)
            vb = v.astype(BF16)
            blast = b[CHUNK - 1:CHUNK, :]
            khat = (k * jnp.exp(blast - b)).astype(BF16)
            st = st * jnp.exp(blast) + lax.dot_general(vb, khat, TN, preferred_element_type=F32)
            blocks = []
            for i in range(CHUNK // SUB_FWD):
                blk = slice(SUB_FWD * i, SUB_FWD * (i + 1))
                qi, ki, vi, bi = q[blk], k[blk], v[blk], b[blk]
                oi = o_inter[blk]
                if i > 0:
                    qfac, _, kt = _sub_factors(b, k, i, SUB_FWD, trim=True)
                    att = lax.dot_general((qi * qfac).astype(BF16), kt, NT,
                                          preferred_element_type=F32)
                    oi = oi + jnp.dot(att.astype(BF16), vb, preferred_element_type=F32)
                for s in range(SUB_FWD):
                    qe = qi * _diag_decay(bi, s)
                    a = jnp.sum(qe * ki[s:s + 1, :], axis=1, keepdims=True)
                    oi = oi + a * vi[s:s + 1, :]
                blocks.append(oi)
            o = jnp.concatenate(blocks, axis=0)
            o_ref[rows, :] = o
            gr = g_ref[rows, :]
            out_ref[rows, :] = o * _rstd(o) * gain * (gr * _sigmoid(gr))
            return st

        st = st_ref[...]
        for ci in range(nch):
            st = chunk(ci, st)
        st_ref[...] = st

    def col(base):
        return pl.BlockSpec((tb, RNN_HD), lambda h, c: (c, base + h))

    res, xres = _call(
        body, name=name, grid=(N_RNN, ntb),
        in_specs=[col(qb), col(fb), col(ib), col(gb),
                  pl.BlockSpec((2, RNN_HD), lambda h, c: (0, h)), pl.BlockSpec((1, RNN_HD), lambda h, c: (0, 0))],
        out_specs=[pl.BlockSpec((tb, RNN_HD), lambda h, c: (c, h)), pl.BlockSpec((tb, RNN_HD), lambda h, c: (c, h)),
                   pl.BlockSpec((None, nch, RNN_HD, RNN_HD), lambda h, c: (h, c, 0, 0))],
        out_shape=[jax.ShapeDtypeStruct((t, RNN_W), F32), jax.ShapeDtypeStruct((t, RNN_W), F32),
                   jax.ShapeDtypeStruct((N_RNN, t // CHUNK, RNN_HD, RNN_HD), F32)],
        scratch_shapes=[pltpu.VMEM((RNN_HD, RNN_HD), F32)],
        args=[proj, proj, proj, proj, lb_logits, norm_gain],
        semantics=("parallel", "arbitrary"), exchanges=exchanges)
    return (*res, xres) if exchanges else res


def _hgrn_bwd(proj, lb_logits, norm_gain, o_pre, s0, dcat, *, tb, name, exchanges=()):
    t = proj.shape[0]
    ntb = t // tb
    nch = tb // CHUNK
    qb, fb, ib, gb = QR_COL // 128, FR_COL // 128, IR_COL // 128, GR_COL // 128
    sub = SUB_BWD
    nsub = CHUNK // sub

    def body(q_ref, f_ref, i_ref, g_ref, lbl_ref, gain_ref, o_ref, s0_ref, dout_ref,
             dq_ref, df_ref, di_ref, dg_ref, dlb_ref, dgain_ref,
             dst_ref, dqs_ref, dks_ref, dvs_ref):
        c = pl.program_id(1)

        @pl.when(c == 0)
        def _():
            dst_ref[...] = jnp.zeros_like(dst_ref)
            dlb_ref[...] = jnp.zeros_like(dlb_ref)
            dgain_ref[...] = jnp.zeros_like(dgain_ref)

        lb = _lower_bound(lbl_ref)
        gain = gain_ref[...]

        def chunk(ci, dst):
            rows = slice(ci * CHUNK, (ci + 1) * CHUNK)
            dqa_ref, dka_ref, dva_ref = dqs_ref.at[ci], dks_ref.at[ci], dvs_ref.at[ci]
            sg, f, lf, k = _hgrn_gates(f_ref[rows, :], lb)
            qr = q_ref[rows, :]
            sq = _sigmoid(qr)
            q = qr * sq
            v = i_ref[rows, :]
            b = _cumsum_rows(lf)

            dout = dout_ref[rows, :].astype(F32)
            o = o_ref[rows, :]
            gr = g_ref[rows, :]
            sgg = _sigmoid(gr)
            gate = gr * sgg
            rs = _rstd(o)
            nrm = o * rs
            dg_ref[rows, :] = (dout * nrm * gain * (sgg * (1.0 + gr * (1.0 - sgg)))).astype(BF16)
            dn = dout * gate
            dgain_ref[...] += jnp.sum(dn * nrm, axis=0, keepdims=True)
            tt = dn * gain
            do = rs * (tt - nrm * jnp.mean(tt * nrm, axis=-1, keepdims=True))

            dob = do.astype(BF16)
            vb = v.astype(BF16)
            eb = jnp.exp(b)
            blast = b[CHUNK - 1:CHUNK, :]
            ebl = jnp.exp(blast - b)
            dstb = dst.astype(BF16)
            khat = (k * ebl).astype(BF16)
            s0 = s0_ref[ci]
            dqa_ref[...] = eb * jnp.dot(dob, s0.astype(BF16), preferred_element_type=F32)
            dk_state = ebl * jnp.dot(vb, dstb, preferred_element_type=F32)
            dka_ref[...] = dk_state
            d_blast = (jnp.sum(k * dk_state, axis=0, keepdims=True)
                       + jnp.exp(blast) * jnp.sum(dst * s0, axis=0, keepdims=True))
            dva_ref[...] = lax.dot_general(khat, dstb, NT, preferred_element_type=F32)
            dst_next = dst * jnp.exp(blast) + lax.dot_general(dob, (q * eb).astype(BF16), TN,
                                                              preferred_element_type=F32)
            pm = lax.dot_general(dob, vb, NT, preferred_element_type=F32)
            for i in range(nsub):
                blk = slice(sub * i, sub * (i + 1))
                qi, ki, vi, bi, doi = q[blk], k[blk], v[blk], b[blk], do[blk]
                dqi = dqa_ref[blk, :]
                if i > 0:
                    qfac, kfac, kt = _sub_factors(b, k, i, sub, trim=False)
                    qt = (qi * qfac).astype(BF16)
                    att = lax.dot_general(qt, kt, NT, preferred_element_type=F32).astype(BF16)
                    pmi = pm[blk, :].astype(BF16)
                    dva_ref[...] += lax.dot_general(att, doi.astype(BF16), TN, preferred_element_type=F32)
                    dqi = dqi + qfac * jnp.dot(pmi, kt, preferred_element_type=F32)
                    dka_ref[...] += kfac * lax.dot_general(pmi, qt, TN, preferred_element_type=F32)
                dqa_ref[blk, :] = dqi
                srow = lax.broadcasted_iota(jnp.int32, (sub, RNN_HD), 0)
                dki = jnp.zeros((sub, RNN_HD), F32)
                dvi = jnp.zeros((sub, RNN_HD), F32)
                for tq in range(sub):
                    qt, dot_ = qi[tq:tq + 1, :], doi[tq:tq + 1, :]
                    e = jnp.where(srow <= tq, jnp.exp(bi[tq:tq + 1, :] - bi), 0.0)
                    ke = ki * e
                    p = jnp.sum(vi * dot_, axis=1, keepdims=True)
                    a = jnp.sum(ke * qt, axis=1, keepdims=True)
                    dki = dki + p * (qt * e)
                    dvi = dvi + a * dot_
                    row = slice(sub * i + tq, sub * i + tq + 1)
                    dqa_ref[row, :] += jnp.sum(p * ke, axis=0, keepdims=True)
                dka_ref[blk, :] += dki
                dva_ref[blk, :] += dvi

            dq = dqa_ref[...]
            dk = dka_ref[...]
            lastrow = lax.broadcasted_iota(jnp.int32, (CHUNK, RNN_HD), 0) == CHUNK - 1
            dlf = _rev_cumsum_rows(q * dq - k * dk + jnp.where(lastrow, d_blast, 0.0))
            dff = dlf / f - dk
            df_ref[rows, :] = (dff * (1.0 - lb) * sg * (1.0 - sg)).astype(BF16)
            dlb_ref[...] += jnp.sum(dff * (1.0 - sg), axis=0, keepdims=True)
            dq_ref[rows, :] = (dq * (sq * (1.0 + qr * (1.0 - sq)))).astype(BF16)
            di_ref[rows, :] = dva_ref[...].astype(BF16)
            return dst_next

        dst = dst_ref[...]
        for ci in reversed(range(nch)):
            dst = chunk(ci, dst)
        dst_ref[...] = dst

    def col(base):
        return pl.BlockSpec((tb, RNN_HD), lambda h, c: (ntb - 1 - c, base + h))

    outc = pl.BlockSpec((tb, RNN_HD), lambda h, c: (ntb - 1 - c, h))
    hb = ATTN_W // RNN_HD
    res, xres = _call(
        body, name=name, grid=(N_RNN, ntb),
        in_specs=[col(qb), col(fb), col(ib), col(gb),
                  pl.BlockSpec((2, RNN_HD), lambda h, c: (0, h)), pl.BlockSpec((1, RNN_HD), lambda h, c: (0, 0)),
                  outc,
                  pl.BlockSpec((None, nch, RNN_HD, RNN_HD), lambda h, c: (h, ntb - 1 - c, 0, 0)),
                  pl.BlockSpec((tb, RNN_HD), lambda h, c: (ntb - 1 - c, hb + h))],
        out_specs=[outc, outc, outc, outc,
                   pl.BlockSpec((1, RNN_HD), lambda h, c: (0, h)),
                   pl.BlockSpec((None, 1, RNN_HD), lambda h, c: (h, 0, 0))],
        out_shape=[jax.ShapeDtypeStruct((t, RNN_W), BF16)] * 4
        + [jax.ShapeDtypeStruct((1, RNN_W), F32), jax.ShapeDtypeStruct((N_RNN, 1, RNN_HD), F32)],
        scratch_shapes=[pltpu.VMEM((RNN_HD, RNN_HD), F32),
                        pltpu.VMEM((nch, CHUNK, RNN_HD), F32), pltpu.VMEM((nch, CHUNK, RNN_HD), F32),
                        pltpu.VMEM((nch, CHUNK, RNN_HD), F32)],
        args=[proj, proj, proj, proj, lb_logits, norm_gain, o_pre, s0, dcat],
        semantics=("parallel", "arbitrary"), exchanges=exchanges)
    return (*res, xres) if exchanges else res


def _cast_slots(w, where, *, name):
    _, rows, cols = w.shape
    rh = rows // 2
    tr = _row_tile(rh, cols)
    nh = rh // tr

    def body(wh_ref, w_ref, o_ref):
        o_ref[...] = w_ref[...].astype(BF16)

    return pl.pallas_call(
        body, name=name,
        grid_spec=pltpu.PrefetchScalarGridSpec(
            num_scalar_prefetch=1, grid=(2, nh),
            in_specs=[pl.BlockSpec((None, tr, cols), lambda h, i, wh: (0, h * nh + i, 0))],
            out_specs=pl.BlockSpec((None, tr, cols), lambda h, i, wh: (2 * wh[0] + h, i, 0))),
        out_shape=jax.ShapeDtypeStruct((8, rh, cols), BF16),
        compiler_params=_params(("parallel", "parallel")),
    )(where, w)


def _row_tile(rows, cols, budget=1 << 20):
    tr = rows
    while tr * cols > budget and tr % 16 == 0:
        tr //= 2
    return tr


def _half_spec(g, tr, halves_last, slab):
    if halves_last:
        return pl.BlockSpec((None, tr, g.shape[2] // 2), lambda *a: (slab(*a), a[-2], a[-1][1]))
    return pl.BlockSpec((None, None, tr, g.shape[3]), lambda *a: (slab(*a), a[-1][1], a[-2], 0))


def _pair_sum(g, sib, where, *, name, halves_last=False):
    rh, cols = sib.shape[1:]
    tr = _row_tile(rh, cols)

    def body(w_ref, g_ref, s_ref, o_ref):
        o_ref[...] = (g_ref[...] + s_ref[...]).astype(BF16)

    return pl.pallas_call(
        body, name=name,
        grid_spec=pltpu.PrefetchScalarGridSpec(
            num_scalar_prefetch=1, grid=(4, rh // tr),
            in_specs=[_half_spec(g, tr, halves_last, lambda s, i, w: s),
                      pl.BlockSpec((None, tr, cols), lambda s, i, w: (s, i, 0))],
            out_specs=pl.BlockSpec((None, tr, cols), lambda s, i, w: (s, i, 0))),
        out_shape=jax.ShapeDtypeStruct((4, rh, cols), BF16),
        compiler_params=_params(("parallel", "parallel")),
    )(where, g, sib)


def _final_half(g, sib, recv, where, *, name, halves_last=False):
    rh, cols = sib.shape[1:]
    tr = _row_tile(rh, cols)

    def body(w_ref, g_ref, s_ref, r_ref, o_ref):
        acc = g_ref[...] + s_ref[...]
        for j in range(3):
            acc = acc + r_ref[j].astype(F32)
        o_ref[...] = acc

    return pl.pallas_call(
        body, name=name,
        grid_spec=pltpu.PrefetchScalarGridSpec(
            num_scalar_prefetch=1, grid=(rh // tr,),
            in_specs=[_half_spec(g, tr, halves_last, lambda i, w: w[0]),
                      pl.BlockSpec((None, tr, cols), lambda i, w: (w[0], i, 0)),
                      pl.BlockSpec((3, tr, cols), lambda i, w: (0, i, 0))],
            out_specs=pl.BlockSpec((tr, cols), lambda i, w: (i, 0))),
        out_shape=jax.ShapeDtypeStruct((rh, cols), F32),
        compiler_params=_params(("parallel",)),
    )(where, g, sib, recv)


def _adamw_math(w, g, m, v):
    m = ADAM_B1 * m + (1.0 - ADAM_B1) * g
    v = ADAM_B2 * v + (1.0 - ADAM_B2) * (g * g)
    m_hat = m / (1.0 - ADAM_B1 ** ADAM_STEP)
    v_hat = v / (1.0 - ADAM_B2 ** ADAM_STEP)
    delta = -ADAM_LR * (m_hat / (jnp.sqrt(v_hat) + ADAM_EPS) + ADAM_WD * w)
    return delta, m, v


def _adamw(w, mine, theirs, m, v, where, *, name, halves_last=False):
    _, rows, cols = w.shape
    if halves_last:
        cols //= 2
        tr = _row_tile(rows, cols, budget=1 << 19)
        grid = (rows // tr, 2)
        blk = pl.BlockSpec((None, tr, cols), lambda i, h, wh: (0, i, h))
        mine_spec = theirs_spec = pl.BlockSpec((tr, cols), lambda i, h, wh: (i, 0))
        which = lambda: pl.program_id(1)
    else:
        tr = _row_tile(rows // 2, cols, budget=1 << 19)
        nh = rows // 2 // tr
        grid = (rows // tr,)
        blk = pl.BlockSpec((None, tr, cols), lambda i, wh: (0, i, 0))
        mine_spec = pl.BlockSpec((tr, cols), lambda i, wh: (jnp.where(i // nh == wh[1], i % nh, 0), 0))
        theirs_spec = pl.BlockSpec((tr, cols), lambda i, wh: (jnp.where(i // nh == wh[1], 0, i % nh), 0))
        which = lambda: pl.program_id(0) // nh

    def body(wh_ref, w_ref, a_ref, b_ref, m_ref, v_ref, g_ref, d_ref, nm_ref, nv_ref):
        g = jnp.where(which() == wh_ref[1], a_ref[...], b_ref[...])
        d, nm, nv = _adamw_math(w_ref[...], g, m_ref[...], v_ref[...])
        g_ref[...] = g
        d_ref[...] = d
        nm_ref[...] = nm
        nv_ref[...] = nv

    rows, cols = w.shape[1:]
    return pl.pallas_call(
        body, name=name,
        grid_spec=pltpu.PrefetchScalarGridSpec(
            num_scalar_prefetch=1, grid=grid,
            in_specs=[blk, mine_spec, theirs_spec, blk, blk], out_specs=[blk] * 4),
        out_shape=[jax.ShapeDtypeStruct((1, rows, cols), F32)] * 4,
        compiler_params=_params(("parallel",) * len(grid)),
    )(where, w, mine, theirs, m, v)


SEG_LOSS = 0
SEG_SINK = 128
SEG_AGAIN = 256
SEG_L0 = SEG_AGAIN + ATTN_W
SEG_L1 = SEG_L0 + RNN_W
SEG_RGAIN = SEG_L1 + RNN_W
SEG_G = SEG_RGAIN + 128
N_PACK = SEG_G + 4 * D_MODEL


def _pack(sinks, again, l0, l1, rgain, gains, loss=None):
    z = lambda k: jnp.zeros((1, k), F32)
    first = z(128) if loss is None else loss
    return jnp.concatenate([first, sinks, z(128 - N_Q), again, l0, l1, rgain] + list(gains), axis=1)


def _small_reduce_adamw(part, w, m, v, *, name):
    def body(p_ref, w_ref, m_ref, v_ref, g_ref, d_ref, nm_ref, nv_ref, buf_ref, send_sems, recv_sems):
        x, y, c = _place()
        me = 4 * x + 2 * y + c
        copies = []
        for k in range(1, 8):
            dx, dy, dc = (k >> 2) & 1, (k >> 1) & 1, k & 1
            to = (x ^ dx, y ^ dy, c ^ dc)
            cp = pltpu.make_async_remote_copy(
                src_ref=p_ref, dst_ref=buf_ref.at[me],
                send_sem=send_sems.at[k - 1], recv_sem=recv_sems.at[k - 1],
                device_id=to, device_id_type=MESH)
            cp.start()
            copies.append(cp)
        buf_ref[me] = p_ref[...]
        for cp in copies:
            cp.wait()
        tot = buf_ref[0]
        for j in range(1, 8):
            tot = tot + buf_ref[j]
        g_ref[...] = tot
        l0 = w_ref[:, SEG_L0:SEG_L0 + RNN_W]
        l1 = w_ref[:, SEG_L1:SEG_L1 + RNN_W]
        mx = jnp.maximum(l0, l1)
        e0 = jnp.exp(l0 - mx)
        e1 = jnp.exp(l1 - mx)
        lb = e0 / (e0 + e1)
        gl0 = tot[:, SEG_L0:SEG_L0 + RNN_W] * lb * (1.0 - lb)
        g_ref[:, SEG_L0:SEG_L0 + RNN_W] = gl0
        g_ref[:, SEG_L1:SEG_L1 + RNN_W] = -gl0
        d, nm, nv = _adamw_math(w_ref[...], g_ref[...], m_ref[...], v_ref[...])
        d_ref[...] = d
        nm_ref[...] = nm
        nv_ref[...] = nv

    vm = pl.BlockSpec(memory_space=pltpu.VMEM)
    return pl.pallas_call(
        body, name=name,
        in_specs=[vm] * 4, out_specs=[vm] * 4,
        out_shape=[jax.ShapeDtypeStruct((1, N_PACK), F32)] * 4,
        scratch_shapes=[pltpu.VMEM((8, 1, N_PACK), F32), pltpu.SemaphoreType.DMA((7,)),
                        pltpu.SemaphoreType.DMA((7,))],
    )(part, w, m, v)


def _layer_grads(xs, tgt, bufs, where, sinks, again, lb_logits, rgain,
                 g_mix_pre, g_mix_post, g_mlp_pre, g_mlp_post):
    tm = 512
    b_in, b_out, b_up, b_dn = bufs

    shard = IN_W // N_CHIPS
    h1, b_in = _rms_cast_gather(xs, g_mix_pre, b_in, tm=tm, name="h1_norm_gather_w_in")
    w_in_t = b_in.reshape(IN_W, D_MODEL)
    proj, ((b_out, b_up),) = _mm(
        h1, w_in_t, tm=1024, tn=768, tk=D_MODEL, out_dtype=F32, w_layout="nk", name="in_proj",
        exchanges=[_x_gather([b_out, b_up], ici=[(0, 256), (0, 336)])])
    attn, lse, ((b_out, b_up),) = _swa_fwd(
        proj, sinks, name="swa_fwd",
        exchanges=[_x_gather([b_out, b_up], ici=[None, (336, 320)], d2d=[(0, 256), None])])
    w_out = b_out.reshape(D_MODEL, D_MODEL)
    o_pre, rnn, s0, ((b_up, b_dn),) = _hgrn_fwd(
        proj, lb_logits, rgain, tb=512, name="hgrn_fwd",
        exchanges=[_x_gather([b_up, b_dn], ici=[(656, 368), (0, 400)])])
    cat = _mix_cat(attn, rnn, again, tm=tm, name="mix_cat")
    mixed, ((b_up, b_dn),) = _mm(
        cat, w_out, tm=1024, tn=1024, tk=D_MODEL, out_dtype=BF16, name="out_proj",
        exchanges=[_x_gather([b_up, b_dn], ici=[None, (400, 240)], d2d=[(0, 1024), (0, 400)])])
    w_up4 = b_up.reshape(N_CHIPS, D_MODEL, D_FF // N_CHIPS)
    x1, h2, ((b_dn,),) = _post_norm_res(
        mixed, g_mix_post, xs, g_mlp_pre, tm=tm, name="mix_post",
        exchanges=[_x_gather([b_dn], d2d=[(400, 240)])])
    u, ((b_dn,),) = _mm(h2, w_up4, tm=1024, tn=1024, tk=D_MODEL, out_dtype=BF16, relu=True, w_layout="skn",
                        name="mlp_up", exchanges=[_x_gather([b_dn], ici=[(640, 384)], cross=[(640, 384)])])
    w_dn = b_dn.reshape(D_FF, D_MODEL)
    yv = _mm(u, w_dn, tm=1024, tn=1024, tk=2048, out_dtype=BF16, a_square=True, name="mlp_down")
    dy, dx2, loss_row, dg_mlp_post = _loss_head(yv, g_mlp_post, x1, tgt, tm=tm, name="loss_head")

    def halved(g):
        return g.reshape(N_CHIPS, 2, g.shape[1] // 2, g.shape[2])
    du = _mm(dy, w_dn, tm=1024, tn=1024, tk=D_MODEL, out_dtype=BF16, mul2=u, w_layout="nk", name="mlp_down_bwd")
    g_dn = halved(_mm_tn(u, dy, tm=1024, tn=1024, tt=2048, a_square=True, name="w_down_grad")
                  .reshape(N_CHIPS, D_FF // N_CHIPS, D_MODEL))
    d_w_up, ((sib_dn,),) = _mm_tn(h2, du, tm=1024, tn=1024, tt=2048, n_split=N_CHIPS, name="w_up_grad",
                                  exchanges=[_x_pair([g_dn])])
    g_up = halved(d_w_up)
    wire_dn = _pair_sum(g_dn, sib_dn, where, name="pair_sum_w_down")
    dh2, ((recv_dn,), (sib_up,)) = _mm(du, w_up4, tm=1024, tn=1024, tk=2048, out_dtype=BF16, w_layout="snk", name="mlp_up_bwd",
                                       exchanges=[_x_chip([wire_dn], rows=[(0, 704)]), _x_pair([g_up])])
    wire_up = _pair_sum(g_up, sib_up, where, name="pair_sum_w_up")
    dx1, dg_mlp_pre, ((recv_dn,),) = _rms_bwd(dh2, x1, g_mlp_pre, dx2, tm=tm, out_dtype=F32, name="mlp_pre_bwd",
                                              exchanges=[_x_chip([wire_dn], rows=[(704, 224)], into=[recv_dn])])
    dmixed, dg_mix_post = _rms_bwd(dx1, mixed, g_mix_post, None, tm=tm, out_dtype=BF16, name="mix_post_bwd")
    d_w_out, ((recv_dn,),) = _mm_tn(cat, dmixed, tm=1024, tn=1024, tt=2048, name="w_out_grad",
                                    exchanges=[_x_chip([wire_dn], rows=[(928, 96)], into=[recv_dn])])
    fin_dn = _final_half(g_dn, sib_dn, recv_dn, where, name="final_half_w_down")
    g_out = halved(d_w_out.reshape(N_CHIPS, D_MODEL // N_CHIPS, D_MODEL))
    dcat, ((sib_out,), (oth_dn,)) = _mm(dmixed, w_out, tm=1024, tn=1024, tk=D_MODEL, out_dtype=BF16, w_layout="nk",
                                        name="out_proj_bwd", exchanges=[_x_pair([g_out]), _x_share([fin_dn])])
    wire_out = _pair_sum(g_out, sib_out, where, name="pair_sum_w_out")
    dattn, dg_again = _rms_bwd(dcat, attn, again, None, tm=tm, out_dtype=BF16, name="attn_norm_bwd")
    dq_a, dkv, dsinks, ((recv_out,), (recv_up,)) = _swa_bwd(
        proj, sinks, dattn, lse, name="swa_bwd",
        exchanges=[_x_chip([wire_out]), _x_chip([wire_up], rows=[(0, 320)])])
    dq_r, df_r, di_r, dg_r, dlb, dgain_h, ((recv_up,),) = _hgrn_bwd(
        proj, lb_logits, rgain, o_pre, s0, dcat, tb=512, name="hgrn_bwd",
        exchanges=[_x_chip([wire_up], rows=[(320, 704)], into=[recv_up])])
    fin_up = _final_half(g_up, sib_up, recv_up, where, name="final_half_w_up")
    fin_out = _final_half(g_out, sib_out, recv_out, where, name="final_half_w_out")
    dproj = jnp.concatenate([dq_a, dkv, dq_r, df_r, di_r, dg_r], axis=1)
    piece_cols = D_MODEL // 4

    def w_in_piece(pc, exchanges):
        d, xres = _mm_tn(dproj, h1, tm=896, tn=2 * piece_cols, tt=2048, b_blocks=(pc, pc + 2),
                         name="w_in_grad_%d" % pc, exchanges=exchanges)
        return d.reshape(N_CHIPS, shard, 2 * piece_cols), xres

    g_in0, ((oth_up, oth_out),) = w_in_piece(0, [_x_share([fin_up, fin_out])])
    g_in1, ((sib_in0,),) = w_in_piece(1, [_x_pair([g_in0], halves_last=True)])
    wire_in0 = _pair_sum(g_in0, sib_in0, where, name="pair_sum_w_in_0", halves_last=True)
    dh1, ((recv_in0,), (sib_in1,)) = _mm(
        dproj, w_in_t, tm=1024, tn=1024, tk=2688, out_dtype=BF16, m_blocks=(0, 2), name="in_proj_bwd_0",
        exchanges=[_x_chip([wire_in0]), _x_pair([g_in1], halves_last=True)])
    wire_in1 = _pair_sum(g_in1, sib_in1, where, name="pair_sum_w_in_1", halves_last=True)
    dh1, ((recv_in1,),) = _mm(
        dproj, w_in_t, tm=1024, tn=1024, tk=2688, out_dtype=BF16, m_blocks=(2, 2), out_into=dh1,
        name="in_proj_bwd_1", exchanges=[_x_chip([wire_in1])])
    gx, dg_mix_pre = _rms_bwd(dh1, xs, g_mix_pre, dx1, tm=tm, out_dtype=F32, name="mix_pre_bwd")
    fin_in0 = _final_half(g_in0, sib_in0, recv_in0, where, name="final_half_w_in_0", halves_last=True)
    fin_in1 = _final_half(g_in1, sib_in1, recv_in1, where, name="final_half_w_in_1", halves_last=True)
    oth_in0, oth_in1 = _run_exchange(_x_share([fin_in0, fin_in1]), name="share_w_in")
    fin_in = jnp.concatenate([fin_in0, fin_in1], axis=1)
    oth_in = jnp.concatenate([oth_in0, oth_in1], axis=1)

    big = [(fin_in, oth_in), (fin_out, oth_out), (fin_up, oth_up), (fin_dn, oth_dn)]
    drgain = jnp.sum(dgain_h, axis=0)
    small = _pack(jnp.sum(dsinks, axis=1)[None, :], dg_again, dlb, jnp.zeros_like(dlb), drgain,
                  [dg_mix_pre, dg_mix_post, dg_mlp_pre, dg_mlp_post], loss=loss_row)
    return gx, big, small


def kernel(x, w_in, attn_sinks, attn_out_gain, rnn_lb_logits, rnn_norm_gain, w_out, mix_pre_gain, mix_post_gain, mlp_pre_gain, mlp_post_gain, w_up, w_down, loss_target, m_w_in, m_attn_sinks, m_attn_out_gain, m_rnn_lb_logits, m_rnn_norm_gain, m_w_out, m_mix_pre_gain, m_mix_post_gain, m_mlp_pre_gain, m_mlp_post_gain, m_w_up, m_w_down, v_w_in, v_attn_sinks, v_attn_out_gain, v_rnn_lb_logits, v_rnn_norm_gain, v_w_out, v_mix_pre_gain, v_mix_post_gain, v_mlp_pre_gain, v_mlp_post_gain, v_w_up, v_w_down):
    ax, ay, ac = _place()
    where = jnp.stack([2 * ax + ay, ac]).astype(jnp.int32)
    t = lambda a: jnp.swapaxes(a, 1, 2)
    big_w = [t(w_in), w_out, w_up, w_down]
    big_m = [t(m_w_in), m_w_out, m_w_up, m_w_down]
    big_v = [t(v_w_in), v_w_out, v_w_up, v_w_down]

    names = ["w_in", "w_out", "w_up", "w_down"]
    bufs = [_cast_slots(w, where, name="cast_" + nm) for w, nm in zip(big_w, names)]
    gx, big_g, small_part = _layer_grads(
        x[0], loss_target[0], bufs, where, attn_sinks, attn_out_gain, rnn_lb_logits, rnn_norm_gain,
        mix_pre_gain, mix_post_gain, mlp_pre_gain, mlp_post_gain)

    grads, deltas, new_m, new_v = [], [], [], []
    for (f, o), w, m, v, nm in zip(big_g, big_w, big_m, big_v, names):
        res = _adamw(w, f, o, m, v, where, name="adamw_" + nm, halves_last=(nm == "w_in"))
        if nm == "w_in":
            res = [t(r) for r in res]
        g, d, nm_, nv_ = res
        grads.append(g)
        deltas.append(d)
        new_m.append(nm_)
        new_v.append(nv_)

    def pack_params(sinks, again, logits, rgain, gains):
        return _pack(sinks, again, logits[0:1], logits[1:2], rgain, gains)

    pw = pack_params(attn_sinks, attn_out_gain, rnn_lb_logits, rnn_norm_gain,
                     [mix_pre_gain, mix_post_gain, mlp_pre_gain, mlp_post_gain])
    pm = pack_params(m_attn_sinks, m_attn_out_gain, m_rnn_lb_logits, m_rnn_norm_gain,
                     [m_mix_pre_gain, m_mix_post_gain, m_mlp_pre_gain, m_mlp_post_gain])
    pv = pack_params(v_attn_sinks, v_attn_out_gain, v_rnn_lb_logits, v_rnn_norm_gain,
                     [v_mix_pre_gain, v_mix_post_gain, v_mlp_pre_gain, v_mlp_post_gain])
    packs = _small_reduce_adamw(small_part, pw, pm, pv, name="small_reduce_adamw")

    def unpack(p):
        seg = lambda o, k: p[:, o:o + k]
        logits = jnp.concatenate([seg(SEG_L0, RNN_W), seg(SEG_L1, RNN_W)], axis=0)
        gains = [seg(SEG_G + i * D_MODEL, D_MODEL) for i in range(4)]
        return dict(sinks=seg(SEG_SINK, N_Q), again=seg(SEG_AGAIN, ATTN_W), logits=logits,
                    rgain=seg(SEG_RGAIN, RNN_HD), gains=gains)

    def order(small, big):
        return [big[0], small["sinks"], small["again"], small["logits"], small["rgain"], big[1],
                *small["gains"], big[2], big[3]]

    loss = packs[0][0, 0]
    outs = [loss, gx[None]]
    for p, b in zip(packs, [grads, deltas, new_m, new_v]):
        outs += order(unpack(p), b)
    return tuple(outs)
```

```python
import functools

import jax
import jax.numpy as jnp
from jax import lax
from jax.experimental import pallas as pl
from jax.experimental.pallas import tpu as pltpu

F32 = jnp.float32
BF16 = jnp.bfloat16
MESH = pl.DeviceIdType.MESH

EPS = 1e-6
D_MODEL = 2048
ATTN_W = 1024
HEAD_DIM = 64
N_Q = 16
N_KV = 2
GROUP = 8
BLK = 128
RNN_W = 1024
RNN_HD = 128
N_RNN = 8
CHUNK = 64
SUB_FWD = 16
SUB_BWD = 8
D_FF = 8192
IN_W = 5376
N_CHIPS = 4
KV_COL = ATTN_W
QR_COL = ATTN_W + 2 * 128
FR_COL = QR_COL + RNN_W
IR_COL = FR_COL + RNN_W
GR_COL = IR_COL + RNN_W

ADAM_LR = 0.001
ADAM_B1 = 0.9
ADAM_B2 = 0.999
ADAM_EPS = 1e-08
ADAM_WD = 0.01
ADAM_STEP = 10

VMEM_LIMIT = 48 * 1024 * 1024

NT = (((1,), (1,)), ((), ()))
TN = (((0,), (0,)), ((), ()))


def _params(sem=None):
    return pltpu.CompilerParams(dimension_semantics=sem, vmem_limit_bytes=VMEM_LIMIT)


def _sigmoid(x):
    return 1.0 / (1.0 + jnp.exp(-x))


ANY = pl.BlockSpec(memory_space=pl.ANY)


def _place():
    return lax.axis_index("x"), lax.axis_index("y"), lax.axis_index("c")


def _other_chips(x, y):
    return [(1 - x, y), (x, 1 - y), (1 - x, 1 - y)]


class _Exchange:
    def __init__(self, srcs, outs, ncopy, build, aliases=None):
        self.srcs, self.outs, self.ncopy, self.build = list(srcs), list(outs), ncopy, build
        self.aliases = aliases or {}


def _remote(src, dst, send_sems, recv_sems, k, to):
    return pltpu.make_async_remote_copy(src_ref=src, dst_ref=dst, send_sem=send_sems.at[k],
                                        recv_sem=recv_sems.at[k], device_id=to, device_id_type=MESH)


def _call(body, *, name, grid, in_specs, out_specs, out_shape, args, scratch_shapes=(), semantics=None,
          exchanges=(), into=None):
    in_specs, out_specs, out_shape = list(in_specs), list(out_specs), list(out_shape)
    scratch_shapes = list(scratch_shapes)
    ni, no, ns = len(in_specs), len(out_specs), len(scratch_shapes)
    xsrc = [s for x in exchanges for s in x.srcs]
    xout = [o for x in exchanges for o in x.outs]
    into = into or {}
    xsrc += [into[k] for k in sorted(into)]
    nxi, nxo = len(xsrc), len(xout)
    aliases = {nxi - len(into) + ni + q: k for q, k in enumerate(sorted(into))}
    a0 = b0 = 0
    for x in exchanges:
        for si, oi in x.aliases.items():
            aliases[ni + a0 + si] = no + b0 + oi
        a0 += len(x.srcs)
        b0 += len(x.outs)
    sems = []
    for x in exchanges:
        sems += [pltpu.SemaphoreType.DMA((x.ncopy,)), pltpu.SemaphoreType.DMA((x.ncopy,))]

    def wrapped(*refs):
        ins, xi = refs[:ni], refs[ni:ni + nxi]
        outs, xo = refs[ni + nxi:ni + nxi + no], refs[ni + nxi + no:ni + nxi + no + nxo]
        rest = refs[ni + nxi + no + nxo:]
        scr, sm = rest[:ns], rest[ns:]

        def copies():
            cps = []
            a = b = 0
            for k, x in enumerate(exchanges):
                cps += x.build(xi[a:a + len(x.srcs)], xo[b:b + len(x.outs)], sm[2 * k], sm[2 * k + 1])
                a += len(x.srcs)
                b += len(x.outs)
            return cps

        def start():
            for cp in copies():
                cp.start()

        def wait():
            for cp in copies():
                cp.wait()

        if not exchanges:
            body(*ins, *outs, *scr)
        elif not grid:
            start()
            body(*ins, *outs, *scr)
            wait()
        else:
            first = last = None
            for ax, g in enumerate(grid):
                f = pl.program_id(ax) == 0
                l = pl.program_id(ax) == g - 1
                first = f if first is None else first & f
                last = l if last is None else last & l
            pl.when(first)(start)
            body(*ins, *outs, *scr)
            pl.when(last)(wait)

    if exchanges and semantics is not None:
        semantics = ("arbitrary",) * len(grid)
    kwargs = dict(grid=grid) if grid else {}
    res = pl.pallas_call(
        wrapped, name=name,
        in_specs=in_specs + [ANY] * nxi, out_specs=out_specs + [ANY] * nxo,
        out_shape=out_shape + xout, scratch_shapes=scratch_shapes + sems,
        input_output_aliases=aliases,
        compiler_params=_params(semantics), **kwargs,
    )(*args, *xsrc)
    res = list(res)
    mine, theirs = res[:no], res[no:]
    per = []
    b = 0
    for x in exchanges:
        per.append(theirs[b:b + len(x.outs)])
        b += len(x.outs)
    return mine, per


def _run_exchange(x, *, name):
    return _call(lambda: None, name=name, grid=(), in_specs=[], out_specs=[], out_shape=[], args=[],
                 exchanges=[x])[1][0]


def _x_gather(bufs, ici=None, d2d=None, cross=None):
    n = len(bufs)
    plan = [(a, kind, rows[a]) for a in range(n) for kind, rows in (("ici", ici), ("d2d", d2d), ("cross", cross))
            if rows is not None and rows[a] is not None]

    def build(srcs, outs, ss, rs):
        x, y, c = _place()
        cps = []
        for q, (a, kind, rows) in enumerate(plan):
            piece = pl.ds(*rows)
            for j, (px, py) in enumerate(_other_chips(x, y)):
                if kind == "d2d":
                    slot, to = 4 * px + 2 * py + c, (x, y, 1 - c)
                else:
                    slot, to = 4 * x + 2 * y + c, (px, py, c if kind == "ici" else 1 - c)
                cps.append(_remote(srcs[a].at[slot, piece], outs[a].at[slot, piece], ss, rs, 3 * q + j, to))
        return cps

    outs = [jax.ShapeDtypeStruct(b.shape, b.dtype) for b in bufs]
    return _Exchange(bufs, outs, 3 * len(plan), build, aliases={a: a for a in range(n)})


def _x_pair(grads, halves_last=False):
    n = len(grads)

    def build(srcs, outs, ss, rs):
        x, y, c = _place()

        def half(r):
            if not halves_last:
                return r.at[:, 1 - c]
            ch = r.shape[2] // 2
            return r.at[:, :, pl.ds(pl.multiple_of((1 - c) * ch, 128), ch)]

        return [_remote(half(srcs[a]), outs[a], ss, rs, a, (x, y, 1 - c)) for a in range(n)]

    if halves_last:
        outs = [jax.ShapeDtypeStruct(g.shape[:2] + (g.shape[2] // 2,), g.dtype) for g in grads]
    else:
        outs = [jax.ShapeDtypeStruct((4,) + g.shape[2:], g.dtype) for g in grads]
    return _Exchange(grads, outs, n, build)


def _x_chip(wires, rows=None, into=None):
    n = len(wires)
    rows = rows or [(0, w.shape[1]) for w in wires]

    def build(srcs, outs, ss, rs):
        x, y, c = _place()
        cps = []
        for a in range(n):
            piece = pl.ds(*rows[a])
            for j, (px, py) in enumerate(_other_chips(x, y)):
                cps.append(_remote(srcs[a].at[2 * px + py, piece], outs[a].at[j, piece], ss, rs,
                                   3 * a + j, (px, py, c)))
        return cps

    outs = [jax.ShapeDtypeStruct((3,) + w.shape[1:], w.dtype) for w in wires]
    if into is None:
        return _Exchange(wires, outs, 3 * n, build)
    return _Exchange(list(wires) + list(into), outs, 3 * n, build, aliases={n + a: a for a in range(n)})


def _x_share(halves):
    n = len(halves)

    def build(srcs, outs, ss, rs):
        x, y, c = _place()
        return [_remote(srcs[a], outs[a], ss, rs, a, (x, y, 1 - c)) for a in range(n)]

    outs = [jax.ShapeDtypeStruct(h.shape, h.dtype) for h in halves]
    return _Exchange(halves, outs, n, build)


def _mm(a, w, *, tm, tn, tk, out_dtype, name, a_square=False, relu=False, mul2=None, w_layout="kn",
        m_blocks=None, out_into=None, exchanges=()):
    m, k = a.shape
    m_first, m_count = m_blocks or (0, m // tm)
    a_spec = pl.BlockSpec((tm, tk), lambda i, j, kk: (i + m_first, kk))
    if w_layout == "kn":
        n = w.shape[1]
        w_spec = pl.BlockSpec((tk, tn), lambda i, j, kk: (kk, j))
    elif w_layout == "nk":
        n = w.shape[0]
        w_spec = pl.BlockSpec((tn, tk), lambda i, j, kk: (j, kk))
    elif w_layout == "skn":
        n = w.shape[0] * w.shape[2]
        per_n = w.shape[2] // tn
        w_spec = pl.BlockSpec((None, tk, tn), lambda i, j, kk: (j // per_n, kk, j % per_n))
    else:
        assert w_layout == "snk"
        n = w.shape[1]
        per_k = w.shape[2] // tk
        w_spec = pl.BlockSpec((None, tn, tk), lambda i, j, kk: (kk // per_k, j, kk % per_k))
    w_dims = NT if w_layout in ("nk", "snk") else (((1,), (0,)), ((), ()))
    nk = k // tk
    assert m % tm == 0 and n % tn == 0 and k % tk == 0

    def body(*refs):
        if mul2 is not None:
            a_ref, w_ref, e_ref, o_ref, acc_ref = refs
        else:
            a_ref, w_ref, o_ref, acc_ref = refs
            e_ref = None
        kk = pl.program_id(2)
        av = a_ref[...]
        if a_square:
            af = av.astype(F32)
            av = (af * af).astype(BF16)
        part = lax.dot_general(av, w_ref[...], w_dims, preferred_element_type=F32)

        def finish(r):
            if relu:
                r = jnp.maximum(r, 0.0)
            if e_ref is not None:
                r = 2.0 * e_ref[...].astype(F32) * r
            o_ref[...] = r.astype(out_dtype)

        if nk == 1:
            finish(part)
        else:
            @pl.when(kk == 0)
            def _():
                acc_ref[...] = part

            @pl.when(kk > 0)
            def _():
                acc_ref[...] += part

            @pl.when(kk == nk - 1)
            def _():
                finish(acc_ref[...])

    in_specs = [a_spec, w_spec]
    args = [a, w]
    if mul2 is not None:
        in_specs.append(pl.BlockSpec((tm, tn), lambda i, j, kk: (i + m_first, j)))
        args.append(mul2)
    acc_shape = (tm, tn) if nk > 1 else (8, 128)
    (out,), per = _call(
        body, name=name, grid=(m_count, n // tn, nk),
        in_specs=in_specs, out_specs=[pl.BlockSpec((tm, tn), lambda i, j, kk: (i + m_first, j))],
        out_shape=[jax.ShapeDtypeStruct((m, n), out_dtype)], args=args,
        scratch_shapes=[pltpu.VMEM(acc_shape, F32)],
        semantics=("parallel", "parallel", "arbitrary"), exchanges=exchanges,
        into=None if out_into is None else {0: out_into})
    return (out, per) if exchanges else out


def _mm_tn(a, b, *, tm, tn, tt, name, a_square=False, n_split=1, b_blocks=None, exchanges=()):
    t, m = a.shape
    nb = len(b_blocks) if b_blocks else 1
    n = tn if b_blocks else b.shape[1]
    assert t % tt == 0 and m % tm == 0 and n % tn == 0 and (n // n_split) % tn == 0
    per = n // n_split // tn

    def body(a_ref, *refs):
        b_refs, o_ref = refs[:nb], refs[nb]
        ti = pl.program_id(2)
        av = a_ref[...]
        if a_square:
            af = av.astype(F32)
            av = (af * af).astype(BF16)
        bv = b_refs[0][...] if nb == 1 else jnp.concatenate([r[...] for r in b_refs], axis=1)
        part = lax.dot_general(av, bv, TN, preferred_element_type=F32)

        @pl.when(ti == 0)
        def _():
            o_ref[...] = part

        @pl.when(ti > 0)
        def _():
            o_ref[...] += part

    if b_blocks:
        b_specs = [pl.BlockSpec((tt, tn // nb), functools.partial(lambda blk, i, j, ti: (ti, blk), blk))
                   for blk in b_blocks]
    else:
        b_specs = [pl.BlockSpec((tt, tn), lambda i, j, ti: (ti, j))]
    (out,), xres = _call(
        body, name=name, grid=(m // tm, n // tn, t // tt),
        in_specs=[pl.BlockSpec((tt, tm), lambda i, j, ti: (ti, i))] + b_specs,
        out_specs=[pl.BlockSpec((None, tm, tn), lambda i, j, ti: (j // per, i, j % per))],
        out_shape=[jax.ShapeDtypeStruct((n_split, m, n // n_split), F32)], args=[a] + [b] * nb,
        semantics=("parallel", "parallel", "arbitrary"), exchanges=exchanges)
    return (out, xres) if exchanges else out


def _rstd(x):
    return lax.rsqrt(jnp.mean(x * x, axis=-1, keepdims=True) + EPS)


def _rms_cast_gather(x, g, buf, *, tm, name):
    t, d = x.shape
    steps = t // tm

    def body(x_ref, g_ref, b_in, o_ref, b_out, send_sems, recv_sems):
        i = pl.program_id(0)
        xc, yc, c = _place()
        chips = _other_chips(xc, yc)

        def slot(px, py, pc):
            return b_out.at[4 * px + 2 * py + pc]

        def sent(j):
            return _remote(b_in.at[4 * xc + 2 * yc + c], slot(xc, yc, c), send_sems, recv_sems, j, (*chips[j], c))

        def passed(j):
            return _remote(slot(*chips[j], c), slot(*chips[j], c), send_sems, recv_sems, 3 + j, (xc, yc, 1 - c))

        @pl.when(i == 0)
        def _():
            for j in range(3):
                sent(j).start()

        xv = x_ref[...]
        o_ref[...] = (xv * _rstd(xv) * g_ref[...]).astype(BF16)

        @pl.when(i == steps - 1)
        def _():
            for j in range(3):
                sent(j).wait_recv()
                passed(j).start()
            for j in range(3):
                passed(j).wait_recv()
                passed(j).wait_send()
                sent(j).wait_send()

    return pl.pallas_call(
        body, name=name, grid=(steps,),
        in_specs=[pl.BlockSpec((tm, d), lambda i: (i, 0)), pl.BlockSpec((1, d), lambda i: (0, 0)), ANY],
        out_specs=[pl.BlockSpec((tm, d), lambda i: (i, 0)), ANY],
        out_shape=[jax.ShapeDtypeStruct((t, d), BF16), jax.ShapeDtypeStruct(buf.shape, buf.dtype)],
        scratch_shapes=[pltpu.SemaphoreType.DMA((6,)), pltpu.SemaphoreType.DMA((6,))],
        input_output_aliases={2: 1},
        compiler_params=_params(("arbitrary",)),
    )(x, g, buf)


def _mix_cat(attn, rnn, gain, *, tm, name):
    t = attn.shape[0]

    def body(a_ref, r_ref, g_ref, o_ref):
        av = a_ref[...]
        o_ref[:, :ATTN_W] = (av * _rstd(av) * g_ref[...]).astype(BF16)
        o_ref[:, ATTN_W:] = r_ref[...].astype(BF16)

    return pl.pallas_call(
        body, name=name, grid=(t // tm,),
        in_specs=[pl.BlockSpec((tm, ATTN_W), lambda i: (i, 0)), pl.BlockSpec((tm, RNN_W), lambda i: (i, 0)),
                  pl.BlockSpec((1, ATTN_W), lambda i: (0, 0))],
        out_specs=pl.BlockSpec((tm, D_MODEL), lambda i: (i, 0)),
        out_shape=jax.ShapeDtypeStruct((t, D_MODEL), BF16),
        compiler_params=_params(("parallel",)),
    )(attn, rnn, gain)


def _post_norm_res(mixed, g_post, res, g_next, *, tm, name, exchanges=()):
    t, d = mixed.shape

    def body(m_ref, gp_ref, r_ref, gn_ref, x1_ref, h2_ref):
        mv = m_ref[...].astype(F32)
        x1 = r_ref[...] + mv * _rstd(mv) * gp_ref[...]
        x1_ref[...] = x1
        h2_ref[...] = (x1 * _rstd(x1) * gn_ref[...]).astype(BF16)

    row = pl.BlockSpec((tm, d), lambda i: (i, 0))
    vec = pl.BlockSpec((1, d), lambda i: (0, 0))
    res_, xres = _call(
        body, name=name, grid=(t // tm,),
        in_specs=[row, vec, row, vec], out_specs=[row, row],
        out_shape=[jax.ShapeDtypeStruct((t, d), F32), jax.ShapeDtypeStruct((t, d), BF16)],
        args=[mixed, g_post, res, g_next], semantics=("parallel",), exchanges=exchanges)
    return (*res_, xres) if exchanges else res_


def _rms_bwd(dyn, xin, g, res, *, tm, out_dtype, name, col_block=0, exchanges=()):
    t, d = xin.shape

    def body(*refs):
        if res is not None:
            dy_ref, x_ref, g_ref, r_ref, dx_ref, dg_ref = refs
        else:
            dy_ref, x_ref, g_ref, dx_ref, dg_ref = refs
        i = pl.program_id(0)
        xv = x_ref[...].astype(F32)
        dy = dy_ref[...].astype(F32)
        r = _rstd(xv)
        xh = xv * r
        part = jnp.sum(dy * xh, axis=0, keepdims=True)

        @pl.when(i == 0)
        def _():
            dg_ref[...] = part

        @pl.when(i > 0)
        def _():
            dg_ref[...] += part

        tt = dy * g_ref[...]
        dx = r * (tt - xh * jnp.mean(tt * xh, axis=-1, keepdims=True))
        if res is not None:
            dx = dx + r_ref[...]
        dx_ref[...] = dx.astype(out_dtype)

    row = pl.BlockSpec((tm, d), lambda i: (i, 0))
    vec = pl.BlockSpec((1, d), lambda i: (0, 0))
    in_specs = [pl.BlockSpec((tm, d), lambda i: (i, col_block)), row, vec]
    args = [dyn, xin, g]
    if res is not None:
        in_specs.append(row)
        args.append(res)
    res, xres = _call(
        body, name=name, grid=(t // tm,),
        in_specs=in_specs, out_specs=[row, vec],
        out_shape=[jax.ShapeDtypeStruct((t, d), out_dtype), jax.ShapeDtypeStruct((1, d), F32)], args=args,
        semantics=("arbitrary",), exchanges=exchanges)
    return (*res, xres) if exchanges else res


def _loss_head(y, g_post, x1, target, *, tm, name):
    t, d = y.shape

    def body(y_ref, g_ref, x1_ref, t_ref, dy_ref, dx2_ref, loss_ref, dg_ref):
        i = pl.program_id(0)
        yv = y_ref[...].astype(F32)
        r = _rstd(yv)
        yh = yv * r
        gv = g_ref[...]
        err = x1_ref[...] + yh * gv - t_ref[...]
        lpart = 0.5 * jnp.sum(jnp.mean(err * err, axis=-1, keepdims=True), axis=0, keepdims=True)
        dx2 = err * (1.0 / d)
        dgp = jnp.sum(dx2 * yh, axis=0, keepdims=True)
        lane = lax.broadcasted_iota(jnp.int32, (1, 128), 1)
        lrow = jnp.where(lane == 0, lpart, 0.0)

        @pl.when(i == 0)
        def _():
            dg_ref[...] = dgp
            loss_ref[...] = lrow

        @pl.when(i > 0)
        def _():
            dg_ref[...] += dgp
            loss_ref[...] += lrow

        tt = dx2 * gv
        dy_ref[...] = (r * (tt - yh * jnp.mean(tt * yh, axis=-1, keepdims=True))).astype(BF16)
        dx2_ref[...] = dx2

    row = pl.BlockSpec((tm, d), lambda i: (i, 0))
    vec = pl.BlockSpec((1, d), lambda i: (0, 0))
    return pl.pallas_call(
        body, name=name, grid=(t // tm,),
        in_specs=[row, vec, row, row],
        out_specs=[row, row, pl.BlockSpec((1, 128), lambda i: (0, 0)), vec],
        out_shape=[jax.ShapeDtypeStruct((t, d), BF16), jax.ShapeDtypeStruct((t, d), F32),
                   jax.ShapeDtypeStruct((1, 128), F32), jax.ShapeDtypeStruct((1, d), F32)],
        compiler_params=_params(("arbitrary",)),
    )(y, g_post, x1, target)


def _alibi_slope(h):
    return 2.0 ** (-8.0 * (h + 1) / N_Q)


PAIR = 2 * HEAD_DIM
N_PAIRS = N_Q // 2
PAIRS_PER_KV = GROUP // 2
SMEM = pl.BlockSpec(memory_space=pltpu.SMEM)


def _swa_mask(n):
    key = lax.broadcasted_iota(jnp.int32, (2 * BLK, BLK), 0)
    qry = lax.broadcasted_iota(jnp.int32, (2 * BLK, BLK), 1)
    dist = qry + BLK - key
    valid = (dist >= 0) & (dist < BLK) & ((key >= BLK) | (n > 0))
    return valid, dist.astype(F32)


def _block_diag(kvp_ref, kvc_ref, off):
    a = jnp.concatenate([kvp_ref[:, off:off + HEAD_DIM], kvc_ref[:, off:off + HEAD_DIM]], axis=0).astype(BF16)
    z = jnp.zeros_like(a)
    return jnp.concatenate([jnp.concatenate([a, z], axis=1), jnp.concatenate([z, a], axis=1)], axis=0)


def _swa_scores(s2, e, hh, valid, distf):
    s = s2[2 * BLK * e:2 * BLK * (e + 1)] * (HEAD_DIM ** -0.5) - _alibi_slope(hh) * distf
    return jnp.where(valid, s, -1e30)


def _swa_fwd(proj, sinks, *, name, exchanges=()):
    t = proj.shape[0]
    nb = t // BLK
    kvb = KV_COL // (2 * 128)

    def body(sink_ref, q_ref, kvc_ref, kvp_ref, o_ref, lse_ref):
        n = pl.program_id(0)
        valid, distf = _swa_mask(n)
        for kvh in range(N_KV):
            k2 = _block_diag(kvp_ref, kvc_ref, kvh * HEAD_DIM)
            v2 = _block_diag(kvp_ref, kvc_ref, 128 + kvh * HEAD_DIM)
            for jp in range(PAIRS_PER_KV):
                pair = kvh * PAIRS_PER_KV + jp
                lanes = slice(pair * PAIR, (pair + 1) * PAIR)
                s2 = lax.dot_general(k2, q_ref[:, lanes].astype(BF16), NT, preferred_element_type=F32)
                probs = []
                for e in range(2):
                    hh = 2 * pair + e
                    s = _swa_scores(s2, e, hh, valid, distf)
                    sink = sink_ref[0, hh]
                    mx = jnp.maximum(jnp.max(s, axis=0, keepdims=True), sink)
                    p = jnp.exp(s - mx)
                    l = jnp.sum(p, axis=0, keepdims=True) + jnp.exp(sink - mx)
                    probs.append((p * (1.0 / l)).astype(BF16))
                    lse_ref[hh:hh + 1, :] = mx + jnp.log(l)
                o_ref[:, lanes] = lax.dot_general(jnp.concatenate(probs, axis=0), v2, TN,
                                                  preferred_element_type=F32)

    res, xres = _call(
        body, name=name, grid=(nb,),
        in_specs=[SMEM,
                  pl.BlockSpec((BLK, ATTN_W), lambda n: (n, 0)),
                  pl.BlockSpec((BLK, 256), lambda n: (n, kvb)),
                  pl.BlockSpec((BLK, 256), lambda n: (jnp.maximum(n - 1, 0), kvb))],
        out_specs=[pl.BlockSpec((BLK, ATTN_W), lambda n: (n, 0)),
                   pl.BlockSpec((None, N_Q, BLK), lambda n: (n, 0, 0))],
        out_shape=[jax.ShapeDtypeStruct((t, ATTN_W), F32), jax.ShapeDtypeStruct((nb, N_Q, BLK), F32)],
        args=[sinks, proj, proj, proj], semantics=("parallel",), exchanges=exchanges)
    return (*res, xres) if exchanges else res


def _swa_bwd(proj, sinks, dattn, lse, *, name, exchanges=()):
    t = proj.shape[0]
    nb = t // BLK
    kvb = KV_COL // (2 * 128)

    def body(sink_ref, q_ref, kvc_ref, kvp_ref, do_ref, lse_ref, dq_ref, dkv_ref, dsink_ref, carry_ref):
        n = pl.program_id(0)

        @pl.when(n == 0)
        def _():
            dsink_ref[...] = jnp.zeros_like(dsink_ref)
            carry_ref[...] = jnp.zeros_like(carry_ref)

        @pl.when(n < nb)
        def _():
            valid, distf = _swa_mask(n)
            for kvh in range(N_KV):
                k2 = _block_diag(kvp_ref, kvc_ref, kvh * HEAD_DIM)
                v2 = _block_diag(kvp_ref, kvc_ref, 128 + kvh * HEAD_DIM)
                dk2 = jnp.zeros((4 * BLK, PAIR), F32)
                dv2 = jnp.zeros((4 * BLK, PAIR), F32)
                for jp in range(PAIRS_PER_KV):
                    pair = kvh * PAIRS_PER_KV + jp
                    lanes = slice(pair * PAIR, (pair + 1) * PAIR)
                    q2 = q_ref[:, lanes].astype(BF16)
                    do2 = do_ref[:, lanes].astype(BF16)
                    s2 = lax.dot_general(k2, q2, NT, preferred_element_type=F32)
                    dp2 = lax.dot_general(v2, do2, NT, preferred_element_type=F32)
                    probs, dss = [], []
                    for e in range(2):
                        hh = 2 * pair + e
                        lse_h = lse_ref[hh:hh + 1, :]
                        p = jnp.exp(_swa_scores(s2, e, hh, valid, distf) - lse_h)
                        dp = dp2[2 * BLK * e:2 * BLK * (e + 1)]
                        delta = jnp.sum(p * dp, axis=0, keepdims=True)
                        dsink_ref[hh:hh + 1, :] += -jnp.exp(sink_ref[0, hh] - lse_h) * delta
                        probs.append(p.astype(BF16))
                        dss.append((p * (dp - delta)).astype(BF16))
                    ds2 = jnp.concatenate(dss, axis=0)
                    dq_ref[:, lanes] = (lax.dot_general(ds2, k2, TN, preferred_element_type=F32)
                                        * (HEAD_DIM ** -0.5)).astype(BF16)
                    dk2 = dk2 + jnp.dot(ds2, q2, preferred_element_type=F32)
                    dv2 = dv2 + jnp.dot(jnp.concatenate(probs, axis=0), do2, preferred_element_type=F32)
                dk_cat = (dk2[:2 * BLK, :HEAD_DIM] + dk2[2 * BLK:, HEAD_DIM:]) * (HEAD_DIM ** -0.5)
                dv_cat = dv2[:2 * BLK, :HEAD_DIM] + dv2[2 * BLK:, HEAD_DIM:]
                ko = kvh * HEAD_DIM
                vo = 128 + kvh * HEAD_DIM
                dkv_ref[:, ko:ko + HEAD_DIM] = (carry_ref[:, ko:ko + HEAD_DIM] + dk_cat[:BLK]).astype(BF16)
                dkv_ref[:, vo:vo + HEAD_DIM] = (carry_ref[:, vo:vo + HEAD_DIM] + dv_cat[:BLK]).astype(BF16)
                carry_ref[:, ko:ko + HEAD_DIM] = dk_cat[BLK:]
                carry_ref[:, vo:vo + HEAD_DIM] = dv_cat[BLK:]

        @pl.when(n == nb)
        def _():
            dkv_ref[...] = carry_ref[...].astype(BF16)

    last = nb - 1
    res, xres = _call(
        body, name=name, grid=(nb + 1,),
        in_specs=[SMEM,
                  pl.BlockSpec((BLK, ATTN_W), lambda n: (jnp.minimum(n, last), 0)),
                  pl.BlockSpec((BLK, 256), lambda n: (jnp.minimum(n, last), kvb)),
                  pl.BlockSpec((BLK, 256), lambda n: (jnp.maximum(jnp.minimum(n, last) - 1, 0), kvb)),
                  pl.BlockSpec((BLK, ATTN_W), lambda n: (jnp.minimum(n, last), 0)),
                  pl.BlockSpec((None, N_Q, BLK), lambda n: (jnp.minimum(n, last), 0, 0))],
        out_specs=[pl.BlockSpec((BLK, ATTN_W), lambda n: (jnp.minimum(n, last), 0)),
                   pl.BlockSpec((BLK, 256), lambda n: (jnp.maximum(n - 1, 0), 0)),
                   pl.BlockSpec((N_Q, BLK), lambda n: (0, 0))],
        out_shape=[jax.ShapeDtypeStruct((t, ATTN_W), BF16), jax.ShapeDtypeStruct((t, 256), BF16),
                   jax.ShapeDtypeStruct((N_Q, BLK), F32)],
        scratch_shapes=[pltpu.VMEM((BLK, 256), F32)],
        args=[sinks, proj, proj, proj, dattn, lse], semantics=("arbitrary",), exchanges=exchanges)
    return (*res, xres) if exchanges else res


def _cumsum_rows(x):
    n = x.shape[0]
    row = lax.broadcasted_iota(jnp.int32, x.shape, 0)
    s = 1
    while s < n:
        x = x + jnp.where(row >= s, pltpu.roll(x, s, axis=0), 0.0)
        s *= 2
    return x


def _rev_cumsum_rows(x):
    n = x.shape[0]
    row = lax.broadcasted_iota(jnp.int32, x.shape, 0)
    s = 1
    while s < n:
        x = x + jnp.where(row < n - s, pltpu.roll(x, n - s, axis=0), 0.0)
        s *= 2
    return x


def _lower_bound(lbl_ref):
    l0 = lbl_ref[0:1, :]
    l1 = lbl_ref[1:2, :]
    mx = jnp.maximum(l0, l1)
    e0 = jnp.exp(l0 - mx)
    e1 = jnp.exp(l1 - mx)
    return e0 / (e0 + e1)


def _hgrn_gates(z, lb):
    sg = _sigmoid(z)
    f = lb + (1.0 - lb) * sg
    return sg, f, jnp.log(f), 1.0 - f


def _sub_factors(b, k, i, sub, trim):
    need = -(-sub * i // 16) * 16 if trim else CHUNK
    rows = lax.broadcasted_iota(jnp.int32, (need, RNN_HD), 0)
    ref = b[sub * i - 1:sub * i, :]
    qfac = jnp.exp(b[sub * i:sub * (i + 1), :] - ref)
    kfac = jnp.where(rows < sub * i, jnp.exp(ref - b[:need]), 0.0)
    kt = (k[:need] * kfac).astype(BF16)
    if need < CHUNK:
        kt = jnp.concatenate([kt, jnp.zeros((CHUNK - need, RNN_HD), BF16)], axis=0)
    return qfac, kfac, kt


def _diag_decay(bi, s):
    trow = lax.broadcasted_iota(jnp.int32, bi.shape, 0)
    return jnp.where(trow >= s, jnp.exp(bi - bi[s:s + 1, :]), 0.0)


def _hgrn_fwd(proj, lb_logits, norm_gain, *, tb, name, exchanges=()):
    t = proj.shape[0]
    ntb = t // tb
    nch = tb // CHUNK
    qb, fb, ib, gb = QR_COL // 128, FR_COL // 128, IR_COL // 128, GR_COL // 128

    def body(q_ref, f_ref, i_ref, g_ref, lbl_ref, gain_ref, o_ref, out_ref, s0_ref, st_ref):
        c = pl.program_id(1)

        @pl.when(c == 0)
        def _():
            st_ref[...] = jnp.zeros_like(st_ref)

        lb = _lower_bound(lbl_ref)
        gain = gain_ref[...]

        def chunk(ci, st):
            rows = slice(ci * CHUNK, (ci + 1) * CHUNK)
            _, _, lf, k = _hgrn_gates(f_ref[rows, :], lb)
            qr = q_ref[rows, :]
            q = qr * _sigmoid(qr)
            v = i_ref[rows, :]
            b = _cumsum_rows(lf)
            s0_ref[ci] = st
            o_inter = lax.dot_general((q * jnp.exp(b)).astype(BF16), st.astype(BF16), NT,
                                      preferred_element_type=F32)
            vb = v.astype(BF16)
            blast = b[CHUNK - 1:CHUNK, :]
            khat = (k * jnp.exp(blast - b)).astype(BF16)
            st = st * jnp.exp(blast) + lax.dot_general(vb, khat, TN, preferred_element_type=F32)
            blocks = []
            for i in range(CHUNK // SUB_FWD):
                blk = slice(SUB_FWD * i, SUB_FWD * (i + 1))
                qi, ki, vi, bi = q[blk], k[blk], v[blk], b[blk]
                oi = o_inter[blk]
                if i > 0:
                    qfac, _, kt = _sub_factors(b, k, i, SUB_FWD, trim=True)
                    att = lax.dot_general((qi * qfac).astype(BF16), kt, NT,
                                          preferred_element_type=F32)
                    oi = oi + jnp.dot(att.astype(BF16), vb, preferred_element_type=F32)
                for s in range(SUB_FWD):
                    qe = qi * _diag_decay(bi, s)
                    a = jnp.sum(qe * ki[s:s + 1, :], axis=1, keepdims=True)
                    oi = oi + a * vi[s:s + 1, :]
                blocks.append(oi)
            o = jnp.concatenate(blocks, axis=0)
            o_ref[rows, :] = o
            gr = g_ref[rows, :]
            out_ref[rows, :] = o * _rstd(o) * gain * (gr * _sigmoid(gr))
            return st

        st = st_ref[...]
        for ci in range(nch):
            st = chunk(ci, st)
        st_ref[...] = st

    def col(base):
        return pl.BlockSpec((tb, RNN_HD), lambda h, c: (c, base + h))

    res, xres = _call(
        body, name=name, grid=(N_RNN, ntb),
        in_specs=[col(qb), col(fb), col(ib), col(gb),
                  pl.BlockSpec((2, RNN_HD), lambda h, c: (0, h)), pl.BlockSpec((1, RNN_HD), lambda h, c: (0, 0))],
        out_specs=[pl.BlockSpec((tb, RNN_HD), lambda h, c: (c, h)), pl.BlockSpec((tb, RNN_HD), lambda h, c: (c, h)),
                   pl.BlockSpec((None, nch, RNN_HD, RNN_HD), lambda h, c: (h, c, 0, 0))],
        out_shape=[jax.ShapeDtypeStruct((t, RNN_W), F32), jax.ShapeDtypeStruct((t, RNN_W), F32),
                   jax.ShapeDtypeStruct((N_RNN, t // CHUNK, RNN_HD, RNN_HD), F32)],
        scratch_shapes=[pltpu.VMEM((RNN_HD, RNN_HD), F32)],
        args=[proj, proj, proj, proj, lb_logits, norm_gain],
        semantics=("parallel", "arbitrary"), exchanges=exchanges)
    return (*res, xres) if exchanges else res


def _hgrn_bwd(proj, lb_logits, norm_gain, o_pre, s0, dcat, *, tb, name, exchanges=()):
    t = proj.shape[0]
    ntb = t // tb
    nch = tb // CHUNK
    qb, fb, ib, gb = QR_COL // 128, FR_COL // 128, IR_COL // 128, GR_COL // 128
    sub = SUB_BWD
    nsub = CHUNK // sub

    def body(q_ref, f_ref, i_ref, g_ref, lbl_ref, gain_ref, o_ref, s0_ref, dout_ref,
             dq_ref, df_ref, di_ref, dg_ref, dlb_ref, dgain_ref,
             dst_ref, dqs_ref, dks_ref, dvs_ref):
        c = pl.program_id(1)

        @pl.when(c == 0)
        def _():
            dst_ref[...] = jnp.zeros_like(dst_ref)
            dlb_ref[...] = jnp.zeros_like(dlb_ref)
            dgain_ref[...] = jnp.zeros_like(dgain_ref)

        lb = _lower_bound(lbl_ref)
        gain = gain_ref[...]

        def chunk(ci, dst):
            rows = slice(ci * CHUNK, (ci + 1) * CHUNK)
            dqa_ref, dka_ref, dva_ref = dqs_ref.at[ci], dks_ref.at[ci], dvs_ref.at[ci]
            sg, f, lf, k = _hgrn_gates(f_ref[rows, :], lb)
            qr = q_ref[rows, :]
            sq = _sigmoid(qr)
            q = qr * sq
            v = i_ref[rows, :]
            b = _cumsum_rows(lf)

            dout = dout_ref[rows, :].astype(F32)
            o = o_ref[rows, :]
            gr = g_ref[rows, :]
            sgg = _sigmoid(gr)
            gate = gr * sgg
            rs = _rstd(o)
            nrm = o * rs
            dg_ref[rows, :] = (dout * nrm * gain * (sgg * (1.0 + gr * (1.0 - sgg)))).astype(BF16)
            dn = dout * gate
            dgain_ref[...] += jnp.sum(dn * nrm, axis=0, keepdims=True)
            tt = dn * gain
            do = rs * (tt - nrm * jnp.mean(tt * nrm, axis=-1, keepdims=True))

            dob = do.astype(BF16)
            vb = v.astype(BF16)
            eb = jnp.exp(b)
            blast = b[CHUNK - 1:CHUNK, :]
            ebl = jnp.exp(blast - b)
            dstb = dst.astype(BF16)
            khat = (k * ebl).astype(BF16)
            s0 = s0_ref[ci]
            dqa_ref[...] = eb * jnp.dot(dob, s0.astype(BF16), preferred_element_type=F32)
            dk_state = ebl * jnp.dot(vb, dstb, preferred_element_type=F32)
            dka_ref[...] = dk_state
            d_blast = (jnp.sum(k * dk_state, axis=0, keepdims=True)
                       + jnp.exp(blast) * jnp.sum(dst * s0, axis=0, keepdims=True))
            dva_ref[...] = lax.dot_general(khat, dstb, NT, preferred_element_type=F32)
            dst_next = dst * jnp.exp(blast) + lax.dot_general(dob, (q * eb).astype(BF16), TN,
                                                              preferred_element_type=F32)
            pm = lax.dot_general(dob, vb, NT, preferred_element_type=F32)
            for i in range(nsub):
                blk = slice(sub * i, sub * (i + 1))
                qi, ki, vi, bi, doi = q[blk], k[blk], v[blk], b[blk], do[blk]
                dqi = dqa_ref[blk, :]
                if i > 0:
                    qfac, kfac, kt = _sub_factors(b, k, i, sub, trim=False)
                    qt = (qi * qfac).astype(BF16)
                    att = lax.dot_general(qt, kt, NT, preferred_element_type=F32).astype(BF16)
                    pmi = pm[blk, :].astype(BF16)
                    dva_ref[...] += lax.dot_general(att, doi.astype(BF16), TN, preferred_element_type=F32)
                    dqi = dqi + qfac * jnp.dot(pmi, kt, preferred_element_type=F32)
                    dka_ref[...] += kfac * lax.dot_general(pmi, qt, TN, preferred_element_type=F32)
                dqa_ref[blk, :] = dqi
                srow = lax.broadcasted_iota(jnp.int32, (sub, RNN_HD), 0)
                dki = jnp.zeros((sub, RNN_HD), F32)
                dvi = jnp.zeros((sub, RNN_HD), F32)
                for tq in range(sub):
                    qt, dot_ = qi[tq:tq + 1, :], doi[tq:tq + 1, :]
                    e = jnp.where(srow <= tq, jnp.exp(bi[tq:tq + 1, :] - bi), 0.0)
                    ke = ki * e
                    p = jnp.sum(vi * dot_, axis=1, keepdims=True)
                    a = jnp.sum(ke * qt, axis=1, keepdims=True)
                    dki = dki + p * (qt * e)
                    dvi = dvi + a * dot_
                    row = slice(sub * i + tq, sub * i + tq + 1)
                    dqa_ref[row, :] += jnp.sum(p * ke, axis=0, keepdims=True)
                dka_ref[blk, :] += dki
                dva_ref[blk, :] += dvi

            dq = dqa_ref[...]
            dk = dka_ref[...]
            lastrow = lax.broadcasted_iota(jnp.int32, (CHUNK, RNN_HD), 0) == CHUNK - 1
            dlf = _rev_cumsum_rows(q * dq - k * dk + jnp.where(lastrow, d_blast, 0.0))
            dff = dlf / f - dk
            df_ref[rows, :] = (dff * (1.0 - lb) * sg * (1.0 - sg)).astype(BF16)
            dlb_ref[...] += jnp.sum(dff * (1.0 - sg), axis=0, keepdims=True)
            dq_ref[rows, :] = (dq * (sq * (1.0 + qr * (1.0 - sq)))).astype(BF16)
            di_ref[rows, :] = dva_ref[...].astype(BF16)
            return dst_next

        dst = dst_ref[...]
        for ci in reversed(range(nch)):
            dst = chunk(ci, dst)
        dst_ref[...] = dst

    def col(base):
        return pl.BlockSpec((tb, RNN_HD), lambda h, c: (ntb - 1 - c, base + h))

    outc = pl.BlockSpec((tb, RNN_HD), lambda h, c: (ntb - 1 - c, h))
    hb = ATTN_W // RNN_HD
    res, xres = _call(
        body, name=name, grid=(N_RNN, ntb),
        in_specs=[col(qb), col(fb), col(ib), col(gb),
                  pl.BlockSpec((2, RNN_HD), lambda h, c: (0, h)), pl.BlockSpec((1, RNN_HD), lambda h, c: (0, 0)),
                  outc,
                  pl.BlockSpec((None, nch, RNN_HD, RNN_HD), lambda h, c: (h, ntb - 1 - c, 0, 0)),
                  pl.BlockSpec((tb, RNN_HD), lambda h, c: (ntb - 1 - c, hb + h))],
        out_specs=[outc, outc, outc, outc,
                   pl.BlockSpec((1, RNN_HD), lambda h, c: (0, h)),
                   pl.BlockSpec((None, 1, RNN_HD), lambda h, c: (h, 0, 0))],
        out_shape=[jax.ShapeDtypeStruct((t, RNN_W), BF16)] * 4
        + [jax.ShapeDtypeStruct((1, RNN_W), F32), jax.ShapeDtypeStruct((N_RNN, 1, RNN_HD), F32)],
        scratch_shapes=[pltpu.VMEM((RNN_HD, RNN_HD), F32),
                        pltpu.VMEM((nch, CHUNK, RNN_HD), F32), pltpu.VMEM((nch, CHUNK, RNN_HD), F32),
                        pltpu.VMEM((nch, CHUNK, RNN_HD), F32)],
        args=[proj, proj, proj, proj, lb_logits, norm_gain, o_pre, s0, dcat],
        semantics=("parallel", "arbitrary"), exchanges=exchanges)
    return (*res, xres) if exchanges else res


def _cast_slots(w, where, *, name):
    _, rows, cols = w.shape
    rh = rows // 2
    tr = _row_tile(rh, cols)
    nh = rh // tr

    def body(wh_ref, w_ref, o_ref):
        o_ref[...] = w_ref[...].astype(BF16)

    return pl.pallas_call(
        body, name=name,
        grid_spec=pltpu.PrefetchScalarGridSpec(
            num_scalar_prefetch=1, grid=(2, nh),
            in_specs=[pl.BlockSpec((None, tr, cols), lambda h, i, wh: (0, h * nh + i, 0))],
            out_specs=pl.BlockSpec((None, tr, cols), lambda h, i, wh: (2 * wh[0] + h, i, 0))),
        out_shape=jax.ShapeDtypeStruct((8, rh, cols), BF16),
        compiler_params=_params(("parallel", "parallel")),
    )(where, w)


def _row_tile(rows, cols, budget=1 << 20):
    tr = rows
    while tr * cols > budget and tr % 16 == 0:
        tr //= 2
    return tr


def _half_spec(g, tr, halves_last, slab):
    if halves_last:
        return pl.BlockSpec((None, tr, g.shape[2] // 2), lambda *a: (slab(*a), a[-2], a[-1][1]))
    return pl.BlockSpec((None, None, tr, g.shape[3]), lambda *a: (slab(*a), a[-1][1], a[-2], 0))


def _pair_sum(g, sib, where, *, name, halves_last=False):
    rh, cols = sib.shape[1:]
    tr = _row_tile(rh, cols)

    def body(w_ref, g_ref, s_ref, o_ref):
        o_ref[...] = (g_ref[...] + s_ref[...]).astype(BF16)

    return pl.pallas_call(
        body, name=name,
        grid_spec=pltpu.PrefetchScalarGridSpec(
            num_scalar_prefetch=1, grid=(4, rh // tr),
            in_specs=[_half_spec(g, tr, halves_last, lambda s, i, w: s),
                      pl.BlockSpec((None, tr, cols), lambda s, i, w: (s, i, 0))],
            out_specs=pl.BlockSpec((None, tr, cols), lambda s, i, w: (s, i, 0))),
        out_shape=jax.ShapeDtypeStruct((4, rh, cols), BF16),
        compiler_params=_params(("parallel", "parallel")),
    )(where, g, sib)


def _final_half(g, sib, recv, where, *, name, halves_last=False):
    rh, cols = sib.shape[1:]
    tr = _row_tile(rh, cols)

    def body(w_ref, g_ref, s_ref, r_ref, o_ref):
        acc = g_ref[...] + s_ref[...]
        for j in range(3):
            acc = acc + r_ref[j].astype(F32)
        o_ref[...] = acc

    return pl.pallas_call(
        body, name=name,
        grid_spec=pltpu.PrefetchScalarGridSpec(
            num_scalar_prefetch=1, grid=(rh // tr,),
            in_specs=[_half_spec(g, tr, halves_last, lambda i, w: w[0]),
                      pl.BlockSpec((None, tr, cols), lambda i, w: (w[0], i, 0)),
                      pl.BlockSpec((3, tr, cols), lambda i, w: (0, i, 0))],
            out_specs=pl.BlockSpec((tr, cols), lambda i, w: (i, 0))),
        out_shape=jax.ShapeDtypeStruct((rh, cols), F32),
        compiler_params=_params(("parallel",)),
    )(where, g, sib, recv)


def _adamw_math(w, g, m, v):
    m = ADAM_B1 * m + (1.0 - ADAM_B1) * g
    v = ADAM_B2 * v + (1.0 - ADAM_B2) * (g * g)
    m_hat = m / (1.0 - ADAM_B1 ** ADAM_STEP)
    v_hat = v / (1.0 - ADAM_B2 ** ADAM_STEP)
    delta = -ADAM_LR * (m_hat / (jnp.sqrt(v_hat) + ADAM_EPS) + ADAM_WD * w)
    return delta, m, v


def _adamw(w, mine, theirs, m, v, where, *, name, halves_last=False):
    _, rows, cols = w.shape
    if halves_last:
        cols //= 2
        tr = _row_tile(rows, cols, budget=1 << 19)
        grid = (rows // tr, 2)
        blk = pl.BlockSpec((None, tr, cols), lambda i, h, wh: (0, i, h))
        mine_spec = theirs_spec = pl.BlockSpec((tr, cols), lambda i, h, wh: (i, 0))
        which = lambda: pl.program_id(1)
    else:
        tr = _row_tile(rows // 2, cols, budget=1 << 19)
        nh = rows // 2 // tr
        grid = (rows // tr,)
        blk = pl.BlockSpec((None, tr, cols), lambda i, wh: (0, i, 0))
        mine_spec = pl.BlockSpec((tr, cols), lambda i, wh: (jnp.where(i // nh == wh[1], i % nh, 0), 0))
        theirs_spec = pl.BlockSpec((tr, cols), lambda i, wh: (jnp.where(i // nh == wh[1], 0, i % nh), 0))
        which = lambda: pl.program_id(0) // nh

    def body(wh_ref, w_ref, a_ref, b_ref, m_ref, v_ref, g_ref, d_ref, nm_ref, nv_ref):
        g = jnp.where(which() == wh_ref[1], a_ref[...], b_ref[...])
        d, nm, nv = _adamw_math(w_ref[...], g, m_ref[...], v_ref[...])
        g_ref[...] = g
        d_ref[...] = d
        nm_ref[...] = nm
        nv_ref[...] = nv

    rows, cols = w.shape[1:]
    return pl.pallas_call(
        body, name=name,
        grid_spec=pltpu.PrefetchScalarGridSpec(
            num_scalar_prefetch=1, grid=grid,
            in_specs=[blk, mine_spec, theirs_spec, blk, blk], out_specs=[blk] * 4),
        out_shape=[jax.ShapeDtypeStruct((1, rows, cols), F32)] * 4,
        compiler_params=_params(("parallel",) * len(grid)),
    )(where, w, mine, theirs, m, v)


SEG_LOSS = 0
SEG_SINK = 128
SEG_AGAIN = 256
SEG_L0 = SEG_AGAIN + ATTN_W
SEG_L1 = SEG_L0 + RNN_W
SEG_RGAIN = SEG_L1 + RNN_W
SEG_G = SEG_RGAIN + 128
N_PACK = SEG_G + 4 * D_MODEL


def _pack(sinks, again, l0, l1, rgain, gains, loss=None):
    z = lambda k: jnp.zeros((1, k), F32)
    first = z(128) if loss is None else loss
    return jnp.concatenate([first, sinks, z(128 - N_Q), again, l0, l1, rgain] + list(gains), axis=1)


def _small_reduce_adamw(part, w, m, v, *, name):
    def body(p_ref, w_ref, m_ref, v_ref, g_ref, d_ref, nm_ref, nv_ref, buf_ref, send_sems, recv_sems):
        x, y, c = _place()
        me = 4 * x + 2 * y + c
        copies = []
        for k in range(1, 8):
            dx, dy, dc = (k >> 2) & 1, (k >> 1) & 1, k & 1
            to = (x ^ dx, y ^ dy, c ^ dc)
            cp = pltpu.make_async_remote_copy(
                src_ref=p_ref, dst_ref=buf_ref.at[me],
                send_sem=send_sems.at[k - 1], recv_sem=recv_sems.at[k - 1],
                device_id=to, device_id_type=MESH)
            cp.start()
            copies.append(cp)
        buf_ref[me] = p_ref[...]
        for cp in copies:
            cp.wait()
        tot = buf_ref[0]
        for j in range(1, 8):
            tot = tot + buf_ref[j]
        g_ref[...] = tot
        l0 = w_ref[:, SEG_L0:SEG_L0 + RNN_W]
        l1 = w_ref[:, SEG_L1:SEG_L1 + RNN_W]
        mx = jnp.maximum(l0, l1)
        e0 = jnp.exp(l0 - mx)
        e1 = jnp.exp(l1 - mx)
        lb = e0 / (e0 + e1)
        gl0 = tot[:, SEG_L0:SEG_L0 + RNN_W] * lb * (1.0 - lb)
        g_ref[:, SEG_L0:SEG_L0 + RNN_W] = gl0
        g_ref[:, SEG_L1:SEG_L1 + RNN_W] = -gl0
        d, nm, nv = _adamw_math(w_ref[...], g_ref[...], m_ref[...], v_ref[...])
        d_ref[...] = d
        nm_ref[...] = nm
        nv_ref[...] = nv

    vm = pl.BlockSpec(memory_space=pltpu.VMEM)
    return pl.pallas_call(
        body, name=name,
        in_specs=[vm] * 4, out_specs=[vm] * 4,
        out_shape=[jax.ShapeDtypeStruct((1, N_PACK), F32)] * 4,
        scratch_shapes=[pltpu.VMEM((8, 1, N_PACK), F32), pltpu.SemaphoreType.DMA((7,)),
                        pltpu.SemaphoreType.DMA((7,))],
    )(part, w, m, v)


def _layer_grads(xs, tgt, bufs, where, sinks, again, lb_logits, rgain,
                 g_mix_pre, g_mix_post, g_mlp_pre, g_mlp_post):
    tm = 512
    b_in, b_out, b_up, b_dn = bufs

    shard = IN_W // N_CHIPS
    h1, b_in = _rms_cast_gather(xs, g_mix_pre, b_in, tm=tm, name="h1_norm_gather_w_in")
    w_in_t = b_in.reshape(IN_W, D_MODEL)
    proj, ((b_out, b_up),) = _mm(
        h1, w_in_t, tm=1024, tn=768, tk=D_MODEL, out_dtype=F32, w_layout="nk", name="in_proj",
        exchanges=[_x_gather([b_out, b_up], ici=[(0, 256), (0, 336)])])
    attn, lse, ((b_out, b_up),) = _swa_fwd(
        proj, sinks, name="swa_fwd",
        exchanges=[_x_gather([b_out, b_up], ici=[None, (336, 320)], d2d=[(0, 256), None])])
    w_out = b_out.reshape(D_MODEL, D_MODEL)
    o_pre, rnn, s0, ((b_up, b_dn),) = _hgrn_fwd(
        proj, lb_logits, rgain, tb=512, name="hgrn_fwd",
        exchanges=[_x_gather([b_up, b_dn], ici=[(656, 368), (0, 400)])])
    cat = _mix_cat(attn, rnn, again, tm=tm, name="mix_cat")
    mixed, ((b_up, b_dn),) = _mm(
        cat, w_out, tm=1024, tn=1024, tk=D_MODEL, out_dtype=BF16, name="out_proj",
        exchanges=[_x_gather([b_up, b_dn], ici=[None, (400, 240)], d2d=[(0, 1024), (0, 400)])])
    w_up4 = b_up.reshape(N_CHIPS, D_MODEL, D_FF // N_CHIPS)
    x1, h2, ((b_dn,),) = _post_norm_res(
        mixed, g_mix_post, xs, g_mlp_pre, tm=tm, name="mix_post",
        exchanges=[_x_gather([b_dn], d2d=[(400, 240)])])
    u, ((b_dn,),) = _mm(h2, w_up4, tm=1024, tn=1024, tk=D_MODEL, out_dtype=BF16, relu=True, w_layout="skn",
                        name="mlp_up", exchanges=[_x_gather([b_dn], ici=[(640, 384)], cross=[(640, 384)])])
    w_dn = b_dn.reshape(D_FF, D_MODEL)
    yv = _mm(u, w_dn, tm=1024, tn=1024, tk=2048, out_dtype=BF16, a_square=True, name="mlp_down")
    dy, dx2, loss_row, dg_mlp_post = _loss_head(yv, g_mlp_post, x1, tgt, tm=tm, name="loss_head")

    def halved(g):
        return g.reshape(N_CHIPS, 2, g.shape[1] // 2, g.shape[2])
    du = _mm(dy, w_dn, tm=1024, tn=1024, tk=D_MODEL, out_dtype=BF16, mul2=u, w_layout="nk", name="mlp_down_bwd")
    g_dn = halved(_mm_tn(u, dy, tm=1024, tn=1024, tt=2048, a_square=True, name="w_down_grad")
                  .reshape(N_CHIPS, D_FF // N_CHIPS, D_MODEL))
    d_w_up, ((sib_dn,),) = _mm_tn(h2, du, tm=1024, tn=1024, tt=2048, n_split=N_CHIPS, name="w_up_grad",
                                  exchanges=[_x_pair([g_dn])])
    g_up = halved(d_w_up)
    wire_dn = _pair_sum(g_dn, sib_dn, where, name="pair_sum_w_down")
    dh2, ((recv_dn,), (sib_up,)) = _mm(du, w_up4, tm=1024, tn=1024, tk=2048, out_dtype=BF16, w_layout="snk", name="mlp_up_bwd",
                                       exchanges=[_x_chip([wire_dn], rows=[(0, 704)]), _x_pair([g_up])])
    wire_up = _pair_sum(g_up, sib_up, where, name="pair_sum_w_up")
    dx1, dg_mlp_pre, ((recv_dn,),) = _rms_bwd(dh2, x1, g_mlp_pre, dx2, tm=tm, out_dtype=F32, name="mlp_pre_bwd",
                                              exchanges=[_x_chip([wire_dn], rows=[(704, 224)], into=[recv_dn])])
    dmixed, dg_mix_post = _rms_bwd(dx1, mixed, g_mix_post, None, tm=tm, out_dtype=BF16, name="mix_post_bwd")
    d_w_out, ((recv_dn,),) = _mm_tn(cat, dmixed, tm=1024, tn=1024, tt=2048, name="w_out_grad",
                                    exchanges=[_x_chip([wire_dn], rows=[(928, 96)], into=[recv_dn])])
    fin_dn = _final_half(g_dn, sib_dn, recv_dn, where, name="final_half_w_down")
    g_out = halved(d_w_out.reshape(N_CHIPS, D_MODEL // N_CHIPS, D_MODEL))
    dcat, ((sib_out,), (oth_dn,)) = _mm(dmixed, w_out, tm=1024, tn=1024, tk=D_MODEL, out_dtype=BF16, w_layout="nk",
                                        name="out_proj_bwd", exchanges=[_x_pair([g_out]), _x_share([fin_dn])])
    wire_out = _pair_sum(g_out, sib_out, where, name="pair_sum_w_out")
    dattn, dg_again = _rms_bwd(dcat, attn, again, None, tm=tm, out_dtype=BF16, name="attn_norm_bwd")
    dq_a, dkv, dsinks, ((recv_up,),) = _swa_bwd(
        proj, sinks, dattn, lse, name="swa_bwd", exchanges=[_x_chip([wire_up], rows=[(0, 512)])])
    dq_r, df_r, di_r, dg_r, dlb, dgain_h, ((recv_up,), (recv_out,)) = _hgrn_bwd(
        proj, lb_logits, rgain, o_pre, s0, dcat, tb=512, name="hgrn_bwd",
        exchanges=[_x_chip([wire_up], rows=[(512, 512)], into=[recv_up]), _x_chip([wire_out])])
    fin_up = _final_half(g_up, sib_up, recv_up, where, name="final_half_w_up")
    fin_out = _final_half(g_out, sib_out, recv_out, where, name="final_half_w_out")
    dproj = jnp.concatenate([dq_a, dkv, dq_r, df_r, di_r, dg_r], axis=1)
    piece_cols = D_MODEL // 4

    def w_in_piece(pc, exchanges):
        d, xres = _mm_tn(dproj, h1, tm=896, tn=2 * piece_cols, tt=2048, b_blocks=(pc, pc + 2),
                         name="w_in_grad_%d" % pc, exchanges=exchanges)
        return d.reshape(N_CHIPS, shard, 2 * piece_cols), xres

    g_in0, ((oth_up, oth_out),) = w_in_piece(0, [_x_share([fin_up, fin_out])])
    g_in1, ((sib_in0,),) = w_in_piece(1, [_x_pair([g_in0], halves_last=True)])
    wire_in0 = _pair_sum(g_in0, sib_in0, where, name="pair_sum_w_in_0", halves_last=True)
    dh1, ((recv_in0,), (sib_in1,)) = _mm(
        dproj, w_in_t, tm=1024, tn=1024, tk=2688, out_dtype=BF16, m_blocks=(0, 2), name="in_proj_bwd_0",
        exchanges=[_x_chip([wire_in0]), _x_pair([g_in1], halves_last=True)])
    wire_in1 = _pair_sum(g_in1, sib_in1, where, name="pair_sum_w_in_1", halves_last=True)
    dh1, ((recv_in1,),) = _mm(
        dproj, w_in_t, tm=1024, tn=1024, tk=2688, out_dtype=BF16, m_blocks=(2, 2), out_into=dh1,
        name="in_proj_bwd_1", exchanges=[_x_chip([wire_in1])])
    gx, dg_mix_pre = _rms_bwd(dh1, xs, g_mix_pre, dx1, tm=tm, out_dtype=F32, name="mix_pre_bwd")
    fin_in0 = _final_half(g_in0, sib_in0, recv_in0, where, name="final_half_w_in_0", halves_last=True)
    fin_in1 = _final_half(g_in1, sib_in1, recv_in1, where, name="final_half_w_in_1", halves_last=True)
    oth_in0, oth_in1 = _run_exchange(_x_share([fin_in0, fin_in1]), name="share_w_in")
    fin_in = jnp.concatenate([fin_in0, fin_in1], axis=1)
    oth_in = jnp.concatenate([oth_in0, oth_in1], axis=1)

    big = [(fin_in, oth_in), (fin_out, oth_out), (fin_up, oth_up), (fin_dn, oth_dn)]
    drgain = jnp.sum(dgain_h, axis=0)
    small = _pack(jnp.sum(dsinks, axis=1)[None, :], dg_again, dlb, jnp.zeros_like(dlb), drgain,
                  [dg_mix_pre, dg_mix_post, dg_mlp_pre, dg_mlp_post], loss=loss_row)
    return gx, big, small


def kernel(x, w_in, attn_sinks, attn_out_gain, rnn_lb_logits, rnn_norm_gain, w_out, mix_pre_gain, mix_post_gain, mlp_pre_gain, mlp_post_gain, w_up, w_down, loss_target, m_w_in, m_attn_sinks, m_attn_out_gain, m_rnn_lb_logits, m_rnn_norm_gain, m_w_out, m_mix_pre_gain, m_mix_post_gain, m_mlp_pre_gain, m_mlp_post_gain, m_w_up, m_w_down, v_w_in, v_attn_sinks, v_attn_out_gain, v_rnn_lb_logits, v_rnn_norm_gain, v_w_out, v_mix_pre_gain, v_mix_post_gain, v_mlp_pre_gain, v_mlp_post_gain, v_w_up, v_w_down):
    ax, ay, ac = _place()
    where = jnp.stack([2 * ax + ay, ac]).astype(jnp.int32)
    t = lambda a: jnp.swapaxes(a, 1, 2)
    big_w = [t(w_in), w_out, w_up, w_down]
    big_m = [t(m_w_in), m_w_out, m_w_up, m_w_down]
    big_v = [t(v_w_in), v_w_out, v_w_up, v_w_down]

    names = ["w_in", "w_out", "w_up", "w_down"]
    bufs = [_cast_slots(w, where, name="cast_" + nm) for w, nm in zip(big_w, names)]
    gx, big_g, small_part = _layer_grads(
        x[0], loss_target[0], bufs, where, attn_sinks, attn_out_gain, rnn_lb_logits, rnn_norm_gain,
        mix_pre_gain, mix_post_gain, mlp_pre_gain, mlp_post_gain)

    grads, deltas, new_m, new_v = [], [], [], []
    for (f, o), w, m, v, nm in zip(big_g, big_w, big_m, big_v, names):
        res = _adamw(w, f, o, m, v, where, name="adamw_" + nm, halves_last=(nm == "w_in"))
        if nm == "w_in":
            res = [t(r) for r in res]
        g, d, nm_, nv_ = res
        grads.append(g)
        deltas.append(d)
        new_m.append(nm_)
        new_v.append(nv_)

    def pack_params(sinks, again, logits, rgain, gains):
        return _pack(sinks, again, logits[0:1], logits[1:2], rgain, gains)

    pw = pack_params(attn_sinks, attn_out_gain, rnn_lb_logits, rnn_norm_gain,
                     [mix_pre_gain, mix_post_gain, mlp_pre_gain, mlp_post_gain])
    pm = pack_params(m_attn_sinks, m_attn_out_gain, m_rnn_lb_logits, m_rnn_norm_gain,
                     [m_mix_pre_gain, m_mix_post_gain, m_mlp_pre_gain, m_mlp_post_gain])
    pv = pack_params(v_attn_sinks, v_attn_out_gain, v_rnn_lb_logits, v_rnn_norm_gain,
                     [v_mix_pre_gain, v_mix_post_gain, v_mlp_pre_gain, v_mlp_post_gain])
    packs = _small_reduce_adamw(small_part, pw, pm, pv, name="small_reduce_adamw")

    def unpack(p):
        seg = lambda o, k: p[:, o:o + k]
        logits = jnp.concatenate([seg(SEG_L0, RNN_W), seg(SEG_L1, RNN_W)], axis=0)
        gains = [seg(SEG_G + i * D_MODEL, D_MODEL) for i in range(4)]
        return dict(sinks=seg(SEG_SINK, N_Q), again=seg(SEG_AGAIN, ATTN_W), logits=logits,
                    rgain=seg(SEG_RGAIN, RNN_HD), gains=gains)

    def order(small, big):
        return [big[0], small["sinks"], small["again"], small["logits"], small["rgain"], big[1],
                *small["gains"], big[2], big[3]]

    loss = packs[0][0, 0]
    outs = [loss, gx[None]]
    for p, b in zip(packs, [grads, deltas, new_m, new_v]):
        outs += order(unpack(p), b)
    return tuple(outs)
```

```python
import functools

import jax
import jax.numpy as jnp
from jax import lax
from jax.experimental import pallas as pl
from jax.experimental.pallas import tpu as pltpu

F32 = jnp.float32
BF16 = jnp.bfloat16
MESH = pl.DeviceIdType.MESH

EPS = 1e-6
D_MODEL = 2048
ATTN_W = 1024
HEAD_DIM = 64
N_Q = 16
N_KV = 2
GROUP = 8
BLK = 128
RNN_W = 1024
RNN_HD = 128
N_RNN = 8
CHUNK = 64
SUB_FWD = 16
SUB_BWD = 8
D_FF = 8192
IN_W = 5376
N_CHIPS = 4
KV_COL = ATTN_W
QR_COL = ATTN_W + 2 * 128
FR_COL = QR_COL + RNN_W
IR_COL = FR_COL + RNN_W
GR_COL = IR_COL + RNN_W

ADAM_LR = 0.001
ADAM_B1 = 0.9
ADAM_B2 = 0.999
ADAM_EPS = 1e-08
ADAM_WD = 0.01
ADAM_STEP = 10

VMEM_LIMIT = 48 * 1024 * 1024

NT = (((1,), (1,)), ((), ()))
TN = (((0,), (0,)), ((), ()))


def _params(sem=None):
    return pltpu.CompilerParams(dimension_semantics=sem, vmem_limit_bytes=VMEM_LIMIT)


def _sigmoid(x):
    return 1.0 / (1.0 + jnp.exp(-x))


ANY = pl.BlockSpec(memory_space=pl.ANY)


def _place():
    return lax.axis_index("x"), lax.axis_index("y"), lax.axis_index("c")


def _other_chips(x, y):
    return [(1 - x, y), (x, 1 - y), (1 - x, 1 - y)]


class _Exchange:
    def __init__(self, srcs, outs, ncopy, build, aliases=None):
        self.srcs, self.outs, self.ncopy, self.build = list(srcs), list(outs), ncopy, build
        self.aliases = aliases or {}


def _remote(src, dst, send_sems, recv_sems, k, to):
    return pltpu.make_async_remote_copy(src_ref=src, dst_ref=dst, send_sem=send_sems.at[k],
                                        recv_sem=recv_sems.at[k], device_id=to, device_id_type=MESH)


def _call(body, *, name, grid, in_specs, out_specs, out_shape, args, scratch_shapes=(), semantics=None,
          exchanges=(), into=None):
    in_specs, out_specs, out_shape = list(in_specs), list(out_specs), list(out_shape)
    scratch_shapes = list(scratch_shapes)
    ni, no, ns = len(in_specs), len(out_specs), len(scratch_shapes)
    xsrc = [s for x in exchanges for s in x.srcs]
    xout = [o for x in exchanges for o in x.outs]
    into = into or {}
    xsrc += [into[k] for k in sorted(into)]
    nxi, nxo = len(xsrc), len(xout)
    aliases = {nxi - len(into) + ni + q: k for q, k in enumerate(sorted(into))}
    a0 = b0 = 0
    for x in exchanges:
        for si, oi in x.aliases.items():
            aliases[ni + a0 + si] = no + b0 + oi
        a0 += len(x.srcs)
        b0 += len(x.outs)
    sems = []
    for x in exchanges:
        sems += [pltpu.SemaphoreType.DMA((x.ncopy,)), pltpu.SemaphoreType.DMA((x.ncopy,))]

    def wrapped(*refs):
        ins, xi = refs[:ni], refs[ni:ni + nxi]
        outs, xo = refs[ni + nxi:ni + nxi + no], refs[ni + nxi + no:ni + nxi + no + nxo]
        rest = refs[ni + nxi + no + nxo:]
        scr, sm = rest[:ns], rest[ns:]

        def copies():
            cps = []
            a = b = 0
            for k, x in enumerate(exchanges):
                cps += x.build(xi[a:a + len(x.srcs)], xo[b:b + len(x.outs)], sm[2 * k], sm[2 * k + 1])
                a += len(x.srcs)
                b += len(x.outs)
            return cps

        def start():
            for cp in copies():
                cp.start()

        def wait():
            for cp in copies():
                cp.wait()

        if not exchanges:
            body(*ins, *outs, *scr)
        elif not grid:
            start()
            body(*ins, *outs, *scr)
            wait()
        else:
            first = last = None
            for ax, g in enumerate(grid):
                f = pl.program_id(ax) == 0
                l = pl.program_id(ax) == g - 1
                first = f if first is None else first & f
                last = l if last is None else last & l
            pl.when(first)(start)
            body(*ins, *outs, *scr)
            pl.when(last)(wait)

    if exchanges and semantics is not None:
        semantics = ("arbitrary",) * len(grid)
    kwargs = dict(grid=grid) if grid else {}
    res = pl.pallas_call(
        wrapped, name=name,
        in_specs=in_specs + [ANY] * nxi, out_specs=out_specs + [ANY] * nxo,
        out_shape=out_shape + xout, scratch_shapes=scratch_shapes + sems,
        input_output_aliases=aliases,
        compiler_params=_params(semantics), **kwargs,
    )(*args, *xsrc)
    res = list(res)
    mine, theirs = res[:no], res[no:]
    per = []
    b = 0
    for x in exchanges:
        per.append(theirs[b:b + len(x.outs)])
        b += len(x.outs)
    return mine, per


def _run_exchange(x, *, name):
    return _call(lambda: None, name=name, grid=(), in_specs=[], out_specs=[], out_shape=[], args=[],
                 exchanges=[x])[1][0]


def _x_gather(bufs, ici=None, d2d=None, cross=None):
    n = len(bufs)
    plan = [(a, kind, rows[a]) for a in range(n) for kind, rows in (("ici", ici), ("d2d", d2d), ("cross", cross))
            if rows is not None and rows[a] is not None]

    def build(srcs, outs, ss, rs):
        x, y, c = _place()
        cps = []
        for q, (a, kind, rows) in enumerate(plan):
            piece = pl.ds(*rows)
            for j, (px, py) in enumerate(_other_chips(x, y)):
                if kind == "d2d":
                    slot, to = 4 * px + 2 * py + c, (x, y, 1 - c)
                else:
                    slot, to = 4 * x + 2 * y + c, (px, py, c if kind == "ici" else 1 - c)
                cps.append(_remote(srcs[a].at[slot, piece], outs[a].at[slot, piece], ss, rs, 3 * q + j, to))
        return cps

    outs = [jax.ShapeDtypeStruct(b.shape, b.dtype) for b in bufs]
    return _Exchange(bufs, outs, 3 * len(plan), build, aliases={a: a for a in range(n)})


def _x_pair(grads, halves_last=False):
    n = len(grads)

    def build(srcs, outs, ss, rs):
        x, y, c = _place()

        def half(r):
            if not halves_last:
                return r.at[:, 1 - c]
            ch = r.shape[2] // 2
            return r.at[:, :, pl.ds(pl.multiple_of((1 - c) * ch, 128), ch)]

        return [_remote(half(srcs[a]), outs[a], ss, rs, a, (x, y, 1 - c)) for a in range(n)]

    if halves_last:
        outs = [jax.ShapeDtypeStruct(g.shape[:2] + (g.shape[2] // 2,), g.dtype) for g in grads]
    else:
        outs = [jax.ShapeDtypeStruct((4,) + g.shape[2:], g.dtype) for g in grads]
    return _Exchange(grads, outs, n, build)


def _x_chip(wires, rows=None, into=None):
    n = len(wires)
    rows = rows or [(0, w.shape[1]) for w in wires]

    def build(srcs, outs, ss, rs):
        x, y, c = _place()
        cps = []
        for a in range(n):
            piece = pl.ds(*rows[a])
            for j, (px, py) in enumerate(_other_chips(x, y)):
                cps.append(_remote(srcs[a].at[2 * px + py, piece], outs[a].at[j, piece], ss, rs,
                                   3 * a + j, (px, py, c)))
        return cps

    outs = [jax.ShapeDtypeStruct((3,) + w.shape[1:], w.dtype) for w in wires]
    if into is None:
        return _Exchange(wires, outs, 3 * n, build)
    return _Exchange(list(wires) + list(into), outs, 3 * n, build, aliases={n + a: a for a in range(n)})


def _x_share(halves):
    n = len(halves)

    def build(srcs, outs, ss, rs):
        x, y, c = _place()
        return [_remote(srcs[a], outs[a], ss, rs, a, (x, y, 1 - c)) for a in range(n)]

    outs = [jax.ShapeDtypeStruct(h.shape, h.dtype) for h in halves]
    return _Exchange(halves, outs, n, build)


def _mm(a, w, *, tm, tn, tk, out_dtype, name, a_square=False, relu=False, mul2=None, w_layout="kn",
        m_blocks=None, out_into=None, exchanges=()):
    m, k = a.shape
    m_first, m_count = m_blocks or (0, m // tm)
    a_spec = pl.BlockSpec((tm, tk), lambda i, j, kk: (i + m_first, kk))
    if w_layout == "kn":
        n = w.shape[1]
        w_spec = pl.BlockSpec((tk, tn), lambda i, j, kk: (kk, j))
    elif w_layout == "nk":
        n = w.shape[0]
        w_spec = pl.BlockSpec((tn, tk), lambda i, j, kk: (j, kk))
    elif w_layout == "skn":
        n = w.shape[0] * w.shape[2]
        per_n = w.shape[2] // tn
        w_spec = pl.BlockSpec((None, tk, tn), lambda i, j, kk: (j // per_n, kk, j % per_n))
    else:
        assert w_layout == "snk"
        n = w.shape[1]
        per_k = w.shape[2] // tk
        w_spec = pl.BlockSpec((None, tn, tk), lambda i, j, kk: (kk // per_k, j, kk % per_k))
    w_dims = NT if w_layout in ("nk", "snk") else (((1,), (0,)), ((), ()))
    nk = k // tk
    assert m % tm == 0 and n % tn == 0 and k % tk == 0

    def body(*refs):
        if mul2 is not None:
            a_ref, w_ref, e_ref, o_ref, acc_ref = refs
        else:
            a_ref, w_ref, o_ref, acc_ref = refs
            e_ref = None
        kk = pl.program_id(2)
        av = a_ref[...]
        if a_square:
            af = av.astype(F32)
            av = (af * af).astype(BF16)
        part = lax.dot_general(av, w_ref[...], w_dims, preferred_element_type=F32)

        def finish(r):
            if relu:
                r = jnp.maximum(r, 0.0)
            if e_ref is not None:
                r = 2.0 * e_ref[...].astype(F32) * r
            o_ref[...] = r.astype(out_dtype)

        if nk == 1:
            finish(part)
        else:
            @pl.when(kk == 0)
            def _():
                acc_ref[...] = part

            @pl.when(kk > 0)
            def _():
                acc_ref[...] += part

            @pl.when(kk == nk - 1)
            def _():
                finish(acc_ref[...])

    in_specs = [a_spec, w_spec]
    args = [a, w]
    if mul2 is not None:
        in_specs.append(pl.BlockSpec((tm, tn), lambda i, j, kk: (i + m_first, j)))
        args.append(mul2)
    acc_shape = (tm, tn) if nk > 1 else (8, 128)
    (out,), per = _call(
        body, name=name, grid=(m_count, n // tn, nk),
        in_specs=in_specs, out_specs=[pl.BlockSpec((tm, tn), lambda i, j, kk: (i + m_first, j))],
        out_shape=[jax.ShapeDtypeStruct((m, n), out_dtype)], args=args,
        scratch_shapes=[pltpu.VMEM(acc_shape, F32)],
        semantics=("parallel", "parallel", "arbitrary"), exchanges=exchanges,
        into=None if out_into is None else {0: out_into})
    return (out, per) if exchanges else out


def _mm_tn(a, b, *, tm, tn, tt, name, a_square=False, n_split=1, b_blocks=None, exchanges=()):
    t, m = a.shape
    nb = len(b_blocks) if b_blocks else 1
    n = tn if b_blocks else b.shape[1]
    assert t % tt == 0 and m % tm == 0 and n % tn == 0 and (n // n_split) % tn == 0
    per = n // n_split // tn

    def body(a_ref, *refs):
        b_refs, o_ref = refs[:nb], refs[nb]
        ti = pl.program_id(2)
        av = a_ref[...]
        if a_square:
            af = av.astype(F32)
            av = (af * af).astype(BF16)
        bv = b_refs[0][...] if nb == 1 else jnp.concatenate([r[...] for r in b_refs], axis=1)
        part = lax.dot_general(av, bv, TN, preferred_element_type=F32)

        @pl.when(ti == 0)
        def _():
            o_ref[...] = part

        @pl.when(ti > 0)
        def _():
            o_ref[...] += part

    if b_blocks:
        b_specs = [pl.BlockSpec((tt, tn // nb), functools.partial(lambda blk, i, j, ti: (ti, blk), blk))
                   for blk in b_blocks]
    else:
        b_specs = [pl.BlockSpec((tt, tn), lambda i, j, ti: (ti, j))]
    (out,), xres = _call(
        body, name=name, grid=(m // tm, n // tn, t // tt),
        in_specs=[pl.BlockSpec((tt, tm), lambda i, j, ti: (ti, i))] + b_specs,
        out_specs=[pl.BlockSpec((None, tm, tn), lambda i, j, ti: (j // per, i, j % per))],
        out_shape=[jax.ShapeDtypeStruct((n_split, m, n // n_split), F32)], args=[a] + [b] * nb,
        semantics=("parallel", "parallel", "arbitrary"), exchanges=exchanges)
    return (out, xres) if exchanges else out


def _rstd(x):
    return lax.rsqrt(jnp.mean(x * x, axis=-1, keepdims=True) + EPS)


def _rms_cast_gather(x, g, buf, *, tm, name):
    t, d = x.shape
    steps = t // tm

    def body(x_ref, g_ref, b_in, o_ref, b_out, send_sems, recv_sems):
        i = pl.program_id(0)
        xc, yc, c = _place()
        chips = _other_chips(xc, yc)

        def slot(px, py, pc):
            return b_out.at[4 * px + 2 * py + pc]

        def sent(j):
            return _remote(b_in.at[4 * xc + 2 * yc + c], slot(xc, yc, c), send_sems, recv_sems, j, (*chips[j], c))

        def passed(j):
            return _remote(slot(*chips[j], c), slot(*chips[j], c), send_sems, recv_sems, 3 + j, (xc, yc, 1 - c))

        @pl.when(i == 0)
        def _():
            for j in range(3):
                sent(j).start()

        xv = x_ref[...]
        o_ref[...] = (xv * _rstd(xv) * g_ref[...]).astype(BF16)

        @pl.when(i == steps - 1)
        def _():
            for j in range(3):
                sent(j).wait_recv()
                passed(j).start()
            for j in range(3):
                passed(j).wait_recv()
                passed(j).wait_send()
                sent(j).wait_send()

    return pl.pallas_call(
        body, name=name, grid=(steps,),
        in_specs=[pl.BlockSpec((tm, d), lambda i: (i, 0)), pl.BlockSpec((1, d), lambda i: (0, 0)), ANY],
        out_specs=[pl.BlockSpec((tm, d), lambda i: (i, 0)), ANY],
        out_shape=[jax.ShapeDtypeStruct((t, d), BF16), jax.ShapeDtypeStruct(buf.shape, buf.dtype)],
        scratch_shapes=[pltpu.SemaphoreType.DMA((6,)), pltpu.SemaphoreType.DMA((6,))],
        input_output_aliases={2: 1},
        compiler_params=_params(("arbitrary",)),
    )(x, g, buf)


def _mix_cat(attn, rnn, gain, *, tm, name):
    t = attn.shape[0]

    def body(a_ref, r_ref, g_ref, o_ref):
        av = a_ref[...]
        o_ref[:, :ATTN_W] = (av * _rstd(av) * g_ref[...]).astype(BF16)
        o_ref[:, ATTN_W:] = r_ref[...].astype(BF16)

    return pl.pallas_call(
        body, name=name, grid=(t // tm,),
        in_specs=[pl.BlockSpec((tm, ATTN_W), lambda i: (i, 0)), pl.BlockSpec((tm, RNN_W), lambda i: (i, 0)),
                  pl.BlockSpec((1, ATTN_W), lambda i: (0, 0))],
        out_specs=pl.BlockSpec((tm, D_MODEL), lambda i: (i, 0)),
        out_shape=jax.ShapeDtypeStruct((t, D_MODEL), BF16),
        compiler_params=_params(("parallel",)),
    )(attn, rnn, gain)


def _post_norm_res(mixed, g_post, res, g_next, *, tm, name, exchanges=()):
    t, d = mixed.shape

    def body(m_ref, gp_ref, r_ref, gn_ref, x1_ref, h2_ref):
        mv = m_ref[...].astype(F32)
        x1 = r_ref[...] + mv * _rstd(mv) * gp_ref[...]
        x1_ref[...] = x1
        h2_ref[...] = (x1 * _rstd(x1) * gn_ref[...]).astype(BF16)

    row = pl.BlockSpec((tm, d), lambda i: (i, 0))
    vec = pl.BlockSpec((1, d), lambda i: (0, 0))
    res_, xres = _call(
        body, name=name, grid=(t // tm,),
        in_specs=[row, vec, row, vec], out_specs=[row, row],
        out_shape=[jax.ShapeDtypeStruct((t, d), F32), jax.ShapeDtypeStruct((t, d), BF16)],
        args=[mixed, g_post, res, g_next], semantics=("parallel",), exchanges=exchanges)
    return (*res_, xres) if exchanges else res_


def _rms_bwd(dyn, xin, g, res, *, tm, out_dtype, name, col_block=0, exchanges=()):
    t, d = xin.shape

    def body(*refs):
        if res is not None:
            dy_ref, x_ref, g_ref, r_ref, dx_ref, dg_ref = refs
        else:
            dy_ref, x_ref, g_ref, dx_ref, dg_ref = refs
        i = pl.program_id(0)
        xv = x_ref[...].astype(F32)
        dy = dy_ref[...].astype(F32)
        r = _rstd(xv)
        xh = xv * r
        part = jnp.sum(dy * xh, axis=0, keepdims=True)

        @pl.when(i == 0)
        def _():
            dg_ref[...] = part

        @pl.when(i > 0)
        def _():
            dg_ref[...] += part

        tt = dy * g_ref[...]
        dx = r * (tt - xh * jnp.mean(tt * xh, axis=-1, keepdims=True))
        if res is not None:
            dx = dx + r_ref[...]
        dx_ref[...] = dx.astype(out_dtype)

    row = pl.BlockSpec((tm, d), lambda i: (i, 0))
    vec = pl.BlockSpec((1, d), lambda i: (0, 0))
    in_specs = [pl.BlockSpec((tm, d), lambda i: (i, col_block)), row, vec]
    args = [dyn, xin, g]
    if res is not None:
        in_specs.append(row)
        args.append(res)
    res, xres = _call(
        body, name=name, grid=(t // tm,),
        in_specs=in_specs, out_specs=[row, vec],
        out_shape=[jax.ShapeDtypeStruct((t, d), out_dtype), jax.ShapeDtypeStruct((1, d), F32)], args=args,
        semantics=("arbitrary",), exchanges=exchanges)
    return (*res, xres) if exchanges else res


def _loss_head(y, g_post, x1, target, *, tm, name):
    t, d = y.shape

    def body(y_ref, g_ref, x1_ref, t_ref, dy_ref, dx2_ref, loss_ref, dg_ref):
        i = pl.program_id(0)
        yv = y_ref[...].astype(F32)
        r = _rstd(yv)
        yh = yv * r
        gv = g_ref[...]
        err = x1_ref[...] + yh * gv - t_ref[...]
        lpart = 0.5 * jnp.sum(jnp.mean(err * err, axis=-1, keepdims=True), axis=0, keepdims=True)
        dx2 = err * (1.0 / d)
        dgp = jnp.sum(dx2 * yh, axis=0, keepdims=True)
        lane = lax.broadcasted_iota(jnp.int32, (1, 128), 1)
        lrow = jnp.where(lane == 0, lpart, 0.0)

        @pl.when(i == 0)
        def _():
            dg_ref[...] = dgp
            loss_ref[...] = lrow

        @pl.when(i > 0)
        def _():
            dg_ref[...] += dgp
            loss_ref[...] += lrow

        tt = dx2 * gv
        dy_ref[...] = (r * (tt - yh * jnp.mean(tt * yh, axis=-1, keepdims=True))).astype(BF16)
        dx2_ref[...] = dx2

    row = pl.BlockSpec((tm, d), lambda i: (i, 0))
    vec = pl.BlockSpec((1, d), lambda i: (0, 0))
    return pl.pallas_call(
        body, name=name, grid=(t // tm,),
        in_specs=[row, vec, row, row],
        out_specs=[row, row, pl.BlockSpec((1, 128), lambda i: (0, 0)), vec],
        out_shape=[jax.ShapeDtypeStruct((t, d), BF16), jax.ShapeDtypeStruct((t, d), F32),
                   jax.ShapeDtypeStruct((1, 128), F32), jax.ShapeDtypeStruct((1, d), F32)],
        compiler_params=_params(("arbitrary",)),
    )(y, g_post, x1, target)


def _alibi_slope(h):
    return 2.0 ** (-8.0 * (h + 1) / N_Q)


PAIR = 2 * HEAD_DIM
N_PAIRS = N_Q // 2
PAIRS_PER_KV = GROUP // 2
SMEM = pl.BlockSpec(memory_space=pltpu.SMEM)


def _swa_mask(n):
    key = lax.broadcasted_iota(jnp.int32, (2 * BLK, BLK), 0)
    qry = lax.broadcasted_iota(jnp.int32, (2 * BLK, BLK), 1)
    dist = qry + BLK - key
    valid = (dist >= 0) & (dist < BLK) & ((key >= BLK) | (n > 0))
    return valid, dist.astype(F32)


def _block_diag(kvp_ref, kvc_ref, off):
    a = jnp.concatenate([kvp_ref[:, off:off + HEAD_DIM], kvc_ref[:, off:off + HEAD_DIM]], axis=0).astype(BF16)
    z = jnp.zeros_like(a)
    return jnp.concatenate([jnp.concatenate([a, z], axis=1), jnp.concatenate([z, a], axis=1)], axis=0)


def _swa_scores(s2, e, hh, valid, distf):
    s = s2[2 * BLK * e:2 * BLK * (e + 1)] * (HEAD_DIM ** -0.5) - _alibi_slope(hh) * distf
    return jnp.where(valid, s, -1e30)


def _swa_fwd(proj, sinks, *, name, exchanges=()):
    t = proj.shape[0]
    nb = t // BLK
    kvb = KV_COL // (2 * 128)

    def body(sink_ref, q_ref, kvc_ref, kvp_ref, o_ref, lse_ref):
        n = pl.program_id(0)
        valid, distf = _swa_mask(n)
        for kvh in range(N_KV):
            k2 = _block_diag(kvp_ref, kvc_ref, kvh * HEAD_DIM)
            v2 = _block_diag(kvp_ref, kvc_ref, 128 + kvh * HEAD_DIM)
            for jp in range(PAIRS_PER_KV):
                pair = kvh * PAIRS_PER_KV + jp
                lanes = slice(pair * PAIR, (pair + 1) * PAIR)
                s2 = lax.dot_general(k2, q_ref[:, lanes].astype(BF16), NT, preferred_element_type=F32)
                probs = []
                for e in range(2):
                    hh = 2 * pair + e
                    s = _swa_scores(s2, e, hh, valid, distf)
                    sink = sink_ref[0, hh]
                    mx = jnp.maximum(jnp.max(s, axis=0, keepdims=True), sink)
                    p = jnp.exp(s - mx)
                    l = jnp.sum(p, axis=0, keepdims=True) + jnp.exp(sink - mx)
                    probs.append((p * (1.0 / l)).astype(BF16))
                    lse_ref[hh:hh + 1, :] = mx + jnp.log(l)
                o_ref[:, lanes] = lax.dot_general(jnp.concatenate(probs, axis=0), v2, TN,
                                                  preferred_element_type=F32)

    res, xres = _call(
        body, name=name, grid=(nb,),
        in_specs=[SMEM,
                  pl.BlockSpec((BLK, ATTN_W), lambda n: (n, 0)),
                  pl.BlockSpec((BLK, 256), lambda n: (n, kvb)),
                  pl.BlockSpec((BLK, 256), lambda n: (jnp.maximum(n - 1, 0), kvb))],
        out_specs=[pl.BlockSpec((BLK, ATTN_W), lambda n: (n, 0)),
                   pl.BlockSpec((None, N_Q, BLK), lambda n: (n, 0, 0))],
        out_shape=[jax.ShapeDtypeStruct((t, ATTN_W), F32), jax.ShapeDtypeStruct((nb, N_Q, BLK), F32)],
        args=[sinks, proj, proj, proj], semantics=("parallel",), exchanges=exchanges)
    return (*res, xres) if exchanges else res


def _swa_bwd(proj, sinks, dattn, lse, *, name, exchanges=()):
    t = proj.shape[0]
    nb = t // BLK
    kvb = KV_COL // (2 * 128)

    def body(sink_ref, q_ref, kvc_ref, kvp_ref, do_ref, lse_ref, dq_ref, dkv_ref, dsink_ref, carry_ref):
        n = pl.program_id(0)

        @pl.when(n == 0)
        def _():
            dsink_ref[...] = jnp.zeros_like(dsink_ref)
            carry_ref[...] = jnp.zeros_like(carry_ref)

        @pl.when(n < nb)
        def _():
            valid, distf = _swa_mask(n)
            for kvh in range(N_KV):
                k2 = _block_diag(kvp_ref, kvc_ref, kvh * HEAD_DIM)
                v2 = _block_diag(kvp_ref, kvc_ref, 128 + kvh * HEAD_DIM)
                dk2 = jnp.zeros((4 * BLK, PAIR), F32)
                dv2 = jnp.zeros((4 * BLK, PAIR), F32)
                for jp in range(PAIRS_PER_KV):
                    pair = kvh * PAIRS_PER_KV + jp
                    lanes = slice(pair * PAIR, (pair + 1) * PAIR)
                    q2 = q_ref[:, lanes].astype(BF16)
                    do2 = do_ref[:, lanes].astype(BF16)
                    s2 = lax.dot_general(k2, q2, NT, preferred_element_type=F32)
                    dp2 = lax.dot_general(v2, do2, NT, preferred_element_type=F32)
                    probs, dss = [], []
                    for e in range(2):
                        hh = 2 * pair + e
                        lse_h = lse_ref[hh:hh + 1, :]
                        p = jnp.exp(_swa_scores(s2, e, hh, valid, distf) - lse_h)
                        dp = dp2[2 * BLK * e:2 * BLK * (e + 1)]
                        delta = jnp.sum(p * dp, axis=0, keepdims=True)
                        dsink_ref[hh:hh + 1, :] += -jnp.exp(sink_ref[0, hh] - lse_h) * delta
                        probs.append(p.astype(BF16))
                        dss.append((p * (dp - delta)).astype(BF16))
                    ds2 = jnp.concatenate(dss, axis=0)
                    dq_ref[:, lanes] = (lax.dot_general(ds2, k2, TN, preferred_element_type=F32)
                                        * (HEAD_DIM ** -0.5)).astype(BF16)
                    dk2 = dk2 + jnp.dot(ds2, q2, preferred_element_type=F32)
                    dv2 = dv2 + jnp.dot(jnp.concatenate(probs, axis=0), do2, preferred_element_type=F32)
                dk_cat = (dk2[:2 * BLK, :HEAD_DIM] + dk2[2 * BLK:, HEAD_DIM:]) * (HEAD_DIM ** -0.5)
                dv_cat = dv2[:2 * BLK, :HEAD_DIM] + dv2[2 * BLK:, HEAD_DIM:]
                ko = kvh * HEAD_DIM
                vo = 128 + kvh * HEAD_DIM
                dkv_ref[:, ko:ko + HEAD_DIM] = (carry_ref[:, ko:ko + HEAD_DIM] + dk_cat[:BLK]).astype(BF16)
                dkv_ref[:, vo:vo + HEAD_DIM] = (carry_ref[:, vo:vo + HEAD_DIM] + dv_cat[:BLK]).astype(BF16)
                carry_ref[:, ko:ko + HEAD_DIM] = dk_cat[BLK:]
                carry_ref[:, vo:vo + HEAD_DIM] = dv_cat[BLK:]

        @pl.when(n == nb)
        def _():
            dkv_ref[...] = carry_ref[...].astype(BF16)

    last = nb - 1
    res, xres = _call(
        body, name=name, grid=(nb + 1,),
        in_specs=[SMEM,
                  pl.BlockSpec((BLK, ATTN_W), lambda n: (jnp.minimum(n, last), 0)),
                  pl.BlockSpec((BLK, 256), lambda n: (jnp.minimum(n, last), kvb)),
                  pl.BlockSpec((BLK, 256), lambda n: (jnp.maximum(jnp.minimum(n, last) - 1, 0), kvb)),
                  pl.BlockSpec((BLK, ATTN_W), lambda n: (jnp.minimum(n, last), 0)),
                  pl.BlockSpec((None, N_Q, BLK), lambda n: (jnp.minimum(n, last), 0, 0))],
        out_specs=[pl.BlockSpec((BLK, ATTN_W), lambda n: (jnp.minimum(n, last), 0)),
                   pl.BlockSpec((BLK, 256), lambda n: (jnp.maximum(n - 1, 0), 0)),
                   pl.BlockSpec((N_Q, BLK), lambda n: (0, 0))],
        out_shape=[jax.ShapeDtypeStruct((t, ATTN_W), BF16), jax.ShapeDtypeStruct((t, 256), BF16),
                   jax.ShapeDtypeStruct((N_Q, BLK), F32)],
        scratch_shapes=[pltpu.VMEM((BLK, 256), F32)],
        args=[sinks, proj, proj, proj, dattn, lse], semantics=("arbitrary",), exchanges=exchanges)
    return (*res, xres) if exchanges else res


def _cumsum_rows(x):
    n = x.shape[0]
    row = lax.broadcasted_iota(jnp.int32, x.shape, 0)
    s = 1
    while s < n:
        x = x + jnp.where(row >= s, pltpu.roll(x, s, axis=0), 0.0)
        s *= 2
    return x


def _rev_cumsum_rows(x):
    n = x.shape[0]
    row = lax.broadcasted_iota(jnp.int32, x.shape, 0)
    s = 1
    while s < n:
        x = x + jnp.where(row < n - s, pltpu.roll(x, n - s, axis=0), 0.0)
        s *= 2
    return x


def _lower_bound(lbl_ref):
    l0 = lbl_ref[0:1, :]
    l1 = lbl_ref[1:2, :]
    mx = jnp.maximum(l0, l1)
    e0 = jnp.exp(l0 - mx)
    e1 = jnp.exp(l1 - mx)
    return e0 / (e0 + e1)


def _hgrn_gates(z, lb):
    sg = _sigmoid(z)
    f = lb + (1.0 - lb) * sg
    return sg, f, jnp.log(f), 1.0 - f


def _sub_factors(b, k, i, sub, trim):
    need = -(-sub * i // 16) * 16 if trim else CHUNK
    rows = lax.broadcasted_iota(jnp.int32, (need, RNN_HD), 0)
    ref = b[sub * i - 1:sub * i, :]
    qfac = jnp.exp(b[sub * i:sub * (i + 1), :] - ref)
    kfac = jnp.where(rows < sub * i, jnp.exp(ref - b[:need]), 0.0)
    kt = (k[:need] * kfac).astype(BF16)
    if need < CHUNK:
        kt = jnp.concatenate([kt, jnp.zeros((CHUNK - need, RNN_HD), BF16)], axis=0)
    return qfac, kfac, kt


def _diag_decay(bi, s):
    trow = lax.broadcasted_iota(jnp.int32, bi.shape, 0)
    return jnp.where(trow >= s, jnp.exp(bi - bi[s:s + 1, :]), 0.0)


def _hgrn_fwd(proj, lb_logits, norm_gain, *, tb, name, exchanges=()):
    t = proj.shape[0]
    ntb = t // tb
    nch = tb // CHUNK
    qb, fb, ib, gb = QR_COL // 128, FR_COL // 128, IR_COL // 128, GR_COL // 128

    def body(q_ref, f_ref, i_ref, g_ref, lbl_ref, gain_ref, o_ref, out_ref, s0_ref, st_ref):
        c = pl.program_id(1)

        @pl.when(c == 0)
        def _():
            st_ref[...] = jnp.zeros_like(st_ref)

        lb = _lower_bound(lbl_ref)
        gain = gain_ref[...]

        def chunk(ci, st):
            rows = slice(ci * CHUNK, (ci + 1) * CHUNK)
            _, _, lf, k = _hgrn_gates(f_ref[rows, :], lb)
            qr = q_ref[rows, :]
            q = qr * _sigmoid(qr)
            v = i_ref[rows, :]
            b = _cumsum_rows(lf)
            s0_ref[ci] = st
            o_inter = lax.dot_general((q * jnp.exp(b)).astype(BF16), st.astype(BF16), NT,
                                      preferred_element_type=F32)
            vb = v.astype(BF16)
            blast = b[CHUNK - 1:CHUNK, :]
            khat = (k * jnp.exp(blast - b)).astype(BF16)
            st = st * jnp.exp(blast) + lax.dot_general(vb, khat, TN, preferred_element_type=F32)
            blocks = []
            for i in range(CHUNK // SUB_FWD):
                blk = slice(SUB_FWD * i, SUB_FWD * (i + 1))
                qi, ki, vi, bi = q[blk], k[blk], v[blk], b[blk]
                oi = o_inter[blk]
                if i > 0:
                    qfac, _, kt = _sub_factors(b, k, i, SUB_FWD, trim=True)
                    att = lax.dot_general((qi * qfac).astype(BF16), kt, NT,
                                          preferred_element_type=F32)
                    oi = oi + jnp.dot(att.astype(BF16), vb, preferred_element_type=F32)
                for s in range(SUB_FWD):
                    qe = qi * _diag_decay(bi, s)
                    a = jnp.sum(qe * ki[s:s + 1, :], axis=1, keepdims=True)
                    oi = oi + a * vi[s:s + 1, :]
                blocks.append(oi)
            o = jnp.concatenate(blocks, axis=0)
            o_ref[rows, :] = o
            gr = g_ref[rows, :]
            out_ref[rows, :] = o * _rstd(o) * gain * (gr * _sigmoid(gr))
            return st

        st = st_ref[...]
        for ci in range(nch):
            st = chunk(ci, st)
        st_ref[...] = st

    def col(base):
        return pl.BlockSpec((tb, RNN_HD), lambda h, c: (c, base + h))

    res, xres = _call(
        body, name=name, grid=(N_RNN, ntb),
        in_specs=[col(qb), col(fb), col(ib), col(gb),
                  pl.BlockSpec((2, RNN_HD), lambda h, c: (0, h)), pl.BlockSpec((1, RNN_HD), lambda h, c: (0, 0))],
        out_specs=[pl.BlockSpec((tb, RNN_HD), lambda h, c: (c, h)), pl.BlockSpec((tb, RNN_HD), lambda h, c: (c, h)),
                   pl.BlockSpec((None, nch, RNN_HD, RNN_HD), lambda h, c: (h, c, 0, 0))],
        out_shape=[jax.ShapeDtypeStruct((t, RNN_W), F32), jax.ShapeDtypeStruct((t, RNN_W), F32),
                   jax.ShapeDtypeStruct((N_RNN, t // CHUNK, RNN_HD, RNN_HD), F32)],
        scratch_shapes=[pltpu.VMEM((RNN_HD, RNN_HD), F32)],
        args=[proj, proj, proj, proj, lb_logits, norm_gain],
        semantics=("parallel", "arbitrary"), exchanges=exchanges)
    return (*res, xres) if exchanges else res


def _hgrn_bwd(proj, lb_logits, norm_gain, o_pre, s0, dcat, *, tb, name, exchanges=()):
    t = proj.shape[0]
    ntb = t // tb
    nch = tb // CHUNK
    qb, fb, ib, gb = QR_COL // 128, FR_COL // 128, IR_COL // 128, GR_COL // 128
    sub = SUB_BWD
    nsub = CHUNK // sub

    def body(q_ref, f_ref, i_ref, g_ref, lbl_ref, gain_ref, o_ref, s0_ref, dout_ref,
             dq_ref, df_ref, di_ref, dg_ref, dlb_ref, dgain_ref,
             dst_ref, dqs_ref, dks_ref, dvs_ref):
        c = pl.program_id(1)

        @pl.when(c == 0)
        def _():
            dst_ref[...] = jnp.zeros_like(dst_ref)
            dlb_ref[...] = jnp.zeros_like(dlb_ref)
            dgain_ref[...] = jnp.zeros_like(dgain_ref)

        lb = _lower_bound(lbl_ref)
        gain = gain_ref[...]

        def chunk(ci, dst):
            rows = slice(ci * CHUNK, (ci + 1) * CHUNK)
            dqa_ref, dka_ref, dva_ref = dqs_ref.at[ci], dks_ref.at[ci], dvs_ref.at[ci]
            sg, f, lf, k = _hgrn_gates(f_ref[rows, :], lb)
            qr = q_ref[rows, :]
            sq = _sigmoid(qr)
            q = qr * sq
            v = i_ref[rows, :]
            b = _cumsum_rows(lf)

            dout = dout_ref[rows, :].astype(F32)
            o = o_ref[rows, :]
            gr = g_ref[rows, :]
            sgg = _sigmoid(gr)
            gate = gr * sgg
            rs = _rstd(o)
            nrm = o * rs
            dg_ref[rows, :] = (dout * nrm * gain * (sgg * (1.0 + gr * (1.0 - sgg)))).astype(BF16)
            dn = dout * gate
            dgain_ref[...] += jnp.sum(dn * nrm, axis=0, keepdims=True)
            tt = dn * gain
            do = rs * (tt - nrm * jnp.mean(tt * nrm, axis=-1, keepdims=True))

            dob = do.astype(BF16)
            vb = v.astype(BF16)
            eb = jnp.exp(b)
            blast = b[CHUNK - 1:CHUNK, :]
            ebl = jnp.exp(blast - b)
            dstb = dst.astype(BF16)
            khat = (k * ebl).astype(BF16)
            s0 = s0_ref[ci]
            dqa_ref[...] = eb * jnp.dot(dob, s0.astype(BF16), preferred_element_type=F32)
            dk_state = ebl * jnp.dot(vb, dstb, preferred_element_type=F32)
            dka_ref[...] = dk_state
            d_blast = (jnp.sum(k * dk_state, axis=0, keepdims=True)
                       + jnp.exp(blast) * jnp.sum(dst * s0, axis=0, keepdims=True))
            dva_ref[...] = lax.dot_general(khat, dstb, NT, preferred_element_type=F32)
            dst_next = dst * jnp.exp(blast) + lax.dot_general(dob, (q * eb).astype(BF16), TN,
                                                              preferred_element_type=F32)
            pm = lax.dot_general(dob, vb, NT, preferred_element_type=F32)
            for i in range(nsub):
                blk = slice(sub * i, sub * (i + 1))
                qi, ki, vi, bi, doi = q[blk], k[blk], v[blk], b[blk], do[blk]
                dqi = dqa_ref[blk, :]
                if i > 0:
                    qfac, kfac, kt = _sub_factors(b, k, i, sub, trim=False)
                    qt = (qi * qfac).astype(BF16)
                    att = lax.dot_general(qt, kt, NT, preferred_element_type=F32).astype(BF16)
                    pmi = pm[blk, :].astype(BF16)
                    dva_ref[...] += lax.dot_general(att, doi.astype(BF16), TN, preferred_element_type=F32)
                    dqi = dqi + qfac * jnp.dot(pmi, kt, preferred_element_type=F32)
                    dka_ref[...] += kfac * lax.dot_general(pmi, qt, TN, preferred_element_type=F32)
                dqa_ref[blk, :] = dqi
                srow = lax.broadcasted_iota(jnp.int32, (sub, RNN_HD), 0)
                dki = jnp.zeros((sub, RNN_HD), F32)
                dvi = jnp.zeros((sub, RNN_HD), F32)
                for tq in range(sub):
                    qt, dot_ = qi[tq:tq + 1, :], doi[tq:tq + 1, :]
                    e = jnp.where(srow <= tq, jnp.exp(bi[tq:tq + 1, :] - bi), 0.0)
                    ke = ki * e
                    p = jnp.sum(vi * dot_, axis=1, keepdims=True)
                    a = jnp.sum(ke * qt, axis=1, keepdims=True)
                    dki = dki + p * (qt * e)
                    dvi = dvi + a * dot_
                    row = slice(sub * i + tq, sub * i + tq + 1)
                    dqa_ref[row, :] += jnp.sum(p * ke, axis=0, keepdims=True)
                dka_ref[blk, :] += dki
                dva_ref[blk, :] += dvi

            dq = dqa_ref[...]
            dk = dka_ref[...]
            lastrow = lax.broadcasted_iota(jnp.int32, (CHUNK, RNN_HD), 0) == CHUNK - 1
            dlf = _rev_cumsum_rows(q * dq - k * dk + jnp.where(lastrow, d_blast, 0.0))
            dff = dlf / f - dk
            df_ref[rows, :] = (dff * (1.0 - lb) * sg * (1.0 - sg)).astype(BF16)
            dlb_ref[...] += jnp.sum(dff * (1.0 - sg), axis=0, keepdims=True)
            dq_ref[rows, :] = (dq * (sq * (1.0 + qr * (1.0 - sq)))).astype(BF16)
            di_ref[rows, :] = dva_ref[...].astype(BF16)
            return dst_next

        dst = dst_ref[...]
        for ci in reversed(range(nch)):
            dst = chunk(ci, dst)
        dst_ref[...] = dst

    def col(base):
        return pl.BlockSpec((tb, RNN_HD), lambda h, c: (ntb - 1 - c, base + h))

    outc = pl.BlockSpec((tb, RNN_HD), lambda h, c: (ntb - 1 - c, h))
    hb = ATTN_W // RNN_HD
    res, xres = _call(
        body, name=name, grid=(N_RNN, ntb),
        in_specs=[col(qb), col(fb), col(ib), col(gb),
                  pl.BlockSpec((2, RNN_HD), lambda h, c: (0, h)), pl.BlockSpec((1, RNN_HD), lambda h, c: (0, 0)),
                  outc,
                  pl.BlockSpec((None, nch, RNN_HD, RNN_HD), lambda h, c: (h, ntb - 1 - c, 0, 0)),
                  pl.BlockSpec((tb, RNN_HD), lambda h, c: (ntb - 1 - c, hb + h))],
        out_specs=[outc, outc, outc, outc,
                   pl.BlockSpec((1, RNN_HD), lambda h, c: (0, h)),
                   pl.BlockSpec((None, 1, RNN_HD), lambda h, c: (h, 0, 0))],
        out_shape=[jax.ShapeDtypeStruct((t, RNN_W), BF16)] * 4
        + [jax.ShapeDtypeStruct((1, RNN_W), F32), jax.ShapeDtypeStruct((N_RNN, 1, RNN_HD), F32)],
        scratch_shapes=[pltpu.VMEM((RNN_HD, RNN_HD), F32),
                        pltpu.VMEM((nch, CHUNK, RNN_HD), F32), pltpu.VMEM((nch, CHUNK, RNN_HD), F32),
                        pltpu.VMEM((nch, CHUNK, RNN_HD), F32)],
        args=[proj, proj, proj, proj, lb_logits, norm_gain, o_pre, s0, dcat],
        semantics=("parallel", "arbitrary"), exchanges=exchanges)
    return (*res, xres) if exchanges else res


def _cast_slots(w, where, *, name):
    _, rows, cols = w.shape
    rh = rows // 2
    tr = _row_tile(rh, cols)
    nh = rh // tr

    def body(wh_ref, w_ref, o_ref):
        o_ref[...] = w_ref[...].astype(BF16)

    return pl.pallas_call(
        body, name=name,
        grid_spec=pltpu.PrefetchScalarGridSpec(
            num_scalar_prefetch=1, grid=(2, nh),
            in_specs=[pl.BlockSpec((None, tr, cols), lambda h, i, wh: (0, h * nh + i, 0))],
            out_specs=pl.BlockSpec((None, tr, cols), lambda h, i, wh: (2 * wh[0] + h, i, 0))),
        out_shape=jax.ShapeDtypeStruct((8, rh, cols), BF16),
        compiler_params=_params(("parallel", "parallel")),
    )(where, w)


def _row_tile(rows, cols, budget=1 << 20):
    tr = rows
    while tr * cols > budget and tr % 16 == 0:
        tr //= 2
    return tr


def _half_spec(g, tr, halves_last, slab):
    if halves_last:
        return pl.BlockSpec((None, tr, g.shape[2] // 2), lambda *a: (slab(*a), a[-2], a[-1][1]))
    return pl.BlockSpec((None, None, tr, g.shape[3]), lambda *a: (slab(*a), a[-1][1], a[-2], 0))


def _pair_sum(g, sib, where, *, name, halves_last=False):
    rh, cols = sib.shape[1:]
    tr = _row_tile(rh, cols)

    def body(w_ref, g_ref, s_ref, o_ref):
        o_ref[...] = (g_ref[...] + s_ref[...]).astype(BF16)

    def foreign(s, i, w):
        return (w[0] + 1 + s) % N_CHIPS

    return pl.pallas_call(
        body, name=name,
        grid_spec=pltpu.PrefetchScalarGridSpec(
            num_scalar_prefetch=1, grid=(N_CHIPS - 1, rh // tr),
            in_specs=[_half_spec(g, tr, halves_last, foreign),
                      pl.BlockSpec((None, tr, cols), lambda s, i, w: (foreign(s, i, w), i, 0))],
            out_specs=pl.BlockSpec((None, tr, cols), lambda s, i, w: (foreign(s, i, w), i, 0))),
        out_shape=jax.ShapeDtypeStruct((4, rh, cols), BF16),
        compiler_params=_params(("parallel", "parallel")),
    )(where, g, sib)


def _final_half(g, sib, recv, where, *, name, halves_last=False):
    rh, cols = sib.shape[1:]
    tr = _row_tile(rh, cols)

    def body(w_ref, g_ref, s_ref, r_ref, o_ref):
        acc = g_ref[...] + s_ref[...]
        for j in range(3):
            acc = acc + r_ref[j].astype(F32)
        o_ref[...] = acc

    return pl.pallas_call(
        body, name=name,
        grid_spec=pltpu.PrefetchScalarGridSpec(
            num_scalar_prefetch=1, grid=(rh // tr,),
            in_specs=[_half_spec(g, tr, halves_last, lambda i, w: w[0]),
                      pl.BlockSpec((None, tr, cols), lambda i, w: (w[0], i, 0)),
                      pl.BlockSpec((3, tr, cols), lambda i, w: (0, i, 0))],
            out_specs=pl.BlockSpec((tr, cols), lambda i, w: (i, 0))),
        out_shape=jax.ShapeDtypeStruct((rh, cols), F32),
        compiler_params=_params(("parallel",)),
    )(where, g, sib, recv)


def _adamw_math(w, g, m, v):
    m = ADAM_B1 * m + (1.0 - ADAM_B1) * g
    v = ADAM_B2 * v + (1.0 - ADAM_B2) * (g * g)
    m_hat = m / (1.0 - ADAM_B1 ** ADAM_STEP)
    v_hat = v / (1.0 - ADAM_B2 ** ADAM_STEP)
    delta = -ADAM_LR * (m_hat / (jnp.sqrt(v_hat) + ADAM_EPS) + ADAM_WD * w)
    return delta, m, v


def _adamw(w, mine, theirs, m, v, where, *, name, halves_last=False):
    _, rows, cols = w.shape
    if halves_last:
        cols //= 2
        tr = _row_tile(rows, cols, budget=1 << 19)
        grid = (rows // tr, 2)
        blk = pl.BlockSpec((None, tr, cols), lambda i, h, wh: (0, i, h))
        mine_spec = theirs_spec = pl.BlockSpec((tr, cols), lambda i, h, wh: (i, 0))
        which = lambda: pl.program_id(1)
    else:
        tr = _row_tile(rows // 2, cols, budget=1 << 19)
        nh = rows // 2 // tr
        grid = (rows // tr,)
        blk = pl.BlockSpec((None, tr, cols), lambda i, wh: (0, i, 0))
        mine_spec = pl.BlockSpec((tr, cols), lambda i, wh: (jnp.where(i // nh == wh[1], i % nh, 0), 0))
        theirs_spec = pl.BlockSpec((tr, cols), lambda i, wh: (jnp.where(i // nh == wh[1], 0, i % nh), 0))
        which = lambda: pl.program_id(0) // nh

    def body(wh_ref, w_ref, a_ref, b_ref, m_ref, v_ref, g_ref, d_ref, nm_ref, nv_ref):
        g = jnp.where(which() == wh_ref[1], a_ref[...], b_ref[...])
        d, nm, nv = _adamw_math(w_ref[...], g, m_ref[...], v_ref[...])
        g_ref[...] = g
        d_ref[...] = d
        nm_ref[...] = nm
        nv_ref[...] = nv

    rows, cols = w.shape[1:]
    return pl.pallas_call(
        body, name=name,
        grid_spec=pltpu.PrefetchScalarGridSpec(
            num_scalar_prefetch=1, grid=grid,
            in_specs=[blk, mine_spec, theirs_spec, blk, blk], out_specs=[blk] * 4),
        out_shape=[jax.ShapeDtypeStruct((1, rows, cols), F32)] * 4,
        compiler_params=_params(("parallel",) * len(grid)),
    )(where, w, mine, theirs, m, v)


SEG_LOSS = 0
SEG_SINK = 128
SEG_AGAIN = 256
SEG_L0 = SEG_AGAIN + ATTN_W
SEG_L1 = SEG_L0 + RNN_W
SEG_RGAIN = SEG_L1 + RNN_W
SEG_G = SEG_RGAIN + 128
N_PACK = SEG_G + 4 * D_MODEL


def _pack(sinks, again, l0, l1, rgain, gains, loss=None):
    z = lambda k: jnp.zeros((1, k), F32)
    first = z(128) if loss is None else loss
    return jnp.concatenate([first, sinks, z(128 - N_Q), again, l0, l1, rgain] + list(gains), axis=1)


def _small_reduce_adamw(part, w, m, v, *, name):
    def body(p_ref, w_ref, m_ref, v_ref, g_ref, d_ref, nm_ref, nv_ref, buf_ref, send_sems, recv_sems):
        x, y, c = _place()
        me = 4 * x + 2 * y + c
        copies = []
        for k in range(1, 8):
            dx, dy, dc = (k >> 2) & 1, (k >> 1) & 1, k & 1
            to = (x ^ dx, y ^ dy, c ^ dc)
            cp = pltpu.make_async_remote_copy(
                src_ref=p_ref, dst_ref=buf_ref.at[me],
                send_sem=send_sems.at[k - 1], recv_sem=recv_sems.at[k - 1],
                device_id=to, device_id_type=MESH)
            cp.start()
            copies.append(cp)
        buf_ref[me] = p_ref[...]
        for cp in copies:
            cp.wait()
        tot = buf_ref[0]
        for j in range(1, 8):
            tot = tot + buf_ref[j]
        g_ref[...] = tot
        l0 = w_ref[:, SEG_L0:SEG_L0 + RNN_W]
        l1 = w_ref[:, SEG_L1:SEG_L1 + RNN_W]
        mx = jnp.maximum(l0, l1)
        e0 = jnp.exp(l0 - mx)
        e1 = jnp.exp(l1 - mx)
        lb = e0 / (e0 + e1)
        gl0 = tot[:, SEG_L0:SEG_L0 + RNN_W] * lb * (1.0 - lb)
        g_ref[:, SEG_L0:SEG_L0 + RNN_W] = gl0
        g_ref[:, SEG_L1:SEG_L1 + RNN_W] = -gl0
        d, nm, nv = _adamw_math(w_ref[...], g_ref[...], m_ref[...], v_ref[...])
        d_ref[...] = d
        nm_ref[...] = nm
        nv_ref[...] = nv

    vm = pl.BlockSpec(memory_space=pltpu.VMEM)
    return pl.pallas_call(
        body, name=name,
        in_specs=[vm] * 4, out_specs=[vm] * 4,
        out_shape=[jax.ShapeDtypeStruct((1, N_PACK), F32)] * 4,
        scratch_shapes=[pltpu.VMEM((8, 1, N_PACK), F32), pltpu.SemaphoreType.DMA((7,)),
                        pltpu.SemaphoreType.DMA((7,))],
    )(part, w, m, v)


def _layer_grads(xs, tgt, bufs, where, sinks, again, lb_logits, rgain,
                 g_mix_pre, g_mix_post, g_mlp_pre, g_mlp_post):
    tm = 512
    b_in, b_out, b_up, b_dn = bufs

    shard = IN_W // N_CHIPS
    h1, b_in = _rms_cast_gather(xs, g_mix_pre, b_in, tm=tm, name="h1_norm_gather_w_in")
    w_in_t = b_in.reshape(IN_W, D_MODEL)
    proj, ((b_out, b_up),) = _mm(
        h1, w_in_t, tm=1024, tn=768, tk=D_MODEL, out_dtype=F32, w_layout="nk", name="in_proj",
        exchanges=[_x_gather([b_out, b_up], ici=[(0, 256), (0, 336)])])
    attn, lse, ((b_out, b_up),) = _swa_fwd(
        proj, sinks, name="swa_fwd",
        exchanges=[_x_gather([b_out, b_up], ici=[None, (336, 320)], d2d=[(0, 256), None])])
    w_out = b_out.reshape(D_MODEL, D_MODEL)
    o_pre, rnn, s0, ((b_up, b_dn),) = _hgrn_fwd(
        proj, lb_logits, rgain, tb=512, name="hgrn_fwd",
        exchanges=[_x_gather([b_up, b_dn], ici=[(656, 368), (0, 400)])])
    cat = _mix_cat(attn, rnn, again, tm=tm, name="mix_cat")
    mixed, ((b_up, b_dn),) = _mm(
        cat, w_out, tm=1024, tn=1024, tk=D_MODEL, out_dtype=BF16, name="out_proj",
        exchanges=[_x_gather([b_up, b_dn], ici=[None, (400, 240)], d2d=[(0, 1024), (0, 400)])])
    w_up4 = b_up.reshape(N_CHIPS, D_MODEL, D_FF // N_CHIPS)
    x1, h2, ((b_dn,),) = _post_norm_res(
        mixed, g_mix_post, xs, g_mlp_pre, tm=tm, name="mix_post",
        exchanges=[_x_gather([b_dn], d2d=[(400, 240)])])
    u, ((b_dn,),) = _mm(h2, w_up4, tm=1024, tn=1024, tk=D_MODEL, out_dtype=BF16, relu=True, w_layout="skn",
                        name="mlp_up", exchanges=[_x_gather([b_dn], ici=[(640, 384)], cross=[(640, 384)])])
    w_dn = b_dn.reshape(D_FF, D_MODEL)
    yv = _mm(u, w_dn, tm=1024, tn=1024, tk=2048, out_dtype=BF16, a_square=True, name="mlp_down")
    dy, dx2, loss_row, dg_mlp_post = _loss_head(yv, g_mlp_post, x1, tgt, tm=tm, name="loss_head")

    def halved(g):
        return g.reshape(N_CHIPS, 2, g.shape[1] // 2, g.shape[2])
    du = _mm(dy, w_dn, tm=1024, tn=1024, tk=D_MODEL, out_dtype=BF16, mul2=u, w_layout="nk", name="mlp_down_bwd")
    g_dn = halved(_mm_tn(u, dy, tm=1024, tn=1024, tt=2048, a_square=True, name="w_down_grad")
                  .reshape(N_CHIPS, D_FF // N_CHIPS, D_MODEL))
    d_w_up, ((sib_dn,),) = _mm_tn(h2, du, tm=1024, tn=1024, tt=2048, n_split=N_CHIPS, name="w_up_grad",
                                  exchanges=[_x_pair([g_dn])])
    g_up = halved(d_w_up)
    wire_dn = _pair_sum(g_dn, sib_dn, where, name="pair_sum_w_down")
    dh2, ((recv_dn,), (sib_up,)) = _mm(du, w_up4, tm=1024, tn=1024, tk=2048, out_dtype=BF16, w_layout="snk", name="mlp_up_bwd",
                                       exchanges=[_x_chip([wire_dn], rows=[(0, 704)]), _x_pair([g_up])])
    wire_up = _pair_sum(g_up, sib_up, where, name="pair_sum_w_up")
    dx1, dg_mlp_pre, ((recv_dn,),) = _rms_bwd(dh2, x1, g_mlp_pre, dx2, tm=tm, out_dtype=F32, name="mlp_pre_bwd",
                                              exchanges=[_x_chip([wire_dn], rows=[(704, 224)], into=[recv_dn])])
    dmixed, dg_mix_post = _rms_bwd(dx1, mixed, g_mix_post, None, tm=tm, out_dtype=BF16, name="mix_post_bwd")
    d_w_out, ((recv_dn,),) = _mm_tn(cat, dmixed, tm=1024, tn=1024, tt=2048, name="w_out_grad",
                                    exchanges=[_x_chip([wire_dn], rows=[(928, 96)], into=[recv_dn])])
    fin_dn = _final_half(g_dn, sib_dn, recv_dn, where, name="final_half_w_down")
    g_out = halved(d_w_out.reshape(N_CHIPS, D_MODEL // N_CHIPS, D_MODEL))
    dcat, ((sib_out,), (oth_dn,)) = _mm(dmixed, w_out, tm=1024, tn=1024, tk=D_MODEL, out_dtype=BF16, w_layout="nk",
                                        name="out_proj_bwd", exchanges=[_x_pair([g_out]), _x_share([fin_dn])])
    wire_out = _pair_sum(g_out, sib_out, where, name="pair_sum_w_out")
    dattn, dg_again = _rms_bwd(dcat, attn, again, None, tm=tm, out_dtype=BF16, name="attn_norm_bwd")
    dq_a, dkv, dsinks, ((recv_up,),) = _swa_bwd(
        proj, sinks, dattn, lse, name="swa_bwd", exchanges=[_x_chip([wire_up], rows=[(0, 512)])])
    dq_r, df_r, di_r, dg_r, dlb, dgain_h, ((recv_up,), (recv_out,)) = _hgrn_bwd(
        proj, lb_logits, rgain, o_pre, s0, dcat, tb=512, name="hgrn_bwd",
        exchanges=[_x_chip([wire_up], rows=[(512, 512)], into=[recv_up]), _x_chip([wire_out])])
    fin_up = _final_half(g_up, sib_up, recv_up, where, name="final_half_w_up")
    fin_out = _final_half(g_out, sib_out, recv_out, where, name="final_half_w_out")
    dproj = jnp.concatenate([dq_a, dkv, dq_r, df_r, di_r, dg_r], axis=1)
    piece_cols = D_MODEL // 4

    def w_in_piece(pc, exchanges):
        d, xres = _mm_tn(dproj, h1, tm=896, tn=2 * piece_cols, tt=2048, b_blocks=(pc, pc + 2),
                         name="w_in_grad_%d" % pc, exchanges=exchanges)
        return d.reshape(N_CHIPS, shard, 2 * piece_cols), xres

    g_in0, ((oth_up, oth_out),) = w_in_piece(0, [_x_share([fin_up, fin_out])])
    g_in1, ((sib_in0,),) = w_in_piece(1, [_x_pair([g_in0], halves_last=True)])
    wire_in0 = _pair_sum(g_in0, sib_in0, where, name="pair_sum_w_in_0", halves_last=True)
    dh1, ((recv_in0,), (sib_in1,)) = _mm(
        dproj, w_in_t, tm=1024, tn=1024, tk=2688, out_dtype=BF16, m_blocks=(0, 2), name="in_proj_bwd_0",
        exchanges=[_x_chip([wire_in0]), _x_pair([g_in1], halves_last=True)])
    wire_in1 = _pair_sum(g_in1, sib_in1, where, name="pair_sum_w_in_1", halves_last=True)
    dh1, ((recv_in1,),) = _mm(
        dproj, w_in_t, tm=1024, tn=1024, tk=2688, out_dtype=BF16, m_blocks=(2, 2), out_into=dh1,
        name="in_proj_bwd_1", exchanges=[_x_chip([wire_in1])])
    gx, dg_mix_pre = _rms_bwd(dh1, xs, g_mix_pre, dx1, tm=tm, out_dtype=F32, name="mix_pre_bwd")
    fin_in0 = _final_half(g_in0, sib_in0, recv_in0, where, name="final_half_w_in_0", halves_last=True)
    fin_in1 = _final_half(g_in1, sib_in1, recv_in1, where, name="final_half_w_in_1", halves_last=True)
    oth_in0, oth_in1 = _run_exchange(_x_share([fin_in0, fin_in1]), name="share_w_in")
    fin_in = jnp.concatenate([fin_in0, fin_in1], axis=1)
    oth_in = jnp.concatenate([oth_in0, oth_in1], axis=1)

    big = [(fin_in, oth_in), (fin_out, oth_out), (fin_up, oth_up), (fin_dn, oth_dn)]
    drgain = jnp.sum(dgain_h, axis=0)
    small = _pack(jnp.sum(dsinks, axis=1)[None, :], dg_again, dlb, jnp.zeros_like(dlb), drgain,
                  [dg_mix_pre, dg_mix_post, dg_mlp_pre, dg_mlp_post], loss=loss_row)
    return gx, big, small


def kernel(x, w_in, attn_sinks, attn_out_gain, rnn_lb_logits, rnn_norm_gain, w_out, mix_pre_gain, mix_post_gain, mlp_pre_gain, mlp_post_gain, w_up, w_down, loss_target, m_w_in, m_attn_sinks, m_attn_out_gain, m_rnn_lb_logits, m_rnn_norm_gain, m_w_out, m_mix_pre_gain, m_mix_post_gain, m_mlp_pre_gain, m_mlp_post_gain, m_w_up, m_w_down, v_w_in, v_attn_sinks, v_attn_out_gain, v_rnn_lb_logits, v_rnn_norm_gain, v_w_out, v_mix_pre_gain, v_mix_post_gain, v_mlp_pre_gain, v_mlp_post_gain, v_w_up, v_w_down):
    ax, ay, ac = _place()
    where = jnp.stack([2 * ax + ay, ac]).astype(jnp.int32)
    t = lambda a: jnp.swapaxes(a, 1, 2)
    big_w = [t(w_in), w_out, w_up, w_down]
    big_m = [t(m_w_in), m_w_out, m_w_up, m_w_down]
    big_v = [t(v_w_in), v_w_out, v_w_up, v_w_down]

    names = ["w_in", "w_out", "w_up", "w_down"]
    bufs = [_cast_slots(w, where, name="cast_" + nm) for w, nm in zip(big_w, names)]
    gx, big_g, small_part = _layer_grads(
        x[0], loss_target[0], bufs, where, attn_sinks, attn_out_gain, rnn_lb_logits, rnn_norm_gain,
        mix_pre_gain, mix_post_gain, mlp_pre_gain, mlp_post_gain)

    grads, deltas, new_m, new_v = [], [], [], []
    for (f, o), w, m, v, nm in zip(big_g, big_w, big_m, big_v, names):
        res = _adamw(w, f, o, m, v, where, name="adamw_" + nm, halves_last=(nm == "w_in"))
        if nm == "w_in":
            res = [t(r) for r in res]
        g, d, nm_, nv_ = res
        grads.append(g)
        deltas.append(d)
        new_m.append(nm_)
        new_v.append(nv_)

    def pack_params(sinks, again, logits, rgain, gains):
        return _pack(sinks, again, logits[0:1], logits[1:2], rgain, gains)

    pw = pack_params(attn_sinks, attn_out_gain, rnn_lb_logits, rnn_norm_gain,
                     [mix_pre_gain, mix_post_gain, mlp_pre_gain, mlp_post_gain])
    pm = pack_params(m_attn_sinks, m_attn_out_gain, m_rnn_lb_logits, m_rnn_norm_gain,
                     [m_mix_pre_gain, m_mix_post_gain, m_mlp_pre_gain, m_mlp_post_gain])
    pv = pack_params(v_attn_sinks, v_attn_out_gain, v_rnn_lb_logits, v_rnn_norm_gain,
                     [v_mix_pre_gain, v_mix_post_gain, v_mlp_pre_gain, v_mlp_post_gain])
    packs = _small_reduce_adamw(small_part, pw, pm, pv, name="small_reduce_adamw")

    def unpack(p):
        seg = lambda o, k: p[:, o:o + k]
        logits = jnp.concatenate([seg(SEG_L0, RNN_W), seg(SEG_L1, RNN_W)], axis=0)
        gains = [seg(SEG_G + i * D_MODEL, D_MODEL) for i in range(4)]
        return dict(sinks=seg(SEG_SINK, N_Q), again=seg(SEG_AGAIN, ATTN_W), logits=logits,
                    rgain=seg(SEG_RGAIN, RNN_HD), gains=gains)

    def order(small, big):
        return [big[0], small["sinks"], small["again"], small["logits"], small["rgain"], big[1],
                *small["gains"], big[2], big[3]]

    loss = packs[0][0, 0]
    outs = [loss, gx[None]]
    for p, b in zip(packs, [grads, deltas, new_m, new_v]):
        outs += order(unpack(p), b)
    return tuple(outs)
```

```python
import functools

import jax
import jax.numpy as jnp
from jax import lax
from jax.experimental import pallas as pl
from jax.experimental.pallas import tpu as pltpu

F32 = jnp.float32
BF16 = jnp.bfloat16
MESH = pl.DeviceIdType.MESH

EPS = 1e-6
D_MODEL = 2048
ATTN_W = 1024
HEAD_DIM = 64
N_Q = 16
N_KV = 2
GROUP = 8
BLK = 128
RNN_W = 1024
RNN_HD = 128
N_RNN = 8
CHUNK = 64
SUB_FWD = 16
SUB_BWD = 8
D_FF = 8192
IN_W = 5376
N_CHIPS = 4
KV_COL = ATTN_W
QR_COL = ATTN_W + 2 * 128
FR_COL = QR_COL + RNN_W
IR_COL = FR_COL + RNN_W
GR_COL = IR_COL + RNN_W

ADAM_LR = 0.001
ADAM_B1 = 0.9
ADAM_B2 = 0.999
ADAM_EPS = 1e-08
ADAM_WD = 0.01
ADAM_STEP = 10

VMEM_LIMIT = 48 * 1024 * 1024

NT = (((1,), (1,)), ((), ()))
TN = (((0,), (0,)), ((), ()))


def _params(sem=None):
    return pltpu.CompilerParams(dimension_semantics=sem, vmem_limit_bytes=VMEM_LIMIT)


def _sigmoid(x):
    return 1.0 / (1.0 + jnp.exp(-x))


ANY = pl.BlockSpec(memory_space=pl.ANY)


def _place():
    return lax.axis_index("x"), lax.axis_index("y"), lax.axis_index("c")


def _other_chips(x, y):
    return [(1 - x, y), (x, 1 - y), (1 - x, 1 - y)]


class _Exchange:
    def __init__(self, srcs, outs, ncopy, build, aliases=None):
        self.srcs, self.outs, self.ncopy, self.build = list(srcs), list(outs), ncopy, build
        self.aliases = aliases or {}


def _remote(src, dst, send_sems, recv_sems, k, to):
    return pltpu.make_async_remote_copy(src_ref=src, dst_ref=dst, send_sem=send_sems.at[k],
                                        recv_sem=recv_sems.at[k], device_id=to, device_id_type=MESH)


def _call(body, *, name, grid, in_specs, out_specs, out_shape, args, scratch_shapes=(), semantics=None,
          exchanges=(), into=None):
    in_specs, out_specs, out_shape = list(in_specs), list(out_specs), list(out_shape)
    scratch_shapes = list(scratch_shapes)
    ni, no, ns = len(in_specs), len(out_specs), len(scratch_shapes)
    xsrc = [s for x in exchanges for s in x.srcs]
    xout = [o for x in exchanges for o in x.outs]
    into = into or {}
    xsrc += [into[k] for k in sorted(into)]
    nxi, nxo = len(xsrc), len(xout)
    aliases = {nxi - len(into) + ni + q: k for q, k in enumerate(sorted(into))}
    a0 = b0 = 0
    for x in exchanges:
        for si, oi in x.aliases.items():
            aliases[ni + a0 + si] = no + b0 + oi
        a0 += len(x.srcs)
        b0 += len(x.outs)
    sems = []
    for x in exchanges:
        sems += [pltpu.SemaphoreType.DMA((x.ncopy,)), pltpu.SemaphoreType.DMA((x.ncopy,))]

    def wrapped(*refs):
        ins, xi = refs[:ni], refs[ni:ni + nxi]
        outs, xo = refs[ni + nxi:ni + nxi + no], refs[ni + nxi + no:ni + nxi + no + nxo]
        rest = refs[ni + nxi + no + nxo:]
        scr, sm = rest[:ns], rest[ns:]

        def copies():
            cps = []
            a = b = 0
            for k, x in enumerate(exchanges):
                cps += x.build(xi[a:a + len(x.srcs)], xo[b:b + len(x.outs)], sm[2 * k], sm[2 * k + 1])
                a += len(x.srcs)
                b += len(x.outs)
            return cps

        def start():
            for cp in copies():
                cp.start()

        def wait():
            for cp in copies():
                cp.wait()

        if not exchanges:
            body(*ins, *outs, *scr)
        elif not grid:
            start()
            body(*ins, *outs, *scr)
            wait()
        else:
            first = last = None
            for ax, g in enumerate(grid):
                f = pl.program_id(ax) == 0
                l = pl.program_id(ax) == g - 1
                first = f if first is None else first & f
                last = l if last is None else last & l
            pl.when(first)(start)
            body(*ins, *outs, *scr)
            pl.when(last)(wait)

    if exchanges and semantics is not None:
        semantics = ("arbitrary",) * len(grid)
    kwargs = dict(grid=grid) if grid else {}
    res = pl.pallas_call(
        wrapped, name=name,
        in_specs=in_specs + [ANY] * nxi, out_specs=out_specs + [ANY] * nxo,
        out_shape=out_shape + xout, scratch_shapes=scratch_shapes + sems,
        input_output_aliases=aliases,
        compiler_params=_params(semantics), **kwargs,
    )(*args, *xsrc)
    res = list(res)
    mine, theirs = res[:no], res[no:]
    per = []
    b = 0
    for x in exchanges:
        per.append(theirs[b:b + len(x.outs)])
        b += len(x.outs)
    return mine, per


def _run_exchange(x, *, name):
    return _call(lambda: None, name=name, grid=(), in_specs=[], out_specs=[], out_shape=[], args=[],
                 exchanges=[x])[1][0]


def _x_gather(bufs, ici=None, d2d=None, cross=None):
    n = len(bufs)
    plan = [(a, kind, rows[a]) for a in range(n) for kind, rows in (("ici", ici), ("d2d", d2d), ("cross", cross))
            if rows is not None and rows[a] is not None]

    def build(srcs, outs, ss, rs):
        x, y, c = _place()
        cps = []
        for q, (a, kind, rows) in enumerate(plan):
            piece = pl.ds(*rows)
            for j, (px, py) in enumerate(_other_chips(x, y)):
                if kind == "d2d":
                    slot, to = 4 * px + 2 * py + c, (x, y, 1 - c)
                else:
                    slot, to = 4 * x + 2 * y + c, (px, py, c if kind == "ici" else 1 - c)
                cps.append(_remote(srcs[a].at[slot, piece], outs[a].at[slot, piece], ss, rs, 3 * q + j, to))
        return cps

    outs = [jax.ShapeDtypeStruct(b.shape, b.dtype) for b in bufs]
    return _Exchange(bufs, outs, 3 * len(plan), build, aliases={a: a for a in range(n)})


def _x_pair(grads, halves_last=False):
    n = len(grads)

    def build(srcs, outs, ss, rs):
        x, y, c = _place()

        def half(r):
            if not halves_last:
                return r.at[:, 1 - c]
            ch = r.shape[2] // 2
            return r.at[:, :, pl.ds(pl.multiple_of((1 - c) * ch, 128), ch)]

        return [_remote(half(srcs[a]), outs[a], ss, rs, a, (x, y, 1 - c)) for a in range(n)]

    if halves_last:
        outs = [jax.ShapeDtypeStruct(g.shape[:2] + (g.shape[2] // 2,), g.dtype) for g in grads]
    else:
        outs = [jax.ShapeDtypeStruct((4,) + g.shape[2:], g.dtype) for g in grads]
    return _Exchange(grads, outs, n, build)


def _x_chip(wires, rows=None, into=None):
    n = len(wires)
    rows = rows or [(0, w.shape[1]) for w in wires]

    def build(srcs, outs, ss, rs):
        x, y, c = _place()
        cps = []
        for a in range(n):
            piece = pl.ds(*rows[a])
            for j, (px, py) in enumerate(_other_chips(x, y)):
                cps.append(_remote(srcs[a].at[2 * px + py, piece], outs[a].at[j, piece], ss, rs,
                                   3 * a + j, (px, py, c)))
        return cps

    outs = [jax.ShapeDtypeStruct((3,) + w.shape[1:], w.dtype) for w in wires]
    if into is None:
        return _Exchange(wires, outs, 3 * n, build)
    return _Exchange(list(wires) + list(into), outs, 3 * n, build, aliases={n + a: a for a in range(n)})


def _x_share(halves):
    n = len(halves)

    def build(srcs, outs, ss, rs):
        x, y, c = _place()
        return [_remote(srcs[a], outs[a], ss, rs, a, (x, y, 1 - c)) for a in range(n)]

    outs = [jax.ShapeDtypeStruct(h.shape, h.dtype) for h in halves]
    return _Exchange(halves, outs, n, build)


def _mm(a, w, *, tm, tn, tk, out_dtype, name, a_square=False, relu=False, mul2=None, w_layout="kn",
        m_blocks=None, out_into=None, exchanges=()):
    m, k = a.shape
    m_first, m_count = m_blocks or (0, m // tm)
    a_spec = pl.BlockSpec((tm, tk), lambda i, j, kk: (i + m_first, kk))
    if w_layout == "kn":
        n = w.shape[1]
        w_spec = pl.BlockSpec((tk, tn), lambda i, j, kk: (kk, j))
    elif w_layout == "nk":
        n = w.shape[0]
        w_spec = pl.BlockSpec((tn, tk), lambda i, j, kk: (j, kk))
    elif w_layout == "skn":
        n = w.shape[0] * w.shape[2]
        per_n = w.shape[2] // tn
        w_spec = pl.BlockSpec((None, tk, tn), lambda i, j, kk: (j // per_n, kk, j % per_n))
    else:
        assert w_layout == "snk"
        n = w.shape[1]
        per_k = w.shape[2] // tk
        w_spec = pl.BlockSpec((None, tn, tk), lambda i, j, kk: (kk // per_k, j, kk % per_k))
    w_dims = NT if w_layout in ("nk", "snk") else (((1,), (0,)), ((), ()))
    nk = k // tk
    assert m % tm == 0 and n % tn == 0 and k % tk == 0

    def body(*refs):
        if mul2 is not None:
            a_ref, w_ref, e_ref, o_ref, acc_ref = refs
        else:
            a_ref, w_ref, o_ref, acc_ref = refs
            e_ref = None
        kk = pl.program_id(2)
        av = a_ref[...]
        if a_square:
            af = av.astype(F32)
            av = (af * af).astype(BF16)
        part = lax.dot_general(av, w_ref[...], w_dims, preferred_element_type=F32)

        def finish(r):
            if relu:
                r = jnp.maximum(r, 0.0)
            if e_ref is not None:
                r = 2.0 * e_ref[...].astype(F32) * r
            o_ref[...] = r.astype(out_dtype)

        if nk == 1:
            finish(part)
        else:
            @pl.when(kk == 0)
            def _():
                acc_ref[...] = part

            @pl.when(kk > 0)
            def _():
                acc_ref[...] += part

            @pl.when(kk == nk - 1)
            def _():
                finish(acc_ref[...])

    in_specs = [a_spec, w_spec]
    args = [a, w]
    if mul2 is not None:
        in_specs.append(pl.BlockSpec((tm, tn), lambda i, j, kk: (i + m_first, j)))
        args.append(mul2)
    acc_shape = (tm, tn) if nk > 1 else (8, 128)
    (out,), per = _call(
        body, name=name, grid=(m_count, n // tn, nk),
        in_specs=in_specs, out_specs=[pl.BlockSpec((tm, tn), lambda i, j, kk: (i + m_first, j))],
        out_shape=[jax.ShapeDtypeStruct((m, n), out_dtype)], args=args,
        scratch_shapes=[pltpu.VMEM(acc_shape, F32)],
        semantics=("parallel", "parallel", "arbitrary"), exchanges=exchanges,
        into=None if out_into is None else {0: out_into})
    return (out, per) if exchanges else out


def _mm_tn(a, b, *, tm, tn, tt, name, a_square=False, n_split=1, b_blocks=None, exchanges=()):
    t, m = a.shape
    nb = len(b_blocks) if b_blocks else 1
    n = tn if b_blocks else b.shape[1]
    assert t % tt == 0 and m % tm == 0 and n % tn == 0 and (n // n_split) % tn == 0
    per = n // n_split // tn

    def body(a_ref, *refs):
        b_refs, o_ref = refs[:nb], refs[nb]
        ti = pl.program_id(2)
        av = a_ref[...]
        if a_square:
            af = av.astype(F32)
            av = (af * af).astype(BF16)
        bv = b_refs[0][...] if nb == 1 else jnp.concatenate([r[...] for r in b_refs], axis=1)
        part = lax.dot_general(av, bv, TN, preferred_element_type=F32)

        @pl.when(ti == 0)
        def _():
            o_ref[...] = part

        @pl.when(ti > 0)
        def _():
            o_ref[...] += part

    if b_blocks:
        b_specs = [pl.BlockSpec((tt, tn // nb), functools.partial(lambda blk, i, j, ti: (ti, blk), blk))
                   for blk in b_blocks]
    else:
        b_specs = [pl.BlockSpec((tt, tn), lambda i, j, ti: (ti, j))]
    (out,), xres = _call(
        body, name=name, grid=(m // tm, n // tn, t // tt),
        in_specs=[pl.BlockSpec((tt, tm), lambda i, j, ti: (ti, i))] + b_specs,
        out_specs=[pl.BlockSpec((None, tm, tn), lambda i, j, ti: (j // per, i, j % per))],
        out_shape=[jax.ShapeDtypeStruct((n_split, m, n // n_split), F32)], args=[a] + [b] * nb,
        semantics=("parallel", "parallel", "arbitrary"), exchanges=exchanges)
    return (out, xres) if exchanges else out


def _rstd(x):
    return lax.rsqrt(jnp.mean(x * x, axis=-1, keepdims=True) + EPS)


def _rms_cast_gather(x, g, buf, *, tm, name):
    t, d = x.shape
    steps = t // tm

    def body(x_ref, g_ref, b_in, o_ref, b_out, send_sems, recv_sems):
        i = pl.program_id(0)
        xc, yc, c = _place()
        chips = _other_chips(xc, yc)

        def slot(px, py, pc):
            return b_out.at[4 * px + 2 * py + pc]

        def sent(j):
            return _remote(b_in.at[4 * xc + 2 * yc + c], slot(xc, yc, c), send_sems, recv_sems, j, (*chips[j], c))

        def passed(j):
            return _remote(slot(*chips[j], c), slot(*chips[j], c), send_sems, recv_sems, 3 + j, (xc, yc, 1 - c))

        @pl.when(i == 0)
        def _():
            for j in range(3):
                sent(j).start()

        xv = x_ref[...]
        o_ref[...] = (xv * _rstd(xv) * g_ref[...]).astype(BF16)

        @pl.when(i == steps - 1)
        def _():
            for j in range(3):
                sent(j).wait_recv()
                passed(j).start()
            for j in range(3):
                passed(j).wait_recv()
                passed(j).wait_send()
                sent(j).wait_send()

    return pl.pallas_call(
        body, name=name, grid=(steps,),
        in_specs=[pl.BlockSpec((tm, d), lambda i: (i, 0)), pl.BlockSpec((1, d), lambda i: (0, 0)), ANY],
        out_specs=[pl.BlockSpec((tm, d), lambda i: (i, 0)), ANY],
        out_shape=[jax.ShapeDtypeStruct((t, d), BF16), jax.ShapeDtypeStruct(buf.shape, buf.dtype)],
        scratch_shapes=[pltpu.SemaphoreType.DMA((6,)), pltpu.SemaphoreType.DMA((6,))],
        input_output_aliases={2: 1},
        compiler_params=_params(("arbitrary",)),
    )(x, g, buf)


def _mix_cat(attn, rnn, gain, *, tm, name):
    t = attn.shape[0]

    def body(a_ref, r_ref, g_ref, o_ref):
        av = a_ref[...]
        o_ref[:, :ATTN_W] = (av * _rstd(av) * g_ref[...]).astype(BF16)
        o_ref[:, ATTN_W:] = r_ref[...].astype(BF16)

    return pl.pallas_call(
        body, name=name, grid=(t // tm,),
        in_specs=[pl.BlockSpec((tm, ATTN_W), lambda i: (i, 0)), pl.BlockSpec((tm, RNN_W), lambda i: (i, 0)),
                  pl.BlockSpec((1, ATTN_W), lambda i: (0, 0))],
        out_specs=pl.BlockSpec((tm, D_MODEL), lambda i: (i, 0)),
        out_shape=jax.ShapeDtypeStruct((t, D_MODEL), BF16),
        compiler_params=_params(("parallel",)),
    )(attn, rnn, gain)


def _post_norm_res(mixed, g_post, res, g_next, *, tm, name, exchanges=()):
    t, d = mixed.shape

    def body(m_ref, gp_ref, r_ref, gn_ref, x1_ref, h2_ref):
        mv = m_ref[...].astype(F32)
        x1 = r_ref[...] + mv * _rstd(mv) * gp_ref[...]
        x1_ref[...] = x1
        h2_ref[...] = (x1 * _rstd(x1) * gn_ref[...]).astype(BF16)

    row = pl.BlockSpec((tm, d), lambda i: (i, 0))
    vec = pl.BlockSpec((1, d), lambda i: (0, 0))
    res_, xres = _call(
        body, name=name, grid=(t // tm,),
        in_specs=[row, vec, row, vec], out_specs=[row, row],
        out_shape=[jax.ShapeDtypeStruct((t, d), F32), jax.ShapeDtypeStruct((t, d), BF16)],
        args=[mixed, g_post, res, g_next], semantics=("parallel",), exchanges=exchanges)
    return (*res_, xres) if exchanges else res_


def _rms_bwd(dyn, xin, g, res, *, tm, out_dtype, name, col_block=0, exchanges=()):
    t, d = xin.shape

    def body(*refs):
        if res is not None:
            dy_ref, x_ref, g_ref, r_ref, dx_ref, dg_ref = refs
        else:
            dy_ref, x_ref, g_ref, dx_ref, dg_ref = refs
        i = pl.program_id(0)
        xv = x_ref[...].astype(F32)
        dy = dy_ref[...].astype(F32)
        r = _rstd(xv)
        xh = xv * r
        part = jnp.sum(dy * xh, axis=0, keepdims=True)

        @pl.when(i == 0)
        def _():
            dg_ref[...] = part

        @pl.when(i > 0)
        def _():
            dg_ref[...] += part

        tt = dy * g_ref[...]
        dx = r * (tt - xh * jnp.mean(tt * xh, axis=-1, keepdims=True))
        if res is not None:
            dx = dx + r_ref[...].astype(F32)
        dx_ref[...] = dx.astype(out_dtype)

    row = pl.BlockSpec((tm, d), lambda i: (i, 0))
    vec = pl.BlockSpec((1, d), lambda i: (0, 0))
    in_specs = [pl.BlockSpec((tm, d), lambda i: (i, col_block)), row, vec]
    args = [dyn, xin, g]
    if res is not None:
        in_specs.append(row)
        args.append(res)
    res, xres = _call(
        body, name=name, grid=(t // tm,),
        in_specs=in_specs, out_specs=[row, vec],
        out_shape=[jax.ShapeDtypeStruct((t, d), out_dtype), jax.ShapeDtypeStruct((1, d), F32)], args=args,
        semantics=("arbitrary",), exchanges=exchanges)
    return (*res, xres) if exchanges else res


def _loss_head(y, g_post, x1, target, *, tm, name):
    t, d = y.shape

    def body(y_ref, g_ref, x1_ref, t_ref, dy_ref, dx2_ref, loss_ref, dg_ref):
        i = pl.program_id(0)
        yv = y_ref[...].astype(F32)
        r = _rstd(yv)
        yh = yv * r
        gv = g_ref[...]
        err = x1_ref[...] + yh * gv - t_ref[...]
        lpart = 0.5 * jnp.sum(jnp.mean(err * err, axis=-1, keepdims=True), axis=0, keepdims=True)
        dx2 = err * (1.0 / d)
        dgp = jnp.sum(dx2 * yh, axis=0, keepdims=True)
        lane = lax.broadcasted_iota(jnp.int32, (1, 128), 1)
        lrow = jnp.where(lane == 0, lpart, 0.0)

        @pl.when(i == 0)
        def _():
            dg_ref[...] = dgp
            loss_ref[...] = lrow

        @pl.when(i > 0)
        def _():
            dg_ref[...] += dgp
            loss_ref[...] += lrow

        tt = dx2 * gv
        dy_ref[...] = (r * (tt - yh * jnp.mean(tt * yh, axis=-1, keepdims=True))).astype(BF16)
        dx2_ref[...] = dx2.astype(BF16)

    row = pl.BlockSpec((tm, d), lambda i: (i, 0))
    vec = pl.BlockSpec((1, d), lambda i: (0, 0))
    return pl.pallas_call(
        body, name=name, grid=(t // tm,),
        in_specs=[row, vec, row, row],
        out_specs=[row, row, pl.BlockSpec((1, 128), lambda i: (0, 0)), vec],
        out_shape=[jax.ShapeDtypeStruct((t, d), BF16), jax.ShapeDtypeStruct((t, d), BF16),
                   jax.ShapeDtypeStruct((1, 128), F32), jax.ShapeDtypeStruct((1, d), F32)],
        compiler_params=_params(("arbitrary",)),
    )(y, g_post, x1, target)


def _alibi_slope(h):
    return 2.0 ** (-8.0 * (h + 1) / N_Q)


PAIR = 2 * HEAD_DIM
N_PAIRS = N_Q // 2
PAIRS_PER_KV = GROUP // 2
SMEM = pl.BlockSpec(memory_space=pltpu.SMEM)


def _swa_mask(n):
    key = lax.broadcasted_iota(jnp.int32, (2 * BLK, BLK), 0)
    qry = lax.broadcasted_iota(jnp.int32, (2 * BLK, BLK), 1)
    dist = qry + BLK - key
    valid = (dist >= 0) & (dist < BLK) & ((key >= BLK) | (n > 0))
    return valid, dist.astype(F32)


def _block_diag(kvp_ref, kvc_ref, off):
    a = jnp.concatenate([kvp_ref[:, off:off + HEAD_DIM], kvc_ref[:, off:off + HEAD_DIM]], axis=0).astype(BF16)
    z = jnp.zeros_like(a)
    return jnp.concatenate([jnp.concatenate([a, z], axis=1), jnp.concatenate([z, a], axis=1)], axis=0)


def _swa_scores(s2, e, hh, valid, distf):
    s = s2[2 * BLK * e:2 * BLK * (e + 1)] * (HEAD_DIM ** -0.5) - _alibi_slope(hh) * distf
    return jnp.where(valid, s, -1e30)


def _swa_fwd(proj, sinks, *, name, exchanges=()):
    t = proj.shape[0]
    nb = t // BLK
    kvb = KV_COL // (2 * 128)

    def body(sink_ref, q_ref, kvc_ref, kvp_ref, o_ref, lse_ref):
        n = pl.program_id(0)
        valid, distf = _swa_mask(n)
        for kvh in range(N_KV):
            k2 = _block_diag(kvp_ref, kvc_ref, kvh * HEAD_DIM)
            v2 = _block_diag(kvp_ref, kvc_ref, 128 + kvh * HEAD_DIM)
            for jp in range(PAIRS_PER_KV):
                pair = kvh * PAIRS_PER_KV + jp
                lanes = slice(pair * PAIR, (pair + 1) * PAIR)
                s2 = lax.dot_general(k2, q_ref[:, lanes].astype(BF16), NT, preferred_element_type=F32)
                probs = []
                for e in range(2):
                    hh = 2 * pair + e
                    s = _swa_scores(s2, e, hh, valid, distf)
                    sink = sink_ref[0, hh]
                    mx = jnp.maximum(jnp.max(s, axis=0, keepdims=True), sink)
                    p = jnp.exp(s - mx)
                    l = jnp.sum(p, axis=0, keepdims=True) + jnp.exp(sink - mx)
                    probs.append((p * (1.0 / l)).astype(BF16))
                    lse_ref[hh:hh + 1, :] = mx + jnp.log(l)
                o_ref[:, lanes] = lax.dot_general(jnp.concatenate(probs, axis=0), v2, TN,
                                                  preferred_element_type=F32)

    res, xres = _call(
        body, name=name, grid=(nb,),
        in_specs=[SMEM,
                  pl.BlockSpec((BLK, ATTN_W), lambda n: (n, 0)),
                  pl.BlockSpec((BLK, 256), lambda n: (n, kvb)),
                  pl.BlockSpec((BLK, 256), lambda n: (jnp.maximum(n - 1, 0), kvb))],
        out_specs=[pl.BlockSpec((BLK, ATTN_W), lambda n: (n, 0)),
                   pl.BlockSpec((None, N_Q, BLK), lambda n: (n, 0, 0))],
        out_shape=[jax.ShapeDtypeStruct((t, ATTN_W), F32), jax.ShapeDtypeStruct((nb, N_Q, BLK), F32)],
        args=[sinks, proj, proj, proj], semantics=("parallel",), exchanges=exchanges)
    return (*res, xres) if exchanges else res


def _swa_bwd(proj, sinks, dattn, lse, *, name, exchanges=()):
    t = proj.shape[0]
    nb = t // BLK
    kvb = KV_COL // (2 * 128)

    def body(sink_ref, q_ref, kvc_ref, kvp_ref, do_ref, lse_ref, dq_ref, dkv_ref, dsink_ref, carry_ref):
        n = pl.program_id(0)

        @pl.when(n == 0)
        def _():
            dsink_ref[...] = jnp.zeros_like(dsink_ref)
            carry_ref[...] = jnp.zeros_like(carry_ref)

        @pl.when(n < nb)
        def _():
            valid, distf = _swa_mask(n)
            for kvh in range(N_KV):
                k2 = _block_diag(kvp_ref, kvc_ref, kvh * HEAD_DIM)
                v2 = _block_diag(kvp_ref, kvc_ref, 128 + kvh * HEAD_DIM)
                dk2 = jnp.zeros((4 * BLK, PAIR), F32)
                dv2 = jnp.zeros((4 * BLK, PAIR), F32)
                for jp in range(PAIRS_PER_KV):
                    pair = kvh * PAIRS_PER_KV + jp
                    lanes = slice(pair * PAIR, (pair + 1) * PAIR)
                    q2 = q_ref[:, lanes].astype(BF16)
                    do2 = do_ref[:, lanes].astype(BF16)
                    s2 = lax.dot_general(k2, q2, NT, preferred_element_type=F32)
                    dp2 = lax.dot_general(v2, do2, NT, preferred_element_type=F32)
                    probs, dss = [], []
                    for e in range(2):
                        hh = 2 * pair + e
                        lse_h = lse_ref[hh:hh + 1, :]
                        p = jnp.exp(_swa_scores(s2, e, hh, valid, distf) - lse_h)
                        dp = dp2[2 * BLK * e:2 * BLK * (e + 1)]
                        delta = jnp.sum(p * dp, axis=0, keepdims=True)
                        dsink_ref[hh:hh + 1, :] += -jnp.exp(sink_ref[0, hh] - lse_h) * delta
                        probs.append(p.astype(BF16))
                        dss.append((p * (dp - delta)).astype(BF16))
                    ds2 = jnp.concatenate(dss, axis=0)
                    dq_ref[:, lanes] = (lax.dot_general(ds2, k2, TN, preferred_element_type=F32)
                                        * (HEAD_DIM ** -0.5)).astype(BF16)
                    dk2 = dk2 + jnp.dot(ds2, q2, preferred_element_type=F32)
                    dv2 = dv2 + jnp.dot(jnp.concatenate(probs, axis=0), do2, preferred_element_type=F32)
                dk_cat = (dk2[:2 * BLK, :HEAD_DIM] + dk2[2 * BLK:, HEAD_DIM:]) * (HEAD_DIM ** -0.5)
                dv_cat = dv2[:2 * BLK, :HEAD_DIM] + dv2[2 * BLK:, HEAD_DIM:]
                ko = kvh * HEAD_DIM
                vo = 128 + kvh * HEAD_DIM
                dkv_ref[:, ko:ko + HEAD_DIM] = (carry_ref[:, ko:ko + HEAD_DIM] + dk_cat[:BLK]).astype(BF16)
                dkv_ref[:, vo:vo + HEAD_DIM] = (carry_ref[:, vo:vo + HEAD_DIM] + dv_cat[:BLK]).astype(BF16)
                carry_ref[:, ko:ko + HEAD_DIM] = dk_cat[BLK:]
                carry_ref[:, vo:vo + HEAD_DIM] = dv_cat[BLK:]

        @pl.when(n == nb)
        def _():
            dkv_ref[...] = carry_ref[...].astype(BF16)

    last = nb - 1
    res, xres = _call(
        body, name=name, grid=(nb + 1,),
        in_specs=[SMEM,
                  pl.BlockSpec((BLK, ATTN_W), lambda n: (jnp.minimum(n, last), 0)),
                  pl.BlockSpec((BLK, 256), lambda n: (jnp.minimum(n, last), kvb)),
                  pl.BlockSpec((BLK, 256), lambda n: (jnp.maximum(jnp.minimum(n, last) - 1, 0), kvb)),
                  pl.BlockSpec((BLK, ATTN_W), lambda n: (jnp.minimum(n, last), 0)),
                  pl.BlockSpec((None, N_Q, BLK), lambda n: (jnp.minimum(n, last), 0, 0))],
        out_specs=[pl.BlockSpec((BLK, ATTN_W), lambda n: (jnp.minimum(n, last), 0)),
                   pl.BlockSpec((BLK, 256), lambda n: (jnp.maximum(n - 1, 0), 0)),
                   pl.BlockSpec((N_Q, BLK), lambda n: (0, 0))],
        out_shape=[jax.ShapeDtypeStruct((t, ATTN_W), BF16), jax.ShapeDtypeStruct((t, 256), BF16),
                   jax.ShapeDtypeStruct((N_Q, BLK), F32)],
        scratch_shapes=[pltpu.VMEM((BLK, 256), F32)],
        args=[sinks, proj, proj, proj, dattn, lse], semantics=("arbitrary",), exchanges=exchanges)
    return (*res, xres) if exchanges else res


def _cumsum_rows(x):
    n = x.shape[0]
    row = lax.broadcasted_iota(jnp.int32, x.shape, 0)
    s = 1
    while s < n:
        x = x + jnp.where(row >= s, pltpu.roll(x, s, axis=0), 0.0)
        s *= 2
    return x


def _rev_cumsum_rows(x):
    n = x.shape[0]
    row = lax.broadcasted_iota(jnp.int32, x.shape, 0)
    s = 1
    while s < n:
        x = x + jnp.where(row < n - s, pltpu.roll(x, n - s, axis=0), 0.0)
        s *= 2
    return x


def _lower_bound(lbl_ref):
    l0 = lbl_ref[0:1, :]
    l1 = lbl_ref[1:2, :]
    mx = jnp.maximum(l0, l1)
    e0 = jnp.exp(l0 - mx)
    e1 = jnp.exp(l1 - mx)
    return e0 / (e0 + e1)


def _hgrn_gates(z, lb):
    sg = _sigmoid(z)
    f = lb + (1.0 - lb) * sg
    return sg, f, jnp.log(f), 1.0 - f


def _sub_factors(b, k, i, sub, trim):
    need = -(-sub * i // 16) * 16 if trim else CHUNK
    rows = lax.broadcasted_iota(jnp.int32, (need, RNN_HD), 0)
    ref = b[sub * i - 1:sub * i, :]
    qfac = jnp.exp(b[sub * i:sub * (i + 1), :] - ref)
    kfac = jnp.where(rows < sub * i, jnp.exp(ref - b[:need]), 0.0)
    kt = (k[:need] * kfac).astype(BF16)
    if need < CHUNK:
        kt = jnp.concatenate([kt, jnp.zeros((CHUNK - need, RNN_HD), BF16)], axis=0)
    return qfac, kfac, kt


def _diag_decay(bi, s):
    trow = lax.broadcasted_iota(jnp.int32, bi.shape, 0)
    return jnp.where(trow >= s, jnp.exp(bi - bi[s:s + 1, :]), 0.0)


def _hgrn_fwd(proj, lb_logits, norm_gain, *, tb, name, exchanges=()):
    t = proj.shape[0]
    ntb = t // tb
    nch = tb // CHUNK
    qb, fb, ib, gb = QR_COL // 128, FR_COL // 128, IR_COL // 128, GR_COL // 128

    def body(q_ref, f_ref, i_ref, g_ref, lbl_ref, gain_ref, o_ref, out_ref, s0_ref, st_ref):
        c = pl.program_id(1)

        @pl.when(c == 0)
        def _():
            st_ref[...] = jnp.zeros_like(st_ref)

        lb = _lower_bound(lbl_ref)
        gain = gain_ref[...]

        def chunk(ci, st):
            rows = slice(ci * CHUNK, (ci + 1) * CHUNK)
            _, _, lf, k = _hgrn_gates(f_ref[rows, :], lb)
            qr = q_ref[rows, :]
            q = qr * _sigmoid(qr)
            v = i_ref[rows, :]
            b = _cumsum_rows(lf)
            s0_ref[ci] = st
            o_inter = lax.dot_general((q * jnp.exp(b)).astype(BF16), st.astype(BF16), NT,
                                      preferred_element_type=F32)
            vb = v.astype(BF16)
            blast = b[CHUNK - 1:CHUNK, :]
            khat = (k * jnp.exp(blast - b)).astype(BF16)
            st = st * jnp.exp(blast) + lax.dot_general(vb, khat, TN, preferred_element_type=F32)
            blocks = []
            for i in range(CHUNK // SUB_FWD):
                blk = slice(SUB_FWD * i, SUB_FWD * (i + 1))
                qi, ki, vi, bi = q[blk], k[blk], v[blk], b[blk]
                oi = o_inter[blk]
                if i > 0:
                    qfac, _, kt = _sub_factors(b, k, i, SUB_FWD, trim=True)
                    att = lax.dot_general((qi * qfac).astype(BF16), kt, NT,
                                          preferred_element_type=F32)
                    oi = oi + jnp.dot(att.astype(BF16), vb, preferred_element_type=F32)
                for s in range(SUB_FWD):
                    qe = qi * _diag_decay(bi, s)
                    a = jnp.sum(qe * ki[s:s + 1, :], axis=1, keepdims=True)
                    oi = oi + a * vi[s:s + 1, :]
                blocks.append(oi)
            o = jnp.concatenate(blocks, axis=0)
            o_ref[rows, :] = o
            gr = g_ref[rows, :]
            out_ref[rows, :] = o * _rstd(o) * gain * (gr * _sigmoid(gr))
            return st

        st = st_ref[...]
        for ci in range(nch):
            st = chunk(ci, st)
        st_ref[...] = st

    def col(base):
        return pl.BlockSpec((tb, RNN_HD), lambda h, c: (c, base + h))

    res, xres = _call(
        body, name=name, grid=(N_RNN, ntb),
        in_specs=[col(qb), col(fb), col(ib), col(gb),
                  pl.BlockSpec((2, RNN_HD), lambda h, c: (0, h)), pl.BlockSpec((1, RNN_HD), lambda h, c: (0, 0))],
        out_specs=[pl.BlockSpec((tb, RNN_HD), lambda h, c: (c, h)), pl.BlockSpec((tb, RNN_HD), lambda h, c: (c, h)),
                   pl.BlockSpec((None, nch, RNN_HD, RNN_HD), lambda h, c: (h, c, 0, 0))],
        out_shape=[jax.ShapeDtypeStruct((t, RNN_W), F32), jax.ShapeDtypeStruct((t, RNN_W), F32),
                   jax.ShapeDtypeStruct((N_RNN, t // CHUNK, RNN_HD, RNN_HD), F32)],
        scratch_shapes=[pltpu.VMEM((RNN_HD, RNN_HD), F32)],
        args=[proj, proj, proj, proj, lb_logits, norm_gain],
        semantics=("parallel", "arbitrary"), exchanges=exchanges)
    return (*res, xres) if exchanges else res


def _hgrn_bwd(proj, lb_logits, norm_gain, o_pre, s0, dcat, *, tb, name, exchanges=()):
    t = proj.shape[0]
    ntb = t // tb
    nch = tb // CHUNK
    qb, fb, ib, gb = QR_COL // 128, FR_COL // 128, IR_COL // 128, GR_COL // 128
    sub = SUB_BWD
    nsub = CHUNK // sub

    def body(q_ref, f_ref, i_ref, g_ref, lbl_ref, gain_ref, o_ref, s0_ref, dout_ref,
             dq_ref, df_ref, di_ref, dg_ref, dlb_ref, dgain_ref,
             dst_ref, dqs_ref, dks_ref, dvs_ref):
        c = pl.program_id(1)

        @pl.when(c == 0)
        def _():
            dst_ref[...] = jnp.zeros_like(dst_ref)
            dlb_ref[...] = jnp.zeros_like(dlb_ref)
            dgain_ref[...] = jnp.zeros_like(dgain_ref)

        lb = _lower_bound(lbl_ref)
        gain = gain_ref[...]

        def chunk(ci, dst):
            rows = slice(ci * CHUNK, (ci + 1) * CHUNK)
            dqa_ref, dka_ref, dva_ref = dqs_ref.at[ci], dks_ref.at[ci], dvs_ref.at[ci]
            sg, f, lf, k = _hgrn_gates(f_ref[rows, :], lb)
            qr = q_ref[rows, :]
            sq = _sigmoid(qr)
            q = qr * sq
            v = i_ref[rows, :]
            b = _cumsum_rows(lf)

            dout = dout_ref[rows, :].astype(F32)
            o = o_ref[rows, :]
            gr = g_ref[rows, :]
            sgg = _sigmoid(gr)
            gate = gr * sgg
            rs = _rstd(o)
            nrm = o * rs
            dg_ref[rows, :] = (dout * nrm * gain * (sgg * (1.0 + gr * (1.0 - sgg)))).astype(BF16)
            dn = dout * gate
            dgain_ref[...] += jnp.sum(dn * nrm, axis=0, keepdims=True)
            tt = dn * gain
            do = rs * (tt - nrm * jnp.mean(tt * nrm, axis=-1, keepdims=True))

            dob = do.astype(BF16)
            vb = v.astype(BF16)
            eb = jnp.exp(b)
            blast = b[CHUNK - 1:CHUNK, :]
            ebl = jnp.exp(blast - b)
            dstb = dst.astype(BF16)
            khat = (k * ebl).astype(BF16)
            s0 = s0_ref[ci]
            dqa_ref[...] = eb * jnp.dot(dob, s0.astype(BF16), preferred_element_type=F32)
            dk_state = ebl * jnp.dot(vb, dstb, preferred_element_type=F32)
            dka_ref[...] = dk_state
            d_blast = (jnp.sum(k * dk_state, axis=0, keepdims=True)
                       + jnp.exp(blast) * jnp.sum(dst * s0, axis=0, keepdims=True))
            dva_ref[...] = lax.dot_general(khat, dstb, NT, preferred_element_type=F32)
            dst_next = dst * jnp.exp(blast) + lax.dot_general(dob, (q * eb).astype(BF16), TN,
                                                              preferred_element_type=F32)
            pm = lax.dot_general(dob, vb, NT, preferred_element_type=F32)
            for i in range(nsub):
                blk = slice(sub * i, sub * (i + 1))
                qi, ki, vi, bi, doi = q[blk], k[blk], v[blk], b[blk], do[blk]
                dqi = dqa_ref[blk, :]
                if i > 0:
                    qfac, kfac, kt = _sub_factors(b, k, i, sub, trim=False)
                    qt = (qi * qfac).astype(BF16)
                    att = lax.dot_general(qt, kt, NT, preferred_element_type=F32).astype(BF16)
                    pmi = pm[blk, :].astype(BF16)
                    dva_ref[...] += lax.dot_general(att, doi.astype(BF16), TN, preferred_element_type=F32)
                    dqi = dqi + qfac * jnp.dot(pmi, kt, preferred_element_type=F32)
                    dka_ref[...] += kfac * lax.dot_general(pmi, qt, TN, preferred_element_type=F32)
                dqa_ref[blk, :] = dqi
                srow = lax.broadcasted_iota(jnp.int32, (sub, RNN_HD), 0)
                dki = jnp.zeros((sub, RNN_HD), F32)
                dvi = jnp.zeros((sub, RNN_HD), F32)
                for tq in range(sub):
                    qt, dot_ = qi[tq:tq + 1, :], doi[tq:tq + 1, :]
                    e = jnp.where(srow <= tq, jnp.exp(bi[tq:tq + 1, :] - bi), 0.0)
                    ke = ki * e
                    p = jnp.sum(vi * dot_, axis=1, keepdims=True)
                    a = jnp.sum(ke * qt, axis=1, keepdims=True)
                    dki = dki + p * (qt * e)
                    dvi = dvi + a * dot_
                    row = slice(sub * i + tq, sub * i + tq + 1)
                    dqa_ref[row, :] += jnp.sum(p * ke, axis=0, keepdims=True)
                dka_ref[blk, :] += dki
                dva_ref[blk, :] += dvi

            dq = dqa_ref[...]
            dk = dka_ref[...]
            lastrow = lax.broadcasted_iota(jnp.int32, (CHUNK, RNN_HD), 0) == CHUNK - 1
            dlf = _rev_cumsum_rows(q * dq - k * dk + jnp.where(lastrow, d_blast, 0.0))
            dff = dlf / f - dk
            df_ref[rows, :] = (dff * (1.0 - lb) * sg * (1.0 - sg)).astype(BF16)
            dlb_ref[...] += jnp.sum(dff * (1.0 - sg), axis=0, keepdims=True)
            dq_ref[rows, :] = (dq * (sq * (1.0 + qr * (1.0 - sq)))).astype(BF16)
            di_ref[rows, :] = dva_ref[...].astype(BF16)
            return dst_next

        dst = dst_ref[...]
        for ci in reversed(range(nch)):
            dst = chunk(ci, dst)
        dst_ref[...] = dst

    def col(base):
        return pl.BlockSpec((tb, RNN_HD), lambda h, c: (ntb - 1 - c, base + h))

    outc = pl.BlockSpec((tb, RNN_HD), lambda h, c: (ntb - 1 - c, h))
    hb = ATTN_W // RNN_HD
    res, xres = _call(
        body, name=name, grid=(N_RNN, ntb),
        in_specs=[col(qb), col(fb), col(ib), col(gb),
                  pl.BlockSpec((2, RNN_HD), lambda h, c: (0, h)), pl.BlockSpec((1, RNN_HD), lambda h, c: (0, 0)),
                  outc,
                  pl.BlockSpec((None, nch, RNN_HD, RNN_HD), lambda h, c: (h, ntb - 1 - c, 0, 0)),
                  pl.BlockSpec((tb, RNN_HD), lambda h, c: (ntb - 1 - c, hb + h))],
        out_specs=[outc, outc, outc, outc,
                   pl.BlockSpec((1, RNN_HD), lambda h, c: (0, h)),
                   pl.BlockSpec((None, 1, RNN_HD), lambda h, c: (h, 0, 0))],
        out_shape=[jax.ShapeDtypeStruct((t, RNN_W), BF16)] * 4
        + [jax.ShapeDtypeStruct((1, RNN_W), F32), jax.ShapeDtypeStruct((N_RNN, 1, RNN_HD), F32)],
        scratch_shapes=[pltpu.VMEM((RNN_HD, RNN_HD), F32),
                        pltpu.VMEM((nch, CHUNK, RNN_HD), F32), pltpu.VMEM((nch, CHUNK, RNN_HD), F32),
                        pltpu.VMEM((nch, CHUNK, RNN_HD), F32)],
        args=[proj, proj, proj, proj, lb_logits, norm_gain, o_pre, s0, dcat],
        semantics=("parallel", "arbitrary"), exchanges=exchanges)
    return (*res, xres) if exchanges else res


def _cast_slots(w, where, *, name):
    _, rows, cols = w.shape
    rh = rows // 2
    tr = _row_tile(rh, cols)
    nh = rh // tr

    def body(wh_ref, w_ref, o_ref):
        o_ref[...] = w_ref[...].astype(BF16)

    return pl.pallas_call(
        body, name=name,
        grid_spec=pltpu.PrefetchScalarGridSpec(
            num_scalar_prefetch=1, grid=(2, nh),
            in_specs=[pl.BlockSpec((None, tr, cols), lambda h, i, wh: (0, h * nh + i, 0))],
            out_specs=pl.BlockSpec((None, tr, cols), lambda h, i, wh: (2 * wh[0] + h, i, 0))),
        out_shape=jax.ShapeDtypeStruct((8, rh, cols), BF16),
        compiler_params=_params(("parallel", "parallel")),
    )(where, w)


def _row_tile(rows, cols, budget=1 << 20):
    tr = rows
    while tr * cols > budget and tr % 16 == 0:
        tr //= 2
    return tr


def _half_spec(g, tr, halves_last, slab):
    if halves_last:
        return pl.BlockSpec((None, tr, g.shape[2] // 2), lambda *a: (slab(*a), a[-2], a[-1][1]))
    return pl.BlockSpec((None, None, tr, g.shape[3]), lambda *a: (slab(*a), a[-1][1], a[-2], 0))


def _pair_sum(g, sib, where, *, name, halves_last=False):
    rh, cols = sib.shape[1:]
    tr = _row_tile(rh, cols)

    def body(w_ref, g_ref, s_ref, o_ref):
        o_ref[...] = (g_ref[...] + s_ref[...]).astype(BF16)

    def foreign(s, i, w):
        return (w[0] + 1 + s) % N_CHIPS

    return pl.pallas_call(
        body, name=name,
        grid_spec=pltpu.PrefetchScalarGridSpec(
            num_scalar_prefetch=1, grid=(N_CHIPS - 1, rh // tr),
            in_specs=[_half_spec(g, tr, halves_last, foreign),
                      pl.BlockSpec((None, tr, cols), lambda s, i, w: (foreign(s, i, w), i, 0))],
            out_specs=pl.BlockSpec((None, tr, cols), lambda s, i, w: (foreign(s, i, w), i, 0))),
        out_shape=jax.ShapeDtypeStruct((4, rh, cols), BF16),
        compiler_params=_params(("parallel", "parallel")),
    )(where, g, sib)


def _final_half(g, sib, recv, where, *, name, halves_last=False):
    rh, cols = sib.shape[1:]
    tr = _row_tile(rh, cols)

    def body(w_ref, g_ref, s_ref, r_ref, o_ref):
        acc = g_ref[...] + s_ref[...]
        for j in range(3):
            acc = acc + r_ref[j].astype(F32)
        o_ref[...] = acc

    return pl.pallas_call(
        body, name=name,
        grid_spec=pltpu.PrefetchScalarGridSpec(
            num_scalar_prefetch=1, grid=(rh // tr,),
            in_specs=[_half_spec(g, tr, halves_last, lambda i, w: w[0]),
                      pl.BlockSpec((None, tr, cols), lambda i, w: (w[0], i, 0)),
                      pl.BlockSpec((3, tr, cols), lambda i, w: (0, i, 0))],
            out_specs=pl.BlockSpec((tr, cols), lambda i, w: (i, 0))),
        out_shape=jax.ShapeDtypeStruct((rh, cols), F32),
        compiler_params=_params(("parallel",)),
    )(where, g, sib, recv)


def _adamw_math(w, g, m, v):
    m = ADAM_B1 * m + (1.0 - ADAM_B1) * g
    v = ADAM_B2 * v + (1.0 - ADAM_B2) * (g * g)
    m_hat = m / (1.0 - ADAM_B1 ** ADAM_STEP)
    v_hat = v / (1.0 - ADAM_B2 ** ADAM_STEP)
    delta = -ADAM_LR * (m_hat / (jnp.sqrt(v_hat) + ADAM_EPS) + ADAM_WD * w)
    return delta, m, v


def _adamw(w, mine, theirs, m, v, where, *, name, halves_last=False):
    _, rows, cols = w.shape
    if halves_last:
        cols //= 2
        tr = _row_tile(rows, cols, budget=1 << 19)
        grid = (rows // tr, 2)
        blk = pl.BlockSpec((None, tr, cols), lambda i, h, wh: (0, i, h))
        mine_spec = theirs_spec = pl.BlockSpec((tr, cols), lambda i, h, wh: (i, 0))
        which = lambda: pl.program_id(1)
    else:
        tr = _row_tile(rows // 2, cols, budget=1 << 19)
        nh = rows // 2 // tr
        grid = (rows // tr,)
        blk = pl.BlockSpec((None, tr, cols), lambda i, wh: (0, i, 0))
        mine_spec = pl.BlockSpec((tr, cols), lambda i, wh: (jnp.where(i // nh == wh[1], i % nh, 0), 0))
        theirs_spec = pl.BlockSpec((tr, cols), lambda i, wh: (jnp.where(i // nh == wh[1], 0, i % nh), 0))
        which = lambda: pl.program_id(0) // nh

    def body(wh_ref, w_ref, a_ref, b_ref, m_ref, v_ref, g_ref, d_ref, nm_ref, nv_ref):
        g = jnp.where(which() == wh_ref[1], a_ref[...], b_ref[...])
        d, nm, nv = _adamw_math(w_ref[...], g, m_ref[...], v_ref[...])
        g_ref[...] = g
        d_ref[...] = d
        nm_ref[...] = nm
        nv_ref[...] = nv

    rows, cols = w.shape[1:]
    return pl.pallas_call(
        body, name=name,
        grid_spec=pltpu.PrefetchScalarGridSpec(
            num_scalar_prefetch=1, grid=grid,
            in_specs=[blk, mine_spec, theirs_spec, blk, blk], out_specs=[blk] * 4),
        out_shape=[jax.ShapeDtypeStruct((1, rows, cols), F32)] * 4,
        compiler_params=_params(("parallel",) * len(grid)),
    )(where, w, mine, theirs, m, v)


SEG_LOSS = 0
SEG_SINK = 128
SEG_AGAIN = 256
SEG_L0 = SEG_AGAIN + ATTN_W
SEG_L1 = SEG_L0 + RNN_W
SEG_RGAIN = SEG_L1 + RNN_W
SEG_G = SEG_RGAIN + 128
N_PACK = SEG_G + 4 * D_MODEL


def _pack(sinks, again, l0, l1, rgain, gains, loss=None):
    z = lambda k: jnp.zeros((1, k), F32)
    first = z(128) if loss is None else loss
    return jnp.concatenate([first, sinks, z(128 - N_Q), again, l0, l1, rgain] + list(gains), axis=1)


def _small_reduce_adamw(part, w, m, v, *, name):
    def body(p_ref, w_ref, m_ref, v_ref, g_ref, d_ref, nm_ref, nv_ref, buf_ref, send_sems, recv_sems):
        x, y, c = _place()
        me = 4 * x + 2 * y + c
        copies = []
        for k in range(1, 8):
            dx, dy, dc = (k >> 2) & 1, (k >> 1) & 1, k & 1
            to = (x ^ dx, y ^ dy, c ^ dc)
            cp = pltpu.make_async_remote_copy(
                src_ref=p_ref, dst_ref=buf_ref.at[me],
                send_sem=send_sems.at[k - 1], recv_sem=recv_sems.at[k - 1],
                device_id=to, device_id_type=MESH)
            cp.start()
            copies.append(cp)
        buf_ref[me] = p_ref[...]
        for cp in copies:
            cp.wait()
        tot = buf_ref[0]
        for j in range(1, 8):
            tot = tot + buf_ref[j]
        g_ref[...] = tot
        l0 = w_ref[:, SEG_L0:SEG_L0 + RNN_W]
        l1 = w_ref[:, SEG_L1:SEG_L1 + RNN_W]
        mx = jnp.maximum(l0, l1)
        e0 = jnp.exp(l0 - mx)
        e1 = jnp.exp(l1 - mx)
        lb = e0 / (e0 + e1)
        gl0 = tot[:, SEG_L0:SEG_L0 + RNN_W] * lb * (1.0 - lb)
        g_ref[:, SEG_L0:SEG_L0 + RNN_W] = gl0
        g_ref[:, SEG_L1:SEG_L1 + RNN_W] = -gl0
        d, nm, nv = _adamw_math(w_ref[...], g_ref[...], m_ref[...], v_ref[...])
        d_ref[...] = d
        nm_ref[...] = nm
        nv_ref[...] = nv

    vm = pl.BlockSpec(memory_space=pltpu.VMEM)
    return pl.pallas_call(
        body, name=name,
        in_specs=[vm] * 4, out_specs=[vm] * 4,
        out_shape=[jax.ShapeDtypeStruct((1, N_PACK), F32)] * 4,
        scratch_shapes=[pltpu.VMEM((8, 1, N_PACK), F32), pltpu.SemaphoreType.DMA((7,)),
                        pltpu.SemaphoreType.DMA((7,))],
    )(part, w, m, v)


def _layer_grads(xs, tgt, bufs, where, sinks, again, lb_logits, rgain,
                 g_mix_pre, g_mix_post, g_mlp_pre, g_mlp_post):
    tm = 512
    b_in, b_out, b_up, b_dn = bufs

    shard = IN_W // N_CHIPS
    h1, b_in = _rms_cast_gather(xs, g_mix_pre, b_in, tm=tm, name="h1_norm_gather_w_in")
    w_in_t = b_in.reshape(IN_W, D_MODEL)
    proj, ((b_out, b_up),) = _mm(
        h1, w_in_t, tm=1024, tn=768, tk=D_MODEL, out_dtype=F32, w_layout="nk", name="in_proj",
        exchanges=[_x_gather([b_out, b_up], ici=[(0, 256), (0, 336)])])
    attn, lse, ((b_out, b_up),) = _swa_fwd(
        proj, sinks, name="swa_fwd",
        exchanges=[_x_gather([b_out, b_up], ici=[None, (336, 320)], d2d=[(0, 256), None])])
    w_out = b_out.reshape(D_MODEL, D_MODEL)
    o_pre, rnn, s0, ((b_up, b_dn),) = _hgrn_fwd(
        proj, lb_logits, rgain, tb=512, name="hgrn_fwd",
        exchanges=[_x_gather([b_up, b_dn], ici=[(656, 368), (0, 400)])])
    cat = _mix_cat(attn, rnn, again, tm=tm, name="mix_cat")
    mixed, ((b_up, b_dn),) = _mm(
        cat, w_out, tm=1024, tn=1024, tk=D_MODEL, out_dtype=BF16, name="out_proj",
        exchanges=[_x_gather([b_up, b_dn], ici=[None, (400, 240)], d2d=[(0, 1024), (0, 400)])])
    w_up4 = b_up.reshape(N_CHIPS, D_MODEL, D_FF // N_CHIPS)
    x1, h2, ((b_dn,),) = _post_norm_res(
        mixed, g_mix_post, xs, g_mlp_pre, tm=tm, name="mix_post",
        exchanges=[_x_gather([b_dn], d2d=[(400, 240)])])
    u, ((b_dn,),) = _mm(h2, w_up4, tm=1024, tn=1024, tk=D_MODEL, out_dtype=BF16, relu=True, w_layout="skn",
                        name="mlp_up", exchanges=[_x_gather([b_dn], ici=[(640, 384)], cross=[(640, 384)])])
    w_dn = b_dn.reshape(D_FF, D_MODEL)
    yv = _mm(u, w_dn, tm=1024, tn=1024, tk=2048, out_dtype=BF16, a_square=True, name="mlp_down")
    dy, dx2, loss_row, dg_mlp_post = _loss_head(yv, g_mlp_post, x1, tgt, tm=tm, name="loss_head")

    def halved(g):
        return g.reshape(N_CHIPS, 2, g.shape[1] // 2, g.shape[2])
    du = _mm(dy, w_dn, tm=1024, tn=1024, tk=D_MODEL, out_dtype=BF16, mul2=u, w_layout="nk", name="mlp_down_bwd")
    g_dn = halved(_mm_tn(u, dy, tm=1024, tn=1024, tt=2048, a_square=True, name="w_down_grad")
                  .reshape(N_CHIPS, D_FF // N_CHIPS, D_MODEL))
    d_w_up, ((sib_dn,),) = _mm_tn(h2, du, tm=1024, tn=1024, tt=2048, n_split=N_CHIPS, name="w_up_grad",
                                  exchanges=[_x_pair([g_dn])])
    g_up = halved(d_w_up)
    wire_dn = _pair_sum(g_dn, sib_dn, where, name="pair_sum_w_down")
    dh2, ((recv_dn,), (sib_up,)) = _mm(du, w_up4, tm=1024, tn=1024, tk=2048, out_dtype=BF16, w_layout="snk", name="mlp_up_bwd",
                                       exchanges=[_x_chip([wire_dn], rows=[(0, 704)]), _x_pair([g_up])])
    wire_up = _pair_sum(g_up, sib_up, where, name="pair_sum_w_up")
    dx1, dg_mlp_pre, ((recv_dn,),) = _rms_bwd(dh2, x1, g_mlp_pre, dx2, tm=tm, out_dtype=BF16, name="mlp_pre_bwd",
                                              exchanges=[_x_chip([wire_dn], rows=[(704, 224)], into=[recv_dn])])
    dmixed, dg_mix_post = _rms_bwd(dx1, mixed, g_mix_post, None, tm=tm, out_dtype=BF16, name="mix_post_bwd")
    d_w_out, ((recv_dn,),) = _mm_tn(cat, dmixed, tm=1024, tn=1024, tt=2048, name="w_out_grad",
                                    exchanges=[_x_chip([wire_dn], rows=[(928, 96)], into=[recv_dn])])
    fin_dn = _final_half(g_dn, sib_dn, recv_dn, where, name="final_half_w_down")
    g_out = halved(d_w_out.reshape(N_CHIPS, D_MODEL // N_CHIPS, D_MODEL))
    dcat, ((sib_out,), (oth_dn,)) = _mm(dmixed, w_out, tm=1024, tn=1024, tk=D_MODEL, out_dtype=BF16, w_layout="nk",
                                        name="out_proj_bwd", exchanges=[_x_pair([g_out]), _x_share([fin_dn])])
    wire_out = _pair_sum(g_out, sib_out, where, name="pair_sum_w_out")
    dattn, dg_again = _rms_bwd(dcat, attn, again, None, tm=tm, out_dtype=BF16, name="attn_norm_bwd")
    dq_a, dkv, dsinks, ((recv_up,),) = _swa_bwd(
        proj, sinks, dattn, lse, name="swa_bwd", exchanges=[_x_chip([wire_up], rows=[(0, 512)])])
    dq_r, df_r, di_r, dg_r, dlb, dgain_h, ((recv_up,), (recv_out,)) = _hgrn_bwd(
        proj, lb_logits, rgain, o_pre, s0, dcat, tb=512, name="hgrn_bwd",
        exchanges=[_x_chip([wire_up], rows=[(512, 512)], into=[recv_up]), _x_chip([wire_out])])
    fin_up = _final_half(g_up, sib_up, recv_up, where, name="final_half_w_up")
    fin_out = _final_half(g_out, sib_out, recv_out, where, name="final_half_w_out")
    dproj = jnp.concatenate([dq_a, dkv, dq_r, df_r, di_r, dg_r], axis=1)
    piece_cols = D_MODEL // 4

    def w_in_piece(pc, exchanges):
        d, xres = _mm_tn(dproj, h1, tm=896, tn=2 * piece_cols, tt=2048, b_blocks=(pc, pc + 2),
                         name="w_in_grad_%d" % pc, exchanges=exchanges)
        return d.reshape(N_CHIPS, shard, 2 * piece_cols), xres

    g_in0, ((oth_up, oth_out),) = w_in_piece(0, [_x_share([fin_up, fin_out])])
    g_in1, ((sib_in0,),) = w_in_piece(1, [_x_pair([g_in0], halves_last=True)])
    wire_in0 = _pair_sum(g_in0, sib_in0, where, name="pair_sum_w_in_0", halves_last=True)
    dh1, ((recv_in0,), (sib_in1,)) = _mm(
        dproj, w_in_t, tm=1024, tn=1024, tk=2688, out_dtype=BF16, m_blocks=(0, 2), name="in_proj_bwd_0",
        exchanges=[_x_chip([wire_in0]), _x_pair([g_in1], halves_last=True)])
    wire_in1 = _pair_sum(g_in1, sib_in1, where, name="pair_sum_w_in_1", halves_last=True)
    dh1, ((recv_in1,),) = _mm(
        dproj, w_in_t, tm=1024, tn=1024, tk=2688, out_dtype=BF16, m_blocks=(2, 2), out_into=dh1,
        name="in_proj_bwd_1", exchanges=[_x_chip([wire_in1])])
    gx, dg_mix_pre = _rms_bwd(dh1, xs, g_mix_pre, dx1, tm=tm, out_dtype=F32, name="mix_pre_bwd")
    fin_in0 = _final_half(g_in0, sib_in0, recv_in0, where, name="final_half_w_in_0", halves_last=True)
    fin_in1 = _final_half(g_in1, sib_in1, recv_in1, where, name="final_half_w_in_1", halves_last=True)
    oth_in0, oth_in1 = _run_exchange(_x_share([fin_in0, fin_in1]), name="share_w_in")
    fin_in = jnp.concatenate([fin_in0, fin_in1], axis=1)
    oth_in = jnp.concatenate([oth_in0, oth_in1], axis=1)

    big = [(fin_in, oth_in), (fin_out, oth_out), (fin_up, oth_up), (fin_dn, oth_dn)]
    drgain = jnp.sum(dgain_h, axis=0)
    small = _pack(jnp.sum(dsinks, axis=1)[None, :], dg_again, dlb, jnp.zeros_like(dlb), drgain,
                  [dg_mix_pre, dg_mix_post, dg_mlp_pre, dg_mlp_post], loss=loss_row)
    return gx, big, small


def kernel(x, w_in, attn_sinks, attn_out_gain, rnn_lb_logits, rnn_norm_gain, w_out, mix_pre_gain, mix_post_gain, mlp_pre_gain, mlp_post_gain, w_up, w_down, loss_target, m_w_in, m_attn_sinks, m_attn_out_gain, m_rnn_lb_logits, m_rnn_norm_gain, m_w_out, m_mix_pre_gain, m_mix_post_gain, m_mlp_pre_gain, m_mlp_post_gain, m_w_up, m_w_down, v_w_in, v_attn_sinks, v_attn_out_gain, v_rnn_lb_logits, v_rnn_norm_gain, v_w_out, v_mix_pre_gain, v_mix_post_gain, v_mlp_pre_gain, v_mlp_post_gain, v_w_up, v_w_down):
    ax, ay, ac = _place()
    where = jnp.stack([2 * ax + ay, ac]).astype(jnp.int32)
    t = lambda a: jnp.swapaxes(a, 1, 2)
    big_w = [t(w_in), w_out, w_up, w_down]
    big_m = [t(m_w_in), m_w_out, m_w_up, m_w_down]
    big_v = [t(v_w_in), v_w_out, v_w_up, v_w_down]

    names = ["w_in", "w_out", "w_up", "w_down"]
    bufs = [_cast_slots(w, where, name="cast_" + nm) for w, nm in zip(big_w, names)]
    gx, big_g, small_part = _layer_grads(
        x[0], loss_target[0], bufs, where, attn_sinks, attn_out_gain, rnn_lb_logits, rnn_norm_gain,
        mix_pre_gain, mix_post_gain, mlp_pre_gain, mlp_post_gain)

    grads, deltas, new_m, new_v = [], [], [], []
    for (f, o), w, m, v, nm in zip(big_g, big_w, big_m, big_v, names):
        res = _adamw(w, f, o, m, v, where, name="adamw_" + nm, halves_last=(nm == "w_in"))
        if nm == "w_in":
            res = [t(r) for r in res]
        g, d, nm_, nv_ = res
        grads.append(g)
        deltas.append(d)
        new_m.append(nm_)
        new_v.append(nv_)

    def pack_params(sinks, again, logits, rgain, gains):
        return _pack(sinks, again, logits[0:1], logits[1:2], rgain, gains)

    pw = pack_params(attn_sinks, attn_out_gain, rnn_lb_logits, rnn_norm_gain,
                     [mix_pre_gain, mix_post_gain, mlp_pre_gain, mlp_post_gain])
    pm = pack_params(m_attn_sinks, m_attn_out_gain, m_rnn_lb_logits, m_rnn_norm_gain,
                     [m_mix_pre_gain, m_mix_post_gain, m_mlp_pre_gain, m_mlp_post_gain])
    pv = pack_params(v_attn_sinks, v_attn_out_gain, v_rnn_lb_logits, v_rnn_norm_gain,
                     [v_mix_pre_gain, v_mix_post_gain, v_mlp_pre_gain, v_mlp_post_gain])
    packs = _small_reduce_adamw(small_part, pw, pm, pv, name="small_reduce_adamw")

    def unpack(p):
        seg = lambda o, k: p[:, o:o + k]
        logits = jnp.concatenate([seg(SEG_L0, RNN_W), seg(SEG_L1, RNN_W)], axis=0)
        gains = [seg(SEG_G + i * D_MODEL, D_MODEL) for i in range(4)]
        return dict(sinks=seg(SEG_SINK, N_Q), again=seg(SEG_AGAIN, ATTN_W), logits=logits,
                    rgain=seg(SEG_RGAIN, RNN_HD), gains=gains)

    def order(small, big):
        return [big[0], small["sinks"], small["again"], small["logits"], small["rgain"], big[1],
                *small["gains"], big[2], big[3]]

    loss = packs[0][0, 0]
    outs = [loss, gx[None]]
    for p, b in zip(packs, [grads, deltas, new_m, new_v]):
        outs += order(unpack(p), b)
    return tuple(outs)
```

```python
import functools

import jax
import jax.numpy as jnp
from jax import lax
from jax.experimental import pallas as pl
from jax.experimental.pallas import tpu as pltpu

F32 = jnp.float32
BF16 = jnp.bfloat16
MESH = pl.DeviceIdType.MESH

EPS = 1e-6
D_MODEL = 2048
ATTN_W = 1024
HEAD_DIM = 64
N_Q = 16
N_KV = 2
GROUP = 8
BLK = 128
RNN_W = 1024
RNN_HD = 128
N_RNN = 8
CHUNK = 64
SUB_FWD = 16
SUB_BWD = 8
D_FF = 8192
IN_W = 5376
N_CHIPS = 4
KV_COL = ATTN_W
QR_COL = ATTN_W + 2 * 128
FR_COL = QR_COL + RNN_W
IR_COL = FR_COL + RNN_W
GR_COL = IR_COL + RNN_W

ADAM_LR = 0.001
ADAM_B1 = 0.9
ADAM_B2 = 0.999
ADAM_EPS = 1e-08
ADAM_WD = 0.01
ADAM_STEP = 10

VMEM_LIMIT = 48 * 1024 * 1024

NT = (((1,), (1,)), ((), ()))
TN = (((0,), (0,)), ((), ()))


def _params(sem=None):
    return pltpu.CompilerParams(dimension_semantics=sem, vmem_limit_bytes=VMEM_LIMIT)


def _sigmoid(x):
    return 1.0 / (1.0 + jnp.exp(-x))


ANY = pl.BlockSpec(memory_space=pl.ANY)


def _place():
    return lax.axis_index("x"), lax.axis_index("y"), lax.axis_index("c")


def _other_chips(x, y):
    return [(1 - x, y), (x, 1 - y), (1 - x, 1 - y)]


class _Exchange:
    def __init__(self, srcs, outs, ncopy, build, aliases=None):
        self.srcs, self.outs, self.ncopy, self.build = list(srcs), list(outs), ncopy, build
        self.aliases = aliases or {}


def _remote(src, dst, send_sems, recv_sems, k, to):
    return pltpu.make_async_remote_copy(src_ref=src, dst_ref=dst, send_sem=send_sems.at[k],
                                        recv_sem=recv_sems.at[k], device_id=to, device_id_type=MESH)


def _call(body, *, name, grid, in_specs, out_specs, out_shape, args, scratch_shapes=(), semantics=None,
          exchanges=(), into=None):
    in_specs, out_specs, out_shape = list(in_specs), list(out_specs), list(out_shape)
    scratch_shapes = list(scratch_shapes)
    ni, no, ns = len(in_specs), len(out_specs), len(scratch_shapes)
    xsrc = [s for x in exchanges for s in x.srcs]
    xout = [o for x in exchanges for o in x.outs]
    into = into or {}
    xsrc += [into[k] for k in sorted(into)]
    nxi, nxo = len(xsrc), len(xout)
    aliases = {nxi - len(into) + ni + q: k for q, k in enumerate(sorted(into))}
    a0 = b0 = 0
    for x in exchanges:
        for si, oi in x.aliases.items():
            aliases[ni + a0 + si] = no + b0 + oi
        a0 += len(x.srcs)
        b0 += len(x.outs)
    sems = []
    for x in exchanges:
        sems += [pltpu.SemaphoreType.DMA((x.ncopy,)), pltpu.SemaphoreType.DMA((x.ncopy,))]

    def wrapped(*refs):
        ins, xi = refs[:ni], refs[ni:ni + nxi]
        outs, xo = refs[ni + nxi:ni + nxi + no], refs[ni + nxi + no:ni + nxi + no + nxo]
        rest = refs[ni + nxi + no + nxo:]
        scr, sm = rest[:ns], rest[ns:]

        def copies():
            cps = []
            a = b = 0
            for k, x in enumerate(exchanges):
                cps += x.build(xi[a:a + len(x.srcs)], xo[b:b + len(x.outs)], sm[2 * k], sm[2 * k + 1])
                a += len(x.srcs)
                b += len(x.outs)
            return cps

        def start():
            for cp in copies():
                cp.start()

        def wait():
            for cp in copies():
                cp.wait()

        if not exchanges:
            body(*ins, *outs, *scr)
        elif not grid:
            start()
            body(*ins, *outs, *scr)
            wait()
        else:
            first = last = None
            for ax, g in enumerate(grid):
                f = pl.program_id(ax) == 0
                l = pl.program_id(ax) == g - 1
                first = f if first is None else first & f
                last = l if last is None else last & l
            pl.when(first)(start)
            body(*ins, *outs, *scr)
            pl.when(last)(wait)

    if exchanges and semantics is not None:
        semantics = ("arbitrary",) * len(grid)
    kwargs = dict(grid=grid) if grid else {}
    res = pl.pallas_call(
        wrapped, name=name,
        in_specs=in_specs + [ANY] * nxi, out_specs=out_specs + [ANY] * nxo,
        out_shape=out_shape + xout, scratch_shapes=scratch_shapes + sems,
        input_output_aliases=aliases,
        compiler_params=_params(semantics), **kwargs,
    )(*args, *xsrc)
    res = list(res)
    mine, theirs = res[:no], res[no:]
    per = []
    b = 0
    for x in exchanges:
        per.append(theirs[b:b + len(x.outs)])
        b += len(x.outs)
    return mine, per


def _run_exchange(x, *, name):
    return _call(lambda: None, name=name, grid=(), in_specs=[], out_specs=[], out_shape=[], args=[],
                 exchanges=[x])[1][0]


def _x_gather(bufs, ici=None, d2d=None, cross=None):
    n = len(bufs)
    plan = [(a, kind, rows[a]) for a in range(n) for kind, rows in (("ici", ici), ("d2d", d2d), ("cross", cross))
            if rows is not None and rows[a] is not None]

    def build(srcs, outs, ss, rs):
        x, y, c = _place()
        cps = []
        for q, (a, kind, rows) in enumerate(plan):
            piece = pl.ds(*rows)
            for j, (px, py) in enumerate(_other_chips(x, y)):
                if kind == "d2d":
                    slot, to = 4 * px + 2 * py + c, (x, y, 1 - c)
                else:
                    slot, to = 4 * x + 2 * y + c, (px, py, c if kind == "ici" else 1 - c)
                cps.append(_remote(srcs[a].at[slot, piece], outs[a].at[slot, piece], ss, rs, 3 * q + j, to))
        return cps

    outs = [jax.ShapeDtypeStruct(b.shape, b.dtype) for b in bufs]
    return _Exchange(bufs, outs, 3 * len(plan), build, aliases={a: a for a in range(n)})


def _x_pair(grads, halves_last=False):
    n = len(grads)

    def build(srcs, outs, ss, rs):
        x, y, c = _place()

        def half(r):
            if not halves_last:
                return r.at[:, 1 - c]
            ch = r.shape[2] // 2
            return r.at[:, :, pl.ds(pl.multiple_of((1 - c) * ch, 128), ch)]

        return [_remote(half(srcs[a]), outs[a], ss, rs, a, (x, y, 1 - c)) for a in range(n)]

    if halves_last:
        outs = [jax.ShapeDtypeStruct(g.shape[:2] + (g.shape[2] // 2,), g.dtype) for g in grads]
    else:
        outs = [jax.ShapeDtypeStruct((4,) + g.shape[2:], g.dtype) for g in grads]
    return _Exchange(grads, outs, n, build)


def _x_chip(wires, rows=None, into=None):
    n = len(wires)
    rows = rows or [(0, w.shape[1]) for w in wires]

    def build(srcs, outs, ss, rs):
        x, y, c = _place()
        cps = []
        for a in range(n):
            piece = pl.ds(*rows[a])
            for j, (px, py) in enumerate(_other_chips(x, y)):
                cps.append(_remote(srcs[a].at[2 * px + py, piece], outs[a].at[j, piece], ss, rs,
                                   3 * a + j, (px, py, c)))
        return cps

    outs = [jax.ShapeDtypeStruct((3,) + w.shape[1:], w.dtype) for w in wires]
    if into is None:
        return _Exchange(wires, outs, 3 * n, build)
    return _Exchange(list(wires) + list(into), outs, 3 * n, build, aliases={n + a: a for a in range(n)})


def _x_share(halves):
    n = len(halves)

    def build(srcs, outs, ss, rs):
        x, y, c = _place()
        return [_remote(srcs[a], outs[a], ss, rs, a, (x, y, 1 - c)) for a in range(n)]

    outs = [jax.ShapeDtypeStruct(h.shape, h.dtype) for h in halves]
    return _Exchange(halves, outs, n, build)


def _mm(a, w, *, tm, tn, tk, out_dtype, name, a_square=False, relu=False, mul2=None, w_layout="kn",
        m_blocks=None, out_into=None, exchanges=()):
    m, k = a.shape
    m_first, m_count = m_blocks or (0, m // tm)
    a_spec = pl.BlockSpec((tm, tk), lambda i, j, kk: (i + m_first, kk))
    if w_layout == "kn":
        n = w.shape[1]
        w_spec = pl.BlockSpec((tk, tn), lambda i, j, kk: (kk, j))
    elif w_layout == "nk":
        n = w.shape[0]
        w_spec = pl.BlockSpec((tn, tk), lambda i, j, kk: (j, kk))
    elif w_layout == "skn":
        n = w.shape[0] * w.shape[2]
        per_n = w.shape[2] // tn
        w_spec = pl.BlockSpec((None, tk, tn), lambda i, j, kk: (j // per_n, kk, j % per_n))
    else:
        assert w_layout == "snk"
        n = w.shape[1]
        per_k = w.shape[2] // tk
        w_spec = pl.BlockSpec((None, tn, tk), lambda i, j, kk: (kk // per_k, j, kk % per_k))
    w_dims = NT if w_layout in ("nk", "snk") else (((1,), (0,)), ((), ()))
    nk = k // tk
    assert m % tm == 0 and n % tn == 0 and k % tk == 0

    def body(*refs):
        if mul2 is not None:
            a_ref, w_ref, e_ref, o_ref, acc_ref = refs
        else:
            a_ref, w_ref, o_ref, acc_ref = refs
            e_ref = None
        kk = pl.program_id(2)
        av = a_ref[...]
        if a_square:
            af = av.astype(F32)
            av = (af * af).astype(BF16)
        part = lax.dot_general(av, w_ref[...], w_dims, preferred_element_type=F32)

        def finish(r):
            if relu:
                r = jnp.maximum(r, 0.0)
            if e_ref is not None:
                r = 2.0 * e_ref[...].astype(F32) * r
            o_ref[...] = r.astype(out_dtype)

        if nk == 1:
            finish(part)
        else:
            @pl.when(kk == 0)
            def _():
                acc_ref[...] = part

            @pl.when(kk > 0)
            def _():
                acc_ref[...] += part

            @pl.when(kk == nk - 1)
            def _():
                finish(acc_ref[...])

    in_specs = [a_spec, w_spec]
    args = [a, w]
    if mul2 is not None:
        in_specs.append(pl.BlockSpec((tm, tn), lambda i, j, kk: (i + m_first, j)))
        args.append(mul2)
    acc_shape = (tm, tn) if nk > 1 else (8, 128)
    (out,), per = _call(
        body, name=name, grid=(m_count, n // tn, nk),
        in_specs=in_specs, out_specs=[pl.BlockSpec((tm, tn), lambda i, j, kk: (i + m_first, j))],
        out_shape=[jax.ShapeDtypeStruct((m, n), out_dtype)], args=args,
        scratch_shapes=[pltpu.VMEM(acc_shape, F32)],
        semantics=("parallel", "parallel", "arbitrary"), exchanges=exchanges,
        into=None if out_into is None else {0: out_into})
    return (out, per) if exchanges else out


def _mm_tn(a, b, *, tm, tn, tt, name, a_square=False, n_split=1, b_blocks=None, exchanges=()):
    t, m = a.shape
    nb = len(b_blocks) if b_blocks else 1
    n = tn if b_blocks else b.shape[1]
    assert t % tt == 0 and m % tm == 0 and n % tn == 0 and (n // n_split) % tn == 0
    per = n // n_split // tn

    def body(a_ref, *refs):
        b_refs, o_ref = refs[:nb], refs[nb]
        ti = pl.program_id(2)
        av = a_ref[...]
        if a_square:
            af = av.astype(F32)
            av = (af * af).astype(BF16)
        bv = b_refs[0][...] if nb == 1 else jnp.concatenate([r[...] for r in b_refs], axis=1)
        part = lax.dot_general(av, bv, TN, preferred_element_type=F32)

        @pl.when(ti == 0)
        def _():
            o_ref[...] = part

        @pl.when(ti > 0)
        def _():
            o_ref[...] += part

    if b_blocks:
        b_specs = [pl.BlockSpec((tt, tn // nb), functools.partial(lambda blk, i, j, ti: (ti, blk), blk))
                   for blk in b_blocks]
    else:
        b_specs = [pl.BlockSpec((tt, tn), lambda i, j, ti: (ti, j))]
    (out,), xres = _call(
        body, name=name, grid=(m // tm, n // tn, t // tt),
        in_specs=[pl.BlockSpec((tt, tm), lambda i, j, ti: (ti, i))] + b_specs,
        out_specs=[pl.BlockSpec((None, tm, tn), lambda i, j, ti: (j // per, i, j % per))],
        out_shape=[jax.ShapeDtypeStruct((n_split, m, n // n_split), F32)], args=[a] + [b] * nb,
        semantics=("parallel", "parallel", "arbitrary"), exchanges=exchanges)
    return (out, xres) if exchanges else out


def _rstd(x):
    return lax.rsqrt(jnp.mean(x * x, axis=-1, keepdims=True) + EPS)


def _rms_cast_gather(x, g, buf, *, tm, name):
    t, d = x.shape
    steps = t // tm

    def body(x_ref, g_ref, b_in, o_ref, b_out, send_sems, recv_sems):
        i = pl.program_id(0)
        xc, yc, c = _place()
        chips = _other_chips(xc, yc)

        def slot(px, py, pc):
            return b_out.at[4 * px + 2 * py + pc]

        def sent(j):
            return _remote(b_in.at[4 * xc + 2 * yc + c], slot(xc, yc, c), send_sems, recv_sems, j, (*chips[j], c))

        def passed(j):
            return _remote(slot(*chips[j], c), slot(*chips[j], c), send_sems, recv_sems, 3 + j, (xc, yc, 1 - c))

        @pl.when(i == 0)
        def _():
            for j in range(3):
                sent(j).start()

        xv = x_ref[...]
        o_ref[...] = (xv * _rstd(xv) * g_ref[...]).astype(BF16)

        @pl.when(i == steps - 1)
        def _():
            for j in range(3):
                sent(j).wait_recv()
                passed(j).start()
            for j in range(3):
                passed(j).wait_recv()
                passed(j).wait_send()
                sent(j).wait_send()

    return pl.pallas_call(
        body, name=name, grid=(steps,),
        in_specs=[pl.BlockSpec((tm, d), lambda i: (i, 0)), pl.BlockSpec((1, d), lambda i: (0, 0)), ANY],
        out_specs=[pl.BlockSpec((tm, d), lambda i: (i, 0)), ANY],
        out_shape=[jax.ShapeDtypeStruct((t, d), BF16), jax.ShapeDtypeStruct(buf.shape, buf.dtype)],
        scratch_shapes=[pltpu.SemaphoreType.DMA((6,)), pltpu.SemaphoreType.DMA((6,))],
        input_output_aliases={2: 1},
        compiler_params=_params(("arbitrary",)),
    )(x, g, buf)


def _mix_cat(attn, rnn, gain, *, tm, name):
    t = attn.shape[0]

    def body(a_ref, r_ref, g_ref, o_ref):
        av = a_ref[...]
        o_ref[:, :ATTN_W] = (av * _rstd(av) * g_ref[...]).astype(BF16)
        o_ref[:, ATTN_W:] = r_ref[...].astype(BF16)

    return pl.pallas_call(
        body, name=name, grid=(t // tm,),
        in_specs=[pl.BlockSpec((tm, ATTN_W), lambda i: (i, 0)), pl.BlockSpec((tm, RNN_W), lambda i: (i, 0)),
                  pl.BlockSpec((1, ATTN_W), lambda i: (0, 0))],
        out_specs=pl.BlockSpec((tm, D_MODEL), lambda i: (i, 0)),
        out_shape=jax.ShapeDtypeStruct((t, D_MODEL), BF16),
        compiler_params=_params(("parallel",)),
    )(attn, rnn, gain)


def _post_norm_res(mixed, g_post, res, g_next, *, tm, name, exchanges=()):
    t, d = mixed.shape

    def body(m_ref, gp_ref, r_ref, gn_ref, x1_ref, h2_ref):
        mv = m_ref[...].astype(F32)
        x1 = r_ref[...] + mv * _rstd(mv) * gp_ref[...]
        x1_ref[...] = x1
        h2_ref[...] = (x1 * _rstd(x1) * gn_ref[...]).astype(BF16)

    row = pl.BlockSpec((tm, d), lambda i: (i, 0))
    vec = pl.BlockSpec((1, d), lambda i: (0, 0))
    res_, xres = _call(
        body, name=name, grid=(t // tm,),
        in_specs=[row, vec, row, vec], out_specs=[row, row],
        out_shape=[jax.ShapeDtypeStruct((t, d), F32), jax.ShapeDtypeStruct((t, d), BF16)],
        args=[mixed, g_post, res, g_next], semantics=("parallel",), exchanges=exchanges)
    return (*res_, xres) if exchanges else res_


def _rms_bwd(dyn, xin, g, res, *, tm, out_dtype, name, col_block=0, exchanges=()):
    t, d = xin.shape

    def body(*refs):
        if res is not None:
            dy_ref, x_ref, g_ref, r_ref, dx_ref, dg_ref = refs
        else:
            dy_ref, x_ref, g_ref, dx_ref, dg_ref = refs
        i = pl.program_id(0)
        xv = x_ref[...].astype(F32)
        dy = dy_ref[...].astype(F32)
        r = _rstd(xv)
        xh = xv * r
        part = jnp.sum(dy * xh, axis=0, keepdims=True)

        @pl.when(i == 0)
        def _():
            dg_ref[...] = part

        @pl.when(i > 0)
        def _():
            dg_ref[...] += part

        tt = dy * g_ref[...]
        dx = r * (tt - xh * jnp.mean(tt * xh, axis=-1, keepdims=True))
        if res is not None:
            dx = dx + r_ref[...].astype(F32)
        dx_ref[...] = dx.astype(out_dtype)

    row = pl.BlockSpec((tm, d), lambda i: (i, 0))
    vec = pl.BlockSpec((1, d), lambda i: (0, 0))
    in_specs = [pl.BlockSpec((tm, d), lambda i: (i, col_block)), row, vec]
    args = [dyn, xin, g]
    if res is not None:
        in_specs.append(row)
        args.append(res)
    res, xres = _call(
        body, name=name, grid=(t // tm,),
        in_specs=in_specs, out_specs=[row, vec],
        out_shape=[jax.ShapeDtypeStruct((t, d), out_dtype), jax.ShapeDtypeStruct((1, d), F32)], args=args,
        semantics=("arbitrary",), exchanges=exchanges)
    return (*res, xres) if exchanges else res


def _loss_head(y, g_post, x1, target, *, tm, name):
    t, d = y.shape

    def body(y_ref, g_ref, x1_ref, t_ref, dy_ref, dx2_ref, loss_ref, dg_ref):
        i = pl.program_id(0)
        yv = y_ref[...].astype(F32)
        r = _rstd(yv)
        yh = yv * r
        gv = g_ref[...]
        err = x1_ref[...] + yh * gv - t_ref[...]
        lpart = 0.5 * jnp.sum(jnp.mean(err * err, axis=-1, keepdims=True), axis=0, keepdims=True)
        dx2 = err * (1.0 / d)
        dgp = jnp.sum(dx2 * yh, axis=0, keepdims=True)
        lane = lax.broadcasted_iota(jnp.int32, (1, 128), 1)
        lrow = jnp.where(lane == 0, lpart, 0.0)

        @pl.when(i == 0)
        def _():
            dg_ref[...] = dgp
            loss_ref[...] = lrow

        @pl.when(i > 0)
        def _():
            dg_ref[...] += dgp
            loss_ref[...] += lrow

        tt = dx2 * gv
        dy_ref[...] = (r * (tt - yh * jnp.mean(tt * yh, axis=-1, keepdims=True))).astype(BF16)
        dx2_ref[...] = dx2.astype(BF16)

    row = pl.BlockSpec((tm, d), lambda i: (i, 0))
    vec = pl.BlockSpec((1, d), lambda i: (0, 0))
    return pl.pallas_call(
        body, name=name, grid=(t // tm,),
        in_specs=[row, vec, row, row],
        out_specs=[row, row, pl.BlockSpec((1, 128), lambda i: (0, 0)), vec],
        out_shape=[jax.ShapeDtypeStruct((t, d), BF16), jax.ShapeDtypeStruct((t, d), BF16),
                   jax.ShapeDtypeStruct((1, 128), F32), jax.ShapeDtypeStruct((1, d), F32)],
        compiler_params=_params(("arbitrary",)),
    )(y, g_post, x1, target)


def _alibi_slope(h):
    return 2.0 ** (-8.0 * (h + 1) / N_Q)


PAIR = 2 * HEAD_DIM
N_PAIRS = N_Q // 2
PAIRS_PER_KV = GROUP // 2
SMEM = pl.BlockSpec(memory_space=pltpu.SMEM)


def _swa_mask(n):
    key = lax.broadcasted_iota(jnp.int32, (2 * BLK, BLK), 0)
    qry = lax.broadcasted_iota(jnp.int32, (2 * BLK, BLK), 1)
    dist = qry + BLK - key
    valid = (dist >= 0) & (dist < BLK) & ((key >= BLK) | (n > 0))
    return valid, dist.astype(F32)


def _block_diag(kvp_ref, kvc_ref, off):
    a = jnp.concatenate([kvp_ref[:, off:off + HEAD_DIM], kvc_ref[:, off:off + HEAD_DIM]], axis=0).astype(BF16)
    z = jnp.zeros_like(a)
    return jnp.concatenate([jnp.concatenate([a, z], axis=1), jnp.concatenate([z, a], axis=1)], axis=0)


def _swa_scores(s2, e, hh, valid, distf):
    s = s2[2 * BLK * e:2 * BLK * (e + 1)] * (HEAD_DIM ** -0.5) - _alibi_slope(hh) * distf
    return jnp.where(valid, s, -1e30)


def _swa_fwd(proj, sinks, *, name, exchanges=()):
    t = proj.shape[0]
    nb = t // BLK
    kvb = KV_COL // (2 * 128)

    def body(sink_ref, q_ref, kvc_ref, kvp_ref, o_ref, lse_ref):
        n = pl.program_id(0)
        valid, distf = _swa_mask(n)
        for kvh in range(N_KV):
            k2 = _block_diag(kvp_ref, kvc_ref, kvh * HEAD_DIM)
            v2 = _block_diag(kvp_ref, kvc_ref, 128 + kvh * HEAD_DIM)
            for jp in range(PAIRS_PER_KV):
                pair = kvh * PAIRS_PER_KV + jp
                lanes = slice(pair * PAIR, (pair + 1) * PAIR)
                s2 = lax.dot_general(k2, q_ref[:, lanes].astype(BF16), NT, preferred_element_type=F32)
                probs = []
                for e in range(2):
                    hh = 2 * pair + e
                    s = _swa_scores(s2, e, hh, valid, distf)
                    sink = sink_ref[0, hh]
                    mx = jnp.maximum(jnp.max(s, axis=0, keepdims=True), sink)
                    p = jnp.exp(s - mx)
                    l = jnp.sum(p, axis=0, keepdims=True) + jnp.exp(sink - mx)
                    probs.append((p * (1.0 / l)).astype(BF16))
                    lse_ref[hh:hh + 1, :] = mx + jnp.log(l)
                o_ref[:, lanes] = lax.dot_general(jnp.concatenate(probs, axis=0), v2, TN,
                                                  preferred_element_type=F32)

    res, xres = _call(
        body, name=name, grid=(nb,),
        in_specs=[SMEM,
                  pl.BlockSpec((BLK, ATTN_W), lambda n: (n, 0)),
                  pl.BlockSpec((BLK, 256), lambda n: (n, kvb)),
                  pl.BlockSpec((BLK, 256), lambda n: (jnp.maximum(n - 1, 0), kvb))],
        out_specs=[pl.BlockSpec((BLK, ATTN_W), lambda n: (n, 0)),
                   pl.BlockSpec((None, N_Q, BLK), lambda n: (n, 0, 0))],
        out_shape=[jax.ShapeDtypeStruct((t, ATTN_W), F32), jax.ShapeDtypeStruct((nb, N_Q, BLK), F32)],
        args=[sinks, proj, proj, proj], semantics=("parallel",), exchanges=exchanges)
    return (*res, xres) if exchanges else res


def _swa_bwd(proj, sinks, dattn, lse, *, name, exchanges=()):
    t = proj.shape[0]
    nb = t // BLK
    kvb = KV_COL // (2 * 128)

    def body(sink_ref, q_ref, kvc_ref, kvp_ref, do_ref, lse_ref, dq_ref, dkv_ref, dsink_ref, carry_ref):
        n = pl.program_id(0)

        @pl.when(n == 0)
        def _():
            dsink_ref[...] = jnp.zeros_like(dsink_ref)
            carry_ref[...] = jnp.zeros_like(carry_ref)

        @pl.when(n < nb)
        def _():
            valid, distf = _swa_mask(n)
            for kvh in range(N_KV):
                k2 = _block_diag(kvp_ref, kvc_ref, kvh * HEAD_DIM)
                v2 = _block_diag(kvp_ref, kvc_ref, 128 + kvh * HEAD_DIM)
                dk2 = jnp.zeros((4 * BLK, PAIR), F32)
                dv2 = jnp.zeros((4 * BLK, PAIR), F32)
                for jp in range(PAIRS_PER_KV):
                    pair = kvh * PAIRS_PER_KV + jp
                    lanes = slice(pair * PAIR, (pair + 1) * PAIR)
                    q2 = q_ref[:, lanes].astype(BF16)
                    do2 = do_ref[:, lanes].astype(BF16)
                    s2 = lax.dot_general(k2, q2, NT, preferred_element_type=F32)
                    dp2 = lax.dot_general(v2, do2, NT, preferred_element_type=F32)
                    probs, dss = [], []
                    for e in range(2):
                        hh = 2 * pair + e
                        lse_h = lse_ref[hh:hh + 1, :]
                        p = jnp.exp(_swa_scores(s2, e, hh, valid, distf) - lse_h)
                        dp = dp2[2 * BLK * e:2 * BLK * (e + 1)]
                        delta = jnp.sum(p * dp, axis=0, keepdims=True)
                        dsink_ref[hh:hh + 1, :] += -jnp.exp(sink_ref[0, hh] - lse_h) * delta
                        probs.append(p.astype(BF16))
                        dss.append((p * (dp - delta)).astype(BF16))
                    ds2 = jnp.concatenate(dss, axis=0)
                    dq_ref[:, lanes] = (lax.dot_general(ds2, k2, TN, preferred_element_type=F32)
                                        * (HEAD_DIM ** -0.5)).astype(BF16)
                    dk2 = dk2 + jnp.dot(ds2, q2, preferred_element_type=F32)
                    dv2 = dv2 + jnp.dot(jnp.concatenate(probs, axis=0), do2, preferred_element_type=F32)
                dk_cat = (dk2[:2 * BLK, :HEAD_DIM] + dk2[2 * BLK:, HEAD_DIM:]) * (HEAD_DIM ** -0.5)
                dv_cat = dv2[:2 * BLK, :HEAD_DIM] + dv2[2 * BLK:, HEAD_DIM:]
                ko = kvh * HEAD_DIM
                vo = 128 + kvh * HEAD_DIM
                dkv_ref[:, ko:ko + HEAD_DIM] = (carry_ref[:, ko:ko + HEAD_DIM] + dk_cat[:BLK]).astype(BF16)
                dkv_ref[:, vo:vo + HEAD_DIM] = (carry_ref[:, vo:vo + HEAD_DIM] + dv_cat[:BLK]).astype(BF16)
                carry_ref[:, ko:ko + HEAD_DIM] = dk_cat[BLK:]
                carry_ref[:, vo:vo + HEAD_DIM] = dv_cat[BLK:]

        @pl.when(n == nb)
        def _():
            dkv_ref[...] = carry_ref[...].astype(BF16)

    last = nb - 1
    res, xres = _call(
        body, name=name, grid=(nb + 1,),
        in_specs=[SMEM,
                  pl.BlockSpec((BLK, ATTN_W), lambda n: (jnp.minimum(n, last), 0)),
                  pl.BlockSpec((BLK, 256), lambda n: (jnp.minimum(n, last), kvb)),
                  pl.BlockSpec((BLK, 256), lambda n: (jnp.maximum(jnp.minimum(n, last) - 1, 0), kvb)),
                  pl.BlockSpec((BLK, ATTN_W), lambda n: (jnp.minimum(n, last), 0)),
                  pl.BlockSpec((None, N_Q, BLK), lambda n: (jnp.minimum(n, last), 0, 0))],
        out_specs=[pl.BlockSpec((BLK, ATTN_W), lambda n: (jnp.minimum(n, last), 0)),
                   pl.BlockSpec((BLK, 256), lambda n: (jnp.maximum(n - 1, 0), 0)),
                   pl.BlockSpec((N_Q, BLK), lambda n: (0, 0))],
        out_shape=[jax.ShapeDtypeStruct((t, ATTN_W), BF16), jax.ShapeDtypeStruct((t, 256), BF16),
                   jax.ShapeDtypeStruct((N_Q, BLK), F32)],
        scratch_shapes=[pltpu.VMEM((BLK, 256), F32)],
        args=[sinks, proj, proj, proj, dattn, lse], semantics=("arbitrary",), exchanges=exchanges)
    return (*res, xres) if exchanges else res


def _cumsum_rows(x):
    n = x.shape[0]
    row = lax.broadcasted_iota(jnp.int32, x.shape, 0)
    s = 1
    while s < n:
        x = x + jnp.where(row >= s, pltpu.roll(x, s, axis=0), 0.0)
        s *= 2
    return x


def _rev_cumsum_rows(x):
    n = x.shape[0]
    row = lax.broadcasted_iota(jnp.int32, x.shape, 0)
    s = 1
    while s < n:
        x = x + jnp.where(row < n - s, pltpu.roll(x, n - s, axis=0), 0.0)
        s *= 2
    return x


def _lower_bound(lbl_ref):
    l0 = lbl_ref[0:1, :]
    l1 = lbl_ref[1:2, :]
    mx = jnp.maximum(l0, l1)
    e0 = jnp.exp(l0 - mx)
    e1 = jnp.exp(l1 - mx)
    return e0 / (e0 + e1)


def _hgrn_gates(z, lb):
    sg = _sigmoid(z)
    f = lb + (1.0 - lb) * sg
    return sg, f, jnp.log(f), 1.0 - f


def _sub_factors(b, k, i, sub, trim):
    need = -(-sub * i // 16) * 16 if trim else CHUNK
    rows = lax.broadcasted_iota(jnp.int32, (need, RNN_HD), 0)
    ref = b[sub * i - 1:sub * i, :]
    qfac = jnp.exp(b[sub * i:sub * (i + 1), :] - ref)
    kfac = jnp.where(rows < sub * i, jnp.exp(ref - b[:need]), 0.0)
    kt = (k[:need] * kfac).astype(BF16)
    if need < CHUNK:
        kt = jnp.concatenate([kt, jnp.zeros((CHUNK - need, RNN_HD), BF16)], axis=0)
    return qfac, kfac, kt


def _diag_decay(bi, s):
    trow = lax.broadcasted_iota(jnp.int32, bi.shape, 0)
    return jnp.where(trow >= s, jnp.exp(bi - bi[s:s + 1, :]), 0.0)


def _hgrn_fwd(proj, lb_logits, norm_gain, *, tb, name, exchanges=()):
    t = proj.shape[0]
    ntb = t // tb
    nch = tb // CHUNK
    qb, fb, ib, gb = QR_COL // 128, FR_COL // 128, IR_COL // 128, GR_COL // 128

    def body(q_ref, f_ref, i_ref, g_ref, lbl_ref, gain_ref, o_ref, out_ref, s0_ref, st_ref):
        c = pl.program_id(1)

        @pl.when(c == 0)
        def _():
            st_ref[...] = jnp.zeros_like(st_ref)

        lb = _lower_bound(lbl_ref)
        gain = gain_ref[...]

        def chunk(ci, st):
            rows = slice(ci * CHUNK, (ci + 1) * CHUNK)
            _, _, lf, k = _hgrn_gates(f_ref[rows, :], lb)
            qr = q_ref[rows, :]
            q = qr * _sigmoid(qr)
            v = i_ref[rows, :]
            b = _cumsum_rows(lf)
            s0_ref[ci] = st
            o_inter = lax.dot_general((q * jnp.exp(b)).astype(BF16), st.astype(BF16), NT,
                                      preferred_element_type=F32)
            vb = v.astype(BF16)
            blast = b[CHUNK - 1:CHUNK, :]
            khat = (k * jnp.exp(blast - b)).astype(BF16)
            st = st * jnp.exp(blast) + lax.dot_general(vb, khat, TN, preferred_element_type=F32)
            blocks = []
            for i in range(CHUNK // SUB_FWD):
                blk = slice(SUB_FWD * i, SUB_FWD * (i + 1))
                qi, ki, vi, bi = q[blk], k[blk], v[blk], b[blk]
                oi = o_inter[blk]
                if i > 0:
                    qfac, _, kt = _sub_factors(b, k, i, SUB_FWD, trim=True)
                    att = lax.dot_general((qi * qfac).astype(BF16), kt, NT,
                                          preferred_element_type=F32)
                    oi = oi + jnp.dot(att.astype(BF16), vb, preferred_element_type=F32)
                for s in range(SUB_FWD):
                    qe = qi * _diag_decay(bi, s)
                    a = jnp.sum(qe * ki[s:s + 1, :], axis=1, keepdims=True)
                    oi = oi + a * vi[s:s + 1, :]
                blocks.append(oi)
            o = jnp.concatenate(blocks, axis=0)
            o_ref[rows, :] = o
            gr = g_ref[rows, :]
            out_ref[rows, :] = (o * _rstd(o) * gain * (gr * _sigmoid(gr))).astype(BF16)
            return st

        st = st_ref[...]
        for ci in range(nch):
            st = chunk(ci, st)
        st_ref[...] = st

    def col(base):
        return pl.BlockSpec((tb, RNN_HD), lambda h, c: (c, base + h))

    res, xres = _call(
        body, name=name, grid=(N_RNN, ntb),
        in_specs=[col(qb), col(fb), col(ib), col(gb),
                  pl.BlockSpec((2, RNN_HD), lambda h, c: (0, h)), pl.BlockSpec((1, RNN_HD), lambda h, c: (0, 0))],
        out_specs=[pl.BlockSpec((tb, RNN_HD), lambda h, c: (c, h)), pl.BlockSpec((tb, RNN_HD), lambda h, c: (c, h)),
                   pl.BlockSpec((None, nch, RNN_HD, RNN_HD), lambda h, c: (h, c, 0, 0))],
        out_shape=[jax.ShapeDtypeStruct((t, RNN_W), F32), jax.ShapeDtypeStruct((t, RNN_W), BF16),
                   jax.ShapeDtypeStruct((N_RNN, t // CHUNK, RNN_HD, RNN_HD), F32)],
        scratch_shapes=[pltpu.VMEM((RNN_HD, RNN_HD), F32)],
        args=[proj, proj, proj, proj, lb_logits, norm_gain],
        semantics=("parallel", "arbitrary"), exchanges=exchanges)
    return (*res, xres) if exchanges else res


def _hgrn_bwd(proj, lb_logits, norm_gain, o_pre, s0, dcat, *, tb, name, exchanges=()):
    t = proj.shape[0]
    ntb = t // tb
    nch = tb // CHUNK
    qb, fb, ib, gb = QR_COL // 128, FR_COL // 128, IR_COL // 128, GR_COL // 128
    sub = SUB_BWD
    nsub = CHUNK // sub

    def body(q_ref, f_ref, i_ref, g_ref, lbl_ref, gain_ref, o_ref, s0_ref, dout_ref,
             dq_ref, df_ref, di_ref, dg_ref, dlb_ref, dgain_ref,
             dst_ref, dqs_ref, dks_ref, dvs_ref):
        c = pl.program_id(1)

        @pl.when(c == 0)
        def _():
            dst_ref[...] = jnp.zeros_like(dst_ref)
            dlb_ref[...] = jnp.zeros_like(dlb_ref)
            dgain_ref[...] = jnp.zeros_like(dgain_ref)

        lb = _lower_bound(lbl_ref)
        gain = gain_ref[...]

        def chunk(ci, dst):
            rows = slice(ci * CHUNK, (ci + 1) * CHUNK)
            dqa_ref, dka_ref, dva_ref = dqs_ref.at[ci], dks_ref.at[ci], dvs_ref.at[ci]
            sg, f, lf, k = _hgrn_gates(f_ref[rows, :], lb)
            qr = q_ref[rows, :]
            sq = _sigmoid(qr)
            q = qr * sq
            v = i_ref[rows, :]
            b = _cumsum_rows(lf)

            dout = dout_ref[rows, :].astype(F32)
            o = o_ref[rows, :]
            gr = g_ref[rows, :]
            sgg = _sigmoid(gr)
            gate = gr * sgg
            rs = _rstd(o)
            nrm = o * rs
            dg_ref[rows, :] = (dout * nrm * gain * (sgg * (1.0 + gr * (1.0 - sgg)))).astype(BF16)
            dn = dout * gate
            dgain_ref[...] += jnp.sum(dn * nrm, axis=0, keepdims=True)
            tt = dn * gain
            do = rs * (tt - nrm * jnp.mean(tt * nrm, axis=-1, keepdims=True))

            dob = do.astype(BF16)
            vb = v.astype(BF16)
            eb = jnp.exp(b)
            blast = b[CHUNK - 1:CHUNK, :]
            ebl = jnp.exp(blast - b)
            dstb = dst.astype(BF16)
            khat = (k * ebl).astype(BF16)
            s0 = s0_ref[ci]
            dqa_ref[...] = eb * jnp.dot(dob, s0.astype(BF16), preferred_element_type=F32)
            dk_state = ebl * jnp.dot(vb, dstb, preferred_element_type=F32)
            dka_ref[...] = dk_state
            d_blast = (jnp.sum(k * dk_state, axis=0, keepdims=True)
                       + jnp.exp(blast) * jnp.sum(dst * s0, axis=0, keepdims=True))
            dva_ref[...] = lax.dot_general(khat, dstb, NT, preferred_element_type=F32)
            dst_next = dst * jnp.exp(blast) + lax.dot_general(dob, (q * eb).astype(BF16), TN,
                                                              preferred_element_type=F32)
            pm = lax.dot_general(dob, vb, NT, preferred_element_type=F32)
            for i in range(nsub):
                blk = slice(sub * i, sub * (i + 1))
                qi, ki, vi, bi, doi = q[blk], k[blk], v[blk], b[blk], do[blk]
                dqi = dqa_ref[blk, :]
                if i > 0:
                    qfac, kfac, kt = _sub_factors(b, k, i, sub, trim=False)
                    qt = (qi * qfac).astype(BF16)
                    att = lax.dot_general(qt, kt, NT, preferred_element_type=F32).astype(BF16)
                    pmi = pm[blk, :].astype(BF16)
                    dva_ref[...] += lax.dot_general(att, doi.astype(BF16), TN, preferred_element_type=F32)
                    dqi = dqi + qfac * jnp.dot(pmi, kt, preferred_element_type=F32)
                    dka_ref[...] += kfac * lax.dot_general(pmi, qt, TN, preferred_element_type=F32)
                dqa_ref[blk, :] = dqi
                srow = lax.broadcasted_iota(jnp.int32, (sub, RNN_HD), 0)
                dki = jnp.zeros((sub, RNN_HD), F32)
                dvi = jnp.zeros((sub, RNN_HD), F32)
                for tq in range(sub):
                    qt, dot_ = qi[tq:tq + 1, :], doi[tq:tq + 1, :]
                    e = jnp.where(srow <= tq, jnp.exp(bi[tq:tq + 1, :] - bi), 0.0)
                    ke = ki * e
                    p = jnp.sum(vi * dot_, axis=1, keepdims=True)
                    a = jnp.sum(ke * qt, axis=1, keepdims=True)
                    dki = dki + p * (qt * e)
                    dvi = dvi + a * dot_
                    row = slice(sub * i + tq, sub * i + tq + 1)
                    dqa_ref[row, :] += jnp.sum(p * ke, axis=0, keepdims=True)
                dka_ref[blk, :] += dki
                dva_ref[blk, :] += dvi

            dq = dqa_ref[...]
            dk = dka_ref[...]
            lastrow = lax.broadcasted_iota(jnp.int32, (CHUNK, RNN_HD), 0) == CHUNK - 1
            dlf = _rev_cumsum_rows(q * dq - k * dk + jnp.where(lastrow, d_blast, 0.0))
            dff = dlf / f - dk
            df_ref[rows, :] = (dff * (1.0 - lb) * sg * (1.0 - sg)).astype(BF16)
            dlb_ref[...] += jnp.sum(dff * (1.0 - sg), axis=0, keepdims=True)
            dq_ref[rows, :] = (dq * (sq * (1.0 + qr * (1.0 - sq)))).astype(BF16)
            di_ref[rows, :] = dva_ref[...].astype(BF16)
            return dst_next

        dst = dst_ref[...]
        for ci in reversed(range(nch)):
            dst = chunk(ci, dst)
        dst_ref[...] = dst

    def col(base):
        return pl.BlockSpec((tb, RNN_HD), lambda h, c: (ntb - 1 - c, base + h))

    outc = pl.BlockSpec((tb, RNN_HD), lambda h, c: (ntb - 1 - c, h))
    hb = ATTN_W // RNN_HD
    res, xres = _call(
        body, name=name, grid=(N_RNN, ntb),
        in_specs=[col(qb), col(fb), col(ib), col(gb),
                  pl.BlockSpec((2, RNN_HD), lambda h, c: (0, h)), pl.BlockSpec((1, RNN_HD), lambda h, c: (0, 0)),
                  outc,
                  pl.BlockSpec((None, nch, RNN_HD, RNN_HD), lambda h, c: (h, ntb - 1 - c, 0, 0)),
                  pl.BlockSpec((tb, RNN_HD), lambda h, c: (ntb - 1 - c, hb + h))],
        out_specs=[outc, outc, outc, outc,
                   pl.BlockSpec((1, RNN_HD), lambda h, c: (0, h)),
                   pl.BlockSpec((None, 1, RNN_HD), lambda h, c: (h, 0, 0))],
        out_shape=[jax.ShapeDtypeStruct((t, RNN_W), BF16)] * 4
        + [jax.ShapeDtypeStruct((1, RNN_W), F32), jax.ShapeDtypeStruct((N_RNN, 1, RNN_HD), F32)],
        scratch_shapes=[pltpu.VMEM((RNN_HD, RNN_HD), F32),
                        pltpu.VMEM((nch, CHUNK, RNN_HD), F32), pltpu.VMEM((nch, CHUNK, RNN_HD), F32),
                        pltpu.VMEM((nch, CHUNK, RNN_HD), F32)],
        args=[proj, proj, proj, proj, lb_logits, norm_gain, o_pre, s0, dcat],
        semantics=("parallel", "arbitrary"), exchanges=exchanges)
    return (*res, xres) if exchanges else res


def _cast_slots(w, where, *, name):
    _, rows, cols = w.shape
    rh = rows // 2
    tr = _row_tile(rh, cols)
    nh = rh // tr

    def body(wh_ref, w_ref, o_ref):
        o_ref[...] = w_ref[...].astype(BF16)

    return pl.pallas_call(
        body, name=name,
        grid_spec=pltpu.PrefetchScalarGridSpec(
            num_scalar_prefetch=1, grid=(2, nh),
            in_specs=[pl.BlockSpec((None, tr, cols), lambda h, i, wh: (0, h * nh + i, 0))],
            out_specs=pl.BlockSpec((None, tr, cols), lambda h, i, wh: (2 * wh[0] + h, i, 0))),
        out_shape=jax.ShapeDtypeStruct((8, rh, cols), BF16),
        compiler_params=_params(("parallel", "parallel")),
    )(where, w)


def _row_tile(rows, cols, budget=1 << 20):
    tr = rows
    while tr * cols > budget and tr % 16 == 0:
        tr //= 2
    return tr


def _half_spec(g, tr, halves_last, slab):
    if halves_last:
        return pl.BlockSpec((None, tr, g.shape[2] // 2), lambda *a: (slab(*a), a[-2], a[-1][1]))
    return pl.BlockSpec((None, None, tr, g.shape[3]), lambda *a: (slab(*a), a[-1][1], a[-2], 0))


def _pair_sum(g, sib, where, *, name, halves_last=False):
    rh, cols = sib.shape[1:]
    tr = _row_tile(rh, cols)

    def body(w_ref, g_ref, s_ref, o_ref):
        o_ref[...] = (g_ref[...] + s_ref[...]).astype(BF16)

    def foreign(s, i, w):
        return (w[0] + 1 + s) % N_CHIPS

    return pl.pallas_call(
        body, name=name,
        grid_spec=pltpu.PrefetchScalarGridSpec(
            num_scalar_prefetch=1, grid=(N_CHIPS - 1, rh // tr),
            in_specs=[_half_spec(g, tr, halves_last, foreign),
                      pl.BlockSpec((None, tr, cols), lambda s, i, w: (foreign(s, i, w), i, 0))],
            out_specs=pl.BlockSpec((None, tr, cols), lambda s, i, w: (foreign(s, i, w), i, 0))),
        out_shape=jax.ShapeDtypeStruct((4, rh, cols), BF16),
        compiler_params=_params(("parallel", "parallel")),
    )(where, g, sib)


def _final_half(g, sib, recv, where, *, name, halves_last=False):
    rh, cols = sib.shape[1:]
    tr = _row_tile(rh, cols)

    def body(w_ref, g_ref, s_ref, r_ref, o_ref):
        acc = g_ref[...] + s_ref[...]
        for j in range(3):
            acc = acc + r_ref[j].astype(F32)
        o_ref[...] = acc

    return pl.pallas_call(
        body, name=name,
        grid_spec=pltpu.PrefetchScalarGridSpec(
            num_scalar_prefetch=1, grid=(rh // tr,),
            in_specs=[_half_spec(g, tr, halves_last, lambda i, w: w[0]),
                      pl.BlockSpec((None, tr, cols), lambda i, w: (w[0], i, 0)),
                      pl.BlockSpec((3, tr, cols), lambda i, w: (0, i, 0))],
            out_specs=pl.BlockSpec((tr, cols), lambda i, w: (i, 0))),
        out_shape=jax.ShapeDtypeStruct((rh, cols), F32),
        compiler_params=_params(("parallel",)),
    )(where, g, sib, recv)


def _adamw_math(w, g, m, v):
    m = ADAM_B1 * m + (1.0 - ADAM_B1) * g
    v = ADAM_B2 * v + (1.0 - ADAM_B2) * (g * g)
    m_hat = m / (1.0 - ADAM_B1 ** ADAM_STEP)
    v_hat = v / (1.0 - ADAM_B2 ** ADAM_STEP)
    delta = -ADAM_LR * (m_hat / (jnp.sqrt(v_hat) + ADAM_EPS) + ADAM_WD * w)
    return delta, m, v


def _adamw(w, mine, theirs, m, v, where, *, name, halves_last=False):
    _, rows, cols = w.shape
    if halves_last:
        cols //= 2
        tr = _row_tile(rows, cols, budget=1 << 19)
        grid = (rows // tr, 2)
        blk = pl.BlockSpec((None, tr, cols), lambda i, h, wh: (0, i, h))
        mine_spec = theirs_spec = pl.BlockSpec((tr, cols), lambda i, h, wh: (i, 0))
        which = lambda: pl.program_id(1)
    else:
        tr = _row_tile(rows // 2, cols, budget=1 << 19)
        nh = rows // 2 // tr
        grid = (rows // tr,)
        blk = pl.BlockSpec((None, tr, cols), lambda i, wh: (0, i, 0))
        mine_spec = pl.BlockSpec((tr, cols), lambda i, wh: (jnp.where(i // nh == wh[1], i % nh, 0), 0))
        theirs_spec = pl.BlockSpec((tr, cols), lambda i, wh: (jnp.where(i // nh == wh[1], 0, i % nh), 0))
        which = lambda: pl.program_id(0) // nh

    def body(wh_ref, w_ref, a_ref, b_ref, m_ref, v_ref, g_ref, d_ref, nm_ref, nv_ref):
        g = jnp.where(which() == wh_ref[1], a_ref[...], b_ref[...])
        d, nm, nv = _adamw_math(w_ref[...], g, m_ref[...], v_ref[...])
        g_ref[...] = g
        d_ref[...] = d
        nm_ref[...] = nm
        nv_ref[...] = nv

    rows, cols = w.shape[1:]
    return pl.pallas_call(
        body, name=name,
        grid_spec=pltpu.PrefetchScalarGridSpec(
            num_scalar_prefetch=1, grid=grid,
            in_specs=[blk, mine_spec, theirs_spec, blk, blk], out_specs=[blk] * 4),
        out_shape=[jax.ShapeDtypeStruct((1, rows, cols), F32)] * 4,
        compiler_params=_params(("parallel",) * len(grid)),
    )(where, w, mine, theirs, m, v)


SEG_LOSS = 0
SEG_SINK = 128
SEG_AGAIN = 256
SEG_L0 = SEG_AGAIN + ATTN_W
SEG_L1 = SEG_L0 + RNN_W
SEG_RGAIN = SEG_L1 + RNN_W
SEG_G = SEG_RGAIN + 128
N_PACK = SEG_G + 4 * D_MODEL


def _pack(sinks, again, l0, l1, rgain, gains, loss=None):
    z = lambda k: jnp.zeros((1, k), F32)
    first = z(128) if loss is None else loss
    return jnp.concatenate([first, sinks, z(128 - N_Q), again, l0, l1, rgain] + list(gains), axis=1)


def _small_reduce_adamw(part, w, m, v, *, name):
    def body(p_ref, w_ref, m_ref, v_ref, g_ref, d_ref, nm_ref, nv_ref, buf_ref, send_sems, recv_sems):
        x, y, c = _place()
        me = 4 * x + 2 * y + c
        copies = []
        for k in range(1, 8):
            dx, dy, dc = (k >> 2) & 1, (k >> 1) & 1, k & 1
            to = (x ^ dx, y ^ dy, c ^ dc)
            cp = pltpu.make_async_remote_copy(
                src_ref=p_ref, dst_ref=buf_ref.at[me],
                send_sem=send_sems.at[k - 1], recv_sem=recv_sems.at[k - 1],
                device_id=to, device_id_type=MESH)
            cp.start()
            copies.append(cp)
        buf_ref[me] = p_ref[...]
        for cp in copies:
            cp.wait()
        tot = buf_ref[0]
        for j in range(1, 8):
            tot = tot + buf_ref[j]
        g_ref[...] = tot
        l0 = w_ref[:, SEG_L0:SEG_L0 + RNN_W]
        l1 = w_ref[:, SEG_L1:SEG_L1 + RNN_W]
        mx = jnp.maximum(l0, l1)
        e0 = jnp.exp(l0 - mx)
        e1 = jnp.exp(l1 - mx)
        lb = e0 / (e0 + e1)
        gl0 = tot[:, SEG_L0:SEG_L0 + RNN_W] * lb * (1.0 - lb)
        g_ref[:, SEG_L0:SEG_L0 + RNN_W] = gl0
        g_ref[:, SEG_L1:SEG_L1 + RNN_W] = -gl0
        d, nm, nv = _adamw_math(w_ref[...], g_ref[...], m_ref[...], v_ref[...])
        d_ref[...] = d
        nm_ref[...] = nm
        nv_ref[...] = nv

    vm = pl.BlockSpec(memory_space=pltpu.VMEM)
    return pl.pallas_call(
        body, name=name,
        in_specs=[vm] * 4, out_specs=[vm] * 4,
        out_shape=[jax.ShapeDtypeStruct((1, N_PACK), F32)] * 4,
        scratch_shapes=[pltpu.VMEM((8, 1, N_PACK), F32), pltpu.SemaphoreType.DMA((7,)),
                        pltpu.SemaphoreType.DMA((7,))],
    )(part, w, m, v)


def _layer_grads(xs, tgt, bufs, where, sinks, again, lb_logits, rgain,
                 g_mix_pre, g_mix_post, g_mlp_pre, g_mlp_post):
    tm = 512
    b_in, b_out, b_up, b_dn = bufs

    shard = IN_W // N_CHIPS
    h1, b_in = _rms_cast_gather(xs, g_mix_pre, b_in, tm=tm, name="h1_norm_gather_w_in")
    w_in_t = b_in.reshape(IN_W, D_MODEL)
    proj, ((b_out, b_up),) = _mm(
        h1, w_in_t, tm=1024, tn=768, tk=D_MODEL, out_dtype=F32, w_layout="nk", name="in_proj",
        exchanges=[_x_gather([b_out, b_up], ici=[(0, 256), (0, 336)])])
    attn, lse, ((b_out, b_up),) = _swa_fwd(
        proj, sinks, name="swa_fwd",
        exchanges=[_x_gather([b_out, b_up], ici=[None, (336, 320)], d2d=[(0, 256), None])])
    w_out = b_out.reshape(D_MODEL, D_MODEL)
    o_pre, rnn, s0, ((b_up, b_dn),) = _hgrn_fwd(
        proj, lb_logits, rgain, tb=512, name="hgrn_fwd",
        exchanges=[_x_gather([b_up, b_dn], ici=[(656, 368), (0, 400)])])
    cat = _mix_cat(attn, rnn, again, tm=tm, name="mix_cat")
    mixed, ((b_up, b_dn),) = _mm(
        cat, w_out, tm=1024, tn=1024, tk=D_MODEL, out_dtype=BF16, name="out_proj",
        exchanges=[_x_gather([b_up, b_dn], ici=[None, (400, 240)], d2d=[(0, 1024), (0, 400)])])
    w_up4 = b_up.reshape(N_CHIPS, D_MODEL, D_FF // N_CHIPS)
    x1, h2, ((b_dn,),) = _post_norm_res(
        mixed, g_mix_post, xs, g_mlp_pre, tm=tm, name="mix_post",
        exchanges=[_x_gather([b_dn], d2d=[(400, 240)])])
    u, ((b_dn,),) = _mm(h2, w_up4, tm=1024, tn=1024, tk=D_MODEL, out_dtype=BF16, relu=True, w_layout="skn",
                        name="mlp_up", exchanges=[_x_gather([b_dn], ici=[(640, 384)], cross=[(640, 384)])])
    w_dn = b_dn.reshape(D_FF, D_MODEL)
    yv = _mm(u, w_dn, tm=1024, tn=1024, tk=2048, out_dtype=BF16, a_square=True, name="mlp_down")
    dy, dx2, loss_row, dg_mlp_post = _loss_head(yv, g_mlp_post, x1, tgt, tm=tm, name="loss_head")

    def halved(g):
        return g.reshape(N_CHIPS, 2, g.shape[1] // 2, g.shape[2])
    du = _mm(dy, w_dn, tm=1024, tn=1024, tk=D_MODEL, out_dtype=BF16, mul2=u, w_layout="nk", name="mlp_down_bwd")
    g_dn = halved(_mm_tn(u, dy, tm=1024, tn=1024, tt=2048, a_square=True, name="w_down_grad")
                  .reshape(N_CHIPS, D_FF // N_CHIPS, D_MODEL))
    d_w_up, ((sib_dn,),) = _mm_tn(h2, du, tm=1024, tn=1024, tt=2048, n_split=N_CHIPS, name="w_up_grad",
                                  exchanges=[_x_pair([g_dn])])
    g_up = halved(d_w_up)
    wire_dn = _pair_sum(g_dn, sib_dn, where, name="pair_sum_w_down")
    dh2, ((recv_dn,), (sib_up,)) = _mm(du, w_up4, tm=1024, tn=1024, tk=2048, out_dtype=BF16, w_layout="snk", name="mlp_up_bwd",
                                       exchanges=[_x_chip([wire_dn], rows=[(0, 928)]), _x_pair([g_up])])
    wire_up = _pair_sum(g_up, sib_up, where, name="pair_sum_w_up")
    dx1, dg_mlp_pre = _rms_bwd(dh2, x1, g_mlp_pre, dx2, tm=tm, out_dtype=BF16, name="mlp_pre_bwd")
    dmixed, dg_mix_post = _rms_bwd(dx1, mixed, g_mix_post, None, tm=tm, out_dtype=BF16, name="mix_post_bwd")
    d_w_out, ((recv_dn,),) = _mm_tn(cat, dmixed, tm=1024, tn=1024, tt=2048, name="w_out_grad",
                                    exchanges=[_x_chip([wire_dn], rows=[(928, 96)], into=[recv_dn])])
    fin_dn = _final_half(g_dn, sib_dn, recv_dn, where, name="final_half_w_down")
    g_out = halved(d_w_out.reshape(N_CHIPS, D_MODEL // N_CHIPS, D_MODEL))
    dcat, ((sib_out,), (oth_dn,)) = _mm(dmixed, w_out, tm=1024, tn=1024, tk=D_MODEL, out_dtype=BF16, w_layout="nk",
                                        name="out_proj_bwd", exchanges=[_x_pair([g_out]), _x_share([fin_dn])])
    wire_out = _pair_sum(g_out, sib_out, where, name="pair_sum_w_out")
    dattn, dg_again = _rms_bwd(dcat, attn, again, None, tm=tm, out_dtype=BF16, name="attn_norm_bwd")
    dq_a, dkv, dsinks, ((recv_up,),) = _swa_bwd(
        proj, sinks, dattn, lse, name="swa_bwd", exchanges=[_x_chip([wire_up], rows=[(0, 512)])])
    dq_r, df_r, di_r, dg_r, dlb, dgain_h, ((recv_up,), (recv_out,)) = _hgrn_bwd(
        proj, lb_logits, rgain, o_pre, s0, dcat, tb=512, name="hgrn_bwd",
        exchanges=[_x_chip([wire_up], rows=[(512, 512)], into=[recv_up]), _x_chip([wire_out])])
    fin_up = _final_half(g_up, sib_up, recv_up, where, name="final_half_w_up")
    fin_out = _final_half(g_out, sib_out, recv_out, where, name="final_half_w_out")
    dproj = jnp.concatenate([dq_a, dkv, dq_r, df_r, di_r, dg_r], axis=1)
    piece_cols = D_MODEL // 4

    def w_in_piece(pc, exchanges):
        d, xres = _mm_tn(dproj, h1, tm=896, tn=2 * piece_cols, tt=2048, b_blocks=(pc, pc + 2),
                         name="w_in_grad_%d" % pc, exchanges=exchanges)
        return d.reshape(N_CHIPS, shard, 2 * piece_cols), xres

    g_in0, ((oth_up, oth_out),) = w_in_piece(0, [_x_share([fin_up, fin_out])])
    g_in1, ((sib_in0,),) = w_in_piece(1, [_x_pair([g_in0], halves_last=True)])
    wire_in0 = _pair_sum(g_in0, sib_in0, where, name="pair_sum_w_in_0", halves_last=True)
    dh1, ((recv_in0,), (sib_in1,)) = _mm(
        dproj, w_in_t, tm=1024, tn=1024, tk=2688, out_dtype=BF16, m_blocks=(0, 2), name="in_proj_bwd_0",
        exchanges=[_x_chip([wire_in0]), _x_pair([g_in1], halves_last=True)])
    wire_in1 = _pair_sum(g_in1, sib_in1, where, name="pair_sum_w_in_1", halves_last=True)
    dh1, ((recv_in1,),) = _mm(
        dproj, w_in_t, tm=1024, tn=1024, tk=2688, out_dtype=BF16, m_blocks=(2, 2), out_into=dh1,
        name="in_proj_bwd_1", exchanges=[_x_chip([wire_in1])])
    gx, dg_mix_pre = _rms_bwd(dh1, xs, g_mix_pre, dx1, tm=tm, out_dtype=F32, name="mix_pre_bwd")
    fin_in0 = _final_half(g_in0, sib_in0, recv_in0, where, name="final_half_w_in_0", halves_last=True)
    fin_in1 = _final_half(g_in1, sib_in1, recv_in1, where, name="final_half_w_in_1", halves_last=True)
    oth_in0, oth_in1 = _run_exchange(_x_share([fin_in0, fin_in1]), name="share_w_in")
    fin_in = jnp.concatenate([fin_in0, fin_in1], axis=1)
    oth_in = jnp.concatenate([oth_in0, oth_in1], axis=1)

    big = [(fin_in, oth_in), (fin_out, oth_out), (fin_up, oth_up), (fin_dn, oth_dn)]
    drgain = jnp.sum(dgain_h, axis=0)
    small = _pack(jnp.sum(dsinks, axis=1)[None, :], dg_again, dlb, jnp.zeros_like(dlb), drgain,
                  [dg_mix_pre, dg_mix_post, dg_mlp_pre, dg_mlp_post], loss=loss_row)
    return gx, big, small


def kernel(x, w_in, attn_sinks, attn_out_gain, rnn_lb_logits, rnn_norm_gain, w_out, mix_pre_gain, mix_post_gain, mlp_pre_gain, mlp_post_gain, w_up, w_down, loss_target, m_w_in, m_attn_sinks, m_attn_out_gain, m_rnn_lb_logits, m_rnn_norm_gain, m_w_out, m_mix_pre_gain, m_mix_post_gain, m_mlp_pre_gain, m_mlp_post_gain, m_w_up, m_w_down, v_w_in, v_attn_sinks, v_attn_out_gain, v_rnn_lb_logits, v_rnn_norm_gain, v_w_out, v_mix_pre_gain, v_mix_post_gain, v_mlp_pre_gain, v_mlp_post_gain, v_w_up, v_w_down):
    ax, ay, ac = _place()
    where = jnp.stack([2 * ax + ay, ac]).astype(jnp.int32)
    t = lambda a: jnp.swapaxes(a, 1, 2)
    big_w = [t(w_in), w_out, w_up, w_down]
    big_m = [t(m_w_in), m_w_out, m_w_up, m_w_down]
    big_v = [t(v_w_in), v_w_out, v_w_up, v_w_down]

    names = ["w_in", "w_out", "w_up", "w_down"]
    bufs = [_cast_slots(w, where, name="cast_" + nm) for w, nm in zip(big_w, names)]
    gx, big_g, small_part = _layer_grads(
        x[0], loss_target[0], bufs, where, attn_sinks, attn_out_gain, rnn_lb_logits, rnn_norm_gain,
        mix_pre_gain, mix_post_gain, mlp_pre_gain, mlp_post_gain)

    grads, deltas, new_m, new_v = [], [], [], []
    for (f, o), w, m, v, nm in zip(big_g, big_w, big_m, big_v, names):
        res = _adamw(w, f, o, m, v, where, name="adamw_" + nm, halves_last=(nm == "w_in"))
        if nm == "w_in":
            res = [t(r) for r in res]
        g, d, nm_, nv_ = res
        grads.append(g)
        deltas.append(d)
        new_m.append(nm_)
        new_v.append(nv_)

    def pack_params(sinks, again, logits, rgain, gains):
        return _pack(sinks, again, logits[0:1], logits[1:2], rgain, gains)

    pw = pack_params(attn_sinks, attn_out_gain, rnn_lb_logits, rnn_norm_gain,
                     [mix_pre_gain, mix_post_gain, mlp_pre_gain, mlp_post_gain])
    pm = pack_params(m_attn_sinks, m_attn_out_gain, m_rnn_lb_logits, m_rnn_norm_gain,
                     [m_mix_pre_gain, m_mix_post_gain, m_mlp_pre_gain, m_mlp_post_gain])
    pv = pack_params(v_attn_sinks, v_attn_out_gain, v_rnn_lb_logits, v_rnn_norm_gain,
                     [v_mix_pre_gain, v_mix_post_gain, v_mlp_pre_gain, v_mlp_post_gain])
    packs = _small_reduce_adamw(small_part, pw, pm, pv, name="small_reduce_adamw")

    def unpack(p):
        seg = lambda o, k: p[:, o:o + k]
        logits = jnp.concatenate([seg(SEG_L0, RNN_W), seg(SEG_L1, RNN_W)], axis=0)
        gains = [seg(SEG_G + i * D_MODEL, D_MODEL) for i in range(4)]
        return dict(sinks=seg(SEG_SINK, N_Q), again=seg(SEG_AGAIN, ATTN_W), logits=logits,
                    rgain=seg(SEG_RGAIN, RNN_HD), gains=gains)

    def order(small, big):
        return [big[0], small["sinks"], small["again"], small["logits"], small["rgain"], big[1],
                *small["gains"], big[2], big[3]]

    loss = packs[0][0, 0]
    outs = [loss, gx[None]]
    for p, b in zip(packs, [grads, deltas, new_m, new_v]):
        outs += order(unpack(p), b)
    return tuple(outs)
```

```python
import functools

import jax
import jax.numpy as jnp
from jax import lax
from jax.experimental import pallas as pl
from jax.experimental.pallas import tpu as pltpu

F32 = jnp.float32
BF16 = jnp.bfloat16
MESH = pl.DeviceIdType.MESH

EPS = 1e-6
D_MODEL = 2048
ATTN_W = 1024
HEAD_DIM = 64
N_Q = 16
N_KV = 2
GROUP = 8
BLK = 128
RNN_W = 1024
RNN_HD = 128
N_RNN = 8
CHUNK = 64
SUB_FWD = 16
SUB_BWD = 8
D_FF = 8192
IN_W = 5376
N_CHIPS = 4
KV_COL = ATTN_W
QR_COL = ATTN_W + 2 * 128
FR_COL = QR_COL + RNN_W
IR_COL = FR_COL + RNN_W
GR_COL = IR_COL + RNN_W

ADAM_LR = 0.001
ADAM_B1 = 0.9
ADAM_B2 = 0.999
ADAM_EPS = 1e-08
ADAM_WD = 0.01
ADAM_STEP = 10

VMEM_LIMIT = 48 * 1024 * 1024

NT = (((1,), (1,)), ((), ()))
TN = (((0,), (0,)), ((), ()))


def _params(sem=None):
    return pltpu.CompilerParams(dimension_semantics=sem, vmem_limit_bytes=VMEM_LIMIT)


def _sigmoid(x):
    return 1.0 / (1.0 + jnp.exp(-x))


ANY = pl.BlockSpec(memory_space=pl.ANY)


def _place():
    return lax.axis_index("x"), lax.axis_index("y"), lax.axis_index("c")


def _other_chips(x, y):
    return [(1 - x, y), (x, 1 - y), (1 - x, 1 - y)]


class _Exchange:
    def __init__(self, srcs, outs, ncopy, build, aliases=None):
        self.srcs, self.outs, self.ncopy, self.build = list(srcs), list(outs), ncopy, build
        self.aliases = aliases or {}


def _remote(src, dst, send_sems, recv_sems, k, to):
    return pltpu.make_async_remote_copy(src_ref=src, dst_ref=dst, send_sem=send_sems.at[k],
                                        recv_sem=recv_sems.at[k], device_id=to, device_id_type=MESH)


def _call(body, *, name, grid, in_specs, out_specs, out_shape, args, scratch_shapes=(), semantics=None,
          exchanges=(), into=None):
    in_specs, out_specs, out_shape = list(in_specs), list(out_specs), list(out_shape)
    scratch_shapes = list(scratch_shapes)
    ni, no, ns = len(in_specs), len(out_specs), len(scratch_shapes)
    xsrc = [s for x in exchanges for s in x.srcs]
    xout = [o for x in exchanges for o in x.outs]
    into = into or {}
    xsrc += [into[k] for k in sorted(into)]
    nxi, nxo = len(xsrc), len(xout)
    aliases = {nxi - len(into) + ni + q: k for q, k in enumerate(sorted(into))}
    a0 = b0 = 0
    for x in exchanges:
        for si, oi in x.aliases.items():
            aliases[ni + a0 + si] = no + b0 + oi
        a0 += len(x.srcs)
        b0 += len(x.outs)
    sems = []
    for x in exchanges:
        sems += [pltpu.SemaphoreType.DMA((x.ncopy,)), pltpu.SemaphoreType.DMA((x.ncopy,))]

    def wrapped(*refs):
        ins, xi = refs[:ni], refs[ni:ni + nxi]
        outs, xo = refs[ni + nxi:ni + nxi + no], refs[ni + nxi + no:ni + nxi + no + nxo]
        rest = refs[ni + nxi + no + nxo:]
        scr, sm = rest[:ns], rest[ns:]

        def copies():
            cps = []
            a = b = 0
            for k, x in enumerate(exchanges):
                cps += x.build(xi[a:a + len(x.srcs)], xo[b:b + len(x.outs)], sm[2 * k], sm[2 * k + 1])
                a += len(x.srcs)
                b += len(x.outs)
            return cps

        def start():
            for cp in copies():
                cp.start()

        def wait():
            for cp in copies():
                cp.wait()

        if not exchanges:
            body(*ins, *outs, *scr)
        elif not grid:
            start()
            body(*ins, *outs, *scr)
            wait()
        else:
            first = last = None
            for ax, g in enumerate(grid):
                f = pl.program_id(ax) == 0
                l = pl.program_id(ax) == g - 1
                first = f if first is None else first & f
                last = l if last is None else last & l
            pl.when(first)(start)
            body(*ins, *outs, *scr)
            pl.when(last)(wait)

    if exchanges and semantics is not None:
        semantics = ("arbitrary",) * len(grid)
    kwargs = dict(grid=grid) if grid else {}
    res = pl.pallas_call(
        wrapped, name=name,
        in_specs=in_specs + [ANY] * nxi, out_specs=out_specs + [ANY] * nxo,
        out_shape=out_shape + xout, scratch_shapes=scratch_shapes + sems,
        input_output_aliases=aliases,
        compiler_params=_params(semantics), **kwargs,
    )(*args, *xsrc)
    res = list(res)
    mine, theirs = res[:no], res[no:]
    per = []
    b = 0
    for x in exchanges:
        per.append(theirs[b:b + len(x.outs)])
        b += len(x.outs)
    return mine, per


def _run_exchange(x, *, name):
    return _call(lambda: None, name=name, grid=(), in_specs=[], out_specs=[], out_shape=[], args=[],
                 exchanges=[x])[1][0]


def _x_gather(bufs, ici=None, d2d=None, cross=None):
    n = len(bufs)
    plan = [(a, kind, rows[a]) for a in range(n) for kind, rows in (("ici", ici), ("d2d", d2d), ("cross", cross))
            if rows is not None and rows[a] is not None]

    def build(srcs, outs, ss, rs):
        x, y, c = _place()
        cps = []
        for q, (a, kind, rows) in enumerate(plan):
            piece = pl.ds(*rows)
            for j, (px, py) in enumerate(_other_chips(x, y)):
                if kind == "d2d":
                    slot, to = 4 * px + 2 * py + c, (x, y, 1 - c)
                else:
                    slot, to = 4 * x + 2 * y + c, (px, py, c if kind == "ici" else 1 - c)
                cps.append(_remote(srcs[a].at[slot, piece], outs[a].at[slot, piece], ss, rs, 3 * q + j, to))
        return cps

    outs = [jax.ShapeDtypeStruct(b.shape, b.dtype) for b in bufs]
    return _Exchange(bufs, outs, 3 * len(plan), build, aliases={a: a for a in range(n)})


def _x_pair(grads, halves_last=False):
    n = len(grads)

    def build(srcs, outs, ss, rs):
        x, y, c = _place()

        def half(r):
            if not halves_last:
                return r.at[:, 1 - c]
            ch = r.shape[2] // 2
            return r.at[:, :, pl.ds(pl.multiple_of((1 - c) * ch, 128), ch)]

        return [_remote(half(srcs[a]), outs[a], ss, rs, a, (x, y, 1 - c)) for a in range(n)]

    if halves_last:
        outs = [jax.ShapeDtypeStruct(g.shape[:2] + (g.shape[2] // 2,), g.dtype) for g in grads]
    else:
        outs = [jax.ShapeDtypeStruct((4,) + g.shape[2:], g.dtype) for g in grads]
    return _Exchange(grads, outs, n, build)


def _x_chip(wires, rows=None, into=None):
    n = len(wires)
    rows = rows or [(0, w.shape[1]) for w in wires]

    def build(srcs, outs, ss, rs):
        x, y, c = _place()
        cps = []
        for a in range(n):
            piece = pl.ds(*rows[a])
            for j, (px, py) in enumerate(_other_chips(x, y)):
                cps.append(_remote(srcs[a].at[2 * px + py, piece], outs[a].at[j, piece], ss, rs,
                                   3 * a + j, (px, py, c)))
        return cps

    outs = [jax.ShapeDtypeStruct((3,) + w.shape[1:], w.dtype) for w in wires]
    if into is None:
        return _Exchange(wires, outs, 3 * n, build)
    return _Exchange(list(wires) + list(into), outs, 3 * n, build, aliases={n + a: a for a in range(n)})


def _x_share(halves):
    n = len(halves)

    def build(srcs, outs, ss, rs):
        x, y, c = _place()
        return [_remote(srcs[a], outs[a], ss, rs, a, (x, y, 1 - c)) for a in range(n)]

    outs = [jax.ShapeDtypeStruct(h.shape, h.dtype) for h in halves]
    return _Exchange(halves, outs, n, build)


def _mm(a, w, *, tm, tn, tk, out_dtype, name, a_square=False, relu=False, mul2=None, w_layout="kn",
        m_blocks=None, out_into=None, exchanges=()):
    m, k = a.shape
    m_first, m_count = m_blocks or (0, m // tm)
    a_spec = pl.BlockSpec((tm, tk), lambda i, j, kk: (i + m_first, kk))
    if w_layout == "kn":
        n = w.shape[1]
        w_spec = pl.BlockSpec((tk, tn), lambda i, j, kk: (kk, j))
    elif w_layout == "nk":
        n = w.shape[0]
        w_spec = pl.BlockSpec((tn, tk), lambda i, j, kk: (j, kk))
    elif w_layout == "skn":
        n = w.shape[0] * w.shape[2]
        per_n = w.shape[2] // tn
        w_spec = pl.BlockSpec((None, tk, tn), lambda i, j, kk: (j // per_n, kk, j % per_n))
    else:
        assert w_layout == "snk"
        n = w.shape[1]
        per_k = w.shape[2] // tk
        w_spec = pl.BlockSpec((None, tn, tk), lambda i, j, kk: (kk // per_k, j, kk % per_k))
    w_dims = NT if w_layout in ("nk", "snk") else (((1,), (0,)), ((), ()))
    nk = k // tk
    assert m % tm == 0 and n % tn == 0 and k % tk == 0

    def body(*refs):
        if mul2 is not None:
            a_ref, w_ref, e_ref, o_ref, acc_ref = refs
        else:
            a_ref, w_ref, o_ref, acc_ref = refs
            e_ref = None
        kk = pl.program_id(2)
        av = a_ref[...]
        if a_square:
            af = av.astype(F32)
            av = (af * af).astype(BF16)
        part = lax.dot_general(av, w_ref[...], w_dims, preferred_element_type=F32)

        def finish(r):
            if relu:
                r = jnp.maximum(r, 0.0)
            if e_ref is not None:
                r = 2.0 * e_ref[...].astype(F32) * r
            o_ref[...] = r.astype(out_dtype)

        if nk == 1:
            finish(part)
        else:
            @pl.when(kk == 0)
            def _():
                acc_ref[...] = part

            @pl.when(kk > 0)
            def _():
                acc_ref[...] += part

            @pl.when(kk == nk - 1)
            def _():
                finish(acc_ref[...])

    in_specs = [a_spec, w_spec]
    args = [a, w]
    if mul2 is not None:
        in_specs.append(pl.BlockSpec((tm, tn), lambda i, j, kk: (i + m_first, j)))
        args.append(mul2)
    acc_shape = (tm, tn) if nk > 1 else (8, 128)
    (out,), per = _call(
        body, name=name, grid=(m_count, n // tn, nk),
        in_specs=in_specs, out_specs=[pl.BlockSpec((tm, tn), lambda i, j, kk: (i + m_first, j))],
        out_shape=[jax.ShapeDtypeStruct((m, n), out_dtype)], args=args,
        scratch_shapes=[pltpu.VMEM(acc_shape, F32)],
        semantics=("parallel", "parallel", "arbitrary"), exchanges=exchanges,
        into=None if out_into is None else {0: out_into})
    return (out, per) if exchanges else out


def _mm_tn(a, b, *, tm, tn, tt, name, a_square=False, n_split=1, b_blocks=None, exchanges=()):
    t, m = a.shape
    nb = len(b_blocks) if b_blocks else 1
    n = tn if b_blocks else b.shape[1]
    assert t % tt == 0 and m % tm == 0 and n % tn == 0 and (n // n_split) % tn == 0
    per = n // n_split // tn

    def body(a_ref, *refs):
        b_refs, o_ref = refs[:nb], refs[nb]
        ti = pl.program_id(2)
        av = a_ref[...]
        if a_square:
            af = av.astype(F32)
            av = (af * af).astype(BF16)
        bv = b_refs[0][...] if nb == 1 else jnp.concatenate([r[...] for r in b_refs], axis=1)
        part = lax.dot_general(av, bv, TN, preferred_element_type=F32)

        @pl.when(ti == 0)
        def _():
            o_ref[...] = part

        @pl.when(ti > 0)
        def _():
            o_ref[...] += part

    if b_blocks:
        b_specs = [pl.BlockSpec((tt, tn // nb), functools.partial(lambda blk, i, j, ti: (ti, blk), blk))
                   for blk in b_blocks]
    else:
        b_specs = [pl.BlockSpec((tt, tn), lambda i, j, ti: (ti, j))]
    (out,), xres = _call(
        body, name=name, grid=(m // tm, n // tn, t // tt),
        in_specs=[pl.BlockSpec((tt, tm), lambda i, j, ti: (ti, i))] + b_specs,
        out_specs=[pl.BlockSpec((None, tm, tn), lambda i, j, ti: (j // per, i, j % per))],
        out_shape=[jax.ShapeDtypeStruct((n_split, m, n // n_split), F32)], args=[a] + [b] * nb,
        semantics=("parallel", "parallel", "arbitrary"), exchanges=exchanges)
    return (out, xres) if exchanges else out


def _rstd(x):
    return lax.rsqrt(jnp.mean(x * x, axis=-1, keepdims=True) + EPS)


def _rms_cast_gather(x, g, buf, *, tm, name):
    t, d = x.shape
    steps = t // tm

    def body(x_ref, g_ref, b_in, o_ref, b_out, send_sems, recv_sems):
        i = pl.program_id(0)
        xc, yc, c = _place()
        chips = _other_chips(xc, yc)

        def slot(px, py, pc):
            return b_out.at[4 * px + 2 * py + pc]

        def sent(j):
            return _remote(b_in.at[4 * xc + 2 * yc + c], slot(xc, yc, c), send_sems, recv_sems, j, (*chips[j], c))

        def passed(j):
            return _remote(slot(*chips[j], c), slot(*chips[j], c), send_sems, recv_sems, 3 + j, (xc, yc, 1 - c))

        @pl.when(i == 0)
        def _():
            for j in range(3):
                sent(j).start()

        xv = x_ref[...]
        o_ref[...] = (xv * _rstd(xv) * g_ref[...]).astype(BF16)

        @pl.when(i == steps - 1)
        def _():
            for j in range(3):
                sent(j).wait_recv()
                passed(j).start()
            for j in range(3):
                passed(j).wait_recv()
                passed(j).wait_send()
                sent(j).wait_send()

    return pl.pallas_call(
        body, name=name, grid=(steps,),
        in_specs=[pl.BlockSpec((tm, d), lambda i: (i, 0)), pl.BlockSpec((1, d), lambda i: (0, 0)), ANY],
        out_specs=[pl.BlockSpec((tm, d), lambda i: (i, 0)), ANY],
        out_shape=[jax.ShapeDtypeStruct((t, d), BF16), jax.ShapeDtypeStruct(buf.shape, buf.dtype)],
        scratch_shapes=[pltpu.SemaphoreType.DMA((6,)), pltpu.SemaphoreType.DMA((6,))],
        input_output_aliases={2: 1},
        compiler_params=_params(("arbitrary",)),
    )(x, g, buf)


def _mix_cat(attn, rnn, gain, *, tm, name):
    t = attn.shape[0]

    def body(a_ref, r_ref, g_ref, o_ref):
        av = a_ref[...].astype(F32)
        o_ref[:, :ATTN_W] = (av * _rstd(av) * g_ref[...]).astype(BF16)
        o_ref[:, ATTN_W:] = r_ref[...].astype(BF16)

    return pl.pallas_call(
        body, name=name, grid=(t // tm,),
        in_specs=[pl.BlockSpec((tm, ATTN_W), lambda i: (i, 0)), pl.BlockSpec((tm, RNN_W), lambda i: (i, 0)),
                  pl.BlockSpec((1, ATTN_W), lambda i: (0, 0))],
        out_specs=pl.BlockSpec((tm, D_MODEL), lambda i: (i, 0)),
        out_shape=jax.ShapeDtypeStruct((t, D_MODEL), BF16),
        compiler_params=_params(("parallel",)),
    )(attn, rnn, gain)


def _post_norm_res(mixed, g_post, res, g_next, *, tm, name, exchanges=()):
    t, d = mixed.shape

    def body(m_ref, gp_ref, r_ref, gn_ref, x1_ref, h2_ref):
        mv = m_ref[...].astype(F32)
        x1 = r_ref[...] + mv * _rstd(mv) * gp_ref[...]
        x1_ref[...] = x1.astype(BF16)
        h2_ref[...] = (x1 * _rstd(x1) * gn_ref[...]).astype(BF16)

    row = pl.BlockSpec((tm, d), lambda i: (i, 0))
    vec = pl.BlockSpec((1, d), lambda i: (0, 0))
    res_, xres = _call(
        body, name=name, grid=(t // tm,),
        in_specs=[row, vec, row, vec], out_specs=[row, row],
        out_shape=[jax.ShapeDtypeStruct((t, d), BF16), jax.ShapeDtypeStruct((t, d), BF16)],
        args=[mixed, g_post, res, g_next], semantics=("parallel",), exchanges=exchanges)
    return (*res_, xres) if exchanges else res_


def _rms_bwd(dyn, xin, g, res, *, tm, out_dtype, name, col_block=0, exchanges=()):
    t, d = xin.shape

    def body(*refs):
        if res is not None:
            dy_ref, x_ref, g_ref, r_ref, dx_ref, dg_ref = refs
        else:
            dy_ref, x_ref, g_ref, dx_ref, dg_ref = refs
        i = pl.program_id(0)
        xv = x_ref[...].astype(F32)
        dy = dy_ref[...].astype(F32)
        r = _rstd(xv)
        xh = xv * r
        part = jnp.sum(dy * xh, axis=0, keepdims=True)

        @pl.when(i == 0)
        def _():
            dg_ref[...] = part

        @pl.when(i > 0)
        def _():
            dg_ref[...] += part

        tt = dy * g_ref[...]
        dx = r * (tt - xh * jnp.mean(tt * xh, axis=-1, keepdims=True))
        if res is not None:
            dx = dx + r_ref[...].astype(F32)
        dx_ref[...] = dx.astype(out_dtype)

    row = pl.BlockSpec((tm, d), lambda i: (i, 0))
    vec = pl.BlockSpec((1, d), lambda i: (0, 0))
    in_specs = [pl.BlockSpec((tm, d), lambda i: (i, col_block)), row, vec]
    args = [dyn, xin, g]
    if res is not None:
        in_specs.append(row)
        args.append(res)
    res, xres = _call(
        body, name=name, grid=(t // tm,),
        in_specs=in_specs, out_specs=[row, vec],
        out_shape=[jax.ShapeDtypeStruct((t, d), out_dtype), jax.ShapeDtypeStruct((1, d), F32)], args=args,
        semantics=("arbitrary",), exchanges=exchanges)
    return (*res, xres) if exchanges else res


def _loss_head(y, g_post, x1, target, *, tm, name):
    t, d = y.shape

    def body(y_ref, g_ref, x1_ref, t_ref, dy_ref, dx2_ref, loss_ref, dg_ref):
        i = pl.program_id(0)
        yv = y_ref[...].astype(F32)
        r = _rstd(yv)
        yh = yv * r
        gv = g_ref[...]
        err = x1_ref[...].astype(F32) + yh * gv - t_ref[...]
        lpart = 0.5 * jnp.sum(jnp.mean(err * err, axis=-1, keepdims=True), axis=0, keepdims=True)
        dx2 = err * (1.0 / d)
        dgp = jnp.sum(dx2 * yh, axis=0, keepdims=True)
        lane = lax.broadcasted_iota(jnp.int32, (1, 128), 1)
        lrow = jnp.where(lane == 0, lpart, 0.0)

        @pl.when(i == 0)
        def _():
            dg_ref[...] = dgp
            loss_ref[...] = lrow

        @pl.when(i > 0)
        def _():
            dg_ref[...] += dgp
            loss_ref[...] += lrow

        tt = dx2 * gv
        dy_ref[...] = (r * (tt - yh * jnp.mean(tt * yh, axis=-1, keepdims=True))).astype(BF16)
        dx2_ref[...] = dx2.astype(BF16)

    row = pl.BlockSpec((tm, d), lambda i: (i, 0))
    vec = pl.BlockSpec((1, d), lambda i: (0, 0))
    return pl.pallas_call(
        body, name=name, grid=(t // tm,),
        in_specs=[row, vec, row, row],
        out_specs=[row, row, pl.BlockSpec((1, 128), lambda i: (0, 0)), vec],
        out_shape=[jax.ShapeDtypeStruct((t, d), BF16), jax.ShapeDtypeStruct((t, d), BF16),
                   jax.ShapeDtypeStruct((1, 128), F32), jax.ShapeDtypeStruct((1, d), F32)],
        compiler_params=_params(("arbitrary",)),
    )(y, g_post, x1, target)


def _alibi_slope(h):
    return 2.0 ** (-8.0 * (h + 1) / N_Q)


PAIR = 2 * HEAD_DIM
N_PAIRS = N_Q // 2
PAIRS_PER_KV = GROUP // 2
SMEM = pl.BlockSpec(memory_space=pltpu.SMEM)


def _swa_mask(n):
    key = lax.broadcasted_iota(jnp.int32, (2 * BLK, BLK), 0)
    qry = lax.broadcasted_iota(jnp.int32, (2 * BLK, BLK), 1)
    dist = qry + BLK - key
    valid = (dist >= 0) & (dist < BLK) & ((key >= BLK) | (n > 0))
    return valid, dist.astype(F32)


def _block_diag(kvp_ref, kvc_ref, off):
    a = jnp.concatenate([kvp_ref[:, off:off + HEAD_DIM], kvc_ref[:, off:off + HEAD_DIM]], axis=0).astype(BF16)
    z = jnp.zeros_like(a)
    return jnp.concatenate([jnp.concatenate([a, z], axis=1), jnp.concatenate([z, a], axis=1)], axis=0)


def _swa_scores(s2, e, hh, valid, distf):
    s = s2[2 * BLK * e:2 * BLK * (e + 1)] * (HEAD_DIM ** -0.5) - _alibi_slope(hh) * distf
    return jnp.where(valid, s, -1e30)


def _swa_fwd(proj, sinks, *, name, exchanges=()):
    t = proj.shape[0]
    nb = t // BLK
    kvb = KV_COL // (2 * 128)

    def body(sink_ref, q_ref, kvc_ref, kvp_ref, o_ref, lse_ref):
        n = pl.program_id(0)
        valid, distf = _swa_mask(n)
        for kvh in range(N_KV):
            k2 = _block_diag(kvp_ref, kvc_ref, kvh * HEAD_DIM)
            v2 = _block_diag(kvp_ref, kvc_ref, 128 + kvh * HEAD_DIM)
            for jp in range(PAIRS_PER_KV):
                pair = kvh * PAIRS_PER_KV + jp
                lanes = slice(pair * PAIR, (pair + 1) * PAIR)
                s2 = lax.dot_general(k2, q_ref[:, lanes].astype(BF16), NT, preferred_element_type=F32)
                probs = []
                for e in range(2):
                    hh = 2 * pair + e
                    s = _swa_scores(s2, e, hh, valid, distf)
                    sink = sink_ref[0, hh]
                    mx = jnp.maximum(jnp.max(s, axis=0, keepdims=True), sink)
                    p = jnp.exp(s - mx)
                    l = jnp.sum(p, axis=0, keepdims=True) + jnp.exp(sink - mx)
                    probs.append((p * (1.0 / l)).astype(BF16))
                    lse_ref[hh:hh + 1, :] = mx + jnp.log(l)
                o_ref[:, lanes] = lax.dot_general(jnp.concatenate(probs, axis=0), v2, TN,
                                                  preferred_element_type=F32).astype(BF16)

    res, xres = _call(
        body, name=name, grid=(nb,),
        in_specs=[SMEM,
                  pl.BlockSpec((BLK, ATTN_W), lambda n: (n, 0)),
                  pl.BlockSpec((BLK, 256), lambda n: (n, kvb)),
                  pl.BlockSpec((BLK, 256), lambda n: (jnp.maximum(n - 1, 0), kvb))],
        out_specs=[pl.BlockSpec((BLK, ATTN_W), lambda n: (n, 0)),
                   pl.BlockSpec((None, N_Q, BLK), lambda n: (n, 0, 0))],
        out_shape=[jax.ShapeDtypeStruct((t, ATTN_W), BF16), jax.ShapeDtypeStruct((nb, N_Q, BLK), F32)],
        args=[sinks, proj, proj, proj], semantics=("parallel",), exchanges=exchanges)
    return (*res, xres) if exchanges else res


def _swa_bwd(proj, sinks, dattn, lse, *, name, exchanges=()):
    t = proj.shape[0]
    nb = t // BLK
    kvb = KV_COL // (2 * 128)

    def body(sink_ref, q_ref, kvc_ref, kvp_ref, do_ref, lse_ref, dq_ref, dkv_ref, dsink_ref, carry_ref):
        n = pl.program_id(0)

        @pl.when(n == 0)
        def _():
            dsink_ref[...] = jnp.zeros_like(dsink_ref)
            carry_ref[...] = jnp.zeros_like(carry_ref)

        @pl.when(n < nb)
        def _():
            valid, distf = _swa_mask(n)
            for kvh in range(N_KV):
                k2 = _block_diag(kvp_ref, kvc_ref, kvh * HEAD_DIM)
                v2 = _block_diag(kvp_ref, kvc_ref, 128 + kvh * HEAD_DIM)
                dk2 = jnp.zeros((4 * BLK, PAIR), F32)
                dv2 = jnp.zeros((4 * BLK, PAIR), F32)
                for jp in range(PAIRS_PER_KV):
                    pair = kvh * PAIRS_PER_KV + jp
                    lanes = slice(pair * PAIR, (pair + 1) * PAIR)
                    q2 = q_ref[:, lanes].astype(BF16)
                    do2 = do_ref[:, lanes].astype(BF16)
                    s2 = lax.dot_general(k2, q2, NT, preferred_element_type=F32)
                    dp2 = lax.dot_general(v2, do2, NT, preferred_element_type=F32)
                    probs, dss = [], []
                    for e in range(2):
                        hh = 2 * pair + e
                        lse_h = lse_ref[hh:hh + 1, :]
                        p = jnp.exp(_swa_scores(s2, e, hh, valid, distf) - lse_h)
                        dp = dp2[2 * BLK * e:2 * BLK * (e + 1)]
                        delta = jnp.sum(p * dp, axis=0, keepdims=True)
                        dsink_ref[hh:hh + 1, :] += -jnp.exp(sink_ref[0, hh] - lse_h) * delta
                        probs.append(p.astype(BF16))
                        dss.append((p * (dp - delta)).astype(BF16))
                    ds2 = jnp.concatenate(dss, axis=0)
                    dq_ref[:, lanes] = (lax.dot_general(ds2, k2, TN, preferred_element_type=F32)
                                        * (HEAD_DIM ** -0.5)).astype(BF16)
                    dk2 = dk2 + jnp.dot(ds2, q2, preferred_element_type=F32)
                    dv2 = dv2 + jnp.dot(jnp.concatenate(probs, axis=0), do2, preferred_element_type=F32)
                dk_cat = (dk2[:2 * BLK, :HEAD_DIM] + dk2[2 * BLK:, HEAD_DIM:]) * (HEAD_DIM ** -0.5)
                dv_cat = dv2[:2 * BLK, :HEAD_DIM] + dv2[2 * BLK:, HEAD_DIM:]
                ko = kvh * HEAD_DIM
                vo = 128 + kvh * HEAD_DIM
                dkv_ref[:, ko:ko + HEAD_DIM] = (carry_ref[:, ko:ko + HEAD_DIM] + dk_cat[:BLK]).astype(BF16)
                dkv_ref[:, vo:vo + HEAD_DIM] = (carry_ref[:, vo:vo + HEAD_DIM] + dv_cat[:BLK]).astype(BF16)
                carry_ref[:, ko:ko + HEAD_DIM] = dk_cat[BLK:]
                carry_ref[:, vo:vo + HEAD_DIM] = dv_cat[BLK:]

        @pl.when(n == nb)
        def _():
            dkv_ref[...] = carry_ref[...].astype(BF16)

    last = nb - 1
    res, xres = _call(
        body, name=name, grid=(nb + 1,),
        in_specs=[SMEM,
                  pl.BlockSpec((BLK, ATTN_W), lambda n: (jnp.minimum(n, last), 0)),
                  pl.BlockSpec((BLK, 256), lambda n: (jnp.minimum(n, last), kvb)),
                  pl.BlockSpec((BLK, 256), lambda n: (jnp.maximum(jnp.minimum(n, last) - 1, 0), kvb)),
                  pl.BlockSpec((BLK, ATTN_W), lambda n: (jnp.minimum(n, last), 0)),
                  pl.BlockSpec((None, N_Q, BLK), lambda n: (jnp.minimum(n, last), 0, 0))],
        out_specs=[pl.BlockSpec((BLK, ATTN_W), lambda n: (jnp.minimum(n, last), 0)),
                   pl.BlockSpec((BLK, 256), lambda n: (jnp.maximum(n - 1, 0), 0)),
                   pl.BlockSpec((N_Q, BLK), lambda n: (0, 0))],
        out_shape=[jax.ShapeDtypeStruct((t, ATTN_W), BF16), jax.ShapeDtypeStruct((t, 256), BF16),
                   jax.ShapeDtypeStruct((N_Q, BLK), F32)],
        scratch_shapes=[pltpu.VMEM((BLK, 256), F32)],
        args=[sinks, proj, proj, proj, dattn, lse], semantics=("arbitrary",), exchanges=exchanges)
    return (*res, xres) if exchanges else res


def _cumsum_rows(x):
    n = x.shape[0]
    row = lax.broadcasted_iota(jnp.int32, x.shape, 0)
    s = 1
    while s < n:
        x = x + jnp.where(row >= s, pltpu.roll(x, s, axis=0), 0.0)
        s *= 2
    return x


def _rev_cumsum_rows(x):
    n = x.shape[0]
    row = lax.broadcasted_iota(jnp.int32, x.shape, 0)
    s = 1
    while s < n:
        x = x + jnp.where(row < n - s, pltpu.roll(x, n - s, axis=0), 0.0)
        s *= 2
    return x


def _lower_bound(lbl_ref):
    l0 = lbl_ref[0:1, :]
    l1 = lbl_ref[1:2, :]
    mx = jnp.maximum(l0, l1)
    e0 = jnp.exp(l0 - mx)
    e1 = jnp.exp(l1 - mx)
    return e0 / (e0 + e1)


def _hgrn_gates(z, lb):
    sg = _sigmoid(z)
    f = lb + (1.0 - lb) * sg
    return sg, f, jnp.log(f), 1.0 - f


def _sub_factors(b, k, i, sub, trim):
    need = -(-sub * i // 16) * 16 if trim else CHUNK
    rows = lax.broadcasted_iota(jnp.int32, (need, RNN_HD), 0)
    ref = b[sub * i - 1:sub * i, :]
    qfac = jnp.exp(b[sub * i:sub * (i + 1), :] - ref)
    kfac = jnp.where(rows < sub * i, jnp.exp(ref - b[:need]), 0.0)
    kt = (k[:need] * kfac).astype(BF16)
    if need < CHUNK:
        kt = jnp.concatenate([kt, jnp.zeros((CHUNK - need, RNN_HD), BF16)], axis=0)
    return qfac, kfac, kt


def _diag_decay(bi, s):
    trow = lax.broadcasted_iota(jnp.int32, bi.shape, 0)
    return jnp.where(trow >= s, jnp.exp(bi - bi[s:s + 1, :]), 0.0)


def _hgrn_fwd(proj, lb_logits, norm_gain, *, tb, name, exchanges=()):
    t = proj.shape[0]
    ntb = t // tb
    nch = tb // CHUNK
    qb, fb, ib, gb = QR_COL // 128, FR_COL // 128, IR_COL // 128, GR_COL // 128

    def body(q_ref, f_ref, i_ref, g_ref, lbl_ref, gain_ref, o_ref, out_ref, s0_ref, st_ref):
        c = pl.program_id(1)

        @pl.when(c == 0)
        def _():
            st_ref[...] = jnp.zeros_like(st_ref)

        lb = _lower_bound(lbl_ref)
        gain = gain_ref[...]

        def chunk(ci, st):
            rows = slice(ci * CHUNK, (ci + 1) * CHUNK)
            _, _, lf, k = _hgrn_gates(f_ref[rows, :], lb)
            qr = q_ref[rows, :]
            q = qr * _sigmoid(qr)
            v = i_ref[rows, :]
            b = _cumsum_rows(lf)
            s0_ref[ci] = st
            o_inter = lax.dot_general((q * jnp.exp(b)).astype(BF16), st.astype(BF16), NT,
                                      preferred_element_type=F32)
            vb = v.astype(BF16)
            blast = b[CHUNK - 1:CHUNK, :]
            khat = (k * jnp.exp(blast - b)).astype(BF16)
            st = st * jnp.exp(blast) + lax.dot_general(vb, khat, TN, preferred_element_type=F32)
            blocks = []
            for i in range(CHUNK // SUB_FWD):
                blk = slice(SUB_FWD * i, SUB_FWD * (i + 1))
                qi, ki, vi, bi = q[blk], k[blk], v[blk], b[blk]
                oi = o_inter[blk]
                if i > 0:
                    qfac, _, kt = _sub_factors(b, k, i, SUB_FWD, trim=True)
                    att = lax.dot_general((qi * qfac).astype(BF16), kt, NT,
                                          preferred_element_type=F32)
                    oi = oi + jnp.dot(att.astype(BF16), vb, preferred_element_type=F32)
                for s in range(SUB_FWD):
                    qe = qi * _diag_decay(bi, s)
                    a = jnp.sum(qe * ki[s:s + 1, :], axis=1, keepdims=True)
                    oi = oi + a * vi[s:s + 1, :]
                blocks.append(oi)
            o = jnp.concatenate(blocks, axis=0)
            o_ref[rows, :] = o
            gr = g_ref[rows, :]
            out_ref[rows, :] = (o * _rstd(o) * gain * (gr * _sigmoid(gr))).astype(BF16)
            return st

        st = st_ref[...]
        for ci in range(nch):
            st = chunk(ci, st)
        st_ref[...] = st

    def col(base):
        return pl.BlockSpec((tb, RNN_HD), lambda h, c: (c, base + h))

    res, xres = _call(
        body, name=name, grid=(N_RNN, ntb),
        in_specs=[col(qb), col(fb), col(ib), col(gb),
                  pl.BlockSpec((2, RNN_HD), lambda h, c: (0, h)), pl.BlockSpec((1, RNN_HD), lambda h, c: (0, 0))],
        out_specs=[pl.BlockSpec((tb, RNN_HD), lambda h, c: (c, h)), pl.BlockSpec((tb, RNN_HD), lambda h, c: (c, h)),
                   pl.BlockSpec((None, nch, RNN_HD, RNN_HD), lambda h, c: (h, c, 0, 0))],
        out_shape=[jax.ShapeDtypeStruct((t, RNN_W), F32), jax.ShapeDtypeStruct((t, RNN_W), BF16),
                   jax.ShapeDtypeStruct((N_RNN, t // CHUNK, RNN_HD, RNN_HD), F32)],
        scratch_shapes=[pltpu.VMEM((RNN_HD, RNN_HD), F32)],
        args=[proj, proj, proj, proj, lb_logits, norm_gain],
        semantics=("parallel", "arbitrary"), exchanges=exchanges)
    return (*res, xres) if exchanges else res


def _hgrn_bwd(proj, lb_logits, norm_gain, o_pre, s0, dcat, *, tb, name, exchanges=()):
    t = proj.shape[0]
    ntb = t // tb
    nch = tb // CHUNK
    qb, fb, ib, gb = QR_COL // 128, FR_COL // 128, IR_COL // 128, GR_COL // 128
    sub = SUB_BWD
    nsub = CHUNK // sub

    def body(q_ref, f_ref, i_ref, g_ref, lbl_ref, gain_ref, o_ref, s0_ref, dout_ref,
             dq_ref, df_ref, di_ref, dg_ref, dlb_ref, dgain_ref,
             dst_ref, dqs_ref, dks_ref, dvs_ref):
        c = pl.program_id(1)

        @pl.when(c == 0)
        def _():
            dst_ref[...] = jnp.zeros_like(dst_ref)
            dlb_ref[...] = jnp.zeros_like(dlb_ref)
            dgain_ref[...] = jnp.zeros_like(dgain_ref)

        lb = _lower_bound(lbl_ref)
        gain = gain_ref[...]

        def chunk(ci, dst):
            rows = slice(ci * CHUNK, (ci + 1) * CHUNK)
            dqa_ref, dka_ref, dva_ref = dqs_ref.at[ci], dks_ref.at[ci], dvs_ref.at[ci]
            sg, f, lf, k = _hgrn_gates(f_ref[rows, :], lb)
            qr = q_ref[rows, :]
            sq = _sigmoid(qr)
            q = qr * sq
            v = i_ref[rows, :]
            b = _cumsum_rows(lf)

            dout = dout_ref[rows, :].astype(F32)
            o = o_ref[rows, :]
            gr = g_ref[rows, :]
            sgg = _sigmoid(gr)
            gate = gr * sgg
            rs = _rstd(o)
            nrm = o * rs
            dg_ref[rows, :] = (dout * nrm * gain * (sgg * (1.0 + gr * (1.0 - sgg)))).astype(BF16)
            dn = dout * gate
            dgain_ref[...] += jnp.sum(dn * nrm, axis=0, keepdims=True)
            tt = dn * gain
            do = rs * (tt - nrm * jnp.mean(tt * nrm, axis=-1, keepdims=True))

            dob = do.astype(BF16)
            vb = v.astype(BF16)
            eb = jnp.exp(b)
            blast = b[CHUNK - 1:CHUNK, :]
            ebl = jnp.exp(blast - b)
            dstb = dst.astype(BF16)
            khat = (k * ebl).astype(BF16)
            s0 = s0_ref[ci]
            dqa_ref[...] = eb * jnp.dot(dob, s0.astype(BF16), preferred_element_type=F32)
            dk_state = ebl * jnp.dot(vb, dstb, preferred_element_type=F32)
            dka_ref[...] = dk_state
            d_blast = (jnp.sum(k * dk_state, axis=0, keepdims=True)
                       + jnp.exp(blast) * jnp.sum(dst * s0, axis=0, keepdims=True))
            dva_ref[...] = lax.dot_general(khat, dstb, NT, preferred_element_type=F32)
            dst_next = dst * jnp.exp(blast) + lax.dot_general(dob, (q * eb).astype(BF16), TN,
                                                              preferred_element_type=F32)
            pm = lax.dot_general(dob, vb, NT, preferred_element_type=F32)
            for i in range(nsub):
                blk = slice(sub * i, sub * (i + 1))
                qi, ki, vi, bi, doi = q[blk], k[blk], v[blk], b[blk], do[blk]
                dqi = dqa_ref[blk, :]
                if i > 0:
                    qfac, kfac, kt = _sub_factors(b, k, i, sub, trim=False)
                    qt = (qi * qfac).astype(BF16)
                    att = lax.dot_general(qt, kt, NT, preferred_element_type=F32).astype(BF16)
                    pmi = pm[blk, :].astype(BF16)
                    dva_ref[...] += lax.dot_general(att, doi.astype(BF16), TN, preferred_element_type=F32)
                    dqi = dqi + qfac * jnp.dot(pmi, kt, preferred_element_type=F32)
                    dka_ref[...] += kfac * lax.dot_general(pmi, qt, TN, preferred_element_type=F32)
                dqa_ref[blk, :] = dqi
                srow = lax.broadcasted_iota(jnp.int32, (sub, RNN_HD), 0)
                dki = jnp.zeros((sub, RNN_HD), F32)
                dvi = jnp.zeros((sub, RNN_HD), F32)
                for tq in range(sub):
                    qt, dot_ = qi[tq:tq + 1, :], doi[tq:tq + 1, :]
                    e = jnp.where(srow <= tq, jnp.exp(bi[tq:tq + 1, :] - bi), 0.0)
                    ke = ki * e
                    p = jnp.sum(vi * dot_, axis=1, keepdims=True)
                    a = jnp.sum(ke * qt, axis=1, keepdims=True)
                    dki = dki + p * (qt * e)
                    dvi = dvi + a * dot_
                    row = slice(sub * i + tq, sub * i + tq + 1)
                    dqa_ref[row, :] += jnp.sum(p * ke, axis=0, keepdims=True)
                dka_ref[blk, :] += dki
                dva_ref[blk, :] += dvi

            dq = dqa_ref[...]
            dk = dka_ref[...]
            lastrow = lax.broadcasted_iota(jnp.int32, (CHUNK, RNN_HD), 0) == CHUNK - 1
            dlf = _rev_cumsum_rows(q * dq - k * dk + jnp.where(lastrow, d_blast, 0.0))
            dff = dlf / f - dk
            df_ref[rows, :] = (dff * (1.0 - lb) * sg * (1.0 - sg)).astype(BF16)
            dlb_ref[...] += jnp.sum(dff * (1.0 - sg), axis=0, keepdims=True)
            dq_ref[rows, :] = (dq * (sq * (1.0 + qr * (1.0 - sq)))).astype(BF16)
            di_ref[rows, :] = dva_ref[...].astype(BF16)
            return dst_next

        dst = dst_ref[...]
        for ci in reversed(range(nch)):
            dst = chunk(ci, dst)
        dst_ref[...] = dst

    def col(base):
        return pl.BlockSpec((tb, RNN_HD), lambda h, c: (ntb - 1 - c, base + h))

    outc = pl.BlockSpec((tb, RNN_HD), lambda h, c: (ntb - 1 - c, h))
    hb = ATTN_W // RNN_HD
    res, xres = _call(
        body, name=name, grid=(N_RNN, ntb),
        in_specs=[col(qb), col(fb), col(ib), col(gb),
                  pl.BlockSpec((2, RNN_HD), lambda h, c: (0, h)), pl.BlockSpec((1, RNN_HD), lambda h, c: (0, 0)),
                  outc,
                  pl.BlockSpec((None, nch, RNN_HD, RNN_HD), lambda h, c: (h, ntb - 1 - c, 0, 0)),
                  pl.BlockSpec((tb, RNN_HD), lambda h, c: (ntb - 1 - c, hb + h))],
        out_specs=[outc, outc, outc, outc,
                   pl.BlockSpec((1, RNN_HD), lambda h, c: (0, h)),
                   pl.BlockSpec((None, 1, RNN_HD), lambda h, c: (h, 0, 0))],
        out_shape=[jax.ShapeDtypeStruct((t, RNN_W), BF16)] * 4
        + [jax.ShapeDtypeStruct((1, RNN_W), F32), jax.ShapeDtypeStruct((N_RNN, 1, RNN_HD), F32)],
        scratch_shapes=[pltpu.VMEM((RNN_HD, RNN_HD), F32),
                        pltpu.VMEM((nch, CHUNK, RNN_HD), F32), pltpu.VMEM((nch, CHUNK, RNN_HD), F32),
                        pltpu.VMEM((nch, CHUNK, RNN_HD), F32)],
        args=[proj, proj, proj, proj, lb_logits, norm_gain, o_pre, s0, dcat],
        semantics=("parallel", "arbitrary"), exchanges=exchanges)
    return (*res, xres) if exchanges else res


def _cast_slots(w, where, *, name):
    _, rows, cols = w.shape
    rh = rows // 2
    tr = _row_tile(rh, cols)
    nh = rh // tr

    def body(wh_ref, w_ref, o_ref):
        o_ref[...] = w_ref[...].astype(BF16)

    return pl.pallas_call(
        body, name=name,
        grid_spec=pltpu.PrefetchScalarGridSpec(
            num_scalar_prefetch=1, grid=(2, nh),
            in_specs=[pl.BlockSpec((None, tr, cols), lambda h, i, wh: (0, h * nh + i, 0))],
            out_specs=pl.BlockSpec((None, tr, cols), lambda h, i, wh: (2 * wh[0] + h, i, 0))),
        out_shape=jax.ShapeDtypeStruct((8, rh, cols), BF16),
        compiler_params=_params(("parallel", "parallel")),
    )(where, w)


def _row_tile(rows, cols, budget=1 << 20):
    tr = rows
    while tr * cols > budget and tr % 16 == 0:
        tr //= 2
    return tr


def _half_spec(g, tr, halves_last, slab):
    if halves_last:
        return pl.BlockSpec((None, tr, g.shape[2] // 2), lambda *a: (slab(*a), a[-2], a[-1][1]))
    return pl.BlockSpec((None, None, tr, g.shape[3]), lambda *a: (slab(*a), a[-1][1], a[-2], 0))


def _pair_sum(g, sib, where, *, name, halves_last=False):
    rh, cols = sib.shape[1:]
    tr = _row_tile(rh, cols)

    def body(w_ref, g_ref, s_ref, o_ref):
        o_ref[...] = (g_ref[...] + s_ref[...]).astype(BF16)

    def foreign(s, i, w):
        return (w[0] + 1 + s) % N_CHIPS

    return pl.pallas_call(
        body, name=name,
        grid_spec=pltpu.PrefetchScalarGridSpec(
            num_scalar_prefetch=1, grid=(N_CHIPS - 1, rh // tr),
            in_specs=[_half_spec(g, tr, halves_last, foreign),
                      pl.BlockSpec((None, tr, cols), lambda s, i, w: (foreign(s, i, w), i, 0))],
            out_specs=pl.BlockSpec((None, tr, cols), lambda s, i, w: (foreign(s, i, w), i, 0))),
        out_shape=jax.ShapeDtypeStruct((4, rh, cols), BF16),
        compiler_params=_params(("parallel", "parallel")),
    )(where, g, sib)


def _final_half(g, sib, recv, where, *, name, halves_last=False):
    rh, cols = sib.shape[1:]
    tr = _row_tile(rh, cols)

    def body(w_ref, g_ref, s_ref, r_ref, o_ref):
        acc = g_ref[...] + s_ref[...]
        for j in range(3):
            acc = acc + r_ref[j].astype(F32)
        o_ref[...] = acc

    return pl.pallas_call(
        body, name=name,
        grid_spec=pltpu.PrefetchScalarGridSpec(
            num_scalar_prefetch=1, grid=(rh // tr,),
            in_specs=[_half_spec(g, tr, halves_last, lambda i, w: w[0]),
                      pl.BlockSpec((None, tr, cols), lambda i, w: (w[0], i, 0)),
                      pl.BlockSpec((3, tr, cols), lambda i, w: (0, i, 0))],
            out_specs=pl.BlockSpec((tr, cols), lambda i, w: (i, 0))),
        out_shape=jax.ShapeDtypeStruct((rh, cols), F32),
        compiler_params=_params(("parallel",)),
    )(where, g, sib, recv)


def _adamw_math(w, g, m, v):
    m = ADAM_B1 * m + (1.0 - ADAM_B1) * g
    v = ADAM_B2 * v + (1.0 - ADAM_B2) * (g * g)
    m_hat = m / (1.0 - ADAM_B1 ** ADAM_STEP)
    v_hat = v / (1.0 - ADAM_B2 ** ADAM_STEP)
    delta = -ADAM_LR * (m_hat / (jnp.sqrt(v_hat) + ADAM_EPS) + ADAM_WD * w)
    return delta, m, v


def _adamw(w, mine, theirs, m, v, where, *, name, halves_last=False):
    _, rows, cols = w.shape
    if halves_last:
        cols //= 2
        tr = _row_tile(rows, cols, budget=1 << 19)
        grid = (rows // tr, 2)
        blk = pl.BlockSpec((None, tr, cols), lambda i, h, wh: (0, i, h))
        mine_spec = theirs_spec = pl.BlockSpec((tr, cols), lambda i, h, wh: (i, 0))
        which = lambda: pl.program_id(1)
    else:
        tr = _row_tile(rows // 2, cols, budget=1 << 19)
        nh = rows // 2 // tr
        grid = (rows // tr,)
        blk = pl.BlockSpec((None, tr, cols), lambda i, wh: (0, i, 0))
        mine_spec = pl.BlockSpec((tr, cols), lambda i, wh: (jnp.where(i // nh == wh[1], i % nh, 0), 0))
        theirs_spec = pl.BlockSpec((tr, cols), lambda i, wh: (jnp.where(i // nh == wh[1], 0, i % nh), 0))
        which = lambda: pl.program_id(0) // nh

    def body(wh_ref, w_ref, a_ref, b_ref, m_ref, v_ref, g_ref, d_ref, nm_ref, nv_ref):
        g = jnp.where(which() == wh_ref[1], a_ref[...], b_ref[...])
        d, nm, nv = _adamw_math(w_ref[...], g, m_ref[...], v_ref[...])
        g_ref[...] = g
        d_ref[...] = d
        nm_ref[...] = nm
        nv_ref[...] = nv

    rows, cols = w.shape[1:]
    return pl.pallas_call(
        body, name=name,
        grid_spec=pltpu.PrefetchScalarGridSpec(
            num_scalar_prefetch=1, grid=grid,
            in_specs=[blk, mine_spec, theirs_spec, blk, blk], out_specs=[blk] * 4),
        out_shape=[jax.ShapeDtypeStruct((1, rows, cols), F32)] * 4,
        compiler_params=_params(("parallel",) * len(grid)),
    )(where, w, mine, theirs, m, v)


SEG_LOSS = 0
SEG_SINK = 128
SEG_AGAIN = 256
SEG_L0 = SEG_AGAIN + ATTN_W
SEG_L1 = SEG_L0 + RNN_W
SEG_RGAIN = SEG_L1 + RNN_W
SEG_G = SEG_RGAIN + 128
N_PACK = SEG_G + 4 * D_MODEL


def _pack(sinks, again, l0, l1, rgain, gains, loss=None):
    z = lambda k: jnp.zeros((1, k), F32)
    first = z(128) if loss is None else loss
    return jnp.concatenate([first, sinks, z(128 - N_Q), again, l0, l1, rgain] + list(gains), axis=1)


def _small_reduce_adamw(part, w, m, v, *, name):
    def body(p_ref, w_ref, m_ref, v_ref, g_ref, d_ref, nm_ref, nv_ref, buf_ref, send_sems, recv_sems):
        x, y, c = _place()
        me = 4 * x + 2 * y + c
        copies = []
        for k in range(1, 8):
            dx, dy, dc = (k >> 2) & 1, (k >> 1) & 1, k & 1
            to = (x ^ dx, y ^ dy, c ^ dc)
            cp = pltpu.make_async_remote_copy(
                src_ref=p_ref, dst_ref=buf_ref.at[me],
                send_sem=send_sems.at[k - 1], recv_sem=recv_sems.at[k - 1],
                device_id=to, device_id_type=MESH)
            cp.start()
            copies.append(cp)
        buf_ref[me] = p_ref[...]
        for cp in copies:
            cp.wait()
        tot = buf_ref[0]
        for j in range(1, 8):
            tot = tot + buf_ref[j]
        g_ref[...] = tot
        l0 = w_ref[:, SEG_L0:SEG_L0 + RNN_W]
        l1 = w_ref[:, SEG_L1:SEG_L1 + RNN_W]
        mx = jnp.maximum(l0, l1)
        e0 = jnp.exp(l0 - mx)
        e1 = jnp.exp(l1 - mx)
        lb = e0 / (e0 + e1)
        gl0 = tot[:, SEG_L0:SEG_L0 + RNN_W] * lb * (1.0 - lb)
        g_ref[:, SEG_L0:SEG_L0 + RNN_W] = gl0
        g_ref[:, SEG_L1:SEG_L1 + RNN_W] = -gl0
        d, nm, nv = _adamw_math(w_ref[...], g_ref[...], m_ref[...], v_ref[...])
        d_ref[...] = d
        nm_ref[...] = nm
        nv_ref[...] = nv

    vm = pl.BlockSpec(memory_space=pltpu.VMEM)
    return pl.pallas_call(
        body, name=name,
        in_specs=[vm] * 4, out_specs=[vm] * 4,
        out_shape=[jax.ShapeDtypeStruct((1, N_PACK), F32)] * 4,
        scratch_shapes=[pltpu.VMEM((8, 1, N_PACK), F32), pltpu.SemaphoreType.DMA((7,)),
                        pltpu.SemaphoreType.DMA((7,))],
    )(part, w, m, v)


def _layer_grads(xs, tgt, bufs, where, sinks, again, lb_logits, rgain,
                 g_mix_pre, g_mix_post, g_mlp_pre, g_mlp_post):
    tm = 512
    b_in, b_out, b_up, b_dn = bufs

    shard = IN_W // N_CHIPS
    h1, b_in = _rms_cast_gather(xs, g_mix_pre, b_in, tm=tm, name="h1_norm_gather_w_in")
    w_in_t = b_in.reshape(IN_W, D_MODEL)
    proj, ((b_out, b_up),) = _mm(
        h1, w_in_t, tm=1024, tn=768, tk=D_MODEL, out_dtype=F32, w_layout="nk", name="in_proj",
        exchanges=[_x_gather([b_out, b_up], ici=[(0, 256), (0, 336)])])
    attn, lse, ((b_out, b_up),) = _swa_fwd(
        proj, sinks, name="swa_fwd",
        exchanges=[_x_gather([b_out, b_up], ici=[None, (336, 320)], d2d=[(0, 256), None])])
    w_out = b_out.reshape(D_MODEL, D_MODEL)
    o_pre, rnn, s0, ((b_up, b_dn),) = _hgrn_fwd(
        proj, lb_logits, rgain, tb=512, name="hgrn_fwd",
        exchanges=[_x_gather([b_up, b_dn], ici=[(656, 368), (0, 400)])])
    cat = _mix_cat(attn, rnn, again, tm=tm, name="mix_cat")
    mixed, ((b_up, b_dn),) = _mm(
        cat, w_out, tm=1024, tn=1024, tk=D_MODEL, out_dtype=BF16, name="out_proj",
        exchanges=[_x_gather([b_up, b_dn], ici=[None, (400, 240)], d2d=[(0, 1024), (0, 400)])])
    w_up4 = b_up.reshape(N_CHIPS, D_MODEL, D_FF // N_CHIPS)
    x1, h2, ((b_dn,),) = _post_norm_res(
        mixed, g_mix_post, xs, g_mlp_pre, tm=tm, name="mix_post",
        exchanges=[_x_gather([b_dn], d2d=[(400, 240)])])
    u, ((b_dn,),) = _mm(h2, w_up4, tm=1024, tn=1024, tk=D_MODEL, out_dtype=BF16, relu=True, w_layout="skn",
                        name="mlp_up", exchanges=[_x_gather([b_dn], ici=[(640, 384)], cross=[(640, 384)])])
    w_dn = b_dn.reshape(D_FF, D_MODEL)
    yv = _mm(u, w_dn, tm=1024, tn=1024, tk=2048, out_dtype=BF16, a_square=True, name="mlp_down")
    dy, dx2, loss_row, dg_mlp_post = _loss_head(yv, g_mlp_post, x1, tgt, tm=tm, name="loss_head")

    def halved(g):
        return g.reshape(N_CHIPS, 2, g.shape[1] // 2, g.shape[2])
    du = _mm(dy, w_dn, tm=1024, tn=1024, tk=D_MODEL, out_dtype=BF16, mul2=u, w_layout="nk", name="mlp_down_bwd")
    g_dn = halved(_mm_tn(u, dy, tm=1024, tn=1024, tt=2048, a_square=True, name="w_down_grad")
                  .reshape(N_CHIPS, D_FF // N_CHIPS, D_MODEL))
    d_w_up, ((sib_dn,),) = _mm_tn(h2, du, tm=1024, tn=1024, tt=2048, n_split=N_CHIPS, name="w_up_grad",
                                  exchanges=[_x_pair([g_dn])])
    g_up = halved(d_w_up)
    wire_dn = _pair_sum(g_dn, sib_dn, where, name="pair_sum_w_down")
    dh2, ((recv_dn,), (sib_up,)) = _mm(du, w_up4, tm=1024, tn=1024, tk=2048, out_dtype=BF16, w_layout="snk", name="mlp_up_bwd",
                                       exchanges=[_x_chip([wire_dn], rows=[(0, 928)]), _x_pair([g_up])])
    wire_up = _pair_sum(g_up, sib_up, where, name="pair_sum_w_up")
    dx1, dg_mlp_pre = _rms_bwd(dh2, x1, g_mlp_pre, dx2, tm=tm, out_dtype=BF16, name="mlp_pre_bwd")
    dmixed, dg_mix_post = _rms_bwd(dx1, mixed, g_mix_post, None, tm=tm, out_dtype=BF16, name="mix_post_bwd")
    d_w_out, ((recv_dn,),) = _mm_tn(cat, dmixed, tm=1024, tn=1024, tt=2048, name="w_out_grad",
                                    exchanges=[_x_chip([wire_dn], rows=[(928, 96)], into=[recv_dn])])
    fin_dn = _final_half(g_dn, sib_dn, recv_dn, where, name="final_half_w_down")
    g_out = halved(d_w_out.reshape(N_CHIPS, D_MODEL // N_CHIPS, D_MODEL))
    dcat, ((sib_out,), (oth_dn,)) = _mm(dmixed, w_out, tm=1024, tn=1024, tk=D_MODEL, out_dtype=BF16, w_layout="nk",
                                        name="out_proj_bwd", exchanges=[_x_pair([g_out]), _x_share([fin_dn])])
    wire_out = _pair_sum(g_out, sib_out, where, name="pair_sum_w_out")
    dattn, dg_again = _rms_bwd(dcat, attn, again, None, tm=tm, out_dtype=BF16, name="attn_norm_bwd")
    dq_a, dkv, dsinks, ((recv_up,),) = _swa_bwd(
        proj, sinks, dattn, lse, name="swa_bwd", exchanges=[_x_chip([wire_up], rows=[(0, 512)])])
    dq_r, df_r, di_r, dg_r, dlb, dgain_h, ((recv_up,), (recv_out,)) = _hgrn_bwd(
        proj, lb_logits, rgain, o_pre, s0, dcat, tb=512, name="hgrn_bwd",
        exchanges=[_x_chip([wire_up], rows=[(512, 512)], into=[recv_up]), _x_chip([wire_out])])
    fin_up = _final_half(g_up, sib_up, recv_up, where, name="final_half_w_up")
    fin_out = _final_half(g_out, sib_out, recv_out, where, name="final_half_w_out")
    dproj = jnp.concatenate([dq_a, dkv, dq_r, df_r, di_r, dg_r], axis=1)
    piece_cols = D_MODEL // 4

    def w_in_piece(pc, exchanges):
        d, xres = _mm_tn(dproj, h1, tm=896, tn=2 * piece_cols, tt=2048, b_blocks=(pc, pc + 2),
                         name="w_in_grad_%d" % pc, exchanges=exchanges)
        return d.reshape(N_CHIPS, shard, 2 * piece_cols), xres

    g_in0, ((oth_up, oth_out),) = w_in_piece(0, [_x_share([fin_up, fin_out])])
    g_in1, ((sib_in0,),) = w_in_piece(1, [_x_pair([g_in0], halves_last=True)])
    wire_in0 = _pair_sum(g_in0, sib_in0, where, name="pair_sum_w_in_0", halves_last=True)
    dh1, ((recv_in0,), (sib_in1,)) = _mm(
        dproj, w_in_t, tm=1024, tn=1024, tk=2688, out_dtype=BF16, m_blocks=(0, 2), name="in_proj_bwd_0",
        exchanges=[_x_chip([wire_in0]), _x_pair([g_in1], halves_last=True)])
    wire_in1 = _pair_sum(g_in1, sib_in1, where, name="pair_sum_w_in_1", halves_last=True)
    dh1, ((recv_in1,),) = _mm(
        dproj, w_in_t, tm=1024, tn=1024, tk=2688, out_dtype=BF16, m_blocks=(2, 2), out_into=dh1,
        name="in_proj_bwd_1", exchanges=[_x_chip([wire_in1])])
    gx, dg_mix_pre = _rms_bwd(dh1, xs, g_mix_pre, dx1, tm=tm, out_dtype=F32, name="mix_pre_bwd")
    fin_in0 = _final_half(g_in0, sib_in0, recv_in0, where, name="final_half_w_in_0", halves_last=True)
    fin_in1 = _final_half(g_in1, sib_in1, recv_in1, where, name="final_half_w_in_1", halves_last=True)
    oth_in0, oth_in1 = _run_exchange(_x_share([fin_in0, fin_in1]), name="share_w_in")
    fin_in = jnp.concatenate([fin_in0, fin_in1], axis=1)
    oth_in = jnp.concatenate([oth_in0, oth_in1], axis=1)

    big = [(fin_in, oth_in), (fin_out, oth_out), (fin_up, oth_up), (fin_dn, oth_dn)]
    drgain = jnp.sum(dgain_h, axis=0)
    small = _pack(jnp.sum(dsinks, axis=1)[None, :], dg_again, dlb, jnp.zeros_like(dlb), drgain,
                  [dg_mix_pre, dg_mix_post, dg_mlp_pre, dg_mlp_post], loss=loss_row)
    return gx, big, small


def kernel(x, w_in, attn_sinks, attn_out_gain, rnn_lb_logits, rnn_norm_gain, w_out, mix_pre_gain, mix_post_gain, mlp_pre_gain, mlp_post_gain, w_up, w_down, loss_target, m_w_in, m_attn_sinks, m_attn_out_gain, m_rnn_lb_logits, m_rnn_norm_gain, m_w_out, m_mix_pre_gain, m_mix_post_gain, m_mlp_pre_gain, m_mlp_post_gain, m_w_up, m_w_down, v_w_in, v_attn_sinks, v_attn_out_gain, v_rnn_lb_logits, v_rnn_norm_gain, v_w_out, v_mix_pre_gain, v_mix_post_gain, v_mlp_pre_gain, v_mlp_post_gain, v_w_up, v_w_down):
    ax, ay, ac = _place()
    where = jnp.stack([2 * ax + ay, ac]).astype(jnp.int32)
    t = lambda a: jnp.swapaxes(a, 1, 2)
    big_w = [t(w_in), w_out, w_up, w_down]
    big_m = [t(m_w_in), m_w_out, m_w_up, m_w_down]
    big_v = [t(v_w_in), v_w_out, v_w_up, v_w_down]

    names = ["w_in", "w_out", "w_up", "w_down"]
    bufs = [_cast_slots(w, where, name="cast_" + nm) for w, nm in zip(big_w, names)]
    gx, big_g, small_part = _layer_grads(
        x[0], loss_target[0], bufs, where, attn_sinks, attn_out_gain, rnn_lb_logits, rnn_norm_gain,
        mix_pre_gain, mix_post_gain, mlp_pre_gain, mlp_post_gain)

    grads, deltas, new_m, new_v = [], [], [], []
    for (f, o), w, m, v, nm in zip(big_g, big_w, big_m, big_v, names):
        res = _adamw(w, f, o, m, v, where, name="adamw_" + nm, halves_last=(nm == "w_in"))
        if nm == "w_in":
            res = [t(r) for r in res]
        g, d, nm_, nv_ = res
        grads.append(g)
        deltas.append(d)
        new_m.append(nm_)
        new_v.append(nv_)

    def pack_params(sinks, again, logits, rgain, gains):
        return _pack(sinks, again, logits[0:1], logits[1:2], rgain, gains)

    pw = pack_params(attn_sinks, attn_out_gain, rnn_lb_logits, rnn_norm_gain,
                     [mix_pre_gain, mix_post_gain, mlp_pre_gain, mlp_post_gain])
    pm = pack_params(m_attn_sinks, m_attn_out_gain, m_rnn_lb_logits, m_rnn_norm_gain,
                     [m_mix_pre_gain, m_mix_post_gain, m_mlp_pre_gain, m_mlp_post_gain])
    pv = pack_params(v_attn_sinks, v_attn_out_gain, v_rnn_lb_logits, v_rnn_norm_gain,
                     [v_mix_pre_gain, v_mix_post_gain, v_mlp_pre_gain, v_mlp_post_gain])
    packs = _small_reduce_adamw(small_part, pw, pm, pv, name="small_reduce_adamw")

    def unpack(p):
        seg = lambda o, k: p[:, o:o + k]
        logits = jnp.concatenate([seg(SEG_L0, RNN_W), seg(SEG_L1, RNN_W)], axis=0)
        gains = [seg(SEG_G + i * D_MODEL, D_MODEL) for i in range(4)]
        return dict(sinks=seg(SEG_SINK, N_Q), again=seg(SEG_AGAIN, ATTN_W), logits=logits,
                    rgain=seg(SEG_RGAIN, RNN_HD), gains=gains)

    def order(small, big):
        return [big[0], small["sinks"], small["again"], small["logits"], small["rgain"], big[1],
                *small["gains"], big[2], big[3]]

    loss = packs[0][0, 0]
    outs = [loss, gx[None]]
    for p, b in zip(packs, [grads, deltas, new_m, new_v]):
        outs += order(unpack(p), b)
    return tuple(outs)
```

```python
import functools

import jax
import jax.numpy as jnp
from jax import lax
from jax.experimental import pallas as pl
from jax.experimental.pallas import tpu as pltpu

F32 = jnp.float32
BF16 = jnp.bfloat16
MESH = pl.DeviceIdType.MESH

EPS = 1e-6
D_MODEL = 2048
ATTN_W = 1024
HEAD_DIM = 64
N_Q = 16
N_KV = 2
GROUP = 8
BLK = 128
RNN_W = 1024
RNN_HD = 128
N_RNN = 8
CHUNK = 64
SUB_FWD = 16
SUB_BWD = 8
D_FF = 8192
IN_W = 5376
N_CHIPS = 4
KV_COL = ATTN_W
QR_COL = ATTN_W + 2 * 128
FR_COL = QR_COL + RNN_W
IR_COL = FR_COL + RNN_W
GR_COL = IR_COL + RNN_W

ADAM_LR = 0.001
ADAM_B1 = 0.9
ADAM_B2 = 0.999
ADAM_EPS = 1e-08
ADAM_WD = 0.01
ADAM_STEP = 10

VMEM_LIMIT = 48 * 1024 * 1024

NT = (((1,), (1,)), ((), ()))
TN = (((0,), (0,)), ((), ()))


def _params(sem=None):
    return pltpu.CompilerParams(dimension_semantics=sem, vmem_limit_bytes=VMEM_LIMIT)


def _sigmoid(x):
    return 1.0 / (1.0 + jnp.exp(-x))


ANY = pl.BlockSpec(memory_space=pl.ANY)


def _place():
    return lax.axis_index("x"), lax.axis_index("y"), lax.axis_index("c")


def _other_chips(x, y):
    return [(1 - x, y), (x, 1 - y), (1 - x, 1 - y)]


class _Exchange:
    def __init__(self, srcs, outs, ncopy, build, aliases=None):
        self.srcs, self.outs, self.ncopy, self.build = list(srcs), list(outs), ncopy, build
        self.aliases = aliases or {}


def _remote(src, dst, send_sems, recv_sems, k, to):
    return pltpu.make_async_remote_copy(src_ref=src, dst_ref=dst, send_sem=send_sems.at[k],
                                        recv_sem=recv_sems.at[k], device_id=to, device_id_type=MESH)


def _call(body, *, name, grid, in_specs, out_specs, out_shape, args, scratch_shapes=(), semantics=None,
          exchanges=(), into=None):
    in_specs, out_specs, out_shape = list(in_specs), list(out_specs), list(out_shape)
    scratch_shapes = list(scratch_shapes)
    ni, no, ns = len(in_specs), len(out_specs), len(scratch_shapes)
    xsrc = [s for x in exchanges for s in x.srcs]
    xout = [o for x in exchanges for o in x.outs]
    into = into or {}
    xsrc += [into[k] for k in sorted(into)]
    nxi, nxo = len(xsrc), len(xout)
    aliases = {nxi - len(into) + ni + q: k for q, k in enumerate(sorted(into))}
    a0 = b0 = 0
    for x in exchanges:
        for si, oi in x.aliases.items():
            aliases[ni + a0 + si] = no + b0 + oi
        a0 += len(x.srcs)
        b0 += len(x.outs)
    sems = []
    for x in exchanges:
        sems += [pltpu.SemaphoreType.DMA((x.ncopy,)), pltpu.SemaphoreType.DMA((x.ncopy,))]

    def wrapped(*refs):
        ins, xi = refs[:ni], refs[ni:ni + nxi]
        outs, xo = refs[ni + nxi:ni + nxi + no], refs[ni + nxi + no:ni + nxi + no + nxo]
        rest = refs[ni + nxi + no + nxo:]
        scr, sm = rest[:ns], rest[ns:]

        def copies():
            cps = []
            a = b = 0
            for k, x in enumerate(exchanges):
                cps += x.build(xi[a:a + len(x.srcs)], xo[b:b + len(x.outs)], sm[2 * k], sm[2 * k + 1])
                a += len(x.srcs)
                b += len(x.outs)
            return cps

        def start():
            for cp in copies():
                cp.start()

        def wait():
            for cp in copies():
                cp.wait()

        if not exchanges:
            body(*ins, *outs, *scr)
        elif not grid:
            start()
            body(*ins, *outs, *scr)
            wait()
        else:
            first = last = None
            for ax, g in enumerate(grid):
                f = pl.program_id(ax) == 0
                l = pl.program_id(ax) == g - 1
                first = f if first is None else first & f
                last = l if last is None else last & l
            pl.when(first)(start)
            body(*ins, *outs, *scr)
            pl.when(last)(wait)

    if exchanges and semantics is not None:
        semantics = ("arbitrary",) * len(grid)
    kwargs = dict(grid=grid) if grid else {}
    res = pl.pallas_call(
        wrapped, name=name,
        in_specs=in_specs + [ANY] * nxi, out_specs=out_specs + [ANY] * nxo,
        out_shape=out_shape + xout, scratch_shapes=scratch_shapes + sems,
        input_output_aliases=aliases,
        compiler_params=_params(semantics), **kwargs,
    )(*args, *xsrc)
    res = list(res)
    mine, theirs = res[:no], res[no:]
    per = []
    b = 0
    for x in exchanges:
        per.append(theirs[b:b + len(x.outs)])
        b += len(x.outs)
    return mine, per


def _run_exchange(x, *, name):
    return _call(lambda: None, name=name, grid=(), in_specs=[], out_specs=[], out_shape=[], args=[],
                 exchanges=[x])[1][0]


def _x_gather(bufs, ici=None, d2d=None, cross=None):
    n = len(bufs)
    plan = [(a, kind, rows[a]) for a in range(n) for kind, rows in (("ici", ici), ("d2d", d2d), ("cross", cross))
            if rows is not None and rows[a] is not None]

    def build(srcs, outs, ss, rs):
        x, y, c = _place()
        cps = []
        for q, (a, kind, rows) in enumerate(plan):
            piece = pl.ds(*rows)
            for j, (px, py) in enumerate(_other_chips(x, y)):
                if kind == "d2d":
                    slot, to = 4 * px + 2 * py + c, (x, y, 1 - c)
                else:
                    slot, to = 4 * x + 2 * y + c, (px, py, c if kind == "ici" else 1 - c)
                cps.append(_remote(srcs[a].at[slot, piece], outs[a].at[slot, piece], ss, rs, 3 * q + j, to))
        return cps

    outs = [jax.ShapeDtypeStruct(b.shape, b.dtype) for b in bufs]
    return _Exchange(bufs, outs, 3 * len(plan), build, aliases={a: a for a in range(n)})


def _x_pair(grads, halves_last=False):
    n = len(grads)

    def build(srcs, outs, ss, rs):
        x, y, c = _place()

        def half(r):
            if not halves_last:
                return r.at[:, 1 - c]
            ch = r.shape[2] // 2
            return r.at[:, :, pl.ds(pl.multiple_of((1 - c) * ch, 128), ch)]

        return [_remote(half(srcs[a]), outs[a], ss, rs, a, (x, y, 1 - c)) for a in range(n)]

    if halves_last:
        outs = [jax.ShapeDtypeStruct(g.shape[:2] + (g.shape[2] // 2,), g.dtype) for g in grads]
    else:
        outs = [jax.ShapeDtypeStruct((4,) + g.shape[2:], g.dtype) for g in grads]
    return _Exchange(grads, outs, n, build)


def _x_chip(wires, rows=None, into=None):
    n = len(wires)
    rows = rows or [(0, w.shape[1]) for w in wires]

    def build(srcs, outs, ss, rs):
        x, y, c = _place()
        cps = []
        for a in range(n):
            piece = pl.ds(*rows[a])
            for j, (px, py) in enumerate(_other_chips(x, y)):
                cps.append(_remote(srcs[a].at[2 * px + py, piece], outs[a].at[j, piece], ss, rs,
                                   3 * a + j, (px, py, c)))
        return cps

    outs = [jax.ShapeDtypeStruct((3,) + w.shape[1:], w.dtype) for w in wires]
    if into is None:
        return _Exchange(wires, outs, 3 * n, build)
    return _Exchange(list(wires) + list(into), outs, 3 * n, build, aliases={n + a: a for a in range(n)})


def _x_share(halves):
    n = len(halves)

    def build(srcs, outs, ss, rs):
        x, y, c = _place()
        return [_remote(srcs[a], outs[a], ss, rs, a, (x, y, 1 - c)) for a in range(n)]

    outs = [jax.ShapeDtypeStruct(h.shape, h.dtype) for h in halves]
    return _Exchange(halves, outs, n, build)


def _mm(a, w, *, tm, tn, tk, out_dtype, name, a_square=False, relu=False, mul2=None, w_layout="kn",
        m_blocks=None, out_into=None, exchanges=()):
    m, k = a.shape
    m_first, m_count = m_blocks or (0, m // tm)
    a_spec = pl.BlockSpec((tm, tk), lambda i, j, kk: (i + m_first, kk))
    if w_layout == "kn":
        n = w.shape[1]
        w_spec = pl.BlockSpec((tk, tn), lambda i, j, kk: (kk, j))
    elif w_layout == "nk":
        n = w.shape[0]
        w_spec = pl.BlockSpec((tn, tk), lambda i, j, kk: (j, kk))
    elif w_layout == "skn":
        n = w.shape[0] * w.shape[2]
        per_n = w.shape[2] // tn
        w_spec = pl.BlockSpec((None, tk, tn), lambda i, j, kk: (j // per_n, kk, j % per_n))
    else:
        assert w_layout == "snk"
        n = w.shape[1]
        per_k = w.shape[2] // tk
        w_spec = pl.BlockSpec((None, tn, tk), lambda i, j, kk: (kk // per_k, j, kk % per_k))
    w_dims = NT if w_layout in ("nk", "snk") else (((1,), (0,)), ((), ()))
    nk = k // tk
    assert m % tm == 0 and n % tn == 0 and k % tk == 0

    def body(*refs):
        if mul2 is not None:
            a_ref, w_ref, e_ref, o_ref, acc_ref = refs
        else:
            a_ref, w_ref, o_ref, acc_ref = refs
            e_ref = None
        kk = pl.program_id(2)
        av = a_ref[...]
        if a_square:
            af = av.astype(F32)
            av = (af * af).astype(BF16)
        part = lax.dot_general(av, w_ref[...], w_dims, preferred_element_type=F32)

        def finish(r):
            if relu:
                r = jnp.maximum(r, 0.0)
            if e_ref is not None:
                r = 2.0 * e_ref[...].astype(F32) * r
            o_ref[...] = r.astype(out_dtype)

        if nk == 1:
            finish(part)
        else:
            @pl.when(kk == 0)
            def _():
                acc_ref[...] = part

            @pl.when(kk > 0)
            def _():
                acc_ref[...] += part

            @pl.when(kk == nk - 1)
            def _():
                finish(acc_ref[...])

    in_specs = [a_spec, w_spec]
    args = [a, w]
    if mul2 is not None:
        in_specs.append(pl.BlockSpec((tm, tn), lambda i, j, kk: (i + m_first, j)))
        args.append(mul2)
    acc_shape = (tm, tn) if nk > 1 else (8, 128)
    (out,), per = _call(
        body, name=name, grid=(m_count, n // tn, nk),
        in_specs=in_specs, out_specs=[pl.BlockSpec((tm, tn), lambda i, j, kk: (i + m_first, j))],
        out_shape=[jax.ShapeDtypeStruct((m, n), out_dtype)], args=args,
        scratch_shapes=[pltpu.VMEM(acc_shape, F32)],
        semantics=("parallel", "parallel", "arbitrary"), exchanges=exchanges,
        into=None if out_into is None else {0: out_into})
    return (out, per) if exchanges else out


def _mm_tn(a, b, *, tm, tn, tt, name, a_square=False, n_split=1, b_blocks=None, exchanges=()):
    t, m = a.shape
    nb = len(b_blocks) if b_blocks else 1
    n = tn if b_blocks else b.shape[1]
    assert t % tt == 0 and m % tm == 0 and n % tn == 0 and (n // n_split) % tn == 0
    per = n // n_split // tn

    def body(a_ref, *refs):
        b_refs, o_ref = refs[:nb], refs[nb]
        ti = pl.program_id(2)
        av = a_ref[...]
        if a_square:
            af = av.astype(F32)
            av = (af * af).astype(BF16)
        bv = b_refs[0][...] if nb == 1 else jnp.concatenate([r[...] for r in b_refs], axis=1)
        part = lax.dot_general(av, bv, TN, preferred_element_type=F32)

        @pl.when(ti == 0)
        def _():
            o_ref[...] = part

        @pl.when(ti > 0)
        def _():
            o_ref[...] += part

    if b_blocks:
        b_specs = [pl.BlockSpec((tt, tn // nb), functools.partial(lambda blk, i, j, ti: (ti, blk), blk))
                   for blk in b_blocks]
    else:
        b_specs = [pl.BlockSpec((tt, tn), lambda i, j, ti: (ti, j))]
    (out,), xres = _call(
        body, name=name, grid=(m // tm, n // tn, t // tt),
        in_specs=[pl.BlockSpec((tt, tm), lambda i, j, ti: (ti, i))] + b_specs,
        out_specs=[pl.BlockSpec((None, tm, tn), lambda i, j, ti: (j // per, i, j % per))],
        out_shape=[jax.ShapeDtypeStruct((n_split, m, n // n_split), F32)], args=[a] + [b] * nb,
        semantics=("parallel", "parallel", "arbitrary"), exchanges=exchanges)
    return (out, xres) if exchanges else out


def _rstd(x):
    return lax.rsqrt(jnp.mean(x * x, axis=-1, keepdims=True) + EPS)


def _rms_cast_gather(x, g, buf, *, tm, name):
    t, d = x.shape
    steps = t // tm

    def body(x_ref, g_ref, b_in, o_ref, b_out, send_sems, recv_sems):
        i = pl.program_id(0)
        xc, yc, c = _place()
        chips = _other_chips(xc, yc)

        def slot(px, py, pc):
            return b_out.at[4 * px + 2 * py + pc]

        def sent(j):
            return _remote(b_in.at[4 * xc + 2 * yc + c], slot(xc, yc, c), send_sems, recv_sems, j, (*chips[j], c))

        def passed(j):
            return _remote(slot(*chips[j], c), slot(*chips[j], c), send_sems, recv_sems, 3 + j, (xc, yc, 1 - c))

        @pl.when(i == 0)
        def _():
            for j in range(3):
                sent(j).start()

        xv = x_ref[...]
        o_ref[...] = (xv * _rstd(xv) * g_ref[...]).astype(BF16)

        @pl.when(i == steps - 1)
        def _():
            for j in range(3):
                sent(j).wait_recv()
                passed(j).start()
            for j in range(3):
                passed(j).wait_recv()
                passed(j).wait_send()
                sent(j).wait_send()

    return pl.pallas_call(
        body, name=name, grid=(steps,),
        in_specs=[pl.BlockSpec((tm, d), lambda i: (i, 0)), pl.BlockSpec((1, d), lambda i: (0, 0)), ANY],
        out_specs=[pl.BlockSpec((tm, d), lambda i: (i, 0)), ANY],
        out_shape=[jax.ShapeDtypeStruct((t, d), BF16), jax.ShapeDtypeStruct(buf.shape, buf.dtype)],
        scratch_shapes=[pltpu.SemaphoreType.DMA((6,)), pltpu.SemaphoreType.DMA((6,))],
        input_output_aliases={2: 1},
        compiler_params=_params(("arbitrary",)),
    )(x, g, buf)


def _mix_cat(attn, rnn, gain, *, tm, name):
    t = attn.shape[0]

    def body(a_ref, r_ref, g_ref, o_ref):
        av = a_ref[...].astype(F32)
        o_ref[:, :ATTN_W] = (av * _rstd(av) * g_ref[...]).astype(BF16)
        o_ref[:, ATTN_W:] = r_ref[...].astype(BF16)

    return pl.pallas_call(
        body, name=name, grid=(t // tm,),
        in_specs=[pl.BlockSpec((tm, ATTN_W), lambda i: (i, 0)), pl.BlockSpec((tm, RNN_W), lambda i: (i, 0)),
                  pl.BlockSpec((1, ATTN_W), lambda i: (0, 0))],
        out_specs=pl.BlockSpec((tm, D_MODEL), lambda i: (i, 0)),
        out_shape=jax.ShapeDtypeStruct((t, D_MODEL), BF16),
        compiler_params=_params(("parallel",)),
    )(attn, rnn, gain)


def _post_norm_res(mixed, g_post, res, g_next, *, tm, name, exchanges=()):
    t, d = mixed.shape

    def body(m_ref, gp_ref, r_ref, gn_ref, x1_ref, h2_ref):
        mv = m_ref[...].astype(F32)
        x1 = r_ref[...] + mv * _rstd(mv) * gp_ref[...]
        x1_ref[...] = x1.astype(BF16)
        h2_ref[...] = (x1 * _rstd(x1) * gn_ref[...]).astype(BF16)

    row = pl.BlockSpec((tm, d), lambda i: (i, 0))
    vec = pl.BlockSpec((1, d), lambda i: (0, 0))
    res_, xres = _call(
        body, name=name, grid=(t // tm,),
        in_specs=[row, vec, row, vec], out_specs=[row, row],
        out_shape=[jax.ShapeDtypeStruct((t, d), BF16), jax.ShapeDtypeStruct((t, d), BF16)],
        args=[mixed, g_post, res, g_next], semantics=("parallel",), exchanges=exchanges)
    return (*res_, xres) if exchanges else res_


def _rms_bwd(dyn, xin, g, res, *, tm, out_dtype, name, col_block=0, exchanges=()):
    t, d = xin.shape

    def body(*refs):
        if res is not None:
            dy_ref, x_ref, g_ref, r_ref, dx_ref, dg_ref = refs
        else:
            dy_ref, x_ref, g_ref, dx_ref, dg_ref = refs
        i = pl.program_id(0)
        xv = x_ref[...].astype(F32)
        dy = dy_ref[...].astype(F32)
        r = _rstd(xv)
        xh = xv * r
        part = jnp.sum(dy * xh, axis=0, keepdims=True)

        @pl.when(i == 0)
        def _():
            dg_ref[...] = part

        @pl.when(i > 0)
        def _():
            dg_ref[...] += part

        tt = dy * g_ref[...]
        dx = r * (tt - xh * jnp.mean(tt * xh, axis=-1, keepdims=True))
        if res is not None:
            dx = dx + r_ref[...].astype(F32)
        dx_ref[...] = dx.astype(out_dtype)

    row = pl.BlockSpec((tm, d), lambda i: (i, 0))
    vec = pl.BlockSpec((1, d), lambda i: (0, 0))
    in_specs = [pl.BlockSpec((tm, d), lambda i: (i, col_block)), row, vec]
    args = [dyn, xin, g]
    if res is not None:
        in_specs.append(row)
        args.append(res)
    res, xres = _call(
        body, name=name, grid=(t // tm,),
        in_specs=in_specs, out_specs=[row, vec],
        out_shape=[jax.ShapeDtypeStruct((t, d), out_dtype), jax.ShapeDtypeStruct((1, d), F32)], args=args,
        semantics=("arbitrary",), exchanges=exchanges)
    return (*res, xres) if exchanges else res


def _loss_head(y, g_post, x1, target, *, tm, name):
    t, d = y.shape

    def body(y_ref, g_ref, x1_ref, t_ref, dy_ref, dx2_ref, loss_ref, dg_ref):
        i = pl.program_id(0)
        yv = y_ref[...].astype(F32)
        r = _rstd(yv)
        yh = yv * r
        gv = g_ref[...]
        err = x1_ref[...].astype(F32) + yh * gv - t_ref[...]
        lpart = 0.5 * jnp.sum(jnp.mean(err * err, axis=-1, keepdims=True), axis=0, keepdims=True)
        dx2 = err * (1.0 / d)
        dgp = jnp.sum(dx2 * yh, axis=0, keepdims=True)
        lane = lax.broadcasted_iota(jnp.int32, (1, 128), 1)
        lrow = jnp.where(lane == 0, lpart, 0.0)

        @pl.when(i == 0)
        def _():
            dg_ref[...] = dgp
            loss_ref[...] = lrow

        @pl.when(i > 0)
        def _():
            dg_ref[...] += dgp
            loss_ref[...] += lrow

        tt = dx2 * gv
        dy_ref[...] = (r * (tt - yh * jnp.mean(tt * yh, axis=-1, keepdims=True))).astype(BF16)
        dx2_ref[...] = dx2.astype(BF16)

    row = pl.BlockSpec((tm, d), lambda i: (i, 0))
    vec = pl.BlockSpec((1, d), lambda i: (0, 0))
    return pl.pallas_call(
        body, name=name, grid=(t // tm,),
        in_specs=[row, vec, row, row],
        out_specs=[row, row, pl.BlockSpec((1, 128), lambda i: (0, 0)), vec],
        out_shape=[jax.ShapeDtypeStruct((t, d), BF16), jax.ShapeDtypeStruct((t, d), BF16),
                   jax.ShapeDtypeStruct((1, 128), F32), jax.ShapeDtypeStruct((1, d), F32)],
        compiler_params=_params(("arbitrary",)),
    )(y, g_post, x1, target)


def _alibi_slope(h):
    return 2.0 ** (-8.0 * (h + 1) / N_Q)


PAIR = 2 * HEAD_DIM
N_PAIRS = N_Q // 2
PAIRS_PER_KV = GROUP // 2
SMEM = pl.BlockSpec(memory_space=pltpu.SMEM)


def _swa_mask(n):
    key = lax.broadcasted_iota(jnp.int32, (2 * BLK, BLK), 0)
    qry = lax.broadcasted_iota(jnp.int32, (2 * BLK, BLK), 1)
    dist = qry + BLK - key
    valid = (dist >= 0) & (dist < BLK) & ((key >= BLK) | (n > 0))
    return valid, dist.astype(F32)


def _block_diag(kvp_ref, kvc_ref, off):
    a = jnp.concatenate([kvp_ref[:, off:off + HEAD_DIM], kvc_ref[:, off:off + HEAD_DIM]], axis=0).astype(BF16)
    z = jnp.zeros_like(a)
    return jnp.concatenate([jnp.concatenate([a, z], axis=1), jnp.concatenate([z, a], axis=1)], axis=0)


def _swa_scores(s2, e, hh, valid, distf):
    s = s2[2 * BLK * e:2 * BLK * (e + 1)] * (HEAD_DIM ** -0.5) - _alibi_slope(hh) * distf
    return jnp.where(valid, s, -1e30)


def _swa_fwd(proj, sinks, *, name, exchanges=()):
    t = proj.shape[0]
    nb = t // BLK
    kvb = KV_COL // (2 * 128)

    def body(sink_ref, q_ref, kvc_ref, kvp_ref, o_ref, lse_ref):
        n = pl.program_id(0)
        valid, distf = _swa_mask(n)
        for kvh in range(N_KV):
            k2 = _block_diag(kvp_ref, kvc_ref, kvh * HEAD_DIM)
            v2 = _block_diag(kvp_ref, kvc_ref, 128 + kvh * HEAD_DIM)
            for jp in range(PAIRS_PER_KV):
                pair = kvh * PAIRS_PER_KV + jp
                lanes = slice(pair * PAIR, (pair + 1) * PAIR)
                s2 = lax.dot_general(k2, q_ref[:, lanes].astype(BF16), NT, preferred_element_type=F32)
                probs = []
                for e in range(2):
                    hh = 2 * pair + e
                    s = _swa_scores(s2, e, hh, valid, distf)
                    sink = sink_ref[0, hh]
                    mx = jnp.maximum(jnp.max(s, axis=0, keepdims=True), sink)
                    p = jnp.exp(s - mx)
                    l = jnp.sum(p, axis=0, keepdims=True) + jnp.exp(sink - mx)
                    probs.append((p * (1.0 / l)).astype(BF16))
                    lse_ref[hh:hh + 1, :] = mx + jnp.log(l)
                o_ref[:, lanes] = lax.dot_general(jnp.concatenate(probs, axis=0), v2, TN,
                                                  preferred_element_type=F32).astype(BF16)

    res, xres = _call(
        body, name=name, grid=(nb,),
        in_specs=[SMEM,
                  pl.BlockSpec((BLK, ATTN_W), lambda n: (n, 0)),
                  pl.BlockSpec((BLK, 256), lambda n: (n, kvb)),
                  pl.BlockSpec((BLK, 256), lambda n: (jnp.maximum(n - 1, 0), kvb))],
        out_specs=[pl.BlockSpec((BLK, ATTN_W), lambda n: (n, 0)),
                   pl.BlockSpec((None, N_Q, BLK), lambda n: (n, 0, 0))],
        out_shape=[jax.ShapeDtypeStruct((t, ATTN_W), BF16), jax.ShapeDtypeStruct((nb, N_Q, BLK), F32)],
        args=[sinks, proj, proj, proj], semantics=("parallel",), exchanges=exchanges)
    return (*res, xres) if exchanges else res


def _swa_bwd(proj, sinks, dattn, lse, *, name, exchanges=()):
    t = proj.shape[0]
    nb = t // BLK
    kvb = KV_COL // (2 * 128)

    def body(sink_ref, q_ref, kvc_ref, kvp_ref, do_ref, lse_ref, dq_ref, dkv_ref, dsink_ref, carry_ref):
        n = pl.program_id(0)

        @pl.when(n == 0)
        def _():
            dsink_ref[...] = jnp.zeros_like(dsink_ref)
            carry_ref[...] = jnp.zeros_like(carry_ref)

        @pl.when(n < nb)
        def _():
            valid, distf = _swa_mask(n)
            for kvh in range(N_KV):
                k2 = _block_diag(kvp_ref, kvc_ref, kvh * HEAD_DIM)
                v2 = _block_diag(kvp_ref, kvc_ref, 128 + kvh * HEAD_DIM)
                dk2 = jnp.zeros((4 * BLK, PAIR), F32)
                dv2 = jnp.zeros((4 * BLK, PAIR), F32)
                for jp in range(PAIRS_PER_KV):
                    pair = kvh * PAIRS_PER_KV + jp
                    lanes = slice(pair * PAIR, (pair + 1) * PAIR)
                    q2 = q_ref[:, lanes].astype(BF16)
                    do2 = do_ref[:, lanes].astype(BF16)
                    s2 = lax.dot_general(k2, q2, NT, preferred_element_type=F32)
                    dp2 = lax.dot_general(v2, do2, NT, preferred_element_type=F32)
                    probs, dss = [], []
                    for e in range(2):
                        hh = 2 * pair + e
                        lse_h = lse_ref[hh:hh + 1, :]
                        p = jnp.exp(_swa_scores(s2, e, hh, valid, distf) - lse_h)
                        dp = dp2[2 * BLK * e:2 * BLK * (e + 1)]
                        delta = jnp.sum(p * dp, axis=0, keepdims=True)
                        dsink_ref[hh:hh + 1, :] += -jnp.exp(sink_ref[0, hh] - lse_h) * delta
                        probs.append(p.astype(BF16))
                        dss.append((p * (dp - delta)).astype(BF16))
                    ds2 = jnp.concatenate(dss, axis=0)
                    dq_ref[:, lanes] = (lax.dot_general(ds2, k2, TN, preferred_element_type=F32)
                                        * (HEAD_DIM ** -0.5)).astype(BF16)
                    dk2 = dk2 + jnp.dot(ds2, q2, preferred_element_type=F32)
                    dv2 = dv2 + jnp.dot(jnp.concatenate(probs, axis=0), do2, preferred_element_type=F32)
                dk_cat = (dk2[:2 * BLK, :HEAD_DIM] + dk2[2 * BLK:, HEAD_DIM:]) * (HEAD_DIM ** -0.5)
                dv_cat = dv2[:2 * BLK, :HEAD_DIM] + dv2[2 * BLK:, HEAD_DIM:]
                ko = kvh * HEAD_DIM
                vo = 128 + kvh * HEAD_DIM
                dkv_ref[:, ko:ko + HEAD_DIM] = (carry_ref[:, ko:ko + HEAD_DIM] + dk_cat[:BLK]).astype(BF16)
                dkv_ref[:, vo:vo + HEAD_DIM] = (carry_ref[:, vo:vo + HEAD_DIM] + dv_cat[:BLK]).astype(BF16)
                carry_ref[:, ko:ko + HEAD_DIM] = dk_cat[BLK:]
                carry_ref[:, vo:vo + HEAD_DIM] = dv_cat[BLK:]

        @pl.when(n == nb)
        def _():
            dkv_ref[...] = carry_ref[...].astype(BF16)

    last = nb - 1
    res, xres = _call(
        body, name=name, grid=(nb + 1,),
        in_specs=[SMEM,
                  pl.BlockSpec((BLK, ATTN_W), lambda n: (jnp.minimum(n, last), 0)),
                  pl.BlockSpec((BLK, 256), lambda n: (jnp.minimum(n, last), kvb)),
                  pl.BlockSpec((BLK, 256), lambda n: (jnp.maximum(jnp.minimum(n, last) - 1, 0), kvb)),
                  pl.BlockSpec((BLK, ATTN_W), lambda n: (jnp.minimum(n, last), 0)),
                  pl.BlockSpec((None, N_Q, BLK), lambda n: (jnp.minimum(n, last), 0, 0))],
        out_specs=[pl.BlockSpec((BLK, ATTN_W), lambda n: (jnp.minimum(n, last), 0)),
                   pl.BlockSpec((BLK, 256), lambda n: (jnp.maximum(n - 1, 0), 0)),
                   pl.BlockSpec((N_Q, BLK), lambda n: (0, 0))],
        out_shape=[jax.ShapeDtypeStruct((t, ATTN_W), BF16), jax.ShapeDtypeStruct((t, 256), BF16),
                   jax.ShapeDtypeStruct((N_Q, BLK), F32)],
        scratch_shapes=[pltpu.VMEM((BLK, 256), F32)],
        args=[sinks, proj, proj, proj, dattn, lse], semantics=("arbitrary",), exchanges=exchanges)
    return (*res, xres) if exchanges else res


def _cumsum_rows(x):
    n = x.shape[0]
    row = lax.broadcasted_iota(jnp.int32, x.shape, 0)
    s = 1
    while s < n:
        x = x + jnp.where(row >= s, pltpu.roll(x, s, axis=0), 0.0)
        s *= 2
    return x


def _rev_cumsum_rows(x):
    n = x.shape[0]
    row = lax.broadcasted_iota(jnp.int32, x.shape, 0)
    s = 1
    while s < n:
        x = x + jnp.where(row < n - s, pltpu.roll(x, n - s, axis=0), 0.0)
        s *= 2
    return x


def _lower_bound(lbl_ref):
    l0 = lbl_ref[0:1, :]
    l1 = lbl_ref[1:2, :]
    mx = jnp.maximum(l0, l1)
    e0 = jnp.exp(l0 - mx)
    e1 = jnp.exp(l1 - mx)
    return e0 / (e0 + e1)


def _hgrn_gates(z, lb):
    sg = _sigmoid(z)
    f = lb + (1.0 - lb) * sg
    return sg, f, jnp.log(f), 1.0 - f


def _sub_factors(b, k, i, sub, trim):
    need = -(-sub * i // 16) * 16 if trim else CHUNK
    rows = lax.broadcasted_iota(jnp.int32, (need, RNN_HD), 0)
    ref = b[sub * i - 1:sub * i, :]
    qfac = jnp.exp(b[sub * i:sub * (i + 1), :] - ref)
    kfac = jnp.where(rows < sub * i, jnp.exp(ref - b[:need]), 0.0)
    kt = (k[:need] * kfac).astype(BF16)
    if need < CHUNK:
        kt = jnp.concatenate([kt, jnp.zeros((CHUNK - need, RNN_HD), BF16)], axis=0)
    return qfac, kfac, kt


def _diag_decay(bi, s):
    trow = lax.broadcasted_iota(jnp.int32, bi.shape, 0)
    return jnp.where(trow >= s, jnp.exp(bi - bi[s:s + 1, :]), 0.0)


def _hgrn_fwd(proj, lb_logits, norm_gain, *, tb, name, exchanges=()):
    t = proj.shape[0]
    ntb = t // tb
    nch = tb // CHUNK
    qb, fb, ib, gb = QR_COL // 128, FR_COL // 128, IR_COL // 128, GR_COL // 128

    def body(q_ref, f_ref, i_ref, g_ref, lbl_ref, gain_ref, o_ref, out_ref, s0_ref, ksave_ref, bsave_ref, st_ref):
        c = pl.program_id(1)

        @pl.when(c == 0)
        def _():
            st_ref[...] = jnp.zeros_like(st_ref)

        lb = _lower_bound(lbl_ref)
        gain = gain_ref[...]

        def chunk(ci, st):
            rows = slice(ci * CHUNK, (ci + 1) * CHUNK)
            _, _, lf, k = _hgrn_gates(f_ref[rows, :], lb)
            qr = q_ref[rows, :]
            q = qr * _sigmoid(qr)
            v = i_ref[rows, :]
            b = _cumsum_rows(lf)
            ksave_ref[rows, :] = k
            bsave_ref[rows, :] = b
            s0_ref[ci] = st
            o_inter = lax.dot_general((q * jnp.exp(b)).astype(BF16), st.astype(BF16), NT,
                                      preferred_element_type=F32)
            vb = v.astype(BF16)
            blast = b[CHUNK - 1:CHUNK, :]
            khat = (k * jnp.exp(blast - b)).astype(BF16)
            st = st * jnp.exp(blast) + lax.dot_general(vb, khat, TN, preferred_element_type=F32)
            blocks = []
            for i in range(CHUNK // SUB_FWD):
                blk = slice(SUB_FWD * i, SUB_FWD * (i + 1))
                qi, ki, vi, bi = q[blk], k[blk], v[blk], b[blk]
                oi = o_inter[blk]
                if i > 0:
                    qfac, _, kt = _sub_factors(b, k, i, SUB_FWD, trim=True)
                    att = lax.dot_general((qi * qfac).astype(BF16), kt, NT,
                                          preferred_element_type=F32)
                    oi = oi + jnp.dot(att.astype(BF16), vb, preferred_element_type=F32)
                for s in range(SUB_FWD):
                    qe = qi * _diag_decay(bi, s)
                    a = jnp.sum(qe * ki[s:s + 1, :], axis=1, keepdims=True)
                    oi = oi + a * vi[s:s + 1, :]
                blocks.append(oi)
            o = jnp.concatenate(blocks, axis=0)
            o_ref[rows, :] = o
            gr = g_ref[rows, :]
            out_ref[rows, :] = (o * _rstd(o) * gain * (gr * _sigmoid(gr))).astype(BF16)
            return st

        st = st_ref[...]
        for ci in range(nch):
            st = chunk(ci, st)
        st_ref[...] = st

    def col(base):
        return pl.BlockSpec((tb, RNN_HD), lambda h, c: (c, base + h))

    res, xres = _call(
        body, name=name, grid=(N_RNN, ntb),
        in_specs=[col(qb), col(fb), col(ib), col(gb),
                  pl.BlockSpec((2, RNN_HD), lambda h, c: (0, h)), pl.BlockSpec((1, RNN_HD), lambda h, c: (0, 0))],
        out_specs=[pl.BlockSpec((tb, RNN_HD), lambda h, c: (c, h)), pl.BlockSpec((tb, RNN_HD), lambda h, c: (c, h)),
                   pl.BlockSpec((None, nch, RNN_HD, RNN_HD), lambda h, c: (h, c, 0, 0)),
                   pl.BlockSpec((tb, RNN_HD), lambda h, c: (c, h)), pl.BlockSpec((tb, RNN_HD), lambda h, c: (c, h))],
        out_shape=[jax.ShapeDtypeStruct((t, RNN_W), F32), jax.ShapeDtypeStruct((t, RNN_W), BF16),
                   jax.ShapeDtypeStruct((N_RNN, t // CHUNK, RNN_HD, RNN_HD), F32),
                   jax.ShapeDtypeStruct((t, RNN_W), F32), jax.ShapeDtypeStruct((t, RNN_W), F32)],
        scratch_shapes=[pltpu.VMEM((RNN_HD, RNN_HD), F32)],
        args=[proj, proj, proj, proj, lb_logits, norm_gain],
        semantics=("parallel", "arbitrary"), exchanges=exchanges)
    return (*res, xres) if exchanges else res


def _hgrn_bwd(proj, lb_logits, norm_gain, o_pre, s0, k_gate, b_cum, dcat, *, tb, name, exchanges=()):
    t = proj.shape[0]
    ntb = t // tb
    nch = tb // CHUNK
    qb, fb, ib, gb = QR_COL // 128, FR_COL // 128, IR_COL // 128, GR_COL // 128
    sub = SUB_BWD
    nsub = CHUNK // sub

    def body(q_ref, k_ref, b_ref, i_ref, g_ref, lbl_ref, gain_ref, o_ref, s0_ref, dout_ref,
             dq_ref, df_ref, di_ref, dg_ref, dlb_ref, dgain_ref,
             dst_ref, dqs_ref, dks_ref, dvs_ref):
        c = pl.program_id(1)

        @pl.when(c == 0)
        def _():
            dst_ref[...] = jnp.zeros_like(dst_ref)
            dlb_ref[...] = jnp.zeros_like(dlb_ref)
            dgain_ref[...] = jnp.zeros_like(dgain_ref)

        lb = _lower_bound(lbl_ref)
        inv_1mlb = 1.0 / (1.0 - lb)
        gain = gain_ref[...]

        def chunk(ci, dst):
            rows = slice(ci * CHUNK, (ci + 1) * CHUNK)
            dqa_ref, dka_ref, dva_ref = dqs_ref.at[ci], dks_ref.at[ci], dvs_ref.at[ci]
            k = k_ref[rows, :]
            b = b_ref[rows, :]
            f = 1.0 - k
            one_minus_sg = k * inv_1mlb
            qr = q_ref[rows, :]
            sq = _sigmoid(qr)
            q = qr * sq
            v = i_ref[rows, :]

            dout = dout_ref[rows, :].astype(F32)
            o = o_ref[rows, :]
            gr = g_ref[rows, :]
            sgg = _sigmoid(gr)
            gate = gr * sgg
            rs = _rstd(o)
            nrm = o * rs
            dg_ref[rows, :] = (dout * nrm * gain * (sgg * (1.0 + gr * (1.0 - sgg)))).astype(BF16)
            dn = dout * gate
            dgain_ref[...] += jnp.sum(dn * nrm, axis=0, keepdims=True)
            tt = dn * gain
            do = rs * (tt - nrm * jnp.mean(tt * nrm, axis=-1, keepdims=True))

            dob = do.astype(BF16)
            vb = v.astype(BF16)
            eb = jnp.exp(b)
            blast = b[CHUNK - 1:CHUNK, :]
            ebl = jnp.exp(blast - b)
            dstb = dst.astype(BF16)
            khat = (k * ebl).astype(BF16)
            s0 = s0_ref[ci]
            dqa_ref[...] = eb * jnp.dot(dob, s0.astype(BF16), preferred_element_type=F32)
            dk_state = ebl * jnp.dot(vb, dstb, preferred_element_type=F32)
            dka_ref[...] = dk_state
            d_blast = (jnp.sum(k * dk_state, axis=0, keepdims=True)
                       + jnp.exp(blast) * jnp.sum(dst * s0, axis=0, keepdims=True))
            dva_ref[...] = lax.dot_general(khat, dstb, NT, preferred_element_type=F32)
            dst_next = dst * jnp.exp(blast) + lax.dot_general(dob, (q * eb).astype(BF16), TN,
                                                              preferred_element_type=F32)
            pm = lax.dot_general(dob, vb, NT, preferred_element_type=F32)
            for i in range(nsub):
                blk = slice(sub * i, sub * (i + 1))
                qi, ki, vi, bi, doi = q[blk], k[blk], v[blk], b[blk], do[blk]
                dqi = dqa_ref[blk, :]
                if i > 0:
                    qfac, kfac, kt = _sub_factors(b, k, i, sub, trim=False)
                    qt = (qi * qfac).astype(BF16)
                    att = lax.dot_general(qt, kt, NT, preferred_element_type=F32).astype(BF16)
                    pmi = pm[blk, :].astype(BF16)
                    dva_ref[...] += lax.dot_general(att, doi.astype(BF16), TN, preferred_element_type=F32)
                    dqi = dqi + qfac * jnp.dot(pmi, kt, preferred_element_type=F32)
                    dka_ref[...] += kfac * lax.dot_general(pmi, qt, TN, preferred_element_type=F32)
                dqa_ref[blk, :] = dqi
                srow = lax.broadcasted_iota(jnp.int32, (sub, RNN_HD), 0)
                dki = jnp.zeros((sub, RNN_HD), F32)
                dvi = jnp.zeros((sub, RNN_HD), F32)
                for tq in range(sub):
                    qt, dot_ = qi[tq:tq + 1, :], doi[tq:tq + 1, :]
                    e = jnp.where(srow <= tq, jnp.exp(bi[tq:tq + 1, :] - bi), 0.0)
                    ke = ki * e
                    p = jnp.sum(vi * dot_, axis=1, keepdims=True)
                    a = jnp.sum(ke * qt, axis=1, keepdims=True)
                    dki = dki + p * (qt * e)
                    dvi = dvi + a * dot_
                    row = slice(sub * i + tq, sub * i + tq + 1)
                    dqa_ref[row, :] += jnp.sum(p * ke, axis=0, keepdims=True)
                dka_ref[blk, :] += dki
                dva_ref[blk, :] += dvi

            dq = dqa_ref[...]
            dk = dka_ref[...]
            lastrow = lax.broadcasted_iota(jnp.int32, (CHUNK, RNN_HD), 0) == CHUNK - 1
            dlf = _rev_cumsum_rows(q * dq - k * dk + jnp.where(lastrow, d_blast, 0.0))
            dff = dlf / f - dk
            df_ref[rows, :] = (dff * k * (1.0 - one_minus_sg)).astype(BF16)
            dlb_ref[...] += jnp.sum(dff * one_minus_sg, axis=0, keepdims=True)
            dq_ref[rows, :] = (dq * (sq * (1.0 + qr * (1.0 - sq)))).astype(BF16)
            di_ref[rows, :] = dva_ref[...].astype(BF16)
            return dst_next

        dst = dst_ref[...]
        for ci in reversed(range(nch)):
            dst = chunk(ci, dst)
        dst_ref[...] = dst

    def col(base):
        return pl.BlockSpec((tb, RNN_HD), lambda h, c: (ntb - 1 - c, base + h))

    outc = pl.BlockSpec((tb, RNN_HD), lambda h, c: (ntb - 1 - c, h))
    hb = ATTN_W // RNN_HD
    res, xres = _call(
        body, name=name, grid=(N_RNN, ntb),
        in_specs=[col(qb), outc, outc, col(ib), col(gb),
                  pl.BlockSpec((2, RNN_HD), lambda h, c: (0, h)), pl.BlockSpec((1, RNN_HD), lambda h, c: (0, 0)),
                  outc,
                  pl.BlockSpec((None, nch, RNN_HD, RNN_HD), lambda h, c: (h, ntb - 1 - c, 0, 0)),
                  pl.BlockSpec((tb, RNN_HD), lambda h, c: (ntb - 1 - c, hb + h))],
        out_specs=[outc, outc, outc, outc,
                   pl.BlockSpec((1, RNN_HD), lambda h, c: (0, h)),
                   pl.BlockSpec((None, 1, RNN_HD), lambda h, c: (h, 0, 0))],
        out_shape=[jax.ShapeDtypeStruct((t, RNN_W), BF16)] * 4
        + [jax.ShapeDtypeStruct((1, RNN_W), F32), jax.ShapeDtypeStruct((N_RNN, 1, RNN_HD), F32)],
        scratch_shapes=[pltpu.VMEM((RNN_HD, RNN_HD), F32),
                        pltpu.VMEM((nch, CHUNK, RNN_HD), F32), pltpu.VMEM((nch, CHUNK, RNN_HD), F32),
                        pltpu.VMEM((nch, CHUNK, RNN_HD), F32)],
        args=[proj, k_gate, b_cum, proj, proj, lb_logits, norm_gain, o_pre, s0, dcat],
        semantics=("parallel", "arbitrary"), exchanges=exchanges)
    return (*res, xres) if exchanges else res


def _cast_slots(w, where, *, name):
    _, rows, cols = w.shape
    rh = rows // 2
    tr = _row_tile(rh, cols)
    nh = rh // tr

    def body(wh_ref, w_ref, o_ref):
        o_ref[...] = w_ref[...].astype(BF16)

    return pl.pallas_call(
        body, name=name,
        grid_spec=pltpu.PrefetchScalarGridSpec(
            num_scalar_prefetch=1, grid=(2, nh),
            in_specs=[pl.BlockSpec((None, tr, cols), lambda h, i, wh: (0, h * nh + i, 0))],
            out_specs=pl.BlockSpec((None, tr, cols), lambda h, i, wh: (2 * wh[0] + h, i, 0))),
        out_shape=jax.ShapeDtypeStruct((8, rh, cols), BF16),
        compiler_params=_params(("parallel", "parallel")),
    )(where, w)


def _row_tile(rows, cols, budget=1 << 20):
    tr = rows
    while tr * cols > budget and tr % 16 == 0:
        tr //= 2
    return tr


def _half_spec(g, tr, halves_last, slab):
    if halves_last:
        return pl.BlockSpec((None, tr, g.shape[2] // 2), lambda *a: (slab(*a), a[-2], a[-1][1]))
    return pl.BlockSpec((None, None, tr, g.shape[3]), lambda *a: (slab(*a), a[-1][1], a[-2], 0))


def _pair_sum(g, sib, where, *, name, halves_last=False):
    rh, cols = sib.shape[1:]
    tr = _row_tile(rh, cols)

    def body(w_ref, g_ref, s_ref, o_ref):
        o_ref[...] = (g_ref[...] + s_ref[...]).astype(BF16)

    def foreign(s, i, w):
        return (w[0] + 1 + s) % N_CHIPS

    return pl.pallas_call(
        body, name=name,
        grid_spec=pltpu.PrefetchScalarGridSpec(
            num_scalar_prefetch=1, grid=(N_CHIPS - 1, rh // tr),
            in_specs=[_half_spec(g, tr, halves_last, foreign),
                      pl.BlockSpec((None, tr, cols), lambda s, i, w: (foreign(s, i, w), i, 0))],
            out_specs=pl.BlockSpec((None, tr, cols), lambda s, i, w: (foreign(s, i, w), i, 0))),
        out_shape=jax.ShapeDtypeStruct((4, rh, cols), BF16),
        compiler_params=_params(("parallel", "parallel")),
    )(where, g, sib)


def _final_half(g, sib, recv, where, *, name, halves_last=False):
    rh, cols = sib.shape[1:]
    tr = _row_tile(rh, cols)

    def body(w_ref, g_ref, s_ref, r_ref, o_ref):
        acc = g_ref[...] + s_ref[...]
        for j in range(3):
            acc = acc + r_ref[j].astype(F32)
        o_ref[...] = acc

    return pl.pallas_call(
        body, name=name,
        grid_spec=pltpu.PrefetchScalarGridSpec(
            num_scalar_prefetch=1, grid=(rh // tr,),
            in_specs=[_half_spec(g, tr, halves_last, lambda i, w: w[0]),
                      pl.BlockSpec((None, tr, cols), lambda i, w: (w[0], i, 0)),
                      pl.BlockSpec((3, tr, cols), lambda i, w: (0, i, 0))],
            out_specs=pl.BlockSpec((tr, cols), lambda i, w: (i, 0))),
        out_shape=jax.ShapeDtypeStruct((rh, cols), F32),
        compiler_params=_params(("parallel",)),
    )(where, g, sib, recv)


def _adamw_math(w, g, m, v):
    m = ADAM_B1 * m + (1.0 - ADAM_B1) * g
    v = ADAM_B2 * v + (1.0 - ADAM_B2) * (g * g)
    m_hat = m / (1.0 - ADAM_B1 ** ADAM_STEP)
    v_hat = v / (1.0 - ADAM_B2 ** ADAM_STEP)
    delta = -ADAM_LR * (m_hat / (jnp.sqrt(v_hat) + ADAM_EPS) + ADAM_WD * w)
    return delta, m, v


def _adamw(w, mine, theirs, m, v, where, *, name, halves_last=False):
    _, rows, cols = w.shape
    if halves_last:
        cols //= 2
        tr = _row_tile(rows, cols, budget=1 << 19)
        grid = (rows // tr, 2)
        blk = pl.BlockSpec((None, tr, cols), lambda i, h, wh: (0, i, h))
        mine_spec = theirs_spec = pl.BlockSpec((tr, cols), lambda i, h, wh: (i, 0))
        which = lambda: pl.program_id(1)
    else:
        tr = _row_tile(rows // 2, cols, budget=1 << 19)
        nh = rows // 2 // tr
        grid = (rows // tr,)
        blk = pl.BlockSpec((None, tr, cols), lambda i, wh: (0, i, 0))
        mine_spec = pl.BlockSpec((tr, cols), lambda i, wh: (jnp.where(i // nh == wh[1], i % nh, 0), 0))
        theirs_spec = pl.BlockSpec((tr, cols), lambda i, wh: (jnp.where(i // nh == wh[1], 0, i % nh), 0))
        which = lambda: pl.program_id(0) // nh

    def body(wh_ref, w_ref, a_ref, b_ref, m_ref, v_ref, g_ref, d_ref, nm_ref, nv_ref):
        g = jnp.where(which() == wh_ref[1], a_ref[...], b_ref[...])
        d, nm, nv = _adamw_math(w_ref[...], g, m_ref[...], v_ref[...])
        g_ref[...] = g
        d_ref[...] = d
        nm_ref[...] = nm
        nv_ref[...] = nv

    rows, cols = w.shape[1:]
    return pl.pallas_call(
        body, name=name,
        grid_spec=pltpu.PrefetchScalarGridSpec(
            num_scalar_prefetch=1, grid=grid,
            in_specs=[blk, mine_spec, theirs_spec, blk, blk], out_specs=[blk] * 4),
        out_shape=[jax.ShapeDtypeStruct((1, rows, cols), F32)] * 4,
        compiler_params=_params(("parallel",) * len(grid)),
    )(where, w, mine, theirs, m, v)


SEG_LOSS = 0
SEG_SINK = 128
SEG_AGAIN = 256
SEG_L0 = SEG_AGAIN + ATTN_W
SEG_L1 = SEG_L0 + RNN_W
SEG_RGAIN = SEG_L1 + RNN_W
SEG_G = SEG_RGAIN + 128
N_PACK = SEG_G + 4 * D_MODEL


def _pack(sinks, again, l0, l1, rgain, gains, loss=None):
    z = lambda k: jnp.zeros((1, k), F32)
    first = z(128) if loss is None else loss
    return jnp.concatenate([first, sinks, z(128 - N_Q), again, l0, l1, rgain] + list(gains), axis=1)


def _small_reduce_adamw(part, w, m, v, *, name):
    def body(p_ref, w_ref, m_ref, v_ref, g_ref, d_ref, nm_ref, nv_ref, buf_ref, send_sems, recv_sems):
        x, y, c = _place()
        me = 4 * x + 2 * y + c
        copies = []
        for k in range(1, 8):
            dx, dy, dc = (k >> 2) & 1, (k >> 1) & 1, k & 1
            to = (x ^ dx, y ^ dy, c ^ dc)
            cp = pltpu.make_async_remote_copy(
                src_ref=p_ref, dst_ref=buf_ref.at[me],
                send_sem=send_sems.at[k - 1], recv_sem=recv_sems.at[k - 1],
                device_id=to, device_id_type=MESH)
            cp.start()
            copies.append(cp)
        buf_ref[me] = p_ref[...]
        for cp in copies:
            cp.wait()
        tot = buf_ref[0]
        for j in range(1, 8):
            tot = tot + buf_ref[j]
        g_ref[...] = tot
        l0 = w_ref[:, SEG_L0:SEG_L0 + RNN_W]
        l1 = w_ref[:, SEG_L1:SEG_L1 + RNN_W]
        mx = jnp.maximum(l0, l1)
        e0 = jnp.exp(l0 - mx)
        e1 = jnp.exp(l1 - mx)
        lb = e0 / (e0 + e1)
        gl0 = tot[:, SEG_L0:SEG_L0 + RNN_W] * lb * (1.0 - lb)
        g_ref[:, SEG_L0:SEG_L0 + RNN_W] = gl0
        g_ref[:, SEG_L1:SEG_L1 + RNN_W] = -gl0
        d, nm, nv = _adamw_math(w_ref[...], g_ref[...], m_ref[...], v_ref[...])
        d_ref[...] = d
        nm_ref[...] = nm
        nv_ref[...] = nv

    vm = pl.BlockSpec(memory_space=pltpu.VMEM)
    return pl.pallas_call(
        body, name=name,
        in_specs=[vm] * 4, out_specs=[vm] * 4,
        out_shape=[jax.ShapeDtypeStruct((1, N_PACK), F32)] * 4,
        scratch_shapes=[pltpu.VMEM((8, 1, N_PACK), F32), pltpu.SemaphoreType.DMA((7,)),
                        pltpu.SemaphoreType.DMA((7,))],
    )(part, w, m, v)


def _layer_grads(xs, tgt, bufs, where, sinks, again, lb_logits, rgain,
                 g_mix_pre, g_mix_post, g_mlp_pre, g_mlp_post):
    tm = 512
    b_in, b_out, b_up, b_dn = bufs

    shard = IN_W // N_CHIPS
    h1, b_in = _rms_cast_gather(xs, g_mix_pre, b_in, tm=tm, name="h1_norm_gather_w_in")
    w_in_t = b_in.reshape(IN_W, D_MODEL)
    proj, ((b_out, b_up),) = _mm(
        h1, w_in_t, tm=1024, tn=768, tk=D_MODEL, out_dtype=F32, w_layout="nk", name="in_proj",
        exchanges=[_x_gather([b_out, b_up], ici=[(0, 256), (0, 336)])])
    attn, lse, ((b_out, b_up),) = _swa_fwd(
        proj, sinks, name="swa_fwd",
        exchanges=[_x_gather([b_out, b_up], ici=[None, (336, 320)], d2d=[(0, 256), None])])
    w_out = b_out.reshape(D_MODEL, D_MODEL)
    o_pre, rnn, s0, k_gate, b_cum, ((b_up, b_dn),) = _hgrn_fwd(
        proj, lb_logits, rgain, tb=512, name="hgrn_fwd",
        exchanges=[_x_gather([b_up, b_dn], ici=[(656, 368), (0, 400)])])
    cat = _mix_cat(attn, rnn, again, tm=tm, name="mix_cat")
    mixed, ((b_up, b_dn),) = _mm(
        cat, w_out, tm=1024, tn=1024, tk=D_MODEL, out_dtype=BF16, name="out_proj",
        exchanges=[_x_gather([b_up, b_dn], ici=[None, (400, 240)], d2d=[(0, 1024), (0, 400)])])
    w_up4 = b_up.reshape(N_CHIPS, D_MODEL, D_FF // N_CHIPS)
    x1, h2, ((b_dn,),) = _post_norm_res(
        mixed, g_mix_post, xs, g_mlp_pre, tm=tm, name="mix_post",
        exchanges=[_x_gather([b_dn], d2d=[(400, 240)])])
    u, ((b_dn,),) = _mm(h2, w_up4, tm=1024, tn=1024, tk=D_MODEL, out_dtype=BF16, relu=True, w_layout="skn",
                        name="mlp_up", exchanges=[_x_gather([b_dn], ici=[(640, 384)], cross=[(640, 384)])])
    w_dn = b_dn.reshape(D_FF, D_MODEL)
    yv = _mm(u, w_dn, tm=1024, tn=1024, tk=2048, out_dtype=BF16, a_square=True, name="mlp_down")
    dy, dx2, loss_row, dg_mlp_post = _loss_head(yv, g_mlp_post, x1, tgt, tm=tm, name="loss_head")

    def halved(g):
        return g.reshape(N_CHIPS, 2, g.shape[1] // 2, g.shape[2])
    du = _mm(dy, w_dn, tm=1024, tn=1024, tk=D_MODEL, out_dtype=BF16, mul2=u, w_layout="nk", name="mlp_down_bwd")
    g_dn = halved(_mm_tn(u, dy, tm=1024, tn=1024, tt=2048, a_square=True, name="w_down_grad")
                  .reshape(N_CHIPS, D_FF // N_CHIPS, D_MODEL))
    d_w_up, ((sib_dn,),) = _mm_tn(h2, du, tm=1024, tn=1024, tt=2048, n_split=N_CHIPS, name="w_up_grad",
                                  exchanges=[_x_pair([g_dn])])
    g_up = halved(d_w_up)
    wire_dn = _pair_sum(g_dn, sib_dn, where, name="pair_sum_w_down")
    dh2, ((recv_dn,), (sib_up,)) = _mm(du, w_up4, tm=1024, tn=1024, tk=2048, out_dtype=BF16, w_layout="snk", name="mlp_up_bwd",
                                       exchanges=[_x_chip([wire_dn], rows=[(0, 928)]), _x_pair([g_up])])
    wire_up = _pair_sum(g_up, sib_up, where, name="pair_sum_w_up")
    dx1, dg_mlp_pre = _rms_bwd(dh2, x1, g_mlp_pre, dx2, tm=tm, out_dtype=BF16, name="mlp_pre_bwd")
    dmixed, dg_mix_post = _rms_bwd(dx1, mixed, g_mix_post, None, tm=tm, out_dtype=BF16, name="mix_post_bwd")
    d_w_out, ((recv_dn,),) = _mm_tn(cat, dmixed, tm=1024, tn=1024, tt=2048, name="w_out_grad",
                                    exchanges=[_x_chip([wire_dn], rows=[(928, 96)], into=[recv_dn])])
    fin_dn = _final_half(g_dn, sib_dn, recv_dn, where, name="final_half_w_down")
    g_out = halved(d_w_out.reshape(N_CHIPS, D_MODEL // N_CHIPS, D_MODEL))
    dcat, ((sib_out,), (oth_dn,)) = _mm(dmixed, w_out, tm=1024, tn=1024, tk=D_MODEL, out_dtype=BF16, w_layout="nk",
                                        name="out_proj_bwd", exchanges=[_x_pair([g_out]), _x_share([fin_dn])])
    wire_out = _pair_sum(g_out, sib_out, where, name="pair_sum_w_out")
    dattn, dg_again = _rms_bwd(dcat, attn, again, None, tm=tm, out_dtype=BF16, name="attn_norm_bwd")
    dq_a, dkv, dsinks, ((recv_up,),) = _swa_bwd(
        proj, sinks, dattn, lse, name="swa_bwd", exchanges=[_x_chip([wire_up], rows=[(0, 512)])])
    dq_r, df_r, di_r, dg_r, dlb, dgain_h, ((recv_up,), (recv_out,)) = _hgrn_bwd(
        proj, lb_logits, rgain, o_pre, s0, k_gate, b_cum, dcat, tb=512, name="hgrn_bwd",
        exchanges=[_x_chip([wire_up], rows=[(512, 512)], into=[recv_up]), _x_chip([wire_out])])
    fin_up = _final_half(g_up, sib_up, recv_up, where, name="final_half_w_up")
    fin_out = _final_half(g_out, sib_out, recv_out, where, name="final_half_w_out")
    dproj = jnp.concatenate([dq_a, dkv, dq_r, df_r, di_r, dg_r], axis=1)
    piece_cols = D_MODEL // 4

    def w_in_piece(pc, exchanges):
        d, xres = _mm_tn(dproj, h1, tm=896, tn=2 * piece_cols, tt=2048, b_blocks=(pc, pc + 2),
                         name="w_in_grad_%d" % pc, exchanges=exchanges)
        return d.reshape(N_CHIPS, shard, 2 * piece_cols), xres

    g_in0, ((oth_up, oth_out),) = w_in_piece(0, [_x_share([fin_up, fin_out])])
    g_in1, ((sib_in0,),) = w_in_piece(1, [_x_pair([g_in0], halves_last=True)])
    wire_in0 = _pair_sum(g_in0, sib_in0, where, name="pair_sum_w_in_0", halves_last=True)
    dh1, ((recv_in0,), (sib_in1,)) = _mm(
        dproj, w_in_t, tm=1024, tn=1024, tk=2688, out_dtype=BF16, m_blocks=(0, 2), name="in_proj_bwd_0",
        exchanges=[_x_chip([wire_in0]), _x_pair([g_in1], halves_last=True)])
    wire_in1 = _pair_sum(g_in1, sib_in1, where, name="pair_sum_w_in_1", halves_last=True)
    dh1, ((recv_in1,),) = _mm(
        dproj, w_in_t, tm=1024, tn=1024, tk=2688, out_dtype=BF16, m_blocks=(2, 2), out_into=dh1,
        name="in_proj_bwd_1", exchanges=[_x_chip([wire_in1])])
    gx, dg_mix_pre = _rms_bwd(dh1, xs, g_mix_pre, dx1, tm=tm, out_dtype=F32, name="mix_pre_bwd")
    fin_in0 = _final_half(g_in0, sib_in0, recv_in0, where, name="final_half_w_in_0", halves_last=True)
    fin_in1 = _final_half(g_in1, sib_in1, recv_in1, where, name="final_half_w_in_1", halves_last=True)
    oth_in0, oth_in1 = _run_exchange(_x_share([fin_in0, fin_in1]), name="share_w_in")
    fin_in = jnp.concatenate([fin_in0, fin_in1], axis=1)
    oth_in = jnp.concatenate([oth_in0, oth_in1], axis=1)

    big = [(fin_in, oth_in), (fin_out, oth_out), (fin_up, oth_up), (fin_dn, oth_dn)]
    drgain = jnp.sum(dgain_h, axis=0)
    small = _pack(jnp.sum(dsinks, axis=1)[None, :], dg_again, dlb, jnp.zeros_like(dlb), drgain,
                  [dg_mix_pre, dg_mix_post, dg_mlp_pre, dg_mlp_post], loss=loss_row)
    return gx, big, small


def kernel(x, w_in, attn_sinks, attn_out_gain, rnn_lb_logits, rnn_norm_gain, w_out, mix_pre_gain, mix_post_gain, mlp_pre_gain, mlp_post_gain, w_up, w_down, loss_target, m_w_in, m_attn_sinks, m_attn_out_gain, m_rnn_lb_logits, m_rnn_norm_gain, m_w_out, m_mix_pre_gain, m_mix_post_gain, m_mlp_pre_gain, m_mlp_post_gain, m_w_up, m_w_down, v_w_in, v_attn_sinks, v_attn_out_gain, v_rnn_lb_logits, v_rnn_norm_gain, v_w_out, v_mix_pre_gain, v_mix_post_gain, v_mlp_pre_gain, v_mlp_post_gain, v_w_up, v_w_down):
    ax, ay, ac = _place()
    where = jnp.stack([2 * ax + ay, ac]).astype(jnp.int32)
    t = lambda a: jnp.swapaxes(a, 1, 2)
    big_w = [t(w_in), w_out, w_up, w_down]
    big_m = [t(m_w_in), m_w_out, m_w_up, m_w_down]
    big_v = [t(v_w_in), v_w_out, v_w_up, v_w_down]

    names = ["w_in", "w_out", "w_up", "w_down"]
    bufs = [_cast_slots(w, where, name="cast_" + nm) for w, nm in zip(big_w, names)]
    gx, big_g, small_part = _layer_grads(
        x[0], loss_target[0], bufs, where, attn_sinks, attn_out_gain, rnn_lb_logits, rnn_norm_gain,
        mix_pre_gain, mix_post_gain, mlp_pre_gain, mlp_post_gain)

    grads, deltas, new_m, new_v = [], [], [], []
    for (f, o), w, m, v, nm in zip(big_g, big_w, big_m, big_v, names):
        res = _adamw(w, f, o, m, v, where, name="adamw_" + nm, halves_last=(nm == "w_in"))
        if nm == "w_in":
            res = [t(r) for r in res]
        g, d, nm_, nv_ = res
        grads.append(g)
        deltas.append(d)
        new_m.append(nm_)
        new_v.append(nv_)

    def pack_params(sinks, again, logits, rgain, gains):
        return _pack(sinks, again, logits[0:1], logits[1:2], rgain, gains)

    pw = pack_params(attn_sinks, attn_out_gain, rnn_lb_logits, rnn_norm_gain,
                     [mix_pre_gain, mix_post_gain, mlp_pre_gain, mlp_post_gain])
    pm = pack_params(m_attn_sinks, m_attn_out_gain, m_rnn_lb_logits, m_rnn_norm_gain,
                     [m_mix_pre_gain, m_mix_post_gain, m_mlp_pre_gain, m_mlp_post_gain])
    pv = pack_params(v_attn_sinks, v_attn_out_gain, v_rnn_lb_logits, v_rnn_norm_gain,
                     [v_mix_pre_gain, v_mix_post_gain, v_mlp_pre_gain, v_mlp_post_gain])
    packs = _small_reduce_adamw(small_part, pw, pm, pv, name="small_reduce_adamw")

    def unpack(p):
        seg = lambda o, k: p[:, o:o + k]
        logits = jnp.concatenate([seg(SEG_L0, RNN_W), seg(SEG_L1, RNN_W)], axis=0)
        gains = [seg(SEG_G + i * D_MODEL, D_MODEL) for i in range(4)]
        return dict(sinks=seg(SEG_SINK, N_Q), again=seg(SEG_AGAIN, ATTN_W), logits=logits,
                    rgain=seg(SEG_RGAIN, RNN_HD), gains=gains)

    def order(small, big):
        return [big[0], small["sinks"], small["again"], small["logits"], small["rgain"], big[1],
                *small["gains"], big[2], big[3]]

    loss = packs[0][0, 0]
    outs = [loss, gx[None]]
    for p, b in zip(packs, [grads, deltas, new_m, new_v]):
        outs += order(unpack(p), b)
    return tuple(outs)
```

```python
import functools

import jax
import jax.numpy as jnp
from jax import lax
from jax.experimental import pallas as pl
from jax.experimental.pallas import tpu as pltpu

F32 = jnp.float32
BF16 = jnp.bfloat16
MESH = pl.DeviceIdType.MESH

EPS = 1e-6
D_MODEL = 2048
ATTN_W = 1024
HEAD_DIM = 64
N_Q = 16
N_KV = 2
GROUP = 8
BLK = 128
RNN_W = 1024
RNN_HD = 128
N_RNN = 8
CHUNK = 64
SUB_FWD = 16
SUB_BWD = 8
D_FF = 8192
IN_W = 5376
N_CHIPS = 4
KV_COL = ATTN_W
QR_COL = ATTN_W + 2 * 128
FR_COL = QR_COL + RNN_W
IR_COL = FR_COL + RNN_W
GR_COL = IR_COL + RNN_W

ADAM_LR = 0.001
ADAM_B1 = 0.9
ADAM_B2 = 0.999
ADAM_EPS = 1e-08
ADAM_WD = 0.01
ADAM_STEP = 10

VMEM_LIMIT = 48 * 1024 * 1024

NT = (((1,), (1,)), ((), ()))
TN = (((0,), (0,)), ((), ()))


def _params(sem=None):
    return pltpu.CompilerParams(dimension_semantics=sem, vmem_limit_bytes=VMEM_LIMIT)


def _sigmoid(x):
    return 1.0 / (1.0 + jnp.exp(-x))


ANY = pl.BlockSpec(memory_space=pl.ANY)


def _place():
    return lax.axis_index("x"), lax.axis_index("y"), lax.axis_index("c")


def _other_chips(x, y):
    return [(1 - x, y), (x, 1 - y), (1 - x, 1 - y)]


class _Exchange:
    def __init__(self, srcs, outs, ncopy, build, aliases=None):
        self.srcs, self.outs, self.ncopy, self.build = list(srcs), list(outs), ncopy, build
        self.aliases = aliases or {}


def _remote(src, dst, send_sems, recv_sems, k, to):
    return pltpu.make_async_remote_copy(src_ref=src, dst_ref=dst, send_sem=send_sems.at[k],
                                        recv_sem=recv_sems.at[k], device_id=to, device_id_type=MESH)


def _call(body, *, name, grid, in_specs, out_specs, out_shape, args, scratch_shapes=(), semantics=None,
          exchanges=(), into=None):
    in_specs, out_specs, out_shape = list(in_specs), list(out_specs), list(out_shape)
    scratch_shapes = list(scratch_shapes)
    ni, no, ns = len(in_specs), len(out_specs), len(scratch_shapes)
    xsrc = [s for x in exchanges for s in x.srcs]
    xout = [o for x in exchanges for o in x.outs]
    into = into or {}
    xsrc += [into[k] for k in sorted(into)]
    nxi, nxo = len(xsrc), len(xout)
    aliases = {nxi - len(into) + ni + q: k for q, k in enumerate(sorted(into))}
    a0 = b0 = 0
    for x in exchanges:
        for si, oi in x.aliases.items():
            aliases[ni + a0 + si] = no + b0 + oi
        a0 += len(x.srcs)
        b0 += len(x.outs)
    sems = []
    for x in exchanges:
        sems += [pltpu.SemaphoreType.DMA((x.ncopy,)), pltpu.SemaphoreType.DMA((x.ncopy,))]

    def wrapped(*refs):
        ins, xi = refs[:ni], refs[ni:ni + nxi]
        outs, xo = refs[ni + nxi:ni + nxi + no], refs[ni + nxi + no:ni + nxi + no + nxo]
        rest = refs[ni + nxi + no + nxo:]
        scr, sm = rest[:ns], rest[ns:]

        def copies():
            cps = []
            a = b = 0
            for k, x in enumerate(exchanges):
                cps += x.build(xi[a:a + len(x.srcs)], xo[b:b + len(x.outs)], sm[2 * k], sm[2 * k + 1])
                a += len(x.srcs)
                b += len(x.outs)
            return cps

        def start():
            for cp in copies():
                cp.start()

        def wait():
            for cp in copies():
                cp.wait()

        if not exchanges:
            body(*ins, *outs, *scr)
        elif not grid:
            start()
            body(*ins, *outs, *scr)
            wait()
        else:
            first = last = None
            for ax, g in enumerate(grid):
                f = pl.program_id(ax) == 0
                l = pl.program_id(ax) == g - 1
                first = f if first is None else first & f
                last = l if last is None else last & l
            pl.when(first)(start)
            body(*ins, *outs, *scr)
            pl.when(last)(wait)

    if exchanges and semantics is not None:
        semantics = ("arbitrary",) * len(grid)
    kwargs = dict(grid=grid) if grid else {}
    res = pl.pallas_call(
        wrapped, name=name,
        in_specs=in_specs + [ANY] * nxi, out_specs=out_specs + [ANY] * nxo,
        out_shape=out_shape + xout, scratch_shapes=scratch_shapes + sems,
        input_output_aliases=aliases,
        compiler_params=_params(semantics), **kwargs,
    )(*args, *xsrc)
    res = list(res)
    mine, theirs = res[:no], res[no:]
    per = []
    b = 0
    for x in exchanges:
        per.append(theirs[b:b + len(x.outs)])
        b += len(x.outs)
    return mine, per


def _run_exchange(x, *, name):
    return _call(lambda: None, name=name, grid=(), in_specs=[], out_specs=[], out_shape=[], args=[],
                 exchanges=[x])[1][0]


def _x_gather(bufs, ici=None, d2d=None, cross=None):
    n = len(bufs)
    plan = [(a, kind, rows[a]) for a in range(n) for kind, rows in (("ici", ici), ("d2d", d2d), ("cross", cross))
            if rows is not None and rows[a] is not None]

    def build(srcs, outs, ss, rs):
        x, y, c = _place()
        cps = []
        for q, (a, kind, rows) in enumerate(plan):
            piece = pl.ds(*rows)
            for j, (px, py) in enumerate(_other_chips(x, y)):
                if kind == "d2d":
                    slot, to = 4 * px + 2 * py + c, (x, y, 1 - c)
                else:
                    slot, to = 4 * x + 2 * y + c, (px, py, c if kind == "ici" else 1 - c)
                cps.append(_remote(srcs[a].at[slot, piece], outs[a].at[slot, piece], ss, rs, 3 * q + j, to))
        return cps

    outs = [jax.ShapeDtypeStruct(b.shape, b.dtype) for b in bufs]
    return _Exchange(bufs, outs, 3 * len(plan), build, aliases={a: a for a in range(n)})


def _x_pair(grads, halves_last=False):
    n = len(grads)

    def build(srcs, outs, ss, rs):
        x, y, c = _place()

        def half(r):
            if not halves_last:
                return r.at[:, 1 - c]
            ch = r.shape[2] // 2
            return r.at[:, :, pl.ds(pl.multiple_of((1 - c) * ch, 128), ch)]

        return [_remote(half(srcs[a]), outs[a], ss, rs, a, (x, y, 1 - c)) for a in range(n)]

    if halves_last:
        outs = [jax.ShapeDtypeStruct(g.shape[:2] + (g.shape[2] // 2,), g.dtype) for g in grads]
    else:
        outs = [jax.ShapeDtypeStruct((4,) + g.shape[2:], g.dtype) for g in grads]
    return _Exchange(grads, outs, n, build)


def _x_chip(wires, rows=None, into=None):
    n = len(wires)
    rows = rows or [(0, w.shape[1]) for w in wires]

    def build(srcs, outs, ss, rs):
        x, y, c = _place()
        cps = []
        for a in range(n):
            piece = pl.ds(*rows[a])
            for j, (px, py) in enumerate(_other_chips(x, y)):
                cps.append(_remote(srcs[a].at[2 * px + py, piece], outs[a].at[j, piece], ss, rs,
                                   3 * a + j, (px, py, c)))
        return cps

    outs = [jax.ShapeDtypeStruct((3,) + w.shape[1:], w.dtype) for w in wires]
    if into is None:
        return _Exchange(wires, outs, 3 * n, build)
    return _Exchange(list(wires) + list(into), outs, 3 * n, build, aliases={n + a: a for a in range(n)})


def _x_share(halves):
    n = len(halves)

    def build(srcs, outs, ss, rs):
        x, y, c = _place()
        return [_remote(srcs[a], outs[a], ss, rs, a, (x, y, 1 - c)) for a in range(n)]

    outs = [jax.ShapeDtypeStruct(h.shape, h.dtype) for h in halves]
    return _Exchange(halves, outs, n, build)


def _mm(a, w, *, tm, tn, tk, out_dtype, name, a_square=False, relu=False, mul2=None, w_layout="kn",
        m_blocks=None, out_into=None, exchanges=()):
    m, k = a.shape
    m_first, m_count = m_blocks or (0, m // tm)
    a_spec = pl.BlockSpec((tm, tk), lambda i, j, kk: (i + m_first, kk))
    if w_layout == "kn":
        n = w.shape[1]
        w_spec = pl.BlockSpec((tk, tn), lambda i, j, kk: (kk, j))
    elif w_layout == "nk":
        n = w.shape[0]
        w_spec = pl.BlockSpec((tn, tk), lambda i, j, kk: (j, kk))
    elif w_layout == "skn":
        n = w.shape[0] * w.shape[2]
        per_n = w.shape[2] // tn
        w_spec = pl.BlockSpec((None, tk, tn), lambda i, j, kk: (j // per_n, kk, j % per_n))
    else:
        assert w_layout == "snk"
        n = w.shape[1]
        per_k = w.shape[2] // tk
        w_spec = pl.BlockSpec((None, tn, tk), lambda i, j, kk: (kk // per_k, j, kk % per_k))
    w_dims = NT if w_layout in ("nk", "snk") else (((1,), (0,)), ((), ()))
    nk = k // tk
    assert m % tm == 0 and n % tn == 0 and k % tk == 0

    def body(*refs):
        if mul2 is not None:
            a_ref, w_ref, e_ref, o_ref, acc_ref = refs
        else:
            a_ref, w_ref, o_ref, acc_ref = refs
            e_ref = None
        kk = pl.program_id(2)
        av = a_ref[...]
        if a_square:
            af = av.astype(F32)
            av = (af * af).astype(BF16)
        part = lax.dot_general(av, w_ref[...], w_dims, preferred_element_type=F32)

        def finish(r):
            if relu:
                r = jnp.maximum(r, 0.0)
            if e_ref is not None:
                r = 2.0 * e_ref[...].astype(F32) * r
            o_ref[...] = r.astype(out_dtype)

        if nk == 1:
            finish(part)
        else:
            @pl.when(kk == 0)
            def _():
                acc_ref[...] = part

            @pl.when(kk > 0)
            def _():
                acc_ref[...] += part

            @pl.when(kk == nk - 1)
            def _():
                finish(acc_ref[...])

    in_specs = [a_spec, w_spec]
    args = [a, w]
    if mul2 is not None:
        in_specs.append(pl.BlockSpec((tm, tn), lambda i, j, kk: (i + m_first, j)))
        args.append(mul2)
    acc_shape = (tm, tn) if nk > 1 else (8, 128)
    (out,), per = _call(
        body, name=name, grid=(m_count, n // tn, nk),
        in_specs=in_specs, out_specs=[pl.BlockSpec((tm, tn), lambda i, j, kk: (i + m_first, j))],
        out_shape=[jax.ShapeDtypeStruct((m, n), out_dtype)], args=args,
        scratch_shapes=[pltpu.VMEM(acc_shape, F32)],
        semantics=("parallel", "parallel", "arbitrary"), exchanges=exchanges,
        into=None if out_into is None else {0: out_into})
    return (out, per) if exchanges else out


def _mm_tn(a, b, *, tm, tn, tt, name, a_square=False, n_split=1, b_blocks=None, exchanges=()):
    t, m = a.shape
    nb = len(b_blocks) if b_blocks else 1
    n = tn if b_blocks else b.shape[1]
    assert t % tt == 0 and m % tm == 0 and n % tn == 0 and (n // n_split) % tn == 0
    per = n // n_split // tn

    def body(a_ref, *refs):
        b_refs, o_ref = refs[:nb], refs[nb]
        ti = pl.program_id(2)
        av = a_ref[...]
        if a_square:
            af = av.astype(F32)
            av = (af * af).astype(BF16)
        bv = b_refs[0][...] if nb == 1 else jnp.concatenate([r[...] for r in b_refs], axis=1)
        part = lax.dot_general(av, bv, TN, preferred_element_type=F32)

        @pl.when(ti == 0)
        def _():
            o_ref[...] = part

        @pl.when(ti > 0)
        def _():
            o_ref[...] += part

    if b_blocks:
        b_specs = [pl.BlockSpec((tt, tn // nb), functools.partial(lambda blk, i, j, ti: (ti, blk), blk))
                   for blk in b_blocks]
    else:
        b_specs = [pl.BlockSpec((tt, tn), lambda i, j, ti: (ti, j))]
    (out,), xres = _call(
        body, name=name, grid=(m // tm, n // tn, t // tt),
        in_specs=[pl.BlockSpec((tt, tm), lambda i, j, ti: (ti, i))] + b_specs,
        out_specs=[pl.BlockSpec((None, tm, tn), lambda i, j, ti: (j // per, i, j % per))],
        out_shape=[jax.ShapeDtypeStruct((n_split, m, n // n_split), F32)], args=[a] + [b] * nb,
        semantics=("parallel", "parallel", "arbitrary"), exchanges=exchanges)
    return (out, xres) if exchanges else out


def _rstd(x):
    return lax.rsqrt(jnp.mean(x * x, axis=-1, keepdims=True) + EPS)


def _rms_cast_gather(x, g, buf, *, tm, name):
    t, d = x.shape
    steps = t // tm

    def body(x_ref, g_ref, b_in, o_ref, b_out, send_sems, recv_sems):
        i = pl.program_id(0)
        xc, yc, c = _place()
        chips = _other_chips(xc, yc)

        def slot(px, py, pc):
            return b_out.at[4 * px + 2 * py + pc]

        def sent(j):
            return _remote(b_in.at[4 * xc + 2 * yc + c], slot(xc, yc, c), send_sems, recv_sems, j, (*chips[j], c))

        def passed(j):
            return _remote(slot(*chips[j], c), slot(*chips[j], c), send_sems, recv_sems, 3 + j, (xc, yc, 1 - c))

        @pl.when(i == 0)
        def _():
            for j in range(3):
                sent(j).start()

        xv = x_ref[...]
        o_ref[...] = (xv * _rstd(xv) * g_ref[...]).astype(BF16)

        @pl.when(i == steps - 1)
        def _():
            for j in range(3):
                sent(j).wait_recv()
                passed(j).start()
            for j in range(3):
                passed(j).wait_recv()
                passed(j).wait_send()
                sent(j).wait_send()

    return pl.pallas_call(
        body, name=name, grid=(steps,),
        in_specs=[pl.BlockSpec((tm, d), lambda i: (i, 0)), pl.BlockSpec((1, d), lambda i: (0, 0)), ANY],
        out_specs=[pl.BlockSpec((tm, d), lambda i: (i, 0)), ANY],
        out_shape=[jax.ShapeDtypeStruct((t, d), BF16), jax.ShapeDtypeStruct(buf.shape, buf.dtype)],
        scratch_shapes=[pltpu.SemaphoreType.DMA((6,)), pltpu.SemaphoreType.DMA((6,))],
        input_output_aliases={2: 1},
        compiler_params=_params(("arbitrary",)),
    )(x, g, buf)


def _mix_cat(attn, rnn, gain, *, tm, name):
    t = attn.shape[0]

    def body(a_ref, r_ref, g_ref, o_ref):
        av = a_ref[...].astype(F32)
        o_ref[:, :ATTN_W] = (av * _rstd(av) * g_ref[...]).astype(BF16)
        o_ref[:, ATTN_W:] = r_ref[...].astype(BF16)

    return pl.pallas_call(
        body, name=name, grid=(t // tm,),
        in_specs=[pl.BlockSpec((tm, ATTN_W), lambda i: (i, 0)), pl.BlockSpec((tm, RNN_W), lambda i: (i, 0)),
                  pl.BlockSpec((1, ATTN_W), lambda i: (0, 0))],
        out_specs=pl.BlockSpec((tm, D_MODEL), lambda i: (i, 0)),
        out_shape=jax.ShapeDtypeStruct((t, D_MODEL), BF16),
        compiler_params=_params(("parallel",)),
    )(attn, rnn, gain)


def _post_norm_res(mixed, g_post, res, g_next, *, tm, name, exchanges=()):
    t, d = mixed.shape

    def body(m_ref, gp_ref, r_ref, gn_ref, x1_ref, h2_ref):
        mv = m_ref[...].astype(F32)
        x1 = r_ref[...] + mv * _rstd(mv) * gp_ref[...]
        x1_ref[...] = x1.astype(BF16)
        h2_ref[...] = (x1 * _rstd(x1) * gn_ref[...]).astype(BF16)

    row = pl.BlockSpec((tm, d), lambda i: (i, 0))
    vec = pl.BlockSpec((1, d), lambda i: (0, 0))
    res_, xres = _call(
        body, name=name, grid=(t // tm,),
        in_specs=[row, vec, row, vec], out_specs=[row, row],
        out_shape=[jax.ShapeDtypeStruct((t, d), BF16), jax.ShapeDtypeStruct((t, d), BF16)],
        args=[mixed, g_post, res, g_next], semantics=("parallel",), exchanges=exchanges)
    return (*res_, xres) if exchanges else res_


def _rms_bwd(dyn, xin, g, res, *, tm, out_dtype, name, col_block=0, exchanges=()):
    t, d = xin.shape

    def body(*refs):
        if res is not None:
            dy_ref, x_ref, g_ref, r_ref, dx_ref, dg_ref = refs
        else:
            dy_ref, x_ref, g_ref, dx_ref, dg_ref = refs
        i = pl.program_id(0)
        xv = x_ref[...].astype(F32)
        dy = dy_ref[...].astype(F32)
        r = _rstd(xv)
        xh = xv * r
        part = jnp.sum(dy * xh, axis=0, keepdims=True)

        @pl.when(i == 0)
        def _():
            dg_ref[...] = part

        @pl.when(i > 0)
        def _():
            dg_ref[...] += part

        tt = dy * g_ref[...]
        dx = r * (tt - xh * jnp.mean(tt * xh, axis=-1, keepdims=True))
        if res is not None:
            dx = dx + r_ref[...].astype(F32)
        dx_ref[...] = dx.astype(out_dtype)

    row = pl.BlockSpec((tm, d), lambda i: (i, 0))
    vec = pl.BlockSpec((1, d), lambda i: (0, 0))
    in_specs = [pl.BlockSpec((tm, d), lambda i: (i, col_block)), row, vec]
    args = [dyn, xin, g]
    if res is not None:
        in_specs.append(row)
        args.append(res)
    res, xres = _call(
        body, name=name, grid=(t // tm,),
        in_specs=in_specs, out_specs=[row, vec],
        out_shape=[jax.ShapeDtypeStruct((t, d), out_dtype), jax.ShapeDtypeStruct((1, d), F32)], args=args,
        semantics=("arbitrary",), exchanges=exchanges)
    return (*res, xres) if exchanges else res


def _loss_head(y, g_post, x1, target, *, tm, name):
    t, d = y.shape

    def body(y_ref, g_ref, x1_ref, t_ref, dy_ref, dx2_ref, loss_ref, dg_ref):
        i = pl.program_id(0)
        yv = y_ref[...].astype(F32)
        r = _rstd(yv)
        yh = yv * r
        gv = g_ref[...]
        err = x1_ref[...].astype(F32) + yh * gv - t_ref[...]
        lpart = 0.5 * jnp.sum(jnp.mean(err * err, axis=-1, keepdims=True), axis=0, keepdims=True)
        dx2 = err * (1.0 / d)
        dgp = jnp.sum(dx2 * yh, axis=0, keepdims=True)
        lane = lax.broadcasted_iota(jnp.int32, (1, 128), 1)
        lrow = jnp.where(lane == 0, lpart, 0.0)

        @pl.when(i == 0)
        def _():
            dg_ref[...] = dgp
            loss_ref[...] = lrow

        @pl.when(i > 0)
        def _():
            dg_ref[...] += dgp
            loss_ref[...] += lrow

        tt = dx2 * gv
        dy_ref[...] = (r * (tt - yh * jnp.mean(tt * yh, axis=-1, keepdims=True))).astype(BF16)
        dx2_ref[...] = dx2.astype(BF16)

    row = pl.BlockSpec((tm, d), lambda i: (i, 0))
    vec = pl.BlockSpec((1, d), lambda i: (0, 0))
    return pl.pallas_call(
        body, name=name, grid=(t // tm,),
        in_specs=[row, vec, row, row],
        out_specs=[row, row, pl.BlockSpec((1, 128), lambda i: (0, 0)), vec],
        out_shape=[jax.ShapeDtypeStruct((t, d), BF16), jax.ShapeDtypeStruct((t, d), BF16),
                   jax.ShapeDtypeStruct((1, 128), F32), jax.ShapeDtypeStruct((1, d), F32)],
        compiler_params=_params(("arbitrary",)),
    )(y, g_post, x1, target)


def _alibi_slope(h):
    return 2.0 ** (-8.0 * (h + 1) / N_Q)


PAIR = 2 * HEAD_DIM
N_PAIRS = N_Q // 2
PAIRS_PER_KV = GROUP // 2
SMEM = pl.BlockSpec(memory_space=pltpu.SMEM)


def _swa_mask(n):
    key = lax.broadcasted_iota(jnp.int32, (2 * BLK, BLK), 0)
    qry = lax.broadcasted_iota(jnp.int32, (2 * BLK, BLK), 1)
    dist = qry + BLK - key
    valid = (dist >= 0) & (dist < BLK) & ((key >= BLK) | (n > 0))
    return valid, dist.astype(F32)


def _block_diag(kvp_ref, kvc_ref, off):
    a = jnp.concatenate([kvp_ref[:, off:off + HEAD_DIM], kvc_ref[:, off:off + HEAD_DIM]], axis=0).astype(BF16)
    z = jnp.zeros_like(a)
    return jnp.concatenate([jnp.concatenate([a, z], axis=1), jnp.concatenate([z, a], axis=1)], axis=0)


def _swa_scores(s2, e, hh, valid, distf):
    s = s2[2 * BLK * e:2 * BLK * (e + 1)] * (HEAD_DIM ** -0.5) - _alibi_slope(hh) * distf
    return jnp.where(valid, s, -1e30)


def _swa_fwd(proj, sinks, *, name, exchanges=()):
    t = proj.shape[0]
    nb = t // BLK
    kvb = KV_COL // (2 * 128)

    def body(sink_ref, q_ref, kvc_ref, kvp_ref, o_ref, lse_ref):
        n = pl.program_id(0)
        valid, distf = _swa_mask(n)
        for kvh in range(N_KV):
            k2 = _block_diag(kvp_ref, kvc_ref, kvh * HEAD_DIM)
            v2 = _block_diag(kvp_ref, kvc_ref, 128 + kvh * HEAD_DIM)
            for jp in range(PAIRS_PER_KV):
                pair = kvh * PAIRS_PER_KV + jp
                lanes = slice(pair * PAIR, (pair + 1) * PAIR)
                s2 = lax.dot_general(k2, q_ref[:, lanes].astype(BF16), NT, preferred_element_type=F32)
                probs = []
                for e in range(2):
                    hh = 2 * pair + e
                    s = _swa_scores(s2, e, hh, valid, distf)
                    sink = sink_ref[0, hh]
                    mx = jnp.maximum(jnp.max(s, axis=0, keepdims=True), sink)
                    p = jnp.exp(s - mx)
                    l = jnp.sum(p, axis=0, keepdims=True) + jnp.exp(sink - mx)
                    probs.append((p * (1.0 / l)).astype(BF16))
                    lse_ref[hh:hh + 1, :] = mx + jnp.log(l)
                o_ref[:, lanes] = lax.dot_general(jnp.concatenate(probs, axis=0), v2, TN,
                                                  preferred_element_type=F32).astype(BF16)

    res, xres = _call(
        body, name=name, grid=(nb,),
        in_specs=[SMEM,
                  pl.BlockSpec((BLK, ATTN_W), lambda n: (n, 0)),
                  pl.BlockSpec((BLK, 256), lambda n: (n, kvb)),
                  pl.BlockSpec((BLK, 256), lambda n: (jnp.maximum(n - 1, 0), kvb))],
        out_specs=[pl.BlockSpec((BLK, ATTN_W), lambda n: (n, 0)),
                   pl.BlockSpec((None, N_Q, BLK), lambda n: (n, 0, 0))],
        out_shape=[jax.ShapeDtypeStruct((t, ATTN_W), BF16), jax.ShapeDtypeStruct((nb, N_Q, BLK), F32)],
        args=[sinks, proj, proj, proj], semantics=("parallel",), exchanges=exchanges)
    return (*res, xres) if exchanges else res


def _swa_bwd(proj, sinks, dattn, lse, *, name, exchanges=()):
    t = proj.shape[0]
    nb = t // BLK
    kvb = KV_COL // (2 * 128)

    def body(sink_ref, q_ref, kvc_ref, kvp_ref, do_ref, lse_ref, dq_ref, dkv_ref, dsink_ref, carry_ref):
        n = pl.program_id(0)

        @pl.when(n == 0)
        def _():
            dsink_ref[...] = jnp.zeros_like(dsink_ref)
            carry_ref[...] = jnp.zeros_like(carry_ref)

        @pl.when(n < nb)
        def _():
            valid, distf = _swa_mask(n)
            for kvh in range(N_KV):
                k2 = _block_diag(kvp_ref, kvc_ref, kvh * HEAD_DIM)
                v2 = _block_diag(kvp_ref, kvc_ref, 128 + kvh * HEAD_DIM)
                dk2 = jnp.zeros((4 * BLK, PAIR), F32)
                dv2 = jnp.zeros((4 * BLK, PAIR), F32)
                for jp in range(PAIRS_PER_KV):
                    pair = kvh * PAIRS_PER_KV + jp
                    lanes = slice(pair * PAIR, (pair + 1) * PAIR)
                    q2 = q_ref[:, lanes].astype(BF16)
                    do2 = do_ref[:, lanes].astype(BF16)
                    s2 = lax.dot_general(k2, q2, NT, preferred_element_type=F32)
                    dp2 = lax.dot_general(v2, do2, NT, preferred_element_type=F32)
                    probs, dss = [], []
                    for e in range(2):
                        hh = 2 * pair + e
                        lse_h = lse_ref[hh:hh + 1, :]
                        p = jnp.exp(_swa_scores(s2, e, hh, valid, distf) - lse_h)
                        dp = dp2[2 * BLK * e:2 * BLK * (e + 1)]
                        delta = jnp.sum(p * dp, axis=0, keepdims=True)
                        dsink_ref[hh:hh + 1, :] += -jnp.exp(sink_ref[0, hh] - lse_h) * delta
                        probs.append(p.astype(BF16))
                        dss.append((p * (dp - delta)).astype(BF16))
                    ds2 = jnp.concatenate(dss, axis=0)
                    dq_ref[:, lanes] = (lax.dot_general(ds2, k2, TN, preferred_element_type=F32)
                                        * (HEAD_DIM ** -0.5)).astype(BF16)
                    dk2 = dk2 + jnp.dot(ds2, q2, preferred_element_type=F32)
                    dv2 = dv2 + jnp.dot(jnp.concatenate(probs, axis=0), do2, preferred_element_type=F32)
                dk_cat = (dk2[:2 * BLK, :HEAD_DIM] + dk2[2 * BLK:, HEAD_DIM:]) * (HEAD_DIM ** -0.5)
                dv_cat = dv2[:2 * BLK, :HEAD_DIM] + dv2[2 * BLK:, HEAD_DIM:]
                ko = kvh * HEAD_DIM
                vo = 128 + kvh * HEAD_DIM
                dkv_ref[:, ko:ko + HEAD_DIM] = (carry_ref[:, ko:ko + HEAD_DIM] + dk_cat[:BLK]).astype(BF16)
                dkv_ref[:, vo:vo + HEAD_DIM] = (carry_ref[:, vo:vo + HEAD_DIM] + dv_cat[:BLK]).astype(BF16)
                carry_ref[:, ko:ko + HEAD_DIM] = dk_cat[BLK:]
                carry_ref[:, vo:vo + HEAD_DIM] = dv_cat[BLK:]

        @pl.when(n == nb)
        def _():
            dkv_ref[...] = carry_ref[...].astype(BF16)

    last = nb - 1
    res, xres = _call(
        body, name=name, grid=(nb + 1,),
        in_specs=[SMEM,
                  pl.BlockSpec((BLK, ATTN_W), lambda n: (jnp.minimum(n, last), 0)),
                  pl.BlockSpec((BLK, 256), lambda n: (jnp.minimum(n, last), kvb)),
                  pl.BlockSpec((BLK, 256), lambda n: (jnp.maximum(jnp.minimum(n, last) - 1, 0), kvb)),
                  pl.BlockSpec((BLK, ATTN_W), lambda n: (jnp.minimum(n, last), 0)),
                  pl.BlockSpec((None, N_Q, BLK), lambda n: (jnp.minimum(n, last), 0, 0))],
        out_specs=[pl.BlockSpec((BLK, ATTN_W), lambda n: (jnp.minimum(n, last), 0)),
                   pl.BlockSpec((BLK, 256), lambda n: (jnp.maximum(n - 1, 0), 0)),
                   pl.BlockSpec((N_Q, BLK), lambda n: (0, 0))],
        out_shape=[jax.ShapeDtypeStruct((t, ATTN_W), BF16), jax.ShapeDtypeStruct((t, 256), BF16),
                   jax.ShapeDtypeStruct((N_Q, BLK), F32)],
        scratch_shapes=[pltpu.VMEM((BLK, 256), F32)],
        args=[sinks, proj, proj, proj, dattn, lse], semantics=("arbitrary",), exchanges=exchanges)
    return (*res, xres) if exchanges else res


def _cumsum_rows(x):
    n = x.shape[0]
    row = lax.broadcasted_iota(jnp.int32, x.shape, 0)
    s = 1
    while s < n:
        x = x + jnp.where(row >= s, pltpu.roll(x, s, axis=0), 0.0)
        s *= 2
    return x


def _rev_cumsum_rows(x):
    n = x.shape[0]
    row = lax.broadcasted_iota(jnp.int32, x.shape, 0)
    s = 1
    while s < n:
        x = x + jnp.where(row < n - s, pltpu.roll(x, n - s, axis=0), 0.0)
        s *= 2
    return x


def _lower_bound(lbl_ref):
    l0 = lbl_ref[0:1, :]
    l1 = lbl_ref[1:2, :]
    mx = jnp.maximum(l0, l1)
    e0 = jnp.exp(l0 - mx)
    e1 = jnp.exp(l1 - mx)
    return e0 / (e0 + e1)


def _hgrn_gates(z, lb):
    sg = _sigmoid(z)
    f = lb + (1.0 - lb) * sg
    return sg, f, jnp.log(f), 1.0 - f


def _sub_factors(b, k, i, sub, trim):
    need = -(-sub * i // 16) * 16 if trim else CHUNK
    rows = lax.broadcasted_iota(jnp.int32, (need, RNN_HD), 0)
    ref = b[sub * i - 1:sub * i, :]
    qfac = jnp.exp(b[sub * i:sub * (i + 1), :] - ref)
    kfac = jnp.where(rows < sub * i, jnp.exp(ref - b[:need]), 0.0)
    kt = (k[:need] * kfac).astype(BF16)
    if need < CHUNK:
        kt = jnp.concatenate([kt, jnp.zeros((CHUNK - need, RNN_HD), BF16)], axis=0)
    return qfac, kfac, kt


def _diag_decay(bi, s):
    trow = lax.broadcasted_iota(jnp.int32, bi.shape, 0)
    return jnp.where(trow >= s, jnp.exp(bi - bi[s:s + 1, :]), 0.0)


def _hgrn_fwd(proj, lb_logits, norm_gain, *, tb, name, exchanges=()):
    t = proj.shape[0]
    ntb = t // tb
    nch = tb // CHUNK
    qb, fb, ib, gb = QR_COL // 128, FR_COL // 128, IR_COL // 128, GR_COL // 128

    def body(q_ref, f_ref, i_ref, g_ref, lbl_ref, gain_ref, o_ref, out_ref, s0_ref, ksave_ref, bsave_ref, st_ref):
        c = pl.program_id(1)

        @pl.when(c == 0)
        def _():
            st_ref[...] = jnp.zeros_like(st_ref)

        lb = _lower_bound(lbl_ref)
        gain = gain_ref[...]

        def chunk(ci, st):
            rows = slice(ci * CHUNK, (ci + 1) * CHUNK)
            _, _, lf, k = _hgrn_gates(f_ref[rows, :], lb)
            qr = q_ref[rows, :]
            q = qr * _sigmoid(qr)
            v = i_ref[rows, :]
            b = _cumsum_rows(lf)
            ksave_ref[rows, :] = k
            bsave_ref[rows, :] = b
            s0_ref[ci] = st
            o_inter = lax.dot_general((q * jnp.exp(b)).astype(BF16), st.astype(BF16), NT,
                                      preferred_element_type=F32)
            vb = v.astype(BF16)
            blast = b[CHUNK - 1:CHUNK, :]
            khat = (k * jnp.exp(blast - b)).astype(BF16)
            st = st * jnp.exp(blast) + lax.dot_general(vb, khat, TN, preferred_element_type=F32)
            blocks = []
            for i in range(CHUNK // SUB_FWD):
                blk = slice(SUB_FWD * i, SUB_FWD * (i + 1))
                qi, ki, vi, bi = q[blk], k[blk], v[blk], b[blk]
                oi = o_inter[blk]
                if i > 0:
                    qfac, _, kt = _sub_factors(b, k, i, SUB_FWD, trim=True)
                    att = lax.dot_general((qi * qfac).astype(BF16), kt, NT,
                                          preferred_element_type=F32)
                    oi = oi + jnp.dot(att.astype(BF16), vb, preferred_element_type=F32)
                for s in range(SUB_FWD):
                    qe = qi * _diag_decay(bi, s)
                    a = jnp.sum(qe * ki[s:s + 1, :], axis=1, keepdims=True)
                    oi = oi + a * vi[s:s + 1, :]
                blocks.append(oi)
            o = jnp.concatenate(blocks, axis=0)
            o_ref[rows, :] = o
            gr = g_ref[rows, :]
            out_ref[rows, :] = (o * _rstd(o) * gain * (gr * _sigmoid(gr))).astype(BF16)
            return st

        st = st_ref[...]
        for ci in range(nch):
            st = chunk(ci, st)
        st_ref[...] = st

    def col(base):
        return pl.BlockSpec((tb, RNN_HD), lambda h, c: (c, base + h))

    res, xres = _call(
        body, name=name, grid=(N_RNN, ntb),
        in_specs=[col(qb), col(fb), col(ib), col(gb),
                  pl.BlockSpec((2, RNN_HD), lambda h, c: (0, h)), pl.BlockSpec((1, RNN_HD), lambda h, c: (0, 0))],
        out_specs=[pl.BlockSpec((tb, RNN_HD), lambda h, c: (c, h)), pl.BlockSpec((tb, RNN_HD), lambda h, c: (c, h)),
                   pl.BlockSpec((None, nch, RNN_HD, RNN_HD), lambda h, c: (h, c, 0, 0)),
                   pl.BlockSpec((tb, RNN_HD), lambda h, c: (c, h)), pl.BlockSpec((tb, RNN_HD), lambda h, c: (c, h))],
        out_shape=[jax.ShapeDtypeStruct((t, RNN_W), F32), jax.ShapeDtypeStruct((t, RNN_W), BF16),
                   jax.ShapeDtypeStruct((N_RNN, t // CHUNK, RNN_HD, RNN_HD), F32),
                   jax.ShapeDtypeStruct((t, RNN_W), F32), jax.ShapeDtypeStruct((t, RNN_W), F32)],
        scratch_shapes=[pltpu.VMEM((RNN_HD, RNN_HD), F32)],
        args=[proj, proj, proj, proj, lb_logits, norm_gain],
        semantics=("parallel", "arbitrary"), exchanges=exchanges)
    return (*res, xres) if exchanges else res


def _hgrn_bwd(proj, lb_logits, norm_gain, o_pre, s0, k_gate, b_cum, dcat, *, tb, name, exchanges=()):
    t = proj.shape[0]
    ntb = t // tb
    nch = tb // CHUNK
    qb, fb, ib, gb = QR_COL // 128, FR_COL // 128, IR_COL // 128, GR_COL // 128
    sub = SUB_BWD
    nsub = CHUNK // sub

    def body(q_ref, k_ref, b_ref, i_ref, g_ref, lbl_ref, gain_ref, o_ref, s0_ref, dout_ref,
             dq_ref, df_ref, di_ref, dg_ref, dlb_ref, dgain_ref,
             dst_ref, dqs_ref, dks_ref, dvs_ref):
        c = pl.program_id(1)

        @pl.when(c == 0)
        def _():
            dst_ref[...] = jnp.zeros_like(dst_ref)
            dlb_ref[...] = jnp.zeros_like(dlb_ref)
            dgain_ref[...] = jnp.zeros_like(dgain_ref)

        lb = _lower_bound(lbl_ref)
        inv_1mlb = 1.0 / (1.0 - lb)
        gain = gain_ref[...]

        def chunk(ci, dst):
            rows = slice(ci * CHUNK, (ci + 1) * CHUNK)
            dqa_ref, dka_ref, dva_ref = dqs_ref.at[ci], dks_ref.at[ci], dvs_ref.at[ci]
            k = k_ref[rows, :]
            b = b_ref[rows, :]
            f = 1.0 - k
            one_minus_sg = k * inv_1mlb
            qr = q_ref[rows, :]
            sq = _sigmoid(qr)
            q = qr * sq
            v = i_ref[rows, :]

            dout = dout_ref[rows, :].astype(F32)
            o = o_ref[rows, :]
            gr = g_ref[rows, :]
            sgg = _sigmoid(gr)
            gate = gr * sgg
            rs = _rstd(o)
            nrm = o * rs
            dg_ref[rows, :] = (dout * nrm * gain * (sgg * (1.0 + gr * (1.0 - sgg)))).astype(BF16)
            dn = dout * gate
            dgain_ref[...] += jnp.sum(dn * nrm, axis=0, keepdims=True)
            tt = dn * gain
            do = rs * (tt - nrm * jnp.mean(tt * nrm, axis=-1, keepdims=True))

            dob = do.astype(BF16)
            vb = v.astype(BF16)
            eb = jnp.exp(b)
            blast = b[CHUNK - 1:CHUNK, :]
            ebl = jnp.exp(blast - b)
            dstb = dst.astype(BF16)
            khat = (k * ebl).astype(BF16)
            s0 = s0_ref[ci]
            dqa_ref[...] = eb * jnp.dot(dob, s0.astype(BF16), preferred_element_type=F32)
            dk_state = ebl * jnp.dot(vb, dstb, preferred_element_type=F32)
            dka_ref[...] = dk_state
            d_blast = (jnp.sum(k * dk_state, axis=0, keepdims=True)
                       + jnp.exp(blast) * jnp.sum(dst * s0, axis=0, keepdims=True))
            dva_ref[...] = lax.dot_general(khat, dstb, NT, preferred_element_type=F32)
            dst_next = dst * jnp.exp(blast) + lax.dot_general(dob, (q * eb).astype(BF16), TN,
                                                              preferred_element_type=F32)
            pm = lax.dot_general(dob, vb, NT, preferred_element_type=F32)
            for i in range(nsub):
                blk = slice(sub * i, sub * (i + 1))
                qi, ki, vi, bi, doi = q[blk], k[blk], v[blk], b[blk], do[blk]
                dqi = dqa_ref[blk, :]
                if i > 0:
                    qfac, kfac, kt = _sub_factors(b, k, i, sub, trim=False)
                    qt = (qi * qfac).astype(BF16)
                    att = lax.dot_general(qt, kt, NT, preferred_element_type=F32).astype(BF16)
                    pmi = pm[blk, :].astype(BF16)
                    dva_ref[...] += lax.dot_general(att, doi.astype(BF16), TN, preferred_element_type=F32)
                    dqi = dqi + qfac * jnp.dot(pmi, kt, preferred_element_type=F32)
                    dka_ref[...] += kfac * lax.dot_general(pmi, qt, TN, preferred_element_type=F32)
                dqa_ref[blk, :] = dqi
                srow = lax.broadcasted_iota(jnp.int32, (sub, RNN_HD), 0)
                dki = jnp.zeros((sub, RNN_HD), F32)
                dvi = jnp.zeros((sub, RNN_HD), F32)
                for tq in range(sub):
                    qt, dot_ = qi[tq:tq + 1, :], doi[tq:tq + 1, :]
                    e = jnp.where(srow <= tq, jnp.exp(bi[tq:tq + 1, :] - bi), 0.0)
                    ke = ki * e
                    p = jnp.sum(vi * dot_, axis=1, keepdims=True)
                    a = jnp.sum(ke * qt, axis=1, keepdims=True)
                    dki = dki + p * (qt * e)
                    dvi = dvi + a * dot_
                    row = slice(sub * i + tq, sub * i + tq + 1)
                    dqa_ref[row, :] += jnp.sum(p * ke, axis=0, keepdims=True)
                dka_ref[blk, :] += dki
                dva_ref[blk, :] += dvi

            dq = dqa_ref[...]
            dk = dka_ref[...]
            lastrow = lax.broadcasted_iota(jnp.int32, (CHUNK, RNN_HD), 0) == CHUNK - 1
            dlf = _rev_cumsum_rows(q * dq - k * dk + jnp.where(lastrow, d_blast, 0.0))
            dff = dlf / f - dk
            df_ref[rows, :] = (dff * k * (1.0 - one_minus_sg)).astype(BF16)
            dlb_ref[...] += jnp.sum(dff * one_minus_sg, axis=0, keepdims=True)
            dq_ref[rows, :] = (dq * (sq * (1.0 + qr * (1.0 - sq)))).astype(BF16)
            di_ref[rows, :] = dva_ref[...].astype(BF16)
            return dst_next

        dst = dst_ref[...]
        for ci in reversed(range(nch)):
            dst = chunk(ci, dst)
        dst_ref[...] = dst

    def col(base):
        return pl.BlockSpec((tb, RNN_HD), lambda h, c: (ntb - 1 - c, base + h))

    outc = pl.BlockSpec((tb, RNN_HD), lambda h, c: (ntb - 1 - c, h))
    hb = ATTN_W // RNN_HD
    res, xres = _call(
        body, name=name, grid=(N_RNN, ntb),
        in_specs=[col(qb), outc, outc, col(ib), col(gb),
                  pl.BlockSpec((2, RNN_HD), lambda h, c: (0, h)), pl.BlockSpec((1, RNN_HD), lambda h, c: (0, 0)),
                  outc,
                  pl.BlockSpec((None, nch, RNN_HD, RNN_HD), lambda h, c: (h, ntb - 1 - c, 0, 0)),
                  pl.BlockSpec((tb, RNN_HD), lambda h, c: (ntb - 1 - c, hb + h))],
        out_specs=[outc, outc, outc, outc,
                   pl.BlockSpec((1, RNN_HD), lambda h, c: (0, h)),
                   pl.BlockSpec((None, 1, RNN_HD), lambda h, c: (h, 0, 0))],
        out_shape=[jax.ShapeDtypeStruct((t, RNN_W), BF16)] * 4
        + [jax.ShapeDtypeStruct((1, RNN_W), F32), jax.ShapeDtypeStruct((N_RNN, 1, RNN_HD), F32)],
        scratch_shapes=[pltpu.VMEM((RNN_HD, RNN_HD), F32),
                        pltpu.VMEM((nch, CHUNK, RNN_HD), F32), pltpu.VMEM((nch, CHUNK, RNN_HD), F32),
                        pltpu.VMEM((nch, CHUNK, RNN_HD), F32)],
        args=[proj, k_gate, b_cum, proj, proj, lb_logits, norm_gain, o_pre, s0, dcat],
        semantics=("parallel", "arbitrary"), exchanges=exchanges)
    return (*res, xres) if exchanges else res


def _cast_slots(w, where, *, name):
    _, rows, cols = w.shape
    rh = rows // 2
    tr = _row_tile(rh, cols)
    nh = rh // tr

    def body(wh_ref, w_ref, o_ref):
        o_ref[...] = w_ref[...].astype(BF16)

    return pl.pallas_call(
        body, name=name,
        grid_spec=pltpu.PrefetchScalarGridSpec(
            num_scalar_prefetch=1, grid=(2, nh),
            in_specs=[pl.BlockSpec((None, tr, cols), lambda h, i, wh: (0, h * nh + i, 0))],
            out_specs=pl.BlockSpec((None, tr, cols), lambda h, i, wh: (2 * wh[0] + h, i, 0))),
        out_shape=jax.ShapeDtypeStruct((8, rh, cols), BF16),
        compiler_params=_params(("parallel", "parallel")),
    )(where, w)


def _row_tile(rows, cols, budget=1 << 20):
    tr = rows
    while tr * cols > budget and tr % 16 == 0:
        tr //= 2
    return tr


def _half_spec(g, tr, halves_last, slab):
    if halves_last:
        return pl.BlockSpec((None, tr, g.shape[2] // 2), lambda *a: (slab(*a), a[-2], a[-1][1]))
    return pl.BlockSpec((None, None, tr, g.shape[3]), lambda *a: (slab(*a), a[-1][1], a[-2], 0))


def _pair_sum(g, sib, where, *, name, halves_last=False):
    rh, cols = sib.shape[1:]
    tr = _row_tile(rh, cols)

    def body(w_ref, g_ref, s_ref, o_ref):
        o_ref[...] = (g_ref[...] + s_ref[...]).astype(BF16)

    def foreign(s, i, w):
        return (w[0] + 1 + s) % N_CHIPS

    return pl.pallas_call(
        body, name=name,
        grid_spec=pltpu.PrefetchScalarGridSpec(
            num_scalar_prefetch=1, grid=(N_CHIPS - 1, rh // tr),
            in_specs=[_half_spec(g, tr, halves_last, foreign),
                      pl.BlockSpec((None, tr, cols), lambda s, i, w: (foreign(s, i, w), i, 0))],
            out_specs=pl.BlockSpec((None, tr, cols), lambda s, i, w: (foreign(s, i, w), i, 0))),
        out_shape=jax.ShapeDtypeStruct((4, rh, cols), BF16),
        compiler_params=_params(("parallel", "parallel")),
    )(where, g, sib)


def _final_half(g, sib, recv, where, *, name, halves_last=False):
    rh, cols = sib.shape[1:]
    tr = _row_tile(rh, cols)

    def body(w_ref, g_ref, s_ref, r_ref, o_ref):
        acc = g_ref[...] + s_ref[...]
        for j in range(3):
            acc = acc + r_ref[j].astype(F32)
        o_ref[...] = acc

    return pl.pallas_call(
        body, name=name,
        grid_spec=pltpu.PrefetchScalarGridSpec(
            num_scalar_prefetch=1, grid=(rh // tr,),
            in_specs=[_half_spec(g, tr, halves_last, lambda i, w: w[0]),
                      pl.BlockSpec((None, tr, cols), lambda i, w: (w[0], i, 0)),
                      pl.BlockSpec((3, tr, cols), lambda i, w: (0, i, 0))],
            out_specs=pl.BlockSpec((tr, cols), lambda i, w: (i, 0))),
        out_shape=jax.ShapeDtypeStruct((rh, cols), F32),
        compiler_params=_params(("parallel",)),
    )(where, g, sib, recv)


def _adamw_math(w, g, m, v):
    m = ADAM_B1 * m + (1.0 - ADAM_B1) * g
    v = ADAM_B2 * v + (1.0 - ADAM_B2) * (g * g)
    m_hat = m / (1.0 - ADAM_B1 ** ADAM_STEP)
    v_hat = v / (1.0 - ADAM_B2 ** ADAM_STEP)
    delta = -ADAM_LR * (m_hat / (jnp.sqrt(v_hat) + ADAM_EPS) + ADAM_WD * w)
    return delta, m, v


def _adamw(w, mine, theirs, m, v, where, *, name, halves_last=False):
    _, rows, cols = w.shape
    if halves_last:
        cols //= 2
        tr = _row_tile(rows, cols, budget=1 << 19)
        grid = (rows // tr, 2)
        blk = pl.BlockSpec((None, tr, cols), lambda i, h, wh: (0, i, h))
        mine_spec = theirs_spec = pl.BlockSpec((tr, cols), lambda i, h, wh: (i, 0))
        which = lambda: pl.program_id(1)
    else:
        tr = _row_tile(rows // 2, cols, budget=1 << 19)
        nh = rows // 2 // tr
        grid = (rows // tr,)
        blk = pl.BlockSpec((None, tr, cols), lambda i, wh: (0, i, 0))
        mine_spec = pl.BlockSpec((tr, cols), lambda i, wh: (jnp.where(i // nh == wh[1], i % nh, 0), 0))
        theirs_spec = pl.BlockSpec((tr, cols), lambda i, wh: (jnp.where(i // nh == wh[1], 0, i % nh), 0))
        which = lambda: pl.program_id(0) // nh

    def body(wh_ref, w_ref, a_ref, b_ref, m_ref, v_ref, g_ref, d_ref, nm_ref, nv_ref):
        g = jnp.where(which() == wh_ref[1], a_ref[...], b_ref[...])
        d, nm, nv = _adamw_math(w_ref[...], g, m_ref[...], v_ref[...])
        g_ref[...] = g
        d_ref[...] = d
        nm_ref[...] = nm
        nv_ref[...] = nv

    rows, cols = w.shape[1:]
    return pl.pallas_call(
        body, name=name,
        grid_spec=pltpu.PrefetchScalarGridSpec(
            num_scalar_prefetch=1, grid=grid,
            in_specs=[blk, mine_spec, theirs_spec, blk, blk], out_specs=[blk] * 4),
        out_shape=[jax.ShapeDtypeStruct((1, rows, cols), F32)] * 4,
        compiler_params=_params(("parallel",) * len(grid)),
    )(where, w, mine, theirs, m, v)


SEG_LOSS = 0
SEG_SINK = 128
SEG_AGAIN = 256
SEG_L0 = SEG_AGAIN + ATTN_W
SEG_L1 = SEG_L0 + RNN_W
SEG_RGAIN = SEG_L1 + RNN_W
SEG_G = SEG_RGAIN + 128
N_PACK = SEG_G + 4 * D_MODEL


def _pack(sinks, again, l0, l1, rgain, gains, loss=None):
    z = lambda k: jnp.zeros((1, k), F32)
    first = z(128) if loss is None else loss
    return jnp.concatenate([first, sinks, z(128 - N_Q), again, l0, l1, rgain] + list(gains), axis=1)


def _small_reduce_adamw(part, w, m, v, *, name):
    def body(p_ref, w_ref, m_ref, v_ref, g_ref, d_ref, nm_ref, nv_ref, buf_ref, send_sems, recv_sems):
        x, y, c = _place()
        me = 4 * x + 2 * y + c
        copies = []
        for k in range(1, 8):
            dx, dy, dc = (k >> 2) & 1, (k >> 1) & 1, k & 1
            to = (x ^ dx, y ^ dy, c ^ dc)
            cp = pltpu.make_async_remote_copy(
                src_ref=p_ref, dst_ref=buf_ref.at[me],
                send_sem=send_sems.at[k - 1], recv_sem=recv_sems.at[k - 1],
                device_id=to, device_id_type=MESH)
            cp.start()
            copies.append(cp)
        buf_ref[me] = p_ref[...]
        for cp in copies:
            cp.wait()
        tot = buf_ref[0]
        for j in range(1, 8):
            tot = tot + buf_ref[j]
        g_ref[...] = tot
        l0 = w_ref[:, SEG_L0:SEG_L0 + RNN_W]
        l1 = w_ref[:, SEG_L1:SEG_L1 + RNN_W]
        mx = jnp.maximum(l0, l1)
        e0 = jnp.exp(l0 - mx)
        e1 = jnp.exp(l1 - mx)
        lb = e0 / (e0 + e1)
        gl0 = tot[:, SEG_L0:SEG_L0 + RNN_W] * lb * (1.0 - lb)
        g_ref[:, SEG_L0:SEG_L0 + RNN_W] = gl0
        g_ref[:, SEG_L1:SEG_L1 + RNN_W] = -gl0
        d, nm, nv = _adamw_math(w_ref[...], g_ref[...], m_ref[...], v_ref[...])
        d_ref[...] = d
        nm_ref[...] = nm
        nv_ref[...] = nv

    vm = pl.BlockSpec(memory_space=pltpu.VMEM)
    return pl.pallas_call(
        body, name=name,
        in_specs=[vm] * 4, out_specs=[vm] * 4,
        out_shape=[jax.ShapeDtypeStruct((1, N_PACK), F32)] * 4,
        scratch_shapes=[pltpu.VMEM((8, 1, N_PACK), F32), pltpu.SemaphoreType.DMA((7,)),
                        pltpu.SemaphoreType.DMA((7,))],
    )(part, w, m, v)


def _layer_grads(xs, tgt, bufs, where, sinks, again, lb_logits, rgain,
                 g_mix_pre, g_mix_post, g_mlp_pre, g_mlp_post):
    tm = 512
    b_in, b_out, b_up, b_dn = bufs

    shard = IN_W // N_CHIPS
    h1, b_in = _rms_cast_gather(xs, g_mix_pre, b_in, tm=tm, name="h1_norm_gather_w_in")
    w_in_t = b_in.reshape(IN_W, D_MODEL)
    proj, ((b_out, b_up),) = _mm(
        h1, w_in_t, tm=1024, tn=768, tk=D_MODEL, out_dtype=F32, w_layout="nk", name="in_proj",
        exchanges=[_x_gather([b_out, b_up], ici=[(0, 256), (0, 336)])])
    attn, lse, ((b_out, b_up),) = _swa_fwd(
        proj, sinks, name="swa_fwd",
        exchanges=[_x_gather([b_out, b_up], ici=[None, (336, 320)], d2d=[(0, 256), None])])
    w_out = b_out.reshape(D_MODEL, D_MODEL)
    o_pre, rnn, s0, k_gate, b_cum, ((b_up, b_dn),) = _hgrn_fwd(
        proj, lb_logits, rgain, tb=512, name="hgrn_fwd",
        exchanges=[_x_gather([b_up, b_dn], ici=[(656, 368), (0, 400)])])
    cat = _mix_cat(attn, rnn, again, tm=tm, name="mix_cat")
    mixed, ((b_up, b_dn),) = _mm(
        cat, w_out, tm=1024, tn=1024, tk=D_MODEL, out_dtype=BF16, name="out_proj",
        exchanges=[_x_gather([b_up, b_dn], ici=[None, (400, 240)], d2d=[(0, 1024), (0, 400)])])
    w_up4 = b_up.reshape(N_CHIPS, D_MODEL, D_FF // N_CHIPS)
    x1, h2, ((b_dn,),) = _post_norm_res(
        mixed, g_mix_post, xs, g_mlp_pre, tm=tm, name="mix_post",
        exchanges=[_x_gather([b_dn], d2d=[(400, 240)])])
    u, ((b_dn,),) = _mm(h2, w_up4, tm=1024, tn=1024, tk=D_MODEL, out_dtype=BF16, relu=True, w_layout="skn",
                        name="mlp_up", exchanges=[_x_gather([b_dn], ici=[(640, 384)], cross=[(640, 384)])])
    w_dn = b_dn.reshape(D_FF, D_MODEL)
    yv = _mm(u, w_dn, tm=1024, tn=1024, tk=2048, out_dtype=BF16, a_square=True, name="mlp_down")
    dy, dx2, loss_row, dg_mlp_post = _loss_head(yv, g_mlp_post, x1, tgt, tm=tm, name="loss_head")

    def halved(g):
        return g.reshape(N_CHIPS, 2, g.shape[1] // 2, g.shape[2])
    du = _mm(dy, w_dn, tm=1024, tn=1024, tk=D_MODEL, out_dtype=BF16, mul2=u, w_layout="nk", name="mlp_down_bwd")
    g_dn = halved(_mm_tn(u, dy, tm=1024, tn=1024, tt=2048, a_square=True, name="w_down_grad")
                  .reshape(N_CHIPS, D_FF // N_CHIPS, D_MODEL))
    d_w_up, ((sib_dn,),) = _mm_tn(h2, du, tm=1024, tn=1024, tt=2048, n_split=N_CHIPS, name="w_up_grad",
                                  exchanges=[_x_pair([g_dn])])
    g_up = halved(d_w_up)
    wire_dn = _pair_sum(g_dn, sib_dn, where, name="pair_sum_w_down")
    dh2, ((recv_dn,), (sib_up,)) = _mm(du, w_up4, tm=1024, tn=1024, tk=2048, out_dtype=BF16, w_layout="snk", name="mlp_up_bwd",
                                       exchanges=[_x_chip([wire_dn], rows=[(0, 928)]), _x_pair([g_up])])
    wire_up = _pair_sum(g_up, sib_up, where, name="pair_sum_w_up")
    dx1, dg_mlp_pre = _rms_bwd(dh2, x1, g_mlp_pre, dx2, tm=tm, out_dtype=BF16, name="mlp_pre_bwd")
    dmixed, dg_mix_post = _rms_bwd(dx1, mixed, g_mix_post, None, tm=tm, out_dtype=BF16, name="mix_post_bwd")
    d_w_out, ((recv_dn,),) = _mm_tn(cat, dmixed, tm=1024, tn=1024, tt=2048, name="w_out_grad",
                                    exchanges=[_x_chip([wire_dn], rows=[(928, 96)], into=[recv_dn])])
    fin_dn = _final_half(g_dn, sib_dn, recv_dn, where, name="final_half_w_down")
    g_out = halved(d_w_out.reshape(N_CHIPS, D_MODEL // N_CHIPS, D_MODEL))
    dcat, ((sib_out,), (oth_dn,)) = _mm(dmixed, w_out, tm=1024, tn=1024, tk=D_MODEL, out_dtype=BF16, w_layout="nk",
                                        name="out_proj_bwd", exchanges=[_x_pair([g_out]), _x_share([fin_dn])])
    wire_out = _pair_sum(g_out, sib_out, where, name="pair_sum_w_out")
    dattn, dg_again = _rms_bwd(dcat, attn, again, None, tm=tm, out_dtype=BF16, name="attn_norm_bwd")
    dq_a, dkv, dsinks, ((recv_up,),) = _swa_bwd(
        proj, sinks, dattn, lse, name="swa_bwd", exchanges=[_x_chip([wire_up], rows=[(0, 512)])])
    dq_r, df_r, di_r, dg_r, dlb, dgain_h, ((recv_up,), (recv_out,)) = _hgrn_bwd(
        proj, lb_logits, rgain, o_pre, s0, k_gate, b_cum, dcat, tb=1024, name="hgrn_bwd",
        exchanges=[_x_chip([wire_up], rows=[(512, 512)], into=[recv_up]), _x_chip([wire_out])])
    fin_up = _final_half(g_up, sib_up, recv_up, where, name="final_half_w_up")
    fin_out = _final_half(g_out, sib_out, recv_out, where, name="final_half_w_out")
    dproj = jnp.concatenate([dq_a, dkv, dq_r, df_r, di_r, dg_r], axis=1)
    piece_cols = D_MODEL // 4

    def w_in_piece(pc, exchanges):
        d, xres = _mm_tn(dproj, h1, tm=896, tn=2 * piece_cols, tt=2048, b_blocks=(pc, pc + 2),
                         name="w_in_grad_%d" % pc, exchanges=exchanges)
        return d.reshape(N_CHIPS, shard, 2 * piece_cols), xres

    g_in0, ((oth_up, oth_out),) = w_in_piece(0, [_x_share([fin_up, fin_out])])
    g_in1, ((sib_in0,),) = w_in_piece(1, [_x_pair([g_in0], halves_last=True)])
    wire_in0 = _pair_sum(g_in0, sib_in0, where, name="pair_sum_w_in_0", halves_last=True)
    dh1, ((recv_in0,), (sib_in1,)) = _mm(
        dproj, w_in_t, tm=1024, tn=1024, tk=2688, out_dtype=BF16, m_blocks=(0, 2), name="in_proj_bwd_0",
        exchanges=[_x_chip([wire_in0]), _x_pair([g_in1], halves_last=True)])
    wire_in1 = _pair_sum(g_in1, sib_in1, where, name="pair_sum_w_in_1", halves_last=True)
    dh1, ((recv_in1,),) = _mm(
        dproj, w_in_t, tm=1024, tn=1024, tk=2688, out_dtype=BF16, m_blocks=(2, 2), out_into=dh1,
        name="in_proj_bwd_1", exchanges=[_x_chip([wire_in1])])
    gx, dg_mix_pre = _rms_bwd(dh1, xs, g_mix_pre, dx1, tm=tm, out_dtype=F32, name="mix_pre_bwd")
    fin_in0 = _final_half(g_in0, sib_in0, recv_in0, where, name="final_half_w_in_0", halves_last=True)
    fin_in1 = _final_half(g_in1, sib_in1, recv_in1, where, name="final_half_w_in_1", halves_last=True)
    oth_in0, oth_in1 = _run_exchange(_x_share([fin_in0, fin_in1]), name="share_w_in")
    fin_in = jnp.concatenate([fin_in0, fin_in1], axis=1)
    oth_in = jnp.concatenate([oth_in0, oth_in1], axis=1)

    big = [(fin_in, oth_in), (fin_out, oth_out), (fin_up, oth_up), (fin_dn, oth_dn)]
    drgain = jnp.sum(dgain_h, axis=0)
    small = _pack(jnp.sum(dsinks, axis=1)[None, :], dg_again, dlb, jnp.zeros_like(dlb), drgain,
                  [dg_mix_pre, dg_mix_post, dg_mlp_pre, dg_mlp_post], loss=loss_row)
    return gx, big, small


def kernel(x, w_in, attn_sinks, attn_out_gain, rnn_lb_logits, rnn_norm_gain, w_out, mix_pre_gain, mix_post_gain, mlp_pre_gain, mlp_post_gain, w_up, w_down, loss_target, m_w_in, m_attn_sinks, m_attn_out_gain, m_rnn_lb_logits, m_rnn_norm_gain, m_w_out, m_mix_pre_gain, m_mix_post_gain, m_mlp_pre_gain, m_mlp_post_gain, m_w_up, m_w_down, v_w_in, v_attn_sinks, v_attn_out_gain, v_rnn_lb_logits, v_rnn_norm_gain, v_w_out, v_mix_pre_gain, v_mix_post_gain, v_mlp_pre_gain, v_mlp_post_gain, v_w_up, v_w_down):
    ax, ay, ac = _place()
    where = jnp.stack([2 * ax + ay, ac]).astype(jnp.int32)
    t = lambda a: jnp.swapaxes(a, 1, 2)
    big_w = [t(w_in), w_out, w_up, w_down]
    big_m = [t(m_w_in), m_w_out, m_w_up, m_w_down]
    big_v = [t(v_w_in), v_w_out, v_w_up, v_w_down]

    names = ["w_in", "w_out", "w_up", "w_down"]
    bufs = [_cast_slots(w, where, name="cast_" + nm) for w, nm in zip(big_w, names)]
    gx, big_g, small_part = _layer_grads(
        x[0], loss_target[0], bufs, where, attn_sinks, attn_out_gain, rnn_lb_logits, rnn_norm_gain,
        mix_pre_gain, mix_post_gain, mlp_pre_gain, mlp_post_gain)

    grads, deltas, new_m, new_v = [], [], [], []
    for (f, o), w, m, v, nm in zip(big_g, big_w, big_m, big_v, names):
        res = _adamw(w, f, o, m, v, where, name="adamw_" + nm, halves_last=(nm == "w_in"))
        if nm == "w_in":
            res = [t(r) for r in res]
        g, d, nm_, nv_ = res
        grads.append(g)
        deltas.append(d)
        new_m.append(nm_)
        new_v.append(nv_)

    def pack_params(sinks, again, logits, rgain, gains):
        return _pack(sinks, again, logits[0:1], logits[1:2], rgain, gains)

    pw = pack_params(attn_sinks, attn_out_gain, rnn_lb_logits, rnn_norm_gain,
                     [mix_pre_gain, mix_post_gain, mlp_pre_gain, mlp_post_gain])
    pm = pack_params(m_attn_sinks, m_attn_out_gain, m_rnn_lb_logits, m_rnn_norm_gain,
                     [m_mix_pre_gain, m_mix_post_gain, m_mlp_pre_gain, m_mlp_post_gain])
    pv = pack_params(v_attn_sinks, v_attn_out_gain, v_rnn_lb_logits, v_rnn_norm_gain,
                     [v_mix_pre_gain, v_mix_post_gain, v_mlp_pre_gain, v_mlp_post_gain])
    packs = _small_reduce_adamw(small_part, pw, pm, pv, name="small_reduce_adamw")

    def unpack(p):
        seg = lambda o, k: p[:, o:o + k]
        logits = jnp.concatenate([seg(SEG_L0, RNN_W), seg(SEG_L1, RNN_W)], axis=0)
        gains = [seg(SEG_G + i * D_MODEL, D_MODEL) for i in range(4)]
        return dict(sinks=seg(SEG_SINK, N_Q), again=seg(SEG_AGAIN, ATTN_W), logits=logits,
                    rgain=seg(SEG_RGAIN, RNN_HD), gains=gains)

    def order(small, big):
        return [big[0], small["sinks"], small["again"], small["logits"], small["rgain"], big[1],
                *small["gains"], big[2], big[3]]

    loss = packs[0][0, 0]
    outs = [loss, gx[None]]
    for p, b in zip(packs, [grads, deltas, new_m, new_v]):
        outs += order(unpack(p), b)
    return tuple(outs)
```

```python
import functools

import jax
import jax.numpy as jnp
from jax import lax
from jax.experimental import pallas as pl
from jax.experimental.pallas import tpu as pltpu

F32 = jnp.float32
BF16 = jnp.bfloat16
MESH = pl.DeviceIdType.MESH

EPS = 1e-6
D_MODEL = 2048
ATTN_W = 1024
HEAD_DIM = 64
N_Q = 16
N_KV = 2
GROUP = 8
BLK = 128
RNN_W = 1024
RNN_HD = 128
N_RNN = 8
CHUNK = 64
SUB_FWD = 16
SUB_BWD = 8
D_FF = 8192
IN_W = 5376
N_CHIPS = 4
KV_COL = ATTN_W
QR_COL = ATTN_W + 2 * 128
FR_COL = QR_COL + RNN_W
IR_COL = FR_COL + RNN_W
GR_COL = IR_COL + RNN_W

ADAM_LR = 0.001
ADAM_B1 = 0.9
ADAM_B2 = 0.999
ADAM_EPS = 1e-08
ADAM_WD = 0.01
ADAM_STEP = 10

VMEM_LIMIT = 48 * 1024 * 1024

NT = (((1,), (1,)), ((), ()))
TN = (((0,), (0,)), ((), ()))


def _params(sem=None):
    return pltpu.CompilerParams(dimension_semantics=sem, vmem_limit_bytes=VMEM_LIMIT)


def _sigmoid(x):
    return 1.0 / (1.0 + jnp.exp(-x))


ANY = pl.BlockSpec(memory_space=pl.ANY)


def _place():
    return lax.axis_index("x"), lax.axis_index("y"), lax.axis_index("c")


def _other_chips(x, y):
    return [(1 - x, y), (x, 1 - y), (1 - x, 1 - y)]


class _Exchange:
    def __init__(self, srcs, outs, ncopy, build, aliases=None):
        self.srcs, self.outs, self.ncopy, self.build = list(srcs), list(outs), ncopy, build
        self.aliases = aliases or {}


def _remote(src, dst, send_sems, recv_sems, k, to):
    return pltpu.make_async_remote_copy(src_ref=src, dst_ref=dst, send_sem=send_sems.at[k],
                                        recv_sem=recv_sems.at[k], device_id=to, device_id_type=MESH)


def _call(body, *, name, grid, in_specs, out_specs, out_shape, args, scratch_shapes=(), semantics=None,
          exchanges=(), into=None):
    in_specs, out_specs, out_shape = list(in_specs), list(out_specs), list(out_shape)
    scratch_shapes = list(scratch_shapes)
    ni, no, ns = len(in_specs), len(out_specs), len(scratch_shapes)
    xsrc = [s for x in exchanges for s in x.srcs]
    xout = [o for x in exchanges for o in x.outs]
    into = into or {}
    xsrc += [into[k] for k in sorted(into)]
    nxi, nxo = len(xsrc), len(xout)
    aliases = {nxi - len(into) + ni + q: k for q, k in enumerate(sorted(into))}
    a0 = b0 = 0
    for x in exchanges:
        for si, oi in x.aliases.items():
            aliases[ni + a0 + si] = no + b0 + oi
        a0 += len(x.srcs)
        b0 += len(x.outs)
    sems = []
    for x in exchanges:
        sems += [pltpu.SemaphoreType.DMA((x.ncopy,)), pltpu.SemaphoreType.DMA((x.ncopy,))]

    def wrapped(*refs):
        ins, xi = refs[:ni], refs[ni:ni + nxi]
        outs, xo = refs[ni + nxi:ni + nxi + no], refs[ni + nxi + no:ni + nxi + no + nxo]
        rest = refs[ni + nxi + no + nxo:]
        scr, sm = rest[:ns], rest[ns:]

        def copies():
            cps = []
            a = b = 0
            for k, x in enumerate(exchanges):
                cps += x.build(xi[a:a + len(x.srcs)], xo[b:b + len(x.outs)], sm[2 * k], sm[2 * k + 1])
                a += len(x.srcs)
                b += len(x.outs)
            return cps

        def start():
            for cp in copies():
                cp.start()

        def wait():
            for cp in copies():
                cp.wait()

        if not exchanges:
            body(*ins, *outs, *scr)
        elif not grid:
            start()
            body(*ins, *outs, *scr)
            wait()
        else:
            first = last = None
            for ax, g in enumerate(grid):
                f = pl.program_id(ax) == 0
                l = pl.program_id(ax) == g - 1
                first = f if first is None else first & f
                last = l if last is None else last & l
            pl.when(first)(start)
            body(*ins, *outs, *scr)
            pl.when(last)(wait)

    if exchanges and semantics is not None:
        semantics = ("arbitrary",) * len(grid)
    kwargs = dict(grid=grid) if grid else {}
    res = pl.pallas_call(
        wrapped, name=name,
        in_specs=in_specs + [ANY] * nxi, out_specs=out_specs + [ANY] * nxo,
        out_shape=out_shape + xout, scratch_shapes=scratch_shapes + sems,
        input_output_aliases=aliases,
        compiler_params=_params(semantics), **kwargs,
    )(*args, *xsrc)
    res = list(res)
    mine, theirs = res[:no], res[no:]
    per = []
    b = 0
    for x in exchanges:
        per.append(theirs[b:b + len(x.outs)])
        b += len(x.outs)
    return mine, per


def _run_exchange(x, *, name):
    return _call(lambda: None, name=name, grid=(), in_specs=[], out_specs=[], out_shape=[], args=[],
                 exchanges=[x])[1][0]


def _x_gather(bufs, ici=None, d2d=None, cross=None):
    n = len(bufs)
    plan = [(a, kind, rows[a]) for a in range(n) for kind, rows in (("ici", ici), ("d2d", d2d), ("cross", cross))
            if rows is not None and rows[a] is not None]

    def build(srcs, outs, ss, rs):
        x, y, c = _place()
        cps = []
        for q, (a, kind, rows) in enumerate(plan):
            piece = pl.ds(*rows)
            for j, (px, py) in enumerate(_other_chips(x, y)):
                if kind == "d2d":
                    slot, to = 4 * px + 2 * py + c, (x, y, 1 - c)
                else:
                    slot, to = 4 * x + 2 * y + c, (px, py, c if kind == "ici" else 1 - c)
                cps.append(_remote(srcs[a].at[slot, piece], outs[a].at[slot, piece], ss, rs, 3 * q + j, to))
        return cps

    outs = [jax.ShapeDtypeStruct(b.shape, b.dtype) for b in bufs]
    return _Exchange(bufs, outs, 3 * len(plan), build, aliases={a: a for a in range(n)})


def _x_pair(grads, halves_last=False):
    n = len(grads)

    def build(srcs, outs, ss, rs):
        x, y, c = _place()

        def half(r):
            if not halves_last:
                return r.at[:, 1 - c]
            ch = r.shape[2] // 2
            return r.at[:, :, pl.ds(pl.multiple_of((1 - c) * ch, 128), ch)]

        return [_remote(half(srcs[a]), outs[a], ss, rs, a, (x, y, 1 - c)) for a in range(n)]

    if halves_last:
        outs = [jax.ShapeDtypeStruct(g.shape[:2] + (g.shape[2] // 2,), g.dtype) for g in grads]
    else:
        outs = [jax.ShapeDtypeStruct((4,) + g.shape[2:], g.dtype) for g in grads]
    return _Exchange(grads, outs, n, build)


def _x_chip(wires, rows=None, into=None):
    n = len(wires)
    rows = rows or [(0, w.shape[1]) for w in wires]

    def build(srcs, outs, ss, rs):
        x, y, c = _place()
        cps = []
        for a in range(n):
            piece = pl.ds(*rows[a])
            for j, (px, py) in enumerate(_other_chips(x, y)):
                cps.append(_remote(srcs[a].at[2 * px + py, piece], outs[a].at[j, piece], ss, rs,
                                   3 * a + j, (px, py, c)))
        return cps

    outs = [jax.ShapeDtypeStruct((3,) + w.shape[1:], w.dtype) for w in wires]
    if into is None:
        return _Exchange(wires, outs, 3 * n, build)
    return _Exchange(list(wires) + list(into), outs, 3 * n, build, aliases={n + a: a for a in range(n)})


def _x_share(halves):
    n = len(halves)

    def build(srcs, outs, ss, rs):
        x, y, c = _place()
        return [_remote(srcs[a], outs[a], ss, rs, a, (x, y, 1 - c)) for a in range(n)]

    outs = [jax.ShapeDtypeStruct(h.shape, h.dtype) for h in halves]
    return _Exchange(halves, outs, n, build)


def _mm(a, w, *, tm, tn, tk, out_dtype, name, a_square=False, relu=False, mul2=None, w_layout="kn",
        m_blocks=None, out_into=None, exchanges=()):
    m, k = a.shape
    m_first, m_count = m_blocks or (0, m // tm)
    a_spec = pl.BlockSpec((tm, tk), lambda i, j, kk: (i + m_first, kk))
    if w_layout == "kn":
        n = w.shape[1]
        w_spec = pl.BlockSpec((tk, tn), lambda i, j, kk: (kk, j))
    elif w_layout == "nk":
        n = w.shape[0]
        w_spec = pl.BlockSpec((tn, tk), lambda i, j, kk: (j, kk))
    elif w_layout == "skn":
        n = w.shape[0] * w.shape[2]
        per_n = w.shape[2] // tn
        w_spec = pl.BlockSpec((None, tk, tn), lambda i, j, kk: (j // per_n, kk, j % per_n))
    else:
        assert w_layout == "snk"
        n = w.shape[1]
        per_k = w.shape[2] // tk
        w_spec = pl.BlockSpec((None, tn, tk), lambda i, j, kk: (kk // per_k, j, kk % per_k))
    w_dims = NT if w_layout in ("nk", "snk") else (((1,), (0,)), ((), ()))
    nk = k // tk
    assert m % tm == 0 and n % tn == 0 and k % tk == 0

    def body(*refs):
        if mul2 is not None:
            a_ref, w_ref, e_ref, o_ref, acc_ref = refs
        else:
            a_ref, w_ref, o_ref, acc_ref = refs
            e_ref = None
        kk = pl.program_id(2)
        av = a_ref[...]
        if a_square:
            af = av.astype(F32)
            av = (af * af).astype(BF16)
        part = lax.dot_general(av, w_ref[...], w_dims, preferred_element_type=F32)

        def finish(r):
            if relu:
                r = jnp.maximum(r, 0.0)
            if e_ref is not None:
                r = 2.0 * e_ref[...].astype(F32) * r
            o_ref[...] = r.astype(out_dtype)

        if nk == 1:
            finish(part)
        else:
            @pl.when(kk == 0)
            def _():
                acc_ref[...] = part

            @pl.when(kk > 0)
            def _():
                acc_ref[...] += part

            @pl.when(kk == nk - 1)
            def _():
                finish(acc_ref[...])

    in_specs = [a_spec, w_spec]
    args = [a, w]
    if mul2 is not None:
        in_specs.append(pl.BlockSpec((tm, tn), lambda i, j, kk: (i + m_first, j)))
        args.append(mul2)
    acc_shape = (tm, tn) if nk > 1 else (8, 128)
    (out,), per = _call(
        body, name=name, grid=(m_count, n // tn, nk),
        in_specs=in_specs, out_specs=[pl.BlockSpec((tm, tn), lambda i, j, kk: (i + m_first, j))],
        out_shape=[jax.ShapeDtypeStruct((m, n), out_dtype)], args=args,
        scratch_shapes=[pltpu.VMEM(acc_shape, F32)],
        semantics=("parallel", "parallel", "arbitrary"), exchanges=exchanges,
        into=None if out_into is None else {0: out_into})
    return (out, per) if exchanges else out


def _mm_tn(a, b, *, tm, tn, tt, name, a_square=False, n_split=1, b_blocks=None, exchanges=()):
    t, m = a.shape
    nb = len(b_blocks) if b_blocks else 1
    n = tn if b_blocks else b.shape[1]
    assert t % tt == 0 and m % tm == 0 and n % tn == 0 and (n // n_split) % tn == 0
    per = n // n_split // tn

    def body(a_ref, *refs):
        b_refs, o_ref = refs[:nb], refs[nb]
        ti = pl.program_id(2)
        av = a_ref[...]
        if a_square:
            af = av.astype(F32)
            av = (af * af).astype(BF16)
        bv = b_refs[0][...] if nb == 1 else jnp.concatenate([r[...] for r in b_refs], axis=1)
        part = lax.dot_general(av, bv, TN, preferred_element_type=F32)

        @pl.when(ti == 0)
        def _():
            o_ref[...] = part

        @pl.when(ti > 0)
        def _():
            o_ref[...] += part

    if b_blocks:
        b_specs = [pl.BlockSpec((tt, tn // nb), functools.partial(lambda blk, i, j, ti: (ti, blk), blk))
                   for blk in b_blocks]
    else:
        b_specs = [pl.BlockSpec((tt, tn), lambda i, j, ti: (ti, j))]
    (out,), xres = _call(
        body, name=name, grid=(m // tm, n // tn, t // tt),
        in_specs=[pl.BlockSpec((tt, tm), lambda i, j, ti: (ti, i))] + b_specs,
        out_specs=[pl.BlockSpec((None, tm, tn), lambda i, j, ti: (j // per, i, j % per))],
        out_shape=[jax.ShapeDtypeStruct((n_split, m, n // n_split), F32)], args=[a] + [b] * nb,
        semantics=("parallel", "parallel", "arbitrary"), exchanges=exchanges)
    return (out, xres) if exchanges else out


def _rstd(x):
    return lax.rsqrt(jnp.mean(x * x, axis=-1, keepdims=True) + EPS)


def _rms_cast_gather(x, g, buf, *, tm, name):
    t, d = x.shape
    steps = t // tm

    def body(x_ref, g_ref, b_in, o_ref, b_out, send_sems, recv_sems):
        i = pl.program_id(0)
        xc, yc, c = _place()
        chips = _other_chips(xc, yc)

        def slot(px, py, pc):
            return b_out.at[4 * px + 2 * py + pc]

        def sent(j):
            return _remote(b_in.at[4 * xc + 2 * yc + c], slot(xc, yc, c), send_sems, recv_sems, j, (*chips[j], c))

        def passed(j):
            return _remote(slot(*chips[j], c), slot(*chips[j], c), send_sems, recv_sems, 3 + j, (xc, yc, 1 - c))

        @pl.when(i == 0)
        def _():
            for j in range(3):
                sent(j).start()

        xv = x_ref[...]
        o_ref[...] = (xv * _rstd(xv) * g_ref[...]).astype(BF16)

        @pl.when(i == steps - 1)
        def _():
            for j in range(3):
                sent(j).wait_recv()
                passed(j).start()
            for j in range(3):
                passed(j).wait_recv()
                passed(j).wait_send()
                sent(j).wait_send()

    return pl.pallas_call(
        body, name=name, grid=(steps,),
        in_specs=[pl.BlockSpec((tm, d), lambda i: (i, 0)), pl.BlockSpec((1, d), lambda i: (0, 0)), ANY],
        out_specs=[pl.BlockSpec((tm, d), lambda i: (i, 0)), ANY],
        out_shape=[jax.ShapeDtypeStruct((t, d), BF16), jax.ShapeDtypeStruct(buf.shape, buf.dtype)],
        scratch_shapes=[pltpu.SemaphoreType.DMA((6,)), pltpu.SemaphoreType.DMA((6,))],
        input_output_aliases={2: 1},
        compiler_params=_params(("arbitrary",)),
    )(x, g, buf)


def _mix_cat(attn, rnn, gain, *, tm, name):
    t = attn.shape[0]

    def body(a_ref, r_ref, g_ref, o_ref):
        av = a_ref[...].astype(F32)
        o_ref[:, :ATTN_W] = (av * _rstd(av) * g_ref[...]).astype(BF16)
        o_ref[:, ATTN_W:] = r_ref[...].astype(BF16)

    return pl.pallas_call(
        body, name=name, grid=(t // tm,),
        in_specs=[pl.BlockSpec((tm, ATTN_W), lambda i: (i, 0)), pl.BlockSpec((tm, RNN_W), lambda i: (i, 0)),
                  pl.BlockSpec((1, ATTN_W), lambda i: (0, 0))],
        out_specs=pl.BlockSpec((tm, D_MODEL), lambda i: (i, 0)),
        out_shape=jax.ShapeDtypeStruct((t, D_MODEL), BF16),
        compiler_params=_params(("parallel",)),
    )(attn, rnn, gain)


def _post_norm_res(mixed, g_post, res, g_next, *, tm, name, exchanges=()):
    t, d = mixed.shape

    def body(m_ref, gp_ref, r_ref, gn_ref, x1_ref, h2_ref):
        mv = m_ref[...].astype(F32)
        x1 = r_ref[...] + mv * _rstd(mv) * gp_ref[...]
        x1_ref[...] = x1.astype(BF16)
        h2_ref[...] = (x1 * _rstd(x1) * gn_ref[...]).astype(BF16)

    row = pl.BlockSpec((tm, d), lambda i: (i, 0))
    vec = pl.BlockSpec((1, d), lambda i: (0, 0))
    res_, xres = _call(
        body, name=name, grid=(t // tm,),
        in_specs=[row, vec, row, vec], out_specs=[row, row],
        out_shape=[jax.ShapeDtypeStruct((t, d), BF16), jax.ShapeDtypeStruct((t, d), BF16)],
        args=[mixed, g_post, res, g_next], semantics=("parallel",), exchanges=exchanges)
    return (*res_, xres) if exchanges else res_


def _rms_bwd(dyn, xin, g, res, *, tm, out_dtype, name, col_block=0, exchanges=()):
    t, d = xin.shape

    def body(*refs):
        if res is not None:
            dy_ref, x_ref, g_ref, r_ref, dx_ref, dg_ref = refs
        else:
            dy_ref, x_ref, g_ref, dx_ref, dg_ref = refs
        i = pl.program_id(0)
        xv = x_ref[...].astype(F32)
        dy = dy_ref[...].astype(F32)
        r = _rstd(xv)
        xh = xv * r
        part = jnp.sum(dy * xh, axis=0, keepdims=True)

        @pl.when(i == 0)
        def _():
            dg_ref[...] = part

        @pl.when(i > 0)
        def _():
            dg_ref[...] += part

        tt = dy * g_ref[...]
        dx = r * (tt - xh * jnp.mean(tt * xh, axis=-1, keepdims=True))
        if res is not None:
            dx = dx + r_ref[...].astype(F32)
        dx_ref[...] = dx.astype(out_dtype)

    row = pl.BlockSpec((tm, d), lambda i: (i, 0))
    vec = pl.BlockSpec((1, d), lambda i: (0, 0))
    in_specs = [pl.BlockSpec((tm, d), lambda i: (i, col_block)), row, vec]
    args = [dyn, xin, g]
    if res is not None:
        in_specs.append(row)
        args.append(res)
    res, xres = _call(
        body, name=name, grid=(t // tm,),
        in_specs=in_specs, out_specs=[row, vec],
        out_shape=[jax.ShapeDtypeStruct((t, d), out_dtype), jax.ShapeDtypeStruct((1, d), F32)], args=args,
        semantics=("arbitrary",), exchanges=exchanges)
    return (*res, xres) if exchanges else res


def _loss_head(y, g_post, x1, target, *, tm, name):
    t, d = y.shape

    def body(y_ref, g_ref, x1_ref, t_ref, dy_ref, dx2_ref, loss_ref, dg_ref):
        i = pl.program_id(0)
        yv = y_ref[...].astype(F32)
        r = _rstd(yv)
        yh = yv * r
        gv = g_ref[...]
        err = x1_ref[...].astype(F32) + yh * gv - t_ref[...]
        lpart = 0.5 * jnp.sum(jnp.mean(err * err, axis=-1, keepdims=True), axis=0, keepdims=True)
        dx2 = err * (1.0 / d)
        dgp = jnp.sum(dx2 * yh, axis=0, keepdims=True)
        lane = lax.broadcasted_iota(jnp.int32, (1, 128), 1)
        lrow = jnp.where(lane == 0, lpart, 0.0)

        @pl.when(i == 0)
        def _():
            dg_ref[...] = dgp
            loss_ref[...] = lrow

        @pl.when(i > 0)
        def _():
            dg_ref[...] += dgp
            loss_ref[...] += lrow

        tt = dx2 * gv
        dy_ref[...] = (r * (tt - yh * jnp.mean(tt * yh, axis=-1, keepdims=True))).astype(BF16)
        dx2_ref[...] = dx2.astype(BF16)

    row = pl.BlockSpec((tm, d), lambda i: (i, 0))
    vec = pl.BlockSpec((1, d), lambda i: (0, 0))
    return pl.pallas_call(
        body, name=name, grid=(t // tm,),
        in_specs=[row, vec, row, row],
        out_specs=[row, row, pl.BlockSpec((1, 128), lambda i: (0, 0)), vec],
        out_shape=[jax.ShapeDtypeStruct((t, d), BF16), jax.ShapeDtypeStruct((t, d), BF16),
                   jax.ShapeDtypeStruct((1, 128), F32), jax.ShapeDtypeStruct((1, d), F32)],
        compiler_params=_params(("arbitrary",)),
    )(y, g_post, x1, target)


def _alibi_slope(h):
    return 2.0 ** (-8.0 * (h + 1) / N_Q)


PAIR = 2 * HEAD_DIM
N_PAIRS = N_Q // 2
PAIRS_PER_KV = GROUP // 2
SMEM = pl.BlockSpec(memory_space=pltpu.SMEM)


def _swa_mask(n):
    key = lax.broadcasted_iota(jnp.int32, (2 * BLK, BLK), 0)
    qry = lax.broadcasted_iota(jnp.int32, (2 * BLK, BLK), 1)
    dist = qry + BLK - key
    valid = (dist >= 0) & (dist < BLK) & ((key >= BLK) | (n > 0))
    return valid, dist.astype(F32)


def _block_diag(kvp_ref, kvc_ref, off):
    a = jnp.concatenate([kvp_ref[:, off:off + HEAD_DIM], kvc_ref[:, off:off + HEAD_DIM]], axis=0).astype(BF16)
    z = jnp.zeros_like(a)
    return jnp.concatenate([jnp.concatenate([a, z], axis=1), jnp.concatenate([z, a], axis=1)], axis=0)


def _swa_scores(s2, e, hh, valid, distf):
    s = s2[2 * BLK * e:2 * BLK * (e + 1)] * (HEAD_DIM ** -0.5) - _alibi_slope(hh) * distf
    return jnp.where(valid, s, -1e30)


def _swa_fwd(proj, sinks, *, name, exchanges=()):
    t = proj.shape[0]
    nb = t // BLK
    kvb = KV_COL // (2 * 128)

    def body(sink_ref, q_ref, kvc_ref, kvp_ref, o_ref, lse_ref):
        n = pl.program_id(0)
        valid, distf = _swa_mask(n)
        for kvh in range(N_KV):
            k2 = _block_diag(kvp_ref, kvc_ref, kvh * HEAD_DIM)
            v2 = _block_diag(kvp_ref, kvc_ref, 128 + kvh * HEAD_DIM)
            for jp in range(PAIRS_PER_KV):
                pair = kvh * PAIRS_PER_KV + jp
                lanes = slice(pair * PAIR, (pair + 1) * PAIR)
                s2 = lax.dot_general(k2, q_ref[:, lanes].astype(BF16), NT, preferred_element_type=F32)
                probs = []
                for e in range(2):
                    hh = 2 * pair + e
                    s = _swa_scores(s2, e, hh, valid, distf)
                    sink = sink_ref[0, hh]
                    mx = jnp.maximum(jnp.max(s, axis=0, keepdims=True), sink)
                    p = jnp.exp(s - mx)
                    l = jnp.sum(p, axis=0, keepdims=True) + jnp.exp(sink - mx)
                    probs.append((p * (1.0 / l)).astype(BF16))
                    lse_ref[hh:hh + 1, :] = mx + jnp.log(l)
                o_ref[:, lanes] = lax.dot_general(jnp.concatenate(probs, axis=0), v2, TN,
                                                  preferred_element_type=F32).astype(BF16)

    res, xres = _call(
        body, name=name, grid=(nb,),
        in_specs=[SMEM,
                  pl.BlockSpec((BLK, ATTN_W), lambda n: (n, 0)),
                  pl.BlockSpec((BLK, 256), lambda n: (n, kvb)),
                  pl.BlockSpec((BLK, 256), lambda n: (jnp.maximum(n - 1, 0), kvb))],
        out_specs=[pl.BlockSpec((BLK, ATTN_W), lambda n: (n, 0)),
                   pl.BlockSpec((None, N_Q, BLK), lambda n: (n, 0, 0))],
        out_shape=[jax.ShapeDtypeStruct((t, ATTN_W), BF16), jax.ShapeDtypeStruct((nb, N_Q, BLK), F32)],
        args=[sinks, proj, proj, proj], semantics=("parallel",), exchanges=exchanges)
    return (*res, xres) if exchanges else res


def _swa_bwd(proj, sinks, dattn, lse, *, name, exchanges=()):
    t = proj.shape[0]
    nb = t // BLK
    kvb = KV_COL // (2 * 128)

    def body(sink_ref, q_ref, kvc_ref, kvp_ref, do_ref, lse_ref, dq_ref, dkv_ref, dsink_ref, carry_ref):
        n = pl.program_id(0)

        @pl.when(n == 0)
        def _():
            dsink_ref[...] = jnp.zeros_like(dsink_ref)
            carry_ref[...] = jnp.zeros_like(carry_ref)

        @pl.when(n < nb)
        def _():
            valid, distf = _swa_mask(n)
            for kvh in range(N_KV):
                k2 = _block_diag(kvp_ref, kvc_ref, kvh * HEAD_DIM)
                v2 = _block_diag(kvp_ref, kvc_ref, 128 + kvh * HEAD_DIM)
                dk2 = jnp.zeros((4 * BLK, PAIR), F32)
                dv2 = jnp.zeros((4 * BLK, PAIR), F32)
                for jp in range(PAIRS_PER_KV):
                    pair = kvh * PAIRS_PER_KV + jp
                    lanes = slice(pair * PAIR, (pair + 1) * PAIR)
                    q2 = q_ref[:, lanes].astype(BF16)
                    do2 = do_ref[:, lanes].astype(BF16)
                    s2 = lax.dot_general(k2, q2, NT, preferred_element_type=F32)
                    dp2 = lax.dot_general(v2, do2, NT, preferred_element_type=F32)
                    probs, dss = [], []
                    for e in range(2):
                        hh = 2 * pair + e
                        lse_h = lse_ref[hh:hh + 1, :]
                        p = jnp.exp(_swa_scores(s2, e, hh, valid, distf) - lse_h)
                        dp = dp2[2 * BLK * e:2 * BLK * (e + 1)]
                        delta = jnp.sum(p * dp, axis=0, keepdims=True)
                        dsink_ref[hh:hh + 1, :] += -jnp.exp(sink_ref[0, hh] - lse_h) * delta
                        probs.append(p.astype(BF16))
                        dss.append((p * (dp - delta)).astype(BF16))
                    ds2 = jnp.concatenate(dss, axis=0)
                    dq_ref[:, lanes] = (lax.dot_general(ds2, k2, TN, preferred_element_type=F32)
                                        * (HEAD_DIM ** -0.5)).astype(BF16)
                    dk2 = dk2 + jnp.dot(ds2, q2, preferred_element_type=F32)
                    dv2 = dv2 + jnp.dot(jnp.concatenate(probs, axis=0), do2, preferred_element_type=F32)
                dk_cat = (dk2[:2 * BLK, :HEAD_DIM] + dk2[2 * BLK:, HEAD_DIM:]) * (HEAD_DIM ** -0.5)
                dv_cat = dv2[:2 * BLK, :HEAD_DIM] + dv2[2 * BLK:, HEAD_DIM:]
                ko = kvh * HEAD_DIM
                vo = 128 + kvh * HEAD_DIM
                dkv_ref[:, ko:ko + HEAD_DIM] = (carry_ref[:, ko:ko + HEAD_DIM] + dk_cat[:BLK]).astype(BF16)
                dkv_ref[:, vo:vo + HEAD_DIM] = (carry_ref[:, vo:vo + HEAD_DIM] + dv_cat[:BLK]).astype(BF16)
                carry_ref[:, ko:ko + HEAD_DIM] = dk_cat[BLK:]
                carry_ref[:, vo:vo + HEAD_DIM] = dv_cat[BLK:]

        @pl.when(n == nb)
        def _():
            dkv_ref[...] = carry_ref[...].astype(BF16)

    last = nb - 1
    res, xres = _call(
        body, name=name, grid=(nb + 1,),
        in_specs=[SMEM,
                  pl.BlockSpec((BLK, ATTN_W), lambda n: (jnp.minimum(n, last), 0)),
                  pl.BlockSpec((BLK, 256), lambda n: (jnp.minimum(n, last), kvb)),
                  pl.BlockSpec((BLK, 256), lambda n: (jnp.maximum(jnp.minimum(n, last) - 1, 0), kvb)),
                  pl.BlockSpec((BLK, ATTN_W), lambda n: (jnp.minimum(n, last), 0)),
                  pl.BlockSpec((None, N_Q, BLK), lambda n: (jnp.minimum(n, last), 0, 0))],
        out_specs=[pl.BlockSpec((BLK, ATTN_W), lambda n: (jnp.minimum(n, last), 0)),
                   pl.BlockSpec((BLK, 256), lambda n: (jnp.maximum(n - 1, 0), 0)),
                   pl.BlockSpec((N_Q, BLK), lambda n: (0, 0))],
        out_shape=[jax.ShapeDtypeStruct((t, ATTN_W), BF16), jax.ShapeDtypeStruct((t, 256), BF16),
                   jax.ShapeDtypeStruct((N_Q, BLK), F32)],
        scratch_shapes=[pltpu.VMEM((BLK, 256), F32)],
        args=[sinks, proj, proj, proj, dattn, lse], semantics=("arbitrary",), exchanges=exchanges)
    return (*res, xres) if exchanges else res


def _cumsum_rows(x):
    n = x.shape[0]
    row = lax.broadcasted_iota(jnp.int32, x.shape, 0)
    s = 1
    while s < n:
        x = x + jnp.where(row >= s, pltpu.roll(x, s, axis=0), 0.0)
        s *= 2
    return x


def _rev_cumsum_rows(x):
    n = x.shape[0]
    row = lax.broadcasted_iota(jnp.int32, x.shape, 0)
    s = 1
    while s < n:
        x = x + jnp.where(row < n - s, pltpu.roll(x, n - s, axis=0), 0.0)
        s *= 2
    return x


def _lower_bound(lbl_ref):
    l0 = lbl_ref[0:1, :]
    l1 = lbl_ref[1:2, :]
    mx = jnp.maximum(l0, l1)
    e0 = jnp.exp(l0 - mx)
    e1 = jnp.exp(l1 - mx)
    return e0 / (e0 + e1)


def _hgrn_gates(z, lb):
    sg = _sigmoid(z)
    f = lb + (1.0 - lb) * sg
    return sg, f, jnp.log(f), 1.0 - f


def _sub_factors(b, k, i, sub, trim):
    need = -(-sub * i // 16) * 16 if trim else CHUNK
    rows = lax.broadcasted_iota(jnp.int32, (need, RNN_HD), 0)
    ref = b[sub * i - 1:sub * i, :]
    qfac = jnp.exp(b[sub * i:sub * (i + 1), :] - ref)
    kfac = jnp.where(rows < sub * i, jnp.exp(ref - b[:need]), 0.0)
    kt = (k[:need] * kfac).astype(BF16)
    if need < CHUNK:
        kt = jnp.concatenate([kt, jnp.zeros((CHUNK - need, RNN_HD), BF16)], axis=0)
    return qfac, kfac, kt


def _diag_decay(bi, s):
    trow = lax.broadcasted_iota(jnp.int32, bi.shape, 0)
    return jnp.where(trow >= s, jnp.exp(bi - bi[s:s + 1, :]), 0.0)


def _hgrn_fwd(proj, lb_logits, norm_gain, *, tb, name, exchanges=()):
    t = proj.shape[0]
    ntb = t // tb
    nch = tb // CHUNK
    qb, fb, ib, gb = QR_COL // 128, FR_COL // 128, IR_COL // 128, GR_COL // 128

    def body(q_ref, f_ref, i_ref, g_ref, lbl_ref, gain_ref, o_ref, out_ref, s0_ref, ksave_ref, bsave_ref, st_ref):
        c = pl.program_id(1)

        @pl.when(c == 0)
        def _():
            st_ref[...] = jnp.zeros_like(st_ref)

        lb = _lower_bound(lbl_ref)
        gain = gain_ref[...]

        def chunk(ci, st):
            rows = slice(ci * CHUNK, (ci + 1) * CHUNK)
            _, _, lf, k = _hgrn_gates(f_ref[rows, :], lb)
            qr = q_ref[rows, :]
            q = qr * _sigmoid(qr)
            v = i_ref[rows, :]
            b = _cumsum_rows(lf)
            ksave_ref[rows, :] = k
            bsave_ref[rows, :] = b
            s0_ref[ci] = st
            o_inter = lax.dot_general((q * jnp.exp(b)).astype(BF16), st.astype(BF16), NT,
                                      preferred_element_type=F32)
            vb = v.astype(BF16)
            blast = b[CHUNK - 1:CHUNK, :]
            khat = (k * jnp.exp(blast - b)).astype(BF16)
            st = st * jnp.exp(blast) + lax.dot_general(vb, khat, TN, preferred_element_type=F32)
            blocks = []
            for i in range(CHUNK // SUB_FWD):
                blk = slice(SUB_FWD * i, SUB_FWD * (i + 1))
                qi, ki, vi, bi = q[blk], k[blk], v[blk], b[blk]
                oi = o_inter[blk]
                if i > 0:
                    qfac, _, kt = _sub_factors(b, k, i, SUB_FWD, trim=True)
                    att = lax.dot_general((qi * qfac).astype(BF16), kt, NT,
                                          preferred_element_type=F32)
                    oi = oi + jnp.dot(att.astype(BF16), vb, preferred_element_type=F32)
                for s in range(SUB_FWD):
                    qe = qi * _diag_decay(bi, s)
                    a = jnp.sum(qe * ki[s:s + 1, :], axis=1, keepdims=True)
                    oi = oi + a * vi[s:s + 1, :]
                blocks.append(oi)
            o = jnp.concatenate(blocks, axis=0)
            o_ref[rows, :] = o
            gr = g_ref[rows, :]
            out_ref[rows, :] = (o * _rstd(o) * gain * (gr * _sigmoid(gr))).astype(BF16)
            return st

        st = st_ref[...]
        for ci in range(nch):
            st = chunk(ci, st)
        st_ref[...] = st

    def col(base):
        return pl.BlockSpec((tb, RNN_HD), lambda h, c: (c, base + h))

    res, xres = _call(
        body, name=name, grid=(N_RNN, ntb),
        in_specs=[col(qb), col(fb), col(ib), col(gb),
                  pl.BlockSpec((2, RNN_HD), lambda h, c: (0, h)), pl.BlockSpec((1, RNN_HD), lambda h, c: (0, 0))],
        out_specs=[pl.BlockSpec((tb, RNN_HD), lambda h, c: (c, h)), pl.BlockSpec((tb, RNN_HD), lambda h, c: (c, h)),
                   pl.BlockSpec((None, nch, RNN_HD, RNN_HD), lambda h, c: (h, c, 0, 0)),
                   pl.BlockSpec((tb, RNN_HD), lambda h, c: (c, h)), pl.BlockSpec((tb, RNN_HD), lambda h, c: (c, h))],
        out_shape=[jax.ShapeDtypeStruct((t, RNN_W), F32), jax.ShapeDtypeStruct((t, RNN_W), BF16),
                   jax.ShapeDtypeStruct((N_RNN, t // CHUNK, RNN_HD, RNN_HD), F32),
                   jax.ShapeDtypeStruct((t, RNN_W), F32), jax.ShapeDtypeStruct((t, RNN_W), F32)],
        scratch_shapes=[pltpu.VMEM((RNN_HD, RNN_HD), F32)],
        args=[proj, proj, proj, proj, lb_logits, norm_gain],
        semantics=("parallel", "arbitrary"), exchanges=exchanges)
    return (*res, xres) if exchanges else res


def _hgrn_bwd(proj, lb_logits, norm_gain, o_pre, s0, k_gate, b_cum, dcat, *, tb, name, exchanges=()):
    t = proj.shape[0]
    ntb = t // tb
    nch = tb // CHUNK
    qb, fb, ib, gb = QR_COL // 128, FR_COL // 128, IR_COL // 128, GR_COL // 128
    sub = SUB_BWD
    nsub = CHUNK // sub

    def body(q_ref, k_ref, b_ref, i_ref, g_ref, lbl_ref, gain_ref, o_ref, s0_ref, dout_ref,
             dq_ref, df_ref, di_ref, dg_ref, dlb_ref, dgain_ref,
             dst_ref, dqs_ref, dks_ref, dvs_ref):
        c = pl.program_id(1)

        @pl.when(c == 0)
        def _():
            dst_ref[...] = jnp.zeros_like(dst_ref)
            dlb_ref[...] = jnp.zeros_like(dlb_ref)
            dgain_ref[...] = jnp.zeros_like(dgain_ref)

        lb = _lower_bound(lbl_ref)
        inv_1mlb = 1.0 / (1.0 - lb)
        gain = gain_ref[...]

        def chunk(ci, dst):
            rows = slice(ci * CHUNK, (ci + 1) * CHUNK)
            dqa_ref, dka_ref, dva_ref = dqs_ref.at[ci], dks_ref.at[ci], dvs_ref.at[ci]
            k = k_ref[rows, :]
            b = b_ref[rows, :]
            f = 1.0 - k
            one_minus_sg = k * inv_1mlb
            qr = q_ref[rows, :]
            sq = _sigmoid(qr)
            q = qr * sq
            v = i_ref[rows, :]

            dout = dout_ref[rows, :].astype(F32)
            o = o_ref[rows, :]
            gr = g_ref[rows, :]
            sgg = _sigmoid(gr)
            gate = gr * sgg
            rs = _rstd(o)
            nrm = o * rs
            dg_ref[rows, :] = (dout * nrm * gain * (sgg * (1.0 + gr * (1.0 - sgg)))).astype(BF16)
            dn = dout * gate
            dgain_ref[...] += jnp.sum(dn * nrm, axis=0, keepdims=True)
            tt = dn * gain
            do = rs * (tt - nrm * jnp.mean(tt * nrm, axis=-1, keepdims=True))

            dob = do.astype(BF16)
            vb = v.astype(BF16)
            eb = jnp.exp(b)
            blast = b[CHUNK - 1:CHUNK, :]
            ebl = jnp.exp(blast - b)
            dstb = dst.astype(BF16)
            khat = (k * ebl).astype(BF16)
            s0 = s0_ref[ci]
            dqa_ref[...] = eb * jnp.dot(dob, s0.astype(BF16), preferred_element_type=F32)
            dk_state = ebl * jnp.dot(vb, dstb, preferred_element_type=F32)
            dka_ref[...] = dk_state
            d_blast = (jnp.sum(k * dk_state, axis=0, keepdims=True)
                       + jnp.exp(blast) * jnp.sum(dst * s0, axis=0, keepdims=True))
            dva_ref[...] = lax.dot_general(khat, dstb, NT, preferred_element_type=F32)
            dst_next = dst * jnp.exp(blast) + lax.dot_general(dob, (q * eb).astype(BF16), TN,
                                                              preferred_element_type=F32)
            pm = lax.dot_general(dob, vb, NT, preferred_element_type=F32)
            for i in range(nsub):
                blk = slice(sub * i, sub * (i + 1))
                qi, ki, vi, bi, doi = q[blk], k[blk], v[blk], b[blk], do[blk]
                dqi = dqa_ref[blk, :]
                if i > 0:
                    qfac, kfac, kt = _sub_factors(b, k, i, sub, trim=False)
                    qt = (qi * qfac).astype(BF16)
                    att = lax.dot_general(qt, kt, NT, preferred_element_type=F32).astype(BF16)
                    pmi = pm[blk, :].astype(BF16)
                    dva_ref[...] += lax.dot_general(att, doi.astype(BF16), TN, preferred_element_type=F32)
                    dqi = dqi + qfac * jnp.dot(pmi, kt, preferred_element_type=F32)
                    dka_ref[...] += kfac * lax.dot_general(pmi, qt, TN, preferred_element_type=F32)
                dqa_ref[blk, :] = dqi
                srow = lax.broadcasted_iota(jnp.int32, (sub, RNN_HD), 0)
                dki = jnp.zeros((sub, RNN_HD), F32)
                dvi = jnp.zeros((sub, RNN_HD), F32)
                for tq in range(sub):
                    qt, dot_ = qi[tq:tq + 1, :], doi[tq:tq + 1, :]
                    e = jnp.where(srow <= tq, jnp.exp(bi[tq:tq + 1, :] - bi), 0.0)
                    ke = ki * e
                    p = jnp.sum(vi * dot_, axis=1, keepdims=True)
                    a = jnp.sum(ke * qt, axis=1, keepdims=True)
                    dki = dki + p * (qt * e)
                    dvi = dvi + a * dot_
                    row = slice(sub * i + tq, sub * i + tq + 1)
                    dqa_ref[row, :] += jnp.sum(p * ke, axis=0, keepdims=True)
                dka_ref[blk, :] += dki
                dva_ref[blk, :] += dvi

            dq = dqa_ref[...]
            dk = dka_ref[...]
            lastrow = lax.broadcasted_iota(jnp.int32, (CHUNK, RNN_HD), 0) == CHUNK - 1
            dlf = _rev_cumsum_rows(q * dq - k * dk + jnp.where(lastrow, d_blast, 0.0))
            dff = dlf / f - dk
            df_ref[rows, :] = (dff * k * (1.0 - one_minus_sg)).astype(BF16)
            dlb_ref[...] += jnp.sum(dff * one_minus_sg, axis=0, keepdims=True)
            dq_ref[rows, :] = (dq * (sq * (1.0 + qr * (1.0 - sq)))).astype(BF16)
            di_ref[rows, :] = dva_ref[...].astype(BF16)
            return dst_next

        dst = dst_ref[...]
        for ci in reversed(range(nch)):
            dst = chunk(ci, dst)
        dst_ref[...] = dst

    def col(base):
        return pl.BlockSpec((tb, RNN_HD), lambda h, c: (ntb - 1 - c, base + h))

    outc = pl.BlockSpec((tb, RNN_HD), lambda h, c: (ntb - 1 - c, h))
    hb = ATTN_W // RNN_HD
    res, xres = _call(
        body, name=name, grid=(N_RNN, ntb),
        in_specs=[col(qb), outc, outc, col(ib), col(gb),
                  pl.BlockSpec((2, RNN_HD), lambda h, c: (0, h)), pl.BlockSpec((1, RNN_HD), lambda h, c: (0, 0)),
                  outc,
                  pl.BlockSpec((None, nch, RNN_HD, RNN_HD), lambda h, c: (h, ntb - 1 - c, 0, 0)),
                  pl.BlockSpec((tb, RNN_HD), lambda h, c: (ntb - 1 - c, hb + h))],
        out_specs=[outc, outc, outc, outc,
                   pl.BlockSpec((1, RNN_HD), lambda h, c: (0, h)),
                   pl.BlockSpec((None, 1, RNN_HD), lambda h, c: (h, 0, 0))],
        out_shape=[jax.ShapeDtypeStruct((t, RNN_W), BF16)] * 4
        + [jax.ShapeDtypeStruct((1, RNN_W), F32), jax.ShapeDtypeStruct((N_RNN, 1, RNN_HD), F32)],
        scratch_shapes=[pltpu.VMEM((RNN_HD, RNN_HD), F32),
                        pltpu.VMEM((nch, CHUNK, RNN_HD), F32), pltpu.VMEM((nch, CHUNK, RNN_HD), F32),
                        pltpu.VMEM((nch, CHUNK, RNN_HD), F32)],
        args=[proj, k_gate, b_cum, proj, proj, lb_logits, norm_gain, o_pre, s0, dcat],
        semantics=("parallel", "arbitrary"), exchanges=exchanges)
    return (*res, xres) if exchanges else res


def _cast_slots(w, where, *, name):
    _, rows, cols = w.shape
    rh = rows // 2
    tr = _row_tile(rh, cols)
    nh = rh // tr

    def body(wh_ref, w_ref, o_ref):
        o_ref[...] = w_ref[...].astype(BF16)

    return pl.pallas_call(
        body, name=name,
        grid_spec=pltpu.PrefetchScalarGridSpec(
            num_scalar_prefetch=1, grid=(2, nh),
            in_specs=[pl.BlockSpec((None, tr, cols), lambda h, i, wh: (0, h * nh + i, 0))],
            out_specs=pl.BlockSpec((None, tr, cols), lambda h, i, wh: (2 * wh[0] + h, i, 0))),
        out_shape=jax.ShapeDtypeStruct((8, rh, cols), BF16),
        compiler_params=_params(("parallel", "parallel")),
    )(where, w)


def _row_tile(rows, cols, budget=1 << 20):
    tr = rows
    while tr * cols > budget and tr % 16 == 0:
        tr //= 2
    return tr


def _half_spec(g, tr, halves_last, slab):
    if halves_last:
        return pl.BlockSpec((None, tr, g.shape[2] // 2), lambda *a: (slab(*a), a[-2], a[-1][1]))
    return pl.BlockSpec((None, None, tr, g.shape[3]), lambda *a: (slab(*a), a[-1][1], a[-2], 0))


def _pair_sum(g, sib, where, *, name, halves_last=False):
    rh, cols = sib.shape[1:]
    tr = _row_tile(rh, cols)

    def body(w_ref, g_ref, s_ref, o_ref):
        o_ref[...] = (g_ref[...] + s_ref[...]).astype(BF16)

    def foreign(s, i, w):
        return (w[0] + 1 + s) % N_CHIPS

    return pl.pallas_call(
        body, name=name,
        grid_spec=pltpu.PrefetchScalarGridSpec(
            num_scalar_prefetch=1, grid=(N_CHIPS - 1, rh // tr),
            in_specs=[_half_spec(g, tr, halves_last, foreign),
                      pl.BlockSpec((None, tr, cols), lambda s, i, w: (foreign(s, i, w), i, 0))],
            out_specs=pl.BlockSpec((None, tr, cols), lambda s, i, w: (foreign(s, i, w), i, 0))),
        out_shape=jax.ShapeDtypeStruct((4, rh, cols), BF16),
        compiler_params=_params(("parallel", "parallel")),
    )(where, g, sib)


def _final_half(g, sib, recv, where, *, name, halves_last=False):
    rh, cols = sib.shape[1:]
    tr = _row_tile(rh, cols)

    def body(w_ref, g_ref, s_ref, r_ref, o_ref):
        acc = g_ref[...] + s_ref[...]
        for j in range(3):
            acc = acc + r_ref[j].astype(F32)
        o_ref[...] = acc

    return pl.pallas_call(
        body, name=name,
        grid_spec=pltpu.PrefetchScalarGridSpec(
            num_scalar_prefetch=1, grid=(rh // tr,),
            in_specs=[_half_spec(g, tr, halves_last, lambda i, w: w[0]),
                      pl.BlockSpec((None, tr, cols), lambda i, w: (w[0], i, 0)),
                      pl.BlockSpec((3, tr, cols), lambda i, w: (0, i, 0))],
            out_specs=pl.BlockSpec((tr, cols), lambda i, w: (i, 0))),
        out_shape=jax.ShapeDtypeStruct((rh, cols), F32),
        compiler_params=_params(("parallel",)),
    )(where, g, sib, recv)


def _adamw_math(w, g, m, v):
    m = ADAM_B1 * m + (1.0 - ADAM_B1) * g
    v = ADAM_B2 * v + (1.0 - ADAM_B2) * (g * g)
    m_hat = m / (1.0 - ADAM_B1 ** ADAM_STEP)
    v_hat = v / (1.0 - ADAM_B2 ** ADAM_STEP)
    delta = -ADAM_LR * (m_hat / (jnp.sqrt(v_hat) + ADAM_EPS) + ADAM_WD * w)
    return delta, m, v


def _adamw(w, mine, theirs, m, v, where, *, name, halves_last=False):
    _, rows, cols = w.shape
    if halves_last:
        cols //= 2
        tr = _row_tile(rows, cols, budget=1 << 19)
        grid = (rows // tr, 2)
        blk = pl.BlockSpec((None, tr, cols), lambda i, h, wh: (0, i, h))
        mine_spec = theirs_spec = pl.BlockSpec((tr, cols), lambda i, h, wh: (i, 0))
        which = lambda: pl.program_id(1)
    else:
        tr = _row_tile(rows // 2, cols, budget=1 << 19)
        nh = rows // 2 // tr
        grid = (rows // tr,)
        blk = pl.BlockSpec((None, tr, cols), lambda i, wh: (0, i, 0))
        mine_spec = pl.BlockSpec((tr, cols), lambda i, wh: (jnp.where(i // nh == wh[1], i % nh, 0), 0))
        theirs_spec = pl.BlockSpec((tr, cols), lambda i, wh: (jnp.where(i // nh == wh[1], 0, i % nh), 0))
        which = lambda: pl.program_id(0) // nh

    def body(wh_ref, w_ref, a_ref, b_ref, m_ref, v_ref, g_ref, d_ref, nm_ref, nv_ref):
        g = jnp.where(which() == wh_ref[1], a_ref[...], b_ref[...])
        d, nm, nv = _adamw_math(w_ref[...], g, m_ref[...], v_ref[...])
        g_ref[...] = g
        d_ref[...] = d
        nm_ref[...] = nm
        nv_ref[...] = nv

    rows, cols = w.shape[1:]
    return pl.pallas_call(
        body, name=name,
        grid_spec=pltpu.PrefetchScalarGridSpec(
            num_scalar_prefetch=1, grid=grid,
            in_specs=[blk, mine_spec, theirs_spec, blk, blk], out_specs=[blk] * 4),
        out_shape=[jax.ShapeDtypeStruct((1, rows, cols), F32)] * 4,
        compiler_params=_params(("parallel",) * len(grid)),
    )(where, w, mine, theirs, m, v)


SEG_LOSS = 0
SEG_SINK = 128
SEG_AGAIN = 256
SEG_L0 = SEG_AGAIN + ATTN_W
SEG_L1 = SEG_L0 + RNN_W
SEG_RGAIN = SEG_L1 + RNN_W
SEG_G = SEG_RGAIN + 128
N_PACK = SEG_G + 4 * D_MODEL


def _pack(sinks, again, l0, l1, rgain, gains, loss=None):
    z = lambda k: jnp.zeros((1, k), F32)
    first = z(128) if loss is None else loss
    return jnp.concatenate([first, sinks, z(128 - N_Q), again, l0, l1, rgain] + list(gains), axis=1)


def _small_reduce_adamw(part, w, m, v, *, name):
    def body(p_ref, w_ref, m_ref, v_ref, g_ref, d_ref, nm_ref, nv_ref, buf_ref, send_sems, recv_sems):
        x, y, c = _place()
        me = 4 * x + 2 * y + c
        copies = []
        for k in range(1, 8):
            dx, dy, dc = (k >> 2) & 1, (k >> 1) & 1, k & 1
            to = (x ^ dx, y ^ dy, c ^ dc)
            cp = pltpu.make_async_remote_copy(
                src_ref=p_ref, dst_ref=buf_ref.at[me],
                send_sem=send_sems.at[k - 1], recv_sem=recv_sems.at[k - 1],
                device_id=to, device_id_type=MESH)
            cp.start()
            copies.append(cp)
        buf_ref[me] = p_ref[...]
        for cp in copies:
            cp.wait()
        tot = buf_ref[0]
        for j in range(1, 8):
            tot = tot + buf_ref[j]
        g_ref[...] = tot
        l0 = w_ref[:, SEG_L0:SEG_L0 + RNN_W]
        l1 = w_ref[:, SEG_L1:SEG_L1 + RNN_W]
        mx = jnp.maximum(l0, l1)
        e0 = jnp.exp(l0 - mx)
        e1 = jnp.exp(l1 - mx)
        lb = e0 / (e0 + e1)
        gl0 = tot[:, SEG_L0:SEG_L0 + RNN_W] * lb * (1.0 - lb)
        g_ref[:, SEG_L0:SEG_L0 + RNN_W] = gl0
        g_ref[:, SEG_L1:SEG_L1 + RNN_W] = -gl0
        d, nm, nv = _adamw_math(w_ref[...], g_ref[...], m_ref[...], v_ref[...])
        d_ref[...] = d
        nm_ref[...] = nm
        nv_ref[...] = nv

    vm = pl.BlockSpec(memory_space=pltpu.VMEM)
    return pl.pallas_call(
        body, name=name,
        in_specs=[vm] * 4, out_specs=[vm] * 4,
        out_shape=[jax.ShapeDtypeStruct((1, N_PACK), F32)] * 4,
        scratch_shapes=[pltpu.VMEM((8, 1, N_PACK), F32), pltpu.SemaphoreType.DMA((7,)),
                        pltpu.SemaphoreType.DMA((7,))],
    )(part, w, m, v)


def _layer_grads(xs, tgt, bufs, where, sinks, again, lb_logits, rgain,
                 g_mix_pre, g_mix_post, g_mlp_pre, g_mlp_post):
    tm = 512
    b_in, b_out, b_up, b_dn = bufs

    shard = IN_W // N_CHIPS
    h1, b_in = _rms_cast_gather(xs, g_mix_pre, b_in, tm=tm, name="h1_norm_gather_w_in")
    w_in_t = b_in.reshape(IN_W, D_MODEL)
    proj, ((b_out, b_up),) = _mm(
        h1, w_in_t, tm=1024, tn=768, tk=D_MODEL, out_dtype=F32, w_layout="nk", name="in_proj",
        exchanges=[_x_gather([b_out, b_up], ici=[(0, 256), (0, 336)])])
    attn, lse, ((b_out, b_up),) = _swa_fwd(
        proj, sinks, name="swa_fwd",
        exchanges=[_x_gather([b_out, b_up], ici=[None, (336, 320)], d2d=[(0, 256), None])])
    w_out = b_out.reshape(D_MODEL, D_MODEL)
    o_pre, rnn, s0, k_gate, b_cum, ((b_up, b_dn),) = _hgrn_fwd(
        proj, lb_logits, rgain, tb=1024, name="hgrn_fwd",
        exchanges=[_x_gather([b_up, b_dn], ici=[(656, 368), (0, 352)])])
    cat = _mix_cat(attn, rnn, again, tm=tm, name="mix_cat")
    mixed, ((b_up, b_dn),) = _mm(
        cat, w_out, tm=1024, tn=1024, tk=D_MODEL, out_dtype=BF16, name="out_proj",
        exchanges=[_x_gather([b_up, b_dn], ici=[None, (352, 240)], d2d=[(0, 1024), (0, 352)])])
    w_up4 = b_up.reshape(N_CHIPS, D_MODEL, D_FF // N_CHIPS)
    x1, h2, ((b_dn,),) = _post_norm_res(
        mixed, g_mix_post, xs, g_mlp_pre, tm=tm, name="mix_post",
        exchanges=[_x_gather([b_dn], d2d=[(352, 240)])])
    u, ((b_dn,),) = _mm(h2, w_up4, tm=1024, tn=1024, tk=D_MODEL, out_dtype=BF16, relu=True, w_layout="skn",
                        name="mlp_up", exchanges=[_x_gather([b_dn], ici=[(592, 432)], cross=[(592, 432)])])
    w_dn = b_dn.reshape(D_FF, D_MODEL)
    yv = _mm(u, w_dn, tm=1024, tn=1024, tk=2048, out_dtype=BF16, a_square=True, name="mlp_down")
    dy, dx2, loss_row, dg_mlp_post = _loss_head(yv, g_mlp_post, x1, tgt, tm=tm, name="loss_head")

    def halved(g):
        return g.reshape(N_CHIPS, 2, g.shape[1] // 2, g.shape[2])
    du = _mm(dy, w_dn, tm=1024, tn=1024, tk=D_MODEL, out_dtype=BF16, mul2=u, w_layout="nk", name="mlp_down_bwd")
    g_dn = halved(_mm_tn(u, dy, tm=1024, tn=1024, tt=2048, a_square=True, name="w_down_grad")
                  .reshape(N_CHIPS, D_FF // N_CHIPS, D_MODEL))
    d_w_up, ((sib_dn,),) = _mm_tn(h2, du, tm=1024, tn=1024, tt=2048, n_split=N_CHIPS, name="w_up_grad",
                                  exchanges=[_x_pair([g_dn])])
    g_up = halved(d_w_up)
    wire_dn = _pair_sum(g_dn, sib_dn, where, name="pair_sum_w_down")
    dh2, ((recv_dn,), (sib_up,)) = _mm(du, w_up4, tm=1024, tn=1024, tk=2048, out_dtype=BF16, w_layout="snk", name="mlp_up_bwd",
                                       exchanges=[_x_chip([wire_dn], rows=[(0, 928)]), _x_pair([g_up])])
    wire_up = _pair_sum(g_up, sib_up, where, name="pair_sum_w_up")
    dx1, dg_mlp_pre = _rms_bwd(dh2, x1, g_mlp_pre, dx2, tm=tm, out_dtype=BF16, name="mlp_pre_bwd")
    dmixed, dg_mix_post = _rms_bwd(dx1, mixed, g_mix_post, None, tm=tm, out_dtype=BF16, name="mix_post_bwd")
    d_w_out, ((recv_dn,),) = _mm_tn(cat, dmixed, tm=1024, tn=1024, tt=2048, name="w_out_grad",
                                    exchanges=[_x_chip([wire_dn], rows=[(928, 96)], into=[recv_dn])])
    fin_dn = _final_half(g_dn, sib_dn, recv_dn, where, name="final_half_w_down")
    g_out = halved(d_w_out.reshape(N_CHIPS, D_MODEL // N_CHIPS, D_MODEL))
    dcat, ((sib_out,), (oth_dn,)) = _mm(dmixed, w_out, tm=1024, tn=1024, tk=D_MODEL, out_dtype=BF16, w_layout="nk",
                                        name="out_proj_bwd", exchanges=[_x_pair([g_out]), _x_share([fin_dn])])
    wire_out = _pair_sum(g_out, sib_out, where, name="pair_sum_w_out")
    dattn, dg_again = _rms_bwd(dcat, attn, again, None, tm=tm, out_dtype=BF16, name="attn_norm_bwd")
    dq_a, dkv, dsinks, ((recv_up,),) = _swa_bwd(
        proj, sinks, dattn, lse, name="swa_bwd", exchanges=[_x_chip([wire_up], rows=[(0, 512)])])
    dq_r, df_r, di_r, dg_r, dlb, dgain_h, ((recv_up,), (recv_out,)) = _hgrn_bwd(
        proj, lb_logits, rgain, o_pre, s0, k_gate, b_cum, dcat, tb=1024, name="hgrn_bwd",
        exchanges=[_x_chip([wire_up], rows=[(512, 512)], into=[recv_up]), _x_chip([wire_out])])
    fin_up = _final_half(g_up, sib_up, recv_up, where, name="final_half_w_up")
    fin_out = _final_half(g_out, sib_out, recv_out, where, name="final_half_w_out")
    dproj = jnp.concatenate([dq_a, dkv, dq_r, df_r, di_r, dg_r], axis=1)
    piece_cols = D_MODEL // 4

    def w_in_piece(pc, exchanges):
        d, xres = _mm_tn(dproj, h1, tm=896, tn=2 * piece_cols, tt=2048, b_blocks=(pc, pc + 2),
                         name="w_in_grad_%d" % pc, exchanges=exchanges)
        return d.reshape(N_CHIPS, shard, 2 * piece_cols), xres

    g_in0, ((oth_up, oth_out),) = w_in_piece(0, [_x_share([fin_up, fin_out])])
    g_in1, ((sib_in0,),) = w_in_piece(1, [_x_pair([g_in0], halves_last=True)])
    wire_in0 = _pair_sum(g_in0, sib_in0, where, name="pair_sum_w_in_0", halves_last=True)
    dh1, ((recv_in0,), (sib_in1,)) = _mm(
        dproj, w_in_t, tm=1024, tn=1024, tk=2688, out_dtype=BF16, m_blocks=(0, 2), name="in_proj_bwd_0",
        exchanges=[_x_chip([wire_in0]), _x_pair([g_in1], halves_last=True)])
    wire_in1 = _pair_sum(g_in1, sib_in1, where, name="pair_sum_w_in_1", halves_last=True)
    dh1, ((recv_in1,),) = _mm(
        dproj, w_in_t, tm=1024, tn=1024, tk=2688, out_dtype=BF16, m_blocks=(2, 2), out_into=dh1,
        name="in_proj_bwd_1", exchanges=[_x_chip([wire_in1])])
    gx, dg_mix_pre = _rms_bwd(dh1, xs, g_mix_pre, dx1, tm=tm, out_dtype=F32, name="mix_pre_bwd")
    fin_in0 = _final_half(g_in0, sib_in0, recv_in0, where, name="final_half_w_in_0", halves_last=True)
    fin_in1 = _final_half(g_in1, sib_in1, recv_in1, where, name="final_half_w_in_1", halves_last=True)
    oth_in0, oth_in1 = _run_exchange(_x_share([fin_in0, fin_in1]), name="share_w_in")
    fin_in = jnp.concatenate([fin_in0, fin_in1], axis=1)
    oth_in = jnp.concatenate([oth_in0, oth_in1], axis=1)

    big = [(fin_in, oth_in), (fin_out, oth_out), (fin_up, oth_up), (fin_dn, oth_dn)]
    drgain = jnp.sum(dgain_h, axis=0)
    small = _pack(jnp.sum(dsinks, axis=1)[None, :], dg_again, dlb, jnp.zeros_like(dlb), drgain,
                  [dg_mix_pre, dg_mix_post, dg_mlp_pre, dg_mlp_post], loss=loss_row)
    return gx, big, small


def kernel(x, w_in, attn_sinks, attn_out_gain, rnn_lb_logits, rnn_norm_gain, w_out, mix_pre_gain, mix_post_gain, mlp_pre_gain, mlp_post_gain, w_up, w_down, loss_target, m_w_in, m_attn_sinks, m_attn_out_gain, m_rnn_lb_logits, m_rnn_norm_gain, m_w_out, m_mix_pre_gain, m_mix_post_gain, m_mlp_pre_gain, m_mlp_post_gain, m_w_up, m_w_down, v_w_in, v_attn_sinks, v_attn_out_gain, v_rnn_lb_logits, v_rnn_norm_gain, v_w_out, v_mix_pre_gain, v_mix_post_gain, v_mlp_pre_gain, v_mlp_post_gain, v_w_up, v_w_down):
    ax, ay, ac = _place()
    where = jnp.stack([2 * ax + ay, ac]).astype(jnp.int32)
    t = lambda a: jnp.swapaxes(a, 1, 2)
    big_w = [t(w_in), w_out, w_up, w_down]
    big_m = [t(m_w_in), m_w_out, m_w_up, m_w_down]
    big_v = [t(v_w_in), v_w_out, v_w_up, v_w_down]

    names = ["w_in", "w_out", "w_up", "w_down"]
    bufs = [_cast_slots(w, where, name="cast_" + nm) for w, nm in zip(big_w, names)]
    gx, big_g, small_part = _layer_grads(
        x[0], loss_target[0], bufs, where, attn_sinks, attn_out_gain, rnn_lb_logits, rnn_norm_gain,
        mix_pre_gain, mix_post_gain, mlp_pre_gain, mlp_post_gain)

    grads, deltas, new_m, new_v = [], [], [], []
    for (f, o), w, m, v, nm in zip(big_g, big_w, big_m, big_v, names):
        res = _adamw(w, f, o, m, v, where, name="adamw_" + nm, halves_last=(nm == "w_in"))
        if nm == "w_in":
            res = [t(r) for r in res]
        g, d, nm_, nv_ = res
        grads.append(g)
        deltas.append(d)
        new_m.append(nm_)
        new_v.append(nv_)

    def pack_params(sinks, again, logits, rgain, gains):
        return _pack(sinks, again, logits[0:1], logits[1:2], rgain, gains)

    pw = pack_params(attn_sinks, attn_out_gain, rnn_lb_logits, rnn_norm_gain,
                     [mix_pre_gain, mix_post_gain, mlp_pre_gain, mlp_post_gain])
    pm = pack_params(m_attn_sinks, m_attn_out_gain, m_rnn_lb_logits, m_rnn_norm_gain,
                     [m_mix_pre_gain, m_mix_post_gain, m_mlp_pre_gain, m_mlp_post_gain])
    pv = pack_params(v_attn_sinks, v_attn_out_gain, v_rnn_lb_logits, v_rnn_norm_gain,
                     [v_mix_pre_gain, v_mix_post_gain, v_mlp_pre_gain, v_mlp_post_gain])
    packs = _small_reduce_adamw(small_part, pw, pm, pv, name="small_reduce_adamw")

    def unpack(p):
        seg = lambda o, k: p[:, o:o + k]
        logits = jnp.concatenate([seg(SEG_L0, RNN_W), seg(SEG_L1, RNN_W)], axis=0)
        gains = [seg(SEG_G + i * D_MODEL, D_MODEL) for i in range(4)]
        return dict(sinks=seg(SEG_SINK, N_Q), again=seg(SEG_AGAIN, ATTN_W), logits=logits,
                    rgain=seg(SEG_RGAIN, RNN_HD), gains=gains)

    def order(small, big):
        return [big[0], small["sinks"], small["again"], small["logits"], small["rgain"], big[1],
                *small["gains"], big[2], big[3]]

    loss = packs[0][0, 0]
    outs = [loss, gx[None]]
    for p, b in zip(packs, [grads, deltas, new_m, new_v]):
        outs += order(unpack(p), b)
    return tuple(outs)
```

```python
import functools

import jax
import jax.numpy as jnp
from jax import lax
from jax.experimental import pallas as pl
from jax.experimental.pallas import tpu as pltpu

F32 = jnp.float32
BF16 = jnp.bfloat16
MESH = pl.DeviceIdType.MESH

EPS = 1e-6
D_MODEL = 2048
ATTN_W = 1024
HEAD_DIM = 64
N_Q = 16
N_KV = 2
GROUP = 8
BLK = 128
RNN_W = 1024
RNN_HD = 128
N_RNN = 8
CHUNK = 64
SUB_FWD = 16
SUB_BWD = 8
D_FF = 8192
IN_W = 5376
N_CHIPS = 4
KV_COL = ATTN_W
QR_COL = ATTN_W + 2 * 128
FR_COL = QR_COL + RNN_W
IR_COL = FR_COL + RNN_W
GR_COL = IR_COL + RNN_W

ADAM_LR = 0.001
ADAM_B1 = 0.9
ADAM_B2 = 0.999
ADAM_EPS = 1e-08
ADAM_WD = 0.01
ADAM_STEP = 10

VMEM_LIMIT = 48 * 1024 * 1024

NT = (((1,), (1,)), ((), ()))
TN = (((0,), (0,)), ((), ()))


def _params(sem=None):
    return pltpu.CompilerParams(dimension_semantics=sem, vmem_limit_bytes=VMEM_LIMIT)


def _sigmoid(x):
    return 1.0 / (1.0 + jnp.exp(-x))


ANY = pl.BlockSpec(memory_space=pl.ANY)


def _place():
    return lax.axis_index("x"), lax.axis_index("y"), lax.axis_index("c")


def _other_chips(x, y):
    return [(1 - x, y), (x, 1 - y), (1 - x, 1 - y)]


class _Exchange:
    def __init__(self, srcs, outs, ncopy, build, aliases=None):
        self.srcs, self.outs, self.ncopy, self.build = list(srcs), list(outs), ncopy, build
        self.aliases = aliases or {}


def _remote(src, dst, send_sems, recv_sems, k, to):
    return pltpu.make_async_remote_copy(src_ref=src, dst_ref=dst, send_sem=send_sems.at[k],
                                        recv_sem=recv_sems.at[k], device_id=to, device_id_type=MESH)


def _call(body, *, name, grid, in_specs, out_specs, out_shape, args, scratch_shapes=(), semantics=None,
          exchanges=(), into=None):
    in_specs, out_specs, out_shape = list(in_specs), list(out_specs), list(out_shape)
    scratch_shapes = list(scratch_shapes)
    ni, no, ns = len(in_specs), len(out_specs), len(scratch_shapes)
    xsrc = [s for x in exchanges for s in x.srcs]
    xout = [o for x in exchanges for o in x.outs]
    into = into or {}
    xsrc += [into[k] for k in sorted(into)]
    nxi, nxo = len(xsrc), len(xout)
    aliases = {nxi - len(into) + ni + q: k for q, k in enumerate(sorted(into))}
    a0 = b0 = 0
    for x in exchanges:
        for si, oi in x.aliases.items():
            aliases[ni + a0 + si] = no + b0 + oi
        a0 += len(x.srcs)
        b0 += len(x.outs)
    sems = []
    for x in exchanges:
        sems += [pltpu.SemaphoreType.DMA((x.ncopy,)), pltpu.SemaphoreType.DMA((x.ncopy,))]

    def wrapped(*refs):
        ins, xi = refs[:ni], refs[ni:ni + nxi]
        outs, xo = refs[ni + nxi:ni + nxi + no], refs[ni + nxi + no:ni + nxi + no + nxo]
        rest = refs[ni + nxi + no + nxo:]
        scr, sm = rest[:ns], rest[ns:]

        def copies():
            cps = []
            a = b = 0
            for k, x in enumerate(exchanges):
                cps += x.build(xi[a:a + len(x.srcs)], xo[b:b + len(x.outs)], sm[2 * k], sm[2 * k + 1])
                a += len(x.srcs)
                b += len(x.outs)
            return cps

        def start():
            for cp in copies():
                cp.start()

        def wait():
            for cp in copies():
                cp.wait()

        if not exchanges:
            body(*ins, *outs, *scr)
        elif not grid:
            start()
            body(*ins, *outs, *scr)
            wait()
        else:
            first = last = None
            for ax, g in enumerate(grid):
                f = pl.program_id(ax) == 0
                l = pl.program_id(ax) == g - 1
                first = f if first is None else first & f
                last = l if last is None else last & l
            pl.when(first)(start)
            body(*ins, *outs, *scr)
            pl.when(last)(wait)

    if exchanges and semantics is not None:
        semantics = ("arbitrary",) * len(grid)
    kwargs = dict(grid=grid) if grid else {}
    res = pl.pallas_call(
        wrapped, name=name,
        in_specs=in_specs + [ANY] * nxi, out_specs=out_specs + [ANY] * nxo,
        out_shape=out_shape + xout, scratch_shapes=scratch_shapes + sems,
        input_output_aliases=aliases,
        compiler_params=_params(semantics), **kwargs,
    )(*args, *xsrc)
    res = list(res)
    mine, theirs = res[:no], res[no:]
    per = []
    b = 0
    for x in exchanges:
        per.append(theirs[b:b + len(x.outs)])
        b += len(x.outs)
    return mine, per


def _run_exchange(x, *, name):
    return _call(lambda: None, name=name, grid=(), in_specs=[], out_specs=[], out_shape=[], args=[],
                 exchanges=[x])[1][0]


def _x_gather(bufs, ici=None, d2d=None, cross=None):
    n = len(bufs)
    plan = [(a, kind, rows[a]) for a in range(n) for kind, rows in (("ici", ici), ("d2d", d2d), ("cross", cross))
            if rows is not None and rows[a] is not None]

    def build(srcs, outs, ss, rs):
        x, y, c = _place()
        cps = []
        for q, (a, kind, rows) in enumerate(plan):
            piece = pl.ds(*rows)
            for j, (px, py) in enumerate(_other_chips(x, y)):
                if kind == "d2d":
                    slot, to = 4 * px + 2 * py + c, (x, y, 1 - c)
                else:
                    slot, to = 4 * x + 2 * y + c, (px, py, c if kind == "ici" else 1 - c)
                cps.append(_remote(srcs[a].at[slot, piece], outs[a].at[slot, piece], ss, rs, 3 * q + j, to))
        return cps

    outs = [jax.ShapeDtypeStruct(b.shape, b.dtype) for b in bufs]
    return _Exchange(bufs, outs, 3 * len(plan), build, aliases={a: a for a in range(n)})


def _x_pair(grads, halves_last=False):
    n = len(grads)

    def build(srcs, outs, ss, rs):
        x, y, c = _place()

        def half(r):
            if not halves_last:
                return r.at[:, 1 - c]
            ch = r.shape[2] // 2
            return r.at[:, :, pl.ds(pl.multiple_of((1 - c) * ch, 128), ch)]

        return [_remote(half(srcs[a]), outs[a], ss, rs, a, (x, y, 1 - c)) for a in range(n)]

    if halves_last:
        outs = [jax.ShapeDtypeStruct(g.shape[:2] + (g.shape[2] // 2,), g.dtype) for g in grads]
    else:
        outs = [jax.ShapeDtypeStruct((4,) + g.shape[2:], g.dtype) for g in grads]
    return _Exchange(grads, outs, n, build)


def _x_chip(wires, rows=None, into=None):
    n = len(wires)
    rows = rows or [(0, w.shape[1]) for w in wires]

    def build(srcs, outs, ss, rs):
        x, y, c = _place()
        cps = []
        for a in range(n):
            piece = pl.ds(*rows[a])
            for j, (px, py) in enumerate(_other_chips(x, y)):
                cps.append(_remote(srcs[a].at[2 * px + py, piece], outs[a].at[j, piece], ss, rs,
                                   3 * a + j, (px, py, c)))
        return cps

    outs = [jax.ShapeDtypeStruct((3,) + w.shape[1:], w.dtype) for w in wires]
    if into is None:
        return _Exchange(wires, outs, 3 * n, build)
    return _Exchange(list(wires) + list(into), outs, 3 * n, build, aliases={n + a: a for a in range(n)})


def _x_share(halves):
    n = len(halves)

    def build(srcs, outs, ss, rs):
        x, y, c = _place()
        return [_remote(srcs[a], outs[a], ss, rs, a, (x, y, 1 - c)) for a in range(n)]

    outs = [jax.ShapeDtypeStruct(h.shape, h.dtype) for h in halves]
    return _Exchange(halves, outs, n, build)


def _mm(a, w, *, tm, tn, tk, out_dtype, name, a_square=False, relu=False, mul2=None, w_layout="kn",
        m_blocks=None, out_into=None, exchanges=()):
    m, k = a.shape
    m_first, m_count = m_blocks or (0, m // tm)
    a_spec = pl.BlockSpec((tm, tk), lambda i, j, kk: (i + m_first, kk))
    if w_layout == "kn":
        n = w.shape[1]
        w_spec = pl.BlockSpec((tk, tn), lambda i, j, kk: (kk, j))
    elif w_layout == "nk":
        n = w.shape[0]
        w_spec = pl.BlockSpec((tn, tk), lambda i, j, kk: (j, kk))
    elif w_layout == "skn":
        n = w.shape[0] * w.shape[2]
        per_n = w.shape[2] // tn
        w_spec = pl.BlockSpec((None, tk, tn), lambda i, j, kk: (j // per_n, kk, j % per_n))
    else:
        assert w_layout == "snk"
        n = w.shape[1]
        per_k = w.shape[2] // tk
        w_spec = pl.BlockSpec((None, tn, tk), lambda i, j, kk: (kk // per_k, j, kk % per_k))
    w_dims = NT if w_layout in ("nk", "snk") else (((1,), (0,)), ((), ()))
    nk = k // tk
    assert m % tm == 0 and n % tn == 0 and k % tk == 0

    def body(*refs):
        if mul2 is not None:
            a_ref, w_ref, e_ref, o_ref, acc_ref = refs
        else:
            a_ref, w_ref, o_ref, acc_ref = refs
            e_ref = None
        kk = pl.program_id(2)
        av = a_ref[...]
        if a_square:
            af = av.astype(F32)
            av = (af * af).astype(BF16)
        part = lax.dot_general(av, w_ref[...], w_dims, preferred_element_type=F32)

        def finish(r):
            if relu:
                r = jnp.maximum(r, 0.0)
            if e_ref is not None:
                r = 2.0 * e_ref[...].astype(F32) * r
            o_ref[...] = r.astype(out_dtype)

        if nk == 1:
            finish(part)
        else:
            @pl.when(kk == 0)
            def _():
                acc_ref[...] = part

            @pl.when(kk > 0)
            def _():
                acc_ref[...] += part

            @pl.when(kk == nk - 1)
            def _():
                finish(acc_ref[...])

    in_specs = [a_spec, w_spec]
    args = [a, w]
    if mul2 is not None:
        in_specs.append(pl.BlockSpec((tm, tn), lambda i, j, kk: (i + m_first, j)))
        args.append(mul2)
    acc_shape = (tm, tn) if nk > 1 else (8, 128)
    (out,), per = _call(
        body, name=name, grid=(m_count, n // tn, nk),
        in_specs=in_specs, out_specs=[pl.BlockSpec((tm, tn), lambda i, j, kk: (i + m_first, j))],
        out_shape=[jax.ShapeDtypeStruct((m, n), out_dtype)], args=args,
        scratch_shapes=[pltpu.VMEM(acc_shape, F32)],
        semantics=("parallel", "parallel", "arbitrary"), exchanges=exchanges,
        into=None if out_into is None else {0: out_into})
    return (out, per) if exchanges else out


def _mm_tn(a, b, *, tm, tn, tt, name, a_square=False, n_split=1, b_blocks=None, exchanges=()):
    t, m = a.shape
    nb = len(b_blocks) if b_blocks else 1
    n = tn if b_blocks else b.shape[1]
    assert t % tt == 0 and m % tm == 0 and n % tn == 0 and (n // n_split) % tn == 0
    per = n // n_split // tn

    def body(a_ref, *refs):
        b_refs, o_ref = refs[:nb], refs[nb]
        ti = pl.program_id(2)
        av = a_ref[...]
        if a_square:
            af = av.astype(F32)
            av = (af * af).astype(BF16)
        bv = b_refs[0][...] if nb == 1 else jnp.concatenate([r[...] for r in b_refs], axis=1)
        part = lax.dot_general(av, bv, TN, preferred_element_type=F32)

        @pl.when(ti == 0)
        def _():
            o_ref[...] = part

        @pl.when(ti > 0)
        def _():
            o_ref[...] += part

    if b_blocks:
        b_specs = [pl.BlockSpec((tt, tn // nb), functools.partial(lambda blk, i, j, ti: (ti, blk), blk))
                   for blk in b_blocks]
    else:
        b_specs = [pl.BlockSpec((tt, tn), lambda i, j, ti: (ti, j))]
    (out,), xres = _call(
        body, name=name, grid=(m // tm, n // tn, t // tt),
        in_specs=[pl.BlockSpec((tt, tm), lambda i, j, ti: (ti, i))] + b_specs,
        out_specs=[pl.BlockSpec((None, tm, tn), lambda i, j, ti: (j // per, i, j % per))],
        out_shape=[jax.ShapeDtypeStruct((n_split, m, n // n_split), F32)], args=[a] + [b] * nb,
        semantics=("parallel", "parallel", "arbitrary"), exchanges=exchanges)
    return (out, xres) if exchanges else out


def _rstd(x):
    return lax.rsqrt(jnp.mean(x * x, axis=-1, keepdims=True) + EPS)


def _rms_cast_gather(x, g, buf, *, tm, name):
    t, d = x.shape
    steps = t // tm

    def body(x_ref, g_ref, b_in, o_ref, b_out, send_sems, recv_sems):
        i = pl.program_id(0)
        xc, yc, c = _place()
        chips = _other_chips(xc, yc)

        def slot(px, py, pc):
            return b_out.at[4 * px + 2 * py + pc]

        def sent(j):
            return _remote(b_in.at[4 * xc + 2 * yc + c], slot(xc, yc, c), send_sems, recv_sems, j, (*chips[j], c))

        def passed(j):
            return _remote(slot(*chips[j], c), slot(*chips[j], c), send_sems, recv_sems, 3 + j, (xc, yc, 1 - c))

        @pl.when(i == 0)
        def _():
            for j in range(3):
                sent(j).start()

        xv = x_ref[...]
        o_ref[...] = (xv * _rstd(xv) * g_ref[...]).astype(BF16)

        @pl.when(i == steps - 1)
        def _():
            for j in range(3):
                sent(j).wait_recv()
                passed(j).start()
            for j in range(3):
                passed(j).wait_recv()
                passed(j).wait_send()
                sent(j).wait_send()

    return pl.pallas_call(
        body, name=name, grid=(steps,),
        in_specs=[pl.BlockSpec((tm, d), lambda i: (i, 0)), pl.BlockSpec((1, d), lambda i: (0, 0)), ANY],
        out_specs=[pl.BlockSpec((tm, d), lambda i: (i, 0)), ANY],
        out_shape=[jax.ShapeDtypeStruct((t, d), BF16), jax.ShapeDtypeStruct(buf.shape, buf.dtype)],
        scratch_shapes=[pltpu.SemaphoreType.DMA((6,)), pltpu.SemaphoreType.DMA((6,))],
        input_output_aliases={2: 1},
        compiler_params=_params(("arbitrary",)),
    )(x, g, buf)


def _mix_cat(attn, rnn, gain, *, tm, name):
    t = attn.shape[0]

    def body(a_ref, r_ref, g_ref, o_ref):
        av = a_ref[...].astype(F32)
        o_ref[:, :ATTN_W] = (av * _rstd(av) * g_ref[...]).astype(BF16)
        o_ref[:, ATTN_W:] = r_ref[...].astype(BF16)

    return pl.pallas_call(
        body, name=name, grid=(t // tm,),
        in_specs=[pl.BlockSpec((tm, ATTN_W), lambda i: (i, 0)), pl.BlockSpec((tm, RNN_W), lambda i: (i, 0)),
                  pl.BlockSpec((1, ATTN_W), lambda i: (0, 0))],
        out_specs=pl.BlockSpec((tm, D_MODEL), lambda i: (i, 0)),
        out_shape=jax.ShapeDtypeStruct((t, D_MODEL), BF16),
        compiler_params=_params(("parallel",)),
    )(attn, rnn, gain)


def _post_norm_res(mixed, g_post, res, g_next, *, tm, name, exchanges=()):
    t, d = mixed.shape

    def body(m_ref, gp_ref, r_ref, gn_ref, x1_ref, h2_ref):
        mv = m_ref[...].astype(F32)
        x1 = r_ref[...] + mv * _rstd(mv) * gp_ref[...]
        x1_ref[...] = x1.astype(BF16)
        h2_ref[...] = (x1 * _rstd(x1) * gn_ref[...]).astype(BF16)

    row = pl.BlockSpec((tm, d), lambda i: (i, 0))
    vec = pl.BlockSpec((1, d), lambda i: (0, 0))
    res_, xres = _call(
        body, name=name, grid=(t // tm,),
        in_specs=[row, vec, row, vec], out_specs=[row, row],
        out_shape=[jax.ShapeDtypeStruct((t, d), BF16), jax.ShapeDtypeStruct((t, d), BF16)],
        args=[mixed, g_post, res, g_next], semantics=("parallel",), exchanges=exchanges)
    return (*res_, xres) if exchanges else res_


def _rms_bwd(dyn, xin, g, res, *, tm, out_dtype, name, col_block=0, exchanges=()):
    t, d = xin.shape

    def body(*refs):
        if res is not None:
            dy_ref, x_ref, g_ref, r_ref, dx_ref, dg_ref = refs
        else:
            dy_ref, x_ref, g_ref, dx_ref, dg_ref = refs
        i = pl.program_id(0)
        xv = x_ref[...].astype(F32)
        dy = dy_ref[...].astype(F32)
        r = _rstd(xv)
        xh = xv * r
        part = jnp.sum(dy * xh, axis=0, keepdims=True)

        @pl.when(i == 0)
        def _():
            dg_ref[...] = part

        @pl.when(i > 0)
        def _():
            dg_ref[...] += part

        tt = dy * g_ref[...]
        dx = r * (tt - xh * jnp.mean(tt * xh, axis=-1, keepdims=True))
        if res is not None:
            dx = dx + r_ref[...].astype(F32)
        dx_ref[...] = dx.astype(out_dtype)

    row = pl.BlockSpec((tm, d), lambda i: (i, 0))
    vec = pl.BlockSpec((1, d), lambda i: (0, 0))
    in_specs = [pl.BlockSpec((tm, d), lambda i: (i, col_block)), row, vec]
    args = [dyn, xin, g]
    if res is not None:
        in_specs.append(row)
        args.append(res)
    res, xres = _call(
        body, name=name, grid=(t // tm,),
        in_specs=in_specs, out_specs=[row, vec],
        out_shape=[jax.ShapeDtypeStruct((t, d), out_dtype), jax.ShapeDtypeStruct((1, d), F32)], args=args,
        semantics=("arbitrary",), exchanges=exchanges)
    return (*res, xres) if exchanges else res


def _loss_head(y, g_post, x1, target, *, tm, name):
    t, d = y.shape

    def body(y_ref, g_ref, x1_ref, t_ref, dy_ref, dx2_ref, loss_ref, dg_ref):
        i = pl.program_id(0)
        yv = y_ref[...].astype(F32)
        r = _rstd(yv)
        yh = yv * r
        gv = g_ref[...]
        err = x1_ref[...].astype(F32) + yh * gv - t_ref[...]
        lpart = 0.5 * jnp.sum(jnp.mean(err * err, axis=-1, keepdims=True), axis=0, keepdims=True)
        dx2 = err * (1.0 / d)
        dgp = jnp.sum(dx2 * yh, axis=0, keepdims=True)
        lane = lax.broadcasted_iota(jnp.int32, (1, 128), 1)
        lrow = jnp.where(lane == 0, lpart, 0.0)

        @pl.when(i == 0)
        def _():
            dg_ref[...] = dgp
            loss_ref[...] = lrow

        @pl.when(i > 0)
        def _():
            dg_ref[...] += dgp
            loss_ref[...] += lrow

        tt = dx2 * gv
        dy_ref[...] = (r * (tt - yh * jnp.mean(tt * yh, axis=-1, keepdims=True))).astype(BF16)
        dx2_ref[...] = dx2.astype(BF16)

    row = pl.BlockSpec((tm, d), lambda i: (i, 0))
    vec = pl.BlockSpec((1, d), lambda i: (0, 0))
    return pl.pallas_call(
        body, name=name, grid=(t // tm,),
        in_specs=[row, vec, row, row],
        out_specs=[row, row, pl.BlockSpec((1, 128), lambda i: (0, 0)), vec],
        out_shape=[jax.ShapeDtypeStruct((t, d), BF16), jax.ShapeDtypeStruct((t, d), BF16),
                   jax.ShapeDtypeStruct((1, 128), F32), jax.ShapeDtypeStruct((1, d), F32)],
        compiler_params=_params(("arbitrary",)),
    )(y, g_post, x1, target)


def _alibi_slope(h):
    return 2.0 ** (-8.0 * (h + 1) / N_Q)


PAIR = 2 * HEAD_DIM
N_PAIRS = N_Q // 2
PAIRS_PER_KV = GROUP // 2
SMEM = pl.BlockSpec(memory_space=pltpu.SMEM)


def _swa_mask(n):
    key = lax.broadcasted_iota(jnp.int32, (2 * BLK, BLK), 0)
    qry = lax.broadcasted_iota(jnp.int32, (2 * BLK, BLK), 1)
    dist = qry + BLK - key
    valid = (dist >= 0) & (dist < BLK) & ((key >= BLK) | (n > 0))
    return valid, dist.astype(F32)


def _block_diag(kvp_ref, kvc_ref, off):
    a = jnp.concatenate([kvp_ref[:, off:off + HEAD_DIM], kvc_ref[:, off:off + HEAD_DIM]], axis=0).astype(BF16)
    z = jnp.zeros_like(a)
    return jnp.concatenate([jnp.concatenate([a, z], axis=1), jnp.concatenate([z, a], axis=1)], axis=0)


def _swa_scores(s2, e, hh, valid, distf):
    s = s2[2 * BLK * e:2 * BLK * (e + 1)] * (HEAD_DIM ** -0.5) - _alibi_slope(hh) * distf
    return jnp.where(valid, s, -1e30)


def _swa_fwd(proj, sinks, *, name, exchanges=()):
    t = proj.shape[0]
    nb = t // BLK
    kvb = KV_COL // (2 * 128)

    def body(sink_ref, q_ref, kvc_ref, kvp_ref, o_ref, lse_ref):
        n = pl.program_id(0)
        valid, distf = _swa_mask(n)
        for kvh in range(N_KV):
            k2 = _block_diag(kvp_ref, kvc_ref, kvh * HEAD_DIM)
            v2 = _block_diag(kvp_ref, kvc_ref, 128 + kvh * HEAD_DIM)
            for jp in range(PAIRS_PER_KV):
                pair = kvh * PAIRS_PER_KV + jp
                lanes = slice(pair * PAIR, (pair + 1) * PAIR)
                s2 = lax.dot_general(k2, q_ref[:, lanes].astype(BF16), NT, preferred_element_type=F32)
                probs = []
                for e in range(2):
                    hh = 2 * pair + e
                    s = _swa_scores(s2, e, hh, valid, distf)
                    sink = sink_ref[0, hh]
                    mx = jnp.maximum(jnp.max(s, axis=0, keepdims=True), sink)
                    p = jnp.exp(s - mx)
                    l = jnp.sum(p, axis=0, keepdims=True) + jnp.exp(sink - mx)
                    probs.append((p * (1.0 / l)).astype(BF16))
                    lse_ref[hh:hh + 1, :] = mx + jnp.log(l)
                o_ref[:, lanes] = lax.dot_general(jnp.concatenate(probs, axis=0), v2, TN,
                                                  preferred_element_type=F32).astype(BF16)

    res, xres = _call(
        body, name=name, grid=(nb,),
        in_specs=[SMEM,
                  pl.BlockSpec((BLK, ATTN_W), lambda n: (n, 0)),
                  pl.BlockSpec((BLK, 256), lambda n: (n, kvb)),
                  pl.BlockSpec((BLK, 256), lambda n: (jnp.maximum(n - 1, 0), kvb))],
        out_specs=[pl.BlockSpec((BLK, ATTN_W), lambda n: (n, 0)),
                   pl.BlockSpec((None, N_Q, BLK), lambda n: (n, 0, 0))],
        out_shape=[jax.ShapeDtypeStruct((t, ATTN_W), BF16), jax.ShapeDtypeStruct((nb, N_Q, BLK), F32)],
        args=[sinks, proj, proj, proj], semantics=("parallel",), exchanges=exchanges)
    return (*res, xres) if exchanges else res


def _swa_bwd(proj, sinks, dattn, lse, *, name, exchanges=()):
    t = proj.shape[0]
    nb = t // BLK
    kvb = KV_COL // (2 * 128)

    def body(sink_ref, q_ref, kvc_ref, kvp_ref, do_ref, lse_ref, dq_ref, dkv_ref, dsink_ref, carry_ref):
        n = pl.program_id(0)

        @pl.when(n == 0)
        def _():
            dsink_ref[...] = jnp.zeros_like(dsink_ref)
            carry_ref[...] = jnp.zeros_like(carry_ref)

        @pl.when(n < nb)
        def _():
            valid, distf = _swa_mask(n)
            for kvh in range(N_KV):
                k2 = _block_diag(kvp_ref, kvc_ref, kvh * HEAD_DIM)
                v2 = _block_diag(kvp_ref, kvc_ref, 128 + kvh * HEAD_DIM)
                dk2 = jnp.zeros((4 * BLK, PAIR), F32)
                dv2 = jnp.zeros((4 * BLK, PAIR), F32)
                for jp in range(PAIRS_PER_KV):
                    pair = kvh * PAIRS_PER_KV + jp
                    lanes = slice(pair * PAIR, (pair + 1) * PAIR)
                    q2 = q_ref[:, lanes].astype(BF16)
                    do2 = do_ref[:, lanes].astype(BF16)
                    s2 = lax.dot_general(k2, q2, NT, preferred_element_type=F32)
                    dp2 = lax.dot_general(v2, do2, NT, preferred_element_type=F32)
                    probs, dss = [], []
                    for e in range(2):
                        hh = 2 * pair + e
                        lse_h = lse_ref[hh:hh + 1, :]
                        p = jnp.exp(_swa_scores(s2, e, hh, valid, distf) - lse_h)
                        dp = dp2[2 * BLK * e:2 * BLK * (e + 1)]
                        delta = jnp.sum(p * dp, axis=0, keepdims=True)
                        dsink_ref[hh:hh + 1, :] += -jnp.exp(sink_ref[0, hh] - lse_h) * delta
                        probs.append(p.astype(BF16))
                        dss.append((p * (dp - delta)).astype(BF16))
                    ds2 = jnp.concatenate(dss, axis=0)
                    dq_ref[:, lanes] = (lax.dot_general(ds2, k2, TN, preferred_element_type=F32)
                                        * (HEAD_DIM ** -0.5)).astype(BF16)
                    dk2 = dk2 + jnp.dot(ds2, q2, preferred_element_type=F32)
                    dv2 = dv2 + jnp.dot(jnp.concatenate(probs, axis=0), do2, preferred_element_type=F32)
                dk_cat = (dk2[:2 * BLK, :HEAD_DIM] + dk2[2 * BLK:, HEAD_DIM:]) * (HEAD_DIM ** -0.5)
                dv_cat = dv2[:2 * BLK, :HEAD_DIM] + dv2[2 * BLK:, HEAD_DIM:]
                ko = kvh * HEAD_DIM
                vo = 128 + kvh * HEAD_DIM
                dkv_ref[:, ko:ko + HEAD_DIM] = (carry_ref[:, ko:ko + HEAD_DIM] + dk_cat[:BLK]).astype(BF16)
                dkv_ref[:, vo:vo + HEAD_DIM] = (carry_ref[:, vo:vo + HEAD_DIM] + dv_cat[:BLK]).astype(BF16)
                carry_ref[:, ko:ko + HEAD_DIM] = dk_cat[BLK:]
                carry_ref[:, vo:vo + HEAD_DIM] = dv_cat[BLK:]

        @pl.when(n == nb)
        def _():
            dkv_ref[...] = carry_ref[...].astype(BF16)

    last = nb - 1
    res, xres = _call(
        body, name=name, grid=(nb + 1,),
        in_specs=[SMEM,
                  pl.BlockSpec((BLK, ATTN_W), lambda n: (jnp.minimum(n, last), 0)),
                  pl.BlockSpec((BLK, 256), lambda n: (jnp.minimum(n, last), kvb)),
                  pl.BlockSpec((BLK, 256), lambda n: (jnp.maximum(jnp.minimum(n, last) - 1, 0), kvb)),
                  pl.BlockSpec((BLK, ATTN_W), lambda n: (jnp.minimum(n, last), 0)),
                  pl.BlockSpec((None, N_Q, BLK), lambda n: (jnp.minimum(n, last), 0, 0))],
        out_specs=[pl.BlockSpec((BLK, ATTN_W), lambda n: (jnp.minimum(n, last), 0)),
                   pl.BlockSpec((BLK, 256), lambda n: (jnp.maximum(n - 1, 0), 0)),
                   pl.BlockSpec((N_Q, BLK), lambda n: (0, 0))],
        out_shape=[jax.ShapeDtypeStruct((t, ATTN_W), BF16), jax.ShapeDtypeStruct((t, 256), BF16),
                   jax.ShapeDtypeStruct((N_Q, BLK), F32)],
        scratch_shapes=[pltpu.VMEM((BLK, 256), F32)],
        args=[sinks, proj, proj, proj, dattn, lse], semantics=("arbitrary",), exchanges=exchanges)
    return (*res, xres) if exchanges else res


def _cumsum_rows(x):
    n = x.shape[0]
    row = lax.broadcasted_iota(jnp.int32, x.shape, 0)
    s = 1
    while s < n:
        x = x + jnp.where(row >= s, pltpu.roll(x, s, axis=0), 0.0)
        s *= 2
    return x


def _rev_cumsum_rows(x):
    n = x.shape[0]
    row = lax.broadcasted_iota(jnp.int32, x.shape, 0)
    s = 1
    while s < n:
        x = x + jnp.where(row < n - s, pltpu.roll(x, n - s, axis=0), 0.0)
        s *= 2
    return x


def _lower_bound(lbl_ref):
    l0 = lbl_ref[0:1, :]
    l1 = lbl_ref[1:2, :]
    mx = jnp.maximum(l0, l1)
    e0 = jnp.exp(l0 - mx)
    e1 = jnp.exp(l1 - mx)
    return e0 / (e0 + e1)


def _hgrn_gates(z, lb):
    sg = _sigmoid(z)
    f = lb + (1.0 - lb) * sg
    return sg, f, jnp.log(f), 1.0 - f


def _sub_factors(b, k, i, sub, trim):
    need = -(-sub * i // 16) * 16 if trim else CHUNK
    rows = lax.broadcasted_iota(jnp.int32, (need, RNN_HD), 0)
    ref = b[sub * i - 1:sub * i, :]
    qfac = jnp.exp(b[sub * i:sub * (i + 1), :] - ref)
    kfac = jnp.where(rows < sub * i, jnp.exp(ref - b[:need]), 0.0)
    kt = (k[:need] * kfac).astype(BF16)
    if need < CHUNK:
        kt = jnp.concatenate([kt, jnp.zeros((CHUNK - need, RNN_HD), BF16)], axis=0)
    return qfac, kfac, kt


def _diag_decay(bi, s):
    trow = lax.broadcasted_iota(jnp.int32, bi.shape, 0)
    return jnp.where(trow >= s, jnp.exp(bi - bi[s:s + 1, :]), 0.0)


def _hgrn_fwd(proj, lb_logits, norm_gain, *, tb, name, exchanges=()):
    t = proj.shape[0]
    ntb = t // tb
    nch = tb // CHUNK
    qb, fb, ib, gb = QR_COL // 128, FR_COL // 128, IR_COL // 128, GR_COL // 128

    def body(q_ref, f_ref, i_ref, g_ref, lbl_ref, gain_ref, o_ref, out_ref, s0_ref, ksave_ref, bsave_ref, st_ref):
        c = pl.program_id(1)

        @pl.when(c == 0)
        def _():
            st_ref[...] = jnp.zeros_like(st_ref)

        lb = _lower_bound(lbl_ref)
        gain = gain_ref[...]

        def chunk(ci, st):
            rows = slice(ci * CHUNK, (ci + 1) * CHUNK)
            _, _, lf, k = _hgrn_gates(f_ref[rows, :], lb)
            qr = q_ref[rows, :]
            q = qr * _sigmoid(qr)
            v = i_ref[rows, :]
            b = _cumsum_rows(lf)
            ksave_ref[rows, :] = k
            bsave_ref[rows, :] = b
            s0_ref[ci] = st
            o_inter = lax.dot_general((q * jnp.exp(b)).astype(BF16), st.astype(BF16), NT,
                                      preferred_element_type=F32)
            vb = v.astype(BF16)
            blast = b[CHUNK - 1:CHUNK, :]
            khat = (k * jnp.exp(blast - b)).astype(BF16)
            st = st * jnp.exp(blast) + lax.dot_general(vb, khat, TN, preferred_element_type=F32)
            blocks = []
            for i in range(CHUNK // SUB_FWD):
                blk = slice(SUB_FWD * i, SUB_FWD * (i + 1))
                qi, ki, vi, bi = q[blk], k[blk], v[blk], b[blk]
                oi = o_inter[blk]
                if i > 0:
                    qfac, _, kt = _sub_factors(b, k, i, SUB_FWD, trim=True)
                    att = lax.dot_general((qi * qfac).astype(BF16), kt, NT,
                                          preferred_element_type=F32)
                    oi = oi + jnp.dot(att.astype(BF16), vb, preferred_element_type=F32)
                for s in range(SUB_FWD):
                    qe = qi * _diag_decay(bi, s)
                    a = jnp.sum(qe * ki[s:s + 1, :], axis=1, keepdims=True)
                    oi = oi + a * vi[s:s + 1, :]
                blocks.append(oi)
            o = jnp.concatenate(blocks, axis=0)
            o_ref[rows, :] = o
            gr = g_ref[rows, :]
            out_ref[rows, :] = (o * _rstd(o) * gain * (gr * _sigmoid(gr))).astype(BF16)
            return st

        st = st_ref[...]
        for ci in range(nch):
            st = chunk(ci, st)
        st_ref[...] = st

    def col(base):
        return pl.BlockSpec((tb, RNN_HD), lambda h, c: (c, base + h))

    res, xres = _call(
        body, name=name, grid=(N_RNN, ntb),
        in_specs=[col(qb), col(fb), col(ib), col(gb),
                  pl.BlockSpec((2, RNN_HD), lambda h, c: (0, h)), pl.BlockSpec((1, RNN_HD), lambda h, c: (0, 0))],
        out_specs=[pl.BlockSpec((tb, RNN_HD), lambda h, c: (c, h)), pl.BlockSpec((tb, RNN_HD), lambda h, c: (c, h)),
                   pl.BlockSpec((None, nch, RNN_HD, RNN_HD), lambda h, c: (h, c, 0, 0)),
                   pl.BlockSpec((tb, RNN_HD), lambda h, c: (c, h)), pl.BlockSpec((tb, RNN_HD), lambda h, c: (c, h))],
        out_shape=[jax.ShapeDtypeStruct((t, RNN_W), F32), jax.ShapeDtypeStruct((t, RNN_W), BF16),
                   jax.ShapeDtypeStruct((N_RNN, t // CHUNK, RNN_HD, RNN_HD), F32),
                   jax.ShapeDtypeStruct((t, RNN_W), F32), jax.ShapeDtypeStruct((t, RNN_W), F32)],
        scratch_shapes=[pltpu.VMEM((RNN_HD, RNN_HD), F32)],
        args=[proj, proj, proj, proj, lb_logits, norm_gain],
        semantics=("parallel", "arbitrary"), exchanges=exchanges)
    return (*res, xres) if exchanges else res


def _hgrn_bwd(proj, lb_logits, norm_gain, o_pre, s0, k_gate, b_cum, dcat, *, tb, name, exchanges=()):
    t = proj.shape[0]
    ntb = t // tb
    nch = tb // CHUNK
    qb, fb, ib, gb = QR_COL // 128, FR_COL // 128, IR_COL // 128, GR_COL // 128
    sub = SUB_BWD
    nsub = CHUNK // sub

    def body(q_ref, k_ref, b_ref, i_ref, g_ref, lbl_ref, gain_ref, o_ref, s0_ref, dout_ref,
             dq_ref, df_ref, di_ref, dg_ref, dlb_ref, dgain_ref,
             dst_ref, dqs_ref, dks_ref, dvs_ref):
        c = pl.program_id(1)

        @pl.when(c == 0)
        def _():
            dst_ref[...] = jnp.zeros_like(dst_ref)
            dlb_ref[...] = jnp.zeros_like(dlb_ref)
            dgain_ref[...] = jnp.zeros_like(dgain_ref)

        lb = _lower_bound(lbl_ref)
        inv_1mlb = 1.0 / (1.0 - lb)
        gain = gain_ref[...]

        def chunk(ci, dst):
            rows = slice(ci * CHUNK, (ci + 1) * CHUNK)
            dqa_ref, dka_ref, dva_ref = dqs_ref.at[ci], dks_ref.at[ci], dvs_ref.at[ci]
            k = k_ref[rows, :]
            b = b_ref[rows, :]
            f = 1.0 - k
            one_minus_sg = k * inv_1mlb
            qr = q_ref[rows, :]
            sq = _sigmoid(qr)
            q = qr * sq
            v = i_ref[rows, :]

            dout = dout_ref[rows, :].astype(F32)
            o = o_ref[rows, :]
            gr = g_ref[rows, :]
            sgg = _sigmoid(gr)
            gate = gr * sgg
            rs = _rstd(o)
            nrm = o * rs
            dg_ref[rows, :] = (dout * nrm * gain * (sgg * (1.0 + gr * (1.0 - sgg)))).astype(BF16)
            dn = dout * gate
            dgain_ref[...] += jnp.sum(dn * nrm, axis=0, keepdims=True)
            tt = dn * gain
            do = rs * (tt - nrm * jnp.mean(tt * nrm, axis=-1, keepdims=True))

            dob = do.astype(BF16)
            vb = v.astype(BF16)
            eb = jnp.exp(b)
            blast = b[CHUNK - 1:CHUNK, :]
            ebl = jnp.exp(blast - b)
            dstb = dst.astype(BF16)
            khat = (k * ebl).astype(BF16)
            s0 = s0_ref[ci]
            dqa_ref[...] = eb * jnp.dot(dob, s0.astype(BF16), preferred_element_type=F32)
            dk_state = ebl * jnp.dot(vb, dstb, preferred_element_type=F32)
            dka_ref[...] = dk_state
            d_blast = (jnp.sum(k * dk_state, axis=0, keepdims=True)
                       + jnp.exp(blast) * jnp.sum(dst * s0, axis=0, keepdims=True))
            dva_ref[...] = lax.dot_general(khat, dstb, NT, preferred_element_type=F32)
            dst_next = dst * jnp.exp(blast) + lax.dot_general(dob, (q * eb).astype(BF16), TN,
                                                              preferred_element_type=F32)
            pm = lax.dot_general(dob, vb, NT, preferred_element_type=F32)
            for i in range(nsub):
                blk = slice(sub * i, sub * (i + 1))
                qi, ki, vi, bi, doi = q[blk], k[blk], v[blk], b[blk], do[blk]
                dqi = dqa_ref[blk, :]
                if i > 0:
                    qfac, kfac, kt = _sub_factors(b, k, i, sub, trim=False)
                    qt = (qi * qfac).astype(BF16)
                    att = lax.dot_general(qt, kt, NT, preferred_element_type=F32).astype(BF16)
                    pmi = pm[blk, :].astype(BF16)
                    dva_ref[...] += lax.dot_general(att, doi.astype(BF16), TN, preferred_element_type=F32)
                    dqi = dqi + qfac * jnp.dot(pmi, kt, preferred_element_type=F32)
                    dka_ref[...] += kfac * lax.dot_general(pmi, qt, TN, preferred_element_type=F32)
                dqa_ref[blk, :] = dqi
                srow = lax.broadcasted_iota(jnp.int32, (sub, RNN_HD), 0)
                dki = jnp.zeros((sub, RNN_HD), F32)
                dvi = jnp.zeros((sub, RNN_HD), F32)
                for tq in range(sub):
                    qt, dot_ = qi[tq:tq + 1, :], doi[tq:tq + 1, :]
                    e = jnp.where(srow <= tq, jnp.exp(bi[tq:tq + 1, :] - bi), 0.0)
                    ke = ki * e
                    p = jnp.sum(vi * dot_, axis=1, keepdims=True)
                    a = jnp.sum(ke * qt, axis=1, keepdims=True)
                    dki = dki + p * (qt * e)
                    dvi = dvi + a * dot_
                    row = slice(sub * i + tq, sub * i + tq + 1)
                    dqa_ref[row, :] += jnp.sum(p * ke, axis=0, keepdims=True)
                dka_ref[blk, :] += dki
                dva_ref[blk, :] += dvi

            dq = dqa_ref[...]
            dk = dka_ref[...]
            lastrow = lax.broadcasted_iota(jnp.int32, (CHUNK, RNN_HD), 0) == CHUNK - 1
            dlf = _rev_cumsum_rows(q * dq - k * dk + jnp.where(lastrow, d_blast, 0.0))
            dff = dlf / f - dk
            df_ref[rows, :] = (dff * k * (1.0 - one_minus_sg)).astype(BF16)
            dlb_ref[...] += jnp.sum(dff * one_minus_sg, axis=0, keepdims=True)
            dq_ref[rows, :] = (dq * (sq * (1.0 + qr * (1.0 - sq)))).astype(BF16)
            di_ref[rows, :] = dva_ref[...].astype(BF16)
            return dst_next

        dst = dst_ref[...]
        for ci in reversed(range(nch)):
            dst = chunk(ci, dst)
        dst_ref[...] = dst

    def col(base):
        return pl.BlockSpec((tb, RNN_HD), lambda h, c: (ntb - 1 - c, base + h))

    outc = pl.BlockSpec((tb, RNN_HD), lambda h, c: (ntb - 1 - c, h))
    hb = ATTN_W // RNN_HD
    res, xres = _call(
        body, name=name, grid=(N_RNN, ntb),
        in_specs=[col(qb), outc, outc, col(ib), col(gb),
                  pl.BlockSpec((2, RNN_HD), lambda h, c: (0, h)), pl.BlockSpec((1, RNN_HD), lambda h, c: (0, 0)),
                  outc,
                  pl.BlockSpec((None, nch, RNN_HD, RNN_HD), lambda h, c: (h, ntb - 1 - c, 0, 0)),
                  pl.BlockSpec((tb, RNN_HD), lambda h, c: (ntb - 1 - c, hb + h))],
        out_specs=[outc, outc, outc, outc,
                   pl.BlockSpec((1, RNN_HD), lambda h, c: (0, h)),
                   pl.BlockSpec((None, 1, RNN_HD), lambda h, c: (h, 0, 0))],
        out_shape=[jax.ShapeDtypeStruct((t, RNN_W), BF16)] * 4
        + [jax.ShapeDtypeStruct((1, RNN_W), F32), jax.ShapeDtypeStruct((N_RNN, 1, RNN_HD), F32)],
        scratch_shapes=[pltpu.VMEM((RNN_HD, RNN_HD), F32),
                        pltpu.VMEM((nch, CHUNK, RNN_HD), F32), pltpu.VMEM((nch, CHUNK, RNN_HD), F32),
                        pltpu.VMEM((nch, CHUNK, RNN_HD), F32)],
        args=[proj, k_gate, b_cum, proj, proj, lb_logits, norm_gain, o_pre, s0, dcat],
        semantics=("parallel", "arbitrary"), exchanges=exchanges)
    return (*res, xres) if exchanges else res


def _cast_slots(w, where, *, name):
    _, rows, cols = w.shape
    rh = rows // 2
    tr = _row_tile(rh, cols)
    nh = rh // tr

    def body(wh_ref, w_ref, o_ref):
        o_ref[...] = w_ref[...].astype(BF16)

    return pl.pallas_call(
        body, name=name,
        grid_spec=pltpu.PrefetchScalarGridSpec(
            num_scalar_prefetch=1, grid=(2, nh),
            in_specs=[pl.BlockSpec((None, tr, cols), lambda h, i, wh: (0, h * nh + i, 0))],
            out_specs=pl.BlockSpec((None, tr, cols), lambda h, i, wh: (2 * wh[0] + h, i, 0))),
        out_shape=jax.ShapeDtypeStruct((8, rh, cols), BF16),
        compiler_params=_params(("parallel", "parallel")),
    )(where, w)


def _row_tile(rows, cols, budget=1 << 20):
    tr = rows
    while tr * cols > budget and tr % 16 == 0:
        tr //= 2
    return tr


def _half_spec(g, tr, halves_last, slab):
    if halves_last:
        return pl.BlockSpec((None, tr, g.shape[2] // 2), lambda *a: (slab(*a), a[-2], a[-1][1]))
    return pl.BlockSpec((None, None, tr, g.shape[3]), lambda *a: (slab(*a), a[-1][1], a[-2], 0))


def _pair_sum(g, sib, where, *, name, halves_last=False):
    rh, cols = sib.shape[1:]
    tr = _row_tile(rh, cols)

    def body(w_ref, g_ref, s_ref, o_ref):
        o_ref[...] = (g_ref[...] + s_ref[...]).astype(BF16)

    def foreign(s, i, w):
        return (w[0] + 1 + s) % N_CHIPS

    return pl.pallas_call(
        body, name=name,
        grid_spec=pltpu.PrefetchScalarGridSpec(
            num_scalar_prefetch=1, grid=(N_CHIPS - 1, rh // tr),
            in_specs=[_half_spec(g, tr, halves_last, foreign),
                      pl.BlockSpec((None, tr, cols), lambda s, i, w: (foreign(s, i, w), i, 0))],
            out_specs=pl.BlockSpec((None, tr, cols), lambda s, i, w: (foreign(s, i, w), i, 0))),
        out_shape=jax.ShapeDtypeStruct((4, rh, cols), BF16),
        compiler_params=_params(("parallel", "parallel")),
    )(where, g, sib)


def _final_half(g, sib, recv, where, *, name, halves_last=False):
    rh, cols = sib.shape[1:]
    tr = _row_tile(rh, cols)

    def body(w_ref, g_ref, s_ref, r_ref, o_ref):
        acc = g_ref[...] + s_ref[...]
        for j in range(3):
            acc = acc + r_ref[j].astype(F32)
        o_ref[...] = acc

    return pl.pallas_call(
        body, name=name,
        grid_spec=pltpu.PrefetchScalarGridSpec(
            num_scalar_prefetch=1, grid=(rh // tr,),
            in_specs=[_half_spec(g, tr, halves_last, lambda i, w: w[0]),
                      pl.BlockSpec((None, tr, cols), lambda i, w: (w[0], i, 0)),
                      pl.BlockSpec((3, tr, cols), lambda i, w: (0, i, 0))],
            out_specs=pl.BlockSpec((tr, cols), lambda i, w: (i, 0))),
        out_shape=jax.ShapeDtypeStruct((rh, cols), F32),
        compiler_params=_params(("parallel",)),
    )(where, g, sib, recv)


def _adamw_math(w, g, m, v):
    m = ADAM_B1 * m + (1.0 - ADAM_B1) * g
    v = ADAM_B2 * v + (1.0 - ADAM_B2) * (g * g)
    m_hat = m / (1.0 - ADAM_B1 ** ADAM_STEP)
    v_hat = v / (1.0 - ADAM_B2 ** ADAM_STEP)
    delta = -ADAM_LR * (m_hat / (jnp.sqrt(v_hat) + ADAM_EPS) + ADAM_WD * w)
    return delta, m, v


def _adamw(w, mine, theirs, m, v, where, *, name, halves_last=False):
    _, rows, cols = w.shape
    if halves_last:
        cols //= 2
        tr = _row_tile(rows, cols, budget=1 << 19)
        grid = (rows // tr, 2)
        blk = pl.BlockSpec((None, tr, cols), lambda i, h, wh: (0, i, h))
        mine_spec = theirs_spec = pl.BlockSpec((tr, cols), lambda i, h, wh: (i, 0))
        which = lambda: pl.program_id(1)
    else:
        tr = _row_tile(rows // 2, cols, budget=1 << 19)
        nh = rows // 2 // tr
        grid = (rows // tr,)
        blk = pl.BlockSpec((None, tr, cols), lambda i, wh: (0, i, 0))
        mine_spec = pl.BlockSpec((tr, cols), lambda i, wh: (jnp.where(i // nh == wh[1], i % nh, 0), 0))
        theirs_spec = pl.BlockSpec((tr, cols), lambda i, wh: (jnp.where(i // nh == wh[1], 0, i % nh), 0))
        which = lambda: pl.program_id(0) // nh

    def body(wh_ref, w_ref, a_ref, b_ref, m_ref, v_ref, g_ref, d_ref, nm_ref, nv_ref):
        g = jnp.where(which() == wh_ref[1], a_ref[...], b_ref[...])
        d, nm, nv = _adamw_math(w_ref[...], g, m_ref[...], v_ref[...])
        g_ref[...] = g
        d_ref[...] = d
        nm_ref[...] = nm
        nv_ref[...] = nv

    rows, cols = w.shape[1:]
    return pl.pallas_call(
        body, name=name,
        grid_spec=pltpu.PrefetchScalarGridSpec(
            num_scalar_prefetch=1, grid=grid,
            in_specs=[blk, mine_spec, theirs_spec, blk, blk], out_specs=[blk] * 4),
        out_shape=[jax.ShapeDtypeStruct((1, rows, cols), F32)] * 4,
        compiler_params=_params(("parallel",) * len(grid)),
    )(where, w, mine, theirs, m, v)


SEG_LOSS = 0
SEG_SINK = 128
SEG_AGAIN = 256
SEG_L0 = SEG_AGAIN + ATTN_W
SEG_L1 = SEG_L0 + RNN_W
SEG_RGAIN = SEG_L1 + RNN_W
SEG_G = SEG_RGAIN + 128
N_PACK = SEG_G + 4 * D_MODEL


def _pack(sinks, again, l0, l1, rgain, gains, loss=None):
    z = lambda k: jnp.zeros((1, k), F32)
    first = z(128) if loss is None else loss
    return jnp.concatenate([first, sinks, z(128 - N_Q), again, l0, l1, rgain] + list(gains), axis=1)


def _small_reduce_adamw(part, w, m, v, *, name):
    def body(p_ref, w_ref, m_ref, v_ref, g_ref, d_ref, nm_ref, nv_ref, buf_ref, send_sems, recv_sems):
        x, y, c = _place()
        me = 4 * x + 2 * y + c
        copies = []
        for k in range(1, 8):
            dx, dy, dc = (k >> 2) & 1, (k >> 1) & 1, k & 1
            to = (x ^ dx, y ^ dy, c ^ dc)
            cp = pltpu.make_async_remote_copy(
                src_ref=p_ref, dst_ref=buf_ref.at[me],
                send_sem=send_sems.at[k - 1], recv_sem=recv_sems.at[k - 1],
                device_id=to, device_id_type=MESH)
            cp.start()
            copies.append(cp)
        buf_ref[me] = p_ref[...]
        for cp in copies:
            cp.wait()
        tot = buf_ref[0]
        for j in range(1, 8):
            tot = tot + buf_ref[j]
        g_ref[...] = tot
        l0 = w_ref[:, SEG_L0:SEG_L0 + RNN_W]
        l1 = w_ref[:, SEG_L1:SEG_L1 + RNN_W]
        mx = jnp.maximum(l0, l1)
        e0 = jnp.exp(l0 - mx)
        e1 = jnp.exp(l1 - mx)
        lb = e0 / (e0 + e1)
        gl0 = tot[:, SEG_L0:SEG_L0 + RNN_W] * lb * (1.0 - lb)
        g_ref[:, SEG_L0:SEG_L0 + RNN_W] = gl0
        g_ref[:, SEG_L1:SEG_L1 + RNN_W] = -gl0
        d, nm, nv = _adamw_math(w_ref[...], g_ref[...], m_ref[...], v_ref[...])
        d_ref[...] = d
        nm_ref[...] = nm
        nv_ref[...] = nv

    vm = pl.BlockSpec(memory_space=pltpu.VMEM)
    return pl.pallas_call(
        body, name=name,
        in_specs=[vm] * 4, out_specs=[vm] * 4,
        out_shape=[jax.ShapeDtypeStruct((1, N_PACK), F32)] * 4,
        scratch_shapes=[pltpu.VMEM((8, 1, N_PACK), F32), pltpu.SemaphoreType.DMA((7,)),
                        pltpu.SemaphoreType.DMA((7,))],
    )(part, w, m, v)


def _layer_grads(xs, tgt, bufs, where, sinks, again, lb_logits, rgain,
                 g_mix_pre, g_mix_post, g_mlp_pre, g_mlp_post):
    tm = 512
    b_in, b_out, b_up, b_dn = bufs

    shard = IN_W // N_CHIPS
    h1, b_in = _rms_cast_gather(xs, g_mix_pre, b_in, tm=tm, name="h1_norm_gather_w_in")
    w_in_t = b_in.reshape(IN_W, D_MODEL)
    proj, ((b_out, b_up),) = _mm(
        h1, w_in_t, tm=1024, tn=768, tk=D_MODEL, out_dtype=F32, w_layout="nk", name="in_proj",
        exchanges=[_x_gather([b_out, b_up], ici=[(0, 256), (0, 336)])])
    attn, lse, ((b_out, b_up),) = _swa_fwd(
        proj, sinks, name="swa_fwd",
        exchanges=[_x_gather([b_out, b_up], ici=[None, (336, 320)], d2d=[(0, 256), None])])
    w_out = b_out.reshape(D_MODEL, D_MODEL)
    o_pre, rnn, s0, k_gate, b_cum, ((b_up, b_dn),) = _hgrn_fwd(
        proj, lb_logits, rgain, tb=512, name="hgrn_fwd",
        exchanges=[_x_gather([b_up, b_dn], ici=[(656, 368), (0, 400)])])
    cat = _mix_cat(attn, rnn, again, tm=tm, name="mix_cat")
    mixed, ((b_up, b_dn),) = _mm(
        cat, w_out, tm=1024, tn=1024, tk=D_MODEL, out_dtype=BF16, name="out_proj",
        exchanges=[_x_gather([b_up, b_dn], ici=[None, (400, 240)], d2d=[(0, 1024), (0, 400)])])
    w_up4 = b_up.reshape(N_CHIPS, D_MODEL, D_FF // N_CHIPS)
    x1, h2, ((b_dn,),) = _post_norm_res(
        mixed, g_mix_post, xs, g_mlp_pre, tm=tm, name="mix_post",
        exchanges=[_x_gather([b_dn], d2d=[(400, 240)])])
    u, ((b_dn,),) = _mm(h2, w_up4, tm=1024, tn=1024, tk=D_MODEL, out_dtype=BF16, relu=True, w_layout="skn",
                        name="mlp_up", exchanges=[_x_gather([b_dn], ici=[(640, 384)], cross=[(640, 384)])])
    w_dn = b_dn.reshape(D_FF, D_MODEL)
    yv = _mm(u, w_dn, tm=1024, tn=1024, tk=2048, out_dtype=BF16, a_square=True, name="mlp_down")
    dy, dx2, loss_row, dg_mlp_post = _loss_head(yv, g_mlp_post, x1, tgt, tm=tm, name="loss_head")

    def halved(g):
        return g.reshape(N_CHIPS, 2, g.shape[1] // 2, g.shape[2])
    du = _mm(dy, w_dn, tm=1024, tn=1024, tk=D_MODEL, out_dtype=BF16, mul2=u, w_layout="nk", name="mlp_down_bwd")
    g_dn = halved(_mm_tn(u, dy, tm=1024, tn=1024, tt=4096, a_square=True, name="w_down_grad")
                  .reshape(N_CHIPS, D_FF // N_CHIPS, D_MODEL))
    d_w_up, ((sib_dn,),) = _mm_tn(h2, du, tm=1024, tn=1024, tt=4096, n_split=N_CHIPS, name="w_up_grad",
                                  exchanges=[_x_pair([g_dn])])
    g_up = halved(d_w_up)
    wire_dn = _pair_sum(g_dn, sib_dn, where, name="pair_sum_w_down")
    dh2, ((recv_dn,), (sib_up,)) = _mm(du, w_up4, tm=1024, tn=1024, tk=2048, out_dtype=BF16, w_layout="snk", name="mlp_up_bwd",
                                       exchanges=[_x_chip([wire_dn], rows=[(0, 928)]), _x_pair([g_up])])
    wire_up = _pair_sum(g_up, sib_up, where, name="pair_sum_w_up")
    dx1, dg_mlp_pre = _rms_bwd(dh2, x1, g_mlp_pre, dx2, tm=tm, out_dtype=BF16, name="mlp_pre_bwd")
    dmixed, dg_mix_post = _rms_bwd(dx1, mixed, g_mix_post, None, tm=tm, out_dtype=BF16, name="mix_post_bwd")
    d_w_out, ((recv_dn,),) = _mm_tn(cat, dmixed, tm=1024, tn=1024, tt=4096, name="w_out_grad",
                                    exchanges=[_x_chip([wire_dn], rows=[(928, 96)], into=[recv_dn])])
    fin_dn = _final_half(g_dn, sib_dn, recv_dn, where, name="final_half_w_down")
    g_out = halved(d_w_out.reshape(N_CHIPS, D_MODEL // N_CHIPS, D_MODEL))
    dcat, ((sib_out,), (oth_dn,)) = _mm(dmixed, w_out, tm=1024, tn=1024, tk=D_MODEL, out_dtype=BF16, w_layout="nk",
                                        name="out_proj_bwd", exchanges=[_x_pair([g_out]), _x_share([fin_dn])])
    wire_out = _pair_sum(g_out, sib_out, where, name="pair_sum_w_out")
    dattn, dg_again = _rms_bwd(dcat, attn, again, None, tm=tm, out_dtype=BF16, name="attn_norm_bwd")
    dq_a, dkv, dsinks, ((recv_up,),) = _swa_bwd(
        proj, sinks, dattn, lse, name="swa_bwd", exchanges=[_x_chip([wire_up], rows=[(0, 512)])])
    dq_r, df_r, di_r, dg_r, dlb, dgain_h, ((recv_up,), (recv_out,)) = _hgrn_bwd(
        proj, lb_logits, rgain, o_pre, s0, k_gate, b_cum, dcat, tb=1024, name="hgrn_bwd",
        exchanges=[_x_chip([wire_up], rows=[(512, 512)], into=[recv_up]), _x_chip([wire_out])])
    fin_up = _final_half(g_up, sib_up, recv_up, where, name="final_half_w_up")
    fin_out = _final_half(g_out, sib_out, recv_out, where, name="final_half_w_out")
    dproj = jnp.concatenate([dq_a, dkv, dq_r, df_r, di_r, dg_r], axis=1)
    piece_cols = D_MODEL // 4

    def w_in_piece(pc, exchanges):
        d, xres = _mm_tn(dproj, h1, tm=896, tn=2 * piece_cols, tt=4096, b_blocks=(pc, pc + 2),
                         name="w_in_grad_%d" % pc, exchanges=exchanges)
        return d.reshape(N_CHIPS, shard, 2 * piece_cols), xres

    g_in0, ((oth_up, oth_out),) = w_in_piece(0, [_x_share([fin_up, fin_out])])
    g_in1, ((sib_in0,),) = w_in_piece(1, [_x_pair([g_in0], halves_last=True)])
    wire_in0 = _pair_sum(g_in0, sib_in0, where, name="pair_sum_w_in_0", halves_last=True)
    dh1, ((recv_in0,), (sib_in1,)) = _mm(
        dproj, w_in_t, tm=1024, tn=1024, tk=2688, out_dtype=BF16, m_blocks=(0, 2), name="in_proj_bwd_0",
        exchanges=[_x_chip([wire_in0]), _x_pair([g_in1], halves_last=True)])
    wire_in1 = _pair_sum(g_in1, sib_in1, where, name="pair_sum_w_in_1", halves_last=True)
    dh1, ((recv_in1,),) = _mm(
        dproj, w_in_t, tm=1024, tn=1024, tk=2688, out_dtype=BF16, m_blocks=(2, 2), out_into=dh1,
        name="in_proj_bwd_1", exchanges=[_x_chip([wire_in1])])
    gx, dg_mix_pre = _rms_bwd(dh1, xs, g_mix_pre, dx1, tm=tm, out_dtype=F32, name="mix_pre_bwd")
    fin_in0 = _final_half(g_in0, sib_in0, recv_in0, where, name="final_half_w_in_0", halves_last=True)
    fin_in1 = _final_half(g_in1, sib_in1, recv_in1, where, name="final_half_w_in_1", halves_last=True)
    oth_in0, oth_in1 = _run_exchange(_x_share([fin_in0, fin_in1]), name="share_w_in")
    fin_in = jnp.concatenate([fin_in0, fin_in1], axis=1)
    oth_in = jnp.concatenate([oth_in0, oth_in1], axis=1)

    big = [(fin_in, oth_in), (fin_out, oth_out), (fin_up, oth_up), (fin_dn, oth_dn)]
    drgain = jnp.sum(dgain_h, axis=0)
    small = _pack(jnp.sum(dsinks, axis=1)[None, :], dg_again, dlb, jnp.zeros_like(dlb), drgain,
                  [dg_mix_pre, dg_mix_post, dg_mlp_pre, dg_mlp_post], loss=loss_row)
    return gx, big, small


def kernel(x, w_in, attn_sinks, attn_out_gain, rnn_lb_logits, rnn_norm_gain, w_out, mix_pre_gain, mix_post_gain, mlp_pre_gain, mlp_post_gain, w_up, w_down, loss_target, m_w_in, m_attn_sinks, m_attn_out_gain, m_rnn_lb_logits, m_rnn_norm_gain, m_w_out, m_mix_pre_gain, m_mix_post_gain, m_mlp_pre_gain, m_mlp_post_gain, m_w_up, m_w_down, v_w_in, v_attn_sinks, v_attn_out_gain, v_rnn_lb_logits, v_rnn_norm_gain, v_w_out, v_mix_pre_gain, v_mix_post_gain, v_mlp_pre_gain, v_mlp_post_gain, v_w_up, v_w_down):
    ax, ay, ac = _place()
    where = jnp.stack([2 * ax + ay, ac]).astype(jnp.int32)
    t = lambda a: jnp.swapaxes(a, 1, 2)
    big_w = [t(w_in), w_out, w_up, w_down]
    big_m = [t(m_w_in), m_w_out, m_w_up, m_w_down]
    big_v = [t(v_w_in), v_w_out, v_w_up, v_w_down]

    names = ["w_in", "w_out", "w_up", "w_down"]
    bufs = [_cast_slots(w, where, name="cast_" + nm) for w, nm in zip(big_w, names)]
    gx, big_g, small_part = _layer_grads(
        x[0], loss_target[0], bufs, where, attn_sinks, attn_out_gain, rnn_lb_logits, rnn_norm_gain,
        mix_pre_gain, mix_post_gain, mlp_pre_gain, mlp_post_gain)

    grads, deltas, new_m, new_v = [], [], [], []
    for (f, o), w, m, v, nm in zip(big_g, big_w, big_m, big_v, names):
        res = _adamw(w, f, o, m, v, where, name="adamw_" + nm, halves_last=(nm == "w_in"))
        if nm == "w_in":
            res = [t(r) for r in res]
        g, d, nm_, nv_ = res
        grads.append(g)
        deltas.append(d)
        new_m.append(nm_)
        new_v.append(nv_)

    def pack_params(sinks, again, logits, rgain, gains):
        return _pack(sinks, again, logits[0:1], logits[1:2], rgain, gains)

    pw = pack_params(attn_sinks, attn_out_gain, rnn_lb_logits, rnn_norm_gain,
                     [mix_pre_gain, mix_post_gain, mlp_pre_gain, mlp_post_gain])
    pm = pack_params(m_attn_sinks, m_attn_out_gain, m_rnn_lb_logits, m_rnn_norm_gain,
                     [m_mix_pre_gain, m_mix_post_gain, m_mlp_pre_gain, m_mlp_post_gain])
    pv = pack_params(v_attn_sinks, v_attn_out_gain, v_rnn_lb_logits, v_rnn_norm_gain,
                     [v_mix_pre_gain, v_mix_post_gain, v_mlp_pre_gain, v_mlp_post_gain])
    packs = _small_reduce_adamw(small_part, pw, pm, pv, name="small_reduce_adamw")

    def unpack(p):
        seg = lambda o, k: p[:, o:o + k]
        logits = jnp.concatenate([seg(SEG_L0, RNN_W), seg(SEG_L1, RNN_W)], axis=0)
        gains = [seg(SEG_G + i * D_MODEL, D_MODEL) for i in range(4)]
        return dict(sinks=seg(SEG_SINK, N_Q), again=seg(SEG_AGAIN, ATTN_W), logits=logits,
                    rgain=seg(SEG_RGAIN, RNN_HD), gains=gains)

    def order(small, big):
        return [big[0], small["sinks"], small["again"], small["logits"], small["rgain"], big[1],
                *small["gains"], big[2], big[3]]

    loss = packs[0][0, 0]
    outs = [loss, gx[None]]
    for p, b in zip(packs, [grads, deltas, new_m, new_v]):
        outs += order(unpack(p), b)
    return tuple(outs)
```
